```python
import jax, jax.numpy as jnp
from jax import lax
import numpy as np

D_MODEL = 2048
BATCH = 8
SEQ = 4096
DEPTH = 1

HEAD_DIM = 64
A_Q_HEADS = 12
A_KV_HEADS = 4
A_GROUP = A_Q_HEADS // A_KV_HEADS
WINDOW = 128
BLOCK = 128
B_HEADS = 12
C_HEADS = 4
C_HEAD_DIM = 128
MEM_TOKENS = 256
N_BRANCHES = 3
A_WIDTH = A_Q_HEADS * HEAD_DIM
A_KV_WIDTH = A_KV_HEADS * HEAD_DIM
B_WIDTH = B_HEADS * HEAD_DIM
C_WIDTH = C_HEADS * C_HEAD_DIM
EPS = 1e-6
NEG = -1e30

IN_SIZES = (A_WIDTH, A_KV_WIDTH, A_KV_WIDTH, A_WIDTH,
            B_WIDTH, B_WIDTH, B_WIDTH, B_WIDTH, B_HEADS,
            C_WIDTH, C_WIDTH,
            N_BRANCHES * D_MODEL)
IN_WIDTH = sum(IN_SIZES)
IN_OFFSETS = tuple(int(o) for o in np.cumsum(IN_SIZES)[:-1])

kernel_name = "hybrid_swa_fox_memory_gated_block"


def rms_norm(x, gain):
    x32 = x.astype(jnp.float32)
    y = x32 * lax.rsqrt(jnp.mean(x32 * x32, axis=-1, keepdims=True) + EPS)
    return (y * gain.astype(jnp.float32)).astype(x.dtype)


def alibi_slopes(n_heads):
    h = jnp.arange(1, n_heads + 1, dtype=jnp.float32)
    return jnp.exp2(-8.0 * h / n_heads)


def sliding_window_attention(q, k, v, q_gain, k_gain, sinks):
    b, s, _, d = q.shape
    nb = s // BLOCK
    q = rms_norm(q, q_gain).astype(jnp.float32)
    k = rms_norm(k, k_gain).astype(jnp.float32)
    v = v.astype(jnp.float32)
    qb = q.reshape(b, nb, BLOCK, A_KV_HEADS, A_GROUP, d)
    pad = jnp.zeros((b, BLOCK, A_KV_HEADS, d), jnp.float32)
    kp = jnp.concatenate([pad, k], axis=1).reshape(b, nb + 1, BLOCK, A_KV_HEADS, d)
    vp = jnp.concatenate([pad, v], axis=1).reshape(b, nb + 1, BLOCK, A_KV_HEADS, d)
    kb = jnp.concatenate([kp[:, :-1], kp[:, 1:]], axis=2)
    vb = jnp.concatenate([vp[:, :-1], vp[:, 1:]], axis=2)
    scores = jnp.einsum('bnqkgd,bnskd->bnkgqs', qb, kb) * (d ** -0.5)
    qi = jnp.arange(BLOCK)[:, None]
    kj = jnp.arange(2 * BLOCK)[None, :]
    rel = qi + BLOCK - kj
    key_pos = jnp.arange(nb)[:, None, None] * BLOCK - BLOCK + kj[None]
    valid = (rel >= 0) & (rel < WINDOW) & (key_pos >= 0)
    slopes = alibi_slopes(A_Q_HEADS).reshape(A_KV_HEADS, A_GROUP)
    bias = -slopes[:, :, None, None] * rel.astype(jnp.float32)
    scores = jnp.where(valid[None, :, None, None], scores + bias[None, None], NEG)
    sink = sinks.astype(jnp.float32).reshape(A_KV_HEADS, A_GROUP)[None, None, :, :, None, None]
    m = jnp.maximum(jnp.max(scores, axis=-1, keepdims=True), sink)
    p = jnp.exp(scores - m)
    denom = jnp.sum(p, axis=-1, keepdims=True) + jnp.exp(sink - m)
    out = jnp.einsum('bnkgqs,bnskd->bnqkgd', p / denom, vb)
    return out.reshape(b, s, A_Q_HEADS * d)


def forgetting_attention(q, k, v, f_logit, q_gain, k_gain):
    b, s, h, d = q.shape
    nb = s // BLOCK
    q = rms_norm(q, q_gain).astype(jnp.float32)
    k = rms_norm(k, k_gain).astype(jnp.float32)
    v = v.astype(jnp.float32)
    log_f = jax.nn.log_sigmoid(f_logit.astype(jnp.float32))
    c = jnp.cumsum(log_f, axis=1).transpose(0, 2, 1)
    q_blocks = q.reshape(b, nb, BLOCK, h, d).transpose(1, 0, 2, 3, 4)
    cq_blocks = c.reshape(b, h, nb, BLOCK).transpose(2, 0, 1, 3)
    key_pos = jnp.arange(s)
    scale = d ** -0.5

    def block_fn(args):
        qb, cqb, n = args
        sc = jnp.einsum('bqhd,bshd->bhqs', qb, k) * scale + cqb[..., None] - c[:, :, None, :]
        qpos = n * BLOCK + jnp.arange(BLOCK)
        mask = key_pos[None, :] <= qpos[:, None]
        sc = jnp.where(mask[None, None], sc, NEG)
        p = jax.nn.softmax(sc, axis=-1)
        return jnp.einsum('bhqs,bshd->bqhd', p, v)

    out = lax.map(block_fn, (q_blocks, cq_blocks, jnp.arange(nb)))
    return out.transpose(1, 0, 2, 3, 4).reshape(b, s, h * d)


def memory_attention(q, mk, mv, q_gain, k_gain):
    b, s, h, d = q.shape
    q = rms_norm(q, q_gain).astype(jnp.float32)
    mk = rms_norm(mk, k_gain).astype(jnp.float32)
    sc = jnp.einsum('bthd,bmhd->bhtm', q, mk) * (d ** -0.5)
    p = jax.nn.softmax(sc, axis=-1)
    out = jnp.einsum('bhtm,bmhd->bthd', p, mv.astype(jnp.float32))
    return out.reshape(b, s, h * d)


def hybrid_layer(x, mem, norm_gain, mem_norm_gain, w_in, b_forget,
                 q_gain_a, k_gain_a, sinks_a, q_gain_b, k_gain_b, q_gain_c, k_gain_c,
                 w_mem_kv, w_branch_a, w_branch_b, w_branch_c, w_out):
    b, s, _ = x.shape
    hn = rms_norm(x, norm_gain)
    proj = hn @ w_in
    (qa, ka, va, za, qb, kb, vb, zb, fb, qc, zc, gate_logits) = jnp.split(proj, IN_OFFSETS, axis=-1)

    ya = sliding_window_attention(qa.reshape(b, s, A_Q_HEADS, HEAD_DIM),
                                  ka.reshape(b, s, A_KV_HEADS, HEAD_DIM),
                                  va.reshape(b, s, A_KV_HEADS, HEAD_DIM),
                                  q_gain_a, k_gain_a, sinks_a).astype(x.dtype)
    ua = (ya * jax.nn.silu(za)) @ w_branch_a

    yb = forgetting_attention(qb.reshape(b, s, B_HEADS, HEAD_DIM),
                              kb.reshape(b, s, B_HEADS, HEAD_DIM),
                              vb.reshape(b, s, B_HEADS, HEAD_DIM),
                              fb + b_forget, q_gain_b, k_gain_b).astype(x.dtype)
    ub = (yb * jax.nn.silu(zb)) @ w_branch_b

    mkv = rms_norm(mem, mem_norm_gain) @ w_mem_kv
    mk, mv = jnp.split(mkv, 2, axis=-1)
    mlen = mem.shape[1]
    yc = memory_attention(qc.reshape(b, s, C_HEADS, C_HEAD_DIM),
                          mk.reshape(b, mlen, C_HEADS, C_HEAD_DIM),
                          mv.reshape(b, mlen, C_HEADS, C_HEAD_DIM),
                          q_gain_c, k_gain_c).astype(x.dtype)
    uc = (yc * jax.nn.silu(zc)) @ w_branch_c

    g = jax.nn.sigmoid(gate_logits.reshape(b, s, N_BRANCHES, D_MODEL))
    y = g[:, :, 0] * ua + g[:, :, 1] * ub + g[:, :, 2] * uc
    return x + y @ w_out


def _fwd_setup_inputs(seed: int = 0) -> dict:
    key = jax.random.key(seed)
    ks = jax.random.split(key, 20)
    f32 = jnp.float32
    nrm = lambda k, shape: jax.random.normal(k, shape, f32)
    return {
        "x": nrm(ks[0], (BATCH, SEQ, D_MODEL)),
        "mem": nrm(ks[1], (BATCH, MEM_TOKENS, D_MODEL)),
        "norm_gain": 1.0 + 0.02 * nrm(ks[2], (DEPTH, D_MODEL)),
        "mem_norm_gain": 1.0 + 0.02 * nrm(ks[3], (DEPTH, D_MODEL)),
        "w_in": nrm(ks[4], (DEPTH, D_MODEL, IN_WIDTH)) * D_MODEL ** -0.5,
        "b_forget": 3.0 + 0.5 * nrm(ks[5], (DEPTH, B_HEADS)),
        "q_gain_a": 1.0 + 0.02 * nrm(ks[6], (DEPTH, HEAD_DIM)),
        "k_gain_a": 1.0 + 0.02 * nrm(ks[7], (DEPTH, HEAD_DIM)),
        "sinks_a": 0.5 * nrm(ks[8], (DEPTH, A_Q_HEADS)),
        "q_gain_b": 1.0 + 0.02 * nrm(ks[9], (DEPTH, HEAD_DIM)),
        "k_gain_b": 1.0 + 0.02 * nrm(ks[10], (DEPTH, HEAD_DIM)),
        "q_gain_c": 1.0 + 0.02 * nrm(ks[11], (DEPTH, C_HEAD_DIM)),
        "k_gain_c": 1.0 + 0.02 * nrm(ks[12], (DEPTH, C_HEAD_DIM)),
        "w_mem_kv": nrm(ks[13], (DEPTH, D_MODEL, 2 * C_WIDTH)) * D_MODEL ** -0.5,
        "w_branch_a": nrm(ks[14], (DEPTH, A_WIDTH, D_MODEL)) * A_WIDTH ** -0.5,
        "w_branch_b": nrm(ks[15], (DEPTH, B_WIDTH, D_MODEL)) * B_WIDTH ** -0.5,
        "w_branch_c": nrm(ks[16], (DEPTH, C_WIDTH, D_MODEL)) * C_WIDTH ** -0.5,
        "w_out": nrm(ks[17], (DEPTH, D_MODEL, D_MODEL)) * D_MODEL ** -0.5,
    }


def _fwd_reference(x, mem, norm_gain, mem_norm_gain, w_in, b_forget,
              q_gain_a, k_gain_a, sinks_a, q_gain_b, k_gain_b, q_gain_c, k_gain_c,
              w_mem_kv, w_branch_a, w_branch_b, w_branch_c, w_out):
    for layer in range(DEPTH):
        x = hybrid_layer(x, mem, norm_gain[layer], mem_norm_gain[layer], w_in[layer], b_forget[layer],
                         q_gain_a[layer], k_gain_a[layer], sinks_a[layer],
                         q_gain_b[layer], k_gain_b[layer], q_gain_c[layer], k_gain_c[layer],
                         w_mem_kv[layer], w_branch_a[layer], w_branch_b[layer], w_branch_c[layer],
                         w_out[layer])
    return x


import jax as _jax
import jax.numpy as _jnp

TWIN_FORMAT = 'train_step'
FWD_PARAMS = ['x', 'mem', 'norm_gain', 'mem_norm_gain', 'w_in', 'b_forget', 'q_gain_a', 'k_gain_a', 'sinks_a', 'q_gain_b', 'k_gain_b', 'q_gain_c', 'k_gain_c', 'w_mem_kv', 'w_branch_a', 'w_branch_b', 'w_branch_c', 'w_out']
TWIN_WEIGHTS = ['norm_gain', 'mem_norm_gain', 'w_in', 'b_forget', 'q_gain_a', 'k_gain_a', 'sinks_a', 'q_gain_b', 'k_gain_b', 'q_gain_c', 'k_gain_c', 'w_mem_kv', 'w_branch_a', 'w_branch_b', 'w_branch_c', 'w_out']
TWIN_DIFF_INPUT = 'x'
TWIN_INPUTS = ['x', 'mem', 'norm_gain', 'mem_norm_gain', 'w_in', 'b_forget', 'q_gain_a', 'k_gain_a', 'sinks_a', 'q_gain_b', 'k_gain_b', 'q_gain_c', 'k_gain_c', 'w_mem_kv', 'w_branch_a', 'w_branch_b', 'w_branch_c', 'w_out', 'loss_target', 'm_norm_gain', 'm_mem_norm_gain', 'm_w_in', 'm_b_forget', 'm_q_gain_a', 'm_k_gain_a', 'm_sinks_a', 'm_q_gain_b', 'm_k_gain_b', 'm_q_gain_c', 'm_k_gain_c', 'm_w_mem_kv', 'm_w_branch_a', 'm_w_branch_b', 'm_w_branch_c', 'm_w_out', 'v_norm_gain', 'v_mem_norm_gain', 'v_w_in', 'v_b_forget', 'v_q_gain_a', 'v_k_gain_a', 'v_sinks_a', 'v_q_gain_b', 'v_k_gain_b', 'v_q_gain_c', 'v_k_gain_c', 'v_w_mem_kv', 'v_w_branch_a', 'v_w_branch_b', 'v_w_branch_c', 'v_w_out']
TWIN_OUTPUTS = ['loss', 'grad_x', 'grad_norm_gain', 'grad_mem_norm_gain', 'grad_w_in', 'grad_b_forget', 'grad_q_gain_a', 'grad_k_gain_a', 'grad_sinks_a', 'grad_q_gain_b', 'grad_k_gain_b', 'grad_q_gain_c', 'grad_k_gain_c', 'grad_w_mem_kv', 'grad_w_branch_a', 'grad_w_branch_b', 'grad_w_branch_c', 'grad_w_out', 'delta_norm_gain', 'delta_mem_norm_gain', 'delta_w_in', 'delta_b_forget', 'delta_q_gain_a', 'delta_k_gain_a', 'delta_sinks_a', 'delta_q_gain_b', 'delta_k_gain_b', 'delta_q_gain_c', 'delta_k_gain_c', 'delta_w_mem_kv', 'delta_w_branch_a', 'delta_w_branch_b', 'delta_w_branch_c', 'delta_w_out', 'new_m_norm_gain', 'new_m_mem_norm_gain', 'new_m_w_in', 'new_m_b_forget', 'new_m_q_gain_a', 'new_m_k_gain_a', 'new_m_sinks_a', 'new_m_q_gain_b', 'new_m_k_gain_b', 'new_m_q_gain_c', 'new_m_k_gain_c', 'new_m_w_mem_kv', 'new_m_w_branch_a', 'new_m_w_branch_b', 'new_m_w_branch_c', 'new_m_w_out', 'new_v_norm_gain', 'new_v_mem_norm_gain', 'new_v_w_in', 'new_v_b_forget', 'new_v_q_gain_a', 'new_v_k_gain_a', 'new_v_sinks_a', 'new_v_q_gain_b', 'new_v_k_gain_b', 'new_v_q_gain_c', 'new_v_k_gain_c', 'new_v_w_mem_kv', 'new_v_w_branch_a', 'new_v_w_branch_b', 'new_v_w_branch_c', 'new_v_w_out']
TWIN_LEAF_KINDS = {'loss': 'loss', 'grad_x': 'grad_x', 'grad_norm_gain': 'grad_w', 'grad_mem_norm_gain': 'grad_w', 'grad_w_in': 'grad_w', 'grad_b_forget': 'grad_w', 'grad_q_gain_a': 'grad_w', 'grad_k_gain_a': 'grad_w', 'grad_sinks_a': 'grad_w', 'grad_q_gain_b': 'grad_w', 'grad_k_gain_b': 'grad_w', 'grad_q_gain_c': 'grad_w', 'grad_k_gain_c': 'grad_w', 'grad_w_mem_kv': 'grad_w', 'grad_w_branch_a': 'grad_w', 'grad_w_branch_b': 'grad_w', 'grad_w_branch_c': 'grad_w', 'grad_w_out': 'grad_w', 'delta_norm_gain': 'delta_w', 'delta_mem_norm_gain': 'delta_w', 'delta_w_in': 'delta_w', 'delta_b_forget': 'delta_w', 'delta_q_gain_a': 'delta_w', 'delta_k_gain_a': 'delta_w', 'delta_sinks_a': 'delta_w', 'delta_q_gain_b': 'delta_w', 'delta_k_gain_b': 'delta_w', 'delta_q_gain_c': 'delta_w', 'delta_k_gain_c': 'delta_w', 'delta_w_mem_kv': 'delta_w', 'delta_w_branch_a': 'delta_w', 'delta_w_branch_b': 'delta_w', 'delta_w_branch_c': 'delta_w', 'delta_w_out': 'delta_w', 'new_m_norm_gain': 'new_m', 'new_m_mem_norm_gain': 'new_m', 'new_m_w_in': 'new_m', 'new_m_b_forget': 'new_m', 'new_m_q_gain_a': 'new_m', 'new_m_k_gain_a': 'new_m', 'new_m_sinks_a': 'new_m', 'new_m_q_gain_b': 'new_m', 'new_m_k_gain_b': 'new_m', 'new_m_q_gain_c': 'new_m', 'new_m_k_gain_c': 'new_m', 'new_m_w_mem_kv': 'new_m', 'new_m_w_branch_a': 'new_m', 'new_m_w_branch_b': 'new_m', 'new_m_w_branch_c': 'new_m', 'new_m_w_out': 'new_m', 'new_v_norm_gain': 'new_v', 'new_v_mem_norm_gain': 'new_v', 'new_v_w_in': 'new_v', 'new_v_b_forget': 'new_v', 'new_v_q_gain_a': 'new_v', 'new_v_k_gain_a': 'new_v', 'new_v_sinks_a': 'new_v', 'new_v_q_gain_b': 'new_v', 'new_v_k_gain_b': 'new_v', 'new_v_q_gain_c': 'new_v', 'new_v_k_gain_c': 'new_v', 'new_v_w_mem_kv': 'new_v', 'new_v_w_branch_a': 'new_v', 'new_v_w_branch_b': 'new_v', 'new_v_w_branch_c': 'new_v', 'new_v_w_out': 'new_v'}


def _forward(args):
    return _fwd_reference(*[args[k] for k in FWD_PARAMS])


def _output_shape():
    def fwd():
        inp = _fwd_setup_inputs(0)
        return _fwd_reference(*[inp[k] for k in FWD_PARAMS])
    out = _jax.eval_shape(fwd)
    return out.shape, out.dtype

N_MICROBATCH = 1
ADAM_LR = 0.001
ADAM_B1 = 0.9
ADAM_B2 = 0.999
ADAM_EPS = 1e-08
ADAM_WD = 0.01
ADAM_STEP = 10
PER_EXAMPLE_BATCH_AXIS = {'x': 0, 'mem': 0, 'loss_target': 0}
SHARED_INPUTS = []
_WEIGHT_DTYPES = {'norm_gain': _jnp.float32, 'mem_norm_gain': _jnp.float32, 'w_in': _jnp.float32, 'b_forget': _jnp.float32, 'q_gain_a': _jnp.float32, 'k_gain_a': _jnp.float32, 'sinks_a': _jnp.float32, 'q_gain_b': _jnp.float32, 'k_gain_b': _jnp.float32, 'q_gain_c': _jnp.float32, 'k_gain_c': _jnp.float32, 'w_mem_kv': _jnp.float32, 'w_branch_a': _jnp.float32, 'w_branch_b': _jnp.float32, 'w_branch_c': _jnp.float32, 'w_out': _jnp.float32}
MOMENT_SCALE = {'norm_gain': 5.956696e-01, 'mem_norm_gain': 1.669842e-02, 'w_in': 1.588258e-02, 'b_forget': 6.502912e+00, 'q_gain_a': 2.028190e+00, 'k_gain_a': 2.030359e+00, 'sinks_a': 4.303738e+00, 'q_gain_b': 1.947266e+00, 'k_gain_b': 1.944361e+00, 'q_gain_c': 2.559501e-01, 'k_gain_c': 2.550663e-01, 'w_mem_kv': 9.899656e-03, 'w_branch_a': 1.386149e-02, 'w_branch_b': 1.211203e-02, 'w_branch_c': 4.592875e-03, 'w_out': 1.835268e-02}


def _to_microbatches(a, axis):
    t = _jnp.moveaxis(a, axis, 0)
    t = t.reshape((N_MICROBATCH, t.shape[0] // N_MICROBATCH) + t.shape[1:])
    return _jnp.moveaxis(t, 1, axis + 1)


def setup_inputs(seed: int = 0) -> dict:
    inp = _fwd_setup_inputs(seed)
    key = _jax.random.fold_in(_jax.random.key(seed), 7919)
    shape, _ = _output_shape()
    out = dict(inp)
    out["loss_target"] = _jax.random.normal(_jax.random.fold_in(key, 0), shape, _jnp.float32)
    for i, name in enumerate(TWIN_WEIGHTS):
        w = inp[name].astype(_jnp.float32)
        if MOMENT_SCALE is None:
            s = _jnp.sqrt(_jnp.mean(_jnp.square(w)) + 1e-30)
        else:
            s = MOMENT_SCALE[name]
        km, kv = _jax.random.split(_jax.random.fold_in(key, i + 1))
        out[name] = w
        out["m_" + name] = s * _jax.random.normal(km, w.shape, _jnp.float32)
        out["v_" + name] = (s * s) * _jax.random.uniform(kv, w.shape, _jnp.float32, 0.5, 1.5)
    if N_MICROBATCH > 1:
        for name, axis in PER_EXAMPLE_BATCH_AXIS.items():
            out[name] = _to_microbatches(out[name], axis)
    return {'x': out['x'], 'mem': out['mem'], 'norm_gain': out['norm_gain'], 'mem_norm_gain': out['mem_norm_gain'], 'w_in': out['w_in'], 'b_forget': out['b_forget'], 'q_gain_a': out['q_gain_a'], 'k_gain_a': out['k_gain_a'], 'sinks_a': out['sinks_a'], 'q_gain_b': out['q_gain_b'], 'k_gain_b': out['k_gain_b'], 'q_gain_c': out['q_gain_c'], 'k_gain_c': out['k_gain_c'], 'w_mem_kv': out['w_mem_kv'], 'w_branch_a': out['w_branch_a'], 'w_branch_b': out['w_branch_b'], 'w_branch_c': out['w_branch_c'], 'w_out': out['w_out'], 'loss_target': out['loss_target'], 'm_norm_gain': out['m_norm_gain'], 'm_mem_norm_gain': out['m_mem_norm_gain'], 'm_w_in': out['m_w_in'], 'm_b_forget': out['m_b_forget'], 'm_q_gain_a': out['m_q_gain_a'], 'm_k_gain_a': out['m_k_gain_a'], 'm_sinks_a': out['m_sinks_a'], 'm_q_gain_b': out['m_q_gain_b'], 'm_k_gain_b': out['m_k_gain_b'], 'm_q_gain_c': out['m_q_gain_c'], 'm_k_gain_c': out['m_k_gain_c'], 'm_w_mem_kv': out['m_w_mem_kv'], 'm_w_branch_a': out['m_w_branch_a'], 'm_w_branch_b': out['m_w_branch_b'], 'm_w_branch_c': out['m_w_branch_c'], 'm_w_out': out['m_w_out'], 'v_norm_gain': out['v_norm_gain'], 'v_mem_norm_gain': out['v_mem_norm_gain'], 'v_w_in': out['v_w_in'], 'v_b_forget': out['v_b_forget'], 'v_q_gain_a': out['v_q_gain_a'], 'v_k_gain_a': out['v_k_gain_a'], 'v_sinks_a': out['v_sinks_a'], 'v_q_gain_b': out['v_q_gain_b'], 'v_k_gain_b': out['v_k_gain_b'], 'v_q_gain_c': out['v_q_gain_c'], 'v_k_gain_c': out['v_k_gain_c'], 'v_w_mem_kv': out['v_w_mem_kv'], 'v_w_branch_a': out['v_w_branch_a'], 'v_w_branch_b': out['v_w_branch_b'], 'v_w_branch_c': out['v_w_branch_c'], 'v_w_out': out['v_w_out']}


def _loss(weights, diff, rest, loss_target):
    with _jax.named_scope("forward"):
        args = {**rest, TWIN_DIFF_INPUT: diff, **{k: w.astype(_WEIGHT_DTYPES[k]) for k, w in weights.items()}}
        y = _forward(args)
    with _jax.named_scope("loss_head"):
        err = _jnp.square(y.astype(_jnp.float32) - loss_target)
        return 0.5 * _jnp.sum(_jnp.mean(err, axis=-1)) if err.ndim else 0.5 * err


def _adamw(w, g, m, v):
    m = ADAM_B1 * m + (1.0 - ADAM_B1) * g
    v = ADAM_B2 * v + (1.0 - ADAM_B2) * _jnp.square(g)
    m_hat = m / (1.0 - ADAM_B1 ** ADAM_STEP)
    v_hat = v / (1.0 - ADAM_B2 ** ADAM_STEP)
    delta = -ADAM_LR * (m_hat / (_jnp.sqrt(v_hat) + ADAM_EPS) + ADAM_WD * w)
    return delta, m, v


def reference(x, mem, norm_gain, mem_norm_gain, w_in, b_forget, q_gain_a, k_gain_a, sinks_a, q_gain_b, k_gain_b, q_gain_c, k_gain_c, w_mem_kv, w_branch_a, w_branch_b, w_branch_c, w_out, loss_target, m_norm_gain, m_mem_norm_gain, m_w_in, m_b_forget, m_q_gain_a, m_k_gain_a, m_sinks_a, m_q_gain_b, m_k_gain_b, m_q_gain_c, m_k_gain_c, m_w_mem_kv, m_w_branch_a, m_w_branch_b, m_w_branch_c, m_w_out, v_norm_gain, v_mem_norm_gain, v_w_in, v_b_forget, v_q_gain_a, v_k_gain_a, v_sinks_a, v_q_gain_b, v_k_gain_b, v_q_gain_c, v_k_gain_c, v_w_mem_kv, v_w_branch_a, v_w_branch_b, v_w_branch_c, v_w_out):
    given = dict(x=x, mem=mem, norm_gain=norm_gain, mem_norm_gain=mem_norm_gain, w_in=w_in, b_forget=b_forget, q_gain_a=q_gain_a, k_gain_a=k_gain_a, sinks_a=sinks_a, q_gain_b=q_gain_b, k_gain_b=k_gain_b, q_gain_c=q_gain_c, k_gain_c=k_gain_c, w_mem_kv=w_mem_kv, w_branch_a=w_branch_a, w_branch_b=w_branch_b, w_branch_c=w_branch_c, w_out=w_out, loss_target=loss_target, m_norm_gain=m_norm_gain, m_mem_norm_gain=m_mem_norm_gain, m_w_in=m_w_in, m_b_forget=m_b_forget, m_q_gain_a=m_q_gain_a, m_k_gain_a=m_k_gain_a, m_sinks_a=m_sinks_a, m_q_gain_b=m_q_gain_b, m_k_gain_b=m_k_gain_b, m_q_gain_c=m_q_gain_c, m_k_gain_c=m_k_gain_c, m_w_mem_kv=m_w_mem_kv, m_w_branch_a=m_w_branch_a, m_w_branch_b=m_w_branch_b, m_w_branch_c=m_w_branch_c, m_w_out=m_w_out, v_norm_gain=v_norm_gain, v_mem_norm_gain=v_mem_norm_gain, v_w_in=v_w_in, v_b_forget=v_b_forget, v_q_gain_a=v_q_gain_a, v_k_gain_a=v_k_gain_a, v_sinks_a=v_sinks_a, v_q_gain_b=v_q_gain_b, v_k_gain_b=v_k_gain_b, v_q_gain_c=v_q_gain_c, v_k_gain_c=v_k_gain_c, v_w_mem_kv=v_w_mem_kv, v_w_branch_a=v_w_branch_a, v_w_branch_b=v_w_branch_b, v_w_branch_c=v_w_branch_c, v_w_out=v_w_out)
    weights = {n: given[n] for n in TWIN_WEIGHTS}
    shared = {n: given[n] for n in SHARED_INPUTS}
    per_example = {n: given[n] for n in ['x', 'mem']}
    grad_fn = _jax.value_and_grad(_loss, argnums=(0, 1))

    def one_microbatch(ex, loss_target):
        ex = dict(ex)
        diff = ex.pop(TWIN_DIFF_INPUT)
        return grad_fn(weights, diff, {**shared, **ex}, loss_target)

    if N_MICROBATCH == 1:
        loss, (grad_w, grad_x) = one_microbatch(per_example, given["loss_target"])
    else:
        def body(carry, xs):
            loss_sum, grad_sum = carry
            l_k, (gw_k, gx_k) = one_microbatch(xs[0], xs[1])
            with _jax.named_scope("update"):
                return (loss_sum + l_k, _jax.tree.map(_jnp.add, grad_sum, gw_k)), gx_k

        init = (_jnp.zeros((), _jnp.float32), _jax.tree.map(_jnp.zeros_like, weights))
        (loss, grad_w), grad_x = _jax.lax.scan(body, init, (per_example, given["loss_target"]))
    with _jax.named_scope("update"):
        delta_w, new_m, new_v = {}, {}, {}
        for n in TWIN_WEIGHTS:
            delta_w[n], new_m[n], new_v[n] = _adamw(weights[n], grad_w[n], given["m_" + n], given["v_" + n])
    return (loss, grad_x, *[grad_w[n] for n in TWIN_WEIGHTS], *[delta_w[n] for n in TWIN_WEIGHTS],
            *[new_m[n] for n in TWIN_WEIGHTS], *[new_v[n] for n in TWIN_WEIGHTS])
```

```python
import functools

import jax
import jax.numpy as jnp
import numpy as np
from jax import lax
from jax.experimental import pallas as pl
from jax.experimental.pallas import tpu as pltpu

F32 = jnp.float32
BF16 = jnp.bfloat16

N_DEV = 8
HEAD_DIM = 64
A_Q_HEADS = 12
A_KV_HEADS = 4
A_GROUP = 3
B_HEADS = 12
C_HEADS = 4
C_HEAD_DIM = 128
WINDOW = 128
A_WIDTH = 768
A_KV_WIDTH = 256
B_WIDTH = 768
C_WIDTH = 512
EPS = 1e-6
NEG = -1e30

COL_QA, COL_KA, COL_VA, COL_ZA = 0, 768, 1024, 1280
COL_QB, COL_KB, COL_VB, COL_ZB = 2048, 2816, 3584, 4352
COL_QC, COL_ZC, COL_GATE = 5120, 5632, 6144
FB_SRC = 5120
FB_PAD = 128

ADAM_LR = 0.001
ADAM_B1 = 0.9
ADAM_B2 = 0.999
ADAM_EPS = 1e-08
ADAM_WD = 0.01
ADAM_STEP = 10

VMEM_BIG = 52 * 1024 * 1024
LANES = 128
MESH = pl.DeviceIdType.MESH


def _tile(n, pref, mult=128):
    if n <= pref:
        return n
    t = (pref // mult) * mult
    while t >= mult:
        if n % t == 0:
            return t
        t -= mult
    return n


def _params(sem=None, vmem=None):
    kw = {}
    if sem is not None:
        kw["dimension_semantics"] = sem
    if vmem is not None:
        kw["vmem_limit_bytes"] = vmem
    return pltpu.CompilerParams(**kw)


def _sigmoid(x):
    return 1.0 / (1.0 + jnp.exp(-x))


def _block_diag(hd):
    r = np.arange(LANES)
    return jnp.asarray((r[:, None] // hd) == (r[None, :] // hd), dtype=BF16)


def _seg_sum(t, bd):
    hi = t.astype(BF16)
    lo = (t - hi.astype(F32)).astype(BF16)
    outs = []
    for c in range(t.shape[1] // LANES):
        sl = slice(c * LANES, (c + 1) * LANES)
        outs.append(jnp.dot(hi[:, sl], bd, preferred_element_type=F32) + jnp.dot(lo[:, sl], bd, preferred_element_type=F32))
    return outs[0] if len(outs) == 1 else jnp.concatenate(outs, axis=1)


def _rmsnorm_fwd(x, gain, name):
    rows, d = x.shape
    bm = _tile(rows, 512, 8)

    def body(x_ref, g_ref, o_ref):
        xv = x_ref[...]
        ms = jnp.mean(xv * xv, axis=-1, keepdims=True)
        o_ref[...] = (xv * lax.rsqrt(ms + EPS) * g_ref[...]).astype(BF16)

    return pl.pallas_call(
        body, name=name, grid=(rows // bm,),
        in_specs=[pl.BlockSpec((bm, d), lambda i: (i, 0)), pl.BlockSpec((1, d), lambda i: (0, 0))],
        out_specs=pl.BlockSpec((bm, d), lambda i: (i, 0)),
        out_shape=jax.ShapeDtypeStruct((rows, d), BF16),
        compiler_params=_params(("parallel",)),
    )(x, gain)


def _rmsnorm_bwd(x, dhn, gain, dy, name):
    rows, d = x.shape
    bm = _tile(rows, 256, 8)
    with_dx = dy is not None

    def body(*refs):
        if with_dx:
            x_ref, dh_ref, g_ref, dy_ref, gx_ref, dg_ref = refs
        else:
            x_ref, dh_ref, g_ref, dg_ref = refs
        i = pl.program_id(0)
        xv = x_ref[...]
        rstd = lax.rsqrt(jnp.mean(xv * xv, axis=-1, keepdims=True) + EPS)
        xhat = xv * rstd
        dh = dh_ref[...]
        part = jnp.sum((dh * xhat).reshape(bm // 8, 8, d), axis=0)

        @pl.when(i == 0)
        def _():
            dg_ref[...] = part

        @pl.when(i > 0)
        def _():
            dg_ref[...] += part

        if with_dx:
            g = dh * g_ref[...]
            mean = jnp.mean(g * xhat, axis=-1, keepdims=True)
            gx_ref[...] = dy_ref[...] + rstd * (g - xhat * mean)

    row_spec = pl.BlockSpec((bm, d), lambda i: (i, 0))
    in_specs = [row_spec, row_spec, pl.BlockSpec((1, d), lambda i: (0, 0))]
    args = [x, dhn, gain]
    dg_spec = pl.BlockSpec((8, d), lambda i: (0, 0))
    dg_shape = jax.ShapeDtypeStruct((8, d), F32)
    if with_dx:
        in_specs.append(row_spec)
        args.append(dy)
        out_specs = [row_spec, dg_spec]
        out_shape = [jax.ShapeDtypeStruct((rows, d), F32), dg_shape]
    else:
        out_specs = [dg_spec]
        out_shape = [dg_shape]
    outs = pl.pallas_call(
        body, name=name, grid=(rows // bm,), in_specs=in_specs, out_specs=out_specs, out_shape=out_shape,
        compiler_params=_params(("arbitrary",)),
    )(*args)
    return outs if with_dx else (None, outs[0])


def _mm(a, b, *, grid, a_spec, b_spec, o_spec, o_shape, o_dtype, contract, name, add=None, add_spec=None, acc_shape=None):
    nk = grid[2]
    has_add = add is not None

    def body(*refs):
        a_ref, b_ref = refs[0], refs[1]
        pos = 3 if has_add else 2
        add_ref = refs[2] if has_add else None
        o_ref = refs[pos]
        part = lax.dot_general(a_ref[...], b_ref[...], (contract, ((), ())), preferred_element_type=F32)
        if nk == 1:
            if has_add:
                part = part + add_ref[...]
            o_ref[...] = part.astype(o_dtype)
        else:
            acc = refs[pos + 1]
            k = pl.program_id(2)

            @pl.when(k == 0)
            def _():
                acc[...] = part

            @pl.when(k > 0)
            def _():
                acc[...] += part

            @pl.when(k == nk - 1)
            def _():
                r = acc[...]
                if has_add:
                    r = r + add_ref[...]
                o_ref[...] = r.astype(o_dtype)

    in_specs = [a_spec, b_spec]
    args = [a, b]
    if has_add:
        in_specs.append(add_spec)
        args.append(add)
    scratch = [pltpu.VMEM(acc_shape, F32)] if nk > 1 else []
    return pl.pallas_call(
        body, name=name, grid=grid, in_specs=in_specs, out_specs=o_spec,
        out_shape=jax.ShapeDtypeStruct(o_shape, o_dtype), scratch_shapes=scratch,
        compiler_params=_params(("parallel", "parallel", "arbitrary"), VMEM_BIG),
    )(*args)


def _mm_nn(a, b, *, bm, bn, bk, o_dtype, name, add=None):
    m, kd = a.shape
    n = b.shape[1]
    bm, bn, bk = _tile(m, bm, 8), _tile(n, bn), _tile(kd, bk)
    o_spec = pl.BlockSpec((bm, bn), lambda i, j, k: (i, j))
    return _mm(a, b, grid=(m // bm, n // bn, kd // bk),
               a_spec=pl.BlockSpec((bm, bk), lambda i, j, k: (i, k)),
               b_spec=pl.BlockSpec((bk, bn), lambda i, j, k: (k, j)),
               o_spec=o_spec, o_shape=(m, n), o_dtype=o_dtype, contract=((1,), (0,)), name=name,
               add=add, add_spec=o_spec, acc_shape=(bm, bn))


def _mm_nt(a, b, *, bm, bn, bk, o_dtype, name, add=None):
    m, kd = a.shape
    n = b.shape[0]
    bm, bn, bk = _tile(m, bm, 8), _tile(n, bn), _tile(kd, bk)
    o_spec = pl.BlockSpec((bm, bn), lambda i, j, k: (i, j))
    return _mm(a, b, grid=(m // bm, n // bn, kd // bk),
               a_spec=pl.BlockSpec((bm, bk), lambda i, j, k: (i, k)),
               b_spec=pl.BlockSpec((bn, bk), lambda i, j, k: (j, k)),
               o_spec=o_spec, o_shape=(m, n), o_dtype=o_dtype, contract=((1,), (1,)), name=name,
               add=add, add_spec=o_spec, acc_shape=(bm, bn))


def _mm_tn(a, b, *, bm, bn, bk, o_dtype, name):
    kd, m = a.shape
    n = b.shape[1]
    bm, bn, bk = _tile(m, bm), _tile(n, bn), _tile(kd, bk, 8)
    return _mm(a, b, grid=(m // bm, n // bn, kd // bk),
               a_spec=pl.BlockSpec((bk, bm), lambda i, j, k: (k, i)),
               b_spec=pl.BlockSpec((bk, bn), lambda i, j, k: (k, j)),
               o_spec=pl.BlockSpec((bm, bn), lambda i, j, k: (i, j)),
               o_shape=(m, n), o_dtype=o_dtype, contract=((0,), (0,)), name=name, acc_shape=(bm, bn))


def _mm_branch_fwd(s, w2d, name):
    m, kb = s.shape
    ds = w2d.shape[1]
    bm = _tile(m, 1024, 8)
    return _mm(s, w2d, grid=(m // bm, N_DEV, 1),
               a_spec=pl.BlockSpec((bm, kb), lambda i, j, k: (i, 0)),
               b_spec=pl.BlockSpec((kb, ds), lambda i, j, k: (j, 0)),
               o_spec=pl.BlockSpec((bm, ds), lambda i, j, k: (i, j)),
               o_shape=(m, N_DEV * ds), o_dtype=F32, contract=((1,), (0,)), name=name)


def _mm_branch_bwd_act(du, w2d, kb, name):
    m = du.shape[0]
    ds = w2d.shape[1]
    bm = _tile(m, 1024, 8)
    return _mm(du, w2d, grid=(m // bm, 1, N_DEV),
               a_spec=pl.BlockSpec((bm, ds), lambda i, j, k: (i, k)),
               b_spec=pl.BlockSpec((kb, ds), lambda i, j, k: (k, 0)),
               o_spec=pl.BlockSpec((bm, kb), lambda i, j, k: (i, 0)),
               o_shape=(m, kb), o_dtype=F32, contract=((1,), (1,)), name=name, acc_shape=(bm, kb))


def _mm_branch_bwd_w(s, du, name):
    m, kb = s.shape
    ds = du.shape[1] // N_DEV
    bk = _tile(m, 2048, 8)
    return _mm(s, du, grid=(1, N_DEV, m // bk),
               a_spec=pl.BlockSpec((bk, kb), lambda i, j, k: (k, 0)),
               b_spec=pl.BlockSpec((bk, ds), lambda i, j, k: (k, j)),
               o_spec=pl.BlockSpec((kb, ds), lambda i, j, k: (j, 0)),
               o_shape=(N_DEV * kb, ds), o_dtype=BF16, contract=((0,), (0,)), name=name, acc_shape=(kb, ds))


def _headnorm_fwd(src, c0, width, bw, hd, gain, nflag, head_major, name):
    rows = src.shape[0]
    bm = _tile(rows, 1024, 8)
    bd = _block_diag(hd)
    cb0 = c0 // bw

    def body(x_ref, g_ref, f_ref, bd_ref, o_ref):
        xv = x_ref[...]
        ss = _seg_sum(xv * xv, bd_ref[...])
        rstd = lax.rsqrt(ss * (1.0 / hd) + EPS)
        y = (xv * jnp.where(f_ref[...] > 0.0, rstd, 1.0) * g_ref[...]).astype(BF16)
        if head_major:
            for h in range(bw // HEAD_DIM):
                o_ref[h] = y[:, h * HEAD_DIM:(h + 1) * HEAD_DIM]
        else:
            o_ref[...] = y

    vec_spec = pl.BlockSpec((1, bw), lambda i, t: (0, t))
    if head_major:
        hpb = bw // HEAD_DIM
        out_spec = pl.BlockSpec((hpb, bm, HEAD_DIM), lambda i, t: (t, i, 0))
        out_shape = jax.ShapeDtypeStruct((width // HEAD_DIM, rows, HEAD_DIM), BF16)
    else:
        out_spec = pl.BlockSpec((bm, bw), lambda i, t: (i, t))
        out_shape = jax.ShapeDtypeStruct((rows, width), BF16)
    return pl.pallas_call(
        body, name=name, grid=(rows // bm, width // bw),
        in_specs=[pl.BlockSpec((bm, bw), lambda i, t: (i, cb0 + t)), vec_spec, vec_spec,
                  pl.BlockSpec((LANES, LANES), lambda i, t: (0, 0))],
        out_specs=out_spec, out_shape=out_shape,
        compiler_params=_params(("parallel", "parallel")),
    )(src, gain, nflag, bd)


def _headnorm_bwd(src, c0, width, bw, hd, gain, nflag, dyn, target, t0, name):
    rows = src.shape[0]
    bm = _tile(rows, 1024, 8)
    bd = _block_diag(hd)
    cb0 = c0 // bw
    tb0 = t0 // bw
    aliased = target is not None

    def body(*refs):
        if aliased:
            x_ref, dy_ref, g_ref, f_ref, bd_ref, _, o_ref, dg_ref = refs
        else:
            x_ref, dy_ref, g_ref, f_ref, bd_ref, o_ref, dg_ref = refs
        i = pl.program_id(1)
        xv = x_ref[...]
        dyv = dy_ref[...]
        bdv = bd_ref[...]
        rstd = lax.rsqrt(_seg_sum(xv * xv, bdv) * (1.0 / hd) + EPS)
        xhat = xv * rstd
        g = dyv * g_ref[...]
        mean = _seg_sum(g * xhat, bdv) * (1.0 / hd)
        dx = jnp.where(f_ref[...] > 0.0, rstd * (g - xhat * mean), g)
        o_ref[...] = dx.astype(BF16)
        part = jnp.sum((dyv * xhat).reshape(bm // 8, 8, bw), axis=0)

        @pl.when(i == 0)
        def _():
            dg_ref[...] = part

        @pl.when(i > 0)
        def _():
            dg_ref[...] += part

    vec_spec = pl.BlockSpec((1, bw), lambda t, i: (0, t))
    in_specs = [pl.BlockSpec((bm, bw), lambda t, i: (i, cb0 + t)), pl.BlockSpec((bm, bw), lambda t, i: (i, t)),
                vec_spec, vec_spec, pl.BlockSpec((LANES, LANES), lambda t, i: (0, 0))]
    args = [src, dyn, gain, nflag, bd]
    aliases = {}
    if aliased:
        in_specs.append(pl.BlockSpec(memory_space=pl.ANY))
        args.append(target)
        aliases = {5: 0}
        o_shape = jax.ShapeDtypeStruct(target.shape, BF16)
    else:
        o_shape = jax.ShapeDtypeStruct((rows, width), BF16)
    out, dg = pl.pallas_call(
        body, name=name, grid=(width // bw, rows // bm), in_specs=in_specs,
        out_specs=[pl.BlockSpec((bm, bw), lambda t, i: (i, tb0 + t)), pl.BlockSpec((8, bw), lambda t, i: (0, t))],
        out_shape=[o_shape, jax.ShapeDtypeStruct((8, width), F32)],
        input_output_aliases=aliases,
        compiler_params=_params(("parallel", "arbitrary")),
    )(*args)
    return out, dg


def _fox_prep(pfb, bpad, name):
    s = pfb.shape[0]

    def body(p_ref, b_ref, c_ref):
        z = p_ref[...] + b_ref[...]
        logf = jnp.minimum(z, 0.0) - jnp.log(1.0 + jnp.exp(-jnp.abs(z)))
        x = logf.T[0:16, :]
        lane = lax.broadcasted_iota(jnp.int32, (16, s), 1)
        sh = 1
        while sh < s:
            x = x + jnp.where(lane >= sh, pltpu.roll(x, sh, 1), 0.0)
            sh *= 2
        c_ref[...] = x

    return pl.pallas_call(
        body, name=name, grid=(1,),
        in_specs=[pl.BlockSpec((s, FB_PAD), lambda i: (0, 0)), pl.BlockSpec((1, FB_PAD), lambda i: (0, 0))],
        out_specs=pl.BlockSpec((16, s), lambda i: (0, 0)),
        out_shape=jax.ShapeDtypeStruct((16, s), F32),
        compiler_params=_params(("arbitrary",)),
    )(pfb, bpad)


def _fox_prep_bwd(pfb, bpad, dct, name):
    s = pfb.shape[0]

    def body(p_ref, b_ref, dc_ref, df_ref, db_ref):
        zt = (p_ref[...] + b_ref[...]).T[0:16, :]
        y = dc_ref[...]
        lane = lax.broadcasted_iota(jnp.int32, (16, s), 1)
        sh = 1
        while sh < s:
            y = y + jnp.where(lane < s - sh, pltpu.roll(y, s - sh, 1), 0.0)
            sh *= 2
        dz = y * _sigmoid(-zt)
        db_ref[...] = jnp.broadcast_to(jnp.sum(dz, axis=1, keepdims=True), (16, FB_PAD))
        full = jnp.concatenate([dz, jnp.zeros((FB_PAD - 16, s), F32)], axis=0)
        df_ref[...] = full.T.astype(BF16)

    return pl.pallas_call(
        body, name=name, grid=(1,),
        in_specs=[pl.BlockSpec((s, FB_PAD), lambda i: (0, 0)), pl.BlockSpec((1, FB_PAD), lambda i: (0, 0)),
                  pl.BlockSpec((16, s), lambda i: (0, 0))],
        out_specs=[pl.BlockSpec((s, FB_PAD), lambda i: (0, 0)), pl.BlockSpec((16, FB_PAD), lambda i: (0, 0))],
        out_shape=[jax.ShapeDtypeStruct((s, FB_PAD), BF16), jax.ShapeDtypeStruct((16, FB_PAD), F32)],
        compiler_params=_params(("arbitrary",)),
    )(pfb, bpad, dct)


def _swa_window(n):
    ws = pl.multiple_of(jnp.maximum(n * WINDOW - WINDOW, 0), WINDOW)
    qi = lax.broadcasted_iota(jnp.int32, (WINDOW, 2 * WINDOW), 0)
    kj = lax.broadcasted_iota(jnp.int32, (WINDOW, 2 * WINDOW), 1)
    rel = qi + (n * WINDOW - ws) - kj
    valid = (rel >= 0) & (rel < WINDOW)
    return ws, valid, rel.astype(F32)


def _attn_a_fwd(q, k, v, sinks, slopes, name):
    s = q.shape[1]
    nb = s // WINDOW
    smem = pl.BlockSpec(memory_space=pltpu.SMEM)

    def body(sink_ref, slope_ref, q_ref, k_ref, v_ref, o_ref, lse_ref):
        n = pl.program_id(0)
        ws, valid, relf = _swa_window(n)
        outs = []
        for h in range(A_Q_HEADS):
            kvh = h // A_GROUP
            kw = k_ref[kvh, pl.ds(ws, 2 * WINDOW), :]
            vw = v_ref[kvh, pl.ds(ws, 2 * WINDOW), :]
            sc = lax.dot_general(q_ref[h], kw, (((1,), (1,)), ((), ())), preferred_element_type=F32)
            sc = jnp.where(valid, sc - slope_ref[h] * relf, NEG)
            sink = sink_ref[h]
            m = jnp.maximum(jnp.max(sc, axis=1, keepdims=True), sink)
            p = jnp.exp(sc - m)
            denom = jnp.sum(p, axis=1, keepdims=True) + jnp.exp(sink - m)
            pn = (p / denom).astype(BF16)
            outs.append(jnp.dot(pn, vw, preferred_element_type=F32))
            lse_ref[h] = jnp.broadcast_to(m + jnp.log(denom), (WINDOW, HEAD_DIM))
        o_ref[...] = jnp.concatenate(outs, axis=1)

    return pl.pallas_call(
        body, name=name, grid=(nb,),
        in_specs=[smem, smem,
                  pl.BlockSpec((A_Q_HEADS, WINDOW, HEAD_DIM), lambda n: (0, n, 0)),
                  pl.BlockSpec((A_KV_HEADS, s, HEAD_DIM), lambda n: (0, 0, 0)),
                  pl.BlockSpec((A_KV_HEADS, s, HEAD_DIM), lambda n: (0, 0, 0))],
        out_specs=[pl.BlockSpec((WINDOW, A_WIDTH), lambda n: (n, 0)),
                   pl.BlockSpec((A_Q_HEADS, WINDOW, HEAD_DIM), lambda n: (0, n, 0))],
        out_shape=[jax.ShapeDtypeStruct((s, A_WIDTH), F32), jax.ShapeDtypeStruct((A_Q_HEADS, s, HEAD_DIM), F32)],
        compiler_params=_params(("parallel",), VMEM_BIG),
    )(sinks, slopes, q, k, v)


def _attn_a_bwd(q, k, v, do, lse, dd, sinks, slopes, name):
    s = q.shape[1]
    nb = s // WINDOW
    smem = pl.BlockSpec(memory_space=pltpu.SMEM)
    last = nb - 1

    def body(sink_ref, slope_ref, q_ref, k_ref, v_ref, do_ref, lse_ref, dd_ref, dq_ref, dkv_ref, ds_ref, carry):
        n = pl.program_id(0)

        @pl.when(n == 0)
        def _():
            carry[...] = jnp.zeros(carry.shape, F32)
            ds_ref[...] = jnp.zeros(ds_ref.shape, F32)

        @pl.when(n < nb)
        def _():
            ws, valid, relf = _swa_window(n)
            dqs = []
            dkw = [None] * A_KV_HEADS
            dvw = [None] * A_KV_HEADS
            for h in range(A_Q_HEADS):
                kvh = h // A_GROUP
                qh = q_ref[h]
                doh = do_ref[h]
                kw = k_ref[kvh, pl.ds(ws, 2 * WINDOW), :]
                vw = v_ref[kvh, pl.ds(ws, 2 * WINDOW), :]
                lse_h = lse_ref[h]
                dd_h = dd_ref[h]
                sc = lax.dot_general(qh, kw, (((1,), (1,)), ((), ())), preferred_element_type=F32)
                sc = jnp.where(valid, sc - slope_ref[h] * relf, NEG)
                p = jnp.exp(sc - lse_h[:, 0:1])
                dp = lax.dot_general(doh, vw, (((1,), (1,)), ((), ())), preferred_element_type=F32)
                dsc = (p * (dp - dd_h[:, 0:1])).astype(BF16)
                pb = p.astype(BF16)
                dqs.append(jnp.dot(dsc, kw, preferred_element_type=F32))
                dk_h = lax.dot_general(dsc, qh, (((0,), (0,)), ((), ())), preferred_element_type=F32)
                dv_h = lax.dot_general(pb, doh, (((0,), (0,)), ((), ())), preferred_element_type=F32)
                dkw[kvh] = dk_h if dkw[kvh] is None else dkw[kvh] + dk_h
                dvw[kvh] = dv_h if dvw[kvh] is None else dvw[kvh] + dv_h
                psink = jnp.exp(sink_ref[h] - lse_h)
                ds_ref[h] += jnp.sum((-psink * dd_h).reshape(WINDOW // 8, 8, HEAD_DIM), axis=0)
            dq_ref[...] = jnp.concatenate(dqs, axis=1)
            win = jnp.concatenate(dkw + dvw, axis=1)
            first = win[0:WINDOW]
            second = win[WINDOW:2 * WINDOW]
            dkv_ref[...] = carry[...] + first
            carry[...] = jnp.where(n == 0, first, second)

        @pl.when(n == nb)
        def _():
            dkv_ref[...] = carry[...]

    hm = lambda heads: pl.BlockSpec((heads, WINDOW, HEAD_DIM), lambda n: (0, jnp.minimum(n, last), 0))
    res = lambda heads: pl.BlockSpec((heads, s, HEAD_DIM), lambda n: (0, 0, 0))
    return pl.pallas_call(
        body, name=name, grid=(nb + 1,),
        in_specs=[smem, smem, hm(A_Q_HEADS), res(A_KV_HEADS), res(A_KV_HEADS), hm(A_Q_HEADS), hm(A_Q_HEADS), hm(A_Q_HEADS)],
        out_specs=[pl.BlockSpec((WINDOW, A_WIDTH), lambda n: (jnp.minimum(n, last), 0)),
                   pl.BlockSpec((WINDOW, 2 * A_KV_WIDTH), lambda n: (jnp.maximum(n - 1, 0), 0)),
                   pl.BlockSpec((A_Q_HEADS, 8, HEAD_DIM), lambda n: (0, 0, 0))],
        out_shape=[jax.ShapeDtypeStruct((s, A_WIDTH), F32), jax.ShapeDtypeStruct((s, 2 * A_KV_WIDTH), F32),
                   jax.ShapeDtypeStruct((A_Q_HEADS, 8, HEAD_DIM), F32)],
        scratch_shapes=[pltpu.VMEM((WINDOW, 2 * A_KV_WIDTH), F32)],
        compiler_params=_params(("arbitrary",), VMEM_BIG),
    )(sinks, slopes, q, k, v, do, lse, dd)


def _attn_b_fwd(q, k, v, c3, name):
    heads, s, _ = q.shape
    bq = min(512, s)
    nq = s // bq
    nt = (((1,), (1,)), ((), ()))

    def body(q_ref, k_ref, v_ref, c_ref, o_ref, lse_ref, m_scr, l_scr, acc_scr):
        i = pl.program_id(1)
        r0 = pl.multiple_of(i * bq, bq)
        row = lax.broadcasted_iota(jnp.int32, (bq, bq), 0)
        col = lax.broadcasted_iota(jnp.int32, (bq, bq), 1)
        outs = []
        for h2 in range(2):
            qv = q_ref[h2]
            cq0 = c_ref[h2, :, pl.ds(r0, LANES)][:, 0:1]
            m_scr[...] = jnp.full((bq, LANES), NEG, F32)
            l_scr[...] = jnp.zeros((bq, LANES), F32)
            acc_scr[...] = jnp.zeros((bq, HEAD_DIM), F32)

            def step(j, masked):
                k0 = pl.multiple_of(j * bq, bq)
                kv = k_ref[h2, pl.ds(k0, bq), :]
                vv = v_ref[h2, pl.ds(k0, bq), :]
                sc = lax.dot_general(qv, kv, nt, preferred_element_type=F32)
                sc = sc + (cq0 - c_ref[h2, :, pl.ds(k0, bq)])
                if masked:
                    sc = jnp.where(col <= row, sc, NEG)
                m_prev = m_scr[...]
                m_new = jnp.maximum(m_prev, jnp.max(sc, axis=1, keepdims=True))
                alpha = jnp.exp(m_prev - m_new)
                p = jnp.exp(sc - m_new[:, 0:1])
                l_scr[...] = alpha * l_scr[...] + jnp.sum(p, axis=1, keepdims=True)
                acc_scr[...] = acc_scr[...] * alpha[:, 0:HEAD_DIM] + jnp.dot(p.astype(BF16), vv, preferred_element_type=F32)
                m_scr[...] = m_new

            def loop_body(j, carry):
                step(j, False)
                return carry

            lax.fori_loop(0, i, loop_body, 0)
            step(i, True)
            l = l_scr[...]
            outs.append(acc_scr[...] / l[:, 0:HEAD_DIM])
            lse_ref[h2] = (m_scr[...] + jnp.log(l))[:, 0:HEAD_DIM]
        o_ref[...] = jnp.concatenate(outs, axis=1)

    res = pl.BlockSpec((2, s, HEAD_DIM), lambda hp, i: (hp, 0, 0))
    return pl.pallas_call(
        body, name=name, grid=(heads // 2, nq),
        in_specs=[pl.BlockSpec((2, bq, HEAD_DIM), lambda hp, i: (hp, i, 0)), res, res,
                  pl.BlockSpec((2, 1, s), lambda hp, i: (hp, 0, 0))],
        out_specs=[pl.BlockSpec((bq, 2 * HEAD_DIM), lambda hp, i: (i, hp)),
                   pl.BlockSpec((2, bq, HEAD_DIM), lambda hp, i: (hp, i, 0))],
        out_shape=[jax.ShapeDtypeStruct((s, heads * HEAD_DIM), F32), jax.ShapeDtypeStruct((heads, s, HEAD_DIM), F32)],
        scratch_shapes=[pltpu.VMEM((bq, LANES), F32), pltpu.VMEM((bq, LANES), F32), pltpu.VMEM((bq, HEAD_DIM), F32)],
        compiler_params=_params(("parallel", "parallel"), VMEM_BIG),
    )(q, k, v, c3)


def _attn_b_bwd(q, k, v, do, lse, dd, c3, name):
    heads, s, _ = q.shape
    bq = min(512, s)
    nq = s // bq
    nt = (((1,), (1,)), ((), ()))
    tn = (((0,), (0,)), ((), ()))

    def body(q_ref, k_ref, v_ref, do_ref, lse_ref, dd_ref, c_ref, dq_ref, dk_ref, dv_ref, dc_ref,
             dq_scr, dk_scr, dv_scr, dc_scr):
        j = pl.program_id(1)
        k0 = pl.multiple_of(j * bq, bq)
        row = lax.broadcasted_iota(jnp.int32, (bq, bq), 0)
        col = lax.broadcasted_iota(jnp.int32, (bq, bq), 1)

        @pl.when(j == 0)
        def _():
            dq_scr[...] = jnp.zeros(dq_scr.shape, F32)

        dks, dvs = [], []
        for h2 in range(2):
            kv = k_ref[h2]
            vv = v_ref[h2]
            c_k = c_ref[h2, :, pl.ds(k0, bq)]
            dk_scr[...] = jnp.zeros((bq, HEAD_DIM), F32)
            dv_scr[...] = jnp.zeros((bq, HEAD_DIM), F32)
            dc_scr[...] = jnp.zeros((1, bq), F32)

            def step(i, masked):
                r0 = pl.multiple_of(i * bq, bq)
                qv = q_ref[h2, pl.ds(r0, bq), :]
                dov = do_ref[h2, pl.ds(r0, bq), :]
                lse_v = lse_ref[h2, pl.ds(r0, bq), :][:, 0:1]
                dd_v = dd_ref[h2, pl.ds(r0, bq), :][:, 0:1]
                cq0 = c_ref[h2, :, pl.ds(r0, LANES)][:, 0:1]
                sc = lax.dot_general(qv, kv, nt, preferred_element_type=F32) + (cq0 - c_k)
                if masked:
                    sc = jnp.where(col <= row, sc, NEG)
                p = jnp.exp(sc - lse_v)
                dp = lax.dot_general(dov, vv, nt, preferred_element_type=F32)
                dsc = p * (dp - dd_v)
                dsb = dsc.astype(BF16)
                dv_scr[...] += lax.dot_general(p.astype(BF16), dov, tn, preferred_element_type=F32)
                dk_scr[...] += lax.dot_general(dsb, qv, tn, preferred_element_type=F32)
                dq_scr[h2, pl.ds(r0, bq), :] += jnp.dot(dsb, kv, preferred_element_type=F32)
                dc_scr[...] -= jnp.sum(dsc, axis=0, keepdims=True)

            def loop_body(i, carry):
                step(i, False)
                return carry

            step(j, True)
            lax.fori_loop(j + 1, nq, loop_body, 0)
            dks.append(dk_scr[...])
            dvs.append(dv_scr[...])
            dc_ref[h2] = dc_scr[...]
        dk_ref[...] = jnp.concatenate(dks, axis=1)
        dv_ref[...] = jnp.concatenate(dvs, axis=1)

        @pl.when(j == nq - 1)
        def _():
            dq_ref[...] = jnp.concatenate([dq_scr[0], dq_scr[1]], axis=1)

    res = pl.BlockSpec((2, s, HEAD_DIM), lambda hp, j: (hp, 0, 0))
    blk = pl.BlockSpec((2, bq, HEAD_DIM), lambda hp, j: (hp, j, 0))
    tm = jax.ShapeDtypeStruct((s, heads * HEAD_DIM), F32)
    return pl.pallas_call(
        body, name=name, grid=(heads // 2, nq),
        in_specs=[res, blk, blk, res, res, res, pl.BlockSpec((2, 1, s), lambda hp, j: (hp, 0, 0))],
        out_specs=[pl.BlockSpec((s, 2 * HEAD_DIM), lambda hp, j: (0, hp)),
                   pl.BlockSpec((bq, 2 * HEAD_DIM), lambda hp, j: (j, hp)),
                   pl.BlockSpec((bq, 2 * HEAD_DIM), lambda hp, j: (j, hp)),
                   pl.BlockSpec((2, 1, bq), lambda hp, j: (hp, 0, j))],
        out_shape=[tm, tm, tm, jax.ShapeDtypeStruct((heads, 1, s), F32)],
        scratch_shapes=[pltpu.VMEM((2, s, HEAD_DIM), F32), pltpu.VMEM((bq, HEAD_DIM), F32),
                        pltpu.VMEM((bq, HEAD_DIM), F32), pltpu.VMEM((1, bq), F32)],
        compiler_params=_params(("parallel", "arbitrary"), VMEM_BIG),
    )(q, k, v, do, lse, dd, c3)


def _attn_c_probs(qh, mkh):
    sc = lax.dot_general(qh, mkh, (((1,), (1,)), ((), ())), preferred_element_type=F32) * (C_HEAD_DIM ** -0.5)
    p = jnp.exp(sc - jnp.max(sc, axis=1, keepdims=True))
    return p / jnp.sum(p, axis=1, keepdims=True)


def _attn_c_fwd(q, mkv, name):
    s = q.shape[0]
    m = mkv.shape[0]
    bq = _tile(s, 512, 8)

    def body(q_ref, mk_ref, mv_ref, o_ref):
        outs = []
        for h in range(C_HEADS):
            sl = slice(h * C_HEAD_DIM, (h + 1) * C_HEAD_DIM)
            pn = _attn_c_probs(q_ref[:, sl], mk_ref[:, sl]).astype(BF16)
            outs.append(jnp.dot(pn, mv_ref[:, sl], preferred_element_type=F32))
        o_ref[...] = jnp.concatenate(outs, axis=1)

    return pl.pallas_call(
        body, name=name, grid=(s // bq,),
        in_specs=[pl.BlockSpec((bq, C_WIDTH), lambda i: (i, 0)), pl.BlockSpec((m, C_WIDTH), lambda i: (0, 0)),
                  pl.BlockSpec((m, C_WIDTH), lambda i: (0, 1))],
        out_specs=pl.BlockSpec((bq, C_WIDTH), lambda i: (i, 0)),
        out_shape=jax.ShapeDtypeStruct((s, C_WIDTH), F32),
        compiler_params=_params(("parallel",)),
    )(q, mkv, mkv)


def _attn_c_bwd(q, mkv, do, name):
    s = q.shape[0]
    m = mkv.shape[0]
    bq = _tile(s, 512, 8)
    tn = (((0,), (0,)), ((), ()))

    def body(q_ref, mk_ref, mv_ref, do_ref, dq_ref, dm_ref):
        i = pl.program_id(0)

        @pl.when(i == 0)
        def _():
            dm_ref[...] = jnp.zeros(dm_ref.shape, F32)

        dqs = []
        for h in range(C_HEADS):
            sl = slice(h * C_HEAD_DIM, (h + 1) * C_HEAD_DIM)
            qh, mkh, mvh, doh = q_ref[:, sl], mk_ref[:, sl], mv_ref[:, sl], do_ref[:, sl]
            pn = _attn_c_probs(qh, mkh)
            dp = lax.dot_general(doh, mvh, (((1,), (1,)), ((), ())), preferred_element_type=F32)
            dsc = (pn * (dp - jnp.sum(pn * dp, axis=1, keepdims=True)) * (C_HEAD_DIM ** -0.5)).astype(BF16)
            dqs.append(jnp.dot(dsc, mkh, preferred_element_type=F32))
            dm_ref[:, sl] += lax.dot_general(dsc, qh, tn, preferred_element_type=F32)
            sv = slice(C_WIDTH + h * C_HEAD_DIM, C_WIDTH + (h + 1) * C_HEAD_DIM)
            dm_ref[:, sv] += lax.dot_general(pn.astype(BF16), doh, tn, preferred_element_type=F32)
        dq_ref[...] = jnp.concatenate(dqs, axis=1)

    row = pl.BlockSpec((bq, C_WIDTH), lambda i: (i, 0))
    return pl.pallas_call(
        body, name=name, grid=(s // bq,),
        in_specs=[row, pl.BlockSpec((m, C_WIDTH), lambda i: (0, 0)), pl.BlockSpec((m, C_WIDTH), lambda i: (0, 1)), row],
        out_specs=[row, pl.BlockSpec((m, 2 * C_WIDTH), lambda i: (0, 0))],
        out_shape=[jax.ShapeDtypeStruct((s, C_WIDTH), F32), jax.ShapeDtypeStruct((m, 2 * C_WIDTH), F32)],
        compiler_params=_params(("arbitrary",)),
    )(q, mkv, mkv, do)


def _gate_fwd(y, proj, zc0, bw, name):
    rows, width = y.shape
    bm = _tile(rows, 1024, 8)
    cb0 = zc0 // bw

    def body(y_ref, z_ref, o_ref):
        z = z_ref[...]
        o_ref[...] = (y_ref[...] * (z * _sigmoid(z))).astype(BF16)

    return pl.pallas_call(
        body, name=name, grid=(rows // bm, width // bw),
        in_specs=[pl.BlockSpec((bm, bw), lambda i, t: (i, t)), pl.BlockSpec((bm, bw), lambda i, t: (i, cb0 + t))],
        out_specs=pl.BlockSpec((bm, bw), lambda i, t: (i, t)),
        out_shape=jax.ShapeDtypeStruct((rows, width), BF16),
        compiler_params=_params(("parallel", "parallel")),
    )(y, proj)


def _gate_bwd(dsv, y, proj, zc0, bw, dproj, head_major, name):
    rows, width = y.shape
    bm = _tile(rows, 1024, 8)
    cb0 = zc0 // bw
    bd = _block_diag(HEAD_DIM)
    hpb = bw // HEAD_DIM

    def body(*refs):
        if head_major:
            ds_ref, y_ref, z_ref, bd_ref, _, dp_ref, dy_ref, dd_ref = refs
        else:
            ds_ref, y_ref, z_ref, _, dp_ref, dy_ref = refs
        z = z_ref[...]
        sig = _sigmoid(z)
        dsx = ds_ref[...]
        yv = y_ref[...]
        dy = dsx * (z * sig)
        dp_ref[...] = (dsx * yv * (sig * (1.0 + z * (1.0 - sig)))).astype(BF16)
        if head_major:
            dd = _seg_sum(dy * yv, bd_ref[...])
            dyb = dy.astype(BF16)
            for h in range(hpb):
                sl = slice(h * HEAD_DIM, (h + 1) * HEAD_DIM)
                dy_ref[h] = dyb[:, sl]
                dd_ref[h] = dd[:, sl]
        else:
            dy_ref[...] = dy.astype(BF16)

    tile = pl.BlockSpec((bm, bw), lambda i, t: (i, t))
    ztile = pl.BlockSpec((bm, bw), lambda i, t: (i, cb0 + t))
    any_spec = pl.BlockSpec(memory_space=pl.ANY)
    dp_shape = jax.ShapeDtypeStruct(dproj.shape, BF16)
    if head_major:
        hm_spec = pl.BlockSpec((hpb, bm, HEAD_DIM), lambda i, t: (t, i, 0))
        nh = width // HEAD_DIM
        outs = pl.pallas_call(
            body, name=name, grid=(rows // bm, width // bw),
            in_specs=[tile, tile, ztile, pl.BlockSpec((LANES, LANES), lambda i, t: (0, 0)), any_spec],
            out_specs=[ztile, hm_spec, hm_spec],
            out_shape=[dp_shape, jax.ShapeDtypeStruct((nh, rows, HEAD_DIM), BF16),
                       jax.ShapeDtypeStruct((nh, rows, HEAD_DIM), F32)],
            input_output_aliases={4: 0},
            compiler_params=_params(("parallel", "parallel")),
        )(dsv, y, proj, bd, dproj)
        return outs[0], outs[1], outs[2]
    outs = pl.pallas_call(
        body, name=name, grid=(rows // bm, width // bw),
        in_specs=[tile, tile, ztile, any_spec],
        out_specs=[ztile, tile],
        out_shape=[dp_shape, jax.ShapeDtypeStruct((rows, width), BF16)],
        input_output_aliases={3: 0},
        compiler_params=_params(("parallel", "parallel")),
    )(dsv, y, proj, dproj)
    return outs[0], outs[1], None


def _merge_fwd(proj, ua, ub, uc, name):
    rows, d = ua.shape
    bm = _tile(rows, 512, 8)
    bw = _tile(d, 512)
    g0 = COL_GATE // bw
    gstep = d // bw

    def body(ga_ref, gb_ref, gc_ref, ua_ref, ub_ref, uc_ref, o_ref):
        y = _sigmoid(ga_ref[...]) * ua_ref[...] + _sigmoid(gb_ref[...]) * ub_ref[...] + _sigmoid(gc_ref[...]) * uc_ref[...]
        o_ref[...] = y.astype(BF16)

    tile = pl.BlockSpec((bm, bw), lambda i, t: (i, t))
    gate = lambda b: pl.BlockSpec((bm, bw), lambda i, t: (i, g0 + b * gstep + t))
    return pl.pallas_call(
        body, name=name, grid=(rows // bm, d // bw),
        in_specs=[gate(0), gate(1), gate(2), tile, tile, tile],
        out_specs=tile, out_shape=jax.ShapeDtypeStruct((rows, d), BF16),
        compiler_params=_params(("parallel", "parallel")),
    )(proj, proj, proj, ua, ub, uc)


def _merge_bwd(dym, u, proj, branch, dproj, name):
    rows, d = u.shape
    bm = _tile(rows, 512, 8)
    bw = _tile(d, 512)
    gb0 = (COL_GATE + branch * d) // bw

    def body(dy_ref, u_ref, gl_ref, _, dp_ref, du_ref):
        g = _sigmoid(gl_ref[...])
        dyv = dy_ref[...]
        du_ref[...] = (g * dyv).astype(BF16)
        dp_ref[...] = (dyv * u_ref[...] * g * (1.0 - g)).astype(BF16)

    tile = pl.BlockSpec((bm, bw), lambda i, t: (i, t))
    gtile = pl.BlockSpec((bm, bw), lambda i, t: (i, gb0 + t))
    outs = pl.pallas_call(
        body, name=name, grid=(rows // bm, d // bw),
        in_specs=[tile, tile, gtile, pl.BlockSpec(memory_space=pl.ANY)],
        out_specs=[gtile, tile],
        out_shape=[jax.ShapeDtypeStruct(dproj.shape, BF16), jax.ShapeDtypeStruct((rows, d), BF16)],
        input_output_aliases={3: 0},
        compiler_params=_params(("parallel", "parallel")),
    )(dym, u, proj, dproj)
    return outs[0], outs[1]


def _loss_head(y, target, name):
    rows, d = y.shape
    bm = _tile(rows, 256, 8)

    def body(y_ref, t_ref, dy_ref, dyb_ref, l_ref):
        i = pl.program_id(0)
        diff = y_ref[...] - t_ref[...]
        dy = diff * (1.0 / d)
        dy_ref[...] = dy
        dyb_ref[...] = dy.astype(BF16)
        sq = diff * diff
        part = sq[:, 0:LANES]
        for c in range(1, d // LANES):
            part = part + sq[:, c * LANES:(c + 1) * LANES]
        part = jnp.sum(part.reshape(bm // 8, 8, LANES), axis=0)

        @pl.when(i == 0)
        def _():
            l_ref[...] = part

        @pl.when(i > 0)
        def _():
            l_ref[...] += part

    row = pl.BlockSpec((bm, d), lambda i: (i, 0))
    return pl.pallas_call(
        body, name=name, grid=(rows // bm,), in_specs=[row, row],
        out_specs=[row, row, pl.BlockSpec((8, LANES), lambda i: (0, 0))],
        out_shape=[jax.ShapeDtypeStruct((rows, d), F32), jax.ShapeDtypeStruct((rows, d), BF16),
                   jax.ShapeDtypeStruct((8, LANES), F32)],
        compiler_params=_params(("arbitrary",)),
    )(y, target)


def _row(vec, reps=1):
    return jnp.tile(vec.reshape(1, -1).astype(F32), (1, reps))


def _local_step(x, mem, target, small, wg):
    s, d = x.shape
    dsz = d // N_DEV
    ones = lambda n: jnp.ones((1, n), F32)
    zeros = lambda n: jnp.zeros((1, n), F32)
    scale_ab = HEAD_DIM ** -0.5

    hn = _rmsnorm_fwd(x, small["norm_gain"], "rms_x_fwd")
    proj = _mm_nn(hn, wg["wm"], bm=1024, bn=1024, bk=d, o_dtype=F32, name="proj_main")
    pfb = _mm_nn(hn, wg["wf"], bm=1024, bn=FB_PAD, bk=d, o_dtype=F32, name="proj_fb")
    mn = _rmsnorm_fwd(mem, small["mem_norm_gain"], "rms_mem_fwd")
    mkv = _mm_nn(mn, wg["wk"], bm=256, bn=1024, bk=d, o_dtype=F32, name="mem_kv")

    gain_a = jnp.concatenate([_row(small["q_gain_a"], A_Q_HEADS) * scale_ab, _row(small["k_gain_a"], A_KV_HEADS), ones(A_KV_WIDTH)], axis=1)
    flag_a = jnp.concatenate([ones(A_WIDTH + A_KV_WIDTH), zeros(A_KV_WIDTH)], axis=1)
    qkv_a = _headnorm_fwd(proj, COL_QA, 1280, 1280, HEAD_DIM, gain_a, flag_a, True, "hn_a_fwd")
    gain_b = jnp.concatenate([_row(small["q_gain_b"], B_HEADS) * scale_ab, _row(small["k_gain_b"], B_HEADS), ones(B_WIDTH)], axis=1)
    flag_b = jnp.concatenate([ones(2 * B_WIDTH), zeros(B_WIDTH)], axis=1)
    qkv_b = _headnorm_fwd(proj, COL_QB, 2304, 256, HEAD_DIM, gain_b, flag_b, True, "hn_b_fwd")
    gain_cq = _row(small["q_gain_c"], C_HEADS)
    q_c = _headnorm_fwd(proj, COL_QC, C_WIDTH, C_WIDTH, C_HEAD_DIM, gain_cq, ones(C_WIDTH), False, "hn_cq_fwd")
    gain_ck = jnp.concatenate([_row(small["k_gain_c"], C_HEADS), ones(C_WIDTH)], axis=1)
    flag_ck = jnp.concatenate([ones(C_WIDTH), zeros(C_WIDTH)], axis=1)
    mkvn = _headnorm_fwd(mkv, 0, 2 * C_WIDTH, 2 * C_WIDTH, C_HEAD_DIM, gain_ck, flag_ck, False, "hn_ck_fwd")

    q_a, k_a, v_a = qkv_a[0:12], qkv_a[12:16], qkv_a[16:20]
    q_b, k_b, v_b = qkv_b[0:12], qkv_b[12:24], qkv_b[24:36]

    bpad = jnp.pad(small["b_forget"].reshape(1, -1), ((0, 0), (0, FB_PAD - B_HEADS)))
    c16 = _fox_prep(pfb, bpad, "fox_prep")
    c3 = c16[0:B_HEADS].reshape(B_HEADS, 1, s)

    sinks = small["sinks_a"].reshape(-1)
    slopes = jnp.exp2(-8.0 * jnp.arange(1, A_Q_HEADS + 1, dtype=F32) / A_Q_HEADS)
    y_a, lse_a = _attn_a_fwd(q_a, k_a, v_a, sinks, slopes, "attn_a_fwd")
    y_b, lse_b = _attn_b_fwd(q_b, k_b, v_b, c3, "attn_b_fwd")
    y_c = _attn_c_fwd(q_c, mkvn, "attn_c_fwd")

    s_a = _gate_fwd(y_a, proj, COL_ZA, 256, "gate_a_fwd")
    s_b = _gate_fwd(y_b, proj, COL_ZB, 256, "gate_b_fwd")
    s_c = _gate_fwd(y_c, proj, COL_ZC, 512, "gate_c_fwd")
    u_a = _mm_branch_fwd(s_a, wg["wa"], "branch_a_fwd")
    u_b = _mm_branch_fwd(s_b, wg["wb"], "branch_b_fwd")
    u_c = _mm_branch_fwd(s_c, wg["wc"], "branch_c_fwd")
    ym = _merge_fwd(proj, u_a, u_b, u_c, "merge_fwd")
    y = _mm_nn(ym, wg["wo"], bm=1024, bn=1024, bk=d, o_dtype=F32, name="out_proj", add=x)
    dy, dyb, lpart = _loss_head(y, target, "loss_head")
    loss = 0.5 / d * jnp.sum(lpart)

    dym = _mm_nt(dyb, wg["wo"], bm=1024, bn=1024, bk=d, o_dtype=F32, name="out_proj_bwd_act")
    g_wo = _mm_tn(ym, dyb, bm=512, bn=1024, bk=s, o_dtype=BF16, name="out_proj_bwd_w")

    dproj = jnp.zeros(proj.shape, BF16)
    dproj, du_a = _merge_bwd(dym, u_a, proj, 0, dproj, "merge_a_bwd")
    dproj, du_b = _merge_bwd(dym, u_b, proj, 1, dproj, "merge_b_bwd")
    dproj, du_c = _merge_bwd(dym, u_c, proj, 2, dproj, "merge_c_bwd")

    ds_a = _mm_branch_bwd_act(du_a, wg["wa"], A_WIDTH, "branch_a_bwd_act")
    ds_b = _mm_branch_bwd_act(du_b, wg["wb"], B_WIDTH, "branch_b_bwd_act")
    ds_c = _mm_branch_bwd_act(du_c, wg["wc"], C_WIDTH, "branch_c_bwd_act")
    g_wa = _mm_branch_bwd_w(s_a, du_a, "branch_a_bwd_w")
    g_wb = _mm_branch_bwd_w(s_b, du_b, "branch_b_bwd_w")
    g_wc = _mm_branch_bwd_w(s_c, du_c, "branch_c_bwd_w")

    dproj, do_a, dd_a = _gate_bwd(ds_a, y_a, proj, COL_ZA, 256, dproj, True, "gate_a_bwd")
    dproj, do_b, dd_b = _gate_bwd(ds_b, y_b, proj, COL_ZB, 256, dproj, True, "gate_b_bwd")
    dproj, do_c, _ = _gate_bwd(ds_c, y_c, proj, COL_ZC, 512, dproj, False, "gate_c_bwd")

    dq_a, dkv_a, dsink = _attn_a_bwd(q_a, k_a, v_a, do_a, lse_a, dd_a, sinks, slopes, "attn_a_bwd")
    dq_b, dk_b, dv_b, dc3 = _attn_b_bwd(q_b, k_b, v_b, do_b, lse_b, dd_b, c3, "attn_b_bwd")
    dq_c, dmkvn = _attn_c_bwd(q_c, mkvn, do_c, "attn_c_bwd")

    dproj, dg_qa = _headnorm_bwd(proj, COL_QA, A_WIDTH, 256, HEAD_DIM, gain_a[:, 0:768], flag_a[:, 0:768], dq_a, dproj, COL_QA, "hn_qa_bwd")
    dproj, dg_kva = _headnorm_bwd(proj, COL_KA, 512, 256, HEAD_DIM, gain_a[:, 768:1280], flag_a[:, 768:1280], dkv_a, dproj, COL_KA, "hn_kva_bwd")
    dproj, dg_qb = _headnorm_bwd(proj, COL_QB, B_WIDTH, 256, HEAD_DIM, gain_b[:, 0:768], flag_b[:, 0:768], dq_b, dproj, COL_QB, "hn_qb_bwd")
    dproj, dg_kb = _headnorm_bwd(proj, COL_KB, B_WIDTH, 256, HEAD_DIM, gain_b[:, 768:1536], flag_b[:, 768:1536], dk_b, dproj, COL_KB, "hn_kb_bwd")
    dproj, _ = _headnorm_bwd(proj, COL_VB, B_WIDTH, 256, HEAD_DIM, gain_b[:, 1536:2304], flag_b[:, 1536:2304], dv_b, dproj, COL_VB, "hn_vb_bwd")
    dproj, dg_qc = _headnorm_bwd(proj, COL_QC, C_WIDTH, 512, C_HEAD_DIM, gain_cq, ones(C_WIDTH), dq_c, dproj, COL_QC, "hn_qc_bwd")
    dmkv, dg_kc = _headnorm_bwd(mkv, 0, 2 * C_WIDTH, 2 * C_WIDTH, C_HEAD_DIM, gain_ck, flag_ck, dmkvn, None, 0, "hn_kc_bwd")

    dct = jnp.pad(dc3.reshape(B_HEADS, s), ((0, 16 - B_HEADS), (0, 0)))
    dfb, dbf = _fox_prep_bwd(pfb, bpad, dct, "fox_prep_bwd")

    dmn = _mm_nt(dmkv, wg["wk"], bm=256, bn=1024, bk=1024, o_dtype=F32, name="mem_kv_bwd_act")
    g_wk = _mm_tn(mn, dmkv, bm=512, bn=1024, bk=mem.shape[0], o_dtype=BF16, name="mem_kv_bwd_w")
    _, dg_mem = _rmsnorm_bwd(mem, dmn, small["mem_norm_gain"], None, "rms_mem_bwd")

    dhn_fb = _mm_nt(dfb, wg["wf"], bm=1024, bn=1024, bk=FB_PAD, o_dtype=F32, name="proj_fb_bwd_act")
    dhn = _mm_nt(dproj, wg["wm"], bm=1024, bn=1024, bk=2048, o_dtype=F32, name="proj_main_bwd_act", add=dhn_fb)
    g_wm = _mm_tn(hn, dproj, bm=512, bn=1024, bk=s, o_dtype=BF16, name="proj_main_bwd_w")
    g_wf = _mm_tn(hn, dfb, bm=512, bn=FB_PAD, bk=s, o_dtype=BF16, name="proj_fb_bwd_w")
    grad_x, dg_x = _rmsnorm_bwd(x, dhn, small["norm_gain"], dy, "rms_x_bwd")

    fold = lambda part, heads, hd: jnp.sum(jnp.sum(part, axis=0).reshape(heads, hd), axis=0).reshape(1, hd)
    small_grads = {
        "norm_gain": jnp.sum(dg_x, axis=0).reshape(1, d),
        "mem_norm_gain": jnp.sum(dg_mem, axis=0).reshape(1, d),
        "b_forget": dbf[0:B_HEADS, 0].reshape(1, B_HEADS),
        "q_gain_a": fold(dg_qa, A_Q_HEADS, HEAD_DIM) * scale_ab,
        "k_gain_a": fold(dg_kva[:, 0:A_KV_WIDTH], A_KV_HEADS, HEAD_DIM),
        "sinks_a": (jnp.sum(dsink, axis=(1, 2)) * (1.0 / HEAD_DIM)).reshape(1, A_Q_HEADS),
        "q_gain_b": fold(dg_qb, B_HEADS, HEAD_DIM) * scale_ab,
        "k_gain_b": fold(dg_kb, B_HEADS, HEAD_DIM),
        "q_gain_c": fold(dg_qc, C_HEADS, C_HEAD_DIM),
        "k_gain_c": fold(dg_kc[:, 0:C_WIDTH], C_HEADS, C_HEAD_DIM),
    }
    big_grads = {"wm": g_wm, "wf": g_wf, "wk": g_wk, "wo": g_wo, "wa": g_wa, "wb": g_wb, "wc": g_wc}
    del dsz
    return loss, grad_x, small_grads, big_grads


def _coords():
    return lax.axis_index("x"), lax.axis_index("y"), lax.axis_index("c")


def _all_gather(shards, name):
    n = len(shards)

    def body(*refs):
        ins = refs[0:n]
        outs = refs[n:2 * n]
        send_sems, recv_sems, local_sems = refs[2 * n:2 * n + 3]
        x, y, c = _coords()
        me, sibling = (x, y, c), (x, y, 1 - c)
        chips = [(1 - x, y), (x, 1 - y), (1 - x, 1 - y)]
        idx = lambda p: 4 * p[0] + 2 * p[1] + p[2]

        def copy(a, k, block, to, src=None):
            slot = outs[a].at[idx(block)]
            return pltpu.make_async_remote_copy(
                src_ref=slot if src is None else src, dst_ref=slot,
                send_sem=send_sems.at[a, k], recv_sem=recv_sems.at[a, k], device_id=to, device_id_type=MESH)

        mine = [pltpu.make_async_copy(ins[a], outs[a].at[idx(me)], local_sems.at[a]) for a in range(n)]
        for cp in mine:
            cp.start()
        first = []
        for a in range(n):
            first.append(copy(a, 0, me, sibling, src=ins[a]))
            first += [copy(a, 1 + j, me, (*chip, c), src=ins[a]) for j, chip in enumerate(chips)]
        for cp in first:
            cp.start()
        passed = []
        for j, chip in enumerate(chips):
            for a in range(n):
                copy(a, 1 + j, (*chip, c), me).wait_recv()
                fwd = copy(a, 4 + j, (*chip, c), sibling)
                fwd.start()
                passed.append(fwd)
        for a in range(n):
            copy(a, 0, sibling, me).wait_recv()
            for j, chip in enumerate(chips):
                copy(a, 4 + j, (*chip, 1 - c), me).wait_recv()
        for cp in first + passed:
            cp.wait_send()
        for cp in mine:
            cp.wait()

    any_spec = pl.BlockSpec(memory_space=pl.ANY)
    return pl.pallas_call(
        body, name=name,
        in_specs=[any_spec] * n, out_specs=[any_spec] * n,
        out_shape=[jax.ShapeDtypeStruct((N_DEV,) + sh.shape, sh.dtype) for sh in shards],
        scratch_shapes=[pltpu.SemaphoreType.DMA((n, 7)), pltpu.SemaphoreType.DMA((n, 7)), pltpu.SemaphoreType.DMA((n,))],
    )(*shards)


def _scatter_parts(grads, name):
    n = len(grads)

    def body(*refs):
        ins = refs[0:n]
        outs = refs[n:2 * n]
        send_sems, recv_sems, local_sems = refs[2 * n:2 * n + 3]
        x, y, c = _coords()
        my = 4 * x + 2 * y + c
        peers = [(x ^ ((k >> 2) & 1), y ^ ((k >> 1) & 1), c ^ (k & 1)) for k in range(1, N_DEV)]
        mine = [pltpu.make_async_copy(ins[a].at[my], outs[a].at[my], local_sems.at[a]) for a in range(n)]
        for cp in mine:
            cp.start()
        sends = []
        for a in range(n):
            for k, peer in enumerate(peers):
                pid = 4 * peer[0] + 2 * peer[1] + peer[2]
                sends.append(pltpu.make_async_remote_copy(
                    src_ref=ins[a].at[pid], dst_ref=outs[a].at[my],
                    send_sem=send_sems.at[a, k], recv_sem=recv_sems.at[a, k], device_id=peer, device_id_type=MESH))
        for cp in sends:
            cp.start()
        for a in range(n):
            for k, peer in enumerate(peers):
                pid = 4 * peer[0] + 2 * peer[1] + peer[2]
                pltpu.make_async_remote_copy(
                    src_ref=ins[a].at[pid], dst_ref=outs[a].at[pid],
                    send_sem=send_sems.at[a, k], recv_sem=recv_sems.at[a, k], device_id=peer, device_id_type=MESH).wait_recv()
        for cp in sends:
            cp.wait_send()
        for cp in mine:
            cp.wait()

    any_spec = pl.BlockSpec(memory_space=pl.ANY)
    return pl.pallas_call(
        body, name=name,
        in_specs=[any_spec] * n, out_specs=[any_spec] * n,
        out_shape=[jax.ShapeDtypeStruct(g.shape, g.dtype) for g in grads],
        scratch_shapes=[pltpu.SemaphoreType.DMA((n, 7)), pltpu.SemaphoreType.DMA((n, 7)), pltpu.SemaphoreType.DMA((n,))],
    )(*grads)


def _all_reduce_small(vec, name):
    p = vec.shape[1]

    def body(v_ref, o_ref, gather, send_sems, recv_sems):
        x, y, c = _coords()
        my = 4 * x + 2 * y + c
        peers = [(x ^ ((k >> 2) & 1), y ^ ((k >> 1) & 1), c ^ (k & 1)) for k in range(1, N_DEV)]
        gather[my] = v_ref[...]
        sends = [pltpu.make_async_remote_copy(
            src_ref=v_ref, dst_ref=gather.at[my], send_sem=send_sems.at[k], recv_sem=recv_sems.at[k],
            device_id=peer, device_id_type=MESH) for k, peer in enumerate(peers)]
        for cp in sends:
            cp.start()
        for k, peer in enumerate(peers):
            pid = 4 * peer[0] + 2 * peer[1] + peer[2]
            pltpu.make_async_remote_copy(
                src_ref=v_ref, dst_ref=gather.at[pid], send_sem=send_sems.at[k], recv_sem=recv_sems.at[k],
                device_id=peer, device_id_type=MESH).wait_recv()
        for cp in sends:
            cp.wait_send()
        total = gather[0]
        for j in range(1, N_DEV):
            total = total + gather[j]
        o_ref[...] = total

    vm = pl.BlockSpec(memory_space=pltpu.VMEM)
    return pl.pallas_call(
        body, name=name, in_specs=[vm], out_specs=vm,
        out_shape=jax.ShapeDtypeStruct((8, p), F32),
        scratch_shapes=[pltpu.VMEM((N_DEV, 8, p), F32), pltpu.SemaphoreType.DMA((7,)), pltpu.SemaphoreType.DMA((7,))],
    )(vec)[0:1]


def _sum_parts(parts, name):
    _, rows, cols = parts.shape
    br = _tile(rows, 64, 16)

    def body(p_ref, o_ref):
        total = p_ref[0].astype(F32)
        for j in range(1, N_DEV):
            total = total + p_ref[j].astype(F32)
        o_ref[...] = total

    return pl.pallas_call(
        body, name=name, grid=(rows // br,),
        in_specs=[pl.BlockSpec((N_DEV, br, cols), lambda i: (0, i, 0))],
        out_specs=pl.BlockSpec((br, cols), lambda i: (i, 0)),
        out_shape=jax.ShapeDtypeStruct((rows, cols), F32),
        compiler_params=_params(("parallel",), VMEM_BIG),
    )(parts)


def _adamw(w, g, m, v, name):
    rows, cols = w.shape
    br = _tile(rows, 32, 8)
    c1 = 1.0 / (1.0 - ADAM_B1 ** ADAM_STEP)
    c2 = 1.0 / (1.0 - ADAM_B2 ** ADAM_STEP)

    def body(w_ref, g_ref, m_ref, v_ref, d_ref, nm_ref, nv_ref):
        gv = g_ref[...]
        nm = ADAM_B1 * m_ref[...] + (1.0 - ADAM_B1) * gv
        nv = ADAM_B2 * v_ref[...] + (1.0 - ADAM_B2) * (gv * gv)
        d_ref[...] = -ADAM_LR * ((nm * c1) / (jnp.sqrt(nv * c2) + ADAM_EPS) + ADAM_WD * w_ref[...])
        nm_ref[...] = nm
        nv_ref[...] = nv

    spec = pl.BlockSpec((br, cols), lambda i: (i, 0))
    shape = jax.ShapeDtypeStruct((rows, cols), F32)
    return pl.pallas_call(
        body, name=name, grid=(rows // br,), in_specs=[spec] * 4, out_specs=[spec] * 3, out_shape=[shape] * 3,
        compiler_params=_params(("parallel",), VMEM_BIG),
    )(w, g, m, v)


SMALL_NAMES = ("norm_gain", "mem_norm_gain", "b_forget", "q_gain_a", "k_gain_a", "sinks_a",
               "q_gain_b", "k_gain_b", "q_gain_c", "k_gain_c")
BIG_NAMES = ("w_in", "w_mem_kv", "w_branch_a", "w_branch_b", "w_branch_c", "w_out")
WEIGHT_ORDER = ("norm_gain", "mem_norm_gain", "w_in", "b_forget", "q_gain_a", "k_gain_a", "sinks_a", "q_gain_b",
                "k_gain_b", "q_gain_c", "k_gain_c", "w_mem_kv", "w_branch_a", "w_branch_b", "w_branch_c", "w_out")


def _pack_small(tree):
    flat = jnp.concatenate([tree[n].reshape(1, -1) for n in SMALL_NAMES], axis=1)
    pad = (-flat.shape[1]) % LANES
    return jnp.pad(flat, ((0, 0), (0, pad)))


def _unpack_small(flat, like):
    out, off = {}, 0
    for n in SMALL_NAMES:
        size = like[n].size
        out[n] = flat[:, off:off + size].reshape(like[n].shape)
        off += size
    return out


def kernel(x, mem, norm_gain, mem_norm_gain, w_in, b_forget, q_gain_a, k_gain_a, sinks_a, q_gain_b, k_gain_b, q_gain_c, k_gain_c, w_mem_kv, w_branch_a, w_branch_b, w_branch_c, w_out, loss_target, m_norm_gain, m_mem_norm_gain, m_w_in, m_b_forget, m_q_gain_a, m_k_gain_a, m_sinks_a, m_q_gain_b, m_k_gain_b, m_q_gain_c, m_k_gain_c, m_w_mem_kv, m_w_branch_a, m_w_branch_b, m_w_branch_c, m_w_out, v_norm_gain, v_mem_norm_gain, v_w_in, v_b_forget, v_q_gain_a, v_k_gain_a, v_sinks_a, v_q_gain_b, v_k_gain_b, v_q_gain_c, v_k_gain_c, v_w_mem_kv, v_w_branch_a, v_w_branch_b, v_w_branch_c, v_w_out):
    weights = dict(norm_gain=norm_gain, mem_norm_gain=mem_norm_gain, w_in=w_in, b_forget=b_forget, q_gain_a=q_gain_a,
                   k_gain_a=k_gain_a, sinks_a=sinks_a, q_gain_b=q_gain_b, k_gain_b=k_gain_b, q_gain_c=q_gain_c,
                   k_gain_c=k_gain_c, w_mem_kv=w_mem_kv, w_branch_a=w_branch_a, w_branch_b=w_branch_b,
                   w_branch_c=w_branch_c, w_out=w_out)
    mom_m = dict(norm_gain=m_norm_gain, mem_norm_gain=m_mem_norm_gain, w_in=m_w_in, b_forget=m_b_forget,
                 q_gain_a=m_q_gain_a, k_gain_a=m_k_gain_a, sinks_a=m_sinks_a, q_gain_b=m_q_gain_b, k_gain_b=m_k_gain_b,
                 q_gain_c=m_q_gain_c, k_gain_c=m_k_gain_c, w_mem_kv=m_w_mem_kv, w_branch_a=m_w_branch_a,
                 w_branch_b=m_w_branch_b, w_branch_c=m_w_branch_c, w_out=m_w_out)
    mom_v = dict(norm_gain=v_norm_gain, mem_norm_gain=v_mem_norm_gain, w_in=v_w_in, b_forget=v_b_forget,
                 q_gain_a=v_q_gain_a, k_gain_a=v_k_gain_a, sinks_a=v_sinks_a, q_gain_b=v_q_gain_b, k_gain_b=v_k_gain_b,
                 q_gain_c=v_q_gain_c, k_gain_c=v_k_gain_c, w_mem_kv=v_w_mem_kv, w_branch_a=v_w_branch_a,
                 w_branch_b=v_w_branch_b, w_branch_c=v_w_branch_c, w_out=v_w_out)
    s, d = x.shape[1], x.shape[2]
    dsz = d // N_DEV

    wi = w_in[0]
    sh_wm = jnp.concatenate([wi[:, 0:FB_SRC], wi[:, FB_SRC + B_HEADS:]], axis=1).astype(BF16)
    sh_wf = jnp.pad(wi[:, FB_SRC:FB_SRC + B_HEADS], ((0, 0), (0, FB_PAD - B_HEADS))).astype(BF16)
    shards = [sh_wm, sh_wf, w_mem_kv[0].astype(BF16), w_out[0].astype(BF16),
              w_branch_a[0].astype(BF16), w_branch_b[0].astype(BF16), w_branch_c[0].astype(BF16)]
    full = _all_gather(shards, "weights_all_gather")
    keys = ("wm", "wf", "wk", "wo", "wa", "wb", "wc")
    wg = {kname: arr.reshape(arr.shape[0] * arr.shape[1], arr.shape[2]) for kname, arr in zip(keys, full)}

    small = {n: weights[n] for n in SMALL_NAMES}
    loss_local, grad_x, small_g, big_g = _local_step(x[0], mem[0], loss_target[0], small, wg)

    parts_in = [big_g[kname].reshape(N_DEV, big_g[kname].shape[0] // N_DEV, big_g[kname].shape[1]) for kname in keys]
    parts = _scatter_parts(parts_in, "grads_scatter")
    summed = {kname: _sum_parts(p, "grad_sum_" + kname) for kname, p in zip(keys, parts)}
    gm, gf = summed["wm"], summed["wf"]
    grads = {
        "w_in": jnp.concatenate([gm[:, 0:FB_SRC], gf[:, 0:B_HEADS], gm[:, FB_SRC:]], axis=1),
        "w_mem_kv": summed["wk"], "w_out": summed["wo"],
        "w_branch_a": summed["wa"], "w_branch_b": summed["wb"], "w_branch_c": summed["wc"],
    }

    packed = _pack_small(small_g)
    reduced = _all_reduce_small(jnp.broadcast_to(packed, (8, packed.shape[1])), "small_all_reduce")
    grads.update(_unpack_small(reduced, small))

    loss = lax.psum(loss_local, ("x", "y", "c"))

    delta, new_m, new_v = {}, {}, {}
    for n in BIG_NAMES:
        dlt, nm, nv = _adamw(weights[n][0], grads[n], mom_m[n][0], mom_v[n][0], "adamw_" + n)
        delta[n], new_m[n], new_v[n] = dlt[None], nm[None], nv[None]
    pw, pm, pv = _pack_small(small), _pack_small({n: mom_m[n] for n in SMALL_NAMES}), _pack_small({n: mom_v[n] for n in SMALL_NAMES})
    rep8 = lambda a: jnp.broadcast_to(a, (8, a.shape[1]))
    pg = jnp.pad(reduced, ((0, 0), (0, 0)))
    dlt, nm, nv = _adamw(rep8(pw), rep8(pg), rep8(pm), rep8(pv), "adamw_small")
    for tree, flat in ((delta, dlt), (new_m, nm), (new_v, nv)):
        tree.update(_unpack_small(flat[0:1], small))
    for n in BIG_NAMES:
        grads[n] = grads[n][None]
    del dsz, s
    return (loss, grad_x[None], *[grads[n] for n in WEIGHT_ORDER], *[delta[n] for n in WEIGHT_ORDER],
            *[new_m[n] for n in WEIGHT_ORDER], *[new_v[n] for n in WEIGHT_ORDER])
```

```python
import functools

import jax
import jax.numpy as jnp
import numpy as np
from jax import lax
from jax.experimental import pallas as pl
from jax.experimental.pallas import tpu as pltpu

F32 = jnp.float32
BF16 = jnp.bfloat16

N_DEV = 8
HEAD_DIM = 64
A_Q_HEADS = 12
A_KV_HEADS = 4
A_GROUP = 3
B_HEADS = 12
C_HEADS = 4
C_HEAD_DIM = 128
WINDOW = 128
A_WIDTH = 768
A_KV_WIDTH = 256
B_WIDTH = 768
C_WIDTH = 512
EPS = 1e-6
NEG = -1e30

COL_QA, COL_KA, COL_VA = 0, 768, 1024
COL_QB, COL_KB, COL_VB = 1280, 2048, 2816
COL_QC = 3584
COL_ZA, COL_ZB, COL_ZC = 4096, 4864, 5632
COL_GATE = 6144
W_QKV, W_Z = 4096, 2048
SRC_RANGES = ((0, 1280), (2048, 4352), (5132, 5644), (1280, 2048), (4352, 5120), (5644, 6156))
SRC_GATE = 6156
FB_SRC = 5120
FB_PAD = 128

ADAM_LR = 0.001
ADAM_B1 = 0.9
ADAM_B2 = 0.999
ADAM_EPS = 1e-08
ADAM_WD = 0.01
ADAM_STEP = 10

VMEM_BIG = 52 * 1024 * 1024
LANES = 128
MESH = pl.DeviceIdType.MESH


def _tile(n, pref, mult=128):
    if n <= pref:
        return n
    t = (pref // mult) * mult
    while t >= mult:
        if n % t == 0:
            return t
        t -= mult
    return n


def _params(sem=None, vmem=None):
    kw = {}
    if sem is not None:
        kw["dimension_semantics"] = sem
    if vmem is not None:
        kw["vmem_limit_bytes"] = vmem
    return pltpu.CompilerParams(**kw)


def _sigmoid(x):
    return 1.0 / (1.0 + jnp.exp(-x))


def _block_diag(hd):
    r = np.arange(LANES)
    return jnp.asarray((r[:, None] // hd) == (r[None, :] // hd), dtype=BF16)


def _seg_sum(t, bd):
    hi = t.astype(BF16)
    lo = (t - hi.astype(F32)).astype(BF16)
    outs = []
    for c in range(t.shape[1] // LANES):
        sl = slice(c * LANES, (c + 1) * LANES)
        outs.append(jnp.dot(hi[:, sl], bd, preferred_element_type=F32) + jnp.dot(lo[:, sl], bd, preferred_element_type=F32))
    return outs[0] if len(outs) == 1 else jnp.concatenate(outs, axis=1)


def _rmsnorm_fwd(x, gain, name):
    rows, d = x.shape
    bm = _tile(rows, 512, 8)

    def body(x_ref, g_ref, o_ref):
        xv = x_ref[...]
        ms = jnp.mean(xv * xv, axis=-1, keepdims=True)
        o_ref[...] = (xv * lax.rsqrt(ms + EPS) * g_ref[...]).astype(BF16)

    return pl.pallas_call(
        body, name=name, grid=(rows // bm,),
        in_specs=[pl.BlockSpec((bm, d), lambda i: (i, 0)), pl.BlockSpec((1, d), lambda i: (0, 0))],
        out_specs=pl.BlockSpec((bm, d), lambda i: (i, 0)),
        out_shape=jax.ShapeDtypeStruct((rows, d), BF16),
        compiler_params=_params(("parallel",)),
    )(x, gain)


def _rmsnorm_bwd(x, dhn, gain, dy, name):
    rows, d = x.shape
    bm = _tile(rows, 256, 8)
    with_dx = dy is not None

    def body(*refs):
        if with_dx:
            x_ref, dh_ref, g_ref, dy_ref, gx_ref, dg_ref = refs
        else:
            x_ref, dh_ref, g_ref, dg_ref = refs
        i = pl.program_id(0)
        xv = x_ref[...]
        rstd = lax.rsqrt(jnp.mean(xv * xv, axis=-1, keepdims=True) + EPS)
        xhat = xv * rstd
        dh = dh_ref[...]
        part = jnp.sum((dh * xhat).reshape(bm // 8, 8, d), axis=0)

        @pl.when(i == 0)
        def _():
            dg_ref[...] = part

        @pl.when(i > 0)
        def _():
            dg_ref[...] += part

        if with_dx:
            g = dh * g_ref[...]
            mean = jnp.mean(g * xhat, axis=-1, keepdims=True)
            gx_ref[...] = dy_ref[...] + rstd * (g - xhat * mean)

    row_spec = pl.BlockSpec((bm, d), lambda i: (i, 0))
    in_specs = [row_spec, row_spec, pl.BlockSpec((1, d), lambda i: (0, 0))]
    args = [x, dhn, gain]
    dg_spec = pl.BlockSpec((8, d), lambda i: (0, 0))
    dg_shape = jax.ShapeDtypeStruct((8, d), F32)
    if with_dx:
        in_specs.append(row_spec)
        args.append(dy)
        out_specs = [row_spec, dg_spec]
        out_shape = [jax.ShapeDtypeStruct((rows, d), F32), dg_shape]
    else:
        out_specs = [dg_spec]
        out_shape = [dg_shape]
    outs = pl.pallas_call(
        body, name=name, grid=(rows // bm,), in_specs=in_specs, out_specs=out_specs, out_shape=out_shape,
        compiler_params=_params(("arbitrary",)),
    )(*args)
    return outs if with_dx else (None, outs[0])


class _Comm:
    def __init__(self, kind, arrays):
        self.kind = kind
        self.arrays = list(arrays)
        self.n = len(self.arrays)

    def out_shapes(self):
        if self.kind == "gather":
            return [jax.ShapeDtypeStruct((N_DEV,) + a.shape, a.dtype) for a in self.arrays]
        return [jax.ShapeDtypeStruct(a.shape, a.dtype) for a in self.arrays]

    def scratch(self):
        return [pltpu.SemaphoreType.DMA((self.n, N_DEV - 1)), pltpu.SemaphoreType.DMA((self.n, N_DEV - 1)),
                pltpu.SemaphoreType.DMA((self.n,))]

    def _plan(self, ins, outs, sems):
        send_sems, recv_sems, local_sems = sems
        x, y, c = lax.axis_index("x"), lax.axis_index("y"), lax.axis_index("c")
        my = 4 * x + 2 * y + c
        gather = self.kind == "gather"
        local, sends, recvs = [], [], []
        for a in range(self.n):
            local.append(pltpu.make_async_copy(ins[a] if gather else ins[a].at[my], outs[a].at[my], local_sems.at[a]))
            for k in range(1, N_DEV):
                peer = (x ^ ((k >> 2) & 1), y ^ ((k >> 1) & 1), c ^ (k & 1))
                pid = 4 * peer[0] + 2 * peer[1] + peer[2]
                src = ins[a] if gather else ins[a].at[pid]
                sem = dict(send_sem=send_sems.at[a, k - 1], recv_sem=recv_sems.at[a, k - 1], device_id=peer, device_id_type=MESH)
                sends.append(pltpu.make_async_remote_copy(src_ref=src, dst_ref=outs[a].at[my], **sem))
                recvs.append(pltpu.make_async_remote_copy(src_ref=src, dst_ref=outs[a].at[pid], **sem))
        return local, sends, recvs

    def start(self, ins, outs, sems):
        local, sends, _ = self._plan(ins, outs, sems)
        for cp in local + sends:
            cp.start()

    def wait(self, ins, outs, sems):
        local, sends, recvs = self._plan(ins, outs, sems)
        for cp in recvs:
            cp.wait_recv()
        for cp in sends:
            cp.wait_send()
        for cp in local:
            cp.wait()


def _grid_edges(grid):
    first = last = None
    for ax, size in enumerate(grid):
        pid = pl.program_id(ax)
        f, l = pid == 0, pid == size - 1
        first = f if first is None else first & f
        last = l if last is None else last & l
    return first, last


def _mm(a, b, *, grid, a_spec, b_spec, o_spec, o_shape, o_dtype, contract, name, add=None, add_spec=None, acc_shape=None,
        comm=None):
    nk = grid[2]
    has_add = add is not None
    n_in = 3 if has_add else 2
    nc = comm.n if comm is not None else 0

    def body(*refs):
        a_ref, b_ref = refs[0], refs[1]
        add_ref = refs[2] if has_add else None
        comm_in = refs[n_in:n_in + nc]
        o_ref = refs[n_in + nc]
        comm_out = refs[n_in + nc + 1:n_in + 2 * nc + 1]
        scratch = refs[n_in + 2 * nc + 1:]
        if nc:
            first, last = _grid_edges(grid)

            @pl.when(first)
            def _():
                comm.start(comm_in, comm_out, scratch[-3:])

        part = lax.dot_general(a_ref[...], b_ref[...], (contract, ((), ())), preferred_element_type=F32)
        if nk == 1:
            if has_add:
                part = part + add_ref[...]
            o_ref[...] = part.astype(o_dtype)
        else:
            acc = scratch[0]
            k = pl.program_id(2)

            @pl.when(k == 0)
            def _():
                acc[...] = part

            @pl.when(k > 0)
            def _():
                acc[...] += part

            @pl.when(k == nk - 1)
            def _():
                r = acc[...]
                if has_add:
                    r = r + add_ref[...]
                o_ref[...] = r.astype(o_dtype)

        if nc:
            @pl.when(last)
            def _():
                comm.wait(comm_in, comm_out, scratch[-3:])

    any_spec = pl.BlockSpec(memory_space=pl.ANY)
    in_specs = [a_spec, b_spec]
    args = [a, b]
    if has_add:
        in_specs.append(add_spec)
        args.append(add)
    scratch = [pltpu.VMEM(acc_shape, F32)] if nk > 1 else []
    out_specs, out_shape = o_spec, jax.ShapeDtypeStruct(o_shape, o_dtype)
    sem = ("parallel", "parallel", "arbitrary")
    if nc:
        in_specs += [any_spec] * nc
        args += comm.arrays
        out_specs = [o_spec] + [any_spec] * nc
        out_shape = [out_shape] + comm.out_shapes()
        scratch += comm.scratch()
        sem = ("arbitrary", "arbitrary", "arbitrary")
    res = pl.pallas_call(
        body, name=name, grid=grid, in_specs=in_specs, out_specs=out_specs, out_shape=out_shape, scratch_shapes=scratch,
        compiler_params=_params(sem, VMEM_BIG),
    )(*args)
    return (res[0], list(res[1:])) if nc else res


def _mm_nn(a, b, *, bm, bn, bk, o_dtype, name, add=None, comm=None):
    m, kd = a.shape
    n = b.shape[1]
    bm, bn, bk = _tile(m, bm, 8), _tile(n, bn), _tile(kd, bk)
    o_spec = pl.BlockSpec((bm, bn), lambda i, j, k: (i, j))
    return _mm(a, b, grid=(m // bm, n // bn, kd // bk),
               a_spec=pl.BlockSpec((bm, bk), lambda i, j, k: (i, k)),
               b_spec=pl.BlockSpec((bk, bn), lambda i, j, k: (k, j)),
               o_spec=o_spec, o_shape=(m, n), o_dtype=o_dtype, contract=((1,), (0,)), name=name,
               add=add, add_spec=o_spec, acc_shape=(bm, bn), comm=comm)


def _mm_nt(a, b, *, bm, bn, bk, o_dtype, name, add=None, b_col0=0, comm=None):
    m, kd = a.shape
    n = b.shape[0]
    bm, bn, bk = _tile(m, bm, 8), _tile(n, bn), _tile(kd, bk)
    kb0 = b_col0 // bk
    assert kb0 * bk == b_col0
    o_spec = pl.BlockSpec((bm, bn), lambda i, j, k: (i, j))
    return _mm(a, b, grid=(m // bm, n // bn, kd // bk),
               a_spec=pl.BlockSpec((bm, bk), lambda i, j, k: (i, k)),
               b_spec=pl.BlockSpec((bn, bk), lambda i, j, k: (j, kb0 + k)),
               o_spec=o_spec, o_shape=(m, n), o_dtype=o_dtype, contract=((1,), (1,)), name=name,
               add=add, add_spec=o_spec, acc_shape=(bm, bn), comm=comm)


def _mm_tn(a, b, *, bm, bn, bk, o_dtype, name):
    kd, m = a.shape
    n = b.shape[1]
    bm, bn, bk = _tile(m, bm), _tile(n, bn), _tile(kd, bk, 8)
    return _mm(a, b, grid=(m // bm, n // bn, kd // bk),
               a_spec=pl.BlockSpec((bk, bm), lambda i, j, k: (k, i)),
               b_spec=pl.BlockSpec((bk, bn), lambda i, j, k: (k, j)),
               o_spec=pl.BlockSpec((bm, bn), lambda i, j, k: (i, j)),
               o_shape=(m, n), o_dtype=o_dtype, contract=((0,), (0,)), name=name, acc_shape=(bm, bn))


def _mm_branch_fwd(s, w2d, name):
    m, kb = s.shape
    ds = w2d.shape[1]
    bm = _tile(m, 1024, 8)
    return _mm(s, w2d, grid=(m // bm, N_DEV, 1),
               a_spec=pl.BlockSpec((bm, kb), lambda i, j, k: (i, 0)),
               b_spec=pl.BlockSpec((kb, ds), lambda i, j, k: (j, 0)),
               o_spec=pl.BlockSpec((bm, ds), lambda i, j, k: (i, j)),
               o_shape=(m, N_DEV * ds), o_dtype=F32, contract=((1,), (0,)), name=name)


def _mm_branch_bwd_act(du, w2d, kb, name):
    m = du.shape[0]
    ds = w2d.shape[1]
    bm = _tile(m, 1024, 8)
    return _mm(du, w2d, grid=(m // bm, 1, N_DEV),
               a_spec=pl.BlockSpec((bm, ds), lambda i, j, k: (i, k)),
               b_spec=pl.BlockSpec((kb, ds), lambda i, j, k: (k, 0)),
               o_spec=pl.BlockSpec((bm, kb), lambda i, j, k: (i, 0)),
               o_shape=(m, kb), o_dtype=F32, contract=((1,), (1,)), name=name, acc_shape=(bm, kb))


def _mm_branch_bwd_w(s, du, name):
    m, kb = s.shape
    ds = du.shape[1] // N_DEV
    bk = _tile(m, 2048, 8)
    return _mm(s, du, grid=(1, N_DEV, m // bk),
               a_spec=pl.BlockSpec((bk, kb), lambda i, j, k: (k, 0)),
               b_spec=pl.BlockSpec((bk, ds), lambda i, j, k: (k, j)),
               o_spec=pl.BlockSpec((kb, ds), lambda i, j, k: (j, 0)),
               o_shape=(N_DEV * kb, ds), o_dtype=BF16, contract=((0,), (0,)), name=name, acc_shape=(kb, ds))


def _headnorm_fwd(src, c0, width, bw, hd, gain, nflag, head_major, name):
    rows = src.shape[0]
    bm = _tile(rows, 1024, 8)
    bd = _block_diag(hd)
    cb0 = c0 // bw

    def body(x_ref, g_ref, f_ref, bd_ref, o_ref):
        xv = x_ref[...]
        ss = _seg_sum(xv * xv, bd_ref[...])
        rstd = lax.rsqrt(ss * (1.0 / hd) + EPS)
        y = (xv * jnp.where(f_ref[...] > 0.0, rstd, 1.0) * g_ref[...]).astype(BF16)
        if head_major:
            for h in range(bw // HEAD_DIM):
                o_ref[h] = y[:, h * HEAD_DIM:(h + 1) * HEAD_DIM]
        else:
            o_ref[...] = y

    vec_spec = pl.BlockSpec((1, bw), lambda i, t: (0, t))
    if head_major:
        hpb = bw // HEAD_DIM
        out_spec = pl.BlockSpec((hpb, bm, HEAD_DIM), lambda i, t: (t, i, 0))
        out_shape = jax.ShapeDtypeStruct((width // HEAD_DIM, rows, HEAD_DIM), BF16)
    else:
        out_spec = pl.BlockSpec((bm, bw), lambda i, t: (i, t))
        out_shape = jax.ShapeDtypeStruct((rows, width), BF16)
    return pl.pallas_call(
        body, name=name, grid=(rows // bm, width // bw),
        in_specs=[pl.BlockSpec((bm, bw), lambda i, t: (i, cb0 + t)), vec_spec, vec_spec,
                  pl.BlockSpec((LANES, LANES), lambda i, t: (0, 0))],
        out_specs=out_spec, out_shape=out_shape,
        compiler_params=_params(("parallel", "parallel")),
    )(src, gain, nflag, bd)


def _headnorm_bwd(src, c0, width, bw, hd, gain, nflag, dyn, target, t0, name):
    rows = src.shape[0]
    bm = _tile(rows, 1024, 8)
    bd = _block_diag(hd)
    cb0 = c0 // bw
    tb0 = t0 // bw
    aliased = target is not None

    def body(*refs):
        if aliased:
            x_ref, dy_ref, g_ref, f_ref, bd_ref, _, o_ref, dg_ref = refs
        else:
            x_ref, dy_ref, g_ref, f_ref, bd_ref, o_ref, dg_ref = refs
        i = pl.program_id(1)
        xv = x_ref[...]
        dyv = dy_ref[...]
        bdv = bd_ref[...]
        rstd = lax.rsqrt(_seg_sum(xv * xv, bdv) * (1.0 / hd) + EPS)
        xhat = xv * rstd
        g = dyv * g_ref[...]
        mean = _seg_sum(g * xhat, bdv) * (1.0 / hd)
        dx = jnp.where(f_ref[...] > 0.0, rstd * (g - xhat * mean), g)
        o_ref[...] = dx.astype(BF16)
        part = jnp.sum((dyv * xhat).reshape(bm // 8, 8, bw), axis=0)

        @pl.when(i == 0)
        def _():
            dg_ref[...] = part

        @pl.when(i > 0)
        def _():
            dg_ref[...] += part

    vec_spec = pl.BlockSpec((1, bw), lambda t, i: (0, t))
    in_specs = [pl.BlockSpec((bm, bw), lambda t, i: (i, cb0 + t)), pl.BlockSpec((bm, bw), lambda t, i: (i, t)),
                vec_spec, vec_spec, pl.BlockSpec((LANES, LANES), lambda t, i: (0, 0))]
    args = [src, dyn, gain, nflag, bd]
    aliases = {}
    if aliased:
        in_specs.append(pl.BlockSpec(memory_space=pl.ANY))
        args.append(target)
        aliases = {5: 0}
        o_shape = jax.ShapeDtypeStruct(target.shape, BF16)
    else:
        o_shape = jax.ShapeDtypeStruct((rows, width), BF16)
    out, dg = pl.pallas_call(
        body, name=name, grid=(width // bw, rows // bm), in_specs=in_specs,
        out_specs=[pl.BlockSpec((bm, bw), lambda t, i: (i, tb0 + t)), pl.BlockSpec((8, bw), lambda t, i: (0, t))],
        out_shape=[o_shape, jax.ShapeDtypeStruct((8, width), F32)],
        input_output_aliases=aliases,
        compiler_params=_params(("parallel", "arbitrary")),
    )(*args)
    return out, dg


def _fox_prep(pfb, bpad, name):
    s = pfb.shape[0]

    def body(p_ref, b_ref, c_ref):
        z = p_ref[...] + b_ref[...]
        logf = jnp.minimum(z, 0.0) - jnp.log(1.0 + jnp.exp(-jnp.abs(z)))
        x = logf.T[0:16, :]
        lane = lax.broadcasted_iota(jnp.int32, (16, s), 1)
        sh = 1
        while sh < s:
            x = x + jnp.where(lane >= sh, pltpu.roll(x, sh, 1), 0.0)
            sh *= 2
        c_ref[...] = x

    return pl.pallas_call(
        body, name=name, grid=(1,),
        in_specs=[pl.BlockSpec((s, FB_PAD), lambda i: (0, 0)), pl.BlockSpec((1, FB_PAD), lambda i: (0, 0))],
        out_specs=pl.BlockSpec((16, s), lambda i: (0, 0)),
        out_shape=jax.ShapeDtypeStruct((16, s), F32),
        compiler_params=_params(("arbitrary",)),
    )(pfb, bpad)


def _fox_prep_bwd(pfb, bpad, dct, name):
    s = pfb.shape[0]

    def body(p_ref, b_ref, dc_ref, df_ref, db_ref):
        zt = (p_ref[...] + b_ref[...]).T[0:16, :]
        y = dc_ref[...]
        lane = lax.broadcasted_iota(jnp.int32, (16, s), 1)
        sh = 1
        while sh < s:
            y = y + jnp.where(lane < s - sh, pltpu.roll(y, s - sh, 1), 0.0)
            sh *= 2
        dz = y * _sigmoid(-zt)
        db_ref[...] = jnp.broadcast_to(jnp.sum(dz, axis=1, keepdims=True), (16, FB_PAD))
        full = jnp.concatenate([dz, jnp.zeros((FB_PAD - 16, s), F32)], axis=0)
        df_ref[...] = full.T.astype(BF16)

    return pl.pallas_call(
        body, name=name, grid=(1,),
        in_specs=[pl.BlockSpec((s, FB_PAD), lambda i: (0, 0)), pl.BlockSpec((1, FB_PAD), lambda i: (0, 0)),
                  pl.BlockSpec((16, s), lambda i: (0, 0))],
        out_specs=[pl.BlockSpec((s, FB_PAD), lambda i: (0, 0)), pl.BlockSpec((16, FB_PAD), lambda i: (0, 0))],
        out_shape=[jax.ShapeDtypeStruct((s, FB_PAD), BF16), jax.ShapeDtypeStruct((16, FB_PAD), F32)],
        compiler_params=_params(("arbitrary",)),
    )(pfb, bpad, dct)


def _swa_window(n):
    ws = pl.multiple_of(jnp.maximum(n * WINDOW - WINDOW, 0), WINDOW)
    qi = lax.broadcasted_iota(jnp.int32, (WINDOW, 2 * WINDOW), 0)
    kj = lax.broadcasted_iota(jnp.int32, (WINDOW, 2 * WINDOW), 1)
    rel = qi + (n * WINDOW - ws) - kj
    valid = (rel >= 0) & (rel < WINDOW)
    return ws, valid, rel.astype(F32)


def _attn_a_fwd(q, k, v, sinks, slopes, name):
    s = q.shape[1]
    nb = s // WINDOW
    smem = pl.BlockSpec(memory_space=pltpu.SMEM)

    def body(sink_ref, slope_ref, q_ref, k_ref, v_ref, o_ref, lse_ref):
        n = pl.program_id(0)
        ws, valid, relf = _swa_window(n)
        outs = []
        for h in range(A_Q_HEADS):
            kvh = h // A_GROUP
            kw = k_ref[kvh, pl.ds(ws, 2 * WINDOW), :]
            vw = v_ref[kvh, pl.ds(ws, 2 * WINDOW), :]
            sc = lax.dot_general(q_ref[h], kw, (((1,), (1,)), ((), ())), preferred_element_type=F32)
            sc = jnp.where(valid, sc - slope_ref[h] * relf, NEG)
            sink = sink_ref[h]
            m = jnp.maximum(jnp.max(sc, axis=1, keepdims=True), sink)
            p = jnp.exp(sc - m)
            denom = jnp.sum(p, axis=1, keepdims=True) + jnp.exp(sink - m)
            pn = (p / denom).astype(BF16)
            outs.append(jnp.dot(pn, vw, preferred_element_type=F32))
            lse_ref[h] = jnp.broadcast_to(m + jnp.log(denom), (WINDOW, HEAD_DIM))
        o_ref[...] = jnp.concatenate(outs, axis=1)

    return pl.pallas_call(
        body, name=name, grid=(nb,),
        in_specs=[smem, smem,
                  pl.BlockSpec((A_Q_HEADS, WINDOW, HEAD_DIM), lambda n: (0, n, 0)),
                  pl.BlockSpec((A_KV_HEADS, s, HEAD_DIM), lambda n: (0, 0, 0)),
                  pl.BlockSpec((A_KV_HEADS, s, HEAD_DIM), lambda n: (0, 0, 0))],
        out_specs=[pl.BlockSpec((WINDOW, A_WIDTH), lambda n: (n, 0)),
                   pl.BlockSpec((A_Q_HEADS, WINDOW, HEAD_DIM), lambda n: (0, n, 0))],
        out_shape=[jax.ShapeDtypeStruct((s, A_WIDTH), F32), jax.ShapeDtypeStruct((A_Q_HEADS, s, HEAD_DIM), F32)],
        compiler_params=_params(("parallel",), VMEM_BIG),
    )(sinks, slopes, q, k, v)


def _attn_a_bwd(q, k, v, do, lse, dd, sinks, slopes, name):
    s = q.shape[1]
    nb = s // WINDOW
    smem = pl.BlockSpec(memory_space=pltpu.SMEM)
    last = nb - 1

    def body(sink_ref, slope_ref, q_ref, k_ref, v_ref, do_ref, lse_ref, dd_ref, dq_ref, dkv_ref, ds_ref, carry):
        n = pl.program_id(0)

        @pl.when(n == 0)
        def _():
            carry[...] = jnp.zeros(carry.shape, F32)
            ds_ref[...] = jnp.zeros(ds_ref.shape, F32)

        @pl.when(n < nb)
        def _():
            ws, valid, relf = _swa_window(n)
            dqs = []
            dkw = [None] * A_KV_HEADS
            dvw = [None] * A_KV_HEADS
            for h in range(A_Q_HEADS):
                kvh = h // A_GROUP
                qh = q_ref[h]
                doh = do_ref[h]
                kw = k_ref[kvh, pl.ds(ws, 2 * WINDOW), :]
                vw = v_ref[kvh, pl.ds(ws, 2 * WINDOW), :]
                lse_h = lse_ref[h]
                dd_h = dd_ref[h]
                sc = lax.dot_general(qh, kw, (((1,), (1,)), ((), ())), preferred_element_type=F32)
                sc = jnp.where(valid, sc - slope_ref[h] * relf, NEG)
                p = jnp.exp(sc - lse_h[:, 0:1])
                dp = lax.dot_general(doh, vw, (((1,), (1,)), ((), ())), preferred_element_type=F32)
                dsc = (p * (dp - dd_h[:, 0:1])).astype(BF16)
                pb = p.astype(BF16)
                dqs.append(jnp.dot(dsc, kw, preferred_element_type=F32))
                dk_h = lax.dot_general(dsc, qh, (((0,), (0,)), ((), ())), preferred_element_type=F32)
                dv_h = lax.dot_general(pb, doh, (((0,), (0,)), ((), ())), preferred_element_type=F32)
                dkw[kvh] = dk_h if dkw[kvh] is None else dkw[kvh] + dk_h
                dvw[kvh] = dv_h if dvw[kvh] is None else dvw[kvh] + dv_h
                psink = jnp.exp(sink_ref[h] - lse_h)
                ds_ref[h] += jnp.sum((-psink * dd_h).reshape(WINDOW // 8, 8, HEAD_DIM), axis=0)
            dq_ref[...] = jnp.concatenate(dqs, axis=1)
            win = jnp.concatenate(dkw + dvw, axis=1)
            first = win[0:WINDOW]
            second = win[WINDOW:2 * WINDOW]
            dkv_ref[...] = carry[...] + first
            carry[...] = jnp.where(n == 0, first, second)

        @pl.when(n == nb)
        def _():
            dkv_ref[...] = carry[...]

    hm = lambda heads: pl.BlockSpec((heads, WINDOW, HEAD_DIM), lambda n: (0, jnp.minimum(n, last), 0))
    res = lambda heads: pl.BlockSpec((heads, s, HEAD_DIM), lambda n: (0, 0, 0))
    return pl.pallas_call(
        body, name=name, grid=(nb + 1,),
        in_specs=[smem, smem, hm(A_Q_HEADS), res(A_KV_HEADS), res(A_KV_HEADS), hm(A_Q_HEADS), hm(A_Q_HEADS), hm(A_Q_HEADS)],
        out_specs=[pl.BlockSpec((WINDOW, A_WIDTH), lambda n: (jnp.minimum(n, last), 0)),
                   pl.BlockSpec((WINDOW, 2 * A_KV_WIDTH), lambda n: (jnp.maximum(n - 1, 0), 0)),
                   pl.BlockSpec((A_Q_HEADS, 8, HEAD_DIM), lambda n: (0, 0, 0))],
        out_shape=[jax.ShapeDtypeStruct((s, A_WIDTH), F32), jax.ShapeDtypeStruct((s, 2 * A_KV_WIDTH), F32),
                   jax.ShapeDtypeStruct((A_Q_HEADS, 8, HEAD_DIM), F32)],
        scratch_shapes=[pltpu.VMEM((WINDOW, 2 * A_KV_WIDTH), F32)],
        compiler_params=_params(("arbitrary",), VMEM_BIG),
    )(sinks, slopes, q, k, v, do, lse, dd)


def _attn_b_fwd(q, k, v, c3, name):
    heads, s, _ = q.shape
    bq = min(512, s)
    nq = s // bq
    nt = (((1,), (1,)), ((), ()))

    def body(q_ref, k_ref, v_ref, c_ref, o_ref, lse_ref, m_scr, l_scr, acc_scr):
        i = pl.program_id(1)
        r0 = pl.multiple_of(i * bq, bq)
        row = lax.broadcasted_iota(jnp.int32, (bq, bq), 0)
        col = lax.broadcasted_iota(jnp.int32, (bq, bq), 1)
        outs = []
        for h2 in range(2):
            qv = q_ref[h2]
            cq0 = c_ref[h2, :, pl.ds(r0, LANES)][:, 0:1]
            m_scr[...] = jnp.full((bq, LANES), NEG, F32)
            l_scr[...] = jnp.zeros((bq, LANES), F32)
            acc_scr[...] = jnp.zeros((bq, HEAD_DIM), F32)

            def step(j, masked):
                k0 = pl.multiple_of(j * bq, bq)
                kv = k_ref[h2, pl.ds(k0, bq), :]
                vv = v_ref[h2, pl.ds(k0, bq), :]
                sc = lax.dot_general(qv, kv, nt, preferred_element_type=F32)
                sc = sc + (cq0 - c_ref[h2, :, pl.ds(k0, bq)])
                if masked:
                    sc = jnp.where(col <= row, sc, NEG)
                m_prev = m_scr[...]
                m_new = jnp.maximum(m_prev, jnp.max(sc, axis=1, keepdims=True))
                alpha = jnp.exp(m_prev - m_new)
                p = jnp.exp(sc - m_new[:, 0:1])
                l_scr[...] = alpha * l_scr[...] + jnp.sum(p, axis=1, keepdims=True)
                p_hi = p.astype(BF16)
                p_lo = (p - p_hi.astype(F32)).astype(BF16)
                pv = jnp.dot(p_hi, vv, preferred_element_type=F32) + jnp.dot(p_lo, vv, preferred_element_type=F32)
                acc_scr[...] = acc_scr[...] * alpha[:, 0:HEAD_DIM] + pv
                m_scr[...] = m_new

            def loop_body(j, carry):
                step(j, False)
                return carry

            lax.fori_loop(0, i, loop_body, 0)
            step(i, True)
            l = l_scr[...]
            outs.append(acc_scr[...] / l[:, 0:HEAD_DIM])
            lse_ref[h2] = (m_scr[...] + jnp.log(l))[:, 0:HEAD_DIM]
        o_ref[...] = jnp.concatenate(outs, axis=1)

    res = pl.BlockSpec((2, s, HEAD_DIM), lambda hp, i: (hp, 0, 0))
    return pl.pallas_call(
        body, name=name, grid=(heads // 2, nq),
        in_specs=[pl.BlockSpec((2, bq, HEAD_DIM), lambda hp, i: (hp, i, 0)), res, res,
                  pl.BlockSpec((2, 1, s), lambda hp, i: (hp, 0, 0))],
        out_specs=[pl.BlockSpec((bq, 2 * HEAD_DIM), lambda hp, i: (i, hp)),
                   pl.BlockSpec((2, bq, HEAD_DIM), lambda hp, i: (hp, i, 0))],
        out_shape=[jax.ShapeDtypeStruct((s, heads * HEAD_DIM), F32), jax.ShapeDtypeStruct((heads, s, HEAD_DIM), F32)],
        scratch_shapes=[pltpu.VMEM((bq, LANES), F32), pltpu.VMEM((bq, LANES), F32), pltpu.VMEM((bq, HEAD_DIM), F32)],
        compiler_params=_params(("parallel", "parallel"), VMEM_BIG),
    )(q, k, v, c3)


def _attn_b_bwd(q, k, v, do, lse, dd, c3, name, comm=None):
    heads, s, _ = q.shape
    bq = min(512, s)
    nq = s // bq
    nt = (((1,), (1,)), ((), ()))
    tn = (((0,), (0,)), ((), ()))
    nc = comm.n if comm is not None else 0
    grid = (heads // 2, nq)

    def body(*refs):
        q_ref, k_ref, v_ref, do_ref, lse_ref, dd_ref, c_ref = refs[0:7]
        comm_in = refs[7:7 + nc]
        dq_ref, dk_ref, dv_ref, dc_ref = refs[7 + nc:11 + nc]
        comm_out = refs[11 + nc:11 + 2 * nc]
        dq_scr, dk_scr, dv_scr, dc_scr = refs[11 + 2 * nc:15 + 2 * nc]
        sems = refs[15 + 2 * nc:]
        j = pl.program_id(1)
        k0 = pl.multiple_of(j * bq, bq)
        row = lax.broadcasted_iota(jnp.int32, (bq, bq), 0)
        col = lax.broadcasted_iota(jnp.int32, (bq, bq), 1)
        if nc:
            first, last = _grid_edges(grid)

            @pl.when(first)
            def _():
                comm.start(comm_in, comm_out, sems)

        @pl.when(j == 0)
        def _():
            dq_scr[...] = jnp.zeros(dq_scr.shape, F32)

        dks, dvs = [], []
        for h2 in range(2):
            kv = k_ref[h2]
            vv = v_ref[h2]
            c_k = c_ref[h2, :, pl.ds(k0, bq)]
            dk_scr[...] = jnp.zeros((bq, HEAD_DIM), F32)
            dv_scr[...] = jnp.zeros((bq, HEAD_DIM), F32)
            dc_scr[...] = jnp.zeros((1, bq), F32)

            def step(i, masked):
                r0 = pl.multiple_of(i * bq, bq)
                qv = q_ref[h2, pl.ds(r0, bq), :]
                dov = do_ref[h2, pl.ds(r0, bq), :]
                lse_v = lse_ref[h2, pl.ds(r0, bq), :][:, 0:1]
                dd_v = dd_ref[h2, pl.ds(r0, bq), :][:, 0:1]
                cq0 = c_ref[h2, :, pl.ds(r0, LANES)][:, 0:1]
                sc = lax.dot_general(qv, kv, nt, preferred_element_type=F32) + (cq0 - c_k)
                if masked:
                    sc = jnp.where(col <= row, sc, NEG)
                p = jnp.exp(sc - lse_v)
                dp = lax.dot_general(dov, vv, nt, preferred_element_type=F32)
                dsc = p * (dp - dd_v)
                dsb = dsc.astype(BF16)
                dv_scr[...] += lax.dot_general(p.astype(BF16), dov, tn, preferred_element_type=F32)
                dk_scr[...] += lax.dot_general(dsb, qv, tn, preferred_element_type=F32)
                dq_scr[h2, pl.ds(r0, bq), :] += jnp.dot(dsb, kv, preferred_element_type=F32)
                dc_scr[...] -= jnp.sum(dsc, axis=0, keepdims=True)

            def loop_body(i, carry):
                step(i, False)
                return carry

            step(j, True)
            lax.fori_loop(j + 1, nq, loop_body, 0)
            dks.append(dk_scr[...])
            dvs.append(dv_scr[...])
            dc_ref[h2] = dc_scr[...]
        dk_ref[...] = jnp.concatenate(dks, axis=1)
        dv_ref[...] = jnp.concatenate(dvs, axis=1)

        @pl.when(j == nq - 1)
        def _():
            dq_ref[...] = jnp.concatenate([dq_scr[0], dq_scr[1]], axis=1)

        if nc:
            @pl.when(last)
            def _():
                comm.wait(comm_in, comm_out, sems)

    res = pl.BlockSpec((2, s, HEAD_DIM), lambda hp, j: (hp, 0, 0))
    blk = pl.BlockSpec((2, bq, HEAD_DIM), lambda hp, j: (hp, j, 0))
    tm = jax.ShapeDtypeStruct((s, heads * HEAD_DIM), F32)
    any_spec = pl.BlockSpec(memory_space=pl.ANY)
    in_specs = [res, blk, blk, res, res, res, pl.BlockSpec((2, 1, s), lambda hp, j: (hp, 0, 0))]
    out_specs = [pl.BlockSpec((s, 2 * HEAD_DIM), lambda hp, j: (0, hp)),
                 pl.BlockSpec((bq, 2 * HEAD_DIM), lambda hp, j: (j, hp)),
                 pl.BlockSpec((bq, 2 * HEAD_DIM), lambda hp, j: (j, hp)),
                 pl.BlockSpec((2, 1, bq), lambda hp, j: (hp, 0, j))]
    out_shape = [tm, tm, tm, jax.ShapeDtypeStruct((heads, 1, s), F32)]
    scratch = [pltpu.VMEM((2, s, HEAD_DIM), F32), pltpu.VMEM((bq, HEAD_DIM), F32),
               pltpu.VMEM((bq, HEAD_DIM), F32), pltpu.VMEM((1, bq), F32)]
    args = [q, k, v, do, lse, dd, c3]
    sem = ("parallel", "arbitrary")
    if nc:
        in_specs += [any_spec] * nc
        args += comm.arrays
        out_specs += [any_spec] * nc
        out_shape += comm.out_shapes()
        scratch += comm.scratch()
        sem = ("arbitrary", "arbitrary")
    res_all = pl.pallas_call(
        body, name=name, grid=grid, in_specs=in_specs, out_specs=out_specs, out_shape=out_shape,
        scratch_shapes=scratch, compiler_params=_params(sem, VMEM_BIG),
    )(*args)
    return res_all[0], res_all[1], res_all[2], res_all[3], list(res_all[4:])


def _attn_c_probs(qh, mkh):
    sc = lax.dot_general(qh, mkh, (((1,), (1,)), ((), ())), preferred_element_type=F32) * (C_HEAD_DIM ** -0.5)
    p = jnp.exp(sc - jnp.max(sc, axis=1, keepdims=True))
    return p / jnp.sum(p, axis=1, keepdims=True)


def _attn_c_fwd(q, mkv, name):
    s = q.shape[0]
    m = mkv.shape[0]
    bq = _tile(s, 512, 8)

    def body(q_ref, mk_ref, mv_ref, o_ref):
        outs = []
        for h in range(C_HEADS):
            sl = slice(h * C_HEAD_DIM, (h + 1) * C_HEAD_DIM)
            pn = _attn_c_probs(q_ref[:, sl], mk_ref[:, sl]).astype(BF16)
            outs.append(jnp.dot(pn, mv_ref[:, sl], preferred_element_type=F32))
        o_ref[...] = jnp.concatenate(outs, axis=1)

    return pl.pallas_call(
        body, name=name, grid=(s // bq,),
        in_specs=[pl.BlockSpec((bq, C_WIDTH), lambda i: (i, 0)), pl.BlockSpec((m, C_WIDTH), lambda i: (0, 0)),
                  pl.BlockSpec((m, C_WIDTH), lambda i: (0, 1))],
        out_specs=pl.BlockSpec((bq, C_WIDTH), lambda i: (i, 0)),
        out_shape=jax.ShapeDtypeStruct((s, C_WIDTH), F32),
        compiler_params=_params(("parallel",)),
    )(q, mkv, mkv)


def _attn_c_bwd(q, mkv, do, name):
    s = q.shape[0]
    m = mkv.shape[0]
    bq = _tile(s, 512, 8)
    tn = (((0,), (0,)), ((), ()))

    def body(q_ref, mk_ref, mv_ref, do_ref, dq_ref, dm_ref):
        i = pl.program_id(0)

        @pl.when(i == 0)
        def _():
            dm_ref[...] = jnp.zeros(dm_ref.shape, F32)

        dqs = []
        for h in range(C_HEADS):
            sl = slice(h * C_HEAD_DIM, (h + 1) * C_HEAD_DIM)
            qh, mkh, mvh, doh = q_ref[:, sl], mk_ref[:, sl], mv_ref[:, sl], do_ref[:, sl]
            pn = _attn_c_probs(qh, mkh)
            dp = lax.dot_general(doh, mvh, (((1,), (1,)), ((), ())), preferred_element_type=F32)
            dsc = (pn * (dp - jnp.sum(pn * dp, axis=1, keepdims=True)) * (C_HEAD_DIM ** -0.5)).astype(BF16)
            dqs.append(jnp.dot(dsc, mkh, preferred_element_type=F32))
            dm_ref[:, sl] += lax.dot_general(dsc, qh, tn, preferred_element_type=F32)
            sv = slice(C_WIDTH + h * C_HEAD_DIM, C_WIDTH + (h + 1) * C_HEAD_DIM)
            dm_ref[:, sv] += lax.dot_general(pn.astype(BF16), doh, tn, preferred_element_type=F32)
        dq_ref[...] = jnp.concatenate(dqs, axis=1)

    row = pl.BlockSpec((bq, C_WIDTH), lambda i: (i, 0))
    return pl.pallas_call(
        body, name=name, grid=(s // bq,),
        in_specs=[row, pl.BlockSpec((m, C_WIDTH), lambda i: (0, 0)), pl.BlockSpec((m, C_WIDTH), lambda i: (0, 1)), row],
        out_specs=[row, pl.BlockSpec((m, 2 * C_WIDTH), lambda i: (0, 0))],
        out_shape=[jax.ShapeDtypeStruct((s, C_WIDTH), F32), jax.ShapeDtypeStruct((m, 2 * C_WIDTH), F32)],
        compiler_params=_params(("arbitrary",)),
    )(q, mkv, mkv, do)


def _gate_fwd(y, proj, zc0, bw, name):
    rows, width = y.shape
    bm = _tile(rows, 1024, 8)
    cb0 = zc0 // bw

    def body(y_ref, z_ref, o_ref):
        z = z_ref[...]
        o_ref[...] = (y_ref[...] * (z * _sigmoid(z))).astype(BF16)

    return pl.pallas_call(
        body, name=name, grid=(rows // bm, width // bw),
        in_specs=[pl.BlockSpec((bm, bw), lambda i, t: (i, t)), pl.BlockSpec((bm, bw), lambda i, t: (i, cb0 + t))],
        out_specs=pl.BlockSpec((bm, bw), lambda i, t: (i, t)),
        out_shape=jax.ShapeDtypeStruct((rows, width), BF16),
        compiler_params=_params(("parallel", "parallel")),
    )(y, proj)


def _gate_bwd(dsv, y, proj, zc0, bw, dproj, t0, head_major, name):
    rows, width = y.shape
    bm = _tile(rows, 1024, 8)
    cb0 = zc0 // bw
    tb0 = t0 // bw
    bd = _block_diag(HEAD_DIM)
    hpb = bw // HEAD_DIM

    def body(*refs):
        if head_major:
            ds_ref, y_ref, z_ref, bd_ref, _, dp_ref, dy_ref, dd_ref = refs
        else:
            ds_ref, y_ref, z_ref, _, dp_ref, dy_ref = refs
        z = z_ref[...]
        sig = _sigmoid(z)
        dsx = ds_ref[...]
        yv = y_ref[...]
        dy = dsx * (z * sig)
        dp_ref[...] = (dsx * yv * (sig * (1.0 + z * (1.0 - sig)))).astype(BF16)
        if head_major:
            dyb = dy.astype(BF16)
            dd = _seg_sum(dyb.astype(F32) * yv, bd_ref[...])
            for h in range(hpb):
                sl = slice(h * HEAD_DIM, (h + 1) * HEAD_DIM)
                dy_ref[h] = dyb[:, sl]
                dd_ref[h] = dd[:, sl]
        else:
            dy_ref[...] = dy.astype(BF16)

    tile = pl.BlockSpec((bm, bw), lambda i, t: (i, t))
    ztile = pl.BlockSpec((bm, bw), lambda i, t: (i, cb0 + t))
    ttile = pl.BlockSpec((bm, bw), lambda i, t: (i, tb0 + t))
    any_spec = pl.BlockSpec(memory_space=pl.ANY)
    dp_shape = jax.ShapeDtypeStruct(dproj.shape, BF16)
    if head_major:
        hm_spec = pl.BlockSpec((hpb, bm, HEAD_DIM), lambda i, t: (t, i, 0))
        nh = width // HEAD_DIM
        outs = pl.pallas_call(
            body, name=name, grid=(rows // bm, width // bw),
            in_specs=[tile, tile, ztile, pl.BlockSpec((LANES, LANES), lambda i, t: (0, 0)), any_spec],
            out_specs=[ttile, hm_spec, hm_spec],
            out_shape=[dp_shape, jax.ShapeDtypeStruct((nh, rows, HEAD_DIM), BF16),
                       jax.ShapeDtypeStruct((nh, rows, HEAD_DIM), F32)],
            input_output_aliases={4: 0},
            compiler_params=_params(("parallel", "parallel")),
        )(dsv, y, proj, bd, dproj)
        return outs[0], outs[1], outs[2]
    outs = pl.pallas_call(
        body, name=name, grid=(rows // bm, width // bw),
        in_specs=[tile, tile, ztile, any_spec],
        out_specs=[ttile, tile],
        out_shape=[dp_shape, jax.ShapeDtypeStruct((rows, width), BF16)],
        input_output_aliases={3: 0},
        compiler_params=_params(("parallel", "parallel")),
    )(dsv, y, proj, dproj)
    return outs[0], outs[1], None


def _merge_fwd(proj, ua, ub, uc, name):
    rows, d = ua.shape
    bm = _tile(rows, 512, 8)
    bw = _tile(d, 512)
    g0 = COL_GATE // bw
    gstep = d // bw

    def body(ga_ref, gb_ref, gc_ref, ua_ref, ub_ref, uc_ref, o_ref):
        y = _sigmoid(ga_ref[...]) * ua_ref[...] + _sigmoid(gb_ref[...]) * ub_ref[...] + _sigmoid(gc_ref[...]) * uc_ref[...]
        o_ref[...] = y.astype(BF16)

    tile = pl.BlockSpec((bm, bw), lambda i, t: (i, t))
    gate = lambda b: pl.BlockSpec((bm, bw), lambda i, t: (i, g0 + b * gstep + t))
    return pl.pallas_call(
        body, name=name, grid=(rows // bm, d // bw),
        in_specs=[gate(0), gate(1), gate(2), tile, tile, tile],
        out_specs=tile, out_shape=jax.ShapeDtypeStruct((rows, d), BF16),
        compiler_params=_params(("parallel", "parallel")),
    )(proj, proj, proj, ua, ub, uc)


def _merge_bwd(dym, u, proj, branch, dproj, name):
    rows, d = u.shape
    bm = _tile(rows, 512, 8)
    bw = _tile(d, 512)
    gb0 = (COL_GATE + branch * d) // bw
    tb0 = (branch * d) // bw

    def body(dy_ref, u_ref, gl_ref, _, dp_ref, du_ref):
        g = _sigmoid(gl_ref[...])
        dyv = dy_ref[...]
        du_ref[...] = (g * dyv).astype(BF16)
        dp_ref[...] = (dyv * u_ref[...] * g * (1.0 - g)).astype(BF16)

    tile = pl.BlockSpec((bm, bw), lambda i, t: (i, t))
    gtile = pl.BlockSpec((bm, bw), lambda i, t: (i, gb0 + t))
    ttile = pl.BlockSpec((bm, bw), lambda i, t: (i, tb0 + t))
    outs = pl.pallas_call(
        body, name=name, grid=(rows // bm, d // bw),
        in_specs=[tile, tile, gtile, pl.BlockSpec(memory_space=pl.ANY)],
        out_specs=[ttile, tile],
        out_shape=[jax.ShapeDtypeStruct(dproj.shape, BF16), jax.ShapeDtypeStruct((rows, d), BF16)],
        input_output_aliases={3: 0},
        compiler_params=_params(("parallel", "parallel")),
    )(dym, u, proj, dproj)
    return outs[0], outs[1]


def _loss_head(y, target, name):
    rows, d = y.shape
    bm = _tile(rows, 256, 8)

    def body(y_ref, t_ref, dy_ref, dyb_ref, l_ref):
        i = pl.program_id(0)
        diff = y_ref[...] - t_ref[...]
        dy = diff * (1.0 / d)
        dy_ref[...] = dy
        dyb_ref[...] = dy.astype(BF16)
        sq = diff * diff
        part = sq[:, 0:LANES]
        for c in range(1, d // LANES):
            part = part + sq[:, c * LANES:(c + 1) * LANES]
        part = jnp.sum(part.reshape(bm // 8, 8, LANES), axis=0)

        @pl.when(i == 0)
        def _():
            l_ref[...] = part

        @pl.when(i > 0)
        def _():
            l_ref[...] += part

    row = pl.BlockSpec((bm, d), lambda i: (i, 0))
    return pl.pallas_call(
        body, name=name, grid=(rows // bm,), in_specs=[row, row],
        out_specs=[row, row, pl.BlockSpec((8, LANES), lambda i: (0, 0))],
        out_shape=[jax.ShapeDtypeStruct((rows, d), F32), jax.ShapeDtypeStruct((rows, d), BF16),
                   jax.ShapeDtypeStruct((8, LANES), F32)],
        compiler_params=_params(("arbitrary",)),
    )(y, target)


def _row(vec, reps=1):
    return jnp.tile(vec.reshape(1, -1).astype(F32), (1, reps))


REST_KEYS = ("wk", "wo", "wa", "wb", "wc")
EARLY_GRADS = ("wm_g", "wm_z", "wo", "wa", "wb", "wc")
LATE_GRADS = ("wm_qkv", "wf", "wk")


def _local_step(x, mem, target, small, wg, rest_shards=None):
    s, d = x.shape
    dist = rest_shards is not None
    wg = dict(wg)
    ones = lambda n: jnp.ones((1, n), F32)
    zeros = lambda n: jnp.zeros((1, n), F32)
    scale_ab = HEAD_DIM ** -0.5
    split8 = lambda g: g.reshape(N_DEV, g.shape[0] // N_DEV, g.shape[1])

    hn = _rmsnorm_fwd(x, small["norm_gain"], "rms_x_fwd")
    if dist:
        proj, gathered = _mm_nn(hn, wg["wm"], bm=1024, bn=1024, bk=d, o_dtype=F32, name="proj_main",
                                comm=_Comm("gather", rest_shards))
        for kname, arr in zip(REST_KEYS, gathered):
            wg[kname] = arr.reshape(arr.shape[0] * arr.shape[1], arr.shape[2])
    else:
        proj = _mm_nn(hn, wg["wm"], bm=1024, bn=1024, bk=d, o_dtype=F32, name="proj_main")
    pfb = _mm_nn(hn, wg["wf"], bm=1024, bn=FB_PAD, bk=d, o_dtype=F32, name="proj_fb")
    mn = _rmsnorm_fwd(mem, small["mem_norm_gain"], "rms_mem_fwd")
    mkv = _mm_nn(mn, wg["wk"], bm=256, bn=1024, bk=d, o_dtype=F32, name="mem_kv")

    gain_a = jnp.concatenate([_row(small["q_gain_a"], A_Q_HEADS) * scale_ab, _row(small["k_gain_a"], A_KV_HEADS), ones(A_KV_WIDTH)], axis=1)
    flag_a = jnp.concatenate([ones(A_WIDTH + A_KV_WIDTH), zeros(A_KV_WIDTH)], axis=1)
    qkv_a = _headnorm_fwd(proj, COL_QA, 1280, 1280, HEAD_DIM, gain_a, flag_a, True, "hn_a_fwd")
    gain_b = jnp.concatenate([_row(small["q_gain_b"], B_HEADS) * scale_ab, _row(small["k_gain_b"], B_HEADS), ones(B_WIDTH)], axis=1)
    flag_b = jnp.concatenate([ones(2 * B_WIDTH), zeros(B_WIDTH)], axis=1)
    qkv_b = _headnorm_fwd(proj, COL_QB, 2304, 256, HEAD_DIM, gain_b, flag_b, True, "hn_b_fwd")
    gain_cq = _row(small["q_gain_c"], C_HEADS)
    q_c = _headnorm_fwd(proj, COL_QC, C_WIDTH, C_WIDTH, C_HEAD_DIM, gain_cq, ones(C_WIDTH), False, "hn_cq_fwd")
    gain_ck = jnp.concatenate([_row(small["k_gain_c"], C_HEADS), ones(C_WIDTH)], axis=1)
    flag_ck = jnp.concatenate([ones(C_WIDTH), zeros(C_WIDTH)], axis=1)
    mkvn = _headnorm_fwd(mkv, 0, 2 * C_WIDTH, 2 * C_WIDTH, C_HEAD_DIM, gain_ck, flag_ck, False, "hn_ck_fwd")

    q_a, k_a, v_a = qkv_a[0:12], qkv_a[12:16], qkv_a[16:20]
    q_b, k_b, v_b = qkv_b[0:12], qkv_b[12:24], qkv_b[24:36]

    bpad = jnp.pad(small["b_forget"].reshape(1, -1), ((0, 0), (0, FB_PAD - B_HEADS)))
    c16 = _fox_prep(pfb, bpad, "fox_prep")
    c3 = c16[0:B_HEADS].reshape(B_HEADS, 1, s)

    sinks = small["sinks_a"].reshape(-1)
    slopes = jnp.exp2(-8.0 * jnp.arange(1, A_Q_HEADS + 1, dtype=F32) / A_Q_HEADS)
    y_a, lse_a = _attn_a_fwd(q_a, k_a, v_a, sinks, slopes, "attn_a_fwd")
    y_b, lse_b = _attn_b_fwd(q_b, k_b, v_b, c3, "attn_b_fwd")
    y_c = _attn_c_fwd(q_c, mkvn, "attn_c_fwd")

    s_a = _gate_fwd(y_a, proj, COL_ZA, 256, "gate_a_fwd")
    s_b = _gate_fwd(y_b, proj, COL_ZB, 256, "gate_b_fwd")
    s_c = _gate_fwd(y_c, proj, COL_ZC, 512, "gate_c_fwd")
    u_a = _mm_branch_fwd(s_a, wg["wa"], "branch_a_fwd")
    u_b = _mm_branch_fwd(s_b, wg["wb"], "branch_b_fwd")
    u_c = _mm_branch_fwd(s_c, wg["wc"], "branch_c_fwd")
    ym = _merge_fwd(proj, u_a, u_b, u_c, "merge_fwd")
    y = _mm_nn(ym, wg["wo"], bm=1024, bn=1024, bk=d, o_dtype=F32, name="out_proj", add=x)
    dy, dyb, lpart = _loss_head(y, target, "loss_head")
    loss = 0.5 / d * jnp.sum(lpart)

    dym = _mm_nt(dyb, wg["wo"], bm=1024, bn=1024, bk=d, o_dtype=F32, name="out_proj_bwd_act")
    g = {"wo": _mm_tn(ym, dyb, bm=512, bn=1024, bk=s, o_dtype=BF16, name="out_proj_bwd_w")}

    dgate = lax.empty((s, 3 * d), BF16)
    dgate, du_a = _merge_bwd(dym, u_a, proj, 0, dgate, "merge_a_bwd")
    dgate, du_b = _merge_bwd(dym, u_b, proj, 1, dgate, "merge_b_bwd")
    dgate, du_c = _merge_bwd(dym, u_c, proj, 2, dgate, "merge_c_bwd")
    g["wm_g"] = _mm_tn(hn, dgate, bm=512, bn=1024, bk=s, o_dtype=BF16, name="proj_gate_bwd_w")
    dhn = _mm_nt(dgate, wg["wm"], bm=1024, bn=1024, bk=2048, o_dtype=F32, name="proj_gate_bwd_act", b_col0=COL_GATE)

    ds_a = _mm_branch_bwd_act(du_a, wg["wa"], A_WIDTH, "branch_a_bwd_act")
    ds_b = _mm_branch_bwd_act(du_b, wg["wb"], B_WIDTH, "branch_b_bwd_act")
    ds_c = _mm_branch_bwd_act(du_c, wg["wc"], C_WIDTH, "branch_c_bwd_act")
    g["wa"] = _mm_branch_bwd_w(s_a, du_a, "branch_a_bwd_w")
    g["wb"] = _mm_branch_bwd_w(s_b, du_b, "branch_b_bwd_w")
    g["wc"] = _mm_branch_bwd_w(s_c, du_c, "branch_c_bwd_w")

    dz = lax.empty((s, W_Z), BF16)
    dz, do_a, dd_a = _gate_bwd(ds_a, y_a, proj, COL_ZA, 256, dz, COL_ZA - COL_ZA, True, "gate_a_bwd")
    dz, do_b, dd_b = _gate_bwd(ds_b, y_b, proj, COL_ZB, 256, dz, COL_ZB - COL_ZA, True, "gate_b_bwd")
    dz, do_c, _ = _gate_bwd(ds_c, y_c, proj, COL_ZC, 512, dz, COL_ZC - COL_ZA, False, "gate_c_bwd")
    g["wm_z"] = _mm_tn(hn, dz, bm=512, bn=1024, bk=s, o_dtype=BF16, name="proj_z_bwd_w")
    dhn = _mm_nt(dz, wg["wm"], bm=1024, bn=1024, bk=2048, o_dtype=F32, name="proj_z_bwd_act", b_col0=COL_ZA, add=dhn)

    dq_a, dkv_a, dsink = _attn_a_bwd(q_a, k_a, v_a, do_a, lse_a, dd_a, sinks, slopes, "attn_a_bwd")
    early = _Comm("scatter", [split8(g[n]) for n in EARLY_GRADS]) if dist else None
    dq_b, dk_b, dv_b, dc3, early_parts = _attn_b_bwd(q_b, k_b, v_b, do_b, lse_b, dd_b, c3, "attn_b_bwd", comm=early)
    dq_c, dmkvn = _attn_c_bwd(q_c, mkvn, do_c, "attn_c_bwd")

    dqkv = lax.empty((s, W_QKV), BF16)
    dqkv, dg_qa = _headnorm_bwd(proj, COL_QA, A_WIDTH, 256, HEAD_DIM, gain_a[:, 0:768], flag_a[:, 0:768], dq_a, dqkv, COL_QA, "hn_qa_bwd")
    dqkv, dg_kva = _headnorm_bwd(proj, COL_KA, 512, 256, HEAD_DIM, gain_a[:, 768:1280], flag_a[:, 768:1280], dkv_a, dqkv, COL_KA, "hn_kva_bwd")
    dqkv, dg_qb = _headnorm_bwd(proj, COL_QB, B_WIDTH, 256, HEAD_DIM, gain_b[:, 0:768], flag_b[:, 0:768], dq_b, dqkv, COL_QB, "hn_qb_bwd")
    dqkv, dg_kb = _headnorm_bwd(proj, COL_KB, B_WIDTH, 256, HEAD_DIM, gain_b[:, 768:1536], flag_b[:, 768:1536], dk_b, dqkv, COL_KB, "hn_kb_bwd")
    dqkv, _ = _headnorm_bwd(proj, COL_VB, B_WIDTH, 256, HEAD_DIM, gain_b[:, 1536:2304], flag_b[:, 1536:2304], dv_b, dqkv, COL_VB, "hn_vb_bwd")
    dqkv, dg_qc = _headnorm_bwd(proj, COL_QC, C_WIDTH, 512, C_HEAD_DIM, gain_cq, ones(C_WIDTH), dq_c, dqkv, COL_QC, "hn_qc_bwd")
    dmkv, dg_kc = _headnorm_bwd(mkv, 0, 2 * C_WIDTH, 2 * C_WIDTH, C_HEAD_DIM, gain_ck, flag_ck, dmkvn, None, 0, "hn_kc_bwd")

    dct = jnp.pad(dc3.reshape(B_HEADS, s), ((0, 16 - B_HEADS), (0, 0)))
    dfb, dbf = _fox_prep_bwd(pfb, bpad, dct, "fox_prep_bwd")

    dmn = _mm_nt(dmkv, wg["wk"], bm=256, bn=1024, bk=1024, o_dtype=F32, name="mem_kv_bwd_act")
    g["wk"] = _mm_tn(mn, dmkv, bm=512, bn=1024, bk=mem.shape[0], o_dtype=BF16, name="mem_kv_bwd_w")
    _, dg_mem = _rmsnorm_bwd(mem, dmn, small["mem_norm_gain"], None, "rms_mem_bwd")

    g["wm_qkv"] = _mm_tn(hn, dqkv, bm=512, bn=1024, bk=s, o_dtype=BF16, name="proj_qkv_bwd_w")
    g["wf"] = _mm_tn(hn, dfb, bm=512, bn=FB_PAD, bk=s, o_dtype=BF16, name="proj_fb_bwd_w")
    dhn = _mm_nt(dfb, wg["wf"], bm=1024, bn=1024, bk=FB_PAD, o_dtype=F32, name="proj_fb_bwd_act", add=dhn)
    late = _Comm("scatter", [split8(g[n]) for n in LATE_GRADS]) if dist else None
    dhn = _mm_nt(dqkv, wg["wm"], bm=1024, bn=1024, bk=2048, o_dtype=F32, name="proj_qkv_bwd_act", b_col0=COL_QA, add=dhn, comm=late)
    if dist:
        dhn, late_parts = dhn
        g = dict(zip(EARLY_GRADS + LATE_GRADS, early_parts + late_parts))
    grad_x, dg_x = _rmsnorm_bwd(x, dhn, small["norm_gain"], dy, "rms_x_bwd")

    fold = lambda part, heads, hd: jnp.sum(jnp.sum(part, axis=0).reshape(heads, hd), axis=0).reshape(1, hd)
    small_grads = {
        "norm_gain": jnp.sum(dg_x, axis=0).reshape(1, d),
        "mem_norm_gain": jnp.sum(dg_mem, axis=0).reshape(1, d),
        "b_forget": dbf[0:B_HEADS, 0].reshape(1, B_HEADS),
        "q_gain_a": fold(dg_qa, A_Q_HEADS, HEAD_DIM) * scale_ab,
        "k_gain_a": fold(dg_kva[:, 0:A_KV_WIDTH], A_KV_HEADS, HEAD_DIM),
        "sinks_a": (jnp.sum(dsink, axis=(1, 2)) * (1.0 / HEAD_DIM)).reshape(1, A_Q_HEADS),
        "q_gain_b": fold(dg_qb, B_HEADS, HEAD_DIM) * scale_ab,
        "k_gain_b": fold(dg_kb, B_HEADS, HEAD_DIM),
        "q_gain_c": fold(dg_qc, C_HEADS, C_HEAD_DIM),
        "k_gain_c": fold(dg_kc[:, 0:C_WIDTH], C_HEADS, C_HEAD_DIM),
    }
    return loss, grad_x, small_grads, g


def _coords():
    return lax.axis_index("x"), lax.axis_index("y"), lax.axis_index("c")


def _all_gather(shards, name):
    n = len(shards)

    def body(*refs):
        ins = refs[0:n]
        outs = refs[n:2 * n]
        send_sems, recv_sems, local_sems = refs[2 * n:2 * n + 3]
        x, y, c = _coords()
        me, sibling = (x, y, c), (x, y, 1 - c)
        chips = [(1 - x, y), (x, 1 - y), (1 - x, 1 - y)]
        idx = lambda p: 4 * p[0] + 2 * p[1] + p[2]

        def copy(a, k, block, to, src=None):
            slot = outs[a].at[idx(block)]
            return pltpu.make_async_remote_copy(
                src_ref=slot if src is None else src, dst_ref=slot,
                send_sem=send_sems.at[a, k], recv_sem=recv_sems.at[a, k], device_id=to, device_id_type=MESH)

        mine = [pltpu.make_async_copy(ins[a], outs[a].at[idx(me)], local_sems.at[a]) for a in range(n)]
        for cp in mine:
            cp.start()
        first = []
        for a in range(n):
            first.append(copy(a, 0, me, sibling, src=ins[a]))
            first += [copy(a, 1 + j, me, (*chip, c), src=ins[a]) for j, chip in enumerate(chips)]
        for cp in first:
            cp.start()
        passed = []
        for j, chip in enumerate(chips):
            for a in range(n):
                copy(a, 1 + j, (*chip, c), me).wait_recv()
                fwd = copy(a, 4 + j, (*chip, c), sibling)
                fwd.start()
                passed.append(fwd)
        for a in range(n):
            copy(a, 0, sibling, me).wait_recv()
            for j, chip in enumerate(chips):
                copy(a, 4 + j, (*chip, 1 - c), me).wait_recv()
        for cp in first + passed:
            cp.wait_send()
        for cp in mine:
            cp.wait()

    any_spec = pl.BlockSpec(memory_space=pl.ANY)
    return pl.pallas_call(
        body, name=name,
        in_specs=[any_spec] * n, out_specs=[any_spec] * n,
        out_shape=[jax.ShapeDtypeStruct((N_DEV,) + sh.shape, sh.dtype) for sh in shards],
        scratch_shapes=[pltpu.SemaphoreType.DMA((n, 7)), pltpu.SemaphoreType.DMA((n, 7)), pltpu.SemaphoreType.DMA((n,))],
    )(*shards)


def _all_reduce_small(vec, name):
    p = vec.shape[1]

    def body(v_ref, o_ref, gather, send_sems, recv_sems):
        x, y, c = _coords()
        my = 4 * x + 2 * y + c
        peers = [(x ^ ((k >> 2) & 1), y ^ ((k >> 1) & 1), c ^ (k & 1)) for k in range(1, N_DEV)]
        gather[my] = v_ref[...]
        sends = [pltpu.make_async_remote_copy(
            src_ref=v_ref, dst_ref=gather.at[my], send_sem=send_sems.at[k], recv_sem=recv_sems.at[k],
            device_id=peer, device_id_type=MESH) for k, peer in enumerate(peers)]
        for cp in sends:
            cp.start()
        for k, peer in enumerate(peers):
            pid = 4 * peer[0] + 2 * peer[1] + peer[2]
            pltpu.make_async_remote_copy(
                src_ref=v_ref, dst_ref=gather.at[pid], send_sem=send_sems.at[k], recv_sem=recv_sems.at[k],
                device_id=peer, device_id_type=MESH).wait_recv()
        for cp in sends:
            cp.wait_send()
        total = gather[0]
        for j in range(1, N_DEV):
            total = total + gather[j]
        o_ref[...] = total

    vm = pl.BlockSpec(memory_space=pltpu.VMEM)
    return pl.pallas_call(
        body, name=name, in_specs=[vm], out_specs=vm,
        out_shape=jax.ShapeDtypeStruct((8, p), F32),
        scratch_shapes=[pltpu.VMEM((N_DEV, 8, p), F32), pltpu.SemaphoreType.DMA((7,)), pltpu.SemaphoreType.DMA((7,))],
    )(vec)[0:1]


def _sum_parts(parts, name):
    _, rows, cols = parts.shape
    br = _tile(rows, 64, 16)

    def body(p_ref, o_ref):
        total = p_ref[0].astype(F32)
        for j in range(1, N_DEV):
            total = total + p_ref[j].astype(F32)
        o_ref[...] = total

    return pl.pallas_call(
        body, name=name, grid=(rows // br,),
        in_specs=[pl.BlockSpec((N_DEV, br, cols), lambda i: (0, i, 0))],
        out_specs=pl.BlockSpec((br, cols), lambda i: (i, 0)),
        out_shape=jax.ShapeDtypeStruct((rows, cols), F32),
        compiler_params=_params(("parallel",), VMEM_BIG),
    )(parts)


def _adamw(w, g, m, v, name):
    rows, cols = w.shape
    br = _tile(rows, 32, 8)
    c1 = 1.0 / (1.0 - ADAM_B1 ** ADAM_STEP)
    c2 = 1.0 / (1.0 - ADAM_B2 ** ADAM_STEP)

    def body(w_ref, g_ref, m_ref, v_ref, d_ref, nm_ref, nv_ref):
        gv = g_ref[...]
        nm = ADAM_B1 * m_ref[...] + (1.0 - ADAM_B1) * gv
        nv = ADAM_B2 * v_ref[...] + (1.0 - ADAM_B2) * (gv * gv)
        d_ref[...] = -ADAM_LR * ((nm * c1) / (jnp.sqrt(nv * c2) + ADAM_EPS) + ADAM_WD * w_ref[...])
        nm_ref[...] = nm
        nv_ref[...] = nv

    spec = pl.BlockSpec((br, cols), lambda i: (i, 0))
    shape = jax.ShapeDtypeStruct((rows, cols), F32)
    return pl.pallas_call(
        body, name=name, grid=(rows // br,), in_specs=[spec] * 4, out_specs=[spec] * 3, out_shape=[shape] * 3,
        compiler_params=_params(("parallel",), VMEM_BIG),
    )(w, g, m, v)


SMALL_NAMES = ("norm_gain", "mem_norm_gain", "b_forget", "q_gain_a", "k_gain_a", "sinks_a",
               "q_gain_b", "k_gain_b", "q_gain_c", "k_gain_c")
BIG_NAMES = ("w_in", "w_mem_kv", "w_branch_a", "w_branch_b", "w_branch_c", "w_out")
WEIGHT_ORDER = ("norm_gain", "mem_norm_gain", "w_in", "b_forget", "q_gain_a", "k_gain_a", "sinks_a", "q_gain_b",
                "k_gain_b", "q_gain_c", "k_gain_c", "w_mem_kv", "w_branch_a", "w_branch_b", "w_branch_c", "w_out")


def _pack_small(tree):
    flat = jnp.concatenate([tree[n].reshape(1, -1) for n in SMALL_NAMES], axis=1)
    pad = (-flat.shape[1]) % LANES
    return jnp.pad(flat, ((0, 0), (0, pad)))


def _unpack_small(flat, like):
    out, off = {}, 0
    for n in SMALL_NAMES:
        size = like[n].size
        out[n] = flat[:, off:off + size].reshape(like[n].shape)
        off += size
    return out


def kernel(x, mem, norm_gain, mem_norm_gain, w_in, b_forget, q_gain_a, k_gain_a, sinks_a, q_gain_b, k_gain_b, q_gain_c, k_gain_c, w_mem_kv, w_branch_a, w_branch_b, w_branch_c, w_out, loss_target, m_norm_gain, m_mem_norm_gain, m_w_in, m_b_forget, m_q_gain_a, m_k_gain_a, m_sinks_a, m_q_gain_b, m_k_gain_b, m_q_gain_c, m_k_gain_c, m_w_mem_kv, m_w_branch_a, m_w_branch_b, m_w_branch_c, m_w_out, v_norm_gain, v_mem_norm_gain, v_w_in, v_b_forget, v_q_gain_a, v_k_gain_a, v_sinks_a, v_q_gain_b, v_k_gain_b, v_q_gain_c, v_k_gain_c, v_w_mem_kv, v_w_branch_a, v_w_branch_b, v_w_branch_c, v_w_out):
    weights = dict(norm_gain=norm_gain, mem_norm_gain=mem_norm_gain, w_in=w_in, b_forget=b_forget, q_gain_a=q_gain_a,
                   k_gain_a=k_gain_a, sinks_a=sinks_a, q_gain_b=q_gain_b, k_gain_b=k_gain_b, q_gain_c=q_gain_c,
                   k_gain_c=k_gain_c, w_mem_kv=w_mem_kv, w_branch_a=w_branch_a, w_branch_b=w_branch_b,
                   w_branch_c=w_branch_c, w_out=w_out)
    mom_m = dict(norm_gain=m_norm_gain, mem_norm_gain=m_mem_norm_gain, w_in=m_w_in, b_forget=m_b_forget,
                 q_gain_a=m_q_gain_a, k_gain_a=m_k_gain_a, sinks_a=m_sinks_a, q_gain_b=m_q_gain_b, k_gain_b=m_k_gain_b,
                 q_gain_c=m_q_gain_c, k_gain_c=m_k_gain_c, w_mem_kv=m_w_mem_kv, w_branch_a=m_w_branch_a,
                 w_branch_b=m_w_branch_b, w_branch_c=m_w_branch_c, w_out=m_w_out)
    mom_v = dict(norm_gain=v_norm_gain, mem_norm_gain=v_mem_norm_gain, w_in=v_w_in, b_forget=v_b_forget,
                 q_gain_a=v_q_gain_a, k_gain_a=v_k_gain_a, sinks_a=v_sinks_a, q_gain_b=v_q_gain_b, k_gain_b=v_k_gain_b,
                 q_gain_c=v_q_gain_c, k_gain_c=v_k_gain_c, w_mem_kv=v_w_mem_kv, w_branch_a=v_w_branch_a,
                 w_branch_b=v_w_branch_b, w_branch_c=v_w_branch_c, w_out=v_w_out)
    wi = w_in[0]
    sh_wm = jnp.concatenate([wi[:, a:b] for a, b in SRC_RANGES] + [wi[:, SRC_GATE:]], axis=1).astype(BF16)
    sh_wf = jnp.pad(wi[:, FB_SRC:FB_SRC + B_HEADS], ((0, 0), (0, FB_PAD - B_HEADS))).astype(BF16)
    rest = [w_mem_kv[0].astype(BF16), w_out[0].astype(BF16),
            w_branch_a[0].astype(BF16), w_branch_b[0].astype(BF16), w_branch_c[0].astype(BF16)]
    full = _all_gather([sh_wm, sh_wf], "weights_all_gather")
    wg = {kname: arr.reshape(arr.shape[0] * arr.shape[1], arr.shape[2]) for kname, arr in zip(("wm", "wf"), full)}

    small = {n: weights[n] for n in SMALL_NAMES}
    loss_local, grad_x, small_g, parts = _local_step(x[0], mem[0], loss_target[0], small, wg, rest)

    summed = {kname: _sum_parts(p, "grad_sum_" + kname) for kname, p in parts.items()}
    gq, gz, gf = summed["wm_qkv"], summed["wm_z"], summed["wf"]
    grads = {
        "w_in": jnp.concatenate([gq[:, COL_QA:COL_QB], gz[:, 0:COL_ZB - COL_ZA], gq[:, COL_QB:COL_QC],
                                 gz[:, COL_ZB - COL_ZA:COL_ZC - COL_ZA], gf[:, 0:B_HEADS], gq[:, COL_QC:W_QKV],
                                 gz[:, COL_ZC - COL_ZA:W_Z], summed["wm_g"]], axis=1),
        "w_mem_kv": summed["wk"], "w_out": summed["wo"],
        "w_branch_a": summed["wa"], "w_branch_b": summed["wb"], "w_branch_c": summed["wc"],
    }

    packed = _pack_small(small_g)
    reduced = _all_reduce_small(jnp.broadcast_to(packed, (8, packed.shape[1])), "small_all_reduce")
    grads.update(_unpack_small(reduced, small))

    loss = lax.psum(loss_local, ("x", "y", "c"))

    delta, new_m, new_v = {}, {}, {}
    for n in BIG_NAMES:
        dlt, nm, nv = _adamw(weights[n][0], grads[n], mom_m[n][0], mom_v[n][0], "adamw_" + n)
        delta[n], new_m[n], new_v[n] = dlt[None], nm[None], nv[None]
    pw, pm, pv = _pack_small(small), _pack_small({n: mom_m[n] for n in SMALL_NAMES}), _pack_small({n: mom_v[n] for n in SMALL_NAMES})
    rep8 = lambda a: jnp.broadcast_to(a, (8, a.shape[1]))
    dlt, nm, nv = _adamw(rep8(pw), rep8(reduced), rep8(pm), rep8(pv), "adamw_small")
    for tree, flat in ((delta, dlt), (new_m, nm), (new_v, nv)):
        tree.update(_unpack_small(flat[0:1], small))
    for n in BIG_NAMES:
        grads[n] = grads[n][None]
    return (loss, grad_x[None], *[grads[n] for n in WEIGHT_ORDER], *[delta[n] for n in WEIGHT_ORDER],
            *[new_m[n] for n in WEIGHT_ORDER], *[new_v[n] for n in WEIGHT_ORDER])
```

```python
import functools

import jax
import jax.numpy as jnp
import numpy as np
from jax import lax
from jax.experimental import pallas as pl
from jax.experimental.pallas import tpu as pltpu

F32 = jnp.float32
BF16 = jnp.bfloat16

N_DEV = 8
HEAD_DIM = 64
A_Q_HEADS = 12
A_KV_HEADS = 4
A_GROUP = 3
B_HEADS = 12
C_HEADS = 4
C_HEAD_DIM = 128
WINDOW = 128
A_WIDTH = 768
A_KV_WIDTH = 256
B_WIDTH = 768
C_WIDTH = 512
EPS = 1e-6
NEG = -1e30

COL_QA, COL_KA, COL_VA = 0, 768, 1024
COL_QB, COL_KB, COL_VB = 1280, 2048, 2816
COL_QC = 3584
COL_ZA, COL_ZB, COL_ZC = 4096, 4864, 5632
COL_GATE = 6144
W_QKV, W_Z = 4096, 2048
SRC_RANGES = ((0, 1280), (2048, 4352), (5132, 5644), (1280, 2048), (4352, 5120), (5644, 6156))
SRC_GATE = 6156
FB_SRC = 5120
FB_PAD = 128

ADAM_LR = 0.001
ADAM_B1 = 0.9
ADAM_B2 = 0.999
ADAM_EPS = 1e-08
ADAM_WD = 0.01
ADAM_STEP = 10

VMEM_BIG = 52 * 1024 * 1024
LANES = 128
MESH = pl.DeviceIdType.MESH


def _tile(n, pref, mult=128):
    if n <= pref:
        return n
    t = (pref // mult) * mult
    while t >= mult:
        if n % t == 0:
            return t
        t -= mult
    return n


def _params(sem=None, vmem=None):
    kw = {}
    if sem is not None:
        kw["dimension_semantics"] = sem
    if vmem is not None:
        kw["vmem_limit_bytes"] = vmem
    return pltpu.CompilerParams(**kw)


def _sigmoid(x):
    return 1.0 / (1.0 + jnp.exp(-x))


def _block_diag(hd):
    r = np.arange(LANES)
    return jnp.asarray((r[:, None] // hd) == (r[None, :] // hd), dtype=BF16)


def _seg_sum(t, bd):
    hi = t.astype(BF16)
    lo = (t - hi.astype(F32)).astype(BF16)
    outs = []
    for c in range(t.shape[1] // LANES):
        sl = slice(c * LANES, (c + 1) * LANES)
        outs.append(jnp.dot(hi[:, sl], bd, preferred_element_type=F32) + jnp.dot(lo[:, sl], bd, preferred_element_type=F32))
    return outs[0] if len(outs) == 1 else jnp.concatenate(outs, axis=1)


def _rmsnorm_fwd(x, gain, name):
    rows, d = x.shape
    bm = _tile(rows, 512, 8)

    def body(x_ref, g_ref, o_ref):
        xv = x_ref[...]
        ms = jnp.mean(xv * xv, axis=-1, keepdims=True)
        o_ref[...] = (xv * lax.rsqrt(ms + EPS) * g_ref[...]).astype(BF16)

    return pl.pallas_call(
        body, name=name, grid=(rows // bm,),
        in_specs=[pl.BlockSpec((bm, d), lambda i: (i, 0)), pl.BlockSpec((1, d), lambda i: (0, 0))],
        out_specs=pl.BlockSpec((bm, d), lambda i: (i, 0)),
        out_shape=jax.ShapeDtypeStruct((rows, d), BF16),
        compiler_params=_params(("parallel",)),
    )(x, gain)


def _rmsnorm_bwd(x, dhn, gain, dy, name):
    rows, d = x.shape
    bm = _tile(rows, 256, 8)
    with_dx = dy is not None

    def body(*refs):
        if with_dx:
            x_ref, dh_ref, g_ref, dy_ref, gx_ref, dg_ref = refs
        else:
            x_ref, dh_ref, g_ref, dg_ref = refs
        i = pl.program_id(0)
        xv = x_ref[...]
        rstd = lax.rsqrt(jnp.mean(xv * xv, axis=-1, keepdims=True) + EPS)
        xhat = xv * rstd
        dh = dh_ref[...]
        part = jnp.sum((dh * xhat).reshape(bm // 8, 8, d), axis=0)

        @pl.when(i == 0)
        def _():
            dg_ref[...] = part

        @pl.when(i > 0)
        def _():
            dg_ref[...] += part

        if with_dx:
            g = dh * g_ref[...]
            mean = jnp.mean(g * xhat, axis=-1, keepdims=True)
            gx_ref[...] = dy_ref[...] + rstd * (g - xhat * mean)

    row_spec = pl.BlockSpec((bm, d), lambda i: (i, 0))
    in_specs = [row_spec, row_spec, pl.BlockSpec((1, d), lambda i: (0, 0))]
    args = [x, dhn, gain]
    dg_spec = pl.BlockSpec((8, d), lambda i: (0, 0))
    dg_shape = jax.ShapeDtypeStruct((8, d), F32)
    if with_dx:
        in_specs.append(row_spec)
        args.append(dy)
        out_specs = [row_spec, dg_spec]
        out_shape = [jax.ShapeDtypeStruct((rows, d), F32), dg_shape]
    else:
        out_specs = [dg_spec]
        out_shape = [dg_shape]
    outs = pl.pallas_call(
        body, name=name, grid=(rows // bm,), in_specs=in_specs, out_specs=out_specs, out_shape=out_shape,
        compiler_params=_params(("arbitrary",)),
    )(*args)
    return outs if with_dx else (None, outs[0])


class _Comm:
    def __init__(self, kind, arrays):
        self.kind = kind
        self.arrays = list(arrays)
        self.n = len(self.arrays)

    def out_shapes(self):
        if self.kind == "gather":
            return [jax.ShapeDtypeStruct((N_DEV,) + a.shape, a.dtype) for a in self.arrays]
        return [jax.ShapeDtypeStruct(a.shape, a.dtype) for a in self.arrays]

    def scratch(self):
        return [pltpu.SemaphoreType.DMA((self.n, N_DEV - 1)), pltpu.SemaphoreType.DMA((self.n, N_DEV - 1)),
                pltpu.SemaphoreType.DMA((self.n,))]

    def _plan(self, ins, outs, sems, with_recvs):
        send_sems, recv_sems, local_sems = sems
        x, y, c = lax.axis_index("x"), lax.axis_index("y"), lax.axis_index("c")
        my = 4 * x + 2 * y + c
        gather = self.kind == "gather"
        local, sends, recvs = [], [], []
        for a in range(self.n):
            local.append(pltpu.make_async_copy(ins[a] if gather else ins[a].at[my], outs[a].at[my], local_sems.at[a]))
            for k in range(1, N_DEV):
                peer = (x ^ ((k >> 2) & 1), y ^ ((k >> 1) & 1), c ^ (k & 1))
                pid = 4 * peer[0] + 2 * peer[1] + peer[2]
                src = ins[a] if gather else ins[a].at[pid]
                sem = dict(send_sem=send_sems.at[a, k - 1], recv_sem=recv_sems.at[a, k - 1], device_id=peer, device_id_type=MESH)
                sends.append(pltpu.make_async_remote_copy(src_ref=src, dst_ref=outs[a].at[my], **sem))
                if with_recvs:
                    recvs.append(pltpu.make_async_remote_copy(src_ref=src, dst_ref=outs[a].at[pid], **sem))
        return local, sends, recvs

    def start(self, ins, outs, sems):
        local, sends, _ = self._plan(ins, outs, sems, False)
        for cp in local + sends:
            cp.start()

    def wait(self, ins, outs, sems):
        local, sends, recvs = self._plan(ins, outs, sems, True)
        for cp in recvs:
            cp.wait_recv()
        for cp in sends:
            cp.wait_send()
        for cp in local:
            cp.wait()


def _grid_edges(grid):
    first = last = None
    for ax, size in enumerate(grid):
        pid = pl.program_id(ax)
        f, l = pid == 0, pid == size - 1
        first = f if first is None else first & f
        last = l if last is None else last & l
    return first, last


def _mm(a, b, *, grid, a_spec, b_spec, o_spec, o_shape, o_dtype, contract, name, add=None, add_spec=None, acc_shape=None,
        comm=None):
    nk = grid[2]
    has_add = add is not None
    n_in = 3 if has_add else 2
    nc = comm.n if comm is not None else 0

    def body(*refs):
        a_ref, b_ref = refs[0], refs[1]
        add_ref = refs[2] if has_add else None
        comm_in = refs[n_in:n_in + nc]
        o_ref = refs[n_in + nc]
        comm_out = refs[n_in + nc + 1:n_in + 2 * nc + 1]
        scratch = refs[n_in + 2 * nc + 1:]
        if nc:
            first, last = _grid_edges(grid)

            @pl.when(first)
            def _():
                comm.start(comm_in, comm_out, scratch[-3:])

        part = lax.dot_general(a_ref[...], b_ref[...], (contract, ((), ())), preferred_element_type=F32)
        if nk == 1:
            if has_add:
                part = part + add_ref[...]
            o_ref[...] = part.astype(o_dtype)
        else:
            acc = scratch[0]
            k = pl.program_id(2)

            @pl.when(k == 0)
            def _():
                acc[...] = part

            @pl.when(k > 0)
            def _():
                acc[...] += part

            @pl.when(k == nk - 1)
            def _():
                r = acc[...]
                if has_add:
                    r = r + add_ref[...]
                o_ref[...] = r.astype(o_dtype)

        if nc:
            @pl.when(last)
            def _():
                comm.wait(comm_in, comm_out, scratch[-3:])

    any_spec = pl.BlockSpec(memory_space=pl.ANY)
    in_specs = [a_spec, b_spec]
    args = [a, b]
    if has_add:
        in_specs.append(add_spec)
        args.append(add)
    scratch = [pltpu.VMEM(acc_shape, F32)] if nk > 1 else []
    out_specs, out_shape = o_spec, jax.ShapeDtypeStruct(o_shape, o_dtype)
    sem = ("parallel", "parallel", "arbitrary")
    if nc:
        in_specs += [any_spec] * nc
        args += comm.arrays
        out_specs = [o_spec] + [any_spec] * nc
        out_shape = [out_shape] + comm.out_shapes()
        scratch += comm.scratch()
        sem = ("arbitrary", "arbitrary", "arbitrary")
    res = pl.pallas_call(
        body, name=name, grid=grid, in_specs=in_specs, out_specs=out_specs, out_shape=out_shape, scratch_shapes=scratch,
        compiler_params=_params(sem, VMEM_BIG),
    )(*args)
    return (res[0], list(res[1:])) if nc else res


def _mm_nn(a, b, *, bm, bn, bk, o_dtype, name, add=None, comm=None):
    m, kd = a.shape
    n = b.shape[1]
    bm, bn, bk = _tile(m, bm, 8), _tile(n, bn), _tile(kd, bk)
    o_spec = pl.BlockSpec((bm, bn), lambda i, j, k: (i, j))
    return _mm(a, b, grid=(m // bm, n // bn, kd // bk),
               a_spec=pl.BlockSpec((bm, bk), lambda i, j, k: (i, k)),
               b_spec=pl.BlockSpec((bk, bn), lambda i, j, k: (k, j)),
               o_spec=o_spec, o_shape=(m, n), o_dtype=o_dtype, contract=((1,), (0,)), name=name,
               add=add, add_spec=o_spec, acc_shape=(bm, bn), comm=comm)


def _mm_nt(a, b, *, bm, bn, bk, o_dtype, name, add=None, b_col0=0, comm=None):
    m, kd = a.shape
    n = b.shape[0]
    bm, bn, bk = _tile(m, bm, 8), _tile(n, bn), _tile(kd, bk)
    kb0 = b_col0 // bk
    assert kb0 * bk == b_col0
    o_spec = pl.BlockSpec((bm, bn), lambda i, j, k: (i, j))
    return _mm(a, b, grid=(m // bm, n // bn, kd // bk),
               a_spec=pl.BlockSpec((bm, bk), lambda i, j, k: (i, k)),
               b_spec=pl.BlockSpec((bn, bk), lambda i, j, k: (j, kb0 + k)),
               o_spec=o_spec, o_shape=(m, n), o_dtype=o_dtype, contract=((1,), (1,)), name=name,
               add=add, add_spec=o_spec, acc_shape=(bm, bn), comm=comm)


def _mm_tn(a, b, *, bm, bn, bk, o_dtype, name):
    kd, m = a.shape
    n = b.shape[1]
    bm, bn, bk = _tile(m, bm), _tile(n, bn), _tile(kd, bk, 8)
    return _mm(a, b, grid=(m // bm, n // bn, kd // bk),
               a_spec=pl.BlockSpec((bk, bm), lambda i, j, k: (k, i)),
               b_spec=pl.BlockSpec((bk, bn), lambda i, j, k: (k, j)),
               o_spec=pl.BlockSpec((bm, bn), lambda i, j, k: (i, j)),
               o_shape=(m, n), o_dtype=o_dtype, contract=((0,), (0,)), name=name, acc_shape=(bm, bn))


def _mm_branch_fwd(s, w2d, name):
    m, kb = s.shape
    ds = w2d.shape[1]
    bm = _tile(m, 1024, 8)
    return _mm(s, w2d, grid=(m // bm, N_DEV, 1),
               a_spec=pl.BlockSpec((bm, kb), lambda i, j, k: (i, 0)),
               b_spec=pl.BlockSpec((kb, ds), lambda i, j, k: (j, 0)),
               o_spec=pl.BlockSpec((bm, ds), lambda i, j, k: (i, j)),
               o_shape=(m, N_DEV * ds), o_dtype=F32, contract=((1,), (0,)), name=name)


def _mm_branch_bwd_act(du, w2d, kb, name):
    m = du.shape[0]
    ds = w2d.shape[1]
    bm = _tile(m, 1024, 8)
    return _mm(du, w2d, grid=(m // bm, 1, N_DEV),
               a_spec=pl.BlockSpec((bm, ds), lambda i, j, k: (i, k)),
               b_spec=pl.BlockSpec((kb, ds), lambda i, j, k: (k, 0)),
               o_spec=pl.BlockSpec((bm, kb), lambda i, j, k: (i, 0)),
               o_shape=(m, kb), o_dtype=F32, contract=((1,), (1,)), name=name, acc_shape=(bm, kb))


def _mm_branch_bwd_w(s, du, name):
    m, kb = s.shape
    ds = du.shape[1] // N_DEV
    bk = _tile(m, 2048, 8)
    return _mm(s, du, grid=(1, N_DEV, m // bk),
               a_spec=pl.BlockSpec((bk, kb), lambda i, j, k: (k, 0)),
               b_spec=pl.BlockSpec((bk, ds), lambda i, j, k: (k, j)),
               o_spec=pl.BlockSpec((kb, ds), lambda i, j, k: (j, 0)),
               o_shape=(N_DEV * kb, ds), o_dtype=BF16, contract=((0,), (0,)), name=name, acc_shape=(kb, ds))


def _headnorm_fwd(src, c0, width, bw, hd, gain, nflag, head_major, name):
    rows = src.shape[0]
    bm = _tile(rows, 1024, 8)
    bd = _block_diag(hd)
    cb0 = c0 // bw

    def body(x_ref, g_ref, f_ref, bd_ref, o_ref):
        xv = x_ref[...]
        ss = _seg_sum(xv * xv, bd_ref[...])
        rstd = lax.rsqrt(ss * (1.0 / hd) + EPS)
        y = (xv * jnp.where(f_ref[...] > 0.0, rstd, 1.0) * g_ref[...]).astype(BF16)
        if head_major:
            for h in range(bw // HEAD_DIM):
                o_ref[h] = y[:, h * HEAD_DIM:(h + 1) * HEAD_DIM]
        else:
            o_ref[...] = y

    vec_spec = pl.BlockSpec((1, bw), lambda i, t: (0, t))
    if head_major:
        hpb = bw // HEAD_DIM
        out_spec = pl.BlockSpec((hpb, bm, HEAD_DIM), lambda i, t: (t, i, 0))
        out_shape = jax.ShapeDtypeStruct((width // HEAD_DIM, rows, HEAD_DIM), BF16)
    else:
        out_spec = pl.BlockSpec((bm, bw), lambda i, t: (i, t))
        out_shape = jax.ShapeDtypeStruct((rows, width), BF16)
    return pl.pallas_call(
        body, name=name, grid=(rows // bm, width // bw),
        in_specs=[pl.BlockSpec((bm, bw), lambda i, t: (i, cb0 + t)), vec_spec, vec_spec,
                  pl.BlockSpec((LANES, LANES), lambda i, t: (0, 0))],
        out_specs=out_spec, out_shape=out_shape,
        compiler_params=_params(("parallel", "parallel")),
    )(src, gain, nflag, bd)


def _headnorm_bwd(src, c0, width, bw, hd, gain, nflag, dyn, target, t0, name):
    rows = src.shape[0]
    bm = _tile(rows, 1024, 8)
    bd = _block_diag(hd)
    cb0 = c0 // bw
    tb0 = t0 // bw
    aliased = target is not None

    def body(*refs):
        if aliased:
            x_ref, dy_ref, g_ref, f_ref, bd_ref, _, o_ref, dg_ref = refs
        else:
            x_ref, dy_ref, g_ref, f_ref, bd_ref, o_ref, dg_ref = refs
        i = pl.program_id(1)
        xv = x_ref[...]
        dyv = dy_ref[...]
        bdv = bd_ref[...]
        rstd = lax.rsqrt(_seg_sum(xv * xv, bdv) * (1.0 / hd) + EPS)
        xhat = xv * rstd
        g = dyv * g_ref[...]
        mean = _seg_sum(g * xhat, bdv) * (1.0 / hd)
        dx = jnp.where(f_ref[...] > 0.0, rstd * (g - xhat * mean), g)
        o_ref[...] = dx.astype(BF16)
        part = jnp.sum((dyv * xhat).reshape(bm // 8, 8, bw), axis=0)

        @pl.when(i == 0)
        def _():
            dg_ref[...] = part

        @pl.when(i > 0)
        def _():
            dg_ref[...] += part

    vec_spec = pl.BlockSpec((1, bw), lambda t, i: (0, t))
    in_specs = [pl.BlockSpec((bm, bw), lambda t, i: (i, cb0 + t)), pl.BlockSpec((bm, bw), lambda t, i: (i, t)),
                vec_spec, vec_spec, pl.BlockSpec((LANES, LANES), lambda t, i: (0, 0))]
    args = [src, dyn, gain, nflag, bd]
    aliases = {}
    if aliased:
        in_specs.append(pl.BlockSpec(memory_space=pl.ANY))
        args.append(target)
        aliases = {5: 0}
        o_shape = jax.ShapeDtypeStruct(target.shape, BF16)
    else:
        o_shape = jax.ShapeDtypeStruct((rows, width), BF16)
    out, dg = pl.pallas_call(
        body, name=name, grid=(width // bw, rows // bm), in_specs=in_specs,
        out_specs=[pl.BlockSpec((bm, bw), lambda t, i: (i, tb0 + t)), pl.BlockSpec((8, bw), lambda t, i: (0, t))],
        out_shape=[o_shape, jax.ShapeDtypeStruct((8, width), F32)],
        input_output_aliases=aliases,
        compiler_params=_params(("parallel", "arbitrary")),
    )(*args)
    return out, dg


def _fox_prep(pfb, bpad, name):
    s = pfb.shape[0]

    def body(p_ref, b_ref, c_ref):
        z = p_ref[...] + b_ref[...]
        logf = jnp.minimum(z, 0.0) - jnp.log(1.0 + jnp.exp(-jnp.abs(z)))
        x = logf.T[0:16, :]
        lane = lax.broadcasted_iota(jnp.int32, (16, s), 1)
        sh = 1
        while sh < s:
            x = x + jnp.where(lane >= sh, pltpu.roll(x, sh, 1), 0.0)
            sh *= 2
        c_ref[...] = x

    return pl.pallas_call(
        body, name=name, grid=(1,),
        in_specs=[pl.BlockSpec((s, FB_PAD), lambda i: (0, 0)), pl.BlockSpec((1, FB_PAD), lambda i: (0, 0))],
        out_specs=pl.BlockSpec((16, s), lambda i: (0, 0)),
        out_shape=jax.ShapeDtypeStruct((16, s), F32),
        compiler_params=_params(("arbitrary",)),
    )(pfb, bpad)


def _fox_prep_bwd(pfb, bpad, dct, name):
    s = pfb.shape[0]

    def body(p_ref, b_ref, dc_ref, df_ref, db_ref):
        zt = (p_ref[...] + b_ref[...]).T[0:16, :]
        y = dc_ref[...]
        lane = lax.broadcasted_iota(jnp.int32, (16, s), 1)
        sh = 1
        while sh < s:
            y = y + jnp.where(lane < s - sh, pltpu.roll(y, s - sh, 1), 0.0)
            sh *= 2
        dz = y * _sigmoid(-zt)
        db_ref[...] = jnp.broadcast_to(jnp.sum(dz, axis=1, keepdims=True), (16, FB_PAD))
        full = jnp.concatenate([dz, jnp.zeros((FB_PAD - 16, s), F32)], axis=0)
        df_ref[...] = full.T.astype(BF16)

    return pl.pallas_call(
        body, name=name, grid=(1,),
        in_specs=[pl.BlockSpec((s, FB_PAD), lambda i: (0, 0)), pl.BlockSpec((1, FB_PAD), lambda i: (0, 0)),
                  pl.BlockSpec((16, s), lambda i: (0, 0))],
        out_specs=[pl.BlockSpec((s, FB_PAD), lambda i: (0, 0)), pl.BlockSpec((16, FB_PAD), lambda i: (0, 0))],
        out_shape=[jax.ShapeDtypeStruct((s, FB_PAD), BF16), jax.ShapeDtypeStruct((16, FB_PAD), F32)],
        compiler_params=_params(("arbitrary",)),
    )(pfb, bpad, dct)


def _swa_window(n):
    ws = pl.multiple_of(jnp.maximum(n * WINDOW - WINDOW, 0), WINDOW)
    qi = lax.broadcasted_iota(jnp.int32, (WINDOW, 2 * WINDOW), 0)
    kj = lax.broadcasted_iota(jnp.int32, (WINDOW, 2 * WINDOW), 1)
    rel = qi + (n * WINDOW - ws) - kj
    valid = (rel >= 0) & (rel < WINDOW)
    return ws, valid, rel.astype(F32)


def _attn_a_fwd(q, k, v, sinks, slopes, name):
    s = q.shape[1]
    nb = s // WINDOW
    smem = pl.BlockSpec(memory_space=pltpu.SMEM)

    def body(sink_ref, slope_ref, q_ref, k_ref, v_ref, o_ref, lse_ref):
        n = pl.program_id(0)
        ws, valid, relf = _swa_window(n)
        outs = []
        for h in range(A_Q_HEADS):
            kvh = h // A_GROUP
            kw = k_ref[kvh, pl.ds(ws, 2 * WINDOW), :]
            vw = v_ref[kvh, pl.ds(ws, 2 * WINDOW), :]
            sc = lax.dot_general(q_ref[h], kw, (((1,), (1,)), ((), ())), preferred_element_type=F32)
            sc = jnp.where(valid, sc - slope_ref[h] * relf, NEG)
            sink = sink_ref[h]
            m = jnp.maximum(jnp.max(sc, axis=1, keepdims=True), sink)
            p = jnp.exp(sc - m)
            denom = jnp.sum(p, axis=1, keepdims=True) + jnp.exp(sink - m)
            pn = (p / denom).astype(BF16)
            outs.append(jnp.dot(pn, vw, preferred_element_type=F32))
            lse_ref[h] = jnp.broadcast_to(m + jnp.log(denom), (WINDOW, HEAD_DIM))
        o_ref[...] = jnp.concatenate(outs, axis=1)

    return pl.pallas_call(
        body, name=name, grid=(nb,),
        in_specs=[smem, smem,
                  pl.BlockSpec((A_Q_HEADS, WINDOW, HEAD_DIM), lambda n: (0, n, 0)),
                  pl.BlockSpec((A_KV_HEADS, s, HEAD_DIM), lambda n: (0, 0, 0)),
                  pl.BlockSpec((A_KV_HEADS, s, HEAD_DIM), lambda n: (0, 0, 0))],
        out_specs=[pl.BlockSpec((WINDOW, A_WIDTH), lambda n: (n, 0)),
                   pl.BlockSpec((A_Q_HEADS, WINDOW, HEAD_DIM), lambda n: (0, n, 0))],
        out_shape=[jax.ShapeDtypeStruct((s, A_WIDTH), F32), jax.ShapeDtypeStruct((A_Q_HEADS, s, HEAD_DIM), F32)],
        compiler_params=_params(("parallel",), VMEM_BIG),
    )(sinks, slopes, q, k, v)


def _attn_a_bwd(q, k, v, do, lse, dd, sinks, slopes, name):
    s = q.shape[1]
    nb = s // WINDOW
    smem = pl.BlockSpec(memory_space=pltpu.SMEM)
    last = nb - 1

    def body(sink_ref, slope_ref, q_ref, k_ref, v_ref, do_ref, lse_ref, dd_ref, dq_ref, dkv_ref, ds_ref, carry):
        n = pl.program_id(0)

        @pl.when(n == 0)
        def _():
            carry[...] = jnp.zeros(carry.shape, F32)
            ds_ref[...] = jnp.zeros(ds_ref.shape, F32)

        @pl.when(n < nb)
        def _():
            ws, valid, relf = _swa_window(n)
            dqs = []
            dkw = [None] * A_KV_HEADS
            dvw = [None] * A_KV_HEADS
            for h in range(A_Q_HEADS):
                kvh = h // A_GROUP
                qh = q_ref[h]
                doh = do_ref[h]
                kw = k_ref[kvh, pl.ds(ws, 2 * WINDOW), :]
                vw = v_ref[kvh, pl.ds(ws, 2 * WINDOW), :]
                lse_h = lse_ref[h]
                dd_h = dd_ref[h]
                sc = lax.dot_general(qh, kw, (((1,), (1,)), ((), ())), preferred_element_type=F32)
                sc = jnp.where(valid, sc - slope_ref[h] * relf, NEG)
                p = jnp.exp(sc - lse_h[:, 0:1])
                dp = lax.dot_general(doh, vw, (((1,), (1,)), ((), ())), preferred_element_type=F32)
                dsc = (p * (dp - dd_h[:, 0:1])).astype(BF16)
                pb = p.astype(BF16)
                dqs.append(jnp.dot(dsc, kw, preferred_element_type=F32))
                dk_h = lax.dot_general(dsc, qh, (((0,), (0,)), ((), ())), preferred_element_type=F32)
                dv_h = lax.dot_general(pb, doh, (((0,), (0,)), ((), ())), preferred_element_type=F32)
                dkw[kvh] = dk_h if dkw[kvh] is None else dkw[kvh] + dk_h
                dvw[kvh] = dv_h if dvw[kvh] is None else dvw[kvh] + dv_h
                psink = jnp.exp(sink_ref[h] - lse_h)
                ds_ref[h] += jnp.sum((-psink * dd_h).reshape(WINDOW // 8, 8, HEAD_DIM), axis=0)
            dq_ref[...] = jnp.concatenate(dqs, axis=1)
            win = jnp.concatenate(dkw + dvw, axis=1)
            first = win[0:WINDOW]
            second = win[WINDOW:2 * WINDOW]
            dkv_ref[...] = carry[...] + first
            carry[...] = jnp.where(n == 0, first, second)

        @pl.when(n == nb)
        def _():
            dkv_ref[...] = carry[...]

    hm = lambda heads: pl.BlockSpec((heads, WINDOW, HEAD_DIM), lambda n: (0, jnp.minimum(n, last), 0))
    res = lambda heads: pl.BlockSpec((heads, s, HEAD_DIM), lambda n: (0, 0, 0))
    return pl.pallas_call(
        body, name=name, grid=(nb + 1,),
        in_specs=[smem, smem, hm(A_Q_HEADS), res(A_KV_HEADS), res(A_KV_HEADS), hm(A_Q_HEADS), hm(A_Q_HEADS), hm(A_Q_HEADS)],
        out_specs=[pl.BlockSpec((WINDOW, A_WIDTH), lambda n: (jnp.minimum(n, last), 0)),
                   pl.BlockSpec((WINDOW, 2 * A_KV_WIDTH), lambda n: (jnp.maximum(n - 1, 0), 0)),
                   pl.BlockSpec((A_Q_HEADS, 8, HEAD_DIM), lambda n: (0, 0, 0))],
        out_shape=[jax.ShapeDtypeStruct((s, A_WIDTH), F32), jax.ShapeDtypeStruct((s, 2 * A_KV_WIDTH), F32),
                   jax.ShapeDtypeStruct((A_Q_HEADS, 8, HEAD_DIM), F32)],
        scratch_shapes=[pltpu.VMEM((WINDOW, 2 * A_KV_WIDTH), F32)],
        compiler_params=_params(("arbitrary",), VMEM_BIG),
    )(sinks, slopes, q, k, v, do, lse, dd)


def _attn_b_fwd(q, k, v, c3, name):
    heads, s, _ = q.shape
    bq = min(512, s)
    nq = s // bq
    nt = (((1,), (1,)), ((), ()))

    def body(q_ref, k_ref, v_ref, c_ref, o_ref, lse_ref, m_scr, l_scr, acc_scr):
        i = pl.program_id(1)
        r0 = pl.multiple_of(i * bq, bq)
        row = lax.broadcasted_iota(jnp.int32, (bq, bq), 0)
        col = lax.broadcasted_iota(jnp.int32, (bq, bq), 1)
        m_scr[...] = jnp.full((2, bq, LANES), NEG, F32)
        l_scr[...] = jnp.zeros((2, bq, LANES), F32)
        acc_scr[...] = jnp.zeros((2, bq, HEAD_DIM), F32)

        def step(j, masked):
            k0 = pl.multiple_of(j * bq, bq)
            for h2 in range(2):
                kv = k_ref[h2, pl.ds(k0, bq), :]
                vv = v_ref[h2, pl.ds(k0, bq), :]
                cq0 = c_ref[h2, :, pl.ds(r0, LANES)][:, 0:1]
                sc = lax.dot_general(q_ref[h2], kv, nt, preferred_element_type=F32)
                sc = sc + (cq0 - c_ref[h2, :, pl.ds(k0, bq)])
                if masked:
                    sc = jnp.where(col <= row, sc, NEG)
                m_prev = m_scr[h2]
                m_new = jnp.maximum(m_prev, jnp.max(sc, axis=1, keepdims=True))
                alpha = jnp.exp(m_prev - m_new)
                p = jnp.exp(sc - m_new[:, 0:1])
                l_scr[h2] = alpha * l_scr[h2] + jnp.sum(p, axis=1, keepdims=True)
                p_hi = p.astype(BF16)
                p_lo = (p - p_hi.astype(F32)).astype(BF16)
                pv = jnp.dot(p_hi, vv, preferred_element_type=F32) + jnp.dot(p_lo, vv, preferred_element_type=F32)
                acc_scr[h2] = acc_scr[h2] * alpha[:, 0:HEAD_DIM] + pv
                m_scr[h2] = m_new

        def loop_body(j, carry):
            step(j, False)
            return carry

        lax.fori_loop(0, i, loop_body, 0)
        step(i, True)
        outs = []
        for h2 in range(2):
            l = l_scr[h2]
            outs.append(acc_scr[h2] / l[:, 0:HEAD_DIM])
            lse_ref[h2] = (m_scr[h2] + jnp.log(l))[:, 0:HEAD_DIM]
        o_ref[...] = jnp.concatenate(outs, axis=1)

    res = pl.BlockSpec((2, s, HEAD_DIM), lambda hp, i: (hp, 0, 0))
    return pl.pallas_call(
        body, name=name, grid=(heads // 2, nq),
        in_specs=[pl.BlockSpec((2, bq, HEAD_DIM), lambda hp, i: (hp, i, 0)), res, res,
                  pl.BlockSpec((2, 1, s), lambda hp, i: (hp, 0, 0))],
        out_specs=[pl.BlockSpec((bq, 2 * HEAD_DIM), lambda hp, i: (i, hp)),
                   pl.BlockSpec((2, bq, HEAD_DIM), lambda hp, i: (hp, i, 0))],
        out_shape=[jax.ShapeDtypeStruct((s, heads * HEAD_DIM), F32), jax.ShapeDtypeStruct((heads, s, HEAD_DIM), F32)],
        scratch_shapes=[pltpu.VMEM((2, bq, LANES), F32), pltpu.VMEM((2, bq, LANES), F32), pltpu.VMEM((2, bq, HEAD_DIM), F32)],
        compiler_params=_params(("parallel", "parallel"), VMEM_BIG),
    )(q, k, v, c3)


def _attn_b_bwd(q, k, v, do, lse, dd, c3, name, comm=None):
    heads, s, _ = q.shape
    bq = min(512, s)
    nq = s // bq
    nt = (((1,), (1,)), ((), ()))
    tn = (((0,), (0,)), ((), ()))
    nc = comm.n if comm is not None else 0
    grid = (heads // 2, nq)

    def body(*refs):
        q_ref, k_ref, v_ref, do_ref, lse_ref, dd_ref, c_ref = refs[0:7]
        comm_in = refs[7:7 + nc]
        dq_ref, dk_ref, dv_ref, dc_ref = refs[7 + nc:11 + nc]
        comm_out = refs[11 + nc:11 + 2 * nc]
        dq_scr, dk_scr, dv_scr, dc_scr = refs[11 + 2 * nc:15 + 2 * nc]
        sems = refs[15 + 2 * nc:]
        j = pl.program_id(1)
        k0 = pl.multiple_of(j * bq, bq)
        row = lax.broadcasted_iota(jnp.int32, (bq, bq), 0)
        col = lax.broadcasted_iota(jnp.int32, (bq, bq), 1)
        if nc:
            first, last = _grid_edges(grid)

            @pl.when(first)
            def _():
                comm.start(comm_in, comm_out, sems)

        @pl.when(j == 0)
        def _():
            dq_scr[...] = jnp.zeros(dq_scr.shape, F32)

        dk_scr[...] = jnp.zeros((2, bq, HEAD_DIM), F32)
        dv_scr[...] = jnp.zeros((2, bq, HEAD_DIM), F32)
        dc_scr[...] = jnp.zeros((2, 1, bq), F32)

        def step(i, masked):
            r0 = pl.multiple_of(i * bq, bq)
            for h2 in range(2):
                kv = k_ref[h2]
                vv = v_ref[h2]
                qv = q_ref[h2, pl.ds(r0, bq), :]
                dov = do_ref[h2, pl.ds(r0, bq), :]
                lse_v = lse_ref[h2, pl.ds(r0, bq), :][:, 0:1]
                dd_v = dd_ref[h2, pl.ds(r0, bq), :][:, 0:1]
                cq0 = c_ref[h2, :, pl.ds(r0, LANES)][:, 0:1]
                sc = lax.dot_general(qv, kv, nt, preferred_element_type=F32) + (cq0 - c_ref[h2, :, pl.ds(k0, bq)])
                if masked:
                    sc = jnp.where(col <= row, sc, NEG)
                p = jnp.exp(sc - lse_v)
                dp = lax.dot_general(dov, vv, nt, preferred_element_type=F32)
                dsc = p * (dp - dd_v)
                dsb = dsc.astype(BF16)
                dv_scr[h2] += lax.dot_general(p.astype(BF16), dov, tn, preferred_element_type=F32)
                dk_scr[h2] += lax.dot_general(dsb, qv, tn, preferred_element_type=F32)
                dq_scr[h2, pl.ds(r0, bq), :] += jnp.dot(dsb, kv, preferred_element_type=F32)
                dc_scr[h2] -= jnp.sum(dsc, axis=0, keepdims=True)

        def loop_body(i, carry):
            step(i, False)
            return carry

        step(j, True)
        lax.fori_loop(j + 1, nq, loop_body, 0)
        dc_ref[...] = dc_scr[...]
        dk_ref[...] = jnp.concatenate([dk_scr[0], dk_scr[1]], axis=1)
        dv_ref[...] = jnp.concatenate([dv_scr[0], dv_scr[1]], axis=1)

        @pl.when(j == nq - 1)
        def _():
            dq_ref[...] = jnp.concatenate([dq_scr[0], dq_scr[1]], axis=1)

        if nc:
            @pl.when(last)
            def _():
                comm.wait(comm_in, comm_out, sems)

    res = pl.BlockSpec((2, s, HEAD_DIM), lambda hp, j: (hp, 0, 0))
    blk = pl.BlockSpec((2, bq, HEAD_DIM), lambda hp, j: (hp, j, 0))
    tm = jax.ShapeDtypeStruct((s, heads * HEAD_DIM), F32)
    any_spec = pl.BlockSpec(memory_space=pl.ANY)
    in_specs = [res, blk, blk, res, res, res, pl.BlockSpec((2, 1, s), lambda hp, j: (hp, 0, 0))]
    out_specs = [pl.BlockSpec((s, 2 * HEAD_DIM), lambda hp, j: (0, hp)),
                 pl.BlockSpec((bq, 2 * HEAD_DIM), lambda hp, j: (j, hp)),
                 pl.BlockSpec((bq, 2 * HEAD_DIM), lambda hp, j: (j, hp)),
                 pl.BlockSpec((2, 1, bq), lambda hp, j: (hp, 0, j))]
    out_shape = [tm, tm, tm, jax.ShapeDtypeStruct((heads, 1, s), F32)]
    scratch = [pltpu.VMEM((2, s, HEAD_DIM), F32), pltpu.VMEM((2, bq, HEAD_DIM), F32),
               pltpu.VMEM((2, bq, HEAD_DIM), F32), pltpu.VMEM((2, 1, bq), F32)]
    args = [q, k, v, do, lse, dd, c3]
    sem = ("parallel", "arbitrary")
    if nc:
        in_specs += [any_spec] * nc
        args += comm.arrays
        out_specs += [any_spec] * nc
        out_shape += comm.out_shapes()
        scratch += comm.scratch()
        sem = ("arbitrary", "arbitrary")
    res_all = pl.pallas_call(
        body, name=name, grid=grid, in_specs=in_specs, out_specs=out_specs, out_shape=out_shape,
        scratch_shapes=scratch, compiler_params=_params(sem, VMEM_BIG),
    )(*args)
    return res_all[0], res_all[1], res_all[2], res_all[3], list(res_all[4:])


def _attn_c_probs(qh, mkh):
    sc = lax.dot_general(qh, mkh, (((1,), (1,)), ((), ())), preferred_element_type=F32) * (C_HEAD_DIM ** -0.5)
    p = jnp.exp(sc - jnp.max(sc, axis=1, keepdims=True))
    return p / jnp.sum(p, axis=1, keepdims=True)


def _attn_c_fwd(q, mkv, name):
    s = q.shape[0]
    m = mkv.shape[0]
    bq = _tile(s, 512, 8)

    def body(q_ref, mk_ref, mv_ref, o_ref):
        outs = []
        for h in range(C_HEADS):
            sl = slice(h * C_HEAD_DIM, (h + 1) * C_HEAD_DIM)
            pn = _attn_c_probs(q_ref[:, sl], mk_ref[:, sl]).astype(BF16)
            outs.append(jnp.dot(pn, mv_ref[:, sl], preferred_element_type=F32))
        o_ref[...] = jnp.concatenate(outs, axis=1)

    return pl.pallas_call(
        body, name=name, grid=(s // bq,),
        in_specs=[pl.BlockSpec((bq, C_WIDTH), lambda i: (i, 0)), pl.BlockSpec((m, C_WIDTH), lambda i: (0, 0)),
                  pl.BlockSpec((m, C_WIDTH), lambda i: (0, 1))],
        out_specs=pl.BlockSpec((bq, C_WIDTH), lambda i: (i, 0)),
        out_shape=jax.ShapeDtypeStruct((s, C_WIDTH), F32),
        compiler_params=_params(("parallel",)),
    )(q, mkv, mkv)


def _attn_c_bwd(q, mkv, do, name):
    s = q.shape[0]
    m = mkv.shape[0]
    bq = _tile(s, 512, 8)
    tn = (((0,), (0,)), ((), ()))

    def body(q_ref, mk_ref, mv_ref, do_ref, dq_ref, dm_ref):
        i = pl.program_id(0)

        @pl.when(i == 0)
        def _():
            dm_ref[...] = jnp.zeros(dm_ref.shape, F32)

        dqs = []
        for h in range(C_HEADS):
            sl = slice(h * C_HEAD_DIM, (h + 1) * C_HEAD_DIM)
            qh, mkh, mvh, doh = q_ref[:, sl], mk_ref[:, sl], mv_ref[:, sl], do_ref[:, sl]
            pn = _attn_c_probs(qh, mkh)
            dp = lax.dot_general(doh, mvh, (((1,), (1,)), ((), ())), preferred_element_type=F32)
            dsc = (pn * (dp - jnp.sum(pn * dp, axis=1, keepdims=True)) * (C_HEAD_DIM ** -0.5)).astype(BF16)
            dqs.append(jnp.dot(dsc, mkh, preferred_element_type=F32))
            dm_ref[:, sl] += lax.dot_general(dsc, qh, tn, preferred_element_type=F32)
            sv = slice(C_WIDTH + h * C_HEAD_DIM, C_WIDTH + (h + 1) * C_HEAD_DIM)
            dm_ref[:, sv] += lax.dot_general(pn.astype(BF16), doh, tn, preferred_element_type=F32)
        dq_ref[...] = jnp.concatenate(dqs, axis=1)

    row = pl.BlockSpec((bq, C_WIDTH), lambda i: (i, 0))
    return pl.pallas_call(
        body, name=name, grid=(s // bq,),
        in_specs=[row, pl.BlockSpec((m, C_WIDTH), lambda i: (0, 0)), pl.BlockSpec((m, C_WIDTH), lambda i: (0, 1)), row],
        out_specs=[row, pl.BlockSpec((m, 2 * C_WIDTH), lambda i: (0, 0))],
        out_shape=[jax.ShapeDtypeStruct((s, C_WIDTH), F32), jax.ShapeDtypeStruct((m, 2 * C_WIDTH), F32)],
        compiler_params=_params(("arbitrary",)),
    )(q, mkv, mkv, do)


def _gate_fwd(y, proj, zc0, bw, name):
    rows, width = y.shape
    bm = _tile(rows, 1024, 8)
    cb0 = zc0 // bw

    def body(y_ref, z_ref, o_ref):
        z = z_ref[...]
        o_ref[...] = (y_ref[...] * (z * _sigmoid(z))).astype(BF16)

    return pl.pallas_call(
        body, name=name, grid=(rows // bm, width // bw),
        in_specs=[pl.BlockSpec((bm, bw), lambda i, t: (i, t)), pl.BlockSpec((bm, bw), lambda i, t: (i, cb0 + t))],
        out_specs=pl.BlockSpec((bm, bw), lambda i, t: (i, t)),
        out_shape=jax.ShapeDtypeStruct((rows, width), BF16),
        compiler_params=_params(("parallel", "parallel")),
    )(y, proj)


def _gate_bwd(dsv, y, proj, zc0, bw, dproj, t0, head_major, name):
    rows, width = y.shape
    bm = _tile(rows, 1024, 8)
    cb0 = zc0 // bw
    tb0 = t0 // bw
    bd = _block_diag(HEAD_DIM)
    hpb = bw // HEAD_DIM

    def body(*refs):
        if head_major:
            ds_ref, y_ref, z_ref, bd_ref, _, dp_ref, dy_ref, dd_ref = refs
        else:
            ds_ref, y_ref, z_ref, _, dp_ref, dy_ref = refs
        z = z_ref[...]
        sig = _sigmoid(z)
        dsx = ds_ref[...]
        yv = y_ref[...]
        dy = dsx * (z * sig)
        dp_ref[...] = (dsx * yv * (sig * (1.0 + z * (1.0 - sig)))).astype(BF16)
        if head_major:
            dyb = dy.astype(BF16)
            dd = _seg_sum(dyb.astype(F32) * yv, bd_ref[...])
            for h in range(hpb):
                sl = slice(h * HEAD_DIM, (h + 1) * HEAD_DIM)
                dy_ref[h] = dyb[:, sl]
                dd_ref[h] = dd[:, sl]
        else:
            dy_ref[...] = dy.astype(BF16)

    tile = pl.BlockSpec((bm, bw), lambda i, t: (i, t))
    ztile = pl.BlockSpec((bm, bw), lambda i, t: (i, cb0 + t))
    ttile = pl.BlockSpec((bm, bw), lambda i, t: (i, tb0 + t))
    any_spec = pl.BlockSpec(memory_space=pl.ANY)
    dp_shape = jax.ShapeDtypeStruct(dproj.shape, BF16)
    if head_major:
        hm_spec = pl.BlockSpec((hpb, bm, HEAD_DIM), lambda i, t: (t, i, 0))
        nh = width // HEAD_DIM
        outs = pl.pallas_call(
            body, name=name, grid=(rows // bm, width // bw),
            in_specs=[tile, tile, ztile, pl.BlockSpec((LANES, LANES), lambda i, t: (0, 0)), any_spec],
            out_specs=[ttile, hm_spec, hm_spec],
            out_shape=[dp_shape, jax.ShapeDtypeStruct((nh, rows, HEAD_DIM), BF16),
                       jax.ShapeDtypeStruct((nh, rows, HEAD_DIM), F32)],
            input_output_aliases={4: 0},
            compiler_params=_params(("parallel", "parallel")),
        )(dsv, y, proj, bd, dproj)
        return outs[0], outs[1], outs[2]
    outs = pl.pallas_call(
        body, name=name, grid=(rows // bm, width // bw),
        in_specs=[tile, tile, ztile, any_spec],
        out_specs=[ttile, tile],
        out_shape=[dp_shape, jax.ShapeDtypeStruct((rows, width), BF16)],
        input_output_aliases={3: 0},
        compiler_params=_params(("parallel", "parallel")),
    )(dsv, y, proj, dproj)
    return outs[0], outs[1], None


def _merge_fwd(proj, ua, ub, uc, name):
    rows, d = ua.shape
    bm = _tile(rows, 512, 8)
    bw = _tile(d, 512)
    g0 = COL_GATE // bw
    gstep = d // bw

    def body(ga_ref, gb_ref, gc_ref, ua_ref, ub_ref, uc_ref, o_ref):
        y = _sigmoid(ga_ref[...]) * ua_ref[...] + _sigmoid(gb_ref[...]) * ub_ref[...] + _sigmoid(gc_ref[...]) * uc_ref[...]
        o_ref[...] = y.astype(BF16)

    tile = pl.BlockSpec((bm, bw), lambda i, t: (i, t))
    gate = lambda b: pl.BlockSpec((bm, bw), lambda i, t: (i, g0 + b * gstep + t))
    return pl.pallas_call(
        body, name=name, grid=(rows // bm, d // bw),
        in_specs=[gate(0), gate(1), gate(2), tile, tile, tile],
        out_specs=tile, out_shape=jax.ShapeDtypeStruct((rows, d), BF16),
        compiler_params=_params(("parallel", "parallel")),
    )(proj, proj, proj, ua, ub, uc)


def _merge_bwd(dym, u, proj, branch, dproj, name):
    rows, d = u.shape
    bm = _tile(rows, 512, 8)
    bw = _tile(d, 512)
    gb0 = (COL_GATE + branch * d) // bw
    tb0 = (branch * d) // bw

    def body(dy_ref, u_ref, gl_ref, _, dp_ref, du_ref):
        g = _sigmoid(gl_ref[...])
        dyv = dy_ref[...]
        du_ref[...] = (g * dyv).astype(BF16)
        dp_ref[...] = (dyv * u_ref[...] * g * (1.0 - g)).astype(BF16)

    tile = pl.BlockSpec((bm, bw), lambda i, t: (i, t))
    gtile = pl.BlockSpec((bm, bw), lambda i, t: (i, gb0 + t))
    ttile = pl.BlockSpec((bm, bw), lambda i, t: (i, tb0 + t))
    outs = pl.pallas_call(
        body, name=name, grid=(rows // bm, d // bw),
        in_specs=[tile, tile, gtile, pl.BlockSpec(memory_space=pl.ANY)],
        out_specs=[ttile, tile],
        out_shape=[jax.ShapeDtypeStruct(dproj.shape, BF16), jax.ShapeDtypeStruct((rows, d), BF16)],
        input_output_aliases={3: 0},
        compiler_params=_params(("parallel", "parallel")),
    )(dym, u, proj, dproj)
    return outs[0], outs[1]


def _loss_head(y, target, name):
    rows, d = y.shape
    bm = _tile(rows, 256, 8)

    def body(y_ref, t_ref, dy_ref, dyb_ref, l_ref):
        i = pl.program_id(0)
        diff = y_ref[...] - t_ref[...]
        dy = diff * (1.0 / d)
        dy_ref[...] = dy
        dyb_ref[...] = dy.astype(BF16)
        sq = diff * diff
        part = sq[:, 0:LANES]
        for c in range(1, d // LANES):
            part = part + sq[:, c * LANES:(c + 1) * LANES]
        part = jnp.sum(part.reshape(bm // 8, 8, LANES), axis=0)

        @pl.when(i == 0)
        def _():
            l_ref[...] = part

        @pl.when(i > 0)
        def _():
            l_ref[...] += part

    row = pl.BlockSpec((bm, d), lambda i: (i, 0))
    return pl.pallas_call(
        body, name=name, grid=(rows // bm,), in_specs=[row, row],
        out_specs=[row, row, pl.BlockSpec((8, LANES), lambda i: (0, 0))],
        out_shape=[jax.ShapeDtypeStruct((rows, d), F32), jax.ShapeDtypeStruct((rows, d), BF16),
                   jax.ShapeDtypeStruct((8, LANES), F32)],
        compiler_params=_params(("arbitrary",)),
    )(y, target)


def _row(vec, reps=1):
    return jnp.tile(vec.reshape(1, -1).astype(F32), (1, reps))


REST_KEYS = ("wk", "wo", "wa", "wb", "wc")
EARLY_GRADS = ("wm_g", "wm_z", "wo", "wa", "wb", "wc")
LATE_GRADS = ("wm_qkv", "wf", "wk")


def _local_step(x, mem, target, small, wg, rest_shards=None):
    s, d = x.shape
    dist = rest_shards is not None
    wg = dict(wg)
    ones = lambda n: jnp.ones((1, n), F32)
    zeros = lambda n: jnp.zeros((1, n), F32)
    scale_ab = HEAD_DIM ** -0.5
    split8 = lambda g: g.reshape(N_DEV, g.shape[0] // N_DEV, g.shape[1])

    hn = _rmsnorm_fwd(x, small["norm_gain"], "rms_x_fwd")
    if dist:
        proj, gathered = _mm_nn(hn, wg["wm"], bm=1024, bn=1024, bk=d, o_dtype=F32, name="proj_main",
                                comm=_Comm("gather", rest_shards))
        for kname, arr in zip(REST_KEYS, gathered):
            wg[kname] = arr.reshape(arr.shape[0] * arr.shape[1], arr.shape[2])
    else:
        proj = _mm_nn(hn, wg["wm"], bm=1024, bn=1024, bk=d, o_dtype=F32, name="proj_main")
    pfb = _mm_nn(hn, wg["wf"], bm=1024, bn=FB_PAD, bk=d, o_dtype=F32, name="proj_fb")
    mn = _rmsnorm_fwd(mem, small["mem_norm_gain"], "rms_mem_fwd")
    mkv = _mm_nn(mn, wg["wk"], bm=256, bn=1024, bk=d, o_dtype=F32, name="mem_kv")

    gain_a = jnp.concatenate([_row(small["q_gain_a"], A_Q_HEADS) * scale_ab, _row(small["k_gain_a"], A_KV_HEADS), ones(A_KV_WIDTH)], axis=1)
    flag_a = jnp.concatenate([ones(A_WIDTH + A_KV_WIDTH), zeros(A_KV_WIDTH)], axis=1)
    qkv_a = _headnorm_fwd(proj, COL_QA, 1280, 1280, HEAD_DIM, gain_a, flag_a, True, "hn_a_fwd")
    gain_b = jnp.concatenate([_row(small["q_gain_b"], B_HEADS) * scale_ab, _row(small["k_gain_b"], B_HEADS), ones(B_WIDTH)], axis=1)
    flag_b = jnp.concatenate([ones(2 * B_WIDTH), zeros(B_WIDTH)], axis=1)
    qkv_b = _headnorm_fwd(proj, COL_QB, 2304, 256, HEAD_DIM, gain_b, flag_b, True, "hn_b_fwd")
    gain_cq = _row(small["q_gain_c"], C_HEADS)
    q_c = _headnorm_fwd(proj, COL_QC, C_WIDTH, C_WIDTH, C_HEAD_DIM, gain_cq, ones(C_WIDTH), False, "hn_cq_fwd")
    gain_ck = jnp.concatenate([_row(small["k_gain_c"], C_HEADS), ones(C_WIDTH)], axis=1)
    flag_ck = jnp.concatenate([ones(C_WIDTH), zeros(C_WIDTH)], axis=1)
    mkvn = _headnorm_fwd(mkv, 0, 2 * C_WIDTH, 2 * C_WIDTH, C_HEAD_DIM, gain_ck, flag_ck, False, "hn_ck_fwd")

    q_a, k_a, v_a = qkv_a[0:12], qkv_a[12:16], qkv_a[16:20]
    q_b, k_b, v_b = qkv_b[0:12], qkv_b[12:24], qkv_b[24:36]

    bpad = jnp.pad(small["b_forget"].reshape(1, -1), ((0, 0), (0, FB_PAD - B_HEADS)))
    c16 = _fox_prep(pfb, bpad, "fox_prep")
    c3 = c16[0:B_HEADS].reshape(B_HEADS, 1, s)

    sinks = small["sinks_a"].reshape(-1)
    slopes = jnp.exp2(-8.0 * jnp.arange(1, A_Q_HEADS + 1, dtype=F32) / A_Q_HEADS)
    y_a, lse_a = _attn_a_fwd(q_a, k_a, v_a, sinks, slopes, "attn_a_fwd")
    y_b, lse_b = _attn_b_fwd(q_b, k_b, v_b, c3, "attn_b_fwd")
    y_c = _attn_c_fwd(q_c, mkvn, "attn_c_fwd")

    s_a = _gate_fwd(y_a, proj, COL_ZA, 256, "gate_a_fwd")
    s_b = _gate_fwd(y_b, proj, COL_ZB, 256, "gate_b_fwd")
    s_c = _gate_fwd(y_c, proj, COL_ZC, 512, "gate_c_fwd")
    u_a = _mm_branch_fwd(s_a, wg["wa"], "branch_a_fwd")
    u_b = _mm_branch_fwd(s_b, wg["wb"], "branch_b_fwd")
    u_c = _mm_branch_fwd(s_c, wg["wc"], "branch_c_fwd")
    ym = _merge_fwd(proj, u_a, u_b, u_c, "merge_fwd")
    y = _mm_nn(ym, wg["wo"], bm=1024, bn=1024, bk=d, o_dtype=F32, name="out_proj", add=x)
    dy, dyb, lpart = _loss_head(y, target, "loss_head")
    loss = 0.5 / d * jnp.sum(lpart)

    dym = _mm_nt(dyb, wg["wo"], bm=1024, bn=1024, bk=d, o_dtype=F32, name="out_proj_bwd_act")
    g = {"wo": _mm_tn(ym, dyb, bm=512, bn=1024, bk=s, o_dtype=BF16, name="out_proj_bwd_w")}

    dgate = lax.empty((s, 3 * d), BF16)
    dgate, du_a = _merge_bwd(dym, u_a, proj, 0, dgate, "merge_a_bwd")
    dgate, du_b = _merge_bwd(dym, u_b, proj, 1, dgate, "merge_b_bwd")
    dgate, du_c = _merge_bwd(dym, u_c, proj, 2, dgate, "merge_c_bwd")
    g["wm_g"] = _mm_tn(hn, dgate, bm=512, bn=1024, bk=s, o_dtype=BF16, name="proj_gate_bwd_w")
    dhn = _mm_nt(dgate, wg["wm"], bm=1024, bn=1024, bk=2048, o_dtype=F32, name="proj_gate_bwd_act", b_col0=COL_GATE)

    ds_a = _mm_branch_bwd_act(du_a, wg["wa"], A_WIDTH, "branch_a_bwd_act")
    ds_b = _mm_branch_bwd_act(du_b, wg["wb"], B_WIDTH, "branch_b_bwd_act")
    ds_c = _mm_branch_bwd_act(du_c, wg["wc"], C_WIDTH, "branch_c_bwd_act")
    g["wa"] = _mm_branch_bwd_w(s_a, du_a, "branch_a_bwd_w")
    g["wb"] = _mm_branch_bwd_w(s_b, du_b, "branch_b_bwd_w")
    g["wc"] = _mm_branch_bwd_w(s_c, du_c, "branch_c_bwd_w")

    dz = lax.empty((s, W_Z), BF16)
    dz, do_a, dd_a = _gate_bwd(ds_a, y_a, proj, COL_ZA, 256, dz, COL_ZA - COL_ZA, True, "gate_a_bwd")
    dz, do_b, dd_b = _gate_bwd(ds_b, y_b, proj, COL_ZB, 256, dz, COL_ZB - COL_ZA, True, "gate_b_bwd")
    dz, do_c, _ = _gate_bwd(ds_c, y_c, proj, COL_ZC, 512, dz, COL_ZC - COL_ZA, False, "gate_c_bwd")
    g["wm_z"] = _mm_tn(hn, dz, bm=512, bn=1024, bk=s, o_dtype=BF16, name="proj_z_bwd_w")
    dhn = _mm_nt(dz, wg["wm"], bm=1024, bn=1024, bk=2048, o_dtype=F32, name="proj_z_bwd_act", b_col0=COL_ZA, add=dhn)

    dq_a, dkv_a, dsink = _attn_a_bwd(q_a, k_a, v_a, do_a, lse_a, dd_a, sinks, slopes, "attn_a_bwd")
    early = _Comm("scatter", [split8(g[n]) for n in EARLY_GRADS]) if dist else None
    dq_b, dk_b, dv_b, dc3, early_parts = _attn_b_bwd(q_b, k_b, v_b, do_b, lse_b, dd_b, c3, "attn_b_bwd", comm=early)
    dq_c, dmkvn = _attn_c_bwd(q_c, mkvn, do_c, "attn_c_bwd")

    dqkv = lax.empty((s, W_QKV), BF16)
    dqkv, dg_qa = _headnorm_bwd(proj, COL_QA, A_WIDTH, 256, HEAD_DIM, gain_a[:, 0:768], flag_a[:, 0:768], dq_a, dqkv, COL_QA, "hn_qa_bwd")
    dqkv, dg_kva = _headnorm_bwd(proj, COL_KA, 512, 256, HEAD_DIM, gain_a[:, 768:1280], flag_a[:, 768:1280], dkv_a, dqkv, COL_KA, "hn_kva_bwd")
    dqkv, dg_qb = _headnorm_bwd(proj, COL_QB, B_WIDTH, 256, HEAD_DIM, gain_b[:, 0:768], flag_b[:, 0:768], dq_b, dqkv, COL_QB, "hn_qb_bwd")
    dqkv, dg_kb = _headnorm_bwd(proj, COL_KB, B_WIDTH, 256, HEAD_DIM, gain_b[:, 768:1536], flag_b[:, 768:1536], dk_b, dqkv, COL_KB, "hn_kb_bwd")
    dqkv, _ = _headnorm_bwd(proj, COL_VB, B_WIDTH, 256, HEAD_DIM, gain_b[:, 1536:2304], flag_b[:, 1536:2304], dv_b, dqkv, COL_VB, "hn_vb_bwd")
    dqkv, dg_qc = _headnorm_bwd(proj, COL_QC, C_WIDTH, 512, C_HEAD_DIM, gain_cq, ones(C_WIDTH), dq_c, dqkv, COL_QC, "hn_qc_bwd")
    dmkv, dg_kc = _headnorm_bwd(mkv, 0, 2 * C_WIDTH, 2 * C_WIDTH, C_HEAD_DIM, gain_ck, flag_ck, dmkvn, None, 0, "hn_kc_bwd")

    dct = jnp.pad(dc3.reshape(B_HEADS, s), ((0, 16 - B_HEADS), (0, 0)))
    dfb, dbf = _fox_prep_bwd(pfb, bpad, dct, "fox_prep_bwd")

    dmn = _mm_nt(dmkv, wg["wk"], bm=256, bn=1024, bk=1024, o_dtype=F32, name="mem_kv_bwd_act")
    g["wk"] = _mm_tn(mn, dmkv, bm=512, bn=1024, bk=mem.shape[0], o_dtype=BF16, name="mem_kv_bwd_w")
    _, dg_mem = _rmsnorm_bwd(mem, dmn, small["mem_norm_gain"], None, "rms_mem_bwd")

    g["wm_qkv"] = _mm_tn(hn, dqkv, bm=512, bn=1024, bk=s, o_dtype=BF16, name="proj_qkv_bwd_w")
    g["wf"] = _mm_tn(hn, dfb, bm=512, bn=FB_PAD, bk=s, o_dtype=BF16, name="proj_fb_bwd_w")
    dhn = _mm_nt(dfb, wg["wf"], bm=1024, bn=1024, bk=FB_PAD, o_dtype=F32, name="proj_fb_bwd_act", add=dhn)
    late = _Comm("scatter", [split8(g[n]) for n in LATE_GRADS]) if dist else None
    dhn = _mm_nt(dqkv, wg["wm"], bm=1024, bn=1024, bk=2048, o_dtype=F32, name="proj_qkv_bwd_act", b_col0=COL_QA, add=dhn, comm=late)
    if dist:
        dhn, late_parts = dhn
        g = dict(zip(EARLY_GRADS + LATE_GRADS, early_parts + late_parts))
    grad_x, dg_x = _rmsnorm_bwd(x, dhn, small["norm_gain"], dy, "rms_x_bwd")

    fold = lambda part, heads, hd: jnp.sum(jnp.sum(part, axis=0).reshape(heads, hd), axis=0).reshape(1, hd)
    small_grads = {
        "norm_gain": jnp.sum(dg_x, axis=0).reshape(1, d),
        "mem_norm_gain": jnp.sum(dg_mem, axis=0).reshape(1, d),
        "b_forget": dbf[0:B_HEADS, 0].reshape(1, B_HEADS),
        "q_gain_a": fold(dg_qa, A_Q_HEADS, HEAD_DIM) * scale_ab,
        "k_gain_a": fold(dg_kva[:, 0:A_KV_WIDTH], A_KV_HEADS, HEAD_DIM),
        "sinks_a": (jnp.sum(dsink, axis=(1, 2)) * (1.0 / HEAD_DIM)).reshape(1, A_Q_HEADS),
        "q_gain_b": fold(dg_qb, B_HEADS, HEAD_DIM) * scale_ab,
        "k_gain_b": fold(dg_kb, B_HEADS, HEAD_DIM),
        "q_gain_c": fold(dg_qc, C_HEADS, C_HEAD_DIM),
        "k_gain_c": fold(dg_kc[:, 0:C_WIDTH], C_HEADS, C_HEAD_DIM),
    }
    return loss, grad_x, small_grads, g


def _coords():
    return lax.axis_index("x"), lax.axis_index("y"), lax.axis_index("c")


def _all_gather(shards, name):
    n = len(shards)

    def body(*refs):
        ins = refs[0:n]
        outs = refs[n:2 * n]
        send_sems, recv_sems, local_sems = refs[2 * n:2 * n + 3]
        x, y, c = _coords()
        me, sibling = (x, y, c), (x, y, 1 - c)
        chips = [(1 - x, y), (x, 1 - y), (1 - x, 1 - y)]
        idx = lambda p: 4 * p[0] + 2 * p[1] + p[2]

        def copy(a, k, block, to, src=None):
            slot = outs[a].at[idx(block)]
            return pltpu.make_async_remote_copy(
                src_ref=slot if src is None else src, dst_ref=slot,
                send_sem=send_sems.at[a, k], recv_sem=recv_sems.at[a, k], device_id=to, device_id_type=MESH)

        mine = [pltpu.make_async_copy(ins[a], outs[a].at[idx(me)], local_sems.at[a]) for a in range(n)]
        for cp in mine:
            cp.start()
        first = []
        for a in range(n):
            first.append(copy(a, 0, me, sibling, src=ins[a]))
            first += [copy(a, 1 + j, me, (*chip, c), src=ins[a]) for j, chip in enumerate(chips)]
        for cp in first:
            cp.start()
        passed = []
        for j, chip in enumerate(chips):
            for a in range(n):
                copy(a, 1 + j, (*chip, c), me).wait_recv()
                fwd = copy(a, 4 + j, (*chip, c), sibling)
                fwd.start()
                passed.append(fwd)
        for a in range(n):
            copy(a, 0, sibling, me).wait_recv()
            for j, chip in enumerate(chips):
                copy(a, 4 + j, (*chip, 1 - c), me).wait_recv()
        for cp in first + passed:
            cp.wait_send()
        for cp in mine:
            cp.wait()

    any_spec = pl.BlockSpec(memory_space=pl.ANY)
    return pl.pallas_call(
        body, name=name,
        in_specs=[any_spec] * n, out_specs=[any_spec] * n,
        out_shape=[jax.ShapeDtypeStruct((N_DEV,) + sh.shape, sh.dtype) for sh in shards],
        scratch_shapes=[pltpu.SemaphoreType.DMA((n, 7)), pltpu.SemaphoreType.DMA((n, 7)), pltpu.SemaphoreType.DMA((n,))],
    )(*shards)


def _all_reduce_small(vec, name):
    p = vec.shape[1]

    def body(v_ref, o_ref, gather, send_sems, recv_sems):
        x, y, c = _coords()
        my = 4 * x + 2 * y + c
        peers = [(x ^ ((k >> 2) & 1), y ^ ((k >> 1) & 1), c ^ (k & 1)) for k in range(1, N_DEV)]
        gather[my] = v_ref[...]
        sends = [pltpu.make_async_remote_copy(
            src_ref=v_ref, dst_ref=gather.at[my], send_sem=send_sems.at[k], recv_sem=recv_sems.at[k],
            device_id=peer, device_id_type=MESH) for k, peer in enumerate(peers)]
        for cp in sends:
            cp.start()
        for k, peer in enumerate(peers):
            pid = 4 * peer[0] + 2 * peer[1] + peer[2]
            pltpu.make_async_remote_copy(
                src_ref=v_ref, dst_ref=gather.at[pid], send_sem=send_sems.at[k], recv_sem=recv_sems.at[k],
                device_id=peer, device_id_type=MESH).wait_recv()
        for cp in sends:
            cp.wait_send()
        total = gather[0]
        for j in range(1, N_DEV):
            total = total + gather[j]
        o_ref[...] = total

    vm = pl.BlockSpec(memory_space=pltpu.VMEM)
    return pl.pallas_call(
        body, name=name, in_specs=[vm], out_specs=vm,
        out_shape=jax.ShapeDtypeStruct((8, p), F32),
        scratch_shapes=[pltpu.VMEM((N_DEV, 8, p), F32), pltpu.SemaphoreType.DMA((7,)), pltpu.SemaphoreType.DMA((7,))],
    )(vec)[0:1]


def _sum_parts(parts, name):
    _, rows, cols = parts.shape
    br = _tile(rows, 64, 16)

    def body(p_ref, o_ref):
        total = p_ref[0].astype(F32)
        for j in range(1, N_DEV):
            total = total + p_ref[j].astype(F32)
        o_ref[...] = total

    return pl.pallas_call(
        body, name=name, grid=(rows // br,),
        in_specs=[pl.BlockSpec((N_DEV, br, cols), lambda i: (0, i, 0))],
        out_specs=pl.BlockSpec((br, cols), lambda i: (i, 0)),
        out_shape=jax.ShapeDtypeStruct((rows, cols), F32),
        compiler_params=_params(("parallel",), VMEM_BIG),
    )(parts)


def _adamw(w, g, m, v, name):
    rows, cols = w.shape
    br = _tile(rows, 32, 8)
    c1 = 1.0 / (1.0 - ADAM_B1 ** ADAM_STEP)
    c2 = 1.0 / (1.0 - ADAM_B2 ** ADAM_STEP)

    def body(w_ref, g_ref, m_ref, v_ref, d_ref, nm_ref, nv_ref):
        gv = g_ref[...]
        nm = ADAM_B1 * m_ref[...] + (1.0 - ADAM_B1) * gv
        nv = ADAM_B2 * v_ref[...] + (1.0 - ADAM_B2) * (gv * gv)
        d_ref[...] = -ADAM_LR * ((nm * c1) / (jnp.sqrt(nv * c2) + ADAM_EPS) + ADAM_WD * w_ref[...])
        nm_ref[...] = nm
        nv_ref[...] = nv

    spec = pl.BlockSpec((br, cols), lambda i: (i, 0))
    shape = jax.ShapeDtypeStruct((rows, cols), F32)
    return pl.pallas_call(
        body, name=name, grid=(rows // br,), in_specs=[spec] * 4, out_specs=[spec] * 3, out_shape=[shape] * 3,
        compiler_params=_params(("parallel",), VMEM_BIG),
    )(w, g, m, v)


SMALL_NAMES = ("norm_gain", "mem_norm_gain", "b_forget", "q_gain_a", "k_gain_a", "sinks_a",
               "q_gain_b", "k_gain_b", "q_gain_c", "k_gain_c")
BIG_NAMES = ("w_in", "w_mem_kv", "w_branch_a", "w_branch_b", "w_branch_c", "w_out")
WEIGHT_ORDER = ("norm_gain", "mem_norm_gain", "w_in", "b_forget", "q_gain_a", "k_gain_a", "sinks_a", "q_gain_b",
                "k_gain_b", "q_gain_c", "k_gain_c", "w_mem_kv", "w_branch_a", "w_branch_b", "w_branch_c", "w_out")


def _pack_small(tree):
    flat = jnp.concatenate([tree[n].reshape(1, -1) for n in SMALL_NAMES], axis=1)
    pad = (-flat.shape[1]) % LANES
    return jnp.pad(flat, ((0, 0), (0, pad)))


def _unpack_small(flat, like):
    out, off = {}, 0
    for n in SMALL_NAMES:
        size = like[n].size
        out[n] = flat[:, off:off + size].reshape(like[n].shape)
        off += size
    return out


def kernel(x, mem, norm_gain, mem_norm_gain, w_in, b_forget, q_gain_a, k_gain_a, sinks_a, q_gain_b, k_gain_b, q_gain_c, k_gain_c, w_mem_kv, w_branch_a, w_branch_b, w_branch_c, w_out, loss_target, m_norm_gain, m_mem_norm_gain, m_w_in, m_b_forget, m_q_gain_a, m_k_gain_a, m_sinks_a, m_q_gain_b, m_k_gain_b, m_q_gain_c, m_k_gain_c, m_w_mem_kv, m_w_branch_a, m_w_branch_b, m_w_branch_c, m_w_out, v_norm_gain, v_mem_norm_gain, v_w_in, v_b_forget, v_q_gain_a, v_k_gain_a, v_sinks_a, v_q_gain_b, v_k_gain_b, v_q_gain_c, v_k_gain_c, v_w_mem_kv, v_w_branch_a, v_w_branch_b, v_w_branch_c, v_w_out):
    weights = dict(norm_gain=norm_gain, mem_norm_gain=mem_norm_gain, w_in=w_in, b_forget=b_forget, q_gain_a=q_gain_a,
                   k_gain_a=k_gain_a, sinks_a=sinks_a, q_gain_b=q_gain_b, k_gain_b=k_gain_b, q_gain_c=q_gain_c,
                   k_gain_c=k_gain_c, w_mem_kv=w_mem_kv, w_branch_a=w_branch_a, w_branch_b=w_branch_b,
                   w_branch_c=w_branch_c, w_out=w_out)
    mom_m = dict(norm_gain=m_norm_gain, mem_norm_gain=m_mem_norm_gain, w_in=m_w_in, b_forget=m_b_forget,
                 q_gain_a=m_q_gain_a, k_gain_a=m_k_gain_a, sinks_a=m_sinks_a, q_gain_b=m_q_gain_b, k_gain_b=m_k_gain_b,
                 q_gain_c=m_q_gain_c, k_gain_c=m_k_gain_c, w_mem_kv=m_w_mem_kv, w_branch_a=m_w_branch_a,
                 w_branch_b=m_w_branch_b, w_branch_c=m_w_branch_c, w_out=m_w_out)
    mom_v = dict(norm_gain=v_norm_gain, mem_norm_gain=v_mem_norm_gain, w_in=v_w_in, b_forget=v_b_forget,
                 q_gain_a=v_q_gain_a, k_gain_a=v_k_gain_a, sinks_a=v_sinks_a, q_gain_b=v_q_gain_b, k_gain_b=v_k_gain_b,
                 q_gain_c=v_q_gain_c, k_gain_c=v_k_gain_c, w_mem_kv=v_w_mem_kv, w_branch_a=v_w_branch_a,
                 w_branch_b=v_w_branch_b, w_branch_c=v_w_branch_c, w_out=v_w_out)
    wi = w_in[0]
    sh_wm = jnp.concatenate([wi[:, a:b] for a, b in SRC_RANGES] + [wi[:, SRC_GATE:]], axis=1).astype(BF16)
    sh_wf = jnp.pad(wi[:, FB_SRC:FB_SRC + B_HEADS], ((0, 0), (0, FB_PAD - B_HEADS))).astype(BF16)
    rest = [w_mem_kv[0].astype(BF16), w_out[0].astype(BF16),
            w_branch_a[0].astype(BF16), w_branch_b[0].astype(BF16), w_branch_c[0].astype(BF16)]
    full = _all_gather([sh_wm, sh_wf], "weights_all_gather")
    wg = {kname: arr.reshape(arr.shape[0] * arr.shape[1], arr.shape[2]) for kname, arr in zip(("wm", "wf"), full)}

    small = {n: weights[n] for n in SMALL_NAMES}
    loss_local, grad_x, small_g, parts = _local_step(x[0], mem[0], loss_target[0], small, wg, rest)

    summed = {kname: _sum_parts(p, "grad_sum_" + kname) for kname, p in parts.items()}
    gq, gz, gf = summed["wm_qkv"], summed["wm_z"], summed["wf"]
    grads = {
        "w_in": jnp.concatenate([gq[:, COL_QA:COL_QB], gz[:, 0:COL_ZB - COL_ZA], gq[:, COL_QB:COL_QC],
                                 gz[:, COL_ZB - COL_ZA:COL_ZC - COL_ZA], gf[:, 0:B_HEADS], gq[:, COL_QC:W_QKV],
                                 gz[:, COL_ZC - COL_ZA:W_Z], summed["wm_g"]], axis=1),
        "w_mem_kv": summed["wk"], "w_out": summed["wo"],
        "w_branch_a": summed["wa"], "w_branch_b": summed["wb"], "w_branch_c": summed["wc"],
    }

    packed = _pack_small(small_g)
    reduced = _all_reduce_small(jnp.broadcast_to(packed, (8, packed.shape[1])), "small_all_reduce")
    grads.update(_unpack_small(reduced, small))

    loss = lax.psum(loss_local, ("x", "y", "c"))

    delta, new_m, new_v = {}, {}, {}
    for n in BIG_NAMES:
        dlt, nm, nv = _adamw(weights[n][0], grads[n], mom_m[n][0], mom_v[n][0], "adamw_" + n)
        delta[n], new_m[n], new_v[n] = dlt[None], nm[None], nv[None]
    pw, pm, pv = _pack_small(small), _pack_small({n: mom_m[n] for n in SMALL_NAMES}), _pack_small({n: mom_v[n] for n in SMALL_NAMES})
    rep8 = lambda a: jnp.broadcast_to(a, (8, a.shape[1]))
    dlt, nm, nv = _adamw(rep8(pw), rep8(reduced), rep8(pm), rep8(pv), "adamw_small")
    for tree, flat in ((delta, dlt), (new_m, nm), (new_v, nv)):
        tree.update(_unpack_small(flat[0:1], small))
    for n in BIG_NAMES:
        grads[n] = grads[n][None]
    return (loss, grad_x[None], *[grads[n] for n in WEIGHT_ORDER], *[delta[n] for n in WEIGHT_ORDER],
            *[new_m[n] for n in WEIGHT_ORDER], *[new_v[n] for n in WEIGHT_ORDER])
```

```python
import math

import jax
import jax.numpy as jnp
import numpy as np
from jax import lax
from jax.experimental import pallas as pl
from jax.experimental.pallas import tpu as pltpu

F32 = jnp.float32
BF16 = jnp.bfloat16

N_DEV = 8
HEAD_DIM = 64
A_Q_HEADS = 12
A_KV_HEADS = 4
A_GROUP = 3
B_HEADS = 12
C_HEADS = 4
C_HEAD_DIM = 128
WINDOW = 128
A_WIDTH = 768
A_KV_WIDTH = 256
B_WIDTH = 768
C_WIDTH = 512
EPS = 1e-6
NEG = -1e30

COL_QA, COL_KA, COL_VA = 0, 768, 1024
COL_QB, COL_KB, COL_VB = 1280, 2048, 2816
COL_QC = 3584
W_QKV = 4096
COL_ZA, COL_ZB, COL_ZC = 0, 768, 1536
COL_GATE = W_Z = 2048
SRC_RANGES = ((0, 1280), (2048, 4352), (5132, 5644), (1280, 2048), (4352, 5120), (5644, 6156))
SRC_GATE = 6156
FB_SRC = 5120
FB_PAD = 128

ADAM_LR = 0.001
ADAM_B1 = 0.9
ADAM_B2 = 0.999
ADAM_EPS = 1e-08
ADAM_WD = 0.01
ADAM_STEP = 10

VMEM_BIG = 52 * 1024 * 1024
LANES = 128
MESH = pl.DeviceIdType.MESH


def _tile(n, pref, mult=128):
    if n <= pref:
        return n
    t = (pref // mult) * mult
    while t >= mult:
        if n % t == 0:
            return t
        t -= mult
    return n


def _params(sem=None, vmem=None):
    kw = {}
    if sem is not None:
        kw["dimension_semantics"] = sem
    if vmem is not None:
        kw["vmem_limit_bytes"] = vmem
    return pltpu.CompilerParams(**kw)


def _sigmoid(x):
    return 1.0 / (1.0 + jnp.exp(-x))


def _block_diag(hd):
    r = np.arange(LANES)
    return jnp.asarray((r[:, None] // hd) == (r[None, :] // hd), dtype=BF16)


def _seg_sum(t, bd):
    hi = t.astype(BF16)
    lo = (t - hi.astype(F32)).astype(BF16)
    outs = []
    for c in range(t.shape[1] // LANES):
        sl = slice(c * LANES, (c + 1) * LANES)
        outs.append(jnp.dot(hi[:, sl], bd, preferred_element_type=F32) + jnp.dot(lo[:, sl], bd, preferred_element_type=F32))
    return outs[0] if len(outs) == 1 else jnp.concatenate(outs, axis=1)


def _rmsnorm_fwd(x, gain, name):
    rows, d = x.shape
    bm = _tile(rows, 512, 8)

    def body(x_ref, g_ref, o_ref):
        xv = x_ref[...]
        ms = jnp.mean(xv * xv, axis=-1, keepdims=True)
        o_ref[...] = (xv * lax.rsqrt(ms + EPS) * g_ref[...]).astype(BF16)

    return pl.pallas_call(
        body, name=name, grid=(rows // bm,),
        in_specs=[pl.BlockSpec((bm, d), lambda i: (i, 0)), pl.BlockSpec((1, d), lambda i: (0, 0))],
        out_specs=pl.BlockSpec((bm, d), lambda i: (i, 0)),
        out_shape=jax.ShapeDtypeStruct((rows, d), BF16),
        compiler_params=_params(("parallel",)),
    )(x, gain)


def _rmsnorm_bwd(x, dhn, gain, dy, name):
    rows, d = x.shape
    bm = _tile(rows, 256, 8)
    with_dx = dy is not None

    def body(*refs):
        if with_dx:
            x_ref, dh_ref, g_ref, dy_ref, gx_ref, dg_ref = refs
        else:
            x_ref, dh_ref, g_ref, dg_ref = refs
        i = pl.program_id(0)
        xv = x_ref[...]
        rstd = lax.rsqrt(jnp.mean(xv * xv, axis=-1, keepdims=True) + EPS)
        xhat = xv * rstd
        dh = dh_ref[...]
        part = jnp.sum((dh * xhat).reshape(bm // 8, 8, d), axis=0)

        @pl.when(i == 0)
        def _():
            dg_ref[...] = part

        @pl.when(i > 0)
        def _():
            dg_ref[...] += part

        if with_dx:
            g = dh * g_ref[...]
            mean = jnp.mean(g * xhat, axis=-1, keepdims=True)
            gx_ref[...] = dy_ref[...] + rstd * (g - xhat * mean)

    row_spec = pl.BlockSpec((bm, d), lambda i: (i, 0))
    in_specs = [row_spec, row_spec, pl.BlockSpec((1, d), lambda i: (0, 0))]
    args = [x, dhn, gain]
    dg_spec = pl.BlockSpec((8, d), lambda i: (0, 0))
    dg_shape = jax.ShapeDtypeStruct((8, d), F32)
    if with_dx:
        in_specs.append(row_spec)
        args.append(dy)
        out_specs = [row_spec, dg_spec]
        out_shape = [jax.ShapeDtypeStruct((rows, d), F32), dg_shape]
    else:
        out_specs = [dg_spec]
        out_shape = [dg_shape]
    outs = pl.pallas_call(
        body, name=name, grid=(rows // bm,), in_specs=in_specs, out_specs=out_specs, out_shape=out_shape,
        compiler_params=_params(("arbitrary",)),
    )(*args)
    return outs if with_dx else (None, outs[0])


class _Comm:
    def __init__(self, kind, arrays):
        self.kind = kind
        self.arrays = list(arrays)
        self.n = len(self.arrays)

    def out_shapes(self):
        if self.kind == "gather":
            return [jax.ShapeDtypeStruct((N_DEV,) + a.shape, a.dtype) for a in self.arrays]
        return [jax.ShapeDtypeStruct(a.shape, a.dtype) for a in self.arrays]

    def scratch(self):
        return [pltpu.SemaphoreType.DMA((self.n, N_DEV - 1)), pltpu.SemaphoreType.DMA((self.n, N_DEV - 1)),
                pltpu.SemaphoreType.DMA((self.n,))]

    def _plan(self, ins, outs, sems, with_recvs):
        send_sems, recv_sems, local_sems = sems
        x, y, c = lax.axis_index("x"), lax.axis_index("y"), lax.axis_index("c")
        my = 4 * x + 2 * y + c
        gather = self.kind == "gather"
        local, sends, recvs = [], [], []
        for a in range(self.n):
            local.append(pltpu.make_async_copy(ins[a] if gather else ins[a].at[my], outs[a].at[my], local_sems.at[a]))
            for k in range(1, N_DEV):
                peer = (x ^ ((k >> 2) & 1), y ^ ((k >> 1) & 1), c ^ (k & 1))
                pid = 4 * peer[0] + 2 * peer[1] + peer[2]
                src = ins[a] if gather else ins[a].at[pid]
                sem = dict(send_sem=send_sems.at[a, k - 1], recv_sem=recv_sems.at[a, k - 1], device_id=peer, device_id_type=MESH)
                sends.append(pltpu.make_async_remote_copy(src_ref=src, dst_ref=outs[a].at[my], **sem))
                if with_recvs:
                    recvs.append(pltpu.make_async_remote_copy(src_ref=src, dst_ref=outs[a].at[pid], **sem))
        return local, sends, recvs

    def start(self, ins, outs, sems):
        local, sends, _ = self._plan(ins, outs, sems, False)
        for cp in local + sends:
            cp.start()

    def wait(self, ins, outs, sems):
        local, sends, recvs = self._plan(ins, outs, sems, True)
        for cp in recvs:
            cp.wait_recv()
        for cp in sends:
            cp.wait_send()
        for cp in local:
            cp.wait()


def _grid_edges(grid):
    first = last = None
    for ax, size in enumerate(grid):
        pid = pl.program_id(ax)
        f, l = pid == 0, pid == size - 1
        first = f if first is None else first & f
        last = l if last is None else last & l
    return first, last


def _hosted_call(body, comm, *, name, grid, in_specs, out_specs, out_shape, scratch_shapes, args, sem, vmem=None):
    in_specs, out_specs, out_shape, scratch_shapes = list(in_specs), list(out_specs), list(out_shape), list(scratch_shapes)
    if comm is None:
        res = pl.pallas_call(body, name=name, grid=grid, in_specs=in_specs, out_specs=out_specs, out_shape=out_shape,
                             scratch_shapes=scratch_shapes, compiler_params=_params(sem, vmem))(*args)
        return list(res), []
    n_in, n_out, n_scr, nc = len(in_specs), len(out_shape), len(scratch_shapes), comm.n

    def hosted(*refs):
        ins = refs[0:n_in]
        comm_in = refs[n_in:n_in + nc]
        outs = refs[n_in + nc:n_in + nc + n_out]
        comm_out = refs[n_in + nc + n_out:n_in + 2 * nc + n_out]
        scr = refs[n_in + 2 * nc + n_out:n_in + 2 * nc + n_out + n_scr]
        sems = refs[n_in + 2 * nc + n_out + n_scr:]
        first, last = _grid_edges(grid)

        @pl.when(first)
        def _():
            comm.start(comm_in, comm_out, sems)

        body(*ins, *outs, *scr)

        @pl.when(last)
        def _():
            comm.wait(comm_in, comm_out, sems)

    any_spec = pl.BlockSpec(memory_space=pl.ANY)
    res = pl.pallas_call(
        hosted, name=name, grid=grid, in_specs=in_specs + [any_spec] * nc, out_specs=out_specs + [any_spec] * nc,
        out_shape=out_shape + comm.out_shapes(), scratch_shapes=scratch_shapes + comm.scratch(),
        compiler_params=_params(("arbitrary",) * len(grid), vmem),
    )(*args, *comm.arrays)
    return list(res[0:n_out]), list(res[n_out:])


def _mm(a, b, *, grid, a_spec, b_spec, o_spec, o_shape, o_dtype, contract, name, add=None, add_spec=None, acc_shape=None,
        comm=None):
    nk = grid[2]
    has_add = add is not None

    def body(*refs):
        a_ref, b_ref = refs[0], refs[1]
        add_ref = refs[2] if has_add else None
        o_ref = refs[3] if has_add else refs[2]
        part = lax.dot_general(a_ref[...], b_ref[...], (contract, ((), ())), preferred_element_type=F32)
        if nk == 1:
            if has_add:
                part = part + add_ref[...]
            o_ref[...] = part.astype(o_dtype)
        else:
            acc = refs[-1]
            k = pl.program_id(2)

            @pl.when(k == 0)
            def _():
                acc[...] = part

            @pl.when(k > 0)
            def _():
                acc[...] += part

            @pl.when(k == nk - 1)
            def _():
                r = acc[...]
                if has_add:
                    r = r + add_ref[...]
                o_ref[...] = r.astype(o_dtype)

    in_specs = [a_spec, b_spec] + ([add_spec] if has_add else [])
    args = [a, b] + ([add] if has_add else [])
    scratch = [pltpu.VMEM(acc_shape, F32)] if nk > 1 else []
    outs, comm_outs = _hosted_call(
        body, comm, name=name, grid=grid, in_specs=in_specs, out_specs=[o_spec],
        out_shape=[jax.ShapeDtypeStruct(o_shape, o_dtype)], scratch_shapes=scratch, args=args,
        sem=("parallel", "parallel", "arbitrary"), vmem=VMEM_BIG)
    return outs[0] if comm is None else (outs[0], comm_outs)


def _mm_nn(a, b, *, bm, bn, bk, o_dtype, name, add=None, comm=None):
    m, kd = a.shape
    n = b.shape[1]
    bm, bn, bk = _tile(m, bm, 8), _tile(n, bn), _tile(kd, bk)
    o_spec = pl.BlockSpec((bm, bn), lambda i, j, k: (i, j))
    return _mm(a, b, grid=(m // bm, n // bn, kd // bk),
               a_spec=pl.BlockSpec((bm, bk), lambda i, j, k: (i, k)),
               b_spec=pl.BlockSpec((bk, bn), lambda i, j, k: (k, j)),
               o_spec=o_spec, o_shape=(m, n), o_dtype=o_dtype, contract=((1,), (0,)), name=name,
               add=add, add_spec=o_spec, acc_shape=(bm, bn), comm=comm)


def _mm_nt(a, b, *, bm, bn, bk, o_dtype, name, add=None, b_col0=0, comm=None):
    m, kd = a.shape
    n = b.shape[0]
    bm, bn, bk = _tile(m, bm, 8), _tile(n, bn), _tile(math.gcd(kd, b_col0), bk)
    kb0 = b_col0 // bk
    o_spec = pl.BlockSpec((bm, bn), lambda i, j, k: (i, j))
    return _mm(a, b, grid=(m // bm, n // bn, kd // bk),
               a_spec=pl.BlockSpec((bm, bk), lambda i, j, k: (i, k)),
               b_spec=pl.BlockSpec((bn, bk), lambda i, j, k: (j, kb0 + k)),
               o_spec=o_spec, o_shape=(m, n), o_dtype=o_dtype, contract=((1,), (1,)), name=name,
               add=add, add_spec=o_spec, acc_shape=(bm, bn), comm=comm)


def _mm_tn(a, b, *, bm, bn, bk, o_dtype, name, comm=None):
    kd, m = a.shape
    n = b.shape[1]
    bm, bn, bk = _tile(m, bm), _tile(n, bn), _tile(kd, bk, 8)
    return _mm(a, b, grid=(m // bm, n // bn, kd // bk),
               a_spec=pl.BlockSpec((bk, bm), lambda i, j, k: (k, i)),
               b_spec=pl.BlockSpec((bk, bn), lambda i, j, k: (k, j)),
               o_spec=pl.BlockSpec((bm, bn), lambda i, j, k: (i, j)),
               o_shape=(m, n), o_dtype=o_dtype, contract=((0,), (0,)), name=name, acc_shape=(bm, bn), comm=comm)


def _mm_branch_fwd(s, w2d, name):
    m, kb = s.shape
    ds = w2d.shape[1]
    bm = _tile(m, 1024, 8)
    return _mm(s, w2d, grid=(m // bm, N_DEV, 1),
               a_spec=pl.BlockSpec((bm, kb), lambda i, j, k: (i, 0)),
               b_spec=pl.BlockSpec((kb, ds), lambda i, j, k: (j, 0)),
               o_spec=pl.BlockSpec((bm, ds), lambda i, j, k: (i, j)),
               o_shape=(m, N_DEV * ds), o_dtype=F32, contract=((1,), (0,)), name=name)


def _mm_branch_bwd_act(du, w2d, kb, name):
    m = du.shape[0]
    ds = w2d.shape[1]
    bm = _tile(m, 1024, 8)
    return _mm(du, w2d, grid=(m // bm, 1, N_DEV),
               a_spec=pl.BlockSpec((bm, ds), lambda i, j, k: (i, k)),
               b_spec=pl.BlockSpec((kb, ds), lambda i, j, k: (k, 0)),
               o_spec=pl.BlockSpec((bm, kb), lambda i, j, k: (i, 0)),
               o_shape=(m, kb), o_dtype=F32, contract=((1,), (1,)), name=name, acc_shape=(bm, kb))


def _mm_branch_bwd_w(s, du, name):
    m, kb = s.shape
    ds = du.shape[1] // N_DEV
    bk = _tile(m, 2048, 8)
    return _mm(s, du, grid=(1, N_DEV, m // bk),
               a_spec=pl.BlockSpec((bk, kb), lambda i, j, k: (k, 0)),
               b_spec=pl.BlockSpec((bk, ds), lambda i, j, k: (k, j)),
               o_spec=pl.BlockSpec((kb, ds), lambda i, j, k: (j, 0)),
               o_shape=(N_DEV * kb, ds), o_dtype=BF16, contract=((0,), (0,)), name=name, acc_shape=(kb, ds))


def _headnorm_fwd(src, c0, width, bw, hd, gain, nflag, head_major, name):
    rows = src.shape[0]
    bm = _tile(rows, 1024, 8)
    bd = _block_diag(hd)
    cb0 = c0 // bw

    def body(x_ref, g_ref, f_ref, bd_ref, o_ref):
        xv = x_ref[...]
        ss = _seg_sum(xv * xv, bd_ref[...])
        rstd = lax.rsqrt(ss * (1.0 / hd) + EPS)
        y = (xv * jnp.where(f_ref[...] > 0.0, rstd, 1.0) * g_ref[...]).astype(BF16)
        if head_major:
            for h in range(bw // HEAD_DIM):
                o_ref[h] = y[:, h * HEAD_DIM:(h + 1) * HEAD_DIM]
        else:
            o_ref[...] = y

    vec_spec = pl.BlockSpec((1, bw), lambda i, t: (0, t))
    if head_major:
        hpb = bw // HEAD_DIM
        out_spec = pl.BlockSpec((hpb, bm, HEAD_DIM), lambda i, t: (t, i, 0))
        out_shape = jax.ShapeDtypeStruct((width // HEAD_DIM, rows, HEAD_DIM), BF16)
    else:
        out_spec = pl.BlockSpec((bm, bw), lambda i, t: (i, t))
        out_shape = jax.ShapeDtypeStruct((rows, width), BF16)
    return pl.pallas_call(
        body, name=name, grid=(rows // bm, width // bw),
        in_specs=[pl.BlockSpec((bm, bw), lambda i, t: (i, cb0 + t)), vec_spec, vec_spec,
                  pl.BlockSpec((LANES, LANES), lambda i, t: (0, 0))],
        out_specs=out_spec, out_shape=out_shape,
        compiler_params=_params(("parallel", "parallel")),
    )(src, gain, nflag, bd)


def _headnorm_bwd(src, c0, width, bw, hd, gain, nflag, dyn, target, t0, name):
    rows = src.shape[0]
    bm = _tile(rows, 1024, 8)
    bd = _block_diag(hd)
    cb0 = c0 // bw
    tb0 = t0 // bw
    aliased = target is not None

    def body(*refs):
        if aliased:
            x_ref, dy_ref, g_ref, f_ref, bd_ref, _, o_ref, dg_ref = refs
        else:
            x_ref, dy_ref, g_ref, f_ref, bd_ref, o_ref, dg_ref = refs
        i = pl.program_id(1)
        xv = x_ref[...]
        dyv = dy_ref[...]
        bdv = bd_ref[...]
        rstd = lax.rsqrt(_seg_sum(xv * xv, bdv) * (1.0 / hd) + EPS)
        xhat = xv * rstd
        g = dyv * g_ref[...]
        mean = _seg_sum(g * xhat, bdv) * (1.0 / hd)
        dx = jnp.where(f_ref[...] > 0.0, rstd * (g - xhat * mean), g)
        o_ref[...] = dx.astype(BF16)
        part = jnp.sum((dyv * xhat).reshape(bm // 8, 8, bw), axis=0)

        @pl.when(i == 0)
        def _():
            dg_ref[...] = part

        @pl.when(i > 0)
        def _():
            dg_ref[...] += part

    vec_spec = pl.BlockSpec((1, bw), lambda t, i: (0, t))
    in_specs = [pl.BlockSpec((bm, bw), lambda t, i: (i, cb0 + t)), pl.BlockSpec((bm, bw), lambda t, i: (i, t)),
                vec_spec, vec_spec, pl.BlockSpec((LANES, LANES), lambda t, i: (0, 0))]
    args = [src, dyn, gain, nflag, bd]
    aliases = {}
    if aliased:
        in_specs.append(pl.BlockSpec(memory_space=pl.ANY))
        args.append(target)
        aliases = {5: 0}
        o_shape = jax.ShapeDtypeStruct(target.shape, BF16)
    else:
        o_shape = jax.ShapeDtypeStruct((rows, width), BF16)
    out, dg = pl.pallas_call(
        body, name=name, grid=(width // bw, rows // bm), in_specs=in_specs,
        out_specs=[pl.BlockSpec((bm, bw), lambda t, i: (i, tb0 + t)), pl.BlockSpec((8, bw), lambda t, i: (0, t))],
        out_shape=[o_shape, jax.ShapeDtypeStruct((8, width), F32)],
        input_output_aliases=aliases,
        compiler_params=_params(("parallel", "arbitrary")),
    )(*args)
    return out, dg


def _fox_prep(pfb, bpad, name):
    s = pfb.shape[0]

    def body(p_ref, b_ref, c_ref):
        z = p_ref[...] + b_ref[...]
        logf = jnp.minimum(z, 0.0) - jnp.log(1.0 + jnp.exp(-jnp.abs(z)))
        x = logf.T[0:16, :]
        lane = lax.broadcasted_iota(jnp.int32, (16, s), 1)
        sh = 1
        while sh < s:
            x = x + jnp.where(lane >= sh, pltpu.roll(x, sh, 1), 0.0)
            sh *= 2
        c_ref[...] = x

    return pl.pallas_call(
        body, name=name, grid=(1,),
        in_specs=[pl.BlockSpec((s, FB_PAD), lambda i: (0, 0)), pl.BlockSpec((1, FB_PAD), lambda i: (0, 0))],
        out_specs=pl.BlockSpec((16, s), lambda i: (0, 0)),
        out_shape=jax.ShapeDtypeStruct((16, s), F32),
        compiler_params=_params(("arbitrary",)),
    )(pfb, bpad)


def _fox_prep_bwd(pfb, bpad, dct, name):
    s = pfb.shape[0]

    def body(p_ref, b_ref, dc_ref, df_ref, db_ref):
        zt = (p_ref[...] + b_ref[...]).T[0:16, :]
        y = dc_ref[...]
        lane = lax.broadcasted_iota(jnp.int32, (16, s), 1)
        sh = 1
        while sh < s:
            y = y + jnp.where(lane < s - sh, pltpu.roll(y, s - sh, 1), 0.0)
            sh *= 2
        dz = y * _sigmoid(-zt)
        db_ref[...] = jnp.broadcast_to(jnp.sum(dz, axis=1, keepdims=True), (16, FB_PAD))
        full = jnp.concatenate([dz, jnp.zeros((FB_PAD - 16, s), F32)], axis=0)
        df_ref[...] = full.T.astype(BF16)

    return pl.pallas_call(
        body, name=name, grid=(1,),
        in_specs=[pl.BlockSpec((s, FB_PAD), lambda i: (0, 0)), pl.BlockSpec((1, FB_PAD), lambda i: (0, 0)),
                  pl.BlockSpec((16, s), lambda i: (0, 0))],
        out_specs=[pl.BlockSpec((s, FB_PAD), lambda i: (0, 0)), pl.BlockSpec((16, FB_PAD), lambda i: (0, 0))],
        out_shape=[jax.ShapeDtypeStruct((s, FB_PAD), BF16), jax.ShapeDtypeStruct((16, FB_PAD), F32)],
        compiler_params=_params(("arbitrary",)),
    )(pfb, bpad, dct)


def _swa_window(n):
    ws = pl.multiple_of(jnp.maximum(n * WINDOW - WINDOW, 0), WINDOW)
    qi = lax.broadcasted_iota(jnp.int32, (WINDOW, 2 * WINDOW), 0)
    kj = lax.broadcasted_iota(jnp.int32, (WINDOW, 2 * WINDOW), 1)
    rel = qi + (n * WINDOW - ws) - kj
    valid = (rel >= 0) & (rel < WINDOW)
    return ws, valid, rel.astype(F32)


def _attn_a_fwd(q, k, v, sinks, slopes, name):
    s = q.shape[1]
    nb = s // WINDOW
    smem = pl.BlockSpec(memory_space=pltpu.SMEM)

    def body(sink_ref, slope_ref, q_ref, k_ref, v_ref, o_ref, lse_ref):
        n = pl.program_id(0)
        ws, valid, relf = _swa_window(n)
        outs = []
        for h in range(A_Q_HEADS):
            kvh = h // A_GROUP
            kw = k_ref[kvh, pl.ds(ws, 2 * WINDOW), :]
            vw = v_ref[kvh, pl.ds(ws, 2 * WINDOW), :]
            sc = lax.dot_general(q_ref[h], kw, (((1,), (1,)), ((), ())), preferred_element_type=F32)
            sc = jnp.where(valid, sc - slope_ref[h] * relf, NEG)
            sink = sink_ref[h]
            m = jnp.maximum(jnp.max(sc, axis=1, keepdims=True), sink)
            p = jnp.exp(sc - m)
            denom = jnp.sum(p, axis=1, keepdims=True) + jnp.exp(sink - m)
            pn = (p / denom).astype(BF16)
            outs.append(jnp.dot(pn, vw, preferred_element_type=F32))
            lse_ref[h] = jnp.broadcast_to(m + jnp.log(denom), (WINDOW, HEAD_DIM))
        o_ref[...] = jnp.concatenate(outs, axis=1)

    return pl.pallas_call(
        body, name=name, grid=(nb,),
        in_specs=[smem, smem,
                  pl.BlockSpec((A_Q_HEADS, WINDOW, HEAD_DIM), lambda n: (0, n, 0)),
                  pl.BlockSpec((A_KV_HEADS, s, HEAD_DIM), lambda n: (0, 0, 0)),
                  pl.BlockSpec((A_KV_HEADS, s, HEAD_DIM), lambda n: (0, 0, 0))],
        out_specs=[pl.BlockSpec((WINDOW, A_WIDTH), lambda n: (n, 0)),
                   pl.BlockSpec((A_Q_HEADS, WINDOW, HEAD_DIM), lambda n: (0, n, 0))],
        out_shape=[jax.ShapeDtypeStruct((s, A_WIDTH), F32), jax.ShapeDtypeStruct((A_Q_HEADS, s, HEAD_DIM), F32)],
        compiler_params=_params(("parallel",), VMEM_BIG),
    )(sinks, slopes, q, k, v)


def _attn_a_bwd(q, k, v, do, lse, dd, sinks, slopes, name, comm=None):
    s = q.shape[1]
    nb = s // WINDOW
    smem = pl.BlockSpec(memory_space=pltpu.SMEM)
    last = nb - 1

    def body(sink_ref, slope_ref, q_ref, k_ref, v_ref, do_ref, lse_ref, dd_ref, dq_ref, dkv_ref, ds_ref, carry):
        n = pl.program_id(0)

        @pl.when(n == 0)
        def _():
            carry[...] = jnp.zeros(carry.shape, F32)
            ds_ref[...] = jnp.zeros(ds_ref.shape, F32)

        @pl.when(n < nb)
        def _():
            ws, valid, relf = _swa_window(n)
            dqs = []
            dkw = [None] * A_KV_HEADS
            dvw = [None] * A_KV_HEADS
            for h in range(A_Q_HEADS):
                kvh = h // A_GROUP
                qh = q_ref[h]
                doh = do_ref[h]
                kw = k_ref[kvh, pl.ds(ws, 2 * WINDOW), :]
                vw = v_ref[kvh, pl.ds(ws, 2 * WINDOW), :]
                lse_h = lse_ref[h]
                dd_h = dd_ref[h]
                sc = lax.dot_general(qh, kw, (((1,), (1,)), ((), ())), preferred_element_type=F32)
                sc = jnp.where(valid, sc - slope_ref[h] * relf, NEG)
                p = jnp.exp(sc - lse_h[:, 0:1])
                dp = lax.dot_general(doh, vw, (((1,), (1,)), ((), ())), preferred_element_type=F32)
                dsc = (p * (dp - dd_h[:, 0:1])).astype(BF16)
                pb = p.astype(BF16)
                dqs.append(jnp.dot(dsc, kw, preferred_element_type=F32))
                dk_h = lax.dot_general(dsc, qh, (((0,), (0,)), ((), ())), preferred_element_type=F32)
                dv_h = lax.dot_general(pb, doh, (((0,), (0,)), ((), ())), preferred_element_type=F32)
                dkw[kvh] = dk_h if dkw[kvh] is None else dkw[kvh] + dk_h
                dvw[kvh] = dv_h if dvw[kvh] is None else dvw[kvh] + dv_h
                psink = jnp.exp(sink_ref[h] - lse_h)
                ds_ref[h] += jnp.sum((-psink * dd_h).reshape(WINDOW // 8, 8, HEAD_DIM), axis=0)
            dq_ref[...] = jnp.concatenate(dqs, axis=1)
            win = jnp.concatenate(dkw + dvw, axis=1)
            first = win[0:WINDOW]
            second = win[WINDOW:2 * WINDOW]
            dkv_ref[...] = carry[...] + first
            carry[...] = jnp.where(n == 0, first, second)

        @pl.when(n == nb)
        def _():
            dkv_ref[...] = carry[...]

    hm = lambda heads: pl.BlockSpec((heads, WINDOW, HEAD_DIM), lambda n: (0, jnp.minimum(n, last), 0))
    res = lambda heads: pl.BlockSpec((heads, s, HEAD_DIM), lambda n: (0, 0, 0))
    outs, comm_outs = _hosted_call(
        body, comm, name=name, grid=(nb + 1,),
        in_specs=[smem, smem, hm(A_Q_HEADS), res(A_KV_HEADS), res(A_KV_HEADS), hm(A_Q_HEADS), hm(A_Q_HEADS), hm(A_Q_HEADS)],
        out_specs=[pl.BlockSpec((WINDOW, A_WIDTH), lambda n: (jnp.minimum(n, last), 0)),
                   pl.BlockSpec((WINDOW, 2 * A_KV_WIDTH), lambda n: (jnp.maximum(n - 1, 0), 0)),
                   pl.BlockSpec((A_Q_HEADS, 8, HEAD_DIM), lambda n: (0, 0, 0))],
        out_shape=[jax.ShapeDtypeStruct((s, A_WIDTH), F32), jax.ShapeDtypeStruct((s, 2 * A_KV_WIDTH), F32),
                   jax.ShapeDtypeStruct((A_Q_HEADS, 8, HEAD_DIM), F32)],
        scratch_shapes=[pltpu.VMEM((WINDOW, 2 * A_KV_WIDTH), F32)],
        args=[sinks, slopes, q, k, v, do, lse, dd], sem=("arbitrary",), vmem=VMEM_BIG)
    return outs[0], outs[1], outs[2], comm_outs


def _attn_b_fwd(q, k, v, c3, name, comm=None):
    heads, s, _ = q.shape
    bq = min(512, s)
    nq = s // bq
    nt = (((1,), (1,)), ((), ()))

    def body(q_ref, k_ref, v_ref, c_ref, o_ref, lse_ref, m_scr, l_scr, acc_scr):
        i = pl.program_id(1)
        r0 = pl.multiple_of(i * bq, bq)
        row = lax.broadcasted_iota(jnp.int32, (bq, bq), 0)
        col = lax.broadcasted_iota(jnp.int32, (bq, bq), 1)
        m_scr[...] = jnp.full((2, bq, LANES), NEG, F32)
        l_scr[...] = jnp.zeros((2, bq, LANES), F32)
        acc_scr[...] = jnp.zeros((2, bq, HEAD_DIM), F32)

        def step(j, masked):
            k0 = pl.multiple_of(j * bq, bq)
            for h2 in range(2):
                kv = k_ref[h2, pl.ds(k0, bq), :]
                vv = v_ref[h2, pl.ds(k0, bq), :]
                cq0 = c_ref[h2, :, pl.ds(r0, LANES)][:, 0:1]
                sc = lax.dot_general(q_ref[h2], kv, nt, preferred_element_type=F32)
                sc = sc + (cq0 - c_ref[h2, :, pl.ds(k0, bq)])
                if masked:
                    sc = jnp.where(col <= row, sc, NEG)
                m_prev = m_scr[h2]
                m_new = jnp.maximum(m_prev, jnp.max(sc, axis=1, keepdims=True))
                alpha = jnp.exp(m_prev - m_new)
                p = jnp.exp(sc - m_new[:, 0:1])
                l_scr[h2] = alpha * l_scr[h2] + jnp.sum(p, axis=1, keepdims=True)
                p_hi = p.astype(BF16)
                p_lo = (p - p_hi.astype(F32)).astype(BF16)
                pv = jnp.dot(p_hi, vv, preferred_element_type=F32) + jnp.dot(p_lo, vv, preferred_element_type=F32)
                acc_scr[h2] = acc_scr[h2] * alpha[:, 0:HEAD_DIM] + pv
                m_scr[h2] = m_new

        def loop_body(j, carry):
            step(j, False)
            return carry

        lax.fori_loop(0, i, loop_body, 0)
        step(i, True)
        outs = []
        for h2 in range(2):
            l = l_scr[h2]
            outs.append(acc_scr[h2] / l[:, 0:HEAD_DIM])
            lse_ref[h2] = (m_scr[h2] + jnp.log(l))[:, 0:HEAD_DIM]
        o_ref[...] = jnp.concatenate(outs, axis=1)

    res = pl.BlockSpec((2, s, HEAD_DIM), lambda hp, i: (hp, 0, 0))
    outs, comm_outs = _hosted_call(
        body, comm, name=name, grid=(heads // 2, nq),
        in_specs=[pl.BlockSpec((2, bq, HEAD_DIM), lambda hp, i: (hp, i, 0)), res, res,
                  pl.BlockSpec((2, 1, s), lambda hp, i: (hp, 0, 0))],
        out_specs=[pl.BlockSpec((bq, 2 * HEAD_DIM), lambda hp, i: (i, hp)),
                   pl.BlockSpec((2, bq, HEAD_DIM), lambda hp, i: (hp, i, 0))],
        out_shape=[jax.ShapeDtypeStruct((s, heads * HEAD_DIM), F32), jax.ShapeDtypeStruct((heads, s, HEAD_DIM), F32)],
        scratch_shapes=[pltpu.VMEM((2, bq, LANES), F32), pltpu.VMEM((2, bq, LANES), F32), pltpu.VMEM((2, bq, HEAD_DIM), F32)],
        args=[q, k, v, c3], sem=("parallel", "parallel"), vmem=VMEM_BIG)
    return outs[0], outs[1], comm_outs


def _attn_b_bwd(q, k, v, do, lse, dd, c3, name, comm=None):
    heads, s, _ = q.shape
    bq = min(512, s)
    nq = s // bq
    nt = (((1,), (1,)), ((), ()))
    tn = (((0,), (0,)), ((), ()))
    grid = (heads // 2, nq)

    def body(q_ref, k_ref, v_ref, do_ref, lse_ref, dd_ref, c_ref, dq_ref, dk_ref, dv_ref, dc_ref,
             dq_scr, dk_scr, dv_scr, dc_scr):
        j = pl.program_id(1)
        k0 = pl.multiple_of(j * bq, bq)
        row = lax.broadcasted_iota(jnp.int32, (bq, bq), 0)
        col = lax.broadcasted_iota(jnp.int32, (bq, bq), 1)

        @pl.when(j == 0)
        def _():
            dq_scr[...] = jnp.zeros(dq_scr.shape, F32)

        dk_scr[...] = jnp.zeros((2, bq, HEAD_DIM), F32)
        dv_scr[...] = jnp.zeros((2, bq, HEAD_DIM), F32)
        dc_scr[...] = jnp.zeros((2, 1, bq), F32)

        def step(i, masked):
            r0 = pl.multiple_of(i * bq, bq)
            for h2 in range(2):
                kv = k_ref[h2]
                vv = v_ref[h2]
                qv = q_ref[h2, pl.ds(r0, bq), :]
                dov = do_ref[h2, pl.ds(r0, bq), :]
                lse_v = lse_ref[h2, pl.ds(r0, bq), :][:, 0:1]
                dd_v = dd_ref[h2, pl.ds(r0, bq), :][:, 0:1]
                cq0 = c_ref[h2, :, pl.ds(r0, LANES)][:, 0:1]
                sc = lax.dot_general(qv, kv, nt, preferred_element_type=F32) + (cq0 - c_ref[h2, :, pl.ds(k0, bq)])
                if masked:
                    sc = jnp.where(col <= row, sc, NEG)
                p = jnp.exp(sc - lse_v)
                dp = lax.dot_general(dov, vv, nt, preferred_element_type=F32)
                dsc = p * (dp - dd_v)
                dsb = dsc.astype(BF16)
                dv_scr[h2] += lax.dot_general(p.astype(BF16), dov, tn, preferred_element_type=F32)
                dk_scr[h2] += lax.dot_general(dsb, qv, tn, preferred_element_type=F32)
                dq_scr[h2, pl.ds(r0, bq), :] += jnp.dot(dsb, kv, preferred_element_type=F32)
                dc_scr[h2] -= jnp.sum(dsc, axis=0, keepdims=True)

        def loop_body(i, carry):
            step(i, False)
            return carry

        step(j, True)
        lax.fori_loop(j + 1, nq, loop_body, 0)
        dc_ref[...] = dc_scr[...]
        dk_ref[...] = jnp.concatenate([dk_scr[0], dk_scr[1]], axis=1)
        dv_ref[...] = jnp.concatenate([dv_scr[0], dv_scr[1]], axis=1)

        @pl.when(j == nq - 1)
        def _():
            dq_ref[...] = jnp.concatenate([dq_scr[0], dq_scr[1]], axis=1)

    res = pl.BlockSpec((2, s, HEAD_DIM), lambda hp, j: (hp, 0, 0))
    blk = pl.BlockSpec((2, bq, HEAD_DIM), lambda hp, j: (hp, j, 0))
    tm = jax.ShapeDtypeStruct((s, heads * HEAD_DIM), F32)
    in_specs = [res, blk, blk, res, res, res, pl.BlockSpec((2, 1, s), lambda hp, j: (hp, 0, 0))]
    out_specs = [pl.BlockSpec((s, 2 * HEAD_DIM), lambda hp, j: (0, hp)),
                 pl.BlockSpec((bq, 2 * HEAD_DIM), lambda hp, j: (j, hp)),
                 pl.BlockSpec((bq, 2 * HEAD_DIM), lambda hp, j: (j, hp)),
                 pl.BlockSpec((2, 1, bq), lambda hp, j: (hp, 0, j))]
    out_shape = [tm, tm, tm, jax.ShapeDtypeStruct((heads, 1, s), F32)]
    scratch = [pltpu.VMEM((2, s, HEAD_DIM), F32), pltpu.VMEM((2, bq, HEAD_DIM), F32),
               pltpu.VMEM((2, bq, HEAD_DIM), F32), pltpu.VMEM((2, 1, bq), F32)]
    outs, comm_outs = _hosted_call(
        body, comm, name=name, grid=grid, in_specs=in_specs, out_specs=out_specs, out_shape=out_shape,
        scratch_shapes=scratch, args=[q, k, v, do, lse, dd, c3], sem=("parallel", "arbitrary"), vmem=VMEM_BIG)
    return outs[0], outs[1], outs[2], outs[3], comm_outs


def _attn_c_probs(qh, mkh):
    sc = lax.dot_general(qh, mkh, (((1,), (1,)), ((), ())), preferred_element_type=F32) * (C_HEAD_DIM ** -0.5)
    p = jnp.exp(sc - jnp.max(sc, axis=1, keepdims=True))
    return p / jnp.sum(p, axis=1, keepdims=True)


def _attn_c_fwd(q, mkv, name):
    s = q.shape[0]
    m = mkv.shape[0]
    bq = _tile(s, 512, 8)

    def body(q_ref, mk_ref, mv_ref, o_ref):
        outs = []
        for h in range(C_HEADS):
            sl = slice(h * C_HEAD_DIM, (h + 1) * C_HEAD_DIM)
            pn = _attn_c_probs(q_ref[:, sl], mk_ref[:, sl]).astype(BF16)
            outs.append(jnp.dot(pn, mv_ref[:, sl], preferred_element_type=F32))
        o_ref[...] = jnp.concatenate(outs, axis=1)

    return pl.pallas_call(
        body, name=name, grid=(s // bq,),
        in_specs=[pl.BlockSpec((bq, C_WIDTH), lambda i: (i, 0)), pl.BlockSpec((m, C_WIDTH), lambda i: (0, 0)),
                  pl.BlockSpec((m, C_WIDTH), lambda i: (0, 1))],
        out_specs=pl.BlockSpec((bq, C_WIDTH), lambda i: (i, 0)),
        out_shape=jax.ShapeDtypeStruct((s, C_WIDTH), F32),
        compiler_params=_params(("parallel",)),
    )(q, mkv, mkv)


def _attn_c_bwd(q, mkv, do, name):
    s = q.shape[0]
    m = mkv.shape[0]
    bq = _tile(s, 512, 8)
    tn = (((0,), (0,)), ((), ()))

    def body(q_ref, mk_ref, mv_ref, do_ref, dq_ref, dm_ref):
        i = pl.program_id(0)

        @pl.when(i == 0)
        def _():
            dm_ref[...] = jnp.zeros(dm_ref.shape, F32)

        dqs = []
        for h in range(C_HEADS):
            sl = slice(h * C_HEAD_DIM, (h + 1) * C_HEAD_DIM)
            qh, mkh, mvh, doh = q_ref[:, sl], mk_ref[:, sl], mv_ref[:, sl], do_ref[:, sl]
            pn = _attn_c_probs(qh, mkh)
            dp = lax.dot_general(doh, mvh, (((1,), (1,)), ((), ())), preferred_element_type=F32)
            dsc = (pn * (dp - jnp.sum(pn * dp, axis=1, keepdims=True)) * (C_HEAD_DIM ** -0.5)).astype(BF16)
            dqs.append(jnp.dot(dsc, mkh, preferred_element_type=F32))
            dm_ref[:, sl] += lax.dot_general(dsc, qh, tn, preferred_element_type=F32)
            sv = slice(C_WIDTH + h * C_HEAD_DIM, C_WIDTH + (h + 1) * C_HEAD_DIM)
            dm_ref[:, sv] += lax.dot_general(pn.astype(BF16), doh, tn, preferred_element_type=F32)
        dq_ref[...] = jnp.concatenate(dqs, axis=1)

    row = pl.BlockSpec((bq, C_WIDTH), lambda i: (i, 0))
    return pl.pallas_call(
        body, name=name, grid=(s // bq,),
        in_specs=[row, pl.BlockSpec((m, C_WIDTH), lambda i: (0, 0)), pl.BlockSpec((m, C_WIDTH), lambda i: (0, 1)), row],
        out_specs=[row, pl.BlockSpec((m, 2 * C_WIDTH), lambda i: (0, 0))],
        out_shape=[jax.ShapeDtypeStruct((s, C_WIDTH), F32), jax.ShapeDtypeStruct((m, 2 * C_WIDTH), F32)],
        compiler_params=_params(("arbitrary",)),
    )(q, mkv, mkv, do)


def _gate_fwd(y, proj, zc0, bw, name):
    rows, width = y.shape
    bm = _tile(rows, 1024, 8)
    cb0 = zc0 // bw

    def body(y_ref, z_ref, o_ref):
        z = z_ref[...]
        o_ref[...] = (y_ref[...] * (z * _sigmoid(z))).astype(BF16)

    return pl.pallas_call(
        body, name=name, grid=(rows // bm, width // bw),
        in_specs=[pl.BlockSpec((bm, bw), lambda i, t: (i, t)), pl.BlockSpec((bm, bw), lambda i, t: (i, cb0 + t))],
        out_specs=pl.BlockSpec((bm, bw), lambda i, t: (i, t)),
        out_shape=jax.ShapeDtypeStruct((rows, width), BF16),
        compiler_params=_params(("parallel", "parallel")),
    )(y, proj)


def _gate_bwd(dsv, y, proj, zc0, bw, dproj, t0, head_major, name):
    rows, width = y.shape
    bm = _tile(rows, 1024, 8)
    cb0 = zc0 // bw
    tb0 = t0 // bw
    bd = _block_diag(HEAD_DIM)
    hpb = bw // HEAD_DIM

    def body(*refs):
        if head_major:
            ds_ref, y_ref, z_ref, bd_ref, _, dp_ref, dy_ref, dd_ref = refs
        else:
            ds_ref, y_ref, z_ref, _, dp_ref, dy_ref = refs
        z = z_ref[...]
        sig = _sigmoid(z)
        dsx = ds_ref[...]
        yv = y_ref[...]
        dy = dsx * (z * sig)
        dp_ref[...] = (dsx * yv * (sig * (1.0 + z * (1.0 - sig)))).astype(BF16)
        if head_major:
            dyb = dy.astype(BF16)
            dd = _seg_sum(dyb.astype(F32) * yv, bd_ref[...])
            for h in range(hpb):
                sl = slice(h * HEAD_DIM, (h + 1) * HEAD_DIM)
                dy_ref[h] = dyb[:, sl]
                dd_ref[h] = dd[:, sl]
        else:
            dy_ref[...] = dy.astype(BF16)

    tile = pl.BlockSpec((bm, bw), lambda i, t: (i, t))
    ztile = pl.BlockSpec((bm, bw), lambda i, t: (i, cb0 + t))
    ttile = pl.BlockSpec((bm, bw), lambda i, t: (i, tb0 + t))
    any_spec = pl.BlockSpec(memory_space=pl.ANY)
    dp_shape = jax.ShapeDtypeStruct(dproj.shape, BF16)
    if head_major:
        hm_spec = pl.BlockSpec((hpb, bm, HEAD_DIM), lambda i, t: (t, i, 0))
        nh = width // HEAD_DIM
        outs = pl.pallas_call(
            body, name=name, grid=(rows // bm, width // bw),
            in_specs=[tile, tile, ztile, pl.BlockSpec((LANES, LANES), lambda i, t: (0, 0)), any_spec],
            out_specs=[ttile, hm_spec, hm_spec],
            out_shape=[dp_shape, jax.ShapeDtypeStruct((nh, rows, HEAD_DIM), BF16),
                       jax.ShapeDtypeStruct((nh, rows, HEAD_DIM), F32)],
            input_output_aliases={4: 0},
            compiler_params=_params(("parallel", "parallel")),
        )(dsv, y, proj, bd, dproj)
        return outs[0], outs[1], outs[2]
    outs = pl.pallas_call(
        body, name=name, grid=(rows // bm, width // bw),
        in_specs=[tile, tile, ztile, any_spec],
        out_specs=[ttile, tile],
        out_shape=[dp_shape, jax.ShapeDtypeStruct((rows, width), BF16)],
        input_output_aliases={3: 0},
        compiler_params=_params(("parallel", "parallel")),
    )(dsv, y, proj, dproj)
    return outs[0], outs[1], None


def _merge_fwd(proj, ua, ub, uc, name):
    rows, d = ua.shape
    bm = _tile(rows, 512, 8)
    bw = _tile(d, 512)
    g0 = COL_GATE // bw
    gstep = d // bw

    def body(ga_ref, gb_ref, gc_ref, ua_ref, ub_ref, uc_ref, o_ref):
        y = _sigmoid(ga_ref[...]) * ua_ref[...] + _sigmoid(gb_ref[...]) * ub_ref[...] + _sigmoid(gc_ref[...]) * uc_ref[...]
        o_ref[...] = y.astype(BF16)

    tile = pl.BlockSpec((bm, bw), lambda i, t: (i, t))
    gate = lambda b: pl.BlockSpec((bm, bw), lambda i, t: (i, g0 + b * gstep + t))
    return pl.pallas_call(
        body, name=name, grid=(rows // bm, d // bw),
        in_specs=[gate(0), gate(1), gate(2), tile, tile, tile],
        out_specs=tile, out_shape=jax.ShapeDtypeStruct((rows, d), BF16),
        compiler_params=_params(("parallel", "parallel")),
    )(proj, proj, proj, ua, ub, uc)


def _merge_bwd(dym, u, proj, branch, dproj, name):
    rows, d = u.shape
    bm = _tile(rows, 512, 8)
    bw = _tile(d, 512)
    gb0 = (COL_GATE + branch * d) // bw
    tb0 = (branch * d) // bw

    def body(dy_ref, u_ref, gl_ref, _, dp_ref, du_ref):
        g = _sigmoid(gl_ref[...])
        dyv = dy_ref[...]
        du_ref[...] = (g * dyv).astype(BF16)
        dp_ref[...] = (dyv * u_ref[...] * g * (1.0 - g)).astype(BF16)

    tile = pl.BlockSpec((bm, bw), lambda i, t: (i, t))
    gtile = pl.BlockSpec((bm, bw), lambda i, t: (i, gb0 + t))
    ttile = pl.BlockSpec((bm, bw), lambda i, t: (i, tb0 + t))
    outs = pl.pallas_call(
        body, name=name, grid=(rows // bm, d // bw),
        in_specs=[tile, tile, gtile, pl.BlockSpec(memory_space=pl.ANY)],
        out_specs=[ttile, tile],
        out_shape=[jax.ShapeDtypeStruct(dproj.shape, BF16), jax.ShapeDtypeStruct((rows, d), BF16)],
        input_output_aliases={3: 0},
        compiler_params=_params(("parallel", "parallel")),
    )(dym, u, proj, dproj)
    return outs[0], outs[1]


def _loss_head(y, target, name):
    rows, d = y.shape
    bm = _tile(rows, 256, 8)

    def body(y_ref, t_ref, dy_ref, dyb_ref, l_ref):
        i = pl.program_id(0)
        diff = y_ref[...] - t_ref[...]
        dy = diff * (1.0 / d)
        dy_ref[...] = dy
        dyb_ref[...] = dy.astype(BF16)
        sq = diff * diff
        part = sq[:, 0:LANES]
        for c in range(1, d // LANES):
            part = part + sq[:, c * LANES:(c + 1) * LANES]
        part = jnp.sum(part.reshape(bm // 8, 8, LANES), axis=0)

        @pl.when(i == 0)
        def _():
            l_ref[...] = part

        @pl.when(i > 0)
        def _():
            l_ref[...] += part

    row = pl.BlockSpec((bm, d), lambda i: (i, 0))
    return pl.pallas_call(
        body, name=name, grid=(rows // bm,), in_specs=[row, row],
        out_specs=[row, row, pl.BlockSpec((8, LANES), lambda i: (0, 0))],
        out_shape=[jax.ShapeDtypeStruct((rows, d), F32), jax.ShapeDtypeStruct((rows, d), BF16),
                   jax.ShapeDtypeStruct((8, LANES), F32)],
        compiler_params=_params(("arbitrary",)),
    )(y, target)


def _row(vec, reps=1):
    return jnp.tile(vec.reshape(1, -1).astype(F32), (1, reps))


def _local_step(x, mem, target, small, wg, shards=None):
    s, d = x.shape
    dist = shards is not None
    wg = dict(wg)
    ones = lambda n: jnp.ones((1, n), F32)
    zeros = lambda n: jnp.zeros((1, n), F32)
    scale_ab = HEAD_DIM ** -0.5
    split8 = lambda g: g.reshape(N_DEV, g.shape[0] // N_DEV, g.shape[1])
    flat8 = lambda g: g.reshape(g.shape[0] * g.shape[1], g.shape[2])
    gather = lambda names: _Comm("gather", [shards[n] for n in names]) if dist else None
    g = {}

    def scatter(names):
        return _Comm("scatter", [split8(g[n]) for n in names]) if dist else None

    def hosted(result, names, store):
        if not dist:
            return result
        out, got = result
        store.update(zip(names, got))
        return out

    hn = _rmsnorm_fwd(x, small["norm_gain"], "rms_x_fwd")
    got = {}
    proj = hosted(_mm_nn(hn, wg["qkv"], bm=1024, bn=1024, bk=d, o_dtype=F32, name="proj_qkv",
                         comm=gather(("wa", "wb", "wc"))), ("wa", "wb", "wc"), got)
    wg.update({n: flat8(a) for n, a in got.items()})
    pfb = _mm_nn(hn, wg["wf"], bm=1024, bn=FB_PAD, bk=d, o_dtype=F32, name="proj_fb")
    mn = _rmsnorm_fwd(mem, small["mem_norm_gain"], "rms_mem_fwd")
    mkv = _mm_nn(mn, wg["wk"], bm=256, bn=1024, bk=d, o_dtype=F32, name="mem_kv")

    gain_a = jnp.concatenate([_row(small["q_gain_a"], A_Q_HEADS) * scale_ab, _row(small["k_gain_a"], A_KV_HEADS), ones(A_KV_WIDTH)], axis=1)
    flag_a = jnp.concatenate([ones(A_WIDTH + A_KV_WIDTH), zeros(A_KV_WIDTH)], axis=1)
    qkv_a = _headnorm_fwd(proj, COL_QA, 1280, 1280, HEAD_DIM, gain_a, flag_a, True, "hn_a_fwd")
    gain_b = jnp.concatenate([_row(small["q_gain_b"], B_HEADS) * scale_ab, _row(small["k_gain_b"], B_HEADS), ones(B_WIDTH)], axis=1)
    flag_b = jnp.concatenate([ones(2 * B_WIDTH), zeros(B_WIDTH)], axis=1)
    qkv_b = _headnorm_fwd(proj, COL_QB, 2304, 256, HEAD_DIM, gain_b, flag_b, True, "hn_b_fwd")
    gain_cq = _row(small["q_gain_c"], C_HEADS)
    q_c = _headnorm_fwd(proj, COL_QC, C_WIDTH, C_WIDTH, C_HEAD_DIM, gain_cq, ones(C_WIDTH), False, "hn_cq_fwd")
    gain_ck = jnp.concatenate([_row(small["k_gain_c"], C_HEADS), ones(C_WIDTH)], axis=1)
    flag_ck = jnp.concatenate([ones(C_WIDTH), zeros(C_WIDTH)], axis=1)
    mkvn = _headnorm_fwd(mkv, 0, 2 * C_WIDTH, 2 * C_WIDTH, C_HEAD_DIM, gain_ck, flag_ck, False, "hn_ck_fwd")

    q_a, k_a, v_a = qkv_a[0:12], qkv_a[12:16], qkv_a[16:20]
    q_b, k_b, v_b = qkv_b[0:12], qkv_b[12:24], qkv_b[24:36]

    bpad = jnp.pad(small["b_forget"].reshape(1, -1), ((0, 0), (0, FB_PAD - B_HEADS)))
    c16 = _fox_prep(pfb, bpad, "fox_prep")
    c3 = c16[0:B_HEADS].reshape(B_HEADS, 1, s)

    sinks = small["sinks_a"].reshape(-1)
    slopes = jnp.exp2(-8.0 * jnp.arange(1, A_Q_HEADS + 1, dtype=F32) / A_Q_HEADS)
    y_a, lse_a = _attn_a_fwd(q_a, k_a, v_a, sinks, slopes, "attn_a_fwd")
    y_b, lse_b, got_zg = _attn_b_fwd(q_b, k_b, v_b, c3, "attn_b_fwd", comm=gather(("zg",)))
    if dist:
        wg["zg"] = flat8(got_zg[0])
    y_c = _attn_c_fwd(q_c, mkvn, "attn_c_fwd")

    got = {}
    pzg = hosted(_mm_nn(hn, wg["zg"], bm=1024, bn=1024, bk=d, o_dtype=F32, name="proj_zg", comm=gather(("wo",))),
                 ("wo",), got)
    wg.update({n: flat8(a) for n, a in got.items()})

    s_a = _gate_fwd(y_a, pzg, COL_ZA, 256, "gate_a_fwd")
    s_b = _gate_fwd(y_b, pzg, COL_ZB, 256, "gate_b_fwd")
    s_c = _gate_fwd(y_c, pzg, COL_ZC, 512, "gate_c_fwd")
    u_a = _mm_branch_fwd(s_a, wg["wa"], "branch_a_fwd")
    u_b = _mm_branch_fwd(s_b, wg["wb"], "branch_b_fwd")
    u_c = _mm_branch_fwd(s_c, wg["wc"], "branch_c_fwd")
    ym = _merge_fwd(pzg, u_a, u_b, u_c, "merge_fwd")
    y = _mm_nn(ym, wg["wo"], bm=1024, bn=1024, bk=d, o_dtype=F32, name="out_proj", add=x)
    dy, dyb, lpart = _loss_head(y, target, "loss_head")
    loss = 0.5 / d * jnp.sum(lpart)

    dym = _mm_nt(dyb, wg["wo"], bm=1024, bn=1024, bk=d, o_dtype=F32, name="out_proj_bwd_act")
    g["wo"] = _mm_tn(ym, dyb, bm=512, bn=1024, bk=s, o_dtype=BF16, name="out_proj_bwd_w")

    dgate = lax.empty((s, 3 * d), BF16)
    dgate, du_a = _merge_bwd(dym, u_a, pzg, 0, dgate, "merge_a_bwd")
    dgate, du_b = _merge_bwd(dym, u_b, pzg, 1, dgate, "merge_b_bwd")
    dgate, du_c = _merge_bwd(dym, u_c, pzg, 2, dgate, "merge_c_bwd")
    parts = {}
    g["wm_g"] = hosted(_mm_tn(hn, dgate, bm=512, bn=1024, bk=s, o_dtype=BF16, name="proj_gate_bwd_w",
                              comm=scatter(("wo",))), ("wo",), parts)
    dhn = _mm_nt(dgate, wg["zg"], bm=1024, bn=1024, bk=2048, o_dtype=F32, name="proj_gate_bwd_act", b_col0=COL_GATE)

    ds_a = _mm_branch_bwd_act(du_a, wg["wa"], A_WIDTH, "branch_a_bwd_act")
    ds_b = _mm_branch_bwd_act(du_b, wg["wb"], B_WIDTH, "branch_b_bwd_act")
    ds_c = _mm_branch_bwd_act(du_c, wg["wc"], C_WIDTH, "branch_c_bwd_act")
    g["wa"] = _mm_branch_bwd_w(s_a, du_a, "branch_a_bwd_w")
    g["wb"] = _mm_branch_bwd_w(s_b, du_b, "branch_b_bwd_w")
    g["wc"] = _mm_branch_bwd_w(s_c, du_c, "branch_c_bwd_w")

    dz = lax.empty((s, W_Z), BF16)
    dz, do_a, dd_a = _gate_bwd(ds_a, y_a, pzg, COL_ZA, 256, dz, COL_ZA, True, "gate_a_bwd")
    dz, do_b, dd_b = _gate_bwd(ds_b, y_b, pzg, COL_ZB, 256, dz, COL_ZB, True, "gate_b_bwd")
    dz, do_c, _ = _gate_bwd(ds_c, y_c, pzg, COL_ZC, 512, dz, COL_ZC, False, "gate_c_bwd")
    g["wm_z"] = _mm_tn(hn, dz, bm=512, bn=1024, bk=s, o_dtype=BF16, name="proj_z_bwd_w")
    dhn = _mm_nt(dz, wg["zg"], bm=1024, bn=1024, bk=2048, o_dtype=F32, name="proj_z_bwd_act", b_col0=COL_ZA, add=dhn)

    names = ("wa", "wb", "wc")
    dq_a, dkv_a, dsink, got = _attn_a_bwd(q_a, k_a, v_a, do_a, lse_a, dd_a, sinks, slopes, "attn_a_bwd", comm=scatter(names))
    parts.update(zip(names, got))
    names = ("wm_g", "wm_z")
    dq_b, dk_b, dv_b, dc3, got = _attn_b_bwd(q_b, k_b, v_b, do_b, lse_b, dd_b, c3, "attn_b_bwd", comm=scatter(names))
    parts.update(zip(names, got))
    dq_c, dmkvn = _attn_c_bwd(q_c, mkvn, do_c, "attn_c_bwd")

    dqkv = lax.empty((s, W_QKV), BF16)
    dqkv, dg_qa = _headnorm_bwd(proj, COL_QA, A_WIDTH, 256, HEAD_DIM, gain_a[:, 0:768], flag_a[:, 0:768], dq_a, dqkv, COL_QA, "hn_qa_bwd")
    dqkv, dg_kva = _headnorm_bwd(proj, COL_KA, 512, 256, HEAD_DIM, gain_a[:, 768:1280], flag_a[:, 768:1280], dkv_a, dqkv, COL_KA, "hn_kva_bwd")
    dqkv, dg_qb = _headnorm_bwd(proj, COL_QB, B_WIDTH, 256, HEAD_DIM, gain_b[:, 0:768], flag_b[:, 0:768], dq_b, dqkv, COL_QB, "hn_qb_bwd")
    dqkv, dg_kb = _headnorm_bwd(proj, COL_KB, B_WIDTH, 256, HEAD_DIM, gain_b[:, 768:1536], flag_b[:, 768:1536], dk_b, dqkv, COL_KB, "hn_kb_bwd")
    dqkv, _ = _headnorm_bwd(proj, COL_VB, B_WIDTH, 256, HEAD_DIM, gain_b[:, 1536:2304], flag_b[:, 1536:2304], dv_b, dqkv, COL_VB, "hn_vb_bwd")
    dqkv, dg_qc = _headnorm_bwd(proj, COL_QC, C_WIDTH, 512, C_HEAD_DIM, gain_cq, ones(C_WIDTH), dq_c, dqkv, COL_QC, "hn_qc_bwd")
    dmkv, dg_kc = _headnorm_bwd(mkv, 0, 2 * C_WIDTH, 2 * C_WIDTH, C_HEAD_DIM, gain_ck, flag_ck, dmkvn, None, 0, "hn_kc_bwd")

    dct = jnp.pad(dc3.reshape(B_HEADS, s), ((0, 16 - B_HEADS), (0, 0)))
    dfb, dbf = _fox_prep_bwd(pfb, bpad, dct, "fox_prep_bwd")

    dmn = _mm_nt(dmkv, wg["wk"], bm=256, bn=1024, bk=1024, o_dtype=F32, name="mem_kv_bwd_act")
    g["wk"] = _mm_tn(mn, dmkv, bm=512, bn=1024, bk=mem.shape[0], o_dtype=BF16, name="mem_kv_bwd_w")
    _, dg_mem = _rmsnorm_bwd(mem, dmn, small["mem_norm_gain"], None, "rms_mem_bwd")

    g["wm_qkv"] = _mm_tn(hn, dqkv, bm=512, bn=1024, bk=s, o_dtype=BF16, name="proj_qkv_bwd_w")
    g["wf"] = _mm_tn(hn, dfb, bm=512, bn=FB_PAD, bk=s, o_dtype=BF16, name="proj_fb_bwd_w")
    dhn = _mm_nt(dfb, wg["wf"], bm=1024, bn=1024, bk=FB_PAD, o_dtype=F32, name="proj_fb_bwd_act", add=dhn)
    names = ("wm_qkv", "wf", "wk")
    dhn = hosted(_mm_nt(dqkv, wg["qkv"], bm=1024, bn=1024, bk=2048, o_dtype=F32, name="proj_qkv_bwd_act", add=dhn,
                        comm=scatter(names)), names, parts)
    if dist:
        g = parts
    grad_x, dg_x = _rmsnorm_bwd(x, dhn, small["norm_gain"], dy, "rms_x_bwd")

    fold = lambda part, heads, hd: jnp.sum(jnp.sum(part, axis=0).reshape(heads, hd), axis=0).reshape(1, hd)
    small_grads = {
        "norm_gain": jnp.sum(dg_x, axis=0).reshape(1, d),
        "mem_norm_gain": jnp.sum(dg_mem, axis=0).reshape(1, d),
        "b_forget": dbf[0:B_HEADS, 0].reshape(1, B_HEADS),
        "q_gain_a": fold(dg_qa, A_Q_HEADS, HEAD_DIM) * scale_ab,
        "k_gain_a": fold(dg_kva[:, 0:A_KV_WIDTH], A_KV_HEADS, HEAD_DIM),
        "sinks_a": (jnp.sum(dsink, axis=(1, 2)) * (1.0 / HEAD_DIM)).reshape(1, A_Q_HEADS),
        "q_gain_b": fold(dg_qb, B_HEADS, HEAD_DIM) * scale_ab,
        "k_gain_b": fold(dg_kb, B_HEADS, HEAD_DIM),
        "q_gain_c": fold(dg_qc, C_HEADS, C_HEAD_DIM),
        "k_gain_c": fold(dg_kc[:, 0:C_WIDTH], C_HEADS, C_HEAD_DIM),
    }
    return loss, grad_x, small_grads, g


def _coords():
    return lax.axis_index("x"), lax.axis_index("y"), lax.axis_index("c")


def _all_gather(shards, name):
    n = len(shards)

    def body(*refs):
        ins = refs[0:n]
        outs = refs[n:2 * n]
        send_sems, recv_sems, local_sems = refs[2 * n:2 * n + 3]
        x, y, c = _coords()
        me, sibling = (x, y, c), (x, y, 1 - c)
        chips = [(1 - x, y), (x, 1 - y), (1 - x, 1 - y)]
        idx = lambda p: 4 * p[0] + 2 * p[1] + p[2]

        def copy(a, k, block, to, src=None):
            slot = outs[a].at[idx(block)]
            return pltpu.make_async_remote_copy(
                src_ref=slot if src is None else src, dst_ref=slot,
                send_sem=send_sems.at[a, k], recv_sem=recv_sems.at[a, k], device_id=to, device_id_type=MESH)

        mine = [pltpu.make_async_copy(ins[a], outs[a].at[idx(me)], local_sems.at[a]) for a in range(n)]
        for cp in mine:
            cp.start()
        first = []
        for a in range(n):
            first.append(copy(a, 0, me, sibling, src=ins[a]))
            first += [copy(a, 1 + j, me, (*chip, c), src=ins[a]) for j, chip in enumerate(chips)]
        for cp in first:
            cp.start()
        passed = []
        for j, chip in enumerate(chips):
            for a in range(n):
                copy(a, 1 + j, (*chip, c), me).wait_recv()
                fwd = copy(a, 4 + j, (*chip, c), sibling)
                fwd.start()
                passed.append(fwd)
        for a in range(n):
            copy(a, 0, sibling, me).wait_recv()
            for j, chip in enumerate(chips):
                copy(a, 4 + j, (*chip, 1 - c), me).wait_recv()
        for cp in first + passed:
            cp.wait_send()
        for cp in mine:
            cp.wait()

    any_spec = pl.BlockSpec(memory_space=pl.ANY)
    return pl.pallas_call(
        body, name=name,
        in_specs=[any_spec] * n, out_specs=[any_spec] * n,
        out_shape=[jax.ShapeDtypeStruct((N_DEV,) + sh.shape, sh.dtype) for sh in shards],
        scratch_shapes=[pltpu.SemaphoreType.DMA((n, 7)), pltpu.SemaphoreType.DMA((n, 7)), pltpu.SemaphoreType.DMA((n,))],
    )(*shards)


def _all_reduce_small(vec, name):
    p = vec.shape[1]

    def body(v_ref, o_ref, gather, send_sems, recv_sems):
        x, y, c = _coords()
        my = 4 * x + 2 * y + c
        peers = [(x ^ ((k >> 2) & 1), y ^ ((k >> 1) & 1), c ^ (k & 1)) for k in range(1, N_DEV)]
        gather[my] = v_ref[...]
        sends = [pltpu.make_async_remote_copy(
            src_ref=v_ref, dst_ref=gather.at[my], send_sem=send_sems.at[k], recv_sem=recv_sems.at[k],
            device_id=peer, device_id_type=MESH) for k, peer in enumerate(peers)]
        for cp in sends:
            cp.start()
        for k, peer in enumerate(peers):
            pid = 4 * peer[0] + 2 * peer[1] + peer[2]
            pltpu.make_async_remote_copy(
                src_ref=v_ref, dst_ref=gather.at[pid], send_sem=send_sems.at[k], recv_sem=recv_sems.at[k],
                device_id=peer, device_id_type=MESH).wait_recv()
        for cp in sends:
            cp.wait_send()
        total = gather[0]
        for j in range(1, N_DEV):
            total = total + gather[j]
        o_ref[...] = total

    vm = pl.BlockSpec(memory_space=pltpu.VMEM)
    return pl.pallas_call(
        body, name=name, in_specs=[vm], out_specs=vm,
        out_shape=jax.ShapeDtypeStruct((8, p), F32),
        scratch_shapes=[pltpu.VMEM((N_DEV, 8, p), F32), pltpu.SemaphoreType.DMA((7,)), pltpu.SemaphoreType.DMA((7,))],
    )(vec)[0:1]


def _sum_parts(parts, name):
    _, rows, cols = parts.shape
    br = _tile(rows, 64, 16)

    def body(p_ref, o_ref):
        total = p_ref[0].astype(F32)
        for j in range(1, N_DEV):
            total = total + p_ref[j].astype(F32)
        o_ref[...] = total

    return pl.pallas_call(
        body, name=name, grid=(rows // br,),
        in_specs=[pl.BlockSpec((N_DEV, br, cols), lambda i: (0, i, 0))],
        out_specs=pl.BlockSpec((br, cols), lambda i: (i, 0)),
        out_shape=jax.ShapeDtypeStruct((rows, cols), F32),
        compiler_params=_params(("parallel",), VMEM_BIG),
    )(parts)


def _adamw(w, g, m, v, name):
    rows, cols = w.shape
    br = _tile(rows, 32, 8)
    c1 = 1.0 / (1.0 - ADAM_B1 ** ADAM_STEP)
    c2 = 1.0 / (1.0 - ADAM_B2 ** ADAM_STEP)

    def body(w_ref, g_ref, m_ref, v_ref, d_ref, nm_ref, nv_ref):
        gv = g_ref[...]
        nm = ADAM_B1 * m_ref[...] + (1.0 - ADAM_B1) * gv
        nv = ADAM_B2 * v_ref[...] + (1.0 - ADAM_B2) * (gv * gv)
        d_ref[...] = -ADAM_LR * ((nm * c1) / (jnp.sqrt(nv * c2) + ADAM_EPS) + ADAM_WD * w_ref[...])
        nm_ref[...] = nm
        nv_ref[...] = nv

    spec = pl.BlockSpec((br, cols), lambda i: (i, 0))
    shape = jax.ShapeDtypeStruct((rows, cols), F32)
    return pl.pallas_call(
        body, name=name, grid=(rows // br,), in_specs=[spec] * 4, out_specs=[spec] * 3, out_shape=[shape] * 3,
        compiler_params=_params(("parallel",), VMEM_BIG),
    )(w, g, m, v)


SMALL_NAMES = ("norm_gain", "mem_norm_gain", "b_forget", "q_gain_a", "k_gain_a", "sinks_a",
               "q_gain_b", "k_gain_b", "q_gain_c", "k_gain_c")
BIG_NAMES = ("w_in", "w_mem_kv", "w_branch_a", "w_branch_b", "w_branch_c", "w_out")
WEIGHT_ORDER = ("norm_gain", "mem_norm_gain", "w_in", "b_forget", "q_gain_a", "k_gain_a", "sinks_a", "q_gain_b",
                "k_gain_b", "q_gain_c", "k_gain_c", "w_mem_kv", "w_branch_a", "w_branch_b", "w_branch_c", "w_out")


def _pack_small(tree):
    flat = jnp.concatenate([tree[n].reshape(1, -1) for n in SMALL_NAMES], axis=1)
    pad = (-flat.shape[1]) % LANES
    return jnp.pad(flat, ((0, 0), (0, pad)))


def _unpack_small(flat, like):
    out, off = {}, 0
    for n in SMALL_NAMES:
        size = like[n].size
        out[n] = flat[:, off:off + size].reshape(like[n].shape)
        off += size
    return out


def kernel(x, mem, norm_gain, mem_norm_gain, w_in, b_forget, q_gain_a, k_gain_a, sinks_a, q_gain_b, k_gain_b, q_gain_c, k_gain_c, w_mem_kv, w_branch_a, w_branch_b, w_branch_c, w_out, loss_target, m_norm_gain, m_mem_norm_gain, m_w_in, m_b_forget, m_q_gain_a, m_k_gain_a, m_sinks_a, m_q_gain_b, m_k_gain_b, m_q_gain_c, m_k_gain_c, m_w_mem_kv, m_w_branch_a, m_w_branch_b, m_w_branch_c, m_w_out, v_norm_gain, v_mem_norm_gain, v_w_in, v_b_forget, v_q_gain_a, v_k_gain_a, v_sinks_a, v_q_gain_b, v_k_gain_b, v_q_gain_c, v_k_gain_c, v_w_mem_kv, v_w_branch_a, v_w_branch_b, v_w_branch_c, v_w_out):
    weights = dict(norm_gain=norm_gain, mem_norm_gain=mem_norm_gain, w_in=w_in, b_forget=b_forget, q_gain_a=q_gain_a,
                   k_gain_a=k_gain_a, sinks_a=sinks_a, q_gain_b=q_gain_b, k_gain_b=k_gain_b, q_gain_c=q_gain_c,
                   k_gain_c=k_gain_c, w_mem_kv=w_mem_kv, w_branch_a=w_branch_a, w_branch_b=w_branch_b,
                   w_branch_c=w_branch_c, w_out=w_out)
    mom_m = dict(norm_gain=m_norm_gain, mem_norm_gain=m_mem_norm_gain, w_in=m_w_in, b_forget=m_b_forget,
                 q_gain_a=m_q_gain_a, k_gain_a=m_k_gain_a, sinks_a=m_sinks_a, q_gain_b=m_q_gain_b, k_gain_b=m_k_gain_b,
                 q_gain_c=m_q_gain_c, k_gain_c=m_k_gain_c, w_mem_kv=m_w_mem_kv, w_branch_a=m_w_branch_a,
                 w_branch_b=m_w_branch_b, w_branch_c=m_w_branch_c, w_out=m_w_out)
    mom_v = dict(norm_gain=v_norm_gain, mem_norm_gain=v_mem_norm_gain, w_in=v_w_in, b_forget=v_b_forget,
                 q_gain_a=v_q_gain_a, k_gain_a=v_k_gain_a, sinks_a=v_sinks_a, q_gain_b=v_q_gain_b, k_gain_b=v_k_gain_b,
                 q_gain_c=v_q_gain_c, k_gain_c=v_k_gain_c, w_mem_kv=v_w_mem_kv, w_branch_a=v_w_branch_a,
                 w_branch_b=v_w_branch_b, w_branch_c=v_w_branch_c, w_out=v_w_out)
    wi = w_in[0]
    sh_qkv = jnp.concatenate([wi[:, a:b] for a, b in SRC_RANGES[0:3]], axis=1).astype(BF16)
    sh_zg = jnp.concatenate([wi[:, a:b] for a, b in SRC_RANGES[3:6]] + [wi[:, SRC_GATE:]], axis=1).astype(BF16)
    sh_wf = jnp.pad(wi[:, FB_SRC:FB_SRC + B_HEADS], ((0, 0), (0, FB_PAD - B_HEADS))).astype(BF16)
    shards = {"zg": sh_zg, "wo": w_out[0].astype(BF16), "wa": w_branch_a[0].astype(BF16),
              "wb": w_branch_b[0].astype(BF16), "wc": w_branch_c[0].astype(BF16)}
    first = ("qkv", "wf", "wk")
    full = _all_gather([sh_qkv, sh_wf, w_mem_kv[0].astype(BF16)], "weights_all_gather")
    wg = {kname: arr.reshape(arr.shape[0] * arr.shape[1], arr.shape[2]) for kname, arr in zip(first, full)}

    small = {n: weights[n] for n in SMALL_NAMES}
    loss_local, grad_x, small_g, parts = _local_step(x[0], mem[0], loss_target[0], small, wg, shards)

    summed = {kname: _sum_parts(p, "grad_sum_" + kname) for kname, p in parts.items()}
    gq, gz, gf = summed["wm_qkv"], summed["wm_z"], summed["wf"]
    grads = {
        "w_in": jnp.concatenate([gq[:, COL_QA:COL_QB], gz[:, 0:COL_ZB - COL_ZA], gq[:, COL_QB:COL_QC],
                                 gz[:, COL_ZB - COL_ZA:COL_ZC - COL_ZA], gf[:, 0:B_HEADS], gq[:, COL_QC:W_QKV],
                                 gz[:, COL_ZC - COL_ZA:W_Z], summed["wm_g"]], axis=1),
        "w_mem_kv": summed["wk"], "w_out": summed["wo"],
        "w_branch_a": summed["wa"], "w_branch_b": summed["wb"], "w_branch_c": summed["wc"],
    }

    packed = _pack_small(small_g)
    reduced = _all_reduce_small(jnp.broadcast_to(packed, (8, packed.shape[1])), "small_all_reduce")
    grads.update(_unpack_small(reduced, small))

    loss = lax.psum(loss_local, ("x", "y", "c"))

    delta, new_m, new_v = {}, {}, {}
    for n in BIG_NAMES:
        dlt, nm, nv = _adamw(weights[n][0], grads[n], mom_m[n][0], mom_v[n][0], "adamw_" + n)
        delta[n], new_m[n], new_v[n] = dlt[None], nm[None], nv[None]
    pw, pm, pv = _pack_small(small), _pack_small({n: mom_m[n] for n in SMALL_NAMES}), _pack_small({n: mom_v[n] for n in SMALL_NAMES})
    rep8 = lambda a: jnp.broadcast_to(a, (8, a.shape[1]))
    dlt, nm, nv = _adamw(rep8(pw), rep8(reduced), rep8(pm), rep8(pv), "adamw_small")
    for tree, flat in ((delta, dlt), (new_m, nm), (new_v, nv)):
        tree.update(_unpack_small(flat[0:1], small))
    for n in BIG_NAMES:
        grads[n] = grads[n][None]
    return (loss, grad_x[None], *[grads[n] for n in WEIGHT_ORDER], *[delta[n] for n in WEIGHT_ORDER],
            *[new_m[n] for n in WEIGHT_ORDER], *[new_v[n] for n in WEIGHT_ORDER])
```

```python
import math

import jax
import jax.numpy as jnp
import numpy as np
from jax import lax
from jax.experimental import pallas as pl
from jax.experimental.pallas import tpu as pltpu

F32 = jnp.float32
BF16 = jnp.bfloat16

N_DEV = 8
HEAD_DIM = 64
A_Q_HEADS = 12
A_KV_HEADS = 4
A_GROUP = 3
B_HEADS = 12
C_HEADS = 4
C_HEAD_DIM = 128
WINDOW = 128
A_WIDTH = 768
A_KV_WIDTH = 256
B_WIDTH = 768
C_WIDTH = 512
EPS = 1e-6
NEG = -1e30

COL_QA, COL_KA, COL_VA = 0, 768, 1024
COL_QB, COL_KB, COL_VB = 1280, 2048, 2816
COL_QC = 3584
W_QKV = 4096
COL_ZA, COL_ZB, COL_ZC = 0, 768, 1536
COL_GATE = W_Z = 2048
SRC_RANGES = ((0, 1280), (2048, 4352), (5132, 5644), (1280, 2048), (4352, 5120), (5644, 6156))
SRC_GATE = 6156
FB_SRC = 5120
FB_PAD = 128

ADAM_LR = 0.001
ADAM_B1 = 0.9
ADAM_B2 = 0.999
ADAM_EPS = 1e-08
ADAM_WD = 0.01
ADAM_STEP = 10

VMEM_BIG = 52 * 1024 * 1024
LANES = 128
MESH = pl.DeviceIdType.MESH


def _tile(n, pref, mult=128):
    if n <= pref:
        return n
    t = (pref // mult) * mult
    while t >= mult:
        if n % t == 0:
            return t
        t -= mult
    return n


def _params(sem=None, vmem=None):
    kw = {}
    if sem is not None:
        kw["dimension_semantics"] = sem
    if vmem is not None:
        kw["vmem_limit_bytes"] = vmem
    return pltpu.CompilerParams(**kw)


def _sigmoid(x):
    return 1.0 / (1.0 + jnp.exp(-x))


def _block_diag(hd):
    r = np.arange(LANES)
    return jnp.asarray((r[:, None] // hd) == (r[None, :] // hd), dtype=BF16)


def _seg_sum(t, bd):
    hi = t.astype(BF16)
    lo = (t - hi.astype(F32)).astype(BF16)
    outs = []
    for c in range(t.shape[1] // LANES):
        sl = slice(c * LANES, (c + 1) * LANES)
        outs.append(jnp.dot(hi[:, sl], bd, preferred_element_type=F32) + jnp.dot(lo[:, sl], bd, preferred_element_type=F32))
    return outs[0] if len(outs) == 1 else jnp.concatenate(outs, axis=1)


def _rmsnorm_fwd(x, gain, name):
    rows, d = x.shape
    bm = _tile(rows, 512, 8)

    def body(x_ref, g_ref, o_ref):
        xv = x_ref[...]
        ms = jnp.mean(xv * xv, axis=-1, keepdims=True)
        o_ref[...] = (xv * lax.rsqrt(ms + EPS) * g_ref[...]).astype(BF16)

    return pl.pallas_call(
        body, name=name, grid=(rows // bm,),
        in_specs=[pl.BlockSpec((bm, d), lambda i: (i, 0)), pl.BlockSpec((1, d), lambda i: (0, 0))],
        out_specs=pl.BlockSpec((bm, d), lambda i: (i, 0)),
        out_shape=jax.ShapeDtypeStruct((rows, d), BF16),
        compiler_params=_params(("parallel",)),
    )(x, gain)


def _rmsnorm_bwd(x, dhn, gain, dy, name):
    rows, d = x.shape
    bm = _tile(rows, 256, 8)
    with_dx = dy is not None

    def body(*refs):
        if with_dx:
            x_ref, dh_ref, g_ref, dy_ref, gx_ref, dg_ref = refs
        else:
            x_ref, dh_ref, g_ref, dg_ref = refs
        i = pl.program_id(0)
        xv = x_ref[...]
        rstd = lax.rsqrt(jnp.mean(xv * xv, axis=-1, keepdims=True) + EPS)
        xhat = xv * rstd
        dh = dh_ref[...]
        part = jnp.sum((dh * xhat).reshape(bm // 8, 8, d), axis=0)

        @pl.when(i == 0)
        def _():
            dg_ref[...] = part

        @pl.when(i > 0)
        def _():
            dg_ref[...] += part

        if with_dx:
            g = dh * g_ref[...]
            mean = jnp.mean(g * xhat, axis=-1, keepdims=True)
            gx_ref[...] = dy_ref[...] + rstd * (g - xhat * mean)

    row_spec = pl.BlockSpec((bm, d), lambda i: (i, 0))
    in_specs = [row_spec, row_spec, pl.BlockSpec((1, d), lambda i: (0, 0))]
    args = [x, dhn, gain]
    dg_spec = pl.BlockSpec((8, d), lambda i: (0, 0))
    dg_shape = jax.ShapeDtypeStruct((8, d), F32)
    if with_dx:
        in_specs.append(row_spec)
        args.append(dy)
        out_specs = [row_spec, dg_spec]
        out_shape = [jax.ShapeDtypeStruct((rows, d), F32), dg_shape]
    else:
        out_specs = [dg_spec]
        out_shape = [dg_shape]
    outs = pl.pallas_call(
        body, name=name, grid=(rows // bm,), in_specs=in_specs, out_specs=out_specs, out_shape=out_shape,
        compiler_params=_params(("arbitrary",)),
    )(*args)
    return outs if with_dx else (None, outs[0])


class _Comm:
    def __init__(self, kind, arrays):
        self.kind = kind
        self.arrays = list(arrays)
        self.n = len(self.arrays)

    def out_shapes(self):
        if self.kind == "gather":
            return [jax.ShapeDtypeStruct((N_DEV,) + a.shape, a.dtype) for a in self.arrays]
        return [jax.ShapeDtypeStruct(a.shape, a.dtype) for a in self.arrays]

    def scratch(self):
        return [pltpu.SemaphoreType.DMA((self.n, N_DEV - 1)), pltpu.SemaphoreType.DMA((self.n, N_DEV - 1)),
                pltpu.SemaphoreType.DMA((self.n,))]

    def _plan(self, ins, outs, sems, with_recvs):
        send_sems, recv_sems, local_sems = sems
        x, y, c = lax.axis_index("x"), lax.axis_index("y"), lax.axis_index("c")
        my = 4 * x + 2 * y + c
        gather = self.kind == "gather"
        local, sends, recvs = [], [], []
        for a in range(self.n):
            local.append(pltpu.make_async_copy(ins[a] if gather else ins[a].at[my], outs[a].at[my], local_sems.at[a]))
            for k in range(1, N_DEV):
                peer = (x ^ ((k >> 2) & 1), y ^ ((k >> 1) & 1), c ^ (k & 1))
                pid = 4 * peer[0] + 2 * peer[1] + peer[2]
                src = ins[a] if gather else ins[a].at[pid]
                sem = dict(send_sem=send_sems.at[a, k - 1], recv_sem=recv_sems.at[a, k - 1], device_id=peer, device_id_type=MESH)
                sends.append(pltpu.make_async_remote_copy(src_ref=src, dst_ref=outs[a].at[my], **sem))
                if with_recvs:
                    recvs.append(pltpu.make_async_remote_copy(src_ref=src, dst_ref=outs[a].at[pid], **sem))
        return local, sends, recvs

    def start(self, ins, outs, sems):
        local, sends, _ = self._plan(ins, outs, sems, False)
        for cp in local + sends:
            cp.start()

    def wait(self, ins, outs, sems):
        local, sends, recvs = self._plan(ins, outs, sems, True)
        for cp in recvs:
            cp.wait_recv()
        for cp in sends:
            cp.wait_send()
        for cp in local:
            cp.wait()


def _grid_edges(grid):
    first = last = None
    for ax, size in enumerate(grid):
        pid = pl.program_id(ax)
        f, l = pid == 0, pid == size - 1
        first = f if first is None else first & f
        last = l if last is None else last & l
    return first, last


def _hosted_call(body, comm, *, name, grid, in_specs, out_specs, out_shape, scratch_shapes, args, sem, vmem=None):
    in_specs, out_specs, out_shape, scratch_shapes = list(in_specs), list(out_specs), list(out_shape), list(scratch_shapes)
    if comm is None:
        res = pl.pallas_call(body, name=name, grid=grid, in_specs=in_specs, out_specs=out_specs, out_shape=out_shape,
                             scratch_shapes=scratch_shapes, compiler_params=_params(sem, vmem))(*args)
        return list(res), []
    n_in, n_out, n_scr, nc = len(in_specs), len(out_shape), len(scratch_shapes), comm.n

    def hosted(*refs):
        ins = refs[0:n_in]
        comm_in = refs[n_in:n_in + nc]
        outs = refs[n_in + nc:n_in + nc + n_out]
        comm_out = refs[n_in + nc + n_out:n_in + 2 * nc + n_out]
        scr = refs[n_in + 2 * nc + n_out:n_in + 2 * nc + n_out + n_scr]
        sems = refs[n_in + 2 * nc + n_out + n_scr:]
        first, last = _grid_edges(grid)

        @pl.when(first)
        def _():
            comm.start(comm_in, comm_out, sems)

        body(*ins, *outs, *scr)

        @pl.when(last)
        def _():
            comm.wait(comm_in, comm_out, sems)

    any_spec = pl.BlockSpec(memory_space=pl.ANY)
    res = pl.pallas_call(
        hosted, name=name, grid=grid, in_specs=in_specs + [any_spec] * nc, out_specs=out_specs + [any_spec] * nc,
        out_shape=out_shape + comm.out_shapes(), scratch_shapes=scratch_shapes + comm.scratch(),
        compiler_params=_params(("arbitrary",) * len(grid), vmem),
    )(*args, *comm.arrays)
    return list(res[0:n_out]), list(res[n_out:])


def _mm(a, b, *, grid, a_spec, b_spec, o_spec, o_shape, o_dtype, contract, name, add=None, add_spec=None, acc_shape=None,
        comm=None):
    nk = grid[2]
    has_add = add is not None

    def body(*refs):
        a_ref, b_ref = refs[0], refs[1]
        add_ref = refs[2] if has_add else None
        o_ref = refs[3] if has_add else refs[2]
        part = lax.dot_general(a_ref[...], b_ref[...], (contract, ((), ())), preferred_element_type=F32)
        if nk == 1:
            if has_add:
                part = part + add_ref[...]
            o_ref[...] = part.astype(o_dtype)
        else:
            acc = refs[-1]
            k = pl.program_id(2)

            @pl.when(k == 0)
            def _():
                acc[...] = part

            @pl.when(k > 0)
            def _():
                acc[...] += part

            @pl.when(k == nk - 1)
            def _():
                r = acc[...]
                if has_add:
                    r = r + add_ref[...]
                o_ref[...] = r.astype(o_dtype)

    in_specs = [a_spec, b_spec] + ([add_spec] if has_add else [])
    args = [a, b] + ([add] if has_add else [])
    scratch = [pltpu.VMEM(acc_shape, F32)] if nk > 1 else []
    outs, comm_outs = _hosted_call(
        body, comm, name=name, grid=grid, in_specs=in_specs, out_specs=[o_spec],
        out_shape=[jax.ShapeDtypeStruct(o_shape, o_dtype)], scratch_shapes=scratch, args=args,
        sem=("parallel", "parallel", "arbitrary"), vmem=VMEM_BIG)
    return outs[0] if comm is None else (outs[0], comm_outs)


def _mm_nn(a, b, *, bm, bn, bk, o_dtype, name, add=None, comm=None):
    m, kd = a.shape
    n = b.shape[1]
    bm, bn, bk = _tile(m, bm, 8), _tile(n, bn), _tile(kd, bk)
    o_spec = pl.BlockSpec((bm, bn), lambda i, j, k: (i, j))
    return _mm(a, b, grid=(m // bm, n // bn, kd // bk),
               a_spec=pl.BlockSpec((bm, bk), lambda i, j, k: (i, k)),
               b_spec=pl.BlockSpec((bk, bn), lambda i, j, k: (k, j)),
               o_spec=o_spec, o_shape=(m, n), o_dtype=o_dtype, contract=((1,), (0,)), name=name,
               add=add, add_spec=o_spec, acc_shape=(bm, bn), comm=comm)


def _mm_nt(a, b, *, bm, bn, bk, o_dtype, name, add=None, b_col0=0, comm=None):
    m, kd = a.shape
    n = b.shape[0]
    bm, bn, bk = _tile(m, bm, 8), _tile(n, bn), _tile(math.gcd(kd, b_col0), bk)
    kb0 = b_col0 // bk
    o_spec = pl.BlockSpec((bm, bn), lambda i, j, k: (i, j))
    return _mm(a, b, grid=(m // bm, n // bn, kd // bk),
               a_spec=pl.BlockSpec((bm, bk), lambda i, j, k: (i, k)),
               b_spec=pl.BlockSpec((bn, bk), lambda i, j, k: (j, kb0 + k)),
               o_spec=o_spec, o_shape=(m, n), o_dtype=o_dtype, contract=((1,), (1,)), name=name,
               add=add, add_spec=o_spec, acc_shape=(bm, bn), comm=comm)


def _mm_nt_cat(a1, a2, b, *, bm, bn, bk, name, add, comm=None):
    m, k1 = a1.shape
    k2 = a2.shape[1]
    n = b.shape[0]
    bm, bn, bk = _tile(m, bm, 8), _tile(n, bn), _tile(math.gcd(k1, k2), bk)
    n1, nk = k1 // bk, (k1 + k2) // bk
    nt = (((1,), (1,)), ((), ()))

    def body(a1_ref, a2_ref, b_ref, add_ref, o_ref, acc):
        k = pl.program_id(2)

        def accumulate(part):
            @pl.when(k == 0)
            def _():
                acc[...] = part

            @pl.when(k > 0)
            def _():
                acc[...] += part

        @pl.when(k < n1)
        def _():
            accumulate(lax.dot_general(a1_ref[...], b_ref[...], nt, preferred_element_type=F32))

        @pl.when(k >= n1)
        def _():
            accumulate(lax.dot_general(a2_ref[...], b_ref[...], nt, preferred_element_type=F32))

        @pl.when(k == nk - 1)
        def _():
            o_ref[...] = acc[...] + add_ref[...]

    o_spec = pl.BlockSpec((bm, bn), lambda i, j, k: (i, j))
    outs, comm_outs = _hosted_call(
        body, comm, name=name, grid=(m // bm, n // bn, nk),
        in_specs=[pl.BlockSpec((bm, bk), lambda i, j, k: (i, jnp.minimum(k, n1 - 1))),
                  pl.BlockSpec((bm, bk), lambda i, j, k: (i, jnp.maximum(k - n1, 0))),
                  pl.BlockSpec((bn, bk), lambda i, j, k: (j, k)), o_spec],
        out_specs=[o_spec], out_shape=[jax.ShapeDtypeStruct((m, n), F32)],
        scratch_shapes=[pltpu.VMEM((bm, bn), F32)], args=[a1, a2, b, add],
        sem=("parallel", "parallel", "arbitrary"), vmem=VMEM_BIG)
    return outs[0] if comm is None else (outs[0], comm_outs)


def _mm_tn(a, b, *, bm, bn, bk, o_dtype, name, comm=None):
    kd, m = a.shape
    n = b.shape[1]
    bm, bn, bk = _tile(m, bm), _tile(n, bn), _tile(kd, bk, 8)
    return _mm(a, b, grid=(m // bm, n // bn, kd // bk),
               a_spec=pl.BlockSpec((bk, bm), lambda i, j, k: (k, i)),
               b_spec=pl.BlockSpec((bk, bn), lambda i, j, k: (k, j)),
               o_spec=pl.BlockSpec((bm, bn), lambda i, j, k: (i, j)),
               o_shape=(m, n), o_dtype=o_dtype, contract=((0,), (0,)), name=name, acc_shape=(bm, bn), comm=comm)


def _mm_branch_fwd(s, w2d, name):
    m, kb = s.shape
    ds = w2d.shape[1]
    bm = _tile(m, 1024, 8)
    return _mm(s, w2d, grid=(m // bm, N_DEV, 1),
               a_spec=pl.BlockSpec((bm, kb), lambda i, j, k: (i, 0)),
               b_spec=pl.BlockSpec((kb, ds), lambda i, j, k: (j, 0)),
               o_spec=pl.BlockSpec((bm, ds), lambda i, j, k: (i, j)),
               o_shape=(m, N_DEV * ds), o_dtype=F32, contract=((1,), (0,)), name=name)


def _mm_branch_bwd_act(du, w2d, kb, name):
    m = du.shape[0]
    ds = w2d.shape[1]
    bm = _tile(m, 1024, 8)
    return _mm(du, w2d, grid=(m // bm, 1, N_DEV),
               a_spec=pl.BlockSpec((bm, ds), lambda i, j, k: (i, k)),
               b_spec=pl.BlockSpec((kb, ds), lambda i, j, k: (k, 0)),
               o_spec=pl.BlockSpec((bm, kb), lambda i, j, k: (i, 0)),
               o_shape=(m, kb), o_dtype=F32, contract=((1,), (1,)), name=name, acc_shape=(bm, kb))


def _mm_branch_bwd_w(s, du, name):
    m, kb = s.shape
    ds = du.shape[1] // N_DEV
    bk = _tile(m, 2048, 8)
    return _mm(s, du, grid=(1, N_DEV, m // bk),
               a_spec=pl.BlockSpec((bk, kb), lambda i, j, k: (k, 0)),
               b_spec=pl.BlockSpec((bk, ds), lambda i, j, k: (k, j)),
               o_spec=pl.BlockSpec((kb, ds), lambda i, j, k: (j, 0)),
               o_shape=(N_DEV * kb, ds), o_dtype=BF16, contract=((0,), (0,)), name=name, acc_shape=(kb, ds))


def _headnorm_fwd(src, c0, width, bw, hd, gain, nflag, head_major, name):
    rows = src.shape[0]
    bm = _tile(rows, 1024, 8)
    bd = _block_diag(hd)
    cb0 = c0 // bw

    def body(x_ref, g_ref, f_ref, bd_ref, o_ref):
        xv = x_ref[...]
        ss = _seg_sum(xv * xv, bd_ref[...])
        rstd = lax.rsqrt(ss * (1.0 / hd) + EPS)
        y = (xv * jnp.where(f_ref[...] > 0.0, rstd, 1.0) * g_ref[...]).astype(BF16)
        if head_major:
            for h in range(bw // HEAD_DIM):
                o_ref[h] = y[:, h * HEAD_DIM:(h + 1) * HEAD_DIM]
        else:
            o_ref[...] = y

    vec_spec = pl.BlockSpec((1, bw), lambda i, t: (0, t))
    if head_major:
        hpb = bw // HEAD_DIM
        out_spec = pl.BlockSpec((hpb, bm, HEAD_DIM), lambda i, t: (t, i, 0))
        out_shape = jax.ShapeDtypeStruct((width // HEAD_DIM, rows, HEAD_DIM), BF16)
    else:
        out_spec = pl.BlockSpec((bm, bw), lambda i, t: (i, t))
        out_shape = jax.ShapeDtypeStruct((rows, width), BF16)
    return pl.pallas_call(
        body, name=name, grid=(rows // bm, width // bw),
        in_specs=[pl.BlockSpec((bm, bw), lambda i, t: (i, cb0 + t)), vec_spec, vec_spec,
                  pl.BlockSpec((LANES, LANES), lambda i, t: (0, 0))],
        out_specs=out_spec, out_shape=out_shape,
        compiler_params=_params(("parallel", "parallel")),
    )(src, gain, nflag, bd)


def _headnorm_bwd(src, c0, width, bw, hd, gain, nflag, dyn, target, t0, name):
    rows = src.shape[0]
    bm = _tile(rows, 1024, 8)
    bd = _block_diag(hd)
    cb0 = c0 // bw
    tb0 = t0 // bw
    aliased = target is not None

    def body(*refs):
        if aliased:
            x_ref, dy_ref, g_ref, f_ref, bd_ref, _, o_ref, dg_ref = refs
        else:
            x_ref, dy_ref, g_ref, f_ref, bd_ref, o_ref, dg_ref = refs
        i = pl.program_id(1)
        xv = x_ref[...]
        dyv = dy_ref[...]
        bdv = bd_ref[...]
        rstd = lax.rsqrt(_seg_sum(xv * xv, bdv) * (1.0 / hd) + EPS)
        xhat = xv * rstd
        g = dyv * g_ref[...]
        mean = _seg_sum(g * xhat, bdv) * (1.0 / hd)
        dx = jnp.where(f_ref[...] > 0.0, rstd * (g - xhat * mean), g)
        o_ref[...] = dx.astype(BF16)
        part = jnp.sum((dyv * xhat).reshape(bm // 8, 8, bw), axis=0)

        @pl.when(i == 0)
        def _():
            dg_ref[...] = part

        @pl.when(i > 0)
        def _():
            dg_ref[...] += part

    vec_spec = pl.BlockSpec((1, bw), lambda t, i: (0, t))
    in_specs = [pl.BlockSpec((bm, bw), lambda t, i: (i, cb0 + t)), pl.BlockSpec((bm, bw), lambda t, i: (i, t)),
                vec_spec, vec_spec, pl.BlockSpec((LANES, LANES), lambda t, i: (0, 0))]
    args = [src, dyn, gain, nflag, bd]
    aliases = {}
    if aliased:
        in_specs.append(pl.BlockSpec(memory_space=pl.ANY))
        args.append(target)
        aliases = {5: 0}
        o_shape = jax.ShapeDtypeStruct(target.shape, BF16)
    else:
        o_shape = jax.ShapeDtypeStruct((rows, width), BF16)
    out, dg = pl.pallas_call(
        body, name=name, grid=(width // bw, rows // bm), in_specs=in_specs,
        out_specs=[pl.BlockSpec((bm, bw), lambda t, i: (i, tb0 + t)), pl.BlockSpec((8, bw), lambda t, i: (0, t))],
        out_shape=[o_shape, jax.ShapeDtypeStruct((8, width), F32)],
        input_output_aliases=aliases,
        compiler_params=_params(("parallel", "arbitrary")),
    )(*args)
    return out, dg


def _fox_prep(pfb, bpad, name):
    s = pfb.shape[0]

    def body(p_ref, b_ref, c_ref):
        z = p_ref[...] + b_ref[...]
        logf = jnp.minimum(z, 0.0) - jnp.log(1.0 + jnp.exp(-jnp.abs(z)))
        x = logf.T[0:16, :]
        lane = lax.broadcasted_iota(jnp.int32, (16, s), 1)
        sh = 1
        while sh < s:
            x = x + jnp.where(lane >= sh, pltpu.roll(x, sh, 1), 0.0)
            sh *= 2
        c_ref[...] = x

    return pl.pallas_call(
        body, name=name, grid=(1,),
        in_specs=[pl.BlockSpec((s, FB_PAD), lambda i: (0, 0)), pl.BlockSpec((1, FB_PAD), lambda i: (0, 0))],
        out_specs=pl.BlockSpec((16, s), lambda i: (0, 0)),
        out_shape=jax.ShapeDtypeStruct((16, s), F32),
        compiler_params=_params(("arbitrary",)),
    )(pfb, bpad)


def _fox_prep_bwd(pfb, bpad, dct, name):
    s = pfb.shape[0]

    def body(p_ref, b_ref, dc_ref, df_ref, db_ref):
        zt = (p_ref[...] + b_ref[...]).T[0:16, :]
        y = dc_ref[...]
        lane = lax.broadcasted_iota(jnp.int32, (16, s), 1)
        sh = 1
        while sh < s:
            y = y + jnp.where(lane < s - sh, pltpu.roll(y, s - sh, 1), 0.0)
            sh *= 2
        dz = y * _sigmoid(-zt)
        db_ref[...] = jnp.broadcast_to(jnp.sum(dz, axis=1, keepdims=True), (16, FB_PAD))
        full = jnp.concatenate([dz, jnp.zeros((FB_PAD - 16, s), F32)], axis=0)
        df_ref[...] = full.T.astype(BF16)

    return pl.pallas_call(
        body, name=name, grid=(1,),
        in_specs=[pl.BlockSpec((s, FB_PAD), lambda i: (0, 0)), pl.BlockSpec((1, FB_PAD), lambda i: (0, 0)),
                  pl.BlockSpec((16, s), lambda i: (0, 0))],
        out_specs=[pl.BlockSpec((s, FB_PAD), lambda i: (0, 0)), pl.BlockSpec((16, FB_PAD), lambda i: (0, 0))],
        out_shape=[jax.ShapeDtypeStruct((s, FB_PAD), BF16), jax.ShapeDtypeStruct((16, FB_PAD), F32)],
        compiler_params=_params(("arbitrary",)),
    )(pfb, bpad, dct)


def _swa_window(n):
    ws = pl.multiple_of(jnp.maximum(n * WINDOW - WINDOW, 0), WINDOW)
    qi = lax.broadcasted_iota(jnp.int32, (WINDOW, 2 * WINDOW), 0)
    kj = lax.broadcasted_iota(jnp.int32, (WINDOW, 2 * WINDOW), 1)
    rel = qi + (n * WINDOW - ws) - kj
    valid = (rel >= 0) & (rel < WINDOW)
    return ws, valid, rel.astype(F32)


def _attn_a_fwd(q, k, v, sinks, slopes, name):
    s = q.shape[1]
    nb = s // WINDOW
    smem = pl.BlockSpec(memory_space=pltpu.SMEM)

    def body(sink_ref, slope_ref, q_ref, k_ref, v_ref, o_ref, lse_ref):
        n = pl.program_id(0)
        ws, valid, relf = _swa_window(n)
        outs = []
        for h in range(A_Q_HEADS):
            kvh = h // A_GROUP
            kw = k_ref[kvh, pl.ds(ws, 2 * WINDOW), :]
            vw = v_ref[kvh, pl.ds(ws, 2 * WINDOW), :]
            sc = lax.dot_general(q_ref[h], kw, (((1,), (1,)), ((), ())), preferred_element_type=F32)
            sc = jnp.where(valid, sc - slope_ref[h] * relf, NEG)
            sink = sink_ref[h]
            m = jnp.maximum(jnp.max(sc, axis=1, keepdims=True), sink)
            p = jnp.exp(sc - m)
            denom = jnp.sum(p, axis=1, keepdims=True) + jnp.exp(sink - m)
            pn = (p / denom).astype(BF16)
            outs.append(jnp.dot(pn, vw, preferred_element_type=F32))
            lse_ref[h] = jnp.broadcast_to(m + jnp.log(denom), (WINDOW, HEAD_DIM))
        o_ref[...] = jnp.concatenate(outs, axis=1)

    return pl.pallas_call(
        body, name=name, grid=(nb,),
        in_specs=[smem, smem,
                  pl.BlockSpec((A_Q_HEADS, WINDOW, HEAD_DIM), lambda n: (0, n, 0)),
                  pl.BlockSpec((A_KV_HEADS, s, HEAD_DIM), lambda n: (0, 0, 0)),
                  pl.BlockSpec((A_KV_HEADS, s, HEAD_DIM), lambda n: (0, 0, 0))],
        out_specs=[pl.BlockSpec((WINDOW, A_WIDTH), lambda n: (n, 0)),
                   pl.BlockSpec((A_Q_HEADS, WINDOW, HEAD_DIM), lambda n: (0, n, 0))],
        out_shape=[jax.ShapeDtypeStruct((s, A_WIDTH), F32), jax.ShapeDtypeStruct((A_Q_HEADS, s, HEAD_DIM), F32)],
        compiler_params=_params(("parallel",), VMEM_BIG),
    )(sinks, slopes, q, k, v)


def _attn_a_bwd(q, k, v, do, lse, dd, sinks, slopes, name, comm=None):
    s = q.shape[1]
    nb = s // WINDOW
    smem = pl.BlockSpec(memory_space=pltpu.SMEM)
    last = nb - 1

    def body(sink_ref, slope_ref, q_ref, k_ref, v_ref, do_ref, lse_ref, dd_ref, dq_ref, dkv_ref, ds_ref, carry):
        n = pl.program_id(0)

        @pl.when(n == 0)
        def _():
            carry[...] = jnp.zeros(carry.shape, F32)
            ds_ref[...] = jnp.zeros(ds_ref.shape, F32)

        @pl.when(n < nb)
        def _():
            ws, valid, relf = _swa_window(n)
            dqs = []
            dkw = [None] * A_KV_HEADS
            dvw = [None] * A_KV_HEADS
            for h in range(A_Q_HEADS):
                kvh = h // A_GROUP
                qh = q_ref[h]
                doh = do_ref[h]
                kw = k_ref[kvh, pl.ds(ws, 2 * WINDOW), :]
                vw = v_ref[kvh, pl.ds(ws, 2 * WINDOW), :]
                lse_h = lse_ref[h]
                dd_h = dd_ref[h]
                sc = lax.dot_general(qh, kw, (((1,), (1,)), ((), ())), preferred_element_type=F32)
                sc = jnp.where(valid, sc - slope_ref[h] * relf, NEG)
                p = jnp.exp(sc - lse_h[:, 0:1])
                dp = lax.dot_general(doh, vw, (((1,), (1,)), ((), ())), preferred_element_type=F32)
                dsc = (p * (dp - dd_h[:, 0:1])).astype(BF16)
                pb = p.astype(BF16)
                dqs.append(jnp.dot(dsc, kw, preferred_element_type=F32))
                dk_h = lax.dot_general(dsc, qh, (((0,), (0,)), ((), ())), preferred_element_type=F32)
                dv_h = lax.dot_general(pb, doh, (((0,), (0,)), ((), ())), preferred_element_type=F32)
                dkw[kvh] = dk_h if dkw[kvh] is None else dkw[kvh] + dk_h
                dvw[kvh] = dv_h if dvw[kvh] is None else dvw[kvh] + dv_h
                psink = jnp.exp(sink_ref[h] - lse_h)
                ds_ref[h] += jnp.sum((-psink * dd_h).reshape(WINDOW // 8, 8, HEAD_DIM), axis=0)
            dq_ref[...] = jnp.concatenate(dqs, axis=1)
            win = jnp.concatenate(dkw + dvw, axis=1)
            first = win[0:WINDOW]
            second = win[WINDOW:2 * WINDOW]
            dkv_ref[...] = carry[...] + first
            carry[...] = jnp.where(n == 0, first, second)

        @pl.when(n == nb)
        def _():
            dkv_ref[...] = carry[...]

    hm = lambda heads: pl.BlockSpec((heads, WINDOW, HEAD_DIM), lambda n: (0, jnp.minimum(n, last), 0))
    res = lambda heads: pl.BlockSpec((heads, s, HEAD_DIM), lambda n: (0, 0, 0))
    outs, comm_outs = _hosted_call(
        body, comm, name=name, grid=(nb + 1,),
        in_specs=[smem, smem, hm(A_Q_HEADS), res(A_KV_HEADS), res(A_KV_HEADS), hm(A_Q_HEADS), hm(A_Q_HEADS), hm(A_Q_HEADS)],
        out_specs=[pl.BlockSpec((WINDOW, A_WIDTH), lambda n: (jnp.minimum(n, last), 0)),
                   pl.BlockSpec((WINDOW, 2 * A_KV_WIDTH), lambda n: (jnp.maximum(n - 1, 0), 0)),
                   pl.BlockSpec((A_Q_HEADS, 8, HEAD_DIM), lambda n: (0, 0, 0))],
        out_shape=[jax.ShapeDtypeStruct((s, A_WIDTH), F32), jax.ShapeDtypeStruct((s, 2 * A_KV_WIDTH), F32),
                   jax.ShapeDtypeStruct((A_Q_HEADS, 8, HEAD_DIM), F32)],
        scratch_shapes=[pltpu.VMEM((WINDOW, 2 * A_KV_WIDTH), F32)],
        args=[sinks, slopes, q, k, v, do, lse, dd], sem=("arbitrary",), vmem=VMEM_BIG)
    return outs[0], outs[1], outs[2], comm_outs


def _attn_b_fwd(q, k, v, c3, name, comm=None):
    heads, s, _ = q.shape
    bq = min(512, s)
    nq = s // bq
    nt = (((1,), (1,)), ((), ()))

    def body(q_ref, k_ref, v_ref, c_ref, o_ref, lse_ref, m_scr, l_scr, acc_scr):
        i = pl.program_id(1)
        r0 = pl.multiple_of(i * bq, bq)
        row = lax.broadcasted_iota(jnp.int32, (bq, bq), 0)
        col = lax.broadcasted_iota(jnp.int32, (bq, bq), 1)
        m_scr[...] = jnp.full((2, bq, LANES), NEG, F32)
        l_scr[...] = jnp.zeros((2, bq, LANES), F32)
        acc_scr[...] = jnp.zeros((2, bq, HEAD_DIM), F32)

        def step(j, masked):
            k0 = pl.multiple_of(j * bq, bq)
            for h2 in range(2):
                kv = k_ref[h2, pl.ds(k0, bq), :]
                vv = v_ref[h2, pl.ds(k0, bq), :]
                cq0 = c_ref[h2, :, pl.ds(r0, LANES)][:, 0:1]
                sc = lax.dot_general(q_ref[h2], kv, nt, preferred_element_type=F32)
                sc = sc + (cq0 - c_ref[h2, :, pl.ds(k0, bq)])
                if masked:
                    sc = jnp.where(col <= row, sc, NEG)
                m_prev = m_scr[h2]
                m_new = jnp.maximum(m_prev, jnp.max(sc, axis=1, keepdims=True))
                alpha = jnp.exp(m_prev - m_new)
                p = jnp.exp(sc - m_new[:, 0:1])
                l_scr[h2] = alpha * l_scr[h2] + jnp.sum(p, axis=1, keepdims=True)
                p_hi = p.astype(BF16)
                p_lo = (p - p_hi.astype(F32)).astype(BF16)
                pv = jnp.dot(p_hi, vv, preferred_element_type=F32) + jnp.dot(p_lo, vv, preferred_element_type=F32)
                acc_scr[h2] = acc_scr[h2] * alpha[:, 0:HEAD_DIM] + pv
                m_scr[h2] = m_new

        def loop_body(j, carry):
            step(j, False)
            return carry

        lax.fori_loop(0, i, loop_body, 0)
        step(i, True)
        outs = []
        for h2 in range(2):
            l = l_scr[h2]
            outs.append(acc_scr[h2] / l[:, 0:HEAD_DIM])
            lse_ref[h2] = (m_scr[h2] + jnp.log(l))[:, 0:HEAD_DIM]
        o_ref[...] = jnp.concatenate(outs, axis=1)

    res = pl.BlockSpec((2, s, HEAD_DIM), lambda hp, i: (hp, 0, 0))
    outs, comm_outs = _hosted_call(
        body, comm, name=name, grid=(heads // 2, nq),
        in_specs=[pl.BlockSpec((2, bq, HEAD_DIM), lambda hp, i: (hp, i, 0)), res, res,
                  pl.BlockSpec((2, 1, s), lambda hp, i: (hp, 0, 0))],
        out_specs=[pl.BlockSpec((bq, 2 * HEAD_DIM), lambda hp, i: (i, hp)),
                   pl.BlockSpec((2, bq, HEAD_DIM), lambda hp, i: (hp, i, 0))],
        out_shape=[jax.ShapeDtypeStruct((s, heads * HEAD_DIM), F32), jax.ShapeDtypeStruct((heads, s, HEAD_DIM), F32)],
        scratch_shapes=[pltpu.VMEM((2, bq, LANES), F32), pltpu.VMEM((2, bq, LANES), F32), pltpu.VMEM((2, bq, HEAD_DIM), F32)],
        args=[q, k, v, c3], sem=("parallel", "parallel"), vmem=VMEM_BIG)
    return outs[0], outs[1], comm_outs


def _attn_b_bwd(q, k, v, do, lse, dd, c3, name, comm=None):
    heads, s, _ = q.shape
    bq = min(512, s)
    nq = s // bq
    nt = (((1,), (1,)), ((), ()))
    tn = (((0,), (0,)), ((), ()))
    grid = (heads // 2, nq)

    def body(q_ref, k_ref, v_ref, do_ref, lse_ref, dd_ref, c_ref, dq_ref, dk_ref, dv_ref, dc_ref,
             dq_scr, dk_scr, dv_scr, dc_scr):
        j = pl.program_id(1)
        k0 = pl.multiple_of(j * bq, bq)
        row = lax.broadcasted_iota(jnp.int32, (bq, bq), 0)
        col = lax.broadcasted_iota(jnp.int32, (bq, bq), 1)

        @pl.when(j == 0)
        def _():
            dq_scr[...] = jnp.zeros(dq_scr.shape, F32)

        dk_scr[...] = jnp.zeros((2, bq, HEAD_DIM), F32)
        dv_scr[...] = jnp.zeros((2, bq, HEAD_DIM), F32)
        dc_scr[...] = jnp.zeros((2, 1, bq), F32)

        def step(i, masked):
            r0 = pl.multiple_of(i * bq, bq)
            for h2 in range(2):
                kv = k_ref[h2]
                vv = v_ref[h2]
                qv = q_ref[h2, pl.ds(r0, bq), :]
                dov = do_ref[h2, pl.ds(r0, bq), :]
                lse_v = lse_ref[h2, pl.ds(r0, bq), :][:, 0:1]
                dd_v = dd_ref[h2, pl.ds(r0, bq), :][:, 0:1]
                cq0 = c_ref[h2, :, pl.ds(r0, LANES)][:, 0:1]
                sc = lax.dot_general(qv, kv, nt, preferred_element_type=F32) + (cq0 - c_ref[h2, :, pl.ds(k0, bq)])
                if masked:
                    sc = jnp.where(col <= row, sc, NEG)
                p = jnp.exp(sc - lse_v)
                dp = lax.dot_general(dov, vv, nt, preferred_element_type=F32)
                dsc = p * (dp - dd_v)
                dsb = dsc.astype(BF16)
                dv_scr[h2] += lax.dot_general(p.astype(BF16), dov, tn, preferred_element_type=F32)
                dk_scr[h2] += lax.dot_general(dsb, qv, tn, preferred_element_type=F32)
                dq_scr[h2, pl.ds(r0, bq), :] += jnp.dot(dsb, kv, preferred_element_type=F32)
                dc_scr[h2] -= jnp.sum(dsc, axis=0, keepdims=True)

        def loop_body(i, carry):
            step(i, False)
            return carry

        step(j, True)
        lax.fori_loop(j + 1, nq, loop_body, 0)
        dc_ref[...] = dc_scr[...]
        dk_ref[...] = jnp.concatenate([dk_scr[0], dk_scr[1]], axis=1)
        dv_ref[...] = jnp.concatenate([dv_scr[0], dv_scr[1]], axis=1)

        @pl.when(j == nq - 1)
        def _():
            dq_ref[...] = jnp.concatenate([dq_scr[0], dq_scr[1]], axis=1)

    res = pl.BlockSpec((2, s, HEAD_DIM), lambda hp, j: (hp, 0, 0))
    blk = pl.BlockSpec((2, bq, HEAD_DIM), lambda hp, j: (hp, j, 0))
    tm = jax.ShapeDtypeStruct((s, heads * HEAD_DIM), F32)
    in_specs = [res, blk, blk, res, res, res, pl.BlockSpec((2, 1, s), lambda hp, j: (hp, 0, 0))]
    out_specs = [pl.BlockSpec((s, 2 * HEAD_DIM), lambda hp, j: (0, hp)),
                 pl.BlockSpec((bq, 2 * HEAD_DIM), lambda hp, j: (j, hp)),
                 pl.BlockSpec((bq, 2 * HEAD_DIM), lambda hp, j: (j, hp)),
                 pl.BlockSpec((2, 1, bq), lambda hp, j: (hp, 0, j))]
    out_shape = [tm, tm, tm, jax.ShapeDtypeStruct((heads, 1, s), F32)]
    scratch = [pltpu.VMEM((2, s, HEAD_DIM), F32), pltpu.VMEM((2, bq, HEAD_DIM), F32),
               pltpu.VMEM((2, bq, HEAD_DIM), F32), pltpu.VMEM((2, 1, bq), F32)]
    outs, comm_outs = _hosted_call(
        body, comm, name=name, grid=grid, in_specs=in_specs, out_specs=out_specs, out_shape=out_shape,
        scratch_shapes=scratch, args=[q, k, v, do, lse, dd, c3], sem=("parallel", "arbitrary"), vmem=VMEM_BIG)
    return outs[0], outs[1], outs[2], outs[3], comm_outs


def _attn_c_probs(qh, mkh):
    sc = lax.dot_general(qh, mkh, (((1,), (1,)), ((), ())), preferred_element_type=F32) * (C_HEAD_DIM ** -0.5)
    p = jnp.exp(sc - jnp.max(sc, axis=1, keepdims=True))
    return p / jnp.sum(p, axis=1, keepdims=True)


def _attn_c_fwd(q, mkv, name):
    s = q.shape[0]
    m = mkv.shape[0]
    bq = _tile(s, 512, 8)

    def body(q_ref, mk_ref, mv_ref, o_ref):
        outs = []
        for h in range(C_HEADS):
            sl = slice(h * C_HEAD_DIM, (h + 1) * C_HEAD_DIM)
            pn = _attn_c_probs(q_ref[:, sl], mk_ref[:, sl]).astype(BF16)
            outs.append(jnp.dot(pn, mv_ref[:, sl], preferred_element_type=F32))
        o_ref[...] = jnp.concatenate(outs, axis=1)

    return pl.pallas_call(
        body, name=name, grid=(s // bq,),
        in_specs=[pl.BlockSpec((bq, C_WIDTH), lambda i: (i, 0)), pl.BlockSpec((m, C_WIDTH), lambda i: (0, 0)),
                  pl.BlockSpec((m, C_WIDTH), lambda i: (0, 1))],
        out_specs=pl.BlockSpec((bq, C_WIDTH), lambda i: (i, 0)),
        out_shape=jax.ShapeDtypeStruct((s, C_WIDTH), F32),
        compiler_params=_params(("parallel",)),
    )(q, mkv, mkv)


def _attn_c_bwd(q, mkv, do, name):
    s = q.shape[0]
    m = mkv.shape[0]
    bq = _tile(s, 512, 8)
    tn = (((0,), (0,)), ((), ()))

    def body(q_ref, mk_ref, mv_ref, do_ref, dq_ref, dm_ref):
        i = pl.program_id(0)

        @pl.when(i == 0)
        def _():
            dm_ref[...] = jnp.zeros(dm_ref.shape, F32)

        dqs = []
        for h in range(C_HEADS):
            sl = slice(h * C_HEAD_DIM, (h + 1) * C_HEAD_DIM)
            qh, mkh, mvh, doh = q_ref[:, sl], mk_ref[:, sl], mv_ref[:, sl], do_ref[:, sl]
            pn = _attn_c_probs(qh, mkh)
            dp = lax.dot_general(doh, mvh, (((1,), (1,)), ((), ())), preferred_element_type=F32)
            dsc = (pn * (dp - jnp.sum(pn * dp, axis=1, keepdims=True)) * (C_HEAD_DIM ** -0.5)).astype(BF16)
            dqs.append(jnp.dot(dsc, mkh, preferred_element_type=F32))
            dm_ref[:, sl] += lax.dot_general(dsc, qh, tn, preferred_element_type=F32)
            sv = slice(C_WIDTH + h * C_HEAD_DIM, C_WIDTH + (h + 1) * C_HEAD_DIM)
            dm_ref[:, sv] += lax.dot_general(pn.astype(BF16), doh, tn, preferred_element_type=F32)
        dq_ref[...] = jnp.concatenate(dqs, axis=1)

    row = pl.BlockSpec((bq, C_WIDTH), lambda i: (i, 0))
    return pl.pallas_call(
        body, name=name, grid=(s // bq,),
        in_specs=[row, pl.BlockSpec((m, C_WIDTH), lambda i: (0, 0)), pl.BlockSpec((m, C_WIDTH), lambda i: (0, 1)), row],
        out_specs=[row, pl.BlockSpec((m, 2 * C_WIDTH), lambda i: (0, 0))],
        out_shape=[jax.ShapeDtypeStruct((s, C_WIDTH), F32), jax.ShapeDtypeStruct((m, 2 * C_WIDTH), F32)],
        compiler_params=_params(("arbitrary",)),
    )(q, mkv, mkv, do)


def _gate_fwd(y, proj, zc0, bw, name):
    rows, width = y.shape
    bm = _tile(rows, 1024, 8)
    cb0 = zc0 // bw

    def body(y_ref, z_ref, o_ref):
        z = z_ref[...]
        o_ref[...] = (y_ref[...] * (z * _sigmoid(z))).astype(BF16)

    return pl.pallas_call(
        body, name=name, grid=(rows // bm, width // bw),
        in_specs=[pl.BlockSpec((bm, bw), lambda i, t: (i, t)), pl.BlockSpec((bm, bw), lambda i, t: (i, cb0 + t))],
        out_specs=pl.BlockSpec((bm, bw), lambda i, t: (i, t)),
        out_shape=jax.ShapeDtypeStruct((rows, width), BF16),
        compiler_params=_params(("parallel", "parallel")),
    )(y, proj)


def _gate_bwd(dsv, y, proj, zc0, bw, dproj, t0, head_major, name):
    rows, width = y.shape
    bm = _tile(rows, 1024, 8)
    cb0 = zc0 // bw
    tb0 = t0 // bw
    bd = _block_diag(HEAD_DIM)
    hpb = bw // HEAD_DIM

    def body(*refs):
        if head_major:
            ds_ref, y_ref, z_ref, bd_ref, _, dp_ref, dy_ref, dd_ref = refs
        else:
            ds_ref, y_ref, z_ref, _, dp_ref, dy_ref = refs
        z = z_ref[...]
        sig = _sigmoid(z)
        dsx = ds_ref[...]
        yv = y_ref[...]
        dy = dsx * (z * sig)
        dp_ref[...] = (dsx * yv * (sig * (1.0 + z * (1.0 - sig)))).astype(BF16)
        if head_major:
            dyb = dy.astype(BF16)
            dd = _seg_sum(dyb.astype(F32) * yv, bd_ref[...])
            for h in range(hpb):
                sl = slice(h * HEAD_DIM, (h + 1) * HEAD_DIM)
                dy_ref[h] = dyb[:, sl]
                dd_ref[h] = dd[:, sl]
        else:
            dy_ref[...] = dy.astype(BF16)

    tile = pl.BlockSpec((bm, bw), lambda i, t: (i, t))
    ztile = pl.BlockSpec((bm, bw), lambda i, t: (i, cb0 + t))
    ttile = pl.BlockSpec((bm, bw), lambda i, t: (i, tb0 + t))
    any_spec = pl.BlockSpec(memory_space=pl.ANY)
    dp_shape = jax.ShapeDtypeStruct(dproj.shape, BF16)
    if head_major:
        hm_spec = pl.BlockSpec((hpb, bm, HEAD_DIM), lambda i, t: (t, i, 0))
        nh = width // HEAD_DIM
        outs = pl.pallas_call(
            body, name=name, grid=(rows // bm, width // bw),
            in_specs=[tile, tile, ztile, pl.BlockSpec((LANES, LANES), lambda i, t: (0, 0)), any_spec],
            out_specs=[ttile, hm_spec, hm_spec],
            out_shape=[dp_shape, jax.ShapeDtypeStruct((nh, rows, HEAD_DIM), BF16),
                       jax.ShapeDtypeStruct((nh, rows, HEAD_DIM), F32)],
            input_output_aliases={4: 0},
            compiler_params=_params(("parallel", "parallel")),
        )(dsv, y, proj, bd, dproj)
        return outs[0], outs[1], outs[2]
    outs = pl.pallas_call(
        body, name=name, grid=(rows // bm, width // bw),
        in_specs=[tile, tile, ztile, any_spec],
        out_specs=[ttile, tile],
        out_shape=[dp_shape, jax.ShapeDtypeStruct((rows, width), BF16)],
        input_output_aliases={3: 0},
        compiler_params=_params(("parallel", "parallel")),
    )(dsv, y, proj, dproj)
    return outs[0], outs[1], None


def _merge_fwd(proj, ua, ub, uc, name):
    rows, d = ua.shape
    bm = _tile(rows, 512, 8)
    bw = _tile(d, 512)
    g0 = COL_GATE // bw
    gstep = d // bw

    def body(ga_ref, gb_ref, gc_ref, ua_ref, ub_ref, uc_ref, o_ref):
        y = _sigmoid(ga_ref[...]) * ua_ref[...] + _sigmoid(gb_ref[...]) * ub_ref[...] + _sigmoid(gc_ref[...]) * uc_ref[...]
        o_ref[...] = y.astype(BF16)

    tile = pl.BlockSpec((bm, bw), lambda i, t: (i, t))
    gate = lambda b: pl.BlockSpec((bm, bw), lambda i, t: (i, g0 + b * gstep + t))
    return pl.pallas_call(
        body, name=name, grid=(rows // bm, d // bw),
        in_specs=[gate(0), gate(1), gate(2), tile, tile, tile],
        out_specs=tile, out_shape=jax.ShapeDtypeStruct((rows, d), BF16),
        compiler_params=_params(("parallel", "parallel")),
    )(proj, proj, proj, ua, ub, uc)


def _merge_bwd(dym, u, proj, branch, dproj, name):
    rows, d = u.shape
    bm = _tile(rows, 512, 8)
    bw = _tile(d, 512)
    gb0 = (COL_GATE + branch * d) // bw
    tb0 = (branch * d) // bw

    def body(dy_ref, u_ref, gl_ref, _, dp_ref, du_ref):
        g = _sigmoid(gl_ref[...])
        dyv = dy_ref[...]
        du_ref[...] = (g * dyv).astype(BF16)
        dp_ref[...] = (dyv * u_ref[...] * g * (1.0 - g)).astype(BF16)

    tile = pl.BlockSpec((bm, bw), lambda i, t: (i, t))
    gtile = pl.BlockSpec((bm, bw), lambda i, t: (i, gb0 + t))
    ttile = pl.BlockSpec((bm, bw), lambda i, t: (i, tb0 + t))
    outs = pl.pallas_call(
        body, name=name, grid=(rows // bm, d // bw),
        in_specs=[tile, tile, gtile, pl.BlockSpec(memory_space=pl.ANY)],
        out_specs=[ttile, tile],
        out_shape=[jax.ShapeDtypeStruct(dproj.shape, BF16), jax.ShapeDtypeStruct((rows, d), BF16)],
        input_output_aliases={3: 0},
        compiler_params=_params(("parallel", "parallel")),
    )(dym, u, proj, dproj)
    return outs[0], outs[1]


def _loss_head(y, target, name):
    rows, d = y.shape
    bm = _tile(rows, 256, 8)

    def body(y_ref, t_ref, dy_ref, dyb_ref, l_ref):
        i = pl.program_id(0)
        diff = y_ref[...] - t_ref[...]
        dy = diff * (1.0 / d)
        dy_ref[...] = dy
        dyb_ref[...] = dy.astype(BF16)
        sq = diff * diff
        part = sq[:, 0:LANES]
        for c in range(1, d // LANES):
            part = part + sq[:, c * LANES:(c + 1) * LANES]
        part = jnp.sum(part.reshape(bm // 8, 8, LANES), axis=0)

        @pl.when(i == 0)
        def _():
            l_ref[...] = part

        @pl.when(i > 0)
        def _():
            l_ref[...] += part

    row = pl.BlockSpec((bm, d), lambda i: (i, 0))
    return pl.pallas_call(
        body, name=name, grid=(rows // bm,), in_specs=[row, row],
        out_specs=[row, row, pl.BlockSpec((8, LANES), lambda i: (0, 0))],
        out_shape=[jax.ShapeDtypeStruct((rows, d), F32), jax.ShapeDtypeStruct((rows, d), BF16),
                   jax.ShapeDtypeStruct((8, LANES), F32)],
        compiler_params=_params(("arbitrary",)),
    )(y, target)


def _row(vec, reps=1):
    return jnp.tile(vec.reshape(1, -1).astype(F32), (1, reps))


def _local_step(x, mem, target, small, wg, shards=None):
    s, d = x.shape
    dist = shards is not None
    wg = dict(wg)
    ones = lambda n: jnp.ones((1, n), F32)
    zeros = lambda n: jnp.zeros((1, n), F32)
    scale_ab = HEAD_DIM ** -0.5
    split8 = lambda g: g.reshape(N_DEV, g.shape[0] // N_DEV, g.shape[1])
    flat8 = lambda g: g.reshape(g.shape[0] * g.shape[1], g.shape[2])
    gather = lambda names: _Comm("gather", [shards[n] for n in names]) if dist else None
    g = {}

    def scatter(names):
        return _Comm("scatter", [split8(g[n]) for n in names]) if dist else None

    def hosted(result, names, store):
        if not dist:
            return result
        out, got = result
        store.update(zip(names, got))
        return out

    hn = _rmsnorm_fwd(x, small["norm_gain"], "rms_x_fwd")
    got = {}
    proj = hosted(_mm_nn(hn, wg["qkv"], bm=1024, bn=1024, bk=d, o_dtype=F32, name="proj_qkv",
                         comm=gather(("wa", "wb", "wc"))), ("wa", "wb", "wc"), got)
    wg.update({n: flat8(a) for n, a in got.items()})
    pfb = _mm_nn(hn, wg["wf"], bm=1024, bn=FB_PAD, bk=d, o_dtype=F32, name="proj_fb")
    mn = _rmsnorm_fwd(mem, small["mem_norm_gain"], "rms_mem_fwd")
    mkv = _mm_nn(mn, wg["wk"], bm=256, bn=1024, bk=d, o_dtype=F32, name="mem_kv")

    gain_a = jnp.concatenate([_row(small["q_gain_a"], A_Q_HEADS) * scale_ab, _row(small["k_gain_a"], A_KV_HEADS), ones(A_KV_WIDTH)], axis=1)
    flag_a = jnp.concatenate([ones(A_WIDTH + A_KV_WIDTH), zeros(A_KV_WIDTH)], axis=1)
    qkv_a = _headnorm_fwd(proj, COL_QA, 1280, 1280, HEAD_DIM, gain_a, flag_a, True, "hn_a_fwd")
    gain_b = jnp.concatenate([_row(small["q_gain_b"], B_HEADS) * scale_ab, _row(small["k_gain_b"], B_HEADS), ones(B_WIDTH)], axis=1)
    flag_b = jnp.concatenate([ones(2 * B_WIDTH), zeros(B_WIDTH)], axis=1)
    qkv_b = _headnorm_fwd(proj, COL_QB, 2304, 256, HEAD_DIM, gain_b, flag_b, True, "hn_b_fwd")
    gain_cq = _row(small["q_gain_c"], C_HEADS)
    q_c = _headnorm_fwd(proj, COL_QC, C_WIDTH, C_WIDTH, C_HEAD_DIM, gain_cq, ones(C_WIDTH), False, "hn_cq_fwd")
    gain_ck = jnp.concatenate([_row(small["k_gain_c"], C_HEADS), ones(C_WIDTH)], axis=1)
    flag_ck = jnp.concatenate([ones(C_WIDTH), zeros(C_WIDTH)], axis=1)
    mkvn = _headnorm_fwd(mkv, 0, 2 * C_WIDTH, 2 * C_WIDTH, C_HEAD_DIM, gain_ck, flag_ck, False, "hn_ck_fwd")

    q_a, k_a, v_a = qkv_a[0:12], qkv_a[12:16], qkv_a[16:20]
    q_b, k_b, v_b = qkv_b[0:12], qkv_b[12:24], qkv_b[24:36]

    bpad = jnp.pad(small["b_forget"].reshape(1, -1), ((0, 0), (0, FB_PAD - B_HEADS)))
    c16 = _fox_prep(pfb, bpad, "fox_prep")
    c3 = c16[0:B_HEADS].reshape(B_HEADS, 1, s)

    sinks = small["sinks_a"].reshape(-1)
    slopes = jnp.exp2(-8.0 * jnp.arange(1, A_Q_HEADS + 1, dtype=F32) / A_Q_HEADS)
    y_a, lse_a = _attn_a_fwd(q_a, k_a, v_a, sinks, slopes, "attn_a_fwd")
    y_b, lse_b, got_zg = _attn_b_fwd(q_b, k_b, v_b, c3, "attn_b_fwd", comm=gather(("zg",)))
    if dist:
        wg["zg"] = flat8(got_zg[0])
    y_c = _attn_c_fwd(q_c, mkvn, "attn_c_fwd")

    got = {}
    pzg = hosted(_mm_nn(hn, wg["zg"], bm=1024, bn=1024, bk=d, o_dtype=F32, name="proj_zg", comm=gather(("wo",))),
                 ("wo",), got)
    wg.update({n: flat8(a) for n, a in got.items()})

    s_a = _gate_fwd(y_a, pzg, COL_ZA, 256, "gate_a_fwd")
    s_b = _gate_fwd(y_b, pzg, COL_ZB, 256, "gate_b_fwd")
    s_c = _gate_fwd(y_c, pzg, COL_ZC, 512, "gate_c_fwd")
    u_a = _mm_branch_fwd(s_a, wg["wa"], "branch_a_fwd")
    u_b = _mm_branch_fwd(s_b, wg["wb"], "branch_b_fwd")
    u_c = _mm_branch_fwd(s_c, wg["wc"], "branch_c_fwd")
    ym = _merge_fwd(pzg, u_a, u_b, u_c, "merge_fwd")
    y = _mm_nn(ym, wg["wo"], bm=1024, bn=1024, bk=d, o_dtype=F32, name="out_proj", add=x)
    dy, dyb, lpart = _loss_head(y, target, "loss_head")
    loss = 0.5 / d * jnp.sum(lpart)

    dym = _mm_nt(dyb, wg["wo"], bm=1024, bn=1024, bk=d, o_dtype=F32, name="out_proj_bwd_act")
    g["wo"] = _mm_tn(ym, dyb, bm=512, bn=1024, bk=s, o_dtype=BF16, name="out_proj_bwd_w")

    dgate = lax.empty((s, 3 * d), BF16)
    dgate, du_a = _merge_bwd(dym, u_a, pzg, 0, dgate, "merge_a_bwd")
    dgate, du_b = _merge_bwd(dym, u_b, pzg, 1, dgate, "merge_b_bwd")
    dgate, du_c = _merge_bwd(dym, u_c, pzg, 2, dgate, "merge_c_bwd")
    parts = {}
    g["wm_g"] = hosted(_mm_tn(hn, dgate, bm=512, bn=1024, bk=s, o_dtype=BF16, name="proj_gate_bwd_w",
                              comm=scatter(("wo",))), ("wo",), parts)

    ds_a = _mm_branch_bwd_act(du_a, wg["wa"], A_WIDTH, "branch_a_bwd_act")
    ds_b = _mm_branch_bwd_act(du_b, wg["wb"], B_WIDTH, "branch_b_bwd_act")
    ds_c = _mm_branch_bwd_act(du_c, wg["wc"], C_WIDTH, "branch_c_bwd_act")
    g["wa"] = _mm_branch_bwd_w(s_a, du_a, "branch_a_bwd_w")
    g["wb"] = _mm_branch_bwd_w(s_b, du_b, "branch_b_bwd_w")
    g["wc"] = _mm_branch_bwd_w(s_c, du_c, "branch_c_bwd_w")

    dz = lax.empty((s, W_Z), BF16)
    dz, do_a, dd_a = _gate_bwd(ds_a, y_a, pzg, COL_ZA, 256, dz, COL_ZA, True, "gate_a_bwd")
    dz, do_b, dd_b = _gate_bwd(ds_b, y_b, pzg, COL_ZB, 256, dz, COL_ZB, True, "gate_b_bwd")
    dz, do_c, _ = _gate_bwd(ds_c, y_c, pzg, COL_ZC, 512, dz, COL_ZC, False, "gate_c_bwd")
    g["wm_z"] = _mm_tn(hn, dz, bm=512, bn=1024, bk=s, o_dtype=BF16, name="proj_z_bwd_w")

    names = ("wa", "wb", "wc")
    dq_a, dkv_a, dsink, got = _attn_a_bwd(q_a, k_a, v_a, do_a, lse_a, dd_a, sinks, slopes, "attn_a_bwd", comm=scatter(names))
    parts.update(zip(names, got))
    names = ("wm_g", "wm_z")
    dq_b, dk_b, dv_b, dc3, got = _attn_b_bwd(q_b, k_b, v_b, do_b, lse_b, dd_b, c3, "attn_b_bwd", comm=scatter(names))
    parts.update(zip(names, got))
    dq_c, dmkvn = _attn_c_bwd(q_c, mkvn, do_c, "attn_c_bwd")

    dqkv = lax.empty((s, W_QKV), BF16)
    dqkv, dg_qa = _headnorm_bwd(proj, COL_QA, A_WIDTH, 256, HEAD_DIM, gain_a[:, 0:768], flag_a[:, 0:768], dq_a, dqkv, COL_QA, "hn_qa_bwd")
    dqkv, dg_kva = _headnorm_bwd(proj, COL_KA, 512, 256, HEAD_DIM, gain_a[:, 768:1280], flag_a[:, 768:1280], dkv_a, dqkv, COL_KA, "hn_kva_bwd")
    dqkv, dg_qb = _headnorm_bwd(proj, COL_QB, B_WIDTH, 256, HEAD_DIM, gain_b[:, 0:768], flag_b[:, 0:768], dq_b, dqkv, COL_QB, "hn_qb_bwd")
    dqkv, dg_kb = _headnorm_bwd(proj, COL_KB, B_WIDTH, 256, HEAD_DIM, gain_b[:, 768:1536], flag_b[:, 768:1536], dk_b, dqkv, COL_KB, "hn_kb_bwd")
    dqkv, _ = _headnorm_bwd(proj, COL_VB, B_WIDTH, 256, HEAD_DIM, gain_b[:, 1536:2304], flag_b[:, 1536:2304], dv_b, dqkv, COL_VB, "hn_vb_bwd")
    dqkv, dg_qc = _headnorm_bwd(proj, COL_QC, C_WIDTH, 512, C_HEAD_DIM, gain_cq, ones(C_WIDTH), dq_c, dqkv, COL_QC, "hn_qc_bwd")
    dmkv, dg_kc = _headnorm_bwd(mkv, 0, 2 * C_WIDTH, 2 * C_WIDTH, C_HEAD_DIM, gain_ck, flag_ck, dmkvn, None, 0, "hn_kc_bwd")

    dct = jnp.pad(dc3.reshape(B_HEADS, s), ((0, 16 - B_HEADS), (0, 0)))
    dfb, dbf = _fox_prep_bwd(pfb, bpad, dct, "fox_prep_bwd")

    dmn = _mm_nt(dmkv, wg["wk"], bm=256, bn=1024, bk=1024, o_dtype=F32, name="mem_kv_bwd_act")
    g["wk"] = _mm_tn(mn, dmkv, bm=512, bn=1024, bk=mem.shape[0], o_dtype=BF16, name="mem_kv_bwd_w")
    _, dg_mem = _rmsnorm_bwd(mem, dmn, small["mem_norm_gain"], None, "rms_mem_bwd")

    g["wm_qkv"] = _mm_tn(hn, dqkv, bm=512, bn=1024, bk=s, o_dtype=BF16, name="proj_qkv_bwd_w")
    g["wf"] = _mm_tn(hn, dfb, bm=512, bn=FB_PAD, bk=s, o_dtype=BF16, name="proj_fb_bwd_w")
    dhn = _mm_nt(dqkv, wg["qkv"], bm=1024, bn=1024, bk=2048, o_dtype=F32, name="proj_qkv_bwd_act")
    dhn = _mm_nt(dfb, wg["wf"], bm=1024, bn=1024, bk=FB_PAD, o_dtype=F32, name="proj_fb_bwd_act", add=dhn)
    names = ("wm_qkv", "wf", "wk")
    dhn = hosted(_mm_nt_cat(dz, dgate, wg["zg"], bm=1024, bn=1024, bk=2048, name="proj_zg_bwd_act", add=dhn,
                            comm=scatter(names)), names, parts)
    if dist:
        g = parts
    grad_x, dg_x = _rmsnorm_bwd(x, dhn, small["norm_gain"], dy, "rms_x_bwd")

    fold = lambda part, heads, hd: jnp.sum(jnp.sum(part, axis=0).reshape(heads, hd), axis=0).reshape(1, hd)
    small_grads = {
        "norm_gain": jnp.sum(dg_x, axis=0).reshape(1, d),
        "mem_norm_gain": jnp.sum(dg_mem, axis=0).reshape(1, d),
        "b_forget": dbf[0:B_HEADS, 0].reshape(1, B_HEADS),
        "q_gain_a": fold(dg_qa, A_Q_HEADS, HEAD_DIM) * scale_ab,
        "k_gain_a": fold(dg_kva[:, 0:A_KV_WIDTH], A_KV_HEADS, HEAD_DIM),
        "sinks_a": (jnp.sum(dsink, axis=(1, 2)) * (1.0 / HEAD_DIM)).reshape(1, A_Q_HEADS),
        "q_gain_b": fold(dg_qb, B_HEADS, HEAD_DIM) * scale_ab,
        "k_gain_b": fold(dg_kb, B_HEADS, HEAD_DIM),
        "q_gain_c": fold(dg_qc, C_HEADS, C_HEAD_DIM),
        "k_gain_c": fold(dg_kc[:, 0:C_WIDTH], C_HEADS, C_HEAD_DIM),
    }
    return loss, grad_x, small_grads, g


def _coords():
    return lax.axis_index("x"), lax.axis_index("y"), lax.axis_index("c")


def _all_gather(shards, name):
    n = len(shards)

    def body(*refs):
        ins = refs[0:n]
        outs = refs[n:2 * n]
        send_sems, recv_sems, local_sems = refs[2 * n:2 * n + 3]
        x, y, c = _coords()
        me, sibling = (x, y, c), (x, y, 1 - c)
        chips = [(1 - x, y), (x, 1 - y), (1 - x, 1 - y)]
        idx = lambda p: 4 * p[0] + 2 * p[1] + p[2]

        def copy(a, k, block, to, src=None):
            slot = outs[a].at[idx(block)]
            return pltpu.make_async_remote_copy(
                src_ref=slot if src is None else src, dst_ref=slot,
                send_sem=send_sems.at[a, k], recv_sem=recv_sems.at[a, k], device_id=to, device_id_type=MESH)

        mine = [pltpu.make_async_copy(ins[a], outs[a].at[idx(me)], local_sems.at[a]) for a in range(n)]
        for cp in mine:
            cp.start()
        first = []
        for a in range(n):
            first.append(copy(a, 0, me, sibling, src=ins[a]))
            first += [copy(a, 1 + j, me, (*chip, c), src=ins[a]) for j, chip in enumerate(chips)]
        for cp in first:
            cp.start()
        passed = []
        for j, chip in enumerate(chips):
            for a in range(n):
                copy(a, 1 + j, (*chip, c), me).wait_recv()
                fwd = copy(a, 4 + j, (*chip, c), sibling)
                fwd.start()
                passed.append(fwd)
        for a in range(n):
            copy(a, 0, sibling, me).wait_recv()
            for j, chip in enumerate(chips):
                copy(a, 4 + j, (*chip, 1 - c), me).wait_recv()
        for cp in first + passed:
            cp.wait_send()
        for cp in mine:
            cp.wait()

    any_spec = pl.BlockSpec(memory_space=pl.ANY)
    return pl.pallas_call(
        body, name=name,
        in_specs=[any_spec] * n, out_specs=[any_spec] * n,
        out_shape=[jax.ShapeDtypeStruct((N_DEV,) + sh.shape, sh.dtype) for sh in shards],
        scratch_shapes=[pltpu.SemaphoreType.DMA((n, 7)), pltpu.SemaphoreType.DMA((n, 7)), pltpu.SemaphoreType.DMA((n,))],
    )(*shards)


def _all_reduce_small(vec, name):
    p = vec.shape[1]

    def body(v_ref, o_ref, gather, send_sems, recv_sems):
        x, y, c = _coords()
        my = 4 * x + 2 * y + c
        peers = [(x ^ ((k >> 2) & 1), y ^ ((k >> 1) & 1), c ^ (k & 1)) for k in range(1, N_DEV)]
        gather[my] = v_ref[...]
        sends = [pltpu.make_async_remote_copy(
            src_ref=v_ref, dst_ref=gather.at[my], send_sem=send_sems.at[k], recv_sem=recv_sems.at[k],
            device_id=peer, device_id_type=MESH) for k, peer in enumerate(peers)]
        for cp in sends:
            cp.start()
        for k, peer in enumerate(peers):
            pid = 4 * peer[0] + 2 * peer[1] + peer[2]
            pltpu.make_async_remote_copy(
                src_ref=v_ref, dst_ref=gather.at[pid], send_sem=send_sems.at[k], recv_sem=recv_sems.at[k],
                device_id=peer, device_id_type=MESH).wait_recv()
        for cp in sends:
            cp.wait_send()
        total = gather[0]
        for j in range(1, N_DEV):
            total = total + gather[j]
        o_ref[...] = total

    vm = pl.BlockSpec(memory_space=pltpu.VMEM)
    return pl.pallas_call(
        body, name=name, in_specs=[vm], out_specs=vm,
        out_shape=jax.ShapeDtypeStruct((8, p), F32),
        scratch_shapes=[pltpu.VMEM((N_DEV, 8, p), F32), pltpu.SemaphoreType.DMA((7,)), pltpu.SemaphoreType.DMA((7,))],
    )(vec)[0:1]


def _sum_parts(parts, name):
    _, rows, cols = parts.shape
    br = _tile(rows, 64, 16)

    def body(p_ref, o_ref):
        total = p_ref[0].astype(F32)
        for j in range(1, N_DEV):
            total = total + p_ref[j].astype(F32)
        o_ref[...] = total

    return pl.pallas_call(
        body, name=name, grid=(rows // br,),
        in_specs=[pl.BlockSpec((N_DEV, br, cols), lambda i: (0, i, 0))],
        out_specs=pl.BlockSpec((br, cols), lambda i: (i, 0)),
        out_shape=jax.ShapeDtypeStruct((rows, cols), F32),
        compiler_params=_params(("parallel",), VMEM_BIG),
    )(parts)


def _adamw(w, g, m, v, name, br=32):
    rows, cols = w.shape
    br = min(br, rows)
    c1 = 1.0 / (1.0 - ADAM_B1 ** ADAM_STEP)
    c2 = 1.0 / (1.0 - ADAM_B2 ** ADAM_STEP)

    def body(w_ref, g_ref, m_ref, v_ref, d_ref, nm_ref, nv_ref):
        gv = g_ref[...]
        nm = ADAM_B1 * m_ref[...] + (1.0 - ADAM_B1) * gv
        nv = ADAM_B2 * v_ref[...] + (1.0 - ADAM_B2) * (gv * gv)
        d_ref[...] = -ADAM_LR * ((nm * c1) / (jnp.sqrt(nv * c2) + ADAM_EPS) + ADAM_WD * w_ref[...])
        nm_ref[...] = nm
        nv_ref[...] = nv

    spec = pl.BlockSpec((br, cols), lambda i: (i, 0))
    shape = jax.ShapeDtypeStruct((rows, cols), F32)
    return pl.pallas_call(
        body, name=name, grid=(pl.cdiv(rows, br),), in_specs=[spec] * 4, out_specs=[spec] * 3, out_shape=[shape] * 3,
        compiler_params=_params(("parallel",), VMEM_BIG),
    )(w, g, m, v)


def _adamw_parts(w, parts, m, v, name):
    rows, cols = w.shape
    br = _tile(rows, 32, 16)
    c1 = 1.0 / (1.0 - ADAM_B1 ** ADAM_STEP)
    c2 = 1.0 / (1.0 - ADAM_B2 ** ADAM_STEP)

    def body(w_ref, p_ref, m_ref, v_ref, g_ref, d_ref, nm_ref, nv_ref):
        gv = p_ref[0].astype(F32)
        for j in range(1, N_DEV):
            gv = gv + p_ref[j].astype(F32)
        nm = ADAM_B1 * m_ref[...] + (1.0 - ADAM_B1) * gv
        nv = ADAM_B2 * v_ref[...] + (1.0 - ADAM_B2) * (gv * gv)
        g_ref[...] = gv
        d_ref[...] = -ADAM_LR * ((nm * c1) / (jnp.sqrt(nv * c2) + ADAM_EPS) + ADAM_WD * w_ref[...])
        nm_ref[...] = nm
        nv_ref[...] = nv

    spec = pl.BlockSpec((br, cols), lambda i: (i, 0))
    shape = jax.ShapeDtypeStruct((rows, cols), F32)
    return pl.pallas_call(
        body, name=name, grid=(rows // br,),
        in_specs=[spec, pl.BlockSpec((N_DEV, br, cols), lambda i: (0, i, 0)), spec, spec],
        out_specs=[spec] * 4, out_shape=[shape] * 4,
        compiler_params=_params(("parallel",), VMEM_BIG),
    )(w, parts, m, v)


SMALL_NAMES = ("norm_gain", "mem_norm_gain", "b_forget", "q_gain_a", "k_gain_a", "sinks_a",
               "q_gain_b", "k_gain_b", "q_gain_c", "k_gain_c")
BIG_NAMES = ("w_in", "w_mem_kv", "w_branch_a", "w_branch_b", "w_branch_c", "w_out")
WEIGHT_ORDER = ("norm_gain", "mem_norm_gain", "w_in", "b_forget", "q_gain_a", "k_gain_a", "sinks_a", "q_gain_b",
                "k_gain_b", "q_gain_c", "k_gain_c", "w_mem_kv", "w_branch_a", "w_branch_b", "w_branch_c", "w_out")


def _pack_small(tree):
    flat = jnp.concatenate([tree[n].reshape(1, -1) for n in SMALL_NAMES], axis=1)
    pad = (-flat.shape[1]) % LANES
    return jnp.pad(flat, ((0, 0), (0, pad)))


def _unpack_small(flat, like):
    out, off = {}, 0
    for n in SMALL_NAMES:
        size = like[n].size
        out[n] = flat[:, off:off + size].reshape(like[n].shape)
        off += size
    return out


def kernel(x, mem, norm_gain, mem_norm_gain, w_in, b_forget, q_gain_a, k_gain_a, sinks_a, q_gain_b, k_gain_b, q_gain_c, k_gain_c, w_mem_kv, w_branch_a, w_branch_b, w_branch_c, w_out, loss_target, m_norm_gain, m_mem_norm_gain, m_w_in, m_b_forget, m_q_gain_a, m_k_gain_a, m_sinks_a, m_q_gain_b, m_k_gain_b, m_q_gain_c, m_k_gain_c, m_w_mem_kv, m_w_branch_a, m_w_branch_b, m_w_branch_c, m_w_out, v_norm_gain, v_mem_norm_gain, v_w_in, v_b_forget, v_q_gain_a, v_k_gain_a, v_sinks_a, v_q_gain_b, v_k_gain_b, v_q_gain_c, v_k_gain_c, v_w_mem_kv, v_w_branch_a, v_w_branch_b, v_w_branch_c, v_w_out):
    weights = dict(norm_gain=norm_gain, mem_norm_gain=mem_norm_gain, w_in=w_in, b_forget=b_forget, q_gain_a=q_gain_a,
                   k_gain_a=k_gain_a, sinks_a=sinks_a, q_gain_b=q_gain_b, k_gain_b=k_gain_b, q_gain_c=q_gain_c,
                   k_gain_c=k_gain_c, w_mem_kv=w_mem_kv, w_branch_a=w_branch_a, w_branch_b=w_branch_b,
                   w_branch_c=w_branch_c, w_out=w_out)
    mom_m = dict(norm_gain=m_norm_gain, mem_norm_gain=m_mem_norm_gain, w_in=m_w_in, b_forget=m_b_forget,
                 q_gain_a=m_q_gain_a, k_gain_a=m_k_gain_a, sinks_a=m_sinks_a, q_gain_b=m_q_gain_b, k_gain_b=m_k_gain_b,
                 q_gain_c=m_q_gain_c, k_gain_c=m_k_gain_c, w_mem_kv=m_w_mem_kv, w_branch_a=m_w_branch_a,
                 w_branch_b=m_w_branch_b, w_branch_c=m_w_branch_c, w_out=m_w_out)
    mom_v = dict(norm_gain=v_norm_gain, mem_norm_gain=v_mem_norm_gain, w_in=v_w_in, b_forget=v_b_forget,
                 q_gain_a=v_q_gain_a, k_gain_a=v_k_gain_a, sinks_a=v_sinks_a, q_gain_b=v_q_gain_b, k_gain_b=v_k_gain_b,
                 q_gain_c=v_q_gain_c, k_gain_c=v_k_gain_c, w_mem_kv=v_w_mem_kv, w_branch_a=v_w_branch_a,
                 w_branch_b=v_w_branch_b, w_branch_c=v_w_branch_c, w_out=v_w_out)
    wi = w_in[0]
    sh_qkv = jnp.concatenate([wi[:, a:b] for a, b in SRC_RANGES[0:3]], axis=1).astype(BF16)
    sh_zg = jnp.concatenate([wi[:, a:b] for a, b in SRC_RANGES[3:6]] + [wi[:, SRC_GATE:]], axis=1).astype(BF16)
    sh_wf = jnp.pad(wi[:, FB_SRC:FB_SRC + B_HEADS], ((0, 0), (0, FB_PAD - B_HEADS))).astype(BF16)
    shards = {"zg": sh_zg, "wo": w_out[0].astype(BF16), "wa": w_branch_a[0].astype(BF16),
              "wb": w_branch_b[0].astype(BF16), "wc": w_branch_c[0].astype(BF16)}
    first = ("qkv", "wf", "wk")
    full = _all_gather([sh_qkv, sh_wf, w_mem_kv[0].astype(BF16)], "weights_all_gather")
    wg = {kname: arr.reshape(arr.shape[0] * arr.shape[1], arr.shape[2]) for kname, arr in zip(first, full)}

    small = {n: weights[n] for n in SMALL_NAMES}
    loss_local, grad_x, small_g, parts = _local_step(x[0], mem[0], loss_target[0], small, wg, shards)

    grads, delta, new_m, new_v = {}, {}, {}, {}
    for n, kname in (("w_mem_kv", "wk"), ("w_out", "wo"), ("w_branch_a", "wa"), ("w_branch_b", "wb"), ("w_branch_c", "wc")):
        gsum, dlt, nm, nv = _adamw_parts(weights[n][0], parts[kname], mom_m[n][0], mom_v[n][0], "adamw_" + n)
        grads[n], delta[n], new_m[n], new_v[n] = gsum, dlt[None], nm[None], nv[None]
    gq, gz, gf, gg = (_sum_parts(parts[k], "grad_sum_" + k) for k in ("wm_qkv", "wm_z", "wf", "wm_g"))
    g_in = jnp.concatenate([gq[:, COL_QA:COL_QB], gz[:, COL_ZA:COL_ZB], gq[:, COL_QB:COL_QC], gz[:, COL_ZB:COL_ZC],
                            gf[:, 0:B_HEADS], gq[:, COL_QC:W_QKV], gz[:, COL_ZC:W_Z], gg], axis=1)
    dlt, nm, nv = _adamw(w_in[0], g_in, m_w_in[0], v_w_in[0], "adamw_w_in")
    grads["w_in"], delta["w_in"], new_m["w_in"], new_v["w_in"] = g_in, dlt[None], nm[None], nv[None]

    packed = _pack_small(small_g)
    reduced = _all_reduce_small(jnp.broadcast_to(packed, (8, packed.shape[1])), "small_all_reduce")
    grads.update(_unpack_small(reduced, small))

    loss = lax.psum(loss_local, ("x", "y", "c"))

    pw, pm, pv = _pack_small(small), _pack_small({n: mom_m[n] for n in SMALL_NAMES}), _pack_small({n: mom_v[n] for n in SMALL_NAMES})
    rep8 = lambda a: jnp.broadcast_to(a, (8, a.shape[1]))
    dlt, nm, nv = _adamw(rep8(pw), rep8(reduced), rep8(pm), rep8(pv), "adamw_small")
    for tree, flat in ((delta, dlt), (new_m, nm), (new_v, nv)):
        tree.update(_unpack_small(flat[0:1], small))
    for n in BIG_NAMES:
        grads[n] = grads[n][None]
    return (loss, grad_x[None], *[grads[n] for n in WEIGHT_ORDER], *[delta[n] for n in WEIGHT_ORDER],
            *[new_m[n] for n in WEIGHT_ORDER], *[new_v[n] for n in WEIGHT_ORDER])
```

```python
import math

import jax
import jax.numpy as jnp
import numpy as np
from jax import lax
from jax.experimental import pallas as pl
from jax.experimental.pallas import tpu as pltpu

F32 = jnp.float32
BF16 = jnp.bfloat16

N_DEV = 8
HEAD_DIM = 64
A_Q_HEADS = 12
A_KV_HEADS = 4
A_GROUP = 3
B_HEADS = 12
C_HEADS = 4
C_HEAD_DIM = 128
WINDOW = 128
A_WIDTH = 768
A_KV_WIDTH = 256
B_WIDTH = 768
C_WIDTH = 512
EPS = 1e-6
NEG = -1e30

COL_QA, COL_KA, COL_VA = 0, 768, 1024
COL_QB, COL_KB, COL_VB = 1280, 2048, 2816
COL_QC = 3584
W_QKV = 4096
COL_ZA, COL_ZB, COL_ZC = 0, 768, 1536
COL_GATE = W_Z = 2048
SRC_RANGES = ((0, 1280), (2048, 4352), (5132, 5644), (1280, 2048), (4352, 5120), (5644, 6156))
SRC_GATE = 6156
FB_SRC = 5120
FB_PAD = 128

ADAM_LR = 0.001
ADAM_B1 = 0.9
ADAM_B2 = 0.999
ADAM_EPS = 1e-08
ADAM_WD = 0.01
ADAM_STEP = 10

VMEM_BIG = 52 * 1024 * 1024
LANES = 128
MESH = pl.DeviceIdType.MESH


def _tile(n, pref, mult=128):
    if n <= pref:
        return n
    t = (pref // mult) * mult
    while t >= mult:
        if n % t == 0:
            return t
        t -= mult
    return n


def _params(sem=None, vmem=None):
    kw = {}
    if sem is not None:
        kw["dimension_semantics"] = sem
    if vmem is not None:
        kw["vmem_limit_bytes"] = vmem
    return pltpu.CompilerParams(**kw)


def _sigmoid(x):
    return 1.0 / (1.0 + jnp.exp(-x))


def _block_diag(hd):
    r = np.arange(LANES)
    return jnp.asarray((r[:, None] // hd) == (r[None, :] // hd), dtype=BF16)


def _seg_sum(t, bd):
    hi = t.astype(BF16)
    lo = (t - hi.astype(F32)).astype(BF16)
    outs = []
    for c in range(t.shape[1] // LANES):
        sl = slice(c * LANES, (c + 1) * LANES)
        outs.append(jnp.dot(hi[:, sl], bd, preferred_element_type=F32) + jnp.dot(lo[:, sl], bd, preferred_element_type=F32))
    return outs[0] if len(outs) == 1 else jnp.concatenate(outs, axis=1)


def _rmsnorm_fwd(x, gain, name):
    rows, d = x.shape
    bm = _tile(rows, 512, 8)

    def body(x_ref, g_ref, o_ref):
        xv = x_ref[...]
        ms = jnp.mean(xv * xv, axis=-1, keepdims=True)
        o_ref[...] = (xv * lax.rsqrt(ms + EPS) * g_ref[...]).astype(BF16)

    return pl.pallas_call(
        body, name=name, grid=(rows // bm,),
        in_specs=[pl.BlockSpec((bm, d), lambda i: (i, 0)), pl.BlockSpec((1, d), lambda i: (0, 0))],
        out_specs=pl.BlockSpec((bm, d), lambda i: (i, 0)),
        out_shape=jax.ShapeDtypeStruct((rows, d), BF16),
        compiler_params=_params(("parallel",)),
    )(x, gain)


def _rmsnorm_bwd(x, dhn, gain, dy, name):
    rows, d = x.shape
    bm = _tile(rows, 256, 8)
    with_dx = dy is not None

    def body(*refs):
        if with_dx:
            x_ref, dh_ref, g_ref, dy_ref, gx_ref, dg_ref = refs
        else:
            x_ref, dh_ref, g_ref, dg_ref = refs
        i = pl.program_id(0)
        xv = x_ref[...]
        rstd = lax.rsqrt(jnp.mean(xv * xv, axis=-1, keepdims=True) + EPS)
        xhat = xv * rstd
        dh = dh_ref[...]
        part = jnp.sum((dh * xhat).reshape(bm // 8, 8, d), axis=0)

        @pl.when(i == 0)
        def _():
            dg_ref[...] = part

        @pl.when(i > 0)
        def _():
            dg_ref[...] += part

        if with_dx:
            g = dh * g_ref[...]
            mean = jnp.mean(g * xhat, axis=-1, keepdims=True)
            gx_ref[...] = dy_ref[...] + rstd * (g - xhat * mean)

    row_spec = pl.BlockSpec((bm, d), lambda i: (i, 0))
    in_specs = [row_spec, row_spec, pl.BlockSpec((1, d), lambda i: (0, 0))]
    args = [x, dhn, gain]
    dg_spec = pl.BlockSpec((8, d), lambda i: (0, 0))
    dg_shape = jax.ShapeDtypeStruct((8, d), F32)
    if with_dx:
        in_specs.append(row_spec)
        args.append(dy)
        out_specs = [row_spec, dg_spec]
        out_shape = [jax.ShapeDtypeStruct((rows, d), F32), dg_shape]
    else:
        out_specs = [dg_spec]
        out_shape = [dg_shape]
    outs = pl.pallas_call(
        body, name=name, grid=(rows // bm,), in_specs=in_specs, out_specs=out_specs, out_shape=out_shape,
        compiler_params=_params(("arbitrary",)),
    )(*args)
    return outs if with_dx else (None, outs[0])


class _Comm:
    def __init__(self, kind, arrays):
        self.kind = kind
        self.arrays = list(arrays)
        self.n = len(self.arrays)

    def out_shapes(self):
        if self.kind == "gather":
            return [jax.ShapeDtypeStruct((N_DEV,) + a.shape, a.dtype) for a in self.arrays]
        return [jax.ShapeDtypeStruct(a.shape, a.dtype) for a in self.arrays]

    def scratch(self):
        return [pltpu.SemaphoreType.DMA((self.n, N_DEV - 1)), pltpu.SemaphoreType.DMA((self.n, N_DEV - 1)),
                pltpu.SemaphoreType.DMA((self.n,))]

    def _plan(self, ins, outs, sems, with_recvs):
        send_sems, recv_sems, local_sems = sems
        x, y, c = lax.axis_index("x"), lax.axis_index("y"), lax.axis_index("c")
        my = 4 * x + 2 * y + c
        gather = self.kind == "gather"
        local, sends, recvs = [], [], []
        for a in range(self.n):
            local.append(pltpu.make_async_copy(ins[a] if gather else ins[a].at[my], outs[a].at[my], local_sems.at[a]))
            for k in range(1, N_DEV):
                peer = (x ^ ((k >> 2) & 1), y ^ ((k >> 1) & 1), c ^ (k & 1))
                pid = 4 * peer[0] + 2 * peer[1] + peer[2]
                src = ins[a] if gather else ins[a].at[pid]
                sem = dict(send_sem=send_sems.at[a, k - 1], recv_sem=recv_sems.at[a, k - 1], device_id=peer, device_id_type=MESH)
                sends.append(pltpu.make_async_remote_copy(src_ref=src, dst_ref=outs[a].at[my], **sem))
                if with_recvs:
                    recvs.append(pltpu.make_async_remote_copy(src_ref=src, dst_ref=outs[a].at[pid], **sem))
        return local, sends, recvs

    def start(self, ins, outs, sems):
        local, sends, _ = self._plan(ins, outs, sems, False)
        for cp in local + sends:
            cp.start()

    def wait(self, ins, outs, sems):
        local, sends, recvs = self._plan(ins, outs, sems, True)
        for cp in recvs:
            cp.wait_recv()
        for cp in sends:
            cp.wait_send()
        for cp in local:
            cp.wait()


def _grid_edges(grid):
    first = last = None
    for ax, size in enumerate(grid):
        pid = pl.program_id(ax)
        f, l = pid == 0, pid == size - 1
        first = f if first is None else first & f
        last = l if last is None else last & l
    return first, last


def _hosted_call(body, comm, *, name, grid, in_specs, out_specs, out_shape, scratch_shapes, args, sem, vmem=None):
    in_specs, out_specs, out_shape, scratch_shapes = list(in_specs), list(out_specs), list(out_shape), list(scratch_shapes)
    if comm is None:
        res = pl.pallas_call(body, name=name, grid=grid, in_specs=in_specs, out_specs=out_specs, out_shape=out_shape,
                             scratch_shapes=scratch_shapes, compiler_params=_params(sem, vmem))(*args)
        return list(res), []
    n_in, n_out, n_scr, nc = len(in_specs), len(out_shape), len(scratch_shapes), comm.n

    def hosted(*refs):
        ins = refs[0:n_in]
        comm_in = refs[n_in:n_in + nc]
        outs = refs[n_in + nc:n_in + nc + n_out]
        comm_out = refs[n_in + nc + n_out:n_in + 2 * nc + n_out]
        scr = refs[n_in + 2 * nc + n_out:n_in + 2 * nc + n_out + n_scr]
        sems = refs[n_in + 2 * nc + n_out + n_scr:]
        first, last = _grid_edges(grid)

        @pl.when(first)
        def _():
            comm.start(comm_in, comm_out, sems)

        body(*ins, *outs, *scr)

        @pl.when(last)
        def _():
            comm.wait(comm_in, comm_out, sems)

    any_spec = pl.BlockSpec(memory_space=pl.ANY)
    res = pl.pallas_call(
        hosted, name=name, grid=grid, in_specs=in_specs + [any_spec] * nc, out_specs=out_specs + [any_spec] * nc,
        out_shape=out_shape + comm.out_shapes(), scratch_shapes=scratch_shapes + comm.scratch(),
        compiler_params=_params(("arbitrary",) * len(grid), vmem),
    )(*args, *comm.arrays)
    return list(res[0:n_out]), list(res[n_out:])


def _mm(a, b, *, grid, a_spec, b_spec, o_spec, o_shape, o_dtype, contract, name, add=None, add_spec=None, acc_shape=None,
        comm=None):
    nk = grid[2]
    has_add = add is not None

    def body(*refs):
        a_ref, b_ref = refs[0], refs[1]
        add_ref = refs[2] if has_add else None
        o_ref = refs[3] if has_add else refs[2]
        part = lax.dot_general(a_ref[...], b_ref[...], (contract, ((), ())), preferred_element_type=F32)
        if nk == 1:
            if has_add:
                part = part + add_ref[...]
            o_ref[...] = part.astype(o_dtype)
        else:
            acc = refs[-1]
            k = pl.program_id(2)

            @pl.when(k == 0)
            def _():
                acc[...] = part

            @pl.when(k > 0)
            def _():
                acc[...] += part

            @pl.when(k == nk - 1)
            def _():
                r = acc[...]
                if has_add:
                    r = r + add_ref[...]
                o_ref[...] = r.astype(o_dtype)

    in_specs = [a_spec, b_spec] + ([add_spec] if has_add else [])
    args = [a, b] + ([add] if has_add else [])
    scratch = [pltpu.VMEM(acc_shape, F32)] if nk > 1 else []
    outs, comm_outs = _hosted_call(
        body, comm, name=name, grid=grid, in_specs=in_specs, out_specs=[o_spec],
        out_shape=[jax.ShapeDtypeStruct(o_shape, o_dtype)], scratch_shapes=scratch, args=args,
        sem=("parallel", "parallel", "arbitrary"), vmem=VMEM_BIG)
    return outs[0] if comm is None else (outs[0], comm_outs)


def _mm_nn(a, b, *, bm, bn, bk, o_dtype, name, add=None, comm=None):
    m, kd = a.shape
    n = b.shape[1]
    bm, bn, bk = _tile(m, bm, 8), _tile(n, bn), _tile(kd, bk)
    o_spec = pl.BlockSpec((bm, bn), lambda i, j, k: (i, j))
    return _mm(a, b, grid=(m // bm, n // bn, kd // bk),
               a_spec=pl.BlockSpec((bm, bk), lambda i, j, k: (i, k)),
               b_spec=pl.BlockSpec((bk, bn), lambda i, j, k: (k, j)),
               o_spec=o_spec, o_shape=(m, n), o_dtype=o_dtype, contract=((1,), (0,)), name=name,
               add=add, add_spec=o_spec, acc_shape=(bm, bn), comm=comm)


def _mm_nt(a, b, *, bm, bn, bk, o_dtype, name, add=None, b_col0=0, comm=None):
    m, kd = a.shape
    n = b.shape[0]
    bm, bn, bk = _tile(m, bm, 8), _tile(n, bn), _tile(math.gcd(kd, b_col0), bk)
    kb0 = b_col0 // bk
    o_spec = pl.BlockSpec((bm, bn), lambda i, j, k: (i, j))
    return _mm(a, b, grid=(m // bm, n // bn, kd // bk),
               a_spec=pl.BlockSpec((bm, bk), lambda i, j, k: (i, k)),
               b_spec=pl.BlockSpec((bn, bk), lambda i, j, k: (j, kb0 + k)),
               o_spec=o_spec, o_shape=(m, n), o_dtype=o_dtype, contract=((1,), (1,)), name=name,
               add=add, add_spec=o_spec, acc_shape=(bm, bn), comm=comm)


def _mm_nt_cat(a1, a2, b, *, bm, bn, bk, name, add, comm=None):
    m, k1 = a1.shape
    k2 = a2.shape[1]
    n = b.shape[0]
    bm, bn, bk = _tile(m, bm, 8), _tile(n, bn), _tile(math.gcd(k1, k2), bk)
    n1, nk = k1 // bk, (k1 + k2) // bk
    nt = (((1,), (1,)), ((), ()))

    def body(a1_ref, a2_ref, b_ref, add_ref, o_ref, acc):
        k = pl.program_id(2)

        def accumulate(part):
            @pl.when(k == 0)
            def _():
                acc[...] = part

            @pl.when(k > 0)
            def _():
                acc[...] += part

        @pl.when(k < n1)
        def _():
            accumulate(lax.dot_general(a1_ref[...], b_ref[...], nt, preferred_element_type=F32))

        @pl.when(k >= n1)
        def _():
            accumulate(lax.dot_general(a2_ref[...], b_ref[...], nt, preferred_element_type=F32))

        @pl.when(k == nk - 1)
        def _():
            o_ref[...] = acc[...] + add_ref[...]

    o_spec = pl.BlockSpec((bm, bn), lambda i, j, k: (i, j))
    outs, comm_outs = _hosted_call(
        body, comm, name=name, grid=(m // bm, n // bn, nk),
        in_specs=[pl.BlockSpec((bm, bk), lambda i, j, k: (i, jnp.minimum(k, n1 - 1))),
                  pl.BlockSpec((bm, bk), lambda i, j, k: (i, jnp.maximum(k - n1, 0))),
                  pl.BlockSpec((bn, bk), lambda i, j, k: (j, k)), o_spec],
        out_specs=[o_spec], out_shape=[jax.ShapeDtypeStruct((m, n), F32)],
        scratch_shapes=[pltpu.VMEM((bm, bn), F32)], args=[a1, a2, b, add],
        sem=("parallel", "parallel", "arbitrary"), vmem=VMEM_BIG)
    return outs[0] if comm is None else (outs[0], comm_outs)


def _mm_tn(a, b, *, bm, bn, bk, o_dtype, name, comm=None):
    kd, m = a.shape
    n = b.shape[1]
    bm, bn, bk = _tile(m, bm), _tile(n, bn), _tile(kd, bk, 8)
    return _mm(a, b, grid=(m // bm, n // bn, kd // bk),
               a_spec=pl.BlockSpec((bk, bm), lambda i, j, k: (k, i)),
               b_spec=pl.BlockSpec((bk, bn), lambda i, j, k: (k, j)),
               o_spec=pl.BlockSpec((bm, bn), lambda i, j, k: (i, j)),
               o_shape=(m, n), o_dtype=o_dtype, contract=((0,), (0,)), name=name, acc_shape=(bm, bn), comm=comm)


def _mm_branch_fwd(s, w2d, name):
    m, kb = s.shape
    ds = w2d.shape[1]
    bm = _tile(m, 1024, 8)
    return _mm(s, w2d, grid=(m // bm, N_DEV, 1),
               a_spec=pl.BlockSpec((bm, kb), lambda i, j, k: (i, 0)),
               b_spec=pl.BlockSpec((kb, ds), lambda i, j, k: (j, 0)),
               o_spec=pl.BlockSpec((bm, ds), lambda i, j, k: (i, j)),
               o_shape=(m, N_DEV * ds), o_dtype=BF16, contract=((1,), (0,)), name=name)


def _mm_branch_bwd_act(du, w2d, kb, name):
    m = du.shape[0]
    ds = w2d.shape[1]
    bm = _tile(m, 1024, 8)
    return _mm(du, w2d, grid=(m // bm, 1, N_DEV),
               a_spec=pl.BlockSpec((bm, ds), lambda i, j, k: (i, k)),
               b_spec=pl.BlockSpec((kb, ds), lambda i, j, k: (k, 0)),
               o_spec=pl.BlockSpec((bm, kb), lambda i, j, k: (i, 0)),
               o_shape=(m, kb), o_dtype=F32, contract=((1,), (1,)), name=name, acc_shape=(bm, kb))


def _mm_branch_bwd_w(s, du, name):
    m, kb = s.shape
    ds = du.shape[1] // N_DEV
    bk = _tile(m, 2048, 8)
    return _mm(s, du, grid=(1, N_DEV, m // bk),
               a_spec=pl.BlockSpec((bk, kb), lambda i, j, k: (k, 0)),
               b_spec=pl.BlockSpec((bk, ds), lambda i, j, k: (k, j)),
               o_spec=pl.BlockSpec((kb, ds), lambda i, j, k: (j, 0)),
               o_shape=(N_DEV * kb, ds), o_dtype=BF16, contract=((0,), (0,)), name=name, acc_shape=(kb, ds))


def _headnorm_fwd(src, c0, width, bw, hd, gain, nflag, head_major, name):
    rows = src.shape[0]
    bm = _tile(rows, 1024, 8)
    bd = _block_diag(hd)
    cb0 = c0 // bw

    def body(x_ref, g_ref, f_ref, bd_ref, o_ref):
        xv = x_ref[...].astype(F32)
        ss = _seg_sum(xv * xv, bd_ref[...])
        rstd = lax.rsqrt(ss * (1.0 / hd) + EPS)
        y = (xv * jnp.where(f_ref[...] > 0.0, rstd, 1.0) * g_ref[...]).astype(BF16)
        if head_major:
            for h in range(bw // HEAD_DIM):
                o_ref[h] = y[:, h * HEAD_DIM:(h + 1) * HEAD_DIM]
        else:
            o_ref[...] = y

    vec_spec = pl.BlockSpec((1, bw), lambda i, t: (0, t))
    if head_major:
        hpb = bw // HEAD_DIM
        out_spec = pl.BlockSpec((hpb, bm, HEAD_DIM), lambda i, t: (t, i, 0))
        out_shape = jax.ShapeDtypeStruct((width // HEAD_DIM, rows, HEAD_DIM), BF16)
    else:
        out_spec = pl.BlockSpec((bm, bw), lambda i, t: (i, t))
        out_shape = jax.ShapeDtypeStruct((rows, width), BF16)
    return pl.pallas_call(
        body, name=name, grid=(rows // bm, width // bw),
        in_specs=[pl.BlockSpec((bm, bw), lambda i, t: (i, cb0 + t)), vec_spec, vec_spec,
                  pl.BlockSpec((LANES, LANES), lambda i, t: (0, 0))],
        out_specs=out_spec, out_shape=out_shape,
        compiler_params=_params(("parallel", "parallel")),
    )(src, gain, nflag, bd)


def _headnorm_bwd(src, c0, width, bw, hd, gain, nflag, dyn, target, t0, name):
    rows = src.shape[0]
    bm = _tile(rows, 1024, 8)
    bd = _block_diag(hd)
    cb0 = c0 // bw
    tb0 = t0 // bw
    aliased = target is not None

    def body(*refs):
        if aliased:
            x_ref, dy_ref, g_ref, f_ref, bd_ref, _, o_ref, dg_ref = refs
        else:
            x_ref, dy_ref, g_ref, f_ref, bd_ref, o_ref, dg_ref = refs
        i = pl.program_id(1)
        xv = x_ref[...].astype(F32)
        dyv = dy_ref[...]
        bdv = bd_ref[...]
        rstd = lax.rsqrt(_seg_sum(xv * xv, bdv) * (1.0 / hd) + EPS)
        xhat = xv * rstd
        g = dyv * g_ref[...]
        mean = _seg_sum(g * xhat, bdv) * (1.0 / hd)
        dx = jnp.where(f_ref[...] > 0.0, rstd * (g - xhat * mean), g)
        o_ref[...] = dx.astype(BF16)
        part = jnp.sum((dyv * xhat).reshape(bm // 8, 8, bw), axis=0)

        @pl.when(i == 0)
        def _():
            dg_ref[...] = part

        @pl.when(i > 0)
        def _():
            dg_ref[...] += part

    vec_spec = pl.BlockSpec((1, bw), lambda t, i: (0, t))
    in_specs = [pl.BlockSpec((bm, bw), lambda t, i: (i, cb0 + t)), pl.BlockSpec((bm, bw), lambda t, i: (i, t)),
                vec_spec, vec_spec, pl.BlockSpec((LANES, LANES), lambda t, i: (0, 0))]
    args = [src, dyn, gain, nflag, bd]
    aliases = {}
    if aliased:
        in_specs.append(pl.BlockSpec(memory_space=pl.ANY))
        args.append(target)
        aliases = {5: 0}
        o_shape = jax.ShapeDtypeStruct(target.shape, BF16)
    else:
        o_shape = jax.ShapeDtypeStruct((rows, width), BF16)
    out, dg = pl.pallas_call(
        body, name=name, grid=(width // bw, rows // bm), in_specs=in_specs,
        out_specs=[pl.BlockSpec((bm, bw), lambda t, i: (i, tb0 + t)), pl.BlockSpec((8, bw), lambda t, i: (0, t))],
        out_shape=[o_shape, jax.ShapeDtypeStruct((8, width), F32)],
        input_output_aliases=aliases,
        compiler_params=_params(("parallel", "arbitrary")),
    )(*args)
    return out, dg


def _fox_prep(pfb, bpad, name):
    s = pfb.shape[0]

    def body(p_ref, b_ref, c_ref):
        z = p_ref[...] + b_ref[...]
        logf = jnp.minimum(z, 0.0) - jnp.log(1.0 + jnp.exp(-jnp.abs(z)))
        x = logf.T[0:16, :]
        lane = lax.broadcasted_iota(jnp.int32, (16, s), 1)
        sh = 1
        while sh < s:
            x = x + jnp.where(lane >= sh, pltpu.roll(x, sh, 1), 0.0)
            sh *= 2
        c_ref[...] = x

    return pl.pallas_call(
        body, name=name, grid=(1,),
        in_specs=[pl.BlockSpec((s, FB_PAD), lambda i: (0, 0)), pl.BlockSpec((1, FB_PAD), lambda i: (0, 0))],
        out_specs=pl.BlockSpec((16, s), lambda i: (0, 0)),
        out_shape=jax.ShapeDtypeStruct((16, s), F32),
        compiler_params=_params(("arbitrary",)),
    )(pfb, bpad)


def _fox_prep_bwd(pfb, bpad, dct, name):
    s = pfb.shape[0]

    def body(p_ref, b_ref, dc_ref, df_ref, db_ref):
        zt = (p_ref[...] + b_ref[...]).T[0:16, :]
        y = dc_ref[...]
        lane = lax.broadcasted_iota(jnp.int32, (16, s), 1)
        sh = 1
        while sh < s:
            y = y + jnp.where(lane < s - sh, pltpu.roll(y, s - sh, 1), 0.0)
            sh *= 2
        dz = y * _sigmoid(-zt)
        db_ref[...] = jnp.broadcast_to(jnp.sum(dz, axis=1, keepdims=True), (16, FB_PAD))
        full = jnp.concatenate([dz, jnp.zeros((FB_PAD - 16, s), F32)], axis=0)
        df_ref[...] = full.T.astype(BF16)

    return pl.pallas_call(
        body, name=name, grid=(1,),
        in_specs=[pl.BlockSpec((s, FB_PAD), lambda i: (0, 0)), pl.BlockSpec((1, FB_PAD), lambda i: (0, 0)),
                  pl.BlockSpec((16, s), lambda i: (0, 0))],
        out_specs=[pl.BlockSpec((s, FB_PAD), lambda i: (0, 0)), pl.BlockSpec((16, FB_PAD), lambda i: (0, 0))],
        out_shape=[jax.ShapeDtypeStruct((s, FB_PAD), BF16), jax.ShapeDtypeStruct((16, FB_PAD), F32)],
        compiler_params=_params(("arbitrary",)),
    )(pfb, bpad, dct)


def _swa_window(n):
    ws = pl.multiple_of(jnp.maximum(n * WINDOW - WINDOW, 0), WINDOW)
    qi = lax.broadcasted_iota(jnp.int32, (WINDOW, 2 * WINDOW), 0)
    kj = lax.broadcasted_iota(jnp.int32, (WINDOW, 2 * WINDOW), 1)
    rel = qi + (n * WINDOW - ws) - kj
    valid = (rel >= 0) & (rel < WINDOW)
    return ws, valid, rel.astype(F32)


def _attn_a_fwd(q, k, v, sinks, slopes, name):
    s = q.shape[1]
    nb = s // WINDOW
    smem = pl.BlockSpec(memory_space=pltpu.SMEM)

    def body(sink_ref, slope_ref, q_ref, k_ref, v_ref, o_ref, lse_ref):
        n = pl.program_id(0)
        ws, valid, relf = _swa_window(n)
        outs = []
        for h in range(A_Q_HEADS):
            kvh = h // A_GROUP
            kw = k_ref[kvh, pl.ds(ws, 2 * WINDOW), :]
            vw = v_ref[kvh, pl.ds(ws, 2 * WINDOW), :]
            sc = lax.dot_general(q_ref[h], kw, (((1,), (1,)), ((), ())), preferred_element_type=F32)
            sc = jnp.where(valid, sc - slope_ref[h] * relf, NEG)
            sink = sink_ref[h]
            m = jnp.maximum(jnp.max(sc, axis=1, keepdims=True), sink)
            p = jnp.exp(sc - m)
            denom = jnp.sum(p, axis=1, keepdims=True) + jnp.exp(sink - m)
            pn = (p / denom).astype(BF16)
            outs.append(jnp.dot(pn, vw, preferred_element_type=F32))
            lse_ref[h] = jnp.broadcast_to(m + jnp.log(denom), (WINDOW, HEAD_DIM))
        o_ref[...] = jnp.concatenate(outs, axis=1)

    return pl.pallas_call(
        body, name=name, grid=(nb,),
        in_specs=[smem, smem,
                  pl.BlockSpec((A_Q_HEADS, WINDOW, HEAD_DIM), lambda n: (0, n, 0)),
                  pl.BlockSpec((A_KV_HEADS, s, HEAD_DIM), lambda n: (0, 0, 0)),
                  pl.BlockSpec((A_KV_HEADS, s, HEAD_DIM), lambda n: (0, 0, 0))],
        out_specs=[pl.BlockSpec((WINDOW, A_WIDTH), lambda n: (n, 0)),
                   pl.BlockSpec((A_Q_HEADS, WINDOW, HEAD_DIM), lambda n: (0, n, 0))],
        out_shape=[jax.ShapeDtypeStruct((s, A_WIDTH), F32), jax.ShapeDtypeStruct((A_Q_HEADS, s, HEAD_DIM), F32)],
        compiler_params=_params(("parallel",), VMEM_BIG),
    )(sinks, slopes, q, k, v)


def _attn_a_bwd(q, k, v, do, lse, dd, sinks, slopes, name, comm=None):
    s = q.shape[1]
    nb = s // WINDOW
    smem = pl.BlockSpec(memory_space=pltpu.SMEM)
    last = nb - 1

    def body(sink_ref, slope_ref, q_ref, k_ref, v_ref, do_ref, lse_ref, dd_ref, dq_ref, dkv_ref, ds_ref, carry):
        n = pl.program_id(0)

        @pl.when(n == 0)
        def _():
            carry[...] = jnp.zeros(carry.shape, F32)
            ds_ref[...] = jnp.zeros(ds_ref.shape, F32)

        @pl.when(n < nb)
        def _():
            ws, valid, relf = _swa_window(n)
            dqs = []
            dkw = [None] * A_KV_HEADS
            dvw = [None] * A_KV_HEADS
            for h in range(A_Q_HEADS):
                kvh = h // A_GROUP
                qh = q_ref[h]
                doh = do_ref[h]
                kw = k_ref[kvh, pl.ds(ws, 2 * WINDOW), :]
                vw = v_ref[kvh, pl.ds(ws, 2 * WINDOW), :]
                lse_h = lse_ref[h]
                dd_h = dd_ref[h]
                sc = lax.dot_general(qh, kw, (((1,), (1,)), ((), ())), preferred_element_type=F32)
                sc = jnp.where(valid, sc - slope_ref[h] * relf, NEG)
                p = jnp.exp(sc - lse_h[:, 0:1])
                dp = lax.dot_general(doh, vw, (((1,), (1,)), ((), ())), preferred_element_type=F32)
                dsc = (p * (dp - dd_h[:, 0:1])).astype(BF16)
                pb = p.astype(BF16)
                dqs.append(jnp.dot(dsc, kw, preferred_element_type=F32))
                dk_h = lax.dot_general(dsc, qh, (((0,), (0,)), ((), ())), preferred_element_type=F32)
                dv_h = lax.dot_general(pb, doh, (((0,), (0,)), ((), ())), preferred_element_type=F32)
                dkw[kvh] = dk_h if dkw[kvh] is None else dkw[kvh] + dk_h
                dvw[kvh] = dv_h if dvw[kvh] is None else dvw[kvh] + dv_h
                psink = jnp.exp(sink_ref[h] - lse_h)
                ds_ref[h] += jnp.sum((-psink * dd_h).reshape(WINDOW // 8, 8, HEAD_DIM), axis=0)
            dq_ref[...] = jnp.concatenate(dqs, axis=1)
            win = jnp.concatenate(dkw + dvw, axis=1)
            first = win[0:WINDOW]
            second = win[WINDOW:2 * WINDOW]
            dkv_ref[...] = carry[...] + first
            carry[...] = jnp.where(n == 0, first, second)

        @pl.when(n == nb)
        def _():
            dkv_ref[...] = carry[...]

    hm = lambda heads: pl.BlockSpec((heads, WINDOW, HEAD_DIM), lambda n: (0, jnp.minimum(n, last), 0))
    res = lambda heads: pl.BlockSpec((heads, s, HEAD_DIM), lambda n: (0, 0, 0))
    outs, comm_outs = _hosted_call(
        body, comm, name=name, grid=(nb + 1,),
        in_specs=[smem, smem, hm(A_Q_HEADS), res(A_KV_HEADS), res(A_KV_HEADS), hm(A_Q_HEADS), hm(A_Q_HEADS), hm(A_Q_HEADS)],
        out_specs=[pl.BlockSpec((WINDOW, A_WIDTH), lambda n: (jnp.minimum(n, last), 0)),
                   pl.BlockSpec((WINDOW, 2 * A_KV_WIDTH), lambda n: (jnp.maximum(n - 1, 0), 0)),
                   pl.BlockSpec((A_Q_HEADS, 8, HEAD_DIM), lambda n: (0, 0, 0))],
        out_shape=[jax.ShapeDtypeStruct((s, A_WIDTH), F32), jax.ShapeDtypeStruct((s, 2 * A_KV_WIDTH), F32),
                   jax.ShapeDtypeStruct((A_Q_HEADS, 8, HEAD_DIM), F32)],
        scratch_shapes=[pltpu.VMEM((WINDOW, 2 * A_KV_WIDTH), F32)],
        args=[sinks, slopes, q, k, v, do, lse, dd], sem=("arbitrary",), vmem=VMEM_BIG)
    return outs[0], outs[1], outs[2], comm_outs


def _attn_b_fwd(q, k, v, c3, name, comm=None):
    heads, s, _ = q.shape
    bq = min(512, s)
    nq = s // bq
    nt = (((1,), (1,)), ((), ()))

    def body(q_ref, k_ref, v_ref, c_ref, o_ref, lse_ref, m_scr, l_scr, acc_scr):
        i = pl.program_id(1)
        r0 = pl.multiple_of(i * bq, bq)
        row = lax.broadcasted_iota(jnp.int32, (bq, bq), 0)
        col = lax.broadcasted_iota(jnp.int32, (bq, bq), 1)
        m_scr[...] = jnp.full((2, bq, LANES), NEG, F32)
        l_scr[...] = jnp.zeros((2, bq, LANES), F32)
        acc_scr[...] = jnp.zeros((2, bq, HEAD_DIM), F32)

        def step(j, masked):
            k0 = pl.multiple_of(j * bq, bq)
            for h2 in range(2):
                kv = k_ref[h2, pl.ds(k0, bq), :]
                vv = v_ref[h2, pl.ds(k0, bq), :]
                cq0 = c_ref[h2, :, pl.ds(r0, LANES)][:, 0:1]
                sc = lax.dot_general(q_ref[h2], kv, nt, preferred_element_type=F32)
                sc = sc + (cq0 - c_ref[h2, :, pl.ds(k0, bq)])
                if masked:
                    sc = jnp.where(col <= row, sc, NEG)
                m_prev = m_scr[h2]
                m_new = jnp.maximum(m_prev, jnp.max(sc, axis=1, keepdims=True))
                alpha = jnp.exp(m_prev - m_new)
                p = jnp.exp(sc - m_new[:, 0:1])
                l_scr[h2] = alpha * l_scr[h2] + jnp.sum(p, axis=1, keepdims=True)
                p_hi = p.astype(BF16)
                p_lo = (p - p_hi.astype(F32)).astype(BF16)
                pv = jnp.dot(p_hi, vv, preferred_element_type=F32) + jnp.dot(p_lo, vv, preferred_element_type=F32)
                acc_scr[h2] = acc_scr[h2] * alpha[:, 0:HEAD_DIM] + pv
                m_scr[h2] = m_new

        def loop_body(j, carry):
            step(j, False)
            return carry

        lax.fori_loop(0, i, loop_body, 0)
        step(i, True)
        outs = []
        for h2 in range(2):
            l = l_scr[h2]
            outs.append(acc_scr[h2] / l[:, 0:HEAD_DIM])
            lse_ref[h2] = (m_scr[h2] + jnp.log(l))[:, 0:HEAD_DIM]
        o_ref[...] = jnp.concatenate(outs, axis=1)

    res = pl.BlockSpec((2, s, HEAD_DIM), lambda hp, i: (hp, 0, 0))
    outs, comm_outs = _hosted_call(
        body, comm, name=name, grid=(heads // 2, nq),
        in_specs=[pl.BlockSpec((2, bq, HEAD_DIM), lambda hp, i: (hp, i, 0)), res, res,
                  pl.BlockSpec((2, 1, s), lambda hp, i: (hp, 0, 0))],
        out_specs=[pl.BlockSpec((bq, 2 * HEAD_DIM), lambda hp, i: (i, hp)),
                   pl.BlockSpec((2, bq, HEAD_DIM), lambda hp, i: (hp, i, 0))],
        out_shape=[jax.ShapeDtypeStruct((s, heads * HEAD_DIM), F32), jax.ShapeDtypeStruct((heads, s, HEAD_DIM), F32)],
        scratch_shapes=[pltpu.VMEM((2, bq, LANES), F32), pltpu.VMEM((2, bq, LANES), F32), pltpu.VMEM((2, bq, HEAD_DIM), F32)],
        args=[q, k, v, c3], sem=("parallel", "parallel"), vmem=VMEM_BIG)
    return outs[0], outs[1], comm_outs


def _attn_b_bwd(q, k, v, do, lse, dd, c3, name, comm=None):
    heads, s, _ = q.shape
    bq = min(512, s)
    nq = s // bq
    nt = (((1,), (1,)), ((), ()))
    tn = (((0,), (0,)), ((), ()))
    grid = (heads // 2, nq)

    def body(q_ref, k_ref, v_ref, do_ref, lse_ref, dd_ref, c_ref, dq_ref, dk_ref, dv_ref, dc_ref,
             dq_scr, dk_scr, dv_scr, dc_scr):
        j = pl.program_id(1)
        k0 = pl.multiple_of(j * bq, bq)
        row = lax.broadcasted_iota(jnp.int32, (bq, bq), 0)
        col = lax.broadcasted_iota(jnp.int32, (bq, bq), 1)

        @pl.when(j == 0)
        def _():
            dq_scr[...] = jnp.zeros(dq_scr.shape, F32)

        dk_scr[...] = jnp.zeros((2, bq, HEAD_DIM), F32)
        dv_scr[...] = jnp.zeros((2, bq, HEAD_DIM), F32)
        dc_scr[...] = jnp.zeros((2, 1, bq), F32)

        def step(i, masked):
            r0 = pl.multiple_of(i * bq, bq)
            for h2 in range(2):
                kv = k_ref[h2]
                vv = v_ref[h2]
                qv = q_ref[h2, pl.ds(r0, bq), :]
                dov = do_ref[h2, pl.ds(r0, bq), :]
                lse_v = lse_ref[h2, pl.ds(r0, bq), :][:, 0:1]
                dd_v = dd_ref[h2, pl.ds(r0, bq), :][:, 0:1]
                cq0 = c_ref[h2, :, pl.ds(r0, LANES)][:, 0:1]
                sc = lax.dot_general(qv, kv, nt, preferred_element_type=F32) + (cq0 - c_ref[h2, :, pl.ds(k0, bq)])
                if masked:
                    sc = jnp.where(col <= row, sc, NEG)
                p = jnp.exp(sc - lse_v)
                dp = lax.dot_general(dov, vv, nt, preferred_element_type=F32)
                dsc = p * (dp - dd_v)
                dsb = dsc.astype(BF16)
                dv_scr[h2] += lax.dot_general(p.astype(BF16), dov, tn, preferred_element_type=F32)
                dk_scr[h2] += lax.dot_general(dsb, qv, tn, preferred_element_type=F32)
                dq_scr[h2, pl.ds(r0, bq), :] += jnp.dot(dsb, kv, preferred_element_type=F32)
                dc_scr[h2] -= jnp.sum(dsc, axis=0, keepdims=True)

        def loop_body(i, carry):
            step(i, False)
            return carry

        step(j, True)
        lax.fori_loop(j + 1, nq, loop_body, 0)
        dc_ref[...] = dc_scr[...]
        dk_ref[...] = jnp.concatenate([dk_scr[0], dk_scr[1]], axis=1)
        dv_ref[...] = jnp.concatenate([dv_scr[0], dv_scr[1]], axis=1)

        @pl.when(j == nq - 1)
        def _():
            dq_ref[...] = jnp.concatenate([dq_scr[0], dq_scr[1]], axis=1)

    res = pl.BlockSpec((2, s, HEAD_DIM), lambda hp, j: (hp, 0, 0))
    blk = pl.BlockSpec((2, bq, HEAD_DIM), lambda hp, j: (hp, j, 0))
    tm = jax.ShapeDtypeStruct((s, heads * HEAD_DIM), F32)
    in_specs = [res, blk, blk, res, res, res, pl.BlockSpec((2, 1, s), lambda hp, j: (hp, 0, 0))]
    out_specs = [pl.BlockSpec((s, 2 * HEAD_DIM), lambda hp, j: (0, hp)),
                 pl.BlockSpec((bq, 2 * HEAD_DIM), lambda hp, j: (j, hp)),
                 pl.BlockSpec((bq, 2 * HEAD_DIM), lambda hp, j: (j, hp)),
                 pl.BlockSpec((2, 1, bq), lambda hp, j: (hp, 0, j))]
    out_shape = [tm, tm, tm, jax.ShapeDtypeStruct((heads, 1, s), F32)]
    scratch = [pltpu.VMEM((2, s, HEAD_DIM), F32), pltpu.VMEM((2, bq, HEAD_DIM), F32),
               pltpu.VMEM((2, bq, HEAD_DIM), F32), pltpu.VMEM((2, 1, bq), F32)]
    outs, comm_outs = _hosted_call(
        body, comm, name=name, grid=grid, in_specs=in_specs, out_specs=out_specs, out_shape=out_shape,
        scratch_shapes=scratch, args=[q, k, v, do, lse, dd, c3], sem=("parallel", "arbitrary"), vmem=VMEM_BIG)
    return outs[0], outs[1], outs[2], outs[3], comm_outs


def _attn_c_probs(qh, mkh):
    sc = lax.dot_general(qh, mkh, (((1,), (1,)), ((), ())), preferred_element_type=F32) * (C_HEAD_DIM ** -0.5)
    p = jnp.exp(sc - jnp.max(sc, axis=1, keepdims=True))
    return p / jnp.sum(p, axis=1, keepdims=True)


def _attn_c_fwd(q, mkv, name):
    s = q.shape[0]
    m = mkv.shape[0]
    bq = _tile(s, 512, 8)

    def body(q_ref, mk_ref, mv_ref, o_ref):
        outs = []
        for h in range(C_HEADS):
            sl = slice(h * C_HEAD_DIM, (h + 1) * C_HEAD_DIM)
            pn = _attn_c_probs(q_ref[:, sl], mk_ref[:, sl]).astype(BF16)
            outs.append(jnp.dot(pn, mv_ref[:, sl], preferred_element_type=F32))
        o_ref[...] = jnp.concatenate(outs, axis=1)

    return pl.pallas_call(
        body, name=name, grid=(s // bq,),
        in_specs=[pl.BlockSpec((bq, C_WIDTH), lambda i: (i, 0)), pl.BlockSpec((m, C_WIDTH), lambda i: (0, 0)),
                  pl.BlockSpec((m, C_WIDTH), lambda i: (0, 1))],
        out_specs=pl.BlockSpec((bq, C_WIDTH), lambda i: (i, 0)),
        out_shape=jax.ShapeDtypeStruct((s, C_WIDTH), F32),
        compiler_params=_params(("parallel",)),
    )(q, mkv, mkv)


def _attn_c_bwd(q, mkv, do, name):
    s = q.shape[0]
    m = mkv.shape[0]
    bq = _tile(s, 512, 8)
    tn = (((0,), (0,)), ((), ()))

    def body(q_ref, mk_ref, mv_ref, do_ref, dq_ref, dm_ref):
        i = pl.program_id(0)

        @pl.when(i == 0)
        def _():
            dm_ref[...] = jnp.zeros(dm_ref.shape, F32)

        dqs = []
        for h in range(C_HEADS):
            sl = slice(h * C_HEAD_DIM, (h + 1) * C_HEAD_DIM)
            qh, mkh, mvh, doh = q_ref[:, sl], mk_ref[:, sl], mv_ref[:, sl], do_ref[:, sl]
            pn = _attn_c_probs(qh, mkh)
            dp = lax.dot_general(doh, mvh, (((1,), (1,)), ((), ())), preferred_element_type=F32)
            dsc = (pn * (dp - jnp.sum(pn * dp, axis=1, keepdims=True)) * (C_HEAD_DIM ** -0.5)).astype(BF16)
            dqs.append(jnp.dot(dsc, mkh, preferred_element_type=F32))
            dm_ref[:, sl] += lax.dot_general(dsc, qh, tn, preferred_element_type=F32)
            sv = slice(C_WIDTH + h * C_HEAD_DIM, C_WIDTH + (h + 1) * C_HEAD_DIM)
            dm_ref[:, sv] += lax.dot_general(pn.astype(BF16), doh, tn, preferred_element_type=F32)
        dq_ref[...] = jnp.concatenate(dqs, axis=1)

    row = pl.BlockSpec((bq, C_WIDTH), lambda i: (i, 0))
    return pl.pallas_call(
        body, name=name, grid=(s // bq,),
        in_specs=[row, pl.BlockSpec((m, C_WIDTH), lambda i: (0, 0)), pl.BlockSpec((m, C_WIDTH), lambda i: (0, 1)), row],
        out_specs=[row, pl.BlockSpec((m, 2 * C_WIDTH), lambda i: (0, 0))],
        out_shape=[jax.ShapeDtypeStruct((s, C_WIDTH), F32), jax.ShapeDtypeStruct((m, 2 * C_WIDTH), F32)],
        compiler_params=_params(("arbitrary",)),
    )(q, mkv, mkv, do)


def _gate_fwd(y, proj, zc0, bw, name):
    rows, width = y.shape
    bm = _tile(rows, 1024, 8)
    cb0 = zc0 // bw

    def body(y_ref, z_ref, o_ref):
        z = z_ref[...].astype(F32)
        o_ref[...] = (y_ref[...] * (z * _sigmoid(z))).astype(BF16)

    return pl.pallas_call(
        body, name=name, grid=(rows // bm, width // bw),
        in_specs=[pl.BlockSpec((bm, bw), lambda i, t: (i, t)), pl.BlockSpec((bm, bw), lambda i, t: (i, cb0 + t))],
        out_specs=pl.BlockSpec((bm, bw), lambda i, t: (i, t)),
        out_shape=jax.ShapeDtypeStruct((rows, width), BF16),
        compiler_params=_params(("parallel", "parallel")),
    )(y, proj)


def _gate_bwd(dsv, y, proj, zc0, bw, dproj, t0, head_major, name):
    rows, width = y.shape
    bm = _tile(rows, 1024, 8)
    cb0 = zc0 // bw
    tb0 = t0 // bw
    bd = _block_diag(HEAD_DIM)
    hpb = bw // HEAD_DIM

    def body(*refs):
        if head_major:
            ds_ref, y_ref, z_ref, bd_ref, _, dp_ref, dy_ref, dd_ref = refs
        else:
            ds_ref, y_ref, z_ref, _, dp_ref, dy_ref = refs
        z = z_ref[...].astype(F32)
        sig = _sigmoid(z)
        dsx = ds_ref[...]
        yv = y_ref[...]
        dy = dsx * (z * sig)
        dp_ref[...] = (dsx * yv * (sig * (1.0 + z * (1.0 - sig)))).astype(BF16)
        if head_major:
            dyb = dy.astype(BF16)
            dd = _seg_sum(dyb.astype(F32) * yv, bd_ref[...])
            for h in range(hpb):
                sl = slice(h * HEAD_DIM, (h + 1) * HEAD_DIM)
                dy_ref[h] = dyb[:, sl]
                dd_ref[h] = dd[:, sl]
        else:
            dy_ref[...] = dy.astype(BF16)

    tile = pl.BlockSpec((bm, bw), lambda i, t: (i, t))
    ztile = pl.BlockSpec((bm, bw), lambda i, t: (i, cb0 + t))
    ttile = pl.BlockSpec((bm, bw), lambda i, t: (i, tb0 + t))
    any_spec = pl.BlockSpec(memory_space=pl.ANY)
    dp_shape = jax.ShapeDtypeStruct(dproj.shape, BF16)
    if head_major:
        hm_spec = pl.BlockSpec((hpb, bm, HEAD_DIM), lambda i, t: (t, i, 0))
        nh = width // HEAD_DIM
        outs = pl.pallas_call(
            body, name=name, grid=(rows // bm, width // bw),
            in_specs=[tile, tile, ztile, pl.BlockSpec((LANES, LANES), lambda i, t: (0, 0)), any_spec],
            out_specs=[ttile, hm_spec, hm_spec],
            out_shape=[dp_shape, jax.ShapeDtypeStruct((nh, rows, HEAD_DIM), BF16),
                       jax.ShapeDtypeStruct((nh, rows, HEAD_DIM), F32)],
            input_output_aliases={4: 0},
            compiler_params=_params(("parallel", "parallel")),
        )(dsv, y, proj, bd, dproj)
        return outs[0], outs[1], outs[2]
    outs = pl.pallas_call(
        body, name=name, grid=(rows // bm, width // bw),
        in_specs=[tile, tile, ztile, any_spec],
        out_specs=[ttile, tile],
        out_shape=[dp_shape, jax.ShapeDtypeStruct((rows, width), BF16)],
        input_output_aliases={3: 0},
        compiler_params=_params(("parallel", "parallel")),
    )(dsv, y, proj, dproj)
    return outs[0], outs[1], None


def _merge_fwd(proj, ua, ub, uc, name):
    rows, d = ua.shape
    bm = _tile(rows, 512, 16)
    bw = _tile(d, 512)
    g0 = COL_GATE // bw
    gstep = d // bw

    def body(la_ref, lb_ref, lc_ref, ua_ref, ub_ref, uc_ref, o_ref, ga_ref, gb_ref, gc_ref):
        y = None
        for l_ref, u_ref, g_ref in ((la_ref, ua_ref, ga_ref), (lb_ref, ub_ref, gb_ref), (lc_ref, uc_ref, gc_ref)):
            g = _sigmoid(l_ref[...].astype(F32))
            g_ref[...] = g.astype(BF16)
            term = g * u_ref[...].astype(F32)
            y = term if y is None else y + term
        o_ref[...] = y.astype(BF16)

    tile = pl.BlockSpec((bm, bw), lambda i, t: (i, t))
    gate = lambda b: pl.BlockSpec((bm, bw), lambda i, t: (i, g0 + b * gstep + t))
    shape = jax.ShapeDtypeStruct((rows, d), BF16)
    return pl.pallas_call(
        body, name=name, grid=(rows // bm, d // bw),
        in_specs=[gate(0), gate(1), gate(2), tile, tile, tile],
        out_specs=[tile] * 4, out_shape=[shape] * 4,
        compiler_params=_params(("parallel", "parallel")),
    )(proj, proj, proj, ua, ub, uc)


def _merge_bwd(dym, us, gs, name):
    rows, d = dym.shape
    bm = _tile(rows, 512, 16)
    bw = _tile(d, 512)
    nb = d // bw

    def body(dy_ref, ua_ref, ub_ref, uc_ref, ga_ref, gb_ref, gc_ref, dg_ref, da_ref, db_ref, dc_ref):
        b = pl.program_id(2)
        dyv = dy_ref[...]
        for idx, (u_ref, g_ref, du_ref) in enumerate(((ua_ref, ga_ref, da_ref), (ub_ref, gb_ref, db_ref), (uc_ref, gc_ref, dc_ref))):
            @pl.when(b == idx)
            def _():
                g = g_ref[...].astype(F32)
                du_ref[...] = (g * dyv).astype(BF16)
                dg_ref[...] = (dyv * u_ref[...].astype(F32) * g * (1.0 - g)).astype(BF16)

    tile = pl.BlockSpec((bm, bw), lambda i, t, b: (i, t))
    shape = jax.ShapeDtypeStruct((rows, d), BF16)
    outs = pl.pallas_call(
        body, name=name, grid=(rows // bm, nb, 3),
        in_specs=[tile] * 7,
        out_specs=[pl.BlockSpec((bm, bw), lambda i, t, b: (i, b * nb + t)), tile, tile, tile],
        out_shape=[jax.ShapeDtypeStruct((rows, 3 * d), BF16), shape, shape, shape],
        compiler_params=_params(("parallel", "parallel", "arbitrary")),
    )(dym, *us, *gs)
    return outs[0], outs[1], outs[2], outs[3]


def _loss_head(y, target, name):
    rows, d = y.shape
    bm = _tile(rows, 256, 8)

    def body(y_ref, t_ref, dy_ref, dyb_ref, l_ref):
        i = pl.program_id(0)
        diff = y_ref[...] - t_ref[...]
        dy = diff * (1.0 / d)
        dy_ref[...] = dy
        dyb_ref[...] = dy.astype(BF16)
        sq = diff * diff
        part = sq[:, 0:LANES]
        for c in range(1, d // LANES):
            part = part + sq[:, c * LANES:(c + 1) * LANES]
        part = jnp.sum(part.reshape(bm // 8, 8, LANES), axis=0)

        @pl.when(i == 0)
        def _():
            l_ref[...] = part

        @pl.when(i > 0)
        def _():
            l_ref[...] += part

    row = pl.BlockSpec((bm, d), lambda i: (i, 0))
    return pl.pallas_call(
        body, name=name, grid=(rows // bm,), in_specs=[row, row],
        out_specs=[row, row, pl.BlockSpec((8, LANES), lambda i: (0, 0))],
        out_shape=[jax.ShapeDtypeStruct((rows, d), F32), jax.ShapeDtypeStruct((rows, d), BF16),
                   jax.ShapeDtypeStruct((8, LANES), F32)],
        compiler_params=_params(("arbitrary",)),
    )(y, target)


def _row(vec, reps=1):
    return jnp.tile(vec.reshape(1, -1).astype(F32), (1, reps))


def _local_step(x, mem, target, small, wg, shards=None):
    s, d = x.shape
    dist = shards is not None
    wg = dict(wg)
    ones = lambda n: jnp.ones((1, n), F32)
    zeros = lambda n: jnp.zeros((1, n), F32)
    scale_ab = HEAD_DIM ** -0.5
    split8 = lambda g: g.reshape(N_DEV, g.shape[0] // N_DEV, g.shape[1])
    flat8 = lambda g: g.reshape(g.shape[0] * g.shape[1], g.shape[2])
    gather = lambda names: _Comm("gather", [shards[n] for n in names]) if dist else None
    g = {}

    def scatter(names):
        return _Comm("scatter", [split8(g[n]) for n in names]) if dist else None

    def hosted(result, names, store):
        if not dist:
            return result
        out, got = result
        store.update(zip(names, got))
        return out

    hn = _rmsnorm_fwd(x, small["norm_gain"], "rms_x_fwd")
    got = {}
    proj = hosted(_mm_nn(hn, wg["qkv"], bm=1024, bn=1024, bk=d, o_dtype=BF16, name="proj_qkv",
                         comm=gather(("wa", "wb", "wc"))), ("wa", "wb", "wc"), got)
    wg.update({n: flat8(a) for n, a in got.items()})
    pfb = _mm_nn(hn, wg["wf"], bm=1024, bn=FB_PAD, bk=d, o_dtype=F32, name="proj_fb")
    mn = _rmsnorm_fwd(mem, small["mem_norm_gain"], "rms_mem_fwd")
    mkv = _mm_nn(mn, wg["wk"], bm=256, bn=1024, bk=d, o_dtype=F32, name="mem_kv")

    gain_a = jnp.concatenate([_row(small["q_gain_a"], A_Q_HEADS) * scale_ab, _row(small["k_gain_a"], A_KV_HEADS), ones(A_KV_WIDTH)], axis=1)
    flag_a = jnp.concatenate([ones(A_WIDTH + A_KV_WIDTH), zeros(A_KV_WIDTH)], axis=1)
    qkv_a = _headnorm_fwd(proj, COL_QA, 1280, 1280, HEAD_DIM, gain_a, flag_a, True, "hn_a_fwd")
    gain_b = jnp.concatenate([_row(small["q_gain_b"], B_HEADS) * scale_ab, _row(small["k_gain_b"], B_HEADS), ones(B_WIDTH)], axis=1)
    flag_b = jnp.concatenate([ones(2 * B_WIDTH), zeros(B_WIDTH)], axis=1)
    qkv_b = _headnorm_fwd(proj, COL_QB, 2304, 256, HEAD_DIM, gain_b, flag_b, True, "hn_b_fwd")
    gain_cq = _row(small["q_gain_c"], C_HEADS)
    q_c = _headnorm_fwd(proj, COL_QC, C_WIDTH, C_WIDTH, C_HEAD_DIM, gain_cq, ones(C_WIDTH), False, "hn_cq_fwd")
    gain_ck = jnp.concatenate([_row(small["k_gain_c"], C_HEADS), ones(C_WIDTH)], axis=1)
    flag_ck = jnp.concatenate([ones(C_WIDTH), zeros(C_WIDTH)], axis=1)
    mkvn = _headnorm_fwd(mkv, 0, 2 * C_WIDTH, 2 * C_WIDTH, C_HEAD_DIM, gain_ck, flag_ck, False, "hn_ck_fwd")

    q_a, k_a, v_a = qkv_a[0:12], qkv_a[12:16], qkv_a[16:20]
    q_b, k_b, v_b = qkv_b[0:12], qkv_b[12:24], qkv_b[24:36]

    bpad = jnp.pad(small["b_forget"].reshape(1, -1), ((0, 0), (0, FB_PAD - B_HEADS)))
    c16 = _fox_prep(pfb, bpad, "fox_prep")
    c3 = c16[0:B_HEADS].reshape(B_HEADS, 1, s)

    sinks = small["sinks_a"].reshape(-1)
    slopes = jnp.exp2(-8.0 * jnp.arange(1, A_Q_HEADS + 1, dtype=F32) / A_Q_HEADS)
    y_a, lse_a = _attn_a_fwd(q_a, k_a, v_a, sinks, slopes, "attn_a_fwd")
    y_b, lse_b, got_zg = _attn_b_fwd(q_b, k_b, v_b, c3, "attn_b_fwd", comm=gather(("zg",)))
    if dist:
        wg["zg"] = flat8(got_zg[0])
    y_c = _attn_c_fwd(q_c, mkvn, "attn_c_fwd")

    got = {}
    pzg = hosted(_mm_nn(hn, wg["zg"], bm=1024, bn=1024, bk=d, o_dtype=BF16, name="proj_zg", comm=gather(("wo",))),
                 ("wo",), got)
    wg.update({n: flat8(a) for n, a in got.items()})

    s_a = _gate_fwd(y_a, pzg, COL_ZA, 256, "gate_a_fwd")
    s_b = _gate_fwd(y_b, pzg, COL_ZB, 256, "gate_b_fwd")
    s_c = _gate_fwd(y_c, pzg, COL_ZC, 512, "gate_c_fwd")
    u_a = _mm_branch_fwd(s_a, wg["wa"], "branch_a_fwd")
    u_b = _mm_branch_fwd(s_b, wg["wb"], "branch_b_fwd")
    u_c = _mm_branch_fwd(s_c, wg["wc"], "branch_c_fwd")
    ym, gate_a, gate_b, gate_c = _merge_fwd(pzg, u_a, u_b, u_c, "merge_fwd")
    y = _mm_nn(ym, wg["wo"], bm=1024, bn=1024, bk=d, o_dtype=F32, name="out_proj", add=x)
    dy, dyb, lpart = _loss_head(y, target, "loss_head")
    loss = 0.5 / d * jnp.sum(lpart)

    dym = _mm_nt(dyb, wg["wo"], bm=1024, bn=1024, bk=d, o_dtype=F32, name="out_proj_bwd_act")
    g["wo"] = _mm_tn(ym, dyb, bm=512, bn=1024, bk=s, o_dtype=BF16, name="out_proj_bwd_w")

    dgate, du_a, du_b, du_c = _merge_bwd(dym, (u_a, u_b, u_c), (gate_a, gate_b, gate_c), "merge_bwd")
    parts = {}
    g["wm_g"] = hosted(_mm_tn(hn, dgate, bm=512, bn=1024, bk=s, o_dtype=BF16, name="proj_gate_bwd_w",
                              comm=scatter(("wo",))), ("wo",), parts)

    ds_a = _mm_branch_bwd_act(du_a, wg["wa"], A_WIDTH, "branch_a_bwd_act")
    ds_b = _mm_branch_bwd_act(du_b, wg["wb"], B_WIDTH, "branch_b_bwd_act")
    ds_c = _mm_branch_bwd_act(du_c, wg["wc"], C_WIDTH, "branch_c_bwd_act")
    g["wa"] = _mm_branch_bwd_w(s_a, du_a, "branch_a_bwd_w")
    g["wb"] = _mm_branch_bwd_w(s_b, du_b, "branch_b_bwd_w")
    g["wc"] = _mm_branch_bwd_w(s_c, du_c, "branch_c_bwd_w")

    dz = lax.empty((s, W_Z), BF16)
    dz, do_a, dd_a = _gate_bwd(ds_a, y_a, pzg, COL_ZA, 256, dz, COL_ZA, True, "gate_a_bwd")
    dz, do_b, dd_b = _gate_bwd(ds_b, y_b, pzg, COL_ZB, 256, dz, COL_ZB, True, "gate_b_bwd")
    dz, do_c, _ = _gate_bwd(ds_c, y_c, pzg, COL_ZC, 512, dz, COL_ZC, False, "gate_c_bwd")
    g["wm_z"] = _mm_tn(hn, dz, bm=512, bn=1024, bk=s, o_dtype=BF16, name="proj_z_bwd_w")

    names = ("wa", "wb", "wc")
    dq_a, dkv_a, dsink, got = _attn_a_bwd(q_a, k_a, v_a, do_a, lse_a, dd_a, sinks, slopes, "attn_a_bwd", comm=scatter(names))
    parts.update(zip(names, got))
    names = ("wm_g", "wm_z")
    dq_b, dk_b, dv_b, dc3, got = _attn_b_bwd(q_b, k_b, v_b, do_b, lse_b, dd_b, c3, "attn_b_bwd", comm=scatter(names))
    parts.update(zip(names, got))
    dq_c, dmkvn = _attn_c_bwd(q_c, mkvn, do_c, "attn_c_bwd")

    dqkv = lax.empty((s, W_QKV), BF16)
    dqkv, dg_qa = _headnorm_bwd(proj, COL_QA, A_WIDTH, 256, HEAD_DIM, gain_a[:, 0:768], flag_a[:, 0:768], dq_a, dqkv, COL_QA, "hn_qa_bwd")
    dqkv, dg_kva = _headnorm_bwd(proj, COL_KA, 512, 256, HEAD_DIM, gain_a[:, 768:1280], flag_a[:, 768:1280], dkv_a, dqkv, COL_KA, "hn_kva_bwd")
    dqkv, dg_qb = _headnorm_bwd(proj, COL_QB, B_WIDTH, 256, HEAD_DIM, gain_b[:, 0:768], flag_b[:, 0:768], dq_b, dqkv, COL_QB, "hn_qb_bwd")
    dqkv, dg_kb = _headnorm_bwd(proj, COL_KB, B_WIDTH, 256, HEAD_DIM, gain_b[:, 768:1536], flag_b[:, 768:1536], dk_b, dqkv, COL_KB, "hn_kb_bwd")
    dqkv, _ = _headnorm_bwd(proj, COL_VB, B_WIDTH, 256, HEAD_DIM, gain_b[:, 1536:2304], flag_b[:, 1536:2304], dv_b, dqkv, COL_VB, "hn_vb_bwd")
    dqkv, dg_qc = _headnorm_bwd(proj, COL_QC, C_WIDTH, 512, C_HEAD_DIM, gain_cq, ones(C_WIDTH), dq_c, dqkv, COL_QC, "hn_qc_bwd")
    dmkv, dg_kc = _headnorm_bwd(mkv, 0, 2 * C_WIDTH, 2 * C_WIDTH, C_HEAD_DIM, gain_ck, flag_ck, dmkvn, None, 0, "hn_kc_bwd")

    dct = jnp.pad(dc3.reshape(B_HEADS, s), ((0, 16 - B_HEADS), (0, 0)))
    dfb, dbf = _fox_prep_bwd(pfb, bpad, dct, "fox_prep_bwd")

    dmn = _mm_nt(dmkv, wg["wk"], bm=256, bn=1024, bk=1024, o_dtype=F32, name="mem_kv_bwd_act")
    g["wk"] = _mm_tn(mn, dmkv, bm=512, bn=1024, bk=mem.shape[0], o_dtype=BF16, name="mem_kv_bwd_w")
    _, dg_mem = _rmsnorm_bwd(mem, dmn, small["mem_norm_gain"], None, "rms_mem_bwd")

    g["wm_qkv"] = _mm_tn(hn, dqkv, bm=512, bn=1024, bk=s, o_dtype=BF16, name="proj_qkv_bwd_w")
    g["wf"] = _mm_tn(hn, dfb, bm=512, bn=FB_PAD, bk=s, o_dtype=BF16, name="proj_fb_bwd_w")
    dhn = _mm_nt(dqkv, wg["qkv"], bm=1024, bn=1024, bk=2048, o_dtype=F32, name="proj_qkv_bwd_act")
    dhn = _mm_nt(dfb, wg["wf"], bm=1024, bn=1024, bk=FB_PAD, o_dtype=F32, name="proj_fb_bwd_act", add=dhn)
    names = ("wm_qkv", "wf", "wk")
    dhn = hosted(_mm_nt_cat(dz, dgate, wg["zg"], bm=1024, bn=1024, bk=2048, name="proj_zg_bwd_act", add=dhn,
                            comm=scatter(names)), names, parts)
    if dist:
        g = parts
    grad_x, dg_x = _rmsnorm_bwd(x, dhn, small["norm_gain"], dy, "rms_x_bwd")

    fold = lambda part, heads, hd: jnp.sum(jnp.sum(part, axis=0).reshape(heads, hd), axis=0).reshape(1, hd)
    small_grads = {
        "norm_gain": jnp.sum(dg_x, axis=0).reshape(1, d),
        "mem_norm_gain": jnp.sum(dg_mem, axis=0).reshape(1, d),
        "b_forget": dbf[0:B_HEADS, 0].reshape(1, B_HEADS),
        "q_gain_a": fold(dg_qa, A_Q_HEADS, HEAD_DIM) * scale_ab,
        "k_gain_a": fold(dg_kva[:, 0:A_KV_WIDTH], A_KV_HEADS, HEAD_DIM),
        "sinks_a": (jnp.sum(dsink, axis=(1, 2)) * (1.0 / HEAD_DIM)).reshape(1, A_Q_HEADS),
        "q_gain_b": fold(dg_qb, B_HEADS, HEAD_DIM) * scale_ab,
        "k_gain_b": fold(dg_kb, B_HEADS, HEAD_DIM),
        "q_gain_c": fold(dg_qc, C_HEADS, C_HEAD_DIM),
        "k_gain_c": fold(dg_kc[:, 0:C_WIDTH], C_HEADS, C_HEAD_DIM),
    }
    return loss, grad_x, small_grads, g


def _coords():
    return lax.axis_index("x"), lax.axis_index("y"), lax.axis_index("c")


def _all_gather(shards, name):
    n = len(shards)

    def body(*refs):
        ins = refs[0:n]
        outs = refs[n:2 * n]
        send_sems, recv_sems, local_sems = refs[2 * n:2 * n + 3]
        x, y, c = _coords()
        me, sibling = (x, y, c), (x, y, 1 - c)
        chips = [(1 - x, y), (x, 1 - y), (1 - x, 1 - y)]
        idx = lambda p: 4 * p[0] + 2 * p[1] + p[2]

        def copy(a, k, block, to, src=None):
            slot = outs[a].at[idx(block)]
            return pltpu.make_async_remote_copy(
                src_ref=slot if src is None else src, dst_ref=slot,
                send_sem=send_sems.at[a, k], recv_sem=recv_sems.at[a, k], device_id=to, device_id_type=MESH)

        mine = [pltpu.make_async_copy(ins[a], outs[a].at[idx(me)], local_sems.at[a]) for a in range(n)]
        for cp in mine:
            cp.start()
        first = []
        for a in range(n):
            first.append(copy(a, 0, me, sibling, src=ins[a]))
            first += [copy(a, 1 + j, me, (*chip, c), src=ins[a]) for j, chip in enumerate(chips)]
        for cp in first:
            cp.start()
        passed = []
        for j, chip in enumerate(chips):
            for a in range(n):
                copy(a, 1 + j, (*chip, c), me).wait_recv()
                fwd = copy(a, 4 + j, (*chip, c), sibling)
                fwd.start()
                passed.append(fwd)
        for a in range(n):
            copy(a, 0, sibling, me).wait_recv()
            for j, chip in enumerate(chips):
                copy(a, 4 + j, (*chip, 1 - c), me).wait_recv()
        for cp in first + passed:
            cp.wait_send()
        for cp in mine:
            cp.wait()

    any_spec = pl.BlockSpec(memory_space=pl.ANY)
    return pl.pallas_call(
        body, name=name,
        in_specs=[any_spec] * n, out_specs=[any_spec] * n,
        out_shape=[jax.ShapeDtypeStruct((N_DEV,) + sh.shape, sh.dtype) for sh in shards],
        scratch_shapes=[pltpu.SemaphoreType.DMA((n, 7)), pltpu.SemaphoreType.DMA((n, 7)), pltpu.SemaphoreType.DMA((n,))],
    )(*shards)


def _all_reduce_small(vec, name):
    p = vec.shape[1]

    def body(v_ref, o_ref, gather, send_sems, recv_sems):
        x, y, c = _coords()
        my = 4 * x + 2 * y + c
        peers = [(x ^ ((k >> 2) & 1), y ^ ((k >> 1) & 1), c ^ (k & 1)) for k in range(1, N_DEV)]
        gather[my] = v_ref[...]
        sends = [pltpu.make_async_remote_copy(
            src_ref=v_ref, dst_ref=gather.at[my], send_sem=send_sems.at[k], recv_sem=recv_sems.at[k],
            device_id=peer, device_id_type=MESH) for k, peer in enumerate(peers)]
        for cp in sends:
            cp.start()
        for k, peer in enumerate(peers):
            pid = 4 * peer[0] + 2 * peer[1] + peer[2]
            pltpu.make_async_remote_copy(
                src_ref=v_ref, dst_ref=gather.at[pid], send_sem=send_sems.at[k], recv_sem=recv_sems.at[k],
                device_id=peer, device_id_type=MESH).wait_recv()
        for cp in sends:
            cp.wait_send()
        total = gather[0]
        for j in range(1, N_DEV):
            total = total + gather[j]
        o_ref[...] = total

    vm = pl.BlockSpec(memory_space=pltpu.VMEM)
    return pl.pallas_call(
        body, name=name, in_specs=[vm], out_specs=vm,
        out_shape=jax.ShapeDtypeStruct((8, p), F32),
        scratch_shapes=[pltpu.VMEM((N_DEV, 8, p), F32), pltpu.SemaphoreType.DMA((7,)), pltpu.SemaphoreType.DMA((7,))],
    )(vec)[0:1]


def _sum_parts(parts, name):
    _, rows, cols = parts.shape
    br = _tile(rows, 64, 16)

    def body(p_ref, o_ref):
        total = p_ref[0].astype(F32)
        for j in range(1, N_DEV):
            total = total + p_ref[j].astype(F32)
        o_ref[...] = total

    return pl.pallas_call(
        body, name=name, grid=(rows // br,),
        in_specs=[pl.BlockSpec((N_DEV, br, cols), lambda i: (0, i, 0))],
        out_specs=pl.BlockSpec((br, cols), lambda i: (i, 0)),
        out_shape=jax.ShapeDtypeStruct((rows, cols), F32),
        compiler_params=_params(("parallel",), VMEM_BIG),
    )(parts)


def _adamw(w, g, m, v, name, br=32):
    rows, cols = w.shape
    br = min(br, rows)
    c1 = 1.0 / (1.0 - ADAM_B1 ** ADAM_STEP)
    c2 = 1.0 / (1.0 - ADAM_B2 ** ADAM_STEP)

    def body(w_ref, g_ref, m_ref, v_ref, d_ref, nm_ref, nv_ref):
        gv = g_ref[...]
        nm = ADAM_B1 * m_ref[...] + (1.0 - ADAM_B1) * gv
        nv = ADAM_B2 * v_ref[...] + (1.0 - ADAM_B2) * (gv * gv)
        d_ref[...] = -ADAM_LR * ((nm * c1) / (jnp.sqrt(nv * c2) + ADAM_EPS) + ADAM_WD * w_ref[...])
        nm_ref[...] = nm
        nv_ref[...] = nv

    spec = pl.BlockSpec((br, cols), lambda i: (i, 0))
    shape = jax.ShapeDtypeStruct((rows, cols), F32)
    return pl.pallas_call(
        body, name=name, grid=(pl.cdiv(rows, br),), in_specs=[spec] * 4, out_specs=[spec] * 3, out_shape=[shape] * 3,
        compiler_params=_params(("parallel",), VMEM_BIG),
    )(w, g, m, v)


def _adamw_parts(w, parts, m, v, name):
    rows, cols = w.shape
    br = _tile(rows, 32, 16)
    c1 = 1.0 / (1.0 - ADAM_B1 ** ADAM_STEP)
    c2 = 1.0 / (1.0 - ADAM_B2 ** ADAM_STEP)

    def body(w_ref, p_ref, m_ref, v_ref, g_ref, d_ref, nm_ref, nv_ref):
        gv = p_ref[0].astype(F32)
        for j in range(1, N_DEV):
            gv = gv + p_ref[j].astype(F32)
        nm = ADAM_B1 * m_ref[...] + (1.0 - ADAM_B1) * gv
        nv = ADAM_B2 * v_ref[...] + (1.0 - ADAM_B2) * (gv * gv)
        g_ref[...] = gv
        d_ref[...] = -ADAM_LR * ((nm * c1) / (jnp.sqrt(nv * c2) + ADAM_EPS) + ADAM_WD * w_ref[...])
        nm_ref[...] = nm
        nv_ref[...] = nv

    spec = pl.BlockSpec((br, cols), lambda i: (i, 0))
    shape = jax.ShapeDtypeStruct((rows, cols), F32)
    return pl.pallas_call(
        body, name=name, grid=(rows // br,),
        in_specs=[spec, pl.BlockSpec((N_DEV, br, cols), lambda i: (0, i, 0)), spec, spec],
        out_specs=[spec] * 4, out_shape=[shape] * 4,
        compiler_params=_params(("parallel",), VMEM_BIG),
    )(w, parts, m, v)


SMALL_NAMES = ("norm_gain", "mem_norm_gain", "b_forget", "q_gain_a", "k_gain_a", "sinks_a",
               "q_gain_b", "k_gain_b", "q_gain_c", "k_gain_c")
BIG_NAMES = ("w_in", "w_mem_kv", "w_branch_a", "w_branch_b", "w_branch_c", "w_out")
WEIGHT_ORDER = ("norm_gain", "mem_norm_gain", "w_in", "b_forget", "q_gain_a", "k_gain_a", "sinks_a", "q_gain_b",
                "k_gain_b", "q_gain_c", "k_gain_c", "w_mem_kv", "w_branch_a", "w_branch_b", "w_branch_c", "w_out")


def _pack_small(tree):
    flat = jnp.concatenate([tree[n].reshape(1, -1) for n in SMALL_NAMES], axis=1)
    pad = (-flat.shape[1]) % LANES
    return jnp.pad(flat, ((0, 0), (0, pad)))


def _unpack_small(flat, like):
    out, off = {}, 0
    for n in SMALL_NAMES:
        size = like[n].size
        out[n] = flat[:, off:off + size].reshape(like[n].shape)
        off += size
    return out


def kernel(x, mem, norm_gain, mem_norm_gain, w_in, b_forget, q_gain_a, k_gain_a, sinks_a, q_gain_b, k_gain_b, q_gain_c, k_gain_c, w_mem_kv, w_branch_a, w_branch_b, w_branch_c, w_out, loss_target, m_norm_gain, m_mem_norm_gain, m_w_in, m_b_forget, m_q_gain_a, m_k_gain_a, m_sinks_a, m_q_gain_b, m_k_gain_b, m_q_gain_c, m_k_gain_c, m_w_mem_kv, m_w_branch_a, m_w_branch_b, m_w_branch_c, m_w_out, v_norm_gain, v_mem_norm_gain, v_w_in, v_b_forget, v_q_gain_a, v_k_gain_a, v_sinks_a, v_q_gain_b, v_k_gain_b, v_q_gain_c, v_k_gain_c, v_w_mem_kv, v_w_branch_a, v_w_branch_b, v_w_branch_c, v_w_out):
    weights = dict(norm_gain=norm_gain, mem_norm_gain=mem_norm_gain, w_in=w_in, b_forget=b_forget, q_gain_a=q_gain_a,
                   k_gain_a=k_gain_a, sinks_a=sinks_a, q_gain_b=q_gain_b, k_gain_b=k_gain_b, q_gain_c=q_gain_c,
                   k_gain_c=k_gain_c, w_mem_kv=w_mem_kv, w_branch_a=w_branch_a, w_branch_b=w_branch_b,
                   w_branch_c=w_branch_c, w_out=w_out)
    mom_m = dict(norm_gain=m_norm_gain, mem_norm_gain=m_mem_norm_gain, w_in=m_w_in, b_forget=m_b_forget,
                 q_gain_a=m_q_gain_a, k_gain_a=m_k_gain_a, sinks_a=m_sinks_a, q_gain_b=m_q_gain_b, k_gain_b=m_k_gain_b,
                 q_gain_c=m_q_gain_c, k_gain_c=m_k_gain_c, w_mem_kv=m_w_mem_kv, w_branch_a=m_w_branch_a,
                 w_branch_b=m_w_branch_b, w_branch_c=m_w_branch_c, w_out=m_w_out)
    mom_v = dict(norm_gain=v_norm_gain, mem_norm_gain=v_mem_norm_gain, w_in=v_w_in, b_forget=v_b_forget,
                 q_gain_a=v_q_gain_a, k_gain_a=v_k_gain_a, sinks_a=v_sinks_a, q_gain_b=v_q_gain_b, k_gain_b=v_k_gain_b,
                 q_gain_c=v_q_gain_c, k_gain_c=v_k_gain_c, w_mem_kv=v_w_mem_kv, w_branch_a=v_w_branch_a,
                 w_branch_b=v_w_branch_b, w_branch_c=v_w_branch_c, w_out=v_w_out)
    wi = w_in[0]
    sh_qkv = jnp.concatenate([wi[:, a:b] for a, b in SRC_RANGES[0:3]], axis=1).astype(BF16)
    sh_zg = jnp.concatenate([wi[:, a:b] for a, b in SRC_RANGES[3:6]] + [wi[:, SRC_GATE:]], axis=1).astype(BF16)
    sh_wf = jnp.pad(wi[:, FB_SRC:FB_SRC + B_HEADS], ((0, 0), (0, FB_PAD - B_HEADS))).astype(BF16)
    shards = {"zg": sh_zg, "wo": w_out[0].astype(BF16), "wa": w_branch_a[0].astype(BF16),
              "wb": w_branch_b[0].astype(BF16), "wc": w_branch_c[0].astype(BF16)}
    first = ("qkv", "wf", "wk")
    full = _all_gather([sh_qkv, sh_wf, w_mem_kv[0].astype(BF16)], "weights_all_gather")
    wg = {kname: arr.reshape(arr.shape[0] * arr.shape[1], arr.shape[2]) for kname, arr in zip(first, full)}

    small = {n: weights[n] for n in SMALL_NAMES}
    loss_local, grad_x, small_g, parts = _local_step(x[0], mem[0], loss_target[0], small, wg, shards)

    grads, delta, new_m, new_v = {}, {}, {}, {}
    for n, kname in (("w_mem_kv", "wk"), ("w_out", "wo"), ("w_branch_a", "wa"), ("w_branch_b", "wb"), ("w_branch_c", "wc")):
        gsum, dlt, nm, nv = _adamw_parts(weights[n][0], parts[kname], mom_m[n][0], mom_v[n][0], "adamw_" + n)
        grads[n], delta[n], new_m[n], new_v[n] = gsum, dlt[None], nm[None], nv[None]
    gq, gz, gf, gg = (_sum_parts(parts[k], "grad_sum_" + k) for k in ("wm_qkv", "wm_z", "wf", "wm_g"))
    g_in = jnp.concatenate([gq[:, COL_QA:COL_QB], gz[:, COL_ZA:COL_ZB], gq[:, COL_QB:COL_QC], gz[:, COL_ZB:COL_ZC],
                            gf[:, 0:B_HEADS], gq[:, COL_QC:W_QKV], gz[:, COL_ZC:W_Z], gg], axis=1)
    dlt, nm, nv = _adamw(w_in[0], g_in, m_w_in[0], v_w_in[0], "adamw_w_in")
    grads["w_in"], delta["w_in"], new_m["w_in"], new_v["w_in"] = g_in, dlt[None], nm[None], nv[None]

    packed = _pack_small(small_g)
    reduced = _all_reduce_small(jnp.broadcast_to(packed, (8, packed.shape[1])), "small_all_reduce")
    grads.update(_unpack_small(reduced, small))

    loss = lax.psum(loss_local, ("x", "y", "c"))

    pw, pm, pv = _pack_small(small), _pack_small({n: mom_m[n] for n in SMALL_NAMES}), _pack_small({n: mom_v[n] for n in SMALL_NAMES})
    rep8 = lambda a: jnp.broadcast_to(a, (8, a.shape[1]))
    dlt, nm, nv = _adamw(rep8(pw), rep8(reduced), rep8(pm), rep8(pv), "adamw_small")
    for tree, flat in ((delta, dlt), (new_m, nm), (new_v, nv)):
        tree.update(_unpack_small(flat[0:1], small))
    for n in BIG_NAMES:
        grads[n] = grads[n][None]
    return (loss, grad_x[None], *[grads[n] for n in WEIGHT_ORDER], *[delta[n] for n in WEIGHT_ORDER],
            *[new_m[n] for n in WEIGHT_ORDER], *[new_v[n] for n in WEIGHT_ORDER])
```

```python
import math

import jax
import jax.numpy as jnp
import numpy as np
from jax import lax
from jax.experimental import pallas as pl
from jax.experimental.pallas import tpu as pltpu

F32 = jnp.float32
BF16 = jnp.bfloat16

N_DEV = 8
HEAD_DIM = 64
A_Q_HEADS = 12
A_KV_HEADS = 4
A_GROUP = 3
B_HEADS = 12
C_HEADS = 4
C_HEAD_DIM = 128
WINDOW = 128
A_WIDTH = 768
A_KV_WIDTH = 256
B_WIDTH = 768
C_WIDTH = 512
EPS = 1e-6
NEG = -1e30

COL_QA, COL_KA, COL_VA = 0, 768, 1024
COL_QB, COL_KB, COL_VB = 1280, 2048, 2816
COL_QC = 3584
W_QKV = 4096
COL_ZA, COL_ZB, COL_ZC = 0, 768, 1536
COL_GATE = W_Z = 2048
SRC_RANGES = ((0, 1280), (2048, 4352), (5132, 5644), (1280, 2048), (4352, 5120), (5644, 6156))
SRC_GATE = 6156
FB_SRC = 5120
FB_PAD = 128

ADAM_LR = 0.001
ADAM_B1 = 0.9
ADAM_B2 = 0.999
ADAM_EPS = 1e-08
ADAM_WD = 0.01
ADAM_STEP = 10

VMEM_BIG = 52 * 1024 * 1024
LANES = 128
MESH = pl.DeviceIdType.MESH


def _tile(n, pref, mult=128):
    if n <= pref:
        return n
    t = (pref // mult) * mult
    while t >= mult:
        if n % t == 0:
            return t
        t -= mult
    return n


def _params(sem=None, vmem=None):
    kw = {}
    if sem is not None:
        kw["dimension_semantics"] = sem
    if vmem is not None:
        kw["vmem_limit_bytes"] = vmem
    return pltpu.CompilerParams(**kw)


def _sigmoid(x):
    return 1.0 / (1.0 + jnp.exp(-x))


def _block_diag(hd):
    r = np.arange(LANES)
    return jnp.asarray((r[:, None] // hd) == (r[None, :] // hd), dtype=BF16)


def _seg_sum(t, bd):
    hi = t.astype(BF16)
    lo = (t - hi.astype(F32)).astype(BF16)
    outs = []
    for c in range(t.shape[1] // LANES):
        sl = slice(c * LANES, (c + 1) * LANES)
        outs.append(jnp.dot(hi[:, sl], bd, preferred_element_type=F32) + jnp.dot(lo[:, sl], bd, preferred_element_type=F32))
    return outs[0] if len(outs) == 1 else jnp.concatenate(outs, axis=1)


def _rmsnorm_fwd(x, gain, name):
    rows, d = x.shape
    bm = _tile(rows, 512, 8)

    def body(x_ref, g_ref, o_ref):
        xv = x_ref[...]
        ms = jnp.mean(xv * xv, axis=-1, keepdims=True)
        o_ref[...] = (xv * lax.rsqrt(ms + EPS) * g_ref[...]).astype(BF16)

    return pl.pallas_call(
        body, name=name, grid=(rows // bm,),
        in_specs=[pl.BlockSpec((bm, d), lambda i: (i, 0)), pl.BlockSpec((1, d), lambda i: (0, 0))],
        out_specs=pl.BlockSpec((bm, d), lambda i: (i, 0)),
        out_shape=jax.ShapeDtypeStruct((rows, d), BF16),
        compiler_params=_params(("parallel",)),
    )(x, gain)


def _rmsnorm_bwd(x, dhn, gain, dy, name):
    rows, d = x.shape
    bm = _tile(rows, 512, 8)
    with_dx = dy is not None

    def body(*refs):
        if with_dx:
            x_ref, dh_ref, g_ref, dy_ref, gx_ref, dg_ref = refs
        else:
            x_ref, dh_ref, g_ref, dg_ref = refs
        i = pl.program_id(0)
        xv = x_ref[...]
        rstd = lax.rsqrt(jnp.mean(xv * xv, axis=-1, keepdims=True) + EPS)
        xhat = xv * rstd
        dh = dh_ref[...]
        part = jnp.sum((dh * xhat).reshape(bm // 8, 8, d), axis=0)

        @pl.when(i == 0)
        def _():
            dg_ref[...] = part

        @pl.when(i > 0)
        def _():
            dg_ref[...] += part

        if with_dx:
            g = dh * g_ref[...]
            mean = jnp.mean(g * xhat, axis=-1, keepdims=True)
            gx_ref[...] = dy_ref[...] + rstd * (g - xhat * mean)

    row_spec = pl.BlockSpec((bm, d), lambda i: (i, 0))
    in_specs = [row_spec, row_spec, pl.BlockSpec((1, d), lambda i: (0, 0))]
    args = [x, dhn, gain]
    dg_spec = pl.BlockSpec((8, d), lambda i: (0, 0))
    dg_shape = jax.ShapeDtypeStruct((8, d), F32)
    if with_dx:
        in_specs.append(row_spec)
        args.append(dy)
        out_specs = [row_spec, dg_spec]
        out_shape = [jax.ShapeDtypeStruct((rows, d), F32), dg_shape]
    else:
        out_specs = [dg_spec]
        out_shape = [dg_shape]
    outs = pl.pallas_call(
        body, name=name, grid=(rows // bm,), in_specs=in_specs, out_specs=out_specs, out_shape=out_shape,
        compiler_params=_params(("arbitrary",), VMEM_BIG),
    )(*args)
    return outs if with_dx else (None, outs[0])


class _Comm:
    def __init__(self, kind, arrays):
        self.kind = kind
        self.arrays = list(arrays)
        self.n = len(self.arrays)

    def out_shapes(self):
        if self.kind == "gather":
            return [jax.ShapeDtypeStruct((N_DEV,) + a.shape, a.dtype) for a in self.arrays]
        return [jax.ShapeDtypeStruct(a.shape, a.dtype) for a in self.arrays]

    def scratch(self):
        return [pltpu.SemaphoreType.DMA((self.n, N_DEV - 1)), pltpu.SemaphoreType.DMA((self.n, N_DEV - 1)),
                pltpu.SemaphoreType.DMA((self.n,))]

    def _plan(self, ins, outs, sems, with_recvs):
        send_sems, recv_sems, local_sems = sems
        x, y, c = lax.axis_index("x"), lax.axis_index("y"), lax.axis_index("c")
        my = 4 * x + 2 * y + c
        gather = self.kind == "gather"
        local, sends, recvs = [], [], []
        for a in range(self.n):
            local.append(pltpu.make_async_copy(ins[a] if gather else ins[a].at[my], outs[a].at[my], local_sems.at[a]))
            for k in range(1, N_DEV):
                peer = (x ^ ((k >> 2) & 1), y ^ ((k >> 1) & 1), c ^ (k & 1))
                pid = 4 * peer[0] + 2 * peer[1] + peer[2]
                src = ins[a] if gather else ins[a].at[pid]
                sem = dict(send_sem=send_sems.at[a, k - 1], recv_sem=recv_sems.at[a, k - 1], device_id=peer, device_id_type=MESH)
                sends.append(pltpu.make_async_remote_copy(src_ref=src, dst_ref=outs[a].at[my], **sem))
                if with_recvs:
                    recvs.append(pltpu.make_async_remote_copy(src_ref=src, dst_ref=outs[a].at[pid], **sem))
        return local, sends, recvs

    def start(self, ins, outs, sems):
        local, sends, _ = self._plan(ins, outs, sems, False)
        for cp in local + sends:
            cp.start()

    def wait(self, ins, outs, sems):
        local, sends, recvs = self._plan(ins, outs, sems, True)
        for cp in recvs:
            cp.wait_recv()
        for cp in sends:
            cp.wait_send()
        for cp in local:
            cp.wait()


def _grid_edges(grid):
    first = last = None
    for ax, size in enumerate(grid):
        pid = pl.program_id(ax)
        f, l = pid == 0, pid == size - 1
        first = f if first is None else first & f
        last = l if last is None else last & l
    return first, last


def _hosted_call(body, comm, *, name, grid, in_specs, out_specs, out_shape, scratch_shapes, args, sem, vmem=None):
    in_specs, out_specs, out_shape, scratch_shapes = list(in_specs), list(out_specs), list(out_shape), list(scratch_shapes)
    if comm is None:
        res = pl.pallas_call(body, name=name, grid=grid, in_specs=in_specs, out_specs=out_specs, out_shape=out_shape,
                             scratch_shapes=scratch_shapes, compiler_params=_params(sem, vmem))(*args)
        return list(res), []
    n_in, n_out, n_scr, nc = len(in_specs), len(out_shape), len(scratch_shapes), comm.n

    def hosted(*refs):
        ins = refs[0:n_in]
        comm_in = refs[n_in:n_in + nc]
        outs = refs[n_in + nc:n_in + nc + n_out]
        comm_out = refs[n_in + nc + n_out:n_in + 2 * nc + n_out]
        scr = refs[n_in + 2 * nc + n_out:n_in + 2 * nc + n_out + n_scr]
        sems = refs[n_in + 2 * nc + n_out + n_scr:]
        first, last = _grid_edges(grid)

        @pl.when(first)
        def _():
            comm.start(comm_in, comm_out, sems)

        body(*ins, *outs, *scr)

        @pl.when(last)
        def _():
            comm.wait(comm_in, comm_out, sems)

    any_spec = pl.BlockSpec(memory_space=pl.ANY)
    res = pl.pallas_call(
        hosted, name=name, grid=grid, in_specs=in_specs + [any_spec] * nc, out_specs=out_specs + [any_spec] * nc,
        out_shape=out_shape + comm.out_shapes(), scratch_shapes=scratch_shapes + comm.scratch(),
        compiler_params=_params(("arbitrary",) * len(grid), vmem),
    )(*args, *comm.arrays)
    return list(res[0:n_out]), list(res[n_out:])


def _mm(a, b, *, grid, a_spec, b_spec, o_spec, o_shape, o_dtype, contract, name, add=None, add_spec=None, acc_shape=None,
        comm=None):
    nk = grid[2]
    has_add = add is not None

    def body(*refs):
        a_ref, b_ref = refs[0], refs[1]
        add_ref = refs[2] if has_add else None
        o_ref = refs[3] if has_add else refs[2]
        part = lax.dot_general(a_ref[...], b_ref[...], (contract, ((), ())), preferred_element_type=F32)
        if nk == 1:
            if has_add:
                part = part + add_ref[...]
            o_ref[...] = part.astype(o_dtype)
        else:
            acc = refs[-1]
            k = pl.program_id(2)

            @pl.when(k == 0)
            def _():
                acc[...] = part

            @pl.when(k > 0)
            def _():
                acc[...] += part

            @pl.when(k == nk - 1)
            def _():
                r = acc[...]
                if has_add:
                    r = r + add_ref[...]
                o_ref[...] = r.astype(o_dtype)

    in_specs = [a_spec, b_spec] + ([add_spec] if has_add else [])
    args = [a, b] + ([add] if has_add else [])
    scratch = [pltpu.VMEM(acc_shape, F32)] if nk > 1 else []
    outs, comm_outs = _hosted_call(
        body, comm, name=name, grid=grid, in_specs=in_specs, out_specs=[o_spec],
        out_shape=[jax.ShapeDtypeStruct(o_shape, o_dtype)], scratch_shapes=scratch, args=args,
        sem=("parallel", "parallel", "arbitrary"), vmem=VMEM_BIG)
    return outs[0] if comm is None else (outs[0], comm_outs)


def _mm_nn(a, b, *, bm, bn, bk, o_dtype, name, add=None, comm=None):
    m, kd = a.shape
    n = b.shape[1]
    bm, bn, bk = _tile(m, bm, 8), _tile(n, bn), _tile(kd, bk)
    o_spec = pl.BlockSpec((bm, bn), lambda i, j, k: (i, j))
    return _mm(a, b, grid=(m // bm, n // bn, kd // bk),
               a_spec=pl.BlockSpec((bm, bk), lambda i, j, k: (i, k)),
               b_spec=pl.BlockSpec((bk, bn), lambda i, j, k: (k, j)),
               o_spec=o_spec, o_shape=(m, n), o_dtype=o_dtype, contract=((1,), (0,)), name=name,
               add=add, add_spec=o_spec, acc_shape=(bm, bn), comm=comm)


def _mm_nt(a, b, *, bm, bn, bk, o_dtype, name, add=None, b_col0=0, comm=None):
    m, kd = a.shape
    n = b.shape[0]
    bm, bn, bk = _tile(m, bm, 8), _tile(n, bn), _tile(math.gcd(kd, b_col0), bk)
    kb0 = b_col0 // bk
    o_spec = pl.BlockSpec((bm, bn), lambda i, j, k: (i, j))
    return _mm(a, b, grid=(m // bm, n // bn, kd // bk),
               a_spec=pl.BlockSpec((bm, bk), lambda i, j, k: (i, k)),
               b_spec=pl.BlockSpec((bn, bk), lambda i, j, k: (j, kb0 + k)),
               o_spec=o_spec, o_shape=(m, n), o_dtype=o_dtype, contract=((1,), (1,)), name=name,
               add=add, add_spec=o_spec, acc_shape=(bm, bn), comm=comm)


def _mm_nt_cat(a1, a2, b, *, bm, bn, bk, name, add, comm=None):
    m, k1 = a1.shape
    k2 = a2.shape[1]
    n = b.shape[0]
    bm, bn, bk = _tile(m, bm, 8), _tile(n, bn), _tile(math.gcd(k1, k2), bk)
    n1, nk = k1 // bk, (k1 + k2) // bk
    nt = (((1,), (1,)), ((), ()))

    def body(a1_ref, a2_ref, b_ref, add_ref, o_ref, acc):
        k = pl.program_id(2)

        def accumulate(part):
            @pl.when(k == 0)
            def _():
                acc[...] = part

            @pl.when(k > 0)
            def _():
                acc[...] += part

        @pl.when(k < n1)
        def _():
            accumulate(lax.dot_general(a1_ref[...], b_ref[...], nt, preferred_element_type=F32))

        @pl.when(k >= n1)
        def _():
            accumulate(lax.dot_general(a2_ref[...], b_ref[...], nt, preferred_element_type=F32))

        @pl.when(k == nk - 1)
        def _():
            o_ref[...] = acc[...] + add_ref[...]

    o_spec = pl.BlockSpec((bm, bn), lambda i, j, k: (i, j))
    outs, comm_outs = _hosted_call(
        body, comm, name=name, grid=(m // bm, n // bn, nk),
        in_specs=[pl.BlockSpec((bm, bk), lambda i, j, k: (i, jnp.minimum(k, n1 - 1))),
                  pl.BlockSpec((bm, bk), lambda i, j, k: (i, jnp.maximum(k - n1, 0))),
                  pl.BlockSpec((bn, bk), lambda i, j, k: (j, k)), o_spec],
        out_specs=[o_spec], out_shape=[jax.ShapeDtypeStruct((m, n), F32)],
        scratch_shapes=[pltpu.VMEM((bm, bn), F32)], args=[a1, a2, b, add],
        sem=("parallel", "parallel", "arbitrary"), vmem=VMEM_BIG)
    return outs[0] if comm is None else (outs[0], comm_outs)


def _mm_tn(a, b, *, bm, bn, bk, o_dtype, name, comm=None):
    kd, m = a.shape
    n = b.shape[1]
    bm, bn, bk = _tile(m, bm), _tile(n, bn), _tile(kd, bk, 8)
    return _mm(a, b, grid=(m // bm, n // bn, kd // bk),
               a_spec=pl.BlockSpec((bk, bm), lambda i, j, k: (k, i)),
               b_spec=pl.BlockSpec((bk, bn), lambda i, j, k: (k, j)),
               o_spec=pl.BlockSpec((bm, bn), lambda i, j, k: (i, j)),
               o_shape=(m, n), o_dtype=o_dtype, contract=((0,), (0,)), name=name, acc_shape=(bm, bn), comm=comm)


def _branch_full(w8):
    kb, ds = w8.shape[0] // N_DEV, w8.shape[1]
    return w8.reshape(N_DEV, kb, ds).transpose(1, 0, 2).reshape(kb, N_DEV * ds)


def _branch_shards(g):
    kb, ds = g.shape[0], g.shape[1] // N_DEV
    return g.reshape(kb, N_DEV, ds).transpose(1, 0, 2).reshape(N_DEV * kb, ds)


def _headnorm_fwd(src, c0, width, bw, hd, gain, nflag, head_major, name):
    rows = src.shape[0]
    bm = _tile(rows, 2048 if bw <= 256 else 1024, 16)
    bd = _block_diag(hd)
    cb0 = c0 // bw

    def body(x_ref, g_ref, f_ref, bd_ref, o_ref):
        xv = x_ref[...].astype(F32)
        ss = _seg_sum(xv * xv, bd_ref[...])
        rstd = lax.rsqrt(ss * (1.0 / hd) + EPS)
        y = (xv * jnp.where(f_ref[...] > 0.0, rstd, 1.0) * g_ref[...]).astype(BF16)
        if head_major:
            for h in range(bw // HEAD_DIM):
                o_ref[h] = y[:, h * HEAD_DIM:(h + 1) * HEAD_DIM]
        else:
            o_ref[...] = y

    vec_spec = pl.BlockSpec((1, bw), lambda i, t: (0, t))
    if head_major:
        hpb = bw // HEAD_DIM
        out_spec = pl.BlockSpec((hpb, bm, HEAD_DIM), lambda i, t: (t, i, 0))
        out_shape = jax.ShapeDtypeStruct((width // HEAD_DIM, rows, HEAD_DIM), BF16)
    else:
        out_spec = pl.BlockSpec((bm, bw), lambda i, t: (i, t))
        out_shape = jax.ShapeDtypeStruct((rows, width), BF16)
    return pl.pallas_call(
        body, name=name, grid=(rows // bm, width // bw),
        in_specs=[pl.BlockSpec((bm, bw), lambda i, t: (i, cb0 + t)), vec_spec, vec_spec,
                  pl.BlockSpec((LANES, LANES), lambda i, t: (0, 0))],
        out_specs=out_spec, out_shape=out_shape,
        compiler_params=_params(("parallel", "parallel")),
    )(src, gain, nflag, bd)


def _headnorm_bwd(src, c0, width, bw, hd, gain, nflag, dyn, target, t0, name):
    rows = src.shape[0]
    bm = _tile(rows, 2048 if bw <= 256 else 1024, 16)
    bd = _block_diag(hd)
    cb0 = c0 // bw
    tb0 = t0 // bw
    aliased = target is not None

    def body(*refs):
        if aliased:
            x_ref, dy_ref, g_ref, f_ref, bd_ref, _, o_ref, dg_ref = refs
        else:
            x_ref, dy_ref, g_ref, f_ref, bd_ref, o_ref, dg_ref = refs
        i = pl.program_id(1)
        xv = x_ref[...].astype(F32)
        dyv = dy_ref[...]
        bdv = bd_ref[...]
        rstd = lax.rsqrt(_seg_sum(xv * xv, bdv) * (1.0 / hd) + EPS)
        xhat = xv * rstd
        g = dyv * g_ref[...]
        mean = _seg_sum(g * xhat, bdv) * (1.0 / hd)
        dx = jnp.where(f_ref[...] > 0.0, rstd * (g - xhat * mean), g)
        o_ref[...] = dx.astype(BF16)
        part = jnp.sum((dyv * xhat).reshape(bm // 8, 8, bw), axis=0)

        @pl.when(i == 0)
        def _():
            dg_ref[...] = part

        @pl.when(i > 0)
        def _():
            dg_ref[...] += part

    vec_spec = pl.BlockSpec((1, bw), lambda t, i: (0, t))
    in_specs = [pl.BlockSpec((bm, bw), lambda t, i: (i, cb0 + t)), pl.BlockSpec((bm, bw), lambda t, i: (i, t)),
                vec_spec, vec_spec, pl.BlockSpec((LANES, LANES), lambda t, i: (0, 0))]
    args = [src, dyn, gain, nflag, bd]
    aliases = {}
    if aliased:
        in_specs.append(pl.BlockSpec(memory_space=pl.ANY))
        args.append(target)
        aliases = {5: 0}
        o_shape = jax.ShapeDtypeStruct(target.shape, BF16)
    else:
        o_shape = jax.ShapeDtypeStruct((rows, width), BF16)
    out, dg = pl.pallas_call(
        body, name=name, grid=(width // bw, rows // bm), in_specs=in_specs,
        out_specs=[pl.BlockSpec((bm, bw), lambda t, i: (i, tb0 + t)), pl.BlockSpec((8, bw), lambda t, i: (0, t))],
        out_shape=[o_shape, jax.ShapeDtypeStruct((8, width), F32)],
        input_output_aliases=aliases,
        compiler_params=_params(("parallel", "arbitrary")),
    )(*args)
    return out, dg


def _fox_prep(pfb, bpad, name):
    s = pfb.shape[0]

    def body(p_ref, b_ref, c_ref):
        z = p_ref[...] + b_ref[...]
        logf = jnp.minimum(z, 0.0) - jnp.log(1.0 + jnp.exp(-jnp.abs(z)))
        x = logf.T[0:16, :]
        lane = lax.broadcasted_iota(jnp.int32, (16, s), 1)
        sh = 1
        while sh < s:
            x = x + jnp.where(lane >= sh, pltpu.roll(x, sh, 1), 0.0)
            sh *= 2
        c_ref[...] = x

    return pl.pallas_call(
        body, name=name, grid=(1,),
        in_specs=[pl.BlockSpec((s, FB_PAD), lambda i: (0, 0)), pl.BlockSpec((1, FB_PAD), lambda i: (0, 0))],
        out_specs=pl.BlockSpec((16, s), lambda i: (0, 0)),
        out_shape=jax.ShapeDtypeStruct((16, s), F32),
        compiler_params=_params(("arbitrary",)),
    )(pfb, bpad)


def _fox_prep_bwd(pfb, bpad, dct, name):
    s = pfb.shape[0]

    def body(p_ref, b_ref, dc_ref, df_ref, db_ref):
        zt = (p_ref[...] + b_ref[...]).T[0:16, :]
        y = dc_ref[...]
        lane = lax.broadcasted_iota(jnp.int32, (16, s), 1)
        sh = 1
        while sh < s:
            y = y + jnp.where(lane < s - sh, pltpu.roll(y, s - sh, 1), 0.0)
            sh *= 2
        dz = y * _sigmoid(-zt)
        db_ref[...] = jnp.broadcast_to(jnp.sum(dz, axis=1, keepdims=True), (16, FB_PAD))
        full = jnp.concatenate([dz, jnp.zeros((FB_PAD - 16, s), F32)], axis=0)
        df_ref[...] = full.T.astype(BF16)

    return pl.pallas_call(
        body, name=name, grid=(1,),
        in_specs=[pl.BlockSpec((s, FB_PAD), lambda i: (0, 0)), pl.BlockSpec((1, FB_PAD), lambda i: (0, 0)),
                  pl.BlockSpec((16, s), lambda i: (0, 0))],
        out_specs=[pl.BlockSpec((s, FB_PAD), lambda i: (0, 0)), pl.BlockSpec((16, FB_PAD), lambda i: (0, 0))],
        out_shape=[jax.ShapeDtypeStruct((s, FB_PAD), BF16), jax.ShapeDtypeStruct((16, FB_PAD), F32)],
        compiler_params=_params(("arbitrary",)),
    )(pfb, bpad, dct)


def _swa_window(n):
    ws = pl.multiple_of(jnp.maximum(n * WINDOW - WINDOW, 0), WINDOW)
    qi = lax.broadcasted_iota(jnp.int32, (WINDOW, 2 * WINDOW), 0)
    kj = lax.broadcasted_iota(jnp.int32, (WINDOW, 2 * WINDOW), 1)
    rel = qi + (n * WINDOW - ws) - kj
    valid = (rel >= 0) & (rel < WINDOW)
    return ws, valid, rel.astype(F32)


def _attn_a_fwd(q, k, v, sinks, slopes, name):
    s = q.shape[1]
    nb = s // WINDOW
    smem = pl.BlockSpec(memory_space=pltpu.SMEM)

    def body(sink_ref, slope_ref, q_ref, k_ref, v_ref, o_ref, lse_ref):
        n = pl.program_id(0)
        ws, valid, relf = _swa_window(n)
        outs = []
        for h in range(A_Q_HEADS):
            kvh = h // A_GROUP
            kw = k_ref[kvh, pl.ds(ws, 2 * WINDOW), :]
            vw = v_ref[kvh, pl.ds(ws, 2 * WINDOW), :]
            sc = lax.dot_general(q_ref[h], kw, (((1,), (1,)), ((), ())), preferred_element_type=F32)
            sc = jnp.where(valid, sc - slope_ref[h] * relf, NEG)
            sink = sink_ref[h]
            m = jnp.maximum(jnp.max(sc, axis=1, keepdims=True), sink)
            p = jnp.exp(sc - m)
            denom = jnp.sum(p, axis=1, keepdims=True) + jnp.exp(sink - m)
            pn = (p / denom).astype(BF16)
            outs.append(jnp.dot(pn, vw, preferred_element_type=F32))
            lse_ref[h] = jnp.broadcast_to(m + jnp.log(denom), (WINDOW, HEAD_DIM))
        o_ref[...] = jnp.concatenate(outs, axis=1)

    return pl.pallas_call(
        body, name=name, grid=(nb,),
        in_specs=[smem, smem,
                  pl.BlockSpec((A_Q_HEADS, WINDOW, HEAD_DIM), lambda n: (0, n, 0)),
                  pl.BlockSpec((A_KV_HEADS, s, HEAD_DIM), lambda n: (0, 0, 0)),
                  pl.BlockSpec((A_KV_HEADS, s, HEAD_DIM), lambda n: (0, 0, 0))],
        out_specs=[pl.BlockSpec((WINDOW, A_WIDTH), lambda n: (n, 0)),
                   pl.BlockSpec((A_Q_HEADS, WINDOW, HEAD_DIM), lambda n: (0, n, 0))],
        out_shape=[jax.ShapeDtypeStruct((s, A_WIDTH), F32), jax.ShapeDtypeStruct((A_Q_HEADS, s, HEAD_DIM), F32)],
        compiler_params=_params(("parallel",), VMEM_BIG),
    )(sinks, slopes, q, k, v)


def _attn_a_bwd(q, k, v, do, lse, dd, sinks, slopes, name, comm=None):
    s = q.shape[1]
    nb = s // WINDOW
    smem = pl.BlockSpec(memory_space=pltpu.SMEM)
    last = nb - 1

    def body(sink_ref, slope_ref, q_ref, k_ref, v_ref, do_ref, lse_ref, dd_ref, dq_ref, dkv_ref, ds_ref, carry):
        n = pl.program_id(0)

        @pl.when(n == 0)
        def _():
            carry[...] = jnp.zeros(carry.shape, F32)
            ds_ref[...] = jnp.zeros(ds_ref.shape, F32)

        @pl.when(n < nb)
        def _():
            ws, valid, relf = _swa_window(n)
            dqs = []
            dkw = [None] * A_KV_HEADS
            dvw = [None] * A_KV_HEADS
            for h in range(A_Q_HEADS):
                kvh = h // A_GROUP
                qh = q_ref[h]
                doh = do_ref[h]
                kw = k_ref[kvh, pl.ds(ws, 2 * WINDOW), :]
                vw = v_ref[kvh, pl.ds(ws, 2 * WINDOW), :]
                lse_h = lse_ref[h]
                dd_h = dd_ref[h]
                sc = lax.dot_general(qh, kw, (((1,), (1,)), ((), ())), preferred_element_type=F32)
                sc = jnp.where(valid, sc - slope_ref[h] * relf, NEG)
                p = jnp.exp(sc - lse_h[:, 0:1])
                dp = lax.dot_general(doh, vw, (((1,), (1,)), ((), ())), preferred_element_type=F32)
                dsc = (p * (dp - dd_h[:, 0:1])).astype(BF16)
                pb = p.astype(BF16)
                dqs.append(jnp.dot(dsc, kw, preferred_element_type=F32))
                dk_h = lax.dot_general(dsc, qh, (((0,), (0,)), ((), ())), preferred_element_type=F32)
                dv_h = lax.dot_general(pb, doh, (((0,), (0,)), ((), ())), preferred_element_type=F32)
                dkw[kvh] = dk_h if dkw[kvh] is None else dkw[kvh] + dk_h
                dvw[kvh] = dv_h if dvw[kvh] is None else dvw[kvh] + dv_h
                psink = jnp.exp(sink_ref[h] - lse_h)
                ds_ref[h] += jnp.sum((-psink * dd_h).reshape(WINDOW // 8, 8, HEAD_DIM), axis=0)
            dq_ref[...] = jnp.concatenate(dqs, axis=1)
            win = jnp.concatenate(dkw + dvw, axis=1)
            first = win[0:WINDOW]
            second = win[WINDOW:2 * WINDOW]
            dkv_ref[...] = carry[...] + first
            carry[...] = jnp.where(n == 0, first, second)

        @pl.when(n == nb)
        def _():
            dkv_ref[...] = carry[...]

    hm = lambda heads: pl.BlockSpec((heads, WINDOW, HEAD_DIM), lambda n: (0, jnp.minimum(n, last), 0))
    res = lambda heads: pl.BlockSpec((heads, s, HEAD_DIM), lambda n: (0, 0, 0))
    outs, comm_outs = _hosted_call(
        body, comm, name=name, grid=(nb + 1,),
        in_specs=[smem, smem, hm(A_Q_HEADS), res(A_KV_HEADS), res(A_KV_HEADS), hm(A_Q_HEADS), hm(A_Q_HEADS), hm(A_Q_HEADS)],
        out_specs=[pl.BlockSpec((WINDOW, A_WIDTH), lambda n: (jnp.minimum(n, last), 0)),
                   pl.BlockSpec((WINDOW, 2 * A_KV_WIDTH), lambda n: (jnp.maximum(n - 1, 0), 0)),
                   pl.BlockSpec((A_Q_HEADS, 8, HEAD_DIM), lambda n: (0, 0, 0))],
        out_shape=[jax.ShapeDtypeStruct((s, A_WIDTH), F32), jax.ShapeDtypeStruct((s, 2 * A_KV_WIDTH), F32),
                   jax.ShapeDtypeStruct((A_Q_HEADS, 8, HEAD_DIM), F32)],
        scratch_shapes=[pltpu.VMEM((WINDOW, 2 * A_KV_WIDTH), F32)],
        args=[sinks, slopes, q, k, v, do, lse, dd], sem=("arbitrary",), vmem=VMEM_BIG)
    return outs[0], outs[1], outs[2], comm_outs


def _attn_b_fwd(q, k, v, c3, name, comm=None):
    heads, s, _ = q.shape
    bq = min(512, s)
    nq = s // bq
    nt = (((1,), (1,)), ((), ()))

    def body(q_ref, k_ref, v_ref, c_ref, o_ref, lse_ref, m_scr, l_scr, acc_scr):
        i = pl.program_id(1)
        r0 = pl.multiple_of(i * bq, bq)
        row = lax.broadcasted_iota(jnp.int32, (bq, bq), 0)
        col = lax.broadcasted_iota(jnp.int32, (bq, bq), 1)
        m_scr[...] = jnp.full((2, bq, LANES), NEG, F32)
        l_scr[...] = jnp.zeros((2, bq, LANES), F32)
        acc_scr[...] = jnp.zeros((2, bq, HEAD_DIM), F32)

        def step(j, masked):
            k0 = pl.multiple_of(j * bq, bq)
            for h2 in range(2):
                kv = k_ref[h2, pl.ds(k0, bq), :]
                vv = v_ref[h2, pl.ds(k0, bq), :]
                cq0 = c_ref[h2, :, pl.ds(r0, LANES)][:, 0:1]
                sc = lax.dot_general(q_ref[h2], kv, nt, preferred_element_type=F32)
                sc = sc + (cq0 - c_ref[h2, :, pl.ds(k0, bq)])
                if masked:
                    sc = jnp.where(col <= row, sc, NEG)
                m_prev = m_scr[h2]
                m_new = jnp.maximum(m_prev, jnp.max(sc, axis=1, keepdims=True))
                alpha = jnp.exp(m_prev - m_new)
                p = jnp.exp(sc - m_new[:, 0:1])
                l_scr[h2] = alpha * l_scr[h2] + jnp.sum(p, axis=1, keepdims=True)
                p_hi = p.astype(BF16)
                p_lo = (p - p_hi.astype(F32)).astype(BF16)
                pv = jnp.dot(p_hi, vv, preferred_element_type=F32) + jnp.dot(p_lo, vv, preferred_element_type=F32)
                acc_scr[h2] = acc_scr[h2] * alpha[:, 0:HEAD_DIM] + pv
                m_scr[h2] = m_new

        def loop_body(j, carry):
            step(j, False)
            return carry

        lax.fori_loop(0, i, loop_body, 0)
        step(i, True)
        outs = []
        for h2 in range(2):
            l = l_scr[h2]
            outs.append(acc_scr[h2] / l[:, 0:HEAD_DIM])
            lse_ref[h2] = (m_scr[h2] + jnp.log(l))[:, 0:HEAD_DIM]
        o_ref[...] = jnp.concatenate(outs, axis=1)

    res = pl.BlockSpec((2, s, HEAD_DIM), lambda hp, i: (hp, 0, 0))
    outs, comm_outs = _hosted_call(
        body, comm, name=name, grid=(heads // 2, nq),
        in_specs=[pl.BlockSpec((2, bq, HEAD_DIM), lambda hp, i: (hp, i, 0)), res, res,
                  pl.BlockSpec((2, 1, s), lambda hp, i: (hp, 0, 0))],
        out_specs=[pl.BlockSpec((bq, 2 * HEAD_DIM), lambda hp, i: (i, hp)),
                   pl.BlockSpec((2, bq, HEAD_DIM), lambda hp, i: (hp, i, 0))],
        out_shape=[jax.ShapeDtypeStruct((s, heads * HEAD_DIM), F32), jax.ShapeDtypeStruct((heads, s, HEAD_DIM), F32)],
        scratch_shapes=[pltpu.VMEM((2, bq, LANES), F32), pltpu.VMEM((2, bq, LANES), F32), pltpu.VMEM((2, bq, HEAD_DIM), F32)],
        args=[q, k, v, c3], sem=("parallel", "parallel"), vmem=VMEM_BIG)
    return outs[0], outs[1], comm_outs


def _attn_b_bwd(q, k, v, do, lse, dd, c3, name, comm=None):
    heads, s, _ = q.shape
    bq = min(512, s)
    nq = s // bq
    nt = (((1,), (1,)), ((), ()))
    tn = (((0,), (0,)), ((), ()))
    grid = (heads // 2, nq)

    def body(q_ref, k_ref, v_ref, do_ref, lse_ref, dd_ref, c_ref, dq_ref, dk_ref, dv_ref, dc_ref,
             dq_scr, dk_scr, dv_scr, dc_scr):
        j = pl.program_id(1)
        k0 = pl.multiple_of(j * bq, bq)
        row = lax.broadcasted_iota(jnp.int32, (bq, bq), 0)
        col = lax.broadcasted_iota(jnp.int32, (bq, bq), 1)

        @pl.when(j == 0)
        def _():
            dq_scr[...] = jnp.zeros(dq_scr.shape, F32)

        dk_scr[...] = jnp.zeros((2, bq, HEAD_DIM), F32)
        dv_scr[...] = jnp.zeros((2, bq, HEAD_DIM), F32)
        dc_scr[...] = jnp.zeros((2, 1, bq), F32)

        def step(i, masked):
            r0 = pl.multiple_of(i * bq, bq)
            for h2 in range(2):
                kv = k_ref[h2]
                vv = v_ref[h2]
                qv = q_ref[h2, pl.ds(r0, bq), :]
                dov = do_ref[h2, pl.ds(r0, bq), :]
                lse_v = lse_ref[h2, pl.ds(r0, bq), :][:, 0:1]
                dd_v = dd_ref[h2, pl.ds(r0, bq), :][:, 0:1]
                cq0 = c_ref[h2, :, pl.ds(r0, LANES)][:, 0:1]
                sc = lax.dot_general(qv, kv, nt, preferred_element_type=F32) + (cq0 - c_ref[h2, :, pl.ds(k0, bq)])
                if masked:
                    sc = jnp.where(col <= row, sc, NEG)
                p = jnp.exp(sc - lse_v)
                dp = lax.dot_general(dov, vv, nt, preferred_element_type=F32)
                dsc = p * (dp - dd_v)
                dsb = dsc.astype(BF16)
                dv_scr[h2] += lax.dot_general(p.astype(BF16), dov, tn, preferred_element_type=F32)
                dk_scr[h2] += lax.dot_general(dsb, qv, tn, preferred_element_type=F32)
                dq_scr[h2, pl.ds(r0, bq), :] += jnp.dot(dsb, kv, preferred_element_type=F32)
                dc_scr[h2] -= jnp.sum(dsc, axis=0, keepdims=True)

        def loop_body(i, carry):
            step(i, False)
            return carry

        step(j, True)
        lax.fori_loop(j + 1, nq, loop_body, 0)
        dc_ref[...] = dc_scr[...]
        dk_ref[...] = jnp.concatenate([dk_scr[0], dk_scr[1]], axis=1)
        dv_ref[...] = jnp.concatenate([dv_scr[0], dv_scr[1]], axis=1)

        @pl.when(j == nq - 1)
        def _():
            dq_ref[...] = jnp.concatenate([dq_scr[0], dq_scr[1]], axis=1)

    res = pl.BlockSpec((2, s, HEAD_DIM), lambda hp, j: (hp, 0, 0))
    blk = pl.BlockSpec((2, bq, HEAD_DIM), lambda hp, j: (hp, j, 0))
    tm = jax.ShapeDtypeStruct((s, heads * HEAD_DIM), F32)
    in_specs = [res, blk, blk, res, res, res, pl.BlockSpec((2, 1, s), lambda hp, j: (hp, 0, 0))]
    out_specs = [pl.BlockSpec((s, 2 * HEAD_DIM), lambda hp, j: (0, hp)),
                 pl.BlockSpec((bq, 2 * HEAD_DIM), lambda hp, j: (j, hp)),
                 pl.BlockSpec((bq, 2 * HEAD_DIM), lambda hp, j: (j, hp)),
                 pl.BlockSpec((2, 1, bq), lambda hp, j: (hp, 0, j))]
    out_shape = [tm, tm, tm, jax.ShapeDtypeStruct((heads, 1, s), F32)]
    scratch = [pltpu.VMEM((2, s, HEAD_DIM), F32), pltpu.VMEM((2, bq, HEAD_DIM), F32),
               pltpu.VMEM((2, bq, HEAD_DIM), F32), pltpu.VMEM((2, 1, bq), F32)]
    outs, comm_outs = _hosted_call(
        body, comm, name=name, grid=grid, in_specs=in_specs, out_specs=out_specs, out_shape=out_shape,
        scratch_shapes=scratch, args=[q, k, v, do, lse, dd, c3], sem=("parallel", "arbitrary"), vmem=VMEM_BIG)
    return outs[0], outs[1], outs[2], outs[3], comm_outs


def _attn_c_probs(qh, mkh):
    sc = lax.dot_general(qh, mkh, (((1,), (1,)), ((), ())), preferred_element_type=F32) * (C_HEAD_DIM ** -0.5)
    p = jnp.exp(sc - jnp.max(sc, axis=1, keepdims=True))
    return p / jnp.sum(p, axis=1, keepdims=True)


def _attn_c_fwd(q, mkv, name):
    s = q.shape[0]
    m = mkv.shape[0]
    bq = _tile(s, 512, 8)

    def body(q_ref, mk_ref, mv_ref, o_ref):
        outs = []
        for h in range(C_HEADS):
            sl = slice(h * C_HEAD_DIM, (h + 1) * C_HEAD_DIM)
            pn = _attn_c_probs(q_ref[:, sl], mk_ref[:, sl]).astype(BF16)
            outs.append(jnp.dot(pn, mv_ref[:, sl], preferred_element_type=F32))
        o_ref[...] = jnp.concatenate(outs, axis=1)

    return pl.pallas_call(
        body, name=name, grid=(s // bq,),
        in_specs=[pl.BlockSpec((bq, C_WIDTH), lambda i: (i, 0)), pl.BlockSpec((m, C_WIDTH), lambda i: (0, 0)),
                  pl.BlockSpec((m, C_WIDTH), lambda i: (0, 1))],
        out_specs=pl.BlockSpec((bq, C_WIDTH), lambda i: (i, 0)),
        out_shape=jax.ShapeDtypeStruct((s, C_WIDTH), F32),
        compiler_params=_params(("parallel",)),
    )(q, mkv, mkv)


def _attn_c_bwd(q, mkv, do, name):
    s = q.shape[0]
    m = mkv.shape[0]
    bq = _tile(s, 512, 8)
    tn = (((0,), (0,)), ((), ()))

    def body(q_ref, mk_ref, mv_ref, do_ref, dq_ref, dm_ref):
        i = pl.program_id(0)

        @pl.when(i == 0)
        def _():
            dm_ref[...] = jnp.zeros(dm_ref.shape, F32)

        dqs = []
        for h in range(C_HEADS):
            sl = slice(h * C_HEAD_DIM, (h + 1) * C_HEAD_DIM)
            qh, mkh, mvh, doh = q_ref[:, sl], mk_ref[:, sl], mv_ref[:, sl], do_ref[:, sl]
            pn = _attn_c_probs(qh, mkh)
            dp = lax.dot_general(doh, mvh, (((1,), (1,)), ((), ())), preferred_element_type=F32)
            dsc = (pn * (dp - jnp.sum(pn * dp, axis=1, keepdims=True)) * (C_HEAD_DIM ** -0.5)).astype(BF16)
            dqs.append(jnp.dot(dsc, mkh, preferred_element_type=F32))
            dm_ref[:, sl] += lax.dot_general(dsc, qh, tn, preferred_element_type=F32)
            sv = slice(C_WIDTH + h * C_HEAD_DIM, C_WIDTH + (h + 1) * C_HEAD_DIM)
            dm_ref[:, sv] += lax.dot_general(pn.astype(BF16), doh, tn, preferred_element_type=F32)
        dq_ref[...] = jnp.concatenate(dqs, axis=1)

    row = pl.BlockSpec((bq, C_WIDTH), lambda i: (i, 0))
    return pl.pallas_call(
        body, name=name, grid=(s // bq,),
        in_specs=[row, pl.BlockSpec((m, C_WIDTH), lambda i: (0, 0)), pl.BlockSpec((m, C_WIDTH), lambda i: (0, 1)), row],
        out_specs=[row, pl.BlockSpec((m, 2 * C_WIDTH), lambda i: (0, 0))],
        out_shape=[jax.ShapeDtypeStruct((s, C_WIDTH), F32), jax.ShapeDtypeStruct((m, 2 * C_WIDTH), F32)],
        compiler_params=_params(("arbitrary",)),
    )(q, mkv, mkv, do)


def _gate_fwd(y, proj, zc0, bw, name):
    rows, width = y.shape
    bm = _tile(rows, 2048 if bw <= 256 else 1024, 16)
    cb0 = zc0 // bw

    def body(y_ref, z_ref, o_ref):
        z = z_ref[...].astype(F32)
        o_ref[...] = (y_ref[...] * (z * _sigmoid(z))).astype(BF16)

    return pl.pallas_call(
        body, name=name, grid=(rows // bm, width // bw),
        in_specs=[pl.BlockSpec((bm, bw), lambda i, t: (i, t)), pl.BlockSpec((bm, bw), lambda i, t: (i, cb0 + t))],
        out_specs=pl.BlockSpec((bm, bw), lambda i, t: (i, t)),
        out_shape=jax.ShapeDtypeStruct((rows, width), BF16),
        compiler_params=_params(("parallel", "parallel")),
    )(y, proj)


def _gate_bwd(dsv, y, proj, zc0, bw, dproj, t0, head_major, name):
    rows, width = y.shape
    bm = _tile(rows, 2048 if bw <= 256 else 1024, 16)
    cb0 = zc0 // bw
    tb0 = t0 // bw
    bd = _block_diag(HEAD_DIM)
    hpb = bw // HEAD_DIM

    def body(*refs):
        if head_major:
            ds_ref, y_ref, z_ref, bd_ref, _, dp_ref, dy_ref, dd_ref = refs
        else:
            ds_ref, y_ref, z_ref, _, dp_ref, dy_ref = refs
        z = z_ref[...].astype(F32)
        sig = _sigmoid(z)
        dsx = ds_ref[...]
        yv = y_ref[...]
        dy = dsx * (z * sig)
        dp_ref[...] = (dsx * yv * (sig * (1.0 + z * (1.0 - sig)))).astype(BF16)
        if head_major:
            dyb = dy.astype(BF16)
            dd = _seg_sum(dyb.astype(F32) * yv, bd_ref[...])
            for h in range(hpb):
                sl = slice(h * HEAD_DIM, (h + 1) * HEAD_DIM)
                dy_ref[h] = dyb[:, sl]
                dd_ref[h] = dd[:, sl]
        else:
            dy_ref[...] = dy.astype(BF16)

    tile = pl.BlockSpec((bm, bw), lambda i, t: (i, t))
    ztile = pl.BlockSpec((bm, bw), lambda i, t: (i, cb0 + t))
    ttile = pl.BlockSpec((bm, bw), lambda i, t: (i, tb0 + t))
    any_spec = pl.BlockSpec(memory_space=pl.ANY)
    dp_shape = jax.ShapeDtypeStruct(dproj.shape, BF16)
    if head_major:
        hm_spec = pl.BlockSpec((hpb, bm, HEAD_DIM), lambda i, t: (t, i, 0))
        nh = width // HEAD_DIM
        outs = pl.pallas_call(
            body, name=name, grid=(rows // bm, width // bw),
            in_specs=[tile, tile, ztile, pl.BlockSpec((LANES, LANES), lambda i, t: (0, 0)), any_spec],
            out_specs=[ttile, hm_spec, hm_spec],
            out_shape=[dp_shape, jax.ShapeDtypeStruct((nh, rows, HEAD_DIM), BF16),
                       jax.ShapeDtypeStruct((nh, rows, HEAD_DIM), F32)],
            input_output_aliases={4: 0},
            compiler_params=_params(("parallel", "parallel")),
        )(dsv, y, proj, bd, dproj)
        return outs[0], outs[1], outs[2]
    outs = pl.pallas_call(
        body, name=name, grid=(rows // bm, width // bw),
        in_specs=[tile, tile, ztile, any_spec],
        out_specs=[ttile, tile],
        out_shape=[dp_shape, jax.ShapeDtypeStruct((rows, width), BF16)],
        input_output_aliases={3: 0},
        compiler_params=_params(("parallel", "parallel")),
    )(dsv, y, proj, dproj)
    return outs[0], outs[1], None


def _merge_fwd(proj, ua, ub, uc, name):
    rows, d = ua.shape
    bm = _tile(rows, 1024, 16)
    bw = _tile(d, 512)
    g0 = COL_GATE // bw
    gstep = d // bw

    def body(la_ref, lb_ref, lc_ref, ua_ref, ub_ref, uc_ref, o_ref, ga_ref, gb_ref, gc_ref):
        y = None
        for l_ref, u_ref, g_ref in ((la_ref, ua_ref, ga_ref), (lb_ref, ub_ref, gb_ref), (lc_ref, uc_ref, gc_ref)):
            g = _sigmoid(l_ref[...].astype(F32))
            g_ref[...] = g.astype(BF16)
            term = g * u_ref[...].astype(F32)
            y = term if y is None else y + term
        o_ref[...] = y.astype(BF16)

    tile = pl.BlockSpec((bm, bw), lambda i, t: (i, t))
    gate = lambda b: pl.BlockSpec((bm, bw), lambda i, t: (i, g0 + b * gstep + t))
    shape = jax.ShapeDtypeStruct((rows, d), BF16)
    return pl.pallas_call(
        body, name=name, grid=(rows // bm, d // bw),
        in_specs=[gate(0), gate(1), gate(2), tile, tile, tile],
        out_specs=[tile] * 4, out_shape=[shape] * 4,
        compiler_params=_params(("parallel", "parallel")),
    )(proj, proj, proj, ua, ub, uc)


def _merge_bwd(dym, us, gs, name):
    rows, d = dym.shape
    bm = _tile(rows, 1024, 16)
    bw = _tile(d, 512)
    nb = d // bw

    def body(dy_ref, ua_ref, ub_ref, uc_ref, ga_ref, gb_ref, gc_ref, dg_ref, da_ref, db_ref, dc_ref):
        b = pl.program_id(2)
        dyv = dy_ref[...]
        for idx, (u_ref, g_ref, du_ref) in enumerate(((ua_ref, ga_ref, da_ref), (ub_ref, gb_ref, db_ref), (uc_ref, gc_ref, dc_ref))):
            @pl.when(b == idx)
            def _():
                g = g_ref[...].astype(F32)
                du_ref[...] = (g * dyv).astype(BF16)
                dg_ref[...] = (dyv * u_ref[...].astype(F32) * g * (1.0 - g)).astype(BF16)

    tile = pl.BlockSpec((bm, bw), lambda i, t, b: (i, t))
    shape = jax.ShapeDtypeStruct((rows, d), BF16)
    outs = pl.pallas_call(
        body, name=name, grid=(rows // bm, nb, 3),
        in_specs=[tile] * 7,
        out_specs=[pl.BlockSpec((bm, bw), lambda i, t, b: (i, b * nb + t)), tile, tile, tile],
        out_shape=[jax.ShapeDtypeStruct((rows, 3 * d), BF16), shape, shape, shape],
        compiler_params=_params(("parallel", "parallel", "arbitrary")),
    )(dym, *us, *gs)
    return outs[0], outs[1], outs[2], outs[3]


def _loss_head(y, target, name):
    rows, d = y.shape
    bm = _tile(rows, 256, 8)

    def body(y_ref, t_ref, dy_ref, dyb_ref, l_ref):
        i = pl.program_id(0)
        diff = y_ref[...] - t_ref[...]
        dy = diff * (1.0 / d)
        dy_ref[...] = dy
        dyb_ref[...] = dy.astype(BF16)
        sq = diff * diff
        part = sq[:, 0:LANES]
        for c in range(1, d // LANES):
            part = part + sq[:, c * LANES:(c + 1) * LANES]
        part = jnp.sum(part.reshape(bm // 8, 8, LANES), axis=0)

        @pl.when(i == 0)
        def _():
            l_ref[...] = part

        @pl.when(i > 0)
        def _():
            l_ref[...] += part

    row = pl.BlockSpec((bm, d), lambda i: (i, 0))
    return pl.pallas_call(
        body, name=name, grid=(rows // bm,), in_specs=[row, row],
        out_specs=[row, row, pl.BlockSpec((8, LANES), lambda i: (0, 0))],
        out_shape=[jax.ShapeDtypeStruct((rows, d), F32), jax.ShapeDtypeStruct((rows, d), BF16),
                   jax.ShapeDtypeStruct((8, LANES), F32)],
        compiler_params=_params(("arbitrary",)),
    )(y, target)


def _row(vec, reps=1):
    return jnp.tile(vec.reshape(1, -1).astype(F32), (1, reps))


def _local_step(x, mem, target, small, wg, shards=None):
    s, d = x.shape
    dist = shards is not None
    wg = dict(wg)
    ones = lambda n: jnp.ones((1, n), F32)
    zeros = lambda n: jnp.zeros((1, n), F32)
    scale_ab = HEAD_DIM ** -0.5
    split8 = lambda g: g.reshape(N_DEV, g.shape[0] // N_DEV, g.shape[1])
    flat8 = lambda g: g.reshape(g.shape[0] * g.shape[1], g.shape[2])
    gather = lambda names: _Comm("gather", [shards[n] for n in names]) if dist else None
    g = {}

    def scatter(names):
        return _Comm("scatter", [split8(g[n]) for n in names]) if dist else None

    def hosted(result, names, store):
        if not dist:
            return result
        out, got = result
        store.update(zip(names, got))
        return out

    hn = _rmsnorm_fwd(x, small["norm_gain"], "rms_x_fwd")
    got = {}
    proj = hosted(_mm_nn(hn, wg["qkv"], bm=1024, bn=1024, bk=d, o_dtype=BF16, name="proj_qkv",
                         comm=gather(("wa", "wb", "wc"))), ("wa", "wb", "wc"), got)
    wg.update({n: flat8(a) for n, a in got.items()})
    pfb = _mm_nn(hn, wg["wf"], bm=1024, bn=FB_PAD, bk=d, o_dtype=F32, name="proj_fb")
    mn = _rmsnorm_fwd(mem, small["mem_norm_gain"], "rms_mem_fwd")
    mkv = _mm_nn(mn, wg["wk"], bm=256, bn=1024, bk=d, o_dtype=F32, name="mem_kv")

    gain_a = jnp.concatenate([_row(small["q_gain_a"], A_Q_HEADS) * scale_ab, _row(small["k_gain_a"], A_KV_HEADS), ones(A_KV_WIDTH)], axis=1)
    flag_a = jnp.concatenate([ones(A_WIDTH + A_KV_WIDTH), zeros(A_KV_WIDTH)], axis=1)
    qkv_a = _headnorm_fwd(proj, COL_QA, 1280, 1280, HEAD_DIM, gain_a, flag_a, True, "hn_a_fwd")
    gain_b = jnp.concatenate([_row(small["q_gain_b"], B_HEADS) * scale_ab, _row(small["k_gain_b"], B_HEADS), ones(B_WIDTH)], axis=1)
    flag_b = jnp.concatenate([ones(2 * B_WIDTH), zeros(B_WIDTH)], axis=1)
    qkv_b = _headnorm_fwd(proj, COL_QB, 2304, 256, HEAD_DIM, gain_b, flag_b, True, "hn_b_fwd")
    gain_cq = _row(small["q_gain_c"], C_HEADS)
    q_c = _headnorm_fwd(proj, COL_QC, C_WIDTH, C_WIDTH, C_HEAD_DIM, gain_cq, ones(C_WIDTH), False, "hn_cq_fwd")
    gain_ck = jnp.concatenate([_row(small["k_gain_c"], C_HEADS), ones(C_WIDTH)], axis=1)
    flag_ck = jnp.concatenate([ones(C_WIDTH), zeros(C_WIDTH)], axis=1)
    mkvn = _headnorm_fwd(mkv, 0, 2 * C_WIDTH, 2 * C_WIDTH, C_HEAD_DIM, gain_ck, flag_ck, False, "hn_ck_fwd")

    q_a, k_a, v_a = qkv_a[0:12], qkv_a[12:16], qkv_a[16:20]
    q_b, k_b, v_b = qkv_b[0:12], qkv_b[12:24], qkv_b[24:36]

    bpad = jnp.pad(small["b_forget"].reshape(1, -1), ((0, 0), (0, FB_PAD - B_HEADS)))
    c16 = _fox_prep(pfb, bpad, "fox_prep")
    c3 = c16[0:B_HEADS].reshape(B_HEADS, 1, s)

    sinks = small["sinks_a"].reshape(-1)
    slopes = jnp.exp2(-8.0 * jnp.arange(1, A_Q_HEADS + 1, dtype=F32) / A_Q_HEADS)
    y_a, lse_a = _attn_a_fwd(q_a, k_a, v_a, sinks, slopes, "attn_a_fwd")
    y_b, lse_b, got_zg = _attn_b_fwd(q_b, k_b, v_b, c3, "attn_b_fwd", comm=gather(("zg",)))
    if dist:
        wg["zg"] = flat8(got_zg[0])
    y_c = _attn_c_fwd(q_c, mkvn, "attn_c_fwd")

    got = {}
    pzg = hosted(_mm_nn(hn, wg["zg"], bm=1024, bn=1024, bk=d, o_dtype=BF16, name="proj_zg", comm=gather(("wo",))),
                 ("wo",), got)
    wg.update({n: flat8(a) for n, a in got.items()})

    s_a = _gate_fwd(y_a, pzg, COL_ZA, 256, "gate_a_fwd")
    s_b = _gate_fwd(y_b, pzg, COL_ZB, 256, "gate_b_fwd")
    s_c = _gate_fwd(y_c, pzg, COL_ZC, 512, "gate_c_fwd")
    w_a, w_b, w_c = _branch_full(wg["wa"]), _branch_full(wg["wb"]), _branch_full(wg["wc"])
    u_a = _mm_nn(s_a, w_a, bm=1024, bn=2048, bk=A_WIDTH, o_dtype=BF16, name="branch_a_fwd")
    u_b = _mm_nn(s_b, w_b, bm=1024, bn=2048, bk=B_WIDTH, o_dtype=BF16, name="branch_b_fwd")
    u_c = _mm_nn(s_c, w_c, bm=1024, bn=2048, bk=C_WIDTH, o_dtype=BF16, name="branch_c_fwd")
    ym, gate_a, gate_b, gate_c = _merge_fwd(pzg, u_a, u_b, u_c, "merge_fwd")
    y = _mm_nn(ym, wg["wo"], bm=1024, bn=1024, bk=d, o_dtype=F32, name="out_proj", add=x)
    dy, dyb, lpart = _loss_head(y, target, "loss_head")
    loss = 0.5 / d * jnp.sum(lpart)

    dym = _mm_nt(dyb, wg["wo"], bm=1024, bn=1024, bk=d, o_dtype=F32, name="out_proj_bwd_act")
    g["wo"] = _mm_tn(ym, dyb, bm=512, bn=1024, bk=s, o_dtype=BF16, name="out_proj_bwd_w")

    dgate, du_a, du_b, du_c = _merge_bwd(dym, (u_a, u_b, u_c), (gate_a, gate_b, gate_c), "merge_bwd")
    parts = {}
    g["wm_g"] = hosted(_mm_tn(hn, dgate, bm=512, bn=1024, bk=s, o_dtype=BF16, name="proj_gate_bwd_w",
                              comm=scatter(("wo",))), ("wo",), parts)

    ds_a = _mm_nt(du_a, w_a, bm=1024, bn=A_WIDTH, bk=d, o_dtype=F32, name="branch_a_bwd_act")
    ds_b = _mm_nt(du_b, w_b, bm=1024, bn=B_WIDTH, bk=d, o_dtype=F32, name="branch_b_bwd_act")
    ds_c = _mm_nt(du_c, w_c, bm=1024, bn=C_WIDTH, bk=d, o_dtype=F32, name="branch_c_bwd_act")
    g["wa"] = _branch_shards(_mm_tn(s_a, du_a, bm=A_WIDTH, bn=1024, bk=s, o_dtype=BF16, name="branch_a_bwd_w"))
    g["wb"] = _branch_shards(_mm_tn(s_b, du_b, bm=B_WIDTH, bn=1024, bk=s, o_dtype=BF16, name="branch_b_bwd_w"))
    g["wc"] = _branch_shards(_mm_tn(s_c, du_c, bm=C_WIDTH, bn=1024, bk=s, o_dtype=BF16, name="branch_c_bwd_w"))

    dz = lax.empty((s, W_Z), BF16)
    dz, do_a, dd_a = _gate_bwd(ds_a, y_a, pzg, COL_ZA, 256, dz, COL_ZA, True, "gate_a_bwd")
    dz, do_b, dd_b = _gate_bwd(ds_b, y_b, pzg, COL_ZB, 256, dz, COL_ZB, True, "gate_b_bwd")
    dz, do_c, _ = _gate_bwd(ds_c, y_c, pzg, COL_ZC, 512, dz, COL_ZC, False, "gate_c_bwd")
    g["wm_z"] = _mm_tn(hn, dz, bm=512, bn=1024, bk=s, o_dtype=BF16, name="proj_z_bwd_w")

    names = ("wa", "wb", "wc")
    dq_a, dkv_a, dsink, got = _attn_a_bwd(q_a, k_a, v_a, do_a, lse_a, dd_a, sinks, slopes, "attn_a_bwd", comm=scatter(names))
    parts.update(zip(names, got))
    names = ("wm_g", "wm_z")
    dq_b, dk_b, dv_b, dc3, got = _attn_b_bwd(q_b, k_b, v_b, do_b, lse_b, dd_b, c3, "attn_b_bwd", comm=scatter(names))
    parts.update(zip(names, got))
    dq_c, dmkvn = _attn_c_bwd(q_c, mkvn, do_c, "attn_c_bwd")

    dqkv = lax.empty((s, W_QKV), BF16)
    dqkv, dg_qa = _headnorm_bwd(proj, COL_QA, A_WIDTH, 256, HEAD_DIM, gain_a[:, 0:768], flag_a[:, 0:768], dq_a, dqkv, COL_QA, "hn_qa_bwd")
    dqkv, dg_kva = _headnorm_bwd(proj, COL_KA, 512, 256, HEAD_DIM, gain_a[:, 768:1280], flag_a[:, 768:1280], dkv_a, dqkv, COL_KA, "hn_kva_bwd")
    dqkv, dg_qb = _headnorm_bwd(proj, COL_QB, B_WIDTH, 256, HEAD_DIM, gain_b[:, 0:768], flag_b[:, 0:768], dq_b, dqkv, COL_QB, "hn_qb_bwd")
    dqkv, dg_kb = _headnorm_bwd(proj, COL_KB, B_WIDTH, 256, HEAD_DIM, gain_b[:, 768:1536], flag_b[:, 768:1536], dk_b, dqkv, COL_KB, "hn_kb_bwd")
    dqkv, _ = _headnorm_bwd(proj, COL_VB, B_WIDTH, 256, HEAD_DIM, gain_b[:, 1536:2304], flag_b[:, 1536:2304], dv_b, dqkv, COL_VB, "hn_vb_bwd")
    dqkv, dg_qc = _headnorm_bwd(proj, COL_QC, C_WIDTH, 512, C_HEAD_DIM, gain_cq, ones(C_WIDTH), dq_c, dqkv, COL_QC, "hn_qc_bwd")
    dmkv, dg_kc = _headnorm_bwd(mkv, 0, 2 * C_WIDTH, 2 * C_WIDTH, C_HEAD_DIM, gain_ck, flag_ck, dmkvn, None, 0, "hn_kc_bwd")

    dct = jnp.pad(dc3.reshape(B_HEADS, s), ((0, 16 - B_HEADS), (0, 0)))
    dfb, dbf = _fox_prep_bwd(pfb, bpad, dct, "fox_prep_bwd")

    dmn = _mm_nt(dmkv, wg["wk"], bm=256, bn=1024, bk=1024, o_dtype=F32, name="mem_kv_bwd_act")
    g["wk"] = _mm_tn(mn, dmkv, bm=512, bn=1024, bk=mem.shape[0], o_dtype=BF16, name="mem_kv_bwd_w")
    _, dg_mem = _rmsnorm_bwd(mem, dmn, small["mem_norm_gain"], None, "rms_mem_bwd")

    g["wm_qkv"] = _mm_tn(hn, dqkv, bm=512, bn=1024, bk=s, o_dtype=BF16, name="proj_qkv_bwd_w")
    g["wf"] = _mm_tn(hn, dfb, bm=512, bn=FB_PAD, bk=s, o_dtype=BF16, name="proj_fb_bwd_w")
    dhn = _mm_nt(dqkv, wg["qkv"], bm=1024, bn=1024, bk=2048, o_dtype=F32, name="proj_qkv_bwd_act")
    dhn = _mm_nt(dfb, wg["wf"], bm=1024, bn=1024, bk=FB_PAD, o_dtype=F32, name="proj_fb_bwd_act", add=dhn)
    names = ("wm_qkv", "wf", "wk")
    dhn = hosted(_mm_nt_cat(dz, dgate, wg["zg"], bm=1024, bn=1024, bk=2048, name="proj_zg_bwd_act", add=dhn,
                            comm=scatter(names)), names, parts)
    if dist:
        g = parts
    grad_x, dg_x = _rmsnorm_bwd(x, dhn, small["norm_gain"], dy, "rms_x_bwd")

    fold = lambda part, heads, hd: jnp.sum(jnp.sum(part, axis=0).reshape(heads, hd), axis=0).reshape(1, hd)
    small_grads = {
        "norm_gain": jnp.sum(dg_x, axis=0).reshape(1, d),
        "mem_norm_gain": jnp.sum(dg_mem, axis=0).reshape(1, d),
        "b_forget": dbf[0:B_HEADS, 0].reshape(1, B_HEADS),
        "q_gain_a": fold(dg_qa, A_Q_HEADS, HEAD_DIM) * scale_ab,
        "k_gain_a": fold(dg_kva[:, 0:A_KV_WIDTH], A_KV_HEADS, HEAD_DIM),
        "sinks_a": (jnp.sum(dsink, axis=(1, 2)) * (1.0 / HEAD_DIM)).reshape(1, A_Q_HEADS),
        "q_gain_b": fold(dg_qb, B_HEADS, HEAD_DIM) * scale_ab,
        "k_gain_b": fold(dg_kb, B_HEADS, HEAD_DIM),
        "q_gain_c": fold(dg_qc, C_HEADS, C_HEAD_DIM),
        "k_gain_c": fold(dg_kc[:, 0:C_WIDTH], C_HEADS, C_HEAD_DIM),
    }
    return loss, grad_x, small_grads, g


def _coords():
    return lax.axis_index("x"), lax.axis_index("y"), lax.axis_index("c")


def _all_gather(shards, name):
    n = len(shards)

    def body(*refs):
        ins = refs[0:n]
        outs = refs[n:2 * n]
        send_sems, recv_sems, local_sems = refs[2 * n:2 * n + 3]
        x, y, c = _coords()
        me, sibling = (x, y, c), (x, y, 1 - c)
        chips = [(1 - x, y), (x, 1 - y), (1 - x, 1 - y)]
        idx = lambda p: 4 * p[0] + 2 * p[1] + p[2]

        def copy(a, k, block, to, src=None):
            slot = outs[a].at[idx(block)]
            return pltpu.make_async_remote_copy(
                src_ref=slot if src is None else src, dst_ref=slot,
                send_sem=send_sems.at[a, k], recv_sem=recv_sems.at[a, k], device_id=to, device_id_type=MESH)

        mine = [pltpu.make_async_copy(ins[a], outs[a].at[idx(me)], local_sems.at[a]) for a in range(n)]
        for cp in mine:
            cp.start()
        first = []
        for a in range(n):
            first.append(copy(a, 0, me, sibling, src=ins[a]))
            first += [copy(a, 1 + j, me, (*chip, c), src=ins[a]) for j, chip in enumerate(chips)]
        for cp in first:
            cp.start()
        passed = []
        for j, chip in enumerate(chips):
            for a in range(n):
                copy(a, 1 + j, (*chip, c), me).wait_recv()
                fwd = copy(a, 4 + j, (*chip, c), sibling)
                fwd.start()
                passed.append(fwd)
        for a in range(n):
            copy(a, 0, sibling, me).wait_recv()
            for j, chip in enumerate(chips):
                copy(a, 4 + j, (*chip, 1 - c), me).wait_recv()
        for cp in first + passed:
            cp.wait_send()
        for cp in mine:
            cp.wait()

    any_spec = pl.BlockSpec(memory_space=pl.ANY)
    return pl.pallas_call(
        body, name=name,
        in_specs=[any_spec] * n, out_specs=[any_spec] * n,
        out_shape=[jax.ShapeDtypeStruct((N_DEV,) + sh.shape, sh.dtype) for sh in shards],
        scratch_shapes=[pltpu.SemaphoreType.DMA((n, 7)), pltpu.SemaphoreType.DMA((n, 7)), pltpu.SemaphoreType.DMA((n,))],
    )(*shards)


def _all_reduce_small(vec, name):
    p = vec.shape[1]

    def body(v_ref, o_ref, gather, send_sems, recv_sems):
        x, y, c = _coords()
        my = 4 * x + 2 * y + c
        peers = [(x ^ ((k >> 2) & 1), y ^ ((k >> 1) & 1), c ^ (k & 1)) for k in range(1, N_DEV)]
        gather[my] = v_ref[...]
        sends = [pltpu.make_async_remote_copy(
            src_ref=v_ref, dst_ref=gather.at[my], send_sem=send_sems.at[k], recv_sem=recv_sems.at[k],
            device_id=peer, device_id_type=MESH) for k, peer in enumerate(peers)]
        for cp in sends:
            cp.start()
        for k, peer in enumerate(peers):
            pid = 4 * peer[0] + 2 * peer[1] + peer[2]
            pltpu.make_async_remote_copy(
                src_ref=v_ref, dst_ref=gather.at[pid], send_sem=send_sems.at[k], recv_sem=recv_sems.at[k],
                device_id=peer, device_id_type=MESH).wait_recv()
        for cp in sends:
            cp.wait_send()
        total = gather[0]
        for j in range(1, N_DEV):
            total = total + gather[j]
        o_ref[...] = total

    vm = pl.BlockSpec(memory_space=pltpu.VMEM)
    return pl.pallas_call(
        body, name=name, in_specs=[vm], out_specs=vm,
        out_shape=jax.ShapeDtypeStruct((8, p), F32),
        scratch_shapes=[pltpu.VMEM((N_DEV, 8, p), F32), pltpu.SemaphoreType.DMA((7,)), pltpu.SemaphoreType.DMA((7,))],
    )(vec)[0:1]


def _sum_parts(parts, name):
    _, rows, cols = parts.shape
    br = _tile(rows, 64, 16)

    def body(p_ref, o_ref):
        total = p_ref[0].astype(F32)
        for j in range(1, N_DEV):
            total = total + p_ref[j].astype(F32)
        o_ref[...] = total

    return pl.pallas_call(
        body, name=name, grid=(rows // br,),
        in_specs=[pl.BlockSpec((N_DEV, br, cols), lambda i: (0, i, 0))],
        out_specs=pl.BlockSpec((br, cols), lambda i: (i, 0)),
        out_shape=jax.ShapeDtypeStruct((rows, cols), F32),
        compiler_params=_params(("parallel",), VMEM_BIG),
    )(parts)


def _adamw(w, g, m, v, name, br=32):
    rows, cols = w.shape
    br = min(br, rows)
    c1 = 1.0 / (1.0 - ADAM_B1 ** ADAM_STEP)
    c2 = 1.0 / (1.0 - ADAM_B2 ** ADAM_STEP)

    def body(w_ref, g_ref, m_ref, v_ref, d_ref, nm_ref, nv_ref):
        gv = g_ref[...]
        nm = ADAM_B1 * m_ref[...] + (1.0 - ADAM_B1) * gv
        nv = ADAM_B2 * v_ref[...] + (1.0 - ADAM_B2) * (gv * gv)
        d_ref[...] = -ADAM_LR * ((nm * c1) / (jnp.sqrt(nv * c2) + ADAM_EPS) + ADAM_WD * w_ref[...])
        nm_ref[...] = nm
        nv_ref[...] = nv

    spec = pl.BlockSpec((br, cols), lambda i: (i, 0))
    shape = jax.ShapeDtypeStruct((rows, cols), F32)
    return pl.pallas_call(
        body, name=name, grid=(pl.cdiv(rows, br),), in_specs=[spec] * 4, out_specs=[spec] * 3, out_shape=[shape] * 3,
        compiler_params=_params(("parallel",), VMEM_BIG),
    )(w, g, m, v)


def _adamw_parts(w, parts, m, v, name):
    rows, cols = w.shape
    br = _tile(rows, 32, 16)
    c1 = 1.0 / (1.0 - ADAM_B1 ** ADAM_STEP)
    c2 = 1.0 / (1.0 - ADAM_B2 ** ADAM_STEP)

    def body(w_ref, p_ref, m_ref, v_ref, g_ref, d_ref, nm_ref, nv_ref):
        gv = p_ref[0].astype(F32)
        for j in range(1, N_DEV):
            gv = gv + p_ref[j].astype(F32)
        nm = ADAM_B1 * m_ref[...] + (1.0 - ADAM_B1) * gv
        nv = ADAM_B2 * v_ref[...] + (1.0 - ADAM_B2) * (gv * gv)
        g_ref[...] = gv
        d_ref[...] = -ADAM_LR * ((nm * c1) / (jnp.sqrt(nv * c2) + ADAM_EPS) + ADAM_WD * w_ref[...])
        nm_ref[...] = nm
        nv_ref[...] = nv

    spec = pl.BlockSpec((br, cols), lambda i: (i, 0))
    shape = jax.ShapeDtypeStruct((rows, cols), F32)
    return pl.pallas_call(
        body, name=name, grid=(rows // br,),
        in_specs=[spec, pl.BlockSpec((N_DEV, br, cols), lambda i: (0, i, 0)), spec, spec],
        out_specs=[spec] * 4, out_shape=[shape] * 4,
        compiler_params=_params(("parallel",), VMEM_BIG),
    )(w, parts, m, v)


SMALL_NAMES = ("norm_gain", "mem_norm_gain", "b_forget", "q_gain_a", "k_gain_a", "sinks_a",
               "q_gain_b", "k_gain_b", "q_gain_c", "k_gain_c")
BIG_NAMES = ("w_in", "w_mem_kv", "w_branch_a", "w_branch_b", "w_branch_c", "w_out")
WEIGHT_ORDER = ("norm_gain", "mem_norm_gain", "w_in", "b_forget", "q_gain_a", "k_gain_a", "sinks_a", "q_gain_b",
                "k_gain_b", "q_gain_c", "k_gain_c", "w_mem_kv", "w_branch_a", "w_branch_b", "w_branch_c", "w_out")


def _pack_small(tree):
    flat = jnp.concatenate([tree[n].reshape(1, -1) for n in SMALL_NAMES], axis=1)
    pad = (-flat.shape[1]) % LANES
    return jnp.pad(flat, ((0, 0), (0, pad)))


def _unpack_small(flat, like):
    out, off = {}, 0
    for n in SMALL_NAMES:
        size = like[n].size
        out[n] = flat[:, off:off + size].reshape(like[n].shape)
        off += size
    return out


def kernel(x, mem, norm_gain, mem_norm_gain, w_in, b_forget, q_gain_a, k_gain_a, sinks_a, q_gain_b, k_gain_b, q_gain_c, k_gain_c, w_mem_kv, w_branch_a, w_branch_b, w_branch_c, w_out, loss_target, m_norm_gain, m_mem_norm_gain, m_w_in, m_b_forget, m_q_gain_a, m_k_gain_a, m_sinks_a, m_q_gain_b, m_k_gain_b, m_q_gain_c, m_k_gain_c, m_w_mem_kv, m_w_branch_a, m_w_branch_b, m_w_branch_c, m_w_out, v_norm_gain, v_mem_norm_gain, v_w_in, v_b_forget, v_q_gain_a, v_k_gain_a, v_sinks_a, v_q_gain_b, v_k_gain_b, v_q_gain_c, v_k_gain_c, v_w_mem_kv, v_w_branch_a, v_w_branch_b, v_w_branch_c, v_w_out):
    weights = dict(norm_gain=norm_gain, mem_norm_gain=mem_norm_gain, w_in=w_in, b_forget=b_forget, q_gain_a=q_gain_a,
                   k_gain_a=k_gain_a, sinks_a=sinks_a, q_gain_b=q_gain_b, k_gain_b=k_gain_b, q_gain_c=q_gain_c,
                   k_gain_c=k_gain_c, w_mem_kv=w_mem_kv, w_branch_a=w_branch_a, w_branch_b=w_branch_b,
                   w_branch_c=w_branch_c, w_out=w_out)
    mom_m = dict(norm_gain=m_norm_gain, mem_norm_gain=m_mem_norm_gain, w_in=m_w_in, b_forget=m_b_forget,
                 q_gain_a=m_q_gain_a, k_gain_a=m_k_gain_a, sinks_a=m_sinks_a, q_gain_b=m_q_gain_b, k_gain_b=m_k_gain_b,
                 q_gain_c=m_q_gain_c, k_gain_c=m_k_gain_c, w_mem_kv=m_w_mem_kv, w_branch_a=m_w_branch_a,
                 w_branch_b=m_w_branch_b, w_branch_c=m_w_branch_c, w_out=m_w_out)
    mom_v = dict(norm_gain=v_norm_gain, mem_norm_gain=v_mem_norm_gain, w_in=v_w_in, b_forget=v_b_forget,
                 q_gain_a=v_q_gain_a, k_gain_a=v_k_gain_a, sinks_a=v_sinks_a, q_gain_b=v_q_gain_b, k_gain_b=v_k_gain_b,
                 q_gain_c=v_q_gain_c, k_gain_c=v_k_gain_c, w_mem_kv=v_w_mem_kv, w_branch_a=v_w_branch_a,
                 w_branch_b=v_w_branch_b, w_branch_c=v_w_branch_c, w_out=v_w_out)
    wi = w_in[0]
    sh_qkv = jnp.concatenate([wi[:, a:b] for a, b in SRC_RANGES[0:3]], axis=1).astype(BF16)
    sh_zg = jnp.concatenate([wi[:, a:b] for a, b in SRC_RANGES[3:6]] + [wi[:, SRC_GATE:]], axis=1).astype(BF16)
    sh_wf = jnp.pad(wi[:, FB_SRC:FB_SRC + B_HEADS], ((0, 0), (0, FB_PAD - B_HEADS))).astype(BF16)
    shards = {"zg": sh_zg, "wo": w_out[0].astype(BF16), "wa": w_branch_a[0].astype(BF16),
              "wb": w_branch_b[0].astype(BF16), "wc": w_branch_c[0].astype(BF16)}
    first = ("qkv", "wf", "wk")
    full = _all_gather([sh_qkv, sh_wf, w_mem_kv[0].astype(BF16)], "weights_all_gather")
    wg = {kname: arr.reshape(arr.shape[0] * arr.shape[1], arr.shape[2]) for kname, arr in zip(first, full)}

    small = {n: weights[n] for n in SMALL_NAMES}
    loss_local, grad_x, small_g, parts = _local_step(x[0], mem[0], loss_target[0], small, wg, shards)

    grads, delta, new_m, new_v = {}, {}, {}, {}
    for n, kname in (("w_mem_kv", "wk"), ("w_out", "wo"), ("w_branch_a", "wa"), ("w_branch_b", "wb"), ("w_branch_c", "wc")):
        gsum, dlt, nm, nv = _adamw_parts(weights[n][0], parts[kname], mom_m[n][0], mom_v[n][0], "adamw_" + n)
        grads[n], delta[n], new_m[n], new_v[n] = gsum, dlt[None], nm[None], nv[None]
    gq, gz, gf, gg = (_sum_parts(parts[k], "grad_sum_" + k) for k in ("wm_qkv", "wm_z", "wf", "wm_g"))
    g_in = jnp.concatenate([gq[:, COL_QA:COL_QB], gz[:, COL_ZA:COL_ZB], gq[:, COL_QB:COL_QC], gz[:, COL_ZB:COL_ZC],
                            gf[:, 0:B_HEADS], gq[:, COL_QC:W_QKV], gz[:, COL_ZC:W_Z], gg], axis=1)
    dlt, nm, nv = _adamw(w_in[0], g_in, m_w_in[0], v_w_in[0], "adamw_w_in")
    grads["w_in"], delta["w_in"], new_m["w_in"], new_v["w_in"] = g_in, dlt[None], nm[None], nv[None]

    packed = _pack_small(small_g)
    reduced = _all_reduce_small(jnp.broadcast_to(packed, (8, packed.shape[1])), "small_all_reduce")
    grads.update(_unpack_small(reduced, small))

    loss = lax.psum(loss_local, ("x", "y", "c"))

    pw, pm, pv = _pack_small(small), _pack_small({n: mom_m[n] for n in SMALL_NAMES}), _pack_small({n: mom_v[n] for n in SMALL_NAMES})
    rep8 = lambda a: jnp.broadcast_to(a, (8, a.shape[1]))
    dlt, nm, nv = _adamw(rep8(pw), rep8(reduced), rep8(pm), rep8(pv), "adamw_small")
    for tree, flat in ((delta, dlt), (new_m, nm), (new_v, nv)):
        tree.update(_unpack_small(flat[0:1], small))
    for n in BIG_NAMES:
        grads[n] = grads[n][None]
    return (loss, grad_x[None], *[grads[n] for n in WEIGHT_ORDER], *[delta[n] for n in WEIGHT_ORDER],
            *[new_m[n] for n in WEIGHT_ORDER], *[new_v[n] for n in WEIGHT_ORDER])
```

```python
import math

import jax
import jax.numpy as jnp
import numpy as np
from jax import lax
from jax.experimental import pallas as pl
from jax.experimental.pallas import tpu as pltpu

F32 = jnp.float32
BF16 = jnp.bfloat16

N_DEV = 8
HEAD_DIM = 64
A_Q_HEADS = 12
A_KV_HEADS = 4
A_GROUP = 3
B_HEADS = 12
C_HEADS = 4
C_HEAD_DIM = 128
WINDOW = 128
A_WIDTH = 768
A_KV_WIDTH = 256
B_WIDTH = 768
C_WIDTH = 512
EPS = 1e-6
NEG = -1e30

COL_QA, COL_KA, COL_VA = 0, 768, 1024
COL_QB, COL_KB, COL_VB = 1280, 2048, 2816
COL_QC = 3584
W_QKV = 4096
COL_ZA, COL_ZB, COL_ZC = 0, 768, 1536
COL_GATE = W_Z = 2048
SRC_RANGES = ((0, 1280), (2048, 4352), (5132, 5644), (1280, 2048), (4352, 5120), (5644, 6156))
SRC_GATE = 6156
FB_SRC = 5120
FB_PAD = 128

ADAM_LR = 0.001
ADAM_B1 = 0.9
ADAM_B2 = 0.999
ADAM_EPS = 1e-08
ADAM_WD = 0.01
ADAM_STEP = 10

VMEM_BIG = 52 * 1024 * 1024
LANES = 128
MESH = pl.DeviceIdType.MESH


def _tile(n, pref, mult=128):
    if n <= pref:
        return n
    t = (pref // mult) * mult
    while t >= mult:
        if n % t == 0:
            return t
        t -= mult
    return n


def _params(sem=None, vmem=None):
    kw = {}
    if sem is not None:
        kw["dimension_semantics"] = sem
    if vmem is not None:
        kw["vmem_limit_bytes"] = vmem
    return pltpu.CompilerParams(**kw)


def _sigmoid(x):
    return 1.0 / (1.0 + jnp.exp(-x))


def _block_diag(hd):
    r = np.arange(LANES)
    return jnp.asarray((r[:, None] // hd) == (r[None, :] // hd), dtype=BF16)


def _seg_sum(t, bd):
    hi = t.astype(BF16)
    lo = (t - hi.astype(F32)).astype(BF16)
    outs = []
    for c in range(t.shape[1] // LANES):
        sl = slice(c * LANES, (c + 1) * LANES)
        outs.append(jnp.dot(hi[:, sl], bd, preferred_element_type=F32) + jnp.dot(lo[:, sl], bd, preferred_element_type=F32))
    return outs[0] if len(outs) == 1 else jnp.concatenate(outs, axis=1)


def _rmsnorm_fwd(x, gain, name):
    rows, d = x.shape
    bm = _tile(rows, 512, 8)

    def body(x_ref, g_ref, o_ref):
        xv = x_ref[...]
        ms = jnp.mean(xv * xv, axis=-1, keepdims=True)
        o_ref[...] = (xv * lax.rsqrt(ms + EPS) * g_ref[...]).astype(BF16)

    return pl.pallas_call(
        body, name=name, grid=(rows // bm,),
        in_specs=[pl.BlockSpec((bm, d), lambda i: (i, 0)), pl.BlockSpec((1, d), lambda i: (0, 0))],
        out_specs=pl.BlockSpec((bm, d), lambda i: (i, 0)),
        out_shape=jax.ShapeDtypeStruct((rows, d), BF16),
        compiler_params=_params(("parallel",)),
    )(x, gain)


def _rmsnorm_bwd(x, dhn, gain, dy, name):
    rows, d = x.shape
    bm = _tile(rows, 512, 8)
    with_dx = dy is not None

    def body(*refs):
        if with_dx:
            x_ref, dh_ref, g_ref, dy_ref, gx_ref, dg_ref = refs
        else:
            x_ref, dh_ref, g_ref, dg_ref = refs
        i = pl.program_id(0)
        xv = x_ref[...]
        rstd = lax.rsqrt(jnp.mean(xv * xv, axis=-1, keepdims=True) + EPS)
        xhat = xv * rstd
        dh = dh_ref[...]
        part = jnp.sum((dh * xhat).reshape(bm // 8, 8, d), axis=0)

        @pl.when(i == 0)
        def _():
            dg_ref[...] = part

        @pl.when(i > 0)
        def _():
            dg_ref[...] += part

        if with_dx:
            g = dh * g_ref[...]
            mean = jnp.mean(g * xhat, axis=-1, keepdims=True)
            gx_ref[...] = dy_ref[...] + rstd * (g - xhat * mean)

    row_spec = pl.BlockSpec((bm, d), lambda i: (i, 0))
    in_specs = [row_spec, row_spec, pl.BlockSpec((1, d), lambda i: (0, 0))]
    args = [x, dhn, gain]
    dg_spec = pl.BlockSpec((8, d), lambda i: (0, 0))
    dg_shape = jax.ShapeDtypeStruct((8, d), F32)
    if with_dx:
        in_specs.append(row_spec)
        args.append(dy)
        out_specs = [row_spec, dg_spec]
        out_shape = [jax.ShapeDtypeStruct((rows, d), F32), dg_shape]
    else:
        out_specs = [dg_spec]
        out_shape = [dg_shape]
    outs = pl.pallas_call(
        body, name=name, grid=(rows // bm,), in_specs=in_specs, out_specs=out_specs, out_shape=out_shape,
        compiler_params=_params(("arbitrary",), VMEM_BIG),
    )(*args)
    return outs if with_dx else (None, outs[0])


class _Comm:
    def __init__(self, kind, arrays):
        self.kind = kind
        self.arrays = list(arrays)
        self.n = len(self.arrays)

    def out_shapes(self):
        if self.kind == "gather":
            return [jax.ShapeDtypeStruct((N_DEV,) + a.shape, a.dtype) for a in self.arrays]
        return [jax.ShapeDtypeStruct(a.shape, a.dtype) for a in self.arrays]

    def scratch(self):
        return [pltpu.SemaphoreType.DMA((self.n, N_DEV - 1)), pltpu.SemaphoreType.DMA((self.n, N_DEV - 1)),
                pltpu.SemaphoreType.DMA((self.n,))]

    def _plan(self, ins, outs, sems, with_recvs):
        send_sems, recv_sems, local_sems = sems
        x, y, c = lax.axis_index("x"), lax.axis_index("y"), lax.axis_index("c")
        my = 4 * x + 2 * y + c
        gather = self.kind == "gather"
        local, sends, recvs = [], [], []
        for a in range(self.n):
            local.append(pltpu.make_async_copy(ins[a] if gather else ins[a].at[my], outs[a].at[my], local_sems.at[a]))
            for k in range(1, N_DEV):
                peer = (x ^ ((k >> 2) & 1), y ^ ((k >> 1) & 1), c ^ (k & 1))
                pid = 4 * peer[0] + 2 * peer[1] + peer[2]
                src = ins[a] if gather else ins[a].at[pid]
                sem = dict(send_sem=send_sems.at[a, k - 1], recv_sem=recv_sems.at[a, k - 1], device_id=peer, device_id_type=MESH)
                sends.append(pltpu.make_async_remote_copy(src_ref=src, dst_ref=outs[a].at[my], **sem))
                if with_recvs:
                    recvs.append(pltpu.make_async_remote_copy(src_ref=src, dst_ref=outs[a].at[pid], **sem))
        return local, sends, recvs

    def start(self, ins, outs, sems):
        local, sends, _ = self._plan(ins, outs, sems, False)
        for cp in local + sends:
            cp.start()

    def wait(self, ins, outs, sems):
        local, sends, recvs = self._plan(ins, outs, sems, True)
        for cp in recvs:
            cp.wait_recv()
        for cp in sends:
            cp.wait_send()
        for cp in local:
            cp.wait()


def _grid_edges(grid):
    first = last = None
    for ax, size in enumerate(grid):
        pid = pl.program_id(ax)
        f, l = pid == 0, pid == size - 1
        first = f if first is None else first & f
        last = l if last is None else last & l
    return first, last


def _hosted_call(body, comm, *, name, grid, in_specs, out_specs, out_shape, scratch_shapes, args, sem, vmem=None):
    in_specs, out_specs, out_shape, scratch_shapes = list(in_specs), list(out_specs), list(out_shape), list(scratch_shapes)
    if comm is None:
        res = pl.pallas_call(body, name=name, grid=grid, in_specs=in_specs, out_specs=out_specs, out_shape=out_shape,
                             scratch_shapes=scratch_shapes, compiler_params=_params(sem, vmem))(*args)
        return list(res), []
    n_in, n_out, n_scr, nc = len(in_specs), len(out_shape), len(scratch_shapes), comm.n

    def hosted(*refs):
        ins = refs[0:n_in]
        comm_in = refs[n_in:n_in + nc]
        outs = refs[n_in + nc:n_in + nc + n_out]
        comm_out = refs[n_in + nc + n_out:n_in + 2 * nc + n_out]
        scr = refs[n_in + 2 * nc + n_out:n_in + 2 * nc + n_out + n_scr]
        sems = refs[n_in + 2 * nc + n_out + n_scr:]
        first, last = _grid_edges(grid)

        @pl.when(first)
        def _():
            comm.start(comm_in, comm_out, sems)

        body(*ins, *outs, *scr)

        @pl.when(last)
        def _():
            comm.wait(comm_in, comm_out, sems)

    any_spec = pl.BlockSpec(memory_space=pl.ANY)
    res = pl.pallas_call(
        hosted, name=name, grid=grid, in_specs=in_specs + [any_spec] * nc, out_specs=out_specs + [any_spec] * nc,
        out_shape=out_shape + comm.out_shapes(), scratch_shapes=scratch_shapes + comm.scratch(),
        compiler_params=_params(("arbitrary",) * len(grid), vmem),
    )(*args, *comm.arrays)
    return list(res[0:n_out]), list(res[n_out:])


def _mm(a, b, *, grid, a_spec, b_spec, o_spec, o_shape, o_dtype, contract, name, add=None, add_spec=None, acc_shape=None,
        comm=None):
    nk = grid[2]
    has_add = add is not None

    def body(*refs):
        a_ref, b_ref = refs[0], refs[1]
        add_ref = refs[2] if has_add else None
        o_ref = refs[3] if has_add else refs[2]
        part = lax.dot_general(a_ref[...], b_ref[...], (contract, ((), ())), preferred_element_type=F32)
        if nk == 1:
            if has_add:
                part = part + add_ref[...]
            o_ref[...] = part.astype(o_dtype)
        else:
            acc = refs[-1]
            k = pl.program_id(2)

            @pl.when(k == 0)
            def _():
                acc[...] = part

            @pl.when(k > 0)
            def _():
                acc[...] += part

            @pl.when(k == nk - 1)
            def _():
                r = acc[...]
                if has_add:
                    r = r + add_ref[...]
                o_ref[...] = r.astype(o_dtype)

    in_specs = [a_spec, b_spec] + ([add_spec] if has_add else [])
    args = [a, b] + ([add] if has_add else [])
    scratch = [pltpu.VMEM(acc_shape, F32)] if nk > 1 else []
    outs, comm_outs = _hosted_call(
        body, comm, name=name, grid=grid, in_specs=in_specs, out_specs=[o_spec],
        out_shape=[jax.ShapeDtypeStruct(o_shape, o_dtype)], scratch_shapes=scratch, args=args,
        sem=("parallel", "parallel", "arbitrary"), vmem=VMEM_BIG)
    return outs[0] if comm is None else (outs[0], comm_outs)


def _mm_nn(a, b, *, bm, bn, bk, o_dtype, name, add=None, comm=None):
    m, kd = a.shape
    n = b.shape[1]
    bm, bn, bk = _tile(m, bm, 8), _tile(n, bn), _tile(kd, bk)
    o_spec = pl.BlockSpec((bm, bn), lambda i, j, k: (i, j))
    return _mm(a, b, grid=(m // bm, n // bn, kd // bk),
               a_spec=pl.BlockSpec((bm, bk), lambda i, j, k: (i, k)),
               b_spec=pl.BlockSpec((bk, bn), lambda i, j, k: (k, j)),
               o_spec=o_spec, o_shape=(m, n), o_dtype=o_dtype, contract=((1,), (0,)), name=name,
               add=add, add_spec=o_spec, acc_shape=(bm, bn), comm=comm)


def _mm_nt(a, b, *, bm, bn, bk, o_dtype, name, add=None, b_col0=0, comm=None):
    m, kd = a.shape
    n = b.shape[0]
    bm, bn, bk = _tile(m, bm, 8), _tile(n, bn), _tile(math.gcd(kd, b_col0), bk)
    kb0 = b_col0 // bk
    o_spec = pl.BlockSpec((bm, bn), lambda i, j, k: (i, j))
    return _mm(a, b, grid=(m // bm, n // bn, kd // bk),
               a_spec=pl.BlockSpec((bm, bk), lambda i, j, k: (i, k)),
               b_spec=pl.BlockSpec((bn, bk), lambda i, j, k: (j, kb0 + k)),
               o_spec=o_spec, o_shape=(m, n), o_dtype=o_dtype, contract=((1,), (1,)), name=name,
               add=add, add_spec=o_spec, acc_shape=(bm, bn), comm=comm)


def _mm_nt_cat(a1, a2, b, *, bm, bn, bk, name, add, comm=None):
    m, k1 = a1.shape
    k2 = a2.shape[1]
    n = b.shape[0]
    bm, bn, bk = _tile(m, bm, 8), _tile(n, bn), _tile(math.gcd(k1, k2), bk)
    n1, nk = k1 // bk, (k1 + k2) // bk
    nt = (((1,), (1,)), ((), ()))

    def body(a1_ref, a2_ref, b_ref, add_ref, o_ref, acc):
        k = pl.program_id(2)

        def accumulate(part):
            @pl.when(k == 0)
            def _():
                acc[...] = part

            @pl.when(k > 0)
            def _():
                acc[...] += part

        @pl.when(k < n1)
        def _():
            accumulate(lax.dot_general(a1_ref[...], b_ref[...], nt, preferred_element_type=F32))

        @pl.when(k >= n1)
        def _():
            accumulate(lax.dot_general(a2_ref[...], b_ref[...], nt, preferred_element_type=F32))

        @pl.when(k == nk - 1)
        def _():
            o_ref[...] = acc[...] + add_ref[...]

    o_spec = pl.BlockSpec((bm, bn), lambda i, j, k: (i, j))
    outs, comm_outs = _hosted_call(
        body, comm, name=name, grid=(m // bm, n // bn, nk),
        in_specs=[pl.BlockSpec((bm, bk), lambda i, j, k: (i, jnp.minimum(k, n1 - 1))),
                  pl.BlockSpec((bm, bk), lambda i, j, k: (i, jnp.maximum(k - n1, 0))),
                  pl.BlockSpec((bn, bk), lambda i, j, k: (j, k)), o_spec],
        out_specs=[o_spec], out_shape=[jax.ShapeDtypeStruct((m, n), F32)],
        scratch_shapes=[pltpu.VMEM((bm, bn), F32)], args=[a1, a2, b, add],
        sem=("parallel", "parallel", "arbitrary"), vmem=VMEM_BIG)
    return outs[0] if comm is None else (outs[0], comm_outs)


def _mm_tn(a, b, *, bm, bn, bk, o_dtype, name, comm=None):
    kd, m = a.shape
    n = b.shape[1]
    bm, bn, bk = _tile(m, bm), _tile(n, bn), _tile(kd, bk, 8)
    return _mm(a, b, grid=(m // bm, n // bn, kd // bk),
               a_spec=pl.BlockSpec((bk, bm), lambda i, j, k: (k, i)),
               b_spec=pl.BlockSpec((bk, bn), lambda i, j, k: (k, j)),
               o_spec=pl.BlockSpec((bm, bn), lambda i, j, k: (i, j)),
               o_shape=(m, n), o_dtype=o_dtype, contract=((0,), (0,)), name=name, acc_shape=(bm, bn), comm=comm)


def _branch_full(w8):
    kb, ds = w8.shape[0] // N_DEV, w8.shape[1]
    return w8.reshape(N_DEV, kb, ds).transpose(1, 0, 2).reshape(kb, N_DEV * ds)


def _branch_shards(g):
    kb, ds = g.shape[0], g.shape[1] // N_DEV
    return g.reshape(kb, N_DEV, ds).transpose(1, 0, 2).reshape(N_DEV * kb, ds)


def _headnorm_fwd(src, c0, width, bw, hd, gain, nflag, head_major, name):
    rows = src.shape[0]
    bm = _tile(rows, 2048 if bw <= 256 else 1024, 16)
    bd = _block_diag(hd)
    cb0 = c0 // bw

    def body(x_ref, g_ref, f_ref, bd_ref, o_ref):
        xv = x_ref[...].astype(F32)
        ss = _seg_sum(xv * xv, bd_ref[...])
        rstd = lax.rsqrt(ss * (1.0 / hd) + EPS)
        y = (xv * jnp.where(f_ref[...] > 0.0, rstd, 1.0) * g_ref[...]).astype(BF16)
        if head_major:
            for h in range(bw // HEAD_DIM):
                o_ref[h] = y[:, h * HEAD_DIM:(h + 1) * HEAD_DIM]
        else:
            o_ref[...] = y

    vec_spec = pl.BlockSpec((1, bw), lambda i, t: (0, t))
    if head_major:
        hpb = bw // HEAD_DIM
        out_spec = pl.BlockSpec((hpb, bm, HEAD_DIM), lambda i, t: (t, i, 0))
        out_shape = jax.ShapeDtypeStruct((width // HEAD_DIM, rows, HEAD_DIM), BF16)
    else:
        out_spec = pl.BlockSpec((bm, bw), lambda i, t: (i, t))
        out_shape = jax.ShapeDtypeStruct((rows, width), BF16)
    return pl.pallas_call(
        body, name=name, grid=(rows // bm, width // bw),
        in_specs=[pl.BlockSpec((bm, bw), lambda i, t: (i, cb0 + t)), vec_spec, vec_spec,
                  pl.BlockSpec((LANES, LANES), lambda i, t: (0, 0))],
        out_specs=out_spec, out_shape=out_shape,
        compiler_params=_params(("parallel", "parallel")),
    )(src, gain, nflag, bd)


def _headnorm_bwd(src, c0, width, bw, hd, gain, nflag, dyn, target, t0, name):
    rows = src.shape[0]
    bm = _tile(rows, 2048 if bw <= 256 else 1024, 16)
    bd = _block_diag(hd)
    cb0 = c0 // bw
    tb0 = t0 // bw
    aliased = target is not None

    def body(*refs):
        if aliased:
            x_ref, dy_ref, g_ref, f_ref, bd_ref, _, o_ref, dg_ref = refs
        else:
            x_ref, dy_ref, g_ref, f_ref, bd_ref, o_ref, dg_ref = refs
        i = pl.program_id(1)
        xv = x_ref[...].astype(F32)
        dyv = dy_ref[...]
        bdv = bd_ref[...]
        rstd = lax.rsqrt(_seg_sum(xv * xv, bdv) * (1.0 / hd) + EPS)
        xhat = xv * rstd
        g = dyv * g_ref[...]
        mean = _seg_sum(g * xhat, bdv) * (1.0 / hd)
        dx = jnp.where(f_ref[...] > 0.0, rstd * (g - xhat * mean), g)
        o_ref[...] = dx.astype(BF16)
        part = jnp.sum((dyv * xhat).reshape(bm // 8, 8, bw), axis=0)

        @pl.when(i == 0)
        def _():
            dg_ref[...] = part

        @pl.when(i > 0)
        def _():
            dg_ref[...] += part

    vec_spec = pl.BlockSpec((1, bw), lambda t, i: (0, t))
    in_specs = [pl.BlockSpec((bm, bw), lambda t, i: (i, cb0 + t)), pl.BlockSpec((bm, bw), lambda t, i: (i, t)),
                vec_spec, vec_spec, pl.BlockSpec((LANES, LANES), lambda t, i: (0, 0))]
    args = [src, dyn, gain, nflag, bd]
    aliases = {}
    if aliased:
        in_specs.append(pl.BlockSpec(memory_space=pl.ANY))
        args.append(target)
        aliases = {5: 0}
        o_shape = jax.ShapeDtypeStruct(target.shape, BF16)
    else:
        o_shape = jax.ShapeDtypeStruct((rows, width), BF16)
    out, dg = pl.pallas_call(
        body, name=name, grid=(width // bw, rows // bm), in_specs=in_specs,
        out_specs=[pl.BlockSpec((bm, bw), lambda t, i: (i, tb0 + t)), pl.BlockSpec((8, bw), lambda t, i: (0, t))],
        out_shape=[o_shape, jax.ShapeDtypeStruct((8, width), F32)],
        input_output_aliases=aliases,
        compiler_params=_params(("parallel", "arbitrary")),
    )(*args)
    return out, dg


def _fox_prep(pfb, bpad, name):
    s = pfb.shape[0]

    def body(p_ref, b_ref, c_ref):
        z = p_ref[...] + b_ref[...]
        logf = jnp.minimum(z, 0.0) - jnp.log(1.0 + jnp.exp(-jnp.abs(z)))
        x = logf.T[0:16, :]
        lane = lax.broadcasted_iota(jnp.int32, (16, s), 1)
        sh = 1
        while sh < s:
            x = x + jnp.where(lane >= sh, pltpu.roll(x, sh, 1), 0.0)
            sh *= 2
        c_ref[...] = x

    return pl.pallas_call(
        body, name=name, grid=(1,),
        in_specs=[pl.BlockSpec((s, FB_PAD), lambda i: (0, 0)), pl.BlockSpec((1, FB_PAD), lambda i: (0, 0))],
        out_specs=pl.BlockSpec((16, s), lambda i: (0, 0)),
        out_shape=jax.ShapeDtypeStruct((16, s), F32),
        compiler_params=_params(("arbitrary",)),
    )(pfb, bpad)


def _fox_prep_bwd(pfb, bpad, dct, name):
    s = pfb.shape[0]

    def body(p_ref, b_ref, dc_ref, df_ref, db_ref):
        zt = (p_ref[...] + b_ref[...]).T[0:16, :]
        y = dc_ref[...]
        lane = lax.broadcasted_iota(jnp.int32, (16, s), 1)
        sh = 1
        while sh < s:
            y = y + jnp.where(lane < s - sh, pltpu.roll(y, s - sh, 1), 0.0)
            sh *= 2
        dz = y * _sigmoid(-zt)
        db_ref[...] = jnp.broadcast_to(jnp.sum(dz, axis=1, keepdims=True), (16, FB_PAD))
        full = jnp.concatenate([dz, jnp.zeros((FB_PAD - 16, s), F32)], axis=0)
        df_ref[...] = full.T.astype(BF16)

    return pl.pallas_call(
        body, name=name, grid=(1,),
        in_specs=[pl.BlockSpec((s, FB_PAD), lambda i: (0, 0)), pl.BlockSpec((1, FB_PAD), lambda i: (0, 0)),
                  pl.BlockSpec((16, s), lambda i: (0, 0))],
        out_specs=[pl.BlockSpec((s, FB_PAD), lambda i: (0, 0)), pl.BlockSpec((16, FB_PAD), lambda i: (0, 0))],
        out_shape=[jax.ShapeDtypeStruct((s, FB_PAD), BF16), jax.ShapeDtypeStruct((16, FB_PAD), F32)],
        compiler_params=_params(("arbitrary",)),
    )(pfb, bpad, dct)


def _swa_window(n):
    ws = pl.multiple_of(jnp.maximum(n * WINDOW - WINDOW, 0), WINDOW)
    qi = lax.broadcasted_iota(jnp.int32, (WINDOW, 2 * WINDOW), 0)
    kj = lax.broadcasted_iota(jnp.int32, (WINDOW, 2 * WINDOW), 1)
    rel = qi + (n * WINDOW - ws) - kj
    valid = (rel >= 0) & (rel < WINDOW)
    return ws, valid, rel.astype(F32)


def _attn_a_fwd(q, k, v, sinks, slopes, name):
    s = q.shape[1]
    nb = s // WINDOW
    smem = pl.BlockSpec(memory_space=pltpu.SMEM)

    def body(sink_ref, slope_ref, q_ref, k_ref, v_ref, o_ref, lse_ref):
        n = pl.program_id(0)
        ws, valid, relf = _swa_window(n)
        outs = []
        for h in range(A_Q_HEADS):
            kvh = h // A_GROUP
            kw = k_ref[kvh, pl.ds(ws, 2 * WINDOW), :]
            vw = v_ref[kvh, pl.ds(ws, 2 * WINDOW), :]
            sc = lax.dot_general(q_ref[h], kw, (((1,), (1,)), ((), ())), preferred_element_type=F32)
            sc = jnp.where(valid, sc - slope_ref[h] * relf, NEG)
            sink = sink_ref[h]
            m = jnp.maximum(jnp.max(sc, axis=1, keepdims=True), sink)
            p = jnp.exp(sc - m)
            denom = jnp.sum(p, axis=1, keepdims=True) + jnp.exp(sink - m)
            pn = (p / denom).astype(BF16)
            outs.append(jnp.dot(pn, vw, preferred_element_type=F32))
            lse_ref[h] = jnp.broadcast_to(m + jnp.log(denom), (WINDOW, HEAD_DIM))
        o_ref[...] = jnp.concatenate(outs, axis=1)

    return pl.pallas_call(
        body, name=name, grid=(nb,),
        in_specs=[smem, smem,
                  pl.BlockSpec((A_Q_HEADS, WINDOW, HEAD_DIM), lambda n: (0, n, 0)),
                  pl.BlockSpec((A_KV_HEADS, s, HEAD_DIM), lambda n: (0, 0, 0)),
                  pl.BlockSpec((A_KV_HEADS, s, HEAD_DIM), lambda n: (0, 0, 0))],
        out_specs=[pl.BlockSpec((WINDOW, A_WIDTH), lambda n: (n, 0)),
                   pl.BlockSpec((A_Q_HEADS, WINDOW, HEAD_DIM), lambda n: (0, n, 0))],
        out_shape=[jax.ShapeDtypeStruct((s, A_WIDTH), F32), jax.ShapeDtypeStruct((A_Q_HEADS, s, HEAD_DIM), F32)],
        compiler_params=_params(("parallel",), VMEM_BIG),
    )(sinks, slopes, q, k, v)


def _attn_a_bwd(q, k, v, do, lse, dd, sinks, slopes, name, comm=None):
    s = q.shape[1]
    nb = s // WINDOW
    smem = pl.BlockSpec(memory_space=pltpu.SMEM)
    last = nb - 1

    def body(sink_ref, slope_ref, q_ref, k_ref, v_ref, do_ref, lse_ref, dd_ref, dq_ref, dkv_ref, ds_ref, carry):
        n = pl.program_id(0)

        @pl.when(n == 0)
        def _():
            carry[...] = jnp.zeros(carry.shape, F32)
            ds_ref[...] = jnp.zeros(ds_ref.shape, F32)

        @pl.when(n < nb)
        def _():
            ws, valid, relf = _swa_window(n)
            dqs = []
            dkw = [None] * A_KV_HEADS
            dvw = [None] * A_KV_HEADS
            for h in range(A_Q_HEADS):
                kvh = h // A_GROUP
                qh = q_ref[h]
                doh = do_ref[h]
                kw = k_ref[kvh, pl.ds(ws, 2 * WINDOW), :]
                vw = v_ref[kvh, pl.ds(ws, 2 * WINDOW), :]
                lse_h = lse_ref[h]
                dd_h = dd_ref[h]
                sc = lax.dot_general(qh, kw, (((1,), (1,)), ((), ())), preferred_element_type=F32)
                sc = jnp.where(valid, sc - slope_ref[h] * relf, NEG)
                p = jnp.exp(sc - lse_h[:, 0:1])
                dp = lax.dot_general(doh, vw, (((1,), (1,)), ((), ())), preferred_element_type=F32)
                dsc = (p * (dp - dd_h[:, 0:1])).astype(BF16)
                pb = p.astype(BF16)
                dqs.append(jnp.dot(dsc, kw, preferred_element_type=F32))
                dk_h = lax.dot_general(dsc, qh, (((0,), (0,)), ((), ())), preferred_element_type=F32)
                dv_h = lax.dot_general(pb, doh, (((0,), (0,)), ((), ())), preferred_element_type=F32)
                dkw[kvh] = dk_h if dkw[kvh] is None else dkw[kvh] + dk_h
                dvw[kvh] = dv_h if dvw[kvh] is None else dvw[kvh] + dv_h
                psink = jnp.exp(sink_ref[h] - lse_h)
                ds_ref[h] += jnp.sum((-psink * dd_h).reshape(WINDOW // 8, 8, HEAD_DIM), axis=0)
            dq_ref[...] = jnp.concatenate(dqs, axis=1)
            win = jnp.concatenate(dkw + dvw, axis=1)
            first = win[0:WINDOW]
            second = win[WINDOW:2 * WINDOW]
            dkv_ref[...] = carry[...] + first
            carry[...] = jnp.where(n == 0, first, second)

        @pl.when(n == nb)
        def _():
            dkv_ref[...] = carry[...]

    hm = lambda heads: pl.BlockSpec((heads, WINDOW, HEAD_DIM), lambda n: (0, jnp.minimum(n, last), 0))
    res = lambda heads: pl.BlockSpec((heads, s, HEAD_DIM), lambda n: (0, 0, 0))
    outs, comm_outs = _hosted_call(
        body, comm, name=name, grid=(nb + 1,),
        in_specs=[smem, smem, hm(A_Q_HEADS), res(A_KV_HEADS), res(A_KV_HEADS), hm(A_Q_HEADS), hm(A_Q_HEADS), hm(A_Q_HEADS)],
        out_specs=[pl.BlockSpec((WINDOW, A_WIDTH), lambda n: (jnp.minimum(n, last), 0)),
                   pl.BlockSpec((WINDOW, 2 * A_KV_WIDTH), lambda n: (jnp.maximum(n - 1, 0), 0)),
                   pl.BlockSpec((A_Q_HEADS, 8, HEAD_DIM), lambda n: (0, 0, 0))],
        out_shape=[jax.ShapeDtypeStruct((s, A_WIDTH), F32), jax.ShapeDtypeStruct((s, 2 * A_KV_WIDTH), F32),
                   jax.ShapeDtypeStruct((A_Q_HEADS, 8, HEAD_DIM), F32)],
        scratch_shapes=[pltpu.VMEM((WINDOW, 2 * A_KV_WIDTH), F32)],
        args=[sinks, slopes, q, k, v, do, lse, dd], sem=("arbitrary",), vmem=VMEM_BIG)
    return outs[0], outs[1], outs[2], comm_outs


def _attn_b_fwd(q, k, v, c3, name, comm=None):
    heads, s, _ = q.shape
    bq = min(512, s)
    nq = s // bq
    nt = (((1,), (1,)), ((), ()))

    def body(q_ref, k_ref, v_ref, c_ref, o_ref, lse_ref, m_scr, l_scr, acc_scr):
        i = pl.program_id(1)
        r0 = pl.multiple_of(i * bq, bq)
        row = lax.broadcasted_iota(jnp.int32, (bq, bq), 0)
        col = lax.broadcasted_iota(jnp.int32, (bq, bq), 1)
        m_scr[...] = jnp.full((2, bq, LANES), NEG, F32)
        l_scr[...] = jnp.zeros((2, bq, LANES), F32)
        acc_scr[...] = jnp.zeros((2, bq, HEAD_DIM), F32)

        def step(j, masked):
            k0 = pl.multiple_of(j * bq, bq)
            for h2 in range(2):
                kv = k_ref[h2, pl.ds(k0, bq), :]
                vv = v_ref[h2, pl.ds(k0, bq), :]
                cq0 = c_ref[h2, :, pl.ds(r0, LANES)][:, 0:1]
                sc = lax.dot_general(q_ref[h2], kv, nt, preferred_element_type=F32)
                sc = sc + (cq0 - c_ref[h2, :, pl.ds(k0, bq)])
                if masked:
                    sc = jnp.where(col <= row, sc, NEG)
                m_prev = m_scr[h2]
                m_new = jnp.maximum(m_prev, jnp.max(sc, axis=1, keepdims=True))
                alpha = jnp.exp(m_prev - m_new)
                p = jnp.exp(sc - m_new[:, 0:1])
                l_scr[h2] = alpha * l_scr[h2] + jnp.sum(p, axis=1, keepdims=True)
                p_hi = p.astype(BF16)
                p_lo = (p - p_hi.astype(F32)).astype(BF16)
                pv = jnp.dot(p_hi, vv, preferred_element_type=F32) + jnp.dot(p_lo, vv, preferred_element_type=F32)
                acc_scr[h2] = acc_scr[h2] * alpha[:, 0:HEAD_DIM] + pv
                m_scr[h2] = m_new

        def loop_body(j, carry):
            step(j, False)
            return carry

        lax.fori_loop(0, i, loop_body, 0)
        step(i, True)
        outs = []
        for h2 in range(2):
            l = l_scr[h2]
            outs.append(acc_scr[h2] / l[:, 0:HEAD_DIM])
            lse_ref[h2] = (m_scr[h2] + jnp.log(l))[:, 0:HEAD_DIM]
        o_ref[...] = jnp.concatenate(outs, axis=1)

    res = pl.BlockSpec((2, s, HEAD_DIM), lambda hp, i: (hp, 0, 0))
    outs, comm_outs = _hosted_call(
        body, comm, name=name, grid=(heads // 2, nq),
        in_specs=[pl.BlockSpec((2, bq, HEAD_DIM), lambda hp, i: (hp, i, 0)), res, res,
                  pl.BlockSpec((2, 1, s), lambda hp, i: (hp, 0, 0))],
        out_specs=[pl.BlockSpec((bq, 2 * HEAD_DIM), lambda hp, i: (i, hp)),
                   pl.BlockSpec((2, bq, HEAD_DIM), lambda hp, i: (hp, i, 0))],
        out_shape=[jax.ShapeDtypeStruct((s, heads * HEAD_DIM), F32), jax.ShapeDtypeStruct((heads, s, HEAD_DIM), F32)],
        scratch_shapes=[pltpu.VMEM((2, bq, LANES), F32), pltpu.VMEM((2, bq, LANES), F32), pltpu.VMEM((2, bq, HEAD_DIM), F32)],
        args=[q, k, v, c3], sem=("parallel", "parallel"), vmem=VMEM_BIG)
    return outs[0], outs[1], comm_outs


def _attn_b_bwd(q, k, v, do, lse, dd, c3, name, comm=None):
    heads, s, _ = q.shape
    bq = min(512, s)
    nq = s // bq
    nt = (((1,), (1,)), ((), ()))
    tn = (((0,), (0,)), ((), ()))
    grid = (heads // 2, nq)

    def body(q_ref, k_ref, v_ref, do_ref, lse_ref, dd_ref, c_ref, dq_ref, dk_ref, dv_ref, dc_ref,
             dq_scr, dk_scr, dv_scr, dc_scr):
        j = pl.program_id(1)
        k0 = pl.multiple_of(j * bq, bq)
        row = lax.broadcasted_iota(jnp.int32, (bq, bq), 0)
        col = lax.broadcasted_iota(jnp.int32, (bq, bq), 1)

        @pl.when(j == 0)
        def _():
            dq_scr[...] = jnp.zeros(dq_scr.shape, F32)

        dk_scr[...] = jnp.zeros((2, bq, HEAD_DIM), F32)
        dv_scr[...] = jnp.zeros((2, bq, HEAD_DIM), F32)
        dc_scr[...] = jnp.zeros((2, 1, bq), F32)

        def step(i, masked):
            r0 = pl.multiple_of(i * bq, bq)
            for h2 in range(2):
                kv = k_ref[h2]
                vv = v_ref[h2]
                qv = q_ref[h2, pl.ds(r0, bq), :]
                dov = do_ref[h2, pl.ds(r0, bq), :]
                lse_v = lse_ref[h2, pl.ds(r0, bq), :][:, 0:1]
                dd_v = dd_ref[h2, pl.ds(r0, bq), :][:, 0:1]
                cq0 = c_ref[h2, :, pl.ds(r0, LANES)][:, 0:1]
                sc = lax.dot_general(qv, kv, nt, preferred_element_type=F32) + (cq0 - c_ref[h2, :, pl.ds(k0, bq)])
                if masked:
                    sc = jnp.where(col <= row, sc, NEG)
                p = jnp.exp(sc - lse_v)
                dp = lax.dot_general(dov, vv, nt, preferred_element_type=F32)
                dsc = p * (dp - dd_v)
                dsb = dsc.astype(BF16)
                dv_scr[h2] += lax.dot_general(p.astype(BF16), dov, tn, preferred_element_type=F32)
                dk_scr[h2] += lax.dot_general(dsb, qv, tn, preferred_element_type=F32)
                dq_scr[h2, pl.ds(r0, bq), :] += jnp.dot(dsb, kv, preferred_element_type=F32)
                dc_scr[h2] -= jnp.sum(dsc, axis=0, keepdims=True)

        def loop_body(i, carry):
            step(i, False)
            return carry

        step(j, True)
        lax.fori_loop(j + 1, nq, loop_body, 0)
        dc_ref[...] = dc_scr[...]
        dk_ref[...] = jnp.concatenate([dk_scr[0], dk_scr[1]], axis=1)
        dv_ref[...] = jnp.concatenate([dv_scr[0], dv_scr[1]], axis=1)

        @pl.when(j == nq - 1)
        def _():
            dq_ref[...] = jnp.concatenate([dq_scr[0], dq_scr[1]], axis=1)

    res = pl.BlockSpec((2, s, HEAD_DIM), lambda hp, j: (hp, 0, 0))
    blk = pl.BlockSpec((2, bq, HEAD_DIM), lambda hp, j: (hp, j, 0))
    tm = jax.ShapeDtypeStruct((s, heads * HEAD_DIM), F32)
    in_specs = [res, blk, blk, res, res, res, pl.BlockSpec((2, 1, s), lambda hp, j: (hp, 0, 0))]
    out_specs = [pl.BlockSpec((s, 2 * HEAD_DIM), lambda hp, j: (0, hp)),
                 pl.BlockSpec((bq, 2 * HEAD_DIM), lambda hp, j: (j, hp)),
                 pl.BlockSpec((bq, 2 * HEAD_DIM), lambda hp, j: (j, hp)),
                 pl.BlockSpec((2, 1, bq), lambda hp, j: (hp, 0, j))]
    out_shape = [tm, tm, tm, jax.ShapeDtypeStruct((heads, 1, s), F32)]
    scratch = [pltpu.VMEM((2, s, HEAD_DIM), F32), pltpu.VMEM((2, bq, HEAD_DIM), F32),
               pltpu.VMEM((2, bq, HEAD_DIM), F32), pltpu.VMEM((2, 1, bq), F32)]
    outs, comm_outs = _hosted_call(
        body, comm, name=name, grid=grid, in_specs=in_specs, out_specs=out_specs, out_shape=out_shape,
        scratch_shapes=scratch, args=[q, k, v, do, lse, dd, c3], sem=("parallel", "arbitrary"), vmem=VMEM_BIG)
    return outs[0], outs[1], outs[2], outs[3], comm_outs


def _attn_c_probs(qh, mkh):
    sc = lax.dot_general(qh, mkh, (((1,), (1,)), ((), ())), preferred_element_type=F32) * (C_HEAD_DIM ** -0.5)
    p = jnp.exp(sc - jnp.max(sc, axis=1, keepdims=True))
    return p / jnp.sum(p, axis=1, keepdims=True)


def _attn_c_fwd(q, mkv, name):
    s = q.shape[0]
    m = mkv.shape[0]
    bq = _tile(s, 512, 8)

    def body(q_ref, mk_ref, mv_ref, o_ref):
        outs = []
        for h in range(C_HEADS):
            sl = slice(h * C_HEAD_DIM, (h + 1) * C_HEAD_DIM)
            pn = _attn_c_probs(q_ref[:, sl], mk_ref[:, sl]).astype(BF16)
            outs.append(jnp.dot(pn, mv_ref[:, sl], preferred_element_type=F32))
        o_ref[...] = jnp.concatenate(outs, axis=1)

    return pl.pallas_call(
        body, name=name, grid=(s // bq,),
        in_specs=[pl.BlockSpec((bq, C_WIDTH), lambda i: (i, 0)), pl.BlockSpec((m, C_WIDTH), lambda i: (0, 0)),
                  pl.BlockSpec((m, C_WIDTH), lambda i: (0, 1))],
        out_specs=pl.BlockSpec((bq, C_WIDTH), lambda i: (i, 0)),
        out_shape=jax.ShapeDtypeStruct((s, C_WIDTH), F32),
        compiler_params=_params(("parallel",)),
    )(q, mkv, mkv)


def _attn_c_bwd(q, mkv, do, name):
    s = q.shape[0]
    m = mkv.shape[0]
    bq = _tile(s, 512, 8)
    tn = (((0,), (0,)), ((), ()))

    def body(q_ref, mk_ref, mv_ref, do_ref, dq_ref, dm_ref):
        i = pl.program_id(0)

        @pl.when(i == 0)
        def _():
            dm_ref[...] = jnp.zeros(dm_ref.shape, F32)

        dqs = []
        for h in range(C_HEADS):
            sl = slice(h * C_HEAD_DIM, (h + 1) * C_HEAD_DIM)
            qh, mkh, mvh, doh = q_ref[:, sl], mk_ref[:, sl], mv_ref[:, sl], do_ref[:, sl]
            pn = _attn_c_probs(qh, mkh)
            dp = lax.dot_general(doh, mvh, (((1,), (1,)), ((), ())), preferred_element_type=F32)
            dsc = (pn * (dp - jnp.sum(pn * dp, axis=1, keepdims=True)) * (C_HEAD_DIM ** -0.5)).astype(BF16)
            dqs.append(jnp.dot(dsc, mkh, preferred_element_type=F32))
            dm_ref[:, sl] += lax.dot_general(dsc, qh, tn, preferred_element_type=F32)
            sv = slice(C_WIDTH + h * C_HEAD_DIM, C_WIDTH + (h + 1) * C_HEAD_DIM)
            dm_ref[:, sv] += lax.dot_general(pn.astype(BF16), doh, tn, preferred_element_type=F32)
        dq_ref[...] = jnp.concatenate(dqs, axis=1)

    row = pl.BlockSpec((bq, C_WIDTH), lambda i: (i, 0))
    return pl.pallas_call(
        body, name=name, grid=(s // bq,),
        in_specs=[row, pl.BlockSpec((m, C_WIDTH), lambda i: (0, 0)), pl.BlockSpec((m, C_WIDTH), lambda i: (0, 1)), row],
        out_specs=[row, pl.BlockSpec((m, 2 * C_WIDTH), lambda i: (0, 0))],
        out_shape=[jax.ShapeDtypeStruct((s, C_WIDTH), F32), jax.ShapeDtypeStruct((m, 2 * C_WIDTH), F32)],
        compiler_params=_params(("arbitrary",)),
    )(q, mkv, mkv, do)


def _gate_fwd(y, proj, zc0, bw, name):
    rows, width = y.shape
    bm = _tile(rows, 2048 if bw <= 256 else 1024, 16)
    cb0 = zc0 // bw

    def body(y_ref, z_ref, o_ref):
        z = z_ref[...].astype(F32)
        o_ref[...] = (y_ref[...] * (z * _sigmoid(z))).astype(BF16)

    return pl.pallas_call(
        body, name=name, grid=(rows // bm, width // bw),
        in_specs=[pl.BlockSpec((bm, bw), lambda i, t: (i, t)), pl.BlockSpec((bm, bw), lambda i, t: (i, cb0 + t))],
        out_specs=pl.BlockSpec((bm, bw), lambda i, t: (i, t)),
        out_shape=jax.ShapeDtypeStruct((rows, width), BF16),
        compiler_params=_params(("parallel", "parallel")),
    )(y, proj)


def _gate_bwd(dsv, y, proj, zc0, bw, dproj, t0, head_major, name):
    rows, width = y.shape
    bm = _tile(rows, 2048 if bw <= 256 else 1024, 16)
    cb0 = zc0 // bw
    tb0 = t0 // bw
    bd = _block_diag(HEAD_DIM)
    hpb = bw // HEAD_DIM

    def body(*refs):
        if head_major:
            ds_ref, y_ref, z_ref, bd_ref, _, dp_ref, dy_ref, dd_ref = refs
        else:
            ds_ref, y_ref, z_ref, _, dp_ref, dy_ref = refs
        z = z_ref[...].astype(F32)
        sig = _sigmoid(z)
        dsx = ds_ref[...]
        yv = y_ref[...]
        dy = dsx * (z * sig)
        dp_ref[...] = (dsx * yv * (sig * (1.0 + z * (1.0 - sig)))).astype(BF16)
        if head_major:
            dyb = dy.astype(BF16)
            dd = _seg_sum(dyb.astype(F32) * yv, bd_ref[...])
            for h in range(hpb):
                sl = slice(h * HEAD_DIM, (h + 1) * HEAD_DIM)
                dy_ref[h] = dyb[:, sl]
                dd_ref[h] = dd[:, sl]
        else:
            dy_ref[...] = dy.astype(BF16)

    tile = pl.BlockSpec((bm, bw), lambda i, t: (i, t))
    ztile = pl.BlockSpec((bm, bw), lambda i, t: (i, cb0 + t))
    ttile = pl.BlockSpec((bm, bw), lambda i, t: (i, tb0 + t))
    any_spec = pl.BlockSpec(memory_space=pl.ANY)
    dp_shape = jax.ShapeDtypeStruct(dproj.shape, BF16)
    if head_major:
        hm_spec = pl.BlockSpec((hpb, bm, HEAD_DIM), lambda i, t: (t, i, 0))
        nh = width // HEAD_DIM
        outs = pl.pallas_call(
            body, name=name, grid=(rows // bm, width // bw),
            in_specs=[tile, tile, ztile, pl.BlockSpec((LANES, LANES), lambda i, t: (0, 0)), any_spec],
            out_specs=[ttile, hm_spec, hm_spec],
            out_shape=[dp_shape, jax.ShapeDtypeStruct((nh, rows, HEAD_DIM), BF16),
                       jax.ShapeDtypeStruct((nh, rows, HEAD_DIM), F32)],
            input_output_aliases={4: 0},
            compiler_params=_params(("parallel", "parallel")),
        )(dsv, y, proj, bd, dproj)
        return outs[0], outs[1], outs[2]
    outs = pl.pallas_call(
        body, name=name, grid=(rows // bm, width // bw),
        in_specs=[tile, tile, ztile, any_spec],
        out_specs=[ttile, tile],
        out_shape=[dp_shape, jax.ShapeDtypeStruct((rows, width), BF16)],
        input_output_aliases={3: 0},
        compiler_params=_params(("parallel", "parallel")),
    )(dsv, y, proj, dproj)
    return outs[0], outs[1], None


def _merge_fwd(proj, ua, ub, uc, name):
    rows, d = ua.shape
    bm = _tile(rows, 1024, 16)
    bw = _tile(d, 512)
    g0 = COL_GATE // bw
    gstep = d // bw

    def body(la_ref, lb_ref, lc_ref, ua_ref, ub_ref, uc_ref, o_ref, ga_ref, gb_ref, gc_ref):
        y = None
        for l_ref, u_ref, g_ref in ((la_ref, ua_ref, ga_ref), (lb_ref, ub_ref, gb_ref), (lc_ref, uc_ref, gc_ref)):
            g = _sigmoid(l_ref[...].astype(F32))
            g_ref[...] = g.astype(BF16)
            term = g * u_ref[...].astype(F32)
            y = term if y is None else y + term
        o_ref[...] = y.astype(BF16)

    tile = pl.BlockSpec((bm, bw), lambda i, t: (i, t))
    gate = lambda b: pl.BlockSpec((bm, bw), lambda i, t: (i, g0 + b * gstep + t))
    shape = jax.ShapeDtypeStruct((rows, d), BF16)
    return pl.pallas_call(
        body, name=name, grid=(rows // bm, d // bw),
        in_specs=[gate(0), gate(1), gate(2), tile, tile, tile],
        out_specs=[tile] * 4, out_shape=[shape] * 4,
        compiler_params=_params(("parallel", "parallel")),
    )(proj, proj, proj, ua, ub, uc)


def _merge_bwd(dym, us, gs, name):
    rows, d = dym.shape
    bm = _tile(rows, 1024, 16)
    bw = _tile(d, 512)
    nb = d // bw

    def body(dy_ref, ua_ref, ub_ref, uc_ref, ga_ref, gb_ref, gc_ref, dg_ref, da_ref, db_ref, dc_ref):
        b = pl.program_id(2)
        dyv = dy_ref[...]
        for idx, (u_ref, g_ref, du_ref) in enumerate(((ua_ref, ga_ref, da_ref), (ub_ref, gb_ref, db_ref), (uc_ref, gc_ref, dc_ref))):
            @pl.when(b == idx)
            def _():
                g = g_ref[...].astype(F32)
                du_ref[...] = (g * dyv).astype(BF16)
                dg_ref[...] = (dyv * u_ref[...].astype(F32) * g * (1.0 - g)).astype(BF16)

    tile = pl.BlockSpec((bm, bw), lambda i, t, b: (i, t))
    shape = jax.ShapeDtypeStruct((rows, d), BF16)
    outs = pl.pallas_call(
        body, name=name, grid=(rows // bm, nb, 3),
        in_specs=[tile] * 7,
        out_specs=[pl.BlockSpec((bm, bw), lambda i, t, b: (i, b * nb + t)), tile, tile, tile],
        out_shape=[jax.ShapeDtypeStruct((rows, 3 * d), BF16), shape, shape, shape],
        compiler_params=_params(("parallel", "parallel", "arbitrary")),
    )(dym, *us, *gs)
    return outs[0], outs[1], outs[2], outs[3]


def _loss_head(y, target, name):
    rows, d = y.shape
    bm = _tile(rows, 256, 8)

    def body(y_ref, t_ref, dy_ref, dyb_ref, l_ref):
        i = pl.program_id(0)
        diff = y_ref[...] - t_ref[...]
        dy = diff * (1.0 / d)
        dy_ref[...] = dy
        dyb_ref[...] = dy.astype(BF16)
        sq = diff * diff
        part = sq[:, 0:LANES]
        for c in range(1, d // LANES):
            part = part + sq[:, c * LANES:(c + 1) * LANES]
        part = jnp.sum(part.reshape(bm // 8, 8, LANES), axis=0)

        @pl.when(i == 0)
        def _():
            l_ref[...] = part

        @pl.when(i > 0)
        def _():
            l_ref[...] += part

    row = pl.BlockSpec((bm, d), lambda i: (i, 0))
    return pl.pallas_call(
        body, name=name, grid=(rows // bm,), in_specs=[row, row],
        out_specs=[row, row, pl.BlockSpec((8, LANES), lambda i: (0, 0))],
        out_shape=[jax.ShapeDtypeStruct((rows, d), F32), jax.ShapeDtypeStruct((rows, d), BF16),
                   jax.ShapeDtypeStruct((8, LANES), F32)],
        compiler_params=_params(("arbitrary",)),
    )(y, target)


def _row(vec, reps=1):
    return jnp.tile(vec.reshape(1, -1).astype(F32), (1, reps))


def _local_step(x, mem, target, small, wg, shards=None):
    s, d = x.shape
    dist = shards is not None
    wg = dict(wg)
    ones = lambda n: jnp.ones((1, n), F32)
    zeros = lambda n: jnp.zeros((1, n), F32)
    scale_ab = HEAD_DIM ** -0.5
    split8 = lambda g: g.reshape(N_DEV, g.shape[0] // N_DEV, g.shape[1])
    flat8 = lambda g: g.reshape(g.shape[0] * g.shape[1], g.shape[2])
    gather = lambda names: _Comm("gather", [shards[n] for n in names]) if dist else None
    g = {}

    def scatter(names):
        return _Comm("scatter", [split8(g[n]) for n in names]) if dist else None

    def hosted(result, names, store):
        if not dist:
            return result
        out, got = result
        store.update(zip(names, got))
        return out

    hn = _rmsnorm_fwd(x, small["norm_gain"], "rms_x_fwd")
    got = {}
    proj = hosted(_mm_nn(hn, wg["qkv"], bm=1024, bn=1024, bk=d, o_dtype=BF16, name="proj_qkv",
                         comm=gather(("wa", "wb", "wc"))), ("wa", "wb", "wc"), got)
    wg.update({n: flat8(a) for n, a in got.items()})
    pfb = _mm_nn(hn, wg["wf"], bm=1024, bn=FB_PAD, bk=d, o_dtype=F32, name="proj_fb")
    mn = _rmsnorm_fwd(mem, small["mem_norm_gain"], "rms_mem_fwd")
    mkv = _mm_nn(mn, wg["wk"], bm=256, bn=1024, bk=d, o_dtype=F32, name="mem_kv")

    gain_a = jnp.concatenate([_row(small["q_gain_a"], A_Q_HEADS) * scale_ab, _row(small["k_gain_a"], A_KV_HEADS), ones(A_KV_WIDTH)], axis=1)
    flag_a = jnp.concatenate([ones(A_WIDTH + A_KV_WIDTH), zeros(A_KV_WIDTH)], axis=1)
    qkv_a = _headnorm_fwd(proj, COL_QA, 1280, 1280, HEAD_DIM, gain_a, flag_a, True, "hn_a_fwd")
    gain_b = jnp.concatenate([_row(small["q_gain_b"], B_HEADS) * scale_ab, _row(small["k_gain_b"], B_HEADS), ones(B_WIDTH)], axis=1)
    flag_b = jnp.concatenate([ones(2 * B_WIDTH), zeros(B_WIDTH)], axis=1)
    qkv_b = _headnorm_fwd(proj, COL_QB, 2304, 256, HEAD_DIM, gain_b, flag_b, True, "hn_b_fwd")
    gain_cq = _row(small["q_gain_c"], C_HEADS)
    q_c = _headnorm_fwd(proj, COL_QC, C_WIDTH, C_WIDTH, C_HEAD_DIM, gain_cq, ones(C_WIDTH), False, "hn_cq_fwd")
    gain_ck = jnp.concatenate([_row(small["k_gain_c"], C_HEADS), ones(C_WIDTH)], axis=1)
    flag_ck = jnp.concatenate([ones(C_WIDTH), zeros(C_WIDTH)], axis=1)
    mkvn = _headnorm_fwd(mkv, 0, 2 * C_WIDTH, 2 * C_WIDTH, C_HEAD_DIM, gain_ck, flag_ck, False, "hn_ck_fwd")

    q_a, k_a, v_a = qkv_a[0:12], qkv_a[12:16], qkv_a[16:20]
    q_b, k_b, v_b = qkv_b[0:12], qkv_b[12:24], qkv_b[24:36]

    bpad = jnp.pad(small["b_forget"].reshape(1, -1), ((0, 0), (0, FB_PAD - B_HEADS)))
    c16 = _fox_prep(pfb, bpad, "fox_prep")
    c3 = c16[0:B_HEADS].reshape(B_HEADS, 1, s)

    sinks = small["sinks_a"].reshape(-1)
    slopes = jnp.exp2(-8.0 * jnp.arange(1, A_Q_HEADS + 1, dtype=F32) / A_Q_HEADS)
    y_a, lse_a = _attn_a_fwd(q_a, k_a, v_a, sinks, slopes, "attn_a_fwd")
    y_b, lse_b, got_zg = _attn_b_fwd(q_b, k_b, v_b, c3, "attn_b_fwd", comm=gather(("zg",)))
    if dist:
        wg["zg"] = flat8(got_zg[0])
    y_c = _attn_c_fwd(q_c, mkvn, "attn_c_fwd")

    got = {}
    pzg = hosted(_mm_nn(hn, wg["zg"], bm=1024, bn=1024, bk=d, o_dtype=BF16, name="proj_zg", comm=gather(("wo",))),
                 ("wo",), got)
    wg.update({n: flat8(a) for n, a in got.items()})

    s_a = _gate_fwd(y_a, pzg, COL_ZA, 256, "gate_a_fwd")
    s_b = _gate_fwd(y_b, pzg, COL_ZB, 256, "gate_b_fwd")
    s_c = _gate_fwd(y_c, pzg, COL_ZC, 512, "gate_c_fwd")
    w_a, w_b, w_c = _branch_full(wg["wa"]), _branch_full(wg["wb"]), _branch_full(wg["wc"])
    u_a = _mm_nn(s_a, w_a, bm=1024, bn=2048, bk=A_WIDTH, o_dtype=BF16, name="branch_a_fwd")
    u_b = _mm_nn(s_b, w_b, bm=1024, bn=2048, bk=B_WIDTH, o_dtype=BF16, name="branch_b_fwd")
    u_c = _mm_nn(s_c, w_c, bm=1024, bn=2048, bk=C_WIDTH, o_dtype=BF16, name="branch_c_fwd")
    ym, gate_a, gate_b, gate_c = _merge_fwd(pzg, u_a, u_b, u_c, "merge_fwd")
    y = _mm_nn(ym, wg["wo"], bm=1024, bn=1024, bk=d, o_dtype=F32, name="out_proj", add=x)
    dy, dyb, lpart = _loss_head(y, target, "loss_head")
    loss = 0.5 / d * jnp.sum(lpart)

    dym = _mm_nt(dyb, wg["wo"], bm=1024, bn=1024, bk=d, o_dtype=F32, name="out_proj_bwd_act")
    g["wo"] = _mm_tn(ym, dyb, bm=512, bn=1024, bk=s, o_dtype=BF16, name="out_proj_bwd_w")

    dgate, du_a, du_b, du_c = _merge_bwd(dym, (u_a, u_b, u_c), (gate_a, gate_b, gate_c), "merge_bwd")
    parts = {}
    g["wm_g"] = hosted(_mm_tn(hn, dgate, bm=512, bn=1024, bk=s, o_dtype=BF16, name="proj_gate_bwd_w",
                              comm=scatter(("wo",))), ("wo",), parts)

    ds_a = _mm_nt(du_a, w_a, bm=1024, bn=A_WIDTH, bk=d, o_dtype=F32, name="branch_a_bwd_act")
    ds_b = _mm_nt(du_b, w_b, bm=1024, bn=B_WIDTH, bk=d, o_dtype=F32, name="branch_b_bwd_act")
    ds_c = _mm_nt(du_c, w_c, bm=1024, bn=C_WIDTH, bk=d, o_dtype=F32, name="branch_c_bwd_act")
    g["wa"] = _branch_shards(_mm_tn(s_a, du_a, bm=A_WIDTH, bn=1024, bk=s, o_dtype=BF16, name="branch_a_bwd_w"))
    g["wb"] = _branch_shards(_mm_tn(s_b, du_b, bm=B_WIDTH, bn=1024, bk=s, o_dtype=BF16, name="branch_b_bwd_w"))
    g["wc"] = _branch_shards(_mm_tn(s_c, du_c, bm=C_WIDTH, bn=1024, bk=s, o_dtype=BF16, name="branch_c_bwd_w"))

    dz = lax.empty((s, W_Z), BF16)
    dz, do_a, dd_a = _gate_bwd(ds_a, y_a, pzg, COL_ZA, 256, dz, COL_ZA, True, "gate_a_bwd")
    dz, do_b, dd_b = _gate_bwd(ds_b, y_b, pzg, COL_ZB, 256, dz, COL_ZB, True, "gate_b_bwd")
    dz, do_c, _ = _gate_bwd(ds_c, y_c, pzg, COL_ZC, 512, dz, COL_ZC, False, "gate_c_bwd")
    g["wm_z"] = _mm_tn(hn, dz, bm=512, bn=1024, bk=s, o_dtype=BF16, name="proj_z_bwd_w")

    names = ("wa", "wb", "wc")
    dq_a, dkv_a, dsink, got = _attn_a_bwd(q_a, k_a, v_a, do_a, lse_a, dd_a, sinks, slopes, "attn_a_bwd", comm=scatter(names))
    parts.update(zip(names, got))
    names = ("wm_g", "wm_z")
    dq_b, dk_b, dv_b, dc3, got = _attn_b_bwd(q_b, k_b, v_b, do_b, lse_b, dd_b, c3, "attn_b_bwd", comm=scatter(names))
    parts.update(zip(names, got))
    dq_c, dmkvn = _attn_c_bwd(q_c, mkvn, do_c, "attn_c_bwd")

    dqkv = lax.empty((s, W_QKV), BF16)
    dqkv, dg_qa = _headnorm_bwd(proj, COL_QA, A_WIDTH, 256, HEAD_DIM, gain_a[:, 0:768], flag_a[:, 0:768], dq_a, dqkv, COL_QA, "hn_qa_bwd")
    dqkv, dg_kva = _headnorm_bwd(proj, COL_KA, 512, 256, HEAD_DIM, gain_a[:, 768:1280], flag_a[:, 768:1280], dkv_a, dqkv, COL_KA, "hn_kva_bwd")
    dqkv, dg_qb = _headnorm_bwd(proj, COL_QB, B_WIDTH, 256, HEAD_DIM, gain_b[:, 0:768], flag_b[:, 0:768], dq_b, dqkv, COL_QB, "hn_qb_bwd")
    dqkv, dg_kb = _headnorm_bwd(proj, COL_KB, B_WIDTH, 256, HEAD_DIM, gain_b[:, 768:1536], flag_b[:, 768:1536], dk_b, dqkv, COL_KB, "hn_kb_bwd")
    dqkv, _ = _headnorm_bwd(proj, COL_VB, B_WIDTH, 256, HEAD_DIM, gain_b[:, 1536:2304], flag_b[:, 1536:2304], dv_b, dqkv, COL_VB, "hn_vb_bwd")
    dqkv, dg_qc = _headnorm_bwd(proj, COL_QC, C_WIDTH, 512, C_HEAD_DIM, gain_cq, ones(C_WIDTH), dq_c, dqkv, COL_QC, "hn_qc_bwd")
    dmkv, dg_kc = _headnorm_bwd(mkv, 0, 2 * C_WIDTH, 2 * C_WIDTH, C_HEAD_DIM, gain_ck, flag_ck, dmkvn, None, 0, "hn_kc_bwd")

    dct = jnp.pad(dc3.reshape(B_HEADS, s), ((0, 16 - B_HEADS), (0, 0)))
    dfb, dbf = _fox_prep_bwd(pfb, bpad, dct, "fox_prep_bwd")

    dmn = _mm_nt(dmkv, wg["wk"], bm=256, bn=1024, bk=1024, o_dtype=F32, name="mem_kv_bwd_act")
    g["wk"] = _mm_tn(mn, dmkv, bm=512, bn=1024, bk=mem.shape[0], o_dtype=BF16, name="mem_kv_bwd_w")
    _, dg_mem = _rmsnorm_bwd(mem, dmn, small["mem_norm_gain"], None, "rms_mem_bwd")

    g["wm_qkv"] = _mm_tn(hn, dqkv, bm=512, bn=1024, bk=s, o_dtype=BF16, name="proj_qkv_bwd_w")
    g["wf"] = _mm_tn(hn, dfb, bm=512, bn=FB_PAD, bk=s, o_dtype=BF16, name="proj_fb_bwd_w")
    dhn = _mm_nt(dqkv, wg["qkv"], bm=1024, bn=1024, bk=2048, o_dtype=F32, name="proj_qkv_bwd_act")
    dhn = _mm_nt(dfb, wg["wf"], bm=1024, bn=1024, bk=FB_PAD, o_dtype=F32, name="proj_fb_bwd_act", add=dhn)
    names = ("wm_qkv", "wf", "wk")
    dhn = hosted(_mm_nt_cat(dz, dgate, wg["zg"], bm=1024, bn=1024, bk=2048, name="proj_zg_bwd_act", add=dhn,
                            comm=scatter(names)), names, parts)
    if dist:
        g = parts
    grad_x, dg_x = _rmsnorm_bwd(x, dhn, small["norm_gain"], dy, "rms_x_bwd")

    fold = lambda part, heads, hd: jnp.sum(jnp.sum(part, axis=0).reshape(heads, hd), axis=0).reshape(1, hd)
    small_grads = {
        "norm_gain": jnp.sum(dg_x, axis=0).reshape(1, d),
        "mem_norm_gain": jnp.sum(dg_mem, axis=0).reshape(1, d),
        "b_forget": dbf[0:B_HEADS, 0].reshape(1, B_HEADS),
        "q_gain_a": fold(dg_qa, A_Q_HEADS, HEAD_DIM) * scale_ab,
        "k_gain_a": fold(dg_kva[:, 0:A_KV_WIDTH], A_KV_HEADS, HEAD_DIM),
        "sinks_a": (jnp.sum(dsink, axis=(1, 2)) * (1.0 / HEAD_DIM)).reshape(1, A_Q_HEADS),
        "q_gain_b": fold(dg_qb, B_HEADS, HEAD_DIM) * scale_ab,
        "k_gain_b": fold(dg_kb, B_HEADS, HEAD_DIM),
        "q_gain_c": fold(dg_qc, C_HEADS, C_HEAD_DIM),
        "k_gain_c": fold(dg_kc[:, 0:C_WIDTH], C_HEADS, C_HEAD_DIM),
    }
    return loss, grad_x, small_grads, g


def _coords():
    return lax.axis_index("x"), lax.axis_index("y"), lax.axis_index("c")


def _all_gather(shards, name):
    n = len(shards)

    def body(*refs):
        ins = refs[0:n]
        outs = refs[n:2 * n]
        send_sems, recv_sems, local_sems = refs[2 * n:2 * n + 3]
        x, y, c = _coords()
        me, sibling = (x, y, c), (x, y, 1 - c)
        chips = [(1 - x, y), (x, 1 - y), (1 - x, 1 - y)]
        idx = lambda p: 4 * p[0] + 2 * p[1] + p[2]

        def copy(a, k, block, to, src=None):
            slot = outs[a].at[idx(block)]
            return pltpu.make_async_remote_copy(
                src_ref=slot if src is None else src, dst_ref=slot,
                send_sem=send_sems.at[a, k], recv_sem=recv_sems.at[a, k], device_id=to, device_id_type=MESH)

        mine = [pltpu.make_async_copy(ins[a], outs[a].at[idx(me)], local_sems.at[a]) for a in range(n)]
        for cp in mine:
            cp.start()
        first = []
        for a in range(n):
            first.append(copy(a, 0, me, sibling, src=ins[a]))
            first += [copy(a, 1 + j, me, (*chip, c), src=ins[a]) for j, chip in enumerate(chips)]
        for cp in first:
            cp.start()
        passed = []
        for j, chip in enumerate(chips):
            for a in range(n):
                copy(a, 1 + j, (*chip, c), me).wait_recv()
                fwd = copy(a, 4 + j, (*chip, c), sibling)
                fwd.start()
                passed.append(fwd)
        for a in range(n):
            copy(a, 0, sibling, me).wait_recv()
            for j, chip in enumerate(chips):
                copy(a, 4 + j, (*chip, 1 - c), me).wait_recv()
        for cp in first + passed:
            cp.wait_send()
        for cp in mine:
            cp.wait()

    any_spec = pl.BlockSpec(memory_space=pl.ANY)
    return pl.pallas_call(
        body, name=name,
        in_specs=[any_spec] * n, out_specs=[any_spec] * n,
        out_shape=[jax.ShapeDtypeStruct((N_DEV,) + sh.shape, sh.dtype) for sh in shards],
        scratch_shapes=[pltpu.SemaphoreType.DMA((n, 7)), pltpu.SemaphoreType.DMA((n, 7)), pltpu.SemaphoreType.DMA((n,))],
    )(*shards)


def _all_reduce_small(vec, name):
    p = vec.shape[1]

    def body(v_ref, o_ref, gather, send_sems, recv_sems):
        x, y, c = _coords()
        my = 4 * x + 2 * y + c
        peers = [(x ^ ((k >> 2) & 1), y ^ ((k >> 1) & 1), c ^ (k & 1)) for k in range(1, N_DEV)]
        gather[my] = v_ref[...]
        sends = [pltpu.make_async_remote_copy(
            src_ref=v_ref, dst_ref=gather.at[my], send_sem=send_sems.at[k], recv_sem=recv_sems.at[k],
            device_id=peer, device_id_type=MESH) for k, peer in enumerate(peers)]
        for cp in sends:
            cp.start()
        for k, peer in enumerate(peers):
            pid = 4 * peer[0] + 2 * peer[1] + peer[2]
            pltpu.make_async_remote_copy(
                src_ref=v_ref, dst_ref=gather.at[pid], send_sem=send_sems.at[k], recv_sem=recv_sems.at[k],
                device_id=peer, device_id_type=MESH).wait_recv()
        for cp in sends:
            cp.wait_send()
        total = gather[0]
        for j in range(1, N_DEV):
            total = total + gather[j]
        o_ref[...] = total

    vm = pl.BlockSpec(memory_space=pltpu.VMEM)
    return pl.pallas_call(
        body, name=name, in_specs=[vm], out_specs=vm,
        out_shape=jax.ShapeDtypeStruct((8, p), F32),
        scratch_shapes=[pltpu.VMEM((N_DEV, 8, p), F32), pltpu.SemaphoreType.DMA((7,)), pltpu.SemaphoreType.DMA((7,))],
    )(vec)[0:1]


def _sum_parts(parts, name):
    _, rows, cols = parts.shape
    br = _tile(rows, 64, 16)

    def body(p_ref, o_ref):
        total = p_ref[0].astype(F32)
        for j in range(1, N_DEV):
            total = total + p_ref[j].astype(F32)
        o_ref[...] = total

    return pl.pallas_call(
        body, name=name, grid=(rows // br,),
        in_specs=[pl.BlockSpec((N_DEV, br, cols), lambda i: (0, i, 0))],
        out_specs=pl.BlockSpec((br, cols), lambda i: (i, 0)),
        out_shape=jax.ShapeDtypeStruct((rows, cols), F32),
        compiler_params=_params(("parallel",), VMEM_BIG),
    )(parts)


def _adamw(w, g, m, v, name, br=32):
    rows, cols = w.shape
    br = min(br, rows)
    c1 = 1.0 / (1.0 - ADAM_B1 ** ADAM_STEP)
    c2 = 1.0 / (1.0 - ADAM_B2 ** ADAM_STEP)

    def body(w_ref, g_ref, m_ref, v_ref, d_ref, nm_ref, nv_ref):
        gv = g_ref[...]
        nm = ADAM_B1 * m_ref[...] + (1.0 - ADAM_B1) * gv
        nv = ADAM_B2 * v_ref[...] + (1.0 - ADAM_B2) * (gv * gv)
        d_ref[...] = -ADAM_LR * ((nm * c1) / (jnp.sqrt(nv * c2) + ADAM_EPS) + ADAM_WD * w_ref[...])
        nm_ref[...] = nm
        nv_ref[...] = nv

    spec = pl.BlockSpec((br, cols), lambda i: (i, 0))
    shape = jax.ShapeDtypeStruct((rows, cols), F32)
    return pl.pallas_call(
        body, name=name, grid=(pl.cdiv(rows, br),), in_specs=[spec] * 4, out_specs=[spec] * 3, out_shape=[shape] * 3,
        compiler_params=_params(("parallel",), VMEM_BIG),
    )(w, g, m, v)


def _adamw_parts(w, parts, m, v, name):
    rows, cols = w.shape
    br = _tile(rows, 32, 16)
    c1 = 1.0 / (1.0 - ADAM_B1 ** ADAM_STEP)
    c2 = 1.0 / (1.0 - ADAM_B2 ** ADAM_STEP)

    def body(w_ref, p_ref, m_ref, v_ref, g_ref, d_ref, nm_ref, nv_ref):
        gv = p_ref[0].astype(F32)
        for j in range(1, N_DEV):
            gv = gv + p_ref[j].astype(F32)
        nm = ADAM_B1 * m_ref[...] + (1.0 - ADAM_B1) * gv
        nv = ADAM_B2 * v_ref[...] + (1.0 - ADAM_B2) * (gv * gv)
        g_ref[...] = gv
        d_ref[...] = -ADAM_LR * ((nm * c1) / (jnp.sqrt(nv * c2) + ADAM_EPS) + ADAM_WD * w_ref[...])
        nm_ref[...] = nm
        nv_ref[...] = nv

    spec = pl.BlockSpec((br, cols), lambda i: (i, 0))
    shape = jax.ShapeDtypeStruct((rows, cols), F32)
    return pl.pallas_call(
        body, name=name, grid=(rows // br,),
        in_specs=[spec, pl.BlockSpec((N_DEV, br, cols), lambda i: (0, i, 0)), spec, spec],
        out_specs=[spec] * 4, out_shape=[shape] * 4,
        compiler_params=_params(("parallel",), VMEM_BIG),
    )(w, parts, m, v)


SMALL_NAMES = ("norm_gain", "mem_norm_gain", "b_forget", "q_gain_a", "k_gain_a", "sinks_a",
               "q_gain_b", "k_gain_b", "q_gain_c", "k_gain_c")
BIG_NAMES = ("w_in", "w_mem_kv", "w_branch_a", "w_branch_b", "w_branch_c", "w_out")
WEIGHT_ORDER = ("norm_gain", "mem_norm_gain", "w_in", "b_forget", "q_gain_a", "k_gain_a", "sinks_a", "q_gain_b",
                "k_gain_b", "q_gain_c", "k_gain_c", "w_mem_kv", "w_branch_a", "w_branch_b", "w_branch_c", "w_out")


def _pack_small(tree):
    flat = jnp.concatenate([tree[n].reshape(1, -1) for n in SMALL_NAMES], axis=1)
    pad = (-flat.shape[1]) % LANES
    return jnp.pad(flat, ((0, 0), (0, pad)))


def _unpack_small(flat, like):
    out, off = {}, 0
    for n in SMALL_NAMES:
        size = like[n].size
        out[n] = flat[:, off:off + size].reshape(like[n].shape)
        off += size
    return out


def kernel(x, mem, norm_gain, mem_norm_gain, w_in, b_forget, q_gain_a, k_gain_a, sinks_a, q_gain_b, k_gain_b, q_gain_c, k_gain_c, w_mem_kv, w_branch_a, w_branch_b, w_branch_c, w_out, loss_target, m_norm_gain, m_mem_norm_gain, m_w_in, m_b_forget, m_q_gain_a, m_k_gain_a, m_sinks_a, m_q_gain_b, m_k_gain_b, m_q_gain_c, m_k_gain_c, m_w_mem_kv, m_w_branch_a, m_w_branch_b, m_w_branch_c, m_w_out, v_norm_gain, v_mem_norm_gain, v_w_in, v_b_forget, v_q_gain_a, v_k_gain_a, v_sinks_a, v_q_gain_b, v_k_gain_b, v_q_gain_c, v_k_gain_c, v_w_mem_kv, v_w_branch_a, v_w_branch_b, v_w_branch_c, v_w_out):
    weights = dict(norm_gain=norm_gain, mem_norm_gain=mem_norm_gain, w_in=w_in, b_forget=b_forget, q_gain_a=q_gain_a,
                   k_gain_a=k_gain_a, sinks_a=sinks_a, q_gain_b=q_gain_b, k_gain_b=k_gain_b, q_gain_c=q_gain_c,
                   k_gain_c=k_gain_c, w_mem_kv=w_mem_kv, w_branch_a=w_branch_a, w_branch_b=w_branch_b,
                   w_branch_c=w_branch_c, w_out=w_out)
    mom_m = dict(norm_gain=m_norm_gain, mem_norm_gain=m_mem_norm_gain, w_in=m_w_in, b_forget=m_b_forget,
                 q_gain_a=m_q_gain_a, k_gain_a=m_k_gain_a, sinks_a=m_sinks_a, q_gain_b=m_q_gain_b, k_gain_b=m_k_gain_b,
                 q_gain_c=m_q_gain_c, k_gain_c=m_k_gain_c, w_mem_kv=m_w_mem_kv, w_branch_a=m_w_branch_a,
                 w_branch_b=m_w_branch_b, w_branch_c=m_w_branch_c, w_out=m_w_out)
    mom_v = dict(norm_gain=v_norm_gain, mem_norm_gain=v_mem_norm_gain, w_in=v_w_in, b_forget=v_b_forget,
                 q_gain_a=v_q_gain_a, k_gain_a=v_k_gain_a, sinks_a=v_sinks_a, q_gain_b=v_q_gain_b, k_gain_b=v_k_gain_b,
                 q_gain_c=v_q_gain_c, k_gain_c=v_k_gain_c, w_mem_kv=v_w_mem_kv, w_branch_a=v_w_branch_a,
                 w_branch_b=v_w_branch_b, w_branch_c=v_w_branch_c, w_out=v_w_out)
    wi = w_in[0]
    sh_qkv = jnp.concatenate([wi[:, a:b] for a, b in SRC_RANGES[0:3]], axis=1).astype(BF16)
    sh_zg = jnp.concatenate([wi[:, a:b] for a, b in SRC_RANGES[3:6]] + [wi[:, SRC_GATE:]], axis=1).astype(BF16)
    sh_wf = jnp.pad(wi[:, FB_SRC:FB_SRC + B_HEADS], ((0, 0), (0, FB_PAD - B_HEADS))).astype(BF16)
    shards = {"zg": sh_zg, "wo": w_out[0].astype(BF16), "wa": w_branch_a[0].astype(BF16),
              "wb": w_branch_b[0].astype(BF16), "wc": w_branch_c[0].astype(BF16)}
    first = ("qkv", "wf", "wk")
    full = _all_gather([sh_qkv, sh_wf, w_mem_kv[0].astype(BF16)], "weights_all_gather")
    wg = {kname: arr.reshape(arr.shape[0] * arr.shape[1], arr.shape[2]) for kname, arr in zip(first, full)}

    small = {n: weights[n] for n in SMALL_NAMES}
    loss_local, grad_x, small_g, parts = _local_step(x[0], mem[0], loss_target[0], small, wg, shards)

    grads, delta, new_m, new_v = {}, {}, {}, {}
    for n, kname in (("w_mem_kv", "wk"), ("w_out", "wo"), ("w_branch_a", "wa"), ("w_branch_b", "wb"), ("w_branch_c", "wc")):
        gsum, dlt, nm, nv = _adamw_parts(weights[n][0], parts[kname], mom_m[n][0], mom_v[n][0], "adamw_" + n)
        grads[n], delta[n], new_m[n], new_v[n] = gsum, dlt[None], nm[None], nv[None]
    gq, gz, gf, gg = (_sum_parts(parts[k], "grad_sum_" + k) for k in ("wm_qkv", "wm_z", "wf", "wm_g"))
    g_in = jnp.concatenate([gq[:, COL_QA:COL_QB], gz[:, COL_ZA:COL_ZB], gq[:, COL_QB:COL_QC], gz[:, COL_ZB:COL_ZC],
                            gf[:, 0:B_HEADS], gq[:, COL_QC:W_QKV], gz[:, COL_ZC:W_Z], gg], axis=1)
    lin = lambda a: a.T.reshape(-1, LANES)
    unlin = lambda a: a.reshape(g_in.shape[1], g_in.shape[0]).T[None]
    dlt, nm, nv = _adamw(lin(w_in[0]), lin(g_in), lin(m_w_in[0]), lin(v_w_in[0]), "adamw_w_in", br=1640)
    grads["w_in"], delta["w_in"], new_m["w_in"], new_v["w_in"] = g_in, unlin(dlt), unlin(nm), unlin(nv)

    packed = _pack_small(small_g)
    reduced = _all_reduce_small(jnp.broadcast_to(packed, (8, packed.shape[1])), "small_all_reduce")
    grads.update(_unpack_small(reduced, small))

    loss = lax.psum(loss_local, ("x", "y", "c"))

    pw, pm, pv = _pack_small(small), _pack_small({n: mom_m[n] for n in SMALL_NAMES}), _pack_small({n: mom_v[n] for n in SMALL_NAMES})
    rep8 = lambda a: jnp.broadcast_to(a, (8, a.shape[1]))
    dlt, nm, nv = _adamw(rep8(pw), rep8(reduced), rep8(pm), rep8(pv), "adamw_small")
    for tree, flat in ((delta, dlt), (new_m, nm), (new_v, nv)):
        tree.update(_unpack_small(flat[0:1], small))
    for n in BIG_NAMES:
        grads[n] = grads[n][None]
    return (loss, grad_x[None], *[grads[n] for n in WEIGHT_ORDER], *[delta[n] for n in WEIGHT_ORDER],
            *[new_m[n] for n in WEIGHT_ORDER], *[new_v[n] for n in WEIGHT_ORDER])
```

```python
import math

import jax
import jax.numpy as jnp
import numpy as np
from jax import lax
from jax.experimental import pallas as pl
from jax.experimental.pallas import tpu as pltpu

F32 = jnp.float32
BF16 = jnp.bfloat16

N_DEV = 8
HEAD_DIM = 64
A_Q_HEADS = 12
A_KV_HEADS = 4
A_GROUP = 3
B_HEADS = 12
C_HEADS = 4
C_HEAD_DIM = 128
WINDOW = 128
A_WIDTH = 768
A_KV_WIDTH = 256
B_WIDTH = 768
C_WIDTH = 512
EPS = 1e-6
NEG = -1e30

COL_QA, COL_KA, COL_VA = 0, 768, 1024
COL_QB, COL_KB, COL_VB = 1280, 2048, 2816
COL_QC = 3584
W_QKV = 4096
COL_ZA, COL_ZB, COL_ZC = 0, 768, 1536
COL_GATE = W_Z = 2048
SRC_RANGES = ((0, 1280), (2048, 4352), (5132, 5644), (1280, 2048), (4352, 5120), (5644, 6156))
SRC_GATE = 6156
FB_SRC = 5120
FB_PAD = 128

ADAM_LR = 0.001
ADAM_B1 = 0.9
ADAM_B2 = 0.999
ADAM_EPS = 1e-08
ADAM_WD = 0.01
ADAM_STEP = 10

VMEM_BIG = 52 * 1024 * 1024
LANES = 128
MESH = pl.DeviceIdType.MESH


def _tile(n, pref, mult=128):
    if n <= pref:
        return n
    t = (pref // mult) * mult
    while t >= mult:
        if n % t == 0:
            return t
        t -= mult
    return n


def _params(sem=None, vmem=None):
    kw = {}
    if sem is not None:
        kw["dimension_semantics"] = sem
    if vmem is not None:
        kw["vmem_limit_bytes"] = vmem
    return pltpu.CompilerParams(**kw)


def _sigmoid(x):
    return 1.0 / (1.0 + jnp.exp(-x))


def _block_diag(hd):
    r = np.arange(LANES)
    return jnp.asarray((r[:, None] // hd) == (r[None, :] // hd), dtype=BF16)


def _seg_sum(t, bd):
    hi = t.astype(BF16)
    lo = (t - hi.astype(F32)).astype(BF16)
    outs = []
    for c in range(t.shape[1] // LANES):
        sl = slice(c * LANES, (c + 1) * LANES)
        outs.append(jnp.dot(hi[:, sl], bd, preferred_element_type=F32) + jnp.dot(lo[:, sl], bd, preferred_element_type=F32))
    return outs[0] if len(outs) == 1 else jnp.concatenate(outs, axis=1)


def _rmsnorm_fwd(x, gain, name):
    rows, d = x.shape
    bm = _tile(rows, 512, 8)

    def body(x_ref, g_ref, o_ref):
        xv = x_ref[...]
        ms = jnp.mean(xv * xv, axis=-1, keepdims=True)
        o_ref[...] = (xv * lax.rsqrt(ms + EPS) * g_ref[...]).astype(BF16)

    return pl.pallas_call(
        body, name=name, grid=(rows // bm,),
        in_specs=[pl.BlockSpec((bm, d), lambda i: (i, 0)), pl.BlockSpec((1, d), lambda i: (0, 0))],
        out_specs=pl.BlockSpec((bm, d), lambda i: (i, 0)),
        out_shape=jax.ShapeDtypeStruct((rows, d), BF16),
        compiler_params=_params(("parallel",)),
    )(x, gain)


def _rmsnorm_bwd(x, dhn, gain, dy, name):
    rows, d = x.shape
    bm = _tile(rows, 512, 8)
    with_dx = dy is not None

    def body(*refs):
        if with_dx:
            x_ref, dh_ref, g_ref, dy_ref, gx_ref, dg_ref = refs
        else:
            x_ref, dh_ref, g_ref, dg_ref = refs
        i = pl.program_id(0)
        xv = x_ref[...]
        rstd = lax.rsqrt(jnp.mean(xv * xv, axis=-1, keepdims=True) + EPS)
        xhat = xv * rstd
        dh = dh_ref[...]
        part = jnp.sum((dh * xhat).reshape(bm // 8, 8, d), axis=0)

        @pl.when(i == 0)
        def _():
            dg_ref[...] = part

        @pl.when(i > 0)
        def _():
            dg_ref[...] += part

        if with_dx:
            g = dh * g_ref[...]
            mean = jnp.mean(g * xhat, axis=-1, keepdims=True)
            gx_ref[...] = dy_ref[...] + rstd * (g - xhat * mean)

    row_spec = pl.BlockSpec((bm, d), lambda i: (i, 0))
    in_specs = [row_spec, row_spec, pl.BlockSpec((1, d), lambda i: (0, 0))]
    args = [x, dhn, gain]
    dg_spec = pl.BlockSpec((8, d), lambda i: (0, 0))
    dg_shape = jax.ShapeDtypeStruct((8, d), F32)
    if with_dx:
        in_specs.append(row_spec)
        args.append(dy)
        out_specs = [row_spec, dg_spec]
        out_shape = [jax.ShapeDtypeStruct((rows, d), F32), dg_shape]
    else:
        out_specs = [dg_spec]
        out_shape = [dg_shape]
    outs = pl.pallas_call(
        body, name=name, grid=(rows // bm,), in_specs=in_specs, out_specs=out_specs, out_shape=out_shape,
        compiler_params=_params(("arbitrary",), VMEM_BIG),
    )(*args)
    return outs if with_dx else (None, outs[0])


class _Comm:
    def __init__(self, kind, arrays):
        self.kind = kind
        self.arrays = list(arrays)
        self.n = len(self.arrays)

    def out_shapes(self):
        if self.kind == "gather":
            return [jax.ShapeDtypeStruct((N_DEV,) + a.shape, a.dtype) for a in self.arrays]
        return [jax.ShapeDtypeStruct(a.shape, a.dtype) for a in self.arrays]

    def scratch(self):
        return [pltpu.SemaphoreType.DMA((self.n, N_DEV - 1)), pltpu.SemaphoreType.DMA((self.n, N_DEV - 1)),
                pltpu.SemaphoreType.DMA((self.n,))]

    def _plan(self, ins, outs, sems, with_recvs):
        send_sems, recv_sems, local_sems = sems
        x, y, c = lax.axis_index("x"), lax.axis_index("y"), lax.axis_index("c")
        my = 4 * x + 2 * y + c
        gather = self.kind == "gather"
        local, sends, recvs = [], [], []
        for a in range(self.n):
            local.append(pltpu.make_async_copy(ins[a] if gather else ins[a].at[my], outs[a].at[my], local_sems.at[a]))
            for k in range(1, N_DEV):
                peer = (x ^ ((k >> 2) & 1), y ^ ((k >> 1) & 1), c ^ (k & 1))
                pid = 4 * peer[0] + 2 * peer[1] + peer[2]
                src = ins[a] if gather else ins[a].at[pid]
                sem = dict(send_sem=send_sems.at[a, k - 1], recv_sem=recv_sems.at[a, k - 1], device_id=peer, device_id_type=MESH)
                sends.append(pltpu.make_async_remote_copy(src_ref=src, dst_ref=outs[a].at[my], **sem))
                if with_recvs:
                    recvs.append(pltpu.make_async_remote_copy(src_ref=src, dst_ref=outs[a].at[pid], **sem))
        return local, sends, recvs

    def start(self, ins, outs, sems):
        local, sends, _ = self._plan(ins, outs, sems, False)
        for cp in local + sends:
            cp.start()

    def wait(self, ins, outs, sems):
        local, sends, recvs = self._plan(ins, outs, sems, True)
        for cp in recvs:
            cp.wait_recv()
        for cp in sends:
            cp.wait_send()
        for cp in local:
            cp.wait()


def _grid_edges(grid):
    first = last = None
    for ax, size in enumerate(grid):
        pid = pl.program_id(ax)
        f, l = pid == 0, pid == size - 1
        first = f if first is None else first & f
        last = l if last is None else last & l
    return first, last


def _hosted_call(body, comm, *, name, grid, in_specs, out_specs, out_shape, scratch_shapes, args, sem, vmem=None):
    in_specs, out_specs, out_shape, scratch_shapes = list(in_specs), list(out_specs), list(out_shape), list(scratch_shapes)
    if comm is None:
        res = pl.pallas_call(body, name=name, grid=grid, in_specs=in_specs, out_specs=out_specs, out_shape=out_shape,
                             scratch_shapes=scratch_shapes, compiler_params=_params(sem, vmem))(*args)
        return list(res), []
    n_in, n_out, n_scr, nc = len(in_specs), len(out_shape), len(scratch_shapes), comm.n

    def hosted(*refs):
        ins = refs[0:n_in]
        comm_in = refs[n_in:n_in + nc]
        outs = refs[n_in + nc:n_in + nc + n_out]
        comm_out = refs[n_in + nc + n_out:n_in + 2 * nc + n_out]
        scr = refs[n_in + 2 * nc + n_out:n_in + 2 * nc + n_out + n_scr]
        sems = refs[n_in + 2 * nc + n_out + n_scr:]
        first, last = _grid_edges(grid)

        @pl.when(first)
        def _():
            comm.start(comm_in, comm_out, sems)

        body(*ins, *outs, *scr)

        @pl.when(last)
        def _():
            comm.wait(comm_in, comm_out, sems)

    any_spec = pl.BlockSpec(memory_space=pl.ANY)
    res = pl.pallas_call(
        hosted, name=name, grid=grid, in_specs=in_specs + [any_spec] * nc, out_specs=out_specs + [any_spec] * nc,
        out_shape=out_shape + comm.out_shapes(), scratch_shapes=scratch_shapes + comm.scratch(),
        compiler_params=_params(("arbitrary",) * len(grid), vmem),
    )(*args, *comm.arrays)
    return list(res[0:n_out]), list(res[n_out:])


def _mm(a, b, *, grid, a_spec, b_spec, o_spec, o_shape, o_dtype, contract, name, add=None, add_spec=None, acc_shape=None,
        comm=None):
    nk = grid[2]
    has_add = add is not None

    def body(*refs):
        a_ref, b_ref = refs[0], refs[1]
        add_ref = refs[2] if has_add else None
        o_ref = refs[3] if has_add else refs[2]
        part = lax.dot_general(a_ref[...], b_ref[...], (contract, ((), ())), preferred_element_type=F32)
        if nk == 1:
            if has_add:
                part = part + add_ref[...]
            o_ref[...] = part.astype(o_dtype)
        else:
            acc = refs[-1]
            k = pl.program_id(2)

            @pl.when(k == 0)
            def _():
                acc[...] = part

            @pl.when(k > 0)
            def _():
                acc[...] += part

            @pl.when(k == nk - 1)
            def _():
                r = acc[...]
                if has_add:
                    r = r + add_ref[...]
                o_ref[...] = r.astype(o_dtype)

    in_specs = [a_spec, b_spec] + ([add_spec] if has_add else [])
    args = [a, b] + ([add] if has_add else [])
    scratch = [pltpu.VMEM(acc_shape, F32)] if nk > 1 else []
    outs, comm_outs = _hosted_call(
        body, comm, name=name, grid=grid, in_specs=in_specs, out_specs=[o_spec],
        out_shape=[jax.ShapeDtypeStruct(o_shape, o_dtype)], scratch_shapes=scratch, args=args,
        sem=("parallel", "parallel", "arbitrary"), vmem=VMEM_BIG)
    return outs[0] if comm is None else (outs[0], comm_outs)


def _mm_nn(a, b, *, bm, bn, bk, o_dtype, name, add=None, comm=None):
    m, kd = a.shape
    n = b.shape[1]
    bm, bn, bk = _tile(m, bm, 8), _tile(n, bn), _tile(kd, bk)
    o_spec = pl.BlockSpec((bm, bn), lambda i, j, k: (i, j))
    return _mm(a, b, grid=(m // bm, n // bn, kd // bk),
               a_spec=pl.BlockSpec((bm, bk), lambda i, j, k: (i, k)),
               b_spec=pl.BlockSpec((bk, bn), lambda i, j, k: (k, j)),
               o_spec=o_spec, o_shape=(m, n), o_dtype=o_dtype, contract=((1,), (0,)), name=name,
               add=add, add_spec=o_spec, acc_shape=(bm, bn), comm=comm)


def _mm_nt(a, b, *, bm, bn, bk, o_dtype, name, add=None, b_col0=0, comm=None):
    m, kd = a.shape
    n = b.shape[0]
    bm, bn, bk = _tile(m, bm, 8), _tile(n, bn), _tile(math.gcd(kd, b_col0), bk)
    kb0 = b_col0 // bk
    o_spec = pl.BlockSpec((bm, bn), lambda i, j, k: (i, j))
    return _mm(a, b, grid=(m // bm, n // bn, kd // bk),
               a_spec=pl.BlockSpec((bm, bk), lambda i, j, k: (i, k)),
               b_spec=pl.BlockSpec((bn, bk), lambda i, j, k: (j, kb0 + k)),
               o_spec=o_spec, o_shape=(m, n), o_dtype=o_dtype, contract=((1,), (1,)), name=name,
               add=add, add_spec=o_spec, acc_shape=(bm, bn), comm=comm)


def _mm_nt_sum(terms, *, bm, bn, bk, name, comm=None):
    m = terms[0][0].shape[0]
    n = terms[0][1].shape[0]
    bm, bn = _tile(m, bm, 8), _tile(n, bn)
    nt = (((1,), (1,)), ((), ()))
    plan, start = [], 0
    for a, b, col0 in terms:
        kd = a.shape[1]
        tk = _tile(math.gcd(kd, col0), bk)
        plan.append((start, kd // tk, col0 // tk, tk))
        start += kd // tk
    nk = start
    nterm = len(terms)

    def body(*refs):
        o_ref, acc = refs[2 * nterm], refs[2 * nterm + 1]
        k = pl.program_id(2)
        for t, (s0, steps, _, _) in enumerate(plan):
            @pl.when((k >= s0) & (k < s0 + steps))
            def _():
                part = lax.dot_general(refs[2 * t][...], refs[2 * t + 1][...], nt, preferred_element_type=F32)

                @pl.when(k == 0)
                def _():
                    acc[...] = part

                @pl.when(k > 0)
                def _():
                    acc[...] += part

        @pl.when(k == nk - 1)
        def _():
            o_ref[...] = acc[...]

    def a_spec(tk, s0, steps):
        return pl.BlockSpec((bm, tk), lambda i, j, k: (i, jnp.clip(k - s0, 0, steps - 1)))

    def b_spec(tk, s0, steps, off):
        return pl.BlockSpec((bn, tk), lambda i, j, k: (j, off + jnp.clip(k - s0, 0, steps - 1)))

    in_specs, args = [], []
    for (a, b, _), (s0, steps, cb0, tk) in zip(terms, plan):
        in_specs += [a_spec(tk, s0, steps), b_spec(tk, s0, steps, cb0)]
        args += [a, b]
    o_spec = pl.BlockSpec((bm, bn), lambda i, j, k: (i, j))
    outs, comm_outs = _hosted_call(
        body, comm, name=name, grid=(m // bm, n // bn, nk), in_specs=in_specs,
        out_specs=[o_spec], out_shape=[jax.ShapeDtypeStruct((m, n), F32)],
        scratch_shapes=[pltpu.VMEM((bm, bn), F32)], args=args,
        sem=("parallel", "parallel", "arbitrary"), vmem=VMEM_BIG)
    return outs[0] if comm is None else (outs[0], comm_outs)


def _mm_tn(a, b, *, bm, bn, bk, o_dtype, name, comm=None):
    kd, m = a.shape
    n = b.shape[1]
    bm, bn, bk = _tile(m, bm), _tile(n, bn), _tile(kd, bk, 8)
    return _mm(a, b, grid=(m // bm, n // bn, kd // bk),
               a_spec=pl.BlockSpec((bk, bm), lambda i, j, k: (k, i)),
               b_spec=pl.BlockSpec((bk, bn), lambda i, j, k: (k, j)),
               o_spec=pl.BlockSpec((bm, bn), lambda i, j, k: (i, j)),
               o_shape=(m, n), o_dtype=o_dtype, contract=((0,), (0,)), name=name, acc_shape=(bm, bn), comm=comm)


def _branch_full(w8):
    kb, ds = w8.shape[0] // N_DEV, w8.shape[1]
    return w8.reshape(N_DEV, kb, ds).transpose(1, 0, 2).reshape(kb, N_DEV * ds)


def _branch_shards(g):
    kb, ds = g.shape[0], g.shape[1] // N_DEV
    return g.reshape(kb, N_DEV, ds).transpose(1, 0, 2).reshape(N_DEV * kb, ds)


def _headnorm_fwd(src, c0, width, bw, hd, gain, nflag, head_major, name):
    rows = src.shape[0]
    bm = _tile(rows, 2048 if bw <= 256 else 1024, 16)
    bd = _block_diag(hd)
    cb0 = c0 // bw

    def body(x_ref, g_ref, f_ref, bd_ref, o_ref):
        xv = x_ref[...].astype(F32)
        ss = _seg_sum(xv * xv, bd_ref[...])
        rstd = lax.rsqrt(ss * (1.0 / hd) + EPS)
        y = (xv * jnp.where(f_ref[...] > 0.0, rstd, 1.0) * g_ref[...]).astype(BF16)
        if head_major:
            for h in range(bw // HEAD_DIM):
                o_ref[h] = y[:, h * HEAD_DIM:(h + 1) * HEAD_DIM]
        else:
            o_ref[...] = y

    vec_spec = pl.BlockSpec((1, bw), lambda i, t: (0, t))
    if head_major:
        hpb = bw // HEAD_DIM
        out_spec = pl.BlockSpec((hpb, bm, HEAD_DIM), lambda i, t: (t, i, 0))
        out_shape = jax.ShapeDtypeStruct((width // HEAD_DIM, rows, HEAD_DIM), BF16)
    else:
        out_spec = pl.BlockSpec((bm, bw), lambda i, t: (i, t))
        out_shape = jax.ShapeDtypeStruct((rows, width), BF16)
    return pl.pallas_call(
        body, name=name, grid=(rows // bm, width // bw),
        in_specs=[pl.BlockSpec((bm, bw), lambda i, t: (i, cb0 + t)), vec_spec, vec_spec,
                  pl.BlockSpec((LANES, LANES), lambda i, t: (0, 0))],
        out_specs=out_spec, out_shape=out_shape,
        compiler_params=_params(("parallel", "parallel")),
    )(src, gain, nflag, bd)


def _headnorm_bwd(src, c0, width, bw, hd, gain, nflag, dyn, target, t0, name):
    rows = src.shape[0]
    bm = _tile(rows, 2048 if bw <= 256 else 1024, 16)
    bd = _block_diag(hd)
    cb0 = c0 // bw
    tb0 = t0 // bw
    aliased = target is not None

    def body(*refs):
        if aliased:
            x_ref, dy_ref, g_ref, f_ref, bd_ref, _, o_ref, dg_ref = refs
        else:
            x_ref, dy_ref, g_ref, f_ref, bd_ref, o_ref, dg_ref = refs
        i = pl.program_id(1)
        xv = x_ref[...].astype(F32)
        dyv = dy_ref[...]
        bdv = bd_ref[...]
        rstd = lax.rsqrt(_seg_sum(xv * xv, bdv) * (1.0 / hd) + EPS)
        xhat = xv * rstd
        g = dyv * g_ref[...]
        mean = _seg_sum(g * xhat, bdv) * (1.0 / hd)
        dx = jnp.where(f_ref[...] > 0.0, rstd * (g - xhat * mean), g)
        o_ref[...] = dx.astype(BF16)
        part = jnp.sum((dyv * xhat).reshape(bm // 8, 8, bw), axis=0)

        @pl.when(i == 0)
        def _():
            dg_ref[...] = part

        @pl.when(i > 0)
        def _():
            dg_ref[...] += part

    vec_spec = pl.BlockSpec((1, bw), lambda t, i: (0, t))
    in_specs = [pl.BlockSpec((bm, bw), lambda t, i: (i, cb0 + t)), pl.BlockSpec((bm, bw), lambda t, i: (i, t)),
                vec_spec, vec_spec, pl.BlockSpec((LANES, LANES), lambda t, i: (0, 0))]
    args = [src, dyn, gain, nflag, bd]
    aliases = {}
    if aliased:
        in_specs.append(pl.BlockSpec(memory_space=pl.ANY))
        args.append(target)
        aliases = {5: 0}
        o_shape = jax.ShapeDtypeStruct(target.shape, BF16)
    else:
        o_shape = jax.ShapeDtypeStruct((rows, width), BF16)
    out, dg = pl.pallas_call(
        body, name=name, grid=(width // bw, rows // bm), in_specs=in_specs,
        out_specs=[pl.BlockSpec((bm, bw), lambda t, i: (i, tb0 + t)), pl.BlockSpec((8, bw), lambda t, i: (0, t))],
        out_shape=[o_shape, jax.ShapeDtypeStruct((8, width), F32)],
        input_output_aliases=aliases,
        compiler_params=_params(("parallel", "arbitrary")),
    )(*args)
    return out, dg


def _fox_prep(pfb, bpad, name):
    s = pfb.shape[0]

    def body(p_ref, b_ref, c_ref):
        z = p_ref[...] + b_ref[...]
        logf = jnp.minimum(z, 0.0) - jnp.log(1.0 + jnp.exp(-jnp.abs(z)))
        x = logf.T[0:16, :]
        lane = lax.broadcasted_iota(jnp.int32, (16, s), 1)
        sh = 1
        while sh < s:
            x = x + jnp.where(lane >= sh, pltpu.roll(x, sh, 1), 0.0)
            sh *= 2
        c_ref[...] = x

    return pl.pallas_call(
        body, name=name, grid=(1,),
        in_specs=[pl.BlockSpec((s, FB_PAD), lambda i: (0, 0)), pl.BlockSpec((1, FB_PAD), lambda i: (0, 0))],
        out_specs=pl.BlockSpec((16, s), lambda i: (0, 0)),
        out_shape=jax.ShapeDtypeStruct((16, s), F32),
        compiler_params=_params(("arbitrary",)),
    )(pfb, bpad)


def _fox_prep_bwd(pfb, bpad, dct, name):
    s = pfb.shape[0]

    def body(p_ref, b_ref, dc_ref, df_ref, db_ref):
        zt = (p_ref[...] + b_ref[...]).T[0:16, :]
        y = dc_ref[...]
        lane = lax.broadcasted_iota(jnp.int32, (16, s), 1)
        sh = 1
        while sh < s:
            y = y + jnp.where(lane < s - sh, pltpu.roll(y, s - sh, 1), 0.0)
            sh *= 2
        dz = y * _sigmoid(-zt)
        db_ref[...] = jnp.broadcast_to(jnp.sum(dz, axis=1, keepdims=True), (16, FB_PAD))
        full = jnp.concatenate([dz, jnp.zeros((FB_PAD - 16, s), F32)], axis=0)
        df_ref[...] = full.T.astype(BF16)

    return pl.pallas_call(
        body, name=name, grid=(1,),
        in_specs=[pl.BlockSpec((s, FB_PAD), lambda i: (0, 0)), pl.BlockSpec((1, FB_PAD), lambda i: (0, 0)),
                  pl.BlockSpec((16, s), lambda i: (0, 0))],
        out_specs=[pl.BlockSpec((s, FB_PAD), lambda i: (0, 0)), pl.BlockSpec((16, FB_PAD), lambda i: (0, 0))],
        out_shape=[jax.ShapeDtypeStruct((s, FB_PAD), BF16), jax.ShapeDtypeStruct((16, FB_PAD), F32)],
        compiler_params=_params(("arbitrary",)),
    )(pfb, bpad, dct)


def _swa_window(n):
    ws = pl.multiple_of(jnp.maximum(n * WINDOW - WINDOW, 0), WINDOW)
    qi = lax.broadcasted_iota(jnp.int32, (WINDOW, 2 * WINDOW), 0)
    kj = lax.broadcasted_iota(jnp.int32, (WINDOW, 2 * WINDOW), 1)
    rel = qi + (n * WINDOW - ws) - kj
    valid = (rel >= 0) & (rel < WINDOW)
    return ws, valid, rel.astype(F32)


def _attn_a_fwd(q, k, v, sinks, slopes, name):
    s = q.shape[1]
    nb = s // WINDOW
    smem = pl.BlockSpec(memory_space=pltpu.SMEM)

    def body(sink_ref, slope_ref, q_ref, k_ref, v_ref, o_ref, lse_ref):
        n = pl.program_id(0)
        ws, valid, relf = _swa_window(n)
        outs = []
        for h in range(A_Q_HEADS):
            kvh = h // A_GROUP
            kw = k_ref[kvh, pl.ds(ws, 2 * WINDOW), :]
            vw = v_ref[kvh, pl.ds(ws, 2 * WINDOW), :]
            sc = lax.dot_general(q_ref[h], kw, (((1,), (1,)), ((), ())), preferred_element_type=F32)
            sc = jnp.where(valid, sc - slope_ref[h] * relf, NEG)
            sink = sink_ref[h]
            m = jnp.maximum(jnp.max(sc, axis=1, keepdims=True), sink)
            p = jnp.exp(sc - m)
            denom = jnp.sum(p, axis=1, keepdims=True) + jnp.exp(sink - m)
            pn = (p / denom).astype(BF16)
            outs.append(jnp.dot(pn, vw, preferred_element_type=F32))
            lse_ref[h] = jnp.broadcast_to(m + jnp.log(denom), (WINDOW, HEAD_DIM))
        o_ref[...] = jnp.concatenate(outs, axis=1)

    return pl.pallas_call(
        body, name=name, grid=(nb,),
        in_specs=[smem, smem,
                  pl.BlockSpec((A_Q_HEADS, WINDOW, HEAD_DIM), lambda n: (0, n, 0)),
                  pl.BlockSpec((A_KV_HEADS, s, HEAD_DIM), lambda n: (0, 0, 0)),
                  pl.BlockSpec((A_KV_HEADS, s, HEAD_DIM), lambda n: (0, 0, 0))],
        out_specs=[pl.BlockSpec((WINDOW, A_WIDTH), lambda n: (n, 0)),
                   pl.BlockSpec((A_Q_HEADS, WINDOW, HEAD_DIM), lambda n: (0, n, 0))],
        out_shape=[jax.ShapeDtypeStruct((s, A_WIDTH), F32), jax.ShapeDtypeStruct((A_Q_HEADS, s, HEAD_DIM), F32)],
        compiler_params=_params(("parallel",), VMEM_BIG),
    )(sinks, slopes, q, k, v)


def _attn_a_bwd(q, k, v, do, lse, dd, sinks, slopes, name, comm=None):
    s = q.shape[1]
    nb = s // WINDOW
    smem = pl.BlockSpec(memory_space=pltpu.SMEM)
    last = nb - 1

    def body(sink_ref, slope_ref, q_ref, k_ref, v_ref, do_ref, lse_ref, dd_ref, dq_ref, dkv_ref, ds_ref, carry):
        n = pl.program_id(0)

        @pl.when(n == 0)
        def _():
            carry[...] = jnp.zeros(carry.shape, F32)
            ds_ref[...] = jnp.zeros(ds_ref.shape, F32)

        @pl.when(n < nb)
        def _():
            ws, valid, relf = _swa_window(n)
            dqs = []
            dkw = [None] * A_KV_HEADS
            dvw = [None] * A_KV_HEADS
            for h in range(A_Q_HEADS):
                kvh = h // A_GROUP
                qh = q_ref[h]
                doh = do_ref[h]
                kw = k_ref[kvh, pl.ds(ws, 2 * WINDOW), :]
                vw = v_ref[kvh, pl.ds(ws, 2 * WINDOW), :]
                lse_h = lse_ref[h]
                dd_h = dd_ref[h]
                sc = lax.dot_general(qh, kw, (((1,), (1,)), ((), ())), preferred_element_type=F32)
                sc = jnp.where(valid, sc - slope_ref[h] * relf, NEG)
                p = jnp.exp(sc - lse_h[:, 0:1])
                dp = lax.dot_general(doh, vw, (((1,), (1,)), ((), ())), preferred_element_type=F32)
                dsc = (p * (dp - dd_h[:, 0:1])).astype(BF16)
                pb = p.astype(BF16)
                dqs.append(jnp.dot(dsc, kw, preferred_element_type=F32))
                dk_h = lax.dot_general(dsc, qh, (((0,), (0,)), ((), ())), preferred_element_type=F32)
                dv_h = lax.dot_general(pb, doh, (((0,), (0,)), ((), ())), preferred_element_type=F32)
                dkw[kvh] = dk_h if dkw[kvh] is None else dkw[kvh] + dk_h
                dvw[kvh] = dv_h if dvw[kvh] is None else dvw[kvh] + dv_h
                psink = jnp.exp(sink_ref[h] - lse_h)
                ds_ref[h] += jnp.sum((-psink * dd_h).reshape(WINDOW // 8, 8, HEAD_DIM), axis=0)
            dq_ref[...] = jnp.concatenate(dqs, axis=1)
            win = jnp.concatenate(dkw + dvw, axis=1)
            first = win[0:WINDOW]
            second = win[WINDOW:2 * WINDOW]
            dkv_ref[...] = carry[...] + first
            carry[...] = jnp.where(n == 0, first, second)

        @pl.when(n == nb)
        def _():
            dkv_ref[...] = carry[...]

    hm = lambda heads: pl.BlockSpec((heads, WINDOW, HEAD_DIM), lambda n: (0, jnp.minimum(n, last), 0))
    res = lambda heads: pl.BlockSpec((heads, s, HEAD_DIM), lambda n: (0, 0, 0))
    outs, comm_outs = _hosted_call(
        body, comm, name=name, grid=(nb + 1,),
        in_specs=[smem, smem, hm(A_Q_HEADS), res(A_KV_HEADS), res(A_KV_HEADS), hm(A_Q_HEADS), hm(A_Q_HEADS), hm(A_Q_HEADS)],
        out_specs=[pl.BlockSpec((WINDOW, A_WIDTH), lambda n: (jnp.minimum(n, last), 0)),
                   pl.BlockSpec((WINDOW, 2 * A_KV_WIDTH), lambda n: (jnp.maximum(n - 1, 0), 0)),
                   pl.BlockSpec((A_Q_HEADS, 8, HEAD_DIM), lambda n: (0, 0, 0))],
        out_shape=[jax.ShapeDtypeStruct((s, A_WIDTH), F32), jax.ShapeDtypeStruct((s, 2 * A_KV_WIDTH), F32),
                   jax.ShapeDtypeStruct((A_Q_HEADS, 8, HEAD_DIM), F32)],
        scratch_shapes=[pltpu.VMEM((WINDOW, 2 * A_KV_WIDTH), F32)],
        args=[sinks, slopes, q, k, v, do, lse, dd], sem=("arbitrary",), vmem=VMEM_BIG)
    return outs[0], outs[1], outs[2], comm_outs


def _attn_b_fwd(q, k, v, c3, name, comm=None):
    heads, s, _ = q.shape
    bq = min(512, s)
    nq = s // bq
    nt = (((1,), (1,)), ((), ()))

    def body(q_ref, k_ref, v_ref, c_ref, o_ref, lse_ref, m_scr, l_scr, acc_scr):
        i = pl.program_id(1)
        r0 = pl.multiple_of(i * bq, bq)
        row = lax.broadcasted_iota(jnp.int32, (bq, bq), 0)
        col = lax.broadcasted_iota(jnp.int32, (bq, bq), 1)
        m_scr[...] = jnp.full((2, bq, LANES), NEG, F32)
        l_scr[...] = jnp.zeros((2, bq, LANES), F32)
        acc_scr[...] = jnp.zeros((2, bq, HEAD_DIM), F32)

        def step(j, masked):
            k0 = pl.multiple_of(j * bq, bq)
            for h2 in range(2):
                kv = k_ref[h2, pl.ds(k0, bq), :]
                vv = v_ref[h2, pl.ds(k0, bq), :]
                cq0 = c_ref[h2, :, pl.ds(r0, LANES)][:, 0:1]
                sc = lax.dot_general(q_ref[h2], kv, nt, preferred_element_type=F32)
                sc = sc + (cq0 - c_ref[h2, :, pl.ds(k0, bq)])
                if masked:
                    sc = jnp.where(col <= row, sc, NEG)
                m_prev = m_scr[h2]
                m_new = jnp.maximum(m_prev, jnp.max(sc, axis=1, keepdims=True))
                alpha = jnp.exp(m_prev - m_new)
                p = jnp.exp(sc - m_new[:, 0:1])
                l_scr[h2] = alpha * l_scr[h2] + jnp.sum(p, axis=1, keepdims=True)
                p_hi = p.astype(BF16)
                p_lo = (p - p_hi.astype(F32)).astype(BF16)
                pv = jnp.dot(p_hi, vv, preferred_element_type=F32) + jnp.dot(p_lo, vv, preferred_element_type=F32)
                acc_scr[h2] = acc_scr[h2] * alpha[:, 0:HEAD_DIM] + pv
                m_scr[h2] = m_new

        def loop_body(j, carry):
            step(j, False)
            return carry

        lax.fori_loop(0, i, loop_body, 0)
        step(i, True)
        outs = []
        for h2 in range(2):
            l = l_scr[h2]
            outs.append(acc_scr[h2] / l[:, 0:HEAD_DIM])
            lse_ref[h2] = (m_scr[h2] + jnp.log(l))[:, 0:HEAD_DIM]
        o_ref[...] = jnp.concatenate(outs, axis=1)

    res = pl.BlockSpec((2, s, HEAD_DIM), lambda hp, i: (hp, 0, 0))
    outs, comm_outs = _hosted_call(
        body, comm, name=name, grid=(heads // 2, nq),
        in_specs=[pl.BlockSpec((2, bq, HEAD_DIM), lambda hp, i: (hp, i, 0)), res, res,
                  pl.BlockSpec((2, 1, s), lambda hp, i: (hp, 0, 0))],
        out_specs=[pl.BlockSpec((bq, 2 * HEAD_DIM), lambda hp, i: (i, hp)),
                   pl.BlockSpec((2, bq, HEAD_DIM), lambda hp, i: (hp, i, 0))],
        out_shape=[jax.ShapeDtypeStruct((s, heads * HEAD_DIM), F32), jax.ShapeDtypeStruct((heads, s, HEAD_DIM), F32)],
        scratch_shapes=[pltpu.VMEM((2, bq, LANES), F32), pltpu.VMEM((2, bq, LANES), F32), pltpu.VMEM((2, bq, HEAD_DIM), F32)],
        args=[q, k, v, c3], sem=("parallel", "parallel"), vmem=VMEM_BIG)
    return outs[0], outs[1], comm_outs


def _attn_b_bwd(q, k, v, do, lse, dd, c3, name, comm=None):
    heads, s, _ = q.shape
    bq = min(512, s)
    nq = s // bq
    nt = (((1,), (1,)), ((), ()))
    tn = (((0,), (0,)), ((), ()))
    grid = (heads // 2, nq)

    def body(q_ref, k_ref, v_ref, do_ref, lse_ref, dd_ref, c_ref, dq_ref, dk_ref, dv_ref, dc_ref,
             dq_scr, dk_scr, dv_scr, dc_scr):
        j = pl.program_id(1)
        k0 = pl.multiple_of(j * bq, bq)
        row = lax.broadcasted_iota(jnp.int32, (bq, bq), 0)
        col = lax.broadcasted_iota(jnp.int32, (bq, bq), 1)

        @pl.when(j == 0)
        def _():
            dq_scr[...] = jnp.zeros(dq_scr.shape, F32)

        dk_scr[...] = jnp.zeros((2, bq, HEAD_DIM), F32)
        dv_scr[...] = jnp.zeros((2, bq, HEAD_DIM), F32)
        dc_scr[...] = jnp.zeros((2, 1, bq), F32)

        def step(i, masked):
            r0 = pl.multiple_of(i * bq, bq)
            for h2 in range(2):
                kv = k_ref[h2]
                vv = v_ref[h2]
                qv = q_ref[h2, pl.ds(r0, bq), :]
                dov = do_ref[h2, pl.ds(r0, bq), :]
                lse_v = lse_ref[h2, pl.ds(r0, bq), :][:, 0:1]
                dd_v = dd_ref[h2, pl.ds(r0, bq), :][:, 0:1]
                cq0 = c_ref[h2, :, pl.ds(r0, LANES)][:, 0:1]
                sc = lax.dot_general(qv, kv, nt, preferred_element_type=F32) + (cq0 - c_ref[h2, :, pl.ds(k0, bq)])
                if masked:
                    sc = jnp.where(col <= row, sc, NEG)
                p = jnp.exp(sc - lse_v)
                dp = lax.dot_general(dov, vv, nt, preferred_element_type=F32)
                dsc = p * (dp - dd_v)
                dsb = dsc.astype(BF16)
                dv_scr[h2] += lax.dot_general(p.astype(BF16), dov, tn, preferred_element_type=F32)
                dk_scr[h2] += lax.dot_general(dsb, qv, tn, preferred_element_type=F32)
                dq_scr[h2, pl.ds(r0, bq), :] += jnp.dot(dsb, kv, preferred_element_type=F32)
                dc_scr[h2] -= jnp.sum(dsc, axis=0, keepdims=True)

        def loop_body(i, carry):
            step(i, False)
            return carry

        step(j, True)
        lax.fori_loop(j + 1, nq, loop_body, 0)
        dc_ref[...] = dc_scr[...]
        dk_ref[...] = jnp.concatenate([dk_scr[0], dk_scr[1]], axis=1)
        dv_ref[...] = jnp.concatenate([dv_scr[0], dv_scr[1]], axis=1)

        @pl.when(j == nq - 1)
        def _():
            dq_ref[...] = jnp.concatenate([dq_scr[0], dq_scr[1]], axis=1)

    res = pl.BlockSpec((2, s, HEAD_DIM), lambda hp, j: (hp, 0, 0))
    blk = pl.BlockSpec((2, bq, HEAD_DIM), lambda hp, j: (hp, j, 0))
    tm = jax.ShapeDtypeStruct((s, heads * HEAD_DIM), F32)
    in_specs = [res, blk, blk, res, res, res, pl.BlockSpec((2, 1, s), lambda hp, j: (hp, 0, 0))]
    out_specs = [pl.BlockSpec((s, 2 * HEAD_DIM), lambda hp, j: (0, hp)),
                 pl.BlockSpec((bq, 2 * HEAD_DIM), lambda hp, j: (j, hp)),
                 pl.BlockSpec((bq, 2 * HEAD_DIM), lambda hp, j: (j, hp)),
                 pl.BlockSpec((2, 1, bq), lambda hp, j: (hp, 0, j))]
    out_shape = [tm, tm, tm, jax.ShapeDtypeStruct((heads, 1, s), F32)]
    scratch = [pltpu.VMEM((2, s, HEAD_DIM), F32), pltpu.VMEM((2, bq, HEAD_DIM), F32),
               pltpu.VMEM((2, bq, HEAD_DIM), F32), pltpu.VMEM((2, 1, bq), F32)]
    outs, comm_outs = _hosted_call(
        body, comm, name=name, grid=grid, in_specs=in_specs, out_specs=out_specs, out_shape=out_shape,
        scratch_shapes=scratch, args=[q, k, v, do, lse, dd, c3], sem=("parallel", "arbitrary"), vmem=VMEM_BIG)
    return outs[0], outs[1], outs[2], outs[3], comm_outs


def _attn_c_probs(qh, mkh):
    sc = lax.dot_general(qh, mkh, (((1,), (1,)), ((), ())), preferred_element_type=F32) * (C_HEAD_DIM ** -0.5)
    p = jnp.exp(sc - jnp.max(sc, axis=1, keepdims=True))
    return p / jnp.sum(p, axis=1, keepdims=True)


def _attn_c_fwd(q, mkv, name):
    s = q.shape[0]
    m = mkv.shape[0]
    bq = _tile(s, 512, 8)

    def body(q_ref, mk_ref, mv_ref, o_ref):
        outs = []
        for h in range(C_HEADS):
            sl = slice(h * C_HEAD_DIM, (h + 1) * C_HEAD_DIM)
            pn = _attn_c_probs(q_ref[:, sl], mk_ref[:, sl]).astype(BF16)
            outs.append(jnp.dot(pn, mv_ref[:, sl], preferred_element_type=F32))
        o_ref[...] = jnp.concatenate(outs, axis=1)

    return pl.pallas_call(
        body, name=name, grid=(s // bq,),
        in_specs=[pl.BlockSpec((bq, C_WIDTH), lambda i: (i, 0)), pl.BlockSpec((m, C_WIDTH), lambda i: (0, 0)),
                  pl.BlockSpec((m, C_WIDTH), lambda i: (0, 1))],
        out_specs=pl.BlockSpec((bq, C_WIDTH), lambda i: (i, 0)),
        out_shape=jax.ShapeDtypeStruct((s, C_WIDTH), F32),
        compiler_params=_params(("parallel",)),
    )(q, mkv, mkv)


def _attn_c_bwd(q, mkv, do, name):
    s = q.shape[0]
    m = mkv.shape[0]
    bq = _tile(s, 512, 8)
    tn = (((0,), (0,)), ((), ()))

    def body(q_ref, mk_ref, mv_ref, do_ref, dq_ref, dm_ref):
        i = pl.program_id(0)

        @pl.when(i == 0)
        def _():
            dm_ref[...] = jnp.zeros(dm_ref.shape, F32)

        dqs = []
        for h in range(C_HEADS):
            sl = slice(h * C_HEAD_DIM, (h + 1) * C_HEAD_DIM)
            qh, mkh, mvh, doh = q_ref[:, sl], mk_ref[:, sl], mv_ref[:, sl], do_ref[:, sl]
            pn = _attn_c_probs(qh, mkh)
            dp = lax.dot_general(doh, mvh, (((1,), (1,)), ((), ())), preferred_element_type=F32)
            dsc = (pn * (dp - jnp.sum(pn * dp, axis=1, keepdims=True)) * (C_HEAD_DIM ** -0.5)).astype(BF16)
            dqs.append(jnp.dot(dsc, mkh, preferred_element_type=F32))
            dm_ref[:, sl] += lax.dot_general(dsc, qh, tn, preferred_element_type=F32)
            sv = slice(C_WIDTH + h * C_HEAD_DIM, C_WIDTH + (h + 1) * C_HEAD_DIM)
            dm_ref[:, sv] += lax.dot_general(pn.astype(BF16), doh, tn, preferred_element_type=F32)
        dq_ref[...] = jnp.concatenate(dqs, axis=1)

    row = pl.BlockSpec((bq, C_WIDTH), lambda i: (i, 0))
    return pl.pallas_call(
        body, name=name, grid=(s // bq,),
        in_specs=[row, pl.BlockSpec((m, C_WIDTH), lambda i: (0, 0)), pl.BlockSpec((m, C_WIDTH), lambda i: (0, 1)), row],
        out_specs=[row, pl.BlockSpec((m, 2 * C_WIDTH), lambda i: (0, 0))],
        out_shape=[jax.ShapeDtypeStruct((s, C_WIDTH), F32), jax.ShapeDtypeStruct((m, 2 * C_WIDTH), F32)],
        compiler_params=_params(("arbitrary",)),
    )(q, mkv, mkv, do)


def _gate_fwd(y, proj, zc0, bw, name):
    rows, width = y.shape
    bm = _tile(rows, 2048 if bw <= 256 else 1024, 16)
    cb0 = zc0 // bw

    def body(y_ref, z_ref, o_ref):
        z = z_ref[...].astype(F32)
        o_ref[...] = (y_ref[...] * (z * _sigmoid(z))).astype(BF16)

    return pl.pallas_call(
        body, name=name, grid=(rows // bm, width // bw),
        in_specs=[pl.BlockSpec((bm, bw), lambda i, t: (i, t)), pl.BlockSpec((bm, bw), lambda i, t: (i, cb0 + t))],
        out_specs=pl.BlockSpec((bm, bw), lambda i, t: (i, t)),
        out_shape=jax.ShapeDtypeStruct((rows, width), BF16),
        compiler_params=_params(("parallel", "parallel")),
    )(y, proj)


def _gate_bwd(dsv, y, proj, zc0, bw, dproj, t0, head_major, name):
    rows, width = y.shape
    bm = _tile(rows, 2048 if bw <= 256 else 1024, 16)
    cb0 = zc0 // bw
    tb0 = t0 // bw
    bd = _block_diag(HEAD_DIM)
    hpb = bw // HEAD_DIM

    def body(*refs):
        if head_major:
            ds_ref, y_ref, z_ref, bd_ref, _, dp_ref, dy_ref, dd_ref = refs
        else:
            ds_ref, y_ref, z_ref, _, dp_ref, dy_ref = refs
        z = z_ref[...].astype(F32)
        sig = _sigmoid(z)
        dsx = ds_ref[...]
        yv = y_ref[...]
        dy = dsx * (z * sig)
        dp_ref[...] = (dsx * yv * (sig * (1.0 + z * (1.0 - sig)))).astype(BF16)
        if head_major:
            dyb = dy.astype(BF16)
            dd = _seg_sum(dyb.astype(F32) * yv, bd_ref[...])
            for h in range(hpb):
                sl = slice(h * HEAD_DIM, (h + 1) * HEAD_DIM)
                dy_ref[h] = dyb[:, sl]
                dd_ref[h] = dd[:, sl]
        else:
            dy_ref[...] = dy.astype(BF16)

    tile = pl.BlockSpec((bm, bw), lambda i, t: (i, t))
    ztile = pl.BlockSpec((bm, bw), lambda i, t: (i, cb0 + t))
    ttile = pl.BlockSpec((bm, bw), lambda i, t: (i, tb0 + t))
    any_spec = pl.BlockSpec(memory_space=pl.ANY)
    dp_shape = jax.ShapeDtypeStruct(dproj.shape, BF16)
    if head_major:
        hm_spec = pl.BlockSpec((hpb, bm, HEAD_DIM), lambda i, t: (t, i, 0))
        nh = width // HEAD_DIM
        outs = pl.pallas_call(
            body, name=name, grid=(rows // bm, width // bw),
            in_specs=[tile, tile, ztile, pl.BlockSpec((LANES, LANES), lambda i, t: (0, 0)), any_spec],
            out_specs=[ttile, hm_spec, hm_spec],
            out_shape=[dp_shape, jax.ShapeDtypeStruct((nh, rows, HEAD_DIM), BF16),
                       jax.ShapeDtypeStruct((nh, rows, HEAD_DIM), F32)],
            input_output_aliases={4: 0},
            compiler_params=_params(("parallel", "parallel")),
        )(dsv, y, proj, bd, dproj)
        return outs[0], outs[1], outs[2]
    outs = pl.pallas_call(
        body, name=name, grid=(rows // bm, width // bw),
        in_specs=[tile, tile, ztile, any_spec],
        out_specs=[ttile, tile],
        out_shape=[dp_shape, jax.ShapeDtypeStruct((rows, width), BF16)],
        input_output_aliases={3: 0},
        compiler_params=_params(("parallel", "parallel")),
    )(dsv, y, proj, dproj)
    return outs[0], outs[1], None


def _merge_fwd(proj, ua, ub, uc, name):
    rows, d = ua.shape
    bm = _tile(rows, 1024, 16)
    bw = _tile(d, 512)
    g0 = COL_GATE // bw
    gstep = d // bw

    def body(la_ref, lb_ref, lc_ref, ua_ref, ub_ref, uc_ref, o_ref, ga_ref, gb_ref, gc_ref):
        y = None
        for l_ref, u_ref, g_ref in ((la_ref, ua_ref, ga_ref), (lb_ref, ub_ref, gb_ref), (lc_ref, uc_ref, gc_ref)):
            g = _sigmoid(l_ref[...].astype(F32))
            g_ref[...] = g.astype(BF16)
            term = g * u_ref[...].astype(F32)
            y = term if y is None else y + term
        o_ref[...] = y.astype(BF16)

    tile = pl.BlockSpec((bm, bw), lambda i, t: (i, t))
    gate = lambda b: pl.BlockSpec((bm, bw), lambda i, t: (i, g0 + b * gstep + t))
    shape = jax.ShapeDtypeStruct((rows, d), BF16)
    return pl.pallas_call(
        body, name=name, grid=(rows // bm, d // bw),
        in_specs=[gate(0), gate(1), gate(2), tile, tile, tile],
        out_specs=[tile] * 4, out_shape=[shape] * 4,
        compiler_params=_params(("parallel", "parallel")),
    )(proj, proj, proj, ua, ub, uc)


def _merge_bwd(dym, us, gs, name):
    rows, d = dym.shape
    bm = _tile(rows, 1024, 16)
    bw = _tile(d, 512)
    nb = d // bw

    def body(dy_ref, ua_ref, ub_ref, uc_ref, ga_ref, gb_ref, gc_ref, dg_ref, da_ref, db_ref, dc_ref):
        b = pl.program_id(2)
        dyv = dy_ref[...]
        for idx, (u_ref, g_ref, du_ref) in enumerate(((ua_ref, ga_ref, da_ref), (ub_ref, gb_ref, db_ref), (uc_ref, gc_ref, dc_ref))):
            @pl.when(b == idx)
            def _():
                g = g_ref[...].astype(F32)
                du_ref[...] = (g * dyv).astype(BF16)
                dg_ref[...] = (dyv * u_ref[...].astype(F32) * g * (1.0 - g)).astype(BF16)

    tile = pl.BlockSpec((bm, bw), lambda i, t, b: (i, t))
    shape = jax.ShapeDtypeStruct((rows, d), BF16)
    outs = pl.pallas_call(
        body, name=name, grid=(rows // bm, nb, 3),
        in_specs=[tile] * 7,
        out_specs=[pl.BlockSpec((bm, bw), lambda i, t, b: (i, b * nb + t)), tile, tile, tile],
        out_shape=[jax.ShapeDtypeStruct((rows, 3 * d), BF16), shape, shape, shape],
        compiler_params=_params(("parallel", "parallel", "arbitrary")),
    )(dym, *us, *gs)
    return outs[0], outs[1], outs[2], outs[3]


def _out_proj_loss(ym, wo, x, target, name):
    m, d = x.shape
    bm, bn = _tile(m, 1024, 16), _tile(d, 1024)
    grid = (m // bm, d // bn)

    def body(a_ref, b_ref, x_ref, t_ref, dy_ref, dyb_ref, l_ref):
        first, _ = _grid_edges(grid)
        y = jnp.dot(a_ref[...], b_ref[...], preferred_element_type=F32) + x_ref[...]
        diff = y - t_ref[...]
        dy = diff * (1.0 / d)
        dy_ref[...] = dy
        dyb_ref[...] = dy.astype(BF16)
        sq = diff * diff
        part = sq[:, 0:LANES]
        for c in range(1, bn // LANES):
            part = part + sq[:, c * LANES:(c + 1) * LANES]
        part = jnp.sum(part.reshape(bm // 8, 8, LANES), axis=0)

        @pl.when(first)
        def _():
            l_ref[...] = part

        @pl.when(jnp.logical_not(first))
        def _():
            l_ref[...] += part

    tile = pl.BlockSpec((bm, bn), lambda i, j: (i, j))
    return pl.pallas_call(
        body, name=name, grid=grid,
        in_specs=[pl.BlockSpec((bm, d), lambda i, j: (i, 0)), pl.BlockSpec((d, bn), lambda i, j: (0, j)), tile, tile],
        out_specs=[tile, tile, pl.BlockSpec((8, LANES), lambda i, j: (0, 0))],
        out_shape=[jax.ShapeDtypeStruct((m, d), F32), jax.ShapeDtypeStruct((m, d), BF16),
                   jax.ShapeDtypeStruct((8, LANES), F32)],
        compiler_params=_params(("arbitrary", "arbitrary"), VMEM_BIG),
    )(ym, wo, x, target)


def _row(vec, reps=1):
    return jnp.tile(vec.reshape(1, -1).astype(F32), (1, reps))


def _local_step(x, mem, target, small, wg, shards=None):
    s, d = x.shape
    dist = shards is not None
    wg = dict(wg)
    ones = lambda n: jnp.ones((1, n), F32)
    zeros = lambda n: jnp.zeros((1, n), F32)
    scale_ab = HEAD_DIM ** -0.5
    split8 = lambda g: g.reshape(N_DEV, g.shape[0] // N_DEV, g.shape[1])
    flat8 = lambda g: g.reshape(g.shape[0] * g.shape[1], g.shape[2])
    gather = lambda names: _Comm("gather", [shards[n] for n in names]) if dist else None
    g = {}

    def scatter(names):
        return _Comm("scatter", [split8(g[n]) for n in names]) if dist else None

    def hosted(result, names, store):
        if not dist:
            return result
        out, got = result
        store.update(zip(names, got))
        return out

    hn = _rmsnorm_fwd(x, small["norm_gain"], "rms_x_fwd")
    got = {}
    proj = hosted(_mm_nn(hn, wg["qkv"], bm=1024, bn=1024, bk=d, o_dtype=BF16, name="proj_qkv",
                         comm=gather(("wa", "wb", "wc"))), ("wa", "wb", "wc"), got)
    wg.update({n: flat8(a) for n, a in got.items()})
    pfb = _mm_nn(hn, wg["wf"], bm=1024, bn=FB_PAD, bk=d, o_dtype=F32, name="proj_fb")
    mn = _rmsnorm_fwd(mem, small["mem_norm_gain"], "rms_mem_fwd")
    mkv = _mm_nn(mn, wg["wk"], bm=256, bn=1024, bk=d, o_dtype=F32, name="mem_kv")

    gain_a = jnp.concatenate([_row(small["q_gain_a"], A_Q_HEADS) * scale_ab, _row(small["k_gain_a"], A_KV_HEADS), ones(A_KV_WIDTH)], axis=1)
    flag_a = jnp.concatenate([ones(A_WIDTH + A_KV_WIDTH), zeros(A_KV_WIDTH)], axis=1)
    qkv_a = _headnorm_fwd(proj, COL_QA, 1280, 1280, HEAD_DIM, gain_a, flag_a, True, "hn_a_fwd")
    gain_b = jnp.concatenate([_row(small["q_gain_b"], B_HEADS) * scale_ab, _row(small["k_gain_b"], B_HEADS), ones(B_WIDTH)], axis=1)
    flag_b = jnp.concatenate([ones(2 * B_WIDTH), zeros(B_WIDTH)], axis=1)
    qkv_b = _headnorm_fwd(proj, COL_QB, 2304, 256, HEAD_DIM, gain_b, flag_b, True, "hn_b_fwd")
    gain_cq = _row(small["q_gain_c"], C_HEADS)
    q_c = _headnorm_fwd(proj, COL_QC, C_WIDTH, C_WIDTH, C_HEAD_DIM, gain_cq, ones(C_WIDTH), False, "hn_cq_fwd")
    gain_ck = jnp.concatenate([_row(small["k_gain_c"], C_HEADS), ones(C_WIDTH)], axis=1)
    flag_ck = jnp.concatenate([ones(C_WIDTH), zeros(C_WIDTH)], axis=1)
    mkvn = _headnorm_fwd(mkv, 0, 2 * C_WIDTH, 2 * C_WIDTH, C_HEAD_DIM, gain_ck, flag_ck, False, "hn_ck_fwd")

    q_a, k_a, v_a = qkv_a[0:12], qkv_a[12:16], qkv_a[16:20]
    q_b, k_b, v_b = qkv_b[0:12], qkv_b[12:24], qkv_b[24:36]

    bpad = jnp.pad(small["b_forget"].reshape(1, -1), ((0, 0), (0, FB_PAD - B_HEADS)))
    c16 = _fox_prep(pfb, bpad, "fox_prep")
    c3 = c16[0:B_HEADS].reshape(B_HEADS, 1, s)

    sinks = small["sinks_a"].reshape(-1)
    slopes = jnp.exp2(-8.0 * jnp.arange(1, A_Q_HEADS + 1, dtype=F32) / A_Q_HEADS)
    y_a, lse_a = _attn_a_fwd(q_a, k_a, v_a, sinks, slopes, "attn_a_fwd")
    y_b, lse_b, got_zg = _attn_b_fwd(q_b, k_b, v_b, c3, "attn_b_fwd", comm=gather(("zg",)))
    if dist:
        wg["zg"] = flat8(got_zg[0])
    y_c = _attn_c_fwd(q_c, mkvn, "attn_c_fwd")

    got = {}
    pzg = hosted(_mm_nn(hn, wg["zg"], bm=1024, bn=1024, bk=d, o_dtype=BF16, name="proj_zg", comm=gather(("wo",))),
                 ("wo",), got)
    wg.update({n: flat8(a) for n, a in got.items()})

    s_a = _gate_fwd(y_a, pzg, COL_ZA, 256, "gate_a_fwd")
    s_b = _gate_fwd(y_b, pzg, COL_ZB, 256, "gate_b_fwd")
    s_c = _gate_fwd(y_c, pzg, COL_ZC, 512, "gate_c_fwd")
    w_a, w_b, w_c = _branch_full(wg["wa"]), _branch_full(wg["wb"]), _branch_full(wg["wc"])
    u_a = _mm_nn(s_a, w_a, bm=1024, bn=2048, bk=A_WIDTH, o_dtype=BF16, name="branch_a_fwd")
    u_b = _mm_nn(s_b, w_b, bm=1024, bn=2048, bk=B_WIDTH, o_dtype=BF16, name="branch_b_fwd")
    u_c = _mm_nn(s_c, w_c, bm=1024, bn=2048, bk=C_WIDTH, o_dtype=BF16, name="branch_c_fwd")
    ym, gate_a, gate_b, gate_c = _merge_fwd(pzg, u_a, u_b, u_c, "merge_fwd")
    dy, dyb, lpart = _out_proj_loss(ym, wg["wo"], x, target, "out_proj_loss")
    loss = 0.5 / d * jnp.sum(lpart)

    dym = _mm_nt(dyb, wg["wo"], bm=1024, bn=1024, bk=d, o_dtype=F32, name="out_proj_bwd_act")
    g["wo"] = _mm_tn(ym, dyb, bm=512, bn=1024, bk=s, o_dtype=BF16, name="out_proj_bwd_w")

    dgate, du_a, du_b, du_c = _merge_bwd(dym, (u_a, u_b, u_c), (gate_a, gate_b, gate_c), "merge_bwd")
    parts = {}
    g["wm_g"] = hosted(_mm_tn(hn, dgate, bm=512, bn=1024, bk=s, o_dtype=BF16, name="proj_gate_bwd_w",
                              comm=scatter(("wo",))), ("wo",), parts)

    ds_a = _mm_nt(du_a, w_a, bm=1024, bn=A_WIDTH, bk=d, o_dtype=F32, name="branch_a_bwd_act")
    ds_b = _mm_nt(du_b, w_b, bm=1024, bn=B_WIDTH, bk=d, o_dtype=F32, name="branch_b_bwd_act")
    ds_c = _mm_nt(du_c, w_c, bm=1024, bn=C_WIDTH, bk=d, o_dtype=F32, name="branch_c_bwd_act")
    g["wa"] = _branch_shards(_mm_tn(s_a, du_a, bm=A_WIDTH, bn=1024, bk=s, o_dtype=BF16, name="branch_a_bwd_w"))
    g["wb"] = _branch_shards(_mm_tn(s_b, du_b, bm=B_WIDTH, bn=1024, bk=s, o_dtype=BF16, name="branch_b_bwd_w"))
    g["wc"] = _branch_shards(_mm_tn(s_c, du_c, bm=C_WIDTH, bn=1024, bk=s, o_dtype=BF16, name="branch_c_bwd_w"))

    dz = lax.empty((s, W_Z), BF16)
    dz, do_a, dd_a = _gate_bwd(ds_a, y_a, pzg, COL_ZA, 256, dz, COL_ZA, True, "gate_a_bwd")
    dz, do_b, dd_b = _gate_bwd(ds_b, y_b, pzg, COL_ZB, 256, dz, COL_ZB, True, "gate_b_bwd")
    dz, do_c, _ = _gate_bwd(ds_c, y_c, pzg, COL_ZC, 512, dz, COL_ZC, False, "gate_c_bwd")
    g["wm_z"] = _mm_tn(hn, dz, bm=512, bn=1024, bk=s, o_dtype=BF16, name="proj_z_bwd_w")

    names = ("wa", "wb", "wc")
    dq_a, dkv_a, dsink, got = _attn_a_bwd(q_a, k_a, v_a, do_a, lse_a, dd_a, sinks, slopes, "attn_a_bwd", comm=scatter(names))
    parts.update(zip(names, got))
    names = ("wm_g", "wm_z")
    dq_b, dk_b, dv_b, dc3, got = _attn_b_bwd(q_b, k_b, v_b, do_b, lse_b, dd_b, c3, "attn_b_bwd", comm=scatter(names))
    parts.update(zip(names, got))
    dq_c, dmkvn = _attn_c_bwd(q_c, mkvn, do_c, "attn_c_bwd")

    dqkv = lax.empty((s, W_QKV), BF16)
    dqkv, dg_qa = _headnorm_bwd(proj, COL_QA, A_WIDTH, 256, HEAD_DIM, gain_a[:, 0:768], flag_a[:, 0:768], dq_a, dqkv, COL_QA, "hn_qa_bwd")
    dqkv, dg_kva = _headnorm_bwd(proj, COL_KA, 512, 256, HEAD_DIM, gain_a[:, 768:1280], flag_a[:, 768:1280], dkv_a, dqkv, COL_KA, "hn_kva_bwd")
    dqkv, dg_qb = _headnorm_bwd(proj, COL_QB, B_WIDTH, 256, HEAD_DIM, gain_b[:, 0:768], flag_b[:, 0:768], dq_b, dqkv, COL_QB, "hn_qb_bwd")
    dqkv, dg_kb = _headnorm_bwd(proj, COL_KB, B_WIDTH, 256, HEAD_DIM, gain_b[:, 768:1536], flag_b[:, 768:1536], dk_b, dqkv, COL_KB, "hn_kb_bwd")
    dqkv, _ = _headnorm_bwd(proj, COL_VB, B_WIDTH, 256, HEAD_DIM, gain_b[:, 1536:2304], flag_b[:, 1536:2304], dv_b, dqkv, COL_VB, "hn_vb_bwd")
    dqkv, dg_qc = _headnorm_bwd(proj, COL_QC, C_WIDTH, 512, C_HEAD_DIM, gain_cq, ones(C_WIDTH), dq_c, dqkv, COL_QC, "hn_qc_bwd")
    dmkv, dg_kc = _headnorm_bwd(mkv, 0, 2 * C_WIDTH, 2 * C_WIDTH, C_HEAD_DIM, gain_ck, flag_ck, dmkvn, None, 0, "hn_kc_bwd")

    dct = jnp.pad(dc3.reshape(B_HEADS, s), ((0, 16 - B_HEADS), (0, 0)))
    dfb, dbf = _fox_prep_bwd(pfb, bpad, dct, "fox_prep_bwd")

    dmn = _mm_nt(dmkv, wg["wk"], bm=256, bn=1024, bk=1024, o_dtype=F32, name="mem_kv_bwd_act")
    g["wk"] = _mm_tn(mn, dmkv, bm=512, bn=1024, bk=mem.shape[0], o_dtype=BF16, name="mem_kv_bwd_w")
    _, dg_mem = _rmsnorm_bwd(mem, dmn, small["mem_norm_gain"], None, "rms_mem_bwd")

    g["wm_qkv"] = _mm_tn(hn, dqkv, bm=512, bn=1024, bk=s, o_dtype=BF16, name="proj_qkv_bwd_w")
    g["wf"] = _mm_tn(hn, dfb, bm=512, bn=FB_PAD, bk=s, o_dtype=BF16, name="proj_fb_bwd_w")
    names = ("wm_qkv", "wf", "wk")
    terms = [(dqkv, wg["qkv"], 0), (dz, wg["zg"], COL_ZA), (dgate, wg["zg"], COL_GATE), (dfb, wg["wf"], 0)]
    dhn = hosted(_mm_nt_sum(terms, bm=1024, bn=1024, bk=1024, name="proj_bwd_act", comm=scatter(names)), names, parts)
    if dist:
        g = parts
    grad_x, dg_x = _rmsnorm_bwd(x, dhn, small["norm_gain"], dy, "rms_x_bwd")

    fold = lambda part, heads, hd: jnp.sum(jnp.sum(part, axis=0).reshape(heads, hd), axis=0).reshape(1, hd)
    small_grads = {
        "norm_gain": jnp.sum(dg_x, axis=0).reshape(1, d),
        "mem_norm_gain": jnp.sum(dg_mem, axis=0).reshape(1, d),
        "b_forget": dbf[0:B_HEADS, 0].reshape(1, B_HEADS),
        "q_gain_a": fold(dg_qa, A_Q_HEADS, HEAD_DIM) * scale_ab,
        "k_gain_a": fold(dg_kva[:, 0:A_KV_WIDTH], A_KV_HEADS, HEAD_DIM),
        "sinks_a": (jnp.sum(dsink, axis=(1, 2)) * (1.0 / HEAD_DIM)).reshape(1, A_Q_HEADS),
        "q_gain_b": fold(dg_qb, B_HEADS, HEAD_DIM) * scale_ab,
        "k_gain_b": fold(dg_kb, B_HEADS, HEAD_DIM),
        "q_gain_c": fold(dg_qc, C_HEADS, C_HEAD_DIM),
        "k_gain_c": fold(dg_kc[:, 0:C_WIDTH], C_HEADS, C_HEAD_DIM),
    }
    return loss, grad_x, small_grads, g


def _coords():
    return lax.axis_index("x"), lax.axis_index("y"), lax.axis_index("c")


def _all_gather(shards, name):
    n = len(shards)

    def body(*refs):
        ins = refs[0:n]
        outs = refs[n:2 * n]
        send_sems, recv_sems, local_sems = refs[2 * n:2 * n + 3]
        x, y, c = _coords()
        me, sibling = (x, y, c), (x, y, 1 - c)
        chips = [(1 - x, y), (x, 1 - y), (1 - x, 1 - y)]
        idx = lambda p: 4 * p[0] + 2 * p[1] + p[2]

        def copy(a, k, block, to, src=None):
            slot = outs[a].at[idx(block)]
            return pltpu.make_async_remote_copy(
                src_ref=slot if src is None else src, dst_ref=slot,
                send_sem=send_sems.at[a, k], recv_sem=recv_sems.at[a, k], device_id=to, device_id_type=MESH)

        mine = [pltpu.make_async_copy(ins[a], outs[a].at[idx(me)], local_sems.at[a]) for a in range(n)]
        for cp in mine:
            cp.start()
        first = []
        for a in range(n):
            first.append(copy(a, 0, me, sibling, src=ins[a]))
            first += [copy(a, 1 + j, me, (*chip, c), src=ins[a]) for j, chip in enumerate(chips)]
        for cp in first:
            cp.start()
        passed = []
        for j, chip in enumerate(chips):
            for a in range(n):
                copy(a, 1 + j, (*chip, c), me).wait_recv()
                fwd = copy(a, 4 + j, (*chip, c), sibling)
                fwd.start()
                passed.append(fwd)
        for a in range(n):
            copy(a, 0, sibling, me).wait_recv()
            for j, chip in enumerate(chips):
                copy(a, 4 + j, (*chip, 1 - c), me).wait_recv()
        for cp in first + passed:
            cp.wait_send()
        for cp in mine:
            cp.wait()

    any_spec = pl.BlockSpec(memory_space=pl.ANY)
    return pl.pallas_call(
        body, name=name,
        in_specs=[any_spec] * n, out_specs=[any_spec] * n,
        out_shape=[jax.ShapeDtypeStruct((N_DEV,) + sh.shape, sh.dtype) for sh in shards],
        scratch_shapes=[pltpu.SemaphoreType.DMA((n, 7)), pltpu.SemaphoreType.DMA((n, 7)), pltpu.SemaphoreType.DMA((n,))],
    )(*shards)


def _all_reduce_small(vec, name):
    p = vec.shape[1]

    def body(v_ref, o_ref, gather, send_sems, recv_sems):
        x, y, c = _coords()
        my = 4 * x + 2 * y + c
        peers = [(x ^ ((k >> 2) & 1), y ^ ((k >> 1) & 1), c ^ (k & 1)) for k in range(1, N_DEV)]
        gather[my] = v_ref[...]
        sends = [pltpu.make_async_remote_copy(
            src_ref=v_ref, dst_ref=gather.at[my], send_sem=send_sems.at[k], recv_sem=recv_sems.at[k],
            device_id=peer, device_id_type=MESH) for k, peer in enumerate(peers)]
        for cp in sends:
            cp.start()
        for k, peer in enumerate(peers):
            pid = 4 * peer[0] + 2 * peer[1] + peer[2]
            pltpu.make_async_remote_copy(
                src_ref=v_ref, dst_ref=gather.at[pid], send_sem=send_sems.at[k], recv_sem=recv_sems.at[k],
                device_id=peer, device_id_type=MESH).wait_recv()
        for cp in sends:
            cp.wait_send()
        total = gather[0]
        for j in range(1, N_DEV):
            total = total + gather[j]
        o_ref[...] = total

    vm = pl.BlockSpec(memory_space=pltpu.VMEM)
    return pl.pallas_call(
        body, name=name, in_specs=[vm], out_specs=vm,
        out_shape=jax.ShapeDtypeStruct((8, p), F32),
        scratch_shapes=[pltpu.VMEM((N_DEV, 8, p), F32), pltpu.SemaphoreType.DMA((7,)), pltpu.SemaphoreType.DMA((7,))],
    )(vec)[0:1]


def _sum_parts(parts, name):
    _, rows, cols = parts.shape
    br = _tile(rows, 64, 16)

    def body(p_ref, o_ref):
        total = p_ref[0].astype(F32)
        for j in range(1, N_DEV):
            total = total + p_ref[j].astype(F32)
        o_ref[...] = total

    return pl.pallas_call(
        body, name=name, grid=(rows // br,),
        in_specs=[pl.BlockSpec((N_DEV, br, cols), lambda i: (0, i, 0))],
        out_specs=pl.BlockSpec((br, cols), lambda i: (i, 0)),
        out_shape=jax.ShapeDtypeStruct((rows, cols), F32),
        compiler_params=_params(("parallel",), VMEM_BIG),
    )(parts)


def _adamw(w, g, m, v, name, br=32):
    rows, cols = w.shape
    br = min(br, rows)
    c1 = 1.0 / (1.0 - ADAM_B1 ** ADAM_STEP)
    c2 = 1.0 / (1.0 - ADAM_B2 ** ADAM_STEP)

    def body(w_ref, g_ref, m_ref, v_ref, d_ref, nm_ref, nv_ref):
        gv = g_ref[...]
        nm = ADAM_B1 * m_ref[...] + (1.0 - ADAM_B1) * gv
        nv = ADAM_B2 * v_ref[...] + (1.0 - ADAM_B2) * (gv * gv)
        d_ref[...] = -ADAM_LR * ((nm * c1) / (jnp.sqrt(nv * c2) + ADAM_EPS) + ADAM_WD * w_ref[...])
        nm_ref[...] = nm
        nv_ref[...] = nv

    spec = pl.BlockSpec((br, cols), lambda i: (i, 0))
    shape = jax.ShapeDtypeStruct((rows, cols), F32)
    return pl.pallas_call(
        body, name=name, grid=(pl.cdiv(rows, br),), in_specs=[spec] * 4, out_specs=[spec] * 3, out_shape=[shape] * 3,
        compiler_params=_params(("parallel",), VMEM_BIG),
    )(w, g, m, v)


def _adamw_parts(w, parts, m, v, name):
    rows, cols = w.shape
    br = _tile(rows, 32, 16)
    c1 = 1.0 / (1.0 - ADAM_B1 ** ADAM_STEP)
    c2 = 1.0 / (1.0 - ADAM_B2 ** ADAM_STEP)

    def body(w_ref, p_ref, m_ref, v_ref, g_ref, d_ref, nm_ref, nv_ref):
        gv = p_ref[0].astype(F32)
        for j in range(1, N_DEV):
            gv = gv + p_ref[j].astype(F32)
        nm = ADAM_B1 * m_ref[...] + (1.0 - ADAM_B1) * gv
        nv = ADAM_B2 * v_ref[...] + (1.0 - ADAM_B2) * (gv * gv)
        g_ref[...] = gv
        d_ref[...] = -ADAM_LR * ((nm * c1) / (jnp.sqrt(nv * c2) + ADAM_EPS) + ADAM_WD * w_ref[...])
        nm_ref[...] = nm
        nv_ref[...] = nv

    spec = pl.BlockSpec((br, cols), lambda i: (i, 0))
    shape = jax.ShapeDtypeStruct((rows, cols), F32)
    return pl.pallas_call(
        body, name=name, grid=(rows // br,),
        in_specs=[spec, pl.BlockSpec((N_DEV, br, cols), lambda i: (0, i, 0)), spec, spec],
        out_specs=[spec] * 4, out_shape=[shape] * 4,
        compiler_params=_params(("parallel",), VMEM_BIG),
    )(w, parts, m, v)


SMALL_NAMES = ("norm_gain", "mem_norm_gain", "b_forget", "q_gain_a", "k_gain_a", "sinks_a",
               "q_gain_b", "k_gain_b", "q_gain_c", "k_gain_c")
BIG_NAMES = ("w_in", "w_mem_kv", "w_branch_a", "w_branch_b", "w_branch_c", "w_out")
WEIGHT_ORDER = ("norm_gain", "mem_norm_gain", "w_in", "b_forget", "q_gain_a", "k_gain_a", "sinks_a", "q_gain_b",
                "k_gain_b", "q_gain_c", "k_gain_c", "w_mem_kv", "w_branch_a", "w_branch_b", "w_branch_c", "w_out")


def _pack_small(tree):
    flat = jnp.concatenate([tree[n].reshape(1, -1) for n in SMALL_NAMES], axis=1)
    pad = (-flat.shape[1]) % LANES
    return jnp.pad(flat, ((0, 0), (0, pad)))


def _unpack_small(flat, like):
    out, off = {}, 0
    for n in SMALL_NAMES:
        size = like[n].size
        out[n] = flat[:, off:off + size].reshape(like[n].shape)
        off += size
    return out


def kernel(x, mem, norm_gain, mem_norm_gain, w_in, b_forget, q_gain_a, k_gain_a, sinks_a, q_gain_b, k_gain_b, q_gain_c, k_gain_c, w_mem_kv, w_branch_a, w_branch_b, w_branch_c, w_out, loss_target, m_norm_gain, m_mem_norm_gain, m_w_in, m_b_forget, m_q_gain_a, m_k_gain_a, m_sinks_a, m_q_gain_b, m_k_gain_b, m_q_gain_c, m_k_gain_c, m_w_mem_kv, m_w_branch_a, m_w_branch_b, m_w_branch_c, m_w_out, v_norm_gain, v_mem_norm_gain, v_w_in, v_b_forget, v_q_gain_a, v_k_gain_a, v_sinks_a, v_q_gain_b, v_k_gain_b, v_q_gain_c, v_k_gain_c, v_w_mem_kv, v_w_branch_a, v_w_branch_b, v_w_branch_c, v_w_out):
    weights = dict(norm_gain=norm_gain, mem_norm_gain=mem_norm_gain, w_in=w_in, b_forget=b_forget, q_gain_a=q_gain_a,
                   k_gain_a=k_gain_a, sinks_a=sinks_a, q_gain_b=q_gain_b, k_gain_b=k_gain_b, q_gain_c=q_gain_c,
                   k_gain_c=k_gain_c, w_mem_kv=w_mem_kv, w_branch_a=w_branch_a, w_branch_b=w_branch_b,
                   w_branch_c=w_branch_c, w_out=w_out)
    mom_m = dict(norm_gain=m_norm_gain, mem_norm_gain=m_mem_norm_gain, w_in=m_w_in, b_forget=m_b_forget,
                 q_gain_a=m_q_gain_a, k_gain_a=m_k_gain_a, sinks_a=m_sinks_a, q_gain_b=m_q_gain_b, k_gain_b=m_k_gain_b,
                 q_gain_c=m_q_gain_c, k_gain_c=m_k_gain_c, w_mem_kv=m_w_mem_kv, w_branch_a=m_w_branch_a,
                 w_branch_b=m_w_branch_b, w_branch_c=m_w_branch_c, w_out=m_w_out)
    mom_v = dict(norm_gain=v_norm_gain, mem_norm_gain=v_mem_norm_gain, w_in=v_w_in, b_forget=v_b_forget,
                 q_gain_a=v_q_gain_a, k_gain_a=v_k_gain_a, sinks_a=v_sinks_a, q_gain_b=v_q_gain_b, k_gain_b=v_k_gain_b,
                 q_gain_c=v_q_gain_c, k_gain_c=v_k_gain_c, w_mem_kv=v_w_mem_kv, w_branch_a=v_w_branch_a,
                 w_branch_b=v_w_branch_b, w_branch_c=v_w_branch_c, w_out=v_w_out)
    wi = w_in[0]
    sh_qkv = jnp.concatenate([wi[:, a:b] for a, b in SRC_RANGES[0:3]], axis=1).astype(BF16)
    sh_zg = jnp.concatenate([wi[:, a:b] for a, b in SRC_RANGES[3:6]] + [wi[:, SRC_GATE:]], axis=1).astype(BF16)
    sh_wf = jnp.pad(wi[:, FB_SRC:FB_SRC + B_HEADS], ((0, 0), (0, FB_PAD - B_HEADS))).astype(BF16)
    shards = {"zg": sh_zg, "wo": w_out[0].astype(BF16), "wa": w_branch_a[0].astype(BF16),
              "wb": w_branch_b[0].astype(BF16), "wc": w_branch_c[0].astype(BF16)}
    first = ("qkv", "wf", "wk")
    full = _all_gather([sh_qkv, sh_wf, w_mem_kv[0].astype(BF16)], "weights_all_gather")
    wg = {kname: arr.reshape(arr.shape[0] * arr.shape[1], arr.shape[2]) for kname, arr in zip(first, full)}

    small = {n: weights[n] for n in SMALL_NAMES}
    loss_local, grad_x, small_g, parts = _local_step(x[0], mem[0], loss_target[0], small, wg, shards)

    grads, delta, new_m, new_v = {}, {}, {}, {}
    for n, kname in (("w_mem_kv", "wk"), ("w_out", "wo"), ("w_branch_a", "wa"), ("w_branch_b", "wb"), ("w_branch_c", "wc")):
        gsum, dlt, nm, nv = _adamw_parts(weights[n][0], parts[kname], mom_m[n][0], mom_v[n][0], "adamw_" + n)
        grads[n], delta[n], new_m[n], new_v[n] = gsum, dlt[None], nm[None], nv[None]
    gq, gz, gf, gg = (_sum_parts(parts[k], "grad_sum_" + k) for k in ("wm_qkv", "wm_z", "wf", "wm_g"))
    g_in = jnp.concatenate([gq[:, COL_QA:COL_QB], gz[:, COL_ZA:COL_ZB], gq[:, COL_QB:COL_QC], gz[:, COL_ZB:COL_ZC],
                            gf[:, 0:B_HEADS], gq[:, COL_QC:W_QKV], gz[:, COL_ZC:W_Z], gg], axis=1)
    dlt, nm, nv = _adamw(w_in[0], g_in, m_w_in[0], v_w_in[0], "adamw_w_in")
    grads["w_in"], delta["w_in"], new_m["w_in"], new_v["w_in"] = g_in, dlt[None], nm[None], nv[None]

    packed = _pack_small(small_g)
    reduced = _all_reduce_small(jnp.broadcast_to(packed, (8, packed.shape[1])), "small_all_reduce")
    grads.update(_unpack_small(reduced, small))

    loss = lax.psum(loss_local, ("x", "y", "c"))

    pw, pm, pv = _pack_small(small), _pack_small({n: mom_m[n] for n in SMALL_NAMES}), _pack_small({n: mom_v[n] for n in SMALL_NAMES})
    rep8 = lambda a: jnp.broadcast_to(a, (8, a.shape[1]))
    dlt, nm, nv = _adamw(rep8(pw), rep8(reduced), rep8(pm), rep8(pv), "adamw_small")
    for tree, flat in ((delta, dlt), (new_m, nm), (new_v, nv)):
        tree.update(_unpack_small(flat[0:1], small))
    for n in BIG_NAMES:
        grads[n] = grads[n][None]
    return (loss, grad_x[None], *[grads[n] for n in WEIGHT_ORDER], *[delta[n] for n in WEIGHT_ORDER],
            *[new_m[n] for n in WEIGHT_ORDER], *[new_v[n] for n in WEIGHT_ORDER])
```

```python
import math

import jax
import jax.numpy as jnp
import numpy as np
from jax import lax
from jax.experimental import pallas as pl
from jax.experimental.pallas import tpu as pltpu

F32 = jnp.float32
BF16 = jnp.bfloat16

N_DEV = 8
HEAD_DIM = 64
A_Q_HEADS = 12
A_KV_HEADS = 4
A_GROUP = 3
B_HEADS = 12
C_HEADS = 4
C_HEAD_DIM = 128
WINDOW = 128
A_WIDTH = 768
A_KV_WIDTH = 256
B_WIDTH = 768
C_WIDTH = 512
EPS = 1e-6
NEG = -1e30

COL_QA, COL_KA, COL_VA = 0, 768, 1024
COL_QB, COL_KB, COL_VB = 1280, 2048, 2816
COL_QC = 3584
W_QKV = 4096
COL_ZA, COL_ZB, COL_ZC = 0, 768, 1536
COL_GATE = W_Z = 2048
SRC_RANGES = ((0, 1280), (2048, 4352), (5132, 5644), (1280, 2048), (4352, 5120), (5644, 6156))
SRC_GATE = 6156
FB_SRC = 5120
FB_PAD = 128

ADAM_LR = 0.001
ADAM_B1 = 0.9
ADAM_B2 = 0.999
ADAM_EPS = 1e-08
ADAM_WD = 0.01
ADAM_STEP = 10

VMEM_BIG = 52 * 1024 * 1024
LANES = 128
MESH = pl.DeviceIdType.MESH


def _tile(n, pref, mult=128):
    if n <= pref:
        return n
    t = (pref // mult) * mult
    while t >= mult:
        if n % t == 0:
            return t
        t -= mult
    return n


def _params(sem=None, vmem=None):
    kw = {}
    if sem is not None:
        kw["dimension_semantics"] = sem
    if vmem is not None:
        kw["vmem_limit_bytes"] = vmem
    return pltpu.CompilerParams(**kw)


def _sigmoid(x):
    return 1.0 / (1.0 + jnp.exp(-x))


def _block_diag(hd):
    r = np.arange(LANES)
    return jnp.asarray((r[:, None] // hd) == (r[None, :] // hd), dtype=BF16)


def _seg_sum(t, bd):
    hi = t.astype(BF16)
    lo = (t - hi.astype(F32)).astype(BF16)
    outs = []
    for c in range(t.shape[1] // LANES):
        sl = slice(c * LANES, (c + 1) * LANES)
        outs.append(jnp.dot(hi[:, sl], bd, preferred_element_type=F32) + jnp.dot(lo[:, sl], bd, preferred_element_type=F32))
    return outs[0] if len(outs) == 1 else jnp.concatenate(outs, axis=1)


def _rmsnorm_fwd(x, gain, name):
    rows, d = x.shape
    bm = _tile(rows, 512, 8)

    def body(x_ref, g_ref, o_ref):
        xv = x_ref[...]
        ms = jnp.mean(xv * xv, axis=-1, keepdims=True)
        o_ref[...] = (xv * lax.rsqrt(ms + EPS) * g_ref[...]).astype(BF16)

    return pl.pallas_call(
        body, name=name, grid=(rows // bm,),
        in_specs=[pl.BlockSpec((bm, d), lambda i: (i, 0)), pl.BlockSpec((1, d), lambda i: (0, 0))],
        out_specs=pl.BlockSpec((bm, d), lambda i: (i, 0)),
        out_shape=jax.ShapeDtypeStruct((rows, d), BF16),
        compiler_params=_params(("parallel",)),
    )(x, gain)


def _rmsnorm_bwd(x, dhn, gain, dy, name):
    rows, d = x.shape
    bm = _tile(rows, 512, 8)
    with_dx = dy is not None

    def body(*refs):
        if with_dx:
            x_ref, dh_ref, g_ref, dy_ref, gx_ref, dg_ref = refs
        else:
            x_ref, dh_ref, g_ref, dg_ref = refs
        i = pl.program_id(0)
        xv = x_ref[...]
        rstd = lax.rsqrt(jnp.mean(xv * xv, axis=-1, keepdims=True) + EPS)
        xhat = xv * rstd
        dh = dh_ref[...]
        part = jnp.sum((dh * xhat).reshape(bm // 8, 8, d), axis=0)

        @pl.when(i == 0)
        def _():
            dg_ref[...] = part

        @pl.when(i > 0)
        def _():
            dg_ref[...] += part

        if with_dx:
            g = dh * g_ref[...]
            mean = jnp.mean(g * xhat, axis=-1, keepdims=True)
            gx_ref[...] = dy_ref[...] + rstd * (g - xhat * mean)

    row_spec = pl.BlockSpec((bm, d), lambda i: (i, 0))
    in_specs = [row_spec, row_spec, pl.BlockSpec((1, d), lambda i: (0, 0))]
    args = [x, dhn, gain]
    dg_spec = pl.BlockSpec((8, d), lambda i: (0, 0))
    dg_shape = jax.ShapeDtypeStruct((8, d), F32)
    if with_dx:
        in_specs.append(row_spec)
        args.append(dy)
        out_specs = [row_spec, dg_spec]
        out_shape = [jax.ShapeDtypeStruct((rows, d), F32), dg_shape]
    else:
        out_specs = [dg_spec]
        out_shape = [dg_shape]
    outs = pl.pallas_call(
        body, name=name, grid=(rows // bm,), in_specs=in_specs, out_specs=out_specs, out_shape=out_shape,
        compiler_params=_params(("arbitrary",), VMEM_BIG),
    )(*args)
    return outs if with_dx else (None, outs[0])


class _Comm:
    def __init__(self, kind, arrays):
        self.kind = kind
        self.arrays = list(arrays)
        self.n = len(self.arrays)

    def out_shapes(self):
        if self.kind == "gather":
            return [jax.ShapeDtypeStruct((N_DEV,) + a.shape, a.dtype) for a in self.arrays]
        return [jax.ShapeDtypeStruct(a.shape, a.dtype) for a in self.arrays]

    def scratch(self):
        return [pltpu.SemaphoreType.DMA((self.n, N_DEV - 1)), pltpu.SemaphoreType.DMA((self.n, N_DEV - 1)),
                pltpu.SemaphoreType.DMA((self.n,))]

    def _plan(self, ins, outs, sems, with_recvs):
        send_sems, recv_sems, local_sems = sems
        x, y, c = lax.axis_index("x"), lax.axis_index("y"), lax.axis_index("c")
        my = 4 * x + 2 * y + c
        gather = self.kind == "gather"
        local, sends, recvs = [], [], []
        for a in range(self.n):
            local.append(pltpu.make_async_copy(ins[a] if gather else ins[a].at[my], outs[a].at[my], local_sems.at[a]))
            for k in range(1, N_DEV):
                peer = (x ^ ((k >> 2) & 1), y ^ ((k >> 1) & 1), c ^ (k & 1))
                pid = 4 * peer[0] + 2 * peer[1] + peer[2]
                src = ins[a] if gather else ins[a].at[pid]
                sem = dict(send_sem=send_sems.at[a, k - 1], recv_sem=recv_sems.at[a, k - 1], device_id=peer, device_id_type=MESH)
                sends.append(pltpu.make_async_remote_copy(src_ref=src, dst_ref=outs[a].at[my], **sem))
                if with_recvs:
                    recvs.append(pltpu.make_async_remote_copy(src_ref=src, dst_ref=outs[a].at[pid], **sem))
        return local, sends, recvs

    def start(self, ins, outs, sems):
        local, sends, _ = self._plan(ins, outs, sems, False)
        for cp in local + sends:
            cp.start()

    def wait(self, ins, outs, sems):
        local, sends, recvs = self._plan(ins, outs, sems, True)
        for cp in recvs:
            cp.wait_recv()
        for cp in sends:
            cp.wait_send()
        for cp in local:
            cp.wait()


def _grid_edges(grid):
    first = last = None
    for ax, size in enumerate(grid):
        pid = pl.program_id(ax)
        f, l = pid == 0, pid == size - 1
        first = f if first is None else first & f
        last = l if last is None else last & l
    return first, last


def _hosted_call(body, comm, *, name, grid, in_specs, out_specs, out_shape, scratch_shapes, args, sem, vmem=None):
    in_specs, out_specs, out_shape, scratch_shapes = list(in_specs), list(out_specs), list(out_shape), list(scratch_shapes)
    if comm is None:
        res = pl.pallas_call(body, name=name, grid=grid, in_specs=in_specs, out_specs=out_specs, out_shape=out_shape,
                             scratch_shapes=scratch_shapes, compiler_params=_params(sem, vmem))(*args)
        return list(res), []
    n_in, n_out, n_scr, nc = len(in_specs), len(out_shape), len(scratch_shapes), comm.n

    def hosted(*refs):
        ins = refs[0:n_in]
        comm_in = refs[n_in:n_in + nc]
        outs = refs[n_in + nc:n_in + nc + n_out]
        comm_out = refs[n_in + nc + n_out:n_in + 2 * nc + n_out]
        scr = refs[n_in + 2 * nc + n_out:n_in + 2 * nc + n_out + n_scr]
        sems = refs[n_in + 2 * nc + n_out + n_scr:]
        first, last = _grid_edges(grid)

        @pl.when(first)
        def _():
            comm.start(comm_in, comm_out, sems)

        body(*ins, *outs, *scr)

        @pl.when(last)
        def _():
            comm.wait(comm_in, comm_out, sems)

    any_spec = pl.BlockSpec(memory_space=pl.ANY)
    res = pl.pallas_call(
        hosted, name=name, grid=grid, in_specs=in_specs + [any_spec] * nc, out_specs=out_specs + [any_spec] * nc,
        out_shape=out_shape + comm.out_shapes(), scratch_shapes=scratch_shapes + comm.scratch(),
        compiler_params=_params(("arbitrary",) * len(grid), vmem),
    )(*args, *comm.arrays)
    return list(res[0:n_out]), list(res[n_out:])


def _mm(a, b, *, grid, a_spec, b_spec, o_spec, o_shape, o_dtype, contract, name, add=None, add_spec=None, acc_shape=None,
        comm=None):
    nk = grid[2]
    has_add = add is not None

    def body(*refs):
        a_ref, b_ref = refs[0], refs[1]
        add_ref = refs[2] if has_add else None
        o_ref = refs[3] if has_add else refs[2]
        part = lax.dot_general(a_ref[...], b_ref[...], (contract, ((), ())), preferred_element_type=F32)
        if nk == 1:
            if has_add:
                part = part + add_ref[...]
            o_ref[...] = part.astype(o_dtype)
        else:
            acc = refs[-1]
            k = pl.program_id(2)

            @pl.when(k == 0)
            def _():
                acc[...] = part

            @pl.when(k > 0)
            def _():
                acc[...] += part

            @pl.when(k == nk - 1)
            def _():
                r = acc[...]
                if has_add:
                    r = r + add_ref[...]
                o_ref[...] = r.astype(o_dtype)

    in_specs = [a_spec, b_spec] + ([add_spec] if has_add else [])
    args = [a, b] + ([add] if has_add else [])
    scratch = [pltpu.VMEM(acc_shape, F32)] if nk > 1 else []
    outs, comm_outs = _hosted_call(
        body, comm, name=name, grid=grid, in_specs=in_specs, out_specs=[o_spec],
        out_shape=[jax.ShapeDtypeStruct(o_shape, o_dtype)], scratch_shapes=scratch, args=args,
        sem=("parallel", "parallel", "arbitrary"), vmem=VMEM_BIG)
    return outs[0] if comm is None else (outs[0], comm_outs)


def _mm_nn(a, b, *, bm, bn, bk, o_dtype, name, add=None, comm=None):
    m, kd = a.shape
    n = b.shape[1]
    bm, bn, bk = _tile(m, bm, 8), _tile(n, bn), _tile(kd, bk)
    o_spec = pl.BlockSpec((bm, bn), lambda i, j, k: (i, j))
    return _mm(a, b, grid=(m // bm, n // bn, kd // bk),
               a_spec=pl.BlockSpec((bm, bk), lambda i, j, k: (i, k)),
               b_spec=pl.BlockSpec((bk, bn), lambda i, j, k: (k, j)),
               o_spec=o_spec, o_shape=(m, n), o_dtype=o_dtype, contract=((1,), (0,)), name=name,
               add=add, add_spec=o_spec, acc_shape=(bm, bn), comm=comm)


def _mm_nt(a, b, *, bm, bn, bk, o_dtype, name, add=None, b_col0=0, comm=None):
    m, kd = a.shape
    n = b.shape[0]
    bm, bn, bk = _tile(m, bm, 8), _tile(n, bn), _tile(math.gcd(kd, b_col0), bk)
    kb0 = b_col0 // bk
    o_spec = pl.BlockSpec((bm, bn), lambda i, j, k: (i, j))
    return _mm(a, b, grid=(m // bm, n // bn, kd // bk),
               a_spec=pl.BlockSpec((bm, bk), lambda i, j, k: (i, k)),
               b_spec=pl.BlockSpec((bn, bk), lambda i, j, k: (j, kb0 + k)),
               o_spec=o_spec, o_shape=(m, n), o_dtype=o_dtype, contract=((1,), (1,)), name=name,
               add=add, add_spec=o_spec, acc_shape=(bm, bn), comm=comm)


def _mm_nt_sum(terms, *, bm, bn, bk, name, add=None, comm=None):
    m = terms[0][0].shape[0]
    n = terms[0][1].shape[0]
    bm, bn = _tile(m, bm, 8), _tile(n, bn)
    nt = (((1,), (1,)), ((), ()))
    plan, groups, start = [], [], 0
    for a, b, col0 in terms:
        kd = a.shape[1]
        tk = _tile(math.gcd(kd, col0), bk)
        steps = kd // tk
        last = groups[-1] if groups else None
        if last is not None and last[0] is b and last[4] == tk and (last[3] + last[2]) * tk == col0:
            last[2] += steps
        else:
            groups.append([b, start, steps, col0 // tk, tk])
        plan.append((start, steps, len(groups) - 1))
        start += steps
    nk = start
    nterm, ngroup, has_add = len(terms), len(groups), add is not None

    def body(*refs):
        a_refs, b_refs = refs[0:nterm], refs[nterm:nterm + ngroup]
        add_ref = refs[nterm + ngroup] if has_add else None
        o_ref, acc = refs[nterm + ngroup + has_add], refs[nterm + ngroup + has_add + 1]
        k = pl.program_id(2)
        for t, (s0, steps, grp) in enumerate(plan):
            @pl.when((k >= s0) & (k < s0 + steps))
            def _():
                part = lax.dot_general(a_refs[t][...], b_refs[grp][...], nt, preferred_element_type=F32)

                @pl.when(k == 0)
                def _():
                    acc[...] = part

                @pl.when(k > 0)
                def _():
                    acc[...] += part

        @pl.when(k == nk - 1)
        def _():
            o_ref[...] = acc[...] + add_ref[...] if has_add else acc[...]

    def a_spec(tk, s0, steps):
        return pl.BlockSpec((bm, tk), lambda i, j, k: (i, jnp.clip(k - s0, 0, steps - 1)))

    def b_spec(tk, s0, steps, off):
        return pl.BlockSpec((bn, tk), lambda i, j, k: (j, off + jnp.clip(k - s0, 0, steps - 1)))

    o_spec = pl.BlockSpec((bm, bn), lambda i, j, k: (i, j))
    in_specs = [a_spec(groups[grp][4], s0, steps) for s0, steps, grp in plan]
    in_specs += [b_spec(tk, s0, steps, cb0) for _, s0, steps, cb0, tk in groups]
    args = [a for a, _, _ in terms] + [grp[0] for grp in groups]
    if has_add:
        in_specs.append(o_spec)
        args.append(add)
    outs, comm_outs = _hosted_call(
        body, comm, name=name, grid=(m // bm, n // bn, nk), in_specs=in_specs,
        out_specs=[o_spec], out_shape=[jax.ShapeDtypeStruct((m, n), F32)],
        scratch_shapes=[pltpu.VMEM((bm, bn), F32)], args=args,
        sem=("parallel", "parallel", "arbitrary"), vmem=VMEM_BIG)
    return outs[0] if comm is None else (outs[0], comm_outs)


def _mm_tn(a, b, *, bm, bn, bk, o_dtype, name, comm=None):
    kd, m = a.shape
    n = b.shape[1]
    bm, bn, bk = _tile(m, bm), _tile(n, bn), _tile(kd, bk, 8)
    return _mm(a, b, grid=(m // bm, n // bn, kd // bk),
               a_spec=pl.BlockSpec((bk, bm), lambda i, j, k: (k, i)),
               b_spec=pl.BlockSpec((bk, bn), lambda i, j, k: (k, j)),
               o_spec=pl.BlockSpec((bm, bn), lambda i, j, k: (i, j)),
               o_shape=(m, n), o_dtype=o_dtype, contract=((0,), (0,)), name=name, acc_shape=(bm, bn), comm=comm)


def _branch_full(w8):
    kb, ds = w8.shape[0] // N_DEV, w8.shape[1]
    return w8.reshape(N_DEV, kb, ds).transpose(1, 0, 2).reshape(kb, N_DEV * ds)


def _branch_shards(g):
    kb, ds = g.shape[0], g.shape[1] // N_DEV
    return g.reshape(kb, N_DEV, ds).transpose(1, 0, 2).reshape(N_DEV * kb, ds)


def _headnorm_fwd(src, c0, width, bw, hd, gain, nflag, head_major, name):
    rows = src.shape[0]
    bm = _tile(rows, 2048 if bw <= 256 else 1024, 16)
    bd = _block_diag(hd)
    cb0 = c0 // bw

    def body(x_ref, g_ref, f_ref, bd_ref, o_ref):
        xv = x_ref[...].astype(F32)
        ss = _seg_sum(xv * xv, bd_ref[...])
        rstd = lax.rsqrt(ss * (1.0 / hd) + EPS)
        y = (xv * jnp.where(f_ref[...] > 0.0, rstd, 1.0) * g_ref[...]).astype(BF16)
        if head_major:
            for h in range(bw // HEAD_DIM):
                o_ref[h] = y[:, h * HEAD_DIM:(h + 1) * HEAD_DIM]
        else:
            o_ref[...] = y

    vec_spec = pl.BlockSpec((1, bw), lambda i, t: (0, t))
    if head_major:
        hpb = bw // HEAD_DIM
        out_spec = pl.BlockSpec((hpb, bm, HEAD_DIM), lambda i, t: (t, i, 0))
        out_shape = jax.ShapeDtypeStruct((width // HEAD_DIM, rows, HEAD_DIM), BF16)
    else:
        out_spec = pl.BlockSpec((bm, bw), lambda i, t: (i, t))
        out_shape = jax.ShapeDtypeStruct((rows, width), BF16)
    return pl.pallas_call(
        body, name=name, grid=(rows // bm, width // bw),
        in_specs=[pl.BlockSpec((bm, bw), lambda i, t: (i, cb0 + t)), vec_spec, vec_spec,
                  pl.BlockSpec((LANES, LANES), lambda i, t: (0, 0))],
        out_specs=out_spec, out_shape=out_shape,
        compiler_params=_params(("parallel", "parallel")),
    )(src, gain, nflag, bd)


def _headnorm_bwd(src, c0, width, bw, hd, gain, nflag, dyn, target, t0, name):
    rows = src.shape[0]
    bm = _tile(rows, 2048 if bw <= 256 else 1024, 16)
    bd = _block_diag(hd)
    cb0 = c0 // bw
    tb0 = t0 // bw
    aliased = target is not None

    def body(*refs):
        if aliased:
            x_ref, dy_ref, g_ref, f_ref, bd_ref, _, o_ref, dg_ref = refs
        else:
            x_ref, dy_ref, g_ref, f_ref, bd_ref, o_ref, dg_ref = refs
        i = pl.program_id(1)
        xv = x_ref[...].astype(F32)
        dyv = dy_ref[...]
        bdv = bd_ref[...]
        rstd = lax.rsqrt(_seg_sum(xv * xv, bdv) * (1.0 / hd) + EPS)
        xhat = xv * rstd
        g = dyv * g_ref[...]
        mean = _seg_sum(g * xhat, bdv) * (1.0 / hd)
        dx = jnp.where(f_ref[...] > 0.0, rstd * (g - xhat * mean), g)
        o_ref[...] = dx.astype(BF16)
        part = jnp.sum((dyv * xhat).reshape(bm // 8, 8, bw), axis=0)

        @pl.when(i == 0)
        def _():
            dg_ref[...] = part

        @pl.when(i > 0)
        def _():
            dg_ref[...] += part

    vec_spec = pl.BlockSpec((1, bw), lambda t, i: (0, t))
    in_specs = [pl.BlockSpec((bm, bw), lambda t, i: (i, cb0 + t)), pl.BlockSpec((bm, bw), lambda t, i: (i, t)),
                vec_spec, vec_spec, pl.BlockSpec((LANES, LANES), lambda t, i: (0, 0))]
    args = [src, dyn, gain, nflag, bd]
    aliases = {}
    if aliased:
        in_specs.append(pl.BlockSpec(memory_space=pl.ANY))
        args.append(target)
        aliases = {5: 0}
        o_shape = jax.ShapeDtypeStruct(target.shape, BF16)
    else:
        o_shape = jax.ShapeDtypeStruct((rows, width), BF16)
    out, dg = pl.pallas_call(
        body, name=name, grid=(width // bw, rows // bm), in_specs=in_specs,
        out_specs=[pl.BlockSpec((bm, bw), lambda t, i: (i, tb0 + t)), pl.BlockSpec((8, bw), lambda t, i: (0, t))],
        out_shape=[o_shape, jax.ShapeDtypeStruct((8, width), F32)],
        input_output_aliases=aliases,
        compiler_params=_params(("parallel", "arbitrary")),
    )(*args)
    return out, dg


def _fox_prep(pfb, bpad, name):
    s = pfb.shape[0]

    def body(p_ref, b_ref, c_ref):
        z = p_ref[...] + b_ref[...]
        logf = jnp.minimum(z, 0.0) - jnp.log(1.0 + jnp.exp(-jnp.abs(z)))
        x = logf.T[0:16, :]
        lane = lax.broadcasted_iota(jnp.int32, (16, s), 1)
        sh = 1
        while sh < s:
            x = x + jnp.where(lane >= sh, pltpu.roll(x, sh, 1), 0.0)
            sh *= 2
        c_ref[...] = x

    return pl.pallas_call(
        body, name=name, grid=(1,),
        in_specs=[pl.BlockSpec((s, FB_PAD), lambda i: (0, 0)), pl.BlockSpec((1, FB_PAD), lambda i: (0, 0))],
        out_specs=pl.BlockSpec((16, s), lambda i: (0, 0)),
        out_shape=jax.ShapeDtypeStruct((16, s), F32),
        compiler_params=_params(("arbitrary",)),
    )(pfb, bpad)


def _fox_prep_bwd(pfb, bpad, dct, name):
    s = pfb.shape[0]

    def body(p_ref, b_ref, dc_ref, df_ref, db_ref):
        zt = (p_ref[...] + b_ref[...]).T[0:16, :]
        y = dc_ref[...]
        lane = lax.broadcasted_iota(jnp.int32, (16, s), 1)
        sh = 1
        while sh < s:
            y = y + jnp.where(lane < s - sh, pltpu.roll(y, s - sh, 1), 0.0)
            sh *= 2
        dz = y * _sigmoid(-zt)
        db_ref[...] = jnp.broadcast_to(jnp.sum(dz, axis=1, keepdims=True), (16, FB_PAD))
        full = jnp.concatenate([dz, jnp.zeros((FB_PAD - 16, s), F32)], axis=0)
        df_ref[...] = full.T.astype(BF16)

    return pl.pallas_call(
        body, name=name, grid=(1,),
        in_specs=[pl.BlockSpec((s, FB_PAD), lambda i: (0, 0)), pl.BlockSpec((1, FB_PAD), lambda i: (0, 0)),
                  pl.BlockSpec((16, s), lambda i: (0, 0))],
        out_specs=[pl.BlockSpec((s, FB_PAD), lambda i: (0, 0)), pl.BlockSpec((16, FB_PAD), lambda i: (0, 0))],
        out_shape=[jax.ShapeDtypeStruct((s, FB_PAD), BF16), jax.ShapeDtypeStruct((16, FB_PAD), F32)],
        compiler_params=_params(("arbitrary",)),
    )(pfb, bpad, dct)


def _swa_window(n):
    ws = pl.multiple_of(jnp.maximum(n * WINDOW - WINDOW, 0), WINDOW)
    qi = lax.broadcasted_iota(jnp.int32, (WINDOW, 2 * WINDOW), 0)
    kj = lax.broadcasted_iota(jnp.int32, (WINDOW, 2 * WINDOW), 1)
    rel = qi + (n * WINDOW - ws) - kj
    valid = (rel >= 0) & (rel < WINDOW)
    return ws, valid, rel.astype(F32)


def _attn_a_fwd(q, k, v, sinks, slopes, name):
    s = q.shape[1]
    nb = s // WINDOW
    smem = pl.BlockSpec(memory_space=pltpu.SMEM)

    def body(sink_ref, slope_ref, q_ref, k_ref, v_ref, o_ref, lse_ref):
        n = pl.program_id(0)
        ws, valid, relf = _swa_window(n)
        outs = []
        for h in range(A_Q_HEADS):
            kvh = h // A_GROUP
            kw = k_ref[kvh, pl.ds(ws, 2 * WINDOW), :]
            vw = v_ref[kvh, pl.ds(ws, 2 * WINDOW), :]
            sc = lax.dot_general(q_ref[h], kw, (((1,), (1,)), ((), ())), preferred_element_type=F32)
            sc = jnp.where(valid, sc - slope_ref[h] * relf, NEG)
            sink = sink_ref[h]
            m = jnp.maximum(jnp.max(sc, axis=1, keepdims=True), sink)
            p = jnp.exp(sc - m)
            denom = jnp.sum(p, axis=1, keepdims=True) + jnp.exp(sink - m)
            pn = (p / denom).astype(BF16)
            outs.append(jnp.dot(pn, vw, preferred_element_type=F32))
            lse_ref[h] = jnp.broadcast_to(m + jnp.log(denom), (WINDOW, HEAD_DIM))
        o_ref[...] = jnp.concatenate(outs, axis=1)

    return pl.pallas_call(
        body, name=name, grid=(nb,),
        in_specs=[smem, smem,
                  pl.BlockSpec((A_Q_HEADS, WINDOW, HEAD_DIM), lambda n: (0, n, 0)),
                  pl.BlockSpec((A_KV_HEADS, s, HEAD_DIM), lambda n: (0, 0, 0)),
                  pl.BlockSpec((A_KV_HEADS, s, HEAD_DIM), lambda n: (0, 0, 0))],
        out_specs=[pl.BlockSpec((WINDOW, A_WIDTH), lambda n: (n, 0)),
                   pl.BlockSpec((A_Q_HEADS, WINDOW, HEAD_DIM), lambda n: (0, n, 0))],
        out_shape=[jax.ShapeDtypeStruct((s, A_WIDTH), F32), jax.ShapeDtypeStruct((A_Q_HEADS, s, HEAD_DIM), F32)],
        compiler_params=_params(("parallel",), VMEM_BIG),
    )(sinks, slopes, q, k, v)


def _attn_a_bwd(q, k, v, do, lse, dd, sinks, slopes, name, comm=None):
    s = q.shape[1]
    nb = s // WINDOW
    smem = pl.BlockSpec(memory_space=pltpu.SMEM)
    last = nb - 1

    def body(sink_ref, slope_ref, q_ref, k_ref, v_ref, do_ref, lse_ref, dd_ref, dq_ref, dkv_ref, ds_ref, carry):
        n = pl.program_id(0)

        @pl.when(n == 0)
        def _():
            carry[...] = jnp.zeros(carry.shape, F32)
            ds_ref[...] = jnp.zeros(ds_ref.shape, F32)

        @pl.when(n < nb)
        def _():
            ws, valid, relf = _swa_window(n)
            dqs = []
            dkw = [None] * A_KV_HEADS
            dvw = [None] * A_KV_HEADS
            for h in range(A_Q_HEADS):
                kvh = h // A_GROUP
                qh = q_ref[h]
                doh = do_ref[h]
                kw = k_ref[kvh, pl.ds(ws, 2 * WINDOW), :]
                vw = v_ref[kvh, pl.ds(ws, 2 * WINDOW), :]
                lse_h = lse_ref[h]
                dd_h = dd_ref[h]
                sc = lax.dot_general(qh, kw, (((1,), (1,)), ((), ())), preferred_element_type=F32)
                sc = jnp.where(valid, sc - slope_ref[h] * relf, NEG)
                p = jnp.exp(sc - lse_h[:, 0:1])
                dp = lax.dot_general(doh, vw, (((1,), (1,)), ((), ())), preferred_element_type=F32)
                dsc = (p * (dp - dd_h[:, 0:1])).astype(BF16)
                pb = p.astype(BF16)
                dqs.append(jnp.dot(dsc, kw, preferred_element_type=F32))
                dk_h = lax.dot_general(dsc, qh, (((0,), (0,)), ((), ())), preferred_element_type=F32)
                dv_h = lax.dot_general(pb, doh, (((0,), (0,)), ((), ())), preferred_element_type=F32)
                dkw[kvh] = dk_h if dkw[kvh] is None else dkw[kvh] + dk_h
                dvw[kvh] = dv_h if dvw[kvh] is None else dvw[kvh] + dv_h
                psink = jnp.exp(sink_ref[h] - lse_h)
                ds_ref[h] += jnp.sum((-psink * dd_h).reshape(WINDOW // 8, 8, HEAD_DIM), axis=0)
            dq_ref[...] = jnp.concatenate(dqs, axis=1)
            win = jnp.concatenate(dkw + dvw, axis=1)
            first = win[0:WINDOW]
            second = win[WINDOW:2 * WINDOW]
            dkv_ref[...] = carry[...] + first
            carry[...] = jnp.where(n == 0, first, second)

        @pl.when(n == nb)
        def _():
            dkv_ref[...] = carry[...]

    hm = lambda heads: pl.BlockSpec((heads, WINDOW, HEAD_DIM), lambda n: (0, jnp.minimum(n, last), 0))
    res = lambda heads: pl.BlockSpec((heads, s, HEAD_DIM), lambda n: (0, 0, 0))
    outs, comm_outs = _hosted_call(
        body, comm, name=name, grid=(nb + 1,),
        in_specs=[smem, smem, hm(A_Q_HEADS), res(A_KV_HEADS), res(A_KV_HEADS), hm(A_Q_HEADS), hm(A_Q_HEADS), hm(A_Q_HEADS)],
        out_specs=[pl.BlockSpec((WINDOW, A_WIDTH), lambda n: (jnp.minimum(n, last), 0)),
                   pl.BlockSpec((WINDOW, 2 * A_KV_WIDTH), lambda n: (jnp.maximum(n - 1, 0), 0)),
                   pl.BlockSpec((A_Q_HEADS, 8, HEAD_DIM), lambda n: (0, 0, 0))],
        out_shape=[jax.ShapeDtypeStruct((s, A_WIDTH), F32), jax.ShapeDtypeStruct((s, 2 * A_KV_WIDTH), F32),
                   jax.ShapeDtypeStruct((A_Q_HEADS, 8, HEAD_DIM), F32)],
        scratch_shapes=[pltpu.VMEM((WINDOW, 2 * A_KV_WIDTH), F32)],
        args=[sinks, slopes, q, k, v, do, lse, dd], sem=("arbitrary",), vmem=VMEM_BIG)
    return outs[0], outs[1], outs[2], comm_outs


def _attn_b_fwd(q, k, v, c3, name, comm=None):
    heads, s, _ = q.shape
    bq = min(512, s)
    nq = s // bq
    nt = (((1,), (1,)), ((), ()))

    def body(q_ref, k_ref, v_ref, c_ref, o_ref, lse_ref, m_scr, l_scr, acc_scr):
        i = pl.program_id(1)
        r0 = pl.multiple_of(i * bq, bq)
        row = lax.broadcasted_iota(jnp.int32, (bq, bq), 0)
        col = lax.broadcasted_iota(jnp.int32, (bq, bq), 1)
        m_scr[...] = jnp.full((2, bq, LANES), NEG, F32)
        l_scr[...] = jnp.zeros((2, bq, LANES), F32)
        acc_scr[...] = jnp.zeros((2, bq, HEAD_DIM), F32)

        def step(j, masked):
            k0 = pl.multiple_of(j * bq, bq)
            for h2 in range(2):
                kv = k_ref[h2, pl.ds(k0, bq), :]
                vv = v_ref[h2, pl.ds(k0, bq), :]
                cq0 = c_ref[h2, :, pl.ds(r0, LANES)][:, 0:1]
                sc = lax.dot_general(q_ref[h2], kv, nt, preferred_element_type=F32)
                sc = sc + (cq0 - c_ref[h2, :, pl.ds(k0, bq)])
                if masked:
                    sc = jnp.where(col <= row, sc, NEG)
                m_prev = m_scr[h2]
                m_new = jnp.maximum(m_prev, jnp.max(sc, axis=1, keepdims=True))
                alpha = jnp.exp(m_prev - m_new)
                p = jnp.exp(sc - m_new[:, 0:1])
                l_scr[h2] = alpha * l_scr[h2] + jnp.sum(p, axis=1, keepdims=True)
                p_hi = p.astype(BF16)
                p_lo = (p - p_hi.astype(F32)).astype(BF16)
                pv = jnp.dot(p_hi, vv, preferred_element_type=F32) + jnp.dot(p_lo, vv, preferred_element_type=F32)
                acc_scr[h2] = acc_scr[h2] * alpha[:, 0:HEAD_DIM] + pv
                m_scr[h2] = m_new

        def loop_body(j, carry):
            step(j, False)
            return carry

        lax.fori_loop(0, i, loop_body, 0)
        step(i, True)
        outs = []
        for h2 in range(2):
            l = l_scr[h2]
            outs.append(acc_scr[h2] / l[:, 0:HEAD_DIM])
            lse_ref[h2] = (m_scr[h2] + jnp.log(l))[:, 0:HEAD_DIM]
        o_ref[...] = jnp.concatenate(outs, axis=1)

    res = pl.BlockSpec((2, s, HEAD_DIM), lambda hp, i: (hp, 0, 0))
    outs, comm_outs = _hosted_call(
        body, comm, name=name, grid=(heads // 2, nq),
        in_specs=[pl.BlockSpec((2, bq, HEAD_DIM), lambda hp, i: (hp, i, 0)), res, res,
                  pl.BlockSpec((2, 1, s), lambda hp, i: (hp, 0, 0))],
        out_specs=[pl.BlockSpec((bq, 2 * HEAD_DIM), lambda hp, i: (i, hp)),
                   pl.BlockSpec((2, bq, HEAD_DIM), lambda hp, i: (hp, i, 0))],
        out_shape=[jax.ShapeDtypeStruct((s, heads * HEAD_DIM), F32), jax.ShapeDtypeStruct((heads, s, HEAD_DIM), F32)],
        scratch_shapes=[pltpu.VMEM((2, bq, LANES), F32), pltpu.VMEM((2, bq, LANES), F32), pltpu.VMEM((2, bq, HEAD_DIM), F32)],
        args=[q, k, v, c3], sem=("parallel", "parallel"), vmem=VMEM_BIG)
    return outs[0], outs[1], comm_outs


def _attn_b_bwd(q, k, v, do, lse, dd, c3, name, comm=None):
    heads, s, _ = q.shape
    bq = min(512, s)
    nq = s // bq
    nt = (((1,), (1,)), ((), ()))
    tn = (((0,), (0,)), ((), ()))
    grid = (heads // 2, nq)

    def body(q_ref, k_ref, v_ref, do_ref, lse_ref, dd_ref, c_ref, dq_ref, dk_ref, dv_ref, dc_ref,
             dq_scr, dk_scr, dv_scr, dc_scr):
        j = pl.program_id(1)
        k0 = pl.multiple_of(j * bq, bq)
        row = lax.broadcasted_iota(jnp.int32, (bq, bq), 0)
        col = lax.broadcasted_iota(jnp.int32, (bq, bq), 1)

        @pl.when(j == 0)
        def _():
            dq_scr[...] = jnp.zeros(dq_scr.shape, F32)

        dk_scr[...] = jnp.zeros((2, bq, HEAD_DIM), F32)
        dv_scr[...] = jnp.zeros((2, bq, HEAD_DIM), F32)
        dc_scr[...] = jnp.zeros((2, 1, bq), F32)

        def step(i, masked):
            r0 = pl.multiple_of(i * bq, bq)
            for h2 in range(2):
                kv = k_ref[h2]
                vv = v_ref[h2]
                qv = q_ref[h2, pl.ds(r0, bq), :]
                dov = do_ref[h2, pl.ds(r0, bq), :]
                lse_v = lse_ref[h2, pl.ds(r0, bq), :][:, 0:1]
                dd_v = dd_ref[h2, pl.ds(r0, bq), :][:, 0:1]
                cq0 = c_ref[h2, :, pl.ds(r0, LANES)][:, 0:1]
                sc = lax.dot_general(qv, kv, nt, preferred_element_type=F32) + (cq0 - c_ref[h2, :, pl.ds(k0, bq)])
                if masked:
                    sc = jnp.where(col <= row, sc, NEG)
                p = jnp.exp(sc - lse_v)
                dp = lax.dot_general(dov, vv, nt, preferred_element_type=F32)
                dsc = p * (dp - dd_v)
                dsb = dsc.astype(BF16)
                dv_scr[h2] += lax.dot_general(p.astype(BF16), dov, tn, preferred_element_type=F32)
                dk_scr[h2] += lax.dot_general(dsb, qv, tn, preferred_element_type=F32)
                dq_scr[h2, pl.ds(r0, bq), :] += jnp.dot(dsb, kv, preferred_element_type=F32)
                dc_scr[h2] -= jnp.sum(dsc, axis=0, keepdims=True)

        def loop_body(i, carry):
            step(i, False)
            return carry

        step(j, True)
        lax.fori_loop(j + 1, nq, loop_body, 0)
        dc_ref[...] = dc_scr[...]
        dk_ref[...] = jnp.concatenate([dk_scr[0], dk_scr[1]], axis=1)
        dv_ref[...] = jnp.concatenate([dv_scr[0], dv_scr[1]], axis=1)

        @pl.when(j == nq - 1)
        def _():
            dq_ref[...] = jnp.concatenate([dq_scr[0], dq_scr[1]], axis=1)

    res = pl.BlockSpec((2, s, HEAD_DIM), lambda hp, j: (hp, 0, 0))
    blk = pl.BlockSpec((2, bq, HEAD_DIM), lambda hp, j: (hp, j, 0))
    tm = jax.ShapeDtypeStruct((s, heads * HEAD_DIM), F32)
    in_specs = [res, blk, blk, res, res, res, pl.BlockSpec((2, 1, s), lambda hp, j: (hp, 0, 0))]
    out_specs = [pl.BlockSpec((s, 2 * HEAD_DIM), lambda hp, j: (0, hp)),
                 pl.BlockSpec((bq, 2 * HEAD_DIM), lambda hp, j: (j, hp)),
                 pl.BlockSpec((bq, 2 * HEAD_DIM), lambda hp, j: (j, hp)),
                 pl.BlockSpec((2, 1, bq), lambda hp, j: (hp, 0, j))]
    out_shape = [tm, tm, tm, jax.ShapeDtypeStruct((heads, 1, s), F32)]
    scratch = [pltpu.VMEM((2, s, HEAD_DIM), F32), pltpu.VMEM((2, bq, HEAD_DIM), F32),
               pltpu.VMEM((2, bq, HEAD_DIM), F32), pltpu.VMEM((2, 1, bq), F32)]
    outs, comm_outs = _hosted_call(
        body, comm, name=name, grid=grid, in_specs=in_specs, out_specs=out_specs, out_shape=out_shape,
        scratch_shapes=scratch, args=[q, k, v, do, lse, dd, c3], sem=("parallel", "arbitrary"), vmem=VMEM_BIG)
    return outs[0], outs[1], outs[2], outs[3], comm_outs


def _attn_c_probs(qh, mkh):
    sc = lax.dot_general(qh, mkh, (((1,), (1,)), ((), ())), preferred_element_type=F32) * (C_HEAD_DIM ** -0.5)
    p = jnp.exp(sc - jnp.max(sc, axis=1, keepdims=True))
    return p / jnp.sum(p, axis=1, keepdims=True)


def _attn_c_fwd(q, mkv, name):
    s = q.shape[0]
    m = mkv.shape[0]
    bq = _tile(s, 512, 8)

    def body(q_ref, mk_ref, mv_ref, o_ref):
        outs = []
        for h in range(C_HEADS):
            sl = slice(h * C_HEAD_DIM, (h + 1) * C_HEAD_DIM)
            pn = _attn_c_probs(q_ref[:, sl], mk_ref[:, sl]).astype(BF16)
            outs.append(jnp.dot(pn, mv_ref[:, sl], preferred_element_type=F32))
        o_ref[...] = jnp.concatenate(outs, axis=1)

    return pl.pallas_call(
        body, name=name, grid=(s // bq,),
        in_specs=[pl.BlockSpec((bq, C_WIDTH), lambda i: (i, 0)), pl.BlockSpec((m, C_WIDTH), lambda i: (0, 0)),
                  pl.BlockSpec((m, C_WIDTH), lambda i: (0, 1))],
        out_specs=pl.BlockSpec((bq, C_WIDTH), lambda i: (i, 0)),
        out_shape=jax.ShapeDtypeStruct((s, C_WIDTH), F32),
        compiler_params=_params(("parallel",)),
    )(q, mkv, mkv)


def _attn_c_bwd(q, mkv, do, name):
    s = q.shape[0]
    m = mkv.shape[0]
    bq = _tile(s, 512, 8)
    tn = (((0,), (0,)), ((), ()))

    def body(q_ref, mk_ref, mv_ref, do_ref, dq_ref, dm_ref):
        i = pl.program_id(0)

        @pl.when(i == 0)
        def _():
            dm_ref[...] = jnp.zeros(dm_ref.shape, F32)

        dqs = []
        for h in range(C_HEADS):
            sl = slice(h * C_HEAD_DIM, (h + 1) * C_HEAD_DIM)
            qh, mkh, mvh, doh = q_ref[:, sl], mk_ref[:, sl], mv_ref[:, sl], do_ref[:, sl]
            pn = _attn_c_probs(qh, mkh)
            dp = lax.dot_general(doh, mvh, (((1,), (1,)), ((), ())), preferred_element_type=F32)
            dsc = (pn * (dp - jnp.sum(pn * dp, axis=1, keepdims=True)) * (C_HEAD_DIM ** -0.5)).astype(BF16)
            dqs.append(jnp.dot(dsc, mkh, preferred_element_type=F32))
            dm_ref[:, sl] += lax.dot_general(dsc, qh, tn, preferred_element_type=F32)
            sv = slice(C_WIDTH + h * C_HEAD_DIM, C_WIDTH + (h + 1) * C_HEAD_DIM)
            dm_ref[:, sv] += lax.dot_general(pn.astype(BF16), doh, tn, preferred_element_type=F32)
        dq_ref[...] = jnp.concatenate(dqs, axis=1)

    row = pl.BlockSpec((bq, C_WIDTH), lambda i: (i, 0))
    return pl.pallas_call(
        body, name=name, grid=(s // bq,),
        in_specs=[row, pl.BlockSpec((m, C_WIDTH), lambda i: (0, 0)), pl.BlockSpec((m, C_WIDTH), lambda i: (0, 1)), row],
        out_specs=[row, pl.BlockSpec((m, 2 * C_WIDTH), lambda i: (0, 0))],
        out_shape=[jax.ShapeDtypeStruct((s, C_WIDTH), F32), jax.ShapeDtypeStruct((m, 2 * C_WIDTH), F32)],
        compiler_params=_params(("arbitrary",)),
    )(q, mkv, mkv, do)


def _gate_fwd(y, proj, zc0, bw, name):
    rows, width = y.shape
    bm = _tile(rows, 2048 if bw <= 256 else 1024, 16)
    cb0 = zc0 // bw

    def body(y_ref, z_ref, o_ref):
        z = z_ref[...].astype(F32)
        o_ref[...] = (y_ref[...] * (z * _sigmoid(z))).astype(BF16)

    return pl.pallas_call(
        body, name=name, grid=(rows // bm, width // bw),
        in_specs=[pl.BlockSpec((bm, bw), lambda i, t: (i, t)), pl.BlockSpec((bm, bw), lambda i, t: (i, cb0 + t))],
        out_specs=pl.BlockSpec((bm, bw), lambda i, t: (i, t)),
        out_shape=jax.ShapeDtypeStruct((rows, width), BF16),
        compiler_params=_params(("parallel", "parallel")),
    )(y, proj)


def _gate_bwd(dsv, y, proj, zc0, bw, dproj, t0, head_major, name):
    rows, width = y.shape
    bm = _tile(rows, 2048 if bw <= 256 else 1024, 16)
    cb0 = zc0 // bw
    tb0 = t0 // bw
    bd = _block_diag(HEAD_DIM)
    hpb = bw // HEAD_DIM

    def body(*refs):
        if head_major:
            ds_ref, y_ref, z_ref, bd_ref, _, dp_ref, dy_ref, dd_ref = refs
        else:
            ds_ref, y_ref, z_ref, _, dp_ref, dy_ref = refs
        z = z_ref[...].astype(F32)
        sig = _sigmoid(z)
        dsx = ds_ref[...]
        yv = y_ref[...]
        dy = dsx * (z * sig)
        dp_ref[...] = (dsx * yv * (sig * (1.0 + z * (1.0 - sig)))).astype(BF16)
        if head_major:
            dyb = dy.astype(BF16)
            dd = _seg_sum(dyb.astype(F32) * yv, bd_ref[...])
            for h in range(hpb):
                sl = slice(h * HEAD_DIM, (h + 1) * HEAD_DIM)
                dy_ref[h] = dyb[:, sl]
                dd_ref[h] = dd[:, sl]
        else:
            dy_ref[...] = dy.astype(BF16)

    tile = pl.BlockSpec((bm, bw), lambda i, t: (i, t))
    ztile = pl.BlockSpec((bm, bw), lambda i, t: (i, cb0 + t))
    ttile = pl.BlockSpec((bm, bw), lambda i, t: (i, tb0 + t))
    any_spec = pl.BlockSpec(memory_space=pl.ANY)
    dp_shape = jax.ShapeDtypeStruct(dproj.shape, BF16)
    if head_major:
        hm_spec = pl.BlockSpec((hpb, bm, HEAD_DIM), lambda i, t: (t, i, 0))
        nh = width // HEAD_DIM
        outs = pl.pallas_call(
            body, name=name, grid=(rows // bm, width // bw),
            in_specs=[tile, tile, ztile, pl.BlockSpec((LANES, LANES), lambda i, t: (0, 0)), any_spec],
            out_specs=[ttile, hm_spec, hm_spec],
            out_shape=[dp_shape, jax.ShapeDtypeStruct((nh, rows, HEAD_DIM), BF16),
                       jax.ShapeDtypeStruct((nh, rows, HEAD_DIM), F32)],
            input_output_aliases={4: 0},
            compiler_params=_params(("parallel", "parallel")),
        )(dsv, y, proj, bd, dproj)
        return outs[0], outs[1], outs[2]
    outs = pl.pallas_call(
        body, name=name, grid=(rows // bm, width // bw),
        in_specs=[tile, tile, ztile, any_spec],
        out_specs=[ttile, tile],
        out_shape=[dp_shape, jax.ShapeDtypeStruct((rows, width), BF16)],
        input_output_aliases={3: 0},
        compiler_params=_params(("parallel", "parallel")),
    )(dsv, y, proj, dproj)
    return outs[0], outs[1], None


def _merge_fwd(proj, ua, ub, uc, name):
    rows, d = ua.shape
    bm = _tile(rows, 1024, 16)
    bw = _tile(d, 512)
    g0 = COL_GATE // bw
    gstep = d // bw

    def body(la_ref, lb_ref, lc_ref, ua_ref, ub_ref, uc_ref, o_ref, ga_ref, gb_ref, gc_ref):
        y = None
        for l_ref, u_ref, g_ref in ((la_ref, ua_ref, ga_ref), (lb_ref, ub_ref, gb_ref), (lc_ref, uc_ref, gc_ref)):
            g = _sigmoid(l_ref[...].astype(F32))
            g_ref[...] = g.astype(BF16)
            term = g * u_ref[...].astype(F32)
            y = term if y is None else y + term
        o_ref[...] = y.astype(BF16)

    tile = pl.BlockSpec((bm, bw), lambda i, t: (i, t))
    gate = lambda b: pl.BlockSpec((bm, bw), lambda i, t: (i, g0 + b * gstep + t))
    shape = jax.ShapeDtypeStruct((rows, d), BF16)
    return pl.pallas_call(
        body, name=name, grid=(rows // bm, d // bw),
        in_specs=[gate(0), gate(1), gate(2), tile, tile, tile],
        out_specs=[tile] * 4, out_shape=[shape] * 4,
        compiler_params=_params(("parallel", "parallel")),
    )(proj, proj, proj, ua, ub, uc)


def _merge_bwd(dym, us, gs, name):
    rows, d = dym.shape
    bm = _tile(rows, 1024, 16)
    bw = _tile(d, 512)
    nb = d // bw

    def body(dy_ref, ua_ref, ub_ref, uc_ref, ga_ref, gb_ref, gc_ref, dg_ref, da_ref, db_ref, dc_ref):
        b = pl.program_id(2)
        dyv = dy_ref[...]
        for idx, (u_ref, g_ref, du_ref) in enumerate(((ua_ref, ga_ref, da_ref), (ub_ref, gb_ref, db_ref), (uc_ref, gc_ref, dc_ref))):
            @pl.when(b == idx)
            def _():
                g = g_ref[...].astype(F32)
                du_ref[...] = (g * dyv).astype(BF16)
                dg_ref[...] = (dyv * u_ref[...].astype(F32) * g * (1.0 - g)).astype(BF16)

    tile = pl.BlockSpec((bm, bw), lambda i, t, b: (i, t))
    shape = jax.ShapeDtypeStruct((rows, d), BF16)
    outs = pl.pallas_call(
        body, name=name, grid=(rows // bm, nb, 3),
        in_specs=[tile] * 7,
        out_specs=[pl.BlockSpec((bm, bw), lambda i, t, b: (i, b * nb + t)), tile, tile, tile],
        out_shape=[jax.ShapeDtypeStruct((rows, 3 * d), BF16), shape, shape, shape],
        compiler_params=_params(("parallel", "parallel", "arbitrary")),
    )(dym, *us, *gs)
    return outs[0], outs[1], outs[2], outs[3]


def _out_proj_loss(ym, wo, x, target, name):
    m, d = x.shape
    bm, bn = _tile(m, 1024, 16), _tile(d, 1024)
    grid = (m // bm, d // bn)

    def body(a_ref, b_ref, x_ref, t_ref, dy_ref, dyb_ref, l_ref):
        first, _ = _grid_edges(grid)
        y = jnp.dot(a_ref[...], b_ref[...], preferred_element_type=F32) + x_ref[...]
        diff = y - t_ref[...]
        dy = diff * (1.0 / d)
        dy_ref[...] = dy
        dyb_ref[...] = dy.astype(BF16)
        sq = diff * diff
        part = sq[:, 0:LANES]
        for c in range(1, bn // LANES):
            part = part + sq[:, c * LANES:(c + 1) * LANES]
        part = jnp.sum(part.reshape(bm // 8, 8, LANES), axis=0)

        @pl.when(first)
        def _():
            l_ref[...] = part

        @pl.when(jnp.logical_not(first))
        def _():
            l_ref[...] += part

    tile = pl.BlockSpec((bm, bn), lambda i, j: (i, j))
    return pl.pallas_call(
        body, name=name, grid=grid,
        in_specs=[pl.BlockSpec((bm, d), lambda i, j: (i, 0)), pl.BlockSpec((d, bn), lambda i, j: (0, j)), tile, tile],
        out_specs=[tile, tile, pl.BlockSpec((8, LANES), lambda i, j: (0, 0))],
        out_shape=[jax.ShapeDtypeStruct((m, d), F32), jax.ShapeDtypeStruct((m, d), BF16),
                   jax.ShapeDtypeStruct((8, LANES), F32)],
        compiler_params=_params(("arbitrary", "arbitrary"), VMEM_BIG),
    )(ym, wo, x, target)


def _row(vec, reps=1):
    return jnp.tile(vec.reshape(1, -1).astype(F32), (1, reps))


def _local_step(x, mem, target, small, wg, shards=None):
    s, d = x.shape
    dist = shards is not None
    wg = dict(wg)
    ones = lambda n: jnp.ones((1, n), F32)
    zeros = lambda n: jnp.zeros((1, n), F32)
    scale_ab = HEAD_DIM ** -0.5
    split8 = lambda g: g.reshape(N_DEV, g.shape[0] // N_DEV, g.shape[1])
    flat8 = lambda g: g.reshape(g.shape[0] * g.shape[1], g.shape[2])
    gather = lambda names: _Comm("gather", [shards[n] for n in names]) if dist else None
    g = {}

    def scatter(names):
        return _Comm("scatter", [split8(g[n]) for n in names]) if dist else None

    def hosted(result, names, store):
        if not dist:
            return result
        out, got = result
        store.update(zip(names, got))
        return out

    hn = _rmsnorm_fwd(x, small["norm_gain"], "rms_x_fwd")
    got = {}
    proj = hosted(_mm_nn(hn, wg["qkv"], bm=1024, bn=1024, bk=d, o_dtype=BF16, name="proj_qkv",
                         comm=gather(("wa", "wb", "wc"))), ("wa", "wb", "wc"), got)
    wg.update({n: flat8(a) for n, a in got.items()})
    pfb = _mm_nn(hn, wg["wf"], bm=1024, bn=FB_PAD, bk=d, o_dtype=F32, name="proj_fb")
    mn = _rmsnorm_fwd(mem, small["mem_norm_gain"], "rms_mem_fwd")
    mkv = _mm_nn(mn, wg["wk"], bm=256, bn=1024, bk=d, o_dtype=F32, name="mem_kv")

    gain_a = jnp.concatenate([_row(small["q_gain_a"], A_Q_HEADS) * scale_ab, _row(small["k_gain_a"], A_KV_HEADS), ones(A_KV_WIDTH)], axis=1)
    flag_a = jnp.concatenate([ones(A_WIDTH + A_KV_WIDTH), zeros(A_KV_WIDTH)], axis=1)
    qkv_a = _headnorm_fwd(proj, COL_QA, 1280, 1280, HEAD_DIM, gain_a, flag_a, True, "hn_a_fwd")
    gain_b = jnp.concatenate([_row(small["q_gain_b"], B_HEADS) * scale_ab, _row(small["k_gain_b"], B_HEADS), ones(B_WIDTH)], axis=1)
    flag_b = jnp.concatenate([ones(2 * B_WIDTH), zeros(B_WIDTH)], axis=1)
    qkv_b = _headnorm_fwd(proj, COL_QB, 2304, 256, HEAD_DIM, gain_b, flag_b, True, "hn_b_fwd")
    gain_cq = _row(small["q_gain_c"], C_HEADS)
    q_c = _headnorm_fwd(proj, COL_QC, C_WIDTH, C_WIDTH, C_HEAD_DIM, gain_cq, ones(C_WIDTH), False, "hn_cq_fwd")
    gain_ck = jnp.concatenate([_row(small["k_gain_c"], C_HEADS), ones(C_WIDTH)], axis=1)
    flag_ck = jnp.concatenate([ones(C_WIDTH), zeros(C_WIDTH)], axis=1)
    mkvn = _headnorm_fwd(mkv, 0, 2 * C_WIDTH, 2 * C_WIDTH, C_HEAD_DIM, gain_ck, flag_ck, False, "hn_ck_fwd")

    q_a, k_a, v_a = qkv_a[0:12], qkv_a[12:16], qkv_a[16:20]
    q_b, k_b, v_b = qkv_b[0:12], qkv_b[12:24], qkv_b[24:36]

    bpad = jnp.pad(small["b_forget"].reshape(1, -1), ((0, 0), (0, FB_PAD - B_HEADS)))
    c16 = _fox_prep(pfb, bpad, "fox_prep")
    c3 = c16[0:B_HEADS].reshape(B_HEADS, 1, s)

    sinks = small["sinks_a"].reshape(-1)
    slopes = jnp.exp2(-8.0 * jnp.arange(1, A_Q_HEADS + 1, dtype=F32) / A_Q_HEADS)
    y_a, lse_a = _attn_a_fwd(q_a, k_a, v_a, sinks, slopes, "attn_a_fwd")
    y_b, lse_b, got_zg = _attn_b_fwd(q_b, k_b, v_b, c3, "attn_b_fwd", comm=gather(("zg",)))
    if dist:
        wg["zg"] = flat8(got_zg[0])
    y_c = _attn_c_fwd(q_c, mkvn, "attn_c_fwd")

    got = {}
    pzg = hosted(_mm_nn(hn, wg["zg"], bm=1024, bn=1024, bk=d, o_dtype=BF16, name="proj_zg", comm=gather(("wo",))),
                 ("wo",), got)
    wg.update({n: flat8(a) for n, a in got.items()})

    s_a = _gate_fwd(y_a, pzg, COL_ZA, 256, "gate_a_fwd")
    s_b = _gate_fwd(y_b, pzg, COL_ZB, 256, "gate_b_fwd")
    s_c = _gate_fwd(y_c, pzg, COL_ZC, 512, "gate_c_fwd")
    w_a, w_b, w_c = _branch_full(wg["wa"]), _branch_full(wg["wb"]), _branch_full(wg["wc"])
    u_a = _mm_nn(s_a, w_a, bm=1024, bn=2048, bk=A_WIDTH, o_dtype=BF16, name="branch_a_fwd")
    u_b = _mm_nn(s_b, w_b, bm=1024, bn=2048, bk=B_WIDTH, o_dtype=BF16, name="branch_b_fwd")
    u_c = _mm_nn(s_c, w_c, bm=1024, bn=2048, bk=C_WIDTH, o_dtype=BF16, name="branch_c_fwd")
    ym, gate_a, gate_b, gate_c = _merge_fwd(pzg, u_a, u_b, u_c, "merge_fwd")
    dy, dyb, lpart = _out_proj_loss(ym, wg["wo"], x, target, "out_proj_loss")
    loss = 0.5 / d * jnp.sum(lpart)

    dym = _mm_nt(dyb, wg["wo"], bm=1024, bn=1024, bk=d, o_dtype=F32, name="out_proj_bwd_act")
    g["wo"] = _mm_tn(ym, dyb, bm=512, bn=1024, bk=s, o_dtype=BF16, name="out_proj_bwd_w")

    dgate, du_a, du_b, du_c = _merge_bwd(dym, (u_a, u_b, u_c), (gate_a, gate_b, gate_c), "merge_bwd")
    parts = {}
    g["wm_g"] = hosted(_mm_tn(hn, dgate, bm=512, bn=1024, bk=s, o_dtype=BF16, name="proj_gate_bwd_w",
                              comm=scatter(("wo",))), ("wo",), parts)

    ds_a = _mm_nt(du_a, w_a, bm=1024, bn=A_WIDTH, bk=d, o_dtype=F32, name="branch_a_bwd_act")
    ds_b = _mm_nt(du_b, w_b, bm=1024, bn=B_WIDTH, bk=d, o_dtype=F32, name="branch_b_bwd_act")
    ds_c = _mm_nt(du_c, w_c, bm=1024, bn=C_WIDTH, bk=d, o_dtype=F32, name="branch_c_bwd_act")
    g["wa"] = _branch_shards(_mm_tn(s_a, du_a, bm=A_WIDTH, bn=1024, bk=s, o_dtype=BF16, name="branch_a_bwd_w"))
    g["wb"] = _branch_shards(_mm_tn(s_b, du_b, bm=B_WIDTH, bn=1024, bk=s, o_dtype=BF16, name="branch_b_bwd_w"))
    g["wc"] = _branch_shards(_mm_tn(s_c, du_c, bm=C_WIDTH, bn=1024, bk=s, o_dtype=BF16, name="branch_c_bwd_w"))

    dz = lax.empty((s, W_Z), BF16)
    dz, do_a, dd_a = _gate_bwd(ds_a, y_a, pzg, COL_ZA, 256, dz, COL_ZA, True, "gate_a_bwd")
    dz, do_b, dd_b = _gate_bwd(ds_b, y_b, pzg, COL_ZB, 256, dz, COL_ZB, True, "gate_b_bwd")
    dz, do_c, _ = _gate_bwd(ds_c, y_c, pzg, COL_ZC, 512, dz, COL_ZC, False, "gate_c_bwd")
    g["wm_z"] = _mm_tn(hn, dz, bm=512, bn=1024, bk=s, o_dtype=BF16, name="proj_z_bwd_w")

    names = ("wa", "wb", "wc")
    dq_a, dkv_a, dsink, got = _attn_a_bwd(q_a, k_a, v_a, do_a, lse_a, dd_a, sinks, slopes, "attn_a_bwd", comm=scatter(names))
    parts.update(zip(names, got))
    names = ("wm_g", "wm_z")
    dq_b, dk_b, dv_b, dc3, got = _attn_b_bwd(q_b, k_b, v_b, do_b, lse_b, dd_b, c3, "attn_b_bwd", comm=scatter(names))
    parts.update(zip(names, got))
    dq_c, dmkvn = _attn_c_bwd(q_c, mkvn, do_c, "attn_c_bwd")

    dqkv = lax.empty((s, W_QKV), BF16)
    dqkv, dg_qa = _headnorm_bwd(proj, COL_QA, A_WIDTH, 256, HEAD_DIM, gain_a[:, 0:768], flag_a[:, 0:768], dq_a, dqkv, COL_QA, "hn_qa_bwd")
    dqkv, dg_kva = _headnorm_bwd(proj, COL_KA, 512, 256, HEAD_DIM, gain_a[:, 768:1280], flag_a[:, 768:1280], dkv_a, dqkv, COL_KA, "hn_kva_bwd")
    dqkv, dg_qb = _headnorm_bwd(proj, COL_QB, B_WIDTH, 256, HEAD_DIM, gain_b[:, 0:768], flag_b[:, 0:768], dq_b, dqkv, COL_QB, "hn_qb_bwd")
    dqkv, dg_kb = _headnorm_bwd(proj, COL_KB, B_WIDTH, 256, HEAD_DIM, gain_b[:, 768:1536], flag_b[:, 768:1536], dk_b, dqkv, COL_KB, "hn_kb_bwd")
    dqkv, _ = _headnorm_bwd(proj, COL_VB, B_WIDTH, 256, HEAD_DIM, gain_b[:, 1536:2304], flag_b[:, 1536:2304], dv_b, dqkv, COL_VB, "hn_vb_bwd")
    dqkv, dg_qc = _headnorm_bwd(proj, COL_QC, C_WIDTH, 512, C_HEAD_DIM, gain_cq, ones(C_WIDTH), dq_c, dqkv, COL_QC, "hn_qc_bwd")
    dmkv, dg_kc = _headnorm_bwd(mkv, 0, 2 * C_WIDTH, 2 * C_WIDTH, C_HEAD_DIM, gain_ck, flag_ck, dmkvn, None, 0, "hn_kc_bwd")

    dct = jnp.pad(dc3.reshape(B_HEADS, s), ((0, 16 - B_HEADS), (0, 0)))
    dfb, dbf = _fox_prep_bwd(pfb, bpad, dct, "fox_prep_bwd")

    dmn = _mm_nt(dmkv, wg["wk"], bm=256, bn=1024, bk=1024, o_dtype=F32, name="mem_kv_bwd_act")
    g["wk"] = _mm_tn(mn, dmkv, bm=512, bn=1024, bk=mem.shape[0], o_dtype=BF16, name="mem_kv_bwd_w")
    _, dg_mem = _rmsnorm_bwd(mem, dmn, small["mem_norm_gain"], None, "rms_mem_bwd")

    g["wm_qkv"] = _mm_tn(hn, dqkv, bm=512, bn=1024, bk=s, o_dtype=BF16, name="proj_qkv_bwd_w")
    g["wf"] = _mm_tn(hn, dfb, bm=512, bn=FB_PAD, bk=s, o_dtype=BF16, name="proj_fb_bwd_w")
    half = W_QKV // 2
    g["wm_q1"], g["wm_q2"] = g["wm_qkv"][:, 0:half], g["wm_qkv"][:, half:W_QKV]
    names = ("wm_q1",)
    dhn = hosted(_mm_nt_sum([(dqkv, wg["qkv"], 0), (dfb, wg["wf"], 0)], bm=1024, bn=1024, bk=2048,
                            name="proj_qkv_bwd_act", comm=scatter(names)), names, parts)
    names = ("wm_q2", "wf", "wk")
    dhn = hosted(_mm_nt_sum([(dz, wg["zg"], COL_ZA), (dgate, wg["zg"], COL_GATE)], bm=1024, bn=1024, bk=2048,
                            name="proj_zg_bwd_act", add=dhn, comm=scatter(names)), names, parts)
    if dist:
        g = parts
    grad_x, dg_x = _rmsnorm_bwd(x, dhn, small["norm_gain"], dy, "rms_x_bwd")

    fold = lambda part, heads, hd: jnp.sum(jnp.sum(part, axis=0).reshape(heads, hd), axis=0).reshape(1, hd)
    small_grads = {
        "norm_gain": jnp.sum(dg_x, axis=0).reshape(1, d),
        "mem_norm_gain": jnp.sum(dg_mem, axis=0).reshape(1, d),
        "b_forget": dbf[0:B_HEADS, 0].reshape(1, B_HEADS),
        "q_gain_a": fold(dg_qa, A_Q_HEADS, HEAD_DIM) * scale_ab,
        "k_gain_a": fold(dg_kva[:, 0:A_KV_WIDTH], A_KV_HEADS, HEAD_DIM),
        "sinks_a": (jnp.sum(dsink, axis=(1, 2)) * (1.0 / HEAD_DIM)).reshape(1, A_Q_HEADS),
        "q_gain_b": fold(dg_qb, B_HEADS, HEAD_DIM) * scale_ab,
        "k_gain_b": fold(dg_kb, B_HEADS, HEAD_DIM),
        "q_gain_c": fold(dg_qc, C_HEADS, C_HEAD_DIM),
        "k_gain_c": fold(dg_kc[:, 0:C_WIDTH], C_HEADS, C_HEAD_DIM),
    }
    return loss, grad_x, small_grads, g


def _coords():
    return lax.axis_index("x"), lax.axis_index("y"), lax.axis_index("c")


def _all_gather(shards, name):
    n = len(shards)

    def body(*refs):
        ins = refs[0:n]
        outs = refs[n:2 * n]
        send_sems, recv_sems, local_sems = refs[2 * n:2 * n + 3]
        x, y, c = _coords()
        me, sibling = (x, y, c), (x, y, 1 - c)
        chips = [(1 - x, y), (x, 1 - y), (1 - x, 1 - y)]
        idx = lambda p: 4 * p[0] + 2 * p[1] + p[2]

        def copy(a, k, block, to, src=None):
            slot = outs[a].at[idx(block)]
            return pltpu.make_async_remote_copy(
                src_ref=slot if src is None else src, dst_ref=slot,
                send_sem=send_sems.at[a, k], recv_sem=recv_sems.at[a, k], device_id=to, device_id_type=MESH)

        mine = [pltpu.make_async_copy(ins[a], outs[a].at[idx(me)], local_sems.at[a]) for a in range(n)]
        for cp in mine:
            cp.start()
        first = []
        for a in range(n):
            first.append(copy(a, 0, me, sibling, src=ins[a]))
            first += [copy(a, 1 + j, me, (*chip, c), src=ins[a]) for j, chip in enumerate(chips)]
        for cp in first:
            cp.start()
        passed = []
        for j, chip in enumerate(chips):
            for a in range(n):
                copy(a, 1 + j, (*chip, c), me).wait_recv()
                fwd = copy(a, 4 + j, (*chip, c), sibling)
                fwd.start()
                passed.append(fwd)
        for a in range(n):
            copy(a, 0, sibling, me).wait_recv()
            for j, chip in enumerate(chips):
                copy(a, 4 + j, (*chip, 1 - c), me).wait_recv()
        for cp in first + passed:
            cp.wait_send()
        for cp in mine:
            cp.wait()

    any_spec = pl.BlockSpec(memory_space=pl.ANY)
    return pl.pallas_call(
        body, name=name,
        in_specs=[any_spec] * n, out_specs=[any_spec] * n,
        out_shape=[jax.ShapeDtypeStruct((N_DEV,) + sh.shape, sh.dtype) for sh in shards],
        scratch_shapes=[pltpu.SemaphoreType.DMA((n, 7)), pltpu.SemaphoreType.DMA((n, 7)), pltpu.SemaphoreType.DMA((n,))],
    )(*shards)


def _all_reduce_small(vec, name):
    p = vec.shape[1]

    def body(v_ref, o_ref, gather, send_sems, recv_sems):
        x, y, c = _coords()
        my = 4 * x + 2 * y + c
        peers = [(x ^ ((k >> 2) & 1), y ^ ((k >> 1) & 1), c ^ (k & 1)) for k in range(1, N_DEV)]
        gather[my] = v_ref[...]
        sends = [pltpu.make_async_remote_copy(
            src_ref=v_ref, dst_ref=gather.at[my], send_sem=send_sems.at[k], recv_sem=recv_sems.at[k],
            device_id=peer, device_id_type=MESH) for k, peer in enumerate(peers)]
        for cp in sends:
            cp.start()
        for k, peer in enumerate(peers):
            pid = 4 * peer[0] + 2 * peer[1] + peer[2]
            pltpu.make_async_remote_copy(
                src_ref=v_ref, dst_ref=gather.at[pid], send_sem=send_sems.at[k], recv_sem=recv_sems.at[k],
                device_id=peer, device_id_type=MESH).wait_recv()
        for cp in sends:
            cp.wait_send()
        total = gather[0]
        for j in range(1, N_DEV):
            total = total + gather[j]
        o_ref[...] = total

    vm = pl.BlockSpec(memory_space=pltpu.VMEM)
    return pl.pallas_call(
        body, name=name, in_specs=[vm], out_specs=vm,
        out_shape=jax.ShapeDtypeStruct((8, p), F32),
        scratch_shapes=[pltpu.VMEM((N_DEV, 8, p), F32), pltpu.SemaphoreType.DMA((7,)), pltpu.SemaphoreType.DMA((7,))],
    )(vec)[0:1]


def _sum_parts(parts, name):
    _, rows, cols = parts.shape
    br = _tile(rows, 64, 16)

    def body(p_ref, o_ref):
        total = p_ref[0].astype(F32)
        for j in range(1, N_DEV):
            total = total + p_ref[j].astype(F32)
        o_ref[...] = total

    return pl.pallas_call(
        body, name=name, grid=(rows // br,),
        in_specs=[pl.BlockSpec((N_DEV, br, cols), lambda i: (0, i, 0))],
        out_specs=pl.BlockSpec((br, cols), lambda i: (i, 0)),
        out_shape=jax.ShapeDtypeStruct((rows, cols), F32),
        compiler_params=_params(("parallel",), VMEM_BIG),
    )(parts)


def _adamw(w, g, m, v, name, br=32):
    rows, cols = w.shape
    br = min(br, rows)
    c1 = 1.0 / (1.0 - ADAM_B1 ** ADAM_STEP)
    c2 = 1.0 / (1.0 - ADAM_B2 ** ADAM_STEP)

    def body(w_ref, g_ref, m_ref, v_ref, d_ref, nm_ref, nv_ref):
        gv = g_ref[...]
        nm = ADAM_B1 * m_ref[...] + (1.0 - ADAM_B1) * gv
        nv = ADAM_B2 * v_ref[...] + (1.0 - ADAM_B2) * (gv * gv)
        d_ref[...] = -ADAM_LR * ((nm * c1) / (jnp.sqrt(nv * c2) + ADAM_EPS) + ADAM_WD * w_ref[...])
        nm_ref[...] = nm
        nv_ref[...] = nv

    spec = pl.BlockSpec((br, cols), lambda i: (i, 0))
    shape = jax.ShapeDtypeStruct((rows, cols), F32)
    return pl.pallas_call(
        body, name=name, grid=(pl.cdiv(rows, br),), in_specs=[spec] * 4, out_specs=[spec] * 3, out_shape=[shape] * 3,
        compiler_params=_params(("parallel",), VMEM_BIG),
    )(w, g, m, v)


def _adamw_parts(w, parts, m, v, name):
    rows, cols = w.shape
    br = _tile(rows, 32, 16)
    c1 = 1.0 / (1.0 - ADAM_B1 ** ADAM_STEP)
    c2 = 1.0 / (1.0 - ADAM_B2 ** ADAM_STEP)

    def body(w_ref, p_ref, m_ref, v_ref, g_ref, d_ref, nm_ref, nv_ref):
        gv = p_ref[0].astype(F32)
        for j in range(1, N_DEV):
            gv = gv + p_ref[j].astype(F32)
        nm = ADAM_B1 * m_ref[...] + (1.0 - ADAM_B1) * gv
        nv = ADAM_B2 * v_ref[...] + (1.0 - ADAM_B2) * (gv * gv)
        g_ref[...] = gv
        d_ref[...] = -ADAM_LR * ((nm * c1) / (jnp.sqrt(nv * c2) + ADAM_EPS) + ADAM_WD * w_ref[...])
        nm_ref[...] = nm
        nv_ref[...] = nv

    spec = pl.BlockSpec((br, cols), lambda i: (i, 0))
    shape = jax.ShapeDtypeStruct((rows, cols), F32)
    return pl.pallas_call(
        body, name=name, grid=(rows // br,),
        in_specs=[spec, pl.BlockSpec((N_DEV, br, cols), lambda i: (0, i, 0)), spec, spec],
        out_specs=[spec] * 4, out_shape=[shape] * 4,
        compiler_params=_params(("parallel",), VMEM_BIG),
    )(w, parts, m, v)


SMALL_NAMES = ("norm_gain", "mem_norm_gain", "b_forget", "q_gain_a", "k_gain_a", "sinks_a",
               "q_gain_b", "k_gain_b", "q_gain_c", "k_gain_c")
BIG_NAMES = ("w_in", "w_mem_kv", "w_branch_a", "w_branch_b", "w_branch_c", "w_out")
WEIGHT_ORDER = ("norm_gain", "mem_norm_gain", "w_in", "b_forget", "q_gain_a", "k_gain_a", "sinks_a", "q_gain_b",
                "k_gain_b", "q_gain_c", "k_gain_c", "w_mem_kv", "w_branch_a", "w_branch_b", "w_branch_c", "w_out")


def _pack_small(tree):
    flat = jnp.concatenate([tree[n].reshape(1, -1) for n in SMALL_NAMES], axis=1)
    pad = (-flat.shape[1]) % LANES
    return jnp.pad(flat, ((0, 0), (0, pad)))


def _unpack_small(flat, like):
    out, off = {}, 0
    for n in SMALL_NAMES:
        size = like[n].size
        out[n] = flat[:, off:off + size].reshape(like[n].shape)
        off += size
    return out


def kernel(x, mem, norm_gain, mem_norm_gain, w_in, b_forget, q_gain_a, k_gain_a, sinks_a, q_gain_b, k_gain_b, q_gain_c, k_gain_c, w_mem_kv, w_branch_a, w_branch_b, w_branch_c, w_out, loss_target, m_norm_gain, m_mem_norm_gain, m_w_in, m_b_forget, m_q_gain_a, m_k_gain_a, m_sinks_a, m_q_gain_b, m_k_gain_b, m_q_gain_c, m_k_gain_c, m_w_mem_kv, m_w_branch_a, m_w_branch_b, m_w_branch_c, m_w_out, v_norm_gain, v_mem_norm_gain, v_w_in, v_b_forget, v_q_gain_a, v_k_gain_a, v_sinks_a, v_q_gain_b, v_k_gain_b, v_q_gain_c, v_k_gain_c, v_w_mem_kv, v_w_branch_a, v_w_branch_b, v_w_branch_c, v_w_out):
    weights = dict(norm_gain=norm_gain, mem_norm_gain=mem_norm_gain, w_in=w_in, b_forget=b_forget, q_gain_a=q_gain_a,
                   k_gain_a=k_gain_a, sinks_a=sinks_a, q_gain_b=q_gain_b, k_gain_b=k_gain_b, q_gain_c=q_gain_c,
                   k_gain_c=k_gain_c, w_mem_kv=w_mem_kv, w_branch_a=w_branch_a, w_branch_b=w_branch_b,
                   w_branch_c=w_branch_c, w_out=w_out)
    mom_m = dict(norm_gain=m_norm_gain, mem_norm_gain=m_mem_norm_gain, w_in=m_w_in, b_forget=m_b_forget,
                 q_gain_a=m_q_gain_a, k_gain_a=m_k_gain_a, sinks_a=m_sinks_a, q_gain_b=m_q_gain_b, k_gain_b=m_k_gain_b,
                 q_gain_c=m_q_gain_c, k_gain_c=m_k_gain_c, w_mem_kv=m_w_mem_kv, w_branch_a=m_w_branch_a,
                 w_branch_b=m_w_branch_b, w_branch_c=m_w_branch_c, w_out=m_w_out)
    mom_v = dict(norm_gain=v_norm_gain, mem_norm_gain=v_mem_norm_gain, w_in=v_w_in, b_forget=v_b_forget,
                 q_gain_a=v_q_gain_a, k_gain_a=v_k_gain_a, sinks_a=v_sinks_a, q_gain_b=v_q_gain_b, k_gain_b=v_k_gain_b,
                 q_gain_c=v_q_gain_c, k_gain_c=v_k_gain_c, w_mem_kv=v_w_mem_kv, w_branch_a=v_w_branch_a,
                 w_branch_b=v_w_branch_b, w_branch_c=v_w_branch_c, w_out=v_w_out)
    wi = w_in[0]
    sh_qkv = jnp.concatenate([wi[:, a:b] for a, b in SRC_RANGES[0:3]], axis=1).astype(BF16)
    sh_zg = jnp.concatenate([wi[:, a:b] for a, b in SRC_RANGES[3:6]] + [wi[:, SRC_GATE:]], axis=1).astype(BF16)
    sh_wf = jnp.pad(wi[:, FB_SRC:FB_SRC + B_HEADS], ((0, 0), (0, FB_PAD - B_HEADS))).astype(BF16)
    shards = {"zg": sh_zg, "wo": w_out[0].astype(BF16), "wa": w_branch_a[0].astype(BF16),
              "wb": w_branch_b[0].astype(BF16), "wc": w_branch_c[0].astype(BF16)}
    first = ("qkv", "wf", "wk")
    full = _all_gather([sh_qkv, sh_wf, w_mem_kv[0].astype(BF16)], "weights_all_gather")
    wg = {kname: arr.reshape(arr.shape[0] * arr.shape[1], arr.shape[2]) for kname, arr in zip(first, full)}

    small = {n: weights[n] for n in SMALL_NAMES}
    loss_local, grad_x, small_g, parts = _local_step(x[0], mem[0], loss_target[0], small, wg, shards)

    grads, delta, new_m, new_v = {}, {}, {}, {}
    for n, kname in (("w_mem_kv", "wk"), ("w_out", "wo"), ("w_branch_a", "wa"), ("w_branch_b", "wb"), ("w_branch_c", "wc")):
        gsum, dlt, nm, nv = _adamw_parts(weights[n][0], parts[kname], mom_m[n][0], mom_v[n][0], "adamw_" + n)
        grads[n], delta[n], new_m[n], new_v[n] = gsum, dlt[None], nm[None], nv[None]
    g1, g2, gz, gf, gg = (_sum_parts(parts[k], "grad_sum_" + k) for k in ("wm_q1", "wm_q2", "wm_z", "wf", "wm_g"))
    half = W_QKV // 2
    g_in = jnp.concatenate([g1[:, COL_QA:COL_QB], gz[:, COL_ZA:COL_ZB], g1[:, COL_QB:half], g2[:, 0:COL_QC - half],
                            gz[:, COL_ZB:COL_ZC], gf[:, 0:B_HEADS], g2[:, COL_QC - half:half], gz[:, COL_ZC:W_Z], gg], axis=1)
    dlt, nm, nv = _adamw(w_in[0], g_in, m_w_in[0], v_w_in[0], "adamw_w_in")
    grads["w_in"], delta["w_in"], new_m["w_in"], new_v["w_in"] = g_in, dlt[None], nm[None], nv[None]

    packed = _pack_small(small_g)
    reduced = _all_reduce_small(jnp.broadcast_to(packed, (8, packed.shape[1])), "small_all_reduce")
    grads.update(_unpack_small(reduced, small))

    loss = lax.psum(loss_local, ("x", "y", "c"))

    pw, pm, pv = _pack_small(small), _pack_small({n: mom_m[n] for n in SMALL_NAMES}), _pack_small({n: mom_v[n] for n in SMALL_NAMES})
    rep8 = lambda a: jnp.broadcast_to(a, (8, a.shape[1]))
    dlt, nm, nv = _adamw(rep8(pw), rep8(reduced), rep8(pm), rep8(pv), "adamw_small")
    for tree, flat in ((delta, dlt), (new_m, nm), (new_v, nv)):
        tree.update(_unpack_small(flat[0:1], small))
    for n in BIG_NAMES:
        grads[n] = grads[n][None]
    return (loss, grad_x[None], *[grads[n] for n in WEIGHT_ORDER], *[delta[n] for n in WEIGHT_ORDER],
            *[new_m[n] for n in WEIGHT_ORDER], *[new_v[n] for n in WEIGHT_ORDER])
```

```python
import math

import jax
import jax.numpy as jnp
import numpy as np
from jax import lax
from jax.experimental import pallas as pl
from jax.experimental.pallas import tpu as pltpu

F32 = jnp.float32
BF16 = jnp.bfloat16

N_DEV = 8
HEAD_DIM = 64
A_Q_HEADS = 12
A_KV_HEADS = 4
A_GROUP = 3
B_HEADS = 12
C_HEADS = 4
C_HEAD_DIM = 128
WINDOW = 128
A_WIDTH = 768
A_KV_WIDTH = 256
B_WIDTH = 768
C_WIDTH = 512
EPS = 1e-6
NEG = -1e30

COL_QA, COL_KA, COL_VA = 0, 768, 1024
COL_QB, COL_KB, COL_VB = 1280, 2048, 2816
COL_QC = 3584
W_QKV = 4096
COL_ZA, COL_ZB, COL_ZC = 0, 768, 1536
COL_GATE = W_Z = 2048
SRC_RANGES = ((0, 1280), (2048, 4352), (5132, 5644), (1280, 2048), (4352, 5120), (5644, 6156))
SRC_GATE = 6156
FB_SRC = 5120
FB_PAD = 128

ADAM_LR = 0.001
ADAM_B1 = 0.9
ADAM_B2 = 0.999
ADAM_EPS = 1e-08
ADAM_WD = 0.01
ADAM_STEP = 10

VMEM_BIG = 52 * 1024 * 1024
LANES = 128
MESH = pl.DeviceIdType.MESH


def _tile(n, pref, mult=128):
    if n <= pref:
        return n
    t = (pref // mult) * mult
    while t >= mult:
        if n % t == 0:
            return t
        t -= mult
    return n


def _params(sem=None, vmem=None):
    kw = {}
    if sem is not None:
        kw["dimension_semantics"] = sem
    if vmem is not None:
        kw["vmem_limit_bytes"] = vmem
    return pltpu.CompilerParams(**kw)


def _sigmoid(x):
    return 1.0 / (1.0 + jnp.exp(-x))


def _block_diag(hd):
    r = np.arange(LANES)
    return jnp.asarray((r[:, None] // hd) == (r[None, :] // hd), dtype=BF16)


def _seg_sum(t, bd):
    hi = t.astype(BF16)
    lo = (t - hi.astype(F32)).astype(BF16)
    outs = []
    for c in range(t.shape[1] // LANES):
        sl = slice(c * LANES, (c + 1) * LANES)
        outs.append(jnp.dot(hi[:, sl], bd, preferred_element_type=F32) + jnp.dot(lo[:, sl], bd, preferred_element_type=F32))
    return outs[0] if len(outs) == 1 else jnp.concatenate(outs, axis=1)


def _rmsnorm_fwd(x, gain, name):
    rows, d = x.shape
    bm = _tile(rows, 512, 8)

    def body(x_ref, g_ref, o_ref):
        xv = x_ref[...]
        ms = jnp.mean(xv * xv, axis=-1, keepdims=True)
        o_ref[...] = (xv * lax.rsqrt(ms + EPS) * g_ref[...]).astype(BF16)

    return pl.pallas_call(
        body, name=name, grid=(rows // bm,),
        in_specs=[pl.BlockSpec((bm, d), lambda i: (i, 0)), pl.BlockSpec((1, d), lambda i: (0, 0))],
        out_specs=pl.BlockSpec((bm, d), lambda i: (i, 0)),
        out_shape=jax.ShapeDtypeStruct((rows, d), BF16),
        compiler_params=_params(("parallel",)),
    )(x, gain)


def _rmsnorm_bwd(x, dhn, gain, dy, name):
    rows, d = x.shape
    bm = _tile(rows, 512, 8)
    with_dx = dy is not None

    def body(*refs):
        if with_dx:
            x_ref, dh_ref, g_ref, dy_ref, gx_ref, dg_ref = refs
        else:
            x_ref, dh_ref, g_ref, dg_ref = refs
        i = pl.program_id(0)
        xv = x_ref[...]
        rstd = lax.rsqrt(jnp.mean(xv * xv, axis=-1, keepdims=True) + EPS)
        xhat = xv * rstd
        dh = dh_ref[...]
        part = jnp.sum((dh * xhat).reshape(bm // 8, 8, d), axis=0)

        @pl.when(i == 0)
        def _():
            dg_ref[...] = part

        @pl.when(i > 0)
        def _():
            dg_ref[...] += part

        if with_dx:
            g = dh * g_ref[...]
            mean = jnp.mean(g * xhat, axis=-1, keepdims=True)
            gx_ref[...] = dy_ref[...] + rstd * (g - xhat * mean)

    row_spec = pl.BlockSpec((bm, d), lambda i: (i, 0))
    in_specs = [row_spec, row_spec, pl.BlockSpec((1, d), lambda i: (0, 0))]
    args = [x, dhn, gain]
    dg_spec = pl.BlockSpec((8, d), lambda i: (0, 0))
    dg_shape = jax.ShapeDtypeStruct((8, d), F32)
    if with_dx:
        in_specs.append(row_spec)
        args.append(dy)
        out_specs = [row_spec, dg_spec]
        out_shape = [jax.ShapeDtypeStruct((rows, d), F32), dg_shape]
    else:
        out_specs = [dg_spec]
        out_shape = [dg_shape]
    outs = pl.pallas_call(
        body, name=name, grid=(rows // bm,), in_specs=in_specs, out_specs=out_specs, out_shape=out_shape,
        compiler_params=_params(("arbitrary",), VMEM_BIG),
    )(*args)
    return outs if with_dx else (None, outs[0])


class _Comm:
    def __init__(self, kind, arrays):
        self.kind = kind
        self.arrays = list(arrays)
        self.n = len(self.arrays)

    def out_shapes(self):
        if self.kind == "gather":
            return [jax.ShapeDtypeStruct((N_DEV,) + a.shape, a.dtype) for a in self.arrays]
        return [jax.ShapeDtypeStruct(a.shape, a.dtype) for a in self.arrays]

    def scratch(self):
        return [pltpu.SemaphoreType.DMA((self.n, N_DEV - 1)), pltpu.SemaphoreType.DMA((self.n, N_DEV - 1)),
                pltpu.SemaphoreType.DMA((self.n,))]

    def _plan(self, ins, outs, sems, with_recvs):
        send_sems, recv_sems, local_sems = sems
        x, y, c = lax.axis_index("x"), lax.axis_index("y"), lax.axis_index("c")
        my = 4 * x + 2 * y + c
        gather = self.kind == "gather"
        local, sends, recvs = [], [], []
        for a in range(self.n):
            local.append(pltpu.make_async_copy(ins[a] if gather else ins[a].at[my], outs[a].at[my], local_sems.at[a]))
            for k in range(1, N_DEV):
                peer = (x ^ ((k >> 2) & 1), y ^ ((k >> 1) & 1), c ^ (k & 1))
                pid = 4 * peer[0] + 2 * peer[1] + peer[2]
                src = ins[a] if gather else ins[a].at[pid]
                sem = dict(send_sem=send_sems.at[a, k - 1], recv_sem=recv_sems.at[a, k - 1], device_id=peer, device_id_type=MESH)
                sends.append(pltpu.make_async_remote_copy(src_ref=src, dst_ref=outs[a].at[my], **sem))
                if with_recvs:
                    recvs.append(pltpu.make_async_remote_copy(src_ref=src, dst_ref=outs[a].at[pid], **sem))
        return local, sends, recvs

    def start(self, ins, outs, sems):
        local, sends, _ = self._plan(ins, outs, sems, False)
        for cp in local + sends:
            cp.start()

    def wait(self, ins, outs, sems):
        local, sends, recvs = self._plan(ins, outs, sems, True)
        for cp in recvs:
            cp.wait_recv()
        for cp in sends:
            cp.wait_send()
        for cp in local:
            cp.wait()


def _grid_edges(grid):
    first = last = None
    for ax, size in enumerate(grid):
        pid = pl.program_id(ax)
        f, l = pid == 0, pid == size - 1
        first = f if first is None else first & f
        last = l if last is None else last & l
    return first, last


def _hosted_call(body, comm, *, name, grid, in_specs, out_specs, out_shape, scratch_shapes, args, sem, vmem=None):
    in_specs, out_specs, out_shape, scratch_shapes = list(in_specs), list(out_specs), list(out_shape), list(scratch_shapes)
    if comm is None:
        res = pl.pallas_call(body, name=name, grid=grid, in_specs=in_specs, out_specs=out_specs, out_shape=out_shape,
                             scratch_shapes=scratch_shapes, compiler_params=_params(sem, vmem))(*args)
        return list(res), []
    n_in, n_out, n_scr, nc = len(in_specs), len(out_shape), len(scratch_shapes), comm.n

    def hosted(*refs):
        ins = refs[0:n_in]
        comm_in = refs[n_in:n_in + nc]
        outs = refs[n_in + nc:n_in + nc + n_out]
        comm_out = refs[n_in + nc + n_out:n_in + 2 * nc + n_out]
        scr = refs[n_in + 2 * nc + n_out:n_in + 2 * nc + n_out + n_scr]
        sems = refs[n_in + 2 * nc + n_out + n_scr:]
        first, last = _grid_edges(grid)

        @pl.when(first)
        def _():
            comm.start(comm_in, comm_out, sems)

        body(*ins, *outs, *scr)

        @pl.when(last)
        def _():
            comm.wait(comm_in, comm_out, sems)

    any_spec = pl.BlockSpec(memory_space=pl.ANY)
    res = pl.pallas_call(
        hosted, name=name, grid=grid, in_specs=in_specs + [any_spec] * nc, out_specs=out_specs + [any_spec] * nc,
        out_shape=out_shape + comm.out_shapes(), scratch_shapes=scratch_shapes + comm.scratch(),
        compiler_params=_params(("arbitrary",) * len(grid), vmem),
    )(*args, *comm.arrays)
    return list(res[0:n_out]), list(res[n_out:])


def _mm(a, b, *, grid, a_spec, b_spec, o_spec, o_shape, o_dtype, contract, name, add=None, add_spec=None, acc_shape=None,
        comm=None):
    nk = grid[2]
    has_add = add is not None

    def body(*refs):
        a_ref, b_ref = refs[0], refs[1]
        add_ref = refs[2] if has_add else None
        o_ref = refs[3] if has_add else refs[2]
        part = lax.dot_general(a_ref[...], b_ref[...], (contract, ((), ())), preferred_element_type=F32)
        if nk == 1:
            if has_add:
                part = part + add_ref[...]
            o_ref[...] = part.astype(o_dtype)
        else:
            acc = refs[-1]
            k = pl.program_id(2)

            @pl.when(k == 0)
            def _():
                acc[...] = part

            @pl.when(k > 0)
            def _():
                acc[...] += part

            @pl.when(k == nk - 1)
            def _():
                r = acc[...]
                if has_add:
                    r = r + add_ref[...]
                o_ref[...] = r.astype(o_dtype)

    in_specs = [a_spec, b_spec] + ([add_spec] if has_add else [])
    args = [a, b] + ([add] if has_add else [])
    scratch = [pltpu.VMEM(acc_shape, F32)] if nk > 1 else []
    outs, comm_outs = _hosted_call(
        body, comm, name=name, grid=grid, in_specs=in_specs, out_specs=[o_spec],
        out_shape=[jax.ShapeDtypeStruct(o_shape, o_dtype)], scratch_shapes=scratch, args=args,
        sem=("parallel", "parallel", "arbitrary"), vmem=VMEM_BIG)
    return outs[0] if comm is None else (outs[0], comm_outs)


def _mm_nn(a, b, *, bm, bn, bk, o_dtype, name, add=None, comm=None):
    m, kd = a.shape
    n = b.shape[1]
    bm, bn, bk = _tile(m, bm, 8), _tile(n, bn), _tile(kd, bk)
    o_spec = pl.BlockSpec((bm, bn), lambda i, j, k: (i, j))
    return _mm(a, b, grid=(m // bm, n // bn, kd // bk),
               a_spec=pl.BlockSpec((bm, bk), lambda i, j, k: (i, k)),
               b_spec=pl.BlockSpec((bk, bn), lambda i, j, k: (k, j)),
               o_spec=o_spec, o_shape=(m, n), o_dtype=o_dtype, contract=((1,), (0,)), name=name,
               add=add, add_spec=o_spec, acc_shape=(bm, bn), comm=comm)


def _mm_nt(a, b, *, bm, bn, bk, o_dtype, name, add=None, b_col0=0, comm=None):
    m, kd = a.shape
    n = b.shape[0]
    bm, bn, bk = _tile(m, bm, 8), _tile(n, bn), _tile(math.gcd(kd, b_col0), bk)
    kb0 = b_col0 // bk
    o_spec = pl.BlockSpec((bm, bn), lambda i, j, k: (i, j))
    return _mm(a, b, grid=(m // bm, n // bn, kd // bk),
               a_spec=pl.BlockSpec((bm, bk), lambda i, j, k: (i, k)),
               b_spec=pl.BlockSpec((bn, bk), lambda i, j, k: (j, kb0 + k)),
               o_spec=o_spec, o_shape=(m, n), o_dtype=o_dtype, contract=((1,), (1,)), name=name,
               add=add, add_spec=o_spec, acc_shape=(bm, bn), comm=comm)


def _mm_nt_sum(terms, *, bm, bn, bk, name, add=None, comm=None):
    m = terms[0][0].shape[0]
    n = terms[0][1].shape[0]
    bm, bn = _tile(m, bm, 8), _tile(n, bn)
    nt = (((1,), (1,)), ((), ()))
    plan, groups, start = [], [], 0
    for a, b, col0 in terms:
        kd = a.shape[1]
        tk = _tile(math.gcd(kd, col0), bk)
        steps = kd // tk
        last = groups[-1] if groups else None
        if last is not None and last[0] is b and last[4] == tk and (last[3] + last[2]) * tk == col0:
            last[2] += steps
        else:
            groups.append([b, start, steps, col0 // tk, tk])
        plan.append((start, steps, len(groups) - 1))
        start += steps
    nk = start
    nterm, ngroup, has_add = len(terms), len(groups), add is not None

    def body(*refs):
        a_refs, b_refs = refs[0:nterm], refs[nterm:nterm + ngroup]
        add_ref = refs[nterm + ngroup] if has_add else None
        o_ref, acc = refs[nterm + ngroup + has_add], refs[nterm + ngroup + has_add + 1]
        k = pl.program_id(2)
        for t, (s0, steps, grp) in enumerate(plan):
            @pl.when((k >= s0) & (k < s0 + steps))
            def _():
                part = lax.dot_general(a_refs[t][...], b_refs[grp][...], nt, preferred_element_type=F32)

                @pl.when(k == 0)
                def _():
                    acc[...] = part

                @pl.when(k > 0)
                def _():
                    acc[...] += part

        @pl.when(k == nk - 1)
        def _():
            o_ref[...] = acc[...] + add_ref[...] if has_add else acc[...]

    def a_spec(tk, s0, steps):
        return pl.BlockSpec((bm, tk), lambda i, j, k: (i, jnp.clip(k - s0, 0, steps - 1)))

    def b_spec(tk, s0, steps, off):
        return pl.BlockSpec((bn, tk), lambda i, j, k: (j, off + jnp.clip(k - s0, 0, steps - 1)))

    o_spec = pl.BlockSpec((bm, bn), lambda i, j, k: (i, j))
    in_specs = [a_spec(groups[grp][4], s0, steps) for s0, steps, grp in plan]
    in_specs += [b_spec(tk, s0, steps, cb0) for _, s0, steps, cb0, tk in groups]
    args = [a for a, _, _ in terms] + [grp[0] for grp in groups]
    if has_add:
        in_specs.append(o_spec)
        args.append(add)
    outs, comm_outs = _hosted_call(
        body, comm, name=name, grid=(m // bm, n // bn, nk), in_specs=in_specs,
        out_specs=[o_spec], out_shape=[jax.ShapeDtypeStruct((m, n), F32)],
        scratch_shapes=[pltpu.VMEM((bm, bn), F32)], args=args,
        sem=("parallel", "parallel", "arbitrary"), vmem=VMEM_BIG)
    return outs[0] if comm is None else (outs[0], comm_outs)


def _mm_tn(a, b, *, bm, bn, bk, o_dtype, name, comm=None):
    kd, m = a.shape
    n = b.shape[1]
    bm, bn, bk = _tile(m, bm), _tile(n, bn), _tile(kd, bk, 8)
    return _mm(a, b, grid=(m // bm, n // bn, kd // bk),
               a_spec=pl.BlockSpec((bk, bm), lambda i, j, k: (k, i)),
               b_spec=pl.BlockSpec((bk, bn), lambda i, j, k: (k, j)),
               o_spec=pl.BlockSpec((bm, bn), lambda i, j, k: (i, j)),
               o_shape=(m, n), o_dtype=o_dtype, contract=((0,), (0,)), name=name, acc_shape=(bm, bn), comm=comm)


def _branch_full(w8):
    kb, ds = w8.shape[0] // N_DEV, w8.shape[1]
    return w8.reshape(N_DEV, kb, ds).transpose(1, 0, 2).reshape(kb, N_DEV * ds)


def _branch_shards(g):
    kb, ds = g.shape[0], g.shape[1] // N_DEV
    return g.reshape(kb, N_DEV, ds).transpose(1, 0, 2).reshape(N_DEV * kb, ds)


def _headnorm_fwd(src, c0, width, bw, hd, gain, nflag, head_major, name):
    rows = src.shape[0]
    bm = _tile(rows, 2048 if bw <= 256 else 1024, 16)
    bd = _block_diag(hd)
    cb0 = c0 // bw

    def body(x_ref, g_ref, f_ref, bd_ref, o_ref):
        xv = x_ref[...].astype(F32)
        ss = _seg_sum(xv * xv, bd_ref[...])
        rstd = lax.rsqrt(ss * (1.0 / hd) + EPS)
        y = (xv * jnp.where(f_ref[...] > 0.0, rstd, 1.0) * g_ref[...]).astype(BF16)
        if head_major:
            for h in range(bw // HEAD_DIM):
                o_ref[h] = y[:, h * HEAD_DIM:(h + 1) * HEAD_DIM]
        else:
            o_ref[...] = y

    vec_spec = pl.BlockSpec((1, bw), lambda i, t: (0, t))
    if head_major:
        hpb = bw // HEAD_DIM
        out_spec = pl.BlockSpec((hpb, bm, HEAD_DIM), lambda i, t: (t, i, 0))
        out_shape = jax.ShapeDtypeStruct((width // HEAD_DIM, rows, HEAD_DIM), BF16)
    else:
        out_spec = pl.BlockSpec((bm, bw), lambda i, t: (i, t))
        out_shape = jax.ShapeDtypeStruct((rows, width), BF16)
    return pl.pallas_call(
        body, name=name, grid=(rows // bm, width // bw),
        in_specs=[pl.BlockSpec((bm, bw), lambda i, t: (i, cb0 + t)), vec_spec, vec_spec,
                  pl.BlockSpec((LANES, LANES), lambda i, t: (0, 0))],
        out_specs=out_spec, out_shape=out_shape,
        compiler_params=_params(("parallel", "parallel")),
    )(src, gain, nflag, bd)


def _headnorm_bwd(src, c0, width, bw, hd, gain, nflag, dyn, target, t0, name):
    rows = src.shape[0]
    bm = _tile(rows, 2048 if bw <= 256 else 1024, 16)
    bd = _block_diag(hd)
    cb0 = c0 // bw
    tb0 = t0 // bw
    aliased = target is not None

    def body(*refs):
        if aliased:
            x_ref, dy_ref, g_ref, f_ref, bd_ref, _, o_ref, dg_ref = refs
        else:
            x_ref, dy_ref, g_ref, f_ref, bd_ref, o_ref, dg_ref = refs
        i = pl.program_id(1)
        xv = x_ref[...].astype(F32)
        dyv = dy_ref[...]
        bdv = bd_ref[...]
        rstd = lax.rsqrt(_seg_sum(xv * xv, bdv) * (1.0 / hd) + EPS)
        xhat = xv * rstd
        g = dyv * g_ref[...]
        mean = _seg_sum(g * xhat, bdv) * (1.0 / hd)
        dx = jnp.where(f_ref[...] > 0.0, rstd * (g - xhat * mean), g)
        o_ref[...] = dx.astype(BF16)
        part = jnp.sum((dyv * xhat).reshape(bm // 8, 8, bw), axis=0)

        @pl.when(i == 0)
        def _():
            dg_ref[...] = part

        @pl.when(i > 0)
        def _():
            dg_ref[...] += part

    vec_spec = pl.BlockSpec((1, bw), lambda t, i: (0, t))
    in_specs = [pl.BlockSpec((bm, bw), lambda t, i: (i, cb0 + t)), pl.BlockSpec((bm, bw), lambda t, i: (i, t)),
                vec_spec, vec_spec, pl.BlockSpec((LANES, LANES), lambda t, i: (0, 0))]
    args = [src, dyn, gain, nflag, bd]
    aliases = {}
    if aliased:
        in_specs.append(pl.BlockSpec(memory_space=pl.ANY))
        args.append(target)
        aliases = {5: 0}
        o_shape = jax.ShapeDtypeStruct(target.shape, BF16)
    else:
        o_shape = jax.ShapeDtypeStruct((rows, width), BF16)
    out, dg = pl.pallas_call(
        body, name=name, grid=(width // bw, rows // bm), in_specs=in_specs,
        out_specs=[pl.BlockSpec((bm, bw), lambda t, i: (i, tb0 + t)), pl.BlockSpec((8, bw), lambda t, i: (0, t))],
        out_shape=[o_shape, jax.ShapeDtypeStruct((8, width), F32)],
        input_output_aliases=aliases,
        compiler_params=_params(("parallel", "arbitrary")),
    )(*args)
    return out, dg


def _fox_prep(pfb, bpad, name):
    s = pfb.shape[0]

    def body(p_ref, b_ref, c_ref):
        z = p_ref[...] + b_ref[...]
        logf = jnp.minimum(z, 0.0) - jnp.log(1.0 + jnp.exp(-jnp.abs(z)))
        x = logf.T[0:16, :]
        lane = lax.broadcasted_iota(jnp.int32, (16, s), 1)
        sh = 1
        while sh < s:
            x = x + jnp.where(lane >= sh, pltpu.roll(x, sh, 1), 0.0)
            sh *= 2
        c_ref[...] = x

    return pl.pallas_call(
        body, name=name, grid=(1,),
        in_specs=[pl.BlockSpec((s, FB_PAD), lambda i: (0, 0)), pl.BlockSpec((1, FB_PAD), lambda i: (0, 0))],
        out_specs=pl.BlockSpec((16, s), lambda i: (0, 0)),
        out_shape=jax.ShapeDtypeStruct((16, s), F32),
        compiler_params=_params(("arbitrary",)),
    )(pfb, bpad)


def _fox_prep_bwd(pfb, bpad, dct, name):
    s = pfb.shape[0]

    def body(p_ref, b_ref, dc_ref, df_ref, db_ref):
        zt = (p_ref[...] + b_ref[...]).T[0:16, :]
        y = dc_ref[...]
        lane = lax.broadcasted_iota(jnp.int32, (16, s), 1)
        sh = 1
        while sh < s:
            y = y + jnp.where(lane < s - sh, pltpu.roll(y, s - sh, 1), 0.0)
            sh *= 2
        dz = y * _sigmoid(-zt)
        db_ref[...] = jnp.broadcast_to(jnp.sum(dz, axis=1, keepdims=True), (16, FB_PAD))
        full = jnp.concatenate([dz, jnp.zeros((FB_PAD - 16, s), F32)], axis=0)
        df_ref[...] = full.T.astype(BF16)

    return pl.pallas_call(
        body, name=name, grid=(1,),
        in_specs=[pl.BlockSpec((s, FB_PAD), lambda i: (0, 0)), pl.BlockSpec((1, FB_PAD), lambda i: (0, 0)),
                  pl.BlockSpec((16, s), lambda i: (0, 0))],
        out_specs=[pl.BlockSpec((s, FB_PAD), lambda i: (0, 0)), pl.BlockSpec((16, FB_PAD), lambda i: (0, 0))],
        out_shape=[jax.ShapeDtypeStruct((s, FB_PAD), BF16), jax.ShapeDtypeStruct((16, FB_PAD), F32)],
        compiler_params=_params(("arbitrary",)),
    )(pfb, bpad, dct)


def _swa_window(n):
    ws = pl.multiple_of(jnp.maximum(n * WINDOW - WINDOW, 0), WINDOW)
    qi = lax.broadcasted_iota(jnp.int32, (WINDOW, 2 * WINDOW), 0)
    kj = lax.broadcasted_iota(jnp.int32, (WINDOW, 2 * WINDOW), 1)
    rel = qi + (n * WINDOW - ws) - kj
    valid = (rel >= 0) & (rel < WINDOW)
    return ws, valid, rel.astype(F32)


def _attn_a_fwd(q, k, v, sinks, slopes, name):
    s = q.shape[1]
    nb = s // WINDOW
    smem = pl.BlockSpec(memory_space=pltpu.SMEM)

    def body(sink_ref, slope_ref, q_ref, k_ref, v_ref, o_ref, lse_ref):
        n = pl.program_id(0)
        ws, valid, relf = _swa_window(n)
        outs = []
        for h in range(A_Q_HEADS):
            kvh = h // A_GROUP
            kw = k_ref[kvh, pl.ds(ws, 2 * WINDOW), :]
            vw = v_ref[kvh, pl.ds(ws, 2 * WINDOW), :]
            sc = lax.dot_general(q_ref[h], kw, (((1,), (1,)), ((), ())), preferred_element_type=F32)
            sc = jnp.where(valid, sc - slope_ref[h] * relf, NEG)
            sink = sink_ref[h]
            m = jnp.maximum(jnp.max(sc, axis=1, keepdims=True), sink)
            p = jnp.exp(sc - m)
            denom = jnp.sum(p, axis=1, keepdims=True) + jnp.exp(sink - m)
            pn = (p / denom).astype(BF16)
            outs.append(jnp.dot(pn, vw, preferred_element_type=F32))
            lse_ref[h] = jnp.broadcast_to(m + jnp.log(denom), (WINDOW, HEAD_DIM))
        o_ref[...] = jnp.concatenate(outs, axis=1)

    return pl.pallas_call(
        body, name=name, grid=(nb,),
        in_specs=[smem, smem,
                  pl.BlockSpec((A_Q_HEADS, WINDOW, HEAD_DIM), lambda n: (0, n, 0)),
                  pl.BlockSpec((A_KV_HEADS, s, HEAD_DIM), lambda n: (0, 0, 0)),
                  pl.BlockSpec((A_KV_HEADS, s, HEAD_DIM), lambda n: (0, 0, 0))],
        out_specs=[pl.BlockSpec((WINDOW, A_WIDTH), lambda n: (n, 0)),
                   pl.BlockSpec((A_Q_HEADS, WINDOW, HEAD_DIM), lambda n: (0, n, 0))],
        out_shape=[jax.ShapeDtypeStruct((s, A_WIDTH), F32), jax.ShapeDtypeStruct((A_Q_HEADS, s, HEAD_DIM), F32)],
        compiler_params=_params(("parallel",), VMEM_BIG),
    )(sinks, slopes, q, k, v)


def _attn_a_bwd(q, k, v, do, lse, dd, sinks, slopes, name, comm=None):
    s = q.shape[1]
    nb = s // WINDOW
    smem = pl.BlockSpec(memory_space=pltpu.SMEM)
    last = nb - 1

    def body(sink_ref, slope_ref, q_ref, k_ref, v_ref, do_ref, lse_ref, dd_ref, dq_ref, dkv_ref, ds_ref, carry):
        n = pl.program_id(0)

        @pl.when(n == 0)
        def _():
            carry[...] = jnp.zeros(carry.shape, F32)
            ds_ref[...] = jnp.zeros(ds_ref.shape, F32)

        @pl.when(n < nb)
        def _():
            ws, valid, relf = _swa_window(n)
            dqs = []
            dkw = [None] * A_KV_HEADS
            dvw = [None] * A_KV_HEADS
            for h in range(A_Q_HEADS):
                kvh = h // A_GROUP
                qh = q_ref[h]
                doh = do_ref[h]
                kw = k_ref[kvh, pl.ds(ws, 2 * WINDOW), :]
                vw = v_ref[kvh, pl.ds(ws, 2 * WINDOW), :]
                lse_h = lse_ref[h]
                dd_h = dd_ref[h]
                sc = lax.dot_general(qh, kw, (((1,), (1,)), ((), ())), preferred_element_type=F32)
                sc = jnp.where(valid, sc - slope_ref[h] * relf, NEG)
                p = jnp.exp(sc - lse_h[:, 0:1])
                dp = lax.dot_general(doh, vw, (((1,), (1,)), ((), ())), preferred_element_type=F32)
                dsc = (p * (dp - dd_h[:, 0:1])).astype(BF16)
                pb = p.astype(BF16)
                dqs.append(jnp.dot(dsc, kw, preferred_element_type=F32))
                dk_h = lax.dot_general(dsc, qh, (((0,), (0,)), ((), ())), preferred_element_type=F32)
                dv_h = lax.dot_general(pb, doh, (((0,), (0,)), ((), ())), preferred_element_type=F32)
                dkw[kvh] = dk_h if dkw[kvh] is None else dkw[kvh] + dk_h
                dvw[kvh] = dv_h if dvw[kvh] is None else dvw[kvh] + dv_h
                psink = jnp.exp(sink_ref[h] - lse_h)
                ds_ref[h] += jnp.sum((-psink * dd_h).reshape(WINDOW // 8, 8, HEAD_DIM), axis=0)
            dq_ref[...] = jnp.concatenate(dqs, axis=1)
            win = jnp.concatenate(dkw + dvw, axis=1)
            first = win[0:WINDOW]
            second = win[WINDOW:2 * WINDOW]
            dkv_ref[...] = carry[...] + first
            carry[...] = jnp.where(n == 0, first, second)

        @pl.when(n == nb)
        def _():
            dkv_ref[...] = carry[...]

    hm = lambda heads: pl.BlockSpec((heads, WINDOW, HEAD_DIM), lambda n: (0, jnp.minimum(n, last), 0))
    res = lambda heads: pl.BlockSpec((heads, s, HEAD_DIM), lambda n: (0, 0, 0))
    outs, comm_outs = _hosted_call(
        body, comm, name=name, grid=(nb + 1,),
        in_specs=[smem, smem, hm(A_Q_HEADS), res(A_KV_HEADS), res(A_KV_HEADS), hm(A_Q_HEADS), hm(A_Q_HEADS), hm(A_Q_HEADS)],
        out_specs=[pl.BlockSpec((WINDOW, A_WIDTH), lambda n: (jnp.minimum(n, last), 0)),
                   pl.BlockSpec((WINDOW, 2 * A_KV_WIDTH), lambda n: (jnp.maximum(n - 1, 0), 0)),
                   pl.BlockSpec((A_Q_HEADS, 8, HEAD_DIM), lambda n: (0, 0, 0))],
        out_shape=[jax.ShapeDtypeStruct((s, A_WIDTH), F32), jax.ShapeDtypeStruct((s, 2 * A_KV_WIDTH), F32),
                   jax.ShapeDtypeStruct((A_Q_HEADS, 8, HEAD_DIM), F32)],
        scratch_shapes=[pltpu.VMEM((WINDOW, 2 * A_KV_WIDTH), F32)],
        args=[sinks, slopes, q, k, v, do, lse, dd], sem=("arbitrary",), vmem=VMEM_BIG)
    return outs[0], outs[1], outs[2], comm_outs


def _attn_b_fwd(q, k, v, c3, name, comm=None):
    heads, s, _ = q.shape
    bq = min(512, s)
    nq = s // bq
    nt = (((1,), (1,)), ((), ()))

    def body(q_ref, k_ref, v_ref, c_ref, o_ref, lse_ref, m_scr, l_scr, acc_scr):
        i = pl.program_id(1)
        r0 = pl.multiple_of(i * bq, bq)
        row = lax.broadcasted_iota(jnp.int32, (bq, bq), 0)
        col = lax.broadcasted_iota(jnp.int32, (bq, bq), 1)
        m_scr[...] = jnp.full((2, bq, LANES), NEG, F32)
        l_scr[...] = jnp.zeros((2, bq, LANES), F32)
        acc_scr[...] = jnp.zeros((2, bq, HEAD_DIM), F32)

        def step(j, masked):
            k0 = pl.multiple_of(j * bq, bq)
            for h2 in range(2):
                kv = k_ref[h2, pl.ds(k0, bq), :]
                vv = v_ref[h2, pl.ds(k0, bq), :]
                cq0 = c_ref[h2, :, pl.ds(r0, LANES)][:, 0:1]
                sc = lax.dot_general(q_ref[h2], kv, nt, preferred_element_type=F32)
                sc = sc + (cq0 - c_ref[h2, :, pl.ds(k0, bq)])
                if masked:
                    sc = jnp.where(col <= row, sc, NEG)
                m_prev = m_scr[h2]
                m_new = jnp.maximum(m_prev, jnp.max(sc, axis=1, keepdims=True))
                alpha = jnp.exp(m_prev - m_new)
                p = jnp.exp(sc - m_new[:, 0:1])
                l_scr[h2] = alpha * l_scr[h2] + jnp.sum(p, axis=1, keepdims=True)
                p_hi = p.astype(BF16)
                p_lo = (p - p_hi.astype(F32)).astype(BF16)
                pv = jnp.dot(p_hi, vv, preferred_element_type=F32) + jnp.dot(p_lo, vv, preferred_element_type=F32)
                acc_scr[h2] = acc_scr[h2] * alpha[:, 0:HEAD_DIM] + pv
                m_scr[h2] = m_new

        def loop_body(j, carry):
            step(j, False)
            return carry

        lax.fori_loop(0, i, loop_body, 0)
        step(i, True)
        outs = []
        for h2 in range(2):
            l = l_scr[h2]
            outs.append(acc_scr[h2] / l[:, 0:HEAD_DIM])
            lse_ref[h2] = (m_scr[h2] + jnp.log(l))[:, 0:HEAD_DIM]
        o_ref[...] = jnp.concatenate(outs, axis=1)

    res = pl.BlockSpec((2, s, HEAD_DIM), lambda hp, i: (hp, 0, 0))
    outs, comm_outs = _hosted_call(
        body, comm, name=name, grid=(heads // 2, nq),
        in_specs=[pl.BlockSpec((2, bq, HEAD_DIM), lambda hp, i: (hp, i, 0)), res, res,
                  pl.BlockSpec((2, 1, s), lambda hp, i: (hp, 0, 0))],
        out_specs=[pl.BlockSpec((bq, 2 * HEAD_DIM), lambda hp, i: (i, hp)),
                   pl.BlockSpec((2, bq, HEAD_DIM), lambda hp, i: (hp, i, 0))],
        out_shape=[jax.ShapeDtypeStruct((s, heads * HEAD_DIM), F32), jax.ShapeDtypeStruct((heads, s, HEAD_DIM), F32)],
        scratch_shapes=[pltpu.VMEM((2, bq, LANES), F32), pltpu.VMEM((2, bq, LANES), F32), pltpu.VMEM((2, bq, HEAD_DIM), F32)],
        args=[q, k, v, c3], sem=("parallel", "parallel"), vmem=VMEM_BIG)
    return outs[0], outs[1], comm_outs


def _attn_b_bwd(q, k, v, do, lse, dd, c3, name, comm=None):
    heads, s, _ = q.shape
    bq = min(512, s)
    nq = s // bq
    nt = (((1,), (1,)), ((), ()))
    tn = (((0,), (0,)), ((), ()))
    grid = (heads // 2, nq)

    def body(q_ref, k_ref, v_ref, do_ref, lse_ref, dd_ref, c_ref, dq_ref, dk_ref, dv_ref, dc_ref,
             dq_scr, dk_scr, dv_scr, dc_scr):
        j = pl.program_id(1)
        k0 = pl.multiple_of(j * bq, bq)
        row = lax.broadcasted_iota(jnp.int32, (bq, bq), 0)
        col = lax.broadcasted_iota(jnp.int32, (bq, bq), 1)

        @pl.when(j == 0)
        def _():
            dq_scr[...] = jnp.zeros(dq_scr.shape, F32)

        dk_scr[...] = jnp.zeros((2, bq, HEAD_DIM), F32)
        dv_scr[...] = jnp.zeros((2, bq, HEAD_DIM), F32)
        dc_scr[...] = jnp.zeros((2, 1, bq), F32)

        def step(i, masked):
            r0 = pl.multiple_of(i * bq, bq)
            for h2 in range(2):
                kv = k_ref[h2]
                vv = v_ref[h2]
                qv = q_ref[h2, pl.ds(r0, bq), :]
                dov = do_ref[h2, pl.ds(r0, bq), :]
                lse_v = lse_ref[h2, pl.ds(r0, bq), :][:, 0:1]
                dd_v = dd_ref[h2, pl.ds(r0, bq), :][:, 0:1]
                cq0 = c_ref[h2, :, pl.ds(r0, LANES)][:, 0:1]
                sc = lax.dot_general(qv, kv, nt, preferred_element_type=F32) + (cq0 - c_ref[h2, :, pl.ds(k0, bq)])
                if masked:
                    sc = jnp.where(col <= row, sc, NEG)
                p = jnp.exp(sc - lse_v)
                dp = lax.dot_general(dov, vv, nt, preferred_element_type=F32)
                dsc = p * (dp - dd_v)
                dsb = dsc.astype(BF16)
                dv_scr[h2] += lax.dot_general(p.astype(BF16), dov, tn, preferred_element_type=F32)
                dk_scr[h2] += lax.dot_general(dsb, qv, tn, preferred_element_type=F32)
                dq_scr[h2, pl.ds(r0, bq), :] += jnp.dot(dsb, kv, preferred_element_type=F32)
                dc_scr[h2] -= jnp.sum(dsc, axis=0, keepdims=True)

        def loop_body(i, carry):
            step(i, False)
            return carry

        step(j, True)
        lax.fori_loop(j + 1, nq, loop_body, 0)
        dc_ref[...] = dc_scr[...]
        dk_ref[...] = jnp.concatenate([dk_scr[0], dk_scr[1]], axis=1)
        dv_ref[...] = jnp.concatenate([dv_scr[0], dv_scr[1]], axis=1)

        @pl.when(j == nq - 1)
        def _():
            dq_ref[...] = jnp.concatenate([dq_scr[0], dq_scr[1]], axis=1)

    res = pl.BlockSpec((2, s, HEAD_DIM), lambda hp, j: (hp, 0, 0))
    blk = pl.BlockSpec((2, bq, HEAD_DIM), lambda hp, j: (hp, j, 0))
    tm = jax.ShapeDtypeStruct((s, heads * HEAD_DIM), F32)
    in_specs = [res, blk, blk, res, res, res, pl.BlockSpec((2, 1, s), lambda hp, j: (hp, 0, 0))]
    out_specs = [pl.BlockSpec((s, 2 * HEAD_DIM), lambda hp, j: (0, hp)),
                 pl.BlockSpec((bq, 2 * HEAD_DIM), lambda hp, j: (j, hp)),
                 pl.BlockSpec((bq, 2 * HEAD_DIM), lambda hp, j: (j, hp)),
                 pl.BlockSpec((2, 1, bq), lambda hp, j: (hp, 0, j))]
    out_shape = [tm, tm, tm, jax.ShapeDtypeStruct((heads, 1, s), F32)]
    scratch = [pltpu.VMEM((2, s, HEAD_DIM), F32), pltpu.VMEM((2, bq, HEAD_DIM), F32),
               pltpu.VMEM((2, bq, HEAD_DIM), F32), pltpu.VMEM((2, 1, bq), F32)]
    outs, comm_outs = _hosted_call(
        body, comm, name=name, grid=grid, in_specs=in_specs, out_specs=out_specs, out_shape=out_shape,
        scratch_shapes=scratch, args=[q, k, v, do, lse, dd, c3], sem=("parallel", "arbitrary"), vmem=VMEM_BIG)
    return outs[0], outs[1], outs[2], outs[3], comm_outs


def _attn_c_probs(qh, mkh):
    sc = lax.dot_general(qh, mkh, (((1,), (1,)), ((), ())), preferred_element_type=F32) * (C_HEAD_DIM ** -0.5)
    p = jnp.exp(sc - jnp.max(sc, axis=1, keepdims=True))
    return p / jnp.sum(p, axis=1, keepdims=True)


def _attn_c_fwd(q, mkv, name):
    s = q.shape[0]
    m = mkv.shape[0]
    bq = _tile(s, 512, 8)

    def body(q_ref, mk_ref, mv_ref, o_ref):
        outs = []
        for h in range(C_HEADS):
            sl = slice(h * C_HEAD_DIM, (h + 1) * C_HEAD_DIM)
            pn = _attn_c_probs(q_ref[:, sl], mk_ref[:, sl]).astype(BF16)
            outs.append(jnp.dot(pn, mv_ref[:, sl], preferred_element_type=F32))
        o_ref[...] = jnp.concatenate(outs, axis=1)

    return pl.pallas_call(
        body, name=name, grid=(s // bq,),
        in_specs=[pl.BlockSpec((bq, C_WIDTH), lambda i: (i, 0)), pl.BlockSpec((m, C_WIDTH), lambda i: (0, 0)),
                  pl.BlockSpec((m, C_WIDTH), lambda i: (0, 1))],
        out_specs=pl.BlockSpec((bq, C_WIDTH), lambda i: (i, 0)),
        out_shape=jax.ShapeDtypeStruct((s, C_WIDTH), F32),
        compiler_params=_params(("parallel",)),
    )(q, mkv, mkv)


def _attn_c_bwd(q, mkv, do, name):
    s = q.shape[0]
    m = mkv.shape[0]
    bq = _tile(s, 512, 8)
    tn = (((0,), (0,)), ((), ()))

    def body(q_ref, mk_ref, mv_ref, do_ref, dq_ref, dm_ref):
        i = pl.program_id(0)

        @pl.when(i == 0)
        def _():
            dm_ref[...] = jnp.zeros(dm_ref.shape, F32)

        dqs = []
        for h in range(C_HEADS):
            sl = slice(h * C_HEAD_DIM, (h + 1) * C_HEAD_DIM)
            qh, mkh, mvh, doh = q_ref[:, sl], mk_ref[:, sl], mv_ref[:, sl], do_ref[:, sl]
            pn = _attn_c_probs(qh, mkh)
            dp = lax.dot_general(doh, mvh, (((1,), (1,)), ((), ())), preferred_element_type=F32)
            dsc = (pn * (dp - jnp.sum(pn * dp, axis=1, keepdims=True)) * (C_HEAD_DIM ** -0.5)).astype(BF16)
            dqs.append(jnp.dot(dsc, mkh, preferred_element_type=F32))
            dm_ref[:, sl] += lax.dot_general(dsc, qh, tn, preferred_element_type=F32)
            sv = slice(C_WIDTH + h * C_HEAD_DIM, C_WIDTH + (h + 1) * C_HEAD_DIM)
            dm_ref[:, sv] += lax.dot_general(pn.astype(BF16), doh, tn, preferred_element_type=F32)
        dq_ref[...] = jnp.concatenate(dqs, axis=1)

    row = pl.BlockSpec((bq, C_WIDTH), lambda i: (i, 0))
    return pl.pallas_call(
        body, name=name, grid=(s // bq,),
        in_specs=[row, pl.BlockSpec((m, C_WIDTH), lambda i: (0, 0)), pl.BlockSpec((m, C_WIDTH), lambda i: (0, 1)), row],
        out_specs=[row, pl.BlockSpec((m, 2 * C_WIDTH), lambda i: (0, 0))],
        out_shape=[jax.ShapeDtypeStruct((s, C_WIDTH), F32), jax.ShapeDtypeStruct((m, 2 * C_WIDTH), F32)],
        compiler_params=_params(("arbitrary",)),
    )(q, mkv, mkv, do)


def _gate_fwd(y, proj, zc0, bw, name):
    rows, width = y.shape
    bm = _tile(rows, 2048 if bw <= 256 else 1024, 16)
    cb0 = zc0 // bw

    def body(y_ref, z_ref, o_ref):
        z = z_ref[...].astype(F32)
        o_ref[...] = (y_ref[...] * (z * _sigmoid(z))).astype(BF16)

    return pl.pallas_call(
        body, name=name, grid=(rows // bm, width // bw),
        in_specs=[pl.BlockSpec((bm, bw), lambda i, t: (i, t)), pl.BlockSpec((bm, bw), lambda i, t: (i, cb0 + t))],
        out_specs=pl.BlockSpec((bm, bw), lambda i, t: (i, t)),
        out_shape=jax.ShapeDtypeStruct((rows, width), BF16),
        compiler_params=_params(("parallel", "parallel")),
    )(y, proj)


def _gate_bwd(dsv, y, proj, zc0, bw, dproj, t0, head_major, name):
    rows, width = y.shape
    bm = _tile(rows, 2048 if bw <= 256 else 1024, 16)
    cb0 = zc0 // bw
    tb0 = t0 // bw
    bd = _block_diag(HEAD_DIM)
    hpb = bw // HEAD_DIM

    def body(*refs):
        if head_major:
            ds_ref, y_ref, z_ref, bd_ref, _, dp_ref, dy_ref, dd_ref = refs
        else:
            ds_ref, y_ref, z_ref, _, dp_ref, dy_ref = refs
        z = z_ref[...].astype(F32)
        sig = _sigmoid(z)
        dsx = ds_ref[...]
        yv = y_ref[...]
        dy = dsx * (z * sig)
        dp_ref[...] = (dsx * yv * (sig * (1.0 + z * (1.0 - sig)))).astype(BF16)
        if head_major:
            dyb = dy.astype(BF16)
            dd = _seg_sum(dyb.astype(F32) * yv, bd_ref[...])
            for h in range(hpb):
                sl = slice(h * HEAD_DIM, (h + 1) * HEAD_DIM)
                dy_ref[h] = dyb[:, sl]
                dd_ref[h] = dd[:, sl]
        else:
            dy_ref[...] = dy.astype(BF16)

    tile = pl.BlockSpec((bm, bw), lambda i, t: (i, t))
    ztile = pl.BlockSpec((bm, bw), lambda i, t: (i, cb0 + t))
    ttile = pl.BlockSpec((bm, bw), lambda i, t: (i, tb0 + t))
    any_spec = pl.BlockSpec(memory_space=pl.ANY)
    dp_shape = jax.ShapeDtypeStruct(dproj.shape, BF16)
    if head_major:
        hm_spec = pl.BlockSpec((hpb, bm, HEAD_DIM), lambda i, t: (t, i, 0))
        nh = width // HEAD_DIM
        outs = pl.pallas_call(
            body, name=name, grid=(rows // bm, width // bw),
            in_specs=[tile, tile, ztile, pl.BlockSpec((LANES, LANES), lambda i, t: (0, 0)), any_spec],
            out_specs=[ttile, hm_spec, hm_spec],
            out_shape=[dp_shape, jax.ShapeDtypeStruct((nh, rows, HEAD_DIM), BF16),
                       jax.ShapeDtypeStruct((nh, rows, HEAD_DIM), F32)],
            input_output_aliases={4: 0},
            compiler_params=_params(("parallel", "parallel")),
        )(dsv, y, proj, bd, dproj)
        return outs[0], outs[1], outs[2]
    outs = pl.pallas_call(
        body, name=name, grid=(rows // bm, width // bw),
        in_specs=[tile, tile, ztile, any_spec],
        out_specs=[ttile, tile],
        out_shape=[dp_shape, jax.ShapeDtypeStruct((rows, width), BF16)],
        input_output_aliases={3: 0},
        compiler_params=_params(("parallel", "parallel")),
    )(dsv, y, proj, dproj)
    return outs[0], outs[1], None


def _merge_fwd(proj, ua, ub, uc, name):
    rows, d = ua.shape
    bm = _tile(rows, 1024, 16)
    bw = _tile(d, 512)
    g0 = COL_GATE // bw
    gstep = d // bw

    def body(la_ref, lb_ref, lc_ref, ua_ref, ub_ref, uc_ref, o_ref, ga_ref, gb_ref, gc_ref):
        y = None
        for l_ref, u_ref, g_ref in ((la_ref, ua_ref, ga_ref), (lb_ref, ub_ref, gb_ref), (lc_ref, uc_ref, gc_ref)):
            g = _sigmoid(l_ref[...].astype(F32))
            g_ref[...] = g.astype(BF16)
            term = g * u_ref[...].astype(F32)
            y = term if y is None else y + term
        o_ref[...] = y.astype(BF16)

    tile = pl.BlockSpec((bm, bw), lambda i, t: (i, t))
    gate = lambda b: pl.BlockSpec((bm, bw), lambda i, t: (i, g0 + b * gstep + t))
    shape = jax.ShapeDtypeStruct((rows, d), BF16)
    return pl.pallas_call(
        body, name=name, grid=(rows // bm, d // bw),
        in_specs=[gate(0), gate(1), gate(2), tile, tile, tile],
        out_specs=[tile] * 4, out_shape=[shape] * 4,
        compiler_params=_params(("parallel", "parallel")),
    )(proj, proj, proj, ua, ub, uc)


def _merge_bwd(dym, us, gs, name):
    rows, d = dym.shape
    bm = _tile(rows, 1024, 16)
    bw = _tile(d, 512)
    nb = d // bw

    def body(dy_ref, ua_ref, ub_ref, uc_ref, ga_ref, gb_ref, gc_ref, dg_ref, da_ref, db_ref, dc_ref):
        b = pl.program_id(2)
        dyv = dy_ref[...]
        for idx, (u_ref, g_ref, du_ref) in enumerate(((ua_ref, ga_ref, da_ref), (ub_ref, gb_ref, db_ref), (uc_ref, gc_ref, dc_ref))):
            @pl.when(b == idx)
            def _():
                g = g_ref[...].astype(F32)
                du_ref[...] = (g * dyv).astype(BF16)
                dg_ref[...] = (dyv * u_ref[...].astype(F32) * g * (1.0 - g)).astype(BF16)

    tile = pl.BlockSpec((bm, bw), lambda i, t, b: (i, t))
    shape = jax.ShapeDtypeStruct((rows, d), BF16)
    outs = pl.pallas_call(
        body, name=name, grid=(rows // bm, nb, 3),
        in_specs=[tile] * 7,
        out_specs=[pl.BlockSpec((bm, bw), lambda i, t, b: (i, b * nb + t)), tile, tile, tile],
        out_shape=[jax.ShapeDtypeStruct((rows, 3 * d), BF16), shape, shape, shape],
        compiler_params=_params(("parallel", "parallel", "arbitrary")),
    )(dym, *us, *gs)
    return outs[0], outs[1], outs[2], outs[3]


def _out_proj_loss(ym, wo, x, target, name):
    m, d = x.shape
    bm, bn = _tile(m, 1024, 16), _tile(d, 1024)
    grid = (m // bm, d // bn)

    def body(a_ref, b_ref, x_ref, t_ref, dy_ref, dyb_ref, l_ref):
        first, _ = _grid_edges(grid)
        y = jnp.dot(a_ref[...], b_ref[...], preferred_element_type=F32) + x_ref[...]
        diff = y - t_ref[...]
        dy = diff * (1.0 / d)
        dy_ref[...] = dy
        dyb_ref[...] = dy.astype(BF16)
        sq = diff * diff
        part = sq[:, 0:LANES]
        for c in range(1, bn // LANES):
            part = part + sq[:, c * LANES:(c + 1) * LANES]
        part = jnp.sum(part.reshape(bm // 8, 8, LANES), axis=0)

        @pl.when(first)
        def _():
            l_ref[...] = part

        @pl.when(jnp.logical_not(first))
        def _():
            l_ref[...] += part

    tile = pl.BlockSpec((bm, bn), lambda i, j: (i, j))
    return pl.pallas_call(
        body, name=name, grid=grid,
        in_specs=[pl.BlockSpec((bm, d), lambda i, j: (i, 0)), pl.BlockSpec((d, bn), lambda i, j: (0, j)), tile, tile],
        out_specs=[tile, tile, pl.BlockSpec((8, LANES), lambda i, j: (0, 0))],
        out_shape=[jax.ShapeDtypeStruct((m, d), F32), jax.ShapeDtypeStruct((m, d), BF16),
                   jax.ShapeDtypeStruct((8, LANES), F32)],
        compiler_params=_params(("arbitrary", "arbitrary"), VMEM_BIG),
    )(ym, wo, x, target)


def _row(vec, reps=1):
    return jnp.tile(vec.reshape(1, -1).astype(F32), (1, reps))


def _local_step(x, mem, target, small, wg, shards=None):
    s, d = x.shape
    dist = shards is not None
    wg = dict(wg)
    ones = lambda n: jnp.ones((1, n), F32)
    zeros = lambda n: jnp.zeros((1, n), F32)
    scale_ab = HEAD_DIM ** -0.5
    split8 = lambda g: g.reshape(N_DEV, g.shape[0] // N_DEV, g.shape[1])
    flat8 = lambda g: g.reshape(g.shape[0] * g.shape[1], g.shape[2])
    gather = lambda names: _Comm("gather", [shards[n] for n in names]) if dist else None
    g = {}

    def scatter(names):
        return _Comm("scatter", [split8(g[n]) for n in names]) if dist else None

    def hosted(result, names, store):
        if not dist:
            return result
        out, got = result
        store.update(zip(names, got))
        return out

    hn = _rmsnorm_fwd(x, small["norm_gain"], "rms_x_fwd")
    got = {}
    proj = hosted(_mm_nn(hn, wg["qkv"], bm=1024, bn=1024, bk=d, o_dtype=BF16, name="proj_qkv",
                         comm=gather(("wa", "wb", "wc"))), ("wa", "wb", "wc"), got)
    wg.update({n: flat8(a) for n, a in got.items()})
    pfb = _mm_nn(hn, wg["wf"], bm=1024, bn=FB_PAD, bk=d, o_dtype=F32, name="proj_fb")
    mn = _rmsnorm_fwd(mem, small["mem_norm_gain"], "rms_mem_fwd")
    mkv = _mm_nn(mn, wg["wk"], bm=256, bn=1024, bk=d, o_dtype=F32, name="mem_kv")

    gain_a = jnp.concatenate([_row(small["q_gain_a"], A_Q_HEADS) * scale_ab, _row(small["k_gain_a"], A_KV_HEADS), ones(A_KV_WIDTH)], axis=1)
    flag_a = jnp.concatenate([ones(A_WIDTH + A_KV_WIDTH), zeros(A_KV_WIDTH)], axis=1)
    qkv_a = _headnorm_fwd(proj, COL_QA, 1280, 1280, HEAD_DIM, gain_a, flag_a, True, "hn_a_fwd")
    gain_b = jnp.concatenate([_row(small["q_gain_b"], B_HEADS) * scale_ab, _row(small["k_gain_b"], B_HEADS), ones(B_WIDTH)], axis=1)
    flag_b = jnp.concatenate([ones(2 * B_WIDTH), zeros(B_WIDTH)], axis=1)
    qkv_b = _headnorm_fwd(proj, COL_QB, 2304, 256, HEAD_DIM, gain_b, flag_b, True, "hn_b_fwd")
    gain_cq = _row(small["q_gain_c"], C_HEADS)
    q_c = _headnorm_fwd(proj, COL_QC, C_WIDTH, C_WIDTH, C_HEAD_DIM, gain_cq, ones(C_WIDTH), False, "hn_cq_fwd")
    gain_ck = jnp.concatenate([_row(small["k_gain_c"], C_HEADS), ones(C_WIDTH)], axis=1)
    flag_ck = jnp.concatenate([ones(C_WIDTH), zeros(C_WIDTH)], axis=1)
    mkvn = _headnorm_fwd(mkv, 0, 2 * C_WIDTH, 2 * C_WIDTH, C_HEAD_DIM, gain_ck, flag_ck, False, "hn_ck_fwd")

    q_a, k_a, v_a = qkv_a[0:12], qkv_a[12:16], qkv_a[16:20]
    q_b, k_b, v_b = qkv_b[0:12], qkv_b[12:24], qkv_b[24:36]

    bpad = jnp.pad(small["b_forget"].reshape(1, -1), ((0, 0), (0, FB_PAD - B_HEADS)))
    c16 = _fox_prep(pfb, bpad, "fox_prep")
    c3 = c16[0:B_HEADS].reshape(B_HEADS, 1, s)

    sinks = small["sinks_a"].reshape(-1)
    slopes = jnp.exp2(-8.0 * jnp.arange(1, A_Q_HEADS + 1, dtype=F32) / A_Q_HEADS)
    y_a, lse_a = _attn_a_fwd(q_a, k_a, v_a, sinks, slopes, "attn_a_fwd")
    y_b, lse_b, got_zg = _attn_b_fwd(q_b, k_b, v_b, c3, "attn_b_fwd", comm=gather(("zg",)))
    if dist:
        wg["zg"] = flat8(got_zg[0])
    y_c = _attn_c_fwd(q_c, mkvn, "attn_c_fwd")

    got = {}
    pzg = hosted(_mm_nn(hn, wg["zg"], bm=1024, bn=1024, bk=d, o_dtype=BF16, name="proj_zg", comm=gather(("wo",))),
                 ("wo",), got)
    wg.update({n: flat8(a) for n, a in got.items()})

    s_a = _gate_fwd(y_a, pzg, COL_ZA, 256, "gate_a_fwd")
    s_b = _gate_fwd(y_b, pzg, COL_ZB, 256, "gate_b_fwd")
    s_c = _gate_fwd(y_c, pzg, COL_ZC, 512, "gate_c_fwd")
    w_a, w_b, w_c = _branch_full(wg["wa"]), _branch_full(wg["wb"]), _branch_full(wg["wc"])
    u_a = _mm_nn(s_a, w_a, bm=1024, bn=2048, bk=A_WIDTH, o_dtype=BF16, name="branch_a_fwd")
    u_b = _mm_nn(s_b, w_b, bm=1024, bn=2048, bk=B_WIDTH, o_dtype=BF16, name="branch_b_fwd")
    u_c = _mm_nn(s_c, w_c, bm=1024, bn=2048, bk=C_WIDTH, o_dtype=BF16, name="branch_c_fwd")
    ym, gate_a, gate_b, gate_c = _merge_fwd(pzg, u_a, u_b, u_c, "merge_fwd")
    dy, dyb, lpart = _out_proj_loss(ym, wg["wo"], x, target, "out_proj_loss")
    loss = 0.5 / d * jnp.sum(lpart)

    dym = _mm_nt(dyb, wg["wo"], bm=1024, bn=1024, bk=d, o_dtype=F32, name="out_proj_bwd_act")
    g["wo"] = _mm_tn(ym, dyb, bm=512, bn=1024, bk=s, o_dtype=BF16, name="out_proj_bwd_w")

    dgate, du_a, du_b, du_c = _merge_bwd(dym, (u_a, u_b, u_c), (gate_a, gate_b, gate_c), "merge_bwd")
    parts = {}
    g["wm_g"] = hosted(_mm_tn(hn, dgate, bm=512, bn=1024, bk=s, o_dtype=BF16, name="proj_gate_bwd_w",
                              comm=scatter(("wo",))), ("wo",), parts)

    ds_a = _mm_nt(du_a, w_a, bm=1024, bn=A_WIDTH, bk=d, o_dtype=F32, name="branch_a_bwd_act")
    ds_b = _mm_nt(du_b, w_b, bm=1024, bn=B_WIDTH, bk=d, o_dtype=F32, name="branch_b_bwd_act")
    ds_c = _mm_nt(du_c, w_c, bm=1024, bn=C_WIDTH, bk=d, o_dtype=F32, name="branch_c_bwd_act")
    g["wa"] = _branch_shards(_mm_tn(s_a, du_a, bm=A_WIDTH, bn=1024, bk=s, o_dtype=BF16, name="branch_a_bwd_w"))
    g["wb"] = _branch_shards(_mm_tn(s_b, du_b, bm=B_WIDTH, bn=1024, bk=s, o_dtype=BF16, name="branch_b_bwd_w"))
    g["wc"] = _branch_shards(_mm_tn(s_c, du_c, bm=C_WIDTH, bn=1024, bk=s, o_dtype=BF16, name="branch_c_bwd_w"))

    dz = lax.empty((s, W_Z), BF16)
    dz, do_a, dd_a = _gate_bwd(ds_a, y_a, pzg, COL_ZA, 256, dz, COL_ZA, True, "gate_a_bwd")
    dz, do_b, dd_b = _gate_bwd(ds_b, y_b, pzg, COL_ZB, 256, dz, COL_ZB, True, "gate_b_bwd")
    dz, do_c, _ = _gate_bwd(ds_c, y_c, pzg, COL_ZC, 512, dz, COL_ZC, False, "gate_c_bwd")
    g["wm_z"] = _mm_tn(hn, dz, bm=512, bn=1024, bk=s, o_dtype=BF16, name="proj_z_bwd_w")

    names = ("wa", "wb", "wc")
    dq_a, dkv_a, dsink, got = _attn_a_bwd(q_a, k_a, v_a, do_a, lse_a, dd_a, sinks, slopes, "attn_a_bwd", comm=scatter(names))
    parts.update(zip(names, got))
    names = ("wm_g", "wm_z")
    dq_b, dk_b, dv_b, dc3, got = _attn_b_bwd(q_b, k_b, v_b, do_b, lse_b, dd_b, c3, "attn_b_bwd", comm=scatter(names))
    parts.update(zip(names, got))
    dq_c, dmkvn = _attn_c_bwd(q_c, mkvn, do_c, "attn_c_bwd")

    dqkv = lax.empty((s, W_QKV), BF16)
    dqkv, dg_qa = _headnorm_bwd(proj, COL_QA, A_WIDTH, 256, HEAD_DIM, gain_a[:, 0:768], flag_a[:, 0:768], dq_a, dqkv, COL_QA, "hn_qa_bwd")
    dqkv, dg_kva = _headnorm_bwd(proj, COL_KA, 512, 256, HEAD_DIM, gain_a[:, 768:1280], flag_a[:, 768:1280], dkv_a, dqkv, COL_KA, "hn_kva_bwd")
    dqkv, dg_qb = _headnorm_bwd(proj, COL_QB, B_WIDTH, 256, HEAD_DIM, gain_b[:, 0:768], flag_b[:, 0:768], dq_b, dqkv, COL_QB, "hn_qb_bwd")
    dqkv, dg_kb = _headnorm_bwd(proj, COL_KB, B_WIDTH, 256, HEAD_DIM, gain_b[:, 768:1536], flag_b[:, 768:1536], dk_b, dqkv, COL_KB, "hn_kb_bwd")
    dqkv, _ = _headnorm_bwd(proj, COL_VB, B_WIDTH, 256, HEAD_DIM, gain_b[:, 1536:2304], flag_b[:, 1536:2304], dv_b, dqkv, COL_VB, "hn_vb_bwd")
    dqkv, dg_qc = _headnorm_bwd(proj, COL_QC, C_WIDTH, 512, C_HEAD_DIM, gain_cq, ones(C_WIDTH), dq_c, dqkv, COL_QC, "hn_qc_bwd")
    dmkv, dg_kc = _headnorm_bwd(mkv, 0, 2 * C_WIDTH, 2 * C_WIDTH, C_HEAD_DIM, gain_ck, flag_ck, dmkvn, None, 0, "hn_kc_bwd")

    dct = jnp.pad(dc3.reshape(B_HEADS, s), ((0, 16 - B_HEADS), (0, 0)))
    dfb, dbf = _fox_prep_bwd(pfb, bpad, dct, "fox_prep_bwd")

    dmn = _mm_nt(dmkv, wg["wk"], bm=256, bn=1024, bk=1024, o_dtype=F32, name="mem_kv_bwd_act")
    g["wk"] = _mm_tn(mn, dmkv, bm=512, bn=1024, bk=mem.shape[0], o_dtype=BF16, name="mem_kv_bwd_w")
    _, dg_mem = _rmsnorm_bwd(mem, dmn, small["mem_norm_gain"], None, "rms_mem_bwd")

    g["wm_qkv"] = _mm_tn(hn, dqkv, bm=512, bn=1024, bk=s, o_dtype=BF16, name="proj_qkv_bwd_w")
    g["wf"] = _mm_tn(hn, dfb, bm=512, bn=FB_PAD, bk=s, o_dtype=BF16, name="proj_fb_bwd_w")
    half = W_QKV // 2
    g["wm_q1"], g["wm_q2"] = g["wm_qkv"][:, 0:half], g["wm_qkv"][:, half:W_QKV]
    names = ("wm_q1",)
    dhn = hosted(_mm_nt_sum([(dqkv, wg["qkv"], 0), (dfb, wg["wf"], 0)], bm=1024, bn=1024, bk=2048,
                            name="proj_qkv_bwd_act", comm=scatter(names)), names, parts)
    names = ("wm_q2", "wf", "wk")
    dhn = hosted(_mm_nt_sum([(dz, wg["zg"], COL_ZA), (dgate, wg["zg"], COL_GATE)], bm=1024, bn=1024, bk=2048,
                            name="proj_zg_bwd_act", add=dhn, comm=scatter(names)), names, parts)
    if dist:
        g = parts
    grad_x, dg_x = _rmsnorm_bwd(x, dhn, small["norm_gain"], dy, "rms_x_bwd")

    fold = lambda part, heads, hd: jnp.sum(jnp.sum(part, axis=0).reshape(heads, hd), axis=0).reshape(1, hd)
    small_grads = {
        "norm_gain": jnp.sum(dg_x, axis=0).reshape(1, d),
        "mem_norm_gain": jnp.sum(dg_mem, axis=0).reshape(1, d),
        "b_forget": dbf[0:B_HEADS, 0].reshape(1, B_HEADS),
        "q_gain_a": fold(dg_qa, A_Q_HEADS, HEAD_DIM) * scale_ab,
        "k_gain_a": fold(dg_kva[:, 0:A_KV_WIDTH], A_KV_HEADS, HEAD_DIM),
        "sinks_a": (jnp.sum(dsink, axis=(1, 2)) * (1.0 / HEAD_DIM)).reshape(1, A_Q_HEADS),
        "q_gain_b": fold(dg_qb, B_HEADS, HEAD_DIM) * scale_ab,
        "k_gain_b": fold(dg_kb, B_HEADS, HEAD_DIM),
        "q_gain_c": fold(dg_qc, C_HEADS, C_HEAD_DIM),
        "k_gain_c": fold(dg_kc[:, 0:C_WIDTH], C_HEADS, C_HEAD_DIM),
    }
    return loss, grad_x, small_grads, g


def _coords():
    return lax.axis_index("x"), lax.axis_index("y"), lax.axis_index("c")


def _all_gather(shards, name):
    n = len(shards)

    def body(*refs):
        ins = refs[0:n]
        outs = refs[n:2 * n]
        send_sems, recv_sems, local_sems = refs[2 * n:2 * n + 3]
        x, y, c = _coords()
        me, sibling = (x, y, c), (x, y, 1 - c)
        chips = [(1 - x, y), (x, 1 - y), (1 - x, 1 - y)]
        idx = lambda p: 4 * p[0] + 2 * p[1] + p[2]

        def copy(a, k, block, to, src=None):
            slot = outs[a].at[idx(block)]
            return pltpu.make_async_remote_copy(
                src_ref=slot if src is None else src, dst_ref=slot,
                send_sem=send_sems.at[a, k], recv_sem=recv_sems.at[a, k], device_id=to, device_id_type=MESH)

        mine = [pltpu.make_async_copy(ins[a], outs[a].at[idx(me)], local_sems.at[a]) for a in range(n)]
        for cp in mine:
            cp.start()
        first = []
        for a in range(n):
            first.append(copy(a, 0, me, sibling, src=ins[a]))
            first += [copy(a, 1 + j, me, (*chip, c), src=ins[a]) for j, chip in enumerate(chips)]
        for cp in first:
            cp.start()
        passed = []
        for j, chip in enumerate(chips):
            for a in range(n):
                copy(a, 1 + j, (*chip, c), me).wait_recv()
                fwd = copy(a, 4 + j, (*chip, c), sibling)
                fwd.start()
                passed.append(fwd)
        for a in range(n):
            copy(a, 0, sibling, me).wait_recv()
            for j, chip in enumerate(chips):
                copy(a, 4 + j, (*chip, 1 - c), me).wait_recv()
        for cp in first + passed:
            cp.wait_send()
        for cp in mine:
            cp.wait()

    any_spec = pl.BlockSpec(memory_space=pl.ANY)
    return pl.pallas_call(
        body, name=name,
        in_specs=[any_spec] * n, out_specs=[any_spec] * n,
        out_shape=[jax.ShapeDtypeStruct((N_DEV,) + sh.shape, sh.dtype) for sh in shards],
        scratch_shapes=[pltpu.SemaphoreType.DMA((n, 7)), pltpu.SemaphoreType.DMA((n, 7)), pltpu.SemaphoreType.DMA((n,))],
    )(*shards)


def _all_reduce_small(vec, name):
    p = vec.shape[1]

    def body(v_ref, o_ref, gather, send_sems, recv_sems):
        x, y, c = _coords()
        my = 4 * x + 2 * y + c
        peers = [(x ^ ((k >> 2) & 1), y ^ ((k >> 1) & 1), c ^ (k & 1)) for k in range(1, N_DEV)]
        gather[my] = v_ref[...]
        sends = [pltpu.make_async_remote_copy(
            src_ref=v_ref, dst_ref=gather.at[my], send_sem=send_sems.at[k], recv_sem=recv_sems.at[k],
            device_id=peer, device_id_type=MESH) for k, peer in enumerate(peers)]
        for cp in sends:
            cp.start()
        for k, peer in enumerate(peers):
            pid = 4 * peer[0] + 2 * peer[1] + peer[2]
            pltpu.make_async_remote_copy(
                src_ref=v_ref, dst_ref=gather.at[pid], send_sem=send_sems.at[k], recv_sem=recv_sems.at[k],
                device_id=peer, device_id_type=MESH).wait_recv()
        for cp in sends:
            cp.wait_send()
        total = gather[0]
        for j in range(1, N_DEV):
            total = total + gather[j]
        o_ref[...] = total

    vm = pl.BlockSpec(memory_space=pltpu.VMEM)
    return pl.pallas_call(
        body, name=name, in_specs=[vm], out_specs=vm,
        out_shape=jax.ShapeDtypeStruct((8, p), F32),
        scratch_shapes=[pltpu.VMEM((N_DEV, 8, p), F32), pltpu.SemaphoreType.DMA((7,)), pltpu.SemaphoreType.DMA((7,))],
    )(vec)[0:1]


def _sum_parts(parts, name):
    _, rows, cols = parts.shape
    br = _tile(rows, 64, 16)

    def body(p_ref, o_ref):
        total = p_ref[0].astype(F32)
        for j in range(1, N_DEV):
            total = total + p_ref[j].astype(F32)
        o_ref[...] = total

    return pl.pallas_call(
        body, name=name, grid=(rows // br,),
        in_specs=[pl.BlockSpec((N_DEV, br, cols), lambda i: (0, i, 0))],
        out_specs=pl.BlockSpec((br, cols), lambda i: (i, 0)),
        out_shape=jax.ShapeDtypeStruct((rows, cols), F32),
        compiler_params=_params(("parallel",), VMEM_BIG),
    )(parts)


def _adamw(w, g, m, v, name, br=32):
    rows, cols = w.shape
    br = min(br, rows)
    c1 = 1.0 / (1.0 - ADAM_B1 ** ADAM_STEP)
    c2 = 1.0 / (1.0 - ADAM_B2 ** ADAM_STEP)

    def body(w_ref, g_ref, m_ref, v_ref, d_ref, nm_ref, nv_ref):
        gv = g_ref[...]
        nm = ADAM_B1 * m_ref[...] + (1.0 - ADAM_B1) * gv
        nv = ADAM_B2 * v_ref[...] + (1.0 - ADAM_B2) * (gv * gv)
        d_ref[...] = -ADAM_LR * ((nm * c1) / (jnp.sqrt(nv * c2) + ADAM_EPS) + ADAM_WD * w_ref[...])
        nm_ref[...] = nm
        nv_ref[...] = nv

    spec = pl.BlockSpec((br, cols), lambda i: (i, 0))
    shape = jax.ShapeDtypeStruct((rows, cols), F32)
    return pl.pallas_call(
        body, name=name, grid=(pl.cdiv(rows, br),), in_specs=[spec] * 4, out_specs=[spec] * 3, out_shape=[shape] * 3,
        compiler_params=_params(("parallel",), VMEM_BIG),
    )(w, g, m, v)


def _adamw_t(wt, g, mt, vt, name, br=1024):
    n, r = wt.shape
    c1 = 1.0 / (1.0 - ADAM_B1 ** ADAM_STEP)
    c2 = 1.0 / (1.0 - ADAM_B2 ** ADAM_STEP)

    def body(w_ref, g_ref, m_ref, v_ref, d_ref, nm_ref, nv_ref):
        gv = g_ref[...].T
        nm = ADAM_B1 * m_ref[...] + (1.0 - ADAM_B1) * gv
        nv = ADAM_B2 * v_ref[...] + (1.0 - ADAM_B2) * (gv * gv)
        d_ref[...] = -ADAM_LR * ((nm * c1) / (jnp.sqrt(nv * c2) + ADAM_EPS) + ADAM_WD * w_ref[...])
        nm_ref[...] = nm
        nv_ref[...] = nv

    spec = pl.BlockSpec((br, r), lambda i: (i, 0))
    shape = jax.ShapeDtypeStruct((n, r), F32)
    return pl.pallas_call(
        body, name=name, grid=(pl.cdiv(n, br),),
        in_specs=[spec, pl.BlockSpec((r, br), lambda i: (0, i)), spec, spec], out_specs=[spec] * 3, out_shape=[shape] * 3,
        compiler_params=_params(("parallel",), VMEM_BIG),
    )(wt, g, mt, vt)


def _adamw_parts(w, parts, m, v, name):
    rows, cols = w.shape
    br = _tile(rows, 32, 16)
    c1 = 1.0 / (1.0 - ADAM_B1 ** ADAM_STEP)
    c2 = 1.0 / (1.0 - ADAM_B2 ** ADAM_STEP)

    def body(w_ref, p_ref, m_ref, v_ref, g_ref, d_ref, nm_ref, nv_ref):
        gv = p_ref[0].astype(F32)
        for j in range(1, N_DEV):
            gv = gv + p_ref[j].astype(F32)
        nm = ADAM_B1 * m_ref[...] + (1.0 - ADAM_B1) * gv
        nv = ADAM_B2 * v_ref[...] + (1.0 - ADAM_B2) * (gv * gv)
        g_ref[...] = gv
        d_ref[...] = -ADAM_LR * ((nm * c1) / (jnp.sqrt(nv * c2) + ADAM_EPS) + ADAM_WD * w_ref[...])
        nm_ref[...] = nm
        nv_ref[...] = nv

    spec = pl.BlockSpec((br, cols), lambda i: (i, 0))
    shape = jax.ShapeDtypeStruct((rows, cols), F32)
    return pl.pallas_call(
        body, name=name, grid=(rows // br,),
        in_specs=[spec, pl.BlockSpec((N_DEV, br, cols), lambda i: (0, i, 0)), spec, spec],
        out_specs=[spec] * 4, out_shape=[shape] * 4,
        compiler_params=_params(("parallel",), VMEM_BIG),
    )(w, parts, m, v)


SMALL_NAMES = ("norm_gain", "mem_norm_gain", "b_forget", "q_gain_a", "k_gain_a", "sinks_a",
               "q_gain_b", "k_gain_b", "q_gain_c", "k_gain_c")
BIG_NAMES = ("w_in", "w_mem_kv", "w_branch_a", "w_branch_b", "w_branch_c", "w_out")
WEIGHT_ORDER = ("norm_gain", "mem_norm_gain", "w_in", "b_forget", "q_gain_a", "k_gain_a", "sinks_a", "q_gain_b",
                "k_gain_b", "q_gain_c", "k_gain_c", "w_mem_kv", "w_branch_a", "w_branch_b", "w_branch_c", "w_out")


def _pack_small(tree):
    flat = jnp.concatenate([tree[n].reshape(1, -1) for n in SMALL_NAMES], axis=1)
    pad = (-flat.shape[1]) % LANES
    return jnp.pad(flat, ((0, 0), (0, pad)))


def _unpack_small(flat, like):
    out, off = {}, 0
    for n in SMALL_NAMES:
        size = like[n].size
        out[n] = flat[:, off:off + size].reshape(like[n].shape)
        off += size
    return out


def kernel(x, mem, norm_gain, mem_norm_gain, w_in, b_forget, q_gain_a, k_gain_a, sinks_a, q_gain_b, k_gain_b, q_gain_c, k_gain_c, w_mem_kv, w_branch_a, w_branch_b, w_branch_c, w_out, loss_target, m_norm_gain, m_mem_norm_gain, m_w_in, m_b_forget, m_q_gain_a, m_k_gain_a, m_sinks_a, m_q_gain_b, m_k_gain_b, m_q_gain_c, m_k_gain_c, m_w_mem_kv, m_w_branch_a, m_w_branch_b, m_w_branch_c, m_w_out, v_norm_gain, v_mem_norm_gain, v_w_in, v_b_forget, v_q_gain_a, v_k_gain_a, v_sinks_a, v_q_gain_b, v_k_gain_b, v_q_gain_c, v_k_gain_c, v_w_mem_kv, v_w_branch_a, v_w_branch_b, v_w_branch_c, v_w_out):
    weights = dict(norm_gain=norm_gain, mem_norm_gain=mem_norm_gain, w_in=w_in, b_forget=b_forget, q_gain_a=q_gain_a,
                   k_gain_a=k_gain_a, sinks_a=sinks_a, q_gain_b=q_gain_b, k_gain_b=k_gain_b, q_gain_c=q_gain_c,
                   k_gain_c=k_gain_c, w_mem_kv=w_mem_kv, w_branch_a=w_branch_a, w_branch_b=w_branch_b,
                   w_branch_c=w_branch_c, w_out=w_out)
    mom_m = dict(norm_gain=m_norm_gain, mem_norm_gain=m_mem_norm_gain, w_in=m_w_in, b_forget=m_b_forget,
                 q_gain_a=m_q_gain_a, k_gain_a=m_k_gain_a, sinks_a=m_sinks_a, q_gain_b=m_q_gain_b, k_gain_b=m_k_gain_b,
                 q_gain_c=m_q_gain_c, k_gain_c=m_k_gain_c, w_mem_kv=m_w_mem_kv, w_branch_a=m_w_branch_a,
                 w_branch_b=m_w_branch_b, w_branch_c=m_w_branch_c, w_out=m_w_out)
    mom_v = dict(norm_gain=v_norm_gain, mem_norm_gain=v_mem_norm_gain, w_in=v_w_in, b_forget=v_b_forget,
                 q_gain_a=v_q_gain_a, k_gain_a=v_k_gain_a, sinks_a=v_sinks_a, q_gain_b=v_q_gain_b, k_gain_b=v_k_gain_b,
                 q_gain_c=v_q_gain_c, k_gain_c=v_k_gain_c, w_mem_kv=v_w_mem_kv, w_branch_a=v_w_branch_a,
                 w_branch_b=v_w_branch_b, w_branch_c=v_w_branch_c, w_out=v_w_out)
    wi = w_in[0]
    sh_qkv = jnp.concatenate([wi[:, a:b] for a, b in SRC_RANGES[0:3]], axis=1).astype(BF16)
    sh_zg = jnp.concatenate([wi[:, a:b] for a, b in SRC_RANGES[3:6]] + [wi[:, SRC_GATE:]], axis=1).astype(BF16)
    sh_wf = jnp.pad(wi[:, FB_SRC:FB_SRC + B_HEADS], ((0, 0), (0, FB_PAD - B_HEADS))).astype(BF16)
    shards = {"zg": sh_zg, "wo": w_out[0].astype(BF16), "wa": w_branch_a[0].astype(BF16),
              "wb": w_branch_b[0].astype(BF16), "wc": w_branch_c[0].astype(BF16)}
    first = ("qkv", "wf", "wk")
    full = _all_gather([sh_qkv, sh_wf, w_mem_kv[0].astype(BF16)], "weights_all_gather")
    wg = {kname: arr.reshape(arr.shape[0] * arr.shape[1], arr.shape[2]) for kname, arr in zip(first, full)}

    small = {n: weights[n] for n in SMALL_NAMES}
    loss_local, grad_x, small_g, parts = _local_step(x[0], mem[0], loss_target[0], small, wg, shards)

    grads, delta, new_m, new_v = {}, {}, {}, {}
    for n, kname in (("w_mem_kv", "wk"), ("w_out", "wo"), ("w_branch_a", "wa"), ("w_branch_b", "wb"), ("w_branch_c", "wc")):
        gsum, dlt, nm, nv = _adamw_parts(weights[n][0], parts[kname], mom_m[n][0], mom_v[n][0], "adamw_" + n)
        grads[n], delta[n], new_m[n], new_v[n] = gsum, dlt[None], nm[None], nv[None]
    g1, g2, gz, gf, gg = (_sum_parts(parts[k], "grad_sum_" + k) for k in ("wm_q1", "wm_q2", "wm_z", "wf", "wm_g"))
    half = W_QKV // 2
    g_in = jnp.concatenate([g1[:, COL_QA:COL_QB], gz[:, COL_ZA:COL_ZB], g1[:, COL_QB:half], g2[:, 0:COL_QC - half],
                            gz[:, COL_ZB:COL_ZC], gf[:, 0:B_HEADS], g2[:, COL_QC - half:half], gz[:, COL_ZC:W_Z], gg], axis=1)
    dlt, nm, nv = _adamw_t(w_in[0].T, g_in, m_w_in[0].T, v_w_in[0].T, "adamw_w_in")
    grads["w_in"], delta["w_in"], new_m["w_in"], new_v["w_in"] = g_in, dlt.T[None], nm.T[None], nv.T[None]

    packed = _pack_small(small_g)
    reduced = _all_reduce_small(jnp.broadcast_to(packed, (8, packed.shape[1])), "small_all_reduce")
    grads.update(_unpack_small(reduced, small))

    loss = lax.psum(loss_local, ("x", "y", "c"))

    pw, pm, pv = _pack_small(small), _pack_small({n: mom_m[n] for n in SMALL_NAMES}), _pack_small({n: mom_v[n] for n in SMALL_NAMES})
    rep8 = lambda a: jnp.broadcast_to(a, (8, a.shape[1]))
    dlt, nm, nv = _adamw(rep8(pw), rep8(reduced), rep8(pm), rep8(pv), "adamw_small")
    for tree, flat in ((delta, dlt), (new_m, nm), (new_v, nv)):
        tree.update(_unpack_small(flat[0:1], small))
    for n in BIG_NAMES:
        grads[n] = grads[n][None]
    return (loss, grad_x[None], *[grads[n] for n in WEIGHT_ORDER], *[delta[n] for n in WEIGHT_ORDER],
            *[new_m[n] for n in WEIGHT_ORDER], *[new_v[n] for n in WEIGHT_ORDER])
```

```python
import math

import jax
import jax.numpy as jnp
import numpy as np
from jax import lax
from jax.experimental import pallas as pl
from jax.experimental.pallas import tpu as pltpu

F32 = jnp.float32
BF16 = jnp.bfloat16

N_DEV = 8
HEAD_DIM = 64
A_Q_HEADS = 12
A_KV_HEADS = 4
A_GROUP = 3
B_HEADS = 12
C_HEADS = 4
C_HEAD_DIM = 128
WINDOW = 128
A_WIDTH = 768
A_KV_WIDTH = 256
B_WIDTH = 768
C_WIDTH = 512
EPS = 1e-6
NEG = -1e30

COL_QA, COL_KA, COL_VA = 0, 768, 1024
COL_QB, COL_KB, COL_VB = 1280, 2048, 2816
COL_QC = 3584
W_QKV = 4096
COL_ZA, COL_ZB, COL_ZC = 0, 768, 1536
COL_GATE = W_Z = 2048
SRC_RANGES = ((0, 1280), (2048, 4352), (5132, 5644), (1280, 2048), (4352, 5120), (5644, 6156))
SRC_GATE = 6156
FB_SRC = 5120
FB_PAD = 128

ADAM_LR = 0.001
ADAM_B1 = 0.9
ADAM_B2 = 0.999
ADAM_EPS = 1e-08
ADAM_WD = 0.01
ADAM_STEP = 10

VMEM_BIG = 52 * 1024 * 1024
LANES = 128
MESH = pl.DeviceIdType.MESH


def _tile(n, pref, mult=128):
    if n <= pref:
        return n
    t = (pref // mult) * mult
    while t >= mult:
        if n % t == 0:
            return t
        t -= mult
    return n


def _params(sem=None, vmem=None):
    kw = {}
    if sem is not None:
        kw["dimension_semantics"] = sem
    if vmem is not None:
        kw["vmem_limit_bytes"] = vmem
    return pltpu.CompilerParams(**kw)


def _sigmoid(x):
    return 1.0 / (1.0 + jnp.exp(-x))


def _block_diag(hd):
    r = np.arange(LANES)
    return jnp.asarray((r[:, None] // hd) == (r[None, :] // hd), dtype=BF16)


def _seg_sum(t, bd):
    hi = t.astype(BF16)
    lo = (t - hi.astype(F32)).astype(BF16)
    outs = []
    for c in range(t.shape[1] // LANES):
        sl = slice(c * LANES, (c + 1) * LANES)
        outs.append(jnp.dot(hi[:, sl], bd, preferred_element_type=F32) + jnp.dot(lo[:, sl], bd, preferred_element_type=F32))
    return outs[0] if len(outs) == 1 else jnp.concatenate(outs, axis=1)


def _rmsnorm_fwd(x, gain, name):
    rows, d = x.shape
    bm = _tile(rows, 512, 8)

    def body(x_ref, g_ref, o_ref):
        xv = x_ref[...]
        ms = jnp.mean(xv * xv, axis=-1, keepdims=True)
        o_ref[...] = (xv * lax.rsqrt(ms + EPS) * g_ref[...]).astype(BF16)

    return pl.pallas_call(
        body, name=name, grid=(rows // bm,),
        in_specs=[pl.BlockSpec((bm, d), lambda i: (i, 0)), pl.BlockSpec((1, d), lambda i: (0, 0))],
        out_specs=pl.BlockSpec((bm, d), lambda i: (i, 0)),
        out_shape=jax.ShapeDtypeStruct((rows, d), BF16),
        compiler_params=_params(("parallel",)),
    )(x, gain)


def _rmsnorm_bwd(x, dhn, gain, dy, name):
    rows, d = x.shape
    bm = _tile(rows, 512, 8)
    with_dx = dy is not None

    def body(*refs):
        if with_dx:
            x_ref, dh_ref, g_ref, dy_ref, gx_ref, dg_ref = refs
        else:
            x_ref, dh_ref, g_ref, dg_ref = refs
        i = pl.program_id(0)
        xv = x_ref[...]
        rstd = lax.rsqrt(jnp.mean(xv * xv, axis=-1, keepdims=True) + EPS)
        xhat = xv * rstd
        dh = dh_ref[...]
        part = jnp.sum((dh * xhat).reshape(bm // 8, 8, d), axis=0)

        @pl.when(i == 0)
        def _():
            dg_ref[...] = part

        @pl.when(i > 0)
        def _():
            dg_ref[...] += part

        if with_dx:
            g = dh * g_ref[...]
            mean = jnp.mean(g * xhat, axis=-1, keepdims=True)
            gx_ref[...] = dy_ref[...] + rstd * (g - xhat * mean)

    row_spec = pl.BlockSpec((bm, d), lambda i: (i, 0))
    in_specs = [row_spec, row_spec, pl.BlockSpec((1, d), lambda i: (0, 0))]
    args = [x, dhn, gain]
    dg_spec = pl.BlockSpec((8, d), lambda i: (0, 0))
    dg_shape = jax.ShapeDtypeStruct((8, d), F32)
    if with_dx:
        in_specs.append(row_spec)
        args.append(dy)
        out_specs = [row_spec, dg_spec]
        out_shape = [jax.ShapeDtypeStruct((rows, d), F32), dg_shape]
    else:
        out_specs = [dg_spec]
        out_shape = [dg_shape]
    outs = pl.pallas_call(
        body, name=name, grid=(rows // bm,), in_specs=in_specs, out_specs=out_specs, out_shape=out_shape,
        compiler_params=_params(("arbitrary",), VMEM_BIG),
    )(*args)
    return outs if with_dx else (None, outs[0])


class _Comm:
    def __init__(self, kind, arrays):
        self.kind = kind
        self.arrays = list(arrays)
        self.n = len(self.arrays)

    def out_shapes(self):
        if self.kind == "gather":
            return [jax.ShapeDtypeStruct((N_DEV,) + a.shape, a.dtype) for a in self.arrays]
        return [jax.ShapeDtypeStruct(a.shape, a.dtype) for a in self.arrays]

    def scratch(self):
        return [pltpu.SemaphoreType.DMA((self.n, N_DEV - 1)), pltpu.SemaphoreType.DMA((self.n, N_DEV - 1)),
                pltpu.SemaphoreType.DMA((self.n,))]

    def _plan(self, ins, outs, sems, with_recvs):
        send_sems, recv_sems, local_sems = sems
        x, y, c = lax.axis_index("x"), lax.axis_index("y"), lax.axis_index("c")
        my = 4 * x + 2 * y + c
        gather = self.kind == "gather"
        local, sends, recvs = [], [], []
        for a in range(self.n):
            local.append(pltpu.make_async_copy(ins[a] if gather else ins[a].at[my], outs[a].at[my], local_sems.at[a]))
            for k in range(1, N_DEV):
                peer = (x ^ ((k >> 2) & 1), y ^ ((k >> 1) & 1), c ^ (k & 1))
                pid = 4 * peer[0] + 2 * peer[1] + peer[2]
                src = ins[a] if gather else ins[a].at[pid]
                sem = dict(send_sem=send_sems.at[a, k - 1], recv_sem=recv_sems.at[a, k - 1], device_id=peer, device_id_type=MESH)
                sends.append(pltpu.make_async_remote_copy(src_ref=src, dst_ref=outs[a].at[my], **sem))
                if with_recvs:
                    recvs.append(pltpu.make_async_remote_copy(src_ref=src, dst_ref=outs[a].at[pid], **sem))
        return local, sends, recvs

    def start(self, ins, outs, sems):
        local, sends, _ = self._plan(ins, outs, sems, False)
        for cp in local + sends:
            cp.start()

    def wait(self, ins, outs, sems):
        local, sends, recvs = self._plan(ins, outs, sems, True)
        for cp in recvs:
            cp.wait_recv()
        for cp in sends:
            cp.wait_send()
        for cp in local:
            cp.wait()


def _grid_edges(grid):
    first = last = None
    for ax, size in enumerate(grid):
        pid = pl.program_id(ax)
        f, l = pid == 0, pid == size - 1
        first = f if first is None else first & f
        last = l if last is None else last & l
    return first, last


def _hosted_call(body, comm, *, name, grid, in_specs, out_specs, out_shape, scratch_shapes, args, sem, vmem=None):
    in_specs, out_specs, out_shape, scratch_shapes = list(in_specs), list(out_specs), list(out_shape), list(scratch_shapes)
    if comm is None:
        res = pl.pallas_call(body, name=name, grid=grid, in_specs=in_specs, out_specs=out_specs, out_shape=out_shape,
                             scratch_shapes=scratch_shapes, compiler_params=_params(sem, vmem))(*args)
        return list(res), []
    n_in, n_out, n_scr, nc = len(in_specs), len(out_shape), len(scratch_shapes), comm.n

    def hosted(*refs):
        ins = refs[0:n_in]
        comm_in = refs[n_in:n_in + nc]
        outs = refs[n_in + nc:n_in + nc + n_out]
        comm_out = refs[n_in + nc + n_out:n_in + 2 * nc + n_out]
        scr = refs[n_in + 2 * nc + n_out:n_in + 2 * nc + n_out + n_scr]
        sems = refs[n_in + 2 * nc + n_out + n_scr:]
        first, last = _grid_edges(grid)

        @pl.when(first)
        def _():
            comm.start(comm_in, comm_out, sems)

        body(*ins, *outs, *scr)

        @pl.when(last)
        def _():
            comm.wait(comm_in, comm_out, sems)

    any_spec = pl.BlockSpec(memory_space=pl.ANY)
    res = pl.pallas_call(
        hosted, name=name, grid=grid, in_specs=in_specs + [any_spec] * nc, out_specs=out_specs + [any_spec] * nc,
        out_shape=out_shape + comm.out_shapes(), scratch_shapes=scratch_shapes + comm.scratch(),
        compiler_params=_params(("arbitrary",) * len(grid), vmem),
    )(*args, *comm.arrays)
    return list(res[0:n_out]), list(res[n_out:])


def _mm(a, b, *, grid, a_spec, b_spec, o_spec, o_shape, o_dtype, contract, name, add=None, add_spec=None, acc_shape=None,
        comm=None):
    nk = grid[2]
    has_add = add is not None

    def body(*refs):
        a_ref, b_ref = refs[0], refs[1]
        add_ref = refs[2] if has_add else None
        o_ref = refs[3] if has_add else refs[2]
        part = lax.dot_general(a_ref[...], b_ref[...], (contract, ((), ())), preferred_element_type=F32)
        if nk == 1:
            if has_add:
                part = part + add_ref[...]
            o_ref[...] = part.astype(o_dtype)
        else:
            acc = refs[-1]
            k = pl.program_id(2)

            @pl.when(k == 0)
            def _():
                acc[...] = part

            @pl.when(k > 0)
            def _():
                acc[...] += part

            @pl.when(k == nk - 1)
            def _():
                r = acc[...]
                if has_add:
                    r = r + add_ref[...]
                o_ref[...] = r.astype(o_dtype)

    in_specs = [a_spec, b_spec] + ([add_spec] if has_add else [])
    args = [a, b] + ([add] if has_add else [])
    scratch = [pltpu.VMEM(acc_shape, F32)] if nk > 1 else []
    outs, comm_outs = _hosted_call(
        body, comm, name=name, grid=grid, in_specs=in_specs, out_specs=[o_spec],
        out_shape=[jax.ShapeDtypeStruct(o_shape, o_dtype)], scratch_shapes=scratch, args=args,
        sem=("parallel", "parallel", "arbitrary"), vmem=VMEM_BIG)
    return outs[0] if comm is None else (outs[0], comm_outs)


def _mm_nn(a, b, *, bm, bn, bk, o_dtype, name, add=None, comm=None):
    m, kd = a.shape
    n = b.shape[1]
    bm, bn, bk = _tile(m, bm, 8), _tile(n, bn), _tile(kd, bk)
    o_spec = pl.BlockSpec((bm, bn), lambda i, j, k: (i, j))
    return _mm(a, b, grid=(m // bm, n // bn, kd // bk),
               a_spec=pl.BlockSpec((bm, bk), lambda i, j, k: (i, k)),
               b_spec=pl.BlockSpec((bk, bn), lambda i, j, k: (k, j)),
               o_spec=o_spec, o_shape=(m, n), o_dtype=o_dtype, contract=((1,), (0,)), name=name,
               add=add, add_spec=o_spec, acc_shape=(bm, bn), comm=comm)


def _mm_nt(a, b, *, bm, bn, bk, o_dtype, name, add=None, b_col0=0, comm=None):
    m, kd = a.shape
    n = b.shape[0]
    bm, bn, bk = _tile(m, bm, 8), _tile(n, bn), _tile(math.gcd(kd, b_col0), bk)
    kb0 = b_col0 // bk
    o_spec = pl.BlockSpec((bm, bn), lambda i, j, k: (i, j))
    return _mm(a, b, grid=(m // bm, n // bn, kd // bk),
               a_spec=pl.BlockSpec((bm, bk), lambda i, j, k: (i, k)),
               b_spec=pl.BlockSpec((bn, bk), lambda i, j, k: (j, kb0 + k)),
               o_spec=o_spec, o_shape=(m, n), o_dtype=o_dtype, contract=((1,), (1,)), name=name,
               add=add, add_spec=o_spec, acc_shape=(bm, bn), comm=comm)


def _mm_nt_sum(terms, *, bm, bn, bk, name, add=None, comm=None):
    m = terms[0][0].shape[0]
    n = terms[0][1].shape[0]
    bm, bn = _tile(m, bm, 8), _tile(n, bn)
    nt = (((1,), (1,)), ((), ()))
    plan, groups, start = [], [], 0
    for a, b, col0 in terms:
        kd = a.shape[1]
        tk = _tile(math.gcd(kd, col0), bk)
        steps = kd // tk
        last = groups[-1] if groups else None
        if last is not None and last[0] is b and last[4] == tk and (last[3] + last[2]) * tk == col0:
            last[2] += steps
        else:
            groups.append([b, start, steps, col0 // tk, tk])
        plan.append((start, steps, len(groups) - 1))
        start += steps
    nk = start
    nterm, ngroup, has_add = len(terms), len(groups), add is not None

    def body(*refs):
        a_refs, b_refs = refs[0:nterm], refs[nterm:nterm + ngroup]
        add_ref = refs[nterm + ngroup] if has_add else None
        o_ref, acc = refs[nterm + ngroup + has_add], refs[nterm + ngroup + has_add + 1]
        k = pl.program_id(2)
        for t, (s0, steps, grp) in enumerate(plan):
            @pl.when((k >= s0) & (k < s0 + steps))
            def _():
                part = lax.dot_general(a_refs[t][...], b_refs[grp][...], nt, preferred_element_type=F32)

                @pl.when(k == 0)
                def _():
                    acc[...] = part

                @pl.when(k > 0)
                def _():
                    acc[...] += part

        @pl.when(k == nk - 1)
        def _():
            o_ref[...] = acc[...] + add_ref[...] if has_add else acc[...]

    def a_spec(tk, s0, steps):
        return pl.BlockSpec((bm, tk), lambda i, j, k: (i, jnp.clip(k - s0, 0, steps - 1)))

    def b_spec(tk, s0, steps, off):
        return pl.BlockSpec((bn, tk), lambda i, j, k: (j, off + jnp.clip(k - s0, 0, steps - 1)))

    o_spec = pl.BlockSpec((bm, bn), lambda i, j, k: (i, j))
    in_specs = [a_spec(groups[grp][4], s0, steps) for s0, steps, grp in plan]
    in_specs += [b_spec(tk, s0, steps, cb0) for _, s0, steps, cb0, tk in groups]
    args = [a for a, _, _ in terms] + [grp[0] for grp in groups]
    if has_add:
        in_specs.append(o_spec)
        args.append(add)
    outs, comm_outs = _hosted_call(
        body, comm, name=name, grid=(m // bm, n // bn, nk), in_specs=in_specs,
        out_specs=[o_spec], out_shape=[jax.ShapeDtypeStruct((m, n), F32)],
        scratch_shapes=[pltpu.VMEM((bm, bn), F32)], args=args,
        sem=("parallel", "parallel", "arbitrary"), vmem=VMEM_BIG)
    return outs[0] if comm is None else (outs[0], comm_outs)


def _mm_tn(a, b, *, bm, bn, bk, o_dtype, name, comm=None):
    kd, m = a.shape
    n = b.shape[1]
    bm, bn, bk = _tile(m, bm), _tile(n, bn), _tile(kd, bk, 8)
    return _mm(a, b, grid=(m // bm, n // bn, kd // bk),
               a_spec=pl.BlockSpec((bk, bm), lambda i, j, k: (k, i)),
               b_spec=pl.BlockSpec((bk, bn), lambda i, j, k: (k, j)),
               o_spec=pl.BlockSpec((bm, bn), lambda i, j, k: (i, j)),
               o_shape=(m, n), o_dtype=o_dtype, contract=((0,), (0,)), name=name, acc_shape=(bm, bn), comm=comm)


def _branch_full(w8):
    kb, ds = w8.shape[0] // N_DEV, w8.shape[1]
    return w8.reshape(N_DEV, kb, ds).transpose(1, 0, 2).reshape(kb, N_DEV * ds)


def _branch_shards(g):
    kb, ds = g.shape[0], g.shape[1] // N_DEV
    return g.reshape(kb, N_DEV, ds).transpose(1, 0, 2).reshape(N_DEV * kb, ds)


def _headnorm_fwd(src, c0, width, bw, hd, gain, nflag, head_major, name):
    rows = src.shape[0]
    bm = _tile(rows, 2048 if bw <= 256 else 1024, 16)
    bd = _block_diag(hd)
    cb0 = c0 // bw

    def body(x_ref, g_ref, f_ref, bd_ref, o_ref):
        xv = x_ref[...].astype(F32)
        ss = _seg_sum(xv * xv, bd_ref[...])
        rstd = lax.rsqrt(ss * (1.0 / hd) + EPS)
        y = (xv * jnp.where(f_ref[...] > 0.0, rstd, 1.0) * g_ref[...]).astype(BF16)
        if head_major:
            for h in range(bw // HEAD_DIM):
                o_ref[h] = y[:, h * HEAD_DIM:(h + 1) * HEAD_DIM]
        else:
            o_ref[...] = y

    vec_spec = pl.BlockSpec((1, bw), lambda i, t: (0, t))
    if head_major:
        hpb = bw // HEAD_DIM
        out_spec = pl.BlockSpec((hpb, bm, HEAD_DIM), lambda i, t: (t, i, 0))
        out_shape = jax.ShapeDtypeStruct((width // HEAD_DIM, rows, HEAD_DIM), BF16)
    else:
        out_spec = pl.BlockSpec((bm, bw), lambda i, t: (i, t))
        out_shape = jax.ShapeDtypeStruct((rows, width), BF16)
    return pl.pallas_call(
        body, name=name, grid=(rows // bm, width // bw),
        in_specs=[pl.BlockSpec((bm, bw), lambda i, t: (i, cb0 + t)), vec_spec, vec_spec,
                  pl.BlockSpec((LANES, LANES), lambda i, t: (0, 0))],
        out_specs=out_spec, out_shape=out_shape,
        compiler_params=_params(("parallel", "parallel")),
    )(src, gain, nflag, bd)


def _headnorm_bwd(src, c0, width, bw, hd, gain, nflag, dyn, target, t0, name):
    rows = src.shape[0]
    bm = _tile(rows, 2048 if bw <= 256 else 1024, 16)
    bd = _block_diag(hd)
    cb0 = c0 // bw
    tb0 = t0 // bw
    aliased = target is not None

    def body(*refs):
        if aliased:
            x_ref, dy_ref, g_ref, f_ref, bd_ref, _, o_ref, dg_ref = refs
        else:
            x_ref, dy_ref, g_ref, f_ref, bd_ref, o_ref, dg_ref = refs
        i = pl.program_id(1)
        xv = x_ref[...].astype(F32)
        dyv = dy_ref[...]
        bdv = bd_ref[...]
        rstd = lax.rsqrt(_seg_sum(xv * xv, bdv) * (1.0 / hd) + EPS)
        xhat = xv * rstd
        g = dyv * g_ref[...]
        mean = _seg_sum(g * xhat, bdv) * (1.0 / hd)
        dx = jnp.where(f_ref[...] > 0.0, rstd * (g - xhat * mean), g)
        o_ref[...] = dx.astype(BF16)
        part = jnp.sum((dyv * xhat).reshape(bm // 8, 8, bw), axis=0)

        @pl.when(i == 0)
        def _():
            dg_ref[...] = part

        @pl.when(i > 0)
        def _():
            dg_ref[...] += part

    vec_spec = pl.BlockSpec((1, bw), lambda t, i: (0, t))
    in_specs = [pl.BlockSpec((bm, bw), lambda t, i: (i, cb0 + t)), pl.BlockSpec((bm, bw), lambda t, i: (i, t)),
                vec_spec, vec_spec, pl.BlockSpec((LANES, LANES), lambda t, i: (0, 0))]
    args = [src, dyn, gain, nflag, bd]
    aliases = {}
    if aliased:
        in_specs.append(pl.BlockSpec(memory_space=pl.ANY))
        args.append(target)
        aliases = {5: 0}
        o_shape = jax.ShapeDtypeStruct(target.shape, BF16)
    else:
        o_shape = jax.ShapeDtypeStruct((rows, width), BF16)
    out, dg = pl.pallas_call(
        body, name=name, grid=(width // bw, rows // bm), in_specs=in_specs,
        out_specs=[pl.BlockSpec((bm, bw), lambda t, i: (i, tb0 + t)), pl.BlockSpec((8, bw), lambda t, i: (0, t))],
        out_shape=[o_shape, jax.ShapeDtypeStruct((8, width), F32)],
        input_output_aliases=aliases,
        compiler_params=_params(("parallel", "arbitrary")),
    )(*args)
    return out, dg


def _fox_prep(pfb, bpad, name):
    s = pfb.shape[0]

    def body(p_ref, b_ref, c_ref):
        z = p_ref[...] + b_ref[...]
        logf = jnp.minimum(z, 0.0) - jnp.log(1.0 + jnp.exp(-jnp.abs(z)))
        x = logf.T[0:16, :]
        lane = lax.broadcasted_iota(jnp.int32, (16, s), 1)
        sh = 1
        while sh < s:
            x = x + jnp.where(lane >= sh, pltpu.roll(x, sh, 1), 0.0)
            sh *= 2
        c_ref[...] = x

    return pl.pallas_call(
        body, name=name, grid=(1,),
        in_specs=[pl.BlockSpec((s, FB_PAD), lambda i: (0, 0)), pl.BlockSpec((1, FB_PAD), lambda i: (0, 0))],
        out_specs=pl.BlockSpec((16, s), lambda i: (0, 0)),
        out_shape=jax.ShapeDtypeStruct((16, s), F32),
        compiler_params=_params(("arbitrary",)),
    )(pfb, bpad)


def _fox_prep_bwd(pfb, bpad, dct, name):
    s = pfb.shape[0]

    def body(p_ref, b_ref, dc_ref, df_ref, db_ref):
        zt = (p_ref[...] + b_ref[...]).T[0:16, :]
        y = dc_ref[...]
        lane = lax.broadcasted_iota(jnp.int32, (16, s), 1)
        sh = 1
        while sh < s:
            y = y + jnp.where(lane < s - sh, pltpu.roll(y, s - sh, 1), 0.0)
            sh *= 2
        dz = y * _sigmoid(-zt)
        db_ref[...] = jnp.broadcast_to(jnp.sum(dz, axis=1, keepdims=True), (16, FB_PAD))
        full = jnp.concatenate([dz, jnp.zeros((FB_PAD - 16, s), F32)], axis=0)
        df_ref[...] = full.T.astype(BF16)

    return pl.pallas_call(
        body, name=name, grid=(1,),
        in_specs=[pl.BlockSpec((s, FB_PAD), lambda i: (0, 0)), pl.BlockSpec((1, FB_PAD), lambda i: (0, 0)),
                  pl.BlockSpec((16, s), lambda i: (0, 0))],
        out_specs=[pl.BlockSpec((s, FB_PAD), lambda i: (0, 0)), pl.BlockSpec((16, FB_PAD), lambda i: (0, 0))],
        out_shape=[jax.ShapeDtypeStruct((s, FB_PAD), BF16), jax.ShapeDtypeStruct((16, FB_PAD), F32)],
        compiler_params=_params(("arbitrary",)),
    )(pfb, bpad, dct)


def _swa_window(n):
    ws = pl.multiple_of(jnp.maximum(n * WINDOW - WINDOW, 0), WINDOW)
    qi = lax.broadcasted_iota(jnp.int32, (WINDOW, 2 * WINDOW), 0)
    kj = lax.broadcasted_iota(jnp.int32, (WINDOW, 2 * WINDOW), 1)
    rel = qi + (n * WINDOW - ws) - kj
    valid = (rel >= 0) & (rel < WINDOW)
    return ws, valid, rel.astype(F32)


def _attn_a_fwd(q, k, v, sinks, slopes, name):
    s = q.shape[1]
    nb = s // WINDOW
    smem = pl.BlockSpec(memory_space=pltpu.SMEM)

    def body(sink_ref, slope_ref, q_ref, k_ref, v_ref, o_ref, lse_ref):
        n = pl.program_id(0)
        ws, valid, relf = _swa_window(n)
        outs = []
        for h in range(A_Q_HEADS):
            kvh = h // A_GROUP
            kw = k_ref[kvh, pl.ds(ws, 2 * WINDOW), :]
            vw = v_ref[kvh, pl.ds(ws, 2 * WINDOW), :]
            sc = lax.dot_general(q_ref[h], kw, (((1,), (1,)), ((), ())), preferred_element_type=F32)
            sc = jnp.where(valid, sc - slope_ref[h] * relf, NEG)
            sink = sink_ref[h]
            m = jnp.maximum(jnp.max(sc, axis=1, keepdims=True), sink)
            p = jnp.exp(sc - m)
            denom = jnp.sum(p, axis=1, keepdims=True) + jnp.exp(sink - m)
            pn = (p / denom).astype(BF16)
            outs.append(jnp.dot(pn, vw, preferred_element_type=F32))
            lse_ref[h] = jnp.broadcast_to(m + jnp.log(denom), (WINDOW, HEAD_DIM))
        o_ref[...] = jnp.concatenate(outs, axis=1)

    return pl.pallas_call(
        body, name=name, grid=(nb,),
        in_specs=[smem, smem,
                  pl.BlockSpec((A_Q_HEADS, WINDOW, HEAD_DIM), lambda n: (0, n, 0)),
                  pl.BlockSpec((A_KV_HEADS, s, HEAD_DIM), lambda n: (0, 0, 0)),
                  pl.BlockSpec((A_KV_HEADS, s, HEAD_DIM), lambda n: (0, 0, 0))],
        out_specs=[pl.BlockSpec((WINDOW, A_WIDTH), lambda n: (n, 0)),
                   pl.BlockSpec((A_Q_HEADS, WINDOW, HEAD_DIM), lambda n: (0, n, 0))],
        out_shape=[jax.ShapeDtypeStruct((s, A_WIDTH), F32), jax.ShapeDtypeStruct((A_Q_HEADS, s, HEAD_DIM), F32)],
        compiler_params=_params(("parallel",), VMEM_BIG),
    )(sinks, slopes, q, k, v)


def _attn_a_bwd(q, k, v, do, lse, dd, sinks, slopes, name, comm=None):
    s = q.shape[1]
    nb = s // WINDOW
    smem = pl.BlockSpec(memory_space=pltpu.SMEM)
    last = nb - 1

    def body(sink_ref, slope_ref, q_ref, k_ref, v_ref, do_ref, lse_ref, dd_ref, dq_ref, dkv_ref, ds_ref, carry):
        n = pl.program_id(0)

        @pl.when(n == 0)
        def _():
            carry[...] = jnp.zeros(carry.shape, F32)
            ds_ref[...] = jnp.zeros(ds_ref.shape, F32)

        @pl.when(n < nb)
        def _():
            ws, valid, relf = _swa_window(n)
            dqs = []
            dkw = [None] * A_KV_HEADS
            dvw = [None] * A_KV_HEADS
            for h in range(A_Q_HEADS):
                kvh = h // A_GROUP
                qh = q_ref[h]
                doh = do_ref[h]
                kw = k_ref[kvh, pl.ds(ws, 2 * WINDOW), :]
                vw = v_ref[kvh, pl.ds(ws, 2 * WINDOW), :]
                lse_h = lse_ref[h]
                dd_h = dd_ref[h]
                sc = lax.dot_general(qh, kw, (((1,), (1,)), ((), ())), preferred_element_type=F32)
                sc = jnp.where(valid, sc - slope_ref[h] * relf, NEG)
                p = jnp.exp(sc - lse_h[:, 0:1])
                dp = lax.dot_general(doh, vw, (((1,), (1,)), ((), ())), preferred_element_type=F32)
                dsc = (p * (dp - dd_h[:, 0:1])).astype(BF16)
                pb = p.astype(BF16)
                dqs.append(jnp.dot(dsc, kw, preferred_element_type=F32))
                dk_h = lax.dot_general(dsc, qh, (((0,), (0,)), ((), ())), preferred_element_type=F32)
                dv_h = lax.dot_general(pb, doh, (((0,), (0,)), ((), ())), preferred_element_type=F32)
                dkw[kvh] = dk_h if dkw[kvh] is None else dkw[kvh] + dk_h
                dvw[kvh] = dv_h if dvw[kvh] is None else dvw[kvh] + dv_h
                psink = jnp.exp(sink_ref[h] - lse_h)
                ds_ref[h] += jnp.sum((-psink * dd_h).reshape(WINDOW // 8, 8, HEAD_DIM), axis=0)
            dq_ref[...] = jnp.concatenate(dqs, axis=1)
            win = jnp.concatenate(dkw + dvw, axis=1)
            first = win[0:WINDOW]
            second = win[WINDOW:2 * WINDOW]
            dkv_ref[...] = carry[...] + first
            carry[...] = jnp.where(n == 0, first, second)

        @pl.when(n == nb)
        def _():
            dkv_ref[...] = carry[...]

    hm = lambda heads: pl.BlockSpec((heads, WINDOW, HEAD_DIM), lambda n: (0, jnp.minimum(n, last), 0))
    res = lambda heads: pl.BlockSpec((heads, s, HEAD_DIM), lambda n: (0, 0, 0))
    outs, comm_outs = _hosted_call(
        body, comm, name=name, grid=(nb + 1,),
        in_specs=[smem, smem, hm(A_Q_HEADS), res(A_KV_HEADS), res(A_KV_HEADS), hm(A_Q_HEADS), hm(A_Q_HEADS), hm(A_Q_HEADS)],
        out_specs=[pl.BlockSpec((WINDOW, A_WIDTH), lambda n: (jnp.minimum(n, last), 0)),
                   pl.BlockSpec((WINDOW, 2 * A_KV_WIDTH), lambda n: (jnp.maximum(n - 1, 0), 0)),
                   pl.BlockSpec((A_Q_HEADS, 8, HEAD_DIM), lambda n: (0, 0, 0))],
        out_shape=[jax.ShapeDtypeStruct((s, A_WIDTH), F32), jax.ShapeDtypeStruct((s, 2 * A_KV_WIDTH), F32),
                   jax.ShapeDtypeStruct((A_Q_HEADS, 8, HEAD_DIM), F32)],
        scratch_shapes=[pltpu.VMEM((WINDOW, 2 * A_KV_WIDTH), F32)],
        args=[sinks, slopes, q, k, v, do, lse, dd], sem=("arbitrary",), vmem=VMEM_BIG)
    return outs[0], outs[1], outs[2], comm_outs


def _attn_b_fwd(q, k, v, c3, name, comm=None):
    heads, s, _ = q.shape
    bq = min(512, s)
    nq = s // bq
    nt = (((1,), (1,)), ((), ()))

    def body(q_ref, k_ref, v_ref, c_ref, o_ref, lse_ref, m_scr, l_scr, acc_scr):
        i = pl.program_id(1)
        r0 = pl.multiple_of(i * bq, bq)
        row = lax.broadcasted_iota(jnp.int32, (bq, bq), 0)
        col = lax.broadcasted_iota(jnp.int32, (bq, bq), 1)
        m_scr[...] = jnp.full((2, bq, LANES), NEG, F32)
        l_scr[...] = jnp.zeros((2, bq, LANES), F32)
        acc_scr[...] = jnp.zeros((2, bq, HEAD_DIM), F32)

        def step(j, masked):
            k0 = pl.multiple_of(j * bq, bq)
            for h2 in range(2):
                kv = k_ref[h2, pl.ds(k0, bq), :]
                vv = v_ref[h2, pl.ds(k0, bq), :]
                cq0 = c_ref[h2, :, pl.ds(r0, LANES)][:, 0:1]
                sc = lax.dot_general(q_ref[h2], kv, nt, preferred_element_type=F32)
                sc = sc + (cq0 - c_ref[h2, :, pl.ds(k0, bq)])
                if masked:
                    sc = jnp.where(col <= row, sc, NEG)
                m_prev = m_scr[h2]
                m_new = jnp.maximum(m_prev, jnp.max(sc, axis=1, keepdims=True))
                alpha = jnp.exp(m_prev - m_new)
                p = jnp.exp(sc - m_new[:, 0:1])
                l_scr[h2] = alpha * l_scr[h2] + jnp.sum(p, axis=1, keepdims=True)
                p_hi = p.astype(BF16)
                p_lo = (p - p_hi.astype(F32)).astype(BF16)
                pv = jnp.dot(p_hi, vv, preferred_element_type=F32) + jnp.dot(p_lo, vv, preferred_element_type=F32)
                acc_scr[h2] = acc_scr[h2] * alpha[:, 0:HEAD_DIM] + pv
                m_scr[h2] = m_new

        def loop_body(j, carry):
            step(j, False)
            return carry

        lax.fori_loop(0, i, loop_body, 0)
        step(i, True)
        outs = []
        for h2 in range(2):
            l = l_scr[h2]
            outs.append(acc_scr[h2] / l[:, 0:HEAD_DIM])
            lse_ref[h2] = (m_scr[h2] + jnp.log(l))[:, 0:HEAD_DIM]
        o_ref[...] = jnp.concatenate(outs, axis=1)

    res = pl.BlockSpec((2, s, HEAD_DIM), lambda hp, i: (hp, 0, 0))
    outs, comm_outs = _hosted_call(
        body, comm, name=name, grid=(heads // 2, nq),
        in_specs=[pl.BlockSpec((2, bq, HEAD_DIM), lambda hp, i: (hp, i, 0)), res, res,
                  pl.BlockSpec((2, 1, s), lambda hp, i: (hp, 0, 0))],
        out_specs=[pl.BlockSpec((bq, 2 * HEAD_DIM), lambda hp, i: (i, hp)),
                   pl.BlockSpec((2, bq, HEAD_DIM), lambda hp, i: (hp, i, 0))],
        out_shape=[jax.ShapeDtypeStruct((s, heads * HEAD_DIM), F32), jax.ShapeDtypeStruct((heads, s, HEAD_DIM), F32)],
        scratch_shapes=[pltpu.VMEM((2, bq, LANES), F32), pltpu.VMEM((2, bq, LANES), F32), pltpu.VMEM((2, bq, HEAD_DIM), F32)],
        args=[q, k, v, c3], sem=("parallel", "parallel"), vmem=VMEM_BIG)
    return outs[0], outs[1], comm_outs


def _attn_b_bwd(q, k, v, do, lse, dd, c3, name, comm=None):
    heads, s, _ = q.shape
    bq = min(512, s)
    nq = s // bq
    nt = (((1,), (1,)), ((), ()))
    tn = (((0,), (0,)), ((), ()))
    grid = (heads // 2, nq)

    def body(q_ref, k_ref, v_ref, do_ref, lse_ref, dd_ref, c_ref, dq_ref, dk_ref, dv_ref, dc_ref,
             dq_scr, dk_scr, dv_scr, dc_scr):
        j = pl.program_id(1)
        k0 = pl.multiple_of(j * bq, bq)
        row = lax.broadcasted_iota(jnp.int32, (bq, bq), 0)
        col = lax.broadcasted_iota(jnp.int32, (bq, bq), 1)

        @pl.when(j == 0)
        def _():
            dq_scr[...] = jnp.zeros(dq_scr.shape, F32)

        dk_scr[...] = jnp.zeros((2, HEAD_DIM, bq), F32)
        dv_scr[...] = jnp.zeros((2, HEAD_DIM, bq), F32)
        dc_scr[...] = jnp.zeros((2, 1, bq), F32)
        k_t = [k_ref[h2].T for h2 in range(2)]

        def step(i, masked):
            r0 = pl.multiple_of(i * bq, bq)
            for h2 in range(2):
                kv = k_ref[h2]
                vv = v_ref[h2]
                qv = q_ref[h2, pl.ds(r0, bq), :]
                dov = do_ref[h2, pl.ds(r0, bq), :]
                lse_v = lse_ref[h2, pl.ds(r0, bq), :][:, 0:1]
                dd_v = dd_ref[h2, pl.ds(r0, bq), :][:, 0:1]
                cq0 = c_ref[h2, :, pl.ds(r0, LANES)][:, 0:1]
                sc = lax.dot_general(qv, kv, nt, preferred_element_type=F32) + (cq0 - c_ref[h2, :, pl.ds(k0, bq)])
                if masked:
                    sc = jnp.where(col <= row, sc, NEG)
                p = jnp.exp(sc - lse_v)
                dp = lax.dot_general(dov, vv, nt, preferred_element_type=F32)
                dsc = p * (dp - dd_v)
                dsb = dsc.astype(BF16)
                dv_scr[h2] += jnp.dot(dov.T, p.astype(BF16), preferred_element_type=F32)
                dk_scr[h2] += jnp.dot(qv.T, dsb, preferred_element_type=F32)
                dq_scr[h2, :, pl.ds(r0, bq)] += jnp.dot(k_t[h2], dsb.T, preferred_element_type=F32)
                dc_scr[h2] -= jnp.sum(dsc, axis=0, keepdims=True)

        def loop_body(i, carry):
            step(i, False)
            return carry

        step(j, True)
        lax.fori_loop(j + 1, nq, loop_body, 0)
        dc_ref[...] = dc_scr[...]
        dk_ref[...] = jnp.concatenate([dk_scr[0].T, dk_scr[1].T], axis=1)
        dv_ref[...] = jnp.concatenate([dv_scr[0].T, dv_scr[1].T], axis=1)

        @pl.when(j == nq - 1)
        def _():
            dq_ref[...] = jnp.concatenate([dq_scr[0].T, dq_scr[1].T], axis=1)

    res = pl.BlockSpec((2, s, HEAD_DIM), lambda hp, j: (hp, 0, 0))
    blk = pl.BlockSpec((2, bq, HEAD_DIM), lambda hp, j: (hp, j, 0))
    tm = jax.ShapeDtypeStruct((s, heads * HEAD_DIM), F32)
    in_specs = [res, blk, blk, res, res, res, pl.BlockSpec((2, 1, s), lambda hp, j: (hp, 0, 0))]
    out_specs = [pl.BlockSpec((s, 2 * HEAD_DIM), lambda hp, j: (0, hp)),
                 pl.BlockSpec((bq, 2 * HEAD_DIM), lambda hp, j: (j, hp)),
                 pl.BlockSpec((bq, 2 * HEAD_DIM), lambda hp, j: (j, hp)),
                 pl.BlockSpec((2, 1, bq), lambda hp, j: (hp, 0, j))]
    out_shape = [tm, tm, tm, jax.ShapeDtypeStruct((heads, 1, s), F32)]
    scratch = [pltpu.VMEM((2, HEAD_DIM, s), F32), pltpu.VMEM((2, HEAD_DIM, bq), F32),
               pltpu.VMEM((2, HEAD_DIM, bq), F32), pltpu.VMEM((2, 1, bq), F32)]
    outs, comm_outs = _hosted_call(
        body, comm, name=name, grid=grid, in_specs=in_specs, out_specs=out_specs, out_shape=out_shape,
        scratch_shapes=scratch, args=[q, k, v, do, lse, dd, c3], sem=("parallel", "arbitrary"), vmem=VMEM_BIG)
    return outs[0], outs[1], outs[2], outs[3], comm_outs


def _attn_c_probs(qh, mkh):
    sc = lax.dot_general(qh, mkh, (((1,), (1,)), ((), ())), preferred_element_type=F32) * (C_HEAD_DIM ** -0.5)
    p = jnp.exp(sc - jnp.max(sc, axis=1, keepdims=True))
    return p / jnp.sum(p, axis=1, keepdims=True)


def _attn_c_fwd(q, mkv, name):
    s = q.shape[0]
    m = mkv.shape[0]
    bq = _tile(s, 512, 8)

    def body(q_ref, mk_ref, mv_ref, o_ref):
        outs = []
        for h in range(C_HEADS):
            sl = slice(h * C_HEAD_DIM, (h + 1) * C_HEAD_DIM)
            pn = _attn_c_probs(q_ref[:, sl], mk_ref[:, sl]).astype(BF16)
            outs.append(jnp.dot(pn, mv_ref[:, sl], preferred_element_type=F32))
        o_ref[...] = jnp.concatenate(outs, axis=1)

    return pl.pallas_call(
        body, name=name, grid=(s // bq,),
        in_specs=[pl.BlockSpec((bq, C_WIDTH), lambda i: (i, 0)), pl.BlockSpec((m, C_WIDTH), lambda i: (0, 0)),
                  pl.BlockSpec((m, C_WIDTH), lambda i: (0, 1))],
        out_specs=pl.BlockSpec((bq, C_WIDTH), lambda i: (i, 0)),
        out_shape=jax.ShapeDtypeStruct((s, C_WIDTH), F32),
        compiler_params=_params(("parallel",)),
    )(q, mkv, mkv)


def _attn_c_bwd(q, mkv, do, name):
    s = q.shape[0]
    m = mkv.shape[0]
    bq = _tile(s, 512, 8)
    tn = (((0,), (0,)), ((), ()))

    def body(q_ref, mk_ref, mv_ref, do_ref, dq_ref, dm_ref):
        i = pl.program_id(0)

        @pl.when(i == 0)
        def _():
            dm_ref[...] = jnp.zeros(dm_ref.shape, F32)

        dqs = []
        for h in range(C_HEADS):
            sl = slice(h * C_HEAD_DIM, (h + 1) * C_HEAD_DIM)
            qh, mkh, mvh, doh = q_ref[:, sl], mk_ref[:, sl], mv_ref[:, sl], do_ref[:, sl]
            pn = _attn_c_probs(qh, mkh)
            dp = lax.dot_general(doh, mvh, (((1,), (1,)), ((), ())), preferred_element_type=F32)
            dsc = (pn * (dp - jnp.sum(pn * dp, axis=1, keepdims=True)) * (C_HEAD_DIM ** -0.5)).astype(BF16)
            dqs.append(jnp.dot(dsc, mkh, preferred_element_type=F32))
            dm_ref[:, sl] += lax.dot_general(dsc, qh, tn, preferred_element_type=F32)
            sv = slice(C_WIDTH + h * C_HEAD_DIM, C_WIDTH + (h + 1) * C_HEAD_DIM)
            dm_ref[:, sv] += lax.dot_general(pn.astype(BF16), doh, tn, preferred_element_type=F32)
        dq_ref[...] = jnp.concatenate(dqs, axis=1)

    row = pl.BlockSpec((bq, C_WIDTH), lambda i: (i, 0))
    return pl.pallas_call(
        body, name=name, grid=(s // bq,),
        in_specs=[row, pl.BlockSpec((m, C_WIDTH), lambda i: (0, 0)), pl.BlockSpec((m, C_WIDTH), lambda i: (0, 1)), row],
        out_specs=[row, pl.BlockSpec((m, 2 * C_WIDTH), lambda i: (0, 0))],
        out_shape=[jax.ShapeDtypeStruct((s, C_WIDTH), F32), jax.ShapeDtypeStruct((m, 2 * C_WIDTH), F32)],
        compiler_params=_params(("arbitrary",)),
    )(q, mkv, mkv, do)


def _gate_fwd(y, proj, zc0, bw, name):
    rows, width = y.shape
    bm = _tile(rows, 2048 if bw <= 256 else 1024, 16)
    cb0 = zc0 // bw

    def body(y_ref, z_ref, o_ref):
        z = z_ref[...].astype(F32)
        o_ref[...] = (y_ref[...] * (z * _sigmoid(z))).astype(BF16)

    return pl.pallas_call(
        body, name=name, grid=(rows // bm, width // bw),
        in_specs=[pl.BlockSpec((bm, bw), lambda i, t: (i, t)), pl.BlockSpec((bm, bw), lambda i, t: (i, cb0 + t))],
        out_specs=pl.BlockSpec((bm, bw), lambda i, t: (i, t)),
        out_shape=jax.ShapeDtypeStruct((rows, width), BF16),
        compiler_params=_params(("parallel", "parallel")),
    )(y, proj)


def _gate_bwd(dsv, y, proj, zc0, bw, dproj, t0, head_major, name):
    rows, width = y.shape
    bm = _tile(rows, 2048 if bw <= 256 else 1024, 16)
    cb0 = zc0 // bw
    tb0 = t0 // bw
    bd = _block_diag(HEAD_DIM)
    hpb = bw // HEAD_DIM

    def body(*refs):
        if head_major:
            ds_ref, y_ref, z_ref, bd_ref, _, dp_ref, dy_ref, dd_ref = refs
        else:
            ds_ref, y_ref, z_ref, _, dp_ref, dy_ref = refs
        z = z_ref[...].astype(F32)
        sig = _sigmoid(z)
        dsx = ds_ref[...]
        yv = y_ref[...]
        dy = dsx * (z * sig)
        dp_ref[...] = (dsx * yv * (sig * (1.0 + z * (1.0 - sig)))).astype(BF16)
        if head_major:
            dyb = dy.astype(BF16)
            dd = _seg_sum(dyb.astype(F32) * yv, bd_ref[...])
            for h in range(hpb):
                sl = slice(h * HEAD_DIM, (h + 1) * HEAD_DIM)
                dy_ref[h] = dyb[:, sl]
                dd_ref[h] = dd[:, sl]
        else:
            dy_ref[...] = dy.astype(BF16)

    tile = pl.BlockSpec((bm, bw), lambda i, t: (i, t))
    ztile = pl.BlockSpec((bm, bw), lambda i, t: (i, cb0 + t))
    ttile = pl.BlockSpec((bm, bw), lambda i, t: (i, tb0 + t))
    any_spec = pl.BlockSpec(memory_space=pl.ANY)
    dp_shape = jax.ShapeDtypeStruct(dproj.shape, BF16)
    if head_major:
        hm_spec = pl.BlockSpec((hpb, bm, HEAD_DIM), lambda i, t: (t, i, 0))
        nh = width // HEAD_DIM
        outs = pl.pallas_call(
            body, name=name, grid=(rows // bm, width // bw),
            in_specs=[tile, tile, ztile, pl.BlockSpec((LANES, LANES), lambda i, t: (0, 0)), any_spec],
            out_specs=[ttile, hm_spec, hm_spec],
            out_shape=[dp_shape, jax.ShapeDtypeStruct((nh, rows, HEAD_DIM), BF16),
                       jax.ShapeDtypeStruct((nh, rows, HEAD_DIM), F32)],
            input_output_aliases={4: 0},
            compiler_params=_params(("parallel", "parallel")),
        )(dsv, y, proj, bd, dproj)
        return outs[0], outs[1], outs[2]
    outs = pl.pallas_call(
        body, name=name, grid=(rows // bm, width // bw),
        in_specs=[tile, tile, ztile, any_spec],
        out_specs=[ttile, tile],
        out_shape=[dp_shape, jax.ShapeDtypeStruct((rows, width), BF16)],
        input_output_aliases={3: 0},
        compiler_params=_params(("parallel", "parallel")),
    )(dsv, y, proj, dproj)
    return outs[0], outs[1], None


def _merge_fwd(proj, ua, ub, uc, name):
    rows, d = ua.shape
    bm = _tile(rows, 1024, 16)
    bw = _tile(d, 512)
    g0 = COL_GATE // bw
    gstep = d // bw

    def body(la_ref, lb_ref, lc_ref, ua_ref, ub_ref, uc_ref, o_ref, ga_ref, gb_ref, gc_ref):
        y = None
        for l_ref, u_ref, g_ref in ((la_ref, ua_ref, ga_ref), (lb_ref, ub_ref, gb_ref), (lc_ref, uc_ref, gc_ref)):
            g = _sigmoid(l_ref[...].astype(F32))
            g_ref[...] = g.astype(BF16)
            term = g * u_ref[...].astype(F32)
            y = term if y is None else y + term
        o_ref[...] = y.astype(BF16)

    tile = pl.BlockSpec((bm, bw), lambda i, t: (i, t))
    gate = lambda b: pl.BlockSpec((bm, bw), lambda i, t: (i, g0 + b * gstep + t))
    shape = jax.ShapeDtypeStruct((rows, d), BF16)
    return pl.pallas_call(
        body, name=name, grid=(rows // bm, d // bw),
        in_specs=[gate(0), gate(1), gate(2), tile, tile, tile],
        out_specs=[tile] * 4, out_shape=[shape] * 4,
        compiler_params=_params(("parallel", "parallel")),
    )(proj, proj, proj, ua, ub, uc)


def _merge_bwd(dym, us, gs, name):
    rows, d = dym.shape
    bm = _tile(rows, 1024, 16)
    bw = _tile(d, 512)
    nb = d // bw

    def body(dy_ref, ua_ref, ub_ref, uc_ref, ga_ref, gb_ref, gc_ref, dg_ref, da_ref, db_ref, dc_ref):
        b = pl.program_id(2)
        dyv = dy_ref[...]
        for idx, (u_ref, g_ref, du_ref) in enumerate(((ua_ref, ga_ref, da_ref), (ub_ref, gb_ref, db_ref), (uc_ref, gc_ref, dc_ref))):
            @pl.when(b == idx)
            def _():
                g = g_ref[...].astype(F32)
                du_ref[...] = (g * dyv).astype(BF16)
                dg_ref[...] = (dyv * u_ref[...].astype(F32) * g * (1.0 - g)).astype(BF16)

    tile = pl.BlockSpec((bm, bw), lambda i, t, b: (i, t))
    shape = jax.ShapeDtypeStruct((rows, d), BF16)
    outs = pl.pallas_call(
        body, name=name, grid=(rows // bm, nb, 3),
        in_specs=[tile] * 7,
        out_specs=[pl.BlockSpec((bm, bw), lambda i, t, b: (i, b * nb + t)), tile, tile, tile],
        out_shape=[jax.ShapeDtypeStruct((rows, 3 * d), BF16), shape, shape, shape],
        compiler_params=_params(("parallel", "parallel", "arbitrary")),
    )(dym, *us, *gs)
    return outs[0], outs[1], outs[2], outs[3]


def _out_proj_loss(ym, wo, x, target, name):
    m, d = x.shape
    bm, bn = _tile(m, 1024, 16), _tile(d, 1024)
    grid = (m // bm, d // bn)

    def body(a_ref, b_ref, x_ref, t_ref, dy_ref, dyb_ref, l_ref):
        first, _ = _grid_edges(grid)
        y = jnp.dot(a_ref[...], b_ref[...], preferred_element_type=F32) + x_ref[...]
        diff = y - t_ref[...]
        dy = diff * (1.0 / d)
        dy_ref[...] = dy
        dyb_ref[...] = dy.astype(BF16)
        sq = diff * diff
        part = sq[:, 0:LANES]
        for c in range(1, bn // LANES):
            part = part + sq[:, c * LANES:(c + 1) * LANES]
        part = jnp.sum(part.reshape(bm // 8, 8, LANES), axis=0)

        @pl.when(first)
        def _():
            l_ref[...] = part

        @pl.when(jnp.logical_not(first))
        def _():
            l_ref[...] += part

    tile = pl.BlockSpec((bm, bn), lambda i, j: (i, j))
    return pl.pallas_call(
        body, name=name, grid=grid,
        in_specs=[pl.BlockSpec((bm, d), lambda i, j: (i, 0)), pl.BlockSpec((d, bn), lambda i, j: (0, j)), tile, tile],
        out_specs=[tile, tile, pl.BlockSpec((8, LANES), lambda i, j: (0, 0))],
        out_shape=[jax.ShapeDtypeStruct((m, d), F32), jax.ShapeDtypeStruct((m, d), BF16),
                   jax.ShapeDtypeStruct((8, LANES), F32)],
        compiler_params=_params(("arbitrary", "arbitrary"), VMEM_BIG),
    )(ym, wo, x, target)


def _row(vec, reps=1):
    return jnp.tile(vec.reshape(1, -1).astype(F32), (1, reps))


def _local_step(x, mem, target, small, wg, shards=None):
    s, d = x.shape
    dist = shards is not None
    wg = dict(wg)
    ones = lambda n: jnp.ones((1, n), F32)
    zeros = lambda n: jnp.zeros((1, n), F32)
    scale_ab = HEAD_DIM ** -0.5
    split8 = lambda g: g.reshape(N_DEV, g.shape[0] // N_DEV, g.shape[1])
    flat8 = lambda g: g.reshape(g.shape[0] * g.shape[1], g.shape[2])
    gather = lambda names: _Comm("gather", [shards[n] for n in names]) if dist else None
    g = {}

    def scatter(names):
        return _Comm("scatter", [split8(g[n]) for n in names]) if dist else None

    def hosted(result, names, store):
        if not dist:
            return result
        out, got = result
        store.update(zip(names, got))
        return out

    hn = _rmsnorm_fwd(x, small["norm_gain"], "rms_x_fwd")
    got = {}
    proj = hosted(_mm_nn(hn, wg["qkv"], bm=1024, bn=1024, bk=d, o_dtype=BF16, name="proj_qkv",
                         comm=gather(("wa", "wb", "wc"))), ("wa", "wb", "wc"), got)
    wg.update({n: flat8(a) for n, a in got.items()})
    pfb = _mm_nn(hn, wg["wf"], bm=1024, bn=FB_PAD, bk=d, o_dtype=F32, name="proj_fb")
    mn = _rmsnorm_fwd(mem, small["mem_norm_gain"], "rms_mem_fwd")
    mkv = _mm_nn(mn, wg["wk"], bm=256, bn=1024, bk=d, o_dtype=F32, name="mem_kv")

    gain_a = jnp.concatenate([_row(small["q_gain_a"], A_Q_HEADS) * scale_ab, _row(small["k_gain_a"], A_KV_HEADS), ones(A_KV_WIDTH)], axis=1)
    flag_a = jnp.concatenate([ones(A_WIDTH + A_KV_WIDTH), zeros(A_KV_WIDTH)], axis=1)
    qkv_a = _headnorm_fwd(proj, COL_QA, 1280, 1280, HEAD_DIM, gain_a, flag_a, True, "hn_a_fwd")
    gain_b = jnp.concatenate([_row(small["q_gain_b"], B_HEADS) * scale_ab, _row(small["k_gain_b"], B_HEADS), ones(B_WIDTH)], axis=1)
    flag_b = jnp.concatenate([ones(2 * B_WIDTH), zeros(B_WIDTH)], axis=1)
    qkv_b = _headnorm_fwd(proj, COL_QB, 2304, 256, HEAD_DIM, gain_b, flag_b, True, "hn_b_fwd")
    gain_cq = _row(small["q_gain_c"], C_HEADS)
    q_c = _headnorm_fwd(proj, COL_QC, C_WIDTH, C_WIDTH, C_HEAD_DIM, gain_cq, ones(C_WIDTH), False, "hn_cq_fwd")
    gain_ck = jnp.concatenate([_row(small["k_gain_c"], C_HEADS), ones(C_WIDTH)], axis=1)
    flag_ck = jnp.concatenate([ones(C_WIDTH), zeros(C_WIDTH)], axis=1)
    mkvn = _headnorm_fwd(mkv, 0, 2 * C_WIDTH, 2 * C_WIDTH, C_HEAD_DIM, gain_ck, flag_ck, False, "hn_ck_fwd")

    q_a, k_a, v_a = qkv_a[0:12], qkv_a[12:16], qkv_a[16:20]
    q_b, k_b, v_b = qkv_b[0:12], qkv_b[12:24], qkv_b[24:36]

    bpad = jnp.pad(small["b_forget"].reshape(1, -1), ((0, 0), (0, FB_PAD - B_HEADS)))
    c16 = _fox_prep(pfb, bpad, "fox_prep")
    c3 = c16[0:B_HEADS].reshape(B_HEADS, 1, s)

    sinks = small["sinks_a"].reshape(-1)
    slopes = jnp.exp2(-8.0 * jnp.arange(1, A_Q_HEADS + 1, dtype=F32) / A_Q_HEADS)
    y_a, lse_a = _attn_a_fwd(q_a, k_a, v_a, sinks, slopes, "attn_a_fwd")
    y_b, lse_b, got_zg = _attn_b_fwd(q_b, k_b, v_b, c3, "attn_b_fwd", comm=gather(("zg",)))
    if dist:
        wg["zg"] = flat8(got_zg[0])
    y_c = _attn_c_fwd(q_c, mkvn, "attn_c_fwd")

    got = {}
    pzg = hosted(_mm_nn(hn, wg["zg"], bm=1024, bn=1024, bk=d, o_dtype=BF16, name="proj_zg", comm=gather(("wo",))),
                 ("wo",), got)
    wg.update({n: flat8(a) for n, a in got.items()})

    s_a = _gate_fwd(y_a, pzg, COL_ZA, 256, "gate_a_fwd")
    s_b = _gate_fwd(y_b, pzg, COL_ZB, 256, "gate_b_fwd")
    s_c = _gate_fwd(y_c, pzg, COL_ZC, 512, "gate_c_fwd")
    w_a, w_b, w_c = _branch_full(wg["wa"]), _branch_full(wg["wb"]), _branch_full(wg["wc"])
    u_a = _mm_nn(s_a, w_a, bm=1024, bn=2048, bk=A_WIDTH, o_dtype=BF16, name="branch_a_fwd")
    u_b = _mm_nn(s_b, w_b, bm=1024, bn=2048, bk=B_WIDTH, o_dtype=BF16, name="branch_b_fwd")
    u_c = _mm_nn(s_c, w_c, bm=1024, bn=2048, bk=C_WIDTH, o_dtype=BF16, name="branch_c_fwd")
    ym, gate_a, gate_b, gate_c = _merge_fwd(pzg, u_a, u_b, u_c, "merge_fwd")
    dy, dyb, lpart = _out_proj_loss(ym, wg["wo"], x, target, "out_proj_loss")
    loss = 0.5 / d * jnp.sum(lpart)

    dym = _mm_nt(dyb, wg["wo"], bm=1024, bn=1024, bk=d, o_dtype=F32, name="out_proj_bwd_act")
    g["wo"] = _mm_tn(ym, dyb, bm=512, bn=1024, bk=s, o_dtype=BF16, name="out_proj_bwd_w")

    dgate, du_a, du_b, du_c = _merge_bwd(dym, (u_a, u_b, u_c), (gate_a, gate_b, gate_c), "merge_bwd")
    parts = {}
    g["wm_g"] = hosted(_mm_tn(hn, dgate, bm=512, bn=1024, bk=s, o_dtype=BF16, name="proj_gate_bwd_w",
                              comm=scatter(("wo",))), ("wo",), parts)

    ds_a = _mm_nt(du_a, w_a, bm=1024, bn=A_WIDTH, bk=d, o_dtype=F32, name="branch_a_bwd_act")
    ds_b = _mm_nt(du_b, w_b, bm=1024, bn=B_WIDTH, bk=d, o_dtype=F32, name="branch_b_bwd_act")
    ds_c = _mm_nt(du_c, w_c, bm=1024, bn=C_WIDTH, bk=d, o_dtype=F32, name="branch_c_bwd_act")
    g["wa"] = _branch_shards(_mm_tn(s_a, du_a, bm=A_WIDTH, bn=1024, bk=s, o_dtype=BF16, name="branch_a_bwd_w"))
    g["wb"] = _branch_shards(_mm_tn(s_b, du_b, bm=B_WIDTH, bn=1024, bk=s, o_dtype=BF16, name="branch_b_bwd_w"))
    g["wc"] = _branch_shards(_mm_tn(s_c, du_c, bm=C_WIDTH, bn=1024, bk=s, o_dtype=BF16, name="branch_c_bwd_w"))

    dz = lax.empty((s, W_Z), BF16)
    dz, do_a, dd_a = _gate_bwd(ds_a, y_a, pzg, COL_ZA, 256, dz, COL_ZA, True, "gate_a_bwd")
    dz, do_b, dd_b = _gate_bwd(ds_b, y_b, pzg, COL_ZB, 256, dz, COL_ZB, True, "gate_b_bwd")
    dz, do_c, _ = _gate_bwd(ds_c, y_c, pzg, COL_ZC, 512, dz, COL_ZC, False, "gate_c_bwd")
    g["wm_z"] = _mm_tn(hn, dz, bm=512, bn=1024, bk=s, o_dtype=BF16, name="proj_z_bwd_w")

    names = ("wa", "wb", "wc")
    dq_a, dkv_a, dsink, got = _attn_a_bwd(q_a, k_a, v_a, do_a, lse_a, dd_a, sinks, slopes, "attn_a_bwd", comm=scatter(names))
    parts.update(zip(names, got))
    names = ("wm_g", "wm_z")
    dq_b, dk_b, dv_b, dc3, got = _attn_b_bwd(q_b, k_b, v_b, do_b, lse_b, dd_b, c3, "attn_b_bwd", comm=scatter(names))
    parts.update(zip(names, got))
    dq_c, dmkvn = _attn_c_bwd(q_c, mkvn, do_c, "attn_c_bwd")

    dqkv = lax.empty((s, W_QKV), BF16)
    dqkv, dg_qa = _headnorm_bwd(proj, COL_QA, A_WIDTH, 256, HEAD_DIM, gain_a[:, 0:768], flag_a[:, 0:768], dq_a, dqkv, COL_QA, "hn_qa_bwd")
    dqkv, dg_kva = _headnorm_bwd(proj, COL_KA, 512, 256, HEAD_DIM, gain_a[:, 768:1280], flag_a[:, 768:1280], dkv_a, dqkv, COL_KA, "hn_kva_bwd")
    dqkv, dg_qb = _headnorm_bwd(proj, COL_QB, B_WIDTH, 256, HEAD_DIM, gain_b[:, 0:768], flag_b[:, 0:768], dq_b, dqkv, COL_QB, "hn_qb_bwd")
    dqkv, dg_kb = _headnorm_bwd(proj, COL_KB, B_WIDTH, 256, HEAD_DIM, gain_b[:, 768:1536], flag_b[:, 768:1536], dk_b, dqkv, COL_KB, "hn_kb_bwd")
    dqkv, _ = _headnorm_bwd(proj, COL_VB, B_WIDTH, 256, HEAD_DIM, gain_b[:, 1536:2304], flag_b[:, 1536:2304], dv_b, dqkv, COL_VB, "hn_vb_bwd")
    dqkv, dg_qc = _headnorm_bwd(proj, COL_QC, C_WIDTH, 512, C_HEAD_DIM, gain_cq, ones(C_WIDTH), dq_c, dqkv, COL_QC, "hn_qc_bwd")
    dmkv, dg_kc = _headnorm_bwd(mkv, 0, 2 * C_WIDTH, 2 * C_WIDTH, C_HEAD_DIM, gain_ck, flag_ck, dmkvn, None, 0, "hn_kc_bwd")

    dct = jnp.pad(dc3.reshape(B_HEADS, s), ((0, 16 - B_HEADS), (0, 0)))
    dfb, dbf = _fox_prep_bwd(pfb, bpad, dct, "fox_prep_bwd")

    dmn = _mm_nt(dmkv, wg["wk"], bm=256, bn=1024, bk=1024, o_dtype=F32, name="mem_kv_bwd_act")
    g["wk"] = _mm_tn(mn, dmkv, bm=512, bn=1024, bk=mem.shape[0], o_dtype=BF16, name="mem_kv_bwd_w")
    _, dg_mem = _rmsnorm_bwd(mem, dmn, small["mem_norm_gain"], None, "rms_mem_bwd")

    g["wm_qkv"] = _mm_tn(hn, dqkv, bm=512, bn=1024, bk=s, o_dtype=BF16, name="proj_qkv_bwd_w")
    g["wf"] = _mm_tn(hn, dfb, bm=512, bn=FB_PAD, bk=s, o_dtype=BF16, name="proj_fb_bwd_w")
    half = W_QKV // 2
    g["wm_q1"], g["wm_q2"] = g["wm_qkv"][:, 0:half], g["wm_qkv"][:, half:W_QKV]
    names = ("wm_q1",)
    dhn = hosted(_mm_nt_sum([(dqkv, wg["qkv"], 0), (dfb, wg["wf"], 0)], bm=1024, bn=1024, bk=2048,
                            name="proj_qkv_bwd_act", comm=scatter(names)), names, parts)
    names = ("wm_q2", "wf", "wk")
    dhn = hosted(_mm_nt_sum([(dz, wg["zg"], COL_ZA), (dgate, wg["zg"], COL_GATE)], bm=1024, bn=1024, bk=2048,
                            name="proj_zg_bwd_act", add=dhn, comm=scatter(names)), names, parts)
    if dist:
        g = parts
    grad_x, dg_x = _rmsnorm_bwd(x, dhn, small["norm_gain"], dy, "rms_x_bwd")

    fold = lambda part, heads, hd: jnp.sum(jnp.sum(part, axis=0).reshape(heads, hd), axis=0).reshape(1, hd)
    small_grads = {
        "norm_gain": jnp.sum(dg_x, axis=0).reshape(1, d),
        "mem_norm_gain": jnp.sum(dg_mem, axis=0).reshape(1, d),
        "b_forget": dbf[0:B_HEADS, 0].reshape(1, B_HEADS),
        "q_gain_a": fold(dg_qa, A_Q_HEADS, HEAD_DIM) * scale_ab,
        "k_gain_a": fold(dg_kva[:, 0:A_KV_WIDTH], A_KV_HEADS, HEAD_DIM),
        "sinks_a": (jnp.sum(dsink, axis=(1, 2)) * (1.0 / HEAD_DIM)).reshape(1, A_Q_HEADS),
        "q_gain_b": fold(dg_qb, B_HEADS, HEAD_DIM) * scale_ab,
        "k_gain_b": fold(dg_kb, B_HEADS, HEAD_DIM),
        "q_gain_c": fold(dg_qc, C_HEADS, C_HEAD_DIM),
        "k_gain_c": fold(dg_kc[:, 0:C_WIDTH], C_HEADS, C_HEAD_DIM),
    }
    return loss, grad_x, small_grads, g


def _coords():
    return lax.axis_index("x"), lax.axis_index("y"), lax.axis_index("c")


def _all_gather(shards, name):
    n = len(shards)

    def body(*refs):
        ins = refs[0:n]
        outs = refs[n:2 * n]
        send_sems, recv_sems, local_sems = refs[2 * n:2 * n + 3]
        x, y, c = _coords()
        me, sibling = (x, y, c), (x, y, 1 - c)
        chips = [(1 - x, y), (x, 1 - y), (1 - x, 1 - y)]
        idx = lambda p: 4 * p[0] + 2 * p[1] + p[2]

        def copy(a, k, block, to, src=None):
            slot = outs[a].at[idx(block)]
            return pltpu.make_async_remote_copy(
                src_ref=slot if src is None else src, dst_ref=slot,
                send_sem=send_sems.at[a, k], recv_sem=recv_sems.at[a, k], device_id=to, device_id_type=MESH)

        mine = [pltpu.make_async_copy(ins[a], outs[a].at[idx(me)], local_sems.at[a]) for a in range(n)]
        for cp in mine:
            cp.start()
        first = []
        for a in range(n):
            first.append(copy(a, 0, me, sibling, src=ins[a]))
            first += [copy(a, 1 + j, me, (*chip, c), src=ins[a]) for j, chip in enumerate(chips)]
        for cp in first:
            cp.start()
        passed = []
        for j, chip in enumerate(chips):
            for a in range(n):
                copy(a, 1 + j, (*chip, c), me).wait_recv()
                fwd = copy(a, 4 + j, (*chip, c), sibling)
                fwd.start()
                passed.append(fwd)
        for a in range(n):
            copy(a, 0, sibling, me).wait_recv()
            for j, chip in enumerate(chips):
                copy(a, 4 + j, (*chip, 1 - c), me).wait_recv()
        for cp in first + passed:
            cp.wait_send()
        for cp in mine:
            cp.wait()

    any_spec = pl.BlockSpec(memory_space=pl.ANY)
    return pl.pallas_call(
        body, name=name,
        in_specs=[any_spec] * n, out_specs=[any_spec] * n,
        out_shape=[jax.ShapeDtypeStruct((N_DEV,) + sh.shape, sh.dtype) for sh in shards],
        scratch_shapes=[pltpu.SemaphoreType.DMA((n, 7)), pltpu.SemaphoreType.DMA((n, 7)), pltpu.SemaphoreType.DMA((n,))],
    )(*shards)


def _all_reduce_small(vec, name):
    p = vec.shape[1]

    def body(v_ref, o_ref, gather, send_sems, recv_sems):
        x, y, c = _coords()
        my = 4 * x + 2 * y + c
        peers = [(x ^ ((k >> 2) & 1), y ^ ((k >> 1) & 1), c ^ (k & 1)) for k in range(1, N_DEV)]
        gather[my] = v_ref[...]
        sends = [pltpu.make_async_remote_copy(
            src_ref=v_ref, dst_ref=gather.at[my], send_sem=send_sems.at[k], recv_sem=recv_sems.at[k],
            device_id=peer, device_id_type=MESH) for k, peer in enumerate(peers)]
        for cp in sends:
            cp.start()
        for k, peer in enumerate(peers):
            pid = 4 * peer[0] + 2 * peer[1] + peer[2]
            pltpu.make_async_remote_copy(
                src_ref=v_ref, dst_ref=gather.at[pid], send_sem=send_sems.at[k], recv_sem=recv_sems.at[k],
                device_id=peer, device_id_type=MESH).wait_recv()
        for cp in sends:
            cp.wait_send()
        total = gather[0]
        for j in range(1, N_DEV):
            total = total + gather[j]
        o_ref[...] = total

    vm = pl.BlockSpec(memory_space=pltpu.VMEM)
    return pl.pallas_call(
        body, name=name, in_specs=[vm], out_specs=vm,
        out_shape=jax.ShapeDtypeStruct((8, p), F32),
        scratch_shapes=[pltpu.VMEM((N_DEV, 8, p), F32), pltpu.SemaphoreType.DMA((7,)), pltpu.SemaphoreType.DMA((7,))],
    )(vec)[0:1]


def _sum_parts(parts, name):
    _, rows, cols = parts.shape
    br = _tile(rows, 64, 16)

    def body(p_ref, o_ref):
        total = p_ref[0].astype(F32)
        for j in range(1, N_DEV):
            total = total + p_ref[j].astype(F32)
        o_ref[...] = total

    return pl.pallas_call(
        body, name=name, grid=(rows // br,),
        in_specs=[pl.BlockSpec((N_DEV, br, cols), lambda i: (0, i, 0))],
        out_specs=pl.BlockSpec((br, cols), lambda i: (i, 0)),
        out_shape=jax.ShapeDtypeStruct((rows, cols), F32),
        compiler_params=_params(("parallel",), VMEM_BIG),
    )(parts)


def _adamw(w, g, m, v, name, br=32):
    rows, cols = w.shape
    br = min(br, rows)
    c1 = 1.0 / (1.0 - ADAM_B1 ** ADAM_STEP)
    c2 = 1.0 / (1.0 - ADAM_B2 ** ADAM_STEP)

    def body(w_ref, g_ref, m_ref, v_ref, d_ref, nm_ref, nv_ref):
        gv = g_ref[...]
        nm = ADAM_B1 * m_ref[...] + (1.0 - ADAM_B1) * gv
        nv = ADAM_B2 * v_ref[...] + (1.0 - ADAM_B2) * (gv * gv)
        d_ref[...] = -ADAM_LR * ((nm * c1) / (jnp.sqrt(nv * c2) + ADAM_EPS) + ADAM_WD * w_ref[...])
        nm_ref[...] = nm
        nv_ref[...] = nv

    spec = pl.BlockSpec((br, cols), lambda i: (i, 0))
    shape = jax.ShapeDtypeStruct((rows, cols), F32)
    return pl.pallas_call(
        body, name=name, grid=(pl.cdiv(rows, br),), in_specs=[spec] * 4, out_specs=[spec] * 3, out_shape=[shape] * 3,
        compiler_params=_params(("parallel",), VMEM_BIG),
    )(w, g, m, v)


def _adamw_t(wt, g, mt, vt, name, br=1024):
    n, r = wt.shape
    c1 = 1.0 / (1.0 - ADAM_B1 ** ADAM_STEP)
    c2 = 1.0 / (1.0 - ADAM_B2 ** ADAM_STEP)

    def body(w_ref, g_ref, m_ref, v_ref, d_ref, nm_ref, nv_ref):
        gv = g_ref[...].T
        nm = ADAM_B1 * m_ref[...] + (1.0 - ADAM_B1) * gv
        nv = ADAM_B2 * v_ref[...] + (1.0 - ADAM_B2) * (gv * gv)
        d_ref[...] = -ADAM_LR * ((nm * c1) / (jnp.sqrt(nv * c2) + ADAM_EPS) + ADAM_WD * w_ref[...])
        nm_ref[...] = nm
        nv_ref[...] = nv

    spec = pl.BlockSpec((br, r), lambda i: (i, 0))
    shape = jax.ShapeDtypeStruct((n, r), F32)
    return pl.pallas_call(
        body, name=name, grid=(pl.cdiv(n, br),),
        in_specs=[spec, pl.BlockSpec((r, br), lambda i: (0, i)), spec, spec], out_specs=[spec] * 3, out_shape=[shape] * 3,
        compiler_params=_params(("parallel",), VMEM_BIG),
    )(wt, g, mt, vt)


def _adamw_parts(w, parts, m, v, name):
    rows, cols = w.shape
    br = _tile(rows, 32, 16)
    c1 = 1.0 / (1.0 - ADAM_B1 ** ADAM_STEP)
    c2 = 1.0 / (1.0 - ADAM_B2 ** ADAM_STEP)

    def body(w_ref, p_ref, m_ref, v_ref, g_ref, d_ref, nm_ref, nv_ref):
        gv = p_ref[0].astype(F32)
        for j in range(1, N_DEV):
            gv = gv + p_ref[j].astype(F32)
        nm = ADAM_B1 * m_ref[...] + (1.0 - ADAM_B1) * gv
        nv = ADAM_B2 * v_ref[...] + (1.0 - ADAM_B2) * (gv * gv)
        g_ref[...] = gv
        d_ref[...] = -ADAM_LR * ((nm * c1) / (jnp.sqrt(nv * c2) + ADAM_EPS) + ADAM_WD * w_ref[...])
        nm_ref[...] = nm
        nv_ref[...] = nv

    spec = pl.BlockSpec((br, cols), lambda i: (i, 0))
    shape = jax.ShapeDtypeStruct((rows, cols), F32)
    return pl.pallas_call(
        body, name=name, grid=(rows // br,),
        in_specs=[spec, pl.BlockSpec((N_DEV, br, cols), lambda i: (0, i, 0)), spec, spec],
        out_specs=[spec] * 4, out_shape=[shape] * 4,
        compiler_params=_params(("parallel",), VMEM_BIG),
    )(w, parts, m, v)


SMALL_NAMES = ("norm_gain", "mem_norm_gain", "b_forget", "q_gain_a", "k_gain_a", "sinks_a",
               "q_gain_b", "k_gain_b", "q_gain_c", "k_gain_c")
BIG_NAMES = ("w_in", "w_mem_kv", "w_branch_a", "w_branch_b", "w_branch_c", "w_out")
WEIGHT_ORDER = ("norm_gain", "mem_norm_gain", "w_in", "b_forget", "q_gain_a", "k_gain_a", "sinks_a", "q_gain_b",
                "k_gain_b", "q_gain_c", "k_gain_c", "w_mem_kv", "w_branch_a", "w_branch_b", "w_branch_c", "w_out")


def _pack_small(tree):
    flat = jnp.concatenate([tree[n].reshape(1, -1) for n in SMALL_NAMES], axis=1)
    pad = (-flat.shape[1]) % LANES
    return jnp.pad(flat, ((0, 0), (0, pad)))


def _unpack_small(flat, like):
    out, off = {}, 0
    for n in SMALL_NAMES:
        size = like[n].size
        out[n] = flat[:, off:off + size].reshape(like[n].shape)
        off += size
    return out


def kernel(x, mem, norm_gain, mem_norm_gain, w_in, b_forget, q_gain_a, k_gain_a, sinks_a, q_gain_b, k_gain_b, q_gain_c, k_gain_c, w_mem_kv, w_branch_a, w_branch_b, w_branch_c, w_out, loss_target, m_norm_gain, m_mem_norm_gain, m_w_in, m_b_forget, m_q_gain_a, m_k_gain_a, m_sinks_a, m_q_gain_b, m_k_gain_b, m_q_gain_c, m_k_gain_c, m_w_mem_kv, m_w_branch_a, m_w_branch_b, m_w_branch_c, m_w_out, v_norm_gain, v_mem_norm_gain, v_w_in, v_b_forget, v_q_gain_a, v_k_gain_a, v_sinks_a, v_q_gain_b, v_k_gain_b, v_q_gain_c, v_k_gain_c, v_w_mem_kv, v_w_branch_a, v_w_branch_b, v_w_branch_c, v_w_out):
    weights = dict(norm_gain=norm_gain, mem_norm_gain=mem_norm_gain, w_in=w_in, b_forget=b_forget, q_gain_a=q_gain_a,
                   k_gain_a=k_gain_a, sinks_a=sinks_a, q_gain_b=q_gain_b, k_gain_b=k_gain_b, q_gain_c=q_gain_c,
                   k_gain_c=k_gain_c, w_mem_kv=w_mem_kv, w_branch_a=w_branch_a, w_branch_b=w_branch_b,
                   w_branch_c=w_branch_c, w_out=w_out)
    mom_m = dict(norm_gain=m_norm_gain, mem_norm_gain=m_mem_norm_gain, w_in=m_w_in, b_forget=m_b_forget,
                 q_gain_a=m_q_gain_a, k_gain_a=m_k_gain_a, sinks_a=m_sinks_a, q_gain_b=m_q_gain_b, k_gain_b=m_k_gain_b,
                 q_gain_c=m_q_gain_c, k_gain_c=m_k_gain_c, w_mem_kv=m_w_mem_kv, w_branch_a=m_w_branch_a,
                 w_branch_b=m_w_branch_b, w_branch_c=m_w_branch_c, w_out=m_w_out)
    mom_v = dict(norm_gain=v_norm_gain, mem_norm_gain=v_mem_norm_gain, w_in=v_w_in, b_forget=v_b_forget,
                 q_gain_a=v_q_gain_a, k_gain_a=v_k_gain_a, sinks_a=v_sinks_a, q_gain_b=v_q_gain_b, k_gain_b=v_k_gain_b,
                 q_gain_c=v_q_gain_c, k_gain_c=v_k_gain_c, w_mem_kv=v_w_mem_kv, w_branch_a=v_w_branch_a,
                 w_branch_b=v_w_branch_b, w_branch_c=v_w_branch_c, w_out=v_w_out)
    wi = w_in[0]
    sh_qkv = jnp.concatenate([wi[:, a:b] for a, b in SRC_RANGES[0:3]], axis=1).astype(BF16)
    sh_zg = jnp.concatenate([wi[:, a:b] for a, b in SRC_RANGES[3:6]] + [wi[:, SRC_GATE:]], axis=1).astype(BF16)
    sh_wf = jnp.pad(wi[:, FB_SRC:FB_SRC + B_HEADS], ((0, 0), (0, FB_PAD - B_HEADS))).astype(BF16)
    shards = {"zg": sh_zg, "wo": w_out[0].astype(BF16), "wa": w_branch_a[0].astype(BF16),
              "wb": w_branch_b[0].astype(BF16), "wc": w_branch_c[0].astype(BF16)}
    first = ("qkv", "wf", "wk")
    full = _all_gather([sh_qkv, sh_wf, w_mem_kv[0].astype(BF16)], "weights_all_gather")
    wg = {kname: arr.reshape(arr.shape[0] * arr.shape[1], arr.shape[2]) for kname, arr in zip(first, full)}

    small = {n: weights[n] for n in SMALL_NAMES}
    loss_local, grad_x, small_g, parts = _local_step(x[0], mem[0], loss_target[0], small, wg, shards)

    grads, delta, new_m, new_v = {}, {}, {}, {}
    for n, kname in (("w_mem_kv", "wk"), ("w_out", "wo"), ("w_branch_a", "wa"), ("w_branch_b", "wb"), ("w_branch_c", "wc")):
        gsum, dlt, nm, nv = _adamw_parts(weights[n][0], parts[kname], mom_m[n][0], mom_v[n][0], "adamw_" + n)
        grads[n], delta[n], new_m[n], new_v[n] = gsum, dlt[None], nm[None], nv[None]
    g1, g2, gz, gf, gg = (_sum_parts(parts[k], "grad_sum_" + k) for k in ("wm_q1", "wm_q2", "wm_z", "wf", "wm_g"))
    half = W_QKV // 2
    g_in = jnp.concatenate([g1[:, COL_QA:COL_QB], gz[:, COL_ZA:COL_ZB], g1[:, COL_QB:half], g2[:, 0:COL_QC - half],
                            gz[:, COL_ZB:COL_ZC], gf[:, 0:B_HEADS], g2[:, COL_QC - half:half], gz[:, COL_ZC:W_Z], gg], axis=1)
    dlt, nm, nv = _adamw_t(w_in[0].T, g_in, m_w_in[0].T, v_w_in[0].T, "adamw_w_in")
    grads["w_in"], delta["w_in"], new_m["w_in"], new_v["w_in"] = g_in, dlt.T[None], nm.T[None], nv.T[None]

    packed = _pack_small(small_g)
    reduced = _all_reduce_small(jnp.broadcast_to(packed, (8, packed.shape[1])), "small_all_reduce")
    grads.update(_unpack_small(reduced, small))

    loss = lax.psum(loss_local, ("x", "y", "c"))

    pw, pm, pv = _pack_small(small), _pack_small({n: mom_m[n] for n in SMALL_NAMES}), _pack_small({n: mom_v[n] for n in SMALL_NAMES})
    rep8 = lambda a: jnp.broadcast_to(a, (8, a.shape[1]))
    dlt, nm, nv = _adamw(rep8(pw), rep8(reduced), rep8(pm), rep8(pv), "adamw_small")
    for tree, flat in ((delta, dlt), (new_m, nm), (new_v, nv)):
        tree.update(_unpack_small(flat[0:1], small))
    for n in BIG_NAMES:
        grads[n] = grads[n][None]
    return (loss, grad_x[None], *[grads[n] for n in WEIGHT_ORDER], *[delta[n] for n in WEIGHT_ORDER],
            *[new_m[n] for n in WEIGHT_ORDER], *[new_v[n] for n in WEIGHT_ORDER])
```

```python
import math

import jax
import jax.numpy as jnp
import numpy as np
from jax import lax
from jax.experimental import pallas as pl
from jax.experimental.pallas import tpu as pltpu

F32 = jnp.float32
BF16 = jnp.bfloat16

N_DEV = 8
HEAD_DIM = 64
A_Q_HEADS = 12
A_KV_HEADS = 4
A_GROUP = 3
B_HEADS = 12
C_HEADS = 4
C_HEAD_DIM = 128
WINDOW = 128
A_WIDTH = 768
A_KV_WIDTH = 256
B_WIDTH = 768
C_WIDTH = 512
EPS = 1e-6
NEG = -1e30

COL_QA, COL_KA, COL_VA = 0, 768, 1024
COL_QB, COL_KB, COL_VB = 1280, 2048, 2816
COL_QC = 3584
W_QKV = 4096
COL_ZA, COL_ZB, COL_ZC = 0, 768, 1536
COL_GATE = W_Z = 2048
SRC_RANGES = ((0, 1280), (2048, 4352), (5132, 5644), (1280, 2048), (4352, 5120), (5644, 6156))
SRC_GATE = 6156
FB_SRC = 5120
FB_PAD = 128

ADAM_LR = 0.001
ADAM_B1 = 0.9
ADAM_B2 = 0.999
ADAM_EPS = 1e-08
ADAM_WD = 0.01
ADAM_STEP = 10

VMEM_BIG = 52 * 1024 * 1024
LANES = 128
MESH = pl.DeviceIdType.MESH


def _tile(n, pref, mult=128):
    if n <= pref:
        return n
    t = (pref // mult) * mult
    while t >= mult:
        if n % t == 0:
            return t
        t -= mult
    return n


def _params(sem=None, vmem=None):
    kw = {}
    if sem is not None:
        kw["dimension_semantics"] = sem
    if vmem is not None:
        kw["vmem_limit_bytes"] = vmem
    return pltpu.CompilerParams(**kw)


def _sigmoid(x):
    return 1.0 / (1.0 + jnp.exp(-x))


def _block_diag(hd):
    r = np.arange(LANES)
    return jnp.asarray((r[:, None] // hd) == (r[None, :] // hd), dtype=BF16)


def _seg_sum(t, bd):
    hi = t.astype(BF16)
    lo = (t - hi.astype(F32)).astype(BF16)
    outs = []
    for c in range(t.shape[1] // LANES):
        sl = slice(c * LANES, (c + 1) * LANES)
        outs.append(jnp.dot(hi[:, sl], bd, preferred_element_type=F32) + jnp.dot(lo[:, sl], bd, preferred_element_type=F32))
    return outs[0] if len(outs) == 1 else jnp.concatenate(outs, axis=1)


def _rmsnorm_fwd(x, gain, name):
    rows, d = x.shape
    bm = _tile(rows, 512, 8)

    def body(x_ref, g_ref, o_ref):
        xv = x_ref[...]
        ms = jnp.mean(xv * xv, axis=-1, keepdims=True)
        o_ref[...] = (xv * lax.rsqrt(ms + EPS) * g_ref[...]).astype(BF16)

    return pl.pallas_call(
        body, name=name, grid=(rows // bm,),
        in_specs=[pl.BlockSpec((bm, d), lambda i: (i, 0)), pl.BlockSpec((1, d), lambda i: (0, 0))],
        out_specs=pl.BlockSpec((bm, d), lambda i: (i, 0)),
        out_shape=jax.ShapeDtypeStruct((rows, d), BF16),
        compiler_params=_params(("parallel",)),
    )(x, gain)


def _rmsnorm_bwd(x, dhn, gain, dy, name):
    rows, d = x.shape
    bm = _tile(rows, 512, 8)
    with_dx = dy is not None

    def body(*refs):
        if with_dx:
            x_ref, dh_ref, g_ref, dy_ref, gx_ref, dg_ref = refs
        else:
            x_ref, dh_ref, g_ref, dg_ref = refs
        i = pl.program_id(0)
        xv = x_ref[...]
        rstd = lax.rsqrt(jnp.mean(xv * xv, axis=-1, keepdims=True) + EPS)
        xhat = xv * rstd
        dh = dh_ref[...]
        part = jnp.sum((dh * xhat).reshape(bm // 8, 8, d), axis=0)

        @pl.when(i == 0)
        def _():
            dg_ref[...] = part

        @pl.when(i > 0)
        def _():
            dg_ref[...] += part

        if with_dx:
            g = dh * g_ref[...]
            mean = jnp.mean(g * xhat, axis=-1, keepdims=True)
            gx_ref[...] = dy_ref[...] + rstd * (g - xhat * mean)

    row_spec = pl.BlockSpec((bm, d), lambda i: (i, 0))
    in_specs = [row_spec, row_spec, pl.BlockSpec((1, d), lambda i: (0, 0))]
    args = [x, dhn, gain]
    dg_spec = pl.BlockSpec((8, d), lambda i: (0, 0))
    dg_shape = jax.ShapeDtypeStruct((8, d), F32)
    if with_dx:
        in_specs.append(row_spec)
        args.append(dy)
        out_specs = [row_spec, dg_spec]
        out_shape = [jax.ShapeDtypeStruct((rows, d), F32), dg_shape]
    else:
        out_specs = [dg_spec]
        out_shape = [dg_shape]
    outs = pl.pallas_call(
        body, name=name, grid=(rows // bm,), in_specs=in_specs, out_specs=out_specs, out_shape=out_shape,
        compiler_params=_params(("arbitrary",), VMEM_BIG),
    )(*args)
    return outs if with_dx else (None, outs[0])


class _Comm:
    def __init__(self, kind, arrays):
        self.kind = kind
        self.arrays = list(arrays)
        self.n = len(self.arrays)

    def out_shapes(self):
        if self.kind == "gather":
            return [jax.ShapeDtypeStruct((N_DEV,) + a.shape, a.dtype) for a in self.arrays]
        return [jax.ShapeDtypeStruct(a.shape, a.dtype) for a in self.arrays]

    def scratch(self):
        return [pltpu.SemaphoreType.DMA((self.n, N_DEV - 1)), pltpu.SemaphoreType.DMA((self.n, N_DEV - 1)),
                pltpu.SemaphoreType.DMA((self.n,))]

    def _plan(self, ins, outs, sems, with_recvs):
        send_sems, recv_sems, local_sems = sems
        x, y, c = lax.axis_index("x"), lax.axis_index("y"), lax.axis_index("c")
        my = 4 * x + 2 * y + c
        gather = self.kind == "gather"
        local, sends, recvs = [], [], []
        for a in range(self.n):
            local.append(pltpu.make_async_copy(ins[a] if gather else ins[a].at[my], outs[a].at[my], local_sems.at[a]))
            for k in range(1, N_DEV):
                peer = (x ^ ((k >> 2) & 1), y ^ ((k >> 1) & 1), c ^ (k & 1))
                pid = 4 * peer[0] + 2 * peer[1] + peer[2]
                src = ins[a] if gather else ins[a].at[pid]
                sem = dict(send_sem=send_sems.at[a, k - 1], recv_sem=recv_sems.at[a, k - 1], device_id=peer, device_id_type=MESH)
                sends.append(pltpu.make_async_remote_copy(src_ref=src, dst_ref=outs[a].at[my], **sem))
                if with_recvs:
                    recvs.append(pltpu.make_async_remote_copy(src_ref=src, dst_ref=outs[a].at[pid], **sem))
        return local, sends, recvs

    def start(self, ins, outs, sems):
        local, sends, _ = self._plan(ins, outs, sems, False)
        for cp in local + sends:
            cp.start()

    def wait(self, ins, outs, sems):
        local, sends, recvs = self._plan(ins, outs, sems, True)
        for cp in recvs:
            cp.wait_recv()
        for cp in sends:
            cp.wait_send()
        for cp in local:
            cp.wait()


def _grid_edges(grid):
    first = last = None
    for ax, size in enumerate(grid):
        pid = pl.program_id(ax)
        f, l = pid == 0, pid == size - 1
        first = f if first is None else first & f
        last = l if last is None else last & l
    return first, last


def _hosted_call(body, comm, *, name, grid, in_specs, out_specs, out_shape, scratch_shapes, args, sem, vmem=None):
    in_specs, out_specs, out_shape, scratch_shapes = list(in_specs), list(out_specs), list(out_shape), list(scratch_shapes)
    if comm is None:
        res = pl.pallas_call(body, name=name, grid=grid, in_specs=in_specs, out_specs=out_specs, out_shape=out_shape,
                             scratch_shapes=scratch_shapes, compiler_params=_params(sem, vmem))(*args)
        return list(res), []
    n_in, n_out, n_scr, nc = len(in_specs), len(out_shape), len(scratch_shapes), comm.n

    def hosted(*refs):
        ins = refs[0:n_in]
        comm_in = refs[n_in:n_in + nc]
        outs = refs[n_in + nc:n_in + nc + n_out]
        comm_out = refs[n_in + nc + n_out:n_in + 2 * nc + n_out]
        scr = refs[n_in + 2 * nc + n_out:n_in + 2 * nc + n_out + n_scr]
        sems = refs[n_in + 2 * nc + n_out + n_scr:]
        first, last = _grid_edges(grid)

        @pl.when(first)
        def _():
            comm.start(comm_in, comm_out, sems)

        body(*ins, *outs, *scr)

        @pl.when(last)
        def _():
            comm.wait(comm_in, comm_out, sems)

    any_spec = pl.BlockSpec(memory_space=pl.ANY)
    res = pl.pallas_call(
        hosted, name=name, grid=grid, in_specs=in_specs + [any_spec] * nc, out_specs=out_specs + [any_spec] * nc,
        out_shape=out_shape + comm.out_shapes(), scratch_shapes=scratch_shapes + comm.scratch(),
        compiler_params=_params(("arbitrary",) * len(grid), vmem),
    )(*args, *comm.arrays)
    return list(res[0:n_out]), list(res[n_out:])


def _mm(a, b, *, grid, a_spec, b_spec, o_spec, o_shape, o_dtype, contract, name, add=None, add_spec=None, acc_shape=None,
        comm=None):
    nk = grid[2]
    has_add = add is not None

    def body(*refs):
        a_ref, b_ref = refs[0], refs[1]
        add_ref = refs[2] if has_add else None
        o_ref = refs[3] if has_add else refs[2]
        part = lax.dot_general(a_ref[...], b_ref[...], (contract, ((), ())), preferred_element_type=F32)
        if nk == 1:
            if has_add:
                part = part + add_ref[...]
            o_ref[...] = part.astype(o_dtype)
        else:
            acc = refs[-1]
            k = pl.program_id(2)

            @pl.when(k == 0)
            def _():
                acc[...] = part

            @pl.when(k > 0)
            def _():
                acc[...] += part

            @pl.when(k == nk - 1)
            def _():
                r = acc[...]
                if has_add:
                    r = r + add_ref[...]
                o_ref[...] = r.astype(o_dtype)

    in_specs = [a_spec, b_spec] + ([add_spec] if has_add else [])
    args = [a, b] + ([add] if has_add else [])
    scratch = [pltpu.VMEM(acc_shape, F32)] if nk > 1 else []
    outs, comm_outs = _hosted_call(
        body, comm, name=name, grid=grid, in_specs=in_specs, out_specs=[o_spec],
        out_shape=[jax.ShapeDtypeStruct(o_shape, o_dtype)], scratch_shapes=scratch, args=args,
        sem=("parallel", "parallel", "arbitrary"), vmem=VMEM_BIG)
    return outs[0] if comm is None else (outs[0], comm_outs)


def _mm_nn(a, b, *, bm, bn, bk, o_dtype, name, add=None, comm=None):
    m, kd = a.shape
    n = b.shape[1]
    bm, bn, bk = _tile(m, bm, 8), _tile(n, bn), _tile(kd, bk)
    o_spec = pl.BlockSpec((bm, bn), lambda i, j, k: (i, j))
    return _mm(a, b, grid=(m // bm, n // bn, kd // bk),
               a_spec=pl.BlockSpec((bm, bk), lambda i, j, k: (i, k)),
               b_spec=pl.BlockSpec((bk, bn), lambda i, j, k: (k, j)),
               o_spec=o_spec, o_shape=(m, n), o_dtype=o_dtype, contract=((1,), (0,)), name=name,
               add=add, add_spec=o_spec, acc_shape=(bm, bn), comm=comm)


def _mm_nt(a, b, *, bm, bn, bk, o_dtype, name, add=None, b_col0=0, comm=None):
    m, kd = a.shape
    n = b.shape[0]
    bm, bn, bk = _tile(m, bm, 8), _tile(n, bn), _tile(math.gcd(kd, b_col0), bk)
    kb0 = b_col0 // bk
    o_spec = pl.BlockSpec((bm, bn), lambda i, j, k: (i, j))
    return _mm(a, b, grid=(m // bm, n // bn, kd // bk),
               a_spec=pl.BlockSpec((bm, bk), lambda i, j, k: (i, k)),
               b_spec=pl.BlockSpec((bn, bk), lambda i, j, k: (j, kb0 + k)),
               o_spec=o_spec, o_shape=(m, n), o_dtype=o_dtype, contract=((1,), (1,)), name=name,
               add=add, add_spec=o_spec, acc_shape=(bm, bn), comm=comm)


def _mm_nt_sum(terms, *, bm, bn, bk, name, add=None, comm=None):
    m = terms[0][0].shape[0]
    n = terms[0][1].shape[0]
    bm, bn = _tile(m, bm, 8), _tile(n, bn)
    nt = (((1,), (1,)), ((), ()))
    plan, groups, start = [], [], 0
    for a, b, col0 in terms:
        kd = a.shape[1]
        tk = _tile(math.gcd(kd, col0), bk)
        steps = kd // tk
        last = groups[-1] if groups else None
        if last is not None and last[0] is b and last[4] == tk and (last[3] + last[2]) * tk == col0:
            last[2] += steps
        else:
            groups.append([b, start, steps, col0 // tk, tk])
        plan.append((start, steps, len(groups) - 1))
        start += steps
    nk = start
    nterm, ngroup, has_add = len(terms), len(groups), add is not None

    def body(*refs):
        a_refs, b_refs = refs[0:nterm], refs[nterm:nterm + ngroup]
        add_ref = refs[nterm + ngroup] if has_add else None
        o_ref, acc = refs[nterm + ngroup + has_add], refs[nterm + ngroup + has_add + 1]
        k = pl.program_id(2)
        for t, (s0, steps, grp) in enumerate(plan):
            @pl.when((k >= s0) & (k < s0 + steps))
            def _():
                part = lax.dot_general(a_refs[t][...], b_refs[grp][...], nt, preferred_element_type=F32)

                @pl.when(k == 0)
                def _():
                    acc[...] = part

                @pl.when(k > 0)
                def _():
                    acc[...] += part

        @pl.when(k == nk - 1)
        def _():
            o_ref[...] = acc[...] + add_ref[...] if has_add else acc[...]

    def a_spec(tk, s0, steps):
        return pl.BlockSpec((bm, tk), lambda i, j, k: (i, jnp.clip(k - s0, 0, steps - 1)))

    def b_spec(tk, s0, steps, off):
        return pl.BlockSpec((bn, tk), lambda i, j, k: (j, off + jnp.clip(k - s0, 0, steps - 1)))

    o_spec = pl.BlockSpec((bm, bn), lambda i, j, k: (i, j))
    in_specs = [a_spec(groups[grp][4], s0, steps) for s0, steps, grp in plan]
    in_specs += [b_spec(tk, s0, steps, cb0) for _, s0, steps, cb0, tk in groups]
    args = [a for a, _, _ in terms] + [grp[0] for grp in groups]
    if has_add:
        in_specs.append(o_spec)
        args.append(add)
    outs, comm_outs = _hosted_call(
        body, comm, name=name, grid=(m // bm, n // bn, nk), in_specs=in_specs,
        out_specs=[o_spec], out_shape=[jax.ShapeDtypeStruct((m, n), F32)],
        scratch_shapes=[pltpu.VMEM((bm, bn), F32)], args=args,
        sem=("parallel", "parallel", "arbitrary"), vmem=VMEM_BIG)
    return outs[0] if comm is None else (outs[0], comm_outs)


def _mm_tn(a, b, *, bm, bn, bk, o_dtype, name, comm=None):
    kd, m = a.shape
    n = b.shape[1]
    bm, bn, bk = _tile(m, bm), _tile(n, bn), _tile(kd, bk, 8)
    return _mm(a, b, grid=(m // bm, n // bn, kd // bk),
               a_spec=pl.BlockSpec((bk, bm), lambda i, j, k: (k, i)),
               b_spec=pl.BlockSpec((bk, bn), lambda i, j, k: (k, j)),
               o_spec=pl.BlockSpec((bm, bn), lambda i, j, k: (i, j)),
               o_shape=(m, n), o_dtype=o_dtype, contract=((0,), (0,)), name=name, acc_shape=(bm, bn), comm=comm)


def _branch_full(w8):
    kb, ds = w8.shape[0] // N_DEV, w8.shape[1]
    return w8.reshape(N_DEV, kb, ds).transpose(1, 0, 2).reshape(kb, N_DEV * ds)


def _branch_shards(g):
    kb, ds = g.shape[0], g.shape[1] // N_DEV
    return g.reshape(kb, N_DEV, ds).transpose(1, 0, 2).reshape(N_DEV * kb, ds)


def _headnorm_fwd(src, c0, width, bw, hd, gain, nflag, head_major, name):
    rows = src.shape[0]
    bm = _tile(rows, 2048 if bw <= 256 else 1024, 16)
    bd = _block_diag(hd)
    cb0 = c0 // bw

    def body(x_ref, g_ref, f_ref, bd_ref, o_ref):
        xv = x_ref[...].astype(F32)
        ss = _seg_sum(xv * xv, bd_ref[...])
        rstd = lax.rsqrt(ss * (1.0 / hd) + EPS)
        y = (xv * jnp.where(f_ref[...] > 0.0, rstd, 1.0) * g_ref[...]).astype(BF16)
        if head_major:
            for h in range(bw // HEAD_DIM):
                o_ref[h] = y[:, h * HEAD_DIM:(h + 1) * HEAD_DIM]
        else:
            o_ref[...] = y

    vec_spec = pl.BlockSpec((1, bw), lambda i, t: (0, t))
    if head_major:
        hpb = bw // HEAD_DIM
        out_spec = pl.BlockSpec((hpb, bm, HEAD_DIM), lambda i, t: (t, i, 0))
        out_shape = jax.ShapeDtypeStruct((width // HEAD_DIM, rows, HEAD_DIM), BF16)
    else:
        out_spec = pl.BlockSpec((bm, bw), lambda i, t: (i, t))
        out_shape = jax.ShapeDtypeStruct((rows, width), BF16)
    return pl.pallas_call(
        body, name=name, grid=(rows // bm, width // bw),
        in_specs=[pl.BlockSpec((bm, bw), lambda i, t: (i, cb0 + t)), vec_spec, vec_spec,
                  pl.BlockSpec((LANES, LANES), lambda i, t: (0, 0))],
        out_specs=out_spec, out_shape=out_shape,
        compiler_params=_params(("parallel", "parallel")),
    )(src, gain, nflag, bd)


def _headnorm_bwd(src, c0, width, bw, hd, gain, nflag, dyn, target, t0, name):
    rows = src.shape[0]
    bm = _tile(rows, 2048 if bw <= 256 else 1024, 16)
    bd = _block_diag(hd)
    cb0 = c0 // bw
    tb0 = t0 // bw
    aliased = target is not None

    def body(*refs):
        if aliased:
            x_ref, dy_ref, g_ref, f_ref, bd_ref, _, o_ref, dg_ref = refs
        else:
            x_ref, dy_ref, g_ref, f_ref, bd_ref, o_ref, dg_ref = refs
        i = pl.program_id(1)
        xv = x_ref[...].astype(F32)
        dyv = dy_ref[...]
        bdv = bd_ref[...]
        rstd = lax.rsqrt(_seg_sum(xv * xv, bdv) * (1.0 / hd) + EPS)
        xhat = xv * rstd
        g = dyv * g_ref[...]
        mean = _seg_sum(g * xhat, bdv) * (1.0 / hd)
        dx = jnp.where(f_ref[...] > 0.0, rstd * (g - xhat * mean), g)
        o_ref[...] = dx.astype(BF16)
        part = jnp.sum((dyv * xhat).reshape(bm // 8, 8, bw), axis=0)

        @pl.when(i == 0)
        def _():
            dg_ref[...] = part

        @pl.when(i > 0)
        def _():
            dg_ref[...] += part

    vec_spec = pl.BlockSpec((1, bw), lambda t, i: (0, t))
    in_specs = [pl.BlockSpec((bm, bw), lambda t, i: (i, cb0 + t)), pl.BlockSpec((bm, bw), lambda t, i: (i, t)),
                vec_spec, vec_spec, pl.BlockSpec((LANES, LANES), lambda t, i: (0, 0))]
    args = [src, dyn, gain, nflag, bd]
    aliases = {}
    if aliased:
        in_specs.append(pl.BlockSpec(memory_space=pl.ANY))
        args.append(target)
        aliases = {5: 0}
        o_shape = jax.ShapeDtypeStruct(target.shape, BF16)
    else:
        o_shape = jax.ShapeDtypeStruct((rows, width), BF16)
    out, dg = pl.pallas_call(
        body, name=name, grid=(width // bw, rows // bm), in_specs=in_specs,
        out_specs=[pl.BlockSpec((bm, bw), lambda t, i: (i, tb0 + t)), pl.BlockSpec((8, bw), lambda t, i: (0, t))],
        out_shape=[o_shape, jax.ShapeDtypeStruct((8, width), F32)],
        input_output_aliases=aliases,
        compiler_params=_params(("parallel", "arbitrary")),
    )(*args)
    return out, dg


def _fox_prep(pfb, bpad, name):
    s = pfb.shape[0]

    def body(p_ref, b_ref, c_ref):
        z = p_ref[...] + b_ref[...]
        logf = jnp.minimum(z, 0.0) - jnp.log(1.0 + jnp.exp(-jnp.abs(z)))
        x = logf.T[0:16, :]
        lane = lax.broadcasted_iota(jnp.int32, (16, s), 1)
        sh = 1
        while sh < s:
            x = x + jnp.where(lane >= sh, pltpu.roll(x, sh, 1), 0.0)
            sh *= 2
        c_ref[...] = x

    return pl.pallas_call(
        body, name=name, grid=(1,),
        in_specs=[pl.BlockSpec((s, FB_PAD), lambda i: (0, 0)), pl.BlockSpec((1, FB_PAD), lambda i: (0, 0))],
        out_specs=pl.BlockSpec((16, s), lambda i: (0, 0)),
        out_shape=jax.ShapeDtypeStruct((16, s), F32),
        compiler_params=_params(("arbitrary",)),
    )(pfb, bpad)


def _fox_prep_bwd(pfb, bpad, dct, name):
    s = pfb.shape[0]

    def body(p_ref, b_ref, dc_ref, df_ref, db_ref):
        zt = (p_ref[...] + b_ref[...]).T[0:16, :]
        y = dc_ref[...]
        lane = lax.broadcasted_iota(jnp.int32, (16, s), 1)
        sh = 1
        while sh < s:
            y = y + jnp.where(lane < s - sh, pltpu.roll(y, s - sh, 1), 0.0)
            sh *= 2
        dz = y * _sigmoid(-zt)
        db_ref[...] = jnp.broadcast_to(jnp.sum(dz, axis=1, keepdims=True), (16, FB_PAD))
        full = jnp.concatenate([dz, jnp.zeros((FB_PAD - 16, s), F32)], axis=0)
        df_ref[...] = full.T.astype(BF16)

    return pl.pallas_call(
        body, name=name, grid=(1,),
        in_specs=[pl.BlockSpec((s, FB_PAD), lambda i: (0, 0)), pl.BlockSpec((1, FB_PAD), lambda i: (0, 0)),
                  pl.BlockSpec((16, s), lambda i: (0, 0))],
        out_specs=[pl.BlockSpec((s, FB_PAD), lambda i: (0, 0)), pl.BlockSpec((16, FB_PAD), lambda i: (0, 0))],
        out_shape=[jax.ShapeDtypeStruct((s, FB_PAD), BF16), jax.ShapeDtypeStruct((16, FB_PAD), F32)],
        compiler_params=_params(("arbitrary",)),
    )(pfb, bpad, dct)


def _swa_window(n):
    ws = pl.multiple_of(jnp.maximum(n * WINDOW - WINDOW, 0), WINDOW)
    qi = lax.broadcasted_iota(jnp.int32, (WINDOW, 2 * WINDOW), 0)
    kj = lax.broadcasted_iota(jnp.int32, (WINDOW, 2 * WINDOW), 1)
    rel = qi + (n * WINDOW - ws) - kj
    valid = (rel >= 0) & (rel < WINDOW)
    return ws, valid, rel.astype(F32)


def _attn_a_fwd(q, k, v, sinks, slopes, name):
    s = q.shape[1]
    nb = s // WINDOW
    smem = pl.BlockSpec(memory_space=pltpu.SMEM)

    def body(sink_ref, slope_ref, q_ref, k_ref, v_ref, o_ref, lse_ref):
        n = pl.program_id(0)
        ws, valid, relf = _swa_window(n)
        outs = []
        for h in range(A_Q_HEADS):
            kvh = h // A_GROUP
            kw = k_ref[kvh, pl.ds(ws, 2 * WINDOW), :]
            vw = v_ref[kvh, pl.ds(ws, 2 * WINDOW), :]
            sc = lax.dot_general(q_ref[h], kw, (((1,), (1,)), ((), ())), preferred_element_type=F32)
            sc = jnp.where(valid, sc - slope_ref[h] * relf, NEG)
            sink = sink_ref[h]
            m = jnp.maximum(jnp.max(sc, axis=1, keepdims=True), sink)
            p = jnp.exp(sc - m)
            denom = jnp.sum(p, axis=1, keepdims=True) + jnp.exp(sink - m)
            pn = (p / denom).astype(BF16)
            outs.append(jnp.dot(pn, vw, preferred_element_type=F32))
            lse_ref[h] = jnp.broadcast_to(m + jnp.log(denom), (WINDOW, HEAD_DIM))
        o_ref[...] = jnp.concatenate(outs, axis=1)

    return pl.pallas_call(
        body, name=name, grid=(nb,),
        in_specs=[smem, smem,
                  pl.BlockSpec((A_Q_HEADS, WINDOW, HEAD_DIM), lambda n: (0, n, 0)),
                  pl.BlockSpec((A_KV_HEADS, s, HEAD_DIM), lambda n: (0, 0, 0)),
                  pl.BlockSpec((A_KV_HEADS, s, HEAD_DIM), lambda n: (0, 0, 0))],
        out_specs=[pl.BlockSpec((WINDOW, A_WIDTH), lambda n: (n, 0)),
                   pl.BlockSpec((A_Q_HEADS, WINDOW, HEAD_DIM), lambda n: (0, n, 0))],
        out_shape=[jax.ShapeDtypeStruct((s, A_WIDTH), F32), jax.ShapeDtypeStruct((A_Q_HEADS, s, HEAD_DIM), F32)],
        compiler_params=_params(("parallel",), VMEM_BIG),
    )(sinks, slopes, q, k, v)


def _attn_a_bwd(q, k, v, do, lse, dd, sinks, slopes, name, comm=None):
    s = q.shape[1]
    nb = s // WINDOW
    smem = pl.BlockSpec(memory_space=pltpu.SMEM)
    last = nb - 1

    def body(sink_ref, slope_ref, q_ref, k_ref, v_ref, do_ref, lse_ref, dd_ref, dq_ref, dkv_ref, ds_ref, carry):
        n = pl.program_id(0)

        @pl.when(n == 0)
        def _():
            carry[...] = jnp.zeros(carry.shape, F32)
            ds_ref[...] = jnp.zeros(ds_ref.shape, F32)

        @pl.when(n < nb)
        def _():
            ws, valid, relf = _swa_window(n)
            dqs = []
            dkw = [None] * A_KV_HEADS
            dvw = [None] * A_KV_HEADS
            for h in range(A_Q_HEADS):
                kvh = h // A_GROUP
                qh = q_ref[h]
                doh = do_ref[h]
                kw = k_ref[kvh, pl.ds(ws, 2 * WINDOW), :]
                vw = v_ref[kvh, pl.ds(ws, 2 * WINDOW), :]
                lse_h = lse_ref[h]
                dd_h = dd_ref[h]
                sc = lax.dot_general(qh, kw, (((1,), (1,)), ((), ())), preferred_element_type=F32)
                sc = jnp.where(valid, sc - slope_ref[h] * relf, NEG)
                p = jnp.exp(sc - lse_h[:, 0:1])
                dp = lax.dot_general(doh, vw, (((1,), (1,)), ((), ())), preferred_element_type=F32)
                dsc = (p * (dp - dd_h[:, 0:1])).astype(BF16)
                pb = p.astype(BF16)
                dqs.append(jnp.dot(dsc, kw, preferred_element_type=F32))
                dk_h = lax.dot_general(dsc, qh, (((0,), (0,)), ((), ())), preferred_element_type=F32)
                dv_h = lax.dot_general(pb, doh, (((0,), (0,)), ((), ())), preferred_element_type=F32)
                dkw[kvh] = dk_h if dkw[kvh] is None else dkw[kvh] + dk_h
                dvw[kvh] = dv_h if dvw[kvh] is None else dvw[kvh] + dv_h
                psink = jnp.exp(sink_ref[h] - lse_h)
                ds_ref[h] += jnp.sum((-psink * dd_h).reshape(WINDOW // 8, 8, HEAD_DIM), axis=0)
            dq_ref[...] = jnp.concatenate(dqs, axis=1)
            win = jnp.concatenate(dkw + dvw, axis=1)
            first = win[0:WINDOW]
            second = win[WINDOW:2 * WINDOW]
            dkv_ref[...] = carry[...] + first
            carry[...] = jnp.where(n == 0, first, second)

        @pl.when(n == nb)
        def _():
            dkv_ref[...] = carry[...]

    hm = lambda heads: pl.BlockSpec((heads, WINDOW, HEAD_DIM), lambda n: (0, jnp.minimum(n, last), 0))
    res = lambda heads: pl.BlockSpec((heads, s, HEAD_DIM), lambda n: (0, 0, 0))
    outs, comm_outs = _hosted_call(
        body, comm, name=name, grid=(nb + 1,),
        in_specs=[smem, smem, hm(A_Q_HEADS), res(A_KV_HEADS), res(A_KV_HEADS), hm(A_Q_HEADS), hm(A_Q_HEADS), hm(A_Q_HEADS)],
        out_specs=[pl.BlockSpec((WINDOW, A_WIDTH), lambda n: (jnp.minimum(n, last), 0)),
                   pl.BlockSpec((WINDOW, 2 * A_KV_WIDTH), lambda n: (jnp.maximum(n - 1, 0), 0)),
                   pl.BlockSpec((A_Q_HEADS, 8, HEAD_DIM), lambda n: (0, 0, 0))],
        out_shape=[jax.ShapeDtypeStruct((s, A_WIDTH), F32), jax.ShapeDtypeStruct((s, 2 * A_KV_WIDTH), F32),
                   jax.ShapeDtypeStruct((A_Q_HEADS, 8, HEAD_DIM), F32)],
        scratch_shapes=[pltpu.VMEM((WINDOW, 2 * A_KV_WIDTH), F32)],
        args=[sinks, slopes, q, k, v, do, lse, dd], sem=("arbitrary",), vmem=VMEM_BIG)
    return outs[0], outs[1], outs[2], comm_outs


def _attn_b_fwd(q, k, v, c3, name, comm=None):
    heads, s, _ = q.shape
    bq = min(512, s)
    nq = s // bq
    nt = (((1,), (1,)), ((), ()))

    def body(q_ref, k_ref, v_ref, c_ref, o_ref, lse_ref, m_scr, l_scr, acc_scr):
        i = pl.program_id(1)
        r0 = pl.multiple_of(i * bq, bq)
        row = lax.broadcasted_iota(jnp.int32, (bq, bq), 0)
        col = lax.broadcasted_iota(jnp.int32, (bq, bq), 1)
        m_scr[...] = jnp.full((2, bq, LANES), NEG, F32)
        l_scr[...] = jnp.zeros((2, bq, LANES), F32)
        acc_scr[...] = jnp.zeros((2, bq, HEAD_DIM), F32)

        def step(j, masked):
            k0 = pl.multiple_of(j * bq, bq)
            for h2 in range(2):
                kv = k_ref[h2, pl.ds(k0, bq), :]
                vv = v_ref[h2, pl.ds(k0, bq), :]
                cq0 = c_ref[h2, :, pl.ds(r0, LANES)][:, 0:1]
                sc = lax.dot_general(q_ref[h2], kv, nt, preferred_element_type=F32)
                sc = sc + (cq0 - c_ref[h2, :, pl.ds(k0, bq)])
                if masked:
                    sc = jnp.where(col <= row, sc, NEG)
                m_prev = m_scr[h2]
                m_new = jnp.maximum(m_prev, jnp.max(sc, axis=1, keepdims=True))
                alpha = jnp.exp(m_prev - m_new)
                p = jnp.exp(sc - m_new[:, 0:1])
                l_scr[h2] = alpha * l_scr[h2] + jnp.sum(p, axis=1, keepdims=True)
                p_hi = p.astype(BF16)
                p_lo = (p - p_hi.astype(F32)).astype(BF16)
                pv = jnp.dot(p_hi, vv, preferred_element_type=F32) + jnp.dot(p_lo, vv, preferred_element_type=F32)
                acc_scr[h2] = acc_scr[h2] * alpha[:, 0:HEAD_DIM] + pv
                m_scr[h2] = m_new

        def loop_body(j, carry):
            step(j, False)
            return carry

        lax.fori_loop(0, i, loop_body, 0)
        step(i, True)
        outs = []
        for h2 in range(2):
            l = l_scr[h2]
            outs.append(acc_scr[h2] / l[:, 0:HEAD_DIM])
            lse_ref[h2] = (m_scr[h2] + jnp.log(l))[:, 0:HEAD_DIM]
        o_ref[...] = jnp.concatenate(outs, axis=1)

    res = pl.BlockSpec((2, s, HEAD_DIM), lambda hp, i: (hp, 0, 0))
    outs, comm_outs = _hosted_call(
        body, comm, name=name, grid=(heads // 2, nq),
        in_specs=[pl.BlockSpec((2, bq, HEAD_DIM), lambda hp, i: (hp, i, 0)), res, res,
                  pl.BlockSpec((2, 1, s), lambda hp, i: (hp, 0, 0))],
        out_specs=[pl.BlockSpec((bq, 2 * HEAD_DIM), lambda hp, i: (i, hp)),
                   pl.BlockSpec((2, bq, HEAD_DIM), lambda hp, i: (hp, i, 0))],
        out_shape=[jax.ShapeDtypeStruct((s, heads * HEAD_DIM), F32), jax.ShapeDtypeStruct((heads, s, HEAD_DIM), F32)],
        scratch_shapes=[pltpu.VMEM((2, bq, LANES), F32), pltpu.VMEM((2, bq, LANES), F32), pltpu.VMEM((2, bq, HEAD_DIM), F32)],
        args=[q, k, v, c3], sem=("parallel", "parallel"), vmem=VMEM_BIG)
    return outs[0], outs[1], comm_outs


def _attn_b_bwd(q, k, v, do, lse, dd, c3, name, comm=None):
    heads, s, _ = q.shape
    bq = min(512, s)
    nq = s // bq
    nt = (((1,), (1,)), ((), ()))
    tn = (((0,), (0,)), ((), ()))
    grid = (heads // 2, nq)

    def body(q_ref, k_ref, v_ref, do_ref, lse_ref, dd_ref, c_ref, dq_ref, dk_ref, dv_ref, dc_ref,
             dq_scr, dk_scr, dv_scr, dc_scr):
        j = pl.program_id(1)
        k0 = pl.multiple_of(j * bq, bq)
        row = lax.broadcasted_iota(jnp.int32, (bq, bq), 0)
        col = lax.broadcasted_iota(jnp.int32, (bq, bq), 1)

        @pl.when(j == 0)
        def _():
            dq_scr[...] = jnp.zeros(dq_scr.shape, F32)

        dk_scr[...] = jnp.zeros((2, HEAD_DIM, bq), F32)
        dv_scr[...] = jnp.zeros((2, HEAD_DIM, bq), F32)
        dc_scr[...] = jnp.zeros((2, 1, bq), F32)
        k_t = [k_ref[h2].T for h2 in range(2)]

        def step(i, masked):
            r0 = pl.multiple_of(i * bq, bq)
            for h2 in range(2):
                kv = k_ref[h2]
                vv = v_ref[h2]
                qv = q_ref[h2, pl.ds(r0, bq), :]
                dov = do_ref[h2, pl.ds(r0, bq), :]
                lse_v = lse_ref[h2, pl.ds(r0, bq), :][:, 0:1]
                dd_v = dd_ref[h2, pl.ds(r0, bq), :][:, 0:1]
                cq0 = c_ref[h2, :, pl.ds(r0, LANES)][:, 0:1]
                sc = lax.dot_general(qv, kv, nt, preferred_element_type=F32) + (cq0 - c_ref[h2, :, pl.ds(k0, bq)])
                if masked:
                    sc = jnp.where(col <= row, sc, NEG)
                p = jnp.exp(sc - lse_v)
                dp = lax.dot_general(dov, vv, nt, preferred_element_type=F32)
                dsc = p * (dp - dd_v)
                dsb = dsc.astype(BF16)
                dv_scr[h2] += jnp.dot(dov.T, p.astype(BF16), preferred_element_type=F32)
                dk_scr[h2] += jnp.dot(qv.T, dsb, preferred_element_type=F32)
                dq_scr[h2, :, pl.ds(r0, bq)] += jnp.dot(k_t[h2], dsb.T, preferred_element_type=F32)
                dc_scr[h2] -= jnp.sum(dsc, axis=0, keepdims=True)

        def loop_body(i, carry):
            step(i, False)
            return carry

        step(j, True)
        lax.fori_loop(j + 1, nq, loop_body, 0)
        dc_ref[...] = dc_scr[...]
        dk_ref[...] = jnp.concatenate([dk_scr[0].T, dk_scr[1].T], axis=1)
        dv_ref[...] = jnp.concatenate([dv_scr[0].T, dv_scr[1].T], axis=1)

        @pl.when(j == nq - 1)
        def _():
            dq_ref[...] = jnp.concatenate([dq_scr[0].T, dq_scr[1].T], axis=1)

    res = pl.BlockSpec((2, s, HEAD_DIM), lambda hp, j: (hp, 0, 0))
    blk = pl.BlockSpec((2, bq, HEAD_DIM), lambda hp, j: (hp, j, 0))
    tm = jax.ShapeDtypeStruct((s, heads * HEAD_DIM), F32)
    in_specs = [res, blk, blk, res, res, res, pl.BlockSpec((2, 1, s), lambda hp, j: (hp, 0, 0))]
    out_specs = [pl.BlockSpec((s, 2 * HEAD_DIM), lambda hp, j: (0, hp)),
                 pl.BlockSpec((bq, 2 * HEAD_DIM), lambda hp, j: (j, hp)),
                 pl.BlockSpec((bq, 2 * HEAD_DIM), lambda hp, j: (j, hp)),
                 pl.BlockSpec((2, 1, bq), lambda hp, j: (hp, 0, j))]
    out_shape = [tm, tm, tm, jax.ShapeDtypeStruct((heads, 1, s), F32)]
    scratch = [pltpu.VMEM((2, HEAD_DIM, s), F32), pltpu.VMEM((2, HEAD_DIM, bq), F32),
               pltpu.VMEM((2, HEAD_DIM, bq), F32), pltpu.VMEM((2, 1, bq), F32)]
    outs, comm_outs = _hosted_call(
        body, comm, name=name, grid=grid, in_specs=in_specs, out_specs=out_specs, out_shape=out_shape,
        scratch_shapes=scratch, args=[q, k, v, do, lse, dd, c3], sem=("parallel", "arbitrary"), vmem=VMEM_BIG)
    return outs[0], outs[1], outs[2], outs[3], comm_outs


def _attn_c_probs(qh, mkh):
    sc = lax.dot_general(qh, mkh, (((1,), (1,)), ((), ())), preferred_element_type=F32) * (C_HEAD_DIM ** -0.5)
    p = jnp.exp(sc - jnp.max(sc, axis=1, keepdims=True))
    return p / jnp.sum(p, axis=1, keepdims=True)


def _attn_c_fwd(q, mkv, name):
    s = q.shape[0]
    m = mkv.shape[0]
    bq = _tile(s, 512, 8)

    def body(q_ref, mk_ref, mv_ref, o_ref):
        outs = []
        for h in range(C_HEADS):
            sl = slice(h * C_HEAD_DIM, (h + 1) * C_HEAD_DIM)
            pn = _attn_c_probs(q_ref[:, sl], mk_ref[:, sl]).astype(BF16)
            outs.append(jnp.dot(pn, mv_ref[:, sl], preferred_element_type=F32))
        o_ref[...] = jnp.concatenate(outs, axis=1)

    return pl.pallas_call(
        body, name=name, grid=(s // bq,),
        in_specs=[pl.BlockSpec((bq, C_WIDTH), lambda i: (i, 0)), pl.BlockSpec((m, C_WIDTH), lambda i: (0, 0)),
                  pl.BlockSpec((m, C_WIDTH), lambda i: (0, 1))],
        out_specs=pl.BlockSpec((bq, C_WIDTH), lambda i: (i, 0)),
        out_shape=jax.ShapeDtypeStruct((s, C_WIDTH), F32),
        compiler_params=_params(("parallel",)),
    )(q, mkv, mkv)


def _attn_c_bwd(q, mkv, do, name):
    s = q.shape[0]
    m = mkv.shape[0]
    bq = _tile(s, 512, 8)
    tn = (((0,), (0,)), ((), ()))

    def body(q_ref, mk_ref, mv_ref, do_ref, dq_ref, dm_ref):
        i = pl.program_id(0)

        @pl.when(i == 0)
        def _():
            dm_ref[...] = jnp.zeros(dm_ref.shape, F32)

        dqs = []
        for h in range(C_HEADS):
            sl = slice(h * C_HEAD_DIM, (h + 1) * C_HEAD_DIM)
            qh, mkh, mvh, doh = q_ref[:, sl], mk_ref[:, sl], mv_ref[:, sl], do_ref[:, sl]
            pn = _attn_c_probs(qh, mkh)
            dp = lax.dot_general(doh, mvh, (((1,), (1,)), ((), ())), preferred_element_type=F32)
            dsc = (pn * (dp - jnp.sum(pn * dp, axis=1, keepdims=True)) * (C_HEAD_DIM ** -0.5)).astype(BF16)
            dqs.append(jnp.dot(dsc, mkh, preferred_element_type=F32))
            dm_ref[:, sl] += lax.dot_general(dsc, qh, tn, preferred_element_type=F32)
            sv = slice(C_WIDTH + h * C_HEAD_DIM, C_WIDTH + (h + 1) * C_HEAD_DIM)
            dm_ref[:, sv] += lax.dot_general(pn.astype(BF16), doh, tn, preferred_element_type=F32)
        dq_ref[...] = jnp.concatenate(dqs, axis=1)

    row = pl.BlockSpec((bq, C_WIDTH), lambda i: (i, 0))
    return pl.pallas_call(
        body, name=name, grid=(s // bq,),
        in_specs=[row, pl.BlockSpec((m, C_WIDTH), lambda i: (0, 0)), pl.BlockSpec((m, C_WIDTH), lambda i: (0, 1)), row],
        out_specs=[row, pl.BlockSpec((m, 2 * C_WIDTH), lambda i: (0, 0))],
        out_shape=[jax.ShapeDtypeStruct((s, C_WIDTH), F32), jax.ShapeDtypeStruct((m, 2 * C_WIDTH), F32)],
        compiler_params=_params(("arbitrary",)),
    )(q, mkv, mkv, do)


def _gate_fwd(y, proj, zc0, bw, name):
    rows, width = y.shape
    bm = _tile(rows, 2048 if bw <= 256 else 1024, 16)
    cb0 = zc0 // bw

    def body(y_ref, z_ref, o_ref):
        z = z_ref[...].astype(F32)
        o_ref[...] = (y_ref[...] * (z * _sigmoid(z))).astype(BF16)

    return pl.pallas_call(
        body, name=name, grid=(rows // bm, width // bw),
        in_specs=[pl.BlockSpec((bm, bw), lambda i, t: (i, t)), pl.BlockSpec((bm, bw), lambda i, t: (i, cb0 + t))],
        out_specs=pl.BlockSpec((bm, bw), lambda i, t: (i, t)),
        out_shape=jax.ShapeDtypeStruct((rows, width), BF16),
        compiler_params=_params(("parallel", "parallel")),
    )(y, proj)


def _gate_bwd(dsv, y, proj, zc0, bw, dproj, t0, head_major, name):
    rows, width = y.shape
    bm = _tile(rows, 2048 if bw <= 256 else 1024, 16)
    cb0 = zc0 // bw
    tb0 = t0 // bw
    bd = _block_diag(HEAD_DIM)
    hpb = bw // HEAD_DIM

    def body(*refs):
        if head_major:
            ds_ref, y_ref, z_ref, bd_ref, _, dp_ref, dy_ref, dd_ref = refs
        else:
            ds_ref, y_ref, z_ref, _, dp_ref, dy_ref = refs
        z = z_ref[...].astype(F32)
        sig = _sigmoid(z)
        dsx = ds_ref[...]
        yv = y_ref[...]
        dy = dsx * (z * sig)
        dp_ref[...] = (dsx * yv * (sig * (1.0 + z * (1.0 - sig)))).astype(BF16)
        if head_major:
            dyb = dy.astype(BF16)
            dd = _seg_sum(dyb.astype(F32) * yv, bd_ref[...])
            for h in range(hpb):
                sl = slice(h * HEAD_DIM, (h + 1) * HEAD_DIM)
                dy_ref[h] = dyb[:, sl]
                dd_ref[h] = dd[:, sl]
        else:
            dy_ref[...] = dy.astype(BF16)

    tile = pl.BlockSpec((bm, bw), lambda i, t: (i, t))
    ztile = pl.BlockSpec((bm, bw), lambda i, t: (i, cb0 + t))
    ttile = pl.BlockSpec((bm, bw), lambda i, t: (i, tb0 + t))
    any_spec = pl.BlockSpec(memory_space=pl.ANY)
    dp_shape = jax.ShapeDtypeStruct(dproj.shape, BF16)
    if head_major:
        hm_spec = pl.BlockSpec((hpb, bm, HEAD_DIM), lambda i, t: (t, i, 0))
        nh = width // HEAD_DIM
        outs = pl.pallas_call(
            body, name=name, grid=(rows // bm, width // bw),
            in_specs=[tile, tile, ztile, pl.BlockSpec((LANES, LANES), lambda i, t: (0, 0)), any_spec],
            out_specs=[ttile, hm_spec, hm_spec],
            out_shape=[dp_shape, jax.ShapeDtypeStruct((nh, rows, HEAD_DIM), BF16),
                       jax.ShapeDtypeStruct((nh, rows, HEAD_DIM), F32)],
            input_output_aliases={4: 0},
            compiler_params=_params(("parallel", "parallel")),
        )(dsv, y, proj, bd, dproj)
        return outs[0], outs[1], outs[2]
    outs = pl.pallas_call(
        body, name=name, grid=(rows // bm, width // bw),
        in_specs=[tile, tile, ztile, any_spec],
        out_specs=[ttile, tile],
        out_shape=[dp_shape, jax.ShapeDtypeStruct((rows, width), BF16)],
        input_output_aliases={3: 0},
        compiler_params=_params(("parallel", "parallel")),
    )(dsv, y, proj, dproj)
    return outs[0], outs[1], None


def _merge_fwd(proj, ua, ub, uc, name):
    rows, d = ua.shape
    bm = _tile(rows, 1024, 16)
    bw = _tile(d, 512)
    g0 = COL_GATE // bw
    gstep = d // bw

    def body(la_ref, lb_ref, lc_ref, ua_ref, ub_ref, uc_ref, o_ref, ga_ref, gb_ref, gc_ref):
        y = None
        for l_ref, u_ref, g_ref in ((la_ref, ua_ref, ga_ref), (lb_ref, ub_ref, gb_ref), (lc_ref, uc_ref, gc_ref)):
            g = _sigmoid(l_ref[...].astype(F32))
            g_ref[...] = g.astype(BF16)
            term = g * u_ref[...].astype(F32)
            y = term if y is None else y + term
        o_ref[...] = y.astype(BF16)

    tile = pl.BlockSpec((bm, bw), lambda i, t: (i, t))
    gate = lambda b: pl.BlockSpec((bm, bw), lambda i, t: (i, g0 + b * gstep + t))
    shape = jax.ShapeDtypeStruct((rows, d), BF16)
    return pl.pallas_call(
        body, name=name, grid=(rows // bm, d // bw),
        in_specs=[gate(0), gate(1), gate(2), tile, tile, tile],
        out_specs=[tile] * 4, out_shape=[shape] * 4,
        compiler_params=_params(("parallel", "parallel")),
    )(proj, proj, proj, ua, ub, uc)


def _merge_bwd(dym, us, gs, name, comm=None):
    rows, d = dym.shape
    bm = _tile(rows, 1024, 16)
    bw = _tile(d, 512)
    nb = d // bw

    def body(dy_ref, ua_ref, ub_ref, uc_ref, ga_ref, gb_ref, gc_ref, dg_ref, da_ref, db_ref, dc_ref):
        b = pl.program_id(2)
        dyv = dy_ref[...]
        for idx, (u_ref, g_ref, du_ref) in enumerate(((ua_ref, ga_ref, da_ref), (ub_ref, gb_ref, db_ref), (uc_ref, gc_ref, dc_ref))):
            @pl.when(b == idx)
            def _():
                g = g_ref[...].astype(F32)
                du_ref[...] = (g * dyv).astype(BF16)
                dg_ref[...] = (dyv * u_ref[...].astype(F32) * g * (1.0 - g)).astype(BF16)

    tile = pl.BlockSpec((bm, bw), lambda i, t, b: (i, t))
    shape = jax.ShapeDtypeStruct((rows, d), BF16)
    outs, comm_outs = _hosted_call(
        body, comm, name=name, grid=(rows // bm, nb, 3),
        in_specs=[tile] * 7,
        out_specs=[pl.BlockSpec((bm, bw), lambda i, t, b: (i, b * nb + t)), tile, tile, tile],
        out_shape=[jax.ShapeDtypeStruct((rows, 3 * d), BF16), shape, shape, shape],
        scratch_shapes=[], args=[dym, *us, *gs], sem=("parallel", "parallel", "arbitrary"))
    return outs[0], outs[1], outs[2], outs[3], comm_outs


def _out_proj_loss(ym, wo, x, target, name):
    m, d = x.shape
    bm, bn = _tile(m, 1024, 16), _tile(d, 1024)
    grid = (m // bm, d // bn)

    def body(a_ref, b_ref, x_ref, t_ref, dy_ref, dyb_ref, l_ref):
        first, _ = _grid_edges(grid)
        y = jnp.dot(a_ref[...], b_ref[...], preferred_element_type=F32) + x_ref[...]
        diff = y - t_ref[...]
        dy = diff * (1.0 / d)
        dy_ref[...] = dy
        dyb_ref[...] = dy.astype(BF16)
        sq = diff * diff
        part = sq[:, 0:LANES]
        for c in range(1, bn // LANES):
            part = part + sq[:, c * LANES:(c + 1) * LANES]
        part = jnp.sum(part.reshape(bm // 8, 8, LANES), axis=0)

        @pl.when(first)
        def _():
            l_ref[...] = part

        @pl.when(jnp.logical_not(first))
        def _():
            l_ref[...] += part

    tile = pl.BlockSpec((bm, bn), lambda i, j: (i, j))
    return pl.pallas_call(
        body, name=name, grid=grid,
        in_specs=[pl.BlockSpec((bm, d), lambda i, j: (i, 0)), pl.BlockSpec((d, bn), lambda i, j: (0, j)), tile, tile],
        out_specs=[tile, tile, pl.BlockSpec((8, LANES), lambda i, j: (0, 0))],
        out_shape=[jax.ShapeDtypeStruct((m, d), F32), jax.ShapeDtypeStruct((m, d), BF16),
                   jax.ShapeDtypeStruct((8, LANES), F32)],
        compiler_params=_params(("arbitrary", "arbitrary"), VMEM_BIG),
    )(ym, wo, x, target)


def _row(vec, reps=1):
    return jnp.tile(vec.reshape(1, -1).astype(F32), (1, reps))


def _local_step(x, mem, target, small, wg, shards=None):
    s, d = x.shape
    dist = shards is not None
    wg = dict(wg)
    ones = lambda n: jnp.ones((1, n), F32)
    zeros = lambda n: jnp.zeros((1, n), F32)
    scale_ab = HEAD_DIM ** -0.5
    split8 = lambda g: g.reshape(N_DEV, g.shape[0] // N_DEV, g.shape[1])
    flat8 = lambda g: g.reshape(g.shape[0] * g.shape[1], g.shape[2])
    gather = lambda names: _Comm("gather", [shards[n] for n in names]) if dist else None
    g = {}

    def scatter(names):
        return _Comm("scatter", [split8(g[n]) for n in names]) if dist else None

    def hosted(result, names, store):
        if not dist:
            return result
        out, got = result
        store.update(zip(names, got))
        return out

    hn = _rmsnorm_fwd(x, small["norm_gain"], "rms_x_fwd")
    got = {}
    proj = hosted(_mm_nn(hn, wg["qkv"], bm=1024, bn=1024, bk=d, o_dtype=BF16, name="proj_qkv",
                         comm=gather(("wa", "wb", "wc"))), ("wa", "wb", "wc"), got)
    wg.update({n: flat8(a) for n, a in got.items()})
    pfb = _mm_nn(hn, wg["wf"], bm=1024, bn=FB_PAD, bk=d, o_dtype=F32, name="proj_fb")
    mn = _rmsnorm_fwd(mem, small["mem_norm_gain"], "rms_mem_fwd")
    mkv = _mm_nn(mn, wg["wk"], bm=256, bn=1024, bk=d, o_dtype=F32, name="mem_kv")

    gain_a = jnp.concatenate([_row(small["q_gain_a"], A_Q_HEADS) * scale_ab, _row(small["k_gain_a"], A_KV_HEADS), ones(A_KV_WIDTH)], axis=1)
    flag_a = jnp.concatenate([ones(A_WIDTH + A_KV_WIDTH), zeros(A_KV_WIDTH)], axis=1)
    qkv_a = _headnorm_fwd(proj, COL_QA, 1280, 1280, HEAD_DIM, gain_a, flag_a, True, "hn_a_fwd")
    gain_b = jnp.concatenate([_row(small["q_gain_b"], B_HEADS) * scale_ab, _row(small["k_gain_b"], B_HEADS), ones(B_WIDTH)], axis=1)
    flag_b = jnp.concatenate([ones(2 * B_WIDTH), zeros(B_WIDTH)], axis=1)
    qkv_b = _headnorm_fwd(proj, COL_QB, 2304, 256, HEAD_DIM, gain_b, flag_b, True, "hn_b_fwd")
    gain_cq = _row(small["q_gain_c"], C_HEADS)
    q_c = _headnorm_fwd(proj, COL_QC, C_WIDTH, C_WIDTH, C_HEAD_DIM, gain_cq, ones(C_WIDTH), False, "hn_cq_fwd")
    gain_ck = jnp.concatenate([_row(small["k_gain_c"], C_HEADS), ones(C_WIDTH)], axis=1)
    flag_ck = jnp.concatenate([ones(C_WIDTH), zeros(C_WIDTH)], axis=1)
    mkvn = _headnorm_fwd(mkv, 0, 2 * C_WIDTH, 2 * C_WIDTH, C_HEAD_DIM, gain_ck, flag_ck, False, "hn_ck_fwd")

    q_a, k_a, v_a = qkv_a[0:12], qkv_a[12:16], qkv_a[16:20]
    q_b, k_b, v_b = qkv_b[0:12], qkv_b[12:24], qkv_b[24:36]

    bpad = jnp.pad(small["b_forget"].reshape(1, -1), ((0, 0), (0, FB_PAD - B_HEADS)))
    c16 = _fox_prep(pfb, bpad, "fox_prep")
    c3 = c16[0:B_HEADS].reshape(B_HEADS, 1, s)

    sinks = small["sinks_a"].reshape(-1)
    slopes = jnp.exp2(-8.0 * jnp.arange(1, A_Q_HEADS + 1, dtype=F32) / A_Q_HEADS)
    y_a, lse_a = _attn_a_fwd(q_a, k_a, v_a, sinks, slopes, "attn_a_fwd")
    y_b, lse_b, got_zg = _attn_b_fwd(q_b, k_b, v_b, c3, "attn_b_fwd", comm=gather(("zg",)))
    if dist:
        wg["zg"] = flat8(got_zg[0])
    y_c = _attn_c_fwd(q_c, mkvn, "attn_c_fwd")

    got = {}
    pzg = hosted(_mm_nn(hn, wg["zg"], bm=1024, bn=1024, bk=d, o_dtype=BF16, name="proj_zg", comm=gather(("wo",))),
                 ("wo",), got)
    wg.update({n: flat8(a) for n, a in got.items()})

    s_a = _gate_fwd(y_a, pzg, COL_ZA, 256, "gate_a_fwd")
    s_b = _gate_fwd(y_b, pzg, COL_ZB, 256, "gate_b_fwd")
    s_c = _gate_fwd(y_c, pzg, COL_ZC, 512, "gate_c_fwd")
    w_a, w_b, w_c = _branch_full(wg["wa"]), _branch_full(wg["wb"]), _branch_full(wg["wc"])
    u_a = _mm_nn(s_a, w_a, bm=1024, bn=2048, bk=A_WIDTH, o_dtype=BF16, name="branch_a_fwd")
    u_b = _mm_nn(s_b, w_b, bm=1024, bn=2048, bk=B_WIDTH, o_dtype=BF16, name="branch_b_fwd")
    u_c = _mm_nn(s_c, w_c, bm=1024, bn=2048, bk=C_WIDTH, o_dtype=BF16, name="branch_c_fwd")
    ym, gate_a, gate_b, gate_c = _merge_fwd(pzg, u_a, u_b, u_c, "merge_fwd")
    dy, dyb, lpart = _out_proj_loss(ym, wg["wo"], x, target, "out_proj_loss")
    loss = 0.5 / d * jnp.sum(lpart)

    dym = _mm_nt(dyb, wg["wo"], bm=1024, bn=1024, bk=d, o_dtype=F32, name="out_proj_bwd_act")
    g["wo"] = _mm_tn(ym, dyb, bm=512, bn=1024, bk=s, o_dtype=BF16, name="out_proj_bwd_w")

    parts = {}
    names = ("wo",)
    dgate, du_a, du_b, du_c, got = _merge_bwd(dym, (u_a, u_b, u_c), (gate_a, gate_b, gate_c), "merge_bwd", comm=scatter(names))
    parts.update(zip(names, got))

    ds_a = _mm_nt(du_a, w_a, bm=1024, bn=A_WIDTH, bk=d, o_dtype=F32, name="branch_a_bwd_act")
    ds_b = _mm_nt(du_b, w_b, bm=1024, bn=B_WIDTH, bk=d, o_dtype=F32, name="branch_b_bwd_act")
    ds_c = _mm_nt(du_c, w_c, bm=1024, bn=C_WIDTH, bk=d, o_dtype=F32, name="branch_c_bwd_act")
    g["wa"] = _branch_shards(_mm_tn(s_a, du_a, bm=A_WIDTH, bn=1024, bk=s, o_dtype=BF16, name="branch_a_bwd_w"))
    g["wb"] = _branch_shards(_mm_tn(s_b, du_b, bm=B_WIDTH, bn=1024, bk=s, o_dtype=BF16, name="branch_b_bwd_w"))
    g["wc"] = _branch_shards(_mm_tn(s_c, du_c, bm=C_WIDTH, bn=1024, bk=s, o_dtype=BF16, name="branch_c_bwd_w"))
    names = ("wa", "wb", "wc")
    g["wm_g"] = hosted(_mm_tn(hn, dgate, bm=512, bn=1024, bk=s, o_dtype=BF16, name="proj_gate_bwd_w",
                              comm=scatter(names)), names, parts)

    dz = lax.empty((s, W_Z), BF16)
    dz, do_a, dd_a = _gate_bwd(ds_a, y_a, pzg, COL_ZA, 256, dz, COL_ZA, True, "gate_a_bwd")
    dz, do_b, dd_b = _gate_bwd(ds_b, y_b, pzg, COL_ZB, 256, dz, COL_ZB, True, "gate_b_bwd")
    dz, do_c, _ = _gate_bwd(ds_c, y_c, pzg, COL_ZC, 512, dz, COL_ZC, False, "gate_c_bwd")
    g["wm_z"] = _mm_tn(hn, dz, bm=512, bn=1024, bk=s, o_dtype=BF16, name="proj_z_bwd_w")

    names = ("wm_z",)
    dq_a, dkv_a, dsink, got = _attn_a_bwd(q_a, k_a, v_a, do_a, lse_a, dd_a, sinks, slopes, "attn_a_bwd", comm=scatter(names))
    parts.update(zip(names, got))
    names = ("wm_g",)
    dq_b, dk_b, dv_b, dc3, got = _attn_b_bwd(q_b, k_b, v_b, do_b, lse_b, dd_b, c3, "attn_b_bwd", comm=scatter(names))
    parts.update(zip(names, got))
    dq_c, dmkvn = _attn_c_bwd(q_c, mkvn, do_c, "attn_c_bwd")

    dqkv = lax.empty((s, W_QKV), BF16)
    dqkv, dg_qa = _headnorm_bwd(proj, COL_QA, A_WIDTH, 256, HEAD_DIM, gain_a[:, 0:768], flag_a[:, 0:768], dq_a, dqkv, COL_QA, "hn_qa_bwd")
    dqkv, dg_kva = _headnorm_bwd(proj, COL_KA, 512, 256, HEAD_DIM, gain_a[:, 768:1280], flag_a[:, 768:1280], dkv_a, dqkv, COL_KA, "hn_kva_bwd")
    dqkv, dg_qb = _headnorm_bwd(proj, COL_QB, B_WIDTH, 256, HEAD_DIM, gain_b[:, 0:768], flag_b[:, 0:768], dq_b, dqkv, COL_QB, "hn_qb_bwd")
    dqkv, dg_kb = _headnorm_bwd(proj, COL_KB, B_WIDTH, 256, HEAD_DIM, gain_b[:, 768:1536], flag_b[:, 768:1536], dk_b, dqkv, COL_KB, "hn_kb_bwd")
    dqkv, _ = _headnorm_bwd(proj, COL_VB, B_WIDTH, 256, HEAD_DIM, gain_b[:, 1536:2304], flag_b[:, 1536:2304], dv_b, dqkv, COL_VB, "hn_vb_bwd")
    dqkv, dg_qc = _headnorm_bwd(proj, COL_QC, C_WIDTH, 512, C_HEAD_DIM, gain_cq, ones(C_WIDTH), dq_c, dqkv, COL_QC, "hn_qc_bwd")
    dmkv, dg_kc = _headnorm_bwd(mkv, 0, 2 * C_WIDTH, 2 * C_WIDTH, C_HEAD_DIM, gain_ck, flag_ck, dmkvn, None, 0, "hn_kc_bwd")

    dct = jnp.pad(dc3.reshape(B_HEADS, s), ((0, 16 - B_HEADS), (0, 0)))
    dfb, dbf = _fox_prep_bwd(pfb, bpad, dct, "fox_prep_bwd")

    dmn = _mm_nt(dmkv, wg["wk"], bm=256, bn=1024, bk=1024, o_dtype=F32, name="mem_kv_bwd_act")
    g["wk"] = _mm_tn(mn, dmkv, bm=512, bn=1024, bk=mem.shape[0], o_dtype=BF16, name="mem_kv_bwd_w")
    _, dg_mem = _rmsnorm_bwd(mem, dmn, small["mem_norm_gain"], None, "rms_mem_bwd")

    g["wm_qkv"] = _mm_tn(hn, dqkv, bm=512, bn=1024, bk=s, o_dtype=BF16, name="proj_qkv_bwd_w")
    g["wf"] = _mm_tn(hn, dfb, bm=512, bn=FB_PAD, bk=s, o_dtype=BF16, name="proj_fb_bwd_w")
    half = W_QKV // 2
    g["wm_q1"], g["wm_q2"] = g["wm_qkv"][:, 0:half], g["wm_qkv"][:, half:W_QKV]
    names = ("wm_q1",)
    dhn = hosted(_mm_nt_sum([(dqkv, wg["qkv"], 0), (dfb, wg["wf"], 0)], bm=1024, bn=1024, bk=2048,
                            name="proj_qkv_bwd_act", comm=scatter(names)), names, parts)
    names = ("wm_q2", "wf", "wk")
    dhn = hosted(_mm_nt_sum([(dz, wg["zg"], COL_ZA), (dgate, wg["zg"], COL_GATE)], bm=1024, bn=1024, bk=2048,
                            name="proj_zg_bwd_act", add=dhn, comm=scatter(names)), names, parts)
    if dist:
        g = parts
    grad_x, dg_x = _rmsnorm_bwd(x, dhn, small["norm_gain"], dy, "rms_x_bwd")

    fold = lambda part, heads, hd: jnp.sum(jnp.sum(part, axis=0).reshape(heads, hd), axis=0).reshape(1, hd)
    small_grads = {
        "norm_gain": jnp.sum(dg_x, axis=0).reshape(1, d),
        "mem_norm_gain": jnp.sum(dg_mem, axis=0).reshape(1, d),
        "b_forget": dbf[0:B_HEADS, 0].reshape(1, B_HEADS),
        "q_gain_a": fold(dg_qa, A_Q_HEADS, HEAD_DIM) * scale_ab,
        "k_gain_a": fold(dg_kva[:, 0:A_KV_WIDTH], A_KV_HEADS, HEAD_DIM),
        "sinks_a": (jnp.sum(dsink, axis=(1, 2)) * (1.0 / HEAD_DIM)).reshape(1, A_Q_HEADS),
        "q_gain_b": fold(dg_qb, B_HEADS, HEAD_DIM) * scale_ab,
        "k_gain_b": fold(dg_kb, B_HEADS, HEAD_DIM),
        "q_gain_c": fold(dg_qc, C_HEADS, C_HEAD_DIM),
        "k_gain_c": fold(dg_kc[:, 0:C_WIDTH], C_HEADS, C_HEAD_DIM),
    }
    return loss, grad_x, small_grads, g


def _coords():
    return lax.axis_index("x"), lax.axis_index("y"), lax.axis_index("c")


def _all_gather(shards, name):
    n = len(shards)

    def body(*refs):
        ins = refs[0:n]
        outs = refs[n:2 * n]
        send_sems, recv_sems, local_sems = refs[2 * n:2 * n + 3]
        x, y, c = _coords()
        me, sibling = (x, y, c), (x, y, 1 - c)
        chips = [(1 - x, y), (x, 1 - y), (1 - x, 1 - y)]
        idx = lambda p: 4 * p[0] + 2 * p[1] + p[2]

        def copy(a, k, block, to, src=None):
            slot = outs[a].at[idx(block)]
            return pltpu.make_async_remote_copy(
                src_ref=slot if src is None else src, dst_ref=slot,
                send_sem=send_sems.at[a, k], recv_sem=recv_sems.at[a, k], device_id=to, device_id_type=MESH)

        mine = [pltpu.make_async_copy(ins[a], outs[a].at[idx(me)], local_sems.at[a]) for a in range(n)]
        for cp in mine:
            cp.start()
        first = []
        for a in range(n):
            first.append(copy(a, 0, me, sibling, src=ins[a]))
            first += [copy(a, 1 + j, me, (*chip, c), src=ins[a]) for j, chip in enumerate(chips)]
        for cp in first:
            cp.start()
        passed = []
        for j, chip in enumerate(chips):
            for a in range(n):
                copy(a, 1 + j, (*chip, c), me).wait_recv()
                fwd = copy(a, 4 + j, (*chip, c), sibling)
                fwd.start()
                passed.append(fwd)
        for a in range(n):
            copy(a, 0, sibling, me).wait_recv()
            for j, chip in enumerate(chips):
                copy(a, 4 + j, (*chip, 1 - c), me).wait_recv()
        for cp in first + passed:
            cp.wait_send()
        for cp in mine:
            cp.wait()

    any_spec = pl.BlockSpec(memory_space=pl.ANY)
    return pl.pallas_call(
        body, name=name,
        in_specs=[any_spec] * n, out_specs=[any_spec] * n,
        out_shape=[jax.ShapeDtypeStruct((N_DEV,) + sh.shape, sh.dtype) for sh in shards],
        scratch_shapes=[pltpu.SemaphoreType.DMA((n, 7)), pltpu.SemaphoreType.DMA((n, 7)), pltpu.SemaphoreType.DMA((n,))],
    )(*shards)


def _all_reduce_small(vec, name):
    p = vec.shape[1]

    def body(v_ref, o_ref, gather, send_sems, recv_sems):
        x, y, c = _coords()
        my = 4 * x + 2 * y + c
        peers = [(x ^ ((k >> 2) & 1), y ^ ((k >> 1) & 1), c ^ (k & 1)) for k in range(1, N_DEV)]
        gather[my] = v_ref[...]
        sends = [pltpu.make_async_remote_copy(
            src_ref=v_ref, dst_ref=gather.at[my], send_sem=send_sems.at[k], recv_sem=recv_sems.at[k],
            device_id=peer, device_id_type=MESH) for k, peer in enumerate(peers)]
        for cp in sends:
            cp.start()
        for k, peer in enumerate(peers):
            pid = 4 * peer[0] + 2 * peer[1] + peer[2]
            pltpu.make_async_remote_copy(
                src_ref=v_ref, dst_ref=gather.at[pid], send_sem=send_sems.at[k], recv_sem=recv_sems.at[k],
                device_id=peer, device_id_type=MESH).wait_recv()
        for cp in sends:
            cp.wait_send()
        total = gather[0]
        for j in range(1, N_DEV):
            total = total + gather[j]
        o_ref[...] = total

    vm = pl.BlockSpec(memory_space=pltpu.VMEM)
    return pl.pallas_call(
        body, name=name, in_specs=[vm], out_specs=vm,
        out_shape=jax.ShapeDtypeStruct((8, p), F32),
        scratch_shapes=[pltpu.VMEM((N_DEV, 8, p), F32), pltpu.SemaphoreType.DMA((7,)), pltpu.SemaphoreType.DMA((7,))],
    )(vec)[0:1]


def _sum_parts(parts, name):
    _, rows, cols = parts.shape
    br = _tile(rows, 64, 16)

    def body(p_ref, o_ref):
        total = p_ref[0].astype(F32)
        for j in range(1, N_DEV):
            total = total + p_ref[j].astype(F32)
        o_ref[...] = total

    return pl.pallas_call(
        body, name=name, grid=(rows // br,),
        in_specs=[pl.BlockSpec((N_DEV, br, cols), lambda i: (0, i, 0))],
        out_specs=pl.BlockSpec((br, cols), lambda i: (i, 0)),
        out_shape=jax.ShapeDtypeStruct((rows, cols), F32),
        compiler_params=_params(("parallel",), VMEM_BIG),
    )(parts)


def _adamw(w, g, m, v, name, br=32):
    rows, cols = w.shape
    br = min(br, rows)
    c1 = 1.0 / (1.0 - ADAM_B1 ** ADAM_STEP)
    c2 = 1.0 / (1.0 - ADAM_B2 ** ADAM_STEP)

    def body(w_ref, g_ref, m_ref, v_ref, d_ref, nm_ref, nv_ref):
        gv = g_ref[...]
        nm = ADAM_B1 * m_ref[...] + (1.0 - ADAM_B1) * gv
        nv = ADAM_B2 * v_ref[...] + (1.0 - ADAM_B2) * (gv * gv)
        d_ref[...] = -ADAM_LR * ((nm * c1) / (jnp.sqrt(nv * c2) + ADAM_EPS) + ADAM_WD * w_ref[...])
        nm_ref[...] = nm
        nv_ref[...] = nv

    spec = pl.BlockSpec((br, cols), lambda i: (i, 0))
    shape = jax.ShapeDtypeStruct((rows, cols), F32)
    return pl.pallas_call(
        body, name=name, grid=(pl.cdiv(rows, br),), in_specs=[spec] * 4, out_specs=[spec] * 3, out_shape=[shape] * 3,
        compiler_params=_params(("parallel",), VMEM_BIG),
    )(w, g, m, v)


def _adamw_t(wt, g, mt, vt, name, br=1024):
    n, r = wt.shape
    c1 = 1.0 / (1.0 - ADAM_B1 ** ADAM_STEP)
    c2 = 1.0 / (1.0 - ADAM_B2 ** ADAM_STEP)

    def body(w_ref, g_ref, m_ref, v_ref, d_ref, nm_ref, nv_ref):
        gv = g_ref[...].T
        nm = ADAM_B1 * m_ref[...] + (1.0 - ADAM_B1) * gv
        nv = ADAM_B2 * v_ref[...] + (1.0 - ADAM_B2) * (gv * gv)
        d_ref[...] = -ADAM_LR * ((nm * c1) / (jnp.sqrt(nv * c2) + ADAM_EPS) + ADAM_WD * w_ref[...])
        nm_ref[...] = nm
        nv_ref[...] = nv

    spec = pl.BlockSpec((br, r), lambda i: (i, 0))
    shape = jax.ShapeDtypeStruct((n, r), F32)
    return pl.pallas_call(
        body, name=name, grid=(pl.cdiv(n, br),),
        in_specs=[spec, pl.BlockSpec((r, br), lambda i: (0, i)), spec, spec], out_specs=[spec] * 3, out_shape=[shape] * 3,
        compiler_params=_params(("parallel",), VMEM_BIG),
    )(wt, g, mt, vt)


def _adamw_parts(w, parts, m, v, name):
    rows, cols = w.shape
    br = _tile(rows, 32, 16)
    c1 = 1.0 / (1.0 - ADAM_B1 ** ADAM_STEP)
    c2 = 1.0 / (1.0 - ADAM_B2 ** ADAM_STEP)

    def body(w_ref, p_ref, m_ref, v_ref, g_ref, d_ref, nm_ref, nv_ref):
        gv = p_ref[0].astype(F32)
        for j in range(1, N_DEV):
            gv = gv + p_ref[j].astype(F32)
        nm = ADAM_B1 * m_ref[...] + (1.0 - ADAM_B1) * gv
        nv = ADAM_B2 * v_ref[...] + (1.0 - ADAM_B2) * (gv * gv)
        g_ref[...] = gv
        d_ref[...] = -ADAM_LR * ((nm * c1) / (jnp.sqrt(nv * c2) + ADAM_EPS) + ADAM_WD * w_ref[...])
        nm_ref[...] = nm
        nv_ref[...] = nv

    spec = pl.BlockSpec((br, cols), lambda i: (i, 0))
    shape = jax.ShapeDtypeStruct((rows, cols), F32)
    return pl.pallas_call(
        body, name=name, grid=(rows // br,),
        in_specs=[spec, pl.BlockSpec((N_DEV, br, cols), lambda i: (0, i, 0)), spec, spec],
        out_specs=[spec] * 4, out_shape=[shape] * 4,
        compiler_params=_params(("parallel",), VMEM_BIG),
    )(w, parts, m, v)


SMALL_NAMES = ("norm_gain", "mem_norm_gain", "b_forget", "q_gain_a", "k_gain_a", "sinks_a",
               "q_gain_b", "k_gain_b", "q_gain_c", "k_gain_c")
BIG_NAMES = ("w_in", "w_mem_kv", "w_branch_a", "w_branch_b", "w_branch_c", "w_out")
WEIGHT_ORDER = ("norm_gain", "mem_norm_gain", "w_in", "b_forget", "q_gain_a", "k_gain_a", "sinks_a", "q_gain_b",
                "k_gain_b", "q_gain_c", "k_gain_c", "w_mem_kv", "w_branch_a", "w_branch_b", "w_branch_c", "w_out")


def _pack_small(tree):
    flat = jnp.concatenate([tree[n].reshape(1, -1) for n in SMALL_NAMES], axis=1)
    pad = (-flat.shape[1]) % LANES
    return jnp.pad(flat, ((0, 0), (0, pad)))


def _unpack_small(flat, like):
    out, off = {}, 0
    for n in SMALL_NAMES:
        size = like[n].size
        out[n] = flat[:, off:off + size].reshape(like[n].shape)
        off += size
    return out


def kernel(x, mem, norm_gain, mem_norm_gain, w_in, b_forget, q_gain_a, k_gain_a, sinks_a, q_gain_b, k_gain_b, q_gain_c, k_gain_c, w_mem_kv, w_branch_a, w_branch_b, w_branch_c, w_out, loss_target, m_norm_gain, m_mem_norm_gain, m_w_in, m_b_forget, m_q_gain_a, m_k_gain_a, m_sinks_a, m_q_gain_b, m_k_gain_b, m_q_gain_c, m_k_gain_c, m_w_mem_kv, m_w_branch_a, m_w_branch_b, m_w_branch_c, m_w_out, v_norm_gain, v_mem_norm_gain, v_w_in, v_b_forget, v_q_gain_a, v_k_gain_a, v_sinks_a, v_q_gain_b, v_k_gain_b, v_q_gain_c, v_k_gain_c, v_w_mem_kv, v_w_branch_a, v_w_branch_b, v_w_branch_c, v_w_out):
    weights = dict(norm_gain=norm_gain, mem_norm_gain=mem_norm_gain, w_in=w_in, b_forget=b_forget, q_gain_a=q_gain_a,
                   k_gain_a=k_gain_a, sinks_a=sinks_a, q_gain_b=q_gain_b, k_gain_b=k_gain_b, q_gain_c=q_gain_c,
                   k_gain_c=k_gain_c, w_mem_kv=w_mem_kv, w_branch_a=w_branch_a, w_branch_b=w_branch_b,
                   w_branch_c=w_branch_c, w_out=w_out)
    mom_m = dict(norm_gain=m_norm_gain, mem_norm_gain=m_mem_norm_gain, w_in=m_w_in, b_forget=m_b_forget,
                 q_gain_a=m_q_gain_a, k_gain_a=m_k_gain_a, sinks_a=m_sinks_a, q_gain_b=m_q_gain_b, k_gain_b=m_k_gain_b,
                 q_gain_c=m_q_gain_c, k_gain_c=m_k_gain_c, w_mem_kv=m_w_mem_kv, w_branch_a=m_w_branch_a,
                 w_branch_b=m_w_branch_b, w_branch_c=m_w_branch_c, w_out=m_w_out)
    mom_v = dict(norm_gain=v_norm_gain, mem_norm_gain=v_mem_norm_gain, w_in=v_w_in, b_forget=v_b_forget,
                 q_gain_a=v_q_gain_a, k_gain_a=v_k_gain_a, sinks_a=v_sinks_a, q_gain_b=v_q_gain_b, k_gain_b=v_k_gain_b,
                 q_gain_c=v_q_gain_c, k_gain_c=v_k_gain_c, w_mem_kv=v_w_mem_kv, w_branch_a=v_w_branch_a,
                 w_branch_b=v_w_branch_b, w_branch_c=v_w_branch_c, w_out=v_w_out)
    wi = w_in[0]
    sh_qkv = jnp.concatenate([wi[:, a:b] for a, b in SRC_RANGES[0:3]], axis=1).astype(BF16)
    sh_zg = jnp.concatenate([wi[:, a:b] for a, b in SRC_RANGES[3:6]] + [wi[:, SRC_GATE:]], axis=1).astype(BF16)
    sh_wf = jnp.pad(wi[:, FB_SRC:FB_SRC + B_HEADS], ((0, 0), (0, FB_PAD - B_HEADS))).astype(BF16)
    shards = {"zg": sh_zg, "wo": w_out[0].astype(BF16), "wa": w_branch_a[0].astype(BF16),
              "wb": w_branch_b[0].astype(BF16), "wc": w_branch_c[0].astype(BF16)}
    first = ("qkv", "wf", "wk")
    full = _all_gather([sh_qkv, sh_wf, w_mem_kv[0].astype(BF16)], "weights_all_gather")
    wg = {kname: arr.reshape(arr.shape[0] * arr.shape[1], arr.shape[2]) for kname, arr in zip(first, full)}

    small = {n: weights[n] for n in SMALL_NAMES}
    loss_local, grad_x, small_g, parts = _local_step(x[0], mem[0], loss_target[0], small, wg, shards)

    grads, delta, new_m, new_v = {}, {}, {}, {}
    for n, kname in (("w_mem_kv", "wk"), ("w_out", "wo"), ("w_branch_a", "wa"), ("w_branch_b", "wb"), ("w_branch_c", "wc")):
        gsum, dlt, nm, nv = _adamw_parts(weights[n][0], parts[kname], mom_m[n][0], mom_v[n][0], "adamw_" + n)
        grads[n], delta[n], new_m[n], new_v[n] = gsum, dlt[None], nm[None], nv[None]
    g1, g2, gz, gf, gg = (_sum_parts(parts[k], "grad_sum_" + k) for k in ("wm_q1", "wm_q2", "wm_z", "wf", "wm_g"))
    half = W_QKV // 2
    g_in = jnp.concatenate([g1[:, COL_QA:COL_QB], gz[:, COL_ZA:COL_ZB], g1[:, COL_QB:half], g2[:, 0:COL_QC - half],
                            gz[:, COL_ZB:COL_ZC], gf[:, 0:B_HEADS], g2[:, COL_QC - half:half], gz[:, COL_ZC:W_Z], gg], axis=1)
    dlt, nm, nv = _adamw_t(w_in[0].T, g_in, m_w_in[0].T, v_w_in[0].T, "adamw_w_in")
    grads["w_in"], delta["w_in"], new_m["w_in"], new_v["w_in"] = g_in, dlt.T[None], nm.T[None], nv.T[None]

    packed = _pack_small(small_g)
    reduced = _all_reduce_small(jnp.broadcast_to(packed, (8, packed.shape[1])), "small_all_reduce")
    grads.update(_unpack_small(reduced, small))

    loss = lax.psum(loss_local, ("x", "y", "c"))

    pw, pm, pv = _pack_small(small), _pack_small({n: mom_m[n] for n in SMALL_NAMES}), _pack_small({n: mom_v[n] for n in SMALL_NAMES})
    rep8 = lambda a: jnp.broadcast_to(a, (8, a.shape[1]))
    dlt, nm, nv = _adamw(rep8(pw), rep8(reduced), rep8(pm), rep8(pv), "adamw_small")
    for tree, flat in ((delta, dlt), (new_m, nm), (new_v, nv)):
        tree.update(_unpack_small(flat[0:1], small))
    for n in BIG_NAMES:
        grads[n] = grads[n][None]
    return (loss, grad_x[None], *[grads[n] for n in WEIGHT_ORDER], *[delta[n] for n in WEIGHT_ORDER],
            *[new_m[n] for n in WEIGHT_ORDER], *[new_v[n] for n in WEIGHT_ORDER])
```

```python
import math

import jax
import jax.numpy as jnp
import numpy as np
from jax import lax
from jax.experimental import pallas as pl
from jax.experimental.pallas import tpu as pltpu

F32 = jnp.float32
BF16 = jnp.bfloat16

N_DEV = 8
HEAD_DIM = 64
A_Q_HEADS = 12
A_KV_HEADS = 4
A_GROUP = 3
B_HEADS = 12
C_HEADS = 4
C_HEAD_DIM = 128
WINDOW = 128
A_WIDTH = 768
A_KV_WIDTH = 256
B_WIDTH = 768
C_WIDTH = 512
EPS = 1e-6
NEG = -1e30

COL_QA, COL_KA, COL_VA = 0, 768, 1024
COL_QB, COL_KB, COL_VB = 1280, 2048, 2816
COL_QC = 3584
W_QKV = 4096
Q_SPLIT = 1536
COL_ZA, COL_ZB, COL_ZC = 0, 768, 1536
COL_GATE = W_Z = 2048
SRC_RANGES = ((0, 1280), (2048, 4352), (5132, 5644), (1280, 2048), (4352, 5120), (5644, 6156))
SRC_GATE = 6156
FB_SRC = 5120
FB_PAD = 128

ADAM_LR = 0.001
ADAM_B1 = 0.9
ADAM_B2 = 0.999
ADAM_EPS = 1e-08
ADAM_WD = 0.01
ADAM_STEP = 10

VMEM_BIG = 52 * 1024 * 1024
LANES = 128
MESH = pl.DeviceIdType.MESH


def _tile(n, pref, mult=128):
    if n <= pref:
        return n
    t = (pref // mult) * mult
    while t >= mult:
        if n % t == 0:
            return t
        t -= mult
    return n


def _params(sem=None, vmem=None):
    kw = {}
    if sem is not None:
        kw["dimension_semantics"] = sem
    if vmem is not None:
        kw["vmem_limit_bytes"] = vmem
    return pltpu.CompilerParams(**kw)


def _sigmoid(x):
    return 1.0 / (1.0 + jnp.exp(-x))


def _block_diag(hd):
    r = np.arange(LANES)
    return jnp.asarray((r[:, None] // hd) == (r[None, :] // hd), dtype=BF16)


def _seg_sum(t, bd):
    hi = t.astype(BF16)
    lo = (t - hi.astype(F32)).astype(BF16)
    outs = []
    for c in range(t.shape[1] // LANES):
        sl = slice(c * LANES, (c + 1) * LANES)
        outs.append(jnp.dot(hi[:, sl], bd, preferred_element_type=F32) + jnp.dot(lo[:, sl], bd, preferred_element_type=F32))
    return outs[0] if len(outs) == 1 else jnp.concatenate(outs, axis=1)


def _rmsnorm_fwd(x, gain, name):
    rows, d = x.shape
    bm = _tile(rows, 512, 8)

    def body(x_ref, g_ref, o_ref):
        xv = x_ref[...]
        ms = jnp.mean(xv * xv, axis=-1, keepdims=True)
        o_ref[...] = (xv * lax.rsqrt(ms + EPS) * g_ref[...]).astype(BF16)

    return pl.pallas_call(
        body, name=name, grid=(rows // bm,),
        in_specs=[pl.BlockSpec((bm, d), lambda i: (i, 0)), pl.BlockSpec((1, d), lambda i: (0, 0))],
        out_specs=pl.BlockSpec((bm, d), lambda i: (i, 0)),
        out_shape=jax.ShapeDtypeStruct((rows, d), BF16),
        compiler_params=_params(("parallel",)),
    )(x, gain)


def _rmsnorm_bwd(x, dhn, gain, dy, name):
    rows, d = x.shape
    bm = _tile(rows, 512, 8)
    with_dx = dy is not None

    def body(*refs):
        if with_dx:
            x_ref, dh_ref, g_ref, dy_ref, gx_ref, dg_ref = refs
        else:
            x_ref, dh_ref, g_ref, dg_ref = refs
        i = pl.program_id(0)
        xv = x_ref[...]
        rstd = lax.rsqrt(jnp.mean(xv * xv, axis=-1, keepdims=True) + EPS)
        xhat = xv * rstd
        dh = dh_ref[...]
        part = jnp.sum((dh * xhat).reshape(bm // 8, 8, d), axis=0)

        @pl.when(i == 0)
        def _():
            dg_ref[...] = part

        @pl.when(i > 0)
        def _():
            dg_ref[...] += part

        if with_dx:
            g = dh * g_ref[...]
            mean = jnp.mean(g * xhat, axis=-1, keepdims=True)
            gx_ref[...] = dy_ref[...] + rstd * (g - xhat * mean)

    row_spec = pl.BlockSpec((bm, d), lambda i: (i, 0))
    in_specs = [row_spec, row_spec, pl.BlockSpec((1, d), lambda i: (0, 0))]
    args = [x, dhn, gain]
    dg_spec = pl.BlockSpec((8, d), lambda i: (0, 0))
    dg_shape = jax.ShapeDtypeStruct((8, d), F32)
    if with_dx:
        in_specs.append(row_spec)
        args.append(dy)
        out_specs = [row_spec, dg_spec]
        out_shape = [jax.ShapeDtypeStruct((rows, d), F32), dg_shape]
    else:
        out_specs = [dg_spec]
        out_shape = [dg_shape]
    outs = pl.pallas_call(
        body, name=name, grid=(rows // bm,), in_specs=in_specs, out_specs=out_specs, out_shape=out_shape,
        compiler_params=_params(("arbitrary",), VMEM_BIG),
    )(*args)
    return outs if with_dx else (None, outs[0])


class _Comm:
    def __init__(self, kind, arrays):
        self.kind = kind
        self.arrays = list(arrays)
        self.n = len(self.arrays)

    def out_shapes(self):
        if self.kind == "gather":
            return [jax.ShapeDtypeStruct((N_DEV,) + a.shape, a.dtype) for a in self.arrays]
        return [jax.ShapeDtypeStruct(a.shape, a.dtype) for a in self.arrays]

    def scratch(self):
        return [pltpu.SemaphoreType.DMA((self.n, N_DEV - 1)), pltpu.SemaphoreType.DMA((self.n, N_DEV - 1)),
                pltpu.SemaphoreType.DMA((self.n,))]

    def _plan(self, ins, outs, sems, with_recvs):
        send_sems, recv_sems, local_sems = sems
        x, y, c = lax.axis_index("x"), lax.axis_index("y"), lax.axis_index("c")
        my = 4 * x + 2 * y + c
        gather = self.kind == "gather"
        local, sends, recvs = [], [], []
        for a in range(self.n):
            local.append(pltpu.make_async_copy(ins[a] if gather else ins[a].at[my], outs[a].at[my], local_sems.at[a]))
            for k in range(1, N_DEV):
                peer = (x ^ ((k >> 2) & 1), y ^ ((k >> 1) & 1), c ^ (k & 1))
                pid = 4 * peer[0] + 2 * peer[1] + peer[2]
                src = ins[a] if gather else ins[a].at[pid]
                sem = dict(send_sem=send_sems.at[a, k - 1], recv_sem=recv_sems.at[a, k - 1], device_id=peer, device_id_type=MESH)
                sends.append(pltpu.make_async_remote_copy(src_ref=src, dst_ref=outs[a].at[my], **sem))
                if with_recvs:
                    recvs.append(pltpu.make_async_remote_copy(src_ref=src, dst_ref=outs[a].at[pid], **sem))
        return local, sends, recvs

    def start(self, ins, outs, sems):
        local, sends, _ = self._plan(ins, outs, sems, False)
        for cp in local + sends:
            cp.start()

    def wait(self, ins, outs, sems):
        local, sends, recvs = self._plan(ins, outs, sems, True)
        for cp in recvs:
            cp.wait_recv()
        for cp in sends:
            cp.wait_send()
        for cp in local:
            cp.wait()


def _grid_edges(grid):
    first = last = None
    for ax, size in enumerate(grid):
        pid = pl.program_id(ax)
        f, l = pid == 0, pid == size - 1
        first = f if first is None else first & f
        last = l if last is None else last & l
    return first, last


def _hosted_call(body, comm, *, name, grid, in_specs, out_specs, out_shape, scratch_shapes, args, sem, vmem=None):
    in_specs, out_specs, out_shape, scratch_shapes = list(in_specs), list(out_specs), list(out_shape), list(scratch_shapes)
    if comm is None:
        res = pl.pallas_call(body, name=name, grid=grid, in_specs=in_specs, out_specs=out_specs, out_shape=out_shape,
                             scratch_shapes=scratch_shapes, compiler_params=_params(sem, vmem))(*args)
        return list(res), []
    n_in, n_out, n_scr, nc = len(in_specs), len(out_shape), len(scratch_shapes), comm.n

    def hosted(*refs):
        ins = refs[0:n_in]
        comm_in = refs[n_in:n_in + nc]
        outs = refs[n_in + nc:n_in + nc + n_out]
        comm_out = refs[n_in + nc + n_out:n_in + 2 * nc + n_out]
        scr = refs[n_in + 2 * nc + n_out:n_in + 2 * nc + n_out + n_scr]
        sems = refs[n_in + 2 * nc + n_out + n_scr:]
        first, last = _grid_edges(grid)

        @pl.when(first)
        def _():
            comm.start(comm_in, comm_out, sems)

        body(*ins, *outs, *scr)

        @pl.when(last)
        def _():
            comm.wait(comm_in, comm_out, sems)

    any_spec = pl.BlockSpec(memory_space=pl.ANY)
    res = pl.pallas_call(
        hosted, name=name, grid=grid, in_specs=in_specs + [any_spec] * nc, out_specs=out_specs + [any_spec] * nc,
        out_shape=out_shape + comm.out_shapes(), scratch_shapes=scratch_shapes + comm.scratch(),
        compiler_params=_params(("arbitrary",) * len(grid), vmem),
    )(*args, *comm.arrays)
    return list(res[0:n_out]), list(res[n_out:])


def _mm(a, b, *, grid, a_spec, b_spec, o_spec, o_shape, o_dtype, contract, name, add=None, add_spec=None, acc_shape=None,
        comm=None):
    nk = grid[2]
    has_add = add is not None

    def body(*refs):
        a_ref, b_ref = refs[0], refs[1]
        add_ref = refs[2] if has_add else None
        o_ref = refs[3] if has_add else refs[2]
        part = lax.dot_general(a_ref[...], b_ref[...], (contract, ((), ())), preferred_element_type=F32)
        if nk == 1:
            if has_add:
                part = part + add_ref[...]
            o_ref[...] = part.astype(o_dtype)
        else:
            acc = refs[-1]
            k = pl.program_id(2)

            @pl.when(k == 0)
            def _():
                acc[...] = part

            @pl.when(k > 0)
            def _():
                acc[...] += part

            @pl.when(k == nk - 1)
            def _():
                r = acc[...]
                if has_add:
                    r = r + add_ref[...]
                o_ref[...] = r.astype(o_dtype)

    in_specs = [a_spec, b_spec] + ([add_spec] if has_add else [])
    args = [a, b] + ([add] if has_add else [])
    scratch = [pltpu.VMEM(acc_shape, F32)] if nk > 1 else []
    outs, comm_outs = _hosted_call(
        body, comm, name=name, grid=grid, in_specs=in_specs, out_specs=[o_spec],
        out_shape=[jax.ShapeDtypeStruct(o_shape, o_dtype)], scratch_shapes=scratch, args=args,
        sem=("parallel", "parallel", "arbitrary"), vmem=VMEM_BIG)
    return outs[0] if comm is None else (outs[0], comm_outs)


def _mm_nn(a, b, *, bm, bn, bk, o_dtype, name, add=None, comm=None):
    m, kd = a.shape
    n = b.shape[1]
    bm, bn, bk = _tile(m, bm, 8), _tile(n, bn), _tile(kd, bk)
    o_spec = pl.BlockSpec((bm, bn), lambda i, j, k: (i, j))
    return _mm(a, b, grid=(m // bm, n // bn, kd // bk),
               a_spec=pl.BlockSpec((bm, bk), lambda i, j, k: (i, k)),
               b_spec=pl.BlockSpec((bk, bn), lambda i, j, k: (k, j)),
               o_spec=o_spec, o_shape=(m, n), o_dtype=o_dtype, contract=((1,), (0,)), name=name,
               add=add, add_spec=o_spec, acc_shape=(bm, bn), comm=comm)


def _mm_nt(a, b, *, bm, bn, bk, o_dtype, name, add=None, b_col0=0, comm=None):
    m, kd = a.shape
    n = b.shape[0]
    bm, bn, bk = _tile(m, bm, 8), _tile(n, bn), _tile(math.gcd(kd, b_col0), bk)
    kb0 = b_col0 // bk
    o_spec = pl.BlockSpec((bm, bn), lambda i, j, k: (i, j))
    return _mm(a, b, grid=(m // bm, n // bn, kd // bk),
               a_spec=pl.BlockSpec((bm, bk), lambda i, j, k: (i, k)),
               b_spec=pl.BlockSpec((bn, bk), lambda i, j, k: (j, kb0 + k)),
               o_spec=o_spec, o_shape=(m, n), o_dtype=o_dtype, contract=((1,), (1,)), name=name,
               add=add, add_spec=o_spec, acc_shape=(bm, bn), comm=comm)


def _mm_nt_sum(terms, *, bm, bn, bk, name, add=None, comm=None):
    m = terms[0][0].shape[0]
    n = terms[0][1].shape[0]
    bm, bn = _tile(m, bm, 8), _tile(n, bn)
    nt = (((1,), (1,)), ((), ()))
    plan, groups, start = [], [], 0
    for a, b, col0 in terms:
        kd = a.shape[1]
        tk = _tile(math.gcd(kd, col0), bk)
        steps = kd // tk
        last = groups[-1] if groups else None
        if last is not None and last[0] is b and last[4] == tk and (last[3] + last[2]) * tk == col0:
            last[2] += steps
        else:
            groups.append([b, start, steps, col0 // tk, tk])
        plan.append((start, steps, len(groups) - 1))
        start += steps
    nk = start
    nterm, ngroup, has_add = len(terms), len(groups), add is not None

    def body(*refs):
        a_refs, b_refs = refs[0:nterm], refs[nterm:nterm + ngroup]
        add_ref = refs[nterm + ngroup] if has_add else None
        o_ref, acc = refs[nterm + ngroup + has_add], refs[nterm + ngroup + has_add + 1]
        k = pl.program_id(2)
        for t, (s0, steps, grp) in enumerate(plan):
            @pl.when((k >= s0) & (k < s0 + steps))
            def _():
                part = lax.dot_general(a_refs[t][...], b_refs[grp][...], nt, preferred_element_type=F32)

                @pl.when(k == 0)
                def _():
                    acc[...] = part

                @pl.when(k > 0)
                def _():
                    acc[...] += part

        @pl.when(k == nk - 1)
        def _():
            o_ref[...] = acc[...] + add_ref[...] if has_add else acc[...]

    def a_spec(tk, s0, steps):
        return pl.BlockSpec((bm, tk), lambda i, j, k: (i, jnp.clip(k - s0, 0, steps - 1)))

    def b_spec(tk, s0, steps, off):
        return pl.BlockSpec((bn, tk), lambda i, j, k: (j, off + jnp.clip(k - s0, 0, steps - 1)))

    o_spec = pl.BlockSpec((bm, bn), lambda i, j, k: (i, j))
    in_specs = [a_spec(groups[grp][4], s0, steps) for s0, steps, grp in plan]
    in_specs += [b_spec(tk, s0, steps, cb0) for _, s0, steps, cb0, tk in groups]
    args = [a for a, _, _ in terms] + [grp[0] for grp in groups]
    if has_add:
        in_specs.append(o_spec)
        args.append(add)
    outs, comm_outs = _hosted_call(
        body, comm, name=name, grid=(m // bm, n // bn, nk), in_specs=in_specs,
        out_specs=[o_spec], out_shape=[jax.ShapeDtypeStruct((m, n), F32)],
        scratch_shapes=[pltpu.VMEM((bm, bn), F32)], args=args,
        sem=("parallel", "parallel", "arbitrary"), vmem=VMEM_BIG)
    return outs[0] if comm is None else (outs[0], comm_outs)


def _mm_tn(a, b, *, bm, bn, bk, o_dtype, name, comm=None):
    kd, m = a.shape
    n = b.shape[1]
    bm, bn, bk = _tile(m, bm), _tile(n, bn), _tile(kd, bk, 8)
    return _mm(a, b, grid=(m // bm, n // bn, kd // bk),
               a_spec=pl.BlockSpec((bk, bm), lambda i, j, k: (k, i)),
               b_spec=pl.BlockSpec((bk, bn), lambda i, j, k: (k, j)),
               o_spec=pl.BlockSpec((bm, bn), lambda i, j, k: (i, j)),
               o_shape=(m, n), o_dtype=o_dtype, contract=((0,), (0,)), name=name, acc_shape=(bm, bn), comm=comm)


def _branch_full(w8):
    kb, ds = w8.shape[0] // N_DEV, w8.shape[1]
    return w8.reshape(N_DEV, kb, ds).transpose(1, 0, 2).reshape(kb, N_DEV * ds)


def _branch_shards(g):
    kb, ds = g.shape[0], g.shape[1] // N_DEV
    return g.reshape(kb, N_DEV, ds).transpose(1, 0, 2).reshape(N_DEV * kb, ds)


def _headnorm_fwd(src, c0, width, bw, hd, gain, nflag, head_major, name):
    rows = src.shape[0]
    bm = _tile(rows, 2048 if bw <= 256 else 1024, 16)
    bd = _block_diag(hd)
    cb0 = c0 // bw

    def body(x_ref, g_ref, f_ref, bd_ref, o_ref):
        xv = x_ref[...].astype(F32)
        ss = _seg_sum(xv * xv, bd_ref[...])
        rstd = lax.rsqrt(ss * (1.0 / hd) + EPS)
        y = (xv * jnp.where(f_ref[...] > 0.0, rstd, 1.0) * g_ref[...]).astype(BF16)
        if head_major:
            for h in range(bw // HEAD_DIM):
                o_ref[h] = y[:, h * HEAD_DIM:(h + 1) * HEAD_DIM]
        else:
            o_ref[...] = y

    vec_spec = pl.BlockSpec((1, bw), lambda i, t: (0, t))
    if head_major:
        hpb = bw // HEAD_DIM
        out_spec = pl.BlockSpec((hpb, bm, HEAD_DIM), lambda i, t: (t, i, 0))
        out_shape = jax.ShapeDtypeStruct((width // HEAD_DIM, rows, HEAD_DIM), BF16)
    else:
        out_spec = pl.BlockSpec((bm, bw), lambda i, t: (i, t))
        out_shape = jax.ShapeDtypeStruct((rows, width), BF16)
    return pl.pallas_call(
        body, name=name, grid=(rows // bm, width // bw),
        in_specs=[pl.BlockSpec((bm, bw), lambda i, t: (i, cb0 + t)), vec_spec, vec_spec,
                  pl.BlockSpec((LANES, LANES), lambda i, t: (0, 0))],
        out_specs=out_spec, out_shape=out_shape,
        compiler_params=_params(("parallel", "parallel")),
    )(src, gain, nflag, bd)


def _headnorm_bwd(src, c0, width, bw, hd, gain, nflag, dyn, target, t0, name):
    rows = src.shape[0]
    bm = _tile(rows, 2048 if bw <= 256 else 1024, 16)
    bd = _block_diag(hd)
    cb0 = c0 // bw
    tb0 = t0 // bw
    aliased = target is not None

    def body(*refs):
        if aliased:
            x_ref, dy_ref, g_ref, f_ref, bd_ref, _, o_ref, dg_ref = refs
        else:
            x_ref, dy_ref, g_ref, f_ref, bd_ref, o_ref, dg_ref = refs
        i = pl.program_id(1)
        xv = x_ref[...].astype(F32)
        dyv = dy_ref[...]
        bdv = bd_ref[...]
        rstd = lax.rsqrt(_seg_sum(xv * xv, bdv) * (1.0 / hd) + EPS)
        xhat = xv * rstd
        g = dyv * g_ref[...]
        mean = _seg_sum(g * xhat, bdv) * (1.0 / hd)
        dx = jnp.where(f_ref[...] > 0.0, rstd * (g - xhat * mean), g)
        o_ref[...] = dx.astype(BF16)
        part = jnp.sum((dyv * xhat).reshape(bm // 8, 8, bw), axis=0)

        @pl.when(i == 0)
        def _():
            dg_ref[...] = part

        @pl.when(i > 0)
        def _():
            dg_ref[...] += part

    vec_spec = pl.BlockSpec((1, bw), lambda t, i: (0, t))
    in_specs = [pl.BlockSpec((bm, bw), lambda t, i: (i, cb0 + t)), pl.BlockSpec((bm, bw), lambda t, i: (i, t)),
                vec_spec, vec_spec, pl.BlockSpec((LANES, LANES), lambda t, i: (0, 0))]
    args = [src, dyn, gain, nflag, bd]
    aliases = {}
    if aliased:
        in_specs.append(pl.BlockSpec(memory_space=pl.ANY))
        args.append(target)
        aliases = {5: 0}
        o_shape = jax.ShapeDtypeStruct(target.shape, BF16)
    else:
        o_shape = jax.ShapeDtypeStruct((rows, width), BF16)
    out, dg = pl.pallas_call(
        body, name=name, grid=(width // bw, rows // bm), in_specs=in_specs,
        out_specs=[pl.BlockSpec((bm, bw), lambda t, i: (i, tb0 + t)), pl.BlockSpec((8, bw), lambda t, i: (0, t))],
        out_shape=[o_shape, jax.ShapeDtypeStruct((8, width), F32)],
        input_output_aliases=aliases,
        compiler_params=_params(("parallel", "arbitrary")),
    )(*args)
    return out, dg


def _fox_prep(pfb, bpad, name):
    s = pfb.shape[0]

    def body(p_ref, b_ref, c_ref):
        z = p_ref[...] + b_ref[...]
        logf = jnp.minimum(z, 0.0) - jnp.log(1.0 + jnp.exp(-jnp.abs(z)))
        x = logf.T[0:16, :]
        lane = lax.broadcasted_iota(jnp.int32, (16, s), 1)
        sh = 1
        while sh < s:
            x = x + jnp.where(lane >= sh, pltpu.roll(x, sh, 1), 0.0)
            sh *= 2
        c_ref[...] = x

    return pl.pallas_call(
        body, name=name, grid=(1,),
        in_specs=[pl.BlockSpec((s, FB_PAD), lambda i: (0, 0)), pl.BlockSpec((1, FB_PAD), lambda i: (0, 0))],
        out_specs=pl.BlockSpec((16, s), lambda i: (0, 0)),
        out_shape=jax.ShapeDtypeStruct((16, s), F32),
        compiler_params=_params(("arbitrary",)),
    )(pfb, bpad)


def _fox_prep_bwd(pfb, bpad, dct, name):
    s = pfb.shape[0]

    def body(p_ref, b_ref, dc_ref, df_ref, db_ref):
        zt = (p_ref[...] + b_ref[...]).T[0:16, :]
        y = dc_ref[...]
        lane = lax.broadcasted_iota(jnp.int32, (16, s), 1)
        sh = 1
        while sh < s:
            y = y + jnp.where(lane < s - sh, pltpu.roll(y, s - sh, 1), 0.0)
            sh *= 2
        dz = y * _sigmoid(-zt)
        db_ref[...] = jnp.broadcast_to(jnp.sum(dz, axis=1, keepdims=True), (16, FB_PAD))
        full = jnp.concatenate([dz, jnp.zeros((FB_PAD - 16, s), F32)], axis=0)
        df_ref[...] = full.T.astype(BF16)

    return pl.pallas_call(
        body, name=name, grid=(1,),
        in_specs=[pl.BlockSpec((s, FB_PAD), lambda i: (0, 0)), pl.BlockSpec((1, FB_PAD), lambda i: (0, 0)),
                  pl.BlockSpec((16, s), lambda i: (0, 0))],
        out_specs=[pl.BlockSpec((s, FB_PAD), lambda i: (0, 0)), pl.BlockSpec((16, FB_PAD), lambda i: (0, 0))],
        out_shape=[jax.ShapeDtypeStruct((s, FB_PAD), BF16), jax.ShapeDtypeStruct((16, FB_PAD), F32)],
        compiler_params=_params(("arbitrary",)),
    )(pfb, bpad, dct)


def _swa_window(n):
    ws = pl.multiple_of(jnp.maximum(n * WINDOW - WINDOW, 0), WINDOW)
    qi = lax.broadcasted_iota(jnp.int32, (WINDOW, 2 * WINDOW), 0)
    kj = lax.broadcasted_iota(jnp.int32, (WINDOW, 2 * WINDOW), 1)
    rel = qi + (n * WINDOW - ws) - kj
    valid = (rel >= 0) & (rel < WINDOW)
    return ws, valid, rel.astype(F32)


def _attn_a_fwd(qkv, sinks, slopes, name):
    s = qkv.shape[1]
    nb = s // WINDOW
    smem = pl.BlockSpec(memory_space=pltpu.SMEM)

    def body(sink_ref, slope_ref, q_ref, k_ref, v_ref, o_ref, lse_ref):
        n = pl.program_id(0)
        ws, valid, relf = _swa_window(n)
        outs = []
        for h in range(A_Q_HEADS):
            kvh = h // A_GROUP
            kw = k_ref[kvh, pl.ds(ws, 2 * WINDOW), :]
            vw = v_ref[kvh, pl.ds(ws, 2 * WINDOW), :]
            sc = lax.dot_general(q_ref[h], kw, (((1,), (1,)), ((), ())), preferred_element_type=F32)
            sc = jnp.where(valid, sc - slope_ref[h] * relf, NEG)
            sink = sink_ref[h]
            m = jnp.maximum(jnp.max(sc, axis=1, keepdims=True), sink)
            p = jnp.exp(sc - m)
            denom = jnp.sum(p, axis=1, keepdims=True) + jnp.exp(sink - m)
            pn = (p / denom).astype(BF16)
            outs.append(jnp.dot(pn, vw, preferred_element_type=F32))
            lse_ref[h] = jnp.broadcast_to(m + jnp.log(denom), (WINDOW, HEAD_DIM))
        o_ref[...] = jnp.concatenate(outs, axis=1)

    return pl.pallas_call(
        body, name=name, grid=(nb,),
        in_specs=[smem, smem,
                  pl.BlockSpec((A_Q_HEADS, WINDOW, HEAD_DIM), lambda n: (0, n, 0)),
                  pl.BlockSpec((A_KV_HEADS, s, HEAD_DIM), lambda n: (A_GROUP, 0, 0)),
                  pl.BlockSpec((A_KV_HEADS, s, HEAD_DIM), lambda n: (A_GROUP + 1, 0, 0))],
        out_specs=[pl.BlockSpec((WINDOW, A_WIDTH), lambda n: (n, 0)),
                   pl.BlockSpec((A_Q_HEADS, WINDOW, HEAD_DIM), lambda n: (0, n, 0))],
        out_shape=[jax.ShapeDtypeStruct((s, A_WIDTH), F32), jax.ShapeDtypeStruct((A_Q_HEADS, s, HEAD_DIM), F32)],
        compiler_params=_params(("parallel",), VMEM_BIG),
    )(sinks, slopes, qkv, qkv, qkv)


def _attn_a_bwd(qkv, do, lse, dd, sinks, slopes, name, comm=None):
    s = qkv.shape[1]
    nb = s // WINDOW
    smem = pl.BlockSpec(memory_space=pltpu.SMEM)
    last = nb - 1

    def body(sink_ref, slope_ref, q_ref, k_ref, v_ref, do_ref, lse_ref, dd_ref, dq_ref, dkv_ref, ds_ref, carry):
        n = pl.program_id(0)

        @pl.when(n == 0)
        def _():
            carry[...] = jnp.zeros(carry.shape, F32)
            ds_ref[...] = jnp.zeros(ds_ref.shape, F32)

        @pl.when(n < nb)
        def _():
            ws, valid, relf = _swa_window(n)
            dqs = []
            dkw = [None] * A_KV_HEADS
            dvw = [None] * A_KV_HEADS
            for h in range(A_Q_HEADS):
                kvh = h // A_GROUP
                qh = q_ref[h]
                doh = do_ref[h]
                kw = k_ref[kvh, pl.ds(ws, 2 * WINDOW), :]
                vw = v_ref[kvh, pl.ds(ws, 2 * WINDOW), :]
                lse_h = lse_ref[h]
                dd_h = dd_ref[h]
                sc = lax.dot_general(qh, kw, (((1,), (1,)), ((), ())), preferred_element_type=F32)
                sc = jnp.where(valid, sc - slope_ref[h] * relf, NEG)
                p = jnp.exp(sc - lse_h[:, 0:1])
                dp = lax.dot_general(doh, vw, (((1,), (1,)), ((), ())), preferred_element_type=F32)
                dsc = (p * (dp - dd_h[:, 0:1])).astype(BF16)
                pb = p.astype(BF16)
                dqs.append(jnp.dot(dsc, kw, preferred_element_type=F32))
                dk_h = lax.dot_general(dsc, qh, (((0,), (0,)), ((), ())), preferred_element_type=F32)
                dv_h = lax.dot_general(pb, doh, (((0,), (0,)), ((), ())), preferred_element_type=F32)
                dkw[kvh] = dk_h if dkw[kvh] is None else dkw[kvh] + dk_h
                dvw[kvh] = dv_h if dvw[kvh] is None else dvw[kvh] + dv_h
                psink = jnp.exp(sink_ref[h] - lse_h)
                ds_ref[h] += jnp.sum((-psink * dd_h).reshape(WINDOW // 8, 8, HEAD_DIM), axis=0)
            dq_ref[...] = jnp.concatenate(dqs, axis=1)
            win = jnp.concatenate(dkw + dvw, axis=1)
            first = win[0:WINDOW]
            second = win[WINDOW:2 * WINDOW]
            dkv_ref[...] = carry[...] + first
            carry[...] = jnp.where(n == 0, first, second)

        @pl.when(n == nb)
        def _():
            dkv_ref[...] = carry[...]

    hm = lambda heads: pl.BlockSpec((heads, WINDOW, HEAD_DIM), lambda n: (0, jnp.minimum(n, last), 0))
    res = lambda blk: pl.BlockSpec((A_KV_HEADS, s, HEAD_DIM), lambda n: (blk, 0, 0))
    outs, comm_outs = _hosted_call(
        body, comm, name=name, grid=(nb + 1,),
        in_specs=[smem, smem, hm(A_Q_HEADS), res(A_GROUP), res(A_GROUP + 1), hm(A_Q_HEADS), hm(A_Q_HEADS), hm(A_Q_HEADS)],
        out_specs=[pl.BlockSpec((WINDOW, A_WIDTH), lambda n: (jnp.minimum(n, last), 0)),
                   pl.BlockSpec((WINDOW, 2 * A_KV_WIDTH), lambda n: (jnp.maximum(n - 1, 0), 0)),
                   pl.BlockSpec((A_Q_HEADS, 8, HEAD_DIM), lambda n: (0, 0, 0))],
        out_shape=[jax.ShapeDtypeStruct((s, A_WIDTH), F32), jax.ShapeDtypeStruct((s, 2 * A_KV_WIDTH), F32),
                   jax.ShapeDtypeStruct((A_Q_HEADS, 8, HEAD_DIM), F32)],
        scratch_shapes=[pltpu.VMEM((WINDOW, 2 * A_KV_WIDTH), F32)],
        args=[sinks, slopes, qkv, qkv, qkv, do, lse, dd], sem=("arbitrary",), vmem=VMEM_BIG)
    return outs[0], outs[1], outs[2], comm_outs


def _attn_b_fwd(qkv, c3, name, comm=None):
    heads, s = qkv.shape[0] // 3, qkv.shape[1]
    hpairs = heads // 2
    bq = min(512, s)
    nq = s // bq
    nt = (((1,), (1,)), ((), ()))

    def body(q_ref, k_ref, v_ref, c_ref, o_ref, lse_ref, m_scr, l_scr, acc_scr):
        i = pl.program_id(1)
        r0 = pl.multiple_of(i * bq, bq)
        row = lax.broadcasted_iota(jnp.int32, (bq, bq), 0)
        col = lax.broadcasted_iota(jnp.int32, (bq, bq), 1)
        m_scr[...] = jnp.full((2, bq, LANES), NEG, F32)
        l_scr[...] = jnp.zeros((2, bq, LANES), F32)
        acc_scr[...] = jnp.zeros((2, bq, HEAD_DIM), F32)

        def step(j, masked):
            k0 = pl.multiple_of(j * bq, bq)
            for h2 in range(2):
                kv = k_ref[h2, pl.ds(k0, bq), :]
                vv = v_ref[h2, pl.ds(k0, bq), :]
                cq0 = c_ref[h2, :, pl.ds(r0, LANES)][:, 0:1]
                sc = lax.dot_general(q_ref[h2], kv, nt, preferred_element_type=F32)
                sc = sc + (cq0 - c_ref[h2, :, pl.ds(k0, bq)])
                if masked:
                    sc = jnp.where(col <= row, sc, NEG)
                m_prev = m_scr[h2]
                m_new = jnp.maximum(m_prev, jnp.max(sc, axis=1, keepdims=True))
                alpha = jnp.exp(m_prev - m_new)
                p = jnp.exp(sc - m_new[:, 0:1])
                l_scr[h2] = alpha * l_scr[h2] + jnp.sum(p, axis=1, keepdims=True)
                p_hi = p.astype(BF16)
                p_lo = (p - p_hi.astype(F32)).astype(BF16)
                pv = jnp.dot(p_hi, vv, preferred_element_type=F32) + jnp.dot(p_lo, vv, preferred_element_type=F32)
                acc_scr[h2] = acc_scr[h2] * alpha[:, 0:HEAD_DIM] + pv
                m_scr[h2] = m_new

        def loop_body(j, carry):
            step(j, False)
            return carry

        lax.fori_loop(0, i, loop_body, 0)
        step(i, True)
        outs = []
        for h2 in range(2):
            l = l_scr[h2]
            outs.append(acc_scr[h2] / l[:, 0:HEAD_DIM])
            lse_ref[h2] = (m_scr[h2] + jnp.log(l))[:, 0:HEAD_DIM]
        o_ref[...] = jnp.concatenate(outs, axis=1)

    res = lambda off: pl.BlockSpec((2, s, HEAD_DIM), lambda hp, i: (off + hp, 0, 0))
    outs, comm_outs = _hosted_call(
        body, comm, name=name, grid=(hpairs, nq),
        in_specs=[pl.BlockSpec((2, bq, HEAD_DIM), lambda hp, i: (hp, i, 0)), res(hpairs), res(2 * hpairs),
                  pl.BlockSpec((2, 1, s), lambda hp, i: (hp, 0, 0))],
        out_specs=[pl.BlockSpec((bq, 2 * HEAD_DIM), lambda hp, i: (i, hp)),
                   pl.BlockSpec((2, bq, HEAD_DIM), lambda hp, i: (hp, i, 0))],
        out_shape=[jax.ShapeDtypeStruct((s, heads * HEAD_DIM), F32), jax.ShapeDtypeStruct((heads, s, HEAD_DIM), F32)],
        scratch_shapes=[pltpu.VMEM((2, bq, LANES), F32), pltpu.VMEM((2, bq, LANES), F32), pltpu.VMEM((2, bq, HEAD_DIM), F32)],
        args=[qkv, qkv, qkv, c3], sem=("parallel", "parallel"), vmem=VMEM_BIG)
    return outs[0], outs[1], comm_outs


def _attn_b_bwd(qkv, do, lse, dd, c3, name, comm=None):
    heads, s = qkv.shape[0] // 3, qkv.shape[1]
    hpairs = heads // 2
    bq = min(512, s)
    nq = s // bq
    nt = (((1,), (1,)), ((), ()))
    tn = (((0,), (0,)), ((), ()))
    grid = (heads // 2, nq)

    def body(q_ref, k_ref, v_ref, do_ref, lse_ref, dd_ref, c_ref, dq_ref, dk_ref, dv_ref, dc_ref,
             dq_scr, dk_scr, dv_scr, dc_scr):
        j = pl.program_id(1)
        k0 = pl.multiple_of(j * bq, bq)
        row = lax.broadcasted_iota(jnp.int32, (bq, bq), 0)
        col = lax.broadcasted_iota(jnp.int32, (bq, bq), 1)

        @pl.when(j == 0)
        def _():
            dq_scr[...] = jnp.zeros(dq_scr.shape, F32)

        dk_scr[...] = jnp.zeros((2, HEAD_DIM, bq), F32)
        dv_scr[...] = jnp.zeros((2, HEAD_DIM, bq), F32)
        dc_scr[...] = jnp.zeros((2, 1, bq), F32)
        k_t = [k_ref[h2].T for h2 in range(2)]

        def step(i, masked):
            r0 = pl.multiple_of(i * bq, bq)
            for h2 in range(2):
                kv = k_ref[h2]
                vv = v_ref[h2]
                qv = q_ref[h2, pl.ds(r0, bq), :]
                dov = do_ref[h2, pl.ds(r0, bq), :]
                lse_v = lse_ref[h2, pl.ds(r0, bq), :][:, 0:1]
                dd_v = dd_ref[h2, pl.ds(r0, bq), :][:, 0:1]
                cq0 = c_ref[h2, :, pl.ds(r0, LANES)][:, 0:1]
                sc = lax.dot_general(qv, kv, nt, preferred_element_type=F32) + (cq0 - c_ref[h2, :, pl.ds(k0, bq)])
                if masked:
                    sc = jnp.where(col <= row, sc, NEG)
                p = jnp.exp(sc - lse_v)
                dp = lax.dot_general(dov, vv, nt, preferred_element_type=F32)
                dsc = p * (dp - dd_v)
                dsb = dsc.astype(BF16)
                dv_scr[h2] += jnp.dot(dov.T, p.astype(BF16), preferred_element_type=F32)
                dk_scr[h2] += jnp.dot(qv.T, dsb, preferred_element_type=F32)
                dq_scr[h2, :, pl.ds(r0, bq)] += jnp.dot(k_t[h2], dsb.T, preferred_element_type=F32)
                dc_scr[h2] -= jnp.sum(dsc, axis=0, keepdims=True)

        def loop_body(i, carry):
            step(i, False)
            return carry

        step(j, True)
        lax.fori_loop(j + 1, nq, loop_body, 0)
        dc_ref[...] = dc_scr[...]
        dk_ref[...] = jnp.concatenate([dk_scr[0].T, dk_scr[1].T], axis=1)
        dv_ref[...] = jnp.concatenate([dv_scr[0].T, dv_scr[1].T], axis=1)

        @pl.when(j == nq - 1)
        def _():
            dq_ref[...] = jnp.concatenate([dq_scr[0].T, dq_scr[1].T], axis=1)

    res = pl.BlockSpec((2, s, HEAD_DIM), lambda hp, j: (hp, 0, 0))
    blk = lambda off: pl.BlockSpec((2, bq, HEAD_DIM), lambda hp, j: (off + hp, j, 0))
    tm = jax.ShapeDtypeStruct((s, heads * HEAD_DIM), F32)
    in_specs = [res, blk(hpairs), blk(2 * hpairs), res, res, res, pl.BlockSpec((2, 1, s), lambda hp, j: (hp, 0, 0))]
    out_specs = [pl.BlockSpec((s, 2 * HEAD_DIM), lambda hp, j: (0, hp)),
                 pl.BlockSpec((bq, 2 * HEAD_DIM), lambda hp, j: (j, hp)),
                 pl.BlockSpec((bq, 2 * HEAD_DIM), lambda hp, j: (j, hp)),
                 pl.BlockSpec((2, 1, bq), lambda hp, j: (hp, 0, j))]
    out_shape = [tm, tm, tm, jax.ShapeDtypeStruct((heads, 1, s), F32)]
    scratch = [pltpu.VMEM((2, HEAD_DIM, s), F32), pltpu.VMEM((2, HEAD_DIM, bq), F32),
               pltpu.VMEM((2, HEAD_DIM, bq), F32), pltpu.VMEM((2, 1, bq), F32)]
    outs, comm_outs = _hosted_call(
        body, comm, name=name, grid=grid, in_specs=in_specs, out_specs=out_specs, out_shape=out_shape,
        scratch_shapes=scratch, args=[qkv, qkv, qkv, do, lse, dd, c3], sem=("parallel", "arbitrary"), vmem=VMEM_BIG)
    return outs[0], outs[1], outs[2], outs[3], comm_outs


def _attn_c_probs(qh, mkh):
    sc = lax.dot_general(qh, mkh, (((1,), (1,)), ((), ())), preferred_element_type=F32) * (C_HEAD_DIM ** -0.5)
    p = jnp.exp(sc - jnp.max(sc, axis=1, keepdims=True))
    return p / jnp.sum(p, axis=1, keepdims=True)


def _attn_c_fwd(q, mkv, name):
    s = q.shape[0]
    m = mkv.shape[0]
    bq = _tile(s, 512, 8)

    def body(q_ref, mk_ref, mv_ref, o_ref):
        outs = []
        for h in range(C_HEADS):
            sl = slice(h * C_HEAD_DIM, (h + 1) * C_HEAD_DIM)
            pn = _attn_c_probs(q_ref[:, sl], mk_ref[:, sl]).astype(BF16)
            outs.append(jnp.dot(pn, mv_ref[:, sl], preferred_element_type=F32))
        o_ref[...] = jnp.concatenate(outs, axis=1)

    return pl.pallas_call(
        body, name=name, grid=(s // bq,),
        in_specs=[pl.BlockSpec((bq, C_WIDTH), lambda i: (i, 0)), pl.BlockSpec((m, C_WIDTH), lambda i: (0, 0)),
                  pl.BlockSpec((m, C_WIDTH), lambda i: (0, 1))],
        out_specs=pl.BlockSpec((bq, C_WIDTH), lambda i: (i, 0)),
        out_shape=jax.ShapeDtypeStruct((s, C_WIDTH), F32),
        compiler_params=_params(("parallel",)),
    )(q, mkv, mkv)


def _attn_c_bwd(q, mkv, do, name):
    s = q.shape[0]
    m = mkv.shape[0]
    bq = _tile(s, 512, 8)
    tn = (((0,), (0,)), ((), ()))

    def body(q_ref, mk_ref, mv_ref, do_ref, dq_ref, dm_ref):
        i = pl.program_id(0)

        @pl.when(i == 0)
        def _():
            dm_ref[...] = jnp.zeros(dm_ref.shape, F32)

        dqs = []
        for h in range(C_HEADS):
            sl = slice(h * C_HEAD_DIM, (h + 1) * C_HEAD_DIM)
            qh, mkh, mvh, doh = q_ref[:, sl], mk_ref[:, sl], mv_ref[:, sl], do_ref[:, sl]
            pn = _attn_c_probs(qh, mkh)
            dp = lax.dot_general(doh, mvh, (((1,), (1,)), ((), ())), preferred_element_type=F32)
            dsc = (pn * (dp - jnp.sum(pn * dp, axis=1, keepdims=True)) * (C_HEAD_DIM ** -0.5)).astype(BF16)
            dqs.append(jnp.dot(dsc, mkh, preferred_element_type=F32))
            dm_ref[:, sl] += lax.dot_general(dsc, qh, tn, preferred_element_type=F32)
            sv = slice(C_WIDTH + h * C_HEAD_DIM, C_WIDTH + (h + 1) * C_HEAD_DIM)
            dm_ref[:, sv] += lax.dot_general(pn.astype(BF16), doh, tn, preferred_element_type=F32)
        dq_ref[...] = jnp.concatenate(dqs, axis=1)

    row = pl.BlockSpec((bq, C_WIDTH), lambda i: (i, 0))
    return pl.pallas_call(
        body, name=name, grid=(s // bq,),
        in_specs=[row, pl.BlockSpec((m, C_WIDTH), lambda i: (0, 0)), pl.BlockSpec((m, C_WIDTH), lambda i: (0, 1)), row],
        out_specs=[row, pl.BlockSpec((m, 2 * C_WIDTH), lambda i: (0, 0))],
        out_shape=[jax.ShapeDtypeStruct((s, C_WIDTH), F32), jax.ShapeDtypeStruct((m, 2 * C_WIDTH), F32)],
        compiler_params=_params(("arbitrary",)),
    )(q, mkv, mkv, do)


def _gate_fwd(y, proj, zc0, bw, name):
    rows, width = y.shape
    bm = _tile(rows, 2048 if bw <= 256 else 1024, 16)
    cb0 = zc0 // bw

    def body(y_ref, z_ref, o_ref):
        z = z_ref[...].astype(F32)
        o_ref[...] = (y_ref[...] * (z * _sigmoid(z))).astype(BF16)

    return pl.pallas_call(
        body, name=name, grid=(rows // bm, width // bw),
        in_specs=[pl.BlockSpec((bm, bw), lambda i, t: (i, t)), pl.BlockSpec((bm, bw), lambda i, t: (i, cb0 + t))],
        out_specs=pl.BlockSpec((bm, bw), lambda i, t: (i, t)),
        out_shape=jax.ShapeDtypeStruct((rows, width), BF16),
        compiler_params=_params(("parallel", "parallel")),
    )(y, proj)


def _gate_bwd(dsv, y, proj, zc0, bw, dproj, t0, head_major, name):
    rows, width = y.shape
    bm = _tile(rows, 2048 if bw <= 256 else 1024, 16)
    cb0 = zc0 // bw
    tb0 = t0 // bw
    bd = _block_diag(HEAD_DIM)
    hpb = bw // HEAD_DIM

    def body(*refs):
        if head_major:
            ds_ref, y_ref, z_ref, bd_ref, _, dp_ref, dy_ref, dd_ref = refs
        else:
            ds_ref, y_ref, z_ref, _, dp_ref, dy_ref = refs
        z = z_ref[...].astype(F32)
        sig = _sigmoid(z)
        dsx = ds_ref[...]
        yv = y_ref[...]
        dy = dsx * (z * sig)
        dp_ref[...] = (dsx * yv * (sig * (1.0 + z * (1.0 - sig)))).astype(BF16)
        if head_major:
            dyb = dy.astype(BF16)
            dd = _seg_sum(dyb.astype(F32) * yv, bd_ref[...])
            for h in range(hpb):
                sl = slice(h * HEAD_DIM, (h + 1) * HEAD_DIM)
                dy_ref[h] = dyb[:, sl]
                dd_ref[h] = dd[:, sl]
        else:
            dy_ref[...] = dy.astype(BF16)

    tile = pl.BlockSpec((bm, bw), lambda i, t: (i, t))
    ztile = pl.BlockSpec((bm, bw), lambda i, t: (i, cb0 + t))
    ttile = pl.BlockSpec((bm, bw), lambda i, t: (i, tb0 + t))
    any_spec = pl.BlockSpec(memory_space=pl.ANY)
    dp_shape = jax.ShapeDtypeStruct(dproj.shape, BF16)
    if head_major:
        hm_spec = pl.BlockSpec((hpb, bm, HEAD_DIM), lambda i, t: (t, i, 0))
        nh = width // HEAD_DIM
        outs = pl.pallas_call(
            body, name=name, grid=(rows // bm, width // bw),
            in_specs=[tile, tile, ztile, pl.BlockSpec((LANES, LANES), lambda i, t: (0, 0)), any_spec],
            out_specs=[ttile, hm_spec, hm_spec],
            out_shape=[dp_shape, jax.ShapeDtypeStruct((nh, rows, HEAD_DIM), BF16),
                       jax.ShapeDtypeStruct((nh, rows, HEAD_DIM), F32)],
            input_output_aliases={4: 0},
            compiler_params=_params(("parallel", "parallel")),
        )(dsv, y, proj, bd, dproj)
        return outs[0], outs[1], outs[2]
    outs = pl.pallas_call(
        body, name=name, grid=(rows // bm, width // bw),
        in_specs=[tile, tile, ztile, any_spec],
        out_specs=[ttile, tile],
        out_shape=[dp_shape, jax.ShapeDtypeStruct((rows, width), BF16)],
        input_output_aliases={3: 0},
        compiler_params=_params(("parallel", "parallel")),
    )(dsv, y, proj, dproj)
    return outs[0], outs[1], None


def _merge_fwd(proj, ua, ub, uc, name):
    rows, d = ua.shape
    bm = _tile(rows, 1024, 16)
    bw = _tile(d, 512)
    g0 = COL_GATE // bw
    gstep = d // bw

    def body(la_ref, lb_ref, lc_ref, ua_ref, ub_ref, uc_ref, o_ref, ga_ref, gb_ref, gc_ref):
        y = None
        for l_ref, u_ref, g_ref in ((la_ref, ua_ref, ga_ref), (lb_ref, ub_ref, gb_ref), (lc_ref, uc_ref, gc_ref)):
            g = _sigmoid(l_ref[...].astype(F32))
            g_ref[...] = g.astype(BF16)
            term = g * u_ref[...].astype(F32)
            y = term if y is None else y + term
        o_ref[...] = y.astype(BF16)

    tile = pl.BlockSpec((bm, bw), lambda i, t: (i, t))
    gate = lambda b: pl.BlockSpec((bm, bw), lambda i, t: (i, g0 + b * gstep + t))
    shape = jax.ShapeDtypeStruct((rows, d), BF16)
    return pl.pallas_call(
        body, name=name, grid=(rows // bm, d // bw),
        in_specs=[gate(0), gate(1), gate(2), tile, tile, tile],
        out_specs=[tile] * 4, out_shape=[shape] * 4,
        compiler_params=_params(("parallel", "parallel")),
    )(proj, proj, proj, ua, ub, uc)


def _merge_bwd(dym, us, gs, name):
    rows, d = dym.shape
    bm = _tile(rows, 1024, 16)
    bw = _tile(d, 512)
    nb = d // bw

    def body(dy_ref, ua_ref, ub_ref, uc_ref, ga_ref, gb_ref, gc_ref, dg_ref, da_ref, db_ref, dc_ref):
        b = pl.program_id(2)
        dyv = dy_ref[...]
        for idx, (u_ref, g_ref, du_ref) in enumerate(((ua_ref, ga_ref, da_ref), (ub_ref, gb_ref, db_ref), (uc_ref, gc_ref, dc_ref))):
            @pl.when(b == idx)
            def _():
                g = g_ref[...].astype(F32)
                du_ref[...] = (g * dyv).astype(BF16)
                dg_ref[...] = (dyv * u_ref[...].astype(F32) * g * (1.0 - g)).astype(BF16)

    tile = pl.BlockSpec((bm, bw), lambda i, t, b: (i, t))
    shape = jax.ShapeDtypeStruct((rows, d), BF16)
    outs = pl.pallas_call(
        body, name=name, grid=(rows // bm, nb, 3),
        in_specs=[tile] * 7,
        out_specs=[pl.BlockSpec((bm, bw), lambda i, t, b: (i, b * nb + t)), tile, tile, tile],
        out_shape=[jax.ShapeDtypeStruct((rows, 3 * d), BF16), shape, shape, shape],
        compiler_params=_params(("parallel", "parallel", "arbitrary")),
    )(dym, *us, *gs)
    return outs[0], outs[1], outs[2], outs[3]


def _out_proj_loss(ym, wo, x, target, name):
    m, d = x.shape
    bm, bn = _tile(m, 1024, 16), _tile(d, 1024)
    grid = (m // bm, d // bn)

    def body(a_ref, b_ref, x_ref, t_ref, dy_ref, dyb_ref, l_ref):
        first, _ = _grid_edges(grid)
        y = jnp.dot(a_ref[...], b_ref[...], preferred_element_type=F32) + x_ref[...]
        diff = y - t_ref[...]
        dy = diff * (1.0 / d)
        dy_ref[...] = dy
        dyb_ref[...] = dy.astype(BF16)
        sq = diff * diff
        part = sq[:, 0:LANES]
        for c in range(1, bn // LANES):
            part = part + sq[:, c * LANES:(c + 1) * LANES]
        part = jnp.sum(part.reshape(bm // 8, 8, LANES), axis=0)

        @pl.when(first)
        def _():
            l_ref[...] = part

        @pl.when(jnp.logical_not(first))
        def _():
            l_ref[...] += part

    tile = pl.BlockSpec((bm, bn), lambda i, j: (i, j))
    return pl.pallas_call(
        body, name=name, grid=grid,
        in_specs=[pl.BlockSpec((bm, d), lambda i, j: (i, 0)), pl.BlockSpec((d, bn), lambda i, j: (0, j)), tile, tile],
        out_specs=[tile, tile, pl.BlockSpec((8, LANES), lambda i, j: (0, 0))],
        out_shape=[jax.ShapeDtypeStruct((m, d), F32), jax.ShapeDtypeStruct((m, d), BF16),
                   jax.ShapeDtypeStruct((8, LANES), F32)],
        compiler_params=_params(("arbitrary", "arbitrary"), VMEM_BIG),
    )(ym, wo, x, target)


def _row(vec, reps=1):
    return jnp.tile(vec.reshape(1, -1).astype(F32), (1, reps))


def _local_step(x, mem, target, small, wg, shards=None):
    s, d = x.shape
    dist = shards is not None
    wg = dict(wg)
    ones = lambda n: jnp.ones((1, n), F32)
    zeros = lambda n: jnp.zeros((1, n), F32)
    scale_ab = HEAD_DIM ** -0.5
    split8 = lambda g: g.reshape(N_DEV, g.shape[0] // N_DEV, g.shape[1])
    flat8 = lambda g: g.reshape(g.shape[0] * g.shape[1], g.shape[2])
    gather = lambda names: _Comm("gather", [shards[n] for n in names]) if dist else None
    g = {}

    def scatter(names):
        return _Comm("scatter", [split8(g[n]) for n in names]) if dist else None

    def hosted(result, names, store):
        if not dist:
            return result
        out, got = result
        store.update(zip(names, got))
        return out

    hn = _rmsnorm_fwd(x, small["norm_gain"], "rms_x_fwd")
    got = {}
    proj = hosted(_mm_nn(hn, wg["qkv"], bm=1024, bn=1024, bk=d, o_dtype=BF16, name="proj_qkv",
                         comm=gather(("wa", "wb", "wc"))), ("wa", "wb", "wc"), got)
    wg.update({n: flat8(a) for n, a in got.items()})
    pfb = _mm_nn(hn, wg["wf"], bm=1024, bn=FB_PAD, bk=d, o_dtype=F32, name="proj_fb")
    mn = _rmsnorm_fwd(mem, small["mem_norm_gain"], "rms_mem_fwd")
    mkv = _mm_nn(mn, wg["wk"], bm=256, bn=1024, bk=d, o_dtype=F32, name="mem_kv")

    gain_a = jnp.concatenate([_row(small["q_gain_a"], A_Q_HEADS) * scale_ab, _row(small["k_gain_a"], A_KV_HEADS), ones(A_KV_WIDTH)], axis=1)
    flag_a = jnp.concatenate([ones(A_WIDTH + A_KV_WIDTH), zeros(A_KV_WIDTH)], axis=1)
    qkv_a = _headnorm_fwd(proj, COL_QA, 1280, 1280, HEAD_DIM, gain_a, flag_a, True, "hn_a_fwd")
    gain_b = jnp.concatenate([_row(small["q_gain_b"], B_HEADS) * scale_ab, _row(small["k_gain_b"], B_HEADS), ones(B_WIDTH)], axis=1)
    flag_b = jnp.concatenate([ones(2 * B_WIDTH), zeros(B_WIDTH)], axis=1)
    qkv_b = _headnorm_fwd(proj, COL_QB, 2304, 256, HEAD_DIM, gain_b, flag_b, True, "hn_b_fwd")
    gain_cq = _row(small["q_gain_c"], C_HEADS)
    q_c = _headnorm_fwd(proj, COL_QC, C_WIDTH, C_WIDTH, C_HEAD_DIM, gain_cq, ones(C_WIDTH), False, "hn_cq_fwd")
    gain_ck = jnp.concatenate([_row(small["k_gain_c"], C_HEADS), ones(C_WIDTH)], axis=1)
    flag_ck = jnp.concatenate([ones(C_WIDTH), zeros(C_WIDTH)], axis=1)
    mkvn = _headnorm_fwd(mkv, 0, 2 * C_WIDTH, 2 * C_WIDTH, C_HEAD_DIM, gain_ck, flag_ck, False, "hn_ck_fwd")


    bpad = jnp.pad(small["b_forget"].reshape(1, -1), ((0, 0), (0, FB_PAD - B_HEADS)))
    c16 = _fox_prep(pfb, bpad, "fox_prep")
    c3 = c16[0:B_HEADS].reshape(B_HEADS, 1, s)

    sinks = small["sinks_a"].reshape(-1)
    slopes = jnp.exp2(-8.0 * jnp.arange(1, A_Q_HEADS + 1, dtype=F32) / A_Q_HEADS)
    y_a, lse_a = _attn_a_fwd(qkv_a, sinks, slopes, "attn_a_fwd")
    y_b, lse_b, got_zg = _attn_b_fwd(qkv_b, c3, "attn_b_fwd", comm=gather(("zg",)))
    if dist:
        wg["zg"] = flat8(got_zg[0])
    y_c = _attn_c_fwd(q_c, mkvn, "attn_c_fwd")

    got = {}
    pzg = hosted(_mm_nn(hn, wg["zg"], bm=1024, bn=1024, bk=d, o_dtype=BF16, name="proj_zg", comm=gather(("wo",))),
                 ("wo",), got)
    wg.update({n: flat8(a) for n, a in got.items()})

    s_a = _gate_fwd(y_a, pzg, COL_ZA, 256, "gate_a_fwd")
    s_b = _gate_fwd(y_b, pzg, COL_ZB, 256, "gate_b_fwd")
    s_c = _gate_fwd(y_c, pzg, COL_ZC, 512, "gate_c_fwd")
    w_a, w_b, w_c = _branch_full(wg["wa"]), _branch_full(wg["wb"]), _branch_full(wg["wc"])
    u_a = _mm_nn(s_a, w_a, bm=1024, bn=2048, bk=A_WIDTH, o_dtype=BF16, name="branch_a_fwd")
    u_b = _mm_nn(s_b, w_b, bm=1024, bn=2048, bk=B_WIDTH, o_dtype=BF16, name="branch_b_fwd")
    u_c = _mm_nn(s_c, w_c, bm=1024, bn=2048, bk=C_WIDTH, o_dtype=BF16, name="branch_c_fwd")
    ym, gate_a, gate_b, gate_c = _merge_fwd(pzg, u_a, u_b, u_c, "merge_fwd")
    dy, dyb, lpart = _out_proj_loss(ym, wg["wo"], x, target, "out_proj_loss")
    loss = 0.5 / d * jnp.sum(lpart)

    dym = _mm_nt(dyb, wg["wo"], bm=1024, bn=1024, bk=d, o_dtype=F32, name="out_proj_bwd_act")
    g["wo"] = _mm_tn(ym, dyb, bm=512, bn=1024, bk=s, o_dtype=BF16, name="out_proj_bwd_w")

    dgate, du_a, du_b, du_c = _merge_bwd(dym, (u_a, u_b, u_c), (gate_a, gate_b, gate_c), "merge_bwd")
    parts = {}
    g["wm_g"] = hosted(_mm_tn(hn, dgate, bm=512, bn=1024, bk=s, o_dtype=BF16, name="proj_gate_bwd_w",
                              comm=scatter(("wo",))), ("wo",), parts)

    ds_a = _mm_nt(du_a, w_a, bm=1024, bn=A_WIDTH, bk=d, o_dtype=F32, name="branch_a_bwd_act")
    ds_b = _mm_nt(du_b, w_b, bm=1024, bn=B_WIDTH, bk=d, o_dtype=F32, name="branch_b_bwd_act")
    ds_c = _mm_nt(du_c, w_c, bm=1024, bn=C_WIDTH, bk=d, o_dtype=F32, name="branch_c_bwd_act")
    g["wa"] = _branch_shards(_mm_tn(s_a, du_a, bm=A_WIDTH, bn=1024, bk=s, o_dtype=BF16, name="branch_a_bwd_w"))
    g["wb"] = _branch_shards(_mm_tn(s_b, du_b, bm=B_WIDTH, bn=1024, bk=s, o_dtype=BF16, name="branch_b_bwd_w"))
    g["wc"] = _branch_shards(_mm_tn(s_c, du_c, bm=C_WIDTH, bn=1024, bk=s, o_dtype=BF16, name="branch_c_bwd_w"))

    dz = lax.empty((s, W_Z), BF16)
    dz, do_a, dd_a = _gate_bwd(ds_a, y_a, pzg, COL_ZA, 256, dz, COL_ZA, True, "gate_a_bwd")
    dz, do_b, dd_b = _gate_bwd(ds_b, y_b, pzg, COL_ZB, 256, dz, COL_ZB, True, "gate_b_bwd")
    dz, do_c, _ = _gate_bwd(ds_c, y_c, pzg, COL_ZC, 512, dz, COL_ZC, False, "gate_c_bwd")
    g["wm_z"] = _mm_tn(hn, dz, bm=512, bn=1024, bk=s, o_dtype=BF16, name="proj_z_bwd_w")

    names = ("wa", "wb", "wc")
    dq_a, dkv_a, dsink, got = _attn_a_bwd(qkv_a, do_a, lse_a, dd_a, sinks, slopes, "attn_a_bwd", comm=scatter(names))
    parts.update(zip(names, got))
    names = ("wm_g", "wm_z")
    dq_b, dk_b, dv_b, dc3, got = _attn_b_bwd(qkv_b, do_b, lse_b, dd_b, c3, "attn_b_bwd", comm=scatter(names))
    parts.update(zip(names, got))
    dq_c, dmkvn = _attn_c_bwd(q_c, mkvn, do_c, "attn_c_bwd")

    dqkv = lax.empty((s, W_QKV), BF16)
    dqkv, dg_qa = _headnorm_bwd(proj, COL_QA, A_WIDTH, 256, HEAD_DIM, gain_a[:, 0:768], flag_a[:, 0:768], dq_a, dqkv, COL_QA, "hn_qa_bwd")
    dqkv, dg_kva = _headnorm_bwd(proj, COL_KA, 512, 256, HEAD_DIM, gain_a[:, 768:1280], flag_a[:, 768:1280], dkv_a, dqkv, COL_KA, "hn_kva_bwd")
    dqkv, dg_qb = _headnorm_bwd(proj, COL_QB, B_WIDTH, 256, HEAD_DIM, gain_b[:, 0:768], flag_b[:, 0:768], dq_b, dqkv, COL_QB, "hn_qb_bwd")
    dqkv, dg_kb = _headnorm_bwd(proj, COL_KB, B_WIDTH, 256, HEAD_DIM, gain_b[:, 768:1536], flag_b[:, 768:1536], dk_b, dqkv, COL_KB, "hn_kb_bwd")
    dqkv, _ = _headnorm_bwd(proj, COL_VB, B_WIDTH, 256, HEAD_DIM, gain_b[:, 1536:2304], flag_b[:, 1536:2304], dv_b, dqkv, COL_VB, "hn_vb_bwd")
    dqkv, dg_qc = _headnorm_bwd(proj, COL_QC, C_WIDTH, 512, C_HEAD_DIM, gain_cq, ones(C_WIDTH), dq_c, dqkv, COL_QC, "hn_qc_bwd")
    dmkv, dg_kc = _headnorm_bwd(mkv, 0, 2 * C_WIDTH, 2 * C_WIDTH, C_HEAD_DIM, gain_ck, flag_ck, dmkvn, None, 0, "hn_kc_bwd")

    dct = jnp.pad(dc3.reshape(B_HEADS, s), ((0, 16 - B_HEADS), (0, 0)))
    dfb, dbf = _fox_prep_bwd(pfb, bpad, dct, "fox_prep_bwd")

    dmn = _mm_nt(dmkv, wg["wk"], bm=256, bn=1024, bk=1024, o_dtype=F32, name="mem_kv_bwd_act")
    g["wk"] = _mm_tn(mn, dmkv, bm=512, bn=1024, bk=mem.shape[0], o_dtype=BF16, name="mem_kv_bwd_w")
    _, dg_mem = _rmsnorm_bwd(mem, dmn, small["mem_norm_gain"], None, "rms_mem_bwd")

    g["wm_qkv"] = _mm_tn(hn, dqkv, bm=512, bn=1024, bk=s, o_dtype=BF16, name="proj_qkv_bwd_w")
    g["wf"] = _mm_tn(hn, dfb, bm=512, bn=FB_PAD, bk=s, o_dtype=BF16, name="proj_fb_bwd_w")
    half = Q_SPLIT
    g["wm_q1"], g["wm_q2"] = g["wm_qkv"][:, 0:half], g["wm_qkv"][:, half:W_QKV]
    names = ("wm_q1",)
    dhn = hosted(_mm_nt_sum([(dqkv, wg["qkv"], 0), (dfb, wg["wf"], 0)], bm=1024, bn=1024, bk=2048,
                            name="proj_qkv_bwd_act", comm=scatter(names)), names, parts)
    names = ("wm_q2", "wf", "wk")
    dhn = hosted(_mm_nt_sum([(dz, wg["zg"], COL_ZA), (dgate, wg["zg"], COL_GATE)], bm=1024, bn=1024, bk=2048,
                            name="proj_zg_bwd_act", add=dhn, comm=scatter(names)), names, parts)
    if dist:
        g = parts
    grad_x, dg_x = _rmsnorm_bwd(x, dhn, small["norm_gain"], dy, "rms_x_bwd")

    fold = lambda part, heads, hd: jnp.sum(jnp.sum(part, axis=0).reshape(heads, hd), axis=0).reshape(1, hd)
    small_grads = {
        "norm_gain": jnp.sum(dg_x, axis=0).reshape(1, d),
        "mem_norm_gain": jnp.sum(dg_mem, axis=0).reshape(1, d),
        "b_forget": dbf[0:B_HEADS, 0].reshape(1, B_HEADS),
        "q_gain_a": fold(dg_qa, A_Q_HEADS, HEAD_DIM) * scale_ab,
        "k_gain_a": fold(dg_kva[:, 0:A_KV_WIDTH], A_KV_HEADS, HEAD_DIM),
        "sinks_a": (jnp.sum(dsink, axis=(1, 2)) * (1.0 / HEAD_DIM)).reshape(1, A_Q_HEADS),
        "q_gain_b": fold(dg_qb, B_HEADS, HEAD_DIM) * scale_ab,
        "k_gain_b": fold(dg_kb, B_HEADS, HEAD_DIM),
        "q_gain_c": fold(dg_qc, C_HEADS, C_HEAD_DIM),
        "k_gain_c": fold(dg_kc[:, 0:C_WIDTH], C_HEADS, C_HEAD_DIM),
    }
    return loss, grad_x, small_grads, g


def _coords():
    return lax.axis_index("x"), lax.axis_index("y"), lax.axis_index("c")


def _all_gather(shards, name):
    n = len(shards)

    def body(*refs):
        ins = refs[0:n]
        outs = refs[n:2 * n]
        send_sems, recv_sems, local_sems = refs[2 * n:2 * n + 3]
        x, y, c = _coords()
        me, sibling = (x, y, c), (x, y, 1 - c)
        chips = [(1 - x, y), (x, 1 - y), (1 - x, 1 - y)]
        idx = lambda p: 4 * p[0] + 2 * p[1] + p[2]

        def copy(a, k, block, to, src=None):
            slot = outs[a].at[idx(block)]
            return pltpu.make_async_remote_copy(
                src_ref=slot if src is None else src, dst_ref=slot,
                send_sem=send_sems.at[a, k], recv_sem=recv_sems.at[a, k], device_id=to, device_id_type=MESH)

        mine = [pltpu.make_async_copy(ins[a], outs[a].at[idx(me)], local_sems.at[a]) for a in range(n)]
        for cp in mine:
            cp.start()
        first = []
        for a in range(n):
            first.append(copy(a, 0, me, sibling, src=ins[a]))
            first += [copy(a, 1 + j, me, (*chip, c), src=ins[a]) for j, chip in enumerate(chips)]
        for cp in first:
            cp.start()
        passed = []
        for j, chip in enumerate(chips):
            for a in range(n):
                copy(a, 1 + j, (*chip, c), me).wait_recv()
                fwd = copy(a, 4 + j, (*chip, c), sibling)
                fwd.start()
                passed.append(fwd)
        for a in range(n):
            copy(a, 0, sibling, me).wait_recv()
            for j, chip in enumerate(chips):
                copy(a, 4 + j, (*chip, 1 - c), me).wait_recv()
        for cp in first + passed:
            cp.wait_send()
        for cp in mine:
            cp.wait()

    any_spec = pl.BlockSpec(memory_space=pl.ANY)
    return pl.pallas_call(
        body, name=name,
        in_specs=[any_spec] * n, out_specs=[any_spec] * n,
        out_shape=[jax.ShapeDtypeStruct((N_DEV,) + sh.shape, sh.dtype) for sh in shards],
        scratch_shapes=[pltpu.SemaphoreType.DMA((n, 7)), pltpu.SemaphoreType.DMA((n, 7)), pltpu.SemaphoreType.DMA((n,))],
    )(*shards)


def _all_reduce_small(vec, name):
    p = vec.shape[1]

    def body(v_ref, o_ref, gather, send_sems, recv_sems):
        x, y, c = _coords()
        my = 4 * x + 2 * y + c
        peers = [(x ^ ((k >> 2) & 1), y ^ ((k >> 1) & 1), c ^ (k & 1)) for k in range(1, N_DEV)]
        gather[my] = v_ref[...]
        sends = [pltpu.make_async_remote_copy(
            src_ref=v_ref, dst_ref=gather.at[my], send_sem=send_sems.at[k], recv_sem=recv_sems.at[k],
            device_id=peer, device_id_type=MESH) for k, peer in enumerate(peers)]
        for cp in sends:
            cp.start()
        for k, peer in enumerate(peers):
            pid = 4 * peer[0] + 2 * peer[1] + peer[2]
            pltpu.make_async_remote_copy(
                src_ref=v_ref, dst_ref=gather.at[pid], send_sem=send_sems.at[k], recv_sem=recv_sems.at[k],
                device_id=peer, device_id_type=MESH).wait_recv()
        for cp in sends:
            cp.wait_send()
        total = gather[0]
        for j in range(1, N_DEV):
            total = total + gather[j]
        o_ref[...] = total

    vm = pl.BlockSpec(memory_space=pltpu.VMEM)
    return pl.pallas_call(
        body, name=name, in_specs=[vm], out_specs=vm,
        out_shape=jax.ShapeDtypeStruct((8, p), F32),
        scratch_shapes=[pltpu.VMEM((N_DEV, 8, p), F32), pltpu.SemaphoreType.DMA((7,)), pltpu.SemaphoreType.DMA((7,))],
    )(vec)[0:1]


def _sum_parts(parts, name):
    _, rows, cols = parts.shape
    br = _tile(rows, 64, 16)

    def body(p_ref, o_ref):
        total = p_ref[0].astype(F32)
        for j in range(1, N_DEV):
            total = total + p_ref[j].astype(F32)
        o_ref[...] = total

    return pl.pallas_call(
        body, name=name, grid=(rows // br,),
        in_specs=[pl.BlockSpec((N_DEV, br, cols), lambda i: (0, i, 0))],
        out_specs=pl.BlockSpec((br, cols), lambda i: (i, 0)),
        out_shape=jax.ShapeDtypeStruct((rows, cols), F32),
        compiler_params=_params(("parallel",), VMEM_BIG),
    )(parts)


def _adamw(w, g, m, v, name, br=32):
    rows, cols = w.shape
    br = min(br, rows)
    c1 = 1.0 / (1.0 - ADAM_B1 ** ADAM_STEP)
    c2 = 1.0 / (1.0 - ADAM_B2 ** ADAM_STEP)

    def body(w_ref, g_ref, m_ref, v_ref, d_ref, nm_ref, nv_ref):
        gv = g_ref[...]
        nm = ADAM_B1 * m_ref[...] + (1.0 - ADAM_B1) * gv
        nv = ADAM_B2 * v_ref[...] + (1.0 - ADAM_B2) * (gv * gv)
        d_ref[...] = -ADAM_LR * ((nm * c1) / (jnp.sqrt(nv * c2) + ADAM_EPS) + ADAM_WD * w_ref[...])
        nm_ref[...] = nm
        nv_ref[...] = nv

    spec = pl.BlockSpec((br, cols), lambda i: (i, 0))
    shape = jax.ShapeDtypeStruct((rows, cols), F32)
    return pl.pallas_call(
        body, name=name, grid=(pl.cdiv(rows, br),), in_specs=[spec] * 4, out_specs=[spec] * 3, out_shape=[shape] * 3,
        compiler_params=_params(("parallel",), VMEM_BIG),
    )(w, g, m, v)


def _adamw_t(wt, g, mt, vt, name, br=1024):
    n, r = wt.shape
    c1 = 1.0 / (1.0 - ADAM_B1 ** ADAM_STEP)
    c2 = 1.0 / (1.0 - ADAM_B2 ** ADAM_STEP)

    def body(w_ref, g_ref, m_ref, v_ref, d_ref, nm_ref, nv_ref):
        gv = g_ref[...].T
        nm = ADAM_B1 * m_ref[...] + (1.0 - ADAM_B1) * gv
        nv = ADAM_B2 * v_ref[...] + (1.0 - ADAM_B2) * (gv * gv)
        d_ref[...] = -ADAM_LR * ((nm * c1) / (jnp.sqrt(nv * c2) + ADAM_EPS) + ADAM_WD * w_ref[...])
        nm_ref[...] = nm
        nv_ref[...] = nv

    spec = pl.BlockSpec((br, r), lambda i: (i, 0))
    shape = jax.ShapeDtypeStruct((n, r), F32)
    return pl.pallas_call(
        body, name=name, grid=(pl.cdiv(n, br),),
        in_specs=[spec, pl.BlockSpec((r, br), lambda i: (0, i)), spec, spec], out_specs=[spec] * 3, out_shape=[shape] * 3,
        compiler_params=_params(("parallel",), VMEM_BIG),
    )(wt, g, mt, vt)


def _adamw_parts(w, parts, m, v, name):
    rows, cols = w.shape
    br = _tile(rows, 32, 16)
    c1 = 1.0 / (1.0 - ADAM_B1 ** ADAM_STEP)
    c2 = 1.0 / (1.0 - ADAM_B2 ** ADAM_STEP)

    def body(w_ref, p_ref, m_ref, v_ref, g_ref, d_ref, nm_ref, nv_ref):
        gv = p_ref[0].astype(F32)
        for j in range(1, N_DEV):
            gv = gv + p_ref[j].astype(F32)
        nm = ADAM_B1 * m_ref[...] + (1.0 - ADAM_B1) * gv
        nv = ADAM_B2 * v_ref[...] + (1.0 - ADAM_B2) * (gv * gv)
        g_ref[...] = gv
        d_ref[...] = -ADAM_LR * ((nm * c1) / (jnp.sqrt(nv * c2) + ADAM_EPS) + ADAM_WD * w_ref[...])
        nm_ref[...] = nm
        nv_ref[...] = nv

    spec = pl.BlockSpec((br, cols), lambda i: (i, 0))
    shape = jax.ShapeDtypeStruct((rows, cols), F32)
    return pl.pallas_call(
        body, name=name, grid=(rows // br,),
        in_specs=[spec, pl.BlockSpec((N_DEV, br, cols), lambda i: (0, i, 0)), spec, spec],
        out_specs=[spec] * 4, out_shape=[shape] * 4,
        compiler_params=_params(("parallel",), VMEM_BIG),
    )(w, parts, m, v)


SMALL_NAMES = ("norm_gain", "mem_norm_gain", "b_forget", "q_gain_a", "k_gain_a", "sinks_a",
               "q_gain_b", "k_gain_b", "q_gain_c", "k_gain_c")
BIG_NAMES = ("w_in", "w_mem_kv", "w_branch_a", "w_branch_b", "w_branch_c", "w_out")
WEIGHT_ORDER = ("norm_gain", "mem_norm_gain", "w_in", "b_forget", "q_gain_a", "k_gain_a", "sinks_a", "q_gain_b",
                "k_gain_b", "q_gain_c", "k_gain_c", "w_mem_kv", "w_branch_a", "w_branch_b", "w_branch_c", "w_out")


def _pack_small(tree):
    flat = jnp.concatenate([tree[n].reshape(1, -1) for n in SMALL_NAMES], axis=1)
    pad = (-flat.shape[1]) % LANES
    return jnp.pad(flat, ((0, 0), (0, pad)))


def _unpack_small(flat, like):
    out, off = {}, 0
    for n in SMALL_NAMES:
        size = like[n].size
        out[n] = flat[:, off:off + size].reshape(like[n].shape)
        off += size
    return out


def kernel(x, mem, norm_gain, mem_norm_gain, w_in, b_forget, q_gain_a, k_gain_a, sinks_a, q_gain_b, k_gain_b, q_gain_c, k_gain_c, w_mem_kv, w_branch_a, w_branch_b, w_branch_c, w_out, loss_target, m_norm_gain, m_mem_norm_gain, m_w_in, m_b_forget, m_q_gain_a, m_k_gain_a, m_sinks_a, m_q_gain_b, m_k_gain_b, m_q_gain_c, m_k_gain_c, m_w_mem_kv, m_w_branch_a, m_w_branch_b, m_w_branch_c, m_w_out, v_norm_gain, v_mem_norm_gain, v_w_in, v_b_forget, v_q_gain_a, v_k_gain_a, v_sinks_a, v_q_gain_b, v_k_gain_b, v_q_gain_c, v_k_gain_c, v_w_mem_kv, v_w_branch_a, v_w_branch_b, v_w_branch_c, v_w_out):
    weights = dict(norm_gain=norm_gain, mem_norm_gain=mem_norm_gain, w_in=w_in, b_forget=b_forget, q_gain_a=q_gain_a,
                   k_gain_a=k_gain_a, sinks_a=sinks_a, q_gain_b=q_gain_b, k_gain_b=k_gain_b, q_gain_c=q_gain_c,
                   k_gain_c=k_gain_c, w_mem_kv=w_mem_kv, w_branch_a=w_branch_a, w_branch_b=w_branch_b,
                   w_branch_c=w_branch_c, w_out=w_out)
    mom_m = dict(norm_gain=m_norm_gain, mem_norm_gain=m_mem_norm_gain, w_in=m_w_in, b_forget=m_b_forget,
                 q_gain_a=m_q_gain_a, k_gain_a=m_k_gain_a, sinks_a=m_sinks_a, q_gain_b=m_q_gain_b, k_gain_b=m_k_gain_b,
                 q_gain_c=m_q_gain_c, k_gain_c=m_k_gain_c, w_mem_kv=m_w_mem_kv, w_branch_a=m_w_branch_a,
                 w_branch_b=m_w_branch_b, w_branch_c=m_w_branch_c, w_out=m_w_out)
    mom_v = dict(norm_gain=v_norm_gain, mem_norm_gain=v_mem_norm_gain, w_in=v_w_in, b_forget=v_b_forget,
                 q_gain_a=v_q_gain_a, k_gain_a=v_k_gain_a, sinks_a=v_sinks_a, q_gain_b=v_q_gain_b, k_gain_b=v_k_gain_b,
                 q_gain_c=v_q_gain_c, k_gain_c=v_k_gain_c, w_mem_kv=v_w_mem_kv, w_branch_a=v_w_branch_a,
                 w_branch_b=v_w_branch_b, w_branch_c=v_w_branch_c, w_out=v_w_out)
    wi = w_in[0]
    sh_qkv = jnp.concatenate([wi[:, a:b] for a, b in SRC_RANGES[0:3]], axis=1).astype(BF16)
    sh_zg = jnp.concatenate([wi[:, a:b] for a, b in SRC_RANGES[3:6]] + [wi[:, SRC_GATE:]], axis=1).astype(BF16)
    sh_wf = jnp.pad(wi[:, FB_SRC:FB_SRC + B_HEADS], ((0, 0), (0, FB_PAD - B_HEADS))).astype(BF16)
    shards = {"zg": sh_zg, "wo": w_out[0].astype(BF16), "wa": w_branch_a[0].astype(BF16),
              "wb": w_branch_b[0].astype(BF16), "wc": w_branch_c[0].astype(BF16)}
    first = ("qkv", "wf", "wk")
    full = _all_gather([sh_qkv, sh_wf, w_mem_kv[0].astype(BF16)], "weights_all_gather")
    wg = {kname: arr.reshape(arr.shape[0] * arr.shape[1], arr.shape[2]) for kname, arr in zip(first, full)}

    small = {n: weights[n] for n in SMALL_NAMES}
    loss_local, grad_x, small_g, parts = _local_step(x[0], mem[0], loss_target[0], small, wg, shards)

    grads, delta, new_m, new_v = {}, {}, {}, {}
    for n, kname in (("w_mem_kv", "wk"), ("w_out", "wo"), ("w_branch_a", "wa"), ("w_branch_b", "wb"), ("w_branch_c", "wc")):
        gsum, dlt, nm, nv = _adamw_parts(weights[n][0], parts[kname], mom_m[n][0], mom_v[n][0], "adamw_" + n)
        grads[n], delta[n], new_m[n], new_v[n] = gsum, dlt[None], nm[None], nv[None]
    g1, g2, gz, gf, gg = (_sum_parts(parts[k], "grad_sum_" + k) for k in ("wm_q1", "wm_q2", "wm_z", "wf", "wm_g"))
    half = Q_SPLIT
    g_in = jnp.concatenate([g1[:, COL_QA:COL_QB], gz[:, COL_ZA:COL_ZB], g1[:, COL_QB:half], g2[:, 0:COL_QC - half],
                            gz[:, COL_ZB:COL_ZC], gf[:, 0:B_HEADS], g2[:, COL_QC - half:W_QKV - half], gz[:, COL_ZC:W_Z], gg], axis=1)
    dlt, nm, nv = _adamw_t(w_in[0].T, g_in, m_w_in[0].T, v_w_in[0].T, "adamw_w_in")
    grads["w_in"], delta["w_in"], new_m["w_in"], new_v["w_in"] = g_in, dlt.T[None], nm.T[None], nv.T[None]

    packed = _pack_small(small_g)
    packed = jnp.concatenate([packed[:, :-1], loss_local.reshape(1, 1)], axis=1)
    reduced = _all_reduce_small(jnp.broadcast_to(packed, (8, packed.shape[1])), "small_all_reduce")
    grads.update(_unpack_small(reduced, small))
    loss = reduced[0, -1]

    pw, pm, pv = _pack_small(small), _pack_small({n: mom_m[n] for n in SMALL_NAMES}), _pack_small({n: mom_v[n] for n in SMALL_NAMES})
    rep8 = lambda a: jnp.broadcast_to(a, (8, a.shape[1]))
    dlt, nm, nv = _adamw(rep8(pw), rep8(reduced), rep8(pm), rep8(pv), "adamw_small")
    for tree, flat in ((delta, dlt), (new_m, nm), (new_v, nv)):
        tree.update(_unpack_small(flat[0:1], small))
    for n in BIG_NAMES:
        grads[n] = grads[n][None]
    return (loss, grad_x[None], *[grads[n] for n in WEIGHT_ORDER], *[delta[n] for n in WEIGHT_ORDER],
            *[new_m[n] for n in WEIGHT_ORDER], *[new_v[n] for n in WEIGHT_ORDER])
```

```python
import math

import jax
import jax.numpy as jnp
import numpy as np
from jax import lax
from jax.experimental import pallas as pl
from jax.experimental.pallas import tpu as pltpu

F32 = jnp.float32
BF16 = jnp.bfloat16

N_DEV = 8
HEAD_DIM = 64
A_Q_HEADS = 12
A_KV_HEADS = 4
A_GROUP = 3
B_HEADS = 12
C_HEADS = 4
C_HEAD_DIM = 128
WINDOW = 128
A_WIDTH = 768
A_KV_WIDTH = 256
B_WIDTH = 768
C_WIDTH = 512
EPS = 1e-6
NEG = -1e30

COL_QA, COL_KA, COL_VA = 0, 768, 1024
COL_QB, COL_KB, COL_VB = 1280, 2048, 2816
COL_QC = 3584
W_QKV = 4096
Q_SPLIT = 1536
COL_ZA, COL_ZB, COL_ZC = 0, 768, 1536
COL_GATE = W_Z = 2048
SRC_RANGES = ((0, 1280), (2048, 4352), (5132, 5644), (1280, 2048), (4352, 5120), (5644, 6156))
SRC_GATE = 6156
FB_SRC = 5120
FB_PAD = 128

ADAM_LR = 0.001
ADAM_B1 = 0.9
ADAM_B2 = 0.999
ADAM_EPS = 1e-08
ADAM_WD = 0.01
ADAM_STEP = 10

VMEM_BIG = 52 * 1024 * 1024
LANES = 128
MESH = pl.DeviceIdType.MESH


def _tile(n, pref, mult=128):
    if n <= pref:
        return n
    t = (pref // mult) * mult
    while t >= mult:
        if n % t == 0:
            return t
        t -= mult
    return n


def _params(sem=None, vmem=None):
    kw = {}
    if sem is not None:
        kw["dimension_semantics"] = sem
    if vmem is not None:
        kw["vmem_limit_bytes"] = vmem
    return pltpu.CompilerParams(**kw)


def _sigmoid(x):
    return 1.0 / (1.0 + jnp.exp(-x))


def _block_diag(hd):
    r = np.arange(LANES)
    return jnp.asarray((r[:, None] // hd) == (r[None, :] // hd), dtype=BF16)


def _seg_sum(t, bd):
    hi = t.astype(BF16)
    lo = (t - hi.astype(F32)).astype(BF16)
    outs = []
    for c in range(t.shape[1] // LANES):
        sl = slice(c * LANES, (c + 1) * LANES)
        outs.append(jnp.dot(hi[:, sl], bd, preferred_element_type=F32) + jnp.dot(lo[:, sl], bd, preferred_element_type=F32))
    return outs[0] if len(outs) == 1 else jnp.concatenate(outs, axis=1)


def _rmsnorm_fwd(x, gain, name):
    rows, d = x.shape
    bm = _tile(rows, 512, 8)

    def body(x_ref, g_ref, o_ref):
        xv = x_ref[...]
        ms = jnp.mean(xv * xv, axis=-1, keepdims=True)
        o_ref[...] = (xv * lax.rsqrt(ms + EPS) * g_ref[...]).astype(BF16)

    return pl.pallas_call(
        body, name=name, grid=(rows // bm,),
        in_specs=[pl.BlockSpec((bm, d), lambda i: (i, 0)), pl.BlockSpec((1, d), lambda i: (0, 0))],
        out_specs=pl.BlockSpec((bm, d), lambda i: (i, 0)),
        out_shape=jax.ShapeDtypeStruct((rows, d), BF16),
        compiler_params=_params(("parallel",)),
    )(x, gain)


def _rmsnorm_bwd(x, dhn, gain, dy, name):
    rows, d = x.shape
    bm = _tile(rows, 512, 8)
    with_dx = dy is not None

    def body(*refs):
        if with_dx:
            x_ref, dh_ref, g_ref, dy_ref, gx_ref, dg_ref = refs
        else:
            x_ref, dh_ref, g_ref, dg_ref = refs
        i = pl.program_id(0)
        xv = x_ref[...]
        rstd = lax.rsqrt(jnp.mean(xv * xv, axis=-1, keepdims=True) + EPS)
        xhat = xv * rstd
        dh = dh_ref[...]
        part = jnp.sum((dh * xhat).reshape(bm // 8, 8, d), axis=0)

        @pl.when(i == 0)
        def _():
            dg_ref[...] = part

        @pl.when(i > 0)
        def _():
            dg_ref[...] += part

        if with_dx:
            g = dh * g_ref[...]
            mean = jnp.mean(g * xhat, axis=-1, keepdims=True)
            gx_ref[...] = dy_ref[...] + rstd * (g - xhat * mean)

    row_spec = pl.BlockSpec((bm, d), lambda i: (i, 0))
    in_specs = [row_spec, row_spec, pl.BlockSpec((1, d), lambda i: (0, 0))]
    args = [x, dhn, gain]
    dg_spec = pl.BlockSpec((8, d), lambda i: (0, 0))
    dg_shape = jax.ShapeDtypeStruct((8, d), F32)
    if with_dx:
        in_specs.append(row_spec)
        args.append(dy)
        out_specs = [row_spec, dg_spec]
        out_shape = [jax.ShapeDtypeStruct((rows, d), F32), dg_shape]
    else:
        out_specs = [dg_spec]
        out_shape = [dg_shape]
    outs = pl.pallas_call(
        body, name=name, grid=(rows // bm,), in_specs=in_specs, out_specs=out_specs, out_shape=out_shape,
        compiler_params=_params(("arbitrary",), VMEM_BIG),
    )(*args)
    return outs if with_dx else (None, outs[0])


class _Comm:
    def __init__(self, kind, arrays):
        self.kind = kind
        self.arrays = list(arrays)
        self.n = len(self.arrays)

    def out_shapes(self):
        if self.kind == "gather":
            return [jax.ShapeDtypeStruct((N_DEV,) + a.shape, a.dtype) for a in self.arrays]
        return [jax.ShapeDtypeStruct(a.shape, a.dtype) for a in self.arrays]

    def scratch(self):
        return [pltpu.SemaphoreType.DMA((self.n, N_DEV - 1)), pltpu.SemaphoreType.DMA((self.n, N_DEV - 1)),
                pltpu.SemaphoreType.DMA((self.n,))]

    def _plan(self, ins, outs, sems, with_recvs):
        send_sems, recv_sems, local_sems = sems
        x, y, c = lax.axis_index("x"), lax.axis_index("y"), lax.axis_index("c")
        my = 4 * x + 2 * y + c
        gather = self.kind == "gather"
        local, sends, recvs = [], [], []
        for a in range(self.n):
            local.append(pltpu.make_async_copy(ins[a] if gather else ins[a].at[my], outs[a].at[my], local_sems.at[a]))
            for k in range(1, N_DEV):
                peer = (x ^ ((k >> 2) & 1), y ^ ((k >> 1) & 1), c ^ (k & 1))
                pid = 4 * peer[0] + 2 * peer[1] + peer[2]
                src = ins[a] if gather else ins[a].at[pid]
                sem = dict(send_sem=send_sems.at[a, k - 1], recv_sem=recv_sems.at[a, k - 1], device_id=peer, device_id_type=MESH)
                sends.append(pltpu.make_async_remote_copy(src_ref=src, dst_ref=outs[a].at[my], **sem))
                if with_recvs:
                    recvs.append(pltpu.make_async_remote_copy(src_ref=src, dst_ref=outs[a].at[pid], **sem))
        return local, sends, recvs

    def start(self, ins, outs, sems):
        local, sends, _ = self._plan(ins, outs, sems, False)
        for cp in local + sends:
            cp.start()

    def wait(self, ins, outs, sems):
        local, sends, recvs = self._plan(ins, outs, sems, True)
        for cp in recvs:
            cp.wait_recv()
        for cp in sends:
            cp.wait_send()
        for cp in local:
            cp.wait()


def _grid_edges(grid):
    first = last = None
    for ax, size in enumerate(grid):
        pid = pl.program_id(ax)
        f, l = pid == 0, pid == size - 1
        first = f if first is None else first & f
        last = l if last is None else last & l
    return first, last


def _hosted_call(body, comm, *, name, grid, in_specs, out_specs, out_shape, scratch_shapes, args, sem, vmem=None):
    in_specs, out_specs, out_shape, scratch_shapes = list(in_specs), list(out_specs), list(out_shape), list(scratch_shapes)
    if comm is None:
        res = pl.pallas_call(body, name=name, grid=grid, in_specs=in_specs, out_specs=out_specs, out_shape=out_shape,
                             scratch_shapes=scratch_shapes, compiler_params=_params(sem, vmem))(*args)
        return list(res), []
    n_in, n_out, n_scr, nc = len(in_specs), len(out_shape), len(scratch_shapes), comm.n

    def hosted(*refs):
        ins = refs[0:n_in]
        comm_in = refs[n_in:n_in + nc]
        outs = refs[n_in + nc:n_in + nc + n_out]
        comm_out = refs[n_in + nc + n_out:n_in + 2 * nc + n_out]
        scr = refs[n_in + 2 * nc + n_out:n_in + 2 * nc + n_out + n_scr]
        sems = refs[n_in + 2 * nc + n_out + n_scr:]
        first, last = _grid_edges(grid)

        @pl.when(first)
        def _():
            comm.start(comm_in, comm_out, sems)

        body(*ins, *outs, *scr)

        @pl.when(last)
        def _():
            comm.wait(comm_in, comm_out, sems)

    any_spec = pl.BlockSpec(memory_space=pl.ANY)
    res = pl.pallas_call(
        hosted, name=name, grid=grid, in_specs=in_specs + [any_spec] * nc, out_specs=out_specs + [any_spec] * nc,
        out_shape=out_shape + comm.out_shapes(), scratch_shapes=scratch_shapes + comm.scratch(),
        compiler_params=_params(("arbitrary",) * len(grid), vmem),
    )(*args, *comm.arrays)
    return list(res[0:n_out]), list(res[n_out:])


def _mm(a, b, *, grid, a_spec, b_spec, o_spec, o_shape, o_dtype, contract, name, add=None, add_spec=None, acc_shape=None,
        comm=None):
    nk = grid[2]
    has_add = add is not None

    def body(*refs):
        a_ref, b_ref = refs[0], refs[1]
        add_ref = refs[2] if has_add else None
        o_ref = refs[3] if has_add else refs[2]
        part = lax.dot_general(a_ref[...], b_ref[...], (contract, ((), ())), preferred_element_type=F32)
        if nk == 1:
            if has_add:
                part = part + add_ref[...]
            o_ref[...] = part.astype(o_dtype)
        else:
            acc = refs[-1]
            k = pl.program_id(2)

            @pl.when(k == 0)
            def _():
                acc[...] = part

            @pl.when(k > 0)
            def _():
                acc[...] += part

            @pl.when(k == nk - 1)
            def _():
                r = acc[...]
                if has_add:
                    r = r + add_ref[...]
                o_ref[...] = r.astype(o_dtype)

    in_specs = [a_spec, b_spec] + ([add_spec] if has_add else [])
    args = [a, b] + ([add] if has_add else [])
    scratch = [pltpu.VMEM(acc_shape, F32)] if nk > 1 else []
    outs, comm_outs = _hosted_call(
        body, comm, name=name, grid=grid, in_specs=in_specs, out_specs=[o_spec],
        out_shape=[jax.ShapeDtypeStruct(o_shape, o_dtype)], scratch_shapes=scratch, args=args,
        sem=("parallel", "parallel", "arbitrary"), vmem=VMEM_BIG)
    return outs[0] if comm is None else (outs[0], comm_outs)


def _mm_nn(a, b, *, bm, bn, bk, o_dtype, name, add=None, comm=None):
    m, kd = a.shape
    n = b.shape[1]
    bm, bn, bk = _tile(m, bm, 8), _tile(n, bn), _tile(kd, bk)
    o_spec = pl.BlockSpec((bm, bn), lambda i, j, k: (i, j))
    return _mm(a, b, grid=(m // bm, n // bn, kd // bk),
               a_spec=pl.BlockSpec((bm, bk), lambda i, j, k: (i, k)),
               b_spec=pl.BlockSpec((bk, bn), lambda i, j, k: (k, j)),
               o_spec=o_spec, o_shape=(m, n), o_dtype=o_dtype, contract=((1,), (0,)), name=name,
               add=add, add_spec=o_spec, acc_shape=(bm, bn), comm=comm)


def _mm_nt(a, b, *, bm, bn, bk, o_dtype, name, add=None, b_col0=0, comm=None):
    m, kd = a.shape
    n = b.shape[0]
    bm, bn, bk = _tile(m, bm, 8), _tile(n, bn), _tile(math.gcd(kd, b_col0), bk)
    kb0 = b_col0 // bk
    o_spec = pl.BlockSpec((bm, bn), lambda i, j, k: (i, j))
    return _mm(a, b, grid=(m // bm, n // bn, kd // bk),
               a_spec=pl.BlockSpec((bm, bk), lambda i, j, k: (i, k)),
               b_spec=pl.BlockSpec((bn, bk), lambda i, j, k: (j, kb0 + k)),
               o_spec=o_spec, o_shape=(m, n), o_dtype=o_dtype, contract=((1,), (1,)), name=name,
               add=add, add_spec=o_spec, acc_shape=(bm, bn), comm=comm)


def _mm_nt_sum(terms, *, bm, bn, bk, name, add=None, comm=None):
    m = terms[0][0].shape[0]
    n = terms[0][1].shape[0]
    bm, bn = _tile(m, bm, 8), _tile(n, bn)
    nt = (((1,), (1,)), ((), ()))
    plan, groups, start = [], [], 0
    for a, b, col0 in terms:
        kd = a.shape[1]
        tk = _tile(math.gcd(kd, col0), bk)
        steps = kd // tk
        last = groups[-1] if groups else None
        if last is not None and last[0] is b and last[4] == tk and (last[3] + last[2]) * tk == col0:
            last[2] += steps
        else:
            groups.append([b, start, steps, col0 // tk, tk])
        plan.append((start, steps, len(groups) - 1))
        start += steps
    nk = start
    nterm, ngroup, has_add = len(terms), len(groups), add is not None

    def body(*refs):
        a_refs, b_refs = refs[0:nterm], refs[nterm:nterm + ngroup]
        add_ref = refs[nterm + ngroup] if has_add else None
        o_ref, acc = refs[nterm + ngroup + has_add], refs[nterm + ngroup + has_add + 1]
        k = pl.program_id(2)
        for t, (s0, steps, grp) in enumerate(plan):
            @pl.when((k >= s0) & (k < s0 + steps))
            def _():
                part = lax.dot_general(a_refs[t][...], b_refs[grp][...], nt, preferred_element_type=F32)

                @pl.when(k == 0)
                def _():
                    acc[...] = part

                @pl.when(k > 0)
                def _():
                    acc[...] += part

        @pl.when(k == nk - 1)
        def _():
            o_ref[...] = acc[...] + add_ref[...] if has_add else acc[...]

    def a_spec(tk, s0, steps):
        return pl.BlockSpec((bm, tk), lambda i, j, k: (i, jnp.clip(k - s0, 0, steps - 1)))

    def b_spec(tk, s0, steps, off):
        return pl.BlockSpec((bn, tk), lambda i, j, k: (j, off + jnp.clip(k - s0, 0, steps - 1)))

    o_spec = pl.BlockSpec((bm, bn), lambda i, j, k: (i, j))
    in_specs = [a_spec(groups[grp][4], s0, steps) for s0, steps, grp in plan]
    in_specs += [b_spec(tk, s0, steps, cb0) for _, s0, steps, cb0, tk in groups]
    args = [a for a, _, _ in terms] + [grp[0] for grp in groups]
    if has_add:
        in_specs.append(o_spec)
        args.append(add)
    outs, comm_outs = _hosted_call(
        body, comm, name=name, grid=(m // bm, n // bn, nk), in_specs=in_specs,
        out_specs=[o_spec], out_shape=[jax.ShapeDtypeStruct((m, n), F32)],
        scratch_shapes=[pltpu.VMEM((bm, bn), F32)], args=args,
        sem=("parallel", "parallel", "arbitrary"), vmem=VMEM_BIG)
    return outs[0] if comm is None else (outs[0], comm_outs)


def _mm_tn(a, b, *, bm, bn, bk, o_dtype, name, comm=None):
    kd, m = a.shape
    n = b.shape[1]
    bm, bn, bk = _tile(m, bm), _tile(n, bn), _tile(kd, bk, 8)
    return _mm(a, b, grid=(m // bm, n // bn, kd // bk),
               a_spec=pl.BlockSpec((bk, bm), lambda i, j, k: (k, i)),
               b_spec=pl.BlockSpec((bk, bn), lambda i, j, k: (k, j)),
               o_spec=pl.BlockSpec((bm, bn), lambda i, j, k: (i, j)),
               o_shape=(m, n), o_dtype=o_dtype, contract=((0,), (0,)), name=name, acc_shape=(bm, bn), comm=comm)


def _branch_full(w8):
    kb, ds = w8.shape[0] // N_DEV, w8.shape[1]
    return w8.reshape(N_DEV, kb, ds).transpose(1, 0, 2).reshape(kb, N_DEV * ds)


def _branch_shards(g):
    kb, ds = g.shape[0], g.shape[1] // N_DEV
    return g.reshape(kb, N_DEV, ds).transpose(1, 0, 2).reshape(N_DEV * kb, ds)


def _headnorm_fwd(src, c0, width, bw, hd, gain, nflag, head_major, name):
    rows = src.shape[0]
    bm = _tile(rows, 2048 if bw <= 256 else 1024, 16)
    bd = _block_diag(hd)
    cb0 = c0 // bw

    def body(x_ref, g_ref, f_ref, bd_ref, o_ref):
        xv = x_ref[...].astype(F32)
        ss = _seg_sum(xv * xv, bd_ref[...])
        rstd = lax.rsqrt(ss * (1.0 / hd) + EPS)
        y = (xv * jnp.where(f_ref[...] > 0.0, rstd, 1.0) * g_ref[...]).astype(BF16)
        if head_major:
            for h in range(bw // HEAD_DIM):
                o_ref[h] = y[:, h * HEAD_DIM:(h + 1) * HEAD_DIM]
        else:
            o_ref[...] = y

    vec_spec = pl.BlockSpec((1, bw), lambda i, t: (0, t))
    if head_major:
        hpb = bw // HEAD_DIM
        out_spec = pl.BlockSpec((hpb, bm, HEAD_DIM), lambda i, t: (t, i, 0))
        out_shape = jax.ShapeDtypeStruct((width // HEAD_DIM, rows, HEAD_DIM), BF16)
    else:
        out_spec = pl.BlockSpec((bm, bw), lambda i, t: (i, t))
        out_shape = jax.ShapeDtypeStruct((rows, width), BF16)
    return pl.pallas_call(
        body, name=name, grid=(rows // bm, width // bw),
        in_specs=[pl.BlockSpec((bm, bw), lambda i, t: (i, cb0 + t)), vec_spec, vec_spec,
                  pl.BlockSpec((LANES, LANES), lambda i, t: (0, 0))],
        out_specs=out_spec, out_shape=out_shape,
        compiler_params=_params(("parallel", "parallel")),
    )(src, gain, nflag, bd)


def _headnorm_bwd(src, c0, width, bw, hd, gain, nflag, dyn, target, t0, name):
    rows = src.shape[0]
    bm = _tile(rows, 2048 if bw <= 256 else 1024, 16)
    bd = _block_diag(hd)
    cb0 = c0 // bw
    tb0 = t0 // bw
    aliased = target is not None

    def body(*refs):
        if aliased:
            x_ref, dy_ref, g_ref, f_ref, bd_ref, _, o_ref, dg_ref = refs
        else:
            x_ref, dy_ref, g_ref, f_ref, bd_ref, o_ref, dg_ref = refs
        i = pl.program_id(1)
        xv = x_ref[...].astype(F32)
        dyv = dy_ref[...]
        bdv = bd_ref[...]
        rstd = lax.rsqrt(_seg_sum(xv * xv, bdv) * (1.0 / hd) + EPS)
        xhat = xv * rstd
        g = dyv * g_ref[...]
        mean = _seg_sum(g * xhat, bdv) * (1.0 / hd)
        dx = jnp.where(f_ref[...] > 0.0, rstd * (g - xhat * mean), g)
        o_ref[...] = dx.astype(BF16)
        part = jnp.sum((dyv * xhat).reshape(bm // 8, 8, bw), axis=0)

        @pl.when(i == 0)
        def _():
            dg_ref[...] = part

        @pl.when(i > 0)
        def _():
            dg_ref[...] += part

    vec_spec = pl.BlockSpec((1, bw), lambda t, i: (0, t))
    in_specs = [pl.BlockSpec((bm, bw), lambda t, i: (i, cb0 + t)), pl.BlockSpec((bm, bw), lambda t, i: (i, t)),
                vec_spec, vec_spec, pl.BlockSpec((LANES, LANES), lambda t, i: (0, 0))]
    args = [src, dyn, gain, nflag, bd]
    aliases = {}
    if aliased:
        in_specs.append(pl.BlockSpec(memory_space=pl.ANY))
        args.append(target)
        aliases = {5: 0}
        o_shape = jax.ShapeDtypeStruct(target.shape, BF16)
    else:
        o_shape = jax.ShapeDtypeStruct((rows, width), BF16)
    out, dg = pl.pallas_call(
        body, name=name, grid=(width // bw, rows // bm), in_specs=in_specs,
        out_specs=[pl.BlockSpec((bm, bw), lambda t, i: (i, tb0 + t)), pl.BlockSpec((8, bw), lambda t, i: (0, t))],
        out_shape=[o_shape, jax.ShapeDtypeStruct((8, width), F32)],
        input_output_aliases=aliases,
        compiler_params=_params(("parallel", "arbitrary")),
    )(*args)
    return out, dg


def _fox_prep(pfb, bpad, name):
    s = pfb.shape[0]

    def body(p_ref, b_ref, c_ref):
        z = p_ref[...] + b_ref[...]
        logf = jnp.minimum(z, 0.0) - jnp.log(1.0 + jnp.exp(-jnp.abs(z)))
        x = logf.T[0:16, :]
        lane = lax.broadcasted_iota(jnp.int32, (16, s), 1)
        sh = 1
        while sh < s:
            x = x + jnp.where(lane >= sh, pltpu.roll(x, sh, 1), 0.0)
            sh *= 2
        c_ref[...] = x

    return pl.pallas_call(
        body, name=name, grid=(1,),
        in_specs=[pl.BlockSpec((s, FB_PAD), lambda i: (0, 0)), pl.BlockSpec((1, FB_PAD), lambda i: (0, 0))],
        out_specs=pl.BlockSpec((16, s), lambda i: (0, 0)),
        out_shape=jax.ShapeDtypeStruct((16, s), F32),
        compiler_params=_params(("arbitrary",)),
    )(pfb, bpad)


def _fox_prep_bwd(pfb, bpad, dct, name):
    s = pfb.shape[0]

    def body(p_ref, b_ref, dc_ref, df_ref, db_ref):
        zt = (p_ref[...] + b_ref[...]).T[0:16, :]
        y = dc_ref[...]
        lane = lax.broadcasted_iota(jnp.int32, (16, s), 1)
        sh = 1
        while sh < s:
            y = y + jnp.where(lane < s - sh, pltpu.roll(y, s - sh, 1), 0.0)
            sh *= 2
        dz = y * _sigmoid(-zt)
        db_ref[...] = jnp.broadcast_to(jnp.sum(dz, axis=1, keepdims=True), (16, FB_PAD))
        full = jnp.concatenate([dz, jnp.zeros((FB_PAD - 16, s), F32)], axis=0)
        df_ref[...] = full.T.astype(BF16)

    return pl.pallas_call(
        body, name=name, grid=(1,),
        in_specs=[pl.BlockSpec((s, FB_PAD), lambda i: (0, 0)), pl.BlockSpec((1, FB_PAD), lambda i: (0, 0)),
                  pl.BlockSpec((16, s), lambda i: (0, 0))],
        out_specs=[pl.BlockSpec((s, FB_PAD), lambda i: (0, 0)), pl.BlockSpec((16, FB_PAD), lambda i: (0, 0))],
        out_shape=[jax.ShapeDtypeStruct((s, FB_PAD), BF16), jax.ShapeDtypeStruct((16, FB_PAD), F32)],
        compiler_params=_params(("arbitrary",)),
    )(pfb, bpad, dct)


def _swa_window(n):
    ws = pl.multiple_of(jnp.maximum(n * WINDOW - WINDOW, 0), WINDOW)
    qi = lax.broadcasted_iota(jnp.int32, (WINDOW, 2 * WINDOW), 0)
    kj = lax.broadcasted_iota(jnp.int32, (WINDOW, 2 * WINDOW), 1)
    rel = qi + (n * WINDOW - ws) - kj
    valid = (rel >= 0) & (rel < WINDOW)
    return ws, valid, rel.astype(F32)


def _attn_a_fwd(qkv, sinks, slopes, name):
    s = qkv.shape[1]
    nb = s // WINDOW
    smem = pl.BlockSpec(memory_space=pltpu.SMEM)

    def body(sink_ref, slope_ref, q_ref, k_ref, v_ref, o_ref, lse_ref):
        n = pl.program_id(0)
        ws, valid, relf = _swa_window(n)
        outs = []
        for h in range(A_Q_HEADS):
            kvh = h // A_GROUP
            kw = k_ref[kvh, pl.ds(ws, 2 * WINDOW), :]
            vw = v_ref[kvh, pl.ds(ws, 2 * WINDOW), :]
            sc = lax.dot_general(q_ref[h], kw, (((1,), (1,)), ((), ())), preferred_element_type=F32)
            sc = jnp.where(valid, sc - slope_ref[h] * relf, NEG)
            sink = sink_ref[h]
            m = jnp.maximum(jnp.max(sc, axis=1, keepdims=True), sink)
            p = jnp.exp(sc - m)
            denom = jnp.sum(p, axis=1, keepdims=True) + jnp.exp(sink - m)
            pn = (p / denom).astype(BF16)
            outs.append(jnp.dot(pn, vw, preferred_element_type=F32))
            lse_ref[h] = jnp.broadcast_to(m + jnp.log(denom), (WINDOW, HEAD_DIM))
        o_ref[...] = jnp.concatenate(outs, axis=1)

    return pl.pallas_call(
        body, name=name, grid=(nb,),
        in_specs=[smem, smem,
                  pl.BlockSpec((A_Q_HEADS, WINDOW, HEAD_DIM), lambda n: (0, n, 0)),
                  pl.BlockSpec((A_KV_HEADS, s, HEAD_DIM), lambda n: (A_GROUP, 0, 0)),
                  pl.BlockSpec((A_KV_HEADS, s, HEAD_DIM), lambda n: (A_GROUP + 1, 0, 0))],
        out_specs=[pl.BlockSpec((WINDOW, A_WIDTH), lambda n: (n, 0)),
                   pl.BlockSpec((A_Q_HEADS, WINDOW, HEAD_DIM), lambda n: (0, n, 0))],
        out_shape=[jax.ShapeDtypeStruct((s, A_WIDTH), F32), jax.ShapeDtypeStruct((A_Q_HEADS, s, HEAD_DIM), F32)],
        compiler_params=_params(("parallel",), VMEM_BIG),
    )(sinks, slopes, qkv, qkv, qkv)


def _attn_a_bwd(qkv, do, lse, dd, sinks, slopes, name, comm=None):
    s = qkv.shape[1]
    nb = s // WINDOW
    smem = pl.BlockSpec(memory_space=pltpu.SMEM)
    last = nb - 1

    def body(sink_ref, slope_ref, q_ref, k_ref, v_ref, do_ref, lse_ref, dd_ref, dq_ref, dkv_ref, ds_ref, carry):
        n = pl.program_id(0)

        @pl.when(n == 0)
        def _():
            carry[...] = jnp.zeros(carry.shape, F32)
            ds_ref[...] = jnp.zeros(ds_ref.shape, F32)

        @pl.when(n < nb)
        def _():
            ws, valid, relf = _swa_window(n)
            dqs = []
            dkw = [None] * A_KV_HEADS
            dvw = [None] * A_KV_HEADS
            for h in range(A_Q_HEADS):
                kvh = h // A_GROUP
                qh = q_ref[h]
                doh = do_ref[h]
                kw = k_ref[kvh, pl.ds(ws, 2 * WINDOW), :]
                vw = v_ref[kvh, pl.ds(ws, 2 * WINDOW), :]
                lse_h = lse_ref[h]
                dd_h = dd_ref[h]
                sc = lax.dot_general(qh, kw, (((1,), (1,)), ((), ())), preferred_element_type=F32)
                sc = jnp.where(valid, sc - slope_ref[h] * relf, NEG)
                p = jnp.exp(sc - lse_h[:, 0:1])
                dp = lax.dot_general(doh, vw, (((1,), (1,)), ((), ())), preferred_element_type=F32)
                dsc = (p * (dp - dd_h[:, 0:1])).astype(BF16)
                pb = p.astype(BF16)
                dqs.append(jnp.dot(dsc, kw, preferred_element_type=F32))
                dk_h = lax.dot_general(dsc, qh, (((0,), (0,)), ((), ())), preferred_element_type=F32)
                dv_h = lax.dot_general(pb, doh, (((0,), (0,)), ((), ())), preferred_element_type=F32)
                dkw[kvh] = dk_h if dkw[kvh] is None else dkw[kvh] + dk_h
                dvw[kvh] = dv_h if dvw[kvh] is None else dvw[kvh] + dv_h
                psink = jnp.exp(sink_ref[h] - lse_h)
                ds_ref[h] += jnp.sum((-psink * dd_h).reshape(WINDOW // 8, 8, HEAD_DIM), axis=0)
            dq_ref[...] = jnp.concatenate(dqs, axis=1)
            win = jnp.concatenate(dkw + dvw, axis=1)
            first = win[0:WINDOW]
            second = win[WINDOW:2 * WINDOW]
            dkv_ref[...] = carry[...] + first
            carry[...] = jnp.where(n == 0, first, second)

        @pl.when(n == nb)
        def _():
            dkv_ref[...] = carry[...]

    hm = lambda heads: pl.BlockSpec((heads, WINDOW, HEAD_DIM), lambda n: (0, jnp.minimum(n, last), 0))
    res = lambda blk: pl.BlockSpec((A_KV_HEADS, s, HEAD_DIM), lambda n: (blk, 0, 0))
    outs, comm_outs = _hosted_call(
        body, comm, name=name, grid=(nb + 1,),
        in_specs=[smem, smem, hm(A_Q_HEADS), res(A_GROUP), res(A_GROUP + 1), hm(A_Q_HEADS), hm(A_Q_HEADS), hm(A_Q_HEADS)],
        out_specs=[pl.BlockSpec((WINDOW, A_WIDTH), lambda n: (jnp.minimum(n, last), 0)),
                   pl.BlockSpec((WINDOW, 2 * A_KV_WIDTH), lambda n: (jnp.maximum(n - 1, 0), 0)),
                   pl.BlockSpec((A_Q_HEADS, 8, HEAD_DIM), lambda n: (0, 0, 0))],
        out_shape=[jax.ShapeDtypeStruct((s, A_WIDTH), F32), jax.ShapeDtypeStruct((s, 2 * A_KV_WIDTH), F32),
                   jax.ShapeDtypeStruct((A_Q_HEADS, 8, HEAD_DIM), F32)],
        scratch_shapes=[pltpu.VMEM((WINDOW, 2 * A_KV_WIDTH), F32)],
        args=[sinks, slopes, qkv, qkv, qkv, do, lse, dd], sem=("arbitrary",), vmem=VMEM_BIG)
    return outs[0], outs[1], outs[2], comm_outs


def _attn_b_fwd(qkv, c3, name, comm=None):
    heads, s = qkv.shape[0] // 3, qkv.shape[1]
    hpairs = heads // 2
    bq = min(512, s)
    nq = s // bq
    nt = (((1,), (1,)), ((), ()))

    def body(q_ref, k_ref, v_ref, c_ref, o_ref, lse_ref, m_scr, l_scr, acc_scr):
        i = pl.program_id(1)
        r0 = pl.multiple_of(i * bq, bq)
        row = lax.broadcasted_iota(jnp.int32, (bq, bq), 0)
        col = lax.broadcasted_iota(jnp.int32, (bq, bq), 1)
        m_scr[...] = jnp.full((2, bq, LANES), NEG, F32)
        l_scr[...] = jnp.zeros((2, bq, LANES), F32)
        acc_scr[...] = jnp.zeros((2, bq, HEAD_DIM), F32)

        def step(j, masked):
            k0 = pl.multiple_of(j * bq, bq)
            for h2 in range(2):
                kv = k_ref[h2, pl.ds(k0, bq), :]
                vv = v_ref[h2, pl.ds(k0, bq), :]
                cq0 = c_ref[h2, :, pl.ds(r0, LANES)][:, 0:1]
                sc = lax.dot_general(q_ref[h2], kv, nt, preferred_element_type=F32)
                sc = sc + (cq0 - c_ref[h2, :, pl.ds(k0, bq)])
                if masked:
                    sc = jnp.where(col <= row, sc, NEG)
                m_prev = m_scr[h2]
                m_new = jnp.maximum(m_prev, jnp.max(sc, axis=1, keepdims=True))
                alpha = jnp.exp(m_prev - m_new)
                p = jnp.exp(sc - m_new[:, 0:1])
                l_scr[h2] = alpha * l_scr[h2] + jnp.sum(p, axis=1, keepdims=True)
                p_hi = p.astype(BF16)
                p_lo = (p - p_hi.astype(F32)).astype(BF16)
                pv = jnp.dot(p_hi, vv, preferred_element_type=F32) + jnp.dot(p_lo, vv, preferred_element_type=F32)
                acc_scr[h2] = acc_scr[h2] * alpha[:, 0:HEAD_DIM] + pv
                m_scr[h2] = m_new

        def loop_body(j, carry):
            step(j, False)
            return carry

        lax.fori_loop(0, i, loop_body, 0)
        step(i, True)
        outs = []
        for h2 in range(2):
            l = l_scr[h2]
            outs.append(acc_scr[h2] / l[:, 0:HEAD_DIM])
            lse_ref[h2] = (m_scr[h2] + jnp.log(l))[:, 0:HEAD_DIM]
        o_ref[...] = jnp.concatenate(outs, axis=1)

    res = lambda off: pl.BlockSpec((2, s, HEAD_DIM), lambda hp, i: (off + hp, 0, 0))
    outs, comm_outs = _hosted_call(
        body, comm, name=name, grid=(hpairs, nq),
        in_specs=[pl.BlockSpec((2, bq, HEAD_DIM), lambda hp, i: (hp, i, 0)), res(hpairs), res(2 * hpairs),
                  pl.BlockSpec((2, 1, s), lambda hp, i: (hp, 0, 0))],
        out_specs=[pl.BlockSpec((bq, 2 * HEAD_DIM), lambda hp, i: (i, hp)),
                   pl.BlockSpec((2, bq, HEAD_DIM), lambda hp, i: (hp, i, 0))],
        out_shape=[jax.ShapeDtypeStruct((s, heads * HEAD_DIM), F32), jax.ShapeDtypeStruct((heads, s, HEAD_DIM), F32)],
        scratch_shapes=[pltpu.VMEM((2, bq, LANES), F32), pltpu.VMEM((2, bq, LANES), F32), pltpu.VMEM((2, bq, HEAD_DIM), F32)],
        args=[qkv, qkv, qkv, c3], sem=("parallel", "parallel"), vmem=VMEM_BIG)
    return outs[0], outs[1], comm_outs


def _attn_b_bwd(qkv, do, lse, dd, c3, name, comm=None):
    heads, s = qkv.shape[0] // 3, qkv.shape[1]
    hpairs = heads // 2
    bq = min(512, s)
    nq = s // bq
    nt = (((1,), (1,)), ((), ()))
    tn = (((0,), (0,)), ((), ()))
    grid = (heads // 2, nq)

    def body(q_ref, k_ref, v_ref, do_ref, lse_ref, dd_ref, c_ref, dq_ref, dk_ref, dv_ref, dc_ref,
             dq_scr, dk_scr, dv_scr, dc_scr):
        j = pl.program_id(1)
        k0 = pl.multiple_of(j * bq, bq)
        row = lax.broadcasted_iota(jnp.int32, (bq, bq), 0)
        col = lax.broadcasted_iota(jnp.int32, (bq, bq), 1)

        @pl.when(j == 0)
        def _():
            dq_scr[...] = jnp.zeros(dq_scr.shape, F32)

        dk_scr[...] = jnp.zeros((2, HEAD_DIM, bq), F32)
        dv_scr[...] = jnp.zeros((2, HEAD_DIM, bq), F32)
        dc_scr[...] = jnp.zeros((2, 1, bq), F32)
        k_t = [k_ref[h2].T for h2 in range(2)]

        def step(i, masked):
            r0 = pl.multiple_of(i * bq, bq)
            for h2 in range(2):
                kv = k_ref[h2]
                vv = v_ref[h2]
                qv = q_ref[h2, pl.ds(r0, bq), :]
                dov = do_ref[h2, pl.ds(r0, bq), :]
                lse_v = lse_ref[h2, pl.ds(r0, bq), :][:, 0:1]
                dd_v = dd_ref[h2, pl.ds(r0, bq), :][:, 0:1]
                cq0 = c_ref[h2, :, pl.ds(r0, LANES)][:, 0:1]
                sc = lax.dot_general(qv, kv, nt, preferred_element_type=F32) + (cq0 - c_ref[h2, :, pl.ds(k0, bq)])
                if masked:
                    sc = jnp.where(col <= row, sc, NEG)
                p = jnp.exp(sc - lse_v)
                dp = lax.dot_general(dov, vv, nt, preferred_element_type=F32)
                dsc = p * (dp - dd_v)
                dsb = dsc.astype(BF16)
                dv_scr[h2] += jnp.dot(dov.T, p.astype(BF16), preferred_element_type=F32)
                dk_scr[h2] += jnp.dot(qv.T, dsb, preferred_element_type=F32)
                dq_scr[h2, :, pl.ds(r0, bq)] += jnp.dot(k_t[h2], dsb.T, preferred_element_type=F32)
                dc_scr[h2] -= jnp.sum(dsc, axis=0, keepdims=True)

        def loop_body(i, carry):
            step(i, False)
            return carry

        step(j, True)
        lax.fori_loop(j + 1, nq, loop_body, 0)
        dc_ref[...] = dc_scr[...]
        dk_ref[...] = jnp.concatenate([dk_scr[0].T, dk_scr[1].T], axis=1)
        dv_ref[...] = jnp.concatenate([dv_scr[0].T, dv_scr[1].T], axis=1)

        @pl.when(j == nq - 1)
        def _():
            dq_ref[...] = jnp.concatenate([dq_scr[0].T, dq_scr[1].T], axis=1)

    res = pl.BlockSpec((2, s, HEAD_DIM), lambda hp, j: (hp, 0, 0))
    blk = lambda off: pl.BlockSpec((2, bq, HEAD_DIM), lambda hp, j: (off + hp, j, 0))
    tm = jax.ShapeDtypeStruct((s, heads * HEAD_DIM), F32)
    in_specs = [res, blk(hpairs), blk(2 * hpairs), res, res, res, pl.BlockSpec((2, 1, s), lambda hp, j: (hp, 0, 0))]
    out_specs = [pl.BlockSpec((s, 2 * HEAD_DIM), lambda hp, j: (0, hp)),
                 pl.BlockSpec((bq, 2 * HEAD_DIM), lambda hp, j: (j, hp)),
                 pl.BlockSpec((bq, 2 * HEAD_DIM), lambda hp, j: (j, hp)),
                 pl.BlockSpec((2, 1, bq), lambda hp, j: (hp, 0, j))]
    out_shape = [tm, tm, tm, jax.ShapeDtypeStruct((heads, 1, s), F32)]
    scratch = [pltpu.VMEM((2, HEAD_DIM, s), F32), pltpu.VMEM((2, HEAD_DIM, bq), F32),
               pltpu.VMEM((2, HEAD_DIM, bq), F32), pltpu.VMEM((2, 1, bq), F32)]
    outs, comm_outs = _hosted_call(
        body, comm, name=name, grid=grid, in_specs=in_specs, out_specs=out_specs, out_shape=out_shape,
        scratch_shapes=scratch, args=[qkv, qkv, qkv, do, lse, dd, c3], sem=("parallel", "arbitrary"), vmem=VMEM_BIG)
    return outs[0], outs[1], outs[2], outs[3], comm_outs


def _attn_c_probs(qh, mkh):
    sc = lax.dot_general(qh, mkh, (((1,), (1,)), ((), ())), preferred_element_type=F32) * (C_HEAD_DIM ** -0.5)
    p = jnp.exp(sc - jnp.max(sc, axis=1, keepdims=True))
    return p / jnp.sum(p, axis=1, keepdims=True)


def _attn_c_fwd(q, mkv, name):
    s = q.shape[0]
    m = mkv.shape[0]
    bq = _tile(s, 512, 8)

    def body(q_ref, mk_ref, mv_ref, o_ref):
        outs = []
        for h in range(C_HEADS):
            sl = slice(h * C_HEAD_DIM, (h + 1) * C_HEAD_DIM)
            pn = _attn_c_probs(q_ref[:, sl], mk_ref[:, sl]).astype(BF16)
            outs.append(jnp.dot(pn, mv_ref[:, sl], preferred_element_type=F32))
        o_ref[...] = jnp.concatenate(outs, axis=1)

    return pl.pallas_call(
        body, name=name, grid=(s // bq,),
        in_specs=[pl.BlockSpec((bq, C_WIDTH), lambda i: (i, 0)), pl.BlockSpec((m, C_WIDTH), lambda i: (0, 0)),
                  pl.BlockSpec((m, C_WIDTH), lambda i: (0, 1))],
        out_specs=pl.BlockSpec((bq, C_WIDTH), lambda i: (i, 0)),
        out_shape=jax.ShapeDtypeStruct((s, C_WIDTH), F32),
        compiler_params=_params(("parallel",)),
    )(q, mkv, mkv)


def _attn_c_bwd(q, mkv, do, name):
    s = q.shape[0]
    m = mkv.shape[0]
    bq = _tile(s, 512, 8)
    tn = (((0,), (0,)), ((), ()))

    def body(q_ref, mk_ref, mv_ref, do_ref, dq_ref, dm_ref):
        i = pl.program_id(0)

        @pl.when(i == 0)
        def _():
            dm_ref[...] = jnp.zeros(dm_ref.shape, F32)

        dqs = []
        for h in range(C_HEADS):
            sl = slice(h * C_HEAD_DIM, (h + 1) * C_HEAD_DIM)
            qh, mkh, mvh, doh = q_ref[:, sl], mk_ref[:, sl], mv_ref[:, sl], do_ref[:, sl]
            pn = _attn_c_probs(qh, mkh)
            dp = lax.dot_general(doh, mvh, (((1,), (1,)), ((), ())), preferred_element_type=F32)
            dsc = (pn * (dp - jnp.sum(pn * dp, axis=1, keepdims=True)) * (C_HEAD_DIM ** -0.5)).astype(BF16)
            dqs.append(jnp.dot(dsc, mkh, preferred_element_type=F32))
            dm_ref[:, sl] += lax.dot_general(dsc, qh, tn, preferred_element_type=F32)
            sv = slice(C_WIDTH + h * C_HEAD_DIM, C_WIDTH + (h + 1) * C_HEAD_DIM)
            dm_ref[:, sv] += lax.dot_general(pn.astype(BF16), doh, tn, preferred_element_type=F32)
        dq_ref[...] = jnp.concatenate(dqs, axis=1)

    row = pl.BlockSpec((bq, C_WIDTH), lambda i: (i, 0))
    return pl.pallas_call(
        body, name=name, grid=(s // bq,),
        in_specs=[row, pl.BlockSpec((m, C_WIDTH), lambda i: (0, 0)), pl.BlockSpec((m, C_WIDTH), lambda i: (0, 1)), row],
        out_specs=[row, pl.BlockSpec((m, 2 * C_WIDTH), lambda i: (0, 0))],
        out_shape=[jax.ShapeDtypeStruct((s, C_WIDTH), F32), jax.ShapeDtypeStruct((m, 2 * C_WIDTH), F32)],
        compiler_params=_params(("arbitrary",)),
    )(q, mkv, mkv, do)


def _gate_fwd(y, proj, zc0, bw, name):
    rows, width = y.shape
    bm = _tile(rows, 2048 if bw <= 256 else 1024, 16)
    cb0 = zc0 // bw

    def body(y_ref, z_ref, o_ref):
        z = z_ref[...].astype(F32)
        o_ref[...] = (y_ref[...] * (z * _sigmoid(z))).astype(BF16)

    return pl.pallas_call(
        body, name=name, grid=(rows // bm, width // bw),
        in_specs=[pl.BlockSpec((bm, bw), lambda i, t: (i, t)), pl.BlockSpec((bm, bw), lambda i, t: (i, cb0 + t))],
        out_specs=pl.BlockSpec((bm, bw), lambda i, t: (i, t)),
        out_shape=jax.ShapeDtypeStruct((rows, width), BF16),
        compiler_params=_params(("parallel", "parallel")),
    )(y, proj)


def _gate_bwd(dsv, y, proj, zc0, bw, dproj, t0, head_major, name):
    rows, width = y.shape
    bm = _tile(rows, 2048 if bw <= 256 else 1024, 16)
    cb0 = zc0 // bw
    tb0 = t0 // bw
    bd = _block_diag(HEAD_DIM)
    hpb = bw // HEAD_DIM

    def body(*refs):
        if head_major:
            ds_ref, y_ref, z_ref, bd_ref, _, dp_ref, dy_ref, dd_ref = refs
        else:
            ds_ref, y_ref, z_ref, _, dp_ref, dy_ref = refs
        z = z_ref[...].astype(F32)
        sig = _sigmoid(z)
        dsx = ds_ref[...]
        yv = y_ref[...]
        dy = dsx * (z * sig)
        dp_ref[...] = (dsx * yv * (sig * (1.0 + z * (1.0 - sig)))).astype(BF16)
        if head_major:
            dyb = dy.astype(BF16)
            dd = _seg_sum(dyb.astype(F32) * yv, bd_ref[...])
            for h in range(hpb):
                sl = slice(h * HEAD_DIM, (h + 1) * HEAD_DIM)
                dy_ref[h] = dyb[:, sl]
                dd_ref[h] = dd[:, sl]
        else:
            dy_ref[...] = dy.astype(BF16)

    tile = pl.BlockSpec((bm, bw), lambda i, t: (i, t))
    ztile = pl.BlockSpec((bm, bw), lambda i, t: (i, cb0 + t))
    ttile = pl.BlockSpec((bm, bw), lambda i, t: (i, tb0 + t))
    any_spec = pl.BlockSpec(memory_space=pl.ANY)
    dp_shape = jax.ShapeDtypeStruct(dproj.shape, BF16)
    if head_major:
        hm_spec = pl.BlockSpec((hpb, bm, HEAD_DIM), lambda i, t: (t, i, 0))
        nh = width // HEAD_DIM
        outs = pl.pallas_call(
            body, name=name, grid=(rows // bm, width // bw),
            in_specs=[tile, tile, ztile, pl.BlockSpec((LANES, LANES), lambda i, t: (0, 0)), any_spec],
            out_specs=[ttile, hm_spec, hm_spec],
            out_shape=[dp_shape, jax.ShapeDtypeStruct((nh, rows, HEAD_DIM), BF16),
                       jax.ShapeDtypeStruct((nh, rows, HEAD_DIM), F32)],
            input_output_aliases={4: 0},
            compiler_params=_params(("parallel", "parallel")),
        )(dsv, y, proj, bd, dproj)
        return outs[0], outs[1], outs[2]
    outs = pl.pallas_call(
        body, name=name, grid=(rows // bm, width // bw),
        in_specs=[tile, tile, ztile, any_spec],
        out_specs=[ttile, tile],
        out_shape=[dp_shape, jax.ShapeDtypeStruct((rows, width), BF16)],
        input_output_aliases={3: 0},
        compiler_params=_params(("parallel", "parallel")),
    )(dsv, y, proj, dproj)
    return outs[0], outs[1], None


def _merge_fwd(proj, ua, ub, uc, name):
    rows, d = ua.shape
    bm = _tile(rows, 1024, 16)
    bw = _tile(d, 512)
    g0 = COL_GATE // bw
    gstep = d // bw

    def body(la_ref, lb_ref, lc_ref, ua_ref, ub_ref, uc_ref, o_ref, ga_ref, gb_ref, gc_ref):
        y = None
        for l_ref, u_ref, g_ref in ((la_ref, ua_ref, ga_ref), (lb_ref, ub_ref, gb_ref), (lc_ref, uc_ref, gc_ref)):
            g = _sigmoid(l_ref[...].astype(F32))
            g_ref[...] = g.astype(BF16)
            term = g * u_ref[...].astype(F32)
            y = term if y is None else y + term
        o_ref[...] = y.astype(BF16)

    tile = pl.BlockSpec((bm, bw), lambda i, t: (i, t))
    gate = lambda b: pl.BlockSpec((bm, bw), lambda i, t: (i, g0 + b * gstep + t))
    shape = jax.ShapeDtypeStruct((rows, d), BF16)
    return pl.pallas_call(
        body, name=name, grid=(rows // bm, d // bw),
        in_specs=[gate(0), gate(1), gate(2), tile, tile, tile],
        out_specs=[tile] * 4, out_shape=[shape] * 4,
        compiler_params=_params(("parallel", "parallel")),
    )(proj, proj, proj, ua, ub, uc)


def _merge_bwd(dym, us, gs, name):
    rows, d = dym.shape
    bm = _tile(rows, 256, 16)

    def body(dy_ref, ua_ref, ub_ref, uc_ref, ga_ref, gb_ref, gc_ref, dg_ref, da_ref, db_ref, dc_ref):
        dyv = dy_ref[...]
        for b, (u_ref, g_ref, du_ref) in enumerate(((ua_ref, ga_ref, da_ref), (ub_ref, gb_ref, db_ref), (uc_ref, gc_ref, dc_ref))):
            g = g_ref[...].astype(F32)
            du_ref[...] = (g * dyv).astype(BF16)
            dg_ref[:, b * d:(b + 1) * d] = (dyv * u_ref[...].astype(F32) * g * (1.0 - g)).astype(BF16)

    tile = pl.BlockSpec((bm, d), lambda i: (i, 0))
    shape = jax.ShapeDtypeStruct((rows, d), BF16)
    outs = pl.pallas_call(
        body, name=name, grid=(rows // bm,),
        in_specs=[tile] * 7,
        out_specs=[pl.BlockSpec((bm, 3 * d), lambda i: (i, 0)), tile, tile, tile],
        out_shape=[jax.ShapeDtypeStruct((rows, 3 * d), BF16), shape, shape, shape],
        compiler_params=_params(("parallel",), VMEM_BIG),
    )(dym, *us, *gs)
    return outs[0], outs[1], outs[2], outs[3]


def _out_proj_loss(ym, wo, x, target, name):
    m, d = x.shape
    bm, bn = _tile(m, 1024, 16), _tile(d, 1024)
    grid = (m // bm, d // bn)

    def body(a_ref, b_ref, x_ref, t_ref, dy_ref, dyb_ref, l_ref):
        first, _ = _grid_edges(grid)
        y = jnp.dot(a_ref[...], b_ref[...], preferred_element_type=F32) + x_ref[...]
        diff = y - t_ref[...]
        dy = diff * (1.0 / d)
        dy_ref[...] = dy
        dyb_ref[...] = dy.astype(BF16)
        sq = diff * diff
        part = sq[:, 0:LANES]
        for c in range(1, bn // LANES):
            part = part + sq[:, c * LANES:(c + 1) * LANES]
        part = jnp.sum(part.reshape(bm // 8, 8, LANES), axis=0)

        @pl.when(first)
        def _():
            l_ref[...] = part

        @pl.when(jnp.logical_not(first))
        def _():
            l_ref[...] += part

    tile = pl.BlockSpec((bm, bn), lambda i, j: (i, j))
    return pl.pallas_call(
        body, name=name, grid=grid,
        in_specs=[pl.BlockSpec((bm, d), lambda i, j: (i, 0)), pl.BlockSpec((d, bn), lambda i, j: (0, j)), tile, tile],
        out_specs=[tile, tile, pl.BlockSpec((8, LANES), lambda i, j: (0, 0))],
        out_shape=[jax.ShapeDtypeStruct((m, d), F32), jax.ShapeDtypeStruct((m, d), BF16),
                   jax.ShapeDtypeStruct((8, LANES), F32)],
        compiler_params=_params(("arbitrary", "arbitrary"), VMEM_BIG),
    )(ym, wo, x, target)


def _row(vec, reps=1):
    return jnp.tile(vec.reshape(1, -1).astype(F32), (1, reps))


def _local_step(x, mem, target, small, wg, shards=None):
    s, d = x.shape
    dist = shards is not None
    wg = dict(wg)
    ones = lambda n: jnp.ones((1, n), F32)
    zeros = lambda n: jnp.zeros((1, n), F32)
    scale_ab = HEAD_DIM ** -0.5
    split8 = lambda g: g.reshape(N_DEV, g.shape[0] // N_DEV, g.shape[1])
    flat8 = lambda g: g.reshape(g.shape[0] * g.shape[1], g.shape[2])
    gather = lambda names: _Comm("gather", [shards[n] for n in names]) if dist else None
    g = {}

    def scatter(names):
        return _Comm("scatter", [split8(g[n]) for n in names]) if dist else None

    def hosted(result, names, store):
        if not dist:
            return result
        out, got = result
        store.update(zip(names, got))
        return out

    hn = _rmsnorm_fwd(x, small["norm_gain"], "rms_x_fwd")
    got = {}
    proj = hosted(_mm_nn(hn, wg["qkv"], bm=1024, bn=1024, bk=d, o_dtype=BF16, name="proj_qkv",
                         comm=gather(("wa", "wb", "wc"))), ("wa", "wb", "wc"), got)
    wg.update({n: flat8(a) for n, a in got.items()})
    pfb = _mm_nn(hn, wg["wf"], bm=1024, bn=FB_PAD, bk=d, o_dtype=F32, name="proj_fb")
    mn = _rmsnorm_fwd(mem, small["mem_norm_gain"], "rms_mem_fwd")
    mkv = _mm_nn(mn, wg["wk"], bm=256, bn=1024, bk=d, o_dtype=F32, name="mem_kv")

    gain_a = jnp.concatenate([_row(small["q_gain_a"], A_Q_HEADS) * scale_ab, _row(small["k_gain_a"], A_KV_HEADS), ones(A_KV_WIDTH)], axis=1)
    flag_a = jnp.concatenate([ones(A_WIDTH + A_KV_WIDTH), zeros(A_KV_WIDTH)], axis=1)
    qkv_a = _headnorm_fwd(proj, COL_QA, 1280, 1280, HEAD_DIM, gain_a, flag_a, True, "hn_a_fwd")
    gain_b = jnp.concatenate([_row(small["q_gain_b"], B_HEADS) * scale_ab, _row(small["k_gain_b"], B_HEADS), ones(B_WIDTH)], axis=1)
    flag_b = jnp.concatenate([ones(2 * B_WIDTH), zeros(B_WIDTH)], axis=1)
    qkv_b = _headnorm_fwd(proj, COL_QB, 2304, 256, HEAD_DIM, gain_b, flag_b, True, "hn_b_fwd")
    gain_cq = _row(small["q_gain_c"], C_HEADS)
    q_c = _headnorm_fwd(proj, COL_QC, C_WIDTH, C_WIDTH, C_HEAD_DIM, gain_cq, ones(C_WIDTH), False, "hn_cq_fwd")
    gain_ck = jnp.concatenate([_row(small["k_gain_c"], C_HEADS), ones(C_WIDTH)], axis=1)
    flag_ck = jnp.concatenate([ones(C_WIDTH), zeros(C_WIDTH)], axis=1)
    mkvn = _headnorm_fwd(mkv, 0, 2 * C_WIDTH, 2 * C_WIDTH, C_HEAD_DIM, gain_ck, flag_ck, False, "hn_ck_fwd")


    bpad = jnp.pad(small["b_forget"].reshape(1, -1), ((0, 0), (0, FB_PAD - B_HEADS)))
    c16 = _fox_prep(pfb, bpad, "fox_prep")
    c3 = c16[0:B_HEADS].reshape(B_HEADS, 1, s)

    sinks = small["sinks_a"].reshape(-1)
    slopes = jnp.exp2(-8.0 * jnp.arange(1, A_Q_HEADS + 1, dtype=F32) / A_Q_HEADS)
    y_a, lse_a = _attn_a_fwd(qkv_a, sinks, slopes, "attn_a_fwd")
    y_b, lse_b, got_zg = _attn_b_fwd(qkv_b, c3, "attn_b_fwd", comm=gather(("zg",)))
    if dist:
        wg["zg"] = flat8(got_zg[0])
    y_c = _attn_c_fwd(q_c, mkvn, "attn_c_fwd")

    got = {}
    pzg = hosted(_mm_nn(hn, wg["zg"], bm=1024, bn=1024, bk=d, o_dtype=BF16, name="proj_zg", comm=gather(("wo",))),
                 ("wo",), got)
    wg.update({n: flat8(a) for n, a in got.items()})

    s_a = _gate_fwd(y_a, pzg, COL_ZA, 256, "gate_a_fwd")
    s_b = _gate_fwd(y_b, pzg, COL_ZB, 256, "gate_b_fwd")
    s_c = _gate_fwd(y_c, pzg, COL_ZC, 512, "gate_c_fwd")
    w_a, w_b, w_c = _branch_full(wg["wa"]), _branch_full(wg["wb"]), _branch_full(wg["wc"])
    u_a = _mm_nn(s_a, w_a, bm=1024, bn=2048, bk=A_WIDTH, o_dtype=BF16, name="branch_a_fwd")
    u_b = _mm_nn(s_b, w_b, bm=1024, bn=2048, bk=B_WIDTH, o_dtype=BF16, name="branch_b_fwd")
    u_c = _mm_nn(s_c, w_c, bm=1024, bn=2048, bk=C_WIDTH, o_dtype=BF16, name="branch_c_fwd")
    ym, gate_a, gate_b, gate_c = _merge_fwd(pzg, u_a, u_b, u_c, "merge_fwd")
    dy, dyb, lpart = _out_proj_loss(ym, wg["wo"], x, target, "out_proj_loss")
    loss = 0.5 / d * jnp.sum(lpart)

    dym = _mm_nt(dyb, wg["wo"], bm=1024, bn=1024, bk=d, o_dtype=F32, name="out_proj_bwd_act")
    g["wo"] = _mm_tn(ym, dyb, bm=512, bn=1024, bk=s, o_dtype=BF16, name="out_proj_bwd_w")

    dgate, du_a, du_b, du_c = _merge_bwd(dym, (u_a, u_b, u_c), (gate_a, gate_b, gate_c), "merge_bwd")
    parts = {}
    g["wm_g"] = hosted(_mm_tn(hn, dgate, bm=512, bn=1024, bk=s, o_dtype=BF16, name="proj_gate_bwd_w",
                              comm=scatter(("wo",))), ("wo",), parts)

    ds_a = _mm_nt(du_a, w_a, bm=1024, bn=A_WIDTH, bk=d, o_dtype=F32, name="branch_a_bwd_act")
    ds_b = _mm_nt(du_b, w_b, bm=1024, bn=B_WIDTH, bk=d, o_dtype=F32, name="branch_b_bwd_act")
    ds_c = _mm_nt(du_c, w_c, bm=1024, bn=C_WIDTH, bk=d, o_dtype=F32, name="branch_c_bwd_act")
    g["wa"] = _branch_shards(_mm_tn(s_a, du_a, bm=A_WIDTH, bn=1024, bk=s, o_dtype=BF16, name="branch_a_bwd_w"))
    g["wb"] = _branch_shards(_mm_tn(s_b, du_b, bm=B_WIDTH, bn=1024, bk=s, o_dtype=BF16, name="branch_b_bwd_w"))
    g["wc"] = _branch_shards(_mm_tn(s_c, du_c, bm=C_WIDTH, bn=1024, bk=s, o_dtype=BF16, name="branch_c_bwd_w"))

    dz = lax.empty((s, W_Z), BF16)
    dz, do_a, dd_a = _gate_bwd(ds_a, y_a, pzg, COL_ZA, 256, dz, COL_ZA, True, "gate_a_bwd")
    dz, do_b, dd_b = _gate_bwd(ds_b, y_b, pzg, COL_ZB, 256, dz, COL_ZB, True, "gate_b_bwd")
    dz, do_c, _ = _gate_bwd(ds_c, y_c, pzg, COL_ZC, 512, dz, COL_ZC, False, "gate_c_bwd")
    g["wm_z"] = _mm_tn(hn, dz, bm=512, bn=1024, bk=s, o_dtype=BF16, name="proj_z_bwd_w")

    names = ("wa", "wb", "wc")
    dq_a, dkv_a, dsink, got = _attn_a_bwd(qkv_a, do_a, lse_a, dd_a, sinks, slopes, "attn_a_bwd", comm=scatter(names))
    parts.update(zip(names, got))
    names = ("wm_g", "wm_z")
    dq_b, dk_b, dv_b, dc3, got = _attn_b_bwd(qkv_b, do_b, lse_b, dd_b, c3, "attn_b_bwd", comm=scatter(names))
    parts.update(zip(names, got))
    dq_c, dmkvn = _attn_c_bwd(q_c, mkvn, do_c, "attn_c_bwd")

    dqkv = lax.empty((s, W_QKV), BF16)
    dqkv, dg_qa = _headnorm_bwd(proj, COL_QA, A_WIDTH, 256, HEAD_DIM, gain_a[:, 0:768], flag_a[:, 0:768], dq_a, dqkv, COL_QA, "hn_qa_bwd")
    dqkv, dg_kva = _headnorm_bwd(proj, COL_KA, 512, 256, HEAD_DIM, gain_a[:, 768:1280], flag_a[:, 768:1280], dkv_a, dqkv, COL_KA, "hn_kva_bwd")
    dqkv, dg_qb = _headnorm_bwd(proj, COL_QB, B_WIDTH, 256, HEAD_DIM, gain_b[:, 0:768], flag_b[:, 0:768], dq_b, dqkv, COL_QB, "hn_qb_bwd")
    dqkv, dg_kb = _headnorm_bwd(proj, COL_KB, B_WIDTH, 256, HEAD_DIM, gain_b[:, 768:1536], flag_b[:, 768:1536], dk_b, dqkv, COL_KB, "hn_kb_bwd")
    dqkv, _ = _headnorm_bwd(proj, COL_VB, B_WIDTH, 256, HEAD_DIM, gain_b[:, 1536:2304], flag_b[:, 1536:2304], dv_b, dqkv, COL_VB, "hn_vb_bwd")
    dqkv, dg_qc = _headnorm_bwd(proj, COL_QC, C_WIDTH, 512, C_HEAD_DIM, gain_cq, ones(C_WIDTH), dq_c, dqkv, COL_QC, "hn_qc_bwd")
    dmkv, dg_kc = _headnorm_bwd(mkv, 0, 2 * C_WIDTH, 2 * C_WIDTH, C_HEAD_DIM, gain_ck, flag_ck, dmkvn, None, 0, "hn_kc_bwd")

    dct = jnp.pad(dc3.reshape(B_HEADS, s), ((0, 16 - B_HEADS), (0, 0)))
    dfb, dbf = _fox_prep_bwd(pfb, bpad, dct, "fox_prep_bwd")

    dmn = _mm_nt(dmkv, wg["wk"], bm=256, bn=1024, bk=1024, o_dtype=F32, name="mem_kv_bwd_act")
    g["wk"] = _mm_tn(mn, dmkv, bm=512, bn=1024, bk=mem.shape[0], o_dtype=BF16, name="mem_kv_bwd_w")
    _, dg_mem = _rmsnorm_bwd(mem, dmn, small["mem_norm_gain"], None, "rms_mem_bwd")

    g["wm_qkv"] = _mm_tn(hn, dqkv, bm=512, bn=1024, bk=s, o_dtype=BF16, name="proj_qkv_bwd_w")
    g["wf"] = _mm_tn(hn, dfb, bm=512, bn=FB_PAD, bk=s, o_dtype=BF16, name="proj_fb_bwd_w")
    half = Q_SPLIT
    g["wm_q1"], g["wm_q2"] = g["wm_qkv"][:, 0:half], g["wm_qkv"][:, half:W_QKV]
    names = ("wm_q1",)
    dhn = hosted(_mm_nt_sum([(dqkv, wg["qkv"], 0), (dfb, wg["wf"], 0)], bm=1024, bn=1024, bk=2048,
                            name="proj_qkv_bwd_act", comm=scatter(names)), names, parts)
    names = ("wm_q2", "wf", "wk")
    dhn = hosted(_mm_nt_sum([(dz, wg["zg"], COL_ZA), (dgate, wg["zg"], COL_GATE)], bm=1024, bn=1024, bk=2048,
                            name="proj_zg_bwd_act", add=dhn, comm=scatter(names)), names, parts)
    if dist:
        g = parts
    grad_x, dg_x = _rmsnorm_bwd(x, dhn, small["norm_gain"], dy, "rms_x_bwd")

    fold = lambda part, heads, hd: jnp.sum(jnp.sum(part, axis=0).reshape(heads, hd), axis=0).reshape(1, hd)
    small_grads = {
        "norm_gain": jnp.sum(dg_x, axis=0).reshape(1, d),
        "mem_norm_gain": jnp.sum(dg_mem, axis=0).reshape(1, d),
        "b_forget": dbf[0:B_HEADS, 0].reshape(1, B_HEADS),
        "q_gain_a": fold(dg_qa, A_Q_HEADS, HEAD_DIM) * scale_ab,
        "k_gain_a": fold(dg_kva[:, 0:A_KV_WIDTH], A_KV_HEADS, HEAD_DIM),
        "sinks_a": (jnp.sum(dsink, axis=(1, 2)) * (1.0 / HEAD_DIM)).reshape(1, A_Q_HEADS),
        "q_gain_b": fold(dg_qb, B_HEADS, HEAD_DIM) * scale_ab,
        "k_gain_b": fold(dg_kb, B_HEADS, HEAD_DIM),
        "q_gain_c": fold(dg_qc, C_HEADS, C_HEAD_DIM),
        "k_gain_c": fold(dg_kc[:, 0:C_WIDTH], C_HEADS, C_HEAD_DIM),
    }
    return loss, grad_x, small_grads, g


def _coords():
    return lax.axis_index("x"), lax.axis_index("y"), lax.axis_index("c")


def _all_gather(shards, name):
    n = len(shards)

    def body(*refs):
        ins = refs[0:n]
        outs = refs[n:2 * n]
        send_sems, recv_sems, local_sems = refs[2 * n:2 * n + 3]
        x, y, c = _coords()
        me, sibling = (x, y, c), (x, y, 1 - c)
        chips = [(1 - x, y), (x, 1 - y), (1 - x, 1 - y)]
        idx = lambda p: 4 * p[0] + 2 * p[1] + p[2]

        def copy(a, k, block, to, src=None):
            slot = outs[a].at[idx(block)]
            return pltpu.make_async_remote_copy(
                src_ref=slot if src is None else src, dst_ref=slot,
                send_sem=send_sems.at[a, k], recv_sem=recv_sems.at[a, k], device_id=to, device_id_type=MESH)

        mine = [pltpu.make_async_copy(ins[a], outs[a].at[idx(me)], local_sems.at[a]) for a in range(n)]
        for cp in mine:
            cp.start()
        first = []
        for a in range(n):
            first.append(copy(a, 0, me, sibling, src=ins[a]))
            first += [copy(a, 1 + j, me, (*chip, c), src=ins[a]) for j, chip in enumerate(chips)]
        for cp in first:
            cp.start()
        passed = []
        for j, chip in enumerate(chips):
            for a in range(n):
                copy(a, 1 + j, (*chip, c), me).wait_recv()
                fwd = copy(a, 4 + j, (*chip, c), sibling)
                fwd.start()
                passed.append(fwd)
        for a in range(n):
            copy(a, 0, sibling, me).wait_recv()
            for j, chip in enumerate(chips):
                copy(a, 4 + j, (*chip, 1 - c), me).wait_recv()
        for cp in first + passed:
            cp.wait_send()
        for cp in mine:
            cp.wait()

    any_spec = pl.BlockSpec(memory_space=pl.ANY)
    return pl.pallas_call(
        body, name=name,
        in_specs=[any_spec] * n, out_specs=[any_spec] * n,
        out_shape=[jax.ShapeDtypeStruct((N_DEV,) + sh.shape, sh.dtype) for sh in shards],
        scratch_shapes=[pltpu.SemaphoreType.DMA((n, 7)), pltpu.SemaphoreType.DMA((n, 7)), pltpu.SemaphoreType.DMA((n,))],
    )(*shards)


def _all_reduce_small(vec, name):
    p = vec.shape[1]

    def body(v_ref, o_ref, gather, send_sems, recv_sems):
        x, y, c = _coords()
        my = 4 * x + 2 * y + c
        peers = [(x ^ ((k >> 2) & 1), y ^ ((k >> 1) & 1), c ^ (k & 1)) for k in range(1, N_DEV)]
        gather[my] = v_ref[...]
        sends = [pltpu.make_async_remote_copy(
            src_ref=v_ref, dst_ref=gather.at[my], send_sem=send_sems.at[k], recv_sem=recv_sems.at[k],
            device_id=peer, device_id_type=MESH) for k, peer in enumerate(peers)]
        for cp in sends:
            cp.start()
        for k, peer in enumerate(peers):
            pid = 4 * peer[0] + 2 * peer[1] + peer[2]
            pltpu.make_async_remote_copy(
                src_ref=v_ref, dst_ref=gather.at[pid], send_sem=send_sems.at[k], recv_sem=recv_sems.at[k],
                device_id=peer, device_id_type=MESH).wait_recv()
        for cp in sends:
            cp.wait_send()
        total = gather[0]
        for j in range(1, N_DEV):
            total = total + gather[j]
        o_ref[...] = total

    vm = pl.BlockSpec(memory_space=pltpu.VMEM)
    return pl.pallas_call(
        body, name=name, in_specs=[vm], out_specs=vm,
        out_shape=jax.ShapeDtypeStruct((8, p), F32),
        scratch_shapes=[pltpu.VMEM((N_DEV, 8, p), F32), pltpu.SemaphoreType.DMA((7,)), pltpu.SemaphoreType.DMA((7,))],
    )(vec)[0:1]


def _sum_parts(parts, name):
    _, rows, cols = parts.shape
    br = _tile(rows, 64, 16)

    def body(p_ref, o_ref):
        total = p_ref[0].astype(F32)
        for j in range(1, N_DEV):
            total = total + p_ref[j].astype(F32)
        o_ref[...] = total

    return pl.pallas_call(
        body, name=name, grid=(rows // br,),
        in_specs=[pl.BlockSpec((N_DEV, br, cols), lambda i: (0, i, 0))],
        out_specs=pl.BlockSpec((br, cols), lambda i: (i, 0)),
        out_shape=jax.ShapeDtypeStruct((rows, cols), F32),
        compiler_params=_params(("parallel",), VMEM_BIG),
    )(parts)


def _adamw(w, g, m, v, name, br=32):
    rows, cols = w.shape
    br = min(br, rows)
    c1 = 1.0 / (1.0 - ADAM_B1 ** ADAM_STEP)
    c2 = 1.0 / (1.0 - ADAM_B2 ** ADAM_STEP)

    def body(w_ref, g_ref, m_ref, v_ref, d_ref, nm_ref, nv_ref):
        gv = g_ref[...]
        nm = ADAM_B1 * m_ref[...] + (1.0 - ADAM_B1) * gv
        nv = ADAM_B2 * v_ref[...] + (1.0 - ADAM_B2) * (gv * gv)
        d_ref[...] = -ADAM_LR * ((nm * c1) / (jnp.sqrt(nv * c2) + ADAM_EPS) + ADAM_WD * w_ref[...])
        nm_ref[...] = nm
        nv_ref[...] = nv

    spec = pl.BlockSpec((br, cols), lambda i: (i, 0))
    shape = jax.ShapeDtypeStruct((rows, cols), F32)
    return pl.pallas_call(
        body, name=name, grid=(pl.cdiv(rows, br),), in_specs=[spec] * 4, out_specs=[spec] * 3, out_shape=[shape] * 3,
        compiler_params=_params(("parallel",), VMEM_BIG),
    )(w, g, m, v)


def _adamw_t(wt, g, mt, vt, name, br=1024):
    n, r = wt.shape
    c1 = 1.0 / (1.0 - ADAM_B1 ** ADAM_STEP)
    c2 = 1.0 / (1.0 - ADAM_B2 ** ADAM_STEP)

    def body(w_ref, g_ref, m_ref, v_ref, d_ref, nm_ref, nv_ref):
        gv = g_ref[...].T
        nm = ADAM_B1 * m_ref[...] + (1.0 - ADAM_B1) * gv
        nv = ADAM_B2 * v_ref[...] + (1.0 - ADAM_B2) * (gv * gv)
        d_ref[...] = -ADAM_LR * ((nm * c1) / (jnp.sqrt(nv * c2) + ADAM_EPS) + ADAM_WD * w_ref[...])
        nm_ref[...] = nm
        nv_ref[...] = nv

    spec = pl.BlockSpec((br, r), lambda i: (i, 0))
    shape = jax.ShapeDtypeStruct((n, r), F32)
    return pl.pallas_call(
        body, name=name, grid=(pl.cdiv(n, br),),
        in_specs=[spec, pl.BlockSpec((r, br), lambda i: (0, i)), spec, spec], out_specs=[spec] * 3, out_shape=[shape] * 3,
        compiler_params=_params(("parallel",), VMEM_BIG),
    )(wt, g, mt, vt)


def _adamw_parts(w, parts, m, v, name):
    rows, cols = w.shape
    br = _tile(rows, 32, 16)
    c1 = 1.0 / (1.0 - ADAM_B1 ** ADAM_STEP)
    c2 = 1.0 / (1.0 - ADAM_B2 ** ADAM_STEP)

    def body(w_ref, p_ref, m_ref, v_ref, g_ref, d_ref, nm_ref, nv_ref):
        gv = p_ref[0].astype(F32)
        for j in range(1, N_DEV):
            gv = gv + p_ref[j].astype(F32)
        nm = ADAM_B1 * m_ref[...] + (1.0 - ADAM_B1) * gv
        nv = ADAM_B2 * v_ref[...] + (1.0 - ADAM_B2) * (gv * gv)
        g_ref[...] = gv
        d_ref[...] = -ADAM_LR * ((nm * c1) / (jnp.sqrt(nv * c2) + ADAM_EPS) + ADAM_WD * w_ref[...])
        nm_ref[...] = nm
        nv_ref[...] = nv

    spec = pl.BlockSpec((br, cols), lambda i: (i, 0))
    shape = jax.ShapeDtypeStruct((rows, cols), F32)
    return pl.pallas_call(
        body, name=name, grid=(rows // br,),
        in_specs=[spec, pl.BlockSpec((N_DEV, br, cols), lambda i: (0, i, 0)), spec, spec],
        out_specs=[spec] * 4, out_shape=[shape] * 4,
        compiler_params=_params(("parallel",), VMEM_BIG),
    )(w, parts, m, v)


SMALL_NAMES = ("norm_gain", "mem_norm_gain", "b_forget", "q_gain_a", "k_gain_a", "sinks_a",
               "q_gain_b", "k_gain_b", "q_gain_c", "k_gain_c")
BIG_NAMES = ("w_in", "w_mem_kv", "w_branch_a", "w_branch_b", "w_branch_c", "w_out")
WEIGHT_ORDER = ("norm_gain", "mem_norm_gain", "w_in", "b_forget", "q_gain_a", "k_gain_a", "sinks_a", "q_gain_b",
                "k_gain_b", "q_gain_c", "k_gain_c", "w_mem_kv", "w_branch_a", "w_branch_b", "w_branch_c", "w_out")


def _pack_small(tree):
    flat = jnp.concatenate([tree[n].reshape(1, -1) for n in SMALL_NAMES], axis=1)
    pad = (-flat.shape[1]) % LANES
    return jnp.pad(flat, ((0, 0), (0, pad)))


def _unpack_small(flat, like):
    out, off = {}, 0
    for n in SMALL_NAMES:
        size = like[n].size
        out[n] = flat[:, off:off + size].reshape(like[n].shape)
        off += size
    return out


def kernel(x, mem, norm_gain, mem_norm_gain, w_in, b_forget, q_gain_a, k_gain_a, sinks_a, q_gain_b, k_gain_b, q_gain_c, k_gain_c, w_mem_kv, w_branch_a, w_branch_b, w_branch_c, w_out, loss_target, m_norm_gain, m_mem_norm_gain, m_w_in, m_b_forget, m_q_gain_a, m_k_gain_a, m_sinks_a, m_q_gain_b, m_k_gain_b, m_q_gain_c, m_k_gain_c, m_w_mem_kv, m_w_branch_a, m_w_branch_b, m_w_branch_c, m_w_out, v_norm_gain, v_mem_norm_gain, v_w_in, v_b_forget, v_q_gain_a, v_k_gain_a, v_sinks_a, v_q_gain_b, v_k_gain_b, v_q_gain_c, v_k_gain_c, v_w_mem_kv, v_w_branch_a, v_w_branch_b, v_w_branch_c, v_w_out):
    weights = dict(norm_gain=norm_gain, mem_norm_gain=mem_norm_gain, w_in=w_in, b_forget=b_forget, q_gain_a=q_gain_a,
                   k_gain_a=k_gain_a, sinks_a=sinks_a, q_gain_b=q_gain_b, k_gain_b=k_gain_b, q_gain_c=q_gain_c,
                   k_gain_c=k_gain_c, w_mem_kv=w_mem_kv, w_branch_a=w_branch_a, w_branch_b=w_branch_b,
                   w_branch_c=w_branch_c, w_out=w_out)
    mom_m = dict(norm_gain=m_norm_gain, mem_norm_gain=m_mem_norm_gain, w_in=m_w_in, b_forget=m_b_forget,
                 q_gain_a=m_q_gain_a, k_gain_a=m_k_gain_a, sinks_a=m_sinks_a, q_gain_b=m_q_gain_b, k_gain_b=m_k_gain_b,
                 q_gain_c=m_q_gain_c, k_gain_c=m_k_gain_c, w_mem_kv=m_w_mem_kv, w_branch_a=m_w_branch_a,
                 w_branch_b=m_w_branch_b, w_branch_c=m_w_branch_c, w_out=m_w_out)
    mom_v = dict(norm_gain=v_norm_gain, mem_norm_gain=v_mem_norm_gain, w_in=v_w_in, b_forget=v_b_forget,
                 q_gain_a=v_q_gain_a, k_gain_a=v_k_gain_a, sinks_a=v_sinks_a, q_gain_b=v_q_gain_b, k_gain_b=v_k_gain_b,
                 q_gain_c=v_q_gain_c, k_gain_c=v_k_gain_c, w_mem_kv=v_w_mem_kv, w_branch_a=v_w_branch_a,
                 w_branch_b=v_w_branch_b, w_branch_c=v_w_branch_c, w_out=v_w_out)
    wi = w_in[0]
    sh_qkv = jnp.concatenate([wi[:, a:b] for a, b in SRC_RANGES[0:3]], axis=1).astype(BF16)
    sh_zg = jnp.concatenate([wi[:, a:b] for a, b in SRC_RANGES[3:6]] + [wi[:, SRC_GATE:]], axis=1).astype(BF16)
    sh_wf = jnp.pad(wi[:, FB_SRC:FB_SRC + B_HEADS], ((0, 0), (0, FB_PAD - B_HEADS))).astype(BF16)
    shards = {"zg": sh_zg, "wo": w_out[0].astype(BF16), "wa": w_branch_a[0].astype(BF16),
              "wb": w_branch_b[0].astype(BF16), "wc": w_branch_c[0].astype(BF16)}
    first = ("qkv", "wf", "wk")
    full = _all_gather([sh_qkv, sh_wf, w_mem_kv[0].astype(BF16)], "weights_all_gather")
    wg = {kname: arr.reshape(arr.shape[0] * arr.shape[1], arr.shape[2]) for kname, arr in zip(first, full)}

    small = {n: weights[n] for n in SMALL_NAMES}
    loss_local, grad_x, small_g, parts = _local_step(x[0], mem[0], loss_target[0], small, wg, shards)

    grads, delta, new_m, new_v = {}, {}, {}, {}
    for n, kname in (("w_mem_kv", "wk"), ("w_out", "wo"), ("w_branch_a", "wa"), ("w_branch_b", "wb"), ("w_branch_c", "wc")):
        gsum, dlt, nm, nv = _adamw_parts(weights[n][0], parts[kname], mom_m[n][0], mom_v[n][0], "adamw_" + n)
        grads[n], delta[n], new_m[n], new_v[n] = gsum, dlt[None], nm[None], nv[None]
    g1, g2, gz, gf, gg = (_sum_parts(parts[k], "grad_sum_" + k) for k in ("wm_q1", "wm_q2", "wm_z", "wf", "wm_g"))
    half = Q_SPLIT
    g_in = jnp.concatenate([g1[:, COL_QA:COL_QB], gz[:, COL_ZA:COL_ZB], g1[:, COL_QB:half], g2[:, 0:COL_QC - half],
                            gz[:, COL_ZB:COL_ZC], gf[:, 0:B_HEADS], g2[:, COL_QC - half:W_QKV - half], gz[:, COL_ZC:W_Z], gg], axis=1)
    dlt, nm, nv = _adamw_t(w_in[0].T, g_in, m_w_in[0].T, v_w_in[0].T, "adamw_w_in")
    grads["w_in"], delta["w_in"], new_m["w_in"], new_v["w_in"] = g_in, dlt.T[None], nm.T[None], nv.T[None]

    packed = _pack_small(small_g)
    packed = jnp.concatenate([packed[:, :-1], loss_local.reshape(1, 1)], axis=1)
    reduced = _all_reduce_small(jnp.broadcast_to(packed, (8, packed.shape[1])), "small_all_reduce")
    grads.update(_unpack_small(reduced, small))
    loss = reduced[0, -1]

    pw, pm, pv = _pack_small(small), _pack_small({n: mom_m[n] for n in SMALL_NAMES}), _pack_small({n: mom_v[n] for n in SMALL_NAMES})
    rep8 = lambda a: jnp.broadcast_to(a, (8, a.shape[1]))
    dlt, nm, nv = _adamw(rep8(pw), rep8(reduced), rep8(pm), rep8(pv), "adamw_small")
    for tree, flat in ((delta, dlt), (new_m, nm), (new_v, nv)):
        tree.update(_unpack_small(flat[0:1], small))
    for n in BIG_NAMES:
        grads[n] = grads[n][None]
    return (loss, grad_x[None], *[grads[n] for n in WEIGHT_ORDER], *[delta[n] for n in WEIGHT_ORDER],
            *[new_m[n] for n in WEIGHT_ORDER], *[new_v[n] for n in WEIGHT_ORDER])
```

```python
import math

import jax
import jax.numpy as jnp
import numpy as np
from jax import lax
from jax.experimental import pallas as pl
from jax.experimental.pallas import tpu as pltpu

F32 = jnp.float32
BF16 = jnp.bfloat16

N_DEV = 8
HEAD_DIM = 64
A_Q_HEADS = 12
A_KV_HEADS = 4
A_GROUP = 3
B_HEADS = 12
C_HEADS = 4
C_HEAD_DIM = 128
WINDOW = 128
A_WIDTH = 768
A_KV_WIDTH = 256
B_WIDTH = 768
C_WIDTH = 512
EPS = 1e-6
NEG = -1e30

COL_QA, COL_KA, COL_VA = 0, 768, 1024
COL_QB, COL_KB, COL_VB = 1280, 2048, 2816
COL_QC = 3584
W_QKV = 4096
Q_SPLIT = 1536
COL_ZA, COL_ZB, COL_ZC = 0, 768, 1536
COL_GATE = W_Z = 2048
SRC_RANGES = ((0, 1280), (2048, 4352), (5132, 5644), (1280, 2048), (4352, 5120), (5644, 6156))
SRC_GATE = 6156
FB_SRC = 5120
FB_PAD = 128

ADAM_LR = 0.001
ADAM_B1 = 0.9
ADAM_B2 = 0.999
ADAM_EPS = 1e-08
ADAM_WD = 0.01
ADAM_STEP = 10

VMEM_BIG = 52 * 1024 * 1024
LANES = 128
MESH = pl.DeviceIdType.MESH


def _tile(n, pref, mult=128):
    if n <= pref:
        return n
    t = (pref // mult) * mult
    while t >= mult:
        if n % t == 0:
            return t
        t -= mult
    return n


def _params(sem=None, vmem=None):
    kw = {}
    if sem is not None:
        kw["dimension_semantics"] = sem
    if vmem is not None:
        kw["vmem_limit_bytes"] = vmem
    return pltpu.CompilerParams(**kw)


def _sigmoid(x):
    return 1.0 / (1.0 + jnp.exp(-x))


def _block_diag(hd):
    r = np.arange(LANES)
    return jnp.asarray((r[:, None] // hd) == (r[None, :] // hd), dtype=BF16)


def _seg_sum(t, bd):
    hi = t.astype(BF16)
    lo = (t - hi.astype(F32)).astype(BF16)
    outs = []
    for c in range(t.shape[1] // LANES):
        sl = slice(c * LANES, (c + 1) * LANES)
        outs.append(jnp.dot(hi[:, sl], bd, preferred_element_type=F32) + jnp.dot(lo[:, sl], bd, preferred_element_type=F32))
    return outs[0] if len(outs) == 1 else jnp.concatenate(outs, axis=1)


def _rmsnorm_fwd(x, gain, name):
    rows, d = x.shape
    bm = _tile(rows, 512, 8)

    def body(x_ref, g_ref, o_ref):
        xv = x_ref[...]
        ms = jnp.mean(xv * xv, axis=-1, keepdims=True)
        o_ref[...] = (xv * lax.rsqrt(ms + EPS) * g_ref[...]).astype(BF16)

    return pl.pallas_call(
        body, name=name, grid=(rows // bm,),
        in_specs=[pl.BlockSpec((bm, d), lambda i: (i, 0)), pl.BlockSpec((1, d), lambda i: (0, 0))],
        out_specs=pl.BlockSpec((bm, d), lambda i: (i, 0)),
        out_shape=jax.ShapeDtypeStruct((rows, d), BF16),
        compiler_params=_params(("parallel",)),
    )(x, gain)


def _rmsnorm_bwd(x, dhn, gain, dy, name):
    rows, d = x.shape
    bm = _tile(rows, 512, 8)
    with_dx = dy is not None

    def body(*refs):
        if with_dx:
            x_ref, dh_ref, g_ref, dy_ref, gx_ref, dg_ref = refs
        else:
            x_ref, dh_ref, g_ref, dg_ref = refs
        i = pl.program_id(0)
        xv = x_ref[...]
        rstd = lax.rsqrt(jnp.mean(xv * xv, axis=-1, keepdims=True) + EPS)
        xhat = xv * rstd
        dh = dh_ref[...]
        part = jnp.sum((dh * xhat).reshape(bm // 8, 8, d), axis=0)

        @pl.when(i == 0)
        def _():
            dg_ref[...] = part

        @pl.when(i > 0)
        def _():
            dg_ref[...] += part

        if with_dx:
            g = dh * g_ref[...]
            mean = jnp.mean(g * xhat, axis=-1, keepdims=True)
            gx_ref[...] = dy_ref[...] + rstd * (g - xhat * mean)

    row_spec = pl.BlockSpec((bm, d), lambda i: (i, 0))
    in_specs = [row_spec, row_spec, pl.BlockSpec((1, d), lambda i: (0, 0))]
    args = [x, dhn, gain]
    dg_spec = pl.BlockSpec((8, d), lambda i: (0, 0))
    dg_shape = jax.ShapeDtypeStruct((8, d), F32)
    if with_dx:
        in_specs.append(row_spec)
        args.append(dy)
        out_specs = [row_spec, dg_spec]
        out_shape = [jax.ShapeDtypeStruct((rows, d), F32), dg_shape]
    else:
        out_specs = [dg_spec]
        out_shape = [dg_shape]
    outs = pl.pallas_call(
        body, name=name, grid=(rows // bm,), in_specs=in_specs, out_specs=out_specs, out_shape=out_shape,
        compiler_params=_params(("arbitrary",), VMEM_BIG),
    )(*args)
    return outs if with_dx else (None, outs[0])


class _Comm:
    def __init__(self, kind, arrays):
        self.kind = kind
        self.arrays = list(arrays)
        self.n = len(self.arrays)

    def out_shapes(self):
        if self.kind == "gather":
            return [jax.ShapeDtypeStruct((N_DEV,) + a.shape, a.dtype) for a in self.arrays]
        return [jax.ShapeDtypeStruct(a.shape, a.dtype) for a in self.arrays]

    def scratch(self):
        return [pltpu.SemaphoreType.DMA((self.n, N_DEV - 1)), pltpu.SemaphoreType.DMA((self.n, N_DEV - 1)),
                pltpu.SemaphoreType.DMA((self.n,))]

    def _plan(self, ins, outs, sems, with_recvs):
        send_sems, recv_sems, local_sems = sems
        x, y, c = lax.axis_index("x"), lax.axis_index("y"), lax.axis_index("c")
        my = 4 * x + 2 * y + c
        gather = self.kind == "gather"
        local, sends, recvs = [], [], []
        for a in range(self.n):
            local.append(pltpu.make_async_copy(ins[a] if gather else ins[a].at[my], outs[a].at[my], local_sems.at[a]))
            for k in range(1, N_DEV):
                peer = (x ^ ((k >> 2) & 1), y ^ ((k >> 1) & 1), c ^ (k & 1))
                pid = 4 * peer[0] + 2 * peer[1] + peer[2]
                src = ins[a] if gather else ins[a].at[pid]
                sem = dict(send_sem=send_sems.at[a, k - 1], recv_sem=recv_sems.at[a, k - 1], device_id=peer, device_id_type=MESH)
                sends.append(pltpu.make_async_remote_copy(src_ref=src, dst_ref=outs[a].at[my], **sem))
                if with_recvs:
                    recvs.append(pltpu.make_async_remote_copy(src_ref=src, dst_ref=outs[a].at[pid], **sem))
        return local, sends, recvs

    def start(self, ins, outs, sems):
        local, sends, _ = self._plan(ins, outs, sems, False)
        for cp in local + sends:
            cp.start()

    def wait(self, ins, outs, sems):
        local, sends, recvs = self._plan(ins, outs, sems, True)
        for cp in recvs:
            cp.wait_recv()
        for cp in sends:
            cp.wait_send()
        for cp in local:
            cp.wait()


def _grid_edges(grid):
    first = last = None
    for ax, size in enumerate(grid):
        pid = pl.program_id(ax)
        f, l = pid == 0, pid == size - 1
        first = f if first is None else first & f
        last = l if last is None else last & l
    return first, last


def _hosted_call(body, comm, *, name, grid, in_specs, out_specs, out_shape, scratch_shapes, args, sem, vmem=None):
    in_specs, out_specs, out_shape, scratch_shapes = list(in_specs), list(out_specs), list(out_shape), list(scratch_shapes)
    if comm is None:
        res = pl.pallas_call(body, name=name, grid=grid, in_specs=in_specs, out_specs=out_specs, out_shape=out_shape,
                             scratch_shapes=scratch_shapes, compiler_params=_params(sem, vmem))(*args)
        return list(res), []
    n_in, n_out, n_scr, nc = len(in_specs), len(out_shape), len(scratch_shapes), comm.n

    def hosted(*refs):
        ins = refs[0:n_in]
        comm_in = refs[n_in:n_in + nc]
        outs = refs[n_in + nc:n_in + nc + n_out]
        comm_out = refs[n_in + nc + n_out:n_in + 2 * nc + n_out]
        scr = refs[n_in + 2 * nc + n_out:n_in + 2 * nc + n_out + n_scr]
        sems = refs[n_in + 2 * nc + n_out + n_scr:]
        first, last = _grid_edges(grid)

        @pl.when(first)
        def _():
            comm.start(comm_in, comm_out, sems)

        body(*ins, *outs, *scr)

        @pl.when(last)
        def _():
            comm.wait(comm_in, comm_out, sems)

    any_spec = pl.BlockSpec(memory_space=pl.ANY)
    res = pl.pallas_call(
        hosted, name=name, grid=grid, in_specs=in_specs + [any_spec] * nc, out_specs=out_specs + [any_spec] * nc,
        out_shape=out_shape + comm.out_shapes(), scratch_shapes=scratch_shapes + comm.scratch(),
        compiler_params=_params(("arbitrary",) * len(grid), vmem),
    )(*args, *comm.arrays)
    return list(res[0:n_out]), list(res[n_out:])


def _mm(a, b, *, grid, a_spec, b_spec, o_spec, o_shape, o_dtype, contract, name, add=None, add_spec=None, acc_shape=None,
        comm=None):
    nk = grid[2]
    has_add = add is not None

    def body(*refs):
        a_ref, b_ref = refs[0], refs[1]
        add_ref = refs[2] if has_add else None
        o_ref = refs[3] if has_add else refs[2]
        part = lax.dot_general(a_ref[...], b_ref[...], (contract, ((), ())), preferred_element_type=F32)
        if nk == 1:
            if has_add:
                part = part + add_ref[...]
            o_ref[...] = part.astype(o_dtype)
        else:
            acc = refs[-1]
            k = pl.program_id(2)

            @pl.when(k == 0)
            def _():
                acc[...] = part

            @pl.when(k > 0)
            def _():
                acc[...] += part

            @pl.when(k == nk - 1)
            def _():
                r = acc[...]
                if has_add:
                    r = r + add_ref[...]
                o_ref[...] = r.astype(o_dtype)

    in_specs = [a_spec, b_spec] + ([add_spec] if has_add else [])
    args = [a, b] + ([add] if has_add else [])
    scratch = [pltpu.VMEM(acc_shape, F32)] if nk > 1 else []
    outs, comm_outs = _hosted_call(
        body, comm, name=name, grid=grid, in_specs=in_specs, out_specs=[o_spec],
        out_shape=[jax.ShapeDtypeStruct(o_shape, o_dtype)], scratch_shapes=scratch, args=args,
        sem=("parallel", "parallel", "arbitrary"), vmem=VMEM_BIG)
    return outs[0] if comm is None else (outs[0], comm_outs)


def _mm_nn(a, b, *, bm, bn, bk, o_dtype, name, add=None, comm=None):
    m, kd = a.shape
    n = b.shape[1]
    bm, bn, bk = _tile(m, bm, 8), _tile(n, bn), _tile(kd, bk)
    o_spec = pl.BlockSpec((bm, bn), lambda i, j, k: (i, j))
    return _mm(a, b, grid=(m // bm, n // bn, kd // bk),
               a_spec=pl.BlockSpec((bm, bk), lambda i, j, k: (i, k)),
               b_spec=pl.BlockSpec((bk, bn), lambda i, j, k: (k, j)),
               o_spec=o_spec, o_shape=(m, n), o_dtype=o_dtype, contract=((1,), (0,)), name=name,
               add=add, add_spec=o_spec, acc_shape=(bm, bn), comm=comm)


def _mm_nt(a, b, *, bm, bn, bk, o_dtype, name, add=None, b_col0=0, comm=None):
    m, kd = a.shape
    n = b.shape[0]
    bm, bn, bk = _tile(m, bm, 8), _tile(n, bn), _tile(math.gcd(kd, b_col0), bk)
    kb0 = b_col0 // bk
    o_spec = pl.BlockSpec((bm, bn), lambda i, j, k: (i, j))
    return _mm(a, b, grid=(m // bm, n // bn, kd // bk),
               a_spec=pl.BlockSpec((bm, bk), lambda i, j, k: (i, k)),
               b_spec=pl.BlockSpec((bn, bk), lambda i, j, k: (j, kb0 + k)),
               o_spec=o_spec, o_shape=(m, n), o_dtype=o_dtype, contract=((1,), (1,)), name=name,
               add=add, add_spec=o_spec, acc_shape=(bm, bn), comm=comm)


def _mm_nt_sum(terms, *, bm, bn, bk, name, add=None, comm=None):
    m = terms[0][0].shape[0]
    n = terms[0][1].shape[0]
    bm, bn = _tile(m, bm, 8), _tile(n, bn)
    nt = (((1,), (1,)), ((), ()))
    plan, groups, start = [], [], 0
    for a, b, col0 in terms:
        kd = a.shape[1]
        tk = _tile(math.gcd(kd, col0), bk)
        steps = kd // tk
        last = groups[-1] if groups else None
        if last is not None and last[0] is b and last[4] == tk and (last[3] + last[2]) * tk == col0:
            last[2] += steps
        else:
            groups.append([b, start, steps, col0 // tk, tk])
        plan.append((start, steps, len(groups) - 1))
        start += steps
    nk = start
    nterm, ngroup, has_add = len(terms), len(groups), add is not None

    def body(*refs):
        a_refs, b_refs = refs[0:nterm], refs[nterm:nterm + ngroup]
        add_ref = refs[nterm + ngroup] if has_add else None
        o_ref, acc = refs[nterm + ngroup + has_add], refs[nterm + ngroup + has_add + 1]
        k = pl.program_id(2)
        for t, (s0, steps, grp) in enumerate(plan):
            @pl.when((k >= s0) & (k < s0 + steps))
            def _():
                part = lax.dot_general(a_refs[t][...], b_refs[grp][...], nt, preferred_element_type=F32)

                @pl.when(k == 0)
                def _():
                    acc[...] = part

                @pl.when(k > 0)
                def _():
                    acc[...] += part

        @pl.when(k == nk - 1)
        def _():
            o_ref[...] = acc[...] + add_ref[...] if has_add else acc[...]

    def a_spec(tk, s0, steps):
        return pl.BlockSpec((bm, tk), lambda i, j, k: (i, jnp.clip(k - s0, 0, steps - 1)))

    def b_spec(tk, s0, steps, off):
        return pl.BlockSpec((bn, tk), lambda i, j, k: (j, off + jnp.clip(k - s0, 0, steps - 1)))

    o_spec = pl.BlockSpec((bm, bn), lambda i, j, k: (i, j))
    in_specs = [a_spec(groups[grp][4], s0, steps) for s0, steps, grp in plan]
    in_specs += [b_spec(tk, s0, steps, cb0) for _, s0, steps, cb0, tk in groups]
    args = [a for a, _, _ in terms] + [grp[0] for grp in groups]
    if has_add:
        in_specs.append(o_spec)
        args.append(add)
    outs, comm_outs = _hosted_call(
        body, comm, name=name, grid=(m // bm, n // bn, nk), in_specs=in_specs,
        out_specs=[o_spec], out_shape=[jax.ShapeDtypeStruct((m, n), F32)],
        scratch_shapes=[pltpu.VMEM((bm, bn), F32)], args=args,
        sem=("parallel", "parallel", "arbitrary"), vmem=VMEM_BIG)
    return outs[0] if comm is None else (outs[0], comm_outs)


def _mm_tn(a, b, *, bm, bn, bk, o_dtype, name, comm=None):
    kd, m = a.shape
    n = b.shape[1]
    bm, bn, bk = _tile(m, bm), _tile(n, bn), _tile(kd, bk, 8)
    return _mm(a, b, grid=(m // bm, n // bn, kd // bk),
               a_spec=pl.BlockSpec((bk, bm), lambda i, j, k: (k, i)),
               b_spec=pl.BlockSpec((bk, bn), lambda i, j, k: (k, j)),
               o_spec=pl.BlockSpec((bm, bn), lambda i, j, k: (i, j)),
               o_shape=(m, n), o_dtype=o_dtype, contract=((0,), (0,)), name=name, acc_shape=(bm, bn), comm=comm)


def _branch_full(w8):
    kb, ds = w8.shape[0] // N_DEV, w8.shape[1]
    return w8.reshape(N_DEV, kb, ds).transpose(1, 0, 2).reshape(kb, N_DEV * ds)


def _branch_shards(g):
    kb, ds = g.shape[0], g.shape[1] // N_DEV
    return g.reshape(kb, N_DEV, ds).transpose(1, 0, 2).reshape(N_DEV * kb, ds)


def _headnorm_fwd(src, c0, width, bw, hd, gain, nflag, head_major, name):
    rows = src.shape[0]
    bm = _tile(rows, 2048 if bw <= 256 else 1024, 16)
    bd = _block_diag(hd)
    cb0 = c0 // bw

    def body(x_ref, g_ref, f_ref, bd_ref, o_ref):
        xv = x_ref[...].astype(F32)
        ss = _seg_sum(xv * xv, bd_ref[...])
        rstd = lax.rsqrt(ss * (1.0 / hd) + EPS)
        y = (xv * jnp.where(f_ref[...] > 0.0, rstd, 1.0) * g_ref[...]).astype(BF16)
        if head_major:
            for h in range(bw // HEAD_DIM):
                o_ref[h] = y[:, h * HEAD_DIM:(h + 1) * HEAD_DIM]
        else:
            o_ref[...] = y

    vec_spec = pl.BlockSpec((1, bw), lambda i, t: (0, t))
    if head_major:
        hpb = bw // HEAD_DIM
        out_spec = pl.BlockSpec((hpb, bm, HEAD_DIM), lambda i, t: (t, i, 0))
        out_shape = jax.ShapeDtypeStruct((width // HEAD_DIM, rows, HEAD_DIM), BF16)
    else:
        out_spec = pl.BlockSpec((bm, bw), lambda i, t: (i, t))
        out_shape = jax.ShapeDtypeStruct((rows, width), BF16)
    return pl.pallas_call(
        body, name=name, grid=(rows // bm, width // bw),
        in_specs=[pl.BlockSpec((bm, bw), lambda i, t: (i, cb0 + t)), vec_spec, vec_spec,
                  pl.BlockSpec((LANES, LANES), lambda i, t: (0, 0))],
        out_specs=out_spec, out_shape=out_shape,
        compiler_params=_params(("parallel", "parallel")),
    )(src, gain, nflag, bd)


def _headnorm_bwd(src, c0, width, bw, hd, gain, nflag, dyn, target, t0, name):
    rows = src.shape[0]
    bm = _tile(rows, 2048 if bw <= 256 else 1024, 16)
    bd = _block_diag(hd)
    cb0 = c0 // bw
    tb0 = t0 // bw
    aliased = target is not None

    def body(*refs):
        if aliased:
            x_ref, dy_ref, g_ref, f_ref, bd_ref, _, o_ref, dg_ref = refs
        else:
            x_ref, dy_ref, g_ref, f_ref, bd_ref, o_ref, dg_ref = refs
        i = pl.program_id(1)
        xv = x_ref[...].astype(F32)
        dyv = dy_ref[...]
        bdv = bd_ref[...]
        rstd = lax.rsqrt(_seg_sum(xv * xv, bdv) * (1.0 / hd) + EPS)
        xhat = xv * rstd
        g = dyv * g_ref[...]
        mean = _seg_sum(g * xhat, bdv) * (1.0 / hd)
        dx = jnp.where(f_ref[...] > 0.0, rstd * (g - xhat * mean), g)
        o_ref[...] = dx.astype(BF16)
        part = jnp.sum((dyv * xhat).reshape(bm // 8, 8, bw), axis=0)

        @pl.when(i == 0)
        def _():
            dg_ref[...] = part

        @pl.when(i > 0)
        def _():
            dg_ref[...] += part

    vec_spec = pl.BlockSpec((1, bw), lambda t, i: (0, t))
    in_specs = [pl.BlockSpec((bm, bw), lambda t, i: (i, cb0 + t)), pl.BlockSpec((bm, bw), lambda t, i: (i, t)),
                vec_spec, vec_spec, pl.BlockSpec((LANES, LANES), lambda t, i: (0, 0))]
    args = [src, dyn, gain, nflag, bd]
    aliases = {}
    if aliased:
        in_specs.append(pl.BlockSpec(memory_space=pl.ANY))
        args.append(target)
        aliases = {5: 0}
        o_shape = jax.ShapeDtypeStruct(target.shape, BF16)
    else:
        o_shape = jax.ShapeDtypeStruct((rows, width), BF16)
    out, dg = pl.pallas_call(
        body, name=name, grid=(width // bw, rows // bm), in_specs=in_specs,
        out_specs=[pl.BlockSpec((bm, bw), lambda t, i: (i, tb0 + t)), pl.BlockSpec((8, bw), lambda t, i: (0, t))],
        out_shape=[o_shape, jax.ShapeDtypeStruct((8, width), F32)],
        input_output_aliases=aliases,
        compiler_params=_params(("parallel", "arbitrary")),
    )(*args)
    return out, dg


def _fox_prep(pfb, bpad, name):
    s = pfb.shape[0]

    def body(p_ref, b_ref, c_ref):
        z = p_ref[...] + b_ref[...]
        logf = jnp.minimum(z, 0.0) - jnp.log(1.0 + jnp.exp(-jnp.abs(z)))
        x = logf.T[0:16, :]
        lane = lax.broadcasted_iota(jnp.int32, (16, s), 1)
        sh = 1
        while sh < s:
            x = x + jnp.where(lane >= sh, pltpu.roll(x, sh, 1), 0.0)
            sh *= 2
        c_ref[...] = x

    return pl.pallas_call(
        body, name=name, grid=(1,),
        in_specs=[pl.BlockSpec((s, FB_PAD), lambda i: (0, 0)), pl.BlockSpec((1, FB_PAD), lambda i: (0, 0))],
        out_specs=pl.BlockSpec((16, s), lambda i: (0, 0)),
        out_shape=jax.ShapeDtypeStruct((16, s), F32),
        compiler_params=_params(("arbitrary",)),
    )(pfb, bpad)


def _fox_prep_bwd(pfb, bpad, dct, name):
    s = pfb.shape[0]

    def body(p_ref, b_ref, dc_ref, df_ref, db_ref):
        zt = (p_ref[...] + b_ref[...]).T[0:16, :]
        y = dc_ref[...]
        lane = lax.broadcasted_iota(jnp.int32, (16, s), 1)
        sh = 1
        while sh < s:
            y = y + jnp.where(lane < s - sh, pltpu.roll(y, s - sh, 1), 0.0)
            sh *= 2
        dz = y * _sigmoid(-zt)
        db_ref[...] = jnp.broadcast_to(jnp.sum(dz, axis=1, keepdims=True), (16, FB_PAD))
        full = jnp.concatenate([dz, jnp.zeros((FB_PAD - 16, s), F32)], axis=0)
        df_ref[...] = full.T.astype(BF16)

    return pl.pallas_call(
        body, name=name, grid=(1,),
        in_specs=[pl.BlockSpec((s, FB_PAD), lambda i: (0, 0)), pl.BlockSpec((1, FB_PAD), lambda i: (0, 0)),
                  pl.BlockSpec((16, s), lambda i: (0, 0))],
        out_specs=[pl.BlockSpec((s, FB_PAD), lambda i: (0, 0)), pl.BlockSpec((16, FB_PAD), lambda i: (0, 0))],
        out_shape=[jax.ShapeDtypeStruct((s, FB_PAD), BF16), jax.ShapeDtypeStruct((16, FB_PAD), F32)],
        compiler_params=_params(("arbitrary",)),
    )(pfb, bpad, dct)


def _swa_window(n):
    ws = pl.multiple_of(jnp.maximum(n * WINDOW - WINDOW, 0), WINDOW)
    qi = lax.broadcasted_iota(jnp.int32, (WINDOW, 2 * WINDOW), 0)
    kj = lax.broadcasted_iota(jnp.int32, (WINDOW, 2 * WINDOW), 1)
    rel = qi + (n * WINDOW - ws) - kj
    valid = (rel >= 0) & (rel < WINDOW)
    return ws, valid, rel.astype(F32)


def _attn_a_fwd(qkv, sinks, slopes, name):
    s = qkv.shape[1]
    nb = s // WINDOW
    smem = pl.BlockSpec(memory_space=pltpu.SMEM)

    def body(sink_ref, slope_ref, q_ref, k_ref, v_ref, o_ref, lse_ref):
        n = pl.program_id(0)
        ws, valid, relf = _swa_window(n)
        outs = []
        for h in range(A_Q_HEADS):
            kvh = h // A_GROUP
            kw = k_ref[kvh, pl.ds(ws, 2 * WINDOW), :]
            vw = v_ref[kvh, pl.ds(ws, 2 * WINDOW), :]
            sc = lax.dot_general(q_ref[h], kw, (((1,), (1,)), ((), ())), preferred_element_type=F32)
            sc = jnp.where(valid, sc - slope_ref[h] * relf, NEG)
            sink = sink_ref[h]
            m = jnp.maximum(jnp.max(sc, axis=1, keepdims=True), sink)
            p = jnp.exp(sc - m)
            denom = jnp.sum(p, axis=1, keepdims=True) + jnp.exp(sink - m)
            pn = (p / denom).astype(BF16)
            outs.append(jnp.dot(pn, vw, preferred_element_type=F32))
            lse_ref[h] = jnp.broadcast_to(m + jnp.log(denom), (WINDOW, HEAD_DIM))
        o_ref[...] = jnp.concatenate(outs, axis=1)

    return pl.pallas_call(
        body, name=name, grid=(nb,),
        in_specs=[smem, smem,
                  pl.BlockSpec((A_Q_HEADS, WINDOW, HEAD_DIM), lambda n: (0, n, 0)),
                  pl.BlockSpec((A_KV_HEADS, s, HEAD_DIM), lambda n: (A_GROUP, 0, 0)),
                  pl.BlockSpec((A_KV_HEADS, s, HEAD_DIM), lambda n: (A_GROUP + 1, 0, 0))],
        out_specs=[pl.BlockSpec((WINDOW, A_WIDTH), lambda n: (n, 0)),
                   pl.BlockSpec((A_Q_HEADS, WINDOW, HEAD_DIM), lambda n: (0, n, 0))],
        out_shape=[jax.ShapeDtypeStruct((s, A_WIDTH), F32), jax.ShapeDtypeStruct((A_Q_HEADS, s, HEAD_DIM), F32)],
        compiler_params=_params(("parallel",), VMEM_BIG),
    )(sinks, slopes, qkv, qkv, qkv)


def _attn_a_bwd(qkv, do, lse, dd, sinks, slopes, name, comm=None):
    s = qkv.shape[1]
    nb = s // WINDOW
    smem = pl.BlockSpec(memory_space=pltpu.SMEM)
    last = nb - 1

    def body(sink_ref, slope_ref, q_ref, k_ref, v_ref, do_ref, lse_ref, dd_ref, dq_ref, dkv_ref, ds_ref, carry):
        n = pl.program_id(0)

        @pl.when(n == 0)
        def _():
            carry[...] = jnp.zeros(carry.shape, F32)
            ds_ref[...] = jnp.zeros(ds_ref.shape, F32)

        @pl.when(n < nb)
        def _():
            ws, valid, relf = _swa_window(n)
            dqs = []
            dkw = [None] * A_KV_HEADS
            dvw = [None] * A_KV_HEADS
            for h in range(A_Q_HEADS):
                kvh = h // A_GROUP
                qh = q_ref[h]
                doh = do_ref[h]
                kw = k_ref[kvh, pl.ds(ws, 2 * WINDOW), :]
                vw = v_ref[kvh, pl.ds(ws, 2 * WINDOW), :]
                lse_h = lse_ref[h]
                dd_h = dd_ref[h]
                sc = lax.dot_general(qh, kw, (((1,), (1,)), ((), ())), preferred_element_type=F32)
                sc = jnp.where(valid, sc - slope_ref[h] * relf, NEG)
                p = jnp.exp(sc - lse_h[:, 0:1])
                dp = lax.dot_general(doh, vw, (((1,), (1,)), ((), ())), preferred_element_type=F32)
                dsc = (p * (dp - dd_h[:, 0:1])).astype(BF16)
                pb = p.astype(BF16)
                dqs.append(jnp.dot(dsc, kw, preferred_element_type=F32))
                dk_h = lax.dot_general(dsc, qh, (((0,), (0,)), ((), ())), preferred_element_type=F32)
                dv_h = lax.dot_general(pb, doh, (((0,), (0,)), ((), ())), preferred_element_type=F32)
                dkw[kvh] = dk_h if dkw[kvh] is None else dkw[kvh] + dk_h
                dvw[kvh] = dv_h if dvw[kvh] is None else dvw[kvh] + dv_h
                psink = jnp.exp(sink_ref[h] - lse_h)
                ds_ref[h] += jnp.sum((-psink * dd_h).reshape(WINDOW // 8, 8, HEAD_DIM), axis=0)
            dq_ref[...] = jnp.concatenate(dqs, axis=1)
            win = jnp.concatenate(dkw + dvw, axis=1)
            first = win[0:WINDOW]
            second = win[WINDOW:2 * WINDOW]
            dkv_ref[...] = carry[...] + first
            carry[...] = jnp.where(n == 0, first, second)

        @pl.when(n == nb)
        def _():
            dkv_ref[...] = carry[...]

    hm = lambda heads: pl.BlockSpec((heads, WINDOW, HEAD_DIM), lambda n: (0, jnp.minimum(n, last), 0))
    res = lambda blk: pl.BlockSpec((A_KV_HEADS, s, HEAD_DIM), lambda n: (blk, 0, 0))
    outs, comm_outs = _hosted_call(
        body, comm, name=name, grid=(nb + 1,),
        in_specs=[smem, smem, hm(A_Q_HEADS), res(A_GROUP), res(A_GROUP + 1), hm(A_Q_HEADS), hm(A_Q_HEADS), hm(A_Q_HEADS)],
        out_specs=[pl.BlockSpec((WINDOW, A_WIDTH), lambda n: (jnp.minimum(n, last), 0)),
                   pl.BlockSpec((WINDOW, 2 * A_KV_WIDTH), lambda n: (jnp.maximum(n - 1, 0), 0)),
                   pl.BlockSpec((A_Q_HEADS, 8, HEAD_DIM), lambda n: (0, 0, 0))],
        out_shape=[jax.ShapeDtypeStruct((s, A_WIDTH), F32), jax.ShapeDtypeStruct((s, 2 * A_KV_WIDTH), F32),
                   jax.ShapeDtypeStruct((A_Q_HEADS, 8, HEAD_DIM), F32)],
        scratch_shapes=[pltpu.VMEM((WINDOW, 2 * A_KV_WIDTH), F32)],
        args=[sinks, slopes, qkv, qkv, qkv, do, lse, dd], sem=("arbitrary",), vmem=VMEM_BIG)
    return outs[0], outs[1], outs[2], comm_outs


def _attn_b_fwd(qkv, c3, name, comm=None):
    heads, s = qkv.shape[0] // 3, qkv.shape[1]
    hpairs = heads // 2
    bq = min(512, s)
    nq = s // bq
    nt = (((1,), (1,)), ((), ()))

    def body(q_ref, k_ref, v_ref, c_ref, o_ref, lse_ref, m_scr, l_scr, acc_scr):
        i = pl.program_id(1)
        r0 = pl.multiple_of(i * bq, bq)
        row = lax.broadcasted_iota(jnp.int32, (bq, bq), 0)
        col = lax.broadcasted_iota(jnp.int32, (bq, bq), 1)
        m_scr[...] = jnp.full((2, bq, LANES), NEG, F32)
        l_scr[...] = jnp.zeros((2, bq, LANES), F32)
        acc_scr[...] = jnp.zeros((2, bq, HEAD_DIM), F32)

        def step(j, masked):
            k0 = pl.multiple_of(j * bq, bq)
            for h2 in range(2):
                kv = k_ref[h2, pl.ds(k0, bq), :]
                vv = v_ref[h2, pl.ds(k0, bq), :]
                cq0 = c_ref[h2, :, pl.ds(r0, LANES)][:, 0:1]
                sc = lax.dot_general(q_ref[h2], kv, nt, preferred_element_type=F32)
                sc = sc + (cq0 - c_ref[h2, :, pl.ds(k0, bq)])
                if masked:
                    sc = jnp.where(col <= row, sc, NEG)
                m_prev = m_scr[h2]
                m_new = jnp.maximum(m_prev, jnp.max(sc, axis=1, keepdims=True))
                alpha = jnp.exp(m_prev - m_new)
                p = jnp.exp(sc - m_new[:, 0:1])
                l_scr[h2] = alpha * l_scr[h2] + jnp.sum(p, axis=1, keepdims=True)
                p_hi = p.astype(BF16)
                p_lo = (p - p_hi.astype(F32)).astype(BF16)
                pv = jnp.dot(p_hi, vv, preferred_element_type=F32) + jnp.dot(p_lo, vv, preferred_element_type=F32)
                acc_scr[h2] = acc_scr[h2] * alpha[:, 0:HEAD_DIM] + pv
                m_scr[h2] = m_new

        def loop_body(j, carry):
            step(j, False)
            return carry

        lax.fori_loop(0, i, loop_body, 0)
        step(i, True)
        outs = []
        for h2 in range(2):
            l = l_scr[h2]
            outs.append(acc_scr[h2] / l[:, 0:HEAD_DIM])
            lse_ref[h2] = (m_scr[h2] + jnp.log(l))[:, 0:HEAD_DIM]
        o_ref[...] = jnp.concatenate(outs, axis=1)

    res = lambda off: pl.BlockSpec((2, s, HEAD_DIM), lambda hp, i: (off + hp, 0, 0))
    outs, comm_outs = _hosted_call(
        body, comm, name=name, grid=(hpairs, nq),
        in_specs=[pl.BlockSpec((2, bq, HEAD_DIM), lambda hp, i: (hp, i, 0)), res(hpairs), res(2 * hpairs),
                  pl.BlockSpec((2, 1, s), lambda hp, i: (hp, 0, 0))],
        out_specs=[pl.BlockSpec((bq, 2 * HEAD_DIM), lambda hp, i: (i, hp)),
                   pl.BlockSpec((2, bq, HEAD_DIM), lambda hp, i: (hp, i, 0))],
        out_shape=[jax.ShapeDtypeStruct((s, heads * HEAD_DIM), F32), jax.ShapeDtypeStruct((heads, s, HEAD_DIM), F32)],
        scratch_shapes=[pltpu.VMEM((2, bq, LANES), F32), pltpu.VMEM((2, bq, LANES), F32), pltpu.VMEM((2, bq, HEAD_DIM), F32)],
        args=[qkv, qkv, qkv, c3], sem=("parallel", "parallel"), vmem=VMEM_BIG)
    return outs[0], outs[1], comm_outs


def _attn_b_bwd(qkv, do, lse, dd, c3, name, comm=None):
    heads, s = qkv.shape[0] // 3, qkv.shape[1]
    hpairs = heads // 2
    bq = min(512, s)
    nq = s // bq
    nt = (((1,), (1,)), ((), ()))
    tn = (((0,), (0,)), ((), ()))
    grid = (heads // 2, nq)

    def body(q_ref, k_ref, v_ref, do_ref, lse_ref, dd_ref, c_ref, dq_ref, dk_ref, dv_ref, dc_ref,
             dq_scr, dk_scr, dv_scr, dc_scr):
        j = pl.program_id(1)
        k0 = pl.multiple_of(j * bq, bq)
        row = lax.broadcasted_iota(jnp.int32, (bq, bq), 0)
        col = lax.broadcasted_iota(jnp.int32, (bq, bq), 1)

        @pl.when(j == 0)
        def _():
            dq_scr[...] = jnp.zeros(dq_scr.shape, F32)

        dk_scr[...] = jnp.zeros((2, HEAD_DIM, bq), F32)
        dv_scr[...] = jnp.zeros((2, HEAD_DIM, bq), F32)
        dc_scr[...] = jnp.zeros((2, 1, bq), F32)
        k_t = [k_ref[h2].T for h2 in range(2)]

        def step(i, masked):
            r0 = pl.multiple_of(i * bq, bq)
            for h2 in range(2):
                kv = k_ref[h2]
                vv = v_ref[h2]
                qv = q_ref[h2, pl.ds(r0, bq), :]
                dov = do_ref[h2, pl.ds(r0, bq), :]
                lse_v = lse_ref[h2, pl.ds(r0, bq), :][:, 0:1]
                dd_v = dd_ref[h2, pl.ds(r0, bq), :][:, 0:1]
                cq0 = c_ref[h2, :, pl.ds(r0, LANES)][:, 0:1]
                sc = lax.dot_general(qv, kv, nt, preferred_element_type=F32) + (cq0 - c_ref[h2, :, pl.ds(k0, bq)])
                if masked:
                    sc = jnp.where(col <= row, sc, NEG)
                p = jnp.exp(sc - lse_v)
                dp = lax.dot_general(dov, vv, nt, preferred_element_type=F32)
                dsc = p * (dp - dd_v)
                dsb = dsc.astype(BF16)
                dv_scr[h2] += jnp.dot(dov.T, p.astype(BF16), preferred_element_type=F32)
                dk_scr[h2] += jnp.dot(qv.T, dsb, preferred_element_type=F32)
                dq_scr[h2, :, pl.ds(r0, bq)] += jnp.dot(k_t[h2], dsb.T, preferred_element_type=F32)
                dc_scr[h2] -= jnp.sum(dsc, axis=0, keepdims=True)

        def loop_body(i, carry):
            step(i, False)
            return carry

        step(j, True)
        lax.fori_loop(j + 1, nq, loop_body, 0)
        dc_ref[...] = dc_scr[...]
        dk_ref[...] = jnp.concatenate([dk_scr[0].T, dk_scr[1].T], axis=1)
        dv_ref[...] = jnp.concatenate([dv_scr[0].T, dv_scr[1].T], axis=1)

        @pl.when(j == nq - 1)
        def _():
            dq_ref[...] = jnp.concatenate([dq_scr[0].T, dq_scr[1].T], axis=1)

    res = pl.BlockSpec((2, s, HEAD_DIM), lambda hp, j: (hp, 0, 0))
    blk = lambda off: pl.BlockSpec((2, bq, HEAD_DIM), lambda hp, j: (off + hp, j, 0))
    tm = jax.ShapeDtypeStruct((s, heads * HEAD_DIM), F32)
    in_specs = [res, blk(hpairs), blk(2 * hpairs), res, res, res, pl.BlockSpec((2, 1, s), lambda hp, j: (hp, 0, 0))]
    out_specs = [pl.BlockSpec((s, 2 * HEAD_DIM), lambda hp, j: (0, hp)),
                 pl.BlockSpec((bq, 2 * HEAD_DIM), lambda hp, j: (j, hp)),
                 pl.BlockSpec((bq, 2 * HEAD_DIM), lambda hp, j: (j, hp)),
                 pl.BlockSpec((2, 1, bq), lambda hp, j: (hp, 0, j))]
    out_shape = [tm, tm, tm, jax.ShapeDtypeStruct((heads, 1, s), F32)]
    scratch = [pltpu.VMEM((2, HEAD_DIM, s), F32), pltpu.VMEM((2, HEAD_DIM, bq), F32),
               pltpu.VMEM((2, HEAD_DIM, bq), F32), pltpu.VMEM((2, 1, bq), F32)]
    outs, comm_outs = _hosted_call(
        body, comm, name=name, grid=grid, in_specs=in_specs, out_specs=out_specs, out_shape=out_shape,
        scratch_shapes=scratch, args=[qkv, qkv, qkv, do, lse, dd, c3], sem=("parallel", "arbitrary"), vmem=VMEM_BIG)
    return outs[0], outs[1], outs[2], outs[3], comm_outs


def _attn_c_probs(qh, mkh):
    sc = lax.dot_general(qh, mkh, (((1,), (1,)), ((), ())), preferred_element_type=F32) * (C_HEAD_DIM ** -0.5)
    p = jnp.exp(sc - jnp.max(sc, axis=1, keepdims=True))
    return p / jnp.sum(p, axis=1, keepdims=True)


def _attn_c_fwd(q, mkv, name):
    s = q.shape[0]
    m = mkv.shape[0]
    bq = _tile(s, 512, 8)

    def body(q_ref, mk_ref, mv_ref, o_ref):
        outs = []
        for h in range(C_HEADS):
            sl = slice(h * C_HEAD_DIM, (h + 1) * C_HEAD_DIM)
            pn = _attn_c_probs(q_ref[:, sl], mk_ref[:, sl]).astype(BF16)
            outs.append(jnp.dot(pn, mv_ref[:, sl], preferred_element_type=F32))
        o_ref[...] = jnp.concatenate(outs, axis=1)

    return pl.pallas_call(
        body, name=name, grid=(s // bq,),
        in_specs=[pl.BlockSpec((bq, C_WIDTH), lambda i: (i, 0)), pl.BlockSpec((m, C_WIDTH), lambda i: (0, 0)),
                  pl.BlockSpec((m, C_WIDTH), lambda i: (0, 1))],
        out_specs=pl.BlockSpec((bq, C_WIDTH), lambda i: (i, 0)),
        out_shape=jax.ShapeDtypeStruct((s, C_WIDTH), F32),
        compiler_params=_params(("parallel",)),
    )(q, mkv, mkv)


def _attn_c_bwd(q, mkv, do, name):
    s = q.shape[0]
    m = mkv.shape[0]
    bq = _tile(s, 512, 8)
    tn = (((0,), (0,)), ((), ()))

    def body(q_ref, mk_ref, mv_ref, do_ref, dq_ref, dm_ref):
        i = pl.program_id(0)

        @pl.when(i == 0)
        def _():
            dm_ref[...] = jnp.zeros(dm_ref.shape, F32)

        dqs = []
        for h in range(C_HEADS):
            sl = slice(h * C_HEAD_DIM, (h + 1) * C_HEAD_DIM)
            qh, mkh, mvh, doh = q_ref[:, sl], mk_ref[:, sl], mv_ref[:, sl], do_ref[:, sl]
            pn = _attn_c_probs(qh, mkh)
            dp = lax.dot_general(doh, mvh, (((1,), (1,)), ((), ())), preferred_element_type=F32)
            dsc = (pn * (dp - jnp.sum(pn * dp, axis=1, keepdims=True)) * (C_HEAD_DIM ** -0.5)).astype(BF16)
            dqs.append(jnp.dot(dsc, mkh, preferred_element_type=F32))
            dm_ref[:, sl] += lax.dot_general(dsc, qh, tn, preferred_element_type=F32)
            sv = slice(C_WIDTH + h * C_HEAD_DIM, C_WIDTH + (h + 1) * C_HEAD_DIM)
            dm_ref[:, sv] += lax.dot_general(pn.astype(BF16), doh, tn, preferred_element_type=F32)
        dq_ref[...] = jnp.concatenate(dqs, axis=1)

    row = pl.BlockSpec((bq, C_WIDTH), lambda i: (i, 0))
    return pl.pallas_call(
        body, name=name, grid=(s // bq,),
        in_specs=[row, pl.BlockSpec((m, C_WIDTH), lambda i: (0, 0)), pl.BlockSpec((m, C_WIDTH), lambda i: (0, 1)), row],
        out_specs=[row, pl.BlockSpec((m, 2 * C_WIDTH), lambda i: (0, 0))],
        out_shape=[jax.ShapeDtypeStruct((s, C_WIDTH), F32), jax.ShapeDtypeStruct((m, 2 * C_WIDTH), F32)],
        compiler_params=_params(("arbitrary",)),
    )(q, mkv, mkv, do)


def _gate_fwd(y, proj, zc0, bw, name):
    rows, width = y.shape
    bm = _tile(rows, 2048 if bw <= 256 else 1024, 16)
    cb0 = zc0 // bw

    def body(y_ref, z_ref, o_ref):
        z = z_ref[...].astype(F32)
        o_ref[...] = (y_ref[...] * (z * _sigmoid(z))).astype(BF16)

    return pl.pallas_call(
        body, name=name, grid=(rows // bm, width // bw),
        in_specs=[pl.BlockSpec((bm, bw), lambda i, t: (i, t)), pl.BlockSpec((bm, bw), lambda i, t: (i, cb0 + t))],
        out_specs=pl.BlockSpec((bm, bw), lambda i, t: (i, t)),
        out_shape=jax.ShapeDtypeStruct((rows, width), BF16),
        compiler_params=_params(("parallel", "parallel")),
    )(y, proj)


def _gate_bwd(dsv, y, proj, zc0, bw, dproj, t0, head_major, name):
    rows, width = y.shape
    bm = _tile(rows, 2048 if bw <= 256 else 1024, 16)
    cb0 = zc0 // bw
    tb0 = t0 // bw
    bd = _block_diag(HEAD_DIM)
    hpb = bw // HEAD_DIM

    def body(*refs):
        if head_major:
            ds_ref, y_ref, z_ref, bd_ref, _, dp_ref, dy_ref, dd_ref = refs
        else:
            ds_ref, y_ref, z_ref, _, dp_ref, dy_ref = refs
        z = z_ref[...].astype(F32)
        sig = _sigmoid(z)
        dsx = ds_ref[...]
        yv = y_ref[...]
        dy = dsx * (z * sig)
        dp_ref[...] = (dsx * yv * (sig * (1.0 + z * (1.0 - sig)))).astype(BF16)
        if head_major:
            dyb = dy.astype(BF16)
            dd = _seg_sum(dyb.astype(F32) * yv, bd_ref[...])
            for h in range(hpb):
                sl = slice(h * HEAD_DIM, (h + 1) * HEAD_DIM)
                dy_ref[h] = dyb[:, sl]
                dd_ref[h] = dd[:, sl]
        else:
            dy_ref[...] = dy.astype(BF16)

    tile = pl.BlockSpec((bm, bw), lambda i, t: (i, t))
    ztile = pl.BlockSpec((bm, bw), lambda i, t: (i, cb0 + t))
    ttile = pl.BlockSpec((bm, bw), lambda i, t: (i, tb0 + t))
    any_spec = pl.BlockSpec(memory_space=pl.ANY)
    dp_shape = jax.ShapeDtypeStruct(dproj.shape, BF16)
    if head_major:
        hm_spec = pl.BlockSpec((hpb, bm, HEAD_DIM), lambda i, t: (t, i, 0))
        nh = width // HEAD_DIM
        outs = pl.pallas_call(
            body, name=name, grid=(rows // bm, width // bw),
            in_specs=[tile, tile, ztile, pl.BlockSpec((LANES, LANES), lambda i, t: (0, 0)), any_spec],
            out_specs=[ttile, hm_spec, hm_spec],
            out_shape=[dp_shape, jax.ShapeDtypeStruct((nh, rows, HEAD_DIM), BF16),
                       jax.ShapeDtypeStruct((nh, rows, HEAD_DIM), F32)],
            input_output_aliases={4: 0},
            compiler_params=_params(("parallel", "parallel")),
        )(dsv, y, proj, bd, dproj)
        return outs[0], outs[1], outs[2]
    outs = pl.pallas_call(
        body, name=name, grid=(rows // bm, width // bw),
        in_specs=[tile, tile, ztile, any_spec],
        out_specs=[ttile, tile],
        out_shape=[dp_shape, jax.ShapeDtypeStruct((rows, width), BF16)],
        input_output_aliases={3: 0},
        compiler_params=_params(("parallel", "parallel")),
    )(dsv, y, proj, dproj)
    return outs[0], outs[1], None


def _merge_fwd(proj, ua, ub, uc, name):
    rows, d = ua.shape
    bm = _tile(rows, 1024, 16)
    bw = _tile(d, 512)
    g0 = COL_GATE // bw
    gstep = d // bw

    def body(la_ref, lb_ref, lc_ref, ua_ref, ub_ref, uc_ref, o_ref, ga_ref, gb_ref, gc_ref):
        y = None
        for l_ref, u_ref, g_ref in ((la_ref, ua_ref, ga_ref), (lb_ref, ub_ref, gb_ref), (lc_ref, uc_ref, gc_ref)):
            g = _sigmoid(l_ref[...].astype(F32))
            g_ref[...] = g.astype(BF16)
            term = g * u_ref[...].astype(F32)
            y = term if y is None else y + term
        o_ref[...] = y.astype(BF16)

    tile = pl.BlockSpec((bm, bw), lambda i, t: (i, t))
    gate = lambda b: pl.BlockSpec((bm, bw), lambda i, t: (i, g0 + b * gstep + t))
    shape = jax.ShapeDtypeStruct((rows, d), BF16)
    return pl.pallas_call(
        body, name=name, grid=(rows // bm, d // bw),
        in_specs=[gate(0), gate(1), gate(2), tile, tile, tile],
        out_specs=[tile] * 4, out_shape=[shape] * 4,
        compiler_params=_params(("parallel", "parallel")),
    )(proj, proj, proj, ua, ub, uc)


def _merge_bwd(dym, us, gs, name):
    rows, d = dym.shape
    bm = _tile(rows, 256, 16)

    def body(dy_ref, ua_ref, ub_ref, uc_ref, ga_ref, gb_ref, gc_ref, dg_ref, da_ref, db_ref, dc_ref):
        dyv = dy_ref[...]
        for b, (u_ref, g_ref, du_ref) in enumerate(((ua_ref, ga_ref, da_ref), (ub_ref, gb_ref, db_ref), (uc_ref, gc_ref, dc_ref))):
            g = g_ref[...].astype(F32)
            du_ref[...] = (g * dyv).astype(BF16)
            dg_ref[:, b * d:(b + 1) * d] = (dyv * u_ref[...].astype(F32) * g * (1.0 - g)).astype(BF16)

    tile = pl.BlockSpec((bm, d), lambda i: (i, 0))
    shape = jax.ShapeDtypeStruct((rows, d), BF16)
    outs = pl.pallas_call(
        body, name=name, grid=(rows // bm,),
        in_specs=[tile] * 7,
        out_specs=[pl.BlockSpec((bm, 3 * d), lambda i: (i, 0)), tile, tile, tile],
        out_shape=[jax.ShapeDtypeStruct((rows, 3 * d), BF16), shape, shape, shape],
        compiler_params=_params(("parallel",), VMEM_BIG),
    )(dym, *us, *gs)
    return outs[0], outs[1], outs[2], outs[3]


def _out_proj_loss(ym, wo, x, target, name):
    m, d = x.shape
    bm, bn = _tile(m, 1024, 16), _tile(d, 1024)
    grid = (m // bm, d // bn)

    def body(a_ref, b_ref, x_ref, t_ref, dy_ref, dyb_ref, l_ref):
        first, _ = _grid_edges(grid)
        y = jnp.dot(a_ref[...], b_ref[...], preferred_element_type=F32) + x_ref[...]
        diff = y - t_ref[...]
        dy = diff * (1.0 / d)
        dy_ref[...] = dy
        dyb_ref[...] = dy.astype(BF16)
        sq = diff * diff
        part = sq[:, 0:LANES]
        for c in range(1, bn // LANES):
            part = part + sq[:, c * LANES:(c + 1) * LANES]
        part = jnp.sum(part.reshape(bm // 8, 8, LANES), axis=0)

        @pl.when(first)
        def _():
            l_ref[...] = part

        @pl.when(jnp.logical_not(first))
        def _():
            l_ref[...] += part

    tile = pl.BlockSpec((bm, bn), lambda i, j: (i, j))
    return pl.pallas_call(
        body, name=name, grid=grid,
        in_specs=[pl.BlockSpec((bm, d), lambda i, j: (i, 0)), pl.BlockSpec((d, bn), lambda i, j: (0, j)), tile, tile],
        out_specs=[tile, tile, pl.BlockSpec((8, LANES), lambda i, j: (0, 0))],
        out_shape=[jax.ShapeDtypeStruct((m, d), F32), jax.ShapeDtypeStruct((m, d), BF16),
                   jax.ShapeDtypeStruct((8, LANES), F32)],
        compiler_params=_params(("arbitrary", "arbitrary"), VMEM_BIG),
    )(ym, wo, x, target)


def _row(vec, reps=1):
    return jnp.tile(vec.reshape(1, -1).astype(F32), (1, reps))


def _local_step(x, mem, target, small, wg, shards=None):
    s, d = x.shape
    dist = shards is not None
    wg = dict(wg)
    ones = lambda n: jnp.ones((1, n), F32)
    zeros = lambda n: jnp.zeros((1, n), F32)
    scale_ab = HEAD_DIM ** -0.5
    split8 = lambda g: g.reshape(N_DEV, g.shape[0] // N_DEV, g.shape[1])
    flat8 = lambda g: g.reshape(g.shape[0] * g.shape[1], g.shape[2])
    gather = lambda names: _Comm("gather", [shards[n] for n in names]) if dist else None
    g = {}

    def scatter(names):
        return _Comm("scatter", [split8(g[n]) for n in names]) if dist else None

    def hosted(result, names, store):
        if not dist:
            return result
        out, got = result
        store.update(zip(names, got))
        return out

    hn = _rmsnorm_fwd(x, small["norm_gain"], "rms_x_fwd")
    got = {}
    names = ("wf", "wk", "wa", "wb", "wc")
    proj = hosted(_mm_nn(hn, wg["qkv"], bm=1024, bn=1024, bk=d, o_dtype=BF16, name="proj_qkv",
                         comm=gather(names)), names, got)
    wg.update({n: flat8(a) for n, a in got.items()})
    pfb = _mm_nn(hn, wg["wf"], bm=1024, bn=FB_PAD, bk=d, o_dtype=F32, name="proj_fb")
    mn = _rmsnorm_fwd(mem, small["mem_norm_gain"], "rms_mem_fwd")
    mkv = _mm_nn(mn, wg["wk"], bm=256, bn=1024, bk=d, o_dtype=F32, name="mem_kv")

    gain_a = jnp.concatenate([_row(small["q_gain_a"], A_Q_HEADS) * scale_ab, _row(small["k_gain_a"], A_KV_HEADS), ones(A_KV_WIDTH)], axis=1)
    flag_a = jnp.concatenate([ones(A_WIDTH + A_KV_WIDTH), zeros(A_KV_WIDTH)], axis=1)
    qkv_a = _headnorm_fwd(proj, COL_QA, 1280, 1280, HEAD_DIM, gain_a, flag_a, True, "hn_a_fwd")
    gain_b = jnp.concatenate([_row(small["q_gain_b"], B_HEADS) * scale_ab, _row(small["k_gain_b"], B_HEADS), ones(B_WIDTH)], axis=1)
    flag_b = jnp.concatenate([ones(2 * B_WIDTH), zeros(B_WIDTH)], axis=1)
    qkv_b = _headnorm_fwd(proj, COL_QB, 2304, 256, HEAD_DIM, gain_b, flag_b, True, "hn_b_fwd")
    gain_cq = _row(small["q_gain_c"], C_HEADS)
    q_c = _headnorm_fwd(proj, COL_QC, C_WIDTH, C_WIDTH, C_HEAD_DIM, gain_cq, ones(C_WIDTH), False, "hn_cq_fwd")
    gain_ck = jnp.concatenate([_row(small["k_gain_c"], C_HEADS), ones(C_WIDTH)], axis=1)
    flag_ck = jnp.concatenate([ones(C_WIDTH), zeros(C_WIDTH)], axis=1)
    mkvn = _headnorm_fwd(mkv, 0, 2 * C_WIDTH, 2 * C_WIDTH, C_HEAD_DIM, gain_ck, flag_ck, False, "hn_ck_fwd")


    bpad = jnp.pad(small["b_forget"].reshape(1, -1), ((0, 0), (0, FB_PAD - B_HEADS)))
    c16 = _fox_prep(pfb, bpad, "fox_prep")
    c3 = c16[0:B_HEADS].reshape(B_HEADS, 1, s)

    sinks = small["sinks_a"].reshape(-1)
    slopes = jnp.exp2(-8.0 * jnp.arange(1, A_Q_HEADS + 1, dtype=F32) / A_Q_HEADS)
    y_a, lse_a = _attn_a_fwd(qkv_a, sinks, slopes, "attn_a_fwd")
    y_b, lse_b, got_zg = _attn_b_fwd(qkv_b, c3, "attn_b_fwd", comm=gather(("zg",)))
    if dist:
        wg["zg"] = flat8(got_zg[0])
    y_c = _attn_c_fwd(q_c, mkvn, "attn_c_fwd")

    got = {}
    pzg = hosted(_mm_nn(hn, wg["zg"], bm=1024, bn=1024, bk=d, o_dtype=BF16, name="proj_zg", comm=gather(("wo",))),
                 ("wo",), got)
    wg.update({n: flat8(a) for n, a in got.items()})

    s_a = _gate_fwd(y_a, pzg, COL_ZA, 256, "gate_a_fwd")
    s_b = _gate_fwd(y_b, pzg, COL_ZB, 256, "gate_b_fwd")
    s_c = _gate_fwd(y_c, pzg, COL_ZC, 512, "gate_c_fwd")
    w_a, w_b, w_c = _branch_full(wg["wa"]), _branch_full(wg["wb"]), _branch_full(wg["wc"])
    u_a = _mm_nn(s_a, w_a, bm=1024, bn=2048, bk=A_WIDTH, o_dtype=BF16, name="branch_a_fwd")
    u_b = _mm_nn(s_b, w_b, bm=1024, bn=2048, bk=B_WIDTH, o_dtype=BF16, name="branch_b_fwd")
    u_c = _mm_nn(s_c, w_c, bm=1024, bn=2048, bk=C_WIDTH, o_dtype=BF16, name="branch_c_fwd")
    ym, gate_a, gate_b, gate_c = _merge_fwd(pzg, u_a, u_b, u_c, "merge_fwd")
    dy, dyb, lpart = _out_proj_loss(ym, wg["wo"], x, target, "out_proj_loss")
    loss = 0.5 / d * jnp.sum(lpart)

    dym = _mm_nt(dyb, wg["wo"], bm=1024, bn=1024, bk=d, o_dtype=F32, name="out_proj_bwd_act")
    g["wo"] = _mm_tn(ym, dyb, bm=512, bn=1024, bk=s, o_dtype=BF16, name="out_proj_bwd_w")

    dgate, du_a, du_b, du_c = _merge_bwd(dym, (u_a, u_b, u_c), (gate_a, gate_b, gate_c), "merge_bwd")
    parts = {}
    g["wm_g"] = hosted(_mm_tn(hn, dgate, bm=512, bn=1024, bk=s, o_dtype=BF16, name="proj_gate_bwd_w",
                              comm=scatter(("wo",))), ("wo",), parts)

    ds_a = _mm_nt(du_a, w_a, bm=1024, bn=A_WIDTH, bk=d, o_dtype=F32, name="branch_a_bwd_act")
    ds_b = _mm_nt(du_b, w_b, bm=1024, bn=B_WIDTH, bk=d, o_dtype=F32, name="branch_b_bwd_act")
    ds_c = _mm_nt(du_c, w_c, bm=1024, bn=C_WIDTH, bk=d, o_dtype=F32, name="branch_c_bwd_act")
    g["wa"] = _branch_shards(_mm_tn(s_a, du_a, bm=A_WIDTH, bn=1024, bk=s, o_dtype=BF16, name="branch_a_bwd_w"))
    g["wb"] = _branch_shards(_mm_tn(s_b, du_b, bm=B_WIDTH, bn=1024, bk=s, o_dtype=BF16, name="branch_b_bwd_w"))
    g["wc"] = _branch_shards(_mm_tn(s_c, du_c, bm=C_WIDTH, bn=1024, bk=s, o_dtype=BF16, name="branch_c_bwd_w"))

    dz = lax.empty((s, W_Z), BF16)
    dz, do_a, dd_a = _gate_bwd(ds_a, y_a, pzg, COL_ZA, 256, dz, COL_ZA, True, "gate_a_bwd")
    dz, do_b, dd_b = _gate_bwd(ds_b, y_b, pzg, COL_ZB, 256, dz, COL_ZB, True, "gate_b_bwd")
    dz, do_c, _ = _gate_bwd(ds_c, y_c, pzg, COL_ZC, 512, dz, COL_ZC, False, "gate_c_bwd")
    g["wm_z"] = _mm_tn(hn, dz, bm=512, bn=1024, bk=s, o_dtype=BF16, name="proj_z_bwd_w")

    names = ("wa", "wb", "wc")
    dq_a, dkv_a, dsink, got = _attn_a_bwd(qkv_a, do_a, lse_a, dd_a, sinks, slopes, "attn_a_bwd", comm=scatter(names))
    parts.update(zip(names, got))
    names = ("wm_g", "wm_z")
    dq_b, dk_b, dv_b, dc3, got = _attn_b_bwd(qkv_b, do_b, lse_b, dd_b, c3, "attn_b_bwd", comm=scatter(names))
    parts.update(zip(names, got))
    dq_c, dmkvn = _attn_c_bwd(q_c, mkvn, do_c, "attn_c_bwd")

    dqkv = lax.empty((s, W_QKV), BF16)
    dqkv, dg_qa = _headnorm_bwd(proj, COL_QA, A_WIDTH, 256, HEAD_DIM, gain_a[:, 0:768], flag_a[:, 0:768], dq_a, dqkv, COL_QA, "hn_qa_bwd")
    dqkv, dg_kva = _headnorm_bwd(proj, COL_KA, 512, 256, HEAD_DIM, gain_a[:, 768:1280], flag_a[:, 768:1280], dkv_a, dqkv, COL_KA, "hn_kva_bwd")
    dqkv, dg_qb = _headnorm_bwd(proj, COL_QB, B_WIDTH, 256, HEAD_DIM, gain_b[:, 0:768], flag_b[:, 0:768], dq_b, dqkv, COL_QB, "hn_qb_bwd")
    dqkv, dg_kb = _headnorm_bwd(proj, COL_KB, B_WIDTH, 256, HEAD_DIM, gain_b[:, 768:1536], flag_b[:, 768:1536], dk_b, dqkv, COL_KB, "hn_kb_bwd")
    dqkv, _ = _headnorm_bwd(proj, COL_VB, B_WIDTH, 256, HEAD_DIM, gain_b[:, 1536:2304], flag_b[:, 1536:2304], dv_b, dqkv, COL_VB, "hn_vb_bwd")
    dqkv, dg_qc = _headnorm_bwd(proj, COL_QC, C_WIDTH, 512, C_HEAD_DIM, gain_cq, ones(C_WIDTH), dq_c, dqkv, COL_QC, "hn_qc_bwd")
    dmkv, dg_kc = _headnorm_bwd(mkv, 0, 2 * C_WIDTH, 2 * C_WIDTH, C_HEAD_DIM, gain_ck, flag_ck, dmkvn, None, 0, "hn_kc_bwd")

    dct = jnp.pad(dc3.reshape(B_HEADS, s), ((0, 16 - B_HEADS), (0, 0)))
    dfb, dbf = _fox_prep_bwd(pfb, bpad, dct, "fox_prep_bwd")

    dmn = _mm_nt(dmkv, wg["wk"], bm=256, bn=1024, bk=1024, o_dtype=F32, name="mem_kv_bwd_act")
    g["wk"] = _mm_tn(mn, dmkv, bm=512, bn=1024, bk=mem.shape[0], o_dtype=BF16, name="mem_kv_bwd_w")
    _, dg_mem = _rmsnorm_bwd(mem, dmn, small["mem_norm_gain"], None, "rms_mem_bwd")

    g["wm_qkv"] = _mm_tn(hn, dqkv, bm=512, bn=1024, bk=s, o_dtype=BF16, name="proj_qkv_bwd_w")
    g["wf"] = _mm_tn(hn, dfb, bm=512, bn=FB_PAD, bk=s, o_dtype=BF16, name="proj_fb_bwd_w")
    half = Q_SPLIT
    g["wm_q1"], g["wm_q2"] = g["wm_qkv"][:, 0:half], g["wm_qkv"][:, half:W_QKV]
    names = ("wm_q1",)
    dhn = hosted(_mm_nt_sum([(dqkv, wg["qkv"], 0), (dfb, wg["wf"], 0)], bm=1024, bn=1024, bk=2048,
                            name="proj_qkv_bwd_act", comm=scatter(names)), names, parts)
    names = ("wm_q2", "wf", "wk")
    dhn = hosted(_mm_nt_sum([(dz, wg["zg"], COL_ZA), (dgate, wg["zg"], COL_GATE)], bm=1024, bn=1024, bk=2048,
                            name="proj_zg_bwd_act", add=dhn, comm=scatter(names)), names, parts)
    if dist:
        g = parts
    grad_x, dg_x = _rmsnorm_bwd(x, dhn, small["norm_gain"], dy, "rms_x_bwd")

    fold = lambda part, heads, hd: jnp.sum(jnp.sum(part, axis=0).reshape(heads, hd), axis=0).reshape(1, hd)
    small_grads = {
        "norm_gain": jnp.sum(dg_x, axis=0).reshape(1, d),
        "mem_norm_gain": jnp.sum(dg_mem, axis=0).reshape(1, d),
        "b_forget": dbf[0:B_HEADS, 0].reshape(1, B_HEADS),
        "q_gain_a": fold(dg_qa, A_Q_HEADS, HEAD_DIM) * scale_ab,
        "k_gain_a": fold(dg_kva[:, 0:A_KV_WIDTH], A_KV_HEADS, HEAD_DIM),
        "sinks_a": (jnp.sum(dsink, axis=(1, 2)) * (1.0 / HEAD_DIM)).reshape(1, A_Q_HEADS),
        "q_gain_b": fold(dg_qb, B_HEADS, HEAD_DIM) * scale_ab,
        "k_gain_b": fold(dg_kb, B_HEADS, HEAD_DIM),
        "q_gain_c": fold(dg_qc, C_HEADS, C_HEAD_DIM),
        "k_gain_c": fold(dg_kc[:, 0:C_WIDTH], C_HEADS, C_HEAD_DIM),
    }
    return loss, grad_x, small_grads, g


def _coords():
    return lax.axis_index("x"), lax.axis_index("y"), lax.axis_index("c")


def _all_gather(shards, name):
    n = len(shards)

    def body(*refs):
        ins = refs[0:n]
        outs = refs[n:2 * n]
        send_sems, recv_sems, local_sems = refs[2 * n:2 * n + 3]
        x, y, c = _coords()
        me, sibling = (x, y, c), (x, y, 1 - c)
        chips = [(1 - x, y), (x, 1 - y), (1 - x, 1 - y)]
        idx = lambda p: 4 * p[0] + 2 * p[1] + p[2]

        def copy(a, k, block, to, src=None):
            slot = outs[a].at[idx(block)]
            return pltpu.make_async_remote_copy(
                src_ref=slot if src is None else src, dst_ref=slot,
                send_sem=send_sems.at[a, k], recv_sem=recv_sems.at[a, k], device_id=to, device_id_type=MESH)

        mine = [pltpu.make_async_copy(ins[a], outs[a].at[idx(me)], local_sems.at[a]) for a in range(n)]
        for cp in mine:
            cp.start()
        first = []
        for a in range(n):
            first.append(copy(a, 0, me, sibling, src=ins[a]))
            first += [copy(a, 1 + j, me, (*chip, c), src=ins[a]) for j, chip in enumerate(chips)]
        for cp in first:
            cp.start()
        passed = []
        for j, chip in enumerate(chips):
            for a in range(n):
                copy(a, 1 + j, (*chip, c), me).wait_recv()
                fwd = copy(a, 4 + j, (*chip, c), sibling)
                fwd.start()
                passed.append(fwd)
        for a in range(n):
            copy(a, 0, sibling, me).wait_recv()
            for j, chip in enumerate(chips):
                copy(a, 4 + j, (*chip, 1 - c), me).wait_recv()
        for cp in first + passed:
            cp.wait_send()
        for cp in mine:
            cp.wait()

    any_spec = pl.BlockSpec(memory_space=pl.ANY)
    return pl.pallas_call(
        body, name=name,
        in_specs=[any_spec] * n, out_specs=[any_spec] * n,
        out_shape=[jax.ShapeDtypeStruct((N_DEV,) + sh.shape, sh.dtype) for sh in shards],
        scratch_shapes=[pltpu.SemaphoreType.DMA((n, 7)), pltpu.SemaphoreType.DMA((n, 7)), pltpu.SemaphoreType.DMA((n,))],
    )(*shards)


def _all_reduce_small(vec, name):
    p = vec.shape[1]

    def body(v_ref, o_ref, gather, send_sems, recv_sems):
        x, y, c = _coords()
        my = 4 * x + 2 * y + c
        peers = [(x ^ ((k >> 2) & 1), y ^ ((k >> 1) & 1), c ^ (k & 1)) for k in range(1, N_DEV)]
        gather[my] = v_ref[...]
        sends = [pltpu.make_async_remote_copy(
            src_ref=v_ref, dst_ref=gather.at[my], send_sem=send_sems.at[k], recv_sem=recv_sems.at[k],
            device_id=peer, device_id_type=MESH) for k, peer in enumerate(peers)]
        for cp in sends:
            cp.start()
        for k, peer in enumerate(peers):
            pid = 4 * peer[0] + 2 * peer[1] + peer[2]
            pltpu.make_async_remote_copy(
                src_ref=v_ref, dst_ref=gather.at[pid], send_sem=send_sems.at[k], recv_sem=recv_sems.at[k],
                device_id=peer, device_id_type=MESH).wait_recv()
        for cp in sends:
            cp.wait_send()
        total = gather[0]
        for j in range(1, N_DEV):
            total = total + gather[j]
        o_ref[...] = total

    vm = pl.BlockSpec(memory_space=pltpu.VMEM)
    return pl.pallas_call(
        body, name=name, in_specs=[vm], out_specs=vm,
        out_shape=jax.ShapeDtypeStruct((8, p), F32),
        scratch_shapes=[pltpu.VMEM((N_DEV, 8, p), F32), pltpu.SemaphoreType.DMA((7,)), pltpu.SemaphoreType.DMA((7,))],
    )(vec)[0:1]


def _sum_parts(parts, name):
    _, rows, cols = parts.shape
    br = _tile(rows, 64, 16)

    def body(p_ref, o_ref):
        total = p_ref[0].astype(F32)
        for j in range(1, N_DEV):
            total = total + p_ref[j].astype(F32)
        o_ref[...] = total

    return pl.pallas_call(
        body, name=name, grid=(rows // br,),
        in_specs=[pl.BlockSpec((N_DEV, br, cols), lambda i: (0, i, 0))],
        out_specs=pl.BlockSpec((br, cols), lambda i: (i, 0)),
        out_shape=jax.ShapeDtypeStruct((rows, cols), F32),
        compiler_params=_params(("parallel",), VMEM_BIG),
    )(parts)


def _adamw(w, g, m, v, name, br=32):
    rows, cols = w.shape
    br = min(br, rows)
    c1 = 1.0 / (1.0 - ADAM_B1 ** ADAM_STEP)
    c2 = 1.0 / (1.0 - ADAM_B2 ** ADAM_STEP)

    def body(w_ref, g_ref, m_ref, v_ref, d_ref, nm_ref, nv_ref):
        gv = g_ref[...]
        nm = ADAM_B1 * m_ref[...] + (1.0 - ADAM_B1) * gv
        nv = ADAM_B2 * v_ref[...] + (1.0 - ADAM_B2) * (gv * gv)
        d_ref[...] = -ADAM_LR * ((nm * c1) / (jnp.sqrt(nv * c2) + ADAM_EPS) + ADAM_WD * w_ref[...])
        nm_ref[...] = nm
        nv_ref[...] = nv

    spec = pl.BlockSpec((br, cols), lambda i: (i, 0))
    shape = jax.ShapeDtypeStruct((rows, cols), F32)
    return pl.pallas_call(
        body, name=name, grid=(pl.cdiv(rows, br),), in_specs=[spec] * 4, out_specs=[spec] * 3, out_shape=[shape] * 3,
        compiler_params=_params(("parallel",), VMEM_BIG),
    )(w, g, m, v)


def _adamw_t(wt, g, mt, vt, name, br=1024):
    n, r = wt.shape
    c1 = 1.0 / (1.0 - ADAM_B1 ** ADAM_STEP)
    c2 = 1.0 / (1.0 - ADAM_B2 ** ADAM_STEP)

    def body(w_ref, g_ref, m_ref, v_ref, d_ref, nm_ref, nv_ref):
        gv = g_ref[...].T
        nm = ADAM_B1 * m_ref[...] + (1.0 - ADAM_B1) * gv
        nv = ADAM_B2 * v_ref[...] + (1.0 - ADAM_B2) * (gv * gv)
        d_ref[...] = -ADAM_LR * ((nm * c1) / (jnp.sqrt(nv * c2) + ADAM_EPS) + ADAM_WD * w_ref[...])
        nm_ref[...] = nm
        nv_ref[...] = nv

    spec = pl.BlockSpec((br, r), lambda i: (i, 0))
    shape = jax.ShapeDtypeStruct((n, r), F32)
    return pl.pallas_call(
        body, name=name, grid=(pl.cdiv(n, br),),
        in_specs=[spec, pl.BlockSpec((r, br), lambda i: (0, i)), spec, spec], out_specs=[spec] * 3, out_shape=[shape] * 3,
        compiler_params=_params(("parallel",), VMEM_BIG),
    )(wt, g, mt, vt)


def _adamw_parts(w, parts, m, v, name):
    rows, cols = w.shape
    br = _tile(rows, 32, 16)
    c1 = 1.0 / (1.0 - ADAM_B1 ** ADAM_STEP)
    c2 = 1.0 / (1.0 - ADAM_B2 ** ADAM_STEP)

    def body(w_ref, p_ref, m_ref, v_ref, g_ref, d_ref, nm_ref, nv_ref):
        gv = p_ref[0].astype(F32)
        for j in range(1, N_DEV):
            gv = gv + p_ref[j].astype(F32)
        nm = ADAM_B1 * m_ref[...] + (1.0 - ADAM_B1) * gv
        nv = ADAM_B2 * v_ref[...] + (1.0 - ADAM_B2) * (gv * gv)
        g_ref[...] = gv
        d_ref[...] = -ADAM_LR * ((nm * c1) / (jnp.sqrt(nv * c2) + ADAM_EPS) + ADAM_WD * w_ref[...])
        nm_ref[...] = nm
        nv_ref[...] = nv

    spec = pl.BlockSpec((br, cols), lambda i: (i, 0))
    shape = jax.ShapeDtypeStruct((rows, cols), F32)
    return pl.pallas_call(
        body, name=name, grid=(rows // br,),
        in_specs=[spec, pl.BlockSpec((N_DEV, br, cols), lambda i: (0, i, 0)), spec, spec],
        out_specs=[spec] * 4, out_shape=[shape] * 4,
        compiler_params=_params(("parallel",), VMEM_BIG),
    )(w, parts, m, v)


SMALL_NAMES = ("norm_gain", "mem_norm_gain", "b_forget", "q_gain_a", "k_gain_a", "sinks_a",
               "q_gain_b", "k_gain_b", "q_gain_c", "k_gain_c")
BIG_NAMES = ("w_in", "w_mem_kv", "w_branch_a", "w_branch_b", "w_branch_c", "w_out")
WEIGHT_ORDER = ("norm_gain", "mem_norm_gain", "w_in", "b_forget", "q_gain_a", "k_gain_a", "sinks_a", "q_gain_b",
                "k_gain_b", "q_gain_c", "k_gain_c", "w_mem_kv", "w_branch_a", "w_branch_b", "w_branch_c", "w_out")


def _pack_small(tree):
    flat = jnp.concatenate([tree[n].reshape(1, -1) for n in SMALL_NAMES], axis=1)
    pad = (-flat.shape[1]) % LANES
    return jnp.pad(flat, ((0, 0), (0, pad)))


def _unpack_small(flat, like):
    out, off = {}, 0
    for n in SMALL_NAMES:
        size = like[n].size
        out[n] = flat[:, off:off + size].reshape(like[n].shape)
        off += size
    return out


def kernel(x, mem, norm_gain, mem_norm_gain, w_in, b_forget, q_gain_a, k_gain_a, sinks_a, q_gain_b, k_gain_b, q_gain_c, k_gain_c, w_mem_kv, w_branch_a, w_branch_b, w_branch_c, w_out, loss_target, m_norm_gain, m_mem_norm_gain, m_w_in, m_b_forget, m_q_gain_a, m_k_gain_a, m_sinks_a, m_q_gain_b, m_k_gain_b, m_q_gain_c, m_k_gain_c, m_w_mem_kv, m_w_branch_a, m_w_branch_b, m_w_branch_c, m_w_out, v_norm_gain, v_mem_norm_gain, v_w_in, v_b_forget, v_q_gain_a, v_k_gain_a, v_sinks_a, v_q_gain_b, v_k_gain_b, v_q_gain_c, v_k_gain_c, v_w_mem_kv, v_w_branch_a, v_w_branch_b, v_w_branch_c, v_w_out):
    weights = dict(norm_gain=norm_gain, mem_norm_gain=mem_norm_gain, w_in=w_in, b_forget=b_forget, q_gain_a=q_gain_a,
                   k_gain_a=k_gain_a, sinks_a=sinks_a, q_gain_b=q_gain_b, k_gain_b=k_gain_b, q_gain_c=q_gain_c,
                   k_gain_c=k_gain_c, w_mem_kv=w_mem_kv, w_branch_a=w_branch_a, w_branch_b=w_branch_b,
                   w_branch_c=w_branch_c, w_out=w_out)
    mom_m = dict(norm_gain=m_norm_gain, mem_norm_gain=m_mem_norm_gain, w_in=m_w_in, b_forget=m_b_forget,
                 q_gain_a=m_q_gain_a, k_gain_a=m_k_gain_a, sinks_a=m_sinks_a, q_gain_b=m_q_gain_b, k_gain_b=m_k_gain_b,
                 q_gain_c=m_q_gain_c, k_gain_c=m_k_gain_c, w_mem_kv=m_w_mem_kv, w_branch_a=m_w_branch_a,
                 w_branch_b=m_w_branch_b, w_branch_c=m_w_branch_c, w_out=m_w_out)
    mom_v = dict(norm_gain=v_norm_gain, mem_norm_gain=v_mem_norm_gain, w_in=v_w_in, b_forget=v_b_forget,
                 q_gain_a=v_q_gain_a, k_gain_a=v_k_gain_a, sinks_a=v_sinks_a, q_gain_b=v_q_gain_b, k_gain_b=v_k_gain_b,
                 q_gain_c=v_q_gain_c, k_gain_c=v_k_gain_c, w_mem_kv=v_w_mem_kv, w_branch_a=v_w_branch_a,
                 w_branch_b=v_w_branch_b, w_branch_c=v_w_branch_c, w_out=v_w_out)
    wi = w_in[0]
    sh_qkv = jnp.concatenate([wi[:, a:b] for a, b in SRC_RANGES[0:3]], axis=1).astype(BF16)
    sh_zg = jnp.concatenate([wi[:, a:b] for a, b in SRC_RANGES[3:6]] + [wi[:, SRC_GATE:]], axis=1).astype(BF16)
    sh_wf = jnp.pad(wi[:, FB_SRC:FB_SRC + B_HEADS], ((0, 0), (0, FB_PAD - B_HEADS))).astype(BF16)
    shards = {"zg": sh_zg, "wf": sh_wf, "wk": w_mem_kv[0].astype(BF16), "wo": w_out[0].astype(BF16),
              "wa": w_branch_a[0].astype(BF16), "wb": w_branch_b[0].astype(BF16), "wc": w_branch_c[0].astype(BF16)}
    full = _all_gather([sh_qkv], "weights_all_gather")[0]
    wg = {"qkv": full.reshape(full.shape[0] * full.shape[1], full.shape[2])}

    small = {n: weights[n] for n in SMALL_NAMES}
    loss_local, grad_x, small_g, parts = _local_step(x[0], mem[0], loss_target[0], small, wg, shards)

    grads, delta, new_m, new_v = {}, {}, {}, {}
    for n, kname in (("w_mem_kv", "wk"), ("w_out", "wo"), ("w_branch_a", "wa"), ("w_branch_b", "wb"), ("w_branch_c", "wc")):
        gsum, dlt, nm, nv = _adamw_parts(weights[n][0], parts[kname], mom_m[n][0], mom_v[n][0], "adamw_" + n)
        grads[n], delta[n], new_m[n], new_v[n] = gsum, dlt[None], nm[None], nv[None]
    g1, g2, gz, gf, gg = (_sum_parts(parts[k], "grad_sum_" + k) for k in ("wm_q1", "wm_q2", "wm_z", "wf", "wm_g"))
    half = Q_SPLIT
    g_in = jnp.concatenate([g1[:, COL_QA:COL_QB], gz[:, COL_ZA:COL_ZB], g1[:, COL_QB:half], g2[:, 0:COL_QC - half],
                            gz[:, COL_ZB:COL_ZC], gf[:, 0:B_HEADS], g2[:, COL_QC - half:W_QKV - half], gz[:, COL_ZC:W_Z], gg], axis=1)
    dlt, nm, nv = _adamw_t(w_in[0].T, g_in, m_w_in[0].T, v_w_in[0].T, "adamw_w_in")
    grads["w_in"], delta["w_in"], new_m["w_in"], new_v["w_in"] = g_in, dlt.T[None], nm.T[None], nv.T[None]

    packed = _pack_small(small_g)
    packed = jnp.concatenate([packed[:, :-1], loss_local.reshape(1, 1)], axis=1)
    reduced = _all_reduce_small(jnp.broadcast_to(packed, (8, packed.shape[1])), "small_all_reduce")
    grads.update(_unpack_small(reduced, small))
    loss = reduced[0, -1]

    pw, pm, pv = _pack_small(small), _pack_small({n: mom_m[n] for n in SMALL_NAMES}), _pack_small({n: mom_v[n] for n in SMALL_NAMES})
    rep8 = lambda a: jnp.broadcast_to(a, (8, a.shape[1]))
    dlt, nm, nv = _adamw(rep8(pw), rep8(reduced), rep8(pm), rep8(pv), "adamw_small")
    for tree, flat in ((delta, dlt), (new_m, nm), (new_v, nv)):
        tree.update(_unpack_small(flat[0:1], small))
    for n in BIG_NAMES:
        grads[n] = grads[n][None]
    return (loss, grad_x[None], *[grads[n] for n in WEIGHT_ORDER], *[delta[n] for n in WEIGHT_ORDER],
            *[new_m[n] for n in WEIGHT_ORDER], *[new_v[n] for n in WEIGHT_ORDER])
```

```python
import math

import jax
import jax.numpy as jnp
import numpy as np
from jax import lax
from jax.experimental import pallas as pl
from jax.experimental.pallas import tpu as pltpu

F32 = jnp.float32
BF16 = jnp.bfloat16

N_DEV = 8
HEAD_DIM = 64
A_Q_HEADS = 12
A_KV_HEADS = 4
A_GROUP = 3
B_HEADS = 12
C_HEADS = 4
C_HEAD_DIM = 128
WINDOW = 128
A_WIDTH = 768
A_KV_WIDTH = 256
B_WIDTH = 768
C_WIDTH = 512
EPS = 1e-6
NEG = -1e30

COL_QA, COL_KA, COL_VA = 0, 768, 1024
COL_QB, COL_KB, COL_VB = 1280, 2048, 2816
COL_QC = 3584
W_QKV = 4096
Q_SPLIT = 1536
COL_ZA, COL_ZB, COL_ZC = 0, 768, 1536
COL_GATE = W_Z = 2048
SRC_RANGES = ((0, 1280), (2048, 4352), (5132, 5644), (1280, 2048), (4352, 5120), (5644, 6156))
SRC_GATE = 6156
FB_SRC = 5120
FB_PAD = 128

ADAM_LR = 0.001
ADAM_B1 = 0.9
ADAM_B2 = 0.999
ADAM_EPS = 1e-08
ADAM_WD = 0.01
ADAM_STEP = 10

VMEM_BIG = 52 * 1024 * 1024
LANES = 128
MESH = pl.DeviceIdType.MESH


def _tile(n, pref, mult=128):
    if n <= pref:
        return n
    t = (pref // mult) * mult
    while t >= mult:
        if n % t == 0:
            return t
        t -= mult
    return n


def _params(sem=None, vmem=None):
    kw = {}
    if sem is not None:
        kw["dimension_semantics"] = sem
    if vmem is not None:
        kw["vmem_limit_bytes"] = vmem
    return pltpu.CompilerParams(**kw)


def _sigmoid(x):
    return 1.0 / (1.0 + jnp.exp(-x))


def _block_diag(hd):
    r = np.arange(LANES)
    return jnp.asarray((r[:, None] // hd) == (r[None, :] // hd), dtype=BF16)


def _seg_sum(t, bd):
    hi = t.astype(BF16)
    lo = (t - hi.astype(F32)).astype(BF16)
    outs = []
    for c in range(t.shape[1] // LANES):
        sl = slice(c * LANES, (c + 1) * LANES)
        outs.append(jnp.dot(hi[:, sl], bd, preferred_element_type=F32) + jnp.dot(lo[:, sl], bd, preferred_element_type=F32))
    return outs[0] if len(outs) == 1 else jnp.concatenate(outs, axis=1)


def _rmsnorm_fwd(x, gain, name):
    rows, d = x.shape
    bm = _tile(rows, 512, 8)

    def body(x_ref, g_ref, o_ref):
        xv = x_ref[...]
        ms = jnp.mean(xv * xv, axis=-1, keepdims=True)
        o_ref[...] = (xv * lax.rsqrt(ms + EPS) * g_ref[...]).astype(BF16)

    return pl.pallas_call(
        body, name=name, grid=(rows // bm,),
        in_specs=[pl.BlockSpec((bm, d), lambda i: (i, 0)), pl.BlockSpec((1, d), lambda i: (0, 0))],
        out_specs=pl.BlockSpec((bm, d), lambda i: (i, 0)),
        out_shape=jax.ShapeDtypeStruct((rows, d), BF16),
        compiler_params=_params(("parallel",)),
    )(x, gain)


def _rmsnorm_bwd(x, dhn, gain, dy, name):
    rows, d = x.shape
    bm = _tile(rows, 512, 8)
    with_dx = dy is not None

    def body(*refs):
        if with_dx:
            x_ref, dh_ref, g_ref, dy_ref, gx_ref, dg_ref = refs
        else:
            x_ref, dh_ref, g_ref, dg_ref = refs
        i = pl.program_id(0)
        xv = x_ref[...]
        rstd = lax.rsqrt(jnp.mean(xv * xv, axis=-1, keepdims=True) + EPS)
        xhat = xv * rstd
        dh = dh_ref[...]
        part = jnp.sum((dh * xhat).reshape(bm // 8, 8, d), axis=0)

        @pl.when(i == 0)
        def _():
            dg_ref[...] = part

        @pl.when(i > 0)
        def _():
            dg_ref[...] += part

        if with_dx:
            g = dh * g_ref[...]
            mean = jnp.mean(g * xhat, axis=-1, keepdims=True)
            gx_ref[...] = dy_ref[...] + rstd * (g - xhat * mean)

    row_spec = pl.BlockSpec((bm, d), lambda i: (i, 0))
    in_specs = [row_spec, row_spec, pl.BlockSpec((1, d), lambda i: (0, 0))]
    args = [x, dhn, gain]
    dg_spec = pl.BlockSpec((8, d), lambda i: (0, 0))
    dg_shape = jax.ShapeDtypeStruct((8, d), F32)
    if with_dx:
        in_specs.append(row_spec)
        args.append(dy)
        out_specs = [row_spec, dg_spec]
        out_shape = [jax.ShapeDtypeStruct((rows, d), F32), dg_shape]
    else:
        out_specs = [dg_spec]
        out_shape = [dg_shape]
    outs = pl.pallas_call(
        body, name=name, grid=(rows // bm,), in_specs=in_specs, out_specs=out_specs, out_shape=out_shape,
        compiler_params=_params(("arbitrary",), VMEM_BIG),
    )(*args)
    return outs if with_dx else (None, outs[0])


class _Comm:
    def __init__(self, kind, arrays):
        self.kind = kind
        self.arrays = list(arrays)
        self.n = len(self.arrays)

    def out_shapes(self):
        if self.kind == "gather":
            return [jax.ShapeDtypeStruct((N_DEV,) + a.shape, a.dtype) for a in self.arrays]
        return [jax.ShapeDtypeStruct(a.shape, a.dtype) for a in self.arrays]

    def scratch(self):
        return [pltpu.SemaphoreType.DMA((self.n, N_DEV - 1)), pltpu.SemaphoreType.DMA((self.n, N_DEV - 1)),
                pltpu.SemaphoreType.DMA((self.n,))]

    def _plan(self, ins, outs, sems, with_recvs):
        send_sems, recv_sems, local_sems = sems
        x, y, c = lax.axis_index("x"), lax.axis_index("y"), lax.axis_index("c")
        my = 4 * x + 2 * y + c
        gather = self.kind == "gather"
        local, sends, recvs = [], [], []
        for a in range(self.n):
            local.append(pltpu.make_async_copy(ins[a] if gather else ins[a].at[my], outs[a].at[my], local_sems.at[a]))
            for k in range(1, N_DEV):
                peer = (x ^ ((k >> 2) & 1), y ^ ((k >> 1) & 1), c ^ (k & 1))
                pid = 4 * peer[0] + 2 * peer[1] + peer[2]
                src = ins[a] if gather else ins[a].at[pid]
                sem = dict(send_sem=send_sems.at[a, k - 1], recv_sem=recv_sems.at[a, k - 1], device_id=peer, device_id_type=MESH)
                sends.append(pltpu.make_async_remote_copy(src_ref=src, dst_ref=outs[a].at[my], **sem))
                if with_recvs:
                    recvs.append(pltpu.make_async_remote_copy(src_ref=src, dst_ref=outs[a].at[pid], **sem))
        return local, sends, recvs

    def start(self, ins, outs, sems):
        local, sends, _ = self._plan(ins, outs, sems, False)
        for cp in local + sends:
            cp.start()

    def wait(self, ins, outs, sems):
        local, sends, recvs = self._plan(ins, outs, sems, True)
        for cp in recvs:
            cp.wait_recv()
        for cp in sends:
            cp.wait_send()
        for cp in local:
            cp.wait()


def _grid_edges(grid):
    first = last = None
    for ax, size in enumerate(grid):
        pid = pl.program_id(ax)
        f, l = pid == 0, pid == size - 1
        first = f if first is None else first & f
        last = l if last is None else last & l
    return first, last


def _hosted_call(body, comm, *, name, grid, in_specs, out_specs, out_shape, scratch_shapes, args, sem, vmem=None):
    in_specs, out_specs, out_shape, scratch_shapes = list(in_specs), list(out_specs), list(out_shape), list(scratch_shapes)
    if comm is None:
        res = pl.pallas_call(body, name=name, grid=grid, in_specs=in_specs, out_specs=out_specs, out_shape=out_shape,
                             scratch_shapes=scratch_shapes, compiler_params=_params(sem, vmem))(*args)
        return list(res), []
    n_in, n_out, n_scr, nc = len(in_specs), len(out_shape), len(scratch_shapes), comm.n

    def hosted(*refs):
        ins = refs[0:n_in]
        comm_in = refs[n_in:n_in + nc]
        outs = refs[n_in + nc:n_in + nc + n_out]
        comm_out = refs[n_in + nc + n_out:n_in + 2 * nc + n_out]
        scr = refs[n_in + 2 * nc + n_out:n_in + 2 * nc + n_out + n_scr]
        sems = refs[n_in + 2 * nc + n_out + n_scr:]
        first, last = _grid_edges(grid)

        @pl.when(first)
        def _():
            comm.start(comm_in, comm_out, sems)

        body(*ins, *outs, *scr)

        @pl.when(last)
        def _():
            comm.wait(comm_in, comm_out, sems)

    any_spec = pl.BlockSpec(memory_space=pl.ANY)
    res = pl.pallas_call(
        hosted, name=name, grid=grid, in_specs=in_specs + [any_spec] * nc, out_specs=out_specs + [any_spec] * nc,
        out_shape=out_shape + comm.out_shapes(), scratch_shapes=scratch_shapes + comm.scratch(),
        compiler_params=_params(("arbitrary",) * len(grid), vmem),
    )(*args, *comm.arrays)
    return list(res[0:n_out]), list(res[n_out:])


def _mm(a, b, *, grid, a_spec, b_spec, o_spec, o_shape, o_dtype, contract, name, add=None, add_spec=None, acc_shape=None,
        comm=None):
    nk = grid[2]
    has_add = add is not None

    def body(*refs):
        a_ref, b_ref = refs[0], refs[1]
        add_ref = refs[2] if has_add else None
        o_ref = refs[3] if has_add else refs[2]
        part = lax.dot_general(a_ref[...], b_ref[...], (contract, ((), ())), preferred_element_type=F32)
        if nk == 1:
            if has_add:
                part = part + add_ref[...]
            o_ref[...] = part.astype(o_dtype)
        else:
            acc = refs[-1]
            k = pl.program_id(2)

            @pl.when(k == 0)
            def _():
                acc[...] = part

            @pl.when(k > 0)
            def _():
                acc[...] += part

            @pl.when(k == nk - 1)
            def _():
                r = acc[...]
                if has_add:
                    r = r + add_ref[...]
                o_ref[...] = r.astype(o_dtype)

    in_specs = [a_spec, b_spec] + ([add_spec] if has_add else [])
    args = [a, b] + ([add] if has_add else [])
    scratch = [pltpu.VMEM(acc_shape, F32)] if nk > 1 else []
    outs, comm_outs = _hosted_call(
        body, comm, name=name, grid=grid, in_specs=in_specs, out_specs=[o_spec],
        out_shape=[jax.ShapeDtypeStruct(o_shape, o_dtype)], scratch_shapes=scratch, args=args,
        sem=("parallel", "parallel", "arbitrary"), vmem=VMEM_BIG)
    return outs[0] if comm is None else (outs[0], comm_outs)


def _mm_nn(a, b, *, bm, bn, bk, o_dtype, name, add=None, comm=None):
    m, kd = a.shape
    n = b.shape[1]
    bm, bn, bk = _tile(m, bm, 8), _tile(n, bn), _tile(kd, bk)
    o_spec = pl.BlockSpec((bm, bn), lambda i, j, k: (i, j))
    return _mm(a, b, grid=(m // bm, n // bn, kd // bk),
               a_spec=pl.BlockSpec((bm, bk), lambda i, j, k: (i, k)),
               b_spec=pl.BlockSpec((bk, bn), lambda i, j, k: (k, j)),
               o_spec=o_spec, o_shape=(m, n), o_dtype=o_dtype, contract=((1,), (0,)), name=name,
               add=add, add_spec=o_spec, acc_shape=(bm, bn), comm=comm)


def _mm_nt(a, b, *, bm, bn, bk, o_dtype, name, add=None, b_col0=0, comm=None):
    m, kd = a.shape
    n = b.shape[0]
    bm, bn, bk = _tile(m, bm, 8), _tile(n, bn), _tile(math.gcd(kd, b_col0), bk)
    kb0 = b_col0 // bk
    o_spec = pl.BlockSpec((bm, bn), lambda i, j, k: (i, j))
    return _mm(a, b, grid=(m // bm, n // bn, kd // bk),
               a_spec=pl.BlockSpec((bm, bk), lambda i, j, k: (i, k)),
               b_spec=pl.BlockSpec((bn, bk), lambda i, j, k: (j, kb0 + k)),
               o_spec=o_spec, o_shape=(m, n), o_dtype=o_dtype, contract=((1,), (1,)), name=name,
               add=add, add_spec=o_spec, acc_shape=(bm, bn), comm=comm)


def _mm_nt_sum(terms, *, bm, bn, bk, name, add=None, comm=None):
    m = terms[0][0].shape[0]
    n = terms[0][1].shape[0]
    bm, bn = _tile(m, bm, 8), _tile(n, bn)
    nt = (((1,), (1,)), ((), ()))
    plan, groups, start = [], [], 0
    for a, b, col0 in terms:
        kd = a.shape[1]
        tk = _tile(math.gcd(kd, col0), bk)
        steps = kd // tk
        last = groups[-1] if groups else None
        if last is not None and last[0] is b and last[4] == tk and (last[3] + last[2]) * tk == col0:
            last[2] += steps
        else:
            groups.append([b, start, steps, col0 // tk, tk])
        plan.append((start, steps, len(groups) - 1))
        start += steps
    nk = start
    nterm, ngroup, has_add = len(terms), len(groups), add is not None

    def body(*refs):
        a_refs, b_refs = refs[0:nterm], refs[nterm:nterm + ngroup]
        add_ref = refs[nterm + ngroup] if has_add else None
        o_ref, acc = refs[nterm + ngroup + has_add], refs[nterm + ngroup + has_add + 1]
        k = pl.program_id(2)
        for t, (s0, steps, grp) in enumerate(plan):
            @pl.when((k >= s0) & (k < s0 + steps))
            def _():
                part = lax.dot_general(a_refs[t][...], b_refs[grp][...], nt, preferred_element_type=F32)

                @pl.when(k == 0)
                def _():
                    acc[...] = part

                @pl.when(k > 0)
                def _():
                    acc[...] += part

        @pl.when(k == nk - 1)
        def _():
            o_ref[...] = acc[...] + add_ref[...] if has_add else acc[...]

    def a_spec(tk, s0, steps):
        return pl.BlockSpec((bm, tk), lambda i, j, k: (i, jnp.clip(k - s0, 0, steps - 1)))

    def b_spec(tk, s0, steps, off):
        return pl.BlockSpec((bn, tk), lambda i, j, k: (j, off + jnp.clip(k - s0, 0, steps - 1)))

    o_spec = pl.BlockSpec((bm, bn), lambda i, j, k: (i, j))
    in_specs = [a_spec(groups[grp][4], s0, steps) for s0, steps, grp in plan]
    in_specs += [b_spec(tk, s0, steps, cb0) for _, s0, steps, cb0, tk in groups]
    args = [a for a, _, _ in terms] + [grp[0] for grp in groups]
    if has_add:
        in_specs.append(o_spec)
        args.append(add)
    outs, comm_outs = _hosted_call(
        body, comm, name=name, grid=(m // bm, n // bn, nk), in_specs=in_specs,
        out_specs=[o_spec], out_shape=[jax.ShapeDtypeStruct((m, n), F32)],
        scratch_shapes=[pltpu.VMEM((bm, bn), F32)], args=args,
        sem=("parallel", "parallel", "arbitrary"), vmem=VMEM_BIG)
    return outs[0] if comm is None else (outs[0], comm_outs)


def _mm_tn(a, b, *, bm, bn, bk, o_dtype, name, comm=None):
    kd, m = a.shape
    n = b.shape[1]
    bm, bn, bk = _tile(m, bm), _tile(n, bn), _tile(kd, bk, 8)
    return _mm(a, b, grid=(m // bm, n // bn, kd // bk),
               a_spec=pl.BlockSpec((bk, bm), lambda i, j, k: (k, i)),
               b_spec=pl.BlockSpec((bk, bn), lambda i, j, k: (k, j)),
               o_spec=pl.BlockSpec((bm, bn), lambda i, j, k: (i, j)),
               o_shape=(m, n), o_dtype=o_dtype, contract=((0,), (0,)), name=name, acc_shape=(bm, bn), comm=comm)


def _branch_full(w8):
    kb, ds = w8.shape[0] // N_DEV, w8.shape[1]
    return w8.reshape(N_DEV, kb, ds).transpose(1, 0, 2).reshape(kb, N_DEV * ds)


def _branch_shards(g):
    kb, ds = g.shape[0], g.shape[1] // N_DEV
    return g.reshape(kb, N_DEV, ds).transpose(1, 0, 2).reshape(N_DEV * kb, ds)


def _headnorm_fwd(src, c0, width, bw, hd, gain, nflag, head_major, name):
    rows = src.shape[0]
    bm = _tile(rows, 2048 if bw <= 256 else 1024, 16)
    bd = _block_diag(hd)
    cb0 = c0 // bw

    def body(x_ref, g_ref, f_ref, bd_ref, o_ref):
        xv = x_ref[...].astype(F32)
        ss = _seg_sum(xv * xv, bd_ref[...])
        rstd = lax.rsqrt(ss * (1.0 / hd) + EPS)
        y = (xv * jnp.where(f_ref[...] > 0.0, rstd, 1.0) * g_ref[...]).astype(BF16)
        if head_major:
            for h in range(bw // HEAD_DIM):
                o_ref[h] = y[:, h * HEAD_DIM:(h + 1) * HEAD_DIM]
        else:
            o_ref[...] = y

    vec_spec = pl.BlockSpec((1, bw), lambda i, t: (0, t))
    if head_major:
        hpb = bw // HEAD_DIM
        out_spec = pl.BlockSpec((hpb, bm, HEAD_DIM), lambda i, t: (t, i, 0))
        out_shape = jax.ShapeDtypeStruct((width // HEAD_DIM, rows, HEAD_DIM), BF16)
    else:
        out_spec = pl.BlockSpec((bm, bw), lambda i, t: (i, t))
        out_shape = jax.ShapeDtypeStruct((rows, width), BF16)
    return pl.pallas_call(
        body, name=name, grid=(rows // bm, width // bw),
        in_specs=[pl.BlockSpec((bm, bw), lambda i, t: (i, cb0 + t)), vec_spec, vec_spec,
                  pl.BlockSpec((LANES, LANES), lambda i, t: (0, 0))],
        out_specs=out_spec, out_shape=out_shape,
        compiler_params=_params(("parallel", "parallel")),
    )(src, gain, nflag, bd)


def _headnorm_bwd(src, c0, width, bw, hd, gain, nflag, dyn, target, t0, name):
    rows = src.shape[0]
    bm = _tile(rows, 2048 if bw <= 256 else 1024, 16)
    bd = _block_diag(hd)
    cb0 = c0 // bw
    tb0 = t0 // bw
    aliased = target is not None

    def body(*refs):
        if aliased:
            x_ref, dy_ref, g_ref, f_ref, bd_ref, _, o_ref, dg_ref = refs
        else:
            x_ref, dy_ref, g_ref, f_ref, bd_ref, o_ref, dg_ref = refs
        i = pl.program_id(1)
        xv = x_ref[...].astype(F32)
        dyv = dy_ref[...]
        bdv = bd_ref[...]
        rstd = lax.rsqrt(_seg_sum(xv * xv, bdv) * (1.0 / hd) + EPS)
        xhat = xv * rstd
        g = dyv * g_ref[...]
        mean = _seg_sum(g * xhat, bdv) * (1.0 / hd)
        dx = jnp.where(f_ref[...] > 0.0, rstd * (g - xhat * mean), g)
        o_ref[...] = dx.astype(BF16)
        part = jnp.sum((dyv * xhat).reshape(bm // 8, 8, bw), axis=0)

        @pl.when(i == 0)
        def _():
            dg_ref[...] = part

        @pl.when(i > 0)
        def _():
            dg_ref[...] += part

    vec_spec = pl.BlockSpec((1, bw), lambda t, i: (0, t))
    in_specs = [pl.BlockSpec((bm, bw), lambda t, i: (i, cb0 + t)), pl.BlockSpec((bm, bw), lambda t, i: (i, t)),
                vec_spec, vec_spec, pl.BlockSpec((LANES, LANES), lambda t, i: (0, 0))]
    args = [src, dyn, gain, nflag, bd]
    aliases = {}
    if aliased:
        in_specs.append(pl.BlockSpec(memory_space=pl.ANY))
        args.append(target)
        aliases = {5: 0}
        o_shape = jax.ShapeDtypeStruct(target.shape, BF16)
    else:
        o_shape = jax.ShapeDtypeStruct((rows, width), BF16)
    out, dg = pl.pallas_call(
        body, name=name, grid=(width // bw, rows // bm), in_specs=in_specs,
        out_specs=[pl.BlockSpec((bm, bw), lambda t, i: (i, tb0 + t)), pl.BlockSpec((8, bw), lambda t, i: (0, t))],
        out_shape=[o_shape, jax.ShapeDtypeStruct((8, width), F32)],
        input_output_aliases=aliases,
        compiler_params=_params(("parallel", "arbitrary")),
    )(*args)
    return out, dg


def _fox_prep(pfb, bpad, name):
    s = pfb.shape[0]

    def body(p_ref, b_ref, c_ref):
        z = p_ref[...] + b_ref[...]
        logf = jnp.minimum(z, 0.0) - jnp.log(1.0 + jnp.exp(-jnp.abs(z)))
        x = logf.T[0:16, :]
        lane = lax.broadcasted_iota(jnp.int32, (16, s), 1)
        sh = 1
        while sh < s:
            x = x + jnp.where(lane >= sh, pltpu.roll(x, sh, 1), 0.0)
            sh *= 2
        c_ref[...] = x

    return pl.pallas_call(
        body, name=name, grid=(1,),
        in_specs=[pl.BlockSpec((s, FB_PAD), lambda i: (0, 0)), pl.BlockSpec((1, FB_PAD), lambda i: (0, 0))],
        out_specs=pl.BlockSpec((16, s), lambda i: (0, 0)),
        out_shape=jax.ShapeDtypeStruct((16, s), F32),
        compiler_params=_params(("arbitrary",)),
    )(pfb, bpad)


def _fox_prep_bwd(pfb, bpad, dct, name):
    s = pfb.shape[0]

    def body(p_ref, b_ref, dc_ref, df_ref, db_ref):
        zt = (p_ref[...] + b_ref[...]).T[0:16, :]
        y = dc_ref[...]
        lane = lax.broadcasted_iota(jnp.int32, (16, s), 1)
        sh = 1
        while sh < s:
            y = y + jnp.where(lane < s - sh, pltpu.roll(y, s - sh, 1), 0.0)
            sh *= 2
        dz = y * _sigmoid(-zt)
        db_ref[...] = jnp.broadcast_to(jnp.sum(dz, axis=1, keepdims=True), (16, FB_PAD))
        full = jnp.concatenate([dz, jnp.zeros((FB_PAD - 16, s), F32)], axis=0)
        df_ref[...] = full.T.astype(BF16)

    return pl.pallas_call(
        body, name=name, grid=(1,),
        in_specs=[pl.BlockSpec((s, FB_PAD), lambda i: (0, 0)), pl.BlockSpec((1, FB_PAD), lambda i: (0, 0)),
                  pl.BlockSpec((16, s), lambda i: (0, 0))],
        out_specs=[pl.BlockSpec((s, FB_PAD), lambda i: (0, 0)), pl.BlockSpec((16, FB_PAD), lambda i: (0, 0))],
        out_shape=[jax.ShapeDtypeStruct((s, FB_PAD), BF16), jax.ShapeDtypeStruct((16, FB_PAD), F32)],
        compiler_params=_params(("arbitrary",)),
    )(pfb, bpad, dct)


def _swa_window(n):
    ws = pl.multiple_of(jnp.maximum(n * WINDOW - WINDOW, 0), WINDOW)
    qi = lax.broadcasted_iota(jnp.int32, (WINDOW, 2 * WINDOW), 0)
    kj = lax.broadcasted_iota(jnp.int32, (WINDOW, 2 * WINDOW), 1)
    rel = qi + (n * WINDOW - ws) - kj
    valid = (rel >= 0) & (rel < WINDOW)
    return ws, valid, rel.astype(F32)


def _attn_a_fwd(qkv, sinks, slopes, name):
    s = qkv.shape[1]
    nb = s // WINDOW
    smem = pl.BlockSpec(memory_space=pltpu.SMEM)

    def body(sink_ref, slope_ref, q_ref, k_ref, v_ref, o_ref, lse_ref):
        n = pl.program_id(0)
        ws, valid, relf = _swa_window(n)
        outs = []
        for h in range(A_Q_HEADS):
            kvh = h // A_GROUP
            kw = k_ref[kvh, pl.ds(ws, 2 * WINDOW), :]
            vw = v_ref[kvh, pl.ds(ws, 2 * WINDOW), :]
            sc = lax.dot_general(q_ref[h], kw, (((1,), (1,)), ((), ())), preferred_element_type=F32)
            sc = jnp.where(valid, sc - slope_ref[h] * relf, NEG)
            sink = sink_ref[h]
            m = jnp.maximum(jnp.max(sc, axis=1, keepdims=True), sink)
            p = jnp.exp(sc - m)
            denom = jnp.sum(p, axis=1, keepdims=True) + jnp.exp(sink - m)
            pn = (p / denom).astype(BF16)
            outs.append(jnp.dot(pn, vw, preferred_element_type=F32))
            lse_ref[h] = jnp.broadcast_to(m + jnp.log(denom), (WINDOW, HEAD_DIM))
        o_ref[...] = jnp.concatenate(outs, axis=1)

    return pl.pallas_call(
        body, name=name, grid=(nb,),
        in_specs=[smem, smem,
                  pl.BlockSpec((A_Q_HEADS, WINDOW, HEAD_DIM), lambda n: (0, n, 0)),
                  pl.BlockSpec((A_KV_HEADS, s, HEAD_DIM), lambda n: (A_GROUP, 0, 0)),
                  pl.BlockSpec((A_KV_HEADS, s, HEAD_DIM), lambda n: (A_GROUP + 1, 0, 0))],
        out_specs=[pl.BlockSpec((WINDOW, A_WIDTH), lambda n: (n, 0)),
                   pl.BlockSpec((A_Q_HEADS, WINDOW, HEAD_DIM), lambda n: (0, n, 0))],
        out_shape=[jax.ShapeDtypeStruct((s, A_WIDTH), F32), jax.ShapeDtypeStruct((A_Q_HEADS, s, HEAD_DIM), F32)],
        compiler_params=_params(("parallel",), VMEM_BIG),
    )(sinks, slopes, qkv, qkv, qkv)


def _attn_a_bwd(qkv, do, lse, dd, sinks, slopes, name, comm=None):
    s = qkv.shape[1]
    nb = s // WINDOW
    smem = pl.BlockSpec(memory_space=pltpu.SMEM)
    last = nb - 1

    def body(sink_ref, slope_ref, q_ref, k_ref, v_ref, do_ref, lse_ref, dd_ref, dq_ref, dkv_ref, ds_ref, carry):
        n = pl.program_id(0)

        @pl.when(n == 0)
        def _():
            carry[...] = jnp.zeros(carry.shape, F32)
            ds_ref[...] = jnp.zeros(ds_ref.shape, F32)

        @pl.when(n < nb)
        def _():
            ws, valid, relf = _swa_window(n)
            dqs = []
            dkw = [None] * A_KV_HEADS
            dvw = [None] * A_KV_HEADS
            for h in range(A_Q_HEADS):
                kvh = h // A_GROUP
                qh = q_ref[h]
                doh = do_ref[h]
                kw = k_ref[kvh, pl.ds(ws, 2 * WINDOW), :]
                vw = v_ref[kvh, pl.ds(ws, 2 * WINDOW), :]
                lse_h = lse_ref[h]
                dd_h = dd_ref[h]
                sc = lax.dot_general(qh, kw, (((1,), (1,)), ((), ())), preferred_element_type=F32)
                sc = jnp.where(valid, sc - slope_ref[h] * relf, NEG)
                p = jnp.exp(sc - lse_h[:, 0:1])
                dp = lax.dot_general(doh, vw, (((1,), (1,)), ((), ())), preferred_element_type=F32)
                dsc = (p * (dp - dd_h[:, 0:1])).astype(BF16)
                pb = p.astype(BF16)
                dqs.append(jnp.dot(dsc, kw, preferred_element_type=F32))
                dk_h = jnp.dot(qh.T, dsc, preferred_element_type=F32)
                dv_h = jnp.dot(doh.T, pb, preferred_element_type=F32)
                dkw[kvh] = dk_h if dkw[kvh] is None else dkw[kvh] + dk_h
                dvw[kvh] = dv_h if dvw[kvh] is None else dvw[kvh] + dv_h
                psink = jnp.exp(sink_ref[h] - lse_h)
                ds_ref[h] += jnp.sum((-psink * dd_h).reshape(WINDOW // 8, 8, HEAD_DIM), axis=0)
            dq_ref[...] = jnp.concatenate(dqs, axis=1)
            win = jnp.concatenate(dkw + dvw, axis=0)
            first = win[:, 0:WINDOW]
            second = win[:, WINDOW:2 * WINDOW]
            dkv_ref[...] = (carry[...] + first).T
            carry[...] = jnp.where(n == 0, first, second)

        @pl.when(n == nb)
        def _():
            dkv_ref[...] = carry[...].T

    hm = lambda heads: pl.BlockSpec((heads, WINDOW, HEAD_DIM), lambda n: (0, jnp.minimum(n, last), 0))
    res = lambda blk: pl.BlockSpec((A_KV_HEADS, s, HEAD_DIM), lambda n: (blk, 0, 0))
    outs, comm_outs = _hosted_call(
        body, comm, name=name, grid=(nb + 1,),
        in_specs=[smem, smem, hm(A_Q_HEADS), res(A_GROUP), res(A_GROUP + 1), hm(A_Q_HEADS), hm(A_Q_HEADS), hm(A_Q_HEADS)],
        out_specs=[pl.BlockSpec((WINDOW, A_WIDTH), lambda n: (jnp.minimum(n, last), 0)),
                   pl.BlockSpec((WINDOW, 2 * A_KV_WIDTH), lambda n: (jnp.maximum(n - 1, 0), 0)),
                   pl.BlockSpec((A_Q_HEADS, 8, HEAD_DIM), lambda n: (0, 0, 0))],
        out_shape=[jax.ShapeDtypeStruct((s, A_WIDTH), F32), jax.ShapeDtypeStruct((s, 2 * A_KV_WIDTH), F32),
                   jax.ShapeDtypeStruct((A_Q_HEADS, 8, HEAD_DIM), F32)],
        scratch_shapes=[pltpu.VMEM((2 * A_KV_WIDTH, WINDOW), F32)],
        args=[sinks, slopes, qkv, qkv, qkv, do, lse, dd], sem=("arbitrary",), vmem=VMEM_BIG)
    return outs[0], outs[1], outs[2], comm_outs


def _attn_b_fwd(qkv, c3, name, comm=None):
    heads, s = qkv.shape[0] // 3, qkv.shape[1]
    hpairs = heads // 2
    bq = min(512, s)
    nq = s // bq
    nt = (((1,), (1,)), ((), ()))

    def body(q_ref, k_ref, v_ref, c_ref, o_ref, lse_ref, m_scr, l_scr, acc_scr):
        i = pl.program_id(1)
        r0 = pl.multiple_of(i * bq, bq)
        row = lax.broadcasted_iota(jnp.int32, (bq, bq), 0)
        col = lax.broadcasted_iota(jnp.int32, (bq, bq), 1)
        m_scr[...] = jnp.full((2, bq, LANES), NEG, F32)
        l_scr[...] = jnp.zeros((2, bq, LANES), F32)
        acc_scr[...] = jnp.zeros((2, bq, HEAD_DIM), F32)

        def step(j, masked):
            k0 = pl.multiple_of(j * bq, bq)
            for h2 in range(2):
                kv = k_ref[h2, pl.ds(k0, bq), :]
                vv = v_ref[h2, pl.ds(k0, bq), :]
                cq0 = c_ref[h2, :, pl.ds(r0, LANES)][:, 0:1]
                sc = lax.dot_general(q_ref[h2], kv, nt, preferred_element_type=F32)
                sc = sc + (cq0 - c_ref[h2, :, pl.ds(k0, bq)])
                if masked:
                    sc = jnp.where(col <= row, sc, NEG)
                m_prev = m_scr[h2]
                m_new = jnp.maximum(m_prev, jnp.max(sc, axis=1, keepdims=True))
                alpha = jnp.exp(m_prev - m_new)
                p = jnp.exp(sc - m_new[:, 0:1])
                l_scr[h2] = alpha * l_scr[h2] + jnp.sum(p, axis=1, keepdims=True)
                p_hi = p.astype(BF16)
                p_lo = (p - p_hi.astype(F32)).astype(BF16)
                pv = jnp.dot(p_hi, vv, preferred_element_type=F32) + jnp.dot(p_lo, vv, preferred_element_type=F32)
                acc_scr[h2] = acc_scr[h2] * alpha[:, 0:HEAD_DIM] + pv
                m_scr[h2] = m_new

        def loop_body(j, carry):
            step(j, False)
            return carry

        lax.fori_loop(0, i, loop_body, 0)
        step(i, True)
        outs = []
        for h2 in range(2):
            l = l_scr[h2]
            outs.append(acc_scr[h2] / l[:, 0:HEAD_DIM])
            lse_ref[h2] = (m_scr[h2] + jnp.log(l))[:, 0:HEAD_DIM]
        o_ref[...] = jnp.concatenate(outs, axis=1)

    res = lambda off: pl.BlockSpec((2, s, HEAD_DIM), lambda hp, i: (off + hp, 0, 0))
    outs, comm_outs = _hosted_call(
        body, comm, name=name, grid=(hpairs, nq),
        in_specs=[pl.BlockSpec((2, bq, HEAD_DIM), lambda hp, i: (hp, i, 0)), res(hpairs), res(2 * hpairs),
                  pl.BlockSpec((2, 1, s), lambda hp, i: (hp, 0, 0))],
        out_specs=[pl.BlockSpec((bq, 2 * HEAD_DIM), lambda hp, i: (i, hp)),
                   pl.BlockSpec((2, bq, HEAD_DIM), lambda hp, i: (hp, i, 0))],
        out_shape=[jax.ShapeDtypeStruct((s, heads * HEAD_DIM), F32), jax.ShapeDtypeStruct((heads, s, HEAD_DIM), F32)],
        scratch_shapes=[pltpu.VMEM((2, bq, LANES), F32), pltpu.VMEM((2, bq, LANES), F32), pltpu.VMEM((2, bq, HEAD_DIM), F32)],
        args=[qkv, qkv, qkv, c3], sem=("parallel", "parallel"), vmem=VMEM_BIG)
    return outs[0], outs[1], comm_outs


def _attn_b_bwd(qkv, do, lse, dd, c3, name, comm=None):
    heads, s = qkv.shape[0] // 3, qkv.shape[1]
    hpairs = heads // 2
    bq = min(512, s)
    nq = s // bq
    nt = (((1,), (1,)), ((), ()))
    tn = (((0,), (0,)), ((), ()))
    grid = (heads // 2, nq)

    def body(q_ref, k_ref, v_ref, do_ref, lse_ref, dd_ref, c_ref, dq_ref, dk_ref, dv_ref, dc_ref,
             dq_scr, dk_scr, dv_scr, dc_scr):
        j = pl.program_id(1)
        k0 = pl.multiple_of(j * bq, bq)
        row = lax.broadcasted_iota(jnp.int32, (bq, bq), 0)
        col = lax.broadcasted_iota(jnp.int32, (bq, bq), 1)

        @pl.when(j == 0)
        def _():
            dq_scr[...] = jnp.zeros(dq_scr.shape, F32)

        dk_scr[...] = jnp.zeros((2, HEAD_DIM, bq), F32)
        dv_scr[...] = jnp.zeros((2, HEAD_DIM, bq), F32)
        dc_scr[...] = jnp.zeros((2, 1, bq), F32)
        k_t = [k_ref[h2].T for h2 in range(2)]

        def step(i, masked):
            r0 = pl.multiple_of(i * bq, bq)
            for h2 in range(2):
                kv = k_ref[h2]
                vv = v_ref[h2]
                qv = q_ref[h2, pl.ds(r0, bq), :]
                dov = do_ref[h2, pl.ds(r0, bq), :]
                lse_v = lse_ref[h2, pl.ds(r0, bq), :][:, 0:1]
                dd_v = dd_ref[h2, pl.ds(r0, bq), :][:, 0:1]
                cq0 = c_ref[h2, :, pl.ds(r0, LANES)][:, 0:1]
                sc = lax.dot_general(qv, kv, nt, preferred_element_type=F32) + (cq0 - c_ref[h2, :, pl.ds(k0, bq)])
                if masked:
                    sc = jnp.where(col <= row, sc, NEG)
                p = jnp.exp(sc - lse_v)
                dp = lax.dot_general(dov, vv, nt, preferred_element_type=F32)
                dsc = p * (dp - dd_v)
                dsb = dsc.astype(BF16)
                dv_scr[h2] += jnp.dot(dov.T, p.astype(BF16), preferred_element_type=F32)
                dk_scr[h2] += jnp.dot(qv.T, dsb, preferred_element_type=F32)
                dq_scr[h2, :, pl.ds(r0, bq)] += jnp.dot(k_t[h2], dsb.T, preferred_element_type=F32)
                dc_scr[h2] -= jnp.sum(dsc, axis=0, keepdims=True)

        def loop_body(i, carry):
            step(i, False)
            return carry

        step(j, True)
        lax.fori_loop(j + 1, nq, loop_body, 0)
        dc_ref[...] = dc_scr[...]
        dk_ref[...] = jnp.concatenate([dk_scr[0].T, dk_scr[1].T], axis=1)
        dv_ref[...] = jnp.concatenate([dv_scr[0].T, dv_scr[1].T], axis=1)

        @pl.when(j == nq - 1)
        def _():
            dq_ref[...] = jnp.concatenate([dq_scr[0].T, dq_scr[1].T], axis=1)

    res = pl.BlockSpec((2, s, HEAD_DIM), lambda hp, j: (hp, 0, 0))
    blk = lambda off: pl.BlockSpec((2, bq, HEAD_DIM), lambda hp, j: (off + hp, j, 0))
    tm = jax.ShapeDtypeStruct((s, heads * HEAD_DIM), F32)
    in_specs = [res, blk(hpairs), blk(2 * hpairs), res, res, res, pl.BlockSpec((2, 1, s), lambda hp, j: (hp, 0, 0))]
    out_specs = [pl.BlockSpec((s, 2 * HEAD_DIM), lambda hp, j: (0, hp)),
                 pl.BlockSpec((bq, 2 * HEAD_DIM), lambda hp, j: (j, hp)),
                 pl.BlockSpec((bq, 2 * HEAD_DIM), lambda hp, j: (j, hp)),
                 pl.BlockSpec((2, 1, bq), lambda hp, j: (hp, 0, j))]
    out_shape = [tm, tm, tm, jax.ShapeDtypeStruct((heads, 1, s), F32)]
    scratch = [pltpu.VMEM((2, HEAD_DIM, s), F32), pltpu.VMEM((2, HEAD_DIM, bq), F32),
               pltpu.VMEM((2, HEAD_DIM, bq), F32), pltpu.VMEM((2, 1, bq), F32)]
    outs, comm_outs = _hosted_call(
        body, comm, name=name, grid=grid, in_specs=in_specs, out_specs=out_specs, out_shape=out_shape,
        scratch_shapes=scratch, args=[qkv, qkv, qkv, do, lse, dd, c3], sem=("parallel", "arbitrary"), vmem=VMEM_BIG)
    return outs[0], outs[1], outs[2], outs[3], comm_outs


def _attn_c_probs(qh, mkh):
    sc = lax.dot_general(qh, mkh, (((1,), (1,)), ((), ())), preferred_element_type=F32) * (C_HEAD_DIM ** -0.5)
    p = jnp.exp(sc - jnp.max(sc, axis=1, keepdims=True))
    return p / jnp.sum(p, axis=1, keepdims=True)


def _attn_c_fwd(q, mkv, name):
    s = q.shape[0]
    m = mkv.shape[0]
    bq = _tile(s, 512, 8)

    def body(q_ref, mk_ref, mv_ref, o_ref):
        outs = []
        for h in range(C_HEADS):
            sl = slice(h * C_HEAD_DIM, (h + 1) * C_HEAD_DIM)
            pn = _attn_c_probs(q_ref[:, sl], mk_ref[:, sl]).astype(BF16)
            outs.append(jnp.dot(pn, mv_ref[:, sl], preferred_element_type=F32))
        o_ref[...] = jnp.concatenate(outs, axis=1)

    return pl.pallas_call(
        body, name=name, grid=(s // bq,),
        in_specs=[pl.BlockSpec((bq, C_WIDTH), lambda i: (i, 0)), pl.BlockSpec((m, C_WIDTH), lambda i: (0, 0)),
                  pl.BlockSpec((m, C_WIDTH), lambda i: (0, 1))],
        out_specs=pl.BlockSpec((bq, C_WIDTH), lambda i: (i, 0)),
        out_shape=jax.ShapeDtypeStruct((s, C_WIDTH), F32),
        compiler_params=_params(("parallel",)),
    )(q, mkv, mkv)


def _attn_c_bwd(q, mkv, do, name):
    s = q.shape[0]
    m = mkv.shape[0]
    bq = _tile(s, 512, 8)
    tn = (((0,), (0,)), ((), ()))

    def body(q_ref, mk_ref, mv_ref, do_ref, dq_ref, dm_ref):
        i = pl.program_id(0)

        @pl.when(i == 0)
        def _():
            dm_ref[...] = jnp.zeros(dm_ref.shape, F32)

        dqs = []
        for h in range(C_HEADS):
            sl = slice(h * C_HEAD_DIM, (h + 1) * C_HEAD_DIM)
            qh, mkh, mvh, doh = q_ref[:, sl], mk_ref[:, sl], mv_ref[:, sl], do_ref[:, sl]
            pn = _attn_c_probs(qh, mkh)
            dp = lax.dot_general(doh, mvh, (((1,), (1,)), ((), ())), preferred_element_type=F32)
            dsc = (pn * (dp - jnp.sum(pn * dp, axis=1, keepdims=True)) * (C_HEAD_DIM ** -0.5)).astype(BF16)
            dqs.append(jnp.dot(dsc, mkh, preferred_element_type=F32))
            dm_ref[:, sl] += lax.dot_general(dsc, qh, tn, preferred_element_type=F32)
            sv = slice(C_WIDTH + h * C_HEAD_DIM, C_WIDTH + (h + 1) * C_HEAD_DIM)
            dm_ref[:, sv] += lax.dot_general(pn.astype(BF16), doh, tn, preferred_element_type=F32)
        dq_ref[...] = jnp.concatenate(dqs, axis=1)

    row = pl.BlockSpec((bq, C_WIDTH), lambda i: (i, 0))
    return pl.pallas_call(
        body, name=name, grid=(s // bq,),
        in_specs=[row, pl.BlockSpec((m, C_WIDTH), lambda i: (0, 0)), pl.BlockSpec((m, C_WIDTH), lambda i: (0, 1)), row],
        out_specs=[row, pl.BlockSpec((m, 2 * C_WIDTH), lambda i: (0, 0))],
        out_shape=[jax.ShapeDtypeStruct((s, C_WIDTH), F32), jax.ShapeDtypeStruct((m, 2 * C_WIDTH), F32)],
        compiler_params=_params(("arbitrary",)),
    )(q, mkv, mkv, do)


def _gate_fwd(y, proj, zc0, bw, name):
    rows, width = y.shape
    bm = _tile(rows, 2048 if bw <= 256 else 1024, 16)
    cb0 = zc0 // bw

    def body(y_ref, z_ref, o_ref):
        z = z_ref[...].astype(F32)
        o_ref[...] = (y_ref[...] * (z * _sigmoid(z))).astype(BF16)

    return pl.pallas_call(
        body, name=name, grid=(rows // bm, width // bw),
        in_specs=[pl.BlockSpec((bm, bw), lambda i, t: (i, t)), pl.BlockSpec((bm, bw), lambda i, t: (i, cb0 + t))],
        out_specs=pl.BlockSpec((bm, bw), lambda i, t: (i, t)),
        out_shape=jax.ShapeDtypeStruct((rows, width), BF16),
        compiler_params=_params(("parallel", "parallel")),
    )(y, proj)


def _gate_bwd(dsv, y, proj, zc0, bw, dproj, t0, head_major, name):
    rows, width = y.shape
    bm = _tile(rows, 2048 if bw <= 256 else 1024, 16)
    cb0 = zc0 // bw
    tb0 = t0 // bw
    bd = _block_diag(HEAD_DIM)
    hpb = bw // HEAD_DIM

    def body(*refs):
        if head_major:
            ds_ref, y_ref, z_ref, bd_ref, _, dp_ref, dy_ref, dd_ref = refs
        else:
            ds_ref, y_ref, z_ref, _, dp_ref, dy_ref = refs
        z = z_ref[...].astype(F32)
        sig = _sigmoid(z)
        dsx = ds_ref[...]
        yv = y_ref[...]
        dy = dsx * (z * sig)
        dp_ref[...] = (dsx * yv * (sig * (1.0 + z * (1.0 - sig)))).astype(BF16)
        if head_major:
            dyb = dy.astype(BF16)
            dd = _seg_sum(dyb.astype(F32) * yv, bd_ref[...])
            for h in range(hpb):
                sl = slice(h * HEAD_DIM, (h + 1) * HEAD_DIM)
                dy_ref[h] = dyb[:, sl]
                dd_ref[h] = dd[:, sl]
        else:
            dy_ref[...] = dy.astype(BF16)

    tile = pl.BlockSpec((bm, bw), lambda i, t: (i, t))
    ztile = pl.BlockSpec((bm, bw), lambda i, t: (i, cb0 + t))
    ttile = pl.BlockSpec((bm, bw), lambda i, t: (i, tb0 + t))
    any_spec = pl.BlockSpec(memory_space=pl.ANY)
    dp_shape = jax.ShapeDtypeStruct(dproj.shape, BF16)
    if head_major:
        hm_spec = pl.BlockSpec((hpb, bm, HEAD_DIM), lambda i, t: (t, i, 0))
        nh = width // HEAD_DIM
        outs = pl.pallas_call(
            body, name=name, grid=(rows // bm, width // bw),
            in_specs=[tile, tile, ztile, pl.BlockSpec((LANES, LANES), lambda i, t: (0, 0)), any_spec],
            out_specs=[ttile, hm_spec, hm_spec],
            out_shape=[dp_shape, jax.ShapeDtypeStruct((nh, rows, HEAD_DIM), BF16),
                       jax.ShapeDtypeStruct((nh, rows, HEAD_DIM), F32)],
            input_output_aliases={4: 0},
            compiler_params=_params(("parallel", "parallel")),
        )(dsv, y, proj, bd, dproj)
        return outs[0], outs[1], outs[2]
    outs = pl.pallas_call(
        body, name=name, grid=(rows // bm, width // bw),
        in_specs=[tile, tile, ztile, any_spec],
        out_specs=[ttile, tile],
        out_shape=[dp_shape, jax.ShapeDtypeStruct((rows, width), BF16)],
        input_output_aliases={3: 0},
        compiler_params=_params(("parallel", "parallel")),
    )(dsv, y, proj, dproj)
    return outs[0], outs[1], None


def _merge_fwd(proj, ua, ub, uc, name):
    rows, d = ua.shape
    bm = _tile(rows, 1024, 16)
    bw = _tile(d, 512)
    g0 = COL_GATE // bw
    gstep = d // bw

    def body(la_ref, lb_ref, lc_ref, ua_ref, ub_ref, uc_ref, o_ref, ga_ref, gb_ref, gc_ref):
        y = None
        for l_ref, u_ref, g_ref in ((la_ref, ua_ref, ga_ref), (lb_ref, ub_ref, gb_ref), (lc_ref, uc_ref, gc_ref)):
            g = _sigmoid(l_ref[...].astype(F32))
            g_ref[...] = g.astype(BF16)
            term = g * u_ref[...].astype(F32)
            y = term if y is None else y + term
        o_ref[...] = y.astype(BF16)

    tile = pl.BlockSpec((bm, bw), lambda i, t: (i, t))
    gate = lambda b: pl.BlockSpec((bm, bw), lambda i, t: (i, g0 + b * gstep + t))
    shape = jax.ShapeDtypeStruct((rows, d), BF16)
    return pl.pallas_call(
        body, name=name, grid=(rows // bm, d // bw),
        in_specs=[gate(0), gate(1), gate(2), tile, tile, tile],
        out_specs=[tile] * 4, out_shape=[shape] * 4,
        compiler_params=_params(("parallel", "parallel")),
    )(proj, proj, proj, ua, ub, uc)


def _merge_bwd(dym, us, gs, name):
    rows, d = dym.shape
    bm = _tile(rows, 256, 16)

    def body(dy_ref, ua_ref, ub_ref, uc_ref, ga_ref, gb_ref, gc_ref, dg_ref, da_ref, db_ref, dc_ref):
        dyv = dy_ref[...]
        for b, (u_ref, g_ref, du_ref) in enumerate(((ua_ref, ga_ref, da_ref), (ub_ref, gb_ref, db_ref), (uc_ref, gc_ref, dc_ref))):
            g = g_ref[...].astype(F32)
            du_ref[...] = (g * dyv).astype(BF16)
            dg_ref[:, b * d:(b + 1) * d] = (dyv * u_ref[...].astype(F32) * g * (1.0 - g)).astype(BF16)

    tile = pl.BlockSpec((bm, d), lambda i: (i, 0))
    shape = jax.ShapeDtypeStruct((rows, d), BF16)
    outs = pl.pallas_call(
        body, name=name, grid=(rows // bm,),
        in_specs=[tile] * 7,
        out_specs=[pl.BlockSpec((bm, 3 * d), lambda i: (i, 0)), tile, tile, tile],
        out_shape=[jax.ShapeDtypeStruct((rows, 3 * d), BF16), shape, shape, shape],
        compiler_params=_params(("parallel",), VMEM_BIG),
    )(dym, *us, *gs)
    return outs[0], outs[1], outs[2], outs[3]


def _out_proj_loss(ym, wo, x, target, name):
    m, d = x.shape
    bm, bn = _tile(m, 1024, 16), _tile(d, 1024)
    grid = (m // bm, d // bn)

    def body(a_ref, b_ref, x_ref, t_ref, dy_ref, dyb_ref, l_ref):
        first, _ = _grid_edges(grid)
        y = jnp.dot(a_ref[...], b_ref[...], preferred_element_type=F32) + x_ref[...]
        diff = y - t_ref[...]
        dy = diff * (1.0 / d)
        dy_ref[...] = dy
        dyb_ref[...] = dy.astype(BF16)
        sq = diff * diff
        part = sq[:, 0:LANES]
        for c in range(1, bn // LANES):
            part = part + sq[:, c * LANES:(c + 1) * LANES]
        part = jnp.sum(part.reshape(bm // 8, 8, LANES), axis=0)

        @pl.when(first)
        def _():
            l_ref[...] = part

        @pl.when(jnp.logical_not(first))
        def _():
            l_ref[...] += part

    tile = pl.BlockSpec((bm, bn), lambda i, j: (i, j))
    return pl.pallas_call(
        body, name=name, grid=grid,
        in_specs=[pl.BlockSpec((bm, d), lambda i, j: (i, 0)), pl.BlockSpec((d, bn), lambda i, j: (0, j)), tile, tile],
        out_specs=[tile, tile, pl.BlockSpec((8, LANES), lambda i, j: (0, 0))],
        out_shape=[jax.ShapeDtypeStruct((m, d), F32), jax.ShapeDtypeStruct((m, d), BF16),
                   jax.ShapeDtypeStruct((8, LANES), F32)],
        compiler_params=_params(("arbitrary", "arbitrary"), VMEM_BIG),
    )(ym, wo, x, target)


def _row(vec, reps=1):
    return jnp.tile(vec.reshape(1, -1).astype(F32), (1, reps))


def _local_step(x, mem, target, small, wg, shards=None):
    s, d = x.shape
    dist = shards is not None
    wg = dict(wg)
    ones = lambda n: jnp.ones((1, n), F32)
    zeros = lambda n: jnp.zeros((1, n), F32)
    scale_ab = HEAD_DIM ** -0.5
    split8 = lambda g: g.reshape(N_DEV, g.shape[0] // N_DEV, g.shape[1])
    flat8 = lambda g: g.reshape(g.shape[0] * g.shape[1], g.shape[2])
    gather = lambda names: _Comm("gather", [shards[n] for n in names]) if dist else None
    g = {}

    def scatter(names):
        return _Comm("scatter", [split8(g[n]) for n in names]) if dist else None

    def hosted(result, names, store):
        if not dist:
            return result
        out, got = result
        store.update(zip(names, got))
        return out

    hn = _rmsnorm_fwd(x, small["norm_gain"], "rms_x_fwd")
    got = {}
    proj = hosted(_mm_nn(hn, wg["qkv"], bm=1024, bn=1024, bk=d, o_dtype=BF16, name="proj_qkv",
                         comm=gather(("wa", "wb", "wc"))), ("wa", "wb", "wc"), got)
    wg.update({n: flat8(a) for n, a in got.items()})
    pfb = _mm_nn(hn, wg["wf"], bm=1024, bn=FB_PAD, bk=d, o_dtype=F32, name="proj_fb")
    mn = _rmsnorm_fwd(mem, small["mem_norm_gain"], "rms_mem_fwd")
    mkv = _mm_nn(mn, wg["wk"], bm=256, bn=1024, bk=d, o_dtype=F32, name="mem_kv")

    gain_a = jnp.concatenate([_row(small["q_gain_a"], A_Q_HEADS) * scale_ab, _row(small["k_gain_a"], A_KV_HEADS), ones(A_KV_WIDTH)], axis=1)
    flag_a = jnp.concatenate([ones(A_WIDTH + A_KV_WIDTH), zeros(A_KV_WIDTH)], axis=1)
    qkv_a = _headnorm_fwd(proj, COL_QA, 1280, 1280, HEAD_DIM, gain_a, flag_a, True, "hn_a_fwd")
    gain_b = jnp.concatenate([_row(small["q_gain_b"], B_HEADS) * scale_ab, _row(small["k_gain_b"], B_HEADS), ones(B_WIDTH)], axis=1)
    flag_b = jnp.concatenate([ones(2 * B_WIDTH), zeros(B_WIDTH)], axis=1)
    qkv_b = _headnorm_fwd(proj, COL_QB, 2304, 256, HEAD_DIM, gain_b, flag_b, True, "hn_b_fwd")
    gain_cq = _row(small["q_gain_c"], C_HEADS)
    q_c = _headnorm_fwd(proj, COL_QC, C_WIDTH, C_WIDTH, C_HEAD_DIM, gain_cq, ones(C_WIDTH), False, "hn_cq_fwd")
    gain_ck = jnp.concatenate([_row(small["k_gain_c"], C_HEADS), ones(C_WIDTH)], axis=1)
    flag_ck = jnp.concatenate([ones(C_WIDTH), zeros(C_WIDTH)], axis=1)
    mkvn = _headnorm_fwd(mkv, 0, 2 * C_WIDTH, 2 * C_WIDTH, C_HEAD_DIM, gain_ck, flag_ck, False, "hn_ck_fwd")


    bpad = jnp.pad(small["b_forget"].reshape(1, -1), ((0, 0), (0, FB_PAD - B_HEADS)))
    c16 = _fox_prep(pfb, bpad, "fox_prep")
    c3 = c16[0:B_HEADS].reshape(B_HEADS, 1, s)

    sinks = small["sinks_a"].reshape(-1)
    slopes = jnp.exp2(-8.0 * jnp.arange(1, A_Q_HEADS + 1, dtype=F32) / A_Q_HEADS)
    y_a, lse_a = _attn_a_fwd(qkv_a, sinks, slopes, "attn_a_fwd")
    y_b, lse_b, got_zg = _attn_b_fwd(qkv_b, c3, "attn_b_fwd", comm=gather(("zg",)))
    if dist:
        wg["zg"] = flat8(got_zg[0])
    y_c = _attn_c_fwd(q_c, mkvn, "attn_c_fwd")

    got = {}
    pzg = hosted(_mm_nn(hn, wg["zg"], bm=1024, bn=1024, bk=d, o_dtype=BF16, name="proj_zg", comm=gather(("wo",))),
                 ("wo",), got)
    wg.update({n: flat8(a) for n, a in got.items()})

    s_a = _gate_fwd(y_a, pzg, COL_ZA, 256, "gate_a_fwd")
    s_b = _gate_fwd(y_b, pzg, COL_ZB, 256, "gate_b_fwd")
    s_c = _gate_fwd(y_c, pzg, COL_ZC, 512, "gate_c_fwd")
    w_a, w_b, w_c = _branch_full(wg["wa"]), _branch_full(wg["wb"]), _branch_full(wg["wc"])
    u_a = _mm_nn(s_a, w_a, bm=1024, bn=2048, bk=A_WIDTH, o_dtype=BF16, name="branch_a_fwd")
    u_b = _mm_nn(s_b, w_b, bm=1024, bn=2048, bk=B_WIDTH, o_dtype=BF16, name="branch_b_fwd")
    u_c = _mm_nn(s_c, w_c, bm=1024, bn=2048, bk=C_WIDTH, o_dtype=BF16, name="branch_c_fwd")
    ym, gate_a, gate_b, gate_c = _merge_fwd(pzg, u_a, u_b, u_c, "merge_fwd")
    dy, dyb, lpart = _out_proj_loss(ym, wg["wo"], x, target, "out_proj_loss")
    loss = 0.5 / d * jnp.sum(lpart)

    dym = _mm_nt(dyb, wg["wo"], bm=1024, bn=1024, bk=d, o_dtype=F32, name="out_proj_bwd_act")
    g["wo"] = _mm_tn(ym, dyb, bm=512, bn=1024, bk=s, o_dtype=BF16, name="out_proj_bwd_w")

    dgate, du_a, du_b, du_c = _merge_bwd(dym, (u_a, u_b, u_c), (gate_a, gate_b, gate_c), "merge_bwd")
    parts = {}
    g["wm_g"] = hosted(_mm_tn(hn, dgate, bm=512, bn=1024, bk=s, o_dtype=BF16, name="proj_gate_bwd_w",
                              comm=scatter(("wo",))), ("wo",), parts)

    ds_a = _mm_nt(du_a, w_a, bm=1024, bn=A_WIDTH, bk=d, o_dtype=F32, name="branch_a_bwd_act")
    ds_b = _mm_nt(du_b, w_b, bm=1024, bn=B_WIDTH, bk=d, o_dtype=F32, name="branch_b_bwd_act")
    ds_c = _mm_nt(du_c, w_c, bm=1024, bn=C_WIDTH, bk=d, o_dtype=F32, name="branch_c_bwd_act")
    g["wa"] = _branch_shards(_mm_tn(s_a, du_a, bm=A_WIDTH, bn=1024, bk=s, o_dtype=BF16, name="branch_a_bwd_w"))
    g["wb"] = _branch_shards(_mm_tn(s_b, du_b, bm=B_WIDTH, bn=1024, bk=s, o_dtype=BF16, name="branch_b_bwd_w"))
    g["wc"] = _branch_shards(_mm_tn(s_c, du_c, bm=C_WIDTH, bn=1024, bk=s, o_dtype=BF16, name="branch_c_bwd_w"))

    dz = lax.empty((s, W_Z), BF16)
    dz, do_a, dd_a = _gate_bwd(ds_a, y_a, pzg, COL_ZA, 256, dz, COL_ZA, True, "gate_a_bwd")
    dz, do_b, dd_b = _gate_bwd(ds_b, y_b, pzg, COL_ZB, 256, dz, COL_ZB, True, "gate_b_bwd")
    dz, do_c, _ = _gate_bwd(ds_c, y_c, pzg, COL_ZC, 512, dz, COL_ZC, False, "gate_c_bwd")
    g["wm_z"] = _mm_tn(hn, dz, bm=512, bn=1024, bk=s, o_dtype=BF16, name="proj_z_bwd_w")

    names = ("wa", "wb", "wc")
    dq_a, dkv_a, dsink, got = _attn_a_bwd(qkv_a, do_a, lse_a, dd_a, sinks, slopes, "attn_a_bwd", comm=scatter(names))
    parts.update(zip(names, got))
    names = ("wm_g", "wm_z")
    dq_b, dk_b, dv_b, dc3, got = _attn_b_bwd(qkv_b, do_b, lse_b, dd_b, c3, "attn_b_bwd", comm=scatter(names))
    parts.update(zip(names, got))
    dq_c, dmkvn = _attn_c_bwd(q_c, mkvn, do_c, "attn_c_bwd")

    dqkv = lax.empty((s, W_QKV), BF16)
    dqkv, dg_qa = _headnorm_bwd(proj, COL_QA, A_WIDTH, 256, HEAD_DIM, gain_a[:, 0:768], flag_a[:, 0:768], dq_a, dqkv, COL_QA, "hn_qa_bwd")
    dqkv, dg_kva = _headnorm_bwd(proj, COL_KA, 512, 256, HEAD_DIM, gain_a[:, 768:1280], flag_a[:, 768:1280], dkv_a, dqkv, COL_KA, "hn_kva_bwd")
    dqkv, dg_qb = _headnorm_bwd(proj, COL_QB, B_WIDTH, 256, HEAD_DIM, gain_b[:, 0:768], flag_b[:, 0:768], dq_b, dqkv, COL_QB, "hn_qb_bwd")
    dqkv, dg_kb = _headnorm_bwd(proj, COL_KB, B_WIDTH, 256, HEAD_DIM, gain_b[:, 768:1536], flag_b[:, 768:1536], dk_b, dqkv, COL_KB, "hn_kb_bwd")
    dqkv, _ = _headnorm_bwd(proj, COL_VB, B_WIDTH, 256, HEAD_DIM, gain_b[:, 1536:2304], flag_b[:, 1536:2304], dv_b, dqkv, COL_VB, "hn_vb_bwd")
    dqkv, dg_qc = _headnorm_bwd(proj, COL_QC, C_WIDTH, 512, C_HEAD_DIM, gain_cq, ones(C_WIDTH), dq_c, dqkv, COL_QC, "hn_qc_bwd")
    dmkv, dg_kc = _headnorm_bwd(mkv, 0, 2 * C_WIDTH, 2 * C_WIDTH, C_HEAD_DIM, gain_ck, flag_ck, dmkvn, None, 0, "hn_kc_bwd")

    dct = jnp.pad(dc3.reshape(B_HEADS, s), ((0, 16 - B_HEADS), (0, 0)))
    dfb, dbf = _fox_prep_bwd(pfb, bpad, dct, "fox_prep_bwd")

    dmn = _mm_nt(dmkv, wg["wk"], bm=256, bn=1024, bk=1024, o_dtype=F32, name="mem_kv_bwd_act")
    g["wk"] = _mm_tn(mn, dmkv, bm=512, bn=1024, bk=mem.shape[0], o_dtype=BF16, name="mem_kv_bwd_w")
    _, dg_mem = _rmsnorm_bwd(mem, dmn, small["mem_norm_gain"], None, "rms_mem_bwd")

    g["wm_qkv"] = _mm_tn(hn, dqkv, bm=512, bn=1024, bk=s, o_dtype=BF16, name="proj_qkv_bwd_w")
    g["wf"] = _mm_tn(hn, dfb, bm=512, bn=FB_PAD, bk=s, o_dtype=BF16, name="proj_fb_bwd_w")
    half = Q_SPLIT
    g["wm_q1"], g["wm_q2"] = g["wm_qkv"][:, 0:half], g["wm_qkv"][:, half:W_QKV]
    names = ("wm_q1",)
    dhn = hosted(_mm_nt_sum([(dqkv, wg["qkv"], 0), (dfb, wg["wf"], 0)], bm=1024, bn=1024, bk=2048,
                            name="proj_qkv_bwd_act", comm=scatter(names)), names, parts)
    names = ("wm_q2", "wf", "wk")
    dhn = hosted(_mm_nt_sum([(dz, wg["zg"], COL_ZA), (dgate, wg["zg"], COL_GATE)], bm=1024, bn=1024, bk=2048,
                            name="proj_zg_bwd_act", add=dhn, comm=scatter(names)), names, parts)
    if dist:
        g = parts
    grad_x, dg_x = _rmsnorm_bwd(x, dhn, small["norm_gain"], dy, "rms_x_bwd")

    fold = lambda part, heads, hd: jnp.sum(jnp.sum(part, axis=0).reshape(heads, hd), axis=0).reshape(1, hd)
    small_grads = {
        "norm_gain": jnp.sum(dg_x, axis=0).reshape(1, d),
        "mem_norm_gain": jnp.sum(dg_mem, axis=0).reshape(1, d),
        "b_forget": dbf[0:B_HEADS, 0].reshape(1, B_HEADS),
        "q_gain_a": fold(dg_qa, A_Q_HEADS, HEAD_DIM) * scale_ab,
        "k_gain_a": fold(dg_kva[:, 0:A_KV_WIDTH], A_KV_HEADS, HEAD_DIM),
        "sinks_a": (jnp.sum(dsink, axis=(1, 2)) * (1.0 / HEAD_DIM)).reshape(1, A_Q_HEADS),
        "q_gain_b": fold(dg_qb, B_HEADS, HEAD_DIM) * scale_ab,
        "k_gain_b": fold(dg_kb, B_HEADS, HEAD_DIM),
        "q_gain_c": fold(dg_qc, C_HEADS, C_HEAD_DIM),
        "k_gain_c": fold(dg_kc[:, 0:C_WIDTH], C_HEADS, C_HEAD_DIM),
    }
    return loss, grad_x, small_grads, g


def _coords():
    return lax.axis_index("x"), lax.axis_index("y"), lax.axis_index("c")


def _all_gather(shards, name):
    n = len(shards)

    def body(*refs):
        ins = refs[0:n]
        outs = refs[n:2 * n]
        send_sems, recv_sems, local_sems = refs[2 * n:2 * n + 3]
        x, y, c = _coords()
        me, sibling = (x, y, c), (x, y, 1 - c)
        chips = [(1 - x, y), (x, 1 - y), (1 - x, 1 - y)]
        idx = lambda p: 4 * p[0] + 2 * p[1] + p[2]

        def copy(a, k, block, to, src=None):
            slot = outs[a].at[idx(block)]
            return pltpu.make_async_remote_copy(
                src_ref=slot if src is None else src, dst_ref=slot,
                send_sem=send_sems.at[a, k], recv_sem=recv_sems.at[a, k], device_id=to, device_id_type=MESH)

        mine = [pltpu.make_async_copy(ins[a], outs[a].at[idx(me)], local_sems.at[a]) for a in range(n)]
        for cp in mine:
            cp.start()
        first = []
        for a in range(n):
            first.append(copy(a, 0, me, sibling, src=ins[a]))
            first += [copy(a, 1 + j, me, (*chip, c), src=ins[a]) for j, chip in enumerate(chips)]
        for cp in first:
            cp.start()
        passed = []
        for j, chip in enumerate(chips):
            for a in range(n):
                copy(a, 1 + j, (*chip, c), me).wait_recv()
                fwd = copy(a, 4 + j, (*chip, c), sibling)
                fwd.start()
                passed.append(fwd)
        for a in range(n):
            copy(a, 0, sibling, me).wait_recv()
            for j, chip in enumerate(chips):
                copy(a, 4 + j, (*chip, 1 - c), me).wait_recv()
        for cp in first + passed:
            cp.wait_send()
        for cp in mine:
            cp.wait()

    any_spec = pl.BlockSpec(memory_space=pl.ANY)
    return pl.pallas_call(
        body, name=name,
        in_specs=[any_spec] * n, out_specs=[any_spec] * n,
        out_shape=[jax.ShapeDtypeStruct((N_DEV,) + sh.shape, sh.dtype) for sh in shards],
        scratch_shapes=[pltpu.SemaphoreType.DMA((n, 7)), pltpu.SemaphoreType.DMA((n, 7)), pltpu.SemaphoreType.DMA((n,))],
    )(*shards)


def _all_reduce_small(vec, name):
    p = vec.shape[1]

    def body(v_ref, o_ref, gather, send_sems, recv_sems):
        x, y, c = _coords()
        my = 4 * x + 2 * y + c
        peers = [(x ^ ((k >> 2) & 1), y ^ ((k >> 1) & 1), c ^ (k & 1)) for k in range(1, N_DEV)]
        gather[my] = v_ref[...]
        sends = [pltpu.make_async_remote_copy(
            src_ref=v_ref, dst_ref=gather.at[my], send_sem=send_sems.at[k], recv_sem=recv_sems.at[k],
            device_id=peer, device_id_type=MESH) for k, peer in enumerate(peers)]
        for cp in sends:
            cp.start()
        for k, peer in enumerate(peers):
            pid = 4 * peer[0] + 2 * peer[1] + peer[2]
            pltpu.make_async_remote_copy(
                src_ref=v_ref, dst_ref=gather.at[pid], send_sem=send_sems.at[k], recv_sem=recv_sems.at[k],
                device_id=peer, device_id_type=MESH).wait_recv()
        for cp in sends:
            cp.wait_send()
        total = gather[0]
        for j in range(1, N_DEV):
            total = total + gather[j]
        o_ref[...] = total

    vm = pl.BlockSpec(memory_space=pltpu.VMEM)
    return pl.pallas_call(
        body, name=name, in_specs=[vm], out_specs=vm,
        out_shape=jax.ShapeDtypeStruct((8, p), F32),
        scratch_shapes=[pltpu.VMEM((N_DEV, 8, p), F32), pltpu.SemaphoreType.DMA((7,)), pltpu.SemaphoreType.DMA((7,))],
    )(vec)[0:1]


def _sum_parts(parts, name):
    _, rows, cols = parts.shape
    br = _tile(rows, 64, 16)

    def body(p_ref, o_ref):
        total = p_ref[0].astype(F32)
        for j in range(1, N_DEV):
            total = total + p_ref[j].astype(F32)
        o_ref[...] = total

    return pl.pallas_call(
        body, name=name, grid=(rows // br,),
        in_specs=[pl.BlockSpec((N_DEV, br, cols), lambda i: (0, i, 0))],
        out_specs=pl.BlockSpec((br, cols), lambda i: (i, 0)),
        out_shape=jax.ShapeDtypeStruct((rows, cols), F32),
        compiler_params=_params(("parallel",), VMEM_BIG),
    )(parts)


def _adamw(w, g, m, v, name, br=32):
    rows, cols = w.shape
    br = min(br, rows)
    c1 = 1.0 / (1.0 - ADAM_B1 ** ADAM_STEP)
    c2 = 1.0 / (1.0 - ADAM_B2 ** ADAM_STEP)

    def body(w_ref, g_ref, m_ref, v_ref, d_ref, nm_ref, nv_ref):
        gv = g_ref[...]
        nm = ADAM_B1 * m_ref[...] + (1.0 - ADAM_B1) * gv
        nv = ADAM_B2 * v_ref[...] + (1.0 - ADAM_B2) * (gv * gv)
        d_ref[...] = -ADAM_LR * ((nm * c1) / (jnp.sqrt(nv * c2) + ADAM_EPS) + ADAM_WD * w_ref[...])
        nm_ref[...] = nm
        nv_ref[...] = nv

    spec = pl.BlockSpec((br, cols), lambda i: (i, 0))
    shape = jax.ShapeDtypeStruct((rows, cols), F32)
    return pl.pallas_call(
        body, name=name, grid=(pl.cdiv(rows, br),), in_specs=[spec] * 4, out_specs=[spec] * 3, out_shape=[shape] * 3,
        compiler_params=_params(("parallel",), VMEM_BIG),
    )(w, g, m, v)


def _adamw_t(wt, g, mt, vt, name, br=1024):
    n, r = wt.shape
    c1 = 1.0 / (1.0 - ADAM_B1 ** ADAM_STEP)
    c2 = 1.0 / (1.0 - ADAM_B2 ** ADAM_STEP)

    def body(w_ref, g_ref, m_ref, v_ref, d_ref, nm_ref, nv_ref):
        gv = g_ref[...].T
        nm = ADAM_B1 * m_ref[...] + (1.0 - ADAM_B1) * gv
        nv = ADAM_B2 * v_ref[...] + (1.0 - ADAM_B2) * (gv * gv)
        d_ref[...] = -ADAM_LR * ((nm * c1) / (jnp.sqrt(nv * c2) + ADAM_EPS) + ADAM_WD * w_ref[...])
        nm_ref[...] = nm
        nv_ref[...] = nv

    spec = pl.BlockSpec((br, r), lambda i: (i, 0))
    shape = jax.ShapeDtypeStruct((n, r), F32)
    return pl.pallas_call(
        body, name=name, grid=(pl.cdiv(n, br),),
        in_specs=[spec, pl.BlockSpec((r, br), lambda i: (0, i)), spec, spec], out_specs=[spec] * 3, out_shape=[shape] * 3,
        compiler_params=_params(("parallel",), VMEM_BIG),
    )(wt, g, mt, vt)


def _adamw_parts(w, parts, m, v, name):
    rows, cols = w.shape
    br = _tile(rows, 32, 16)
    c1 = 1.0 / (1.0 - ADAM_B1 ** ADAM_STEP)
    c2 = 1.0 / (1.0 - ADAM_B2 ** ADAM_STEP)

    def body(w_ref, p_ref, m_ref, v_ref, g_ref, d_ref, nm_ref, nv_ref):
        gv = p_ref[0].astype(F32)
        for j in range(1, N_DEV):
            gv = gv + p_ref[j].astype(F32)
        nm = ADAM_B1 * m_ref[...] + (1.0 - ADAM_B1) * gv
        nv = ADAM_B2 * v_ref[...] + (1.0 - ADAM_B2) * (gv * gv)
        g_ref[...] = gv
        d_ref[...] = -ADAM_LR * ((nm * c1) / (jnp.sqrt(nv * c2) + ADAM_EPS) + ADAM_WD * w_ref[...])
        nm_ref[...] = nm
        nv_ref[...] = nv

    spec = pl.BlockSpec((br, cols), lambda i: (i, 0))
    shape = jax.ShapeDtypeStruct((rows, cols), F32)
    return pl.pallas_call(
        body, name=name, grid=(rows // br,),
        in_specs=[spec, pl.BlockSpec((N_DEV, br, cols), lambda i: (0, i, 0)), spec, spec],
        out_specs=[spec] * 4, out_shape=[shape] * 4,
        compiler_params=_params(("parallel",), VMEM_BIG),
    )(w, parts, m, v)


SMALL_NAMES = ("norm_gain", "mem_norm_gain", "b_forget", "q_gain_a", "k_gain_a", "sinks_a",
               "q_gain_b", "k_gain_b", "q_gain_c", "k_gain_c")
BIG_NAMES = ("w_in", "w_mem_kv", "w_branch_a", "w_branch_b", "w_branch_c", "w_out")
WEIGHT_ORDER = ("norm_gain", "mem_norm_gain", "w_in", "b_forget", "q_gain_a", "k_gain_a", "sinks_a", "q_gain_b",
                "k_gain_b", "q_gain_c", "k_gain_c", "w_mem_kv", "w_branch_a", "w_branch_b", "w_branch_c", "w_out")


def _pack_small(tree):
    flat = jnp.concatenate([tree[n].reshape(1, -1) for n in SMALL_NAMES], axis=1)
    pad = (-flat.shape[1]) % LANES
    return jnp.pad(flat, ((0, 0), (0, pad)))


def _unpack_small(flat, like):
    out, off = {}, 0
    for n in SMALL_NAMES:
        size = like[n].size
        out[n] = flat[:, off:off + size].reshape(like[n].shape)
        off += size
    return out


def kernel(x, mem, norm_gain, mem_norm_gain, w_in, b_forget, q_gain_a, k_gain_a, sinks_a, q_gain_b, k_gain_b, q_gain_c, k_gain_c, w_mem_kv, w_branch_a, w_branch_b, w_branch_c, w_out, loss_target, m_norm_gain, m_mem_norm_gain, m_w_in, m_b_forget, m_q_gain_a, m_k_gain_a, m_sinks_a, m_q_gain_b, m_k_gain_b, m_q_gain_c, m_k_gain_c, m_w_mem_kv, m_w_branch_a, m_w_branch_b, m_w_branch_c, m_w_out, v_norm_gain, v_mem_norm_gain, v_w_in, v_b_forget, v_q_gain_a, v_k_gain_a, v_sinks_a, v_q_gain_b, v_k_gain_b, v_q_gain_c, v_k_gain_c, v_w_mem_kv, v_w_branch_a, v_w_branch_b, v_w_branch_c, v_w_out):
    weights = dict(norm_gain=norm_gain, mem_norm_gain=mem_norm_gain, w_in=w_in, b_forget=b_forget, q_gain_a=q_gain_a,
                   k_gain_a=k_gain_a, sinks_a=sinks_a, q_gain_b=q_gain_b, k_gain_b=k_gain_b, q_gain_c=q_gain_c,
                   k_gain_c=k_gain_c, w_mem_kv=w_mem_kv, w_branch_a=w_branch_a, w_branch_b=w_branch_b,
                   w_branch_c=w_branch_c, w_out=w_out)
    mom_m = dict(norm_gain=m_norm_gain, mem_norm_gain=m_mem_norm_gain, w_in=m_w_in, b_forget=m_b_forget,
                 q_gain_a=m_q_gain_a, k_gain_a=m_k_gain_a, sinks_a=m_sinks_a, q_gain_b=m_q_gain_b, k_gain_b=m_k_gain_b,
                 q_gain_c=m_q_gain_c, k_gain_c=m_k_gain_c, w_mem_kv=m_w_mem_kv, w_branch_a=m_w_branch_a,
                 w_branch_b=m_w_branch_b, w_branch_c=m_w_branch_c, w_out=m_w_out)
    mom_v = dict(norm_gain=v_norm_gain, mem_norm_gain=v_mem_norm_gain, w_in=v_w_in, b_forget=v_b_forget,
                 q_gain_a=v_q_gain_a, k_gain_a=v_k_gain_a, sinks_a=v_sinks_a, q_gain_b=v_q_gain_b, k_gain_b=v_k_gain_b,
                 q_gain_c=v_q_gain_c, k_gain_c=v_k_gain_c, w_mem_kv=v_w_mem_kv, w_branch_a=v_w_branch_a,
                 w_branch_b=v_w_branch_b, w_branch_c=v_w_branch_c, w_out=v_w_out)
    wi = w_in[0]
    sh_qkv = jnp.concatenate([wi[:, a:b] for a, b in SRC_RANGES[0:3]], axis=1).astype(BF16)
    sh_zg = jnp.concatenate([wi[:, a:b] for a, b in SRC_RANGES[3:6]] + [wi[:, SRC_GATE:]], axis=1).astype(BF16)
    sh_wf = jnp.pad(wi[:, FB_SRC:FB_SRC + B_HEADS], ((0, 0), (0, FB_PAD - B_HEADS))).astype(BF16)
    shards = {"zg": sh_zg, "wo": w_out[0].astype(BF16), "wa": w_branch_a[0].astype(BF16),
              "wb": w_branch_b[0].astype(BF16), "wc": w_branch_c[0].astype(BF16)}
    first = ("qkv", "wf", "wk")
    full = _all_gather([sh_qkv, sh_wf, w_mem_kv[0].astype(BF16)], "weights_all_gather")
    wg = {kname: arr.reshape(arr.shape[0] * arr.shape[1], arr.shape[2]) for kname, arr in zip(first, full)}

    small = {n: weights[n] for n in SMALL_NAMES}
    loss_local, grad_x, small_g, parts = _local_step(x[0], mem[0], loss_target[0], small, wg, shards)

    grads, delta, new_m, new_v = {}, {}, {}, {}
    for n, kname in (("w_mem_kv", "wk"), ("w_out", "wo"), ("w_branch_a", "wa"), ("w_branch_b", "wb"), ("w_branch_c", "wc")):
        gsum, dlt, nm, nv = _adamw_parts(weights[n][0], parts[kname], mom_m[n][0], mom_v[n][0], "adamw_" + n)
        grads[n], delta[n], new_m[n], new_v[n] = gsum, dlt[None], nm[None], nv[None]
    g1, g2, gz, gf, gg = (_sum_parts(parts[k], "grad_sum_" + k) for k in ("wm_q1", "wm_q2", "wm_z", "wf", "wm_g"))
    half = Q_SPLIT
    g_in = jnp.concatenate([g1[:, COL_QA:COL_QB], gz[:, COL_ZA:COL_ZB], g1[:, COL_QB:half], g2[:, 0:COL_QC - half],
                            gz[:, COL_ZB:COL_ZC], gf[:, 0:B_HEADS], g2[:, COL_QC - half:W_QKV - half], gz[:, COL_ZC:W_Z], gg], axis=1)
    dlt, nm, nv = _adamw_t(w_in[0].T, g_in, m_w_in[0].T, v_w_in[0].T, "adamw_w_in")
    grads["w_in"], delta["w_in"], new_m["w_in"], new_v["w_in"] = g_in, dlt.T[None], nm.T[None], nv.T[None]

    packed = _pack_small(small_g)
    packed = jnp.concatenate([packed[:, :-1], loss_local.reshape(1, 1)], axis=1)
    reduced = _all_reduce_small(jnp.broadcast_to(packed, (8, packed.shape[1])), "small_all_reduce")
    grads.update(_unpack_small(reduced, small))
    loss = reduced[0, -1]

    pw, pm, pv = _pack_small(small), _pack_small({n: mom_m[n] for n in SMALL_NAMES}), _pack_small({n: mom_v[n] for n in SMALL_NAMES})
    rep8 = lambda a: jnp.broadcast_to(a, (8, a.shape[1]))
    dlt, nm, nv = _adamw(rep8(pw), rep8(reduced), rep8(pm), rep8(pv), "adamw_small")
    for tree, flat in ((delta, dlt), (new_m, nm), (new_v, nv)):
        tree.update(_unpack_small(flat[0:1], small))
    for n in BIG_NAMES:
        grads[n] = grads[n][None]
    return (loss, grad_x[None], *[grads[n] for n in WEIGHT_ORDER], *[delta[n] for n in WEIGHT_ORDER],
            *[new_m[n] for n in WEIGHT_ORDER], *[new_v[n] for n in WEIGHT_ORDER])
```

```python
import math

import jax
import jax.numpy as jnp
import numpy as np
from jax import lax
from jax.experimental import pallas as pl
from jax.experimental.pallas import tpu as pltpu

F32 = jnp.float32
BF16 = jnp.bfloat16

N_DEV = 8
HEAD_DIM = 64
A_Q_HEADS = 12
A_KV_HEADS = 4
A_GROUP = 3
B_HEADS = 12
C_HEADS = 4
C_HEAD_DIM = 128
WINDOW = 128
A_WIDTH = 768
A_KV_WIDTH = 256
B_WIDTH = 768
C_WIDTH = 512
EPS = 1e-6
NEG = -1e30

COL_QA, COL_KA, COL_VA = 0, 768, 1024
COL_QB, COL_KB, COL_VB = 1280, 2048, 2816
COL_QC = 3584
W_QKV = 4096
Q_SPLIT = 1024
COL_ZA, COL_ZB, COL_ZC = 0, 768, 1536
COL_GATE = W_Z = 2048
SRC_RANGES = ((0, 1280), (2048, 4352), (5132, 5644), (1280, 2048), (4352, 5120), (5644, 6156))
SRC_GATE = 6156
FB_SRC = 5120
FB_PAD = 128

ADAM_LR = 0.001
ADAM_B1 = 0.9
ADAM_B2 = 0.999
ADAM_EPS = 1e-08
ADAM_WD = 0.01
ADAM_STEP = 10

VMEM_BIG = 52 * 1024 * 1024
LANES = 128
MESH = pl.DeviceIdType.MESH


def _tile(n, pref, mult=128):
    if n <= pref:
        return n
    t = (pref // mult) * mult
    while t >= mult:
        if n % t == 0:
            return t
        t -= mult
    return n


def _params(sem=None, vmem=None):
    kw = {}
    if sem is not None:
        kw["dimension_semantics"] = sem
    if vmem is not None:
        kw["vmem_limit_bytes"] = vmem
    return pltpu.CompilerParams(**kw)


def _sigmoid(x):
    return 1.0 / (1.0 + jnp.exp(-x))


def _block_diag(hd):
    r = np.arange(LANES)
    return jnp.asarray((r[:, None] // hd) == (r[None, :] // hd), dtype=BF16)


def _seg_sum(t, bd):
    hi = t.astype(BF16)
    lo = (t - hi.astype(F32)).astype(BF16)
    outs = []
    for c in range(t.shape[1] // LANES):
        sl = slice(c * LANES, (c + 1) * LANES)
        outs.append(jnp.dot(hi[:, sl], bd, preferred_element_type=F32) + jnp.dot(lo[:, sl], bd, preferred_element_type=F32))
    return outs[0] if len(outs) == 1 else jnp.concatenate(outs, axis=1)


def _rmsnorm_fwd(x, gain, name):
    rows, d = x.shape
    bm = _tile(rows, 512, 8)

    def body(x_ref, g_ref, o_ref):
        xv = x_ref[...]
        ms = jnp.mean(xv * xv, axis=-1, keepdims=True)
        o_ref[...] = (xv * lax.rsqrt(ms + EPS) * g_ref[...]).astype(BF16)

    return pl.pallas_call(
        body, name=name, grid=(rows // bm,),
        in_specs=[pl.BlockSpec((bm, d), lambda i: (i, 0)), pl.BlockSpec((1, d), lambda i: (0, 0))],
        out_specs=pl.BlockSpec((bm, d), lambda i: (i, 0)),
        out_shape=jax.ShapeDtypeStruct((rows, d), BF16),
        compiler_params=_params(("parallel",)),
    )(x, gain)


def _rmsnorm_bwd(x, dhn, gain, dy, name):
    rows, d = x.shape
    bm = _tile(rows, 512, 8)
    with_dx = dy is not None

    def body(*refs):
        if with_dx:
            x_ref, dh_ref, g_ref, dy_ref, gx_ref, dg_ref = refs
        else:
            x_ref, dh_ref, g_ref, dg_ref = refs
        i = pl.program_id(0)
        xv = x_ref[...]
        rstd = lax.rsqrt(jnp.mean(xv * xv, axis=-1, keepdims=True) + EPS)
        xhat = xv * rstd
        dh = dh_ref[...]
        part = jnp.sum((dh * xhat).reshape(bm // 8, 8, d), axis=0)

        @pl.when(i == 0)
        def _():
            dg_ref[...] = part

        @pl.when(i > 0)
        def _():
            dg_ref[...] += part

        if with_dx:
            g = dh * g_ref[...]
            mean = jnp.mean(g * xhat, axis=-1, keepdims=True)
            gx_ref[...] = dy_ref[...] + rstd * (g - xhat * mean)

    row_spec = pl.BlockSpec((bm, d), lambda i: (i, 0))
    in_specs = [row_spec, row_spec, pl.BlockSpec((1, d), lambda i: (0, 0))]
    args = [x, dhn, gain]
    dg_spec = pl.BlockSpec((8, d), lambda i: (0, 0))
    dg_shape = jax.ShapeDtypeStruct((8, d), F32)
    if with_dx:
        in_specs.append(row_spec)
        args.append(dy)
        out_specs = [row_spec, dg_spec]
        out_shape = [jax.ShapeDtypeStruct((rows, d), F32), dg_shape]
    else:
        out_specs = [dg_spec]
        out_shape = [dg_shape]
    outs = pl.pallas_call(
        body, name=name, grid=(rows // bm,), in_specs=in_specs, out_specs=out_specs, out_shape=out_shape,
        compiler_params=_params(("arbitrary",), VMEM_BIG),
    )(*args)
    return outs if with_dx else (None, outs[0])


class _Comm:
    def __init__(self, kind, arrays):
        self.kind = kind
        self.arrays = list(arrays)
        self.n = len(self.arrays)

    def out_shapes(self):
        if self.kind == "gather":
            return [jax.ShapeDtypeStruct((N_DEV,) + a.shape, a.dtype) for a in self.arrays]
        return [jax.ShapeDtypeStruct(a.shape, a.dtype) for a in self.arrays]

    def scratch(self):
        return [pltpu.SemaphoreType.DMA((self.n, N_DEV - 1)), pltpu.SemaphoreType.DMA((self.n, N_DEV - 1)),
                pltpu.SemaphoreType.DMA((self.n,))]

    def _plan(self, ins, outs, sems, with_recvs):
        send_sems, recv_sems, local_sems = sems
        x, y, c = lax.axis_index("x"), lax.axis_index("y"), lax.axis_index("c")
        my = 4 * x + 2 * y + c
        gather = self.kind == "gather"
        local, sends, recvs = [], [], []
        for a in range(self.n):
            local.append(pltpu.make_async_copy(ins[a] if gather else ins[a].at[my], outs[a].at[my], local_sems.at[a]))
            for k in range(1, N_DEV):
                peer = (x ^ ((k >> 2) & 1), y ^ ((k >> 1) & 1), c ^ (k & 1))
                pid = 4 * peer[0] + 2 * peer[1] + peer[2]
                src = ins[a] if gather else ins[a].at[pid]
                sem = dict(send_sem=send_sems.at[a, k - 1], recv_sem=recv_sems.at[a, k - 1], device_id=peer, device_id_type=MESH)
                sends.append(pltpu.make_async_remote_copy(src_ref=src, dst_ref=outs[a].at[my], **sem))
                if with_recvs:
                    recvs.append(pltpu.make_async_remote_copy(src_ref=src, dst_ref=outs[a].at[pid], **sem))
        return local, sends, recvs

    def start(self, ins, outs, sems):
        local, sends, _ = self._plan(ins, outs, sems, False)
        for cp in local + sends:
            cp.start()

    def wait(self, ins, outs, sems):
        local, sends, recvs = self._plan(ins, outs, sems, True)
        for cp in recvs:
            cp.wait_recv()
        for cp in sends:
            cp.wait_send()
        for cp in local:
            cp.wait()


def _grid_edges(grid):
    first = last = None
    for ax, size in enumerate(grid):
        pid = pl.program_id(ax)
        f, l = pid == 0, pid == size - 1
        first = f if first is None else first & f
        last = l if last is None else last & l
    return first, last


def _hosted_call(body, comm, *, name, grid, in_specs, out_specs, out_shape, scratch_shapes, args, sem, vmem=None):
    in_specs, out_specs, out_shape, scratch_shapes = list(in_specs), list(out_specs), list(out_shape), list(scratch_shapes)
    if comm is None:
        res = pl.pallas_call(body, name=name, grid=grid, in_specs=in_specs, out_specs=out_specs, out_shape=out_shape,
                             scratch_shapes=scratch_shapes, compiler_params=_params(sem, vmem))(*args)
        return list(res), []
    n_in, n_out, n_scr, nc = len(in_specs), len(out_shape), len(scratch_shapes), comm.n

    def hosted(*refs):
        ins = refs[0:n_in]
        comm_in = refs[n_in:n_in + nc]
        outs = refs[n_in + nc:n_in + nc + n_out]
        comm_out = refs[n_in + nc + n_out:n_in + 2 * nc + n_out]
        scr = refs[n_in + 2 * nc + n_out:n_in + 2 * nc + n_out + n_scr]
        sems = refs[n_in + 2 * nc + n_out + n_scr:]
        first, last = _grid_edges(grid)

        @pl.when(first)
        def _():
            comm.start(comm_in, comm_out, sems)

        body(*ins, *outs, *scr)

        @pl.when(last)
        def _():
            comm.wait(comm_in, comm_out, sems)

    any_spec = pl.BlockSpec(memory_space=pl.ANY)
    res = pl.pallas_call(
        hosted, name=name, grid=grid, in_specs=in_specs + [any_spec] * nc, out_specs=out_specs + [any_spec] * nc,
        out_shape=out_shape + comm.out_shapes(), scratch_shapes=scratch_shapes + comm.scratch(),
        compiler_params=_params(("arbitrary",) * len(grid), vmem),
    )(*args, *comm.arrays)
    return list(res[0:n_out]), list(res[n_out:])


def _mm(a, b, *, grid, a_spec, b_spec, o_spec, o_shape, o_dtype, contract, name, add=None, add_spec=None, acc_shape=None,
        comm=None):
    nk = grid[2]
    has_add = add is not None

    def body(*refs):
        a_ref, b_ref = refs[0], refs[1]
        add_ref = refs[2] if has_add else None
        o_ref = refs[3] if has_add else refs[2]
        part = lax.dot_general(a_ref[...], b_ref[...], (contract, ((), ())), preferred_element_type=F32)
        if nk == 1:
            if has_add:
                part = part + add_ref[...]
            o_ref[...] = part.astype(o_dtype)
        else:
            acc = refs[-1]
            k = pl.program_id(2)

            @pl.when(k == 0)
            def _():
                acc[...] = part

            @pl.when(k > 0)
            def _():
                acc[...] += part

            @pl.when(k == nk - 1)
            def _():
                r = acc[...]
                if has_add:
                    r = r + add_ref[...]
                o_ref[...] = r.astype(o_dtype)

    in_specs = [a_spec, b_spec] + ([add_spec] if has_add else [])
    args = [a, b] + ([add] if has_add else [])
    scratch = [pltpu.VMEM(acc_shape, F32)] if nk > 1 else []
    outs, comm_outs = _hosted_call(
        body, comm, name=name, grid=grid, in_specs=in_specs, out_specs=[o_spec],
        out_shape=[jax.ShapeDtypeStruct(o_shape, o_dtype)], scratch_shapes=scratch, args=args,
        sem=("parallel", "parallel", "arbitrary"), vmem=VMEM_BIG)
    return outs[0] if comm is None else (outs[0], comm_outs)


def _mm_nn(a, b, *, bm, bn, bk, o_dtype, name, add=None, comm=None):
    m, kd = a.shape
    n = b.shape[1]
    bm, bn, bk = _tile(m, bm, 8), _tile(n, bn), _tile(kd, bk)
    o_spec = pl.BlockSpec((bm, bn), lambda i, j, k: (i, j))
    return _mm(a, b, grid=(m // bm, n // bn, kd // bk),
               a_spec=pl.BlockSpec((bm, bk), lambda i, j, k: (i, k)),
               b_spec=pl.BlockSpec((bk, bn), lambda i, j, k: (k, j)),
               o_spec=o_spec, o_shape=(m, n), o_dtype=o_dtype, contract=((1,), (0,)), name=name,
               add=add, add_spec=o_spec, acc_shape=(bm, bn), comm=comm)


def _mm_nt(a, b, *, bm, bn, bk, o_dtype, name, add=None, b_col0=0, comm=None):
    m, kd = a.shape
    n = b.shape[0]
    bm, bn, bk = _tile(m, bm, 8), _tile(n, bn), _tile(math.gcd(kd, b_col0), bk)
    kb0 = b_col0 // bk
    o_spec = pl.BlockSpec((bm, bn), lambda i, j, k: (i, j))
    return _mm(a, b, grid=(m // bm, n // bn, kd // bk),
               a_spec=pl.BlockSpec((bm, bk), lambda i, j, k: (i, k)),
               b_spec=pl.BlockSpec((bn, bk), lambda i, j, k: (j, kb0 + k)),
               o_spec=o_spec, o_shape=(m, n), o_dtype=o_dtype, contract=((1,), (1,)), name=name,
               add=add, add_spec=o_spec, acc_shape=(bm, bn), comm=comm)


def _mm_nt_sum(terms, *, bm, bn, bk, name, add=None, comm=None):
    m = terms[0][0].shape[0]
    n = terms[0][1].shape[0]
    bm, bn = _tile(m, bm, 8), _tile(n, bn)
    nt = (((1,), (1,)), ((), ()))
    plan, groups, start = [], [], 0
    for a, b, col0 in terms:
        kd = a.shape[1]
        tk = _tile(math.gcd(kd, col0), bk)
        steps = kd // tk
        last = groups[-1] if groups else None
        if last is not None and last[0] is b and last[4] == tk and (last[3] + last[2]) * tk == col0:
            last[2] += steps
        else:
            groups.append([b, start, steps, col0 // tk, tk])
        plan.append((start, steps, len(groups) - 1))
        start += steps
    nk = start
    nterm, ngroup, has_add = len(terms), len(groups), add is not None

    def body(*refs):
        a_refs, b_refs = refs[0:nterm], refs[nterm:nterm + ngroup]
        add_ref = refs[nterm + ngroup] if has_add else None
        o_ref, acc = refs[nterm + ngroup + has_add], refs[nterm + ngroup + has_add + 1]
        k = pl.program_id(2)
        for t, (s0, steps, grp) in enumerate(plan):
            @pl.when((k >= s0) & (k < s0 + steps))
            def _():
                part = lax.dot_general(a_refs[t][...], b_refs[grp][...], nt, preferred_element_type=F32)

                @pl.when(k == 0)
                def _():
                    acc[...] = part

                @pl.when(k > 0)
                def _():
                    acc[...] += part

        @pl.when(k == nk - 1)
        def _():
            o_ref[...] = acc[...] + add_ref[...] if has_add else acc[...]

    def a_spec(tk, s0, steps):
        return pl.BlockSpec((bm, tk), lambda i, j, k: (i, jnp.clip(k - s0, 0, steps - 1)))

    def b_spec(tk, s0, steps, off):
        return pl.BlockSpec((bn, tk), lambda i, j, k: (j, off + jnp.clip(k - s0, 0, steps - 1)))

    o_spec = pl.BlockSpec((bm, bn), lambda i, j, k: (i, j))
    in_specs = [a_spec(groups[grp][4], s0, steps) for s0, steps, grp in plan]
    in_specs += [b_spec(tk, s0, steps, cb0) for _, s0, steps, cb0, tk in groups]
    args = [a for a, _, _ in terms] + [grp[0] for grp in groups]
    if has_add:
        in_specs.append(o_spec)
        args.append(add)
    outs, comm_outs = _hosted_call(
        body, comm, name=name, grid=(m // bm, n // bn, nk), in_specs=in_specs,
        out_specs=[o_spec], out_shape=[jax.ShapeDtypeStruct((m, n), F32)],
        scratch_shapes=[pltpu.VMEM((bm, bn), F32)], args=args,
        sem=("parallel", "parallel", "arbitrary"), vmem=VMEM_BIG)
    return outs[0] if comm is None else (outs[0], comm_outs)


def _mm_tn(a, b, *, bm, bn, bk, o_dtype, name, comm=None):
    kd, m = a.shape
    n = b.shape[1]
    bm, bn, bk = _tile(m, bm), _tile(n, bn), _tile(kd, bk, 8)
    return _mm(a, b, grid=(m // bm, n // bn, kd // bk),
               a_spec=pl.BlockSpec((bk, bm), lambda i, j, k: (k, i)),
               b_spec=pl.BlockSpec((bk, bn), lambda i, j, k: (k, j)),
               o_spec=pl.BlockSpec((bm, bn), lambda i, j, k: (i, j)),
               o_shape=(m, n), o_dtype=o_dtype, contract=((0,), (0,)), name=name, acc_shape=(bm, bn), comm=comm)


def _branch_full(w8):
    kb, ds = w8.shape[0] // N_DEV, w8.shape[1]
    return w8.reshape(N_DEV, kb, ds).transpose(1, 0, 2).reshape(kb, N_DEV * ds)


def _branch_shards(g):
    kb, ds = g.shape[0], g.shape[1] // N_DEV
    return g.reshape(kb, N_DEV, ds).transpose(1, 0, 2).reshape(N_DEV * kb, ds)


def _headnorm_fwd(src, c0, width, bw, hd, gain, nflag, head_major, name):
    rows = src.shape[0]
    bm = _tile(rows, 2048 if bw <= 256 else 1024, 16)
    bd = _block_diag(hd)
    cb0 = c0 // bw

    def body(x_ref, g_ref, f_ref, bd_ref, o_ref):
        xv = x_ref[...].astype(F32)
        ss = _seg_sum(xv * xv, bd_ref[...])
        rstd = lax.rsqrt(ss * (1.0 / hd) + EPS)
        y = (xv * jnp.where(f_ref[...] > 0.0, rstd, 1.0) * g_ref[...]).astype(BF16)
        if head_major:
            for h in range(bw // HEAD_DIM):
                o_ref[h] = y[:, h * HEAD_DIM:(h + 1) * HEAD_DIM]
        else:
            o_ref[...] = y

    vec_spec = pl.BlockSpec((1, bw), lambda i, t: (0, t))
    if head_major:
        hpb = bw // HEAD_DIM
        out_spec = pl.BlockSpec((hpb, bm, HEAD_DIM), lambda i, t: (t, i, 0))
        out_shape = jax.ShapeDtypeStruct((width // HEAD_DIM, rows, HEAD_DIM), BF16)
    else:
        out_spec = pl.BlockSpec((bm, bw), lambda i, t: (i, t))
        out_shape = jax.ShapeDtypeStruct((rows, width), BF16)
    return pl.pallas_call(
        body, name=name, grid=(rows // bm, width // bw),
        in_specs=[pl.BlockSpec((bm, bw), lambda i, t: (i, cb0 + t)), vec_spec, vec_spec,
                  pl.BlockSpec((LANES, LANES), lambda i, t: (0, 0))],
        out_specs=out_spec, out_shape=out_shape,
        compiler_params=_params(("parallel", "parallel")),
    )(src, gain, nflag, bd)


def _headnorm_bwd(src, c0, width, bw, hd, gain, nflag, dyn, target, t0, name):
    rows = src.shape[0]
    bm = _tile(rows, 2048 if bw <= 256 else 1024, 16)
    bd = _block_diag(hd)
    cb0 = c0 // bw
    tb0 = t0 // bw
    aliased = target is not None

    def body(*refs):
        if aliased:
            x_ref, dy_ref, g_ref, f_ref, bd_ref, _, o_ref, dg_ref = refs
        else:
            x_ref, dy_ref, g_ref, f_ref, bd_ref, o_ref, dg_ref = refs
        i = pl.program_id(1)
        xv = x_ref[...].astype(F32)
        dyv = dy_ref[...]
        bdv = bd_ref[...]
        rstd = lax.rsqrt(_seg_sum(xv * xv, bdv) * (1.0 / hd) + EPS)
        xhat = xv * rstd
        g = dyv * g_ref[...]
        mean = _seg_sum(g * xhat, bdv) * (1.0 / hd)
        dx = jnp.where(f_ref[...] > 0.0, rstd * (g - xhat * mean), g)
        o_ref[...] = dx.astype(BF16)
        part = jnp.sum((dyv * xhat).reshape(bm // 8, 8, bw), axis=0)

        @pl.when(i == 0)
        def _():
            dg_ref[...] = part

        @pl.when(i > 0)
        def _():
            dg_ref[...] += part

    vec_spec = pl.BlockSpec((1, bw), lambda t, i: (0, t))
    in_specs = [pl.BlockSpec((bm, bw), lambda t, i: (i, cb0 + t)), pl.BlockSpec((bm, bw), lambda t, i: (i, t)),
                vec_spec, vec_spec, pl.BlockSpec((LANES, LANES), lambda t, i: (0, 0))]
    args = [src, dyn, gain, nflag, bd]
    aliases = {}
    if aliased:
        in_specs.append(pl.BlockSpec(memory_space=pl.ANY))
        args.append(target)
        aliases = {5: 0}
        o_shape = jax.ShapeDtypeStruct(target.shape, BF16)
    else:
        o_shape = jax.ShapeDtypeStruct((rows, width), BF16)
    out, dg = pl.pallas_call(
        body, name=name, grid=(width // bw, rows // bm), in_specs=in_specs,
        out_specs=[pl.BlockSpec((bm, bw), lambda t, i: (i, tb0 + t)), pl.BlockSpec((8, bw), lambda t, i: (0, t))],
        out_shape=[o_shape, jax.ShapeDtypeStruct((8, width), F32)],
        input_output_aliases=aliases,
        compiler_params=_params(("parallel", "arbitrary")),
    )(*args)
    return out, dg


def _fox_prep(pfb, bpad, name):
    s = pfb.shape[0]

    def body(p_ref, b_ref, c_ref):
        z = p_ref[...] + b_ref[...]
        logf = jnp.minimum(z, 0.0) - jnp.log(1.0 + jnp.exp(-jnp.abs(z)))
        x = logf.T[0:16, :]
        lane = lax.broadcasted_iota(jnp.int32, (16, s), 1)
        sh = 1
        while sh < s:
            x = x + jnp.where(lane >= sh, pltpu.roll(x, sh, 1), 0.0)
            sh *= 2
        c_ref[...] = x

    return pl.pallas_call(
        body, name=name, grid=(1,),
        in_specs=[pl.BlockSpec((s, FB_PAD), lambda i: (0, 0)), pl.BlockSpec((1, FB_PAD), lambda i: (0, 0))],
        out_specs=pl.BlockSpec((16, s), lambda i: (0, 0)),
        out_shape=jax.ShapeDtypeStruct((16, s), F32),
        compiler_params=_params(("arbitrary",)),
    )(pfb, bpad)


def _fox_prep_bwd(pfb, bpad, dct, name):
    s = pfb.shape[0]

    def body(p_ref, b_ref, dc_ref, df_ref, db_ref):
        zt = (p_ref[...] + b_ref[...]).T[0:16, :]
        y = dc_ref[...]
        lane = lax.broadcasted_iota(jnp.int32, (16, s), 1)
        sh = 1
        while sh < s:
            y = y + jnp.where(lane < s - sh, pltpu.roll(y, s - sh, 1), 0.0)
            sh *= 2
        dz = y * _sigmoid(-zt)
        db_ref[...] = jnp.broadcast_to(jnp.sum(dz, axis=1, keepdims=True), (16, FB_PAD))
        full = jnp.concatenate([dz, jnp.zeros((FB_PAD - 16, s), F32)], axis=0)
        df_ref[...] = full.T.astype(BF16)

    return pl.pallas_call(
        body, name=name, grid=(1,),
        in_specs=[pl.BlockSpec((s, FB_PAD), lambda i: (0, 0)), pl.BlockSpec((1, FB_PAD), lambda i: (0, 0)),
                  pl.BlockSpec((16, s), lambda i: (0, 0))],
        out_specs=[pl.BlockSpec((s, FB_PAD), lambda i: (0, 0)), pl.BlockSpec((16, FB_PAD), lambda i: (0, 0))],
        out_shape=[jax.ShapeDtypeStruct((s, FB_PAD), BF16), jax.ShapeDtypeStruct((16, FB_PAD), F32)],
        compiler_params=_params(("arbitrary",)),
    )(pfb, bpad, dct)


def _swa_window(n):
    ws = pl.multiple_of(jnp.maximum(n * WINDOW - WINDOW, 0), WINDOW)
    qi = lax.broadcasted_iota(jnp.int32, (WINDOW, 2 * WINDOW), 0)
    kj = lax.broadcasted_iota(jnp.int32, (WINDOW, 2 * WINDOW), 1)
    rel = qi + (n * WINDOW - ws) - kj
    valid = (rel >= 0) & (rel < WINDOW)
    return ws, valid, rel.astype(F32)


def _attn_a_fwd(qkv, sinks, slopes, name):
    s = qkv.shape[1]
    nb = s // WINDOW
    smem = pl.BlockSpec(memory_space=pltpu.SMEM)

    def body(sink_ref, slope_ref, q_ref, k_ref, v_ref, o_ref, lse_ref):
        n = pl.program_id(0)
        ws, valid, relf = _swa_window(n)
        outs = []
        for h in range(A_Q_HEADS):
            kvh = h // A_GROUP
            kw = k_ref[kvh, pl.ds(ws, 2 * WINDOW), :]
            vw = v_ref[kvh, pl.ds(ws, 2 * WINDOW), :]
            sc = lax.dot_general(q_ref[h], kw, (((1,), (1,)), ((), ())), preferred_element_type=F32)
            sc = jnp.where(valid, sc - slope_ref[h] * relf, NEG)
            sink = sink_ref[h]
            m = jnp.maximum(jnp.max(sc, axis=1, keepdims=True), sink)
            p = jnp.exp(sc - m)
            denom = jnp.sum(p, axis=1, keepdims=True) + jnp.exp(sink - m)
            pn = (p / denom).astype(BF16)
            outs.append(jnp.dot(pn, vw, preferred_element_type=F32))
            lse_ref[h] = jnp.broadcast_to(m + jnp.log(denom), (WINDOW, HEAD_DIM))
        o_ref[...] = jnp.concatenate(outs, axis=1)

    return pl.pallas_call(
        body, name=name, grid=(nb,),
        in_specs=[smem, smem,
                  pl.BlockSpec((A_Q_HEADS, WINDOW, HEAD_DIM), lambda n: (0, n, 0)),
                  pl.BlockSpec((A_KV_HEADS, s, HEAD_DIM), lambda n: (A_GROUP, 0, 0)),
                  pl.BlockSpec((A_KV_HEADS, s, HEAD_DIM), lambda n: (A_GROUP + 1, 0, 0))],
        out_specs=[pl.BlockSpec((WINDOW, A_WIDTH), lambda n: (n, 0)),
                   pl.BlockSpec((A_Q_HEADS, WINDOW, HEAD_DIM), lambda n: (0, n, 0))],
        out_shape=[jax.ShapeDtypeStruct((s, A_WIDTH), F32), jax.ShapeDtypeStruct((A_Q_HEADS, s, HEAD_DIM), F32)],
        compiler_params=_params(("parallel",), VMEM_BIG),
    )(sinks, slopes, qkv, qkv, qkv)


def _attn_a_bwd(qkv, do, lse, dd, sinks, slopes, name, comm=None):
    s = qkv.shape[1]
    nb = s // WINDOW
    smem = pl.BlockSpec(memory_space=pltpu.SMEM)
    last = nb - 1

    def body(sink_ref, slope_ref, q_ref, k_ref, v_ref, do_ref, lse_ref, dd_ref, dq_ref, dkv_ref, ds_ref, carry):
        n = pl.program_id(0)

        @pl.when(n == 0)
        def _():
            carry[...] = jnp.zeros(carry.shape, F32)
            ds_ref[...] = jnp.zeros(ds_ref.shape, F32)

        @pl.when(n < nb)
        def _():
            ws, valid, relf = _swa_window(n)
            dqs = []
            dkw = [None] * A_KV_HEADS
            dvw = [None] * A_KV_HEADS
            for h in range(A_Q_HEADS):
                kvh = h // A_GROUP
                qh = q_ref[h]
                doh = do_ref[h]
                kw = k_ref[kvh, pl.ds(ws, 2 * WINDOW), :]
                vw = v_ref[kvh, pl.ds(ws, 2 * WINDOW), :]
                lse_h = lse_ref[h]
                dd_h = dd_ref[h]
                sc = lax.dot_general(qh, kw, (((1,), (1,)), ((), ())), preferred_element_type=F32)
                sc = jnp.where(valid, sc - slope_ref[h] * relf, NEG)
                p = jnp.exp(sc - lse_h[:, 0:1])
                dp = lax.dot_general(doh, vw, (((1,), (1,)), ((), ())), preferred_element_type=F32)
                dsc = (p * (dp - dd_h[:, 0:1])).astype(BF16)
                pb = p.astype(BF16)
                dqs.append(jnp.dot(dsc, kw, preferred_element_type=F32))
                dk_h = jnp.dot(qh.T, dsc, preferred_element_type=F32)
                dv_h = jnp.dot(doh.T, pb, preferred_element_type=F32)
                dkw[kvh] = dk_h if dkw[kvh] is None else dkw[kvh] + dk_h
                dvw[kvh] = dv_h if dvw[kvh] is None else dvw[kvh] + dv_h
                psink = jnp.exp(sink_ref[h] - lse_h)
                ds_ref[h] += jnp.sum((-psink * dd_h).reshape(WINDOW // 8, 8, HEAD_DIM), axis=0)
            dq_ref[...] = jnp.concatenate(dqs, axis=1)
            win = jnp.concatenate(dkw + dvw, axis=0)
            first = win[:, 0:WINDOW]
            second = win[:, WINDOW:2 * WINDOW]
            dkv_ref[...] = (carry[...] + first).T
            carry[...] = jnp.where(n == 0, first, second)

        @pl.when(n == nb)
        def _():
            dkv_ref[...] = carry[...].T

    hm = lambda heads: pl.BlockSpec((heads, WINDOW, HEAD_DIM), lambda n: (0, jnp.minimum(n, last), 0))
    res = lambda blk: pl.BlockSpec((A_KV_HEADS, s, HEAD_DIM), lambda n: (blk, 0, 0))
    outs, comm_outs = _hosted_call(
        body, comm, name=name, grid=(nb + 1,),
        in_specs=[smem, smem, hm(A_Q_HEADS), res(A_GROUP), res(A_GROUP + 1), hm(A_Q_HEADS), hm(A_Q_HEADS), hm(A_Q_HEADS)],
        out_specs=[pl.BlockSpec((WINDOW, A_WIDTH), lambda n: (jnp.minimum(n, last), 0)),
                   pl.BlockSpec((WINDOW, 2 * A_KV_WIDTH), lambda n: (jnp.maximum(n - 1, 0), 0)),
                   pl.BlockSpec((A_Q_HEADS, 8, HEAD_DIM), lambda n: (0, 0, 0))],
        out_shape=[jax.ShapeDtypeStruct((s, A_WIDTH), F32), jax.ShapeDtypeStruct((s, 2 * A_KV_WIDTH), F32),
                   jax.ShapeDtypeStruct((A_Q_HEADS, 8, HEAD_DIM), F32)],
        scratch_shapes=[pltpu.VMEM((2 * A_KV_WIDTH, WINDOW), F32)],
        args=[sinks, slopes, qkv, qkv, qkv, do, lse, dd], sem=("arbitrary",), vmem=VMEM_BIG)
    return outs[0], outs[1], outs[2], comm_outs


def _attn_b_fwd(qkv, c3, name, comm=None):
    heads, s = qkv.shape[0] // 3, qkv.shape[1]
    hpairs = heads // 2
    bq = min(512, s)
    nq = s // bq
    nt = (((1,), (1,)), ((), ()))

    def body(q_ref, k_ref, v_ref, c_ref, o_ref, lse_ref, m_scr, l_scr, acc_scr):
        i = pl.program_id(1)
        r0 = pl.multiple_of(i * bq, bq)
        row = lax.broadcasted_iota(jnp.int32, (bq, bq), 0)
        col = lax.broadcasted_iota(jnp.int32, (bq, bq), 1)
        m_scr[...] = jnp.full((2, bq, LANES), NEG, F32)
        l_scr[...] = jnp.zeros((2, bq, LANES), F32)
        acc_scr[...] = jnp.zeros((2, bq, HEAD_DIM), F32)

        def step(j, masked):
            k0 = pl.multiple_of(j * bq, bq)
            for h2 in range(2):
                kv = k_ref[h2, pl.ds(k0, bq), :]
                vv = v_ref[h2, pl.ds(k0, bq), :]
                cq0 = c_ref[h2, :, pl.ds(r0, LANES)][:, 0:1]
                sc = lax.dot_general(q_ref[h2], kv, nt, preferred_element_type=F32)
                sc = sc + (cq0 - c_ref[h2, :, pl.ds(k0, bq)])
                if masked:
                    sc = jnp.where(col <= row, sc, NEG)
                m_prev = m_scr[h2]
                m_new = jnp.maximum(m_prev, jnp.max(sc, axis=1, keepdims=True))
                alpha = jnp.exp(m_prev - m_new)
                p = jnp.exp(sc - m_new[:, 0:1])
                l_scr[h2] = alpha * l_scr[h2] + jnp.sum(p, axis=1, keepdims=True)
                p_hi = p.astype(BF16)
                p_lo = (p - p_hi.astype(F32)).astype(BF16)
                pv = jnp.dot(p_hi, vv, preferred_element_type=F32) + jnp.dot(p_lo, vv, preferred_element_type=F32)
                acc_scr[h2] = acc_scr[h2] * alpha[:, 0:HEAD_DIM] + pv
                m_scr[h2] = m_new

        def loop_body(j, carry):
            step(j, False)
            return carry

        lax.fori_loop(0, i, loop_body, 0)
        step(i, True)
        outs = []
        for h2 in range(2):
            l = l_scr[h2]
            outs.append(acc_scr[h2] / l[:, 0:HEAD_DIM])
            lse_ref[h2] = (m_scr[h2] + jnp.log(l))[:, 0:HEAD_DIM]
        o_ref[...] = jnp.concatenate(outs, axis=1)

    res = lambda off: pl.BlockSpec((2, s, HEAD_DIM), lambda hp, i: (off + hp, 0, 0))
    outs, comm_outs = _hosted_call(
        body, comm, name=name, grid=(hpairs, nq),
        in_specs=[pl.BlockSpec((2, bq, HEAD_DIM), lambda hp, i: (hp, i, 0)), res(hpairs), res(2 * hpairs),
                  pl.BlockSpec((2, 1, s), lambda hp, i: (hp, 0, 0))],
        out_specs=[pl.BlockSpec((bq, 2 * HEAD_DIM), lambda hp, i: (i, hp)),
                   pl.BlockSpec((2, bq, HEAD_DIM), lambda hp, i: (hp, i, 0))],
        out_shape=[jax.ShapeDtypeStruct((s, heads * HEAD_DIM), F32), jax.ShapeDtypeStruct((heads, s, HEAD_DIM), F32)],
        scratch_shapes=[pltpu.VMEM((2, bq, LANES), F32), pltpu.VMEM((2, bq, LANES), F32), pltpu.VMEM((2, bq, HEAD_DIM), F32)],
        args=[qkv, qkv, qkv, c3], sem=("parallel", "parallel"), vmem=VMEM_BIG)
    return outs[0], outs[1], comm_outs


def _attn_b_bwd(qkv, do, lse, dd, c3, name, comm=None):
    heads, s = qkv.shape[0] // 3, qkv.shape[1]
    hpairs = heads // 2
    bq = min(512, s)
    nq = s // bq
    nt = (((1,), (1,)), ((), ()))
    tn = (((0,), (0,)), ((), ()))
    grid = (heads // 2, nq)

    def body(q_ref, k_ref, v_ref, do_ref, lse_ref, dd_ref, c_ref, dq_ref, dk_ref, dv_ref, dc_ref,
             dq_scr, dk_scr, dv_scr, dc_scr):
        j = pl.program_id(1)
        k0 = pl.multiple_of(j * bq, bq)
        row = lax.broadcasted_iota(jnp.int32, (bq, bq), 0)
        col = lax.broadcasted_iota(jnp.int32, (bq, bq), 1)

        @pl.when(j == 0)
        def _():
            dq_scr[...] = jnp.zeros(dq_scr.shape, F32)

        dk_scr[...] = jnp.zeros((2, HEAD_DIM, bq), F32)
        dv_scr[...] = jnp.zeros((2, HEAD_DIM, bq), F32)
        dc_scr[...] = jnp.zeros((2, 1, bq), F32)
        k_t = [k_ref[h2].T for h2 in range(2)]

        def step(i, masked):
            r0 = pl.multiple_of(i * bq, bq)
            for h2 in range(2):
                kv = k_ref[h2]
                vv = v_ref[h2]
                qv = q_ref[h2, pl.ds(r0, bq), :]
                dov = do_ref[h2, pl.ds(r0, bq), :]
                lse_v = lse_ref[h2, pl.ds(r0, bq), :][:, 0:1]
                dd_v = dd_ref[h2, pl.ds(r0, bq), :][:, 0:1]
                cq0 = c_ref[h2, :, pl.ds(r0, LANES)][:, 0:1]
                sc = lax.dot_general(qv, kv, nt, preferred_element_type=F32) + (cq0 - c_ref[h2, :, pl.ds(k0, bq)])
                if masked:
                    sc = jnp.where(col <= row, sc, NEG)
                p = jnp.exp(sc - lse_v)
                dp = lax.dot_general(dov, vv, nt, preferred_element_type=F32)
                dsc = p * (dp - dd_v)
                dsb = dsc.astype(BF16)
                dv_scr[h2] += jnp.dot(dov.T, p.astype(BF16), preferred_element_type=F32)
                dk_scr[h2] += jnp.dot(qv.T, dsb, preferred_element_type=F32)
                dq_scr[h2, :, pl.ds(r0, bq)] += jnp.dot(k_t[h2], dsb.T, preferred_element_type=F32)
                dc_scr[h2] -= jnp.sum(dsc, axis=0, keepdims=True)

        def loop_body(i, carry):
            step(i, False)
            return carry

        step(j, True)
        lax.fori_loop(j + 1, nq, loop_body, 0)
        dc_ref[...] = dc_scr[...]
        dk_ref[...] = jnp.concatenate([dk_scr[0].T, dk_scr[1].T], axis=1)
        dv_ref[...] = jnp.concatenate([dv_scr[0].T, dv_scr[1].T], axis=1)

        @pl.when(j == nq - 1)
        def _():
            dq_ref[...] = jnp.concatenate([dq_scr[0].T, dq_scr[1].T], axis=1)

    res = pl.BlockSpec((2, s, HEAD_DIM), lambda hp, j: (hp, 0, 0))
    blk = lambda off: pl.BlockSpec((2, bq, HEAD_DIM), lambda hp, j: (off + hp, j, 0))
    tm = jax.ShapeDtypeStruct((s, heads * HEAD_DIM), F32)
    in_specs = [res, blk(hpairs), blk(2 * hpairs), res, res, res, pl.BlockSpec((2, 1, s), lambda hp, j: (hp, 0, 0))]
    out_specs = [pl.BlockSpec((s, 2 * HEAD_DIM), lambda hp, j: (0, hp)),
                 pl.BlockSpec((bq, 2 * HEAD_DIM), lambda hp, j: (j, hp)),
                 pl.BlockSpec((bq, 2 * HEAD_DIM), lambda hp, j: (j, hp)),
                 pl.BlockSpec((2, 1, bq), lambda hp, j: (hp, 0, j))]
    out_shape = [tm, tm, tm, jax.ShapeDtypeStruct((heads, 1, s), F32)]
    scratch = [pltpu.VMEM((2, HEAD_DIM, s), F32), pltpu.VMEM((2, HEAD_DIM, bq), F32),
               pltpu.VMEM((2, HEAD_DIM, bq), F32), pltpu.VMEM((2, 1, bq), F32)]
    outs, comm_outs = _hosted_call(
        body, comm, name=name, grid=grid, in_specs=in_specs, out_specs=out_specs, out_shape=out_shape,
        scratch_shapes=scratch, args=[qkv, qkv, qkv, do, lse, dd, c3], sem=("parallel", "arbitrary"), vmem=VMEM_BIG)
    return outs[0], outs[1], outs[2], outs[3], comm_outs


def _attn_c_probs(qh, mkh):
    sc = lax.dot_general(qh, mkh, (((1,), (1,)), ((), ())), preferred_element_type=F32) * (C_HEAD_DIM ** -0.5)
    p = jnp.exp(sc - jnp.max(sc, axis=1, keepdims=True))
    return p / jnp.sum(p, axis=1, keepdims=True)


def _attn_c_fwd(q, mkv, name):
    s = q.shape[0]
    m = mkv.shape[0]
    bq = _tile(s, 512, 8)

    def body(q_ref, mk_ref, mv_ref, o_ref):
        outs = []
        for h in range(C_HEADS):
            sl = slice(h * C_HEAD_DIM, (h + 1) * C_HEAD_DIM)
            pn = _attn_c_probs(q_ref[:, sl], mk_ref[:, sl]).astype(BF16)
            outs.append(jnp.dot(pn, mv_ref[:, sl], preferred_element_type=F32))
        o_ref[...] = jnp.concatenate(outs, axis=1)

    return pl.pallas_call(
        body, name=name, grid=(s // bq,),
        in_specs=[pl.BlockSpec((bq, C_WIDTH), lambda i: (i, 0)), pl.BlockSpec((m, C_WIDTH), lambda i: (0, 0)),
                  pl.BlockSpec((m, C_WIDTH), lambda i: (0, 1))],
        out_specs=pl.BlockSpec((bq, C_WIDTH), lambda i: (i, 0)),
        out_shape=jax.ShapeDtypeStruct((s, C_WIDTH), F32),
        compiler_params=_params(("parallel",)),
    )(q, mkv, mkv)


def _attn_c_bwd(q, mkv, do, name):
    s = q.shape[0]
    m = mkv.shape[0]
    bq = _tile(s, 512, 8)
    tn = (((0,), (0,)), ((), ()))

    def body(q_ref, mk_ref, mv_ref, do_ref, dq_ref, dm_ref):
        i = pl.program_id(0)

        @pl.when(i == 0)
        def _():
            dm_ref[...] = jnp.zeros(dm_ref.shape, F32)

        dqs = []
        for h in range(C_HEADS):
            sl = slice(h * C_HEAD_DIM, (h + 1) * C_HEAD_DIM)
            qh, mkh, mvh, doh = q_ref[:, sl], mk_ref[:, sl], mv_ref[:, sl], do_ref[:, sl]
            pn = _attn_c_probs(qh, mkh)
            dp = lax.dot_general(doh, mvh, (((1,), (1,)), ((), ())), preferred_element_type=F32)
            dsc = (pn * (dp - jnp.sum(pn * dp, axis=1, keepdims=True)) * (C_HEAD_DIM ** -0.5)).astype(BF16)
            dqs.append(jnp.dot(dsc, mkh, preferred_element_type=F32))
            dm_ref[:, sl] += lax.dot_general(dsc, qh, tn, preferred_element_type=F32)
            sv = slice(C_WIDTH + h * C_HEAD_DIM, C_WIDTH + (h + 1) * C_HEAD_DIM)
            dm_ref[:, sv] += lax.dot_general(pn.astype(BF16), doh, tn, preferred_element_type=F32)
        dq_ref[...] = jnp.concatenate(dqs, axis=1)

    row = pl.BlockSpec((bq, C_WIDTH), lambda i: (i, 0))
    return pl.pallas_call(
        body, name=name, grid=(s // bq,),
        in_specs=[row, pl.BlockSpec((m, C_WIDTH), lambda i: (0, 0)), pl.BlockSpec((m, C_WIDTH), lambda i: (0, 1)), row],
        out_specs=[row, pl.BlockSpec((m, 2 * C_WIDTH), lambda i: (0, 0))],
        out_shape=[jax.ShapeDtypeStruct((s, C_WIDTH), F32), jax.ShapeDtypeStruct((m, 2 * C_WIDTH), F32)],
        compiler_params=_params(("arbitrary",)),
    )(q, mkv, mkv, do)


def _gate_fwd(y, proj, zc0, bw, name):
    rows, width = y.shape
    bm = _tile(rows, 2048 if bw <= 256 else 1024, 16)
    cb0 = zc0 // bw

    def body(y_ref, z_ref, o_ref):
        z = z_ref[...].astype(F32)
        o_ref[...] = (y_ref[...] * (z * _sigmoid(z))).astype(BF16)

    return pl.pallas_call(
        body, name=name, grid=(rows // bm, width // bw),
        in_specs=[pl.BlockSpec((bm, bw), lambda i, t: (i, t)), pl.BlockSpec((bm, bw), lambda i, t: (i, cb0 + t))],
        out_specs=pl.BlockSpec((bm, bw), lambda i, t: (i, t)),
        out_shape=jax.ShapeDtypeStruct((rows, width), BF16),
        compiler_params=_params(("parallel", "parallel")),
    )(y, proj)


def _gate_bwd(dsv, y, proj, zc0, bw, dproj, t0, head_major, name):
    rows, width = y.shape
    bm = _tile(rows, 2048 if bw <= 256 else 1024, 16)
    cb0 = zc0 // bw
    tb0 = t0 // bw
    bd = _block_diag(HEAD_DIM)
    hpb = bw // HEAD_DIM

    def body(*refs):
        if head_major:
            ds_ref, y_ref, z_ref, bd_ref, _, dp_ref, dy_ref, dd_ref = refs
        else:
            ds_ref, y_ref, z_ref, _, dp_ref, dy_ref = refs
        z = z_ref[...].astype(F32)
        sig = _sigmoid(z)
        dsx = ds_ref[...]
        yv = y_ref[...]
        dy = dsx * (z * sig)
        dp_ref[...] = (dsx * yv * (sig * (1.0 + z * (1.0 - sig)))).astype(BF16)
        if head_major:
            dyb = dy.astype(BF16)
            dd = _seg_sum(dyb.astype(F32) * yv, bd_ref[...])
            for h in range(hpb):
                sl = slice(h * HEAD_DIM, (h + 1) * HEAD_DIM)
                dy_ref[h] = dyb[:, sl]
                dd_ref[h] = dd[:, sl]
        else:
            dy_ref[...] = dy.astype(BF16)

    tile = pl.BlockSpec((bm, bw), lambda i, t: (i, t))
    ztile = pl.BlockSpec((bm, bw), lambda i, t: (i, cb0 + t))
    ttile = pl.BlockSpec((bm, bw), lambda i, t: (i, tb0 + t))
    any_spec = pl.BlockSpec(memory_space=pl.ANY)
    dp_shape = jax.ShapeDtypeStruct(dproj.shape, BF16)
    if head_major:
        hm_spec = pl.BlockSpec((hpb, bm, HEAD_DIM), lambda i, t: (t, i, 0))
        nh = width // HEAD_DIM
        outs = pl.pallas_call(
            body, name=name, grid=(rows // bm, width // bw),
            in_specs=[tile, tile, ztile, pl.BlockSpec((LANES, LANES), lambda i, t: (0, 0)), any_spec],
            out_specs=[ttile, hm_spec, hm_spec],
            out_shape=[dp_shape, jax.ShapeDtypeStruct((nh, rows, HEAD_DIM), BF16),
                       jax.ShapeDtypeStruct((nh, rows, HEAD_DIM), F32)],
            input_output_aliases={4: 0},
            compiler_params=_params(("parallel", "parallel")),
        )(dsv, y, proj, bd, dproj)
        return outs[0], outs[1], outs[2]
    outs = pl.pallas_call(
        body, name=name, grid=(rows // bm, width // bw),
        in_specs=[tile, tile, ztile, any_spec],
        out_specs=[ttile, tile],
        out_shape=[dp_shape, jax.ShapeDtypeStruct((rows, width), BF16)],
        input_output_aliases={3: 0},
        compiler_params=_params(("parallel", "parallel")),
    )(dsv, y, proj, dproj)
    return outs[0], outs[1], None


def _merge_fwd(proj, ua, ub, uc, name):
    rows, d = ua.shape
    bm = _tile(rows, 1024, 16)
    bw = _tile(d, 512)
    g0 = COL_GATE // bw
    gstep = d // bw

    def body(la_ref, lb_ref, lc_ref, ua_ref, ub_ref, uc_ref, o_ref, ga_ref, gb_ref, gc_ref):
        y = None
        for l_ref, u_ref, g_ref in ((la_ref, ua_ref, ga_ref), (lb_ref, ub_ref, gb_ref), (lc_ref, uc_ref, gc_ref)):
            g = _sigmoid(l_ref[...].astype(F32))
            g_ref[...] = g.astype(BF16)
            term = g * u_ref[...].astype(F32)
            y = term if y is None else y + term
        o_ref[...] = y.astype(BF16)

    tile = pl.BlockSpec((bm, bw), lambda i, t: (i, t))
    gate = lambda b: pl.BlockSpec((bm, bw), lambda i, t: (i, g0 + b * gstep + t))
    shape = jax.ShapeDtypeStruct((rows, d), BF16)
    return pl.pallas_call(
        body, name=name, grid=(rows // bm, d // bw),
        in_specs=[gate(0), gate(1), gate(2), tile, tile, tile],
        out_specs=[tile] * 4, out_shape=[shape] * 4,
        compiler_params=_params(("parallel", "parallel")),
    )(proj, proj, proj, ua, ub, uc)


def _merge_bwd(dym, us, gs, name):
    rows, d = dym.shape
    bm = _tile(rows, 256, 16)

    def body(dy_ref, ua_ref, ub_ref, uc_ref, ga_ref, gb_ref, gc_ref, dg_ref, da_ref, db_ref, dc_ref):
        dyv = dy_ref[...]
        for b, (u_ref, g_ref, du_ref) in enumerate(((ua_ref, ga_ref, da_ref), (ub_ref, gb_ref, db_ref), (uc_ref, gc_ref, dc_ref))):
            g = g_ref[...].astype(F32)
            du_ref[...] = (g * dyv).astype(BF16)
            dg_ref[:, b * d:(b + 1) * d] = (dyv * u_ref[...].astype(F32) * g * (1.0 - g)).astype(BF16)

    tile = pl.BlockSpec((bm, d), lambda i: (i, 0))
    shape = jax.ShapeDtypeStruct((rows, d), BF16)
    outs = pl.pallas_call(
        body, name=name, grid=(rows // bm,),
        in_specs=[tile] * 7,
        out_specs=[pl.BlockSpec((bm, 3 * d), lambda i: (i, 0)), tile, tile, tile],
        out_shape=[jax.ShapeDtypeStruct((rows, 3 * d), BF16), shape, shape, shape],
        compiler_params=_params(("parallel",), VMEM_BIG),
    )(dym, *us, *gs)
    return outs[0], outs[1], outs[2], outs[3]


def _out_proj_loss(ym, wo, x, target, name):
    m, d = x.shape
    bm, bn = _tile(m, 1024, 16), _tile(d, 1024)
    grid = (m // bm, d // bn)

    def body(a_ref, b_ref, x_ref, t_ref, dy_ref, dyb_ref, l_ref):
        first, _ = _grid_edges(grid)
        y = jnp.dot(a_ref[...], b_ref[...], preferred_element_type=F32) + x_ref[...]
        diff = y - t_ref[...]
        dy = diff * (1.0 / d)
        dy_ref[...] = dy
        dyb_ref[...] = dy.astype(BF16)
        sq = diff * diff
        part = sq[:, 0:LANES]
        for c in range(1, bn // LANES):
            part = part + sq[:, c * LANES:(c + 1) * LANES]
        part = jnp.sum(part.reshape(bm // 8, 8, LANES), axis=0)

        @pl.when(first)
        def _():
            l_ref[...] = part

        @pl.when(jnp.logical_not(first))
        def _():
            l_ref[...] += part

    tile = pl.BlockSpec((bm, bn), lambda i, j: (i, j))
    return pl.pallas_call(
        body, name=name, grid=grid,
        in_specs=[pl.BlockSpec((bm, d), lambda i, j: (i, 0)), pl.BlockSpec((d, bn), lambda i, j: (0, j)), tile, tile],
        out_specs=[tile, tile, pl.BlockSpec((8, LANES), lambda i, j: (0, 0))],
        out_shape=[jax.ShapeDtypeStruct((m, d), F32), jax.ShapeDtypeStruct((m, d), BF16),
                   jax.ShapeDtypeStruct((8, LANES), F32)],
        compiler_params=_params(("arbitrary", "arbitrary"), VMEM_BIG),
    )(ym, wo, x, target)


def _row(vec, reps=1):
    return jnp.tile(vec.reshape(1, -1).astype(F32), (1, reps))


def _local_step(x, mem, target, small, wg, shards=None):
    s, d = x.shape
    dist = shards is not None
    wg = dict(wg)
    ones = lambda n: jnp.ones((1, n), F32)
    zeros = lambda n: jnp.zeros((1, n), F32)
    scale_ab = HEAD_DIM ** -0.5
    split8 = lambda g: g.reshape(N_DEV, g.shape[0] // N_DEV, g.shape[1])
    flat8 = lambda g: g.reshape(g.shape[0] * g.shape[1], g.shape[2])
    gather = lambda names: _Comm("gather", [shards[n] for n in names]) if dist else None
    g = {}

    def scatter(names):
        return _Comm("scatter", [split8(g[n]) for n in names]) if dist else None

    def hosted(result, names, store):
        if not dist:
            return result
        out, got = result
        store.update(zip(names, got))
        return out

    hn = _rmsnorm_fwd(x, small["norm_gain"], "rms_x_fwd")
    got = {}
    proj = hosted(_mm_nn(hn, wg["qkv"], bm=1024, bn=1024, bk=d, o_dtype=BF16, name="proj_qkv",
                         comm=gather(("wa", "wb"))), ("wa", "wb"), got)
    wg.update({n: flat8(a) for n, a in got.items()})
    pfb = _mm_nn(hn, wg["wf"], bm=1024, bn=FB_PAD, bk=d, o_dtype=F32, name="proj_fb")
    mn = _rmsnorm_fwd(mem, small["mem_norm_gain"], "rms_mem_fwd")
    mkv = _mm_nn(mn, wg["wk"], bm=256, bn=1024, bk=d, o_dtype=F32, name="mem_kv")

    gain_a = jnp.concatenate([_row(small["q_gain_a"], A_Q_HEADS) * scale_ab, _row(small["k_gain_a"], A_KV_HEADS), ones(A_KV_WIDTH)], axis=1)
    flag_a = jnp.concatenate([ones(A_WIDTH + A_KV_WIDTH), zeros(A_KV_WIDTH)], axis=1)
    qkv_a = _headnorm_fwd(proj, COL_QA, 1280, 1280, HEAD_DIM, gain_a, flag_a, True, "hn_a_fwd")
    gain_b = jnp.concatenate([_row(small["q_gain_b"], B_HEADS) * scale_ab, _row(small["k_gain_b"], B_HEADS), ones(B_WIDTH)], axis=1)
    flag_b = jnp.concatenate([ones(2 * B_WIDTH), zeros(B_WIDTH)], axis=1)
    qkv_b = _headnorm_fwd(proj, COL_QB, 2304, 256, HEAD_DIM, gain_b, flag_b, True, "hn_b_fwd")
    gain_cq = _row(small["q_gain_c"], C_HEADS)
    q_c = _headnorm_fwd(proj, COL_QC, C_WIDTH, C_WIDTH, C_HEAD_DIM, gain_cq, ones(C_WIDTH), False, "hn_cq_fwd")
    gain_ck = jnp.concatenate([_row(small["k_gain_c"], C_HEADS), ones(C_WIDTH)], axis=1)
    flag_ck = jnp.concatenate([ones(C_WIDTH), zeros(C_WIDTH)], axis=1)
    mkvn = _headnorm_fwd(mkv, 0, 2 * C_WIDTH, 2 * C_WIDTH, C_HEAD_DIM, gain_ck, flag_ck, False, "hn_ck_fwd")


    bpad = jnp.pad(small["b_forget"].reshape(1, -1), ((0, 0), (0, FB_PAD - B_HEADS)))
    c16 = _fox_prep(pfb, bpad, "fox_prep")
    c3 = c16[0:B_HEADS].reshape(B_HEADS, 1, s)

    sinks = small["sinks_a"].reshape(-1)
    slopes = jnp.exp2(-8.0 * jnp.arange(1, A_Q_HEADS + 1, dtype=F32) / A_Q_HEADS)
    y_a, lse_a = _attn_a_fwd(qkv_a, sinks, slopes, "attn_a_fwd")
    y_b, lse_b, got_zg = _attn_b_fwd(qkv_b, c3, "attn_b_fwd", comm=gather(("zg",)))
    if dist:
        wg["zg"] = flat8(got_zg[0])
    y_c = _attn_c_fwd(q_c, mkvn, "attn_c_fwd")

    got = {}
    pzg = hosted(_mm_nn(hn, wg["zg"], bm=1024, bn=1024, bk=d, o_dtype=BF16, name="proj_zg", comm=gather(("wo", "wc"))),
                 ("wo", "wc"), got)
    wg.update({n: flat8(a) for n, a in got.items()})

    s_a = _gate_fwd(y_a, pzg, COL_ZA, 256, "gate_a_fwd")
    s_b = _gate_fwd(y_b, pzg, COL_ZB, 256, "gate_b_fwd")
    s_c = _gate_fwd(y_c, pzg, COL_ZC, 512, "gate_c_fwd")
    w_a, w_b, w_c = _branch_full(wg["wa"]), _branch_full(wg["wb"]), _branch_full(wg["wc"])
    u_a = _mm_nn(s_a, w_a, bm=1024, bn=2048, bk=A_WIDTH, o_dtype=BF16, name="branch_a_fwd")
    u_b = _mm_nn(s_b, w_b, bm=1024, bn=2048, bk=B_WIDTH, o_dtype=BF16, name="branch_b_fwd")
    u_c = _mm_nn(s_c, w_c, bm=1024, bn=2048, bk=C_WIDTH, o_dtype=BF16, name="branch_c_fwd")
    ym, gate_a, gate_b, gate_c = _merge_fwd(pzg, u_a, u_b, u_c, "merge_fwd")
    dy, dyb, lpart = _out_proj_loss(ym, wg["wo"], x, target, "out_proj_loss")
    loss = 0.5 / d * jnp.sum(lpart)

    dym = _mm_nt(dyb, wg["wo"], bm=1024, bn=1024, bk=d, o_dtype=F32, name="out_proj_bwd_act")
    g["wo"] = _mm_tn(ym, dyb, bm=512, bn=1024, bk=s, o_dtype=BF16, name="out_proj_bwd_w")

    dgate, du_a, du_b, du_c = _merge_bwd(dym, (u_a, u_b, u_c), (gate_a, gate_b, gate_c), "merge_bwd")
    parts = {}
    g["wm_g"] = hosted(_mm_tn(hn, dgate, bm=512, bn=1024, bk=s, o_dtype=BF16, name="proj_gate_bwd_w",
                              comm=scatter(("wo",))), ("wo",), parts)

    ds_a = _mm_nt(du_a, w_a, bm=1024, bn=A_WIDTH, bk=d, o_dtype=F32, name="branch_a_bwd_act")
    ds_b = _mm_nt(du_b, w_b, bm=1024, bn=B_WIDTH, bk=d, o_dtype=F32, name="branch_b_bwd_act")
    ds_c = _mm_nt(du_c, w_c, bm=1024, bn=C_WIDTH, bk=d, o_dtype=F32, name="branch_c_bwd_act")
    g["wa"] = _branch_shards(_mm_tn(s_a, du_a, bm=A_WIDTH, bn=1024, bk=s, o_dtype=BF16, name="branch_a_bwd_w"))
    g["wb"] = _branch_shards(_mm_tn(s_b, du_b, bm=B_WIDTH, bn=1024, bk=s, o_dtype=BF16, name="branch_b_bwd_w"))
    g["wc"] = _branch_shards(_mm_tn(s_c, du_c, bm=C_WIDTH, bn=1024, bk=s, o_dtype=BF16, name="branch_c_bwd_w"))

    dz = lax.empty((s, W_Z), BF16)
    dz, do_a, dd_a = _gate_bwd(ds_a, y_a, pzg, COL_ZA, 256, dz, COL_ZA, True, "gate_a_bwd")
    dz, do_b, dd_b = _gate_bwd(ds_b, y_b, pzg, COL_ZB, 256, dz, COL_ZB, True, "gate_b_bwd")
    dz, do_c, _ = _gate_bwd(ds_c, y_c, pzg, COL_ZC, 512, dz, COL_ZC, False, "gate_c_bwd")
    g["wm_z"] = _mm_tn(hn, dz, bm=512, bn=1024, bk=s, o_dtype=BF16, name="proj_z_bwd_w")

    names = ("wa", "wb", "wc")
    dq_a, dkv_a, dsink, got = _attn_a_bwd(qkv_a, do_a, lse_a, dd_a, sinks, slopes, "attn_a_bwd", comm=scatter(names))
    parts.update(zip(names, got))
    names = ("wm_g", "wm_z")
    dq_b, dk_b, dv_b, dc3, got = _attn_b_bwd(qkv_b, do_b, lse_b, dd_b, c3, "attn_b_bwd", comm=scatter(names))
    parts.update(zip(names, got))
    dq_c, dmkvn = _attn_c_bwd(q_c, mkvn, do_c, "attn_c_bwd")

    dqkv = lax.empty((s, W_QKV), BF16)
    dqkv, dg_qa = _headnorm_bwd(proj, COL_QA, A_WIDTH, 256, HEAD_DIM, gain_a[:, 0:768], flag_a[:, 0:768], dq_a, dqkv, COL_QA, "hn_qa_bwd")
    dqkv, dg_kva = _headnorm_bwd(proj, COL_KA, 512, 256, HEAD_DIM, gain_a[:, 768:1280], flag_a[:, 768:1280], dkv_a, dqkv, COL_KA, "hn_kva_bwd")
    dqkv, dg_qb = _headnorm_bwd(proj, COL_QB, B_WIDTH, 256, HEAD_DIM, gain_b[:, 0:768], flag_b[:, 0:768], dq_b, dqkv, COL_QB, "hn_qb_bwd")
    dqkv, dg_kb = _headnorm_bwd(proj, COL_KB, B_WIDTH, 256, HEAD_DIM, gain_b[:, 768:1536], flag_b[:, 768:1536], dk_b, dqkv, COL_KB, "hn_kb_bwd")
    dqkv, _ = _headnorm_bwd(proj, COL_VB, B_WIDTH, 256, HEAD_DIM, gain_b[:, 1536:2304], flag_b[:, 1536:2304], dv_b, dqkv, COL_VB, "hn_vb_bwd")
    dqkv, dg_qc = _headnorm_bwd(proj, COL_QC, C_WIDTH, 512, C_HEAD_DIM, gain_cq, ones(C_WIDTH), dq_c, dqkv, COL_QC, "hn_qc_bwd")
    dmkv, dg_kc = _headnorm_bwd(mkv, 0, 2 * C_WIDTH, 2 * C_WIDTH, C_HEAD_DIM, gain_ck, flag_ck, dmkvn, None, 0, "hn_kc_bwd")

    dct = jnp.pad(dc3.reshape(B_HEADS, s), ((0, 16 - B_HEADS), (0, 0)))
    dfb, dbf = _fox_prep_bwd(pfb, bpad, dct, "fox_prep_bwd")

    dmn = _mm_nt(dmkv, wg["wk"], bm=256, bn=1024, bk=1024, o_dtype=F32, name="mem_kv_bwd_act")
    g["wk"] = _mm_tn(mn, dmkv, bm=512, bn=1024, bk=mem.shape[0], o_dtype=BF16, name="mem_kv_bwd_w")
    _, dg_mem = _rmsnorm_bwd(mem, dmn, small["mem_norm_gain"], None, "rms_mem_bwd")

    g["wm_qkv"] = _mm_tn(hn, dqkv, bm=512, bn=1024, bk=s, o_dtype=BF16, name="proj_qkv_bwd_w")
    g["wf"] = _mm_tn(hn, dfb, bm=512, bn=FB_PAD, bk=s, o_dtype=BF16, name="proj_fb_bwd_w")
    half = Q_SPLIT
    g["wm_q1"], g["wm_q2"] = g["wm_qkv"][:, 0:half], g["wm_qkv"][:, half:W_QKV]
    names = ("wm_q1",)
    dhn = hosted(_mm_nt_sum([(dqkv, wg["qkv"], 0), (dfb, wg["wf"], 0)], bm=1024, bn=1024, bk=2048,
                            name="proj_qkv_bwd_act", comm=scatter(names)), names, parts)
    names = ("wm_q2", "wf", "wk")
    dhn = hosted(_mm_nt_sum([(dz, wg["zg"], COL_ZA), (dgate, wg["zg"], COL_GATE)], bm=1024, bn=1024, bk=2048,
                            name="proj_zg_bwd_act", add=dhn, comm=scatter(names)), names, parts)
    if dist:
        g = parts
    grad_x, dg_x = _rmsnorm_bwd(x, dhn, small["norm_gain"], dy, "rms_x_bwd")

    fold = lambda part, heads, hd: jnp.sum(jnp.sum(part, axis=0).reshape(heads, hd), axis=0).reshape(1, hd)
    small_grads = {
        "norm_gain": jnp.sum(dg_x, axis=0).reshape(1, d),
        "mem_norm_gain": jnp.sum(dg_mem, axis=0).reshape(1, d),
        "b_forget": dbf[0:B_HEADS, 0].reshape(1, B_HEADS),
        "q_gain_a": fold(dg_qa, A_Q_HEADS, HEAD_DIM) * scale_ab,
        "k_gain_a": fold(dg_kva[:, 0:A_KV_WIDTH], A_KV_HEADS, HEAD_DIM),
        "sinks_a": (jnp.sum(dsink, axis=(1, 2)) * (1.0 / HEAD_DIM)).reshape(1, A_Q_HEADS),
        "q_gain_b": fold(dg_qb, B_HEADS, HEAD_DIM) * scale_ab,
        "k_gain_b": fold(dg_kb, B_HEADS, HEAD_DIM),
        "q_gain_c": fold(dg_qc, C_HEADS, C_HEAD_DIM),
        "k_gain_c": fold(dg_kc[:, 0:C_WIDTH], C_HEADS, C_HEAD_DIM),
    }
    return loss, grad_x, small_grads, g


def _coords():
    return lax.axis_index("x"), lax.axis_index("y"), lax.axis_index("c")


def _all_gather(shards, name):
    n = len(shards)

    def body(*refs):
        ins = refs[0:n]
        outs = refs[n:2 * n]
        send_sems, recv_sems, local_sems = refs[2 * n:2 * n + 3]
        x, y, c = _coords()
        me, sibling = (x, y, c), (x, y, 1 - c)
        chips = [(1 - x, y), (x, 1 - y), (1 - x, 1 - y)]
        idx = lambda p: 4 * p[0] + 2 * p[1] + p[2]

        def copy(a, k, block, to, src=None):
            slot = outs[a].at[idx(block)]
            return pltpu.make_async_remote_copy(
                src_ref=slot if src is None else src, dst_ref=slot,
                send_sem=send_sems.at[a, k], recv_sem=recv_sems.at[a, k], device_id=to, device_id_type=MESH)

        mine = [pltpu.make_async_copy(ins[a], outs[a].at[idx(me)], local_sems.at[a]) for a in range(n)]
        for cp in mine:
            cp.start()
        first = []
        for a in range(n):
            first.append(copy(a, 0, me, sibling, src=ins[a]))
            first += [copy(a, 1 + j, me, (*chip, c), src=ins[a]) for j, chip in enumerate(chips)]
        for cp in first:
            cp.start()
        passed = []
        for j, chip in enumerate(chips):
            for a in range(n):
                copy(a, 1 + j, (*chip, c), me).wait_recv()
                fwd = copy(a, 4 + j, (*chip, c), sibling)
                fwd.start()
                passed.append(fwd)
        for a in range(n):
            copy(a, 0, sibling, me).wait_recv()
            for j, chip in enumerate(chips):
                copy(a, 4 + j, (*chip, 1 - c), me).wait_recv()
        for cp in first + passed:
            cp.wait_send()
        for cp in mine:
            cp.wait()

    any_spec = pl.BlockSpec(memory_space=pl.ANY)
    return pl.pallas_call(
        body, name=name,
        in_specs=[any_spec] * n, out_specs=[any_spec] * n,
        out_shape=[jax.ShapeDtypeStruct((N_DEV,) + sh.shape, sh.dtype) for sh in shards],
        scratch_shapes=[pltpu.SemaphoreType.DMA((n, 7)), pltpu.SemaphoreType.DMA((n, 7)), pltpu.SemaphoreType.DMA((n,))],
    )(*shards)


def _all_reduce_small(vec, name):
    p = vec.shape[1]

    def body(v_ref, o_ref, gather, send_sems, recv_sems):
        x, y, c = _coords()
        my = 4 * x + 2 * y + c
        peers = [(x ^ ((k >> 2) & 1), y ^ ((k >> 1) & 1), c ^ (k & 1)) for k in range(1, N_DEV)]
        gather[my] = v_ref[...]
        sends = [pltpu.make_async_remote_copy(
            src_ref=v_ref, dst_ref=gather.at[my], send_sem=send_sems.at[k], recv_sem=recv_sems.at[k],
            device_id=peer, device_id_type=MESH) for k, peer in enumerate(peers)]
        for cp in sends:
            cp.start()
        for k, peer in enumerate(peers):
            pid = 4 * peer[0] + 2 * peer[1] + peer[2]
            pltpu.make_async_remote_copy(
                src_ref=v_ref, dst_ref=gather.at[pid], send_sem=send_sems.at[k], recv_sem=recv_sems.at[k],
                device_id=peer, device_id_type=MESH).wait_recv()
        for cp in sends:
            cp.wait_send()
        total = gather[0]
        for j in range(1, N_DEV):
            total = total + gather[j]
        o_ref[...] = total

    vm = pl.BlockSpec(memory_space=pltpu.VMEM)
    return pl.pallas_call(
        body, name=name, in_specs=[vm], out_specs=vm,
        out_shape=jax.ShapeDtypeStruct((8, p), F32),
        scratch_shapes=[pltpu.VMEM((N_DEV, 8, p), F32), pltpu.SemaphoreType.DMA((7,)), pltpu.SemaphoreType.DMA((7,))],
    )(vec)[0:1]


def _sum_parts(parts, name):
    _, rows, cols = parts.shape
    br = _tile(rows, 64, 16)

    def body(p_ref, o_ref):
        total = p_ref[0].astype(F32)
        for j in range(1, N_DEV):
            total = total + p_ref[j].astype(F32)
        o_ref[...] = total

    return pl.pallas_call(
        body, name=name, grid=(rows // br,),
        in_specs=[pl.BlockSpec((N_DEV, br, cols), lambda i: (0, i, 0))],
        out_specs=pl.BlockSpec((br, cols), lambda i: (i, 0)),
        out_shape=jax.ShapeDtypeStruct((rows, cols), F32),
        compiler_params=_params(("parallel",), VMEM_BIG),
    )(parts)


def _adamw(w, g, m, v, name, br=32):
    rows, cols = w.shape
    br = min(br, rows)
    c1 = 1.0 / (1.0 - ADAM_B1 ** ADAM_STEP)
    c2 = 1.0 / (1.0 - ADAM_B2 ** ADAM_STEP)

    def body(w_ref, g_ref, m_ref, v_ref, d_ref, nm_ref, nv_ref):
        gv = g_ref[...]
        nm = ADAM_B1 * m_ref[...] + (1.0 - ADAM_B1) * gv
        nv = ADAM_B2 * v_ref[...] + (1.0 - ADAM_B2) * (gv * gv)
        d_ref[...] = -ADAM_LR * ((nm * c1) / (jnp.sqrt(nv * c2) + ADAM_EPS) + ADAM_WD * w_ref[...])
        nm_ref[...] = nm
        nv_ref[...] = nv

    spec = pl.BlockSpec((br, cols), lambda i: (i, 0))
    shape = jax.ShapeDtypeStruct((rows, cols), F32)
    return pl.pallas_call(
        body, name=name, grid=(pl.cdiv(rows, br),), in_specs=[spec] * 4, out_specs=[spec] * 3, out_shape=[shape] * 3,
        compiler_params=_params(("parallel",), VMEM_BIG),
    )(w, g, m, v)


def _adamw_t(wt, g, mt, vt, name, br=1024):
    n, r = wt.shape
    c1 = 1.0 / (1.0 - ADAM_B1 ** ADAM_STEP)
    c2 = 1.0 / (1.0 - ADAM_B2 ** ADAM_STEP)

    def body(w_ref, g_ref, m_ref, v_ref, d_ref, nm_ref, nv_ref):
        gv = g_ref[...].T
        nm = ADAM_B1 * m_ref[...] + (1.0 - ADAM_B1) * gv
        nv = ADAM_B2 * v_ref[...] + (1.0 - ADAM_B2) * (gv * gv)
        d_ref[...] = -ADAM_LR * ((nm * c1) / (jnp.sqrt(nv * c2) + ADAM_EPS) + ADAM_WD * w_ref[...])
        nm_ref[...] = nm
        nv_ref[...] = nv

    spec = pl.BlockSpec((br, r), lambda i: (i, 0))
    shape = jax.ShapeDtypeStruct((n, r), F32)
    return pl.pallas_call(
        body, name=name, grid=(pl.cdiv(n, br),),
        in_specs=[spec, pl.BlockSpec((r, br), lambda i: (0, i)), spec, spec], out_specs=[spec] * 3, out_shape=[shape] * 3,
        compiler_params=_params(("parallel",), VMEM_BIG),
    )(wt, g, mt, vt)


def _adamw_parts(w, parts, m, v, name):
    rows, cols = w.shape
    br = _tile(rows, 32, 16)
    c1 = 1.0 / (1.0 - ADAM_B1 ** ADAM_STEP)
    c2 = 1.0 / (1.0 - ADAM_B2 ** ADAM_STEP)

    def body(w_ref, p_ref, m_ref, v_ref, g_ref, d_ref, nm_ref, nv_ref):
        gv = p_ref[0].astype(F32)
        for j in range(1, N_DEV):
            gv = gv + p_ref[j].astype(F32)
        nm = ADAM_B1 * m_ref[...] + (1.0 - ADAM_B1) * gv
        nv = ADAM_B2 * v_ref[...] + (1.0 - ADAM_B2) * (gv * gv)
        g_ref[...] = gv
        d_ref[...] = -ADAM_LR * ((nm * c1) / (jnp.sqrt(nv * c2) + ADAM_EPS) + ADAM_WD * w_ref[...])
        nm_ref[...] = nm
        nv_ref[...] = nv

    spec = pl.BlockSpec((br, cols), lambda i: (i, 0))
    shape = jax.ShapeDtypeStruct((rows, cols), F32)
    return pl.pallas_call(
        body, name=name, grid=(rows // br,),
        in_specs=[spec, pl.BlockSpec((N_DEV, br, cols), lambda i: (0, i, 0)), spec, spec],
        out_specs=[spec] * 4, out_shape=[shape] * 4,
        compiler_params=_params(("parallel",), VMEM_BIG),
    )(w, parts, m, v)


SMALL_NAMES = ("norm_gain", "mem_norm_gain", "b_forget", "q_gain_a", "k_gain_a", "sinks_a",
               "q_gain_b", "k_gain_b", "q_gain_c", "k_gain_c")
BIG_NAMES = ("w_in", "w_mem_kv", "w_branch_a", "w_branch_b", "w_branch_c", "w_out")
WEIGHT_ORDER = ("norm_gain", "mem_norm_gain", "w_in", "b_forget", "q_gain_a", "k_gain_a", "sinks_a", "q_gain_b",
                "k_gain_b", "q_gain_c", "k_gain_c", "w_mem_kv", "w_branch_a", "w_branch_b", "w_branch_c", "w_out")


def _pack_small(tree):
    flat = jnp.concatenate([tree[n].reshape(1, -1) for n in SMALL_NAMES], axis=1)
    pad = (-flat.shape[1]) % LANES
    return jnp.pad(flat, ((0, 0), (0, pad)))


def _unpack_small(flat, like):
    out, off = {}, 0
    for n in SMALL_NAMES:
        size = like[n].size
        out[n] = flat[:, off:off + size].reshape(like[n].shape)
        off += size
    return out


def kernel(x, mem, norm_gain, mem_norm_gain, w_in, b_forget, q_gain_a, k_gain_a, sinks_a, q_gain_b, k_gain_b, q_gain_c, k_gain_c, w_mem_kv, w_branch_a, w_branch_b, w_branch_c, w_out, loss_target, m_norm_gain, m_mem_norm_gain, m_w_in, m_b_forget, m_q_gain_a, m_k_gain_a, m_sinks_a, m_q_gain_b, m_k_gain_b, m_q_gain_c, m_k_gain_c, m_w_mem_kv, m_w_branch_a, m_w_branch_b, m_w_branch_c, m_w_out, v_norm_gain, v_mem_norm_gain, v_w_in, v_b_forget, v_q_gain_a, v_k_gain_a, v_sinks_a, v_q_gain_b, v_k_gain_b, v_q_gain_c, v_k_gain_c, v_w_mem_kv, v_w_branch_a, v_w_branch_b, v_w_branch_c, v_w_out):
    weights = dict(norm_gain=norm_gain, mem_norm_gain=mem_norm_gain, w_in=w_in, b_forget=b_forget, q_gain_a=q_gain_a,
                   k_gain_a=k_gain_a, sinks_a=sinks_a, q_gain_b=q_gain_b, k_gain_b=k_gain_b, q_gain_c=q_gain_c,
                   k_gain_c=k_gain_c, w_mem_kv=w_mem_kv, w_branch_a=w_branch_a, w_branch_b=w_branch_b,
                   w_branch_c=w_branch_c, w_out=w_out)
    mom_m = dict(norm_gain=m_norm_gain, mem_norm_gain=m_mem_norm_gain, w_in=m_w_in, b_forget=m_b_forget,
                 q_gain_a=m_q_gain_a, k_gain_a=m_k_gain_a, sinks_a=m_sinks_a, q_gain_b=m_q_gain_b, k_gain_b=m_k_gain_b,
                 q_gain_c=m_q_gain_c, k_gain_c=m_k_gain_c, w_mem_kv=m_w_mem_kv, w_branch_a=m_w_branch_a,
                 w_branch_b=m_w_branch_b, w_branch_c=m_w_branch_c, w_out=m_w_out)
    mom_v = dict(norm_gain=v_norm_gain, mem_norm_gain=v_mem_norm_gain, w_in=v_w_in, b_forget=v_b_forget,
                 q_gain_a=v_q_gain_a, k_gain_a=v_k_gain_a, sinks_a=v_sinks_a, q_gain_b=v_q_gain_b, k_gain_b=v_k_gain_b,
                 q_gain_c=v_q_gain_c, k_gain_c=v_k_gain_c, w_mem_kv=v_w_mem_kv, w_branch_a=v_w_branch_a,
                 w_branch_b=v_w_branch_b, w_branch_c=v_w_branch_c, w_out=v_w_out)
    wi = w_in[0]
    sh_qkv = jnp.concatenate([wi[:, a:b] for a, b in SRC_RANGES[0:3]], axis=1).astype(BF16)
    sh_zg = jnp.concatenate([wi[:, a:b] for a, b in SRC_RANGES[3:6]] + [wi[:, SRC_GATE:]], axis=1).astype(BF16)
    sh_wf = jnp.pad(wi[:, FB_SRC:FB_SRC + B_HEADS], ((0, 0), (0, FB_PAD - B_HEADS))).astype(BF16)
    shards = {"zg": sh_zg, "wo": w_out[0].astype(BF16), "wa": w_branch_a[0].astype(BF16),
              "wb": w_branch_b[0].astype(BF16), "wc": w_branch_c[0].astype(BF16)}
    first = ("qkv", "wf", "wk")
    full = _all_gather([sh_qkv, sh_wf, w_mem_kv[0].astype(BF16)], "weights_all_gather")
    wg = {kname: arr.reshape(arr.shape[0] * arr.shape[1], arr.shape[2]) for kname, arr in zip(first, full)}

    small = {n: weights[n] for n in SMALL_NAMES}
    loss_local, grad_x, small_g, parts = _local_step(x[0], mem[0], loss_target[0], small, wg, shards)

    grads, delta, new_m, new_v = {}, {}, {}, {}
    for n, kname in (("w_mem_kv", "wk"), ("w_out", "wo"), ("w_branch_a", "wa"), ("w_branch_b", "wb"), ("w_branch_c", "wc")):
        gsum, dlt, nm, nv = _adamw_parts(weights[n][0], parts[kname], mom_m[n][0], mom_v[n][0], "adamw_" + n)
        grads[n], delta[n], new_m[n], new_v[n] = gsum, dlt[None], nm[None], nv[None]
    g1, g2, gz, gf, gg = (_sum_parts(parts[k], "grad_sum_" + k) for k in ("wm_q1", "wm_q2", "wm_z", "wf", "wm_g"))
    half = Q_SPLIT
    g_in = jnp.concatenate([g1, g2[:, 0:COL_QB - half], gz[:, COL_ZA:COL_ZB], g2[:, COL_QB - half:COL_QC - half],
                            gz[:, COL_ZB:COL_ZC], gf[:, 0:B_HEADS], g2[:, COL_QC - half:W_QKV - half], gz[:, COL_ZC:W_Z], gg], axis=1)
    dlt, nm, nv = _adamw_t(w_in[0].T, g_in, m_w_in[0].T, v_w_in[0].T, "adamw_w_in")
    grads["w_in"], delta["w_in"], new_m["w_in"], new_v["w_in"] = g_in, dlt.T[None], nm.T[None], nv.T[None]

    packed = _pack_small(small_g)
    packed = jnp.concatenate([packed[:, :-1], loss_local.reshape(1, 1)], axis=1)
    reduced = _all_reduce_small(jnp.broadcast_to(packed, (8, packed.shape[1])), "small_all_reduce")
    grads.update(_unpack_small(reduced, small))
    loss = reduced[0, -1]

    pw, pm, pv = _pack_small(small), _pack_small({n: mom_m[n] for n in SMALL_NAMES}), _pack_small({n: mom_v[n] for n in SMALL_NAMES})
    rep8 = lambda a: jnp.broadcast_to(a, (8, a.shape[1]))
    dlt, nm, nv = _adamw(rep8(pw), rep8(reduced), rep8(pm), rep8(pv), "adamw_small")
    for tree, flat in ((delta, dlt), (new_m, nm), (new_v, nv)):
        tree.update(_unpack_small(flat[0:1], small))
    for n in BIG_NAMES:
        grads[n] = grads[n][None]
    return (loss, grad_x[None], *[grads[n] for n in WEIGHT_ORDER], *[delta[n] for n in WEIGHT_ORDER],
            *[new_m[n] for n in WEIGHT_ORDER], *[new_v[n] for n in WEIGHT_ORDER])
```

```python
import math

import jax
import jax.numpy as jnp
import numpy as np
from jax import lax
from jax.experimental import pallas as pl
from jax.experimental.pallas import tpu as pltpu

F32 = jnp.float32
BF16 = jnp.bfloat16

N_DEV = 8
HEAD_DIM = 64
A_Q_HEADS = 12
A_KV_HEADS = 4
A_GROUP = 3
B_HEADS = 12
C_HEADS = 4
C_HEAD_DIM = 128
WINDOW = 128
A_WIDTH = 768
A_KV_WIDTH = 256
B_WIDTH = 768
C_WIDTH = 512
EPS = 1e-6
NEG = -1e30

COL_QA, COL_KA, COL_VA = 0, 768, 1024
COL_QB, COL_KB, COL_VB = 1280, 2048, 2816
COL_QC = 3584
W_QKV = 4096
Q_SPLIT = 1280
COL_ZA, COL_ZB, COL_ZC = 0, 768, 1536
COL_GATE = W_Z = 2048
SRC_RANGES = ((0, 1280), (2048, 4352), (5132, 5644), (1280, 2048), (4352, 5120), (5644, 6156))
SRC_GATE = 6156
FB_SRC = 5120
FB_PAD = 128

ADAM_LR = 0.001
ADAM_B1 = 0.9
ADAM_B2 = 0.999
ADAM_EPS = 1e-08
ADAM_WD = 0.01
ADAM_STEP = 10

VMEM_BIG = 52 * 1024 * 1024
LANES = 128
MESH = pl.DeviceIdType.MESH


def _tile(n, pref, mult=128):
    if n <= pref:
        return n
    t = (pref // mult) * mult
    while t >= mult:
        if n % t == 0:
            return t
        t -= mult
    return n


def _params(sem=None, vmem=None):
    kw = {}
    if sem is not None:
        kw["dimension_semantics"] = sem
    if vmem is not None:
        kw["vmem_limit_bytes"] = vmem
    return pltpu.CompilerParams(**kw)


def _sigmoid(x):
    return 1.0 / (1.0 + jnp.exp(-x))


def _block_diag(hd):
    r = np.arange(LANES)
    return jnp.asarray((r[:, None] // hd) == (r[None, :] // hd), dtype=BF16)


def _seg_sum(t, bd):
    hi = t.astype(BF16)
    lo = (t - hi.astype(F32)).astype(BF16)
    outs = []
    for c in range(t.shape[1] // LANES):
        sl = slice(c * LANES, (c + 1) * LANES)
        outs.append(jnp.dot(hi[:, sl], bd, preferred_element_type=F32) + jnp.dot(lo[:, sl], bd, preferred_element_type=F32))
    return outs[0] if len(outs) == 1 else jnp.concatenate(outs, axis=1)


def _rmsnorm_fwd(x, gain, name):
    rows, d = x.shape
    bm = _tile(rows, 512, 8)

    def body(x_ref, g_ref, o_ref):
        xv = x_ref[...]
        ms = jnp.mean(xv * xv, axis=-1, keepdims=True)
        o_ref[...] = (xv * lax.rsqrt(ms + EPS) * g_ref[...]).astype(BF16)

    return pl.pallas_call(
        body, name=name, grid=(rows // bm,),
        in_specs=[pl.BlockSpec((bm, d), lambda i: (i, 0)), pl.BlockSpec((1, d), lambda i: (0, 0))],
        out_specs=pl.BlockSpec((bm, d), lambda i: (i, 0)),
        out_shape=jax.ShapeDtypeStruct((rows, d), BF16),
        compiler_params=_params(("parallel",)),
    )(x, gain)


def _rmsnorm_bwd(x, dhn, gain, dy, name):
    rows, d = x.shape
    bm = _tile(rows, 512, 8)
    with_dx = dy is not None

    def body(*refs):
        if with_dx:
            x_ref, dh_ref, g_ref, dy_ref, gx_ref, dg_ref = refs
        else:
            x_ref, dh_ref, g_ref, dg_ref = refs
        i = pl.program_id(0)
        xv = x_ref[...]
        rstd = lax.rsqrt(jnp.mean(xv * xv, axis=-1, keepdims=True) + EPS)
        xhat = xv * rstd
        dh = dh_ref[...]
        part = jnp.sum((dh * xhat).reshape(bm // 8, 8, d), axis=0)

        @pl.when(i == 0)
        def _():
            dg_ref[...] = part

        @pl.when(i > 0)
        def _():
            dg_ref[...] += part

        if with_dx:
            g = dh * g_ref[...]
            mean = jnp.mean(g * xhat, axis=-1, keepdims=True)
            gx_ref[...] = dy_ref[...] + rstd * (g - xhat * mean)

    row_spec = pl.BlockSpec((bm, d), lambda i: (i, 0))
    in_specs = [row_spec, row_spec, pl.BlockSpec((1, d), lambda i: (0, 0))]
    args = [x, dhn, gain]
    dg_spec = pl.BlockSpec((8, d), lambda i: (0, 0))
    dg_shape = jax.ShapeDtypeStruct((8, d), F32)
    if with_dx:
        in_specs.append(row_spec)
        args.append(dy)
        out_specs = [row_spec, dg_spec]
        out_shape = [jax.ShapeDtypeStruct((rows, d), F32), dg_shape]
    else:
        out_specs = [dg_spec]
        out_shape = [dg_shape]
    outs = pl.pallas_call(
        body, name=name, grid=(rows // bm,), in_specs=in_specs, out_specs=out_specs, out_shape=out_shape,
        compiler_params=_params(("arbitrary",), VMEM_BIG),
    )(*args)
    return outs if with_dx else (None, outs[0])


class _Comm:
    def __init__(self, kind, arrays):
        self.kind = kind
        self.arrays = list(arrays)
        self.n = len(self.arrays)

    def out_shapes(self):
        if self.kind == "gather":
            return [jax.ShapeDtypeStruct((N_DEV,) + a.shape, a.dtype) for a in self.arrays]
        return [jax.ShapeDtypeStruct(a.shape, a.dtype) for a in self.arrays]

    def scratch(self):
        return [pltpu.SemaphoreType.DMA((self.n, N_DEV - 1)), pltpu.SemaphoreType.DMA((self.n, N_DEV - 1)),
                pltpu.SemaphoreType.DMA((self.n,))]

    def _plan(self, ins, outs, sems, with_recvs):
        send_sems, recv_sems, local_sems = sems
        x, y, c = lax.axis_index("x"), lax.axis_index("y"), lax.axis_index("c")
        my = 4 * x + 2 * y + c
        gather = self.kind == "gather"
        local, sends, recvs = [], [], []
        for a in range(self.n):
            local.append(pltpu.make_async_copy(ins[a] if gather else ins[a].at[my], outs[a].at[my], local_sems.at[a]))
            for k in range(1, N_DEV):
                peer = (x ^ ((k >> 2) & 1), y ^ ((k >> 1) & 1), c ^ (k & 1))
                pid = 4 * peer[0] + 2 * peer[1] + peer[2]
                src = ins[a] if gather else ins[a].at[pid]
                sem = dict(send_sem=send_sems.at[a, k - 1], recv_sem=recv_sems.at[a, k - 1], device_id=peer, device_id_type=MESH)
                sends.append(pltpu.make_async_remote_copy(src_ref=src, dst_ref=outs[a].at[my], **sem))
                if with_recvs:
                    recvs.append(pltpu.make_async_remote_copy(src_ref=src, dst_ref=outs[a].at[pid], **sem))
        return local, sends, recvs

    def start(self, ins, outs, sems):
        local, sends, _ = self._plan(ins, outs, sems, False)
        for cp in local + sends:
            cp.start()

    def wait(self, ins, outs, sems):
        local, sends, recvs = self._plan(ins, outs, sems, True)
        for cp in recvs:
            cp.wait_recv()
        for cp in sends:
            cp.wait_send()
        for cp in local:
            cp.wait()


def _grid_edges(grid):
    first = last = None
    for ax, size in enumerate(grid):
        pid = pl.program_id(ax)
        f, l = pid == 0, pid == size - 1
        first = f if first is None else first & f
        last = l if last is None else last & l
    return first, last


def _hosted_call(body, comm, *, name, grid, in_specs, out_specs, out_shape, scratch_shapes, args, sem, vmem=None):
    in_specs, out_specs, out_shape, scratch_shapes = list(in_specs), list(out_specs), list(out_shape), list(scratch_shapes)
    if comm is None:
        res = pl.pallas_call(body, name=name, grid=grid, in_specs=in_specs, out_specs=out_specs, out_shape=out_shape,
                             scratch_shapes=scratch_shapes, compiler_params=_params(sem, vmem))(*args)
        return list(res), []
    n_in, n_out, n_scr, nc = len(in_specs), len(out_shape), len(scratch_shapes), comm.n

    def hosted(*refs):
        ins = refs[0:n_in]
        comm_in = refs[n_in:n_in + nc]
        outs = refs[n_in + nc:n_in + nc + n_out]
        comm_out = refs[n_in + nc + n_out:n_in + 2 * nc + n_out]
        scr = refs[n_in + 2 * nc + n_out:n_in + 2 * nc + n_out + n_scr]
        sems = refs[n_in + 2 * nc + n_out + n_scr:]
        first, last = _grid_edges(grid)

        @pl.when(first)
        def _():
            comm.start(comm_in, comm_out, sems)

        body(*ins, *outs, *scr)

        @pl.when(last)
        def _():
            comm.wait(comm_in, comm_out, sems)

    any_spec = pl.BlockSpec(memory_space=pl.ANY)
    res = pl.pallas_call(
        hosted, name=name, grid=grid, in_specs=in_specs + [any_spec] * nc, out_specs=out_specs + [any_spec] * nc,
        out_shape=out_shape + comm.out_shapes(), scratch_shapes=scratch_shapes + comm.scratch(),
        compiler_params=_params(("arbitrary",) * len(grid), vmem),
    )(*args, *comm.arrays)
    return list(res[0:n_out]), list(res[n_out:])


def _mm(a, b, *, grid, a_spec, b_spec, o_spec, o_shape, o_dtype, contract, name, add=None, add_spec=None, acc_shape=None,
        comm=None):
    nk = grid[2]
    has_add = add is not None

    def body(*refs):
        a_ref, b_ref = refs[0], refs[1]
        add_ref = refs[2] if has_add else None
        o_ref = refs[3] if has_add else refs[2]
        part = lax.dot_general(a_ref[...], b_ref[...], (contract, ((), ())), preferred_element_type=F32)
        if nk == 1:
            if has_add:
                part = part + add_ref[...]
            o_ref[...] = part.astype(o_dtype)
        else:
            acc = refs[-1]
            k = pl.program_id(2)

            @pl.when(k == 0)
            def _():
                acc[...] = part

            @pl.when(k > 0)
            def _():
                acc[...] += part

            @pl.when(k == nk - 1)
            def _():
                r = acc[...]
                if has_add:
                    r = r + add_ref[...]
                o_ref[...] = r.astype(o_dtype)

    in_specs = [a_spec, b_spec] + ([add_spec] if has_add else [])
    args = [a, b] + ([add] if has_add else [])
    scratch = [pltpu.VMEM(acc_shape, F32)] if nk > 1 else []
    outs, comm_outs = _hosted_call(
        body, comm, name=name, grid=grid, in_specs=in_specs, out_specs=[o_spec],
        out_shape=[jax.ShapeDtypeStruct(o_shape, o_dtype)], scratch_shapes=scratch, args=args,
        sem=("parallel", "parallel", "arbitrary"), vmem=VMEM_BIG)
    return outs[0] if comm is None else (outs[0], comm_outs)


def _mm_nn(a, b, *, bm, bn, bk, o_dtype, name, add=None, comm=None):
    m, kd = a.shape
    n = b.shape[1]
    bm, bn, bk = _tile(m, bm, 8), _tile(n, bn), _tile(kd, bk)
    o_spec = pl.BlockSpec((bm, bn), lambda i, j, k: (i, j))
    return _mm(a, b, grid=(m // bm, n // bn, kd // bk),
               a_spec=pl.BlockSpec((bm, bk), lambda i, j, k: (i, k)),
               b_spec=pl.BlockSpec((bk, bn), lambda i, j, k: (k, j)),
               o_spec=o_spec, o_shape=(m, n), o_dtype=o_dtype, contract=((1,), (0,)), name=name,
               add=add, add_spec=o_spec, acc_shape=(bm, bn), comm=comm)


def _mm_nt(a, b, *, bm, bn, bk, o_dtype, name, add=None, b_col0=0, comm=None):
    m, kd = a.shape
    n = b.shape[0]
    bm, bn, bk = _tile(m, bm, 8), _tile(n, bn), _tile(math.gcd(kd, b_col0), bk)
    kb0 = b_col0 // bk
    o_spec = pl.BlockSpec((bm, bn), lambda i, j, k: (i, j))
    return _mm(a, b, grid=(m // bm, n // bn, kd // bk),
               a_spec=pl.BlockSpec((bm, bk), lambda i, j, k: (i, k)),
               b_spec=pl.BlockSpec((bn, bk), lambda i, j, k: (j, kb0 + k)),
               o_spec=o_spec, o_shape=(m, n), o_dtype=o_dtype, contract=((1,), (1,)), name=name,
               add=add, add_spec=o_spec, acc_shape=(bm, bn), comm=comm)


def _mm_nt_sum(terms, *, bm, bn, bk, name, add=None, comm=None):
    m = terms[0][0].shape[0]
    n = terms[0][1].shape[0]
    bm, bn = _tile(m, bm, 8), _tile(n, bn)
    nt = (((1,), (1,)), ((), ()))
    plan, groups, start = [], [], 0
    for a, b, col0 in terms:
        kd = a.shape[1]
        tk = _tile(math.gcd(kd, col0), bk)
        steps = kd // tk
        last = groups[-1] if groups else None
        if last is not None and last[0] is b and last[4] == tk and (last[3] + last[2]) * tk == col0:
            last[2] += steps
        else:
            groups.append([b, start, steps, col0 // tk, tk])
        plan.append((start, steps, len(groups) - 1))
        start += steps
    nk = start
    nterm, ngroup, has_add = len(terms), len(groups), add is not None

    def body(*refs):
        a_refs, b_refs = refs[0:nterm], refs[nterm:nterm + ngroup]
        add_ref = refs[nterm + ngroup] if has_add else None
        o_ref, acc = refs[nterm + ngroup + has_add], refs[nterm + ngroup + has_add + 1]
        k = pl.program_id(2)
        for t, (s0, steps, grp) in enumerate(plan):
            @pl.when((k >= s0) & (k < s0 + steps))
            def _():
                part = lax.dot_general(a_refs[t][...], b_refs[grp][...], nt, preferred_element_type=F32)

                @pl.when(k == 0)
                def _():
                    acc[...] = part

                @pl.when(k > 0)
                def _():
                    acc[...] += part

        @pl.when(k == nk - 1)
        def _():
            o_ref[...] = acc[...] + add_ref[...] if has_add else acc[...]

    def a_spec(tk, s0, steps):
        return pl.BlockSpec((bm, tk), lambda i, j, k: (i, jnp.clip(k - s0, 0, steps - 1)))

    def b_spec(tk, s0, steps, off):
        return pl.BlockSpec((bn, tk), lambda i, j, k: (j, off + jnp.clip(k - s0, 0, steps - 1)))

    o_spec = pl.BlockSpec((bm, bn), lambda i, j, k: (i, j))
    in_specs = [a_spec(groups[grp][4], s0, steps) for s0, steps, grp in plan]
    in_specs += [b_spec(tk, s0, steps, cb0) for _, s0, steps, cb0, tk in groups]
    args = [a for a, _, _ in terms] + [grp[0] for grp in groups]
    if has_add:
        in_specs.append(o_spec)
        args.append(add)
    outs, comm_outs = _hosted_call(
        body, comm, name=name, grid=(m // bm, n // bn, nk), in_specs=in_specs,
        out_specs=[o_spec], out_shape=[jax.ShapeDtypeStruct((m, n), F32)],
        scratch_shapes=[pltpu.VMEM((bm, bn), F32)], args=args,
        sem=("parallel", "parallel", "arbitrary"), vmem=VMEM_BIG)
    return outs[0] if comm is None else (outs[0], comm_outs)


def _mm_tn(a, b, *, bm, bn, bk, o_dtype, name, comm=None):
    kd, m = a.shape
    n = b.shape[1]
    bm, bn, bk = _tile(m, bm), _tile(n, bn), _tile(kd, bk, 8)
    return _mm(a, b, grid=(m // bm, n // bn, kd // bk),
               a_spec=pl.BlockSpec((bk, bm), lambda i, j, k: (k, i)),
               b_spec=pl.BlockSpec((bk, bn), lambda i, j, k: (k, j)),
               o_spec=pl.BlockSpec((bm, bn), lambda i, j, k: (i, j)),
               o_shape=(m, n), o_dtype=o_dtype, contract=((0,), (0,)), name=name, acc_shape=(bm, bn), comm=comm)


def _branch_full(w8):
    kb, ds = w8.shape[0] // N_DEV, w8.shape[1]
    return w8.reshape(N_DEV, kb, ds).transpose(1, 0, 2).reshape(kb, N_DEV * ds)


def _branch_shards(g):
    kb, ds = g.shape[0], g.shape[1] // N_DEV
    return g.reshape(kb, N_DEV, ds).transpose(1, 0, 2).reshape(N_DEV * kb, ds)


def _headnorm_fwd(src, c0, width, bw, hd, gain, nflag, head_major, name):
    rows = src.shape[0]
    bm = _tile(rows, 2048 if bw <= 256 else 1024, 16)
    bd = _block_diag(hd)
    cb0 = c0 // bw

    def body(x_ref, g_ref, f_ref, bd_ref, o_ref):
        xv = x_ref[...].astype(F32)
        ss = _seg_sum(xv * xv, bd_ref[...])
        rstd = lax.rsqrt(ss * (1.0 / hd) + EPS)
        y = (xv * jnp.where(f_ref[...] > 0.0, rstd, 1.0) * g_ref[...]).astype(BF16)
        if head_major:
            for h in range(bw // HEAD_DIM):
                o_ref[h] = y[:, h * HEAD_DIM:(h + 1) * HEAD_DIM]
        else:
            o_ref[...] = y

    vec_spec = pl.BlockSpec((1, bw), lambda i, t: (0, t))
    if head_major:
        hpb = bw // HEAD_DIM
        out_spec = pl.BlockSpec((hpb, bm, HEAD_DIM), lambda i, t: (t, i, 0))
        out_shape = jax.ShapeDtypeStruct((width // HEAD_DIM, rows, HEAD_DIM), BF16)
    else:
        out_spec = pl.BlockSpec((bm, bw), lambda i, t: (i, t))
        out_shape = jax.ShapeDtypeStruct((rows, width), BF16)
    return pl.pallas_call(
        body, name=name, grid=(rows // bm, width // bw),
        in_specs=[pl.BlockSpec((bm, bw), lambda i, t: (i, cb0 + t)), vec_spec, vec_spec,
                  pl.BlockSpec((LANES, LANES), lambda i, t: (0, 0))],
        out_specs=out_spec, out_shape=out_shape,
        compiler_params=_params(("parallel", "parallel")),
    )(src, gain, nflag, bd)


def _headnorm_bwd(src, c0, width, bw, hd, gain, nflag, dyn, target, t0, name):
    rows = src.shape[0]
    bm = _tile(rows, 2048 if bw <= 256 else 1024, 16)
    bd = _block_diag(hd)
    cb0 = c0 // bw
    tb0 = t0 // bw
    aliased = target is not None

    def body(*refs):
        if aliased:
            x_ref, dy_ref, g_ref, f_ref, bd_ref, _, o_ref, dg_ref = refs
        else:
            x_ref, dy_ref, g_ref, f_ref, bd_ref, o_ref, dg_ref = refs
        i = pl.program_id(1)
        xv = x_ref[...].astype(F32)
        dyv = dy_ref[...]
        bdv = bd_ref[...]
        rstd = lax.rsqrt(_seg_sum(xv * xv, bdv) * (1.0 / hd) + EPS)
        xhat = xv * rstd
        g = dyv * g_ref[...]
        mean = _seg_sum(g * xhat, bdv) * (1.0 / hd)
        dx = jnp.where(f_ref[...] > 0.0, rstd * (g - xhat * mean), g)
        o_ref[...] = dx.astype(BF16)
        part = jnp.sum((dyv * xhat).reshape(bm // 8, 8, bw), axis=0)

        @pl.when(i == 0)
        def _():
            dg_ref[...] = part

        @pl.when(i > 0)
        def _():
            dg_ref[...] += part

    vec_spec = pl.BlockSpec((1, bw), lambda t, i: (0, t))
    in_specs = [pl.BlockSpec((bm, bw), lambda t, i: (i, cb0 + t)), pl.BlockSpec((bm, bw), lambda t, i: (i, t)),
                vec_spec, vec_spec, pl.BlockSpec((LANES, LANES), lambda t, i: (0, 0))]
    args = [src, dyn, gain, nflag, bd]
    aliases = {}
    if aliased:
        in_specs.append(pl.BlockSpec(memory_space=pl.ANY))
        args.append(target)
        aliases = {5: 0}
        o_shape = jax.ShapeDtypeStruct(target.shape, BF16)
    else:
        o_shape = jax.ShapeDtypeStruct((rows, width), BF16)
    out, dg = pl.pallas_call(
        body, name=name, grid=(width // bw, rows // bm), in_specs=in_specs,
        out_specs=[pl.BlockSpec((bm, bw), lambda t, i: (i, tb0 + t)), pl.BlockSpec((8, bw), lambda t, i: (0, t))],
        out_shape=[o_shape, jax.ShapeDtypeStruct((8, width), F32)],
        input_output_aliases=aliases,
        compiler_params=_params(("parallel", "arbitrary")),
    )(*args)
    return out, dg


def _fox_prep(pfb, bpad, name):
    s = pfb.shape[0]

    def body(p_ref, b_ref, c_ref):
        z = p_ref[...] + b_ref[...]
        logf = jnp.minimum(z, 0.0) - jnp.log(1.0 + jnp.exp(-jnp.abs(z)))
        x = logf.T[0:16, :]
        lane = lax.broadcasted_iota(jnp.int32, (16, s), 1)
        sh = 1
        while sh < s:
            x = x + jnp.where(lane >= sh, pltpu.roll(x, sh, 1), 0.0)
            sh *= 2
        c_ref[...] = x

    return pl.pallas_call(
        body, name=name, grid=(1,),
        in_specs=[pl.BlockSpec((s, FB_PAD), lambda i: (0, 0)), pl.BlockSpec((1, FB_PAD), lambda i: (0, 0))],
        out_specs=pl.BlockSpec((16, s), lambda i: (0, 0)),
        out_shape=jax.ShapeDtypeStruct((16, s), F32),
        compiler_params=_params(("arbitrary",)),
    )(pfb, bpad)


def _fox_prep_bwd(pfb, bpad, dct, name):
    s = pfb.shape[0]

    def body(p_ref, b_ref, dc_ref, df_ref, db_ref):
        zt = (p_ref[...] + b_ref[...]).T[0:16, :]
        y = dc_ref[...]
        lane = lax.broadcasted_iota(jnp.int32, (16, s), 1)
        sh = 1
        while sh < s:
            y = y + jnp.where(lane < s - sh, pltpu.roll(y, s - sh, 1), 0.0)
            sh *= 2
        dz = y * _sigmoid(-zt)
        db_ref[...] = jnp.broadcast_to(jnp.sum(dz, axis=1, keepdims=True), (16, FB_PAD))
        full = jnp.concatenate([dz, jnp.zeros((FB_PAD - 16, s), F32)], axis=0)
        df_ref[...] = full.T.astype(BF16)

    return pl.pallas_call(
        body, name=name, grid=(1,),
        in_specs=[pl.BlockSpec((s, FB_PAD), lambda i: (0, 0)), pl.BlockSpec((1, FB_PAD), lambda i: (0, 0)),
                  pl.BlockSpec((16, s), lambda i: (0, 0))],
        out_specs=[pl.BlockSpec((s, FB_PAD), lambda i: (0, 0)), pl.BlockSpec((16, FB_PAD), lambda i: (0, 0))],
        out_shape=[jax.ShapeDtypeStruct((s, FB_PAD), BF16), jax.ShapeDtypeStruct((16, FB_PAD), F32)],
        compiler_params=_params(("arbitrary",)),
    )(pfb, bpad, dct)


def _swa_window(n):
    ws = pl.multiple_of(jnp.maximum(n * WINDOW - WINDOW, 0), WINDOW)
    qi = lax.broadcasted_iota(jnp.int32, (WINDOW, 2 * WINDOW), 0)
    kj = lax.broadcasted_iota(jnp.int32, (WINDOW, 2 * WINDOW), 1)
    rel = qi + (n * WINDOW - ws) - kj
    valid = (rel >= 0) & (rel < WINDOW)
    return ws, valid, rel.astype(F32)


def _attn_a_fwd(qkv, sinks, slopes, name):
    s = qkv.shape[1]
    nb = s // WINDOW
    smem = pl.BlockSpec(memory_space=pltpu.SMEM)

    def body(sink_ref, slope_ref, q_ref, k_ref, v_ref, o_ref, lse_ref):
        n = pl.program_id(0)
        ws, valid, relf = _swa_window(n)
        outs = []
        for h in range(A_Q_HEADS):
            kvh = h // A_GROUP
            kw = k_ref[kvh, pl.ds(ws, 2 * WINDOW), :]
            vw = v_ref[kvh, pl.ds(ws, 2 * WINDOW), :]
            sc = lax.dot_general(q_ref[h], kw, (((1,), (1,)), ((), ())), preferred_element_type=F32)
            sc = jnp.where(valid, sc - slope_ref[h] * relf, NEG)
            sink = sink_ref[h]
            m = jnp.maximum(jnp.max(sc, axis=1, keepdims=True), sink)
            p = jnp.exp(sc - m)
            denom = jnp.sum(p, axis=1, keepdims=True) + jnp.exp(sink - m)
            pn = (p / denom).astype(BF16)
            outs.append(jnp.dot(pn, vw, preferred_element_type=F32))
            lse_ref[h] = jnp.broadcast_to(m + jnp.log(denom), (WINDOW, HEAD_DIM))
        o_ref[...] = jnp.concatenate(outs, axis=1)

    return pl.pallas_call(
        body, name=name, grid=(nb,),
        in_specs=[smem, smem,
                  pl.BlockSpec((A_Q_HEADS, WINDOW, HEAD_DIM), lambda n: (0, n, 0)),
                  pl.BlockSpec((A_KV_HEADS, s, HEAD_DIM), lambda n: (A_GROUP, 0, 0)),
                  pl.BlockSpec((A_KV_HEADS, s, HEAD_DIM), lambda n: (A_GROUP + 1, 0, 0))],
        out_specs=[pl.BlockSpec((WINDOW, A_WIDTH), lambda n: (n, 0)),
                   pl.BlockSpec((A_Q_HEADS, WINDOW, HEAD_DIM), lambda n: (0, n, 0))],
        out_shape=[jax.ShapeDtypeStruct((s, A_WIDTH), F32), jax.ShapeDtypeStruct((A_Q_HEADS, s, HEAD_DIM), F32)],
        compiler_params=_params(("parallel",), VMEM_BIG),
    )(sinks, slopes, qkv, qkv, qkv)


def _attn_a_bwd(qkv, do, lse, dd, sinks, slopes, name, comm=None):
    s = qkv.shape[1]
    nb = s // WINDOW
    smem = pl.BlockSpec(memory_space=pltpu.SMEM)
    last = nb - 1

    def body(sink_ref, slope_ref, q_ref, k_ref, v_ref, do_ref, lse_ref, dd_ref, dq_ref, dkv_ref, ds_ref, carry):
        n = pl.program_id(0)

        @pl.when(n == 0)
        def _():
            carry[...] = jnp.zeros(carry.shape, F32)
            ds_ref[...] = jnp.zeros(ds_ref.shape, F32)

        @pl.when(n < nb)
        def _():
            ws, valid, relf = _swa_window(n)
            dqs = []
            dkw = [None] * A_KV_HEADS
            dvw = [None] * A_KV_HEADS
            for h in range(A_Q_HEADS):
                kvh = h // A_GROUP
                qh = q_ref[h]
                doh = do_ref[h]
                kw = k_ref[kvh, pl.ds(ws, 2 * WINDOW), :]
                vw = v_ref[kvh, pl.ds(ws, 2 * WINDOW), :]
                lse_h = lse_ref[h]
                dd_h = dd_ref[h]
                sc = lax.dot_general(qh, kw, (((1,), (1,)), ((), ())), preferred_element_type=F32)
                sc = jnp.where(valid, sc - slope_ref[h] * relf, NEG)
                p = jnp.exp(sc - lse_h[:, 0:1])
                dp = lax.dot_general(doh, vw, (((1,), (1,)), ((), ())), preferred_element_type=F32)
                dsc = (p * (dp - dd_h[:, 0:1])).astype(BF16)
                pb = p.astype(BF16)
                dqs.append(jnp.dot(dsc, kw, preferred_element_type=F32))
                dk_h = jnp.dot(qh.T, dsc, preferred_element_type=F32)
                dv_h = jnp.dot(doh.T, pb, preferred_element_type=F32)
                dkw[kvh] = dk_h if dkw[kvh] is None else dkw[kvh] + dk_h
                dvw[kvh] = dv_h if dvw[kvh] is None else dvw[kvh] + dv_h
                psink = jnp.exp(sink_ref[h] - lse_h)
                ds_ref[h] += jnp.sum((-psink * dd_h).reshape(WINDOW // 8, 8, HEAD_DIM), axis=0)
            dq_ref[...] = jnp.concatenate(dqs, axis=1)
            win = jnp.concatenate(dkw + dvw, axis=0)
            first = win[:, 0:WINDOW]
            second = win[:, WINDOW:2 * WINDOW]
            dkv_ref[...] = (carry[...] + first).T
            carry[...] = jnp.where(n == 0, first, second)

        @pl.when(n == nb)
        def _():
            dkv_ref[...] = carry[...].T

    hm = lambda heads: pl.BlockSpec((heads, WINDOW, HEAD_DIM), lambda n: (0, jnp.minimum(n, last), 0))
    res = lambda blk: pl.BlockSpec((A_KV_HEADS, s, HEAD_DIM), lambda n: (blk, 0, 0))
    outs, comm_outs = _hosted_call(
        body, comm, name=name, grid=(nb + 1,),
        in_specs=[smem, smem, hm(A_Q_HEADS), res(A_GROUP), res(A_GROUP + 1), hm(A_Q_HEADS), hm(A_Q_HEADS), hm(A_Q_HEADS)],
        out_specs=[pl.BlockSpec((WINDOW, A_WIDTH), lambda n: (jnp.minimum(n, last), 0)),
                   pl.BlockSpec((WINDOW, 2 * A_KV_WIDTH), lambda n: (jnp.maximum(n - 1, 0), 0)),
                   pl.BlockSpec((A_Q_HEADS, 8, HEAD_DIM), lambda n: (0, 0, 0))],
        out_shape=[jax.ShapeDtypeStruct((s, A_WIDTH), F32), jax.ShapeDtypeStruct((s, 2 * A_KV_WIDTH), F32),
                   jax.ShapeDtypeStruct((A_Q_HEADS, 8, HEAD_DIM), F32)],
        scratch_shapes=[pltpu.VMEM((2 * A_KV_WIDTH, WINDOW), F32)],
        args=[sinks, slopes, qkv, qkv, qkv, do, lse, dd], sem=("arbitrary",), vmem=VMEM_BIG)
    return outs[0], outs[1], outs[2], comm_outs


def _attn_b_fwd(qkv, c3, name, comm=None):
    heads, s = qkv.shape[0] // 3, qkv.shape[1]
    hpairs = heads // 2
    bq = min(512, s)
    nq = s // bq
    nt = (((1,), (1,)), ((), ()))

    def body(q_ref, k_ref, v_ref, c_ref, o_ref, lse_ref, m_scr, l_scr, acc_scr):
        i = pl.program_id(1)
        r0 = pl.multiple_of(i * bq, bq)
        row = lax.broadcasted_iota(jnp.int32, (bq, bq), 0)
        col = lax.broadcasted_iota(jnp.int32, (bq, bq), 1)
        m_scr[...] = jnp.full((2, bq, LANES), NEG, F32)
        l_scr[...] = jnp.zeros((2, bq, LANES), F32)
        acc_scr[...] = jnp.zeros((2, bq, HEAD_DIM), F32)

        def step(j, masked):
            k0 = pl.multiple_of(j * bq, bq)
            for h2 in range(2):
                kv = k_ref[h2, pl.ds(k0, bq), :]
                vv = v_ref[h2, pl.ds(k0, bq), :]
                cq0 = c_ref[h2, :, pl.ds(r0, LANES)][:, 0:1]
                sc = lax.dot_general(q_ref[h2], kv, nt, preferred_element_type=F32)
                sc = sc + (cq0 - c_ref[h2, :, pl.ds(k0, bq)])
                if masked:
                    sc = jnp.where(col <= row, sc, NEG)
                m_prev = m_scr[h2]
                m_new = jnp.maximum(m_prev, jnp.max(sc, axis=1, keepdims=True))
                alpha = jnp.exp(m_prev - m_new)
                p = jnp.exp(sc - m_new[:, 0:1])
                l_scr[h2] = alpha * l_scr[h2] + jnp.sum(p, axis=1, keepdims=True)
                p_hi = p.astype(BF16)
                p_lo = (p - p_hi.astype(F32)).astype(BF16)
                pv = jnp.dot(p_hi, vv, preferred_element_type=F32) + jnp.dot(p_lo, vv, preferred_element_type=F32)
                acc_scr[h2] = acc_scr[h2] * alpha[:, 0:HEAD_DIM] + pv
                m_scr[h2] = m_new

        def loop_body(j, carry):
            step(j, False)
            return carry

        lax.fori_loop(0, i, loop_body, 0)
        step(i, True)
        outs = []
        for h2 in range(2):
            l = l_scr[h2]
            outs.append(acc_scr[h2] / l[:, 0:HEAD_DIM])
            lse_ref[h2] = (m_scr[h2] + jnp.log(l))[:, 0:HEAD_DIM]
        o_ref[...] = jnp.concatenate(outs, axis=1)

    res = lambda off: pl.BlockSpec((2, s, HEAD_DIM), lambda hp, i: (off + hp, 0, 0))
    outs, comm_outs = _hosted_call(
        body, comm, name=name, grid=(hpairs, nq),
        in_specs=[pl.BlockSpec((2, bq, HEAD_DIM), lambda hp, i: (hp, i, 0)), res(hpairs), res(2 * hpairs),
                  pl.BlockSpec((2, 1, s), lambda hp, i: (hp, 0, 0))],
        out_specs=[pl.BlockSpec((bq, 2 * HEAD_DIM), lambda hp, i: (i, hp)),
                   pl.BlockSpec((2, bq, HEAD_DIM), lambda hp, i: (hp, i, 0))],
        out_shape=[jax.ShapeDtypeStruct((s, heads * HEAD_DIM), F32), jax.ShapeDtypeStruct((heads, s, HEAD_DIM), F32)],
        scratch_shapes=[pltpu.VMEM((2, bq, LANES), F32), pltpu.VMEM((2, bq, LANES), F32), pltpu.VMEM((2, bq, HEAD_DIM), F32)],
        args=[qkv, qkv, qkv, c3], sem=("parallel", "parallel"), vmem=VMEM_BIG)
    return outs[0], outs[1], comm_outs


def _attn_b_bwd(qkv, do, lse, dd, c3, name, comm=None):
    heads, s = qkv.shape[0] // 3, qkv.shape[1]
    hpairs = heads // 2
    bq = min(512, s)
    nq = s // bq
    nt = (((1,), (1,)), ((), ()))
    tn = (((0,), (0,)), ((), ()))
    grid = (heads // 2, nq)

    def body(q_ref, k_ref, v_ref, do_ref, lse_ref, dd_ref, c_ref, dq_ref, dk_ref, dv_ref, dc_ref,
             dq_scr, dk_scr, dv_scr, dc_scr):
        j = pl.program_id(1)
        k0 = pl.multiple_of(j * bq, bq)
        row = lax.broadcasted_iota(jnp.int32, (bq, bq), 0)
        col = lax.broadcasted_iota(jnp.int32, (bq, bq), 1)

        @pl.when(j == 0)
        def _():
            dq_scr[...] = jnp.zeros(dq_scr.shape, F32)

        dk_scr[...] = jnp.zeros((2, HEAD_DIM, bq), F32)
        dv_scr[...] = jnp.zeros((2, HEAD_DIM, bq), F32)
        dc_scr[...] = jnp.zeros((2, 1, bq), F32)
        k_t = [k_ref[h2].T for h2 in range(2)]

        def step(i, masked):
            r0 = pl.multiple_of(i * bq, bq)
            for h2 in range(2):
                kv = k_ref[h2]
                vv = v_ref[h2]
                qv = q_ref[h2, pl.ds(r0, bq), :]
                dov = do_ref[h2, pl.ds(r0, bq), :]
                lse_v = lse_ref[h2, pl.ds(r0, bq), :][:, 0:1]
                dd_v = dd_ref[h2, pl.ds(r0, bq), :][:, 0:1]
                cq0 = c_ref[h2, :, pl.ds(r0, LANES)][:, 0:1]
                sc = lax.dot_general(qv, kv, nt, preferred_element_type=F32) + (cq0 - c_ref[h2, :, pl.ds(k0, bq)])
                if masked:
                    sc = jnp.where(col <= row, sc, NEG)
                p = jnp.exp(sc - lse_v)
                dp = lax.dot_general(dov, vv, nt, preferred_element_type=F32)
                dsc = p * (dp - dd_v)
                dsb = dsc.astype(BF16)
                dv_scr[h2] += jnp.dot(dov.T, p.astype(BF16), preferred_element_type=F32)
                dk_scr[h2] += jnp.dot(qv.T, dsb, preferred_element_type=F32)
                dq_scr[h2, :, pl.ds(r0, bq)] += jnp.dot(k_t[h2], dsb.T, preferred_element_type=F32)
                dc_scr[h2] -= jnp.sum(dsc, axis=0, keepdims=True)

        def loop_body(i, carry):
            step(i, False)
            return carry

        step(j, True)
        lax.fori_loop(j + 1, nq, loop_body, 0)
        dc_ref[...] = dc_scr[...]
        dk_ref[...] = jnp.concatenate([dk_scr[0].T, dk_scr[1].T], axis=1)
        dv_ref[...] = jnp.concatenate([dv_scr[0].T, dv_scr[1].T], axis=1)

        @pl.when(j == nq - 1)
        def _():
            dq_ref[...] = jnp.concatenate([dq_scr[0].T, dq_scr[1].T], axis=1)

    res = pl.BlockSpec((2, s, HEAD_DIM), lambda hp, j: (hp, 0, 0))
    blk = lambda off: pl.BlockSpec((2, bq, HEAD_DIM), lambda hp, j: (off + hp, j, 0))
    tm = jax.ShapeDtypeStruct((s, heads * HEAD_DIM), F32)
    in_specs = [res, blk(hpairs), blk(2 * hpairs), res, res, res, pl.BlockSpec((2, 1, s), lambda hp, j: (hp, 0, 0))]
    out_specs = [pl.BlockSpec((s, 2 * HEAD_DIM), lambda hp, j: (0, hp)),
                 pl.BlockSpec((bq, 2 * HEAD_DIM), lambda hp, j: (j, hp)),
                 pl.BlockSpec((bq, 2 * HEAD_DIM), lambda hp, j: (j, hp)),
                 pl.BlockSpec((2, 1, bq), lambda hp, j: (hp, 0, j))]
    out_shape = [tm, tm, tm, jax.ShapeDtypeStruct((heads, 1, s), F32)]
    scratch = [pltpu.VMEM((2, HEAD_DIM, s), F32), pltpu.VMEM((2, HEAD_DIM, bq), F32),
               pltpu.VMEM((2, HEAD_DIM, bq), F32), pltpu.VMEM((2, 1, bq), F32)]
    outs, comm_outs = _hosted_call(
        body, comm, name=name, grid=grid, in_specs=in_specs, out_specs=out_specs, out_shape=out_shape,
        scratch_shapes=scratch, args=[qkv, qkv, qkv, do, lse, dd, c3], sem=("parallel", "arbitrary"), vmem=VMEM_BIG)
    return outs[0], outs[1], outs[2], outs[3], comm_outs


def _attn_c_probs(qh, mkh):
    sc = lax.dot_general(qh, mkh, (((1,), (1,)), ((), ())), preferred_element_type=F32) * (C_HEAD_DIM ** -0.5)
    p = jnp.exp(sc - jnp.max(sc, axis=1, keepdims=True))
    return p / jnp.sum(p, axis=1, keepdims=True)


def _attn_c_fwd(q, mkv, name):
    s = q.shape[0]
    m = mkv.shape[0]
    bq = _tile(s, 512, 8)

    def body(q_ref, mk_ref, mv_ref, o_ref):
        outs = []
        for h in range(C_HEADS):
            sl = slice(h * C_HEAD_DIM, (h + 1) * C_HEAD_DIM)
            pn = _attn_c_probs(q_ref[:, sl], mk_ref[:, sl]).astype(BF16)
            outs.append(jnp.dot(pn, mv_ref[:, sl], preferred_element_type=F32))
        o_ref[...] = jnp.concatenate(outs, axis=1)

    return pl.pallas_call(
        body, name=name, grid=(s // bq,),
        in_specs=[pl.BlockSpec((bq, C_WIDTH), lambda i: (i, 0)), pl.BlockSpec((m, C_WIDTH), lambda i: (0, 0)),
                  pl.BlockSpec((m, C_WIDTH), lambda i: (0, 1))],
        out_specs=pl.BlockSpec((bq, C_WIDTH), lambda i: (i, 0)),
        out_shape=jax.ShapeDtypeStruct((s, C_WIDTH), F32),
        compiler_params=_params(("parallel",)),
    )(q, mkv, mkv)


def _attn_c_bwd(q, mkv, do, name):
    s = q.shape[0]
    m = mkv.shape[0]
    bq = _tile(s, 512, 8)
    tn = (((0,), (0,)), ((), ()))

    def body(q_ref, mk_ref, mv_ref, do_ref, dq_ref, dm_ref):
        i = pl.program_id(0)

        @pl.when(i == 0)
        def _():
            dm_ref[...] = jnp.zeros(dm_ref.shape, F32)

        dqs = []
        for h in range(C_HEADS):
            sl = slice(h * C_HEAD_DIM, (h + 1) * C_HEAD_DIM)
            qh, mkh, mvh, doh = q_ref[:, sl], mk_ref[:, sl], mv_ref[:, sl], do_ref[:, sl]
            pn = _attn_c_probs(qh, mkh)
            dp = lax.dot_general(doh, mvh, (((1,), (1,)), ((), ())), preferred_element_type=F32)
            dsc = (pn * (dp - jnp.sum(pn * dp, axis=1, keepdims=True)) * (C_HEAD_DIM ** -0.5)).astype(BF16)
            dqs.append(jnp.dot(dsc, mkh, preferred_element_type=F32))
            dm_ref[:, sl] += lax.dot_general(dsc, qh, tn, preferred_element_type=F32)
            sv = slice(C_WIDTH + h * C_HEAD_DIM, C_WIDTH + (h + 1) * C_HEAD_DIM)
            dm_ref[:, sv] += lax.dot_general(pn.astype(BF16), doh, tn, preferred_element_type=F32)
        dq_ref[...] = jnp.concatenate(dqs, axis=1)

    row = pl.BlockSpec((bq, C_WIDTH), lambda i: (i, 0))
    return pl.pallas_call(
        body, name=name, grid=(s // bq,),
        in_specs=[row, pl.BlockSpec((m, C_WIDTH), lambda i: (0, 0)), pl.BlockSpec((m, C_WIDTH), lambda i: (0, 1)), row],
        out_specs=[row, pl.BlockSpec((m, 2 * C_WIDTH), lambda i: (0, 0))],
        out_shape=[jax.ShapeDtypeStruct((s, C_WIDTH), F32), jax.ShapeDtypeStruct((m, 2 * C_WIDTH), F32)],
        compiler_params=_params(("arbitrary",)),
    )(q, mkv, mkv, do)


def _gate_fwd(y, proj, zc0, bw, name):
    rows, width = y.shape
    bm = _tile(rows, 2048 if bw <= 256 else 1024, 16)
    cb0 = zc0 // bw

    def body(y_ref, z_ref, o_ref):
        z = z_ref[...].astype(F32)
        o_ref[...] = (y_ref[...] * (z * _sigmoid(z))).astype(BF16)

    return pl.pallas_call(
        body, name=name, grid=(rows // bm, width // bw),
        in_specs=[pl.BlockSpec((bm, bw), lambda i, t: (i, t)), pl.BlockSpec((bm, bw), lambda i, t: (i, cb0 + t))],
        out_specs=pl.BlockSpec((bm, bw), lambda i, t: (i, t)),
        out_shape=jax.ShapeDtypeStruct((rows, width), BF16),
        compiler_params=_params(("parallel", "parallel")),
    )(y, proj)


def _gate_bwd(dsv, y, proj, zc0, bw, dproj, t0, head_major, name):
    rows, width = y.shape
    bm = _tile(rows, 2048 if bw <= 256 else 1024, 16)
    cb0 = zc0 // bw
    tb0 = t0 // bw
    bd = _block_diag(HEAD_DIM)
    hpb = bw // HEAD_DIM

    def body(*refs):
        if head_major:
            ds_ref, y_ref, z_ref, bd_ref, _, dp_ref, dy_ref, dd_ref = refs
        else:
            ds_ref, y_ref, z_ref, _, dp_ref, dy_ref = refs
        z = z_ref[...].astype(F32)
        sig = _sigmoid(z)
        dsx = ds_ref[...]
        yv = y_ref[...]
        dy = dsx * (z * sig)
        dp_ref[...] = (dsx * yv * (sig * (1.0 + z * (1.0 - sig)))).astype(BF16)
        if head_major:
            dyb = dy.astype(BF16)
            dd = _seg_sum(dyb.astype(F32) * yv, bd_ref[...])
            for h in range(hpb):
                sl = slice(h * HEAD_DIM, (h + 1) * HEAD_DIM)
                dy_ref[h] = dyb[:, sl]
                dd_ref[h] = dd[:, sl]
        else:
            dy_ref[...] = dy.astype(BF16)

    tile = pl.BlockSpec((bm, bw), lambda i, t: (i, t))
    ztile = pl.BlockSpec((bm, bw), lambda i, t: (i, cb0 + t))
    ttile = pl.BlockSpec((bm, bw), lambda i, t: (i, tb0 + t))
    any_spec = pl.BlockSpec(memory_space=pl.ANY)
    dp_shape = jax.ShapeDtypeStruct(dproj.shape, BF16)
    if head_major:
        hm_spec = pl.BlockSpec((hpb, bm, HEAD_DIM), lambda i, t: (t, i, 0))
        nh = width // HEAD_DIM
        outs = pl.pallas_call(
            body, name=name, grid=(rows // bm, width // bw),
            in_specs=[tile, tile, ztile, pl.BlockSpec((LANES, LANES), lambda i, t: (0, 0)), any_spec],
            out_specs=[ttile, hm_spec, hm_spec],
            out_shape=[dp_shape, jax.ShapeDtypeStruct((nh, rows, HEAD_DIM), BF16),
                       jax.ShapeDtypeStruct((nh, rows, HEAD_DIM), F32)],
            input_output_aliases={4: 0},
            compiler_params=_params(("parallel", "parallel")),
        )(dsv, y, proj, bd, dproj)
        return outs[0], outs[1], outs[2]
    outs = pl.pallas_call(
        body, name=name, grid=(rows // bm, width // bw),
        in_specs=[tile, tile, ztile, any_spec],
        out_specs=[ttile, tile],
        out_shape=[dp_shape, jax.ShapeDtypeStruct((rows, width), BF16)],
        input_output_aliases={3: 0},
        compiler_params=_params(("parallel", "parallel")),
    )(dsv, y, proj, dproj)
    return outs[0], outs[1], None


def _merge_fwd(proj, ua, ub, uc, name):
    rows, d = ua.shape
    bm = _tile(rows, 1024, 16)
    bw = _tile(d, 512)
    g0 = COL_GATE // bw
    gstep = d // bw

    def body(la_ref, lb_ref, lc_ref, ua_ref, ub_ref, uc_ref, o_ref, ga_ref, gb_ref, gc_ref):
        y = None
        for l_ref, u_ref, g_ref in ((la_ref, ua_ref, ga_ref), (lb_ref, ub_ref, gb_ref), (lc_ref, uc_ref, gc_ref)):
            g = _sigmoid(l_ref[...].astype(F32))
            g_ref[...] = g.astype(BF16)
            term = g * u_ref[...].astype(F32)
            y = term if y is None else y + term
        o_ref[...] = y.astype(BF16)

    tile = pl.BlockSpec((bm, bw), lambda i, t: (i, t))
    gate = lambda b: pl.BlockSpec((bm, bw), lambda i, t: (i, g0 + b * gstep + t))
    shape = jax.ShapeDtypeStruct((rows, d), BF16)
    return pl.pallas_call(
        body, name=name, grid=(rows // bm, d // bw),
        in_specs=[gate(0), gate(1), gate(2), tile, tile, tile],
        out_specs=[tile] * 4, out_shape=[shape] * 4,
        compiler_params=_params(("parallel", "parallel")),
    )(proj, proj, proj, ua, ub, uc)


def _merge_bwd(dym, us, gs, name):
    rows, d = dym.shape
    bm = _tile(rows, 256, 16)

    def body(dy_ref, ua_ref, ub_ref, uc_ref, ga_ref, gb_ref, gc_ref, dg_ref, da_ref, db_ref, dc_ref):
        dyv = dy_ref[...]
        for b, (u_ref, g_ref, du_ref) in enumerate(((ua_ref, ga_ref, da_ref), (ub_ref, gb_ref, db_ref), (uc_ref, gc_ref, dc_ref))):
            g = g_ref[...].astype(F32)
            du_ref[...] = (g * dyv).astype(BF16)
            dg_ref[:, b * d:(b + 1) * d] = (dyv * u_ref[...].astype(F32) * g * (1.0 - g)).astype(BF16)

    tile = pl.BlockSpec((bm, d), lambda i: (i, 0))
    shape = jax.ShapeDtypeStruct((rows, d), BF16)
    outs = pl.pallas_call(
        body, name=name, grid=(rows // bm,),
        in_specs=[tile] * 7,
        out_specs=[pl.BlockSpec((bm, 3 * d), lambda i: (i, 0)), tile, tile, tile],
        out_shape=[jax.ShapeDtypeStruct((rows, 3 * d), BF16), shape, shape, shape],
        compiler_params=_params(("parallel",), VMEM_BIG),
    )(dym, *us, *gs)
    return outs[0], outs[1], outs[2], outs[3]


def _out_proj_loss(ym, wo, x, target, name):
    m, d = x.shape
    bm, bn = _tile(m, 1024, 16), _tile(d, 1024)
    grid = (m // bm, d // bn)

    def body(a_ref, b_ref, x_ref, t_ref, dy_ref, dyb_ref, l_ref):
        first, _ = _grid_edges(grid)
        y = jnp.dot(a_ref[...], b_ref[...], preferred_element_type=F32) + x_ref[...]
        diff = y - t_ref[...]
        dy = diff * (1.0 / d)
        dy_ref[...] = dy
        dyb_ref[...] = dy.astype(BF16)
        sq = diff * diff
        part = sq[:, 0:LANES]
        for c in range(1, bn // LANES):
            part = part + sq[:, c * LANES:(c + 1) * LANES]
        part = jnp.sum(part.reshape(bm // 8, 8, LANES), axis=0)

        @pl.when(first)
        def _():
            l_ref[...] = part

        @pl.when(jnp.logical_not(first))
        def _():
            l_ref[...] += part

    tile = pl.BlockSpec((bm, bn), lambda i, j: (i, j))
    return pl.pallas_call(
        body, name=name, grid=grid,
        in_specs=[pl.BlockSpec((bm, d), lambda i, j: (i, 0)), pl.BlockSpec((d, bn), lambda i, j: (0, j)), tile, tile],
        out_specs=[tile, tile, pl.BlockSpec((8, LANES), lambda i, j: (0, 0))],
        out_shape=[jax.ShapeDtypeStruct((m, d), F32), jax.ShapeDtypeStruct((m, d), BF16),
                   jax.ShapeDtypeStruct((8, LANES), F32)],
        compiler_params=_params(("arbitrary", "arbitrary"), VMEM_BIG),
    )(ym, wo, x, target)


def _row(vec, reps=1):
    return jnp.tile(vec.reshape(1, -1).astype(F32), (1, reps))


def _local_step(x, mem, target, small, wg, shards=None):
    s, d = x.shape
    dist = shards is not None
    wg = dict(wg)
    ones = lambda n: jnp.ones((1, n), F32)
    zeros = lambda n: jnp.zeros((1, n), F32)
    scale_ab = HEAD_DIM ** -0.5
    split8 = lambda g: g.reshape(N_DEV, g.shape[0] // N_DEV, g.shape[1])
    flat8 = lambda g: g.reshape(g.shape[0] * g.shape[1], g.shape[2])
    gather = lambda names: _Comm("gather", [shards[n] for n in names]) if dist else None
    g = {}

    def scatter(names):
        return _Comm("scatter", [split8(g[n]) for n in names]) if dist else None

    def hosted(result, names, store):
        if not dist:
            return result
        out, got = result
        store.update(zip(names, got))
        return out

    hn = _rmsnorm_fwd(x, small["norm_gain"], "rms_x_fwd")
    got = {}
    proj = hosted(_mm_nn(hn, wg["qkv"], bm=1024, bn=1024, bk=d, o_dtype=BF16, name="proj_qkv",
                         comm=gather(("wa", "wb"))), ("wa", "wb"), got)
    wg.update({n: flat8(a) for n, a in got.items()})
    pfb = _mm_nn(hn, wg["wf"], bm=1024, bn=FB_PAD, bk=d, o_dtype=F32, name="proj_fb")
    mn = _rmsnorm_fwd(mem, small["mem_norm_gain"], "rms_mem_fwd")
    mkv = _mm_nn(mn, wg["wk"], bm=256, bn=1024, bk=d, o_dtype=F32, name="mem_kv")

    gain_a = jnp.concatenate([_row(small["q_gain_a"], A_Q_HEADS) * scale_ab, _row(small["k_gain_a"], A_KV_HEADS), ones(A_KV_WIDTH)], axis=1)
    flag_a = jnp.concatenate([ones(A_WIDTH + A_KV_WIDTH), zeros(A_KV_WIDTH)], axis=1)
    qkv_a = _headnorm_fwd(proj, COL_QA, 1280, 1280, HEAD_DIM, gain_a, flag_a, True, "hn_a_fwd")
    gain_b = jnp.concatenate([_row(small["q_gain_b"], B_HEADS) * scale_ab, _row(small["k_gain_b"], B_HEADS), ones(B_WIDTH)], axis=1)
    flag_b = jnp.concatenate([ones(2 * B_WIDTH), zeros(B_WIDTH)], axis=1)
    qkv_b = _headnorm_fwd(proj, COL_QB, 2304, 256, HEAD_DIM, gain_b, flag_b, True, "hn_b_fwd")
    gain_cq = _row(small["q_gain_c"], C_HEADS)
    q_c = _headnorm_fwd(proj, COL_QC, C_WIDTH, C_WIDTH, C_HEAD_DIM, gain_cq, ones(C_WIDTH), False, "hn_cq_fwd")
    gain_ck = jnp.concatenate([_row(small["k_gain_c"], C_HEADS), ones(C_WIDTH)], axis=1)
    flag_ck = jnp.concatenate([ones(C_WIDTH), zeros(C_WIDTH)], axis=1)
    mkvn = _headnorm_fwd(mkv, 0, 2 * C_WIDTH, 2 * C_WIDTH, C_HEAD_DIM, gain_ck, flag_ck, False, "hn_ck_fwd")


    bpad = jnp.pad(small["b_forget"].reshape(1, -1), ((0, 0), (0, FB_PAD - B_HEADS)))
    c16 = _fox_prep(pfb, bpad, "fox_prep")
    c3 = c16[0:B_HEADS].reshape(B_HEADS, 1, s)

    sinks = small["sinks_a"].reshape(-1)
    slopes = jnp.exp2(-8.0 * jnp.arange(1, A_Q_HEADS + 1, dtype=F32) / A_Q_HEADS)
    y_a, lse_a = _attn_a_fwd(qkv_a, sinks, slopes, "attn_a_fwd")
    y_b, lse_b, got_zg = _attn_b_fwd(qkv_b, c3, "attn_b_fwd", comm=gather(("zg",)))
    if dist:
        wg["zg"] = flat8(got_zg[0])
    y_c = _attn_c_fwd(q_c, mkvn, "attn_c_fwd")

    got = {}
    pzg = hosted(_mm_nn(hn, wg["zg"], bm=1024, bn=1024, bk=d, o_dtype=BF16, name="proj_zg", comm=gather(("wo", "wc"))),
                 ("wo", "wc"), got)
    wg.update({n: flat8(a) for n, a in got.items()})

    s_a = _gate_fwd(y_a, pzg, COL_ZA, 256, "gate_a_fwd")
    s_b = _gate_fwd(y_b, pzg, COL_ZB, 256, "gate_b_fwd")
    s_c = _gate_fwd(y_c, pzg, COL_ZC, 512, "gate_c_fwd")
    w_a, w_b, w_c = _branch_full(wg["wa"]), _branch_full(wg["wb"]), _branch_full(wg["wc"])
    u_a = _mm_nn(s_a, w_a, bm=1024, bn=2048, bk=A_WIDTH, o_dtype=BF16, name="branch_a_fwd")
    u_b = _mm_nn(s_b, w_b, bm=1024, bn=2048, bk=B_WIDTH, o_dtype=BF16, name="branch_b_fwd")
    u_c = _mm_nn(s_c, w_c, bm=1024, bn=2048, bk=C_WIDTH, o_dtype=BF16, name="branch_c_fwd")
    ym, gate_a, gate_b, gate_c = _merge_fwd(pzg, u_a, u_b, u_c, "merge_fwd")
    dy, dyb, lpart = _out_proj_loss(ym, wg["wo"], x, target, "out_proj_loss")
    loss = 0.5 / d * jnp.sum(lpart)

    dym = _mm_nt(dyb, wg["wo"], bm=1024, bn=1024, bk=d, o_dtype=F32, name="out_proj_bwd_act")
    g["wo"] = _mm_tn(ym, dyb, bm=512, bn=1024, bk=s, o_dtype=BF16, name="out_proj_bwd_w")

    dgate, du_a, du_b, du_c = _merge_bwd(dym, (u_a, u_b, u_c), (gate_a, gate_b, gate_c), "merge_bwd")
    parts = {}
    g["wm_g"] = hosted(_mm_tn(hn, dgate, bm=512, bn=1024, bk=s, o_dtype=BF16, name="proj_gate_bwd_w",
                              comm=scatter(("wo",))), ("wo",), parts)

    ds_a = _mm_nt(du_a, w_a, bm=1024, bn=A_WIDTH, bk=d, o_dtype=F32, name="branch_a_bwd_act")
    ds_b = _mm_nt(du_b, w_b, bm=1024, bn=B_WIDTH, bk=d, o_dtype=F32, name="branch_b_bwd_act")
    ds_c = _mm_nt(du_c, w_c, bm=1024, bn=C_WIDTH, bk=d, o_dtype=F32, name="branch_c_bwd_act")
    g["wa"] = _branch_shards(_mm_tn(s_a, du_a, bm=A_WIDTH, bn=1024, bk=s, o_dtype=BF16, name="branch_a_bwd_w"))
    g["wb"] = _branch_shards(_mm_tn(s_b, du_b, bm=B_WIDTH, bn=1024, bk=s, o_dtype=BF16, name="branch_b_bwd_w"))
    g["wc"] = _branch_shards(_mm_tn(s_c, du_c, bm=C_WIDTH, bn=1024, bk=s, o_dtype=BF16, name="branch_c_bwd_w"))

    dz = lax.empty((s, W_Z), BF16)
    dz, do_a, dd_a = _gate_bwd(ds_a, y_a, pzg, COL_ZA, 256, dz, COL_ZA, True, "gate_a_bwd")
    dz, do_b, dd_b = _gate_bwd(ds_b, y_b, pzg, COL_ZB, 256, dz, COL_ZB, True, "gate_b_bwd")
    dz, do_c, _ = _gate_bwd(ds_c, y_c, pzg, COL_ZC, 512, dz, COL_ZC, False, "gate_c_bwd")
    g["wm_z"] = _mm_tn(hn, dz, bm=512, bn=1024, bk=s, o_dtype=BF16, name="proj_z_bwd_w")

    names = ("wa", "wb", "wc")
    dq_a, dkv_a, dsink, got = _attn_a_bwd(qkv_a, do_a, lse_a, dd_a, sinks, slopes, "attn_a_bwd", comm=scatter(names))
    parts.update(zip(names, got))
    names = ("wm_g", "wm_z")
    dq_b, dk_b, dv_b, dc3, got = _attn_b_bwd(qkv_b, do_b, lse_b, dd_b, c3, "attn_b_bwd", comm=scatter(names))
    parts.update(zip(names, got))
    dq_c, dmkvn = _attn_c_bwd(q_c, mkvn, do_c, "attn_c_bwd")

    dqkv = lax.empty((s, W_QKV), BF16)
    dqkv, dg_qa = _headnorm_bwd(proj, COL_QA, A_WIDTH, 256, HEAD_DIM, gain_a[:, 0:768], flag_a[:, 0:768], dq_a, dqkv, COL_QA, "hn_qa_bwd")
    dqkv, dg_kva = _headnorm_bwd(proj, COL_KA, 512, 256, HEAD_DIM, gain_a[:, 768:1280], flag_a[:, 768:1280], dkv_a, dqkv, COL_KA, "hn_kva_bwd")
    dqkv, dg_qb = _headnorm_bwd(proj, COL_QB, B_WIDTH, 256, HEAD_DIM, gain_b[:, 0:768], flag_b[:, 0:768], dq_b, dqkv, COL_QB, "hn_qb_bwd")
    dqkv, dg_kb = _headnorm_bwd(proj, COL_KB, B_WIDTH, 256, HEAD_DIM, gain_b[:, 768:1536], flag_b[:, 768:1536], dk_b, dqkv, COL_KB, "hn_kb_bwd")
    dqkv, _ = _headnorm_bwd(proj, COL_VB, B_WIDTH, 256, HEAD_DIM, gain_b[:, 1536:2304], flag_b[:, 1536:2304], dv_b, dqkv, COL_VB, "hn_vb_bwd")
    dqkv, dg_qc = _headnorm_bwd(proj, COL_QC, C_WIDTH, 512, C_HEAD_DIM, gain_cq, ones(C_WIDTH), dq_c, dqkv, COL_QC, "hn_qc_bwd")
    dmkv, dg_kc = _headnorm_bwd(mkv, 0, 2 * C_WIDTH, 2 * C_WIDTH, C_HEAD_DIM, gain_ck, flag_ck, dmkvn, None, 0, "hn_kc_bwd")

    dct = jnp.pad(dc3.reshape(B_HEADS, s), ((0, 16 - B_HEADS), (0, 0)))
    dfb, dbf = _fox_prep_bwd(pfb, bpad, dct, "fox_prep_bwd")

    dmn = _mm_nt(dmkv, wg["wk"], bm=256, bn=1024, bk=1024, o_dtype=F32, name="mem_kv_bwd_act")
    g["wk"] = _mm_tn(mn, dmkv, bm=512, bn=1024, bk=mem.shape[0], o_dtype=BF16, name="mem_kv_bwd_w")
    _, dg_mem = _rmsnorm_bwd(mem, dmn, small["mem_norm_gain"], None, "rms_mem_bwd")

    g["wm_qkv"] = _mm_tn(hn, dqkv, bm=512, bn=1024, bk=s, o_dtype=BF16, name="proj_qkv_bwd_w")
    g["wf"] = _mm_tn(hn, dfb, bm=512, bn=FB_PAD, bk=s, o_dtype=BF16, name="proj_fb_bwd_w")
    half = Q_SPLIT
    g["wm_q1"], g["wm_q2"] = g["wm_qkv"][:, 0:half], g["wm_qkv"][:, half:W_QKV]
    names = ("wm_q1",)
    dhn = hosted(_mm_nt_sum([(dqkv, wg["qkv"], 0), (dfb, wg["wf"], 0)], bm=1024, bn=1024, bk=2048,
                            name="proj_qkv_bwd_act", comm=scatter(names)), names, parts)
    names = ("wm_q2", "wf", "wk")
    dhn = hosted(_mm_nt_sum([(dz, wg["zg"], COL_ZA), (dgate, wg["zg"], COL_GATE)], bm=1024, bn=1024, bk=2048,
                            name="proj_zg_bwd_act", add=dhn, comm=scatter(names)), names, parts)
    if dist:
        g = parts
    grad_x, dg_x = _rmsnorm_bwd(x, dhn, small["norm_gain"], dy, "rms_x_bwd")

    fold = lambda part, heads, hd: jnp.sum(jnp.sum(part, axis=0).reshape(heads, hd), axis=0).reshape(1, hd)
    small_grads = {
        "norm_gain": jnp.sum(dg_x, axis=0).reshape(1, d),
        "mem_norm_gain": jnp.sum(dg_mem, axis=0).reshape(1, d),
        "b_forget": dbf[0:B_HEADS, 0].reshape(1, B_HEADS),
        "q_gain_a": fold(dg_qa, A_Q_HEADS, HEAD_DIM) * scale_ab,
        "k_gain_a": fold(dg_kva[:, 0:A_KV_WIDTH], A_KV_HEADS, HEAD_DIM),
        "sinks_a": (jnp.sum(dsink, axis=(1, 2)) * (1.0 / HEAD_DIM)).reshape(1, A_Q_HEADS),
        "q_gain_b": fold(dg_qb, B_HEADS, HEAD_DIM) * scale_ab,
        "k_gain_b": fold(dg_kb, B_HEADS, HEAD_DIM),
        "q_gain_c": fold(dg_qc, C_HEADS, C_HEAD_DIM),
        "k_gain_c": fold(dg_kc[:, 0:C_WIDTH], C_HEADS, C_HEAD_DIM),
    }
    return loss, grad_x, small_grads, g


def _coords():
    return lax.axis_index("x"), lax.axis_index("y"), lax.axis_index("c")


def _all_gather(shards, name):
    n = len(shards)

    def body(*refs):
        ins = refs[0:n]
        outs = refs[n:2 * n]
        send_sems, recv_sems, local_sems = refs[2 * n:2 * n + 3]
        x, y, c = _coords()
        me, sibling = (x, y, c), (x, y, 1 - c)
        chips = [(1 - x, y), (x, 1 - y), (1 - x, 1 - y)]
        idx = lambda p: 4 * p[0] + 2 * p[1] + p[2]

        def copy(a, k, block, to, src=None):
            slot = outs[a].at[idx(block)]
            return pltpu.make_async_remote_copy(
                src_ref=slot if src is None else src, dst_ref=slot,
                send_sem=send_sems.at[a, k], recv_sem=recv_sems.at[a, k], device_id=to, device_id_type=MESH)

        mine = [pltpu.make_async_copy(ins[a], outs[a].at[idx(me)], local_sems.at[a]) for a in range(n)]
        for cp in mine:
            cp.start()
        first = []
        for a in range(n):
            first.append(copy(a, 0, me, sibling, src=ins[a]))
            first += [copy(a, 1 + j, me, (*chip, c), src=ins[a]) for j, chip in enumerate(chips)]
        for cp in first:
            cp.start()
        passed = []
        for j, chip in enumerate(chips):
            for a in range(n):
                copy(a, 1 + j, (*chip, c), me).wait_recv()
                fwd = copy(a, 4 + j, (*chip, c), sibling)
                fwd.start()
                passed.append(fwd)
        for a in range(n):
            copy(a, 0, sibling, me).wait_recv()
            for j, chip in enumerate(chips):
                copy(a, 4 + j, (*chip, 1 - c), me).wait_recv()
        for cp in first + passed:
            cp.wait_send()
        for cp in mine:
            cp.wait()

    any_spec = pl.BlockSpec(memory_space=pl.ANY)
    return pl.pallas_call(
        body, name=name,
        in_specs=[any_spec] * n, out_specs=[any_spec] * n,
        out_shape=[jax.ShapeDtypeStruct((N_DEV,) + sh.shape, sh.dtype) for sh in shards],
        scratch_shapes=[pltpu.SemaphoreType.DMA((n, 7)), pltpu.SemaphoreType.DMA((n, 7)), pltpu.SemaphoreType.DMA((n,))],
    )(*shards)


def _all_reduce_small(vec, name):
    p = vec.shape[1]

    def body(v_ref, o_ref, gather, send_sems, recv_sems):
        x, y, c = _coords()
        my = 4 * x + 2 * y + c
        peers = [(x ^ ((k >> 2) & 1), y ^ ((k >> 1) & 1), c ^ (k & 1)) for k in range(1, N_DEV)]
        gather[my] = v_ref[...]
        sends = [pltpu.make_async_remote_copy(
            src_ref=v_ref, dst_ref=gather.at[my], send_sem=send_sems.at[k], recv_sem=recv_sems.at[k],
            device_id=peer, device_id_type=MESH) for k, peer in enumerate(peers)]
        for cp in sends:
            cp.start()
        for k, peer in enumerate(peers):
            pid = 4 * peer[0] + 2 * peer[1] + peer[2]
            pltpu.make_async_remote_copy(
                src_ref=v_ref, dst_ref=gather.at[pid], send_sem=send_sems.at[k], recv_sem=recv_sems.at[k],
                device_id=peer, device_id_type=MESH).wait_recv()
        for cp in sends:
            cp.wait_send()
        total = gather[0]
        for j in range(1, N_DEV):
            total = total + gather[j]
        o_ref[...] = total

    vm = pl.BlockSpec(memory_space=pltpu.VMEM)
    return pl.pallas_call(
        body, name=name, in_specs=[vm], out_specs=vm,
        out_shape=jax.ShapeDtypeStruct((8, p), F32),
        scratch_shapes=[pltpu.VMEM((N_DEV, 8, p), F32), pltpu.SemaphoreType.DMA((7,)), pltpu.SemaphoreType.DMA((7,))],
    )(vec)[0:1]


def _sum_parts(parts, name):
    _, rows, cols = parts.shape
    br = _tile(rows, 64, 16)

    def body(p_ref, o_ref):
        total = p_ref[0].astype(F32)
        for j in range(1, N_DEV):
            total = total + p_ref[j].astype(F32)
        o_ref[...] = total

    return pl.pallas_call(
        body, name=name, grid=(rows // br,),
        in_specs=[pl.BlockSpec((N_DEV, br, cols), lambda i: (0, i, 0))],
        out_specs=pl.BlockSpec((br, cols), lambda i: (i, 0)),
        out_shape=jax.ShapeDtypeStruct((rows, cols), F32),
        compiler_params=_params(("parallel",), VMEM_BIG),
    )(parts)


def _adamw(w, g, m, v, name, br=32):
    rows, cols = w.shape
    br = min(br, rows)
    c1 = 1.0 / (1.0 - ADAM_B1 ** ADAM_STEP)
    c2 = 1.0 / (1.0 - ADAM_B2 ** ADAM_STEP)

    def body(w_ref, g_ref, m_ref, v_ref, d_ref, nm_ref, nv_ref):
        gv = g_ref[...]
        nm = ADAM_B1 * m_ref[...] + (1.0 - ADAM_B1) * gv
        nv = ADAM_B2 * v_ref[...] + (1.0 - ADAM_B2) * (gv * gv)
        d_ref[...] = -ADAM_LR * ((nm * c1) / (jnp.sqrt(nv * c2) + ADAM_EPS) + ADAM_WD * w_ref[...])
        nm_ref[...] = nm
        nv_ref[...] = nv

    spec = pl.BlockSpec((br, cols), lambda i: (i, 0))
    shape = jax.ShapeDtypeStruct((rows, cols), F32)
    return pl.pallas_call(
        body, name=name, grid=(pl.cdiv(rows, br),), in_specs=[spec] * 4, out_specs=[spec] * 3, out_shape=[shape] * 3,
        compiler_params=_params(("parallel",), VMEM_BIG),
    )(w, g, m, v)


def _adamw_t(wt, g, mt, vt, name, br=1024):
    n, r = wt.shape
    c1 = 1.0 / (1.0 - ADAM_B1 ** ADAM_STEP)
    c2 = 1.0 / (1.0 - ADAM_B2 ** ADAM_STEP)

    def body(w_ref, g_ref, m_ref, v_ref, d_ref, nm_ref, nv_ref):
        gv = g_ref[...].T
        nm = ADAM_B1 * m_ref[...] + (1.0 - ADAM_B1) * gv
        nv = ADAM_B2 * v_ref[...] + (1.0 - ADAM_B2) * (gv * gv)
        d_ref[...] = -ADAM_LR * ((nm * c1) / (jnp.sqrt(nv * c2) + ADAM_EPS) + ADAM_WD * w_ref[...])
        nm_ref[...] = nm
        nv_ref[...] = nv

    spec = pl.BlockSpec((br, r), lambda i: (i, 0))
    shape = jax.ShapeDtypeStruct((n, r), F32)
    return pl.pallas_call(
        body, name=name, grid=(pl.cdiv(n, br),),
        in_specs=[spec, pl.BlockSpec((r, br), lambda i: (0, i)), spec, spec], out_specs=[spec] * 3, out_shape=[shape] * 3,
        compiler_params=_params(("parallel",), VMEM_BIG),
    )(wt, g, mt, vt)


def _adamw_parts(w, parts, m, v, name):
    rows, cols = w.shape
    br = _tile(rows, 32, 16)
    c1 = 1.0 / (1.0 - ADAM_B1 ** ADAM_STEP)
    c2 = 1.0 / (1.0 - ADAM_B2 ** ADAM_STEP)

    def body(w_ref, p_ref, m_ref, v_ref, g_ref, d_ref, nm_ref, nv_ref):
        gv = p_ref[0].astype(F32)
        for j in range(1, N_DEV):
            gv = gv + p_ref[j].astype(F32)
        nm = ADAM_B1 * m_ref[...] + (1.0 - ADAM_B1) * gv
        nv = ADAM_B2 * v_ref[...] + (1.0 - ADAM_B2) * (gv * gv)
        g_ref[...] = gv
        d_ref[...] = -ADAM_LR * ((nm * c1) / (jnp.sqrt(nv * c2) + ADAM_EPS) + ADAM_WD * w_ref[...])
        nm_ref[...] = nm
        nv_ref[...] = nv

    spec = pl.BlockSpec((br, cols), lambda i: (i, 0))
    shape = jax.ShapeDtypeStruct((rows, cols), F32)
    return pl.pallas_call(
        body, name=name, grid=(rows // br,),
        in_specs=[spec, pl.BlockSpec((N_DEV, br, cols), lambda i: (0, i, 0)), spec, spec],
        out_specs=[spec] * 4, out_shape=[shape] * 4,
        compiler_params=_params(("parallel",), VMEM_BIG),
    )(w, parts, m, v)


SMALL_NAMES = ("norm_gain", "mem_norm_gain", "b_forget", "q_gain_a", "k_gain_a", "sinks_a",
               "q_gain_b", "k_gain_b", "q_gain_c", "k_gain_c")
BIG_NAMES = ("w_in", "w_mem_kv", "w_branch_a", "w_branch_b", "w_branch_c", "w_out")
WEIGHT_ORDER = ("norm_gain", "mem_norm_gain", "w_in", "b_forget", "q_gain_a", "k_gain_a", "sinks_a", "q_gain_b",
                "k_gain_b", "q_gain_c", "k_gain_c", "w_mem_kv", "w_branch_a", "w_branch_b", "w_branch_c", "w_out")


def _pack_small(tree):
    flat = jnp.concatenate([tree[n].reshape(1, -1) for n in SMALL_NAMES], axis=1)
    pad = (-flat.shape[1]) % LANES
    return jnp.pad(flat, ((0, 0), (0, pad)))


def _unpack_small(flat, like):
    out, off = {}, 0
    for n in SMALL_NAMES:
        size = like[n].size
        out[n] = flat[:, off:off + size].reshape(like[n].shape)
        off += size
    return out


def kernel(x, mem, norm_gain, mem_norm_gain, w_in, b_forget, q_gain_a, k_gain_a, sinks_a, q_gain_b, k_gain_b, q_gain_c, k_gain_c, w_mem_kv, w_branch_a, w_branch_b, w_branch_c, w_out, loss_target, m_norm_gain, m_mem_norm_gain, m_w_in, m_b_forget, m_q_gain_a, m_k_gain_a, m_sinks_a, m_q_gain_b, m_k_gain_b, m_q_gain_c, m_k_gain_c, m_w_mem_kv, m_w_branch_a, m_w_branch_b, m_w_branch_c, m_w_out, v_norm_gain, v_mem_norm_gain, v_w_in, v_b_forget, v_q_gain_a, v_k_gain_a, v_sinks_a, v_q_gain_b, v_k_gain_b, v_q_gain_c, v_k_gain_c, v_w_mem_kv, v_w_branch_a, v_w_branch_b, v_w_branch_c, v_w_out):
    weights = dict(norm_gain=norm_gain, mem_norm_gain=mem_norm_gain, w_in=w_in, b_forget=b_forget, q_gain_a=q_gain_a,
                   k_gain_a=k_gain_a, sinks_a=sinks_a, q_gain_b=q_gain_b, k_gain_b=k_gain_b, q_gain_c=q_gain_c,
                   k_gain_c=k_gain_c, w_mem_kv=w_mem_kv, w_branch_a=w_branch_a, w_branch_b=w_branch_b,
                   w_branch_c=w_branch_c, w_out=w_out)
    mom_m = dict(norm_gain=m_norm_gain, mem_norm_gain=m_mem_norm_gain, w_in=m_w_in, b_forget=m_b_forget,
                 q_gain_a=m_q_gain_a, k_gain_a=m_k_gain_a, sinks_a=m_sinks_a, q_gain_b=m_q_gain_b, k_gain_b=m_k_gain_b,
                 q_gain_c=m_q_gain_c, k_gain_c=m_k_gain_c, w_mem_kv=m_w_mem_kv, w_branch_a=m_w_branch_a,
                 w_branch_b=m_w_branch_b, w_branch_c=m_w_branch_c, w_out=m_w_out)
    mom_v = dict(norm_gain=v_norm_gain, mem_norm_gain=v_mem_norm_gain, w_in=v_w_in, b_forget=v_b_forget,
                 q_gain_a=v_q_gain_a, k_gain_a=v_k_gain_a, sinks_a=v_sinks_a, q_gain_b=v_q_gain_b, k_gain_b=v_k_gain_b,
                 q_gain_c=v_q_gain_c, k_gain_c=v_k_gain_c, w_mem_kv=v_w_mem_kv, w_branch_a=v_w_branch_a,
                 w_branch_b=v_w_branch_b, w_branch_c=v_w_branch_c, w_out=v_w_out)
    wi = w_in[0]
    sh_qkv = jnp.concatenate([wi[:, a:b] for a, b in SRC_RANGES[0:3]], axis=1).astype(BF16)
    sh_zg = jnp.concatenate([wi[:, a:b] for a, b in SRC_RANGES[3:6]] + [wi[:, SRC_GATE:]], axis=1).astype(BF16)
    sh_wf = jnp.pad(wi[:, FB_SRC:FB_SRC + B_HEADS], ((0, 0), (0, FB_PAD - B_HEADS))).astype(BF16)
    shards = {"zg": sh_zg, "wo": w_out[0].astype(BF16), "wa": w_branch_a[0].astype(BF16),
              "wb": w_branch_b[0].astype(BF16), "wc": w_branch_c[0].astype(BF16)}
    first = ("qkv", "wf", "wk")
    full = _all_gather([sh_qkv, sh_wf, w_mem_kv[0].astype(BF16)], "weights_all_gather")
    wg = {kname: arr.reshape(arr.shape[0] * arr.shape[1], arr.shape[2]) for kname, arr in zip(first, full)}

    small = {n: weights[n] for n in SMALL_NAMES}
    loss_local, grad_x, small_g, parts = _local_step(x[0], mem[0], loss_target[0], small, wg, shards)

    grads, delta, new_m, new_v = {}, {}, {}, {}
    for n, kname in (("w_mem_kv", "wk"), ("w_out", "wo"), ("w_branch_a", "wa"), ("w_branch_b", "wb"), ("w_branch_c", "wc")):
        gsum, dlt, nm, nv = _adamw_parts(weights[n][0], parts[kname], mom_m[n][0], mom_v[n][0], "adamw_" + n)
        grads[n], delta[n], new_m[n], new_v[n] = gsum, dlt[None], nm[None], nv[None]
    g1, g2, gz, gf, gg = (_sum_parts(parts[k], "grad_sum_" + k) for k in ("wm_q1", "wm_q2", "wm_z", "wf", "wm_g"))
    half = Q_SPLIT
    g_in = jnp.concatenate([g1, g2[:, 0:COL_QB - half], gz[:, COL_ZA:COL_ZB], g2[:, COL_QB - half:COL_QC - half],
                            gz[:, COL_ZB:COL_ZC], gf[:, 0:B_HEADS], g2[:, COL_QC - half:W_QKV - half], gz[:, COL_ZC:W_Z], gg], axis=1)
    dlt, nm, nv = _adamw_t(w_in[0].T, g_in, m_w_in[0].T, v_w_in[0].T, "adamw_w_in")
    grads["w_in"], delta["w_in"], new_m["w_in"], new_v["w_in"] = g_in, dlt.T[None], nm.T[None], nv.T[None]

    packed = _pack_small(small_g)
    packed = jnp.concatenate([packed[:, :-1], loss_local.reshape(1, 1)], axis=1)
    reduced = _all_reduce_small(jnp.broadcast_to(packed, (8, packed.shape[1])), "small_all_reduce")
    grads.update(_unpack_small(reduced, small))
    loss = reduced[0, -1]

    pw, pm, pv = _pack_small(small), _pack_small({n: mom_m[n] for n in SMALL_NAMES}), _pack_small({n: mom_v[n] for n in SMALL_NAMES})
    rep8 = lambda a: jnp.broadcast_to(a, (8, a.shape[1]))
    dlt, nm, nv = _adamw(rep8(pw), rep8(reduced), rep8(pm), rep8(pv), "adamw_small")
    for tree, flat in ((delta, dlt), (new_m, nm), (new_v, nv)):
        tree.update(_unpack_small(flat[0:1], small))
    for n in BIG_NAMES:
        grads[n] = grads[n][None]
    return (loss, grad_x[None], *[grads[n] for n in WEIGHT_ORDER], *[delta[n] for n in WEIGHT_ORDER],
            *[new_m[n] for n in WEIGHT_ORDER], *[new_v[n] for n in WEIGHT_ORDER])
```

```python
import math

import jax
import jax.numpy as jnp
import numpy as np
from jax import lax
from jax.experimental import pallas as pl
from jax.experimental.pallas import tpu as pltpu

F32 = jnp.float32
BF16 = jnp.bfloat16

N_DEV = 8
HEAD_DIM = 64
A_Q_HEADS = 12
A_KV_HEADS = 4
A_GROUP = 3
B_HEADS = 12
C_HEADS = 4
C_HEAD_DIM = 128
WINDOW = 128
A_WIDTH = 768
A_KV_WIDTH = 256
B_WIDTH = 768
C_WIDTH = 512
EPS = 1e-6
NEG = -1e30

COL_QA, COL_KA, COL_VA = 0, 768, 1024
COL_QB, COL_KB, COL_VB = 1280, 2048, 2816
COL_QC = 3584
W_QKV = 4096
Q_SPLIT = 1280
COL_ZA, COL_ZB, COL_ZC = 0, 768, 1536
COL_GATE = W_Z = 2048
SRC_RANGES = ((0, 1280), (2048, 4352), (5132, 5644), (1280, 2048), (4352, 5120), (5644, 6156))
SRC_GATE = 6156
FB_SRC = 5120
FB_PAD = 128

ADAM_LR = 0.001
ADAM_B1 = 0.9
ADAM_B2 = 0.999
ADAM_EPS = 1e-08
ADAM_WD = 0.01
ADAM_STEP = 10

VMEM_BIG = 52 * 1024 * 1024
LANES = 128
MESH = pl.DeviceIdType.MESH


def _tile(n, pref, mult=128):
    if n <= pref:
        return n
    t = (pref // mult) * mult
    while t >= mult:
        if n % t == 0:
            return t
        t -= mult
    return n


def _params(sem=None, vmem=None):
    kw = {}
    if sem is not None:
        kw["dimension_semantics"] = sem
    if vmem is not None:
        kw["vmem_limit_bytes"] = vmem
    return pltpu.CompilerParams(**kw)


def _sigmoid(x):
    return 1.0 / (1.0 + jnp.exp(-x))


def _block_diag(hd):
    r = np.arange(LANES)
    return jnp.asarray((r[:, None] // hd) == (r[None, :] // hd), dtype=BF16)


def _seg_sum(t, bd):
    hi = t.astype(BF16)
    lo = (t - hi.astype(F32)).astype(BF16)
    outs = []
    for c in range(t.shape[1] // LANES):
        sl = slice(c * LANES, (c + 1) * LANES)
        outs.append(jnp.dot(hi[:, sl], bd, preferred_element_type=F32) + jnp.dot(lo[:, sl], bd, preferred_element_type=F32))
    return outs[0] if len(outs) == 1 else jnp.concatenate(outs, axis=1)


def _rmsnorm_fwd(x, gain, name):
    rows, d = x.shape
    bm = _tile(rows, 512, 8)

    def body(x_ref, g_ref, o_ref):
        xv = x_ref[...]
        ms = jnp.mean(xv * xv, axis=-1, keepdims=True)
        o_ref[...] = (xv * lax.rsqrt(ms + EPS) * g_ref[...]).astype(BF16)

    return pl.pallas_call(
        body, name=name, grid=(rows // bm,),
        in_specs=[pl.BlockSpec((bm, d), lambda i: (i, 0)), pl.BlockSpec((1, d), lambda i: (0, 0))],
        out_specs=pl.BlockSpec((bm, d), lambda i: (i, 0)),
        out_shape=jax.ShapeDtypeStruct((rows, d), BF16),
        compiler_params=_params(("parallel",)),
    )(x, gain)


def _rmsnorm_bwd(x, dhn, gain, dy, name):
    rows, d = x.shape
    bm = _tile(rows, 512, 8)
    with_dx = dy is not None

    def body(*refs):
        if with_dx:
            x_ref, dh_ref, g_ref, dy_ref, gx_ref, dg_ref = refs
        else:
            x_ref, dh_ref, g_ref, dg_ref = refs
        i = pl.program_id(0)
        xv = x_ref[...]
        rstd = lax.rsqrt(jnp.mean(xv * xv, axis=-1, keepdims=True) + EPS)
        xhat = xv * rstd
        dh = dh_ref[...]
        part = jnp.sum((dh * xhat).reshape(bm // 8, 8, d), axis=0)

        @pl.when(i == 0)
        def _():
            dg_ref[...] = part

        @pl.when(i > 0)
        def _():
            dg_ref[...] += part

        if with_dx:
            g = dh * g_ref[...]
            mean = jnp.mean(g * xhat, axis=-1, keepdims=True)
            gx_ref[...] = dy_ref[...] + rstd * (g - xhat * mean)

    row_spec = pl.BlockSpec((bm, d), lambda i: (i, 0))
    in_specs = [row_spec, row_spec, pl.BlockSpec((1, d), lambda i: (0, 0))]
    args = [x, dhn, gain]
    dg_spec = pl.BlockSpec((8, d), lambda i: (0, 0))
    dg_shape = jax.ShapeDtypeStruct((8, d), F32)
    if with_dx:
        in_specs.append(row_spec)
        args.append(dy)
        out_specs = [row_spec, dg_spec]
        out_shape = [jax.ShapeDtypeStruct((rows, d), F32), dg_shape]
    else:
        out_specs = [dg_spec]
        out_shape = [dg_shape]
    outs = pl.pallas_call(
        body, name=name, grid=(rows // bm,), in_specs=in_specs, out_specs=out_specs, out_shape=out_shape,
        compiler_params=_params(("arbitrary",), VMEM_BIG),
    )(*args)
    return outs if with_dx else (None, outs[0])


class _Comm:
    def __init__(self, kind, arrays):
        self.kind = kind
        self.arrays = list(arrays)
        self.n = len(self.arrays)

    def out_shapes(self):
        if self.kind == "gather":
            return [jax.ShapeDtypeStruct((N_DEV,) + a.shape, a.dtype) for a in self.arrays]
        return [jax.ShapeDtypeStruct(a.shape, a.dtype) for a in self.arrays]

    def scratch(self):
        return [pltpu.SemaphoreType.DMA((self.n, N_DEV - 1)), pltpu.SemaphoreType.DMA((self.n, N_DEV - 1)),
                pltpu.SemaphoreType.DMA((self.n,))]

    def _plan(self, ins, outs, sems, with_recvs):
        send_sems, recv_sems, local_sems = sems
        x, y, c = lax.axis_index("x"), lax.axis_index("y"), lax.axis_index("c")
        my = 4 * x + 2 * y + c
        gather = self.kind == "gather"
        local, sends, recvs = [], [], []
        for a in range(self.n):
            local.append(pltpu.make_async_copy(ins[a] if gather else ins[a].at[my], outs[a].at[my], local_sems.at[a]))
            for k in range(1, N_DEV):
                peer = (x ^ ((k >> 2) & 1), y ^ ((k >> 1) & 1), c ^ (k & 1))
                pid = 4 * peer[0] + 2 * peer[1] + peer[2]
                src = ins[a] if gather else ins[a].at[pid]
                sem = dict(send_sem=send_sems.at[a, k - 1], recv_sem=recv_sems.at[a, k - 1], device_id=peer, device_id_type=MESH)
                sends.append(pltpu.make_async_remote_copy(src_ref=src, dst_ref=outs[a].at[my], **sem))
                if with_recvs:
                    recvs.append(pltpu.make_async_remote_copy(src_ref=src, dst_ref=outs[a].at[pid], **sem))
        return local, sends, recvs

    def start(self, ins, outs, sems):
        local, sends, _ = self._plan(ins, outs, sems, False)
        for cp in local + sends:
            cp.start()

    def wait(self, ins, outs, sems):
        local, sends, recvs = self._plan(ins, outs, sems, True)
        for cp in recvs:
            cp.wait_recv()
        for cp in sends:
            cp.wait_send()
        for cp in local:
            cp.wait()


def _grid_edges(grid):
    first = last = None
    for ax, size in enumerate(grid):
        pid = pl.program_id(ax)
        f, l = pid == 0, pid == size - 1
        first = f if first is None else first & f
        last = l if last is None else last & l
    return first, last


def _hosted_call(body, comm, *, name, grid, in_specs, out_specs, out_shape, scratch_shapes, args, sem, vmem=None):
    in_specs, out_specs, out_shape, scratch_shapes = list(in_specs), list(out_specs), list(out_shape), list(scratch_shapes)
    if comm is None:
        res = pl.pallas_call(body, name=name, grid=grid, in_specs=in_specs, out_specs=out_specs, out_shape=out_shape,
                             scratch_shapes=scratch_shapes, compiler_params=_params(sem, vmem))(*args)
        return list(res), []
    n_in, n_out, n_scr, nc = len(in_specs), len(out_shape), len(scratch_shapes), comm.n

    def hosted(*refs):
        ins = refs[0:n_in]
        comm_in = refs[n_in:n_in + nc]
        outs = refs[n_in + nc:n_in + nc + n_out]
        comm_out = refs[n_in + nc + n_out:n_in + 2 * nc + n_out]
        scr = refs[n_in + 2 * nc + n_out:n_in + 2 * nc + n_out + n_scr]
        sems = refs[n_in + 2 * nc + n_out + n_scr:]
        first, last = _grid_edges(grid)

        @pl.when(first)
        def _():
            comm.start(comm_in, comm_out, sems)

        body(*ins, *outs, *scr)

        @pl.when(last)
        def _():
            comm.wait(comm_in, comm_out, sems)

    any_spec = pl.BlockSpec(memory_space=pl.ANY)
    res = pl.pallas_call(
        hosted, name=name, grid=grid, in_specs=in_specs + [any_spec] * nc, out_specs=out_specs + [any_spec] * nc,
        out_shape=out_shape + comm.out_shapes(), scratch_shapes=scratch_shapes + comm.scratch(),
        compiler_params=_params(("arbitrary",) * len(grid), vmem),
    )(*args, *comm.arrays)
    return list(res[0:n_out]), list(res[n_out:])


def _mm(a, b, *, grid, a_spec, b_spec, o_spec, o_shape, o_dtype, contract, name, add=None, add_spec=None, acc_shape=None,
        comm=None):
    nk = grid[2]
    has_add = add is not None

    def body(*refs):
        a_ref, b_ref = refs[0], refs[1]
        add_ref = refs[2] if has_add else None
        o_ref = refs[3] if has_add else refs[2]
        part = lax.dot_general(a_ref[...], b_ref[...], (contract, ((), ())), preferred_element_type=F32)
        if nk == 1:
            if has_add:
                part = part + add_ref[...]
            o_ref[...] = part.astype(o_dtype)
        else:
            acc = refs[-1]
            k = pl.program_id(2)

            @pl.when(k == 0)
            def _():
                acc[...] = part

            @pl.when(k > 0)
            def _():
                acc[...] += part

            @pl.when(k == nk - 1)
            def _():
                r = acc[...]
                if has_add:
                    r = r + add_ref[...]
                o_ref[...] = r.astype(o_dtype)

    in_specs = [a_spec, b_spec] + ([add_spec] if has_add else [])
    args = [a, b] + ([add] if has_add else [])
    scratch = [pltpu.VMEM(acc_shape, F32)] if nk > 1 else []
    outs, comm_outs = _hosted_call(
        body, comm, name=name, grid=grid, in_specs=in_specs, out_specs=[o_spec],
        out_shape=[jax.ShapeDtypeStruct(o_shape, o_dtype)], scratch_shapes=scratch, args=args,
        sem=("parallel", "parallel", "arbitrary"), vmem=VMEM_BIG)
    return outs[0] if comm is None else (outs[0], comm_outs)


def _mm_nn(a, b, *, bm, bn, bk, o_dtype, name, add=None, comm=None):
    m, kd = a.shape
    n = b.shape[1]
    bm, bn, bk = _tile(m, bm, 8), _tile(n, bn), _tile(kd, bk)
    o_spec = pl.BlockSpec((bm, bn), lambda i, j, k: (i, j))
    return _mm(a, b, grid=(m // bm, n // bn, kd // bk),
               a_spec=pl.BlockSpec((bm, bk), lambda i, j, k: (i, k)),
               b_spec=pl.BlockSpec((bk, bn), lambda i, j, k: (k, j)),
               o_spec=o_spec, o_shape=(m, n), o_dtype=o_dtype, contract=((1,), (0,)), name=name,
               add=add, add_spec=o_spec, acc_shape=(bm, bn), comm=comm)


def _mm_nt(a, b, *, bm, bn, bk, o_dtype, name, add=None, b_col0=0, comm=None):
    m, kd = a.shape
    n = b.shape[0]
    bm, bn, bk = _tile(m, bm, 8), _tile(n, bn), _tile(math.gcd(kd, b_col0), bk)
    kb0 = b_col0 // bk
    o_spec = pl.BlockSpec((bm, bn), lambda i, j, k: (i, j))
    return _mm(a, b, grid=(m // bm, n // bn, kd // bk),
               a_spec=pl.BlockSpec((bm, bk), lambda i, j, k: (i, k)),
               b_spec=pl.BlockSpec((bn, bk), lambda i, j, k: (j, kb0 + k)),
               o_spec=o_spec, o_shape=(m, n), o_dtype=o_dtype, contract=((1,), (1,)), name=name,
               add=add, add_spec=o_spec, acc_shape=(bm, bn), comm=comm)


def _mm_nt_sum(terms, *, bm, bn, bk, name, add=None, comm=None):
    m = terms[0][0].shape[0]
    n = terms[0][1].shape[0]
    bm, bn = _tile(m, bm, 8), _tile(n, bn)
    nt = (((1,), (1,)), ((), ()))
    plan, groups, start = [], [], 0
    for a, b, col0 in terms:
        kd = a.shape[1]
        tk = _tile(math.gcd(kd, col0), bk)
        steps = kd // tk
        last = groups[-1] if groups else None
        if last is not None and last[0] is b and last[4] == tk and (last[3] + last[2]) * tk == col0:
            last[2] += steps
        else:
            groups.append([b, start, steps, col0 // tk, tk])
        plan.append((start, steps, len(groups) - 1))
        start += steps
    nk = start
    nterm, ngroup, has_add = len(terms), len(groups), add is not None

    def body(*refs):
        a_refs, b_refs = refs[0:nterm], refs[nterm:nterm + ngroup]
        add_ref = refs[nterm + ngroup] if has_add else None
        o_ref, acc = refs[nterm + ngroup + has_add], refs[nterm + ngroup + has_add + 1]
        k = pl.program_id(2)
        for t, (s0, steps, grp) in enumerate(plan):
            @pl.when((k >= s0) & (k < s0 + steps))
            def _():
                part = lax.dot_general(a_refs[t][...], b_refs[grp][...], nt, preferred_element_type=F32)

                @pl.when(k == 0)
                def _():
                    acc[...] = part

                @pl.when(k > 0)
                def _():
                    acc[...] += part

        @pl.when(k == nk - 1)
        def _():
            o_ref[...] = acc[...] + add_ref[...] if has_add else acc[...]

    def a_spec(tk, s0, steps):
        return pl.BlockSpec((bm, tk), lambda i, j, k: (i, jnp.clip(k - s0, 0, steps - 1)))

    def b_spec(tk, s0, steps, off):
        return pl.BlockSpec((bn, tk), lambda i, j, k: (j, off + jnp.clip(k - s0, 0, steps - 1)))

    o_spec = pl.BlockSpec((bm, bn), lambda i, j, k: (i, j))
    in_specs = [a_spec(groups[grp][4], s0, steps) for s0, steps, grp in plan]
    in_specs += [b_spec(tk, s0, steps, cb0) for _, s0, steps, cb0, tk in groups]
    args = [a for a, _, _ in terms] + [grp[0] for grp in groups]
    if has_add:
        in_specs.append(o_spec)
        args.append(add)
    outs, comm_outs = _hosted_call(
        body, comm, name=name, grid=(m // bm, n // bn, nk), in_specs=in_specs,
        out_specs=[o_spec], out_shape=[jax.ShapeDtypeStruct((m, n), F32)],
        scratch_shapes=[pltpu.VMEM((bm, bn), F32)], args=args,
        sem=("parallel", "parallel", "arbitrary"), vmem=VMEM_BIG)
    return outs[0] if comm is None else (outs[0], comm_outs)


def _mm_tn(a, b, *, bm, bn, bk, o_dtype, name, comm=None):
    kd, m = a.shape
    n = b.shape[1]
    bm, bn, bk = _tile(m, bm), _tile(n, bn), _tile(kd, bk, 8)
    return _mm(a, b, grid=(m // bm, n // bn, kd // bk),
               a_spec=pl.BlockSpec((bk, bm), lambda i, j, k: (k, i)),
               b_spec=pl.BlockSpec((bk, bn), lambda i, j, k: (k, j)),
               o_spec=pl.BlockSpec((bm, bn), lambda i, j, k: (i, j)),
               o_shape=(m, n), o_dtype=o_dtype, contract=((0,), (0,)), name=name, acc_shape=(bm, bn), comm=comm)


def _branch_full(w8):
    kb, ds = w8.shape[0] // N_DEV, w8.shape[1]
    return w8.reshape(N_DEV, kb, ds).transpose(1, 0, 2).reshape(kb, N_DEV * ds)


def _branch_shards(g):
    kb, ds = g.shape[0], g.shape[1] // N_DEV
    return g.reshape(kb, N_DEV, ds).transpose(1, 0, 2).reshape(N_DEV * kb, ds)


def _headnorm_fwd(src, c0, width, bw, hd, gain, nflag, head_major, name):
    rows = src.shape[0]
    bm = _tile(rows, 2048 if bw <= 256 else 1024, 16)
    bd = _block_diag(hd)
    cb0 = c0 // bw

    def body(x_ref, g_ref, f_ref, bd_ref, o_ref):
        xv = x_ref[...].astype(F32)
        ss = _seg_sum(xv * xv, bd_ref[...])
        rstd = lax.rsqrt(ss * (1.0 / hd) + EPS)
        y = (xv * jnp.where(f_ref[...] > 0.0, rstd, 1.0) * g_ref[...]).astype(BF16)
        if head_major:
            for h in range(bw // HEAD_DIM):
                o_ref[h] = y[:, h * HEAD_DIM:(h + 1) * HEAD_DIM]
        else:
            o_ref[...] = y

    vec_spec = pl.BlockSpec((1, bw), lambda i, t: (0, t))
    if head_major:
        hpb = bw // HEAD_DIM
        out_spec = pl.BlockSpec((hpb, bm, HEAD_DIM), lambda i, t: (t, i, 0))
        out_shape = jax.ShapeDtypeStruct((width // HEAD_DIM, rows, HEAD_DIM), BF16)
    else:
        out_spec = pl.BlockSpec((bm, bw), lambda i, t: (i, t))
        out_shape = jax.ShapeDtypeStruct((rows, width), BF16)
    return pl.pallas_call(
        body, name=name, grid=(rows // bm, width // bw),
        in_specs=[pl.BlockSpec((bm, bw), lambda i, t: (i, cb0 + t)), vec_spec, vec_spec,
                  pl.BlockSpec((LANES, LANES), lambda i, t: (0, 0))],
        out_specs=out_spec, out_shape=out_shape,
        compiler_params=_params(("parallel", "parallel")),
    )(src, gain, nflag, bd)


def _headnorm_bwd(src, c0, width, bw, hd, gain, nflag, dyn, target, t0, name):
    rows = src.shape[0]
    bm = _tile(rows, 2048 if bw <= 256 else 1024, 16)
    bd = _block_diag(hd)
    cb0 = c0 // bw
    tb0 = t0 // bw
    aliased = target is not None

    def body(*refs):
        if aliased:
            x_ref, dy_ref, g_ref, f_ref, bd_ref, _, o_ref, dg_ref = refs
        else:
            x_ref, dy_ref, g_ref, f_ref, bd_ref, o_ref, dg_ref = refs
        i = pl.program_id(1)
        xv = x_ref[...].astype(F32)
        dyv = dy_ref[...]
        bdv = bd_ref[...]
        rstd = lax.rsqrt(_seg_sum(xv * xv, bdv) * (1.0 / hd) + EPS)
        xhat = xv * rstd
        g = dyv * g_ref[...]
        mean = _seg_sum(g * xhat, bdv) * (1.0 / hd)
        dx = jnp.where(f_ref[...] > 0.0, rstd * (g - xhat * mean), g)
        o_ref[...] = dx.astype(BF16)
        part = jnp.sum((dyv * xhat).reshape(bm // 8, 8, bw), axis=0)

        @pl.when(i == 0)
        def _():
            dg_ref[...] = part

        @pl.when(i > 0)
        def _():
            dg_ref[...] += part

    vec_spec = pl.BlockSpec((1, bw), lambda t, i: (0, t))
    in_specs = [pl.BlockSpec((bm, bw), lambda t, i: (i, cb0 + t)), pl.BlockSpec((bm, bw), lambda t, i: (i, t)),
                vec_spec, vec_spec, pl.BlockSpec((LANES, LANES), lambda t, i: (0, 0))]
    args = [src, dyn, gain, nflag, bd]
    aliases = {}
    if aliased:
        in_specs.append(pl.BlockSpec(memory_space=pl.ANY))
        args.append(target)
        aliases = {5: 0}
        o_shape = jax.ShapeDtypeStruct(target.shape, BF16)
    else:
        o_shape = jax.ShapeDtypeStruct((rows, width), BF16)
    out, dg = pl.pallas_call(
        body, name=name, grid=(width // bw, rows // bm), in_specs=in_specs,
        out_specs=[pl.BlockSpec((bm, bw), lambda t, i: (i, tb0 + t)), pl.BlockSpec((8, bw), lambda t, i: (0, t))],
        out_shape=[o_shape, jax.ShapeDtypeStruct((8, width), F32)],
        input_output_aliases=aliases,
        compiler_params=_params(("parallel", "arbitrary")),
    )(*args)
    return out, dg


def _fox_prep(pfb, bpad, name):
    s = pfb.shape[0]

    def body(p_ref, b_ref, c_ref):
        z = p_ref[...] + b_ref[...]
        logf = jnp.minimum(z, 0.0) - jnp.log(1.0 + jnp.exp(-jnp.abs(z)))
        x = logf.T[0:16, :]
        lane = lax.broadcasted_iota(jnp.int32, (16, s), 1)
        sh = 1
        while sh < s:
            x = x + jnp.where(lane >= sh, pltpu.roll(x, sh, 1), 0.0)
            sh *= 2
        c_ref[...] = x

    return pl.pallas_call(
        body, name=name, grid=(1,),
        in_specs=[pl.BlockSpec((s, FB_PAD), lambda i: (0, 0)), pl.BlockSpec((1, FB_PAD), lambda i: (0, 0))],
        out_specs=pl.BlockSpec((16, s), lambda i: (0, 0)),
        out_shape=jax.ShapeDtypeStruct((16, s), F32),
        compiler_params=_params(("arbitrary",)),
    )(pfb, bpad)


def _fox_prep_bwd(pfb, bpad, dct, name):
    s = pfb.shape[0]

    def body(p_ref, b_ref, dc_ref, df_ref, db_ref):
        zt = (p_ref[...] + b_ref[...]).T[0:16, :]
        y = dc_ref[...]
        lane = lax.broadcasted_iota(jnp.int32, (16, s), 1)
        sh = 1
        while sh < s:
            y = y + jnp.where(lane < s - sh, pltpu.roll(y, s - sh, 1), 0.0)
            sh *= 2
        dz = y * _sigmoid(-zt)
        db_ref[...] = jnp.broadcast_to(jnp.sum(dz, axis=1, keepdims=True), (16, FB_PAD))
        full = jnp.concatenate([dz, jnp.zeros((FB_PAD - 16, s), F32)], axis=0)
        df_ref[...] = full.T.astype(BF16)

    return pl.pallas_call(
        body, name=name, grid=(1,),
        in_specs=[pl.BlockSpec((s, FB_PAD), lambda i: (0, 0)), pl.BlockSpec((1, FB_PAD), lambda i: (0, 0)),
                  pl.BlockSpec((16, s), lambda i: (0, 0))],
        out_specs=[pl.BlockSpec((s, FB_PAD), lambda i: (0, 0)), pl.BlockSpec((16, FB_PAD), lambda i: (0, 0))],
        out_shape=[jax.ShapeDtypeStruct((s, FB_PAD), BF16), jax.ShapeDtypeStruct((16, FB_PAD), F32)],
        compiler_params=_params(("arbitrary",)),
    )(pfb, bpad, dct)


def _swa_window(n):
    ws = pl.multiple_of(jnp.maximum(n * WINDOW - WINDOW, 0), WINDOW)
    qi = lax.broadcasted_iota(jnp.int32, (WINDOW, 2 * WINDOW), 0)
    kj = lax.broadcasted_iota(jnp.int32, (WINDOW, 2 * WINDOW), 1)
    rel = qi + (n * WINDOW - ws) - kj
    valid = (rel >= 0) & (rel < WINDOW)
    return ws, valid, rel.astype(F32)


def _attn_a_fwd(qkv, sinks, slopes, name):
    s = qkv.shape[1]
    nb = s // WINDOW
    smem = pl.BlockSpec(memory_space=pltpu.SMEM)

    def body(sink_ref, slope_ref, q_ref, k_ref, v_ref, o_ref, lse_ref):
        n = pl.program_id(0)
        ws, valid, relf = _swa_window(n)
        outs = []
        for h in range(A_Q_HEADS):
            kvh = h // A_GROUP
            kw = k_ref[kvh, pl.ds(ws, 2 * WINDOW), :]
            vw = v_ref[kvh, pl.ds(ws, 2 * WINDOW), :]
            sc = lax.dot_general(q_ref[h], kw, (((1,), (1,)), ((), ())), preferred_element_type=F32)
            sc = jnp.where(valid, sc - slope_ref[h] * relf, NEG)
            sink = sink_ref[h]
            m = jnp.maximum(jnp.max(sc, axis=1, keepdims=True), sink)
            p = jnp.exp(sc - m)
            denom = jnp.sum(p, axis=1, keepdims=True) + jnp.exp(sink - m)
            pn = (p / denom).astype(BF16)
            outs.append(jnp.dot(pn, vw, preferred_element_type=F32))
            lse_ref[h] = jnp.broadcast_to(m + jnp.log(denom), (WINDOW, HEAD_DIM))
        o_ref[...] = jnp.concatenate(outs, axis=1)

    return pl.pallas_call(
        body, name=name, grid=(nb,),
        in_specs=[smem, smem,
                  pl.BlockSpec((A_Q_HEADS, WINDOW, HEAD_DIM), lambda n: (0, n, 0)),
                  pl.BlockSpec((A_KV_HEADS, s, HEAD_DIM), lambda n: (A_GROUP, 0, 0)),
                  pl.BlockSpec((A_KV_HEADS, s, HEAD_DIM), lambda n: (A_GROUP + 1, 0, 0))],
        out_specs=[pl.BlockSpec((WINDOW, A_WIDTH), lambda n: (n, 0)),
                   pl.BlockSpec((A_Q_HEADS, WINDOW, HEAD_DIM), lambda n: (0, n, 0))],
        out_shape=[jax.ShapeDtypeStruct((s, A_WIDTH), F32), jax.ShapeDtypeStruct((A_Q_HEADS, s, HEAD_DIM), F32)],
        compiler_params=_params(("parallel",), VMEM_BIG),
    )(sinks, slopes, qkv, qkv, qkv)


def _attn_a_bwd(qkv, do, lse, dd, sinks, slopes, name, comm=None):
    s = qkv.shape[1]
    nb = s // WINDOW
    smem = pl.BlockSpec(memory_space=pltpu.SMEM)
    last = nb - 1

    def body(sink_ref, slope_ref, q_ref, k_ref, v_ref, do_ref, lse_ref, dd_ref, dq_ref, dkv_ref, ds_ref, carry):
        n = pl.program_id(0)

        @pl.when(n == 0)
        def _():
            carry[...] = jnp.zeros(carry.shape, F32)
            ds_ref[...] = jnp.zeros(ds_ref.shape, F32)

        @pl.when(n < nb)
        def _():
            ws, valid, relf = _swa_window(n)
            dqs = []
            dkw = [None] * A_KV_HEADS
            dvw = [None] * A_KV_HEADS
            for h in range(A_Q_HEADS):
                kvh = h // A_GROUP
                qh = q_ref[h]
                doh = do_ref[h]
                kw = k_ref[kvh, pl.ds(ws, 2 * WINDOW), :]
                vw = v_ref[kvh, pl.ds(ws, 2 * WINDOW), :]
                lse_h = lse_ref[h]
                dd_h = dd_ref[h]
                sc = lax.dot_general(qh, kw, (((1,), (1,)), ((), ())), preferred_element_type=F32)
                sc = jnp.where(valid, sc - slope_ref[h] * relf, NEG)
                p = jnp.exp(sc - lse_h[:, 0:1])
                dp = lax.dot_general(doh, vw, (((1,), (1,)), ((), ())), preferred_element_type=F32)
                dsc = (p * (dp - dd_h[:, 0:1])).astype(BF16)
                pb = p.astype(BF16)
                dqs.append(jnp.dot(dsc, kw, preferred_element_type=F32))
                dk_h = jnp.dot(qh.T, dsc, preferred_element_type=F32)
                dv_h = jnp.dot(doh.T, pb, preferred_element_type=F32)
                dkw[kvh] = dk_h if dkw[kvh] is None else dkw[kvh] + dk_h
                dvw[kvh] = dv_h if dvw[kvh] is None else dvw[kvh] + dv_h
                psink = jnp.exp(sink_ref[h] - lse_h)
                ds_ref[h] += jnp.sum((-psink * dd_h).reshape(WINDOW // 8, 8, HEAD_DIM), axis=0)
            dq_ref[...] = jnp.concatenate(dqs, axis=1)
            win = jnp.concatenate(dkw + dvw, axis=0)
            first = win[:, 0:WINDOW]
            second = win[:, WINDOW:2 * WINDOW]
            dkv_ref[...] = (carry[...] + first).T
            carry[...] = jnp.where(n == 0, first, second)

        @pl.when(n == nb)
        def _():
            dkv_ref[...] = carry[...].T

    hm = lambda heads: pl.BlockSpec((heads, WINDOW, HEAD_DIM), lambda n: (0, jnp.minimum(n, last), 0))
    res = lambda blk: pl.BlockSpec((A_KV_HEADS, s, HEAD_DIM), lambda n: (blk, 0, 0))
    outs, comm_outs = _hosted_call(
        body, comm, name=name, grid=(nb + 1,),
        in_specs=[smem, smem, hm(A_Q_HEADS), res(A_GROUP), res(A_GROUP + 1), hm(A_Q_HEADS), hm(A_Q_HEADS), hm(A_Q_HEADS)],
        out_specs=[pl.BlockSpec((WINDOW, A_WIDTH), lambda n: (jnp.minimum(n, last), 0)),
                   pl.BlockSpec((WINDOW, 2 * A_KV_WIDTH), lambda n: (jnp.maximum(n - 1, 0), 0)),
                   pl.BlockSpec((A_Q_HEADS, 8, HEAD_DIM), lambda n: (0, 0, 0))],
        out_shape=[jax.ShapeDtypeStruct((s, A_WIDTH), F32), jax.ShapeDtypeStruct((s, 2 * A_KV_WIDTH), F32),
                   jax.ShapeDtypeStruct((A_Q_HEADS, 8, HEAD_DIM), F32)],
        scratch_shapes=[pltpu.VMEM((2 * A_KV_WIDTH, WINDOW), F32)],
        args=[sinks, slopes, qkv, qkv, qkv, do, lse, dd], sem=("arbitrary",), vmem=VMEM_BIG)
    return outs[0], outs[1], outs[2], comm_outs


def _attn_b_fwd(qkv, c3, name, comm=None):
    heads, s = qkv.shape[0] // 3, qkv.shape[1]
    hpairs = heads // 2
    bq = min(512, s)
    nq = s // bq
    nt = (((1,), (1,)), ((), ()))

    def body(q_ref, k_ref, v_ref, c_ref, o_ref, lse_ref, m_scr, l_scr, acc_scr):
        i = pl.program_id(1)
        r0 = pl.multiple_of(i * bq, bq)
        row = lax.broadcasted_iota(jnp.int32, (bq, bq), 0)
        col = lax.broadcasted_iota(jnp.int32, (bq, bq), 1)
        m_scr[...] = jnp.full((2, bq, LANES), NEG, F32)
        l_scr[...] = jnp.zeros((2, bq, LANES), F32)
        acc_scr[...] = jnp.zeros((2, bq, HEAD_DIM), F32)

        def step(j, masked):
            k0 = pl.multiple_of(j * bq, bq)
            for h2 in range(2):
                kv = k_ref[h2, pl.ds(k0, bq), :]
                vv = v_ref[h2, pl.ds(k0, bq), :]
                cq0 = c_ref[h2, :, pl.ds(r0, LANES)][:, 0:1]
                sc = lax.dot_general(q_ref[h2], kv, nt, preferred_element_type=F32)
                sc = sc + (cq0 - c_ref[h2, :, pl.ds(k0, bq)])
                if masked:
                    sc = jnp.where(col <= row, sc, NEG)
                m_prev = m_scr[h2]
                m_new = jnp.maximum(m_prev, jnp.max(sc, axis=1, keepdims=True))
                alpha = jnp.exp(m_prev - m_new)
                p = jnp.exp(sc - m_new[:, 0:1])
                l_scr[h2] = alpha * l_scr[h2] + jnp.sum(p, axis=1, keepdims=True)
                p_hi = p.astype(BF16)
                p_lo = (p - p_hi.astype(F32)).astype(BF16)
                pv = jnp.dot(p_hi, vv, preferred_element_type=F32) + jnp.dot(p_lo, vv, preferred_element_type=F32)
                acc_scr[h2] = acc_scr[h2] * alpha[:, 0:HEAD_DIM] + pv
                m_scr[h2] = m_new

        def loop_body(j, carry):
            step(j, False)
            return carry

        lax.fori_loop(0, i, loop_body, 0)
        step(i, True)
        outs = []
        for h2 in range(2):
            l = l_scr[h2]
            outs.append(acc_scr[h2] / l[:, 0:HEAD_DIM])
            lse_ref[h2] = (m_scr[h2] + jnp.log(l))[:, 0:HEAD_DIM]
        o_ref[...] = jnp.concatenate(outs, axis=1)

    res = lambda off: pl.BlockSpec((2, s, HEAD_DIM), lambda hp, i: (off + hp, 0, 0))
    outs, comm_outs = _hosted_call(
        body, comm, name=name, grid=(hpairs, nq),
        in_specs=[pl.BlockSpec((2, bq, HEAD_DIM), lambda hp, i: (hp, i, 0)), res(hpairs), res(2 * hpairs),
                  pl.BlockSpec((2, 1, s), lambda hp, i: (hp, 0, 0))],
        out_specs=[pl.BlockSpec((bq, 2 * HEAD_DIM), lambda hp, i: (i, hp)),
                   pl.BlockSpec((2, bq, HEAD_DIM), lambda hp, i: (hp, i, 0))],
        out_shape=[jax.ShapeDtypeStruct((s, heads * HEAD_DIM), F32), jax.ShapeDtypeStruct((heads, s, HEAD_DIM), F32)],
        scratch_shapes=[pltpu.VMEM((2, bq, LANES), F32), pltpu.VMEM((2, bq, LANES), F32), pltpu.VMEM((2, bq, HEAD_DIM), F32)],
        args=[qkv, qkv, qkv, c3], sem=("parallel", "parallel"), vmem=VMEM_BIG)
    return outs[0], outs[1], comm_outs


def _attn_b_bwd(qkv, do, lse, dd, c3, name, comm=None):
    heads, s = qkv.shape[0] // 3, qkv.shape[1]
    hpairs = heads // 2
    bq = min(512, s)
    nq = s // bq
    nt = (((1,), (1,)), ((), ()))
    tn = (((0,), (0,)), ((), ()))
    grid = (heads // 2, nq)

    def body(q_ref, k_ref, v_ref, do_ref, lse_ref, dd_ref, c_ref, dq_ref, dk_ref, dv_ref, dc_ref,
             dq_scr, dk_scr, dv_scr, dc_scr):
        j = pl.program_id(1)
        k0 = pl.multiple_of(j * bq, bq)
        row = lax.broadcasted_iota(jnp.int32, (bq, bq), 0)
        col = lax.broadcasted_iota(jnp.int32, (bq, bq), 1)

        @pl.when(j == 0)
        def _():
            dq_scr[...] = jnp.zeros(dq_scr.shape, F32)

        dk_scr[...] = jnp.zeros((2, HEAD_DIM, bq), F32)
        dv_scr[...] = jnp.zeros((2, HEAD_DIM, bq), F32)
        dc_scr[...] = jnp.zeros((2, 1, bq), F32)
        k_t = [k_ref[h2].T for h2 in range(2)]

        def step(i, masked):
            r0 = pl.multiple_of(i * bq, bq)
            for h2 in range(2):
                kv = k_ref[h2]
                vv = v_ref[h2]
                qv = q_ref[h2, pl.ds(r0, bq), :]
                dov = do_ref[h2, pl.ds(r0, bq), :]
                lse_v = lse_ref[h2, pl.ds(r0, bq), :][:, 0:1]
                dd_v = dd_ref[h2, pl.ds(r0, bq), :][:, 0:1]
                cq0 = c_ref[h2, :, pl.ds(r0, LANES)][:, 0:1]
                sc = lax.dot_general(qv, kv, nt, preferred_element_type=F32) + (cq0 - c_ref[h2, :, pl.ds(k0, bq)])
                if masked:
                    sc = jnp.where(col <= row, sc, NEG)
                p = jnp.exp(sc - lse_v)
                dp = lax.dot_general(dov, vv, nt, preferred_element_type=F32)
                dsc = p * (dp - dd_v)
                dsb = dsc.astype(BF16)
                dv_scr[h2] += jnp.dot(dov.T, p.astype(BF16), preferred_element_type=F32)
                dk_scr[h2] += jnp.dot(qv.T, dsb, preferred_element_type=F32)
                dq_scr[h2, :, pl.ds(r0, bq)] += jnp.dot(k_t[h2], dsb.T, preferred_element_type=F32)
                dc_scr[h2] -= jnp.sum(dsc, axis=0, keepdims=True)

        def loop_body(i, carry):
            step(i, False)
            return carry

        step(j, True)
        lax.fori_loop(j + 1, nq, loop_body, 0)
        dc_ref[...] = dc_scr[...]
        dk_ref[...] = jnp.concatenate([dk_scr[0].T, dk_scr[1].T], axis=1)
        dv_ref[...] = jnp.concatenate([dv_scr[0].T, dv_scr[1].T], axis=1)

        @pl.when(j == nq - 1)
        def _():
            dq_ref[...] = jnp.concatenate([dq_scr[0].T, dq_scr[1].T], axis=1)

    res = pl.BlockSpec((2, s, HEAD_DIM), lambda hp, j: (hp, 0, 0))
    blk = lambda off: pl.BlockSpec((2, bq, HEAD_DIM), lambda hp, j: (off + hp, j, 0))
    tm = jax.ShapeDtypeStruct((s, heads * HEAD_DIM), F32)
    in_specs = [res, blk(hpairs), blk(2 * hpairs), res, res, res, pl.BlockSpec((2, 1, s), lambda hp, j: (hp, 0, 0))]
    out_specs = [pl.BlockSpec((s, 2 * HEAD_DIM), lambda hp, j: (0, hp)),
                 pl.BlockSpec((bq, 2 * HEAD_DIM), lambda hp, j: (j, hp)),
                 pl.BlockSpec((bq, 2 * HEAD_DIM), lambda hp, j: (j, hp)),
                 pl.BlockSpec((2, 1, bq), lambda hp, j: (hp, 0, j))]
    out_shape = [tm, tm, tm, jax.ShapeDtypeStruct((heads, 1, s), F32)]
    scratch = [pltpu.VMEM((2, HEAD_DIM, s), F32), pltpu.VMEM((2, HEAD_DIM, bq), F32),
               pltpu.VMEM((2, HEAD_DIM, bq), F32), pltpu.VMEM((2, 1, bq), F32)]
    outs, comm_outs = _hosted_call(
        body, comm, name=name, grid=grid, in_specs=in_specs, out_specs=out_specs, out_shape=out_shape,
        scratch_shapes=scratch, args=[qkv, qkv, qkv, do, lse, dd, c3], sem=("parallel", "arbitrary"), vmem=VMEM_BIG)
    return outs[0], outs[1], outs[2], outs[3], comm_outs


def _attn_c_probs(qh, mkh):
    sc = lax.dot_general(qh, mkh, (((1,), (1,)), ((), ())), preferred_element_type=F32) * (C_HEAD_DIM ** -0.5)
    p = jnp.exp(sc - jnp.max(sc, axis=1, keepdims=True))
    return p / jnp.sum(p, axis=1, keepdims=True)


def _attn_c_fwd(q, mkv, name):
    s = q.shape[0]
    m = mkv.shape[0]
    bq = _tile(s, 512, 8)

    def body(q_ref, mk_ref, mv_ref, o_ref):
        outs = []
        for h in range(C_HEADS):
            sl = slice(h * C_HEAD_DIM, (h + 1) * C_HEAD_DIM)
            pn = _attn_c_probs(q_ref[:, sl], mk_ref[:, sl]).astype(BF16)
            outs.append(jnp.dot(pn, mv_ref[:, sl], preferred_element_type=F32))
        o_ref[...] = jnp.concatenate(outs, axis=1)

    return pl.pallas_call(
        body, name=name, grid=(s // bq,),
        in_specs=[pl.BlockSpec((bq, C_WIDTH), lambda i: (i, 0)), pl.BlockSpec((m, C_WIDTH), lambda i: (0, 0)),
                  pl.BlockSpec((m, C_WIDTH), lambda i: (0, 1))],
        out_specs=pl.BlockSpec((bq, C_WIDTH), lambda i: (i, 0)),
        out_shape=jax.ShapeDtypeStruct((s, C_WIDTH), F32),
        compiler_params=_params(("parallel",)),
    )(q, mkv, mkv)


def _attn_c_bwd(q, mkv, do, name):
    s = q.shape[0]
    m = mkv.shape[0]
    bq = _tile(s, 512, 8)
    tn = (((0,), (0,)), ((), ()))

    def body(q_ref, mk_ref, mv_ref, do_ref, dq_ref, dm_ref):
        i = pl.program_id(0)

        @pl.when(i == 0)
        def _():
            dm_ref[...] = jnp.zeros(dm_ref.shape, F32)

        dqs = []
        for h in range(C_HEADS):
            sl = slice(h * C_HEAD_DIM, (h + 1) * C_HEAD_DIM)
            qh, mkh, mvh, doh = q_ref[:, sl], mk_ref[:, sl], mv_ref[:, sl], do_ref[:, sl]
            pn = _attn_c_probs(qh, mkh)
            dp = lax.dot_general(doh, mvh, (((1,), (1,)), ((), ())), preferred_element_type=F32)
            dsc = (pn * (dp - jnp.sum(pn * dp, axis=1, keepdims=True)) * (C_HEAD_DIM ** -0.5)).astype(BF16)
            dqs.append(jnp.dot(dsc, mkh, preferred_element_type=F32))
            dm_ref[:, sl] += lax.dot_general(dsc, qh, tn, preferred_element_type=F32)
            sv = slice(C_WIDTH + h * C_HEAD_DIM, C_WIDTH + (h + 1) * C_HEAD_DIM)
            dm_ref[:, sv] += lax.dot_general(pn.astype(BF16), doh, tn, preferred_element_type=F32)
        dq_ref[...] = jnp.concatenate(dqs, axis=1)

    row = pl.BlockSpec((bq, C_WIDTH), lambda i: (i, 0))
    return pl.pallas_call(
        body, name=name, grid=(s // bq,),
        in_specs=[row, pl.BlockSpec((m, C_WIDTH), lambda i: (0, 0)), pl.BlockSpec((m, C_WIDTH), lambda i: (0, 1)), row],
        out_specs=[row, pl.BlockSpec((m, 2 * C_WIDTH), lambda i: (0, 0))],
        out_shape=[jax.ShapeDtypeStruct((s, C_WIDTH), F32), jax.ShapeDtypeStruct((m, 2 * C_WIDTH), F32)],
        compiler_params=_params(("arbitrary",)),
    )(q, mkv, mkv, do)


def _gate_fwd(y, proj, zc0, bw, name):
    rows, width = y.shape
    bm = _tile(rows, 2048 if bw <= 256 else 1024, 16)
    cb0 = zc0 // bw

    def body(y_ref, z_ref, o_ref):
        z = z_ref[...].astype(F32)
        o_ref[...] = (y_ref[...] * (z * _sigmoid(z))).astype(BF16)

    return pl.pallas_call(
        body, name=name, grid=(rows // bm, width // bw),
        in_specs=[pl.BlockSpec((bm, bw), lambda i, t: (i, t)), pl.BlockSpec((bm, bw), lambda i, t: (i, cb0 + t))],
        out_specs=pl.BlockSpec((bm, bw), lambda i, t: (i, t)),
        out_shape=jax.ShapeDtypeStruct((rows, width), BF16),
        compiler_params=_params(("parallel", "parallel")),
    )(y, proj)


def _gate_bwd(dsv, y, proj, zc0, bw, dproj, t0, head_major, name):
    rows, width = y.shape
    bm = _tile(rows, 2048 if bw <= 256 else 1024, 16)
    cb0 = zc0 // bw
    tb0 = t0 // bw
    bd = _block_diag(HEAD_DIM)
    hpb = bw // HEAD_DIM

    def body(*refs):
        if head_major:
            ds_ref, y_ref, z_ref, bd_ref, _, dp_ref, dy_ref, dd_ref = refs
        else:
            ds_ref, y_ref, z_ref, _, dp_ref, dy_ref = refs
        z = z_ref[...].astype(F32)
        sig = _sigmoid(z)
        dsx = ds_ref[...]
        yv = y_ref[...]
        dy = dsx * (z * sig)
        dp_ref[...] = (dsx * yv * (sig * (1.0 + z * (1.0 - sig)))).astype(BF16)
        if head_major:
            dyb = dy.astype(BF16)
            dd = _seg_sum(dyb.astype(F32) * yv, bd_ref[...])
            for h in range(hpb):
                sl = slice(h * HEAD_DIM, (h + 1) * HEAD_DIM)
                dy_ref[h] = dyb[:, sl]
                dd_ref[h] = dd[:, sl]
        else:
            dy_ref[...] = dy.astype(BF16)

    tile = pl.BlockSpec((bm, bw), lambda i, t: (i, t))
    ztile = pl.BlockSpec((bm, bw), lambda i, t: (i, cb0 + t))
    ttile = pl.BlockSpec((bm, bw), lambda i, t: (i, tb0 + t))
    any_spec = pl.BlockSpec(memory_space=pl.ANY)
    dp_shape = jax.ShapeDtypeStruct(dproj.shape, BF16)
    if head_major:
        hm_spec = pl.BlockSpec((hpb, bm, HEAD_DIM), lambda i, t: (t, i, 0))
        nh = width // HEAD_DIM
        outs = pl.pallas_call(
            body, name=name, grid=(rows // bm, width // bw),
            in_specs=[tile, tile, ztile, pl.BlockSpec((LANES, LANES), lambda i, t: (0, 0)), any_spec],
            out_specs=[ttile, hm_spec, hm_spec],
            out_shape=[dp_shape, jax.ShapeDtypeStruct((nh, rows, HEAD_DIM), BF16),
                       jax.ShapeDtypeStruct((nh, rows, HEAD_DIM), F32)],
            input_output_aliases={4: 0},
            compiler_params=_params(("parallel", "parallel")),
        )(dsv, y, proj, bd, dproj)
        return outs[0], outs[1], outs[2]
    outs = pl.pallas_call(
        body, name=name, grid=(rows // bm, width // bw),
        in_specs=[tile, tile, ztile, any_spec],
        out_specs=[ttile, tile],
        out_shape=[dp_shape, jax.ShapeDtypeStruct((rows, width), BF16)],
        input_output_aliases={3: 0},
        compiler_params=_params(("parallel", "parallel")),
    )(dsv, y, proj, dproj)
    return outs[0], outs[1], None


def _merge_fwd(proj, ua, ub, uc, name):
    rows, d = ua.shape
    bm = _tile(rows, 1024, 16)
    bw = _tile(d, 512)
    g0 = COL_GATE // bw
    gstep = d // bw

    def body(la_ref, lb_ref, lc_ref, ua_ref, ub_ref, uc_ref, o_ref, ga_ref, gb_ref, gc_ref):
        y = None
        for l_ref, u_ref, g_ref in ((la_ref, ua_ref, ga_ref), (lb_ref, ub_ref, gb_ref), (lc_ref, uc_ref, gc_ref)):
            g = _sigmoid(l_ref[...].astype(F32))
            g_ref[...] = g.astype(BF16)
            term = g * u_ref[...].astype(F32)
            y = term if y is None else y + term
        o_ref[...] = y.astype(BF16)

    tile = pl.BlockSpec((bm, bw), lambda i, t: (i, t))
    gate = lambda b: pl.BlockSpec((bm, bw), lambda i, t: (i, g0 + b * gstep + t))
    shape = jax.ShapeDtypeStruct((rows, d), BF16)
    return pl.pallas_call(
        body, name=name, grid=(rows // bm, d // bw),
        in_specs=[gate(0), gate(1), gate(2), tile, tile, tile],
        out_specs=[tile] * 4, out_shape=[shape] * 4,
        compiler_params=_params(("parallel", "parallel")),
    )(proj, proj, proj, ua, ub, uc)


def _merge_bwd(dym, us, gs, name):
    rows, d = dym.shape
    bm = _tile(rows, 256, 16)

    def body(dy_ref, ua_ref, ub_ref, uc_ref, ga_ref, gb_ref, gc_ref, dg_ref, da_ref, db_ref, dc_ref):
        dyv = dy_ref[...]
        for b, (u_ref, g_ref, du_ref) in enumerate(((ua_ref, ga_ref, da_ref), (ub_ref, gb_ref, db_ref), (uc_ref, gc_ref, dc_ref))):
            g = g_ref[...].astype(F32)
            du_ref[...] = (g * dyv).astype(BF16)
            dg_ref[:, b * d:(b + 1) * d] = (dyv * u_ref[...].astype(F32) * g * (1.0 - g)).astype(BF16)

    tile = pl.BlockSpec((bm, d), lambda i: (i, 0))
    shape = jax.ShapeDtypeStruct((rows, d), BF16)
    outs = pl.pallas_call(
        body, name=name, grid=(rows // bm,),
        in_specs=[tile] * 7,
        out_specs=[pl.BlockSpec((bm, 3 * d), lambda i: (i, 0)), tile, tile, tile],
        out_shape=[jax.ShapeDtypeStruct((rows, 3 * d), BF16), shape, shape, shape],
        compiler_params=_params(("parallel",), VMEM_BIG),
    )(dym, *us, *gs)
    return outs[0], outs[1], outs[2], outs[3]


def _out_proj_loss(ym, wo, x, target, name):
    m, d = x.shape
    bm, bn = _tile(m, 1024, 16), _tile(d, 1024)
    grid = (m // bm, d // bn)

    def body(a_ref, b_ref, x_ref, t_ref, dy_ref, dyb_ref, l_ref):
        first, _ = _grid_edges(grid)
        y = jnp.dot(a_ref[...], b_ref[...], preferred_element_type=F32) + x_ref[...]
        diff = y - t_ref[...]
        dy = diff * (1.0 / d)
        dy_ref[...] = dy
        dyb_ref[...] = dy.astype(BF16)
        sq = diff * diff
        part = sq[:, 0:LANES]
        for c in range(1, bn // LANES):
            part = part + sq[:, c * LANES:(c + 1) * LANES]
        part = jnp.sum(part.reshape(bm // 8, 8, LANES), axis=0)

        @pl.when(first)
        def _():
            l_ref[...] = part

        @pl.when(jnp.logical_not(first))
        def _():
            l_ref[...] += part

    tile = pl.BlockSpec((bm, bn), lambda i, j: (i, j))
    return pl.pallas_call(
        body, name=name, grid=grid,
        in_specs=[pl.BlockSpec((bm, d), lambda i, j: (i, 0)), pl.BlockSpec((d, bn), lambda i, j: (0, j)), tile, tile],
        out_specs=[tile, tile, pl.BlockSpec((8, LANES), lambda i, j: (0, 0))],
        out_shape=[jax.ShapeDtypeStruct((m, d), F32), jax.ShapeDtypeStruct((m, d), BF16),
                   jax.ShapeDtypeStruct((8, LANES), F32)],
        compiler_params=_params(("arbitrary", "arbitrary"), VMEM_BIG),
    )(ym, wo, x, target)


def _row(vec, reps=1):
    return jnp.tile(vec.reshape(1, -1).astype(F32), (1, reps))


def _local_step(x, mem, target, small, wg, shards=None):
    s, d = x.shape
    dist = shards is not None
    wg = dict(wg)
    ones = lambda n: jnp.ones((1, n), F32)
    zeros = lambda n: jnp.zeros((1, n), F32)
    scale_ab = HEAD_DIM ** -0.5
    split8 = lambda g: g.reshape(N_DEV, g.shape[0] // N_DEV, g.shape[1])
    flat8 = lambda g: g.reshape(g.shape[0] * g.shape[1], g.shape[2])
    gather = lambda names: _Comm("gather", [shards[n] for n in names]) if dist else None
    g = {}

    def scatter(names):
        return _Comm("scatter", [split8(g[n]) for n in names]) if dist else None

    def hosted(result, names, store):
        if not dist:
            return result
        out, got = result
        store.update(zip(names, got))
        return out

    hn = _rmsnorm_fwd(x, small["norm_gain"], "rms_x_fwd")
    got = {}
    proj = hosted(_mm_nn(hn, wg["qkv"], bm=1024, bn=1024, bk=d, o_dtype=BF16, name="proj_qkv",
                         comm=gather(("wa", "wb"))), ("wa", "wb"), got)
    wg.update({n: flat8(a) for n, a in got.items()})
    pfb = _mm_nn(hn, wg["wf"], bm=1024, bn=FB_PAD, bk=d, o_dtype=F32, name="proj_fb")
    mn = _rmsnorm_fwd(mem, small["mem_norm_gain"], "rms_mem_fwd")
    mkv = _mm_nn(mn, wg["wk"], bm=256, bn=1024, bk=d, o_dtype=F32, name="mem_kv")

    gain_a = jnp.concatenate([_row(small["q_gain_a"], A_Q_HEADS) * scale_ab, _row(small["k_gain_a"], A_KV_HEADS), ones(A_KV_WIDTH)], axis=1)
    flag_a = jnp.concatenate([ones(A_WIDTH + A_KV_WIDTH), zeros(A_KV_WIDTH)], axis=1)
    qkv_a = _headnorm_fwd(proj, COL_QA, 1280, 1280, HEAD_DIM, gain_a, flag_a, True, "hn_a_fwd")
    gain_b = jnp.concatenate([_row(small["q_gain_b"], B_HEADS) * scale_ab, _row(small["k_gain_b"], B_HEADS), ones(B_WIDTH)], axis=1)
    flag_b = jnp.concatenate([ones(2 * B_WIDTH), zeros(B_WIDTH)], axis=1)
    qkv_b = _headnorm_fwd(proj, COL_QB, 2304, 256, HEAD_DIM, gain_b, flag_b, True, "hn_b_fwd")
    gain_cq = _row(small["q_gain_c"], C_HEADS)
    q_c = _headnorm_fwd(proj, COL_QC, C_WIDTH, C_WIDTH, C_HEAD_DIM, gain_cq, ones(C_WIDTH), False, "hn_cq_fwd")
    gain_ck = jnp.concatenate([_row(small["k_gain_c"], C_HEADS), ones(C_WIDTH)], axis=1)
    flag_ck = jnp.concatenate([ones(C_WIDTH), zeros(C_WIDTH)], axis=1)
    mkvn = _headnorm_fwd(mkv, 0, 2 * C_WIDTH, 2 * C_WIDTH, C_HEAD_DIM, gain_ck, flag_ck, False, "hn_ck_fwd")


    bpad = jnp.pad(small["b_forget"].reshape(1, -1), ((0, 0), (0, FB_PAD - B_HEADS)))
    c16 = _fox_prep(pfb, bpad, "fox_prep")
    c3 = c16[0:B_HEADS].reshape(B_HEADS, 1, s)

    sinks = small["sinks_a"].reshape(-1)
    slopes = jnp.exp2(-8.0 * jnp.arange(1, A_Q_HEADS + 1, dtype=F32) / A_Q_HEADS)
    y_a, lse_a = _attn_a_fwd(qkv_a, sinks, slopes, "attn_a_fwd")
    y_b, lse_b, got_zg = _attn_b_fwd(qkv_b, c3, "attn_b_fwd", comm=gather(("zg",)))
    if dist:
        wg["zg"] = flat8(got_zg[0])
    y_c = _attn_c_fwd(q_c, mkvn, "attn_c_fwd")

    got = {}
    pzg = hosted(_mm_nn(hn, wg["zg"], bm=1024, bn=1024, bk=d, o_dtype=BF16, name="proj_zg", comm=gather(("wo", "wc"))),
                 ("wo", "wc"), got)
    wg.update({n: flat8(a) for n, a in got.items()})

    s_a = _gate_fwd(y_a, pzg, COL_ZA, 256, "gate_a_fwd")
    s_b = _gate_fwd(y_b, pzg, COL_ZB, 256, "gate_b_fwd")
    s_c = _gate_fwd(y_c, pzg, COL_ZC, 512, "gate_c_fwd")
    w_a, w_b, w_c = _branch_full(wg["wa"]), _branch_full(wg["wb"]), _branch_full(wg["wc"])
    u_a = _mm_nn(s_a, w_a, bm=1024, bn=2048, bk=A_WIDTH, o_dtype=BF16, name="branch_a_fwd")
    u_b = _mm_nn(s_b, w_b, bm=1024, bn=2048, bk=B_WIDTH, o_dtype=BF16, name="branch_b_fwd")
    u_c = _mm_nn(s_c, w_c, bm=1024, bn=2048, bk=C_WIDTH, o_dtype=BF16, name="branch_c_fwd")
    ym, gate_a, gate_b, gate_c = _merge_fwd(pzg, u_a, u_b, u_c, "merge_fwd")
    dy, dyb, lpart = _out_proj_loss(ym, wg["wo"], x, target, "out_proj_loss")
    loss = 0.5 / d * jnp.sum(lpart)

    dym = _mm_nt(dyb, wg["wo"], bm=1024, bn=1024, bk=d, o_dtype=F32, name="out_proj_bwd_act")
    g["wo"] = _mm_tn(ym, dyb, bm=512, bn=1024, bk=s, o_dtype=BF16, name="out_proj_bwd_w")

    dgate, du_a, du_b, du_c = _merge_bwd(dym, (u_a, u_b, u_c), (gate_a, gate_b, gate_c), "merge_bwd")
    parts = {}
    g["wm_g"] = hosted(_mm_tn(hn, dgate, bm=512, bn=1024, bk=s, o_dtype=BF16, name="proj_gate_bwd_w",
                              comm=scatter(("wo",))), ("wo",), parts)

    ds_a = _mm_nt(du_a, w_a, bm=1024, bn=A_WIDTH, bk=d, o_dtype=F32, name="branch_a_bwd_act")
    ds_b = _mm_nt(du_b, w_b, bm=1024, bn=B_WIDTH, bk=d, o_dtype=F32, name="branch_b_bwd_act")
    ds_c = _mm_nt(du_c, w_c, bm=1024, bn=C_WIDTH, bk=d, o_dtype=F32, name="branch_c_bwd_act")
    g["wa"] = _branch_shards(_mm_tn(s_a, du_a, bm=A_WIDTH, bn=1024, bk=s, o_dtype=BF16, name="branch_a_bwd_w"))
    g["wb"] = _branch_shards(_mm_tn(s_b, du_b, bm=B_WIDTH, bn=1024, bk=s, o_dtype=BF16, name="branch_b_bwd_w"))
    g["wc"] = _branch_shards(_mm_tn(s_c, du_c, bm=C_WIDTH, bn=1024, bk=s, o_dtype=BF16, name="branch_c_bwd_w"))

    dz = lax.empty((s, W_Z), BF16)
    dz, do_a, dd_a = _gate_bwd(ds_a, y_a, pzg, COL_ZA, 256, dz, COL_ZA, True, "gate_a_bwd")
    dz, do_b, dd_b = _gate_bwd(ds_b, y_b, pzg, COL_ZB, 256, dz, COL_ZB, True, "gate_b_bwd")
    dz, do_c, _ = _gate_bwd(ds_c, y_c, pzg, COL_ZC, 512, dz, COL_ZC, False, "gate_c_bwd")
    g["wm_z"] = _mm_tn(hn, dz, bm=512, bn=1024, bk=s, o_dtype=BF16, name="proj_z_bwd_w")

    names = ("wa", "wb", "wc")
    dq_a, dkv_a, dsink, got = _attn_a_bwd(qkv_a, do_a, lse_a, dd_a, sinks, slopes, "attn_a_bwd", comm=scatter(names))
    parts.update(zip(names, got))
    names = ("wm_g", "wm_z")
    dq_b, dk_b, dv_b, dc3, got = _attn_b_bwd(qkv_b, do_b, lse_b, dd_b, c3, "attn_b_bwd", comm=scatter(names))
    parts.update(zip(names, got))
    dq_c, dmkvn = _attn_c_bwd(q_c, mkvn, do_c, "attn_c_bwd")

    dqkv = lax.empty((s, W_QKV), BF16)
    dqkv, dg_qa = _headnorm_bwd(proj, COL_QA, A_WIDTH, 256, HEAD_DIM, gain_a[:, 0:768], flag_a[:, 0:768], dq_a, dqkv, COL_QA, "hn_qa_bwd")
    dqkv, dg_kva = _headnorm_bwd(proj, COL_KA, 512, 256, HEAD_DIM, gain_a[:, 768:1280], flag_a[:, 768:1280], dkv_a, dqkv, COL_KA, "hn_kva_bwd")
    dqkv, dg_qb = _headnorm_bwd(proj, COL_QB, B_WIDTH, 256, HEAD_DIM, gain_b[:, 0:768], flag_b[:, 0:768], dq_b, dqkv, COL_QB, "hn_qb_bwd")
    dqkv, dg_kb = _headnorm_bwd(proj, COL_KB, B_WIDTH, 256, HEAD_DIM, gain_b[:, 768:1536], flag_b[:, 768:1536], dk_b, dqkv, COL_KB, "hn_kb_bwd")
    dqkv, _ = _headnorm_bwd(proj, COL_VB, B_WIDTH, 256, HEAD_DIM, gain_b[:, 1536:2304], flag_b[:, 1536:2304], dv_b, dqkv, COL_VB, "hn_vb_bwd")
    dqkv, dg_qc = _headnorm_bwd(proj, COL_QC, C_WIDTH, 512, C_HEAD_DIM, gain_cq, ones(C_WIDTH), dq_c, dqkv, COL_QC, "hn_qc_bwd")
    dmkv, dg_kc = _headnorm_bwd(mkv, 0, 2 * C_WIDTH, 2 * C_WIDTH, C_HEAD_DIM, gain_ck, flag_ck, dmkvn, None, 0, "hn_kc_bwd")

    dct = jnp.pad(dc3.reshape(B_HEADS, s), ((0, 16 - B_HEADS), (0, 0)))
    dfb, dbf = _fox_prep_bwd(pfb, bpad, dct, "fox_prep_bwd")

    dmn = _mm_nt(dmkv, wg["wk"], bm=256, bn=1024, bk=1024, o_dtype=F32, name="mem_kv_bwd_act")
    g["wk"] = _mm_tn(mn, dmkv, bm=512, bn=1024, bk=mem.shape[0], o_dtype=BF16, name="mem_kv_bwd_w")
    _, dg_mem = _rmsnorm_bwd(mem, dmn, small["mem_norm_gain"], None, "rms_mem_bwd")

    g["wm_qkv"] = _mm_tn(hn, dqkv, bm=512, bn=1024, bk=s, o_dtype=BF16, name="proj_qkv_bwd_w")
    g["wf"] = _mm_tn(hn, dfb, bm=512, bn=FB_PAD, bk=s, o_dtype=BF16, name="proj_fb_bwd_w")
    half = Q_SPLIT
    g["wm_q1"], g["wm_q2"] = g["wm_qkv"][:, 0:half], g["wm_qkv"][:, half:W_QKV]
    names = ("wm_q1",)
    dhn = hosted(_mm_nt(dqkv, wg["qkv"], bm=1024, bn=1024, bk=2048, o_dtype=F32, name="proj_qkv_bwd_act",
                        comm=scatter(names)), names, parts)
    names = ("wm_q2", "wf", "wk")
    terms = [(dz, wg["zg"], COL_ZA), (dgate, wg["zg"], COL_GATE), (dfb, wg["wf"], 0)]
    dhn = hosted(_mm_nt_sum(terms, bm=1024, bn=1024, bk=2048, name="proj_zg_bwd_act", add=dhn, comm=scatter(names)),
                 names, parts)
    if dist:
        g = parts
    grad_x, dg_x = _rmsnorm_bwd(x, dhn, small["norm_gain"], dy, "rms_x_bwd")

    fold = lambda part, heads, hd: jnp.sum(jnp.sum(part, axis=0).reshape(heads, hd), axis=0).reshape(1, hd)
    small_grads = {
        "norm_gain": jnp.sum(dg_x, axis=0).reshape(1, d),
        "mem_norm_gain": jnp.sum(dg_mem, axis=0).reshape(1, d),
        "b_forget": dbf[0:B_HEADS, 0].reshape(1, B_HEADS),
        "q_gain_a": fold(dg_qa, A_Q_HEADS, HEAD_DIM) * scale_ab,
        "k_gain_a": fold(dg_kva[:, 0:A_KV_WIDTH], A_KV_HEADS, HEAD_DIM),
        "sinks_a": (jnp.sum(dsink, axis=(1, 2)) * (1.0 / HEAD_DIM)).reshape(1, A_Q_HEADS),
        "q_gain_b": fold(dg_qb, B_HEADS, HEAD_DIM) * scale_ab,
        "k_gain_b": fold(dg_kb, B_HEADS, HEAD_DIM),
        "q_gain_c": fold(dg_qc, C_HEADS, C_HEAD_DIM),
        "k_gain_c": fold(dg_kc[:, 0:C_WIDTH], C_HEADS, C_HEAD_DIM),
    }
    return loss, grad_x, small_grads, g


def _coords():
    return lax.axis_index("x"), lax.axis_index("y"), lax.axis_index("c")


def _all_gather(shards, name):
    n = len(shards)

    def body(*refs):
        ins = refs[0:n]
        outs = refs[n:2 * n]
        send_sems, recv_sems, local_sems = refs[2 * n:2 * n + 3]
        x, y, c = _coords()
        me, sibling = (x, y, c), (x, y, 1 - c)
        chips = [(1 - x, y), (x, 1 - y), (1 - x, 1 - y)]
        idx = lambda p: 4 * p[0] + 2 * p[1] + p[2]

        def copy(a, k, block, to, src=None):
            slot = outs[a].at[idx(block)]
            return pltpu.make_async_remote_copy(
                src_ref=slot if src is None else src, dst_ref=slot,
                send_sem=send_sems.at[a, k], recv_sem=recv_sems.at[a, k], device_id=to, device_id_type=MESH)

        mine = [pltpu.make_async_copy(ins[a], outs[a].at[idx(me)], local_sems.at[a]) for a in range(n)]
        for cp in mine:
            cp.start()
        first = []
        for a in range(n):
            first.append(copy(a, 0, me, sibling, src=ins[a]))
            first += [copy(a, 1 + j, me, (*chip, c), src=ins[a]) for j, chip in enumerate(chips)]
        for cp in first:
            cp.start()
        passed = []
        for j, chip in enumerate(chips):
            for a in range(n):
                copy(a, 1 + j, (*chip, c), me).wait_recv()
                fwd = copy(a, 4 + j, (*chip, c), sibling)
                fwd.start()
                passed.append(fwd)
        for a in range(n):
            copy(a, 0, sibling, me).wait_recv()
            for j, chip in enumerate(chips):
                copy(a, 4 + j, (*chip, 1 - c), me).wait_recv()
        for cp in first + passed:
            cp.wait_send()
        for cp in mine:
            cp.wait()

    any_spec = pl.BlockSpec(memory_space=pl.ANY)
    return pl.pallas_call(
        body, name=name,
        in_specs=[any_spec] * n, out_specs=[any_spec] * n,
        out_shape=[jax.ShapeDtypeStruct((N_DEV,) + sh.shape, sh.dtype) for sh in shards],
        scratch_shapes=[pltpu.SemaphoreType.DMA((n, 7)), pltpu.SemaphoreType.DMA((n, 7)), pltpu.SemaphoreType.DMA((n,))],
    )(*shards)


def _all_reduce_small(vec, name):
    p = vec.shape[1]

    def body(v_ref, o_ref, gather, send_sems, recv_sems):
        x, y, c = _coords()
        my = 4 * x + 2 * y + c
        peers = [(x ^ ((k >> 2) & 1), y ^ ((k >> 1) & 1), c ^ (k & 1)) for k in range(1, N_DEV)]
        gather[my] = v_ref[...]
        sends = [pltpu.make_async_remote_copy(
            src_ref=v_ref, dst_ref=gather.at[my], send_sem=send_sems.at[k], recv_sem=recv_sems.at[k],
            device_id=peer, device_id_type=MESH) for k, peer in enumerate(peers)]
        for cp in sends:
            cp.start()
        for k, peer in enumerate(peers):
            pid = 4 * peer[0] + 2 * peer[1] + peer[2]
            pltpu.make_async_remote_copy(
                src_ref=v_ref, dst_ref=gather.at[pid], send_sem=send_sems.at[k], recv_sem=recv_sems.at[k],
                device_id=peer, device_id_type=MESH).wait_recv()
        for cp in sends:
            cp.wait_send()
        total = gather[0]
        for j in range(1, N_DEV):
            total = total + gather[j]
        o_ref[...] = total

    vm = pl.BlockSpec(memory_space=pltpu.VMEM)
    return pl.pallas_call(
        body, name=name, in_specs=[vm], out_specs=vm,
        out_shape=jax.ShapeDtypeStruct((8, p), F32),
        scratch_shapes=[pltpu.VMEM((N_DEV, 8, p), F32), pltpu.SemaphoreType.DMA((7,)), pltpu.SemaphoreType.DMA((7,))],
    )(vec)[0:1]


def _sum_parts(parts, name):
    _, rows, cols = parts.shape
    br = _tile(rows, 64, 16)

    def body(p_ref, o_ref):
        total = p_ref[0].astype(F32)
        for j in range(1, N_DEV):
            total = total + p_ref[j].astype(F32)
        o_ref[...] = total

    return pl.pallas_call(
        body, name=name, grid=(rows // br,),
        in_specs=[pl.BlockSpec((N_DEV, br, cols), lambda i: (0, i, 0))],
        out_specs=pl.BlockSpec((br, cols), lambda i: (i, 0)),
        out_shape=jax.ShapeDtypeStruct((rows, cols), F32),
        compiler_params=_params(("parallel",), VMEM_BIG),
    )(parts)


def _adamw(w, g, m, v, name, br=32):
    rows, cols = w.shape
    br = min(br, rows)
    c1 = 1.0 / (1.0 - ADAM_B1 ** ADAM_STEP)
    c2 = 1.0 / (1.0 - ADAM_B2 ** ADAM_STEP)

    def body(w_ref, g_ref, m_ref, v_ref, d_ref, nm_ref, nv_ref):
        gv = g_ref[...]
        nm = ADAM_B1 * m_ref[...] + (1.0 - ADAM_B1) * gv
        nv = ADAM_B2 * v_ref[...] + (1.0 - ADAM_B2) * (gv * gv)
        d_ref[...] = -ADAM_LR * ((nm * c1) / (jnp.sqrt(nv * c2) + ADAM_EPS) + ADAM_WD * w_ref[...])
        nm_ref[...] = nm
        nv_ref[...] = nv

    spec = pl.BlockSpec((br, cols), lambda i: (i, 0))
    shape = jax.ShapeDtypeStruct((rows, cols), F32)
    return pl.pallas_call(
        body, name=name, grid=(pl.cdiv(rows, br),), in_specs=[spec] * 4, out_specs=[spec] * 3, out_shape=[shape] * 3,
        compiler_params=_params(("parallel",), VMEM_BIG),
    )(w, g, m, v)


def _adamw_t(wt, g, mt, vt, name, br=1024):
    n, r = wt.shape
    c1 = 1.0 / (1.0 - ADAM_B1 ** ADAM_STEP)
    c2 = 1.0 / (1.0 - ADAM_B2 ** ADAM_STEP)

    def body(w_ref, g_ref, m_ref, v_ref, d_ref, nm_ref, nv_ref):
        gv = g_ref[...].T
        nm = ADAM_B1 * m_ref[...] + (1.0 - ADAM_B1) * gv
        nv = ADAM_B2 * v_ref[...] + (1.0 - ADAM_B2) * (gv * gv)
        d_ref[...] = -ADAM_LR * ((nm * c1) / (jnp.sqrt(nv * c2) + ADAM_EPS) + ADAM_WD * w_ref[...])
        nm_ref[...] = nm
        nv_ref[...] = nv

    spec = pl.BlockSpec((br, r), lambda i: (i, 0))
    shape = jax.ShapeDtypeStruct((n, r), F32)
    return pl.pallas_call(
        body, name=name, grid=(pl.cdiv(n, br),),
        in_specs=[spec, pl.BlockSpec((r, br), lambda i: (0, i)), spec, spec], out_specs=[spec] * 3, out_shape=[shape] * 3,
        compiler_params=_params(("parallel",), VMEM_BIG),
    )(wt, g, mt, vt)


def _adamw_parts(w, parts, m, v, name):
    rows, cols = w.shape
    br = _tile(rows, 32, 16)
    c1 = 1.0 / (1.0 - ADAM_B1 ** ADAM_STEP)
    c2 = 1.0 / (1.0 - ADAM_B2 ** ADAM_STEP)

    def body(w_ref, p_ref, m_ref, v_ref, g_ref, d_ref, nm_ref, nv_ref):
        gv = p_ref[0].astype(F32)
        for j in range(1, N_DEV):
            gv = gv + p_ref[j].astype(F32)
        nm = ADAM_B1 * m_ref[...] + (1.0 - ADAM_B1) * gv
        nv = ADAM_B2 * v_ref[...] + (1.0 - ADAM_B2) * (gv * gv)
        g_ref[...] = gv
        d_ref[...] = -ADAM_LR * ((nm * c1) / (jnp.sqrt(nv * c2) + ADAM_EPS) + ADAM_WD * w_ref[...])
        nm_ref[...] = nm
        nv_ref[...] = nv

    spec = pl.BlockSpec((br, cols), lambda i: (i, 0))
    shape = jax.ShapeDtypeStruct((rows, cols), F32)
    return pl.pallas_call(
        body, name=name, grid=(rows // br,),
        in_specs=[spec, pl.BlockSpec((N_DEV, br, cols), lambda i: (0, i, 0)), spec, spec],
        out_specs=[spec] * 4, out_shape=[shape] * 4,
        compiler_params=_params(("parallel",), VMEM_BIG),
    )(w, parts, m, v)


SMALL_NAMES = ("norm_gain", "mem_norm_gain", "b_forget", "q_gain_a", "k_gain_a", "sinks_a",
               "q_gain_b", "k_gain_b", "q_gain_c", "k_gain_c")
BIG_NAMES = ("w_in", "w_mem_kv", "w_branch_a", "w_branch_b", "w_branch_c", "w_out")
WEIGHT_ORDER = ("norm_gain", "mem_norm_gain", "w_in", "b_forget", "q_gain_a", "k_gain_a", "sinks_a", "q_gain_b",
                "k_gain_b", "q_gain_c", "k_gain_c", "w_mem_kv", "w_branch_a", "w_branch_b", "w_branch_c", "w_out")


def _pack_small(tree):
    flat = jnp.concatenate([tree[n].reshape(1, -1) for n in SMALL_NAMES], axis=1)
    pad = (-flat.shape[1]) % LANES
    return jnp.pad(flat, ((0, 0), (0, pad)))


def _unpack_small(flat, like):
    out, off = {}, 0
    for n in SMALL_NAMES:
        size = like[n].size
        out[n] = flat[:, off:off + size].reshape(like[n].shape)
        off += size
    return out


def kernel(x, mem, norm_gain, mem_norm_gain, w_in, b_forget, q_gain_a, k_gain_a, sinks_a, q_gain_b, k_gain_b, q_gain_c, k_gain_c, w_mem_kv, w_branch_a, w_branch_b, w_branch_c, w_out, loss_target, m_norm_gain, m_mem_norm_gain, m_w_in, m_b_forget, m_q_gain_a, m_k_gain_a, m_sinks_a, m_q_gain_b, m_k_gain_b, m_q_gain_c, m_k_gain_c, m_w_mem_kv, m_w_branch_a, m_w_branch_b, m_w_branch_c, m_w_out, v_norm_gain, v_mem_norm_gain, v_w_in, v_b_forget, v_q_gain_a, v_k_gain_a, v_sinks_a, v_q_gain_b, v_k_gain_b, v_q_gain_c, v_k_gain_c, v_w_mem_kv, v_w_branch_a, v_w_branch_b, v_w_branch_c, v_w_out):
    weights = dict(norm_gain=norm_gain, mem_norm_gain=mem_norm_gain, w_in=w_in, b_forget=b_forget, q_gain_a=q_gain_a,
                   k_gain_a=k_gain_a, sinks_a=sinks_a, q_gain_b=q_gain_b, k_gain_b=k_gain_b, q_gain_c=q_gain_c,
                   k_gain_c=k_gain_c, w_mem_kv=w_mem_kv, w_branch_a=w_branch_a, w_branch_b=w_branch_b,
                   w_branch_c=w_branch_c, w_out=w_out)
    mom_m = dict(norm_gain=m_norm_gain, mem_norm_gain=m_mem_norm_gain, w_in=m_w_in, b_forget=m_b_forget,
                 q_gain_a=m_q_gain_a, k_gain_a=m_k_gain_a, sinks_a=m_sinks_a, q_gain_b=m_q_gain_b, k_gain_b=m_k_gain_b,
                 q_gain_c=m_q_gain_c, k_gain_c=m_k_gain_c, w_mem_kv=m_w_mem_kv, w_branch_a=m_w_branch_a,
                 w_branch_b=m_w_branch_b, w_branch_c=m_w_branch_c, w_out=m_w_out)
    mom_v = dict(norm_gain=v_norm_gain, mem_norm_gain=v_mem_norm_gain, w_in=v_w_in, b_forget=v_b_forget,
                 q_gain_a=v_q_gain_a, k_gain_a=v_k_gain_a, sinks_a=v_sinks_a, q_gain_b=v_q_gain_b, k_gain_b=v_k_gain_b,
                 q_gain_c=v_q_gain_c, k_gain_c=v_k_gain_c, w_mem_kv=v_w_mem_kv, w_branch_a=v_w_branch_a,
                 w_branch_b=v_w_branch_b, w_branch_c=v_w_branch_c, w_out=v_w_out)
    wi = w_in[0]
    sh_qkv = jnp.concatenate([wi[:, a:b] for a, b in SRC_RANGES[0:3]], axis=1).astype(BF16)
    sh_zg = jnp.concatenate([wi[:, a:b] for a, b in SRC_RANGES[3:6]] + [wi[:, SRC_GATE:]], axis=1).astype(BF16)
    sh_wf = jnp.pad(wi[:, FB_SRC:FB_SRC + B_HEADS], ((0, 0), (0, FB_PAD - B_HEADS))).astype(BF16)
    shards = {"zg": sh_zg, "wo": w_out[0].astype(BF16), "wa": w_branch_a[0].astype(BF16),
              "wb": w_branch_b[0].astype(BF16), "wc": w_branch_c[0].astype(BF16)}
    first = ("qkv", "wf", "wk")
    full = _all_gather([sh_qkv, sh_wf, w_mem_kv[0].astype(BF16)], "weights_all_gather")
    wg = {kname: arr.reshape(arr.shape[0] * arr.shape[1], arr.shape[2]) for kname, arr in zip(first, full)}

    small = {n: weights[n] for n in SMALL_NAMES}
    loss_local, grad_x, small_g, parts = _local_step(x[0], mem[0], loss_target[0], small, wg, shards)

    grads, delta, new_m, new_v = {}, {}, {}, {}
    for n, kname in (("w_mem_kv", "wk"), ("w_out", "wo"), ("w_branch_a", "wa"), ("w_branch_b", "wb"), ("w_branch_c", "wc")):
        gsum, dlt, nm, nv = _adamw_parts(weights[n][0], parts[kname], mom_m[n][0], mom_v[n][0], "adamw_" + n)
        grads[n], delta[n], new_m[n], new_v[n] = gsum, dlt[None], nm[None], nv[None]
    g1, g2, gz, gf, gg = (_sum_parts(parts[k], "grad_sum_" + k) for k in ("wm_q1", "wm_q2", "wm_z", "wf", "wm_g"))
    half = Q_SPLIT
    g_in = jnp.concatenate([g1, g2[:, 0:COL_QB - half], gz[:, COL_ZA:COL_ZB], g2[:, COL_QB - half:COL_QC - half],
                            gz[:, COL_ZB:COL_ZC], gf[:, 0:B_HEADS], g2[:, COL_QC - half:W_QKV - half], gz[:, COL_ZC:W_Z], gg], axis=1)
    dlt, nm, nv = _adamw_t(w_in[0].T, g_in, m_w_in[0].T, v_w_in[0].T, "adamw_w_in")
    grads["w_in"], delta["w_in"], new_m["w_in"], new_v["w_in"] = g_in, dlt.T[None], nm.T[None], nv.T[None]

    packed = _pack_small(small_g)
    packed = jnp.concatenate([packed[:, :-1], loss_local.reshape(1, 1)], axis=1)
    reduced = _all_reduce_small(jnp.broadcast_to(packed, (8, packed.shape[1])), "small_all_reduce")
    grads.update(_unpack_small(reduced, small))
    loss = reduced[0, -1]

    pw, pm, pv = _pack_small(small), _pack_small({n: mom_m[n] for n in SMALL_NAMES}), _pack_small({n: mom_v[n] for n in SMALL_NAMES})
    rep8 = lambda a: jnp.broadcast_to(a, (8, a.shape[1]))
    dlt, nm, nv = _adamw(rep8(pw), rep8(reduced), rep8(pm), rep8(pv), "adamw_small")
    for tree, flat in ((delta, dlt), (new_m, nm), (new_v, nv)):
        tree.update(_unpack_small(flat[0:1], small))
    for n in BIG_NAMES:
        grads[n] = grads[n][None]
    return (loss, grad_x[None], *[grads[n] for n in WEIGHT_ORDER], *[delta[n] for n in WEIGHT_ORDER],
            *[new_m[n] for n in WEIGHT_ORDER], *[new_v[n] for n in WEIGHT_ORDER])
```

```python
import math

import jax
import jax.numpy as jnp
import numpy as np
from jax import lax
from jax.experimental import pallas as pl
from jax.experimental.pallas import tpu as pltpu

F32 = jnp.float32
BF16 = jnp.bfloat16

N_DEV = 8
HEAD_DIM = 64
A_Q_HEADS = 12
A_KV_HEADS = 4
A_GROUP = 3
B_HEADS = 12
C_HEADS = 4
C_HEAD_DIM = 128
WINDOW = 128
A_WIDTH = 768
A_KV_WIDTH = 256
B_WIDTH = 768
C_WIDTH = 512
EPS = 1e-6
NEG = -1e30

COL_QA, COL_KA, COL_VA = 0, 768, 1024
COL_QB, COL_KB, COL_VB = 1280, 2048, 2816
COL_QC = 3584
W_QKV = 4096
Q_SPLIT = 1280
COL_ZA, COL_ZB, COL_ZC = 0, 768, 1536
COL_GATE = W_Z = 2048
SRC_RANGES = ((0, 1280), (2048, 4352), (5132, 5644), (1280, 2048), (4352, 5120), (5644, 6156))
SRC_GATE = 6156
FB_SRC = 5120
FB_PAD = 128

ADAM_LR = 0.001
ADAM_B1 = 0.9
ADAM_B2 = 0.999
ADAM_EPS = 1e-08
ADAM_WD = 0.01
ADAM_STEP = 10

VMEM_BIG = 52 * 1024 * 1024
LANES = 128
MESH = pl.DeviceIdType.MESH


def _tile(n, pref, mult=128):
    if n <= pref:
        return n
    t = (pref // mult) * mult
    while t >= mult:
        if n % t == 0:
            return t
        t -= mult
    return n


def _params(sem=None, vmem=None):
    kw = {}
    if sem is not None:
        kw["dimension_semantics"] = sem
    if vmem is not None:
        kw["vmem_limit_bytes"] = vmem
    return pltpu.CompilerParams(**kw)


def _sigmoid(x):
    return 1.0 / (1.0 + jnp.exp(-x))


def _block_diag(hd):
    r = np.arange(LANES)
    return jnp.asarray((r[:, None] // hd) == (r[None, :] // hd), dtype=BF16)


def _seg_sum(t, bd):
    hi = t.astype(BF16)
    lo = (t - hi.astype(F32)).astype(BF16)
    outs = []
    for c in range(t.shape[1] // LANES):
        sl = slice(c * LANES, (c + 1) * LANES)
        outs.append(jnp.dot(hi[:, sl], bd, preferred_element_type=F32) + jnp.dot(lo[:, sl], bd, preferred_element_type=F32))
    return outs[0] if len(outs) == 1 else jnp.concatenate(outs, axis=1)


def _rmsnorm_fwd(x, gain, name):
    rows, d = x.shape
    bm = _tile(rows, 512, 8)

    def body(x_ref, g_ref, o_ref):
        xv = x_ref[...]
        ms = jnp.mean(xv * xv, axis=-1, keepdims=True)
        o_ref[...] = (xv * lax.rsqrt(ms + EPS) * g_ref[...]).astype(BF16)

    return pl.pallas_call(
        body, name=name, grid=(rows // bm,),
        in_specs=[pl.BlockSpec((bm, d), lambda i: (i, 0)), pl.BlockSpec((1, d), lambda i: (0, 0))],
        out_specs=pl.BlockSpec((bm, d), lambda i: (i, 0)),
        out_shape=jax.ShapeDtypeStruct((rows, d), BF16),
        compiler_params=_params(("parallel",)),
    )(x, gain)


def _rmsnorm_bwd(x, dhn, gain, dy, name):
    rows, d = x.shape
    bm = _tile(rows, 512, 8)
    with_dx = dy is not None

    def body(*refs):
        if with_dx:
            x_ref, dh_ref, g_ref, dy_ref, gx_ref, dg_ref = refs
        else:
            x_ref, dh_ref, g_ref, dg_ref = refs
        i = pl.program_id(0)
        xv = x_ref[...]
        rstd = lax.rsqrt(jnp.mean(xv * xv, axis=-1, keepdims=True) + EPS)
        xhat = xv * rstd
        dh = dh_ref[...]
        part = jnp.sum((dh * xhat).reshape(bm // 8, 8, d), axis=0)

        @pl.when(i == 0)
        def _():
            dg_ref[...] = part

        @pl.when(i > 0)
        def _():
            dg_ref[...] += part

        if with_dx:
            g = dh * g_ref[...]
            mean = jnp.mean(g * xhat, axis=-1, keepdims=True)
            gx_ref[...] = dy_ref[...] + rstd * (g - xhat * mean)

    row_spec = pl.BlockSpec((bm, d), lambda i: (i, 0))
    in_specs = [row_spec, row_spec, pl.BlockSpec((1, d), lambda i: (0, 0))]
    args = [x, dhn, gain]
    dg_spec = pl.BlockSpec((8, d), lambda i: (0, 0))
    dg_shape = jax.ShapeDtypeStruct((8, d), F32)
    if with_dx:
        in_specs.append(row_spec)
        args.append(dy)
        out_specs = [row_spec, dg_spec]
        out_shape = [jax.ShapeDtypeStruct((rows, d), F32), dg_shape]
    else:
        out_specs = [dg_spec]
        out_shape = [dg_shape]
    outs = pl.pallas_call(
        body, name=name, grid=(rows // bm,), in_specs=in_specs, out_specs=out_specs, out_shape=out_shape,
        compiler_params=_params(("arbitrary",), VMEM_BIG),
    )(*args)
    return outs if with_dx else (None, outs[0])


class _Comm:
    def __init__(self, kind, arrays):
        self.kind = kind
        self.arrays = list(arrays)
        self.n = len(self.arrays)

    def out_shapes(self):
        if self.kind == "gather":
            return [jax.ShapeDtypeStruct((N_DEV,) + a.shape, a.dtype) for a in self.arrays]
        return [jax.ShapeDtypeStruct(a.shape, a.dtype) for a in self.arrays]

    def scratch(self):
        return [pltpu.SemaphoreType.DMA((self.n, N_DEV - 1)), pltpu.SemaphoreType.DMA((self.n, N_DEV - 1)),
                pltpu.SemaphoreType.DMA((self.n,))]

    def _plan(self, ins, outs, sems, with_recvs):
        send_sems, recv_sems, local_sems = sems
        x, y, c = lax.axis_index("x"), lax.axis_index("y"), lax.axis_index("c")
        my = 4 * x + 2 * y + c
        gather = self.kind == "gather"
        local, sends, recvs = [], [], []
        for a in range(self.n):
            local.append(pltpu.make_async_copy(ins[a] if gather else ins[a].at[my], outs[a].at[my], local_sems.at[a]))
            for k in range(1, N_DEV):
                peer = (x ^ ((k >> 2) & 1), y ^ ((k >> 1) & 1), c ^ (k & 1))
                pid = 4 * peer[0] + 2 * peer[1] + peer[2]
                src = ins[a] if gather else ins[a].at[pid]
                sem = dict(send_sem=send_sems.at[a, k - 1], recv_sem=recv_sems.at[a, k - 1], device_id=peer, device_id_type=MESH)
                sends.append(pltpu.make_async_remote_copy(src_ref=src, dst_ref=outs[a].at[my], **sem))
                if with_recvs:
                    recvs.append(pltpu.make_async_remote_copy(src_ref=src, dst_ref=outs[a].at[pid], **sem))
        return local, sends, recvs

    def start(self, ins, outs, sems):
        local, sends, _ = self._plan(ins, outs, sems, False)
        for cp in local + sends:
            cp.start()

    def wait(self, ins, outs, sems):
        local, sends, recvs = self._plan(ins, outs, sems, True)
        for cp in recvs:
            cp.wait_recv()
        for cp in sends:
            cp.wait_send()
        for cp in local:
            cp.wait()


def _grid_edges(grid):
    first = last = None
    for ax, size in enumerate(grid):
        pid = pl.program_id(ax)
        f, l = pid == 0, pid == size - 1
        first = f if first is None else first & f
        last = l if last is None else last & l
    return first, last


def _hosted_call(body, comm, *, name, grid, in_specs, out_specs, out_shape, scratch_shapes, args, sem, vmem=None):
    in_specs, out_specs, out_shape, scratch_shapes = list(in_specs), list(out_specs), list(out_shape), list(scratch_shapes)
    if comm is None:
        res = pl.pallas_call(body, name=name, grid=grid, in_specs=in_specs, out_specs=out_specs, out_shape=out_shape,
                             scratch_shapes=scratch_shapes, compiler_params=_params(sem, vmem))(*args)
        return list(res), []
    n_in, n_out, n_scr, nc = len(in_specs), len(out_shape), len(scratch_shapes), comm.n

    def hosted(*refs):
        ins = refs[0:n_in]
        comm_in = refs[n_in:n_in + nc]
        outs = refs[n_in + nc:n_in + nc + n_out]
        comm_out = refs[n_in + nc + n_out:n_in + 2 * nc + n_out]
        scr = refs[n_in + 2 * nc + n_out:n_in + 2 * nc + n_out + n_scr]
        sems = refs[n_in + 2 * nc + n_out + n_scr:]
        first, last = _grid_edges(grid)

        @pl.when(first)
        def _():
            comm.start(comm_in, comm_out, sems)

        body(*ins, *outs, *scr)

        @pl.when(last)
        def _():
            comm.wait(comm_in, comm_out, sems)

    any_spec = pl.BlockSpec(memory_space=pl.ANY)
    res = pl.pallas_call(
        hosted, name=name, grid=grid, in_specs=in_specs + [any_spec] * nc, out_specs=out_specs + [any_spec] * nc,
        out_shape=out_shape + comm.out_shapes(), scratch_shapes=scratch_shapes + comm.scratch(),
        compiler_params=_params(("arbitrary",) * len(grid), vmem),
    )(*args, *comm.arrays)
    return list(res[0:n_out]), list(res[n_out:])


def _mm(a, b, *, grid, a_spec, b_spec, o_spec, o_shape, o_dtype, contract, name, add=None, add_spec=None, acc_shape=None,
        comm=None):
    nk = grid[2]
    has_add = add is not None

    def body(*refs):
        a_ref, b_ref = refs[0], refs[1]
        add_ref = refs[2] if has_add else None
        o_ref = refs[3] if has_add else refs[2]
        part = lax.dot_general(a_ref[...], b_ref[...], (contract, ((), ())), preferred_element_type=F32)
        if nk == 1:
            if has_add:
                part = part + add_ref[...]
            o_ref[...] = part.astype(o_dtype)
        else:
            acc = refs[-1]
            k = pl.program_id(2)

            @pl.when(k == 0)
            def _():
                acc[...] = part

            @pl.when(k > 0)
            def _():
                acc[...] += part

            @pl.when(k == nk - 1)
            def _():
                r = acc[...]
                if has_add:
                    r = r + add_ref[...]
                o_ref[...] = r.astype(o_dtype)

    in_specs = [a_spec, b_spec] + ([add_spec] if has_add else [])
    args = [a, b] + ([add] if has_add else [])
    scratch = [pltpu.VMEM(acc_shape, F32)] if nk > 1 else []
    outs, comm_outs = _hosted_call(
        body, comm, name=name, grid=grid, in_specs=in_specs, out_specs=[o_spec],
        out_shape=[jax.ShapeDtypeStruct(o_shape, o_dtype)], scratch_shapes=scratch, args=args,
        sem=("parallel", "parallel", "arbitrary"), vmem=VMEM_BIG)
    return outs[0] if comm is None else (outs[0], comm_outs)


def _mm_nn(a, b, *, bm, bn, bk, o_dtype, name, add=None, comm=None):
    m, kd = a.shape
    n = b.shape[1]
    bm, bn, bk = _tile(m, bm, 8), _tile(n, bn), _tile(kd, bk)
    o_spec = pl.BlockSpec((bm, bn), lambda i, j, k: (i, j))
    return _mm(a, b, grid=(m // bm, n // bn, kd // bk),
               a_spec=pl.BlockSpec((bm, bk), lambda i, j, k: (i, k)),
               b_spec=pl.BlockSpec((bk, bn), lambda i, j, k: (k, j)),
               o_spec=o_spec, o_shape=(m, n), o_dtype=o_dtype, contract=((1,), (0,)), name=name,
               add=add, add_spec=o_spec, acc_shape=(bm, bn), comm=comm)


def _mm_nt(a, b, *, bm, bn, bk, o_dtype, name, add=None, b_col0=0, comm=None):
    m, kd = a.shape
    n = b.shape[0]
    bm, bn, bk = _tile(m, bm, 8), _tile(n, bn), _tile(math.gcd(kd, b_col0), bk)
    kb0 = b_col0 // bk
    o_spec = pl.BlockSpec((bm, bn), lambda i, j, k: (i, j))
    return _mm(a, b, grid=(m // bm, n // bn, kd // bk),
               a_spec=pl.BlockSpec((bm, bk), lambda i, j, k: (i, k)),
               b_spec=pl.BlockSpec((bn, bk), lambda i, j, k: (j, kb0 + k)),
               o_spec=o_spec, o_shape=(m, n), o_dtype=o_dtype, contract=((1,), (1,)), name=name,
               add=add, add_spec=o_spec, acc_shape=(bm, bn), comm=comm)


def _mm_nt_sum(terms, *, bm, bn, bk, name, add=None, comm=None):
    m = terms[0][0].shape[0]
    n = terms[0][1].shape[0]
    bm, bn = _tile(m, bm, 8), _tile(n, bn)
    nt = (((1,), (1,)), ((), ()))
    plan, groups, start = [], [], 0
    for a, b, col0 in terms:
        kd = a.shape[1]
        tk = _tile(math.gcd(kd, col0), bk)
        steps = kd // tk
        if plan and kd < bk:
            groups.append([b, start - 1, 1, col0 // tk, tk])
            plan.append((start - 1, 1, len(groups) - 1, True))
            continue
        last = groups[-1] if groups else None
        if last is not None and last[0] is b and last[4] == tk and (last[3] + last[2]) * tk == col0:
            last[2] += steps
        else:
            groups.append([b, start, steps, col0 // tk, tk])
        plan.append((start, steps, len(groups) - 1, False))
        start += steps
    nk = start
    nterm, ngroup, has_add = len(terms), len(groups), add is not None

    def body(*refs):
        a_refs, b_refs = refs[0:nterm], refs[nterm:nterm + ngroup]
        add_ref = refs[nterm + ngroup] if has_add else None
        o_ref, acc = refs[nterm + ngroup + has_add], refs[nterm + ngroup + has_add + 1]
        k = pl.program_id(2)
        for t, (s0, steps, grp, rides) in enumerate(plan):
            @pl.when((k >= s0) & (k < s0 + steps))
            def _():
                part = lax.dot_general(a_refs[t][...], b_refs[grp][...], nt, preferred_element_type=F32)
                if rides:
                    acc[...] += part
                    return

                @pl.when(k == 0)
                def _():
                    acc[...] = part

                @pl.when(k > 0)
                def _():
                    acc[...] += part

        @pl.when(k == nk - 1)
        def _():
            o_ref[...] = acc[...] + add_ref[...] if has_add else acc[...]

    def a_spec(tk, s0, steps):
        return pl.BlockSpec((bm, tk), lambda i, j, k: (i, jnp.clip(k - s0, 0, steps - 1)))

    def b_spec(tk, s0, steps, off):
        return pl.BlockSpec((bn, tk), lambda i, j, k: (j, off + jnp.clip(k - s0, 0, steps - 1)))

    o_spec = pl.BlockSpec((bm, bn), lambda i, j, k: (i, j))
    in_specs = [a_spec(groups[grp][4], s0, steps) for s0, steps, grp, _ in plan]
    in_specs += [b_spec(tk, s0, steps, cb0) for _, s0, steps, cb0, tk in groups]
    args = [a for a, _, _ in terms] + [grp[0] for grp in groups]
    if has_add:
        in_specs.append(o_spec)
        args.append(add)
    outs, comm_outs = _hosted_call(
        body, comm, name=name, grid=(m // bm, n // bn, nk), in_specs=in_specs,
        out_specs=[o_spec], out_shape=[jax.ShapeDtypeStruct((m, n), F32)],
        scratch_shapes=[pltpu.VMEM((bm, bn), F32)], args=args,
        sem=("parallel", "parallel", "arbitrary"), vmem=VMEM_BIG)
    return outs[0] if comm is None else (outs[0], comm_outs)


def _mm_tn(a, b, *, bm, bn, bk, o_dtype, name, comm=None):
    kd, m = a.shape
    n = b.shape[1]
    bm, bn, bk = _tile(m, bm), _tile(n, bn), _tile(kd, bk, 8)
    return _mm(a, b, grid=(m // bm, n // bn, kd // bk),
               a_spec=pl.BlockSpec((bk, bm), lambda i, j, k: (k, i)),
               b_spec=pl.BlockSpec((bk, bn), lambda i, j, k: (k, j)),
               o_spec=pl.BlockSpec((bm, bn), lambda i, j, k: (i, j)),
               o_shape=(m, n), o_dtype=o_dtype, contract=((0,), (0,)), name=name, acc_shape=(bm, bn), comm=comm)


def _branch_full(w8):
    kb, ds = w8.shape[0] // N_DEV, w8.shape[1]
    return w8.reshape(N_DEV, kb, ds).transpose(1, 0, 2).reshape(kb, N_DEV * ds)


def _branch_shards(g):
    kb, ds = g.shape[0], g.shape[1] // N_DEV
    return g.reshape(kb, N_DEV, ds).transpose(1, 0, 2).reshape(N_DEV * kb, ds)


def _headnorm_fwd(src, c0, width, bw, hd, gain, nflag, head_major, name):
    rows = src.shape[0]
    bm = _tile(rows, 2048 if bw <= 256 else 1024, 16)
    bd = _block_diag(hd)
    cb0 = c0 // bw

    def body(x_ref, g_ref, f_ref, bd_ref, o_ref):
        xv = x_ref[...].astype(F32)
        ss = _seg_sum(xv * xv, bd_ref[...])
        rstd = lax.rsqrt(ss * (1.0 / hd) + EPS)
        y = (xv * jnp.where(f_ref[...] > 0.0, rstd, 1.0) * g_ref[...]).astype(BF16)
        if head_major:
            for h in range(bw // HEAD_DIM):
                o_ref[h] = y[:, h * HEAD_DIM:(h + 1) * HEAD_DIM]
        else:
            o_ref[...] = y

    vec_spec = pl.BlockSpec((1, bw), lambda i, t: (0, t))
    if head_major:
        hpb = bw // HEAD_DIM
        out_spec = pl.BlockSpec((hpb, bm, HEAD_DIM), lambda i, t: (t, i, 0))
        out_shape = jax.ShapeDtypeStruct((width // HEAD_DIM, rows, HEAD_DIM), BF16)
    else:
        out_spec = pl.BlockSpec((bm, bw), lambda i, t: (i, t))
        out_shape = jax.ShapeDtypeStruct((rows, width), BF16)
    return pl.pallas_call(
        body, name=name, grid=(rows // bm, width // bw),
        in_specs=[pl.BlockSpec((bm, bw), lambda i, t: (i, cb0 + t)), vec_spec, vec_spec,
                  pl.BlockSpec((LANES, LANES), lambda i, t: (0, 0))],
        out_specs=out_spec, out_shape=out_shape,
        compiler_params=_params(("parallel", "parallel")),
    )(src, gain, nflag, bd)


def _headnorm_bwd(src, c0, width, bw, hd, gain, nflag, dyn, target, t0, name):
    rows = src.shape[0]
    bm = _tile(rows, 2048 if bw <= 256 else 1024, 16)
    bd = _block_diag(hd)
    cb0 = c0 // bw
    tb0 = t0 // bw
    aliased = target is not None

    def body(*refs):
        if aliased:
            x_ref, dy_ref, g_ref, f_ref, bd_ref, _, o_ref, dg_ref = refs
        else:
            x_ref, dy_ref, g_ref, f_ref, bd_ref, o_ref, dg_ref = refs
        i = pl.program_id(1)
        xv = x_ref[...].astype(F32)
        dyv = dy_ref[...]
        bdv = bd_ref[...]
        rstd = lax.rsqrt(_seg_sum(xv * xv, bdv) * (1.0 / hd) + EPS)
        xhat = xv * rstd
        g = dyv * g_ref[...]
        mean = _seg_sum(g * xhat, bdv) * (1.0 / hd)
        dx = jnp.where(f_ref[...] > 0.0, rstd * (g - xhat * mean), g)
        o_ref[...] = dx.astype(BF16)
        part = jnp.sum((dyv * xhat).reshape(bm // 8, 8, bw), axis=0)

        @pl.when(i == 0)
        def _():
            dg_ref[...] = part

        @pl.when(i > 0)
        def _():
            dg_ref[...] += part

    vec_spec = pl.BlockSpec((1, bw), lambda t, i: (0, t))
    in_specs = [pl.BlockSpec((bm, bw), lambda t, i: (i, cb0 + t)), pl.BlockSpec((bm, bw), lambda t, i: (i, t)),
                vec_spec, vec_spec, pl.BlockSpec((LANES, LANES), lambda t, i: (0, 0))]
    args = [src, dyn, gain, nflag, bd]
    aliases = {}
    if aliased:
        in_specs.append(pl.BlockSpec(memory_space=pl.ANY))
        args.append(target)
        aliases = {5: 0}
        o_shape = jax.ShapeDtypeStruct(target.shape, BF16)
    else:
        o_shape = jax.ShapeDtypeStruct((rows, width), BF16)
    out, dg = pl.pallas_call(
        body, name=name, grid=(width // bw, rows // bm), in_specs=in_specs,
        out_specs=[pl.BlockSpec((bm, bw), lambda t, i: (i, tb0 + t)), pl.BlockSpec((8, bw), lambda t, i: (0, t))],
        out_shape=[o_shape, jax.ShapeDtypeStruct((8, width), F32)],
        input_output_aliases=aliases,
        compiler_params=_params(("parallel", "arbitrary")),
    )(*args)
    return out, dg


def _fox_prep(pfb, bpad, name):
    s = pfb.shape[0]

    def body(p_ref, b_ref, c_ref):
        z = p_ref[...] + b_ref[...]
        logf = jnp.minimum(z, 0.0) - jnp.log(1.0 + jnp.exp(-jnp.abs(z)))
        x = logf.T[0:16, :]
        lane = lax.broadcasted_iota(jnp.int32, (16, s), 1)
        sh = 1
        while sh < s:
            x = x + jnp.where(lane >= sh, pltpu.roll(x, sh, 1), 0.0)
            sh *= 2
        c_ref[...] = x

    return pl.pallas_call(
        body, name=name, grid=(1,),
        in_specs=[pl.BlockSpec((s, FB_PAD), lambda i: (0, 0)), pl.BlockSpec((1, FB_PAD), lambda i: (0, 0))],
        out_specs=pl.BlockSpec((16, s), lambda i: (0, 0)),
        out_shape=jax.ShapeDtypeStruct((16, s), F32),
        compiler_params=_params(("arbitrary",)),
    )(pfb, bpad)


def _fox_prep_bwd(pfb, bpad, dct, name):
    s = pfb.shape[0]

    def body(p_ref, b_ref, dc_ref, df_ref, db_ref):
        zt = (p_ref[...] + b_ref[...]).T[0:16, :]
        y = dc_ref[...]
        lane = lax.broadcasted_iota(jnp.int32, (16, s), 1)
        sh = 1
        while sh < s:
            y = y + jnp.where(lane < s - sh, pltpu.roll(y, s - sh, 1), 0.0)
            sh *= 2
        dz = y * _sigmoid(-zt)
        db_ref[...] = jnp.broadcast_to(jnp.sum(dz, axis=1, keepdims=True), (16, FB_PAD))
        full = jnp.concatenate([dz, jnp.zeros((FB_PAD - 16, s), F32)], axis=0)
        df_ref[...] = full.T.astype(BF16)

    return pl.pallas_call(
        body, name=name, grid=(1,),
        in_specs=[pl.BlockSpec((s, FB_PAD), lambda i: (0, 0)), pl.BlockSpec((1, FB_PAD), lambda i: (0, 0)),
                  pl.BlockSpec((16, s), lambda i: (0, 0))],
        out_specs=[pl.BlockSpec((s, FB_PAD), lambda i: (0, 0)), pl.BlockSpec((16, FB_PAD), lambda i: (0, 0))],
        out_shape=[jax.ShapeDtypeStruct((s, FB_PAD), BF16), jax.ShapeDtypeStruct((16, FB_PAD), F32)],
        compiler_params=_params(("arbitrary",)),
    )(pfb, bpad, dct)


def _swa_window(n):
    ws = pl.multiple_of(jnp.maximum(n * WINDOW - WINDOW, 0), WINDOW)
    qi = lax.broadcasted_iota(jnp.int32, (WINDOW, 2 * WINDOW), 0)
    kj = lax.broadcasted_iota(jnp.int32, (WINDOW, 2 * WINDOW), 1)
    rel = qi + (n * WINDOW - ws) - kj
    valid = (rel >= 0) & (rel < WINDOW)
    return ws, valid, rel.astype(F32)


def _attn_a_fwd(qkv, sinks, slopes, name):
    s = qkv.shape[1]
    nb = s // WINDOW
    smem = pl.BlockSpec(memory_space=pltpu.SMEM)

    def body(sink_ref, slope_ref, q_ref, k_ref, v_ref, o_ref, lse_ref):
        n = pl.program_id(0)
        ws, valid, relf = _swa_window(n)
        outs = []
        for h in range(A_Q_HEADS):
            kvh = h // A_GROUP
            kw = k_ref[kvh, pl.ds(ws, 2 * WINDOW), :]
            vw = v_ref[kvh, pl.ds(ws, 2 * WINDOW), :]
            sc = lax.dot_general(q_ref[h], kw, (((1,), (1,)), ((), ())), preferred_element_type=F32)
            sc = jnp.where(valid, sc - slope_ref[h] * relf, NEG)
            sink = sink_ref[h]
            m = jnp.maximum(jnp.max(sc, axis=1, keepdims=True), sink)
            p = jnp.exp(sc - m)
            denom = jnp.sum(p, axis=1, keepdims=True) + jnp.exp(sink - m)
            pn = (p / denom).astype(BF16)
            outs.append(jnp.dot(pn, vw, preferred_element_type=F32))
            lse_ref[h] = jnp.broadcast_to(m + jnp.log(denom), (WINDOW, HEAD_DIM))
        o_ref[...] = jnp.concatenate(outs, axis=1)

    return pl.pallas_call(
        body, name=name, grid=(nb,),
        in_specs=[smem, smem,
                  pl.BlockSpec((A_Q_HEADS, WINDOW, HEAD_DIM), lambda n: (0, n, 0)),
                  pl.BlockSpec((A_KV_HEADS, s, HEAD_DIM), lambda n: (A_GROUP, 0, 0)),
                  pl.BlockSpec((A_KV_HEADS, s, HEAD_DIM), lambda n: (A_GROUP + 1, 0, 0))],
        out_specs=[pl.BlockSpec((WINDOW, A_WIDTH), lambda n: (n, 0)),
                   pl.BlockSpec((A_Q_HEADS, WINDOW, HEAD_DIM), lambda n: (0, n, 0))],
        out_shape=[jax.ShapeDtypeStruct((s, A_WIDTH), F32), jax.ShapeDtypeStruct((A_Q_HEADS, s, HEAD_DIM), F32)],
        compiler_params=_params(("parallel",), VMEM_BIG),
    )(sinks, slopes, qkv, qkv, qkv)


def _attn_a_bwd(qkv, do, lse, dd, sinks, slopes, name, comm=None):
    s = qkv.shape[1]
    nb = s // WINDOW
    smem = pl.BlockSpec(memory_space=pltpu.SMEM)
    last = nb - 1

    def body(sink_ref, slope_ref, q_ref, k_ref, v_ref, do_ref, lse_ref, dd_ref, dq_ref, dkv_ref, ds_ref, carry):
        n = pl.program_id(0)

        @pl.when(n == 0)
        def _():
            carry[...] = jnp.zeros(carry.shape, F32)
            ds_ref[...] = jnp.zeros(ds_ref.shape, F32)

        @pl.when(n < nb)
        def _():
            ws, valid, relf = _swa_window(n)
            dqs = []
            dkw = [None] * A_KV_HEADS
            dvw = [None] * A_KV_HEADS
            for h in range(A_Q_HEADS):
                kvh = h // A_GROUP
                qh = q_ref[h]
                doh = do_ref[h]
                kw = k_ref[kvh, pl.ds(ws, 2 * WINDOW), :]
                vw = v_ref[kvh, pl.ds(ws, 2 * WINDOW), :]
                lse_h = lse_ref[h]
                dd_h = dd_ref[h]
                sc = lax.dot_general(qh, kw, (((1,), (1,)), ((), ())), preferred_element_type=F32)
                sc = jnp.where(valid, sc - slope_ref[h] * relf, NEG)
                p = jnp.exp(sc - lse_h[:, 0:1])
                dp = lax.dot_general(doh, vw, (((1,), (1,)), ((), ())), preferred_element_type=F32)
                dsc = (p * (dp - dd_h[:, 0:1])).astype(BF16)
                pb = p.astype(BF16)
                dqs.append(jnp.dot(dsc, kw, preferred_element_type=F32))
                dk_h = jnp.dot(qh.T, dsc, preferred_element_type=F32)
                dv_h = jnp.dot(doh.T, pb, preferred_element_type=F32)
                dkw[kvh] = dk_h if dkw[kvh] is None else dkw[kvh] + dk_h
                dvw[kvh] = dv_h if dvw[kvh] is None else dvw[kvh] + dv_h
                psink = jnp.exp(sink_ref[h] - lse_h)
                ds_ref[h] += jnp.sum((-psink * dd_h).reshape(WINDOW // 8, 8, HEAD_DIM), axis=0)
            dq_ref[...] = jnp.concatenate(dqs, axis=1)
            win = jnp.concatenate(dkw + dvw, axis=0)
            first = win[:, 0:WINDOW]
            second = win[:, WINDOW:2 * WINDOW]
            dkv_ref[...] = (carry[...] + first).T
            carry[...] = jnp.where(n == 0, first, second)

        @pl.when(n == nb)
        def _():
            dkv_ref[...] = carry[...].T

    hm = lambda heads: pl.BlockSpec((heads, WINDOW, HEAD_DIM), lambda n: (0, jnp.minimum(n, last), 0))
    res = lambda blk: pl.BlockSpec((A_KV_HEADS, s, HEAD_DIM), lambda n: (blk, 0, 0))
    outs, comm_outs = _hosted_call(
        body, comm, name=name, grid=(nb + 1,),
        in_specs=[smem, smem, hm(A_Q_HEADS), res(A_GROUP), res(A_GROUP + 1), hm(A_Q_HEADS), hm(A_Q_HEADS), hm(A_Q_HEADS)],
        out_specs=[pl.BlockSpec((WINDOW, A_WIDTH), lambda n: (jnp.minimum(n, last), 0)),
                   pl.BlockSpec((WINDOW, 2 * A_KV_WIDTH), lambda n: (jnp.maximum(n - 1, 0), 0)),
                   pl.BlockSpec((A_Q_HEADS, 8, HEAD_DIM), lambda n: (0, 0, 0))],
        out_shape=[jax.ShapeDtypeStruct((s, A_WIDTH), F32), jax.ShapeDtypeStruct((s, 2 * A_KV_WIDTH), F32),
                   jax.ShapeDtypeStruct((A_Q_HEADS, 8, HEAD_DIM), F32)],
        scratch_shapes=[pltpu.VMEM((2 * A_KV_WIDTH, WINDOW), F32)],
        args=[sinks, slopes, qkv, qkv, qkv, do, lse, dd], sem=("arbitrary",), vmem=VMEM_BIG)
    return outs[0], outs[1], outs[2], comm_outs


def _attn_b_fwd(qkv, c3, name, comm=None):
    heads, s = qkv.shape[0] // 3, qkv.shape[1]
    hpairs = heads // 2
    bq = min(512, s)
    nq = s // bq
    nt = (((1,), (1,)), ((), ()))

    def body(q_ref, k_ref, v_ref, c_ref, o_ref, lse_ref, m_scr, l_scr, acc_scr):
        i = pl.program_id(1)
        r0 = pl.multiple_of(i * bq, bq)
        row = lax.broadcasted_iota(jnp.int32, (bq, bq), 0)
        col = lax.broadcasted_iota(jnp.int32, (bq, bq), 1)
        m_scr[...] = jnp.full((2, bq, LANES), NEG, F32)
        l_scr[...] = jnp.zeros((2, bq, LANES), F32)
        acc_scr[...] = jnp.zeros((2, bq, HEAD_DIM), F32)

        def step(j, masked):
            k0 = pl.multiple_of(j * bq, bq)
            for h2 in range(2):
                kv = k_ref[h2, pl.ds(k0, bq), :]
                vv = v_ref[h2, pl.ds(k0, bq), :]
                cq0 = c_ref[h2, :, pl.ds(r0, LANES)][:, 0:1]
                sc = lax.dot_general(q_ref[h2], kv, nt, preferred_element_type=F32)
                sc = sc + (cq0 - c_ref[h2, :, pl.ds(k0, bq)])
                if masked:
                    sc = jnp.where(col <= row, sc, NEG)
                m_prev = m_scr[h2]
                m_new = jnp.maximum(m_prev, jnp.max(sc, axis=1, keepdims=True))
                alpha = jnp.exp(m_prev - m_new)
                p = jnp.exp(sc - m_new[:, 0:1])
                l_scr[h2] = alpha * l_scr[h2] + jnp.sum(p, axis=1, keepdims=True)
                p_hi = p.astype(BF16)
                p_lo = (p - p_hi.astype(F32)).astype(BF16)
                pv = jnp.dot(p_hi, vv, preferred_element_type=F32) + jnp.dot(p_lo, vv, preferred_element_type=F32)
                acc_scr[h2] = acc_scr[h2] * alpha[:, 0:HEAD_DIM] + pv
                m_scr[h2] = m_new

        def loop_body(j, carry):
            step(j, False)
            return carry

        lax.fori_loop(0, i, loop_body, 0)
        step(i, True)
        outs = []
        for h2 in range(2):
            l = l_scr[h2]
            outs.append(acc_scr[h2] / l[:, 0:HEAD_DIM])
            lse_ref[h2] = (m_scr[h2] + jnp.log(l))[:, 0:HEAD_DIM]
        o_ref[...] = jnp.concatenate(outs, axis=1)

    res = lambda off: pl.BlockSpec((2, s, HEAD_DIM), lambda hp, i: (off + hp, 0, 0))
    outs, comm_outs = _hosted_call(
        body, comm, name=name, grid=(hpairs, nq),
        in_specs=[pl.BlockSpec((2, bq, HEAD_DIM), lambda hp, i: (hp, i, 0)), res(hpairs), res(2 * hpairs),
                  pl.BlockSpec((2, 1, s), lambda hp, i: (hp, 0, 0))],
        out_specs=[pl.BlockSpec((bq, 2 * HEAD_DIM), lambda hp, i: (i, hp)),
                   pl.BlockSpec((2, bq, HEAD_DIM), lambda hp, i: (hp, i, 0))],
        out_shape=[jax.ShapeDtypeStruct((s, heads * HEAD_DIM), F32), jax.ShapeDtypeStruct((heads, s, HEAD_DIM), F32)],
        scratch_shapes=[pltpu.VMEM((2, bq, LANES), F32), pltpu.VMEM((2, bq, LANES), F32), pltpu.VMEM((2, bq, HEAD_DIM), F32)],
        args=[qkv, qkv, qkv, c3], sem=("parallel", "parallel"), vmem=VMEM_BIG)
    return outs[0], outs[1], comm_outs


def _attn_b_bwd(qkv, do, lse, dd, c3, name, comm=None):
    heads, s = qkv.shape[0] // 3, qkv.shape[1]
    hpairs = heads // 2
    bq = min(512, s)
    nq = s // bq
    nt = (((1,), (1,)), ((), ()))
    tn = (((0,), (0,)), ((), ()))
    grid = (heads // 2, nq)

    def body(q_ref, k_ref, v_ref, do_ref, lse_ref, dd_ref, c_ref, dq_ref, dk_ref, dv_ref, dc_ref,
             dq_scr, dk_scr, dv_scr, dc_scr):
        j = pl.program_id(1)
        k0 = pl.multiple_of(j * bq, bq)
        row = lax.broadcasted_iota(jnp.int32, (bq, bq), 0)
        col = lax.broadcasted_iota(jnp.int32, (bq, bq), 1)

        @pl.when(j == 0)
        def _():
            dq_scr[...] = jnp.zeros(dq_scr.shape, F32)

        dk_scr[...] = jnp.zeros((2, HEAD_DIM, bq), F32)
        dv_scr[...] = jnp.zeros((2, HEAD_DIM, bq), F32)
        dc_scr[...] = jnp.zeros((2, 1, bq), F32)
        k_t = [k_ref[h2].T for h2 in range(2)]

        def step(i, masked):
            r0 = pl.multiple_of(i * bq, bq)
            for h2 in range(2):
                kv = k_ref[h2]
                vv = v_ref[h2]
                qv = q_ref[h2, pl.ds(r0, bq), :]
                dov = do_ref[h2, pl.ds(r0, bq), :]
                lse_v = lse_ref[h2, pl.ds(r0, bq), :][:, 0:1]
                dd_v = dd_ref[h2, pl.ds(r0, bq), :][:, 0:1]
                cq0 = c_ref[h2, :, pl.ds(r0, LANES)][:, 0:1]
                sc = lax.dot_general(qv, kv, nt, preferred_element_type=F32) + (cq0 - c_ref[h2, :, pl.ds(k0, bq)])
                if masked:
                    sc = jnp.where(col <= row, sc, NEG)
                p = jnp.exp(sc - lse_v)
                dp = lax.dot_general(dov, vv, nt, preferred_element_type=F32)
                dsc = p * (dp - dd_v)
                dsb = dsc.astype(BF16)
                dv_scr[h2] += jnp.dot(dov.T, p.astype(BF16), preferred_element_type=F32)
                dk_scr[h2] += jnp.dot(qv.T, dsb, preferred_element_type=F32)
                dq_scr[h2, :, pl.ds(r0, bq)] += jnp.dot(k_t[h2], dsb.T, preferred_element_type=F32)
                dc_scr[h2] -= jnp.sum(dsc, axis=0, keepdims=True)

        def loop_body(i, carry):
            step(i, False)
            return carry

        step(j, True)
        lax.fori_loop(j + 1, nq, loop_body, 0)
        dc_ref[...] = dc_scr[...]
        dk_ref[...] = jnp.concatenate([dk_scr[0].T, dk_scr[1].T], axis=1)
        dv_ref[...] = jnp.concatenate([dv_scr[0].T, dv_scr[1].T], axis=1)

        @pl.when(j == nq - 1)
        def _():
            dq_ref[...] = jnp.concatenate([dq_scr[0].T, dq_scr[1].T], axis=1)

    res = pl.BlockSpec((2, s, HEAD_DIM), lambda hp, j: (hp, 0, 0))
    blk = lambda off: pl.BlockSpec((2, bq, HEAD_DIM), lambda hp, j: (off + hp, j, 0))
    tm = jax.ShapeDtypeStruct((s, heads * HEAD_DIM), F32)
    in_specs = [res, blk(hpairs), blk(2 * hpairs), res, res, res, pl.BlockSpec((2, 1, s), lambda hp, j: (hp, 0, 0))]
    out_specs = [pl.BlockSpec((s, 2 * HEAD_DIM), lambda hp, j: (0, hp)),
                 pl.BlockSpec((bq, 2 * HEAD_DIM), lambda hp, j: (j, hp)),
                 pl.BlockSpec((bq, 2 * HEAD_DIM), lambda hp, j: (j, hp)),
                 pl.BlockSpec((2, 1, bq), lambda hp, j: (hp, 0, j))]
    out_shape = [tm, tm, tm, jax.ShapeDtypeStruct((heads, 1, s), F32)]
    scratch = [pltpu.VMEM((2, HEAD_DIM, s), F32), pltpu.VMEM((2, HEAD_DIM, bq), F32),
               pltpu.VMEM((2, HEAD_DIM, bq), F32), pltpu.VMEM((2, 1, bq), F32)]
    outs, comm_outs = _hosted_call(
        body, comm, name=name, grid=grid, in_specs=in_specs, out_specs=out_specs, out_shape=out_shape,
        scratch_shapes=scratch, args=[qkv, qkv, qkv, do, lse, dd, c3], sem=("parallel", "arbitrary"), vmem=VMEM_BIG)
    return outs[0], outs[1], outs[2], outs[3], comm_outs


def _attn_c_probs(qh, mkh):
    sc = lax.dot_general(qh, mkh, (((1,), (1,)), ((), ())), preferred_element_type=F32) * (C_HEAD_DIM ** -0.5)
    p = jnp.exp(sc - jnp.max(sc, axis=1, keepdims=True))
    return p / jnp.sum(p, axis=1, keepdims=True)


def _attn_c_fwd(q, mkv, name):
    s = q.shape[0]
    m = mkv.shape[0]
    bq = _tile(s, 512, 8)

    def body(q_ref, mk_ref, mv_ref, o_ref):
        outs = []
        for h in range(C_HEADS):
            sl = slice(h * C_HEAD_DIM, (h + 1) * C_HEAD_DIM)
            pn = _attn_c_probs(q_ref[:, sl], mk_ref[:, sl]).astype(BF16)
            outs.append(jnp.dot(pn, mv_ref[:, sl], preferred_element_type=F32))
        o_ref[...] = jnp.concatenate(outs, axis=1)

    return pl.pallas_call(
        body, name=name, grid=(s // bq,),
        in_specs=[pl.BlockSpec((bq, C_WIDTH), lambda i: (i, 0)), pl.BlockSpec((m, C_WIDTH), lambda i: (0, 0)),
                  pl.BlockSpec((m, C_WIDTH), lambda i: (0, 1))],
        out_specs=pl.BlockSpec((bq, C_WIDTH), lambda i: (i, 0)),
        out_shape=jax.ShapeDtypeStruct((s, C_WIDTH), F32),
        compiler_params=_params(("parallel",)),
    )(q, mkv, mkv)


def _attn_c_bwd(q, mkv, do, name):
    s = q.shape[0]
    m = mkv.shape[0]
    bq = _tile(s, 512, 8)
    tn = (((0,), (0,)), ((), ()))

    def body(q_ref, mk_ref, mv_ref, do_ref, dq_ref, dm_ref):
        i = pl.program_id(0)

        @pl.when(i == 0)
        def _():
            dm_ref[...] = jnp.zeros(dm_ref.shape, F32)

        dqs = []
        for h in range(C_HEADS):
            sl = slice(h * C_HEAD_DIM, (h + 1) * C_HEAD_DIM)
            qh, mkh, mvh, doh = q_ref[:, sl], mk_ref[:, sl], mv_ref[:, sl], do_ref[:, sl]
            pn = _attn_c_probs(qh, mkh)
            dp = lax.dot_general(doh, mvh, (((1,), (1,)), ((), ())), preferred_element_type=F32)
            dsc = (pn * (dp - jnp.sum(pn * dp, axis=1, keepdims=True)) * (C_HEAD_DIM ** -0.5)).astype(BF16)
            dqs.append(jnp.dot(dsc, mkh, preferred_element_type=F32))
            dm_ref[:, sl] += lax.dot_general(dsc, qh, tn, preferred_element_type=F32)
            sv = slice(C_WIDTH + h * C_HEAD_DIM, C_WIDTH + (h + 1) * C_HEAD_DIM)
            dm_ref[:, sv] += lax.dot_general(pn.astype(BF16), doh, tn, preferred_element_type=F32)
        dq_ref[...] = jnp.concatenate(dqs, axis=1)

    row = pl.BlockSpec((bq, C_WIDTH), lambda i: (i, 0))
    return pl.pallas_call(
        body, name=name, grid=(s // bq,),
        in_specs=[row, pl.BlockSpec((m, C_WIDTH), lambda i: (0, 0)), pl.BlockSpec((m, C_WIDTH), lambda i: (0, 1)), row],
        out_specs=[row, pl.BlockSpec((m, 2 * C_WIDTH), lambda i: (0, 0))],
        out_shape=[jax.ShapeDtypeStruct((s, C_WIDTH), F32), jax.ShapeDtypeStruct((m, 2 * C_WIDTH), F32)],
        compiler_params=_params(("arbitrary",)),
    )(q, mkv, mkv, do)


def _gate_fwd(y, proj, zc0, bw, name):
    rows, width = y.shape
    bm = _tile(rows, 2048 if bw <= 256 else 1024, 16)
    cb0 = zc0 // bw

    def body(y_ref, z_ref, o_ref):
        z = z_ref[...].astype(F32)
        o_ref[...] = (y_ref[...] * (z * _sigmoid(z))).astype(BF16)

    return pl.pallas_call(
        body, name=name, grid=(rows // bm, width // bw),
        in_specs=[pl.BlockSpec((bm, bw), lambda i, t: (i, t)), pl.BlockSpec((bm, bw), lambda i, t: (i, cb0 + t))],
        out_specs=pl.BlockSpec((bm, bw), lambda i, t: (i, t)),
        out_shape=jax.ShapeDtypeStruct((rows, width), BF16),
        compiler_params=_params(("parallel", "parallel")),
    )(y, proj)


def _gate_bwd(dsv, y, proj, zc0, bw, dproj, t0, head_major, name):
    rows, width = y.shape
    bm = _tile(rows, 2048 if bw <= 256 else 1024, 16)
    cb0 = zc0 // bw
    tb0 = t0 // bw
    bd = _block_diag(HEAD_DIM)
    hpb = bw // HEAD_DIM

    def body(*refs):
        if head_major:
            ds_ref, y_ref, z_ref, bd_ref, _, dp_ref, dy_ref, dd_ref = refs
        else:
            ds_ref, y_ref, z_ref, _, dp_ref, dy_ref = refs
        z = z_ref[...].astype(F32)
        sig = _sigmoid(z)
        dsx = ds_ref[...]
        yv = y_ref[...]
        dy = dsx * (z * sig)
        dp_ref[...] = (dsx * yv * (sig * (1.0 + z * (1.0 - sig)))).astype(BF16)
        if head_major:
            dyb = dy.astype(BF16)
            dd = _seg_sum(dyb.astype(F32) * yv, bd_ref[...])
            for h in range(hpb):
                sl = slice(h * HEAD_DIM, (h + 1) * HEAD_DIM)
                dy_ref[h] = dyb[:, sl]
                dd_ref[h] = dd[:, sl]
        else:
            dy_ref[...] = dy.astype(BF16)

    tile = pl.BlockSpec((bm, bw), lambda i, t: (i, t))
    ztile = pl.BlockSpec((bm, bw), lambda i, t: (i, cb0 + t))
    ttile = pl.BlockSpec((bm, bw), lambda i, t: (i, tb0 + t))
    any_spec = pl.BlockSpec(memory_space=pl.ANY)
    dp_shape = jax.ShapeDtypeStruct(dproj.shape, BF16)
    if head_major:
        hm_spec = pl.BlockSpec((hpb, bm, HEAD_DIM), lambda i, t: (t, i, 0))
        nh = width // HEAD_DIM
        outs = pl.pallas_call(
            body, name=name, grid=(rows // bm, width // bw),
            in_specs=[tile, tile, ztile, pl.BlockSpec((LANES, LANES), lambda i, t: (0, 0)), any_spec],
            out_specs=[ttile, hm_spec, hm_spec],
            out_shape=[dp_shape, jax.ShapeDtypeStruct((nh, rows, HEAD_DIM), BF16),
                       jax.ShapeDtypeStruct((nh, rows, HEAD_DIM), F32)],
            input_output_aliases={4: 0},
            compiler_params=_params(("parallel", "parallel")),
        )(dsv, y, proj, bd, dproj)
        return outs[0], outs[1], outs[2]
    outs = pl.pallas_call(
        body, name=name, grid=(rows // bm, width // bw),
        in_specs=[tile, tile, ztile, any_spec],
        out_specs=[ttile, tile],
        out_shape=[dp_shape, jax.ShapeDtypeStruct((rows, width), BF16)],
        input_output_aliases={3: 0},
        compiler_params=_params(("parallel", "parallel")),
    )(dsv, y, proj, dproj)
    return outs[0], outs[1], None


def _merge_fwd(proj, ua, ub, uc, name):
    rows, d = ua.shape
    bm = _tile(rows, 1024, 16)
    bw = _tile(d, 512)
    g0 = COL_GATE // bw
    gstep = d // bw

    def body(la_ref, lb_ref, lc_ref, ua_ref, ub_ref, uc_ref, o_ref, ga_ref, gb_ref, gc_ref):
        y = None
        for l_ref, u_ref, g_ref in ((la_ref, ua_ref, ga_ref), (lb_ref, ub_ref, gb_ref), (lc_ref, uc_ref, gc_ref)):
            g = _sigmoid(l_ref[...].astype(F32))
            g_ref[...] = g.astype(BF16)
            term = g * u_ref[...].astype(F32)
            y = term if y is None else y + term
        o_ref[...] = y.astype(BF16)

    tile = pl.BlockSpec((bm, bw), lambda i, t: (i, t))
    gate = lambda b: pl.BlockSpec((bm, bw), lambda i, t: (i, g0 + b * gstep + t))
    shape = jax.ShapeDtypeStruct((rows, d), BF16)
    return pl.pallas_call(
        body, name=name, grid=(rows // bm, d // bw),
        in_specs=[gate(0), gate(1), gate(2), tile, tile, tile],
        out_specs=[tile] * 4, out_shape=[shape] * 4,
        compiler_params=_params(("parallel", "parallel")),
    )(proj, proj, proj, ua, ub, uc)


def _merge_bwd(dym, us, gs, name):
    rows, d = dym.shape
    bm = _tile(rows, 256, 16)

    def body(dy_ref, ua_ref, ub_ref, uc_ref, ga_ref, gb_ref, gc_ref, dg_ref, da_ref, db_ref, dc_ref):
        dyv = dy_ref[...]
        for b, (u_ref, g_ref, du_ref) in enumerate(((ua_ref, ga_ref, da_ref), (ub_ref, gb_ref, db_ref), (uc_ref, gc_ref, dc_ref))):
            g = g_ref[...].astype(F32)
            du_ref[...] = (g * dyv).astype(BF16)
            dg_ref[:, b * d:(b + 1) * d] = (dyv * u_ref[...].astype(F32) * g * (1.0 - g)).astype(BF16)

    tile = pl.BlockSpec((bm, d), lambda i: (i, 0))
    shape = jax.ShapeDtypeStruct((rows, d), BF16)
    outs = pl.pallas_call(
        body, name=name, grid=(rows // bm,),
        in_specs=[tile] * 7,
        out_specs=[pl.BlockSpec((bm, 3 * d), lambda i: (i, 0)), tile, tile, tile],
        out_shape=[jax.ShapeDtypeStruct((rows, 3 * d), BF16), shape, shape, shape],
        compiler_params=_params(("parallel",), VMEM_BIG),
    )(dym, *us, *gs)
    return outs[0], outs[1], outs[2], outs[3]


def _out_proj_loss(ym, wo, x, target, name):
    m, d = x.shape
    bm, bn = _tile(m, 1024, 16), _tile(d, 1024)
    grid = (m // bm, d // bn)

    def body(a_ref, b_ref, x_ref, t_ref, dy_ref, dyb_ref, l_ref):
        first, _ = _grid_edges(grid)
        y = jnp.dot(a_ref[...], b_ref[...], preferred_element_type=F32) + x_ref[...]
        diff = y - t_ref[...]
        dy = diff * (1.0 / d)
        dy_ref[...] = dy
        dyb_ref[...] = dy.astype(BF16)
        sq = diff * diff
        part = sq[:, 0:LANES]
        for c in range(1, bn // LANES):
            part = part + sq[:, c * LANES:(c + 1) * LANES]
        part = jnp.sum(part.reshape(bm // 8, 8, LANES), axis=0)

        @pl.when(first)
        def _():
            l_ref[...] = part

        @pl.when(jnp.logical_not(first))
        def _():
            l_ref[...] += part

    tile = pl.BlockSpec((bm, bn), lambda i, j: (i, j))
    return pl.pallas_call(
        body, name=name, grid=grid,
        in_specs=[pl.BlockSpec((bm, d), lambda i, j: (i, 0)), pl.BlockSpec((d, bn), lambda i, j: (0, j)), tile, tile],
        out_specs=[tile, tile, pl.BlockSpec((8, LANES), lambda i, j: (0, 0))],
        out_shape=[jax.ShapeDtypeStruct((m, d), F32), jax.ShapeDtypeStruct((m, d), BF16),
                   jax.ShapeDtypeStruct((8, LANES), F32)],
        compiler_params=_params(("arbitrary", "arbitrary"), VMEM_BIG),
    )(ym, wo, x, target)


def _row(vec, reps=1):
    return jnp.tile(vec.reshape(1, -1).astype(F32), (1, reps))


def _local_step(x, mem, target, small, wg, shards=None):
    s, d = x.shape
    dist = shards is not None
    wg = dict(wg)
    ones = lambda n: jnp.ones((1, n), F32)
    zeros = lambda n: jnp.zeros((1, n), F32)
    scale_ab = HEAD_DIM ** -0.5
    split8 = lambda g: g.reshape(N_DEV, g.shape[0] // N_DEV, g.shape[1])
    flat8 = lambda g: g.reshape(g.shape[0] * g.shape[1], g.shape[2])
    gather = lambda names: _Comm("gather", [shards[n] for n in names]) if dist else None
    g = {}

    def scatter(names):
        return _Comm("scatter", [split8(g[n]) for n in names]) if dist else None

    def hosted(result, names, store):
        if not dist:
            return result
        out, got = result
        store.update(zip(names, got))
        return out

    hn = _rmsnorm_fwd(x, small["norm_gain"], "rms_x_fwd")
    got = {}
    proj = hosted(_mm_nn(hn, wg["qkv"], bm=1024, bn=1024, bk=d, o_dtype=BF16, name="proj_qkv",
                         comm=gather(("wa", "wb"))), ("wa", "wb"), got)
    wg.update({n: flat8(a) for n, a in got.items()})
    pfb = _mm_nn(hn, wg["wf"], bm=1024, bn=FB_PAD, bk=d, o_dtype=F32, name="proj_fb")
    mn = _rmsnorm_fwd(mem, small["mem_norm_gain"], "rms_mem_fwd")
    mkv = _mm_nn(mn, wg["wk"], bm=256, bn=1024, bk=d, o_dtype=F32, name="mem_kv")

    gain_a = jnp.concatenate([_row(small["q_gain_a"], A_Q_HEADS) * scale_ab, _row(small["k_gain_a"], A_KV_HEADS), ones(A_KV_WIDTH)], axis=1)
    flag_a = jnp.concatenate([ones(A_WIDTH + A_KV_WIDTH), zeros(A_KV_WIDTH)], axis=1)
    qkv_a = _headnorm_fwd(proj, COL_QA, 1280, 1280, HEAD_DIM, gain_a, flag_a, True, "hn_a_fwd")
    gain_b = jnp.concatenate([_row(small["q_gain_b"], B_HEADS) * scale_ab, _row(small["k_gain_b"], B_HEADS), ones(B_WIDTH)], axis=1)
    flag_b = jnp.concatenate([ones(2 * B_WIDTH), zeros(B_WIDTH)], axis=1)
    qkv_b = _headnorm_fwd(proj, COL_QB, 2304, 256, HEAD_DIM, gain_b, flag_b, True, "hn_b_fwd")
    gain_cq = _row(small["q_gain_c"], C_HEADS)
    q_c = _headnorm_fwd(proj, COL_QC, C_WIDTH, C_WIDTH, C_HEAD_DIM, gain_cq, ones(C_WIDTH), False, "hn_cq_fwd")
    gain_ck = jnp.concatenate([_row(small["k_gain_c"], C_HEADS), ones(C_WIDTH)], axis=1)
    flag_ck = jnp.concatenate([ones(C_WIDTH), zeros(C_WIDTH)], axis=1)
    mkvn = _headnorm_fwd(mkv, 0, 2 * C_WIDTH, 2 * C_WIDTH, C_HEAD_DIM, gain_ck, flag_ck, False, "hn_ck_fwd")


    bpad = jnp.pad(small["b_forget"].reshape(1, -1), ((0, 0), (0, FB_PAD - B_HEADS)))
    c16 = _fox_prep(pfb, bpad, "fox_prep")
    c3 = c16[0:B_HEADS].reshape(B_HEADS, 1, s)

    sinks = small["sinks_a"].reshape(-1)
    slopes = jnp.exp2(-8.0 * jnp.arange(1, A_Q_HEADS + 1, dtype=F32) / A_Q_HEADS)
    y_a, lse_a = _attn_a_fwd(qkv_a, sinks, slopes, "attn_a_fwd")
    y_b, lse_b, got_zg = _attn_b_fwd(qkv_b, c3, "attn_b_fwd", comm=gather(("zg",)))
    if dist:
        wg["zg"] = flat8(got_zg[0])
    y_c = _attn_c_fwd(q_c, mkvn, "attn_c_fwd")

    got = {}
    pzg = hosted(_mm_nn(hn, wg["zg"], bm=1024, bn=1024, bk=d, o_dtype=BF16, name="proj_zg", comm=gather(("wo", "wc"))),
                 ("wo", "wc"), got)
    wg.update({n: flat8(a) for n, a in got.items()})

    s_a = _gate_fwd(y_a, pzg, COL_ZA, 256, "gate_a_fwd")
    s_b = _gate_fwd(y_b, pzg, COL_ZB, 256, "gate_b_fwd")
    s_c = _gate_fwd(y_c, pzg, COL_ZC, 512, "gate_c_fwd")
    w_a, w_b, w_c = _branch_full(wg["wa"]), _branch_full(wg["wb"]), _branch_full(wg["wc"])
    u_a = _mm_nn(s_a, w_a, bm=1024, bn=2048, bk=A_WIDTH, o_dtype=BF16, name="branch_a_fwd")
    u_b = _mm_nn(s_b, w_b, bm=1024, bn=2048, bk=B_WIDTH, o_dtype=BF16, name="branch_b_fwd")
    u_c = _mm_nn(s_c, w_c, bm=1024, bn=2048, bk=C_WIDTH, o_dtype=BF16, name="branch_c_fwd")
    ym, gate_a, gate_b, gate_c = _merge_fwd(pzg, u_a, u_b, u_c, "merge_fwd")
    dy, dyb, lpart = _out_proj_loss(ym, wg["wo"], x, target, "out_proj_loss")
    loss = 0.5 / d * jnp.sum(lpart)

    dym = _mm_nt(dyb, wg["wo"], bm=1024, bn=1024, bk=d, o_dtype=F32, name="out_proj_bwd_act")
    g["wo"] = _mm_tn(ym, dyb, bm=512, bn=1024, bk=s, o_dtype=BF16, name="out_proj_bwd_w")

    dgate, du_a, du_b, du_c = _merge_bwd(dym, (u_a, u_b, u_c), (gate_a, gate_b, gate_c), "merge_bwd")
    parts = {}
    g["wm_g"] = hosted(_mm_tn(hn, dgate, bm=512, bn=1024, bk=s, o_dtype=BF16, name="proj_gate_bwd_w",
                              comm=scatter(("wo",))), ("wo",), parts)

    ds_a = _mm_nt(du_a, w_a, bm=1024, bn=A_WIDTH, bk=d, o_dtype=F32, name="branch_a_bwd_act")
    ds_b = _mm_nt(du_b, w_b, bm=1024, bn=B_WIDTH, bk=d, o_dtype=F32, name="branch_b_bwd_act")
    ds_c = _mm_nt(du_c, w_c, bm=1024, bn=C_WIDTH, bk=d, o_dtype=F32, name="branch_c_bwd_act")
    g["wa"] = _branch_shards(_mm_tn(s_a, du_a, bm=A_WIDTH, bn=1024, bk=s, o_dtype=BF16, name="branch_a_bwd_w"))
    g["wb"] = _branch_shards(_mm_tn(s_b, du_b, bm=B_WIDTH, bn=1024, bk=s, o_dtype=BF16, name="branch_b_bwd_w"))
    g["wc"] = _branch_shards(_mm_tn(s_c, du_c, bm=C_WIDTH, bn=1024, bk=s, o_dtype=BF16, name="branch_c_bwd_w"))

    dz = lax.empty((s, W_Z), BF16)
    dz, do_a, dd_a = _gate_bwd(ds_a, y_a, pzg, COL_ZA, 256, dz, COL_ZA, True, "gate_a_bwd")
    dz, do_b, dd_b = _gate_bwd(ds_b, y_b, pzg, COL_ZB, 256, dz, COL_ZB, True, "gate_b_bwd")
    dz, do_c, _ = _gate_bwd(ds_c, y_c, pzg, COL_ZC, 512, dz, COL_ZC, False, "gate_c_bwd")
    g["wm_z"] = _mm_tn(hn, dz, bm=512, bn=1024, bk=s, o_dtype=BF16, name="proj_z_bwd_w")

    names = ("wa", "wb", "wc")
    dq_a, dkv_a, dsink, got = _attn_a_bwd(qkv_a, do_a, lse_a, dd_a, sinks, slopes, "attn_a_bwd", comm=scatter(names))
    parts.update(zip(names, got))
    names = ("wm_g", "wm_z")
    dq_b, dk_b, dv_b, dc3, got = _attn_b_bwd(qkv_b, do_b, lse_b, dd_b, c3, "attn_b_bwd", comm=scatter(names))
    parts.update(zip(names, got))
    dq_c, dmkvn = _attn_c_bwd(q_c, mkvn, do_c, "attn_c_bwd")

    dqkv = lax.empty((s, W_QKV), BF16)
    dqkv, dg_qa = _headnorm_bwd(proj, COL_QA, A_WIDTH, 256, HEAD_DIM, gain_a[:, 0:768], flag_a[:, 0:768], dq_a, dqkv, COL_QA, "hn_qa_bwd")
    dqkv, dg_kva = _headnorm_bwd(proj, COL_KA, 512, 256, HEAD_DIM, gain_a[:, 768:1280], flag_a[:, 768:1280], dkv_a, dqkv, COL_KA, "hn_kva_bwd")
    dqkv, dg_qb = _headnorm_bwd(proj, COL_QB, B_WIDTH, 256, HEAD_DIM, gain_b[:, 0:768], flag_b[:, 0:768], dq_b, dqkv, COL_QB, "hn_qb_bwd")
    dqkv, dg_kb = _headnorm_bwd(proj, COL_KB, B_WIDTH, 256, HEAD_DIM, gain_b[:, 768:1536], flag_b[:, 768:1536], dk_b, dqkv, COL_KB, "hn_kb_bwd")
    dqkv, _ = _headnorm_bwd(proj, COL_VB, B_WIDTH, 256, HEAD_DIM, gain_b[:, 1536:2304], flag_b[:, 1536:2304], dv_b, dqkv, COL_VB, "hn_vb_bwd")
    dqkv, dg_qc = _headnorm_bwd(proj, COL_QC, C_WIDTH, 512, C_HEAD_DIM, gain_cq, ones(C_WIDTH), dq_c, dqkv, COL_QC, "hn_qc_bwd")
    dmkv, dg_kc = _headnorm_bwd(mkv, 0, 2 * C_WIDTH, 2 * C_WIDTH, C_HEAD_DIM, gain_ck, flag_ck, dmkvn, None, 0, "hn_kc_bwd")

    dct = jnp.pad(dc3.reshape(B_HEADS, s), ((0, 16 - B_HEADS), (0, 0)))
    dfb, dbf = _fox_prep_bwd(pfb, bpad, dct, "fox_prep_bwd")

    dmn = _mm_nt(dmkv, wg["wk"], bm=256, bn=1024, bk=1024, o_dtype=F32, name="mem_kv_bwd_act")
    g["wk"] = _mm_tn(mn, dmkv, bm=512, bn=1024, bk=mem.shape[0], o_dtype=BF16, name="mem_kv_bwd_w")
    _, dg_mem = _rmsnorm_bwd(mem, dmn, small["mem_norm_gain"], None, "rms_mem_bwd")

    g["wm_qkv"] = _mm_tn(hn, dqkv, bm=512, bn=1024, bk=s, o_dtype=BF16, name="proj_qkv_bwd_w")
    g["wf"] = _mm_tn(hn, dfb, bm=512, bn=FB_PAD, bk=s, o_dtype=BF16, name="proj_fb_bwd_w")
    half = Q_SPLIT
    g["wm_q1"], g["wm_q2"] = g["wm_qkv"][:, 0:half], g["wm_qkv"][:, half:W_QKV]
    names = ("wm_q1",)
    dhn = hosted(_mm_nt_sum([(dqkv, wg["qkv"], 0), (dfb, wg["wf"], 0)], bm=1024, bn=1024, bk=2048,
                            name="proj_qkv_bwd_act", comm=scatter(names)), names, parts)
    names = ("wm_q2", "wf", "wk")
    dhn = hosted(_mm_nt_sum([(dz, wg["zg"], COL_ZA), (dgate, wg["zg"], COL_GATE)], bm=1024, bn=1024, bk=2048,
                            name="proj_zg_bwd_act", add=dhn, comm=scatter(names)), names, parts)
    if dist:
        g = parts
    grad_x, dg_x = _rmsnorm_bwd(x, dhn, small["norm_gain"], dy, "rms_x_bwd")

    fold = lambda part, heads, hd: jnp.sum(jnp.sum(part, axis=0).reshape(heads, hd), axis=0).reshape(1, hd)
    small_grads = {
        "norm_gain": jnp.sum(dg_x, axis=0).reshape(1, d),
        "mem_norm_gain": jnp.sum(dg_mem, axis=0).reshape(1, d),
        "b_forget": dbf[0:B_HEADS, 0].reshape(1, B_HEADS),
        "q_gain_a": fold(dg_qa, A_Q_HEADS, HEAD_DIM) * scale_ab,
        "k_gain_a": fold(dg_kva[:, 0:A_KV_WIDTH], A_KV_HEADS, HEAD_DIM),
        "sinks_a": (jnp.sum(dsink, axis=(1, 2)) * (1.0 / HEAD_DIM)).reshape(1, A_Q_HEADS),
        "q_gain_b": fold(dg_qb, B_HEADS, HEAD_DIM) * scale_ab,
        "k_gain_b": fold(dg_kb, B_HEADS, HEAD_DIM),
        "q_gain_c": fold(dg_qc, C_HEADS, C_HEAD_DIM),
        "k_gain_c": fold(dg_kc[:, 0:C_WIDTH], C_HEADS, C_HEAD_DIM),
    }
    return loss, grad_x, small_grads, g


def _coords():
    return lax.axis_index("x"), lax.axis_index("y"), lax.axis_index("c")


def _all_gather(shards, name):
    n = len(shards)

    def body(*refs):
        ins = refs[0:n]
        outs = refs[n:2 * n]
        send_sems, recv_sems, local_sems = refs[2 * n:2 * n + 3]
        x, y, c = _coords()
        me, sibling = (x, y, c), (x, y, 1 - c)
        chips = [(1 - x, y), (x, 1 - y), (1 - x, 1 - y)]
        idx = lambda p: 4 * p[0] + 2 * p[1] + p[2]

        def copy(a, k, block, to, src=None):
            slot = outs[a].at[idx(block)]
            return pltpu.make_async_remote_copy(
                src_ref=slot if src is None else src, dst_ref=slot,
                send_sem=send_sems.at[a, k], recv_sem=recv_sems.at[a, k], device_id=to, device_id_type=MESH)

        mine = [pltpu.make_async_copy(ins[a], outs[a].at[idx(me)], local_sems.at[a]) for a in range(n)]
        for cp in mine:
            cp.start()
        first = []
        for a in range(n):
            first.append(copy(a, 0, me, sibling, src=ins[a]))
            first += [copy(a, 1 + j, me, (*chip, c), src=ins[a]) for j, chip in enumerate(chips)]
        for cp in first:
            cp.start()
        passed = []
        for j, chip in enumerate(chips):
            for a in range(n):
                copy(a, 1 + j, (*chip, c), me).wait_recv()
                fwd = copy(a, 4 + j, (*chip, c), sibling)
                fwd.start()
                passed.append(fwd)
        for a in range(n):
            copy(a, 0, sibling, me).wait_recv()
            for j, chip in enumerate(chips):
                copy(a, 4 + j, (*chip, 1 - c), me).wait_recv()
        for cp in first + passed:
            cp.wait_send()
        for cp in mine:
            cp.wait()

    any_spec = pl.BlockSpec(memory_space=pl.ANY)
    return pl.pallas_call(
        body, name=name,
        in_specs=[any_spec] * n, out_specs=[any_spec] * n,
        out_shape=[jax.ShapeDtypeStruct((N_DEV,) + sh.shape, sh.dtype) for sh in shards],
        scratch_shapes=[pltpu.SemaphoreType.DMA((n, 7)), pltpu.SemaphoreType.DMA((n, 7)), pltpu.SemaphoreType.DMA((n,))],
    )(*shards)


def _all_reduce_small(vec, name):
    p = vec.shape[1]

    def body(v_ref, o_ref, gather, send_sems, recv_sems):
        x, y, c = _coords()
        my = 4 * x + 2 * y + c
        peers = [(x ^ ((k >> 2) & 1), y ^ ((k >> 1) & 1), c ^ (k & 1)) for k in range(1, N_DEV)]
        gather[my] = v_ref[...]
        sends = [pltpu.make_async_remote_copy(
            src_ref=v_ref, dst_ref=gather.at[my], send_sem=send_sems.at[k], recv_sem=recv_sems.at[k],
            device_id=peer, device_id_type=MESH) for k, peer in enumerate(peers)]
        for cp in sends:
            cp.start()
        for k, peer in enumerate(peers):
            pid = 4 * peer[0] + 2 * peer[1] + peer[2]
            pltpu.make_async_remote_copy(
                src_ref=v_ref, dst_ref=gather.at[pid], send_sem=send_sems.at[k], recv_sem=recv_sems.at[k],
                device_id=peer, device_id_type=MESH).wait_recv()
        for cp in sends:
            cp.wait_send()
        total = gather[0]
        for j in range(1, N_DEV):
            total = total + gather[j]
        o_ref[...] = total

    vm = pl.BlockSpec(memory_space=pltpu.VMEM)
    return pl.pallas_call(
        body, name=name, in_specs=[vm], out_specs=vm,
        out_shape=jax.ShapeDtypeStruct((8, p), F32),
        scratch_shapes=[pltpu.VMEM((N_DEV, 8, p), F32), pltpu.SemaphoreType.DMA((7,)), pltpu.SemaphoreType.DMA((7,))],
    )(vec)[0:1]


def _sum_parts(parts, name):
    _, rows, cols = parts.shape
    br = _tile(rows, 64, 16)

    def body(p_ref, o_ref):
        total = p_ref[0].astype(F32)
        for j in range(1, N_DEV):
            total = total + p_ref[j].astype(F32)
        o_ref[...] = total

    return pl.pallas_call(
        body, name=name, grid=(rows // br,),
        in_specs=[pl.BlockSpec((N_DEV, br, cols), lambda i: (0, i, 0))],
        out_specs=pl.BlockSpec((br, cols), lambda i: (i, 0)),
        out_shape=jax.ShapeDtypeStruct((rows, cols), F32),
        compiler_params=_params(("parallel",), VMEM_BIG),
    )(parts)


def _adamw(w, g, m, v, name, br=32):
    rows, cols = w.shape
    br = min(br, rows)
    c1 = 1.0 / (1.0 - ADAM_B1 ** ADAM_STEP)
    c2 = 1.0 / (1.0 - ADAM_B2 ** ADAM_STEP)

    def body(w_ref, g_ref, m_ref, v_ref, d_ref, nm_ref, nv_ref):
        gv = g_ref[...]
        nm = ADAM_B1 * m_ref[...] + (1.0 - ADAM_B1) * gv
        nv = ADAM_B2 * v_ref[...] + (1.0 - ADAM_B2) * (gv * gv)
        d_ref[...] = -ADAM_LR * ((nm * c1) / (jnp.sqrt(nv * c2) + ADAM_EPS) + ADAM_WD * w_ref[...])
        nm_ref[...] = nm
        nv_ref[...] = nv

    spec = pl.BlockSpec((br, cols), lambda i: (i, 0))
    shape = jax.ShapeDtypeStruct((rows, cols), F32)
    return pl.pallas_call(
        body, name=name, grid=(pl.cdiv(rows, br),), in_specs=[spec] * 4, out_specs=[spec] * 3, out_shape=[shape] * 3,
        compiler_params=_params(("parallel",), VMEM_BIG),
    )(w, g, m, v)


def _adamw_t(wt, g, mt, vt, name, br=1024):
    n, r = wt.shape
    c1 = 1.0 / (1.0 - ADAM_B1 ** ADAM_STEP)
    c2 = 1.0 / (1.0 - ADAM_B2 ** ADAM_STEP)

    def body(w_ref, g_ref, m_ref, v_ref, d_ref, nm_ref, nv_ref):
        gv = g_ref[...].T
        nm = ADAM_B1 * m_ref[...] + (1.0 - ADAM_B1) * gv
        nv = ADAM_B2 * v_ref[...] + (1.0 - ADAM_B2) * (gv * gv)
        d_ref[...] = -ADAM_LR * ((nm * c1) / (jnp.sqrt(nv * c2) + ADAM_EPS) + ADAM_WD * w_ref[...])
        nm_ref[...] = nm
        nv_ref[...] = nv

    spec = pl.BlockSpec((br, r), lambda i: (i, 0))
    shape = jax.ShapeDtypeStruct((n, r), F32)
    return pl.pallas_call(
        body, name=name, grid=(pl.cdiv(n, br),),
        in_specs=[spec, pl.BlockSpec((r, br), lambda i: (0, i)), spec, spec], out_specs=[spec] * 3, out_shape=[shape] * 3,
        compiler_params=_params(("parallel",), VMEM_BIG),
    )(wt, g, mt, vt)


def _adamw_parts(w, parts, m, v, name):
    rows, cols = w.shape
    br = _tile(rows, 32, 16)
    c1 = 1.0 / (1.0 - ADAM_B1 ** ADAM_STEP)
    c2 = 1.0 / (1.0 - ADAM_B2 ** ADAM_STEP)

    def body(w_ref, p_ref, m_ref, v_ref, g_ref, d_ref, nm_ref, nv_ref):
        gv = p_ref[0].astype(F32)
        for j in range(1, N_DEV):
            gv = gv + p_ref[j].astype(F32)
        nm = ADAM_B1 * m_ref[...] + (1.0 - ADAM_B1) * gv
        nv = ADAM_B2 * v_ref[...] + (1.0 - ADAM_B2) * (gv * gv)
        g_ref[...] = gv
        d_ref[...] = -ADAM_LR * ((nm * c1) / (jnp.sqrt(nv * c2) + ADAM_EPS) + ADAM_WD * w_ref[...])
        nm_ref[...] = nm
        nv_ref[...] = nv

    spec = pl.BlockSpec((br, cols), lambda i: (i, 0))
    shape = jax.ShapeDtypeStruct((rows, cols), F32)
    return pl.pallas_call(
        body, name=name, grid=(rows // br,),
        in_specs=[spec, pl.BlockSpec((N_DEV, br, cols), lambda i: (0, i, 0)), spec, spec],
        out_specs=[spec] * 4, out_shape=[shape] * 4,
        compiler_params=_params(("parallel",), VMEM_BIG),
    )(w, parts, m, v)


SMALL_NAMES = ("norm_gain", "mem_norm_gain", "b_forget", "q_gain_a", "k_gain_a", "sinks_a",
               "q_gain_b", "k_gain_b", "q_gain_c", "k_gain_c")
BIG_NAMES = ("w_in", "w_mem_kv", "w_branch_a", "w_branch_b", "w_branch_c", "w_out")
WEIGHT_ORDER = ("norm_gain", "mem_norm_gain", "w_in", "b_forget", "q_gain_a", "k_gain_a", "sinks_a", "q_gain_b",
                "k_gain_b", "q_gain_c", "k_gain_c", "w_mem_kv", "w_branch_a", "w_branch_b", "w_branch_c", "w_out")


def _pack_small(tree):
    flat = jnp.concatenate([tree[n].reshape(1, -1) for n in SMALL_NAMES], axis=1)
    pad = (-flat.shape[1]) % LANES
    return jnp.pad(flat, ((0, 0), (0, pad)))


def _unpack_small(flat, like):
    out, off = {}, 0
    for n in SMALL_NAMES:
        size = like[n].size
        out[n] = flat[:, off:off + size].reshape(like[n].shape)
        off += size
    return out


def kernel(x, mem, norm_gain, mem_norm_gain, w_in, b_forget, q_gain_a, k_gain_a, sinks_a, q_gain_b, k_gain_b, q_gain_c, k_gain_c, w_mem_kv, w_branch_a, w_branch_b, w_branch_c, w_out, loss_target, m_norm_gain, m_mem_norm_gain, m_w_in, m_b_forget, m_q_gain_a, m_k_gain_a, m_sinks_a, m_q_gain_b, m_k_gain_b, m_q_gain_c, m_k_gain_c, m_w_mem_kv, m_w_branch_a, m_w_branch_b, m_w_branch_c, m_w_out, v_norm_gain, v_mem_norm_gain, v_w_in, v_b_forget, v_q_gain_a, v_k_gain_a, v_sinks_a, v_q_gain_b, v_k_gain_b, v_q_gain_c, v_k_gain_c, v_w_mem_kv, v_w_branch_a, v_w_branch_b, v_w_branch_c, v_w_out):
    weights = dict(norm_gain=norm_gain, mem_norm_gain=mem_norm_gain, w_in=w_in, b_forget=b_forget, q_gain_a=q_gain_a,
                   k_gain_a=k_gain_a, sinks_a=sinks_a, q_gain_b=q_gain_b, k_gain_b=k_gain_b, q_gain_c=q_gain_c,
                   k_gain_c=k_gain_c, w_mem_kv=w_mem_kv, w_branch_a=w_branch_a, w_branch_b=w_branch_b,
                   w_branch_c=w_branch_c, w_out=w_out)
    mom_m = dict(norm_gain=m_norm_gain, mem_norm_gain=m_mem_norm_gain, w_in=m_w_in, b_forget=m_b_forget,
                 q_gain_a=m_q_gain_a, k_gain_a=m_k_gain_a, sinks_a=m_sinks_a, q_gain_b=m_q_gain_b, k_gain_b=m_k_gain_b,
                 q_gain_c=m_q_gain_c, k_gain_c=m_k_gain_c, w_mem_kv=m_w_mem_kv, w_branch_a=m_w_branch_a,
                 w_branch_b=m_w_branch_b, w_branch_c=m_w_branch_c, w_out=m_w_out)
    mom_v = dict(norm_gain=v_norm_gain, mem_norm_gain=v_mem_norm_gain, w_in=v_w_in, b_forget=v_b_forget,
                 q_gain_a=v_q_gain_a, k_gain_a=v_k_gain_a, sinks_a=v_sinks_a, q_gain_b=v_q_gain_b, k_gain_b=v_k_gain_b,
                 q_gain_c=v_q_gain_c, k_gain_c=v_k_gain_c, w_mem_kv=v_w_mem_kv, w_branch_a=v_w_branch_a,
                 w_branch_b=v_w_branch_b, w_branch_c=v_w_branch_c, w_out=v_w_out)
    wi = w_in[0]
    sh_qkv = jnp.concatenate([wi[:, a:b] for a, b in SRC_RANGES[0:3]], axis=1).astype(BF16)
    sh_zg = jnp.concatenate([wi[:, a:b] for a, b in SRC_RANGES[3:6]] + [wi[:, SRC_GATE:]], axis=1).astype(BF16)
    sh_wf = jnp.pad(wi[:, FB_SRC:FB_SRC + B_HEADS], ((0, 0), (0, FB_PAD - B_HEADS))).astype(BF16)
    shards = {"zg": sh_zg, "wo": w_out[0].astype(BF16), "wa": w_branch_a[0].astype(BF16),
              "wb": w_branch_b[0].astype(BF16), "wc": w_branch_c[0].astype(BF16)}
    first = ("qkv", "wf", "wk")
    full = _all_gather([sh_qkv, sh_wf, w_mem_kv[0].astype(BF16)], "weights_all_gather")
    wg = {kname: arr.reshape(arr.shape[0] * arr.shape[1], arr.shape[2]) for kname, arr in zip(first, full)}

    small = {n: weights[n] for n in SMALL_NAMES}
    loss_local, grad_x, small_g, parts = _local_step(x[0], mem[0], loss_target[0], small, wg, shards)

    grads, delta, new_m, new_v = {}, {}, {}, {}
    for n, kname in (("w_mem_kv", "wk"), ("w_out", "wo"), ("w_branch_a", "wa"), ("w_branch_b", "wb"), ("w_branch_c", "wc")):
        gsum, dlt, nm, nv = _adamw_parts(weights[n][0], parts[kname], mom_m[n][0], mom_v[n][0], "adamw_" + n)
        grads[n], delta[n], new_m[n], new_v[n] = gsum, dlt[None], nm[None], nv[None]
    g1, g2, gz, gf, gg = (_sum_parts(parts[k], "grad_sum_" + k) for k in ("wm_q1", "wm_q2", "wm_z", "wf", "wm_g"))
    half = Q_SPLIT
    g_in = jnp.concatenate([g1, g2[:, 0:COL_QB - half], gz[:, COL_ZA:COL_ZB], g2[:, COL_QB - half:COL_QC - half],
                            gz[:, COL_ZB:COL_ZC], gf[:, 0:B_HEADS], g2[:, COL_QC - half:W_QKV - half], gz[:, COL_ZC:W_Z], gg], axis=1)
    dlt, nm, nv = _adamw_t(w_in[0].T, g_in, m_w_in[0].T, v_w_in[0].T, "adamw_w_in")
    grads["w_in"], delta["w_in"], new_m["w_in"], new_v["w_in"] = g_in, dlt.T[None], nm.T[None], nv.T[None]

    packed = _pack_small(small_g)
    packed = jnp.concatenate([packed[:, :-1], loss_local.reshape(1, 1)], axis=1)
    reduced = _all_reduce_small(jnp.broadcast_to(packed, (8, packed.shape[1])), "small_all_reduce")
    grads.update(_unpack_small(reduced, small))
    loss = reduced[0, -1]

    pw, pm, pv = _pack_small(small), _pack_small({n: mom_m[n] for n in SMALL_NAMES}), _pack_small({n: mom_v[n] for n in SMALL_NAMES})
    rep8 = lambda a: jnp.broadcast_to(a, (8, a.shape[1]))
    dlt, nm, nv = _adamw(rep8(pw), rep8(reduced), rep8(pm), rep8(pv), "adamw_small")
    for tree, flat in ((delta, dlt), (new_m, nm), (new_v, nv)):
        tree.update(_unpack_small(flat[0:1], small))
    for n in BIG_NAMES:
        grads[n] = grads[n][None]
    return (loss, grad_x[None], *[grads[n] for n in WEIGHT_ORDER], *[delta[n] for n in WEIGHT_ORDER],
            *[new_m[n] for n in WEIGHT_ORDER], *[new_v[n] for n in WEIGHT_ORDER])
```

```python
import math

import jax
import jax.numpy as jnp
import numpy as np
from jax import lax
from jax.experimental import pallas as pl
from jax.experimental.pallas import tpu as pltpu

F32 = jnp.float32
BF16 = jnp.bfloat16

N_DEV = 8
HEAD_DIM = 64
A_Q_HEADS = 12
A_KV_HEADS = 4
A_GROUP = 3
B_HEADS = 12
C_HEADS = 4
C_HEAD_DIM = 128
WINDOW = 128
A_WIDTH = 768
A_KV_WIDTH = 256
B_WIDTH = 768
C_WIDTH = 512
EPS = 1e-6
NEG = -1e30

COL_QA, COL_KA, COL_VA = 0, 768, 1024
COL_QB, COL_KB, COL_VB = 1280, 2048, 2816
COL_QC = 3584
W_QKV = 4096
Q_SPLIT = 1280
COL_ZA, COL_ZB, COL_ZC = 0, 768, 1536
COL_GATE = W_Z = 2048
SRC_RANGES = ((0, 1280), (2048, 4352), (5132, 5644), (1280, 2048), (4352, 5120), (5644, 6156))
SRC_GATE = 6156
FB_SRC = 5120
FB_PAD = 128

ADAM_LR = 0.001
ADAM_B1 = 0.9
ADAM_B2 = 0.999
ADAM_EPS = 1e-08
ADAM_WD = 0.01
ADAM_STEP = 10

VMEM_BIG = 52 * 1024 * 1024
LANES = 128
MESH = pl.DeviceIdType.MESH


def _tile(n, pref, mult=128):
    if n <= pref:
        return n
    t = (pref // mult) * mult
    while t >= mult:
        if n % t == 0:
            return t
        t -= mult
    return n


def _params(sem=None, vmem=None):
    kw = {}
    if sem is not None:
        kw["dimension_semantics"] = sem
    if vmem is not None:
        kw["vmem_limit_bytes"] = vmem
    return pltpu.CompilerParams(**kw)


def _sigmoid(x):
    return 1.0 / (1.0 + jnp.exp(-x))


def _block_diag(hd):
    r = np.arange(LANES)
    return jnp.asarray((r[:, None] // hd) == (r[None, :] // hd), dtype=BF16)


def _seg_sum(t, bd):
    hi = t.astype(BF16)
    lo = (t - hi.astype(F32)).astype(BF16)
    outs = []
    for c in range(t.shape[1] // LANES):
        sl = slice(c * LANES, (c + 1) * LANES)
        outs.append(jnp.dot(hi[:, sl], bd, preferred_element_type=F32) + jnp.dot(lo[:, sl], bd, preferred_element_type=F32))
    return outs[0] if len(outs) == 1 else jnp.concatenate(outs, axis=1)


def _rmsnorm_fwd(x, gain, name):
    rows, d = x.shape
    bm = _tile(rows, 512, 8)

    def body(x_ref, g_ref, o_ref):
        xv = x_ref[...]
        ms = jnp.mean(xv * xv, axis=-1, keepdims=True)
        o_ref[...] = (xv * lax.rsqrt(ms + EPS) * g_ref[...]).astype(BF16)

    return pl.pallas_call(
        body, name=name, grid=(rows // bm,),
        in_specs=[pl.BlockSpec((bm, d), lambda i: (i, 0)), pl.BlockSpec((1, d), lambda i: (0, 0))],
        out_specs=pl.BlockSpec((bm, d), lambda i: (i, 0)),
        out_shape=jax.ShapeDtypeStruct((rows, d), BF16),
        compiler_params=_params(("parallel",)),
    )(x, gain)


def _rmsnorm_bwd(x, dhn, gain, dy, name):
    rows, d = x.shape
    bm = _tile(rows, 512, 8)
    with_dx = dy is not None

    def body(*refs):
        if with_dx:
            x_ref, dh_ref, g_ref, dy_ref, gx_ref, dg_ref = refs
        else:
            x_ref, dh_ref, g_ref, dg_ref = refs
        i = pl.program_id(0)
        xv = x_ref[...]
        rstd = lax.rsqrt(jnp.mean(xv * xv, axis=-1, keepdims=True) + EPS)
        xhat = xv * rstd
        dh = dh_ref[...]
        part = jnp.sum((dh * xhat).reshape(bm // 8, 8, d), axis=0)

        @pl.when(i == 0)
        def _():
            dg_ref[...] = part

        @pl.when(i > 0)
        def _():
            dg_ref[...] += part

        if with_dx:
            g = dh * g_ref[...]
            mean = jnp.mean(g * xhat, axis=-1, keepdims=True)
            gx_ref[...] = dy_ref[...] + rstd * (g - xhat * mean)

    row_spec = pl.BlockSpec((bm, d), lambda i: (i, 0))
    in_specs = [row_spec, row_spec, pl.BlockSpec((1, d), lambda i: (0, 0))]
    args = [x, dhn, gain]
    dg_spec = pl.BlockSpec((8, d), lambda i: (0, 0))
    dg_shape = jax.ShapeDtypeStruct((8, d), F32)
    if with_dx:
        in_specs.append(row_spec)
        args.append(dy)
        out_specs = [row_spec, dg_spec]
        out_shape = [jax.ShapeDtypeStruct((rows, d), F32), dg_shape]
    else:
        out_specs = [dg_spec]
        out_shape = [dg_shape]
    outs = pl.pallas_call(
        body, name=name, grid=(rows // bm,), in_specs=in_specs, out_specs=out_specs, out_shape=out_shape,
        compiler_params=_params(("arbitrary",), VMEM_BIG),
    )(*args)
    return outs if with_dx else (None, outs[0])


class _Comm:
    def __init__(self, kind, arrays):
        self.kind = kind
        self.arrays = list(arrays)
        self.n = len(self.arrays)

    def out_shapes(self):
        if self.kind == "gather":
            return [jax.ShapeDtypeStruct((N_DEV,) + a.shape, a.dtype) for a in self.arrays]
        return [jax.ShapeDtypeStruct(a.shape, a.dtype) for a in self.arrays]

    def scratch(self):
        return [pltpu.SemaphoreType.DMA((self.n, N_DEV - 1)), pltpu.SemaphoreType.DMA((self.n, N_DEV - 1)),
                pltpu.SemaphoreType.DMA((self.n,))]

    def _plan(self, ins, outs, sems, with_recvs):
        send_sems, recv_sems, local_sems = sems
        x, y, c = lax.axis_index("x"), lax.axis_index("y"), lax.axis_index("c")
        my = 4 * x + 2 * y + c
        gather = self.kind == "gather"
        local, sends, recvs = [], [], []
        for a in range(self.n):
            local.append(pltpu.make_async_copy(ins[a] if gather else ins[a].at[my], outs[a].at[my], local_sems.at[a]))
            for k in range(1, N_DEV):
                peer = (x ^ ((k >> 2) & 1), y ^ ((k >> 1) & 1), c ^ (k & 1))
                pid = 4 * peer[0] + 2 * peer[1] + peer[2]
                src = ins[a] if gather else ins[a].at[pid]
                sem = dict(send_sem=send_sems.at[a, k - 1], recv_sem=recv_sems.at[a, k - 1], device_id=peer, device_id_type=MESH)
                sends.append(pltpu.make_async_remote_copy(src_ref=src, dst_ref=outs[a].at[my], **sem))
                if with_recvs:
                    recvs.append(pltpu.make_async_remote_copy(src_ref=src, dst_ref=outs[a].at[pid], **sem))
        return local, sends, recvs

    def start(self, ins, outs, sems):
        local, sends, _ = self._plan(ins, outs, sems, False)
        for cp in local + sends:
            cp.start()

    def wait(self, ins, outs, sems):
        local, sends, recvs = self._plan(ins, outs, sems, True)
        for cp in recvs:
            cp.wait_recv()
        for cp in sends:
            cp.wait_send()
        for cp in local:
            cp.wait()


def _grid_edges(grid):
    first = last = None
    for ax, size in enumerate(grid):
        pid = pl.program_id(ax)
        f, l = pid == 0, pid == size - 1
        first = f if first is None else first & f
        last = l if last is None else last & l
    return first, last


def _hosted_call(body, comm, *, name, grid, in_specs, out_specs, out_shape, scratch_shapes, args, sem, vmem=None):
    in_specs, out_specs, out_shape, scratch_shapes = list(in_specs), list(out_specs), list(out_shape), list(scratch_shapes)
    if comm is None:
        res = pl.pallas_call(body, name=name, grid=grid, in_specs=in_specs, out_specs=out_specs, out_shape=out_shape,
                             scratch_shapes=scratch_shapes, compiler_params=_params(sem, vmem))(*args)
        return list(res), []
    n_in, n_out, n_scr, nc = len(in_specs), len(out_shape), len(scratch_shapes), comm.n

    def hosted(*refs):
        ins = refs[0:n_in]
        comm_in = refs[n_in:n_in + nc]
        outs = refs[n_in + nc:n_in + nc + n_out]
        comm_out = refs[n_in + nc + n_out:n_in + 2 * nc + n_out]
        scr = refs[n_in + 2 * nc + n_out:n_in + 2 * nc + n_out + n_scr]
        sems = refs[n_in + 2 * nc + n_out + n_scr:]
        first, last = _grid_edges(grid)

        @pl.when(first)
        def _():
            comm.start(comm_in, comm_out, sems)

        body(*ins, *outs, *scr)

        @pl.when(last)
        def _():
            comm.wait(comm_in, comm_out, sems)

    any_spec = pl.BlockSpec(memory_space=pl.ANY)
    res = pl.pallas_call(
        hosted, name=name, grid=grid, in_specs=in_specs + [any_spec] * nc, out_specs=out_specs + [any_spec] * nc,
        out_shape=out_shape + comm.out_shapes(), scratch_shapes=scratch_shapes + comm.scratch(),
        compiler_params=_params(("arbitrary",) * len(grid), vmem),
    )(*args, *comm.arrays)
    return list(res[0:n_out]), list(res[n_out:])


def _mm(a, b, *, grid, a_spec, b_spec, o_spec, o_shape, o_dtype, contract, name, add=None, add_spec=None, acc_shape=None,
        comm=None):
    nk = grid[2]
    has_add = add is not None

    def body(*refs):
        a_ref, b_ref = refs[0], refs[1]
        add_ref = refs[2] if has_add else None
        o_ref = refs[3] if has_add else refs[2]
        part = lax.dot_general(a_ref[...], b_ref[...], (contract, ((), ())), preferred_element_type=F32)
        if nk == 1:
            if has_add:
                part = part + add_ref[...]
            o_ref[...] = part.astype(o_dtype)
        else:
            acc = refs[-1]
            k = pl.program_id(2)

            @pl.when(k == 0)
            def _():
                acc[...] = part

            @pl.when(k > 0)
            def _():
                acc[...] += part

            @pl.when(k == nk - 1)
            def _():
                r = acc[...]
                if has_add:
                    r = r + add_ref[...]
                o_ref[...] = r.astype(o_dtype)

    in_specs = [a_spec, b_spec] + ([add_spec] if has_add else [])
    args = [a, b] + ([add] if has_add else [])
    scratch = [pltpu.VMEM(acc_shape, F32)] if nk > 1 else []
    outs, comm_outs = _hosted_call(
        body, comm, name=name, grid=grid, in_specs=in_specs, out_specs=[o_spec],
        out_shape=[jax.ShapeDtypeStruct(o_shape, o_dtype)], scratch_shapes=scratch, args=args,
        sem=("parallel", "parallel", "arbitrary"), vmem=VMEM_BIG)
    return outs[0] if comm is None else (outs[0], comm_outs)


def _mm_nn(a, b, *, bm, bn, bk, o_dtype, name, add=None, comm=None):
    m, kd = a.shape
    n = b.shape[1]
    bm, bn, bk = _tile(m, bm, 8), _tile(n, bn), _tile(kd, bk)
    o_spec = pl.BlockSpec((bm, bn), lambda i, j, k: (i, j))
    return _mm(a, b, grid=(m // bm, n // bn, kd // bk),
               a_spec=pl.BlockSpec((bm, bk), lambda i, j, k: (i, k)),
               b_spec=pl.BlockSpec((bk, bn), lambda i, j, k: (k, j)),
               o_spec=o_spec, o_shape=(m, n), o_dtype=o_dtype, contract=((1,), (0,)), name=name,
               add=add, add_spec=o_spec, acc_shape=(bm, bn), comm=comm)


def _mm_nt(a, b, *, bm, bn, bk, o_dtype, name, add=None, b_col0=0, comm=None):
    m, kd = a.shape
    n = b.shape[0]
    bm, bn, bk = _tile(m, bm, 8), _tile(n, bn), _tile(math.gcd(kd, b_col0), bk)
    kb0 = b_col0 // bk
    o_spec = pl.BlockSpec((bm, bn), lambda i, j, k: (i, j))
    return _mm(a, b, grid=(m // bm, n // bn, kd // bk),
               a_spec=pl.BlockSpec((bm, bk), lambda i, j, k: (i, k)),
               b_spec=pl.BlockSpec((bn, bk), lambda i, j, k: (j, kb0 + k)),
               o_spec=o_spec, o_shape=(m, n), o_dtype=o_dtype, contract=((1,), (1,)), name=name,
               add=add, add_spec=o_spec, acc_shape=(bm, bn), comm=comm)


def _mm_nt_sum(terms, *, bm, bn, bk, name, add=None, comm=None):
    m = terms[0][0].shape[0]
    n = terms[0][1].shape[0]
    bm, bn = _tile(m, bm, 8), _tile(n, bn)
    nt = (((1,), (1,)), ((), ()))
    plan, groups, start = [], [], 0
    for a, b, col0 in terms:
        kd = a.shape[1]
        tk = _tile(math.gcd(kd, col0), bk)
        steps = kd // tk
        if plan and kd < bk:
            groups.append([b, start - 1, 1, col0 // tk, tk])
            plan.append((start - 1, 1, len(groups) - 1, True))
            continue
        last = groups[-1] if groups else None
        if last is not None and last[0] is b and last[4] == tk and (last[3] + last[2]) * tk == col0:
            last[2] += steps
        else:
            groups.append([b, start, steps, col0 // tk, tk])
        plan.append((start, steps, len(groups) - 1, False))
        start += steps
    nk = start
    nterm, ngroup, has_add = len(terms), len(groups), add is not None

    def body(*refs):
        a_refs, b_refs = refs[0:nterm], refs[nterm:nterm + ngroup]
        add_ref = refs[nterm + ngroup] if has_add else None
        o_ref, acc = refs[nterm + ngroup + has_add], refs[nterm + ngroup + has_add + 1]
        k = pl.program_id(2)
        for t, (s0, steps, grp, rides) in enumerate(plan):
            @pl.when((k >= s0) & (k < s0 + steps))
            def _():
                part = lax.dot_general(a_refs[t][...], b_refs[grp][...], nt, preferred_element_type=F32)
                if rides:
                    acc[...] += part
                    return

                @pl.when(k == 0)
                def _():
                    acc[...] = part

                @pl.when(k > 0)
                def _():
                    acc[...] += part

        @pl.when(k == nk - 1)
        def _():
            o_ref[...] = acc[...] + add_ref[...] if has_add else acc[...]

    def a_spec(tk, s0, steps):
        return pl.BlockSpec((bm, tk), lambda i, j, k: (i, jnp.clip(k - s0, 0, steps - 1)))

    def b_spec(tk, s0, steps, off):
        return pl.BlockSpec((bn, tk), lambda i, j, k: (j, off + jnp.clip(k - s0, 0, steps - 1)))

    o_spec = pl.BlockSpec((bm, bn), lambda i, j, k: (i, j))
    in_specs = [a_spec(groups[grp][4], s0, steps) for s0, steps, grp, _ in plan]
    in_specs += [b_spec(tk, s0, steps, cb0) for _, s0, steps, cb0, tk in groups]
    args = [a for a, _, _ in terms] + [grp[0] for grp in groups]
    if has_add:
        in_specs.append(o_spec)
        args.append(add)
    outs, comm_outs = _hosted_call(
        body, comm, name=name, grid=(m // bm, n // bn, nk), in_specs=in_specs,
        out_specs=[o_spec], out_shape=[jax.ShapeDtypeStruct((m, n), F32)],
        scratch_shapes=[pltpu.VMEM((bm, bn), F32)], args=args,
        sem=("parallel", "parallel", "arbitrary"), vmem=VMEM_BIG)
    return outs[0] if comm is None else (outs[0], comm_outs)


def _mm_tn(a, b, *, bm, bn, bk, o_dtype, name, comm=None):
    kd, m = a.shape
    n = b.shape[1]
    bm, bn, bk = _tile(m, bm), _tile(n, bn), _tile(kd, bk, 8)
    return _mm(a, b, grid=(m // bm, n // bn, kd // bk),
               a_spec=pl.BlockSpec((bk, bm), lambda i, j, k: (k, i)),
               b_spec=pl.BlockSpec((bk, bn), lambda i, j, k: (k, j)),
               o_spec=pl.BlockSpec((bm, bn), lambda i, j, k: (i, j)),
               o_shape=(m, n), o_dtype=o_dtype, contract=((0,), (0,)), name=name, acc_shape=(bm, bn), comm=comm)


def _branch_full(w8):
    kb, ds = w8.shape[0] // N_DEV, w8.shape[1]
    return w8.reshape(N_DEV, kb, ds).transpose(1, 0, 2).reshape(kb, N_DEV * ds)


def _branch_shards(g):
    kb, ds = g.shape[0], g.shape[1] // N_DEV
    return g.reshape(kb, N_DEV, ds).transpose(1, 0, 2).reshape(N_DEV * kb, ds)


def _headnorm_fwd(src, c0, width, bw, hd, gain, nflag, head_major, name):
    rows = src.shape[0]
    bm = _tile(rows, 2048 if bw <= 256 else 1024, 16)
    bd = _block_diag(hd)
    cb0 = c0 // bw

    def body(x_ref, g_ref, f_ref, bd_ref, o_ref):
        xv = x_ref[...].astype(F32)
        ss = _seg_sum(xv * xv, bd_ref[...])
        rstd = lax.rsqrt(ss * (1.0 / hd) + EPS)
        y = (xv * jnp.where(f_ref[...] > 0.0, rstd, 1.0) * g_ref[...]).astype(BF16)
        if head_major:
            for h in range(bw // HEAD_DIM):
                o_ref[h] = y[:, h * HEAD_DIM:(h + 1) * HEAD_DIM]
        else:
            o_ref[...] = y

    vec_spec = pl.BlockSpec((1, bw), lambda i, t: (0, t))
    if head_major:
        hpb = bw // HEAD_DIM
        out_spec = pl.BlockSpec((hpb, bm, HEAD_DIM), lambda i, t: (t, i, 0))
        out_shape = jax.ShapeDtypeStruct((width // HEAD_DIM, rows, HEAD_DIM), BF16)
    else:
        out_spec = pl.BlockSpec((bm, bw), lambda i, t: (i, t))
        out_shape = jax.ShapeDtypeStruct((rows, width), BF16)
    return pl.pallas_call(
        body, name=name, grid=(rows // bm, width // bw),
        in_specs=[pl.BlockSpec((bm, bw), lambda i, t: (i, cb0 + t)), vec_spec, vec_spec,
                  pl.BlockSpec((LANES, LANES), lambda i, t: (0, 0))],
        out_specs=out_spec, out_shape=out_shape,
        compiler_params=_params(("parallel", "parallel")),
    )(src, gain, nflag, bd)


def _headnorm_bwd(src, c0, width, bw, hd, gain, nflag, dyn, target, t0, name):
    rows = src.shape[0]
    bm = _tile(rows, 2048 if bw <= 256 else 1024, 16)
    bd = _block_diag(hd)
    cb0 = c0 // bw
    tb0 = t0 // bw
    aliased = target is not None

    def body(*refs):
        if aliased:
            x_ref, dy_ref, g_ref, f_ref, bd_ref, _, o_ref, dg_ref = refs
        else:
            x_ref, dy_ref, g_ref, f_ref, bd_ref, o_ref, dg_ref = refs
        i = pl.program_id(1)
        xv = x_ref[...].astype(F32)
        dyv = dy_ref[...]
        bdv = bd_ref[...]
        rstd = lax.rsqrt(_seg_sum(xv * xv, bdv) * (1.0 / hd) + EPS)
        xhat = xv * rstd
        g = dyv * g_ref[...]
        mean = _seg_sum(g * xhat, bdv) * (1.0 / hd)
        dx = jnp.where(f_ref[...] > 0.0, rstd * (g - xhat * mean), g)
        o_ref[...] = dx.astype(BF16)
        part = jnp.sum((dyv * xhat).reshape(bm // 8, 8, bw), axis=0)

        @pl.when(i == 0)
        def _():
            dg_ref[...] = part

        @pl.when(i > 0)
        def _():
            dg_ref[...] += part

    vec_spec = pl.BlockSpec((1, bw), lambda t, i: (0, t))
    in_specs = [pl.BlockSpec((bm, bw), lambda t, i: (i, cb0 + t)), pl.BlockSpec((bm, bw), lambda t, i: (i, t)),
                vec_spec, vec_spec, pl.BlockSpec((LANES, LANES), lambda t, i: (0, 0))]
    args = [src, dyn, gain, nflag, bd]
    aliases = {}
    if aliased:
        in_specs.append(pl.BlockSpec(memory_space=pl.ANY))
        args.append(target)
        aliases = {5: 0}
        o_shape = jax.ShapeDtypeStruct(target.shape, BF16)
    else:
        o_shape = jax.ShapeDtypeStruct((rows, width), BF16)
    out, dg = pl.pallas_call(
        body, name=name, grid=(width // bw, rows // bm), in_specs=in_specs,
        out_specs=[pl.BlockSpec((bm, bw), lambda t, i: (i, tb0 + t)), pl.BlockSpec((8, bw), lambda t, i: (0, t))],
        out_shape=[o_shape, jax.ShapeDtypeStruct((8, width), F32)],
        input_output_aliases=aliases,
        compiler_params=_params(("parallel", "arbitrary")),
    )(*args)
    return out, dg


def _fox_prep(pfb, bpad, name):
    s = pfb.shape[0]

    def body(p_ref, b_ref, c_ref):
        z = p_ref[...] + b_ref[...]
        logf = jnp.minimum(z, 0.0) - jnp.log(1.0 + jnp.exp(-jnp.abs(z)))
        x = logf.T[0:16, :]
        lane = lax.broadcasted_iota(jnp.int32, (16, s), 1)
        sh = 1
        while sh < s:
            x = x + jnp.where(lane >= sh, pltpu.roll(x, sh, 1), 0.0)
            sh *= 2
        c_ref[...] = x

    return pl.pallas_call(
        body, name=name, grid=(1,),
        in_specs=[pl.BlockSpec((s, FB_PAD), lambda i: (0, 0)), pl.BlockSpec((1, FB_PAD), lambda i: (0, 0))],
        out_specs=pl.BlockSpec((16, s), lambda i: (0, 0)),
        out_shape=jax.ShapeDtypeStruct((16, s), F32),
        compiler_params=_params(("arbitrary",)),
    )(pfb, bpad)


def _fox_prep_bwd(pfb, bpad, dct, name):
    s = pfb.shape[0]

    def body(p_ref, b_ref, dc_ref, df_ref, db_ref):
        zt = (p_ref[...] + b_ref[...]).T[0:16, :]
        y = dc_ref[...]
        lane = lax.broadcasted_iota(jnp.int32, (16, s), 1)
        sh = 1
        while sh < s:
            y = y + jnp.where(lane < s - sh, pltpu.roll(y, s - sh, 1), 0.0)
            sh *= 2
        dz = y * _sigmoid(-zt)
        db_ref[...] = jnp.broadcast_to(jnp.sum(dz, axis=1, keepdims=True), (16, FB_PAD))
        full = jnp.concatenate([dz, jnp.zeros((FB_PAD - 16, s), F32)], axis=0)
        df_ref[...] = full.T.astype(BF16)

    return pl.pallas_call(
        body, name=name, grid=(1,),
        in_specs=[pl.BlockSpec((s, FB_PAD), lambda i: (0, 0)), pl.BlockSpec((1, FB_PAD), lambda i: (0, 0)),
                  pl.BlockSpec((16, s), lambda i: (0, 0))],
        out_specs=[pl.BlockSpec((s, FB_PAD), lambda i: (0, 0)), pl.BlockSpec((16, FB_PAD), lambda i: (0, 0))],
        out_shape=[jax.ShapeDtypeStruct((s, FB_PAD), BF16), jax.ShapeDtypeStruct((16, FB_PAD), F32)],
        compiler_params=_params(("arbitrary",)),
    )(pfb, bpad, dct)


def _swa_window(n):
    ws = pl.multiple_of(jnp.maximum(n * WINDOW - WINDOW, 0), WINDOW)
    qi = lax.broadcasted_iota(jnp.int32, (WINDOW, 2 * WINDOW), 0)
    kj = lax.broadcasted_iota(jnp.int32, (WINDOW, 2 * WINDOW), 1)
    rel = qi + (n * WINDOW - ws) - kj
    valid = (rel >= 0) & (rel < WINDOW)
    return ws, valid, rel.astype(F32)


def _attn_a_fwd(qkv, sinks, slopes, name):
    s = qkv.shape[1]
    nb = s // WINDOW
    smem = pl.BlockSpec(memory_space=pltpu.SMEM)

    def body(sink_ref, slope_ref, q_ref, k_ref, v_ref, o_ref, lse_ref):
        n = pl.program_id(0)
        ws, valid, relf = _swa_window(n)
        outs = []
        for h in range(A_Q_HEADS):
            kvh = h // A_GROUP
            kw = k_ref[kvh, pl.ds(ws, 2 * WINDOW), :]
            vw = v_ref[kvh, pl.ds(ws, 2 * WINDOW), :]
            sc = lax.dot_general(q_ref[h], kw, (((1,), (1,)), ((), ())), preferred_element_type=F32)
            sc = jnp.where(valid, sc - slope_ref[h] * relf, NEG)
            sink = sink_ref[h]
            m = jnp.maximum(jnp.max(sc, axis=1, keepdims=True), sink)
            p = jnp.exp(sc - m)
            denom = jnp.sum(p, axis=1, keepdims=True) + jnp.exp(sink - m)
            pn = (p / denom).astype(BF16)
            outs.append(jnp.dot(pn, vw, preferred_element_type=F32))
            lse_ref[h] = jnp.broadcast_to(m + jnp.log(denom), (WINDOW, HEAD_DIM))
        o_ref[...] = jnp.concatenate(outs, axis=1)

    return pl.pallas_call(
        body, name=name, grid=(nb,),
        in_specs=[smem, smem,
                  pl.BlockSpec((A_Q_HEADS, WINDOW, HEAD_DIM), lambda n: (0, n, 0)),
                  pl.BlockSpec((A_KV_HEADS, s, HEAD_DIM), lambda n: (A_GROUP, 0, 0)),
                  pl.BlockSpec((A_KV_HEADS, s, HEAD_DIM), lambda n: (A_GROUP + 1, 0, 0))],
        out_specs=[pl.BlockSpec((WINDOW, A_WIDTH), lambda n: (n, 0)),
                   pl.BlockSpec((A_Q_HEADS, WINDOW, HEAD_DIM), lambda n: (0, n, 0))],
        out_shape=[jax.ShapeDtypeStruct((s, A_WIDTH), F32), jax.ShapeDtypeStruct((A_Q_HEADS, s, HEAD_DIM), F32)],
        compiler_params=_params(("parallel",), VMEM_BIG),
    )(sinks, slopes, qkv, qkv, qkv)


def _attn_a_bwd(qkv, do, lse, dd, sinks, slopes, name, comm=None):
    s = qkv.shape[1]
    nb = s // WINDOW
    smem = pl.BlockSpec(memory_space=pltpu.SMEM)
    last = nb - 1

    def body(sink_ref, slope_ref, q_ref, k_ref, v_ref, do_ref, lse_ref, dd_ref, dq_ref, dkv_ref, ds_ref, carry):
        n = pl.program_id(0)

        @pl.when(n == 0)
        def _():
            carry[...] = jnp.zeros(carry.shape, F32)
            ds_ref[...] = jnp.zeros(ds_ref.shape, F32)

        @pl.when(n < nb)
        def _():
            ws, valid, relf = _swa_window(n)
            dqs = []
            dkw = [None] * A_KV_HEADS
            dvw = [None] * A_KV_HEADS
            for h in range(A_Q_HEADS):
                kvh = h // A_GROUP
                qh = q_ref[h]
                doh = do_ref[h]
                kw = k_ref[kvh, pl.ds(ws, 2 * WINDOW), :]
                vw = v_ref[kvh, pl.ds(ws, 2 * WINDOW), :]
                lse_h = lse_ref[h]
                dd_h = dd_ref[h]
                sc = lax.dot_general(qh, kw, (((1,), (1,)), ((), ())), preferred_element_type=F32)
                sc = jnp.where(valid, sc - slope_ref[h] * relf, NEG)
                p = jnp.exp(sc - lse_h[:, 0:1])
                dp = lax.dot_general(doh, vw, (((1,), (1,)), ((), ())), preferred_element_type=F32)
                dsc = (p * (dp - dd_h[:, 0:1])).astype(BF16)
                pb = p.astype(BF16)
                dqs.append(jnp.dot(dsc, kw, preferred_element_type=F32))
                dk_h = jnp.dot(qh.T, dsc, preferred_element_type=F32)
                dv_h = jnp.dot(doh.T, pb, preferred_element_type=F32)
                dkw[kvh] = dk_h if dkw[kvh] is None else dkw[kvh] + dk_h
                dvw[kvh] = dv_h if dvw[kvh] is None else dvw[kvh] + dv_h
                psink = jnp.exp(sink_ref[h] - lse_h)
                ds_ref[h] += jnp.sum((-psink * dd_h).reshape(WINDOW // 8, 8, HEAD_DIM), axis=0)
            dq_ref[...] = jnp.concatenate(dqs, axis=1)
            win = jnp.concatenate(dkw + dvw, axis=0)
            first = win[:, 0:WINDOW]
            second = win[:, WINDOW:2 * WINDOW]
            dkv_ref[...] = (carry[...] + first).T
            carry[...] = jnp.where(n == 0, first, second)

        @pl.when(n == nb)
        def _():
            dkv_ref[...] = carry[...].T

    hm = lambda heads: pl.BlockSpec((heads, WINDOW, HEAD_DIM), lambda n: (0, jnp.minimum(n, last), 0))
    res = lambda blk: pl.BlockSpec((A_KV_HEADS, s, HEAD_DIM), lambda n: (blk, 0, 0))
    outs, comm_outs = _hosted_call(
        body, comm, name=name, grid=(nb + 1,),
        in_specs=[smem, smem, hm(A_Q_HEADS), res(A_GROUP), res(A_GROUP + 1), hm(A_Q_HEADS), hm(A_Q_HEADS), hm(A_Q_HEADS)],
        out_specs=[pl.BlockSpec((WINDOW, A_WIDTH), lambda n: (jnp.minimum(n, last), 0)),
                   pl.BlockSpec((WINDOW, 2 * A_KV_WIDTH), lambda n: (jnp.maximum(n - 1, 0), 0)),
                   pl.BlockSpec((A_Q_HEADS, 8, HEAD_DIM), lambda n: (0, 0, 0))],
        out_shape=[jax.ShapeDtypeStruct((s, A_WIDTH), F32), jax.ShapeDtypeStruct((s, 2 * A_KV_WIDTH), F32),
                   jax.ShapeDtypeStruct((A_Q_HEADS, 8, HEAD_DIM), F32)],
        scratch_shapes=[pltpu.VMEM((2 * A_KV_WIDTH, WINDOW), F32)],
        args=[sinks, slopes, qkv, qkv, qkv, do, lse, dd], sem=("arbitrary",), vmem=VMEM_BIG)
    return outs[0], outs[1], outs[2], comm_outs


def _attn_b_fwd(qkv, c3, name, comm=None):
    heads, s = qkv.shape[0] // 3, qkv.shape[1]
    hpairs = heads // 2
    bq = min(512, s)
    nq = s // bq
    nt = (((1,), (1,)), ((), ()))

    def body(q_ref, k_ref, v_ref, c_ref, o_ref, lse_ref, m_scr, l_scr, acc_scr):
        i = pl.program_id(1)
        r0 = pl.multiple_of(i * bq, bq)
        row = lax.broadcasted_iota(jnp.int32, (bq, bq), 0)
        col = lax.broadcasted_iota(jnp.int32, (bq, bq), 1)
        m_scr[...] = jnp.full((2, bq, LANES), NEG, F32)
        l_scr[...] = jnp.zeros((2, bq, LANES), F32)
        acc_scr[...] = jnp.zeros((2, bq, HEAD_DIM), F32)

        def step(j, masked):
            k0 = pl.multiple_of(j * bq, bq)
            for h2 in range(2):
                kv = k_ref[h2, pl.ds(k0, bq), :]
                vv = v_ref[h2, pl.ds(k0, bq), :]
                cq0 = c_ref[h2, :, pl.ds(r0, LANES)][:, 0:1]
                sc = lax.dot_general(q_ref[h2], kv, nt, preferred_element_type=F32)
                sc = sc + (cq0 - c_ref[h2, :, pl.ds(k0, bq)])
                if masked:
                    sc = jnp.where(col <= row, sc, NEG)
                m_prev = m_scr[h2]
                m_new = jnp.maximum(m_prev, jnp.max(sc, axis=1, keepdims=True))
                alpha = jnp.exp(m_prev - m_new)
                p = jnp.exp(sc - m_new[:, 0:1])
                l_scr[h2] = alpha * l_scr[h2] + jnp.sum(p, axis=1, keepdims=True)
                p_hi = p.astype(BF16)
                p_lo = (p - p_hi.astype(F32)).astype(BF16)
                pv = jnp.dot(p_hi, vv, preferred_element_type=F32) + jnp.dot(p_lo, vv, preferred_element_type=F32)
                acc_scr[h2] = acc_scr[h2] * alpha[:, 0:HEAD_DIM] + pv
                m_scr[h2] = m_new

        def loop_body(j, carry):
            step(j, False)
            return carry

        lax.fori_loop(0, i, loop_body, 0)
        step(i, True)
        outs = []
        for h2 in range(2):
            l = l_scr[h2]
            outs.append(acc_scr[h2] / l[:, 0:HEAD_DIM])
            lse_ref[h2] = (m_scr[h2] + jnp.log(l))[:, 0:HEAD_DIM]
        o_ref[...] = jnp.concatenate(outs, axis=1)

    res = lambda off: pl.BlockSpec((2, s, HEAD_DIM), lambda hp, i: (off + hp, 0, 0))
    outs, comm_outs = _hosted_call(
        body, comm, name=name, grid=(hpairs, nq),
        in_specs=[pl.BlockSpec((2, bq, HEAD_DIM), lambda hp, i: (hp, i, 0)), res(hpairs), res(2 * hpairs),
                  pl.BlockSpec((2, 1, s), lambda hp, i: (hp, 0, 0))],
        out_specs=[pl.BlockSpec((bq, 2 * HEAD_DIM), lambda hp, i: (i, hp)),
                   pl.BlockSpec((2, bq, HEAD_DIM), lambda hp, i: (hp, i, 0))],
        out_shape=[jax.ShapeDtypeStruct((s, heads * HEAD_DIM), F32), jax.ShapeDtypeStruct((heads, s, HEAD_DIM), F32)],
        scratch_shapes=[pltpu.VMEM((2, bq, LANES), F32), pltpu.VMEM((2, bq, LANES), F32), pltpu.VMEM((2, bq, HEAD_DIM), F32)],
        args=[qkv, qkv, qkv, c3], sem=("parallel", "parallel"), vmem=VMEM_BIG)
    return outs[0], outs[1], comm_outs


def _attn_b_bwd(qkv, do, lse, dd, c3, name, comm=None):
    heads, s = qkv.shape[0] // 3, qkv.shape[1]
    hpairs = heads // 2
    bq = min(512, s)
    nq = s // bq
    nt = (((1,), (1,)), ((), ()))
    tn = (((0,), (0,)), ((), ()))
    grid = (heads // 2, nq)

    def body(q_ref, k_ref, v_ref, do_ref, lse_ref, dd_ref, c_ref, dq_ref, dk_ref, dv_ref, dc_ref,
             dq_scr, dk_scr, dv_scr, dc_scr):
        j = pl.program_id(1)
        k0 = pl.multiple_of(j * bq, bq)
        row = lax.broadcasted_iota(jnp.int32, (bq, bq), 0)
        col = lax.broadcasted_iota(jnp.int32, (bq, bq), 1)

        @pl.when(j == 0)
        def _():
            dq_scr[...] = jnp.zeros(dq_scr.shape, F32)

        dk_scr[...] = jnp.zeros((2, HEAD_DIM, bq), F32)
        dv_scr[...] = jnp.zeros((2, HEAD_DIM, bq), F32)
        dc_scr[...] = jnp.zeros((2, 1, bq), F32)
        k_t = [k_ref[h2].T for h2 in range(2)]

        def step(i, masked):
            r0 = pl.multiple_of(i * bq, bq)
            for h2 in range(2):
                kv = k_ref[h2]
                vv = v_ref[h2]
                qv = q_ref[h2, pl.ds(r0, bq), :]
                dov = do_ref[h2, pl.ds(r0, bq), :]
                lse_v = lse_ref[h2, pl.ds(r0, bq), :][:, 0:1]
                dd_v = dd_ref[h2, pl.ds(r0, bq), :][:, 0:1]
                cq0 = c_ref[h2, :, pl.ds(r0, LANES)][:, 0:1]
                sc = lax.dot_general(qv, kv, nt, preferred_element_type=F32) + (cq0 - c_ref[h2, :, pl.ds(k0, bq)])
                if masked:
                    sc = jnp.where(col <= row, sc, NEG)
                p = jnp.exp(sc - lse_v)
                dp = lax.dot_general(dov, vv, nt, preferred_element_type=F32)
                dsc = p * (dp - dd_v)
                dsb = dsc.astype(BF16)
                dv_scr[h2] += jnp.dot(dov.T, p.astype(BF16), preferred_element_type=F32)
                dk_scr[h2] += jnp.dot(qv.T, dsb, preferred_element_type=F32)
                dq_scr[h2, :, pl.ds(r0, bq)] += jnp.dot(k_t[h2], dsb.T, preferred_element_type=F32)
                dc_scr[h2] -= jnp.sum(dsc, axis=0, keepdims=True)

        def loop_body(i, carry):
            step(i, False)
            return carry

        step(j, True)
        lax.fori_loop(j + 1, nq, loop_body, 0)
        dc_ref[...] = dc_scr[...]
        dk_ref[...] = jnp.concatenate([dk_scr[0].T, dk_scr[1].T], axis=1)
        dv_ref[...] = jnp.concatenate([dv_scr[0].T, dv_scr[1].T], axis=1)

        @pl.when(j == nq - 1)
        def _():
            dq_ref[...] = jnp.concatenate([dq_scr[0].T, dq_scr[1].T], axis=1)

    res = pl.BlockSpec((2, s, HEAD_DIM), lambda hp, j: (hp, 0, 0))
    blk = lambda off: pl.BlockSpec((2, bq, HEAD_DIM), lambda hp, j: (off + hp, j, 0))
    tm = jax.ShapeDtypeStruct((s, heads * HEAD_DIM), F32)
    in_specs = [res, blk(hpairs), blk(2 * hpairs), res, res, res, pl.BlockSpec((2, 1, s), lambda hp, j: (hp, 0, 0))]
    out_specs = [pl.BlockSpec((s, 2 * HEAD_DIM), lambda hp, j: (0, hp)),
                 pl.BlockSpec((bq, 2 * HEAD_DIM), lambda hp, j: (j, hp)),
                 pl.BlockSpec((bq, 2 * HEAD_DIM), lambda hp, j: (j, hp)),
                 pl.BlockSpec((2, 1, bq), lambda hp, j: (hp, 0, j))]
    out_shape = [tm, tm, tm, jax.ShapeDtypeStruct((heads, 1, s), F32)]
    scratch = [pltpu.VMEM((2, HEAD_DIM, s), F32), pltpu.VMEM((2, HEAD_DIM, bq), F32),
               pltpu.VMEM((2, HEAD_DIM, bq), F32), pltpu.VMEM((2, 1, bq), F32)]
    outs, comm_outs = _hosted_call(
        body, comm, name=name, grid=grid, in_specs=in_specs, out_specs=out_specs, out_shape=out_shape,
        scratch_shapes=scratch, args=[qkv, qkv, qkv, do, lse, dd, c3], sem=("parallel", "arbitrary"), vmem=VMEM_BIG)
    return outs[0], outs[1], outs[2], outs[3], comm_outs


def _attn_c_probs(qh, mkh):
    sc = lax.dot_general(qh, mkh, (((1,), (1,)), ((), ())), preferred_element_type=F32) * (C_HEAD_DIM ** -0.5)
    p = jnp.exp(sc - jnp.max(sc, axis=1, keepdims=True))
    return p / jnp.sum(p, axis=1, keepdims=True)


def _attn_c_fwd(q, mkv, name):
    s = q.shape[0]
    m = mkv.shape[0]
    bq = _tile(s, 512, 8)

    def body(q_ref, mk_ref, mv_ref, o_ref):
        outs = []
        for h in range(C_HEADS):
            sl = slice(h * C_HEAD_DIM, (h + 1) * C_HEAD_DIM)
            pn = _attn_c_probs(q_ref[:, sl], mk_ref[:, sl]).astype(BF16)
            outs.append(jnp.dot(pn, mv_ref[:, sl], preferred_element_type=F32))
        o_ref[...] = jnp.concatenate(outs, axis=1)

    return pl.pallas_call(
        body, name=name, grid=(s // bq,),
        in_specs=[pl.BlockSpec((bq, C_WIDTH), lambda i: (i, 0)), pl.BlockSpec((m, C_WIDTH), lambda i: (0, 0)),
                  pl.BlockSpec((m, C_WIDTH), lambda i: (0, 1))],
        out_specs=pl.BlockSpec((bq, C_WIDTH), lambda i: (i, 0)),
        out_shape=jax.ShapeDtypeStruct((s, C_WIDTH), F32),
        compiler_params=_params(("parallel",)),
    )(q, mkv, mkv)


def _attn_c_bwd(q, mkv, do, name):
    s = q.shape[0]
    m = mkv.shape[0]
    bq = _tile(s, 512, 8)
    tn = (((0,), (0,)), ((), ()))

    def body(q_ref, mk_ref, mv_ref, do_ref, dq_ref, dm_ref):
        i = pl.program_id(0)

        @pl.when(i == 0)
        def _():
            dm_ref[...] = jnp.zeros(dm_ref.shape, F32)

        dqs = []
        for h in range(C_HEADS):
            sl = slice(h * C_HEAD_DIM, (h + 1) * C_HEAD_DIM)
            qh, mkh, mvh, doh = q_ref[:, sl], mk_ref[:, sl], mv_ref[:, sl], do_ref[:, sl]
            pn = _attn_c_probs(qh, mkh)
            dp = lax.dot_general(doh, mvh, (((1,), (1,)), ((), ())), preferred_element_type=F32)
            dsc = (pn * (dp - jnp.sum(pn * dp, axis=1, keepdims=True)) * (C_HEAD_DIM ** -0.5)).astype(BF16)
            dqs.append(jnp.dot(dsc, mkh, preferred_element_type=F32))
            dm_ref[:, sl] += lax.dot_general(dsc, qh, tn, preferred_element_type=F32)
            sv = slice(C_WIDTH + h * C_HEAD_DIM, C_WIDTH + (h + 1) * C_HEAD_DIM)
            dm_ref[:, sv] += lax.dot_general(pn.astype(BF16), doh, tn, preferred_element_type=F32)
        dq_ref[...] = jnp.concatenate(dqs, axis=1)

    row = pl.BlockSpec((bq, C_WIDTH), lambda i: (i, 0))
    return pl.pallas_call(
        body, name=name, grid=(s // bq,),
        in_specs=[row, pl.BlockSpec((m, C_WIDTH), lambda i: (0, 0)), pl.BlockSpec((m, C_WIDTH), lambda i: (0, 1)), row],
        out_specs=[row, pl.BlockSpec((m, 2 * C_WIDTH), lambda i: (0, 0))],
        out_shape=[jax.ShapeDtypeStruct((s, C_WIDTH), F32), jax.ShapeDtypeStruct((m, 2 * C_WIDTH), F32)],
        compiler_params=_params(("arbitrary",)),
    )(q, mkv, mkv, do)


def _gate_fwd(y, proj, zc0, bw, name):
    rows, width = y.shape
    bm = _tile(rows, 2048 if bw <= 256 else 1024, 16)
    cb0 = zc0 // bw

    def body(y_ref, z_ref, o_ref):
        z = z_ref[...].astype(F32)
        o_ref[...] = (y_ref[...] * (z * _sigmoid(z))).astype(BF16)

    return pl.pallas_call(
        body, name=name, grid=(rows // bm, width // bw),
        in_specs=[pl.BlockSpec((bm, bw), lambda i, t: (i, t)), pl.BlockSpec((bm, bw), lambda i, t: (i, cb0 + t))],
        out_specs=pl.BlockSpec((bm, bw), lambda i, t: (i, t)),
        out_shape=jax.ShapeDtypeStruct((rows, width), BF16),
        compiler_params=_params(("parallel", "parallel")),
    )(y, proj)


def _gate_bwd(dsv, y, proj, zc0, bw, dproj, t0, head_major, name):
    rows, width = y.shape
    bm = _tile(rows, 2048 if bw <= 256 else 1024, 16)
    cb0 = zc0 // bw
    tb0 = t0 // bw
    bd = _block_diag(HEAD_DIM)
    hpb = bw // HEAD_DIM

    def body(*refs):
        if head_major:
            ds_ref, y_ref, z_ref, bd_ref, _, dp_ref, dy_ref, dd_ref = refs
        else:
            ds_ref, y_ref, z_ref, _, dp_ref, dy_ref = refs
        z = z_ref[...].astype(F32)
        sig = _sigmoid(z)
        dsx = ds_ref[...]
        yv = y_ref[...]
        dy = dsx * (z * sig)
        dp_ref[...] = (dsx * yv * (sig * (1.0 + z * (1.0 - sig)))).astype(BF16)
        if head_major:
            dyb = dy.astype(BF16)
            dd = _seg_sum(dyb.astype(F32) * yv, bd_ref[...])
            for h in range(hpb):
                sl = slice(h * HEAD_DIM, (h + 1) * HEAD_DIM)
                dy_ref[h] = dyb[:, sl]
                dd_ref[h] = dd[:, sl]
        else:
            dy_ref[...] = dy.astype(BF16)

    tile = pl.BlockSpec((bm, bw), lambda i, t: (i, t))
    ztile = pl.BlockSpec((bm, bw), lambda i, t: (i, cb0 + t))
    ttile = pl.BlockSpec((bm, bw), lambda i, t: (i, tb0 + t))
    any_spec = pl.BlockSpec(memory_space=pl.ANY)
    dp_shape = jax.ShapeDtypeStruct(dproj.shape, BF16)
    if head_major:
        hm_spec = pl.BlockSpec((hpb, bm, HEAD_DIM), lambda i, t: (t, i, 0))
        nh = width // HEAD_DIM
        outs = pl.pallas_call(
            body, name=name, grid=(rows // bm, width // bw),
            in_specs=[tile, tile, ztile, pl.BlockSpec((LANES, LANES), lambda i, t: (0, 0)), any_spec],
            out_specs=[ttile, hm_spec, hm_spec],
            out_shape=[dp_shape, jax.ShapeDtypeStruct((nh, rows, HEAD_DIM), BF16),
                       jax.ShapeDtypeStruct((nh, rows, HEAD_DIM), F32)],
            input_output_aliases={4: 0},
            compiler_params=_params(("parallel", "parallel")),
        )(dsv, y, proj, bd, dproj)
        return outs[0], outs[1], outs[2]
    outs = pl.pallas_call(
        body, name=name, grid=(rows // bm, width // bw),
        in_specs=[tile, tile, ztile, any_spec],
        out_specs=[ttile, tile],
        out_shape=[dp_shape, jax.ShapeDtypeStruct((rows, width), BF16)],
        input_output_aliases={3: 0},
        compiler_params=_params(("parallel", "parallel")),
    )(dsv, y, proj, dproj)
    return outs[0], outs[1], None


def _merge_fwd(proj, ua, ub, uc, name):
    rows, d = ua.shape
    bm = _tile(rows, 1024, 16)
    bw = _tile(d, 512)
    g0 = COL_GATE // bw
    gstep = d // bw

    def body(la_ref, lb_ref, lc_ref, ua_ref, ub_ref, uc_ref, o_ref, ga_ref, gb_ref, gc_ref):
        y = None
        for l_ref, u_ref, g_ref in ((la_ref, ua_ref, ga_ref), (lb_ref, ub_ref, gb_ref), (lc_ref, uc_ref, gc_ref)):
            g = _sigmoid(l_ref[...].astype(F32))
            g_ref[...] = g.astype(BF16)
            term = g * u_ref[...].astype(F32)
            y = term if y is None else y + term
        o_ref[...] = y.astype(BF16)

    tile = pl.BlockSpec((bm, bw), lambda i, t: (i, t))
    gate = lambda b: pl.BlockSpec((bm, bw), lambda i, t: (i, g0 + b * gstep + t))
    shape = jax.ShapeDtypeStruct((rows, d), BF16)
    return pl.pallas_call(
        body, name=name, grid=(rows // bm, d // bw),
        in_specs=[gate(0), gate(1), gate(2), tile, tile, tile],
        out_specs=[tile] * 4, out_shape=[shape] * 4,
        compiler_params=_params(("parallel", "parallel")),
    )(proj, proj, proj, ua, ub, uc)


def _merge_bwd(dym, us, gs, name):
    rows, d = dym.shape
    bm = _tile(rows, 256, 16)

    def body(dy_ref, ua_ref, ub_ref, uc_ref, ga_ref, gb_ref, gc_ref, dg_ref, da_ref, db_ref, dc_ref):
        dyv = dy_ref[...]
        for b, (u_ref, g_ref, du_ref) in enumerate(((ua_ref, ga_ref, da_ref), (ub_ref, gb_ref, db_ref), (uc_ref, gc_ref, dc_ref))):
            g = g_ref[...].astype(F32)
            du_ref[...] = (g * dyv).astype(BF16)
            dg_ref[:, b * d:(b + 1) * d] = (dyv * u_ref[...].astype(F32) * g * (1.0 - g)).astype(BF16)

    tile = pl.BlockSpec((bm, d), lambda i: (i, 0))
    shape = jax.ShapeDtypeStruct((rows, d), BF16)
    outs = pl.pallas_call(
        body, name=name, grid=(rows // bm,),
        in_specs=[tile] * 7,
        out_specs=[pl.BlockSpec((bm, 3 * d), lambda i: (i, 0)), tile, tile, tile],
        out_shape=[jax.ShapeDtypeStruct((rows, 3 * d), BF16), shape, shape, shape],
        compiler_params=_params(("parallel",), VMEM_BIG),
    )(dym, *us, *gs)
    return outs[0], outs[1], outs[2], outs[3]


def _out_proj_loss(ym, wo, x, target, name):
    m, d = x.shape
    bm, bn = _tile(m, 1024, 16), _tile(d, 1024)
    grid = (m // bm, d // bn)

    def body(a_ref, b_ref, x_ref, t_ref, dy_ref, dyb_ref, l_ref):
        first, _ = _grid_edges(grid)
        y = jnp.dot(a_ref[...], b_ref[...], preferred_element_type=F32) + x_ref[...]
        diff = y - t_ref[...]
        dy = diff * (1.0 / d)
        dy_ref[...] = dy
        dyb_ref[...] = dy.astype(BF16)
        sq = diff * diff
        part = sq[:, 0:LANES]
        for c in range(1, bn // LANES):
            part = part + sq[:, c * LANES:(c + 1) * LANES]
        part = jnp.sum(part.reshape(bm // 8, 8, LANES), axis=0)

        @pl.when(first)
        def _():
            l_ref[...] = part

        @pl.when(jnp.logical_not(first))
        def _():
            l_ref[...] += part

    tile = pl.BlockSpec((bm, bn), lambda i, j: (i, j))
    return pl.pallas_call(
        body, name=name, grid=grid,
        in_specs=[pl.BlockSpec((bm, d), lambda i, j: (i, 0)), pl.BlockSpec((d, bn), lambda i, j: (0, j)), tile, tile],
        out_specs=[tile, tile, pl.BlockSpec((8, LANES), lambda i, j: (0, 0))],
        out_shape=[jax.ShapeDtypeStruct((m, d), F32), jax.ShapeDtypeStruct((m, d), BF16),
                   jax.ShapeDtypeStruct((8, LANES), F32)],
        compiler_params=_params(("arbitrary", "arbitrary"), VMEM_BIG),
    )(ym, wo, x, target)


def _row(vec, reps=1):
    return jnp.tile(vec.reshape(1, -1).astype(F32), (1, reps))


def _local_step(x, mem, target, small, wg, shards=None):
    s, d = x.shape
    dist = shards is not None
    wg = dict(wg)
    ones = lambda n: jnp.ones((1, n), F32)
    zeros = lambda n: jnp.zeros((1, n), F32)
    scale_ab = HEAD_DIM ** -0.5
    split8 = lambda g: g.reshape(N_DEV, g.shape[0] // N_DEV, g.shape[1])
    flat8 = lambda g: g.reshape(g.shape[0] * g.shape[1], g.shape[2])
    gather = lambda names: _Comm("gather", [shards[n] for n in names]) if dist else None
    g = {}

    def scatter(names):
        return _Comm("scatter", [split8(g[n]) for n in names]) if dist else None

    def hosted(result, names, store):
        if not dist:
            return result
        out, got = result
        store.update(zip(names, got))
        return out

    hn = _rmsnorm_fwd(x, small["norm_gain"], "rms_x_fwd")
    got = {}
    proj = hosted(_mm_nn(hn, wg["qkv"], bm=1024, bn=1024, bk=d, o_dtype=BF16, name="proj_qkv",
                         comm=gather(("wa", "wb"))), ("wa", "wb"), got)
    wg.update({n: flat8(a) for n, a in got.items()})
    pfb = _mm_nn(hn, wg["wf"], bm=1024, bn=FB_PAD, bk=d, o_dtype=F32, name="proj_fb")
    mn = _rmsnorm_fwd(mem, small["mem_norm_gain"], "rms_mem_fwd")
    mkv = _mm_nn(mn, wg["wk"], bm=256, bn=1024, bk=d, o_dtype=F32, name="mem_kv")

    gain_a = jnp.concatenate([_row(small["q_gain_a"], A_Q_HEADS) * scale_ab, _row(small["k_gain_a"], A_KV_HEADS), ones(A_KV_WIDTH)], axis=1)
    flag_a = jnp.concatenate([ones(A_WIDTH + A_KV_WIDTH), zeros(A_KV_WIDTH)], axis=1)
    qkv_a = _headnorm_fwd(proj, COL_QA, 1280, 1280, HEAD_DIM, gain_a, flag_a, True, "hn_a_fwd")
    gain_b = jnp.concatenate([_row(small["q_gain_b"], B_HEADS) * scale_ab, _row(small["k_gain_b"], B_HEADS), ones(B_WIDTH)], axis=1)
    flag_b = jnp.concatenate([ones(2 * B_WIDTH), zeros(B_WIDTH)], axis=1)
    qkv_b = _headnorm_fwd(proj, COL_QB, 2304, 256, HEAD_DIM, gain_b, flag_b, True, "hn_b_fwd")
    gain_cq = _row(small["q_gain_c"], C_HEADS)
    q_c = _headnorm_fwd(proj, COL_QC, C_WIDTH, C_WIDTH, C_HEAD_DIM, gain_cq, ones(C_WIDTH), False, "hn_cq_fwd")
    gain_ck = jnp.concatenate([_row(small["k_gain_c"], C_HEADS), ones(C_WIDTH)], axis=1)
    flag_ck = jnp.concatenate([ones(C_WIDTH), zeros(C_WIDTH)], axis=1)
    mkvn = _headnorm_fwd(mkv, 0, 2 * C_WIDTH, 2 * C_WIDTH, C_HEAD_DIM, gain_ck, flag_ck, False, "hn_ck_fwd")


    bpad = jnp.pad(small["b_forget"].reshape(1, -1), ((0, 0), (0, FB_PAD - B_HEADS)))
    c16 = _fox_prep(pfb, bpad, "fox_prep")
    c3 = c16[0:B_HEADS].reshape(B_HEADS, 1, s)

    sinks = small["sinks_a"].reshape(-1)
    slopes = jnp.exp2(-8.0 * jnp.arange(1, A_Q_HEADS + 1, dtype=F32) / A_Q_HEADS)
    y_a, lse_a = _attn_a_fwd(qkv_a, sinks, slopes, "attn_a_fwd")
    y_b, lse_b, got_zg = _attn_b_fwd(qkv_b, c3, "attn_b_fwd", comm=gather(("zg",)))
    if dist:
        wg["zg"] = flat8(got_zg[0])
    y_c = _attn_c_fwd(q_c, mkvn, "attn_c_fwd")

    got = {}
    pzg = hosted(_mm_nn(hn, wg["zg"], bm=1024, bn=1024, bk=d, o_dtype=BF16, name="proj_zg", comm=gather(("wo", "wc"))),
                 ("wo", "wc"), got)
    wg.update({n: flat8(a) for n, a in got.items()})

    s_a = _gate_fwd(y_a, pzg, COL_ZA, 256, "gate_a_fwd")
    s_b = _gate_fwd(y_b, pzg, COL_ZB, 256, "gate_b_fwd")
    s_c = _gate_fwd(y_c, pzg, COL_ZC, 512, "gate_c_fwd")
    w_a, w_b, w_c = _branch_full(wg["wa"]), _branch_full(wg["wb"]), _branch_full(wg["wc"])
    u_a = _mm_nn(s_a, w_a, bm=1024, bn=2048, bk=A_WIDTH, o_dtype=BF16, name="branch_a_fwd")
    u_b = _mm_nn(s_b, w_b, bm=1024, bn=2048, bk=B_WIDTH, o_dtype=BF16, name="branch_b_fwd")
    u_c = _mm_nn(s_c, w_c, bm=1024, bn=2048, bk=C_WIDTH, o_dtype=BF16, name="branch_c_fwd")
    ym, gate_a, gate_b, gate_c = _merge_fwd(pzg, u_a, u_b, u_c, "merge_fwd")
    dy, dyb, lpart = _out_proj_loss(ym, wg["wo"], x, target, "out_proj_loss")
    loss = 0.5 / d * jnp.sum(lpart)

    dym = _mm_nt(dyb, wg["wo"], bm=1024, bn=1024, bk=d, o_dtype=F32, name="out_proj_bwd_act")
    g["wo"] = _mm_tn(ym, dyb, bm=512, bn=1024, bk=s, o_dtype=BF16, name="out_proj_bwd_w")

    dgate, du_a, du_b, du_c = _merge_bwd(dym, (u_a, u_b, u_c), (gate_a, gate_b, gate_c), "merge_bwd")
    parts = {}
    g["wm_g"] = hosted(_mm_tn(hn, dgate, bm=512, bn=1024, bk=s, o_dtype=BF16, name="proj_gate_bwd_w",
                              comm=scatter(("wo",))), ("wo",), parts)

    ds_a = _mm_nt(du_a, w_a, bm=1024, bn=A_WIDTH, bk=d, o_dtype=F32, name="branch_a_bwd_act")
    ds_b = _mm_nt(du_b, w_b, bm=1024, bn=B_WIDTH, bk=d, o_dtype=F32, name="branch_b_bwd_act")
    ds_c = _mm_nt(du_c, w_c, bm=1024, bn=C_WIDTH, bk=d, o_dtype=F32, name="branch_c_bwd_act")
    g["wa"] = _branch_shards(_mm_tn(s_a, du_a, bm=A_WIDTH, bn=1024, bk=s, o_dtype=BF16, name="branch_a_bwd_w"))
    g["wb"] = _branch_shards(_mm_tn(s_b, du_b, bm=B_WIDTH, bn=1024, bk=s, o_dtype=BF16, name="branch_b_bwd_w"))
    g["wc"] = _branch_shards(_mm_tn(s_c, du_c, bm=C_WIDTH, bn=1024, bk=s, o_dtype=BF16, name="branch_c_bwd_w"))

    dz = lax.empty((s, W_Z), BF16)
    dz, do_a, dd_a = _gate_bwd(ds_a, y_a, pzg, COL_ZA, 256, dz, COL_ZA, True, "gate_a_bwd")
    dz, do_b, dd_b = _gate_bwd(ds_b, y_b, pzg, COL_ZB, 256, dz, COL_ZB, True, "gate_b_bwd")
    dz, do_c, _ = _gate_bwd(ds_c, y_c, pzg, COL_ZC, 512, dz, COL_ZC, False, "gate_c_bwd")
    g["wm_z"] = _mm_tn(hn, dz, bm=512, bn=1024, bk=s, o_dtype=BF16, name="proj_z_bwd_w")

    names = ("wa", "wb", "wc")
    dq_a, dkv_a, dsink, got = _attn_a_bwd(qkv_a, do_a, lse_a, dd_a, sinks, slopes, "attn_a_bwd", comm=scatter(names))
    parts.update(zip(names, got))
    names = ("wm_g", "wm_z")
    dq_b, dk_b, dv_b, dc3, got = _attn_b_bwd(qkv_b, do_b, lse_b, dd_b, c3, "attn_b_bwd", comm=scatter(names))
    parts.update(zip(names, got))
    dq_c, dmkvn = _attn_c_bwd(q_c, mkvn, do_c, "attn_c_bwd")

    dqkv = lax.empty((s, W_QKV), BF16)
    dqkv, dg_qa = _headnorm_bwd(proj, COL_QA, A_WIDTH, 256, HEAD_DIM, gain_a[:, 0:768], flag_a[:, 0:768], dq_a, dqkv, COL_QA, "hn_qa_bwd")
    dqkv, dg_kva = _headnorm_bwd(proj, COL_KA, 512, 256, HEAD_DIM, gain_a[:, 768:1280], flag_a[:, 768:1280], dkv_a, dqkv, COL_KA, "hn_kva_bwd")
    dqkv, dg_qb = _headnorm_bwd(proj, COL_QB, B_WIDTH, 256, HEAD_DIM, gain_b[:, 0:768], flag_b[:, 0:768], dq_b, dqkv, COL_QB, "hn_qb_bwd")
    dqkv, dg_kb = _headnorm_bwd(proj, COL_KB, B_WIDTH, 256, HEAD_DIM, gain_b[:, 768:1536], flag_b[:, 768:1536], dk_b, dqkv, COL_KB, "hn_kb_bwd")
    dqkv, _ = _headnorm_bwd(proj, COL_VB, B_WIDTH, 256, HEAD_DIM, gain_b[:, 1536:2304], flag_b[:, 1536:2304], dv_b, dqkv, COL_VB, "hn_vb_bwd")
    dqkv, dg_qc = _headnorm_bwd(proj, COL_QC, C_WIDTH, 512, C_HEAD_DIM, gain_cq, ones(C_WIDTH), dq_c, dqkv, COL_QC, "hn_qc_bwd")
    dmkv, dg_kc = _headnorm_bwd(mkv, 0, 2 * C_WIDTH, 2 * C_WIDTH, C_HEAD_DIM, gain_ck, flag_ck, dmkvn, None, 0, "hn_kc_bwd")

    dct = jnp.pad(dc3.reshape(B_HEADS, s), ((0, 16 - B_HEADS), (0, 0)))
    dfb, dbf = _fox_prep_bwd(pfb, bpad, dct, "fox_prep_bwd")

    dmn = _mm_nt(dmkv, wg["wk"], bm=256, bn=1024, bk=1024, o_dtype=F32, name="mem_kv_bwd_act")
    g["wk"] = _mm_tn(mn, dmkv, bm=512, bn=1024, bk=mem.shape[0], o_dtype=BF16, name="mem_kv_bwd_w")
    _, dg_mem = _rmsnorm_bwd(mem, dmn, small["mem_norm_gain"], None, "rms_mem_bwd")

    g["wm_qkv"] = _mm_tn(hn, dqkv, bm=512, bn=1024, bk=s, o_dtype=BF16, name="proj_qkv_bwd_w")
    g["wf"] = _mm_tn(hn, dfb, bm=512, bn=FB_PAD, bk=s, o_dtype=BF16, name="proj_fb_bwd_w")
    half = Q_SPLIT
    g["wm_q1"], g["wm_q2"] = g["wm_qkv"][:, 0:half], g["wm_qkv"][:, half:W_QKV]
    names = ("wm_q1",)
    dhn = hosted(_mm_nt_sum([(dqkv, wg["qkv"], 0), (dfb, wg["wf"], 0)], bm=1024, bn=1024, bk=2048,
                            name="proj_qkv_bwd_act", comm=scatter(names)), names, parts)
    names = ("wm_q2", "wf", "wk")
    dhn = hosted(_mm_nt_sum([(dz, wg["zg"], COL_ZA), (dgate, wg["zg"], COL_GATE)], bm=1024, bn=1024, bk=2048,
                            name="proj_zg_bwd_act", add=dhn, comm=scatter(names)), names, parts)
    if dist:
        g = parts
    grad_x, dg_x = _rmsnorm_bwd(x, dhn, small["norm_gain"], dy, "rms_x_bwd")

    fold = lambda part, heads, hd: jnp.sum(jnp.sum(part, axis=0).reshape(heads, hd), axis=0).reshape(1, hd)
    small_grads = {
        "norm_gain": jnp.sum(dg_x, axis=0).reshape(1, d),
        "mem_norm_gain": jnp.sum(dg_mem, axis=0).reshape(1, d),
        "b_forget": dbf[0:B_HEADS, 0].reshape(1, B_HEADS),
        "q_gain_a": fold(dg_qa, A_Q_HEADS, HEAD_DIM) * scale_ab,
        "k_gain_a": fold(dg_kva[:, 0:A_KV_WIDTH], A_KV_HEADS, HEAD_DIM),
        "sinks_a": (jnp.sum(dsink, axis=(1, 2)) * (1.0 / HEAD_DIM)).reshape(1, A_Q_HEADS),
        "q_gain_b": fold(dg_qb, B_HEADS, HEAD_DIM) * scale_ab,
        "k_gain_b": fold(dg_kb, B_HEADS, HEAD_DIM),
        "q_gain_c": fold(dg_qc, C_HEADS, C_HEAD_DIM),
        "k_gain_c": fold(dg_kc[:, 0:C_WIDTH], C_HEADS, C_HEAD_DIM),
    }
    return loss, grad_x, small_grads, g


def _coords():
    return lax.axis_index("x"), lax.axis_index("y"), lax.axis_index("c")


def _all_gather(shards, name):
    n = len(shards)

    def body(*refs):
        ins = refs[0:n]
        outs = refs[n:2 * n]
        send_sems, recv_sems, local_sems = refs[2 * n:2 * n + 3]
        x, y, c = _coords()
        me, sibling = (x, y, c), (x, y, 1 - c)
        chips = [(1 - x, y), (x, 1 - y), (1 - x, 1 - y)]
        idx = lambda p: 4 * p[0] + 2 * p[1] + p[2]

        def copy(a, k, block, to, src=None):
            slot = outs[a].at[idx(block)]
            return pltpu.make_async_remote_copy(
                src_ref=slot if src is None else src, dst_ref=slot,
                send_sem=send_sems.at[a, k], recv_sem=recv_sems.at[a, k], device_id=to, device_id_type=MESH)

        mine = [pltpu.make_async_copy(ins[a], outs[a].at[idx(me)], local_sems.at[a]) for a in range(n)]
        for cp in mine:
            cp.start()
        first = []
        for a in range(n):
            first.append(copy(a, 0, me, sibling, src=ins[a]))
            first += [copy(a, 1 + j, me, (*chip, c), src=ins[a]) for j, chip in enumerate(chips)]
        for cp in first:
            cp.start()
        passed = []
        for j, chip in enumerate(chips):
            for a in range(n):
                copy(a, 1 + j, (*chip, c), me).wait_recv()
                fwd = copy(a, 4 + j, (*chip, c), sibling)
                fwd.start()
                passed.append(fwd)
        for a in range(n):
            copy(a, 0, sibling, me).wait_recv()
            for j, chip in enumerate(chips):
                copy(a, 4 + j, (*chip, 1 - c), me).wait_recv()
        for cp in first + passed:
            cp.wait_send()
        for cp in mine:
            cp.wait()

    any_spec = pl.BlockSpec(memory_space=pl.ANY)
    return pl.pallas_call(
        body, name=name,
        in_specs=[any_spec] * n, out_specs=[any_spec] * n,
        out_shape=[jax.ShapeDtypeStruct((N_DEV,) + sh.shape, sh.dtype) for sh in shards],
        scratch_shapes=[pltpu.SemaphoreType.DMA((n, 7)), pltpu.SemaphoreType.DMA((n, 7)), pltpu.SemaphoreType.DMA((n,))],
    )(*shards)


def _all_reduce_small(vec, name):
    p = vec.shape[1]

    def body(v_ref, o_ref, gather, send_sems, recv_sems):
        x, y, c = _coords()
        my = 4 * x + 2 * y + c
        peers = [(x ^ ((k >> 2) & 1), y ^ ((k >> 1) & 1), c ^ (k & 1)) for k in range(1, N_DEV)]
        gather[my] = v_ref[...]
        sends = [pltpu.make_async_remote_copy(
            src_ref=v_ref, dst_ref=gather.at[my], send_sem=send_sems.at[k], recv_sem=recv_sems.at[k],
            device_id=peer, device_id_type=MESH) for k, peer in enumerate(peers)]
        for cp in sends:
            cp.start()
        for k, peer in enumerate(peers):
            pid = 4 * peer[0] + 2 * peer[1] + peer[2]
            pltpu.make_async_remote_copy(
                src_ref=v_ref, dst_ref=gather.at[pid], send_sem=send_sems.at[k], recv_sem=recv_sems.at[k],
                device_id=peer, device_id_type=MESH).wait_recv()
        for cp in sends:
            cp.wait_send()
        total = gather[0]
        for j in range(1, N_DEV):
            total = total + gather[j]
        o_ref[...] = total

    vm = pl.BlockSpec(memory_space=pltpu.VMEM)
    return pl.pallas_call(
        body, name=name, in_specs=[vm], out_specs=vm,
        out_shape=jax.ShapeDtypeStruct((8, p), F32),
        scratch_shapes=[pltpu.VMEM((N_DEV, 8, p), F32), pltpu.SemaphoreType.DMA((7,)), pltpu.SemaphoreType.DMA((7,))],
    )(vec)[0:1]


def _sum_parts(parts, name):
    _, rows, cols = parts.shape
    br = _tile(rows, 64, 16)

    def body(p_ref, o_ref):
        total = p_ref[0].astype(F32)
        for j in range(1, N_DEV):
            total = total + p_ref[j].astype(F32)
        o_ref[...] = total

    return pl.pallas_call(
        body, name=name, grid=(rows // br,),
        in_specs=[pl.BlockSpec((N_DEV, br, cols), lambda i: (0, i, 0))],
        out_specs=pl.BlockSpec((br, cols), lambda i: (i, 0)),
        out_shape=jax.ShapeDtypeStruct((rows, cols), F32),
        compiler_params=_params(("parallel",), VMEM_BIG),
    )(parts)


def _adamw(w, g, m, v, name, br=32):
    rows, cols = w.shape
    br = min(br, rows)
    c1 = 1.0 / (1.0 - ADAM_B1 ** ADAM_STEP)
    c2 = 1.0 / (1.0 - ADAM_B2 ** ADAM_STEP)

    def body(w_ref, g_ref, m_ref, v_ref, d_ref, nm_ref, nv_ref):
        gv = g_ref[...]
        nm = ADAM_B1 * m_ref[...] + (1.0 - ADAM_B1) * gv
        nv = ADAM_B2 * v_ref[...] + (1.0 - ADAM_B2) * (gv * gv)
        d_ref[...] = -ADAM_LR * ((nm * c1) / (jnp.sqrt(nv * c2) + ADAM_EPS) + ADAM_WD * w_ref[...])
        nm_ref[...] = nm
        nv_ref[...] = nv

    spec = pl.BlockSpec((br, cols), lambda i: (i, 0))
    shape = jax.ShapeDtypeStruct((rows, cols), F32)
    return pl.pallas_call(
        body, name=name, grid=(pl.cdiv(rows, br),), in_specs=[spec] * 4, out_specs=[spec] * 3, out_shape=[shape] * 3,
        compiler_params=_params(("parallel",), VMEM_BIG),
    )(w, g, m, v)


def _adamw_t(wt, g, mt, vt, name, br=1024):
    n, r = wt.shape
    c1 = 1.0 / (1.0 - ADAM_B1 ** ADAM_STEP)
    c2 = 1.0 / (1.0 - ADAM_B2 ** ADAM_STEP)

    def body(w_ref, g_ref, m_ref, v_ref, d_ref, nm_ref, nv_ref):
        gv = g_ref[...].T
        nm = ADAM_B1 * m_ref[...] + (1.0 - ADAM_B1) * gv
        nv = ADAM_B2 * v_ref[...] + (1.0 - ADAM_B2) * (gv * gv)
        d_ref[...] = -ADAM_LR * ((nm * c1) / (jnp.sqrt(nv * c2) + ADAM_EPS) + ADAM_WD * w_ref[...])
        nm_ref[...] = nm
        nv_ref[...] = nv

    spec = pl.BlockSpec((br, r), lambda i: (i, 0))
    shape = jax.ShapeDtypeStruct((n, r), F32)
    return pl.pallas_call(
        body, name=name, grid=(pl.cdiv(n, br),),
        in_specs=[spec, pl.BlockSpec((r, br), lambda i: (0, i)), spec, spec], out_specs=[spec] * 3, out_shape=[shape] * 3,
        compiler_params=_params(("parallel",), VMEM_BIG),
    )(wt, g, mt, vt)


def _adamw_parts(w, parts, m, v, name):
    rows, cols = w.shape
    br = _tile(rows, 32, 16)
    c1 = 1.0 / (1.0 - ADAM_B1 ** ADAM_STEP)
    c2 = 1.0 / (1.0 - ADAM_B2 ** ADAM_STEP)

    def body(w_ref, p_ref, m_ref, v_ref, g_ref, d_ref, nm_ref, nv_ref):
        gv = p_ref[0].astype(F32)
        for j in range(1, N_DEV):
            gv = gv + p_ref[j].astype(F32)
        nm = ADAM_B1 * m_ref[...] + (1.0 - ADAM_B1) * gv
        nv = ADAM_B2 * v_ref[...] + (1.0 - ADAM_B2) * (gv * gv)
        g_ref[...] = gv
        d_ref[...] = -ADAM_LR * ((nm * c1) / (jnp.sqrt(nv * c2) + ADAM_EPS) + ADAM_WD * w_ref[...])
        nm_ref[...] = nm
        nv_ref[...] = nv

    spec = pl.BlockSpec((br, cols), lambda i: (i, 0))
    shape = jax.ShapeDtypeStruct((rows, cols), F32)
    return pl.pallas_call(
        body, name=name, grid=(rows // br,),
        in_specs=[spec, pl.BlockSpec((N_DEV, br, cols), lambda i: (0, i, 0)), spec, spec],
        out_specs=[spec] * 4, out_shape=[shape] * 4,
        compiler_params=_params(("parallel",), VMEM_BIG),
    )(w, parts, m, v)


SMALL_NAMES = ("norm_gain", "mem_norm_gain", "b_forget", "q_gain_a", "k_gain_a", "sinks_a",
               "q_gain_b", "k_gain_b", "q_gain_c", "k_gain_c")
BIG_NAMES = ("w_in", "w_mem_kv", "w_branch_a", "w_branch_b", "w_branch_c", "w_out")
WEIGHT_ORDER = ("norm_gain", "mem_norm_gain", "w_in", "b_forget", "q_gain_a", "k_gain_a", "sinks_a", "q_gain_b",
                "k_gain_b", "q_gain_c", "k_gain_c", "w_mem_kv", "w_branch_a", "w_branch_b", "w_branch_c", "w_out")


def _pack_small(tree):
    flat = jnp.concatenate([tree[n].reshape(1, -1) for n in SMALL_NAMES], axis=1)
    pad = (-flat.shape[1]) % LANES
    return jnp.pad(flat, ((0, 0), (0, pad)))


def _unpack_small(flat, like):
    out, off = {}, 0
    for n in SMALL_NAMES:
        size = like[n].size
        out[n] = flat[:, off:off + size].reshape(like[n].shape)
        off += size
    return out


def kernel(x, mem, norm_gain, mem_norm_gain, w_in, b_forget, q_gain_a, k_gain_a, sinks_a, q_gain_b, k_gain_b, q_gain_c, k_gain_c, w_mem_kv, w_branch_a, w_branch_b, w_branch_c, w_out, loss_target, m_norm_gain, m_mem_norm_gain, m_w_in, m_b_forget, m_q_gain_a, m_k_gain_a, m_sinks_a, m_q_gain_b, m_k_gain_b, m_q_gain_c, m_k_gain_c, m_w_mem_kv, m_w_branch_a, m_w_branch_b, m_w_branch_c, m_w_out, v_norm_gain, v_mem_norm_gain, v_w_in, v_b_forget, v_q_gain_a, v_k_gain_a, v_sinks_a, v_q_gain_b, v_k_gain_b, v_q_gain_c, v_k_gain_c, v_w_mem_kv, v_w_branch_a, v_w_branch_b, v_w_branch_c, v_w_out):
    weights = dict(norm_gain=norm_gain, mem_norm_gain=mem_norm_gain, w_in=w_in, b_forget=b_forget, q_gain_a=q_gain_a,
                   k_gain_a=k_gain_a, sinks_a=sinks_a, q_gain_b=q_gain_b, k_gain_b=k_gain_b, q_gain_c=q_gain_c,
                   k_gain_c=k_gain_c, w_mem_kv=w_mem_kv, w_branch_a=w_branch_a, w_branch_b=w_branch_b,
                   w_branch_c=w_branch_c, w_out=w_out)
    mom_m = dict(norm_gain=m_norm_gain, mem_norm_gain=m_mem_norm_gain, w_in=m_w_in, b_forget=m_b_forget,
                 q_gain_a=m_q_gain_a, k_gain_a=m_k_gain_a, sinks_a=m_sinks_a, q_gain_b=m_q_gain_b, k_gain_b=m_k_gain_b,
                 q_gain_c=m_q_gain_c, k_gain_c=m_k_gain_c, w_mem_kv=m_w_mem_kv, w_branch_a=m_w_branch_a,
                 w_branch_b=m_w_branch_b, w_branch_c=m_w_branch_c, w_out=m_w_out)
    mom_v = dict(norm_gain=v_norm_gain, mem_norm_gain=v_mem_norm_gain, w_in=v_w_in, b_forget=v_b_forget,
                 q_gain_a=v_q_gain_a, k_gain_a=v_k_gain_a, sinks_a=v_sinks_a, q_gain_b=v_q_gain_b, k_gain_b=v_k_gain_b,
                 q_gain_c=v_q_gain_c, k_gain_c=v_k_gain_c, w_mem_kv=v_w_mem_kv, w_branch_a=v_w_branch_a,
                 w_branch_b=v_w_branch_b, w_branch_c=v_w_branch_c, w_out=v_w_out)
    wi = w_in[0]
    sh_qkv = jnp.concatenate([wi[:, a:b] for a, b in SRC_RANGES[0:3]], axis=1).astype(BF16)
    sh_zg = jnp.concatenate([wi[:, a:b] for a, b in SRC_RANGES[3:6]] + [wi[:, SRC_GATE:]], axis=1).astype(BF16)
    sh_wf = jnp.pad(wi[:, FB_SRC:FB_SRC + B_HEADS], ((0, 0), (0, FB_PAD - B_HEADS))).astype(BF16)
    shards = {"zg": sh_zg, "wo": w_out[0].astype(BF16), "wa": w_branch_a[0].astype(BF16),
              "wb": w_branch_b[0].astype(BF16), "wc": w_branch_c[0].astype(BF16)}
    first = ("qkv", "wf", "wk")
    full = _all_gather([sh_qkv, sh_wf, w_mem_kv[0].astype(BF16)], "weights_all_gather")
    wg = {kname: arr.reshape(arr.shape[0] * arr.shape[1], arr.shape[2]) for kname, arr in zip(first, full)}

    small = {n: weights[n] for n in SMALL_NAMES}
    loss_local, grad_x, small_g, parts = _local_step(x[0], mem[0], loss_target[0], small, wg, shards)

    grads, delta, new_m, new_v = {}, {}, {}, {}
    g1, g2, gz, gf, gg = (_sum_parts(parts[k], "grad_sum_" + k) for k in ("wm_q1", "wm_q2", "wm_z", "wf", "wm_g"))
    half = Q_SPLIT
    g_in = jnp.concatenate([g1, g2[:, 0:COL_QB - half], gz[:, COL_ZA:COL_ZB], g2[:, COL_QB - half:COL_QC - half],
                            gz[:, COL_ZB:COL_ZC], gf[:, 0:B_HEADS], g2[:, COL_QC - half:W_QKV - half], gz[:, COL_ZC:W_Z], gg], axis=1)
    dlt, nm, nv = _adamw_t(w_in[0].T, g_in, m_w_in[0].T, v_w_in[0].T, "adamw_w_in")
    others = ("wk", "wo", "wa", "wb", "wc")
    (dlt, nm, nv), held, small_g, loss_local = lax.optimization_barrier(
        ((dlt, nm, nv), [parts[k] for k in others], small_g, loss_local))
    parts.update(zip(others, held))
    grads["w_in"], delta["w_in"], new_m["w_in"], new_v["w_in"] = g_in, dlt.T[None], nm.T[None], nv.T[None]
    for n, kname in (("w_mem_kv", "wk"), ("w_out", "wo"), ("w_branch_a", "wa"), ("w_branch_b", "wb"), ("w_branch_c", "wc")):
        gsum, dlt, nm, nv = _adamw_parts(weights[n][0], parts[kname], mom_m[n][0], mom_v[n][0], "adamw_" + n)
        grads[n], delta[n], new_m[n], new_v[n] = gsum, dlt[None], nm[None], nv[None]

    packed = _pack_small(small_g)
    packed = jnp.concatenate([packed[:, :-1], loss_local.reshape(1, 1)], axis=1)
    reduced = _all_reduce_small(jnp.broadcast_to(packed, (8, packed.shape[1])), "small_all_reduce")
    grads.update(_unpack_small(reduced, small))
    loss = reduced[0, -1]

    pw, pm, pv = _pack_small(small), _pack_small({n: mom_m[n] for n in SMALL_NAMES}), _pack_small({n: mom_v[n] for n in SMALL_NAMES})
    rep8 = lambda a: jnp.broadcast_to(a, (8, a.shape[1]))
    dlt, nm, nv = _adamw(rep8(pw), rep8(reduced), rep8(pm), rep8(pv), "adamw_small")
    for tree, flat in ((delta, dlt), (new_m, nm), (new_v, nv)):
        tree.update(_unpack_small(flat[0:1], small))
    for n in BIG_NAMES:
        grads[n] = grads[n][None]
    return (loss, grad_x[None], *[grads[n] for n in WEIGHT_ORDER], *[delta[n] for n in WEIGHT_ORDER],
            *[new_m[n] for n in WEIGHT_ORDER], *[new_v[n] for n in WEIGHT_ORDER])
```

```python
import math

import jax
import jax.numpy as jnp
import numpy as np
from jax import lax
from jax.experimental import pallas as pl
from jax.experimental.pallas import tpu as pltpu

F32 = jnp.float32
BF16 = jnp.bfloat16

N_DEV = 8
HEAD_DIM = 64
A_Q_HEADS = 12
A_KV_HEADS = 4
A_GROUP = 3
B_HEADS = 12
C_HEADS = 4
C_HEAD_DIM = 128
WINDOW = 128
A_WIDTH = 768
A_KV_WIDTH = 256
B_WIDTH = 768
C_WIDTH = 512
EPS = 1e-6
NEG = -1e30

COL_QA, COL_KA, COL_VA = 0, 768, 1024
COL_QB, COL_KB, COL_VB = 1280, 2048, 2816
COL_QC = 3584
W_QKV = 4096
Q_SPLIT = 1280
COL_ZA, COL_ZB, COL_ZC = 0, 768, 1536
COL_GATE = W_Z = 2048
SRC_RANGES = ((0, 1280), (2048, 4352), (5132, 5644), (1280, 2048), (4352, 5120), (5644, 6156))
SRC_GATE = 6156
FB_SRC = 5120
FB_PAD = 128

ADAM_LR = 0.001
ADAM_B1 = 0.9
ADAM_B2 = 0.999
ADAM_EPS = 1e-08
ADAM_WD = 0.01
ADAM_STEP = 10

VMEM_BIG = 52 * 1024 * 1024
LANES = 128
MESH = pl.DeviceIdType.MESH


def _tile(n, pref, mult=128):
    if n <= pref:
        return n
    t = (pref // mult) * mult
    while t >= mult:
        if n % t == 0:
            return t
        t -= mult
    return n


def _params(sem=None, vmem=None):
    kw = {}
    if sem is not None:
        kw["dimension_semantics"] = sem
    if vmem is not None:
        kw["vmem_limit_bytes"] = vmem
    return pltpu.CompilerParams(**kw)


def _sigmoid(x):
    return 1.0 / (1.0 + jnp.exp(-x))


def _block_diag(hd):
    r = np.arange(LANES)
    return jnp.asarray((r[:, None] // hd) == (r[None, :] // hd), dtype=BF16)


def _seg_sum(t, bd):
    hi = t.astype(BF16)
    lo = (t - hi.astype(F32)).astype(BF16)
    outs = []
    for c in range(t.shape[1] // LANES):
        sl = slice(c * LANES, (c + 1) * LANES)
        outs.append(jnp.dot(hi[:, sl], bd, preferred_element_type=F32) + jnp.dot(lo[:, sl], bd, preferred_element_type=F32))
    return outs[0] if len(outs) == 1 else jnp.concatenate(outs, axis=1)


def _rmsnorm_fwd(x, gain, name):
    rows, d = x.shape
    bm = _tile(rows, 512, 8)

    def body(x_ref, g_ref, o_ref):
        xv = x_ref[...]
        ms = jnp.mean(xv * xv, axis=-1, keepdims=True)
        o_ref[...] = (xv * lax.rsqrt(ms + EPS) * g_ref[...]).astype(BF16)

    return pl.pallas_call(
        body, name=name, grid=(rows // bm,),
        in_specs=[pl.BlockSpec((bm, d), lambda i: (i, 0)), pl.BlockSpec((1, d), lambda i: (0, 0))],
        out_specs=pl.BlockSpec((bm, d), lambda i: (i, 0)),
        out_shape=jax.ShapeDtypeStruct((rows, d), BF16),
        compiler_params=_params(("parallel",)),
    )(x, gain)


def _rmsnorm_bwd(x, dhn, gain, dy, name):
    rows, d = x.shape
    bm = _tile(rows, 512, 8)
    with_dx = dy is not None

    def body(*refs):
        if with_dx:
            x_ref, dh_ref, g_ref, dy_ref, gx_ref, dg_ref = refs
        else:
            x_ref, dh_ref, g_ref, dg_ref = refs
        i = pl.program_id(0)
        xv = x_ref[...]
        rstd = lax.rsqrt(jnp.mean(xv * xv, axis=-1, keepdims=True) + EPS)
        xhat = xv * rstd
        dh = dh_ref[...]
        part = jnp.sum((dh * xhat).reshape(bm // 8, 8, d), axis=0)

        @pl.when(i == 0)
        def _():
            dg_ref[...] = part

        @pl.when(i > 0)
        def _():
            dg_ref[...] += part

        if with_dx:
            g = dh * g_ref[...]
            mean = jnp.mean(g * xhat, axis=-1, keepdims=True)
            gx_ref[...] = dy_ref[...] + rstd * (g - xhat * mean)

    row_spec = pl.BlockSpec((bm, d), lambda i: (i, 0))
    in_specs = [row_spec, row_spec, pl.BlockSpec((1, d), lambda i: (0, 0))]
    args = [x, dhn, gain]
    dg_spec = pl.BlockSpec((8, d), lambda i: (0, 0))
    dg_shape = jax.ShapeDtypeStruct((8, d), F32)
    if with_dx:
        in_specs.append(row_spec)
        args.append(dy)
        out_specs = [row_spec, dg_spec]
        out_shape = [jax.ShapeDtypeStruct((rows, d), F32), dg_shape]
    else:
        out_specs = [dg_spec]
        out_shape = [dg_shape]
    outs = pl.pallas_call(
        body, name=name, grid=(rows // bm,), in_specs=in_specs, out_specs=out_specs, out_shape=out_shape,
        compiler_params=_params(("arbitrary",), VMEM_BIG),
    )(*args)
    return outs if with_dx else (None, outs[0])


class _Comm:
    def __init__(self, kind, arrays):
        self.kind = kind
        self.arrays = list(arrays)
        self.n = len(self.arrays)

    def out_shapes(self):
        if self.kind == "gather":
            return [jax.ShapeDtypeStruct((N_DEV,) + a.shape, a.dtype) for a in self.arrays]
        return [jax.ShapeDtypeStruct(a.shape, a.dtype) for a in self.arrays]

    def scratch(self):
        return [pltpu.SemaphoreType.DMA((self.n, N_DEV - 1)), pltpu.SemaphoreType.DMA((self.n, N_DEV - 1)),
                pltpu.SemaphoreType.DMA((self.n,))]

    def _plan(self, ins, outs, sems, with_recvs):
        send_sems, recv_sems, local_sems = sems
        x, y, c = lax.axis_index("x"), lax.axis_index("y"), lax.axis_index("c")
        my = 4 * x + 2 * y + c
        gather = self.kind == "gather"
        local, sends, recvs = [], [], []
        for a in range(self.n):
            local.append(pltpu.make_async_copy(ins[a] if gather else ins[a].at[my], outs[a].at[my], local_sems.at[a]))
            for k in range(1, N_DEV):
                peer = (x ^ ((k >> 2) & 1), y ^ ((k >> 1) & 1), c ^ (k & 1))
                pid = 4 * peer[0] + 2 * peer[1] + peer[2]
                src = ins[a] if gather else ins[a].at[pid]
                sem = dict(send_sem=send_sems.at[a, k - 1], recv_sem=recv_sems.at[a, k - 1], device_id=peer, device_id_type=MESH)
                sends.append(pltpu.make_async_remote_copy(src_ref=src, dst_ref=outs[a].at[my], **sem))
                if with_recvs:
                    recvs.append(pltpu.make_async_remote_copy(src_ref=src, dst_ref=outs[a].at[pid], **sem))
        return local, sends, recvs

    def start(self, ins, outs, sems):
        local, sends, _ = self._plan(ins, outs, sems, False)
        for cp in local + sends:
            cp.start()

    def wait(self, ins, outs, sems):
        local, sends, recvs = self._plan(ins, outs, sems, True)
        for cp in recvs:
            cp.wait_recv()
        for cp in sends:
            cp.wait_send()
        for cp in local:
            cp.wait()


def _grid_edges(grid):
    first = last = None
    for ax, size in enumerate(grid):
        pid = pl.program_id(ax)
        f, l = pid == 0, pid == size - 1
        first = f if first is None else first & f
        last = l if last is None else last & l
    return first, last


def _hosted_call(body, comm, *, name, grid, in_specs, out_specs, out_shape, scratch_shapes, args, sem, vmem=None):
    in_specs, out_specs, out_shape, scratch_shapes = list(in_specs), list(out_specs), list(out_shape), list(scratch_shapes)
    if comm is None:
        res = pl.pallas_call(body, name=name, grid=grid, in_specs=in_specs, out_specs=out_specs, out_shape=out_shape,
                             scratch_shapes=scratch_shapes, compiler_params=_params(sem, vmem))(*args)
        return list(res), []
    n_in, n_out, n_scr, nc = len(in_specs), len(out_shape), len(scratch_shapes), comm.n

    def hosted(*refs):
        ins = refs[0:n_in]
        comm_in = refs[n_in:n_in + nc]
        outs = refs[n_in + nc:n_in + nc + n_out]
        comm_out = refs[n_in + nc + n_out:n_in + 2 * nc + n_out]
        scr = refs[n_in + 2 * nc + n_out:n_in + 2 * nc + n_out + n_scr]
        sems = refs[n_in + 2 * nc + n_out + n_scr:]
        first, last = _grid_edges(grid)

        @pl.when(first)
        def _():
            comm.start(comm_in, comm_out, sems)

        body(*ins, *outs, *scr)

        @pl.when(last)
        def _():
            comm.wait(comm_in, comm_out, sems)

    any_spec = pl.BlockSpec(memory_space=pl.ANY)
    res = pl.pallas_call(
        hosted, name=name, grid=grid, in_specs=in_specs + [any_spec] * nc, out_specs=out_specs + [any_spec] * nc,
        out_shape=out_shape + comm.out_shapes(), scratch_shapes=scratch_shapes + comm.scratch(),
        compiler_params=_params(("arbitrary",) * len(grid), vmem),
    )(*args, *comm.arrays)
    return list(res[0:n_out]), list(res[n_out:])


def _mm(a, b, *, grid, a_spec, b_spec, o_spec, o_shape, o_dtype, contract, name, add=None, add_spec=None, acc_shape=None,
        comm=None):
    nk = grid[2]
    has_add = add is not None

    def body(*refs):
        a_ref, b_ref = refs[0], refs[1]
        add_ref = refs[2] if has_add else None
        o_ref = refs[3] if has_add else refs[2]
        part = lax.dot_general(a_ref[...], b_ref[...], (contract, ((), ())), preferred_element_type=F32)
        if nk == 1:
            if has_add:
                part = part + add_ref[...]
            o_ref[...] = part.astype(o_dtype)
        else:
            acc = refs[-1]
            k = pl.program_id(2)

            @pl.when(k == 0)
            def _():
                acc[...] = part

            @pl.when(k > 0)
            def _():
                acc[...] += part

            @pl.when(k == nk - 1)
            def _():
                r = acc[...]
                if has_add:
                    r = r + add_ref[...]
                o_ref[...] = r.astype(o_dtype)

    in_specs = [a_spec, b_spec] + ([add_spec] if has_add else [])
    args = [a, b] + ([add] if has_add else [])
    scratch = [pltpu.VMEM(acc_shape, F32)] if nk > 1 else []
    outs, comm_outs = _hosted_call(
        body, comm, name=name, grid=grid, in_specs=in_specs, out_specs=[o_spec],
        out_shape=[jax.ShapeDtypeStruct(o_shape, o_dtype)], scratch_shapes=scratch, args=args,
        sem=("parallel", "parallel", "arbitrary"), vmem=VMEM_BIG)
    return outs[0] if comm is None else (outs[0], comm_outs)


def _mm_nn(a, b, *, bm, bn, bk, o_dtype, name, add=None, comm=None):
    m, kd = a.shape
    n = b.shape[1]
    bm, bn, bk = _tile(m, bm, 8), _tile(n, bn), _tile(kd, bk)
    o_spec = pl.BlockSpec((bm, bn), lambda i, j, k: (i, j))
    return _mm(a, b, grid=(m // bm, n // bn, kd // bk),
               a_spec=pl.BlockSpec((bm, bk), lambda i, j, k: (i, k)),
               b_spec=pl.BlockSpec((bk, bn), lambda i, j, k: (k, j)),
               o_spec=o_spec, o_shape=(m, n), o_dtype=o_dtype, contract=((1,), (0,)), name=name,
               add=add, add_spec=o_spec, acc_shape=(bm, bn), comm=comm)


def _mm_nt(a, b, *, bm, bn, bk, o_dtype, name, add=None, b_col0=0, comm=None):
    m, kd = a.shape
    n = b.shape[0]
    bm, bn, bk = _tile(m, bm, 8), _tile(n, bn), _tile(math.gcd(kd, b_col0), bk)
    kb0 = b_col0 // bk
    o_spec = pl.BlockSpec((bm, bn), lambda i, j, k: (i, j))
    return _mm(a, b, grid=(m // bm, n // bn, kd // bk),
               a_spec=pl.BlockSpec((bm, bk), lambda i, j, k: (i, k)),
               b_spec=pl.BlockSpec((bn, bk), lambda i, j, k: (j, kb0 + k)),
               o_spec=o_spec, o_shape=(m, n), o_dtype=o_dtype, contract=((1,), (1,)), name=name,
               add=add, add_spec=o_spec, acc_shape=(bm, bn), comm=comm)


def _mm_nt_sum(terms, *, bm, bn, bk, name, add=None, comm=None):
    m = terms[0][0].shape[0]
    n = terms[0][1].shape[0]
    bm, bn = _tile(m, bm, 8), _tile(n, bn)
    nt = (((1,), (1,)), ((), ()))
    plan, groups, start = [], [], 0
    for a, b, col0 in terms:
        kd = a.shape[1]
        tk = _tile(math.gcd(kd, col0), bk)
        steps = kd // tk
        if plan and kd < bk:
            groups.append([b, start - 1, 1, col0 // tk, tk])
            plan.append((start - 1, 1, len(groups) - 1, True))
            continue
        last = groups[-1] if groups else None
        if last is not None and last[0] is b and last[4] == tk and (last[3] + last[2]) * tk == col0:
            last[2] += steps
        else:
            groups.append([b, start, steps, col0 // tk, tk])
        plan.append((start, steps, len(groups) - 1, False))
        start += steps
    nk = start
    nterm, ngroup, has_add = len(terms), len(groups), add is not None

    def body(*refs):
        a_refs, b_refs = refs[0:nterm], refs[nterm:nterm + ngroup]
        add_ref = refs[nterm + ngroup] if has_add else None
        o_ref, acc = refs[nterm + ngroup + has_add], refs[nterm + ngroup + has_add + 1]
        k = pl.program_id(2)
        for t, (s0, steps, grp, rides) in enumerate(plan):
            @pl.when((k >= s0) & (k < s0 + steps))
            def _():
                part = lax.dot_general(a_refs[t][...], b_refs[grp][...], nt, preferred_element_type=F32)
                if rides:
                    acc[...] += part
                    return

                @pl.when(k == 0)
                def _():
                    acc[...] = part

                @pl.when(k > 0)
                def _():
                    acc[...] += part

        @pl.when(k == nk - 1)
        def _():
            o_ref[...] = acc[...] + add_ref[...] if has_add else acc[...]

    def a_spec(tk, s0, steps):
        return pl.BlockSpec((bm, tk), lambda i, j, k: (i, jnp.clip(k - s0, 0, steps - 1)))

    def b_spec(tk, s0, steps, off):
        return pl.BlockSpec((bn, tk), lambda i, j, k: (j, off + jnp.clip(k - s0, 0, steps - 1)))

    o_spec = pl.BlockSpec((bm, bn), lambda i, j, k: (i, j))
    in_specs = [a_spec(groups[grp][4], s0, steps) for s0, steps, grp, _ in plan]
    in_specs += [b_spec(tk, s0, steps, cb0) for _, s0, steps, cb0, tk in groups]
    args = [a for a, _, _ in terms] + [grp[0] for grp in groups]
    if has_add:
        in_specs.append(o_spec)
        args.append(add)
    outs, comm_outs = _hosted_call(
        body, comm, name=name, grid=(m // bm, n // bn, nk), in_specs=in_specs,
        out_specs=[o_spec], out_shape=[jax.ShapeDtypeStruct((m, n), F32)],
        scratch_shapes=[pltpu.VMEM((bm, bn), F32)], args=args,
        sem=("parallel", "parallel", "arbitrary"), vmem=VMEM_BIG)
    return outs[0] if comm is None else (outs[0], comm_outs)


def _mm_tn(a, b, *, bm, bn, bk, o_dtype, name, comm=None):
    kd, m = a.shape
    n = b.shape[1]
    bm, bn, bk = _tile(m, bm), _tile(n, bn), _tile(kd, bk, 8)
    return _mm(a, b, grid=(m // bm, n // bn, kd // bk),
               a_spec=pl.BlockSpec((bk, bm), lambda i, j, k: (k, i)),
               b_spec=pl.BlockSpec((bk, bn), lambda i, j, k: (k, j)),
               o_spec=pl.BlockSpec((bm, bn), lambda i, j, k: (i, j)),
               o_shape=(m, n), o_dtype=o_dtype, contract=((0,), (0,)), name=name, acc_shape=(bm, bn), comm=comm)


def _branch_full(w8):
    kb, ds = w8.shape[0] // N_DEV, w8.shape[1]
    return w8.reshape(N_DEV, kb, ds).transpose(1, 0, 2).reshape(kb, N_DEV * ds)


def _branch_shards(g):
    kb, ds = g.shape[0], g.shape[1] // N_DEV
    return g.reshape(kb, N_DEV, ds).transpose(1, 0, 2).reshape(N_DEV * kb, ds)


def _headnorm_fwd(src, c0, width, bw, hd, gain, nflag, head_major, name):
    rows = src.shape[0]
    bm = _tile(rows, 2048 if bw <= 256 else 1024, 16)
    bd = _block_diag(hd)
    cb0 = c0 // bw

    def body(x_ref, g_ref, f_ref, bd_ref, o_ref):
        xv = x_ref[...].astype(F32)
        ss = _seg_sum(xv * xv, bd_ref[...])
        rstd = lax.rsqrt(ss * (1.0 / hd) + EPS)
        y = (xv * jnp.where(f_ref[...] > 0.0, rstd, 1.0) * g_ref[...]).astype(BF16)
        if head_major:
            for h in range(bw // HEAD_DIM):
                o_ref[h] = y[:, h * HEAD_DIM:(h + 1) * HEAD_DIM]
        else:
            o_ref[...] = y

    vec_spec = pl.BlockSpec((1, bw), lambda i, t: (0, t))
    if head_major:
        hpb = bw // HEAD_DIM
        out_spec = pl.BlockSpec((hpb, bm, HEAD_DIM), lambda i, t: (t, i, 0))
        out_shape = jax.ShapeDtypeStruct((width // HEAD_DIM, rows, HEAD_DIM), BF16)
    else:
        out_spec = pl.BlockSpec((bm, bw), lambda i, t: (i, t))
        out_shape = jax.ShapeDtypeStruct((rows, width), BF16)
    return pl.pallas_call(
        body, name=name, grid=(rows // bm, width // bw),
        in_specs=[pl.BlockSpec((bm, bw), lambda i, t: (i, cb0 + t)), vec_spec, vec_spec,
                  pl.BlockSpec((LANES, LANES), lambda i, t: (0, 0))],
        out_specs=out_spec, out_shape=out_shape,
        compiler_params=_params(("parallel", "parallel")),
    )(src, gain, nflag, bd)


def _headnorm_bwd(src, c0, width, bw, hd, gain, nflag, dyn, target, t0, name):
    rows = src.shape[0]
    bm = _tile(rows, 2048 if bw <= 256 else 1024, 16)
    bd = _block_diag(hd)
    cb0 = c0 // bw
    tb0 = t0 // bw
    aliased = target is not None

    def body(*refs):
        if aliased:
            x_ref, dy_ref, g_ref, f_ref, bd_ref, _, o_ref, dg_ref = refs
        else:
            x_ref, dy_ref, g_ref, f_ref, bd_ref, o_ref, dg_ref = refs
        i = pl.program_id(1)
        xv = x_ref[...].astype(F32)
        dyv = dy_ref[...]
        bdv = bd_ref[...]
        rstd = lax.rsqrt(_seg_sum(xv * xv, bdv) * (1.0 / hd) + EPS)
        xhat = xv * rstd
        g = dyv * g_ref[...]
        mean = _seg_sum(g * xhat, bdv) * (1.0 / hd)
        dx = jnp.where(f_ref[...] > 0.0, rstd * (g - xhat * mean), g)
        o_ref[...] = dx.astype(BF16)
        part = jnp.sum((dyv * xhat).reshape(bm // 8, 8, bw), axis=0)

        @pl.when(i == 0)
        def _():
            dg_ref[...] = part

        @pl.when(i > 0)
        def _():
            dg_ref[...] += part

    vec_spec = pl.BlockSpec((1, bw), lambda t, i: (0, t))
    in_specs = [pl.BlockSpec((bm, bw), lambda t, i: (i, cb0 + t)), pl.BlockSpec((bm, bw), lambda t, i: (i, t)),
                vec_spec, vec_spec, pl.BlockSpec((LANES, LANES), lambda t, i: (0, 0))]
    args = [src, dyn, gain, nflag, bd]
    aliases = {}
    if aliased:
        in_specs.append(pl.BlockSpec(memory_space=pl.ANY))
        args.append(target)
        aliases = {5: 0}
        o_shape = jax.ShapeDtypeStruct(target.shape, BF16)
    else:
        o_shape = jax.ShapeDtypeStruct((rows, width), BF16)
    out, dg = pl.pallas_call(
        body, name=name, grid=(width // bw, rows // bm), in_specs=in_specs,
        out_specs=[pl.BlockSpec((bm, bw), lambda t, i: (i, tb0 + t)), pl.BlockSpec((8, bw), lambda t, i: (0, t))],
        out_shape=[o_shape, jax.ShapeDtypeStruct((8, width), F32)],
        input_output_aliases=aliases,
        compiler_params=_params(("parallel", "arbitrary")),
    )(*args)
    return out, dg


def _fox_prep(pfb, bpad, name):
    s = pfb.shape[0]

    def body(p_ref, b_ref, c_ref):
        z = p_ref[...] + b_ref[...]
        logf = jnp.minimum(z, 0.0) - jnp.log(1.0 + jnp.exp(-jnp.abs(z)))
        x = logf.T[0:16, :]
        lane = lax.broadcasted_iota(jnp.int32, (16, s), 1)
        sh = 1
        while sh < s:
            x = x + jnp.where(lane >= sh, pltpu.roll(x, sh, 1), 0.0)
            sh *= 2
        c_ref[...] = x

    return pl.pallas_call(
        body, name=name, grid=(1,),
        in_specs=[pl.BlockSpec((s, FB_PAD), lambda i: (0, 0)), pl.BlockSpec((1, FB_PAD), lambda i: (0, 0))],
        out_specs=pl.BlockSpec((16, s), lambda i: (0, 0)),
        out_shape=jax.ShapeDtypeStruct((16, s), F32),
        compiler_params=_params(("arbitrary",)),
    )(pfb, bpad)


def _fox_prep_bwd(pfb, bpad, dct, name):
    s = pfb.shape[0]

    def body(p_ref, b_ref, dc_ref, df_ref, db_ref):
        zt = (p_ref[...] + b_ref[...]).T[0:16, :]
        y = dc_ref[...]
        lane = lax.broadcasted_iota(jnp.int32, (16, s), 1)
        sh = 1
        while sh < s:
            y = y + jnp.where(lane < s - sh, pltpu.roll(y, s - sh, 1), 0.0)
            sh *= 2
        dz = y * _sigmoid(-zt)
        db_ref[...] = jnp.broadcast_to(jnp.sum(dz, axis=1, keepdims=True), (16, FB_PAD))
        full = jnp.concatenate([dz, jnp.zeros((FB_PAD - 16, s), F32)], axis=0)
        df_ref[...] = full.T.astype(BF16)

    return pl.pallas_call(
        body, name=name, grid=(1,),
        in_specs=[pl.BlockSpec((s, FB_PAD), lambda i: (0, 0)), pl.BlockSpec((1, FB_PAD), lambda i: (0, 0)),
                  pl.BlockSpec((16, s), lambda i: (0, 0))],
        out_specs=[pl.BlockSpec((s, FB_PAD), lambda i: (0, 0)), pl.BlockSpec((16, FB_PAD), lambda i: (0, 0))],
        out_shape=[jax.ShapeDtypeStruct((s, FB_PAD), BF16), jax.ShapeDtypeStruct((16, FB_PAD), F32)],
        compiler_params=_params(("arbitrary",)),
    )(pfb, bpad, dct)


def _swa_window(n):
    ws = pl.multiple_of(jnp.maximum(n * WINDOW - WINDOW, 0), WINDOW)
    qi = lax.broadcasted_iota(jnp.int32, (WINDOW, 2 * WINDOW), 0)
    kj = lax.broadcasted_iota(jnp.int32, (WINDOW, 2 * WINDOW), 1)
    rel = qi + (n * WINDOW - ws) - kj
    valid = (rel >= 0) & (rel < WINDOW)
    return ws, valid, rel.astype(F32)


def _attn_a_fwd(qkv, sinks, slopes, name):
    s = qkv.shape[1]
    nb = s // WINDOW
    smem = pl.BlockSpec(memory_space=pltpu.SMEM)

    def body(sink_ref, slope_ref, q_ref, k_ref, v_ref, o_ref, lse_ref):
        n = pl.program_id(0)
        ws, valid, relf = _swa_window(n)
        outs = []
        for h in range(A_Q_HEADS):
            kvh = h // A_GROUP
            kw = k_ref[kvh, pl.ds(ws, 2 * WINDOW), :]
            vw = v_ref[kvh, pl.ds(ws, 2 * WINDOW), :]
            sc = lax.dot_general(q_ref[h], kw, (((1,), (1,)), ((), ())), preferred_element_type=F32)
            sc = jnp.where(valid, sc - slope_ref[h] * relf, NEG)
            sink = sink_ref[h]
            m = jnp.maximum(jnp.max(sc, axis=1, keepdims=True), sink)
            p = jnp.exp(sc - m)
            denom = jnp.sum(p, axis=1, keepdims=True) + jnp.exp(sink - m)
            pn = (p / denom).astype(BF16)
            outs.append(jnp.dot(pn, vw, preferred_element_type=F32))
            lse_ref[h] = jnp.broadcast_to(m + jnp.log(denom), (WINDOW, HEAD_DIM))
        o_ref[...] = jnp.concatenate(outs, axis=1)

    return pl.pallas_call(
        body, name=name, grid=(nb,),
        in_specs=[smem, smem,
                  pl.BlockSpec((A_Q_HEADS, WINDOW, HEAD_DIM), lambda n: (0, n, 0)),
                  pl.BlockSpec((A_KV_HEADS, s, HEAD_DIM), lambda n: (A_GROUP, 0, 0)),
                  pl.BlockSpec((A_KV_HEADS, s, HEAD_DIM), lambda n: (A_GROUP + 1, 0, 0))],
        out_specs=[pl.BlockSpec((WINDOW, A_WIDTH), lambda n: (n, 0)),
                   pl.BlockSpec((A_Q_HEADS, WINDOW, HEAD_DIM), lambda n: (0, n, 0))],
        out_shape=[jax.ShapeDtypeStruct((s, A_WIDTH), F32), jax.ShapeDtypeStruct((A_Q_HEADS, s, HEAD_DIM), F32)],
        compiler_params=_params(("parallel",), VMEM_BIG),
    )(sinks, slopes, qkv, qkv, qkv)


def _attn_a_bwd(qkv, do, lse, dd, sinks, slopes, name, comm=None):
    s = qkv.shape[1]
    nb = s // WINDOW
    smem = pl.BlockSpec(memory_space=pltpu.SMEM)
    last = nb - 1

    def body(sink_ref, slope_ref, q_ref, k_ref, v_ref, do_ref, lse_ref, dd_ref, dq_ref, dkv_ref, ds_ref, carry):
        n = pl.program_id(0)

        @pl.when(n == 0)
        def _():
            carry[...] = jnp.zeros(carry.shape, F32)
            ds_ref[...] = jnp.zeros(ds_ref.shape, F32)

        @pl.when(n < nb)
        def _():
            ws, valid, relf = _swa_window(n)
            dqs = []
            dkw = [None] * A_KV_HEADS
            dvw = [None] * A_KV_HEADS
            for h in range(A_Q_HEADS):
                kvh = h // A_GROUP
                qh = q_ref[h]
                doh = do_ref[h]
                kw = k_ref[kvh, pl.ds(ws, 2 * WINDOW), :]
                vw = v_ref[kvh, pl.ds(ws, 2 * WINDOW), :]
                lse_h = lse_ref[h]
                dd_h = dd_ref[h]
                sc = lax.dot_general(qh, kw, (((1,), (1,)), ((), ())), preferred_element_type=F32)
                sc = jnp.where(valid, sc - slope_ref[h] * relf, NEG)
                p = jnp.exp(sc - lse_h[:, 0:1])
                dp = lax.dot_general(doh, vw, (((1,), (1,)), ((), ())), preferred_element_type=F32)
                dsc = (p * (dp - dd_h[:, 0:1])).astype(BF16)
                pb = p.astype(BF16)
                dqs.append(jnp.dot(dsc, kw, preferred_element_type=F32))
                dk_h = jnp.dot(qh.T, dsc, preferred_element_type=F32)
                dv_h = jnp.dot(doh.T, pb, preferred_element_type=F32)
                dkw[kvh] = dk_h if dkw[kvh] is None else dkw[kvh] + dk_h
                dvw[kvh] = dv_h if dvw[kvh] is None else dvw[kvh] + dv_h
                psink = jnp.exp(sink_ref[h] - lse_h)
                ds_ref[h] += jnp.sum((-psink * dd_h).reshape(WINDOW // 8, 8, HEAD_DIM), axis=0)
            dq_ref[...] = jnp.concatenate(dqs, axis=1)
            win = jnp.concatenate(dkw + dvw, axis=0)
            first = win[:, 0:WINDOW]
            second = win[:, WINDOW:2 * WINDOW]
            dkv_ref[...] = (carry[...] + first).T
            carry[...] = jnp.where(n == 0, first, second)

        @pl.when(n == nb)
        def _():
            dkv_ref[...] = carry[...].T

    hm = lambda heads: pl.BlockSpec((heads, WINDOW, HEAD_DIM), lambda n: (0, jnp.minimum(n, last), 0))
    res = lambda blk: pl.BlockSpec((A_KV_HEADS, s, HEAD_DIM), lambda n: (blk, 0, 0))
    outs, comm_outs = _hosted_call(
        body, comm, name=name, grid=(nb + 1,),
        in_specs=[smem, smem, hm(A_Q_HEADS), res(A_GROUP), res(A_GROUP + 1), hm(A_Q_HEADS), hm(A_Q_HEADS), hm(A_Q_HEADS)],
        out_specs=[pl.BlockSpec((WINDOW, A_WIDTH), lambda n: (jnp.minimum(n, last), 0)),
                   pl.BlockSpec((WINDOW, 2 * A_KV_WIDTH), lambda n: (jnp.maximum(n - 1, 0), 0)),
                   pl.BlockSpec((A_Q_HEADS, 8, HEAD_DIM), lambda n: (0, 0, 0))],
        out_shape=[jax.ShapeDtypeStruct((s, A_WIDTH), F32), jax.ShapeDtypeStruct((s, 2 * A_KV_WIDTH), F32),
                   jax.ShapeDtypeStruct((A_Q_HEADS, 8, HEAD_DIM), F32)],
        scratch_shapes=[pltpu.VMEM((2 * A_KV_WIDTH, WINDOW), F32)],
        args=[sinks, slopes, qkv, qkv, qkv, do, lse, dd], sem=("arbitrary",), vmem=VMEM_BIG)
    return outs[0], outs[1], outs[2], comm_outs


def _attn_b_fwd(qkv, c3, name, comm=None):
    heads, s = qkv.shape[0] // 3, qkv.shape[1]
    hpairs = heads // 2
    bq = min(512, s)
    nq = s // bq
    nt = (((1,), (1,)), ((), ()))

    def body(q_ref, k_ref, v_ref, c_ref, o_ref, lse_ref, m_scr, l_scr, acc_scr):
        i = pl.program_id(1)
        r0 = pl.multiple_of(i * bq, bq)
        row = lax.broadcasted_iota(jnp.int32, (bq, bq), 0)
        col = lax.broadcasted_iota(jnp.int32, (bq, bq), 1)
        m_scr[...] = jnp.full((2, bq, LANES), NEG, F32)
        l_scr[...] = jnp.zeros((2, bq, LANES), F32)
        acc_scr[...] = jnp.zeros((2, bq, HEAD_DIM), F32)

        def step(j, masked):
            k0 = pl.multiple_of(j * bq, bq)
            for h2 in range(2):
                kv = k_ref[h2, pl.ds(k0, bq), :]
                vv = v_ref[h2, pl.ds(k0, bq), :]
                cq0 = c_ref[h2, :, pl.ds(r0, LANES)][:, 0:1]
                sc = lax.dot_general(q_ref[h2], kv, nt, preferred_element_type=F32)
                sc = sc + (cq0 - c_ref[h2, :, pl.ds(k0, bq)])
                if masked:
                    sc = jnp.where(col <= row, sc, NEG)
                m_prev = m_scr[h2]
                m_new = jnp.maximum(m_prev, jnp.max(sc, axis=1, keepdims=True))
                alpha = jnp.exp(m_prev - m_new)
                p = jnp.exp(sc - m_new[:, 0:1])
                l_scr[h2] = alpha * l_scr[h2] + jnp.sum(p, axis=1, keepdims=True)
                p_hi = p.astype(BF16)
                p_lo = (p - p_hi.astype(F32)).astype(BF16)
                pv = jnp.dot(p_hi, vv, preferred_element_type=F32) + jnp.dot(p_lo, vv, preferred_element_type=F32)
                acc_scr[h2] = acc_scr[h2] * alpha[:, 0:HEAD_DIM] + pv
                m_scr[h2] = m_new

        def loop_body(j, carry):
            step(j, False)
            return carry

        lax.fori_loop(0, i, loop_body, 0)
        step(i, True)
        outs = []
        for h2 in range(2):
            l = l_scr[h2]
            outs.append(acc_scr[h2] / l[:, 0:HEAD_DIM])
            lse_ref[h2] = (m_scr[h2] + jnp.log(l))[:, 0:HEAD_DIM]
        o_ref[...] = jnp.concatenate(outs, axis=1)

    res = lambda off: pl.BlockSpec((2, s, HEAD_DIM), lambda hp, i: (off + hp, 0, 0))
    outs, comm_outs = _hosted_call(
        body, comm, name=name, grid=(hpairs, nq),
        in_specs=[pl.BlockSpec((2, bq, HEAD_DIM), lambda hp, i: (hp, i, 0)), res(hpairs), res(2 * hpairs),
                  pl.BlockSpec((2, 1, s), lambda hp, i: (hp, 0, 0))],
        out_specs=[pl.BlockSpec((bq, 2 * HEAD_DIM), lambda hp, i: (i, hp)),
                   pl.BlockSpec((2, bq, HEAD_DIM), lambda hp, i: (hp, i, 0))],
        out_shape=[jax.ShapeDtypeStruct((s, heads * HEAD_DIM), F32), jax.ShapeDtypeStruct((heads, s, HEAD_DIM), F32)],
        scratch_shapes=[pltpu.VMEM((2, bq, LANES), F32), pltpu.VMEM((2, bq, LANES), F32), pltpu.VMEM((2, bq, HEAD_DIM), F32)],
        args=[qkv, qkv, qkv, c3], sem=("parallel", "parallel"), vmem=VMEM_BIG)
    return outs[0], outs[1], comm_outs


def _attn_b_bwd(qkv, do, lse, dd, c3, name, comm=None):
    heads, s = qkv.shape[0] // 3, qkv.shape[1]
    hpairs = heads // 2
    bq = min(512, s)
    nq = s // bq
    nt = (((1,), (1,)), ((), ()))
    tn = (((0,), (0,)), ((), ()))
    grid = (heads // 2, nq)

    def body(q_ref, k_ref, v_ref, do_ref, lse_ref, dd_ref, c_ref, dq_ref, dk_ref, dv_ref, dc_ref,
             dq_scr, dk_scr, dv_scr, dc_scr):
        j = pl.program_id(1)
        k0 = pl.multiple_of(j * bq, bq)
        row = lax.broadcasted_iota(jnp.int32, (bq, bq), 0)
        col = lax.broadcasted_iota(jnp.int32, (bq, bq), 1)

        @pl.when(j == 0)
        def _():
            dq_scr[...] = jnp.zeros(dq_scr.shape, F32)

        dk_scr[...] = jnp.zeros((2, HEAD_DIM, bq), F32)
        dv_scr[...] = jnp.zeros((2, HEAD_DIM, bq), F32)
        dc_scr[...] = jnp.zeros((2, 1, bq), F32)
        k_t = [k_ref[h2].T for h2 in range(2)]

        def step(i, masked):
            r0 = pl.multiple_of(i * bq, bq)
            for h2 in range(2):
                kv = k_ref[h2]
                vv = v_ref[h2]
                qv = q_ref[h2, pl.ds(r0, bq), :]
                dov = do_ref[h2, pl.ds(r0, bq), :]
                lse_v = lse_ref[h2, pl.ds(r0, bq), :][:, 0:1]
                dd_v = dd_ref[h2, pl.ds(r0, bq), :][:, 0:1]
                cq0 = c_ref[h2, :, pl.ds(r0, LANES)][:, 0:1]
                sc = lax.dot_general(qv, kv, nt, preferred_element_type=F32) + (cq0 - c_ref[h2, :, pl.ds(k0, bq)])
                if masked:
                    sc = jnp.where(col <= row, sc, NEG)
                p = jnp.exp(sc - lse_v)
                dp = lax.dot_general(dov, vv, nt, preferred_element_type=F32)
                dsc = p * (dp - dd_v)
                dsb = dsc.astype(BF16)
                dv_scr[h2] += jnp.dot(dov.T, p.astype(BF16), preferred_element_type=F32)
                dk_scr[h2] += jnp.dot(qv.T, dsb, preferred_element_type=F32)
                dq_scr[h2, :, pl.ds(r0, bq)] += jnp.dot(k_t[h2], dsb.T, preferred_element_type=F32)
                dc_scr[h2] -= jnp.sum(dsc, axis=0, keepdims=True)

        def loop_body(i, carry):
            step(i, False)
            return carry

        step(j, True)
        lax.fori_loop(j + 1, nq, loop_body, 0)
        dc_ref[...] = dc_scr[...]
        dk_ref[...] = jnp.concatenate([dk_scr[0].T, dk_scr[1].T], axis=1)
        dv_ref[...] = jnp.concatenate([dv_scr[0].T, dv_scr[1].T], axis=1)

        @pl.when(j == nq - 1)
        def _():
            dq_ref[...] = jnp.concatenate([dq_scr[0].T, dq_scr[1].T], axis=1)

    res = pl.BlockSpec((2, s, HEAD_DIM), lambda hp, j: (hp, 0, 0))
    blk = lambda off: pl.BlockSpec((2, bq, HEAD_DIM), lambda hp, j: (off + hp, j, 0))
    tm = jax.ShapeDtypeStruct((s, heads * HEAD_DIM), F32)
    in_specs = [res, blk(hpairs), blk(2 * hpairs), res, res, res, pl.BlockSpec((2, 1, s), lambda hp, j: (hp, 0, 0))]
    out_specs = [pl.BlockSpec((s, 2 * HEAD_DIM), lambda hp, j: (0, hp)),
                 pl.BlockSpec((bq, 2 * HEAD_DIM), lambda hp, j: (j, hp)),
                 pl.BlockSpec((bq, 2 * HEAD_DIM), lambda hp, j: (j, hp)),
                 pl.BlockSpec((2, 1, bq), lambda hp, j: (hp, 0, j))]
    out_shape = [tm, tm, tm, jax.ShapeDtypeStruct((heads, 1, s), F32)]
    scratch = [pltpu.VMEM((2, HEAD_DIM, s), F32), pltpu.VMEM((2, HEAD_DIM, bq), F32),
               pltpu.VMEM((2, HEAD_DIM, bq), F32), pltpu.VMEM((2, 1, bq), F32)]
    outs, comm_outs = _hosted_call(
        body, comm, name=name, grid=grid, in_specs=in_specs, out_specs=out_specs, out_shape=out_shape,
        scratch_shapes=scratch, args=[qkv, qkv, qkv, do, lse, dd, c3], sem=("parallel", "arbitrary"), vmem=VMEM_BIG)
    return outs[0], outs[1], outs[2], outs[3], comm_outs


def _attn_c_probs(qh, mkh):
    sc = lax.dot_general(qh, mkh, (((1,), (1,)), ((), ())), preferred_element_type=F32) * (C_HEAD_DIM ** -0.5)
    p = jnp.exp(sc - jnp.max(sc, axis=1, keepdims=True))
    return p / jnp.sum(p, axis=1, keepdims=True)


def _attn_c_fwd(q, mkv, name):
    s = q.shape[0]
    m = mkv.shape[0]
    bq = _tile(s, 512, 8)

    def body(q_ref, mk_ref, mv_ref, o_ref):
        outs = []
        for h in range(C_HEADS):
            sl = slice(h * C_HEAD_DIM, (h + 1) * C_HEAD_DIM)
            pn = _attn_c_probs(q_ref[:, sl], mk_ref[:, sl]).astype(BF16)
            outs.append(jnp.dot(pn, mv_ref[:, sl], preferred_element_type=F32))
        o_ref[...] = jnp.concatenate(outs, axis=1)

    return pl.pallas_call(
        body, name=name, grid=(s // bq,),
        in_specs=[pl.BlockSpec((bq, C_WIDTH), lambda i: (i, 0)), pl.BlockSpec((m, C_WIDTH), lambda i: (0, 0)),
                  pl.BlockSpec((m, C_WIDTH), lambda i: (0, 1))],
        out_specs=pl.BlockSpec((bq, C_WIDTH), lambda i: (i, 0)),
        out_shape=jax.ShapeDtypeStruct((s, C_WIDTH), F32),
        compiler_params=_params(("parallel",)),
    )(q, mkv, mkv)


def _attn_c_bwd(q, mkv, do, name):
    s = q.shape[0]
    m = mkv.shape[0]
    bq = _tile(s, 512, 8)
    tn = (((0,), (0,)), ((), ()))

    def body(q_ref, mk_ref, mv_ref, do_ref, dq_ref, dm_ref):
        i = pl.program_id(0)

        @pl.when(i == 0)
        def _():
            dm_ref[...] = jnp.zeros(dm_ref.shape, F32)

        dqs = []
        for h in range(C_HEADS):
            sl = slice(h * C_HEAD_DIM, (h + 1) * C_HEAD_DIM)
            qh, mkh, mvh, doh = q_ref[:, sl], mk_ref[:, sl], mv_ref[:, sl], do_ref[:, sl]
            pn = _attn_c_probs(qh, mkh)
            dp = lax.dot_general(doh, mvh, (((1,), (1,)), ((), ())), preferred_element_type=F32)
            dsc = (pn * (dp - jnp.sum(pn * dp, axis=1, keepdims=True)) * (C_HEAD_DIM ** -0.5)).astype(BF16)
            dqs.append(jnp.dot(dsc, mkh, preferred_element_type=F32))
            dm_ref[:, sl] += lax.dot_general(dsc, qh, tn, preferred_element_type=F32)
            sv = slice(C_WIDTH + h * C_HEAD_DIM, C_WIDTH + (h + 1) * C_HEAD_DIM)
            dm_ref[:, sv] += lax.dot_general(pn.astype(BF16), doh, tn, preferred_element_type=F32)
        dq_ref[...] = jnp.concatenate(dqs, axis=1)

    row = pl.BlockSpec((bq, C_WIDTH), lambda i: (i, 0))
    return pl.pallas_call(
        body, name=name, grid=(s // bq,),
        in_specs=[row, pl.BlockSpec((m, C_WIDTH), lambda i: (0, 0)), pl.BlockSpec((m, C_WIDTH), lambda i: (0, 1)), row],
        out_specs=[row, pl.BlockSpec((m, 2 * C_WIDTH), lambda i: (0, 0))],
        out_shape=[jax.ShapeDtypeStruct((s, C_WIDTH), F32), jax.ShapeDtypeStruct((m, 2 * C_WIDTH), F32)],
        compiler_params=_params(("arbitrary",)),
    )(q, mkv, mkv, do)


def _gate_fwd(y, proj, zc0, bw, name):
    rows, width = y.shape
    bm = _tile(rows, 2048 if bw <= 256 else 1024, 16)
    cb0 = zc0 // bw

    def body(y_ref, z_ref, o_ref):
        z = z_ref[...].astype(F32)
        o_ref[...] = (y_ref[...] * (z * _sigmoid(z))).astype(BF16)

    return pl.pallas_call(
        body, name=name, grid=(rows // bm, width // bw),
        in_specs=[pl.BlockSpec((bm, bw), lambda i, t: (i, t)), pl.BlockSpec((bm, bw), lambda i, t: (i, cb0 + t))],
        out_specs=pl.BlockSpec((bm, bw), lambda i, t: (i, t)),
        out_shape=jax.ShapeDtypeStruct((rows, width), BF16),
        compiler_params=_params(("parallel", "parallel")),
    )(y, proj)


def _gate_bwd(dsv, y, proj, zc0, bw, dproj, t0, head_major, name):
    rows, width = y.shape
    bm = _tile(rows, 2048 if bw <= 256 else 1024, 16)
    cb0 = zc0 // bw
    tb0 = t0 // bw
    bd = _block_diag(HEAD_DIM)
    hpb = bw // HEAD_DIM

    def body(*refs):
        if head_major:
            ds_ref, y_ref, z_ref, bd_ref, _, dp_ref, dy_ref, dd_ref = refs
        else:
            ds_ref, y_ref, z_ref, _, dp_ref, dy_ref = refs
        z = z_ref[...].astype(F32)
        sig = _sigmoid(z)
        dsx = ds_ref[...]
        yv = y_ref[...]
        dy = dsx * (z * sig)
        dp_ref[...] = (dsx * yv * (sig * (1.0 + z * (1.0 - sig)))).astype(BF16)
        if head_major:
            dyb = dy.astype(BF16)
            dd = _seg_sum(dyb.astype(F32) * yv, bd_ref[...])
            for h in range(hpb):
                sl = slice(h * HEAD_DIM, (h + 1) * HEAD_DIM)
                dy_ref[h] = dyb[:, sl]
                dd_ref[h] = dd[:, sl]
        else:
            dy_ref[...] = dy.astype(BF16)

    tile = pl.BlockSpec((bm, bw), lambda i, t: (i, t))
    ztile = pl.BlockSpec((bm, bw), lambda i, t: (i, cb0 + t))
    ttile = pl.BlockSpec((bm, bw), lambda i, t: (i, tb0 + t))
    any_spec = pl.BlockSpec(memory_space=pl.ANY)
    dp_shape = jax.ShapeDtypeStruct(dproj.shape, BF16)
    if head_major:
        hm_spec = pl.BlockSpec((hpb, bm, HEAD_DIM), lambda i, t: (t, i, 0))
        nh = width // HEAD_DIM
        outs = pl.pallas_call(
            body, name=name, grid=(rows // bm, width // bw),
            in_specs=[tile, tile, ztile, pl.BlockSpec((LANES, LANES), lambda i, t: (0, 0)), any_spec],
            out_specs=[ttile, hm_spec, hm_spec],
            out_shape=[dp_shape, jax.ShapeDtypeStruct((nh, rows, HEAD_DIM), BF16),
                       jax.ShapeDtypeStruct((nh, rows, HEAD_DIM), F32)],
            input_output_aliases={4: 0},
            compiler_params=_params(("parallel", "parallel")),
        )(dsv, y, proj, bd, dproj)
        return outs[0], outs[1], outs[2]
    outs = pl.pallas_call(
        body, name=name, grid=(rows // bm, width // bw),
        in_specs=[tile, tile, ztile, any_spec],
        out_specs=[ttile, tile],
        out_shape=[dp_shape, jax.ShapeDtypeStruct((rows, width), BF16)],
        input_output_aliases={3: 0},
        compiler_params=_params(("parallel", "parallel")),
    )(dsv, y, proj, dproj)
    return outs[0], outs[1], None


def _merge_fwd(proj, ua, ub, uc, name):
    rows, d = ua.shape
    bm = _tile(rows, 1024, 16)
    bw = _tile(d, 512)
    g0 = COL_GATE // bw
    gstep = d // bw

    def body(la_ref, lb_ref, lc_ref, ua_ref, ub_ref, uc_ref, o_ref, ga_ref, gb_ref, gc_ref):
        y = None
        for l_ref, u_ref, g_ref in ((la_ref, ua_ref, ga_ref), (lb_ref, ub_ref, gb_ref), (lc_ref, uc_ref, gc_ref)):
            g = _sigmoid(l_ref[...].astype(F32))
            g_ref[...] = g.astype(BF16)
            term = g * u_ref[...].astype(F32)
            y = term if y is None else y + term
        o_ref[...] = y.astype(BF16)

    tile = pl.BlockSpec((bm, bw), lambda i, t: (i, t))
    gate = lambda b: pl.BlockSpec((bm, bw), lambda i, t: (i, g0 + b * gstep + t))
    shape = jax.ShapeDtypeStruct((rows, d), BF16)
    return pl.pallas_call(
        body, name=name, grid=(rows // bm, d // bw),
        in_specs=[gate(0), gate(1), gate(2), tile, tile, tile],
        out_specs=[tile] * 4, out_shape=[shape] * 4,
        compiler_params=_params(("parallel", "parallel")),
    )(proj, proj, proj, ua, ub, uc)


def _merge_bwd(dym, us, gs, name):
    rows, d = dym.shape
    bm = _tile(rows, 256, 16)

    def body(dy_ref, ua_ref, ub_ref, uc_ref, ga_ref, gb_ref, gc_ref, dg_ref, da_ref, db_ref, dc_ref):
        dyv = dy_ref[...]
        for b, (u_ref, g_ref, du_ref) in enumerate(((ua_ref, ga_ref, da_ref), (ub_ref, gb_ref, db_ref), (uc_ref, gc_ref, dc_ref))):
            g = g_ref[...].astype(F32)
            du_ref[...] = (g * dyv).astype(BF16)
            dg_ref[:, b * d:(b + 1) * d] = (dyv * u_ref[...].astype(F32) * g * (1.0 - g)).astype(BF16)

    tile = pl.BlockSpec((bm, d), lambda i: (i, 0))
    shape = jax.ShapeDtypeStruct((rows, d), BF16)
    outs = pl.pallas_call(
        body, name=name, grid=(rows // bm,),
        in_specs=[tile] * 7,
        out_specs=[pl.BlockSpec((bm, 3 * d), lambda i: (i, 0)), tile, tile, tile],
        out_shape=[jax.ShapeDtypeStruct((rows, 3 * d), BF16), shape, shape, shape],
        compiler_params=_params(("parallel",), VMEM_BIG),
    )(dym, *us, *gs)
    return outs[0], outs[1], outs[2], outs[3]


def _out_proj_loss(ym, wo, x, target, name):
    m, d = x.shape
    bm, bn = _tile(m, 1024, 16), _tile(d, 1024)
    grid = (m // bm, d // bn)

    def body(a_ref, b_ref, x_ref, t_ref, dy_ref, dyb_ref, l_ref):
        first, _ = _grid_edges(grid)
        y = jnp.dot(a_ref[...], b_ref[...], preferred_element_type=F32) + x_ref[...]
        diff = y - t_ref[...]
        dy = diff * (1.0 / d)
        dy_ref[...] = dy
        dyb_ref[...] = dy.astype(BF16)
        sq = diff * diff
        part = sq[:, 0:LANES]
        for c in range(1, bn // LANES):
            part = part + sq[:, c * LANES:(c + 1) * LANES]
        part = jnp.sum(part.reshape(bm // 8, 8, LANES), axis=0)

        @pl.when(first)
        def _():
            l_ref[...] = part

        @pl.when(jnp.logical_not(first))
        def _():
            l_ref[...] += part

    tile = pl.BlockSpec((bm, bn), lambda i, j: (i, j))
    return pl.pallas_call(
        body, name=name, grid=grid,
        in_specs=[pl.BlockSpec((bm, d), lambda i, j: (i, 0)), pl.BlockSpec((d, bn), lambda i, j: (0, j)), tile, tile],
        out_specs=[tile, tile, pl.BlockSpec((8, LANES), lambda i, j: (0, 0))],
        out_shape=[jax.ShapeDtypeStruct((m, d), F32), jax.ShapeDtypeStruct((m, d), BF16),
                   jax.ShapeDtypeStruct((8, LANES), F32)],
        compiler_params=_params(("arbitrary", "arbitrary"), VMEM_BIG),
    )(ym, wo, x, target)


def _row(vec, reps=1):
    return jnp.tile(vec.reshape(1, -1).astype(F32), (1, reps))


def _local_step(x, mem, target, small, wg, shards=None):
    s, d = x.shape
    dist = shards is not None
    wg = dict(wg)
    ones = lambda n: jnp.ones((1, n), F32)
    zeros = lambda n: jnp.zeros((1, n), F32)
    scale_ab = HEAD_DIM ** -0.5
    split8 = lambda g: g.reshape(N_DEV, g.shape[0] // N_DEV, g.shape[1])
    flat8 = lambda g: g.reshape(g.shape[0] * g.shape[1], g.shape[2])
    gather = lambda names: _Comm("gather", [shards[n] for n in names]) if dist else None
    g = {}

    def scatter(names):
        return _Comm("scatter", [split8(g[n]) for n in names]) if dist else None

    def hosted(result, names, store):
        if not dist:
            return result
        out, got = result
        store.update(zip(names, got))
        return out

    hn = _rmsnorm_fwd(x, small["norm_gain"], "rms_x_fwd")
    got = {}
    proj = hosted(_mm_nn(hn, wg["qkv"], bm=1024, bn=1024, bk=d, o_dtype=BF16, name="proj_qkv",
                         comm=gather(("wa", "wb"))), ("wa", "wb"), got)
    wg.update({n: flat8(a) for n, a in got.items()})
    pfb = _mm_nn(hn, wg["wf"], bm=1024, bn=FB_PAD, bk=d, o_dtype=F32, name="proj_fb")
    mn = _rmsnorm_fwd(mem, small["mem_norm_gain"], "rms_mem_fwd")
    mkv = _mm_nn(mn, wg["wk"], bm=256, bn=1024, bk=d, o_dtype=F32, name="mem_kv")

    gain_a = jnp.concatenate([_row(small["q_gain_a"], A_Q_HEADS) * scale_ab, _row(small["k_gain_a"], A_KV_HEADS), ones(A_KV_WIDTH)], axis=1)
    flag_a = jnp.concatenate([ones(A_WIDTH + A_KV_WIDTH), zeros(A_KV_WIDTH)], axis=1)
    qkv_a = _headnorm_fwd(proj, COL_QA, 1280, 1280, HEAD_DIM, gain_a, flag_a, True, "hn_a_fwd")
    gain_b = jnp.concatenate([_row(small["q_gain_b"], B_HEADS) * scale_ab, _row(small["k_gain_b"], B_HEADS), ones(B_WIDTH)], axis=1)
    flag_b = jnp.concatenate([ones(2 * B_WIDTH), zeros(B_WIDTH)], axis=1)
    qkv_b = _headnorm_fwd(proj, COL_QB, 2304, 256, HEAD_DIM, gain_b, flag_b, True, "hn_b_fwd")
    gain_cq = _row(small["q_gain_c"], C_HEADS)
    q_c = _headnorm_fwd(proj, COL_QC, C_WIDTH, C_WIDTH, C_HEAD_DIM, gain_cq, ones(C_WIDTH), False, "hn_cq_fwd")
    gain_ck = jnp.concatenate([_row(small["k_gain_c"], C_HEADS), ones(C_WIDTH)], axis=1)
    flag_ck = jnp.concatenate([ones(C_WIDTH), zeros(C_WIDTH)], axis=1)
    mkvn = _headnorm_fwd(mkv, 0, 2 * C_WIDTH, 2 * C_WIDTH, C_HEAD_DIM, gain_ck, flag_ck, False, "hn_ck_fwd")


    bpad = jnp.pad(small["b_forget"].reshape(1, -1), ((0, 0), (0, FB_PAD - B_HEADS)))
    c16 = _fox_prep(pfb, bpad, "fox_prep")
    c3 = c16[0:B_HEADS].reshape(B_HEADS, 1, s)

    sinks = small["sinks_a"].reshape(-1)
    slopes = jnp.exp2(-8.0 * jnp.arange(1, A_Q_HEADS + 1, dtype=F32) / A_Q_HEADS)
    y_a, lse_a = _attn_a_fwd(qkv_a, sinks, slopes, "attn_a_fwd")
    y_b, lse_b, got_zg = _attn_b_fwd(qkv_b, c3, "attn_b_fwd", comm=gather(("zg",)))
    if dist:
        wg["zg"] = flat8(got_zg[0])
    y_c = _attn_c_fwd(q_c, mkvn, "attn_c_fwd")

    got = {}
    pzg = hosted(_mm_nn(hn, wg["zg"], bm=1024, bn=1024, bk=d, o_dtype=BF16, name="proj_zg", comm=gather(("wo", "wc"))),
                 ("wo", "wc"), got)
    wg.update({n: flat8(a) for n, a in got.items()})

    s_a = _gate_fwd(y_a, pzg, COL_ZA, 256, "gate_a_fwd")
    s_b = _gate_fwd(y_b, pzg, COL_ZB, 256, "gate_b_fwd")
    s_c = _gate_fwd(y_c, pzg, COL_ZC, 512, "gate_c_fwd")
    w_a, w_b, w_c = _branch_full(wg["wa"]), _branch_full(wg["wb"]), _branch_full(wg["wc"])
    u_a = _mm_nn(s_a, w_a, bm=1024, bn=2048, bk=A_WIDTH, o_dtype=BF16, name="branch_a_fwd")
    u_b = _mm_nn(s_b, w_b, bm=1024, bn=2048, bk=B_WIDTH, o_dtype=BF16, name="branch_b_fwd")
    u_c = _mm_nn(s_c, w_c, bm=1024, bn=2048, bk=C_WIDTH, o_dtype=BF16, name="branch_c_fwd")
    ym, gate_a, gate_b, gate_c = _merge_fwd(pzg, u_a, u_b, u_c, "merge_fwd")
    dy, dyb, lpart = _out_proj_loss(ym, wg["wo"], x, target, "out_proj_loss")
    loss = 0.5 / d * jnp.sum(lpart)

    dym = _mm_nt(dyb, wg["wo"], bm=1024, bn=1024, bk=d, o_dtype=F32, name="out_proj_bwd_act")
    g["wo"] = _mm_tn(ym, dyb, bm=512, bn=1024, bk=s, o_dtype=BF16, name="out_proj_bwd_w")

    dgate, du_a, du_b, du_c = _merge_bwd(dym, (u_a, u_b, u_c), (gate_a, gate_b, gate_c), "merge_bwd")
    parts = {}
    g["wm_g"] = hosted(_mm_tn(hn, dgate, bm=512, bn=1024, bk=s, o_dtype=BF16, name="proj_gate_bwd_w",
                              comm=scatter(("wo",))), ("wo",), parts)

    ds_a = _mm_nt(du_a, w_a, bm=1024, bn=A_WIDTH, bk=d, o_dtype=F32, name="branch_a_bwd_act")
    ds_b = _mm_nt(du_b, w_b, bm=1024, bn=B_WIDTH, bk=d, o_dtype=F32, name="branch_b_bwd_act")
    ds_c = _mm_nt(du_c, w_c, bm=1024, bn=C_WIDTH, bk=d, o_dtype=F32, name="branch_c_bwd_act")
    g["wa"] = _branch_shards(_mm_tn(s_a, du_a, bm=A_WIDTH, bn=1024, bk=s, o_dtype=BF16, name="branch_a_bwd_w"))
    g["wb"] = _branch_shards(_mm_tn(s_b, du_b, bm=B_WIDTH, bn=1024, bk=s, o_dtype=BF16, name="branch_b_bwd_w"))
    g["wc"] = _branch_shards(_mm_tn(s_c, du_c, bm=C_WIDTH, bn=1024, bk=s, o_dtype=BF16, name="branch_c_bwd_w"))

    dz = lax.empty((s, W_Z), BF16)
    dz, do_a, dd_a = _gate_bwd(ds_a, y_a, pzg, COL_ZA, 256, dz, COL_ZA, True, "gate_a_bwd")
    dz, do_b, dd_b = _gate_bwd(ds_b, y_b, pzg, COL_ZB, 256, dz, COL_ZB, True, "gate_b_bwd")
    dz, do_c, _ = _gate_bwd(ds_c, y_c, pzg, COL_ZC, 512, dz, COL_ZC, False, "gate_c_bwd")
    g["wm_z"] = _mm_tn(hn, dz, bm=512, bn=1024, bk=s, o_dtype=BF16, name="proj_z_bwd_w")

    names = ("wa", "wb", "wc")
    dq_a, dkv_a, dsink, got = _attn_a_bwd(qkv_a, do_a, lse_a, dd_a, sinks, slopes, "attn_a_bwd", comm=scatter(names))
    parts.update(zip(names, got))
    names = ("wm_g", "wm_z")
    dq_b, dk_b, dv_b, dc3, got = _attn_b_bwd(qkv_b, do_b, lse_b, dd_b, c3, "attn_b_bwd", comm=scatter(names))
    parts.update(zip(names, got))
    dq_c, dmkvn = _attn_c_bwd(q_c, mkvn, do_c, "attn_c_bwd")

    dqkv = lax.empty((s, W_QKV), BF16)
    dqkv, dg_qa = _headnorm_bwd(proj, COL_QA, A_WIDTH, 256, HEAD_DIM, gain_a[:, 0:768], flag_a[:, 0:768], dq_a, dqkv, COL_QA, "hn_qa_bwd")
    dqkv, dg_kva = _headnorm_bwd(proj, COL_KA, 512, 256, HEAD_DIM, gain_a[:, 768:1280], flag_a[:, 768:1280], dkv_a, dqkv, COL_KA, "hn_kva_bwd")
    dqkv, dg_qb = _headnorm_bwd(proj, COL_QB, B_WIDTH, 256, HEAD_DIM, gain_b[:, 0:768], flag_b[:, 0:768], dq_b, dqkv, COL_QB, "hn_qb_bwd")
    dqkv, dg_kb = _headnorm_bwd(proj, COL_KB, B_WIDTH, 256, HEAD_DIM, gain_b[:, 768:1536], flag_b[:, 768:1536], dk_b, dqkv, COL_KB, "hn_kb_bwd")
    dqkv, _ = _headnorm_bwd(proj, COL_VB, B_WIDTH, 256, HEAD_DIM, gain_b[:, 1536:2304], flag_b[:, 1536:2304], dv_b, dqkv, COL_VB, "hn_vb_bwd")
    dqkv, dg_qc = _headnorm_bwd(proj, COL_QC, C_WIDTH, 512, C_HEAD_DIM, gain_cq, ones(C_WIDTH), dq_c, dqkv, COL_QC, "hn_qc_bwd")
    dmkv, dg_kc = _headnorm_bwd(mkv, 0, 2 * C_WIDTH, 2 * C_WIDTH, C_HEAD_DIM, gain_ck, flag_ck, dmkvn, None, 0, "hn_kc_bwd")

    dct = jnp.pad(dc3.reshape(B_HEADS, s), ((0, 16 - B_HEADS), (0, 0)))
    dfb, dbf = _fox_prep_bwd(pfb, bpad, dct, "fox_prep_bwd")

    dmn = _mm_nt(dmkv, wg["wk"], bm=256, bn=1024, bk=1024, o_dtype=F32, name="mem_kv_bwd_act")
    g["wk"] = _mm_tn(mn, dmkv, bm=512, bn=1024, bk=mem.shape[0], o_dtype=BF16, name="mem_kv_bwd_w")
    _, dg_mem = _rmsnorm_bwd(mem, dmn, small["mem_norm_gain"], None, "rms_mem_bwd")

    g["wm_qkv"] = _mm_tn(hn, dqkv, bm=512, bn=1024, bk=s, o_dtype=BF16, name="proj_qkv_bwd_w")
    g["wf"] = _mm_tn(hn, dfb, bm=512, bn=FB_PAD, bk=s, o_dtype=BF16, name="proj_fb_bwd_w")
    half = Q_SPLIT
    g["wm_q1"], g["wm_q2"] = g["wm_qkv"][:, 0:half], g["wm_qkv"][:, half:W_QKV]
    names = ("wm_q1",)
    dhn = hosted(_mm_nt_sum([(dqkv, wg["qkv"], 0), (dfb, wg["wf"], 0)], bm=1024, bn=1024, bk=2048,
                            name="proj_qkv_bwd_act", comm=scatter(names)), names, parts)
    names = ("wm_q2", "wf", "wk")
    dhn = hosted(_mm_nt_sum([(dz, wg["zg"], COL_ZA), (dgate, wg["zg"], COL_GATE)], bm=1024, bn=1024, bk=2048,
                            name="proj_zg_bwd_act", add=dhn, comm=scatter(names)), names, parts)
    if dist:
        g = parts
    grad_x, dg_x = _rmsnorm_bwd(x, dhn, small["norm_gain"], dy, "rms_x_bwd")

    fold = lambda part, heads, hd: jnp.sum(jnp.sum(part, axis=0).reshape(heads, hd), axis=0).reshape(1, hd)
    small_grads = {
        "norm_gain": jnp.sum(dg_x, axis=0).reshape(1, d),
        "mem_norm_gain": jnp.sum(dg_mem, axis=0).reshape(1, d),
        "b_forget": dbf[0:B_HEADS, 0].reshape(1, B_HEADS),
        "q_gain_a": fold(dg_qa, A_Q_HEADS, HEAD_DIM) * scale_ab,
        "k_gain_a": fold(dg_kva[:, 0:A_KV_WIDTH], A_KV_HEADS, HEAD_DIM),
        "sinks_a": (jnp.sum(dsink, axis=(1, 2)) * (1.0 / HEAD_DIM)).reshape(1, A_Q_HEADS),
        "q_gain_b": fold(dg_qb, B_HEADS, HEAD_DIM) * scale_ab,
        "k_gain_b": fold(dg_kb, B_HEADS, HEAD_DIM),
        "q_gain_c": fold(dg_qc, C_HEADS, C_HEAD_DIM),
        "k_gain_c": fold(dg_kc[:, 0:C_WIDTH], C_HEADS, C_HEAD_DIM),
    }
    return loss, grad_x, small_grads, g


def _coords():
    return lax.axis_index("x"), lax.axis_index("y"), lax.axis_index("c")


def _all_gather(shards, name):
    n = len(shards)

    def body(*refs):
        ins = refs[0:n]
        outs = refs[n:2 * n]
        send_sems, recv_sems, local_sems = refs[2 * n:2 * n + 3]
        x, y, c = _coords()
        me, sibling = (x, y, c), (x, y, 1 - c)
        chips = [(1 - x, y), (x, 1 - y), (1 - x, 1 - y)]
        idx = lambda p: 4 * p[0] + 2 * p[1] + p[2]

        def copy(a, k, block, to, src=None):
            slot = outs[a].at[idx(block)]
            return pltpu.make_async_remote_copy(
                src_ref=slot if src is None else src, dst_ref=slot,
                send_sem=send_sems.at[a, k], recv_sem=recv_sems.at[a, k], device_id=to, device_id_type=MESH)

        mine = [pltpu.make_async_copy(ins[a], outs[a].at[idx(me)], local_sems.at[a]) for a in range(n)]
        for cp in mine:
            cp.start()
        first = []
        for a in range(n):
            first.append(copy(a, 0, me, sibling, src=ins[a]))
            first += [copy(a, 1 + j, me, (*chip, c), src=ins[a]) for j, chip in enumerate(chips)]
        for cp in first:
            cp.start()
        passed = []
        for j, chip in enumerate(chips):
            for a in range(n):
                copy(a, 1 + j, (*chip, c), me).wait_recv()
                fwd = copy(a, 4 + j, (*chip, c), sibling)
                fwd.start()
                passed.append(fwd)
        for a in range(n):
            copy(a, 0, sibling, me).wait_recv()
            for j, chip in enumerate(chips):
                copy(a, 4 + j, (*chip, 1 - c), me).wait_recv()
        for cp in first + passed:
            cp.wait_send()
        for cp in mine:
            cp.wait()

    any_spec = pl.BlockSpec(memory_space=pl.ANY)
    return pl.pallas_call(
        body, name=name,
        in_specs=[any_spec] * n, out_specs=[any_spec] * n,
        out_shape=[jax.ShapeDtypeStruct((N_DEV,) + sh.shape, sh.dtype) for sh in shards],
        scratch_shapes=[pltpu.SemaphoreType.DMA((n, 7)), pltpu.SemaphoreType.DMA((n, 7)), pltpu.SemaphoreType.DMA((n,))],
    )(*shards)


def _sum_parts(parts, name):
    _, rows, cols = parts.shape
    br = _tile(rows, 64, 16)

    def body(p_ref, o_ref):
        total = p_ref[0].astype(F32)
        for j in range(1, N_DEV):
            total = total + p_ref[j].astype(F32)
        o_ref[...] = total

    return pl.pallas_call(
        body, name=name, grid=(rows // br,),
        in_specs=[pl.BlockSpec((N_DEV, br, cols), lambda i: (0, i, 0))],
        out_specs=pl.BlockSpec((br, cols), lambda i: (i, 0)),
        out_shape=jax.ShapeDtypeStruct((rows, cols), F32),
        compiler_params=_params(("parallel",), VMEM_BIG),
    )(parts)


def _adamw(w, g, m, v, name, br=32):
    rows, cols = w.shape
    br = min(br, rows)
    c1 = 1.0 / (1.0 - ADAM_B1 ** ADAM_STEP)
    c2 = 1.0 / (1.0 - ADAM_B2 ** ADAM_STEP)

    def body(w_ref, g_ref, m_ref, v_ref, d_ref, nm_ref, nv_ref):
        gv = g_ref[...]
        nm = ADAM_B1 * m_ref[...] + (1.0 - ADAM_B1) * gv
        nv = ADAM_B2 * v_ref[...] + (1.0 - ADAM_B2) * (gv * gv)
        d_ref[...] = -ADAM_LR * ((nm * c1) / (jnp.sqrt(nv * c2) + ADAM_EPS) + ADAM_WD * w_ref[...])
        nm_ref[...] = nm
        nv_ref[...] = nv

    spec = pl.BlockSpec((br, cols), lambda i: (i, 0))
    shape = jax.ShapeDtypeStruct((rows, cols), F32)
    return pl.pallas_call(
        body, name=name, grid=(pl.cdiv(rows, br),), in_specs=[spec] * 4, out_specs=[spec] * 3, out_shape=[shape] * 3,
        compiler_params=_params(("parallel",), VMEM_BIG),
    )(w, g, m, v)


def _adamw_t(wt, g, mt, vt, name, br=1024, comm=None):
    n, r = wt.shape
    c1 = 1.0 / (1.0 - ADAM_B1 ** ADAM_STEP)
    c2 = 1.0 / (1.0 - ADAM_B2 ** ADAM_STEP)

    def body(w_ref, g_ref, m_ref, v_ref, d_ref, nm_ref, nv_ref):
        gv = g_ref[...].T
        nm = ADAM_B1 * m_ref[...] + (1.0 - ADAM_B1) * gv
        nv = ADAM_B2 * v_ref[...] + (1.0 - ADAM_B2) * (gv * gv)
        d_ref[...] = -ADAM_LR * ((nm * c1) / (jnp.sqrt(nv * c2) + ADAM_EPS) + ADAM_WD * w_ref[...])
        nm_ref[...] = nm
        nv_ref[...] = nv

    spec = pl.BlockSpec((br, r), lambda i: (i, 0))
    shape = jax.ShapeDtypeStruct((n, r), F32)
    return _hosted_call(
        body, comm, name=name, grid=(pl.cdiv(n, br),),
        in_specs=[spec, pl.BlockSpec((r, br), lambda i: (0, i)), spec, spec], out_specs=[spec] * 3, out_shape=[shape] * 3,
        scratch_shapes=[], args=[wt, g, mt, vt], sem=("parallel",), vmem=VMEM_BIG)


def _adamw_parts(w, parts, m, v, name):
    rows, cols = w.shape
    br = _tile(rows, 32, 16)
    c1 = 1.0 / (1.0 - ADAM_B1 ** ADAM_STEP)
    c2 = 1.0 / (1.0 - ADAM_B2 ** ADAM_STEP)

    def body(w_ref, p_ref, m_ref, v_ref, g_ref, d_ref, nm_ref, nv_ref):
        gv = p_ref[0].astype(F32)
        for j in range(1, N_DEV):
            gv = gv + p_ref[j].astype(F32)
        nm = ADAM_B1 * m_ref[...] + (1.0 - ADAM_B1) * gv
        nv = ADAM_B2 * v_ref[...] + (1.0 - ADAM_B2) * (gv * gv)
        g_ref[...] = gv
        d_ref[...] = -ADAM_LR * ((nm * c1) / (jnp.sqrt(nv * c2) + ADAM_EPS) + ADAM_WD * w_ref[...])
        nm_ref[...] = nm
        nv_ref[...] = nv

    spec = pl.BlockSpec((br, cols), lambda i: (i, 0))
    shape = jax.ShapeDtypeStruct((rows, cols), F32)
    return pl.pallas_call(
        body, name=name, grid=(rows // br,),
        in_specs=[spec, pl.BlockSpec((N_DEV, br, cols), lambda i: (0, i, 0)), spec, spec],
        out_specs=[spec] * 4, out_shape=[shape] * 4,
        compiler_params=_params(("parallel",), VMEM_BIG),
    )(w, parts, m, v)


SMALL_NAMES = ("norm_gain", "mem_norm_gain", "b_forget", "q_gain_a", "k_gain_a", "sinks_a",
               "q_gain_b", "k_gain_b", "q_gain_c", "k_gain_c")
BIG_NAMES = ("w_in", "w_mem_kv", "w_branch_a", "w_branch_b", "w_branch_c", "w_out")
WEIGHT_ORDER = ("norm_gain", "mem_norm_gain", "w_in", "b_forget", "q_gain_a", "k_gain_a", "sinks_a", "q_gain_b",
                "k_gain_b", "q_gain_c", "k_gain_c", "w_mem_kv", "w_branch_a", "w_branch_b", "w_branch_c", "w_out")


def _pack_small(tree):
    flat = jnp.concatenate([tree[n].reshape(1, -1) for n in SMALL_NAMES], axis=1)
    pad = (-flat.shape[1]) % LANES
    return jnp.pad(flat, ((0, 0), (0, pad)))


def _unpack_small(flat, like):
    out, off = {}, 0
    for n in SMALL_NAMES:
        size = like[n].size
        out[n] = flat[:, off:off + size].reshape(like[n].shape)
        off += size
    return out


def kernel(x, mem, norm_gain, mem_norm_gain, w_in, b_forget, q_gain_a, k_gain_a, sinks_a, q_gain_b, k_gain_b, q_gain_c, k_gain_c, w_mem_kv, w_branch_a, w_branch_b, w_branch_c, w_out, loss_target, m_norm_gain, m_mem_norm_gain, m_w_in, m_b_forget, m_q_gain_a, m_k_gain_a, m_sinks_a, m_q_gain_b, m_k_gain_b, m_q_gain_c, m_k_gain_c, m_w_mem_kv, m_w_branch_a, m_w_branch_b, m_w_branch_c, m_w_out, v_norm_gain, v_mem_norm_gain, v_w_in, v_b_forget, v_q_gain_a, v_k_gain_a, v_sinks_a, v_q_gain_b, v_k_gain_b, v_q_gain_c, v_k_gain_c, v_w_mem_kv, v_w_branch_a, v_w_branch_b, v_w_branch_c, v_w_out):
    weights = dict(norm_gain=norm_gain, mem_norm_gain=mem_norm_gain, w_in=w_in, b_forget=b_forget, q_gain_a=q_gain_a,
                   k_gain_a=k_gain_a, sinks_a=sinks_a, q_gain_b=q_gain_b, k_gain_b=k_gain_b, q_gain_c=q_gain_c,
                   k_gain_c=k_gain_c, w_mem_kv=w_mem_kv, w_branch_a=w_branch_a, w_branch_b=w_branch_b,
                   w_branch_c=w_branch_c, w_out=w_out)
    mom_m = dict(norm_gain=m_norm_gain, mem_norm_gain=m_mem_norm_gain, w_in=m_w_in, b_forget=m_b_forget,
                 q_gain_a=m_q_gain_a, k_gain_a=m_k_gain_a, sinks_a=m_sinks_a, q_gain_b=m_q_gain_b, k_gain_b=m_k_gain_b,
                 q_gain_c=m_q_gain_c, k_gain_c=m_k_gain_c, w_mem_kv=m_w_mem_kv, w_branch_a=m_w_branch_a,
                 w_branch_b=m_w_branch_b, w_branch_c=m_w_branch_c, w_out=m_w_out)
    mom_v = dict(norm_gain=v_norm_gain, mem_norm_gain=v_mem_norm_gain, w_in=v_w_in, b_forget=v_b_forget,
                 q_gain_a=v_q_gain_a, k_gain_a=v_k_gain_a, sinks_a=v_sinks_a, q_gain_b=v_q_gain_b, k_gain_b=v_k_gain_b,
                 q_gain_c=v_q_gain_c, k_gain_c=v_k_gain_c, w_mem_kv=v_w_mem_kv, w_branch_a=v_w_branch_a,
                 w_branch_b=v_w_branch_b, w_branch_c=v_w_branch_c, w_out=v_w_out)
    wi = w_in[0]
    sh_qkv = jnp.concatenate([wi[:, a:b] for a, b in SRC_RANGES[0:3]], axis=1).astype(BF16)
    sh_zg = jnp.concatenate([wi[:, a:b] for a, b in SRC_RANGES[3:6]] + [wi[:, SRC_GATE:]], axis=1).astype(BF16)
    sh_wf = jnp.pad(wi[:, FB_SRC:FB_SRC + B_HEADS], ((0, 0), (0, FB_PAD - B_HEADS))).astype(BF16)
    shards = {"zg": sh_zg, "wo": w_out[0].astype(BF16), "wa": w_branch_a[0].astype(BF16),
              "wb": w_branch_b[0].astype(BF16), "wc": w_branch_c[0].astype(BF16)}
    first = ("qkv", "wf", "wk")
    full = _all_gather([sh_qkv, sh_wf, w_mem_kv[0].astype(BF16)], "weights_all_gather")
    wg = {kname: arr.reshape(arr.shape[0] * arr.shape[1], arr.shape[2]) for kname, arr in zip(first, full)}

    small = {n: weights[n] for n in SMALL_NAMES}
    loss_local, grad_x, small_g, parts = _local_step(x[0], mem[0], loss_target[0], small, wg, shards)

    grads, delta, new_m, new_v = {}, {}, {}, {}
    g1, g2, gz, gf, gg = (_sum_parts(parts[k], "grad_sum_" + k) for k in ("wm_q1", "wm_q2", "wm_z", "wf", "wm_g"))
    half = Q_SPLIT
    g_in = jnp.concatenate([g1, g2[:, 0:COL_QB - half], gz[:, COL_ZA:COL_ZB], g2[:, COL_QB - half:COL_QC - half],
                            gz[:, COL_ZB:COL_ZC], gf[:, 0:B_HEADS], g2[:, COL_QC - half:W_QKV - half], gz[:, COL_ZC:W_Z], gg], axis=1)
    packed = _pack_small(small_g)
    packed = jnp.concatenate([packed[:, :-1], loss_local.reshape(1, 1)], axis=1)
    all_small = _Comm("gather", [jnp.broadcast_to(packed, (8, packed.shape[1]))])
    (dlt, nm, nv), (packed8,) = _adamw_t(w_in[0].T, g_in, m_w_in[0].T, v_w_in[0].T, "adamw_w_in", comm=all_small)
    others = ("wk", "wo", "wa", "wb", "wc")
    (dlt, nm, nv), held = lax.optimization_barrier(((dlt, nm, nv), [parts[k] for k in others]))
    parts.update(zip(others, held))
    grads["w_in"], delta["w_in"], new_m["w_in"], new_v["w_in"] = g_in, dlt.T[None], nm.T[None], nv.T[None]
    reduced = _sum_parts(packed8, "small_sum")[0:1]
    grads.update(_unpack_small(reduced, small))
    loss = reduced[0, -1]
    for n, kname in (("w_mem_kv", "wk"), ("w_out", "wo"), ("w_branch_a", "wa"), ("w_branch_b", "wb"), ("w_branch_c", "wc")):
        gsum, dlt, nm, nv = _adamw_parts(weights[n][0], parts[kname], mom_m[n][0], mom_v[n][0], "adamw_" + n)
        grads[n], delta[n], new_m[n], new_v[n] = gsum, dlt[None], nm[None], nv[None]

    pw, pm, pv = _pack_small(small), _pack_small({n: mom_m[n] for n in SMALL_NAMES}), _pack_small({n: mom_v[n] for n in SMALL_NAMES})
    rep8 = lambda a: jnp.broadcast_to(a, (8, a.shape[1]))
    dlt, nm, nv = _adamw(rep8(pw), rep8(reduced), rep8(pm), rep8(pv), "adamw_small")
    for tree, flat in ((delta, dlt), (new_m, nm), (new_v, nv)):
        tree.update(_unpack_small(flat[0:1], small))
    for n in BIG_NAMES:
        grads[n] = grads[n][None]
    return (loss, grad_x[None], *[grads[n] for n in WEIGHT_ORDER], *[delta[n] for n in WEIGHT_ORDER],
            *[new_m[n] for n in WEIGHT_ORDER], *[new_v[n] for n in WEIGHT_ORDER])
```

```python
import math

import jax
import jax.numpy as jnp
import numpy as np
from jax import lax
from jax.experimental import pallas as pl
from jax.experimental.pallas import tpu as pltpu

F32 = jnp.float32
BF16 = jnp.bfloat16

N_DEV = 8
HEAD_DIM = 64
A_Q_HEADS = 12
A_KV_HEADS = 4
A_GROUP = 3
B_HEADS = 12
C_HEADS = 4
C_HEAD_DIM = 128
WINDOW = 128
A_WIDTH = 768
A_KV_WIDTH = 256
B_WIDTH = 768
C_WIDTH = 512
EPS = 1e-6
NEG = -1e30

COL_QA, COL_KA, COL_VA = 0, 768, 1024
COL_QB, COL_KB, COL_VB = 1280, 2048, 2816
COL_QC = 3584
W_QKV = 4096
Q_SPLIT = 1280
COL_ZA, COL_ZB, COL_ZC = 0, 768, 1536
COL_GATE = W_Z = 2048
SRC_RANGES = ((0, 1280), (2048, 4352), (5132, 5644), (1280, 2048), (4352, 5120), (5644, 6156))
SRC_GATE = 6156
FB_SRC = 5120
FB_PAD = 128

ADAM_LR = 0.001
ADAM_B1 = 0.9
ADAM_B2 = 0.999
ADAM_EPS = 1e-08
ADAM_WD = 0.01
ADAM_STEP = 10

VMEM_BIG = 52 * 1024 * 1024
LANES = 128
MESH = pl.DeviceIdType.MESH


def _tile(n, pref, mult=128):
    if n <= pref:
        return n
    t = (pref // mult) * mult
    while t >= mult:
        if n % t == 0:
            return t
        t -= mult
    return n


def _params(sem=None, vmem=None):
    kw = {}
    if sem is not None:
        kw["dimension_semantics"] = sem
    if vmem is not None:
        kw["vmem_limit_bytes"] = vmem
    return pltpu.CompilerParams(**kw)


def _sigmoid(x):
    return 1.0 / (1.0 + jnp.exp(-x))


def _block_diag(hd):
    r = np.arange(LANES)
    return jnp.asarray((r[:, None] // hd) == (r[None, :] // hd), dtype=BF16)


def _seg_sum(t, bd):
    hi = t.astype(BF16)
    lo = (t - hi.astype(F32)).astype(BF16)
    outs = []
    for c in range(t.shape[1] // LANES):
        sl = slice(c * LANES, (c + 1) * LANES)
        outs.append(jnp.dot(hi[:, sl], bd, preferred_element_type=F32) + jnp.dot(lo[:, sl], bd, preferred_element_type=F32))
    return outs[0] if len(outs) == 1 else jnp.concatenate(outs, axis=1)


def _rmsnorm_fwd(x, gain, name):
    rows, d = x.shape
    bm = _tile(rows, 512, 8)

    def body(x_ref, g_ref, o_ref):
        xv = x_ref[...]
        ms = jnp.mean(xv * xv, axis=-1, keepdims=True)
        o_ref[...] = (xv * lax.rsqrt(ms + EPS) * g_ref[...]).astype(BF16)

    return pl.pallas_call(
        body, name=name, grid=(rows // bm,),
        in_specs=[pl.BlockSpec((bm, d), lambda i: (i, 0)), pl.BlockSpec((1, d), lambda i: (0, 0))],
        out_specs=pl.BlockSpec((bm, d), lambda i: (i, 0)),
        out_shape=jax.ShapeDtypeStruct((rows, d), BF16),
        compiler_params=_params(("parallel",)),
    )(x, gain)


def _rmsnorm_bwd(x, dhn, gain, dy, name):
    rows, d = x.shape
    bm = _tile(rows, 512, 8)
    with_dx = dy is not None

    def body(*refs):
        if with_dx:
            x_ref, dh_ref, g_ref, dy_ref, gx_ref, dg_ref = refs
        else:
            x_ref, dh_ref, g_ref, dg_ref = refs
        i = pl.program_id(0)
        xv = x_ref[...]
        rstd = lax.rsqrt(jnp.mean(xv * xv, axis=-1, keepdims=True) + EPS)
        xhat = xv * rstd
        dh = dh_ref[...]
        part = jnp.sum((dh * xhat).reshape(bm // 8, 8, d), axis=0)

        @pl.when(i == 0)
        def _():
            dg_ref[...] = part

        @pl.when(i > 0)
        def _():
            dg_ref[...] += part

        if with_dx:
            g = dh * g_ref[...]
            mean = jnp.mean(g * xhat, axis=-1, keepdims=True)
            gx_ref[...] = dy_ref[...] + rstd * (g - xhat * mean)

    row_spec = pl.BlockSpec((bm, d), lambda i: (i, 0))
    in_specs = [row_spec, row_spec, pl.BlockSpec((1, d), lambda i: (0, 0))]
    args = [x, dhn, gain]
    dg_spec = pl.BlockSpec((8, d), lambda i: (0, 0))
    dg_shape = jax.ShapeDtypeStruct((8, d), F32)
    if with_dx:
        in_specs.append(row_spec)
        args.append(dy)
        out_specs = [row_spec, dg_spec]
        out_shape = [jax.ShapeDtypeStruct((rows, d), F32), dg_shape]
    else:
        out_specs = [dg_spec]
        out_shape = [dg_shape]
    outs = pl.pallas_call(
        body, name=name, grid=(rows // bm,), in_specs=in_specs, out_specs=out_specs, out_shape=out_shape,
        compiler_params=_params(("arbitrary",), VMEM_BIG),
    )(*args)
    return outs if with_dx else (None, outs[0])


class _Comm:
    def __init__(self, kind, arrays):
        self.kind = kind
        self.arrays = list(arrays)
        self.n = len(self.arrays)

    def out_shapes(self):
        if self.kind == "gather":
            return [jax.ShapeDtypeStruct((N_DEV,) + a.shape, a.dtype) for a in self.arrays]
        return [jax.ShapeDtypeStruct(a.shape, a.dtype) for a in self.arrays]

    def scratch(self):
        return [pltpu.SemaphoreType.DMA((self.n, N_DEV - 1)), pltpu.SemaphoreType.DMA((self.n, N_DEV - 1)),
                pltpu.SemaphoreType.DMA((self.n,))]

    def _plan(self, ins, outs, sems, with_recvs):
        send_sems, recv_sems, local_sems = sems
        x, y, c = lax.axis_index("x"), lax.axis_index("y"), lax.axis_index("c")
        my = 4 * x + 2 * y + c
        gather = self.kind == "gather"
        local, sends, recvs = [], [], []
        for a in range(self.n):
            local.append(pltpu.make_async_copy(ins[a] if gather else ins[a].at[my], outs[a].at[my], local_sems.at[a]))
            for k in range(1, N_DEV):
                peer = (x ^ ((k >> 2) & 1), y ^ ((k >> 1) & 1), c ^ (k & 1))
                pid = 4 * peer[0] + 2 * peer[1] + peer[2]
                src = ins[a] if gather else ins[a].at[pid]
                sem = dict(send_sem=send_sems.at[a, k - 1], recv_sem=recv_sems.at[a, k - 1], device_id=peer, device_id_type=MESH)
                sends.append(pltpu.make_async_remote_copy(src_ref=src, dst_ref=outs[a].at[my], **sem))
                if with_recvs:
                    recvs.append(pltpu.make_async_remote_copy(src_ref=src, dst_ref=outs[a].at[pid], **sem))
        return local, sends, recvs

    def start(self, ins, outs, sems):
        local, sends, _ = self._plan(ins, outs, sems, False)
        for cp in local + sends:
            cp.start()

    def wait(self, ins, outs, sems):
        local, sends, recvs = self._plan(ins, outs, sems, True)
        for cp in recvs:
            cp.wait_recv()
        for cp in sends:
            cp.wait_send()
        for cp in local:
            cp.wait()


def _grid_edges(grid):
    first = last = None
    for ax, size in enumerate(grid):
        pid = pl.program_id(ax)
        f, l = pid == 0, pid == size - 1
        first = f if first is None else first & f
        last = l if last is None else last & l
    return first, last


def _hosted_call(body, comm, *, name, grid, in_specs, out_specs, out_shape, scratch_shapes, args, sem, vmem=None):
    in_specs, out_specs, out_shape, scratch_shapes = list(in_specs), list(out_specs), list(out_shape), list(scratch_shapes)
    if comm is None:
        res = pl.pallas_call(body, name=name, grid=grid, in_specs=in_specs, out_specs=out_specs, out_shape=out_shape,
                             scratch_shapes=scratch_shapes, compiler_params=_params(sem, vmem))(*args)
        return list(res), []
    n_in, n_out, n_scr, nc = len(in_specs), len(out_shape), len(scratch_shapes), comm.n

    def hosted(*refs):
        ins = refs[0:n_in]
        comm_in = refs[n_in:n_in + nc]
        outs = refs[n_in + nc:n_in + nc + n_out]
        comm_out = refs[n_in + nc + n_out:n_in + 2 * nc + n_out]
        scr = refs[n_in + 2 * nc + n_out:n_in + 2 * nc + n_out + n_scr]
        sems = refs[n_in + 2 * nc + n_out + n_scr:]
        first, last = _grid_edges(grid)

        @pl.when(first)
        def _():
            comm.start(comm_in, comm_out, sems)

        body(*ins, *outs, *scr)

        @pl.when(last)
        def _():
            comm.wait(comm_in, comm_out, sems)

    any_spec = pl.BlockSpec(memory_space=pl.ANY)
    res = pl.pallas_call(
        hosted, name=name, grid=grid, in_specs=in_specs + [any_spec] * nc, out_specs=out_specs + [any_spec] * nc,
        out_shape=out_shape + comm.out_shapes(), scratch_shapes=scratch_shapes + comm.scratch(),
        compiler_params=_params(("arbitrary",) * len(grid), vmem),
    )(*args, *comm.arrays)
    return list(res[0:n_out]), list(res[n_out:])


def _mm(a, b, *, grid, a_spec, b_spec, o_spec, o_shape, o_dtype, contract, name, add=None, add_spec=None, acc_shape=None,
        comm=None):
    nk = grid[2]
    has_add = add is not None

    def body(*refs):
        a_ref, b_ref = refs[0], refs[1]
        add_ref = refs[2] if has_add else None
        o_ref = refs[3] if has_add else refs[2]
        part = lax.dot_general(a_ref[...], b_ref[...], (contract, ((), ())), preferred_element_type=F32)
        if nk == 1:
            if has_add:
                part = part + add_ref[...]
            o_ref[...] = part.astype(o_dtype)
        else:
            acc = refs[-1]
            k = pl.program_id(2)

            @pl.when(k == 0)
            def _():
                acc[...] = part

            @pl.when(k > 0)
            def _():
                acc[...] += part

            @pl.when(k == nk - 1)
            def _():
                r = acc[...]
                if has_add:
                    r = r + add_ref[...]
                o_ref[...] = r.astype(o_dtype)

    in_specs = [a_spec, b_spec] + ([add_spec] if has_add else [])
    args = [a, b] + ([add] if has_add else [])
    scratch = [pltpu.VMEM(acc_shape, F32)] if nk > 1 else []
    outs, comm_outs = _hosted_call(
        body, comm, name=name, grid=grid, in_specs=in_specs, out_specs=[o_spec],
        out_shape=[jax.ShapeDtypeStruct(o_shape, o_dtype)], scratch_shapes=scratch, args=args,
        sem=("parallel", "parallel", "arbitrary"), vmem=VMEM_BIG)
    return outs[0] if comm is None else (outs[0], comm_outs)


def _mm_nn(a, b, *, bm, bn, bk, o_dtype, name, add=None, comm=None):
    m, kd = a.shape
    n = b.shape[1]
    bm, bn, bk = _tile(m, bm, 8), _tile(n, bn), _tile(kd, bk)
    o_spec = pl.BlockSpec((bm, bn), lambda i, j, k: (i, j))
    return _mm(a, b, grid=(m // bm, n // bn, kd // bk),
               a_spec=pl.BlockSpec((bm, bk), lambda i, j, k: (i, k)),
               b_spec=pl.BlockSpec((bk, bn), lambda i, j, k: (k, j)),
               o_spec=o_spec, o_shape=(m, n), o_dtype=o_dtype, contract=((1,), (0,)), name=name,
               add=add, add_spec=o_spec, acc_shape=(bm, bn), comm=comm)


def _mm_nt(a, b, *, bm, bn, bk, o_dtype, name, add=None, b_col0=0, comm=None):
    m, kd = a.shape
    n = b.shape[0]
    bm, bn, bk = _tile(m, bm, 8), _tile(n, bn), _tile(math.gcd(kd, b_col0), bk)
    kb0 = b_col0 // bk
    o_spec = pl.BlockSpec((bm, bn), lambda i, j, k: (i, j))
    return _mm(a, b, grid=(m // bm, n // bn, kd // bk),
               a_spec=pl.BlockSpec((bm, bk), lambda i, j, k: (i, k)),
               b_spec=pl.BlockSpec((bn, bk), lambda i, j, k: (j, kb0 + k)),
               o_spec=o_spec, o_shape=(m, n), o_dtype=o_dtype, contract=((1,), (1,)), name=name,
               add=add, add_spec=o_spec, acc_shape=(bm, bn), comm=comm)


def _mm_nt_sum(terms, *, bm, bn, bk, name, add=None, comm=None):
    m = terms[0][0].shape[0]
    n = terms[0][1].shape[0]
    bm, bn = _tile(m, bm, 8), _tile(n, bn)
    nt = (((1,), (1,)), ((), ()))
    plan, groups, start = [], [], 0
    for a, b, col0 in terms:
        kd = a.shape[1]
        tk = _tile(math.gcd(kd, col0), bk)
        steps = kd // tk
        if plan and kd < bk:
            groups.append([b, start - 1, 1, col0 // tk, tk])
            plan.append((start - 1, 1, len(groups) - 1, True))
            continue
        last = groups[-1] if groups else None
        if last is not None and last[0] is b and last[4] == tk and (last[3] + last[2]) * tk == col0:
            last[2] += steps
        else:
            groups.append([b, start, steps, col0 // tk, tk])
        plan.append((start, steps, len(groups) - 1, False))
        start += steps
    nk = start
    nterm, ngroup, has_add = len(terms), len(groups), add is not None

    def body(*refs):
        a_refs, b_refs = refs[0:nterm], refs[nterm:nterm + ngroup]
        add_ref = refs[nterm + ngroup] if has_add else None
        o_ref, acc = refs[nterm + ngroup + has_add], refs[nterm + ngroup + has_add + 1]
        k = pl.program_id(2)
        for t, (s0, steps, grp, rides) in enumerate(plan):
            @pl.when((k >= s0) & (k < s0 + steps))
            def _():
                part = lax.dot_general(a_refs[t][...], b_refs[grp][...], nt, preferred_element_type=F32)
                if rides:
                    acc[...] += part
                    return

                @pl.when(k == 0)
                def _():
                    acc[...] = part

                @pl.when(k > 0)
                def _():
                    acc[...] += part

        @pl.when(k == nk - 1)
        def _():
            o_ref[...] = acc[...] + add_ref[...] if has_add else acc[...]

    def a_spec(tk, s0, steps):
        return pl.BlockSpec((bm, tk), lambda i, j, k: (i, jnp.clip(k - s0, 0, steps - 1)))

    def b_spec(tk, s0, steps, off):
        return pl.BlockSpec((bn, tk), lambda i, j, k: (j, off + jnp.clip(k - s0, 0, steps - 1)))

    o_spec = pl.BlockSpec((bm, bn), lambda i, j, k: (i, j))
    in_specs = [a_spec(groups[grp][4], s0, steps) for s0, steps, grp, _ in plan]
    in_specs += [b_spec(tk, s0, steps, cb0) for _, s0, steps, cb0, tk in groups]
    args = [a for a, _, _ in terms] + [grp[0] for grp in groups]
    if has_add:
        in_specs.append(o_spec)
        args.append(add)
    outs, comm_outs = _hosted_call(
        body, comm, name=name, grid=(m // bm, n // bn, nk), in_specs=in_specs,
        out_specs=[o_spec], out_shape=[jax.ShapeDtypeStruct((m, n), F32)],
        scratch_shapes=[pltpu.VMEM((bm, bn), F32)], args=args,
        sem=("parallel", "parallel", "arbitrary"), vmem=VMEM_BIG)
    return outs[0] if comm is None else (outs[0], comm_outs)


def _mm_tn(a, b, *, bm, bn, bk, o_dtype, name, comm=None):
    kd, m = a.shape
    n = b.shape[1]
    bm, bn, bk = _tile(m, bm), _tile(n, bn), _tile(kd, bk, 8)
    return _mm(a, b, grid=(m // bm, n // bn, kd // bk),
               a_spec=pl.BlockSpec((bk, bm), lambda i, j, k: (k, i)),
               b_spec=pl.BlockSpec((bk, bn), lambda i, j, k: (k, j)),
               o_spec=pl.BlockSpec((bm, bn), lambda i, j, k: (i, j)),
               o_shape=(m, n), o_dtype=o_dtype, contract=((0,), (0,)), name=name, acc_shape=(bm, bn), comm=comm)


def _branch_full(w8):
    kb, ds = w8.shape[0] // N_DEV, w8.shape[1]
    return w8.reshape(N_DEV, kb, ds).transpose(1, 0, 2).reshape(kb, N_DEV * ds)


def _branch_bwd_w(s, du, name):
    m, kb = s.shape
    ds = du.shape[1] // N_DEV
    return _mm(s, du, grid=(1, N_DEV, 1),
               a_spec=pl.BlockSpec((m, kb), lambda i, j, k: (0, 0)),
               b_spec=pl.BlockSpec((m, ds), lambda i, j, k: (0, j)),
               o_spec=pl.BlockSpec((kb, ds), lambda i, j, k: (j, 0)),
               o_shape=(N_DEV * kb, ds), o_dtype=BF16, contract=((0,), (0,)), name=name)


def _headnorm_fwd(src, c0, width, bw, hd, gain, nflag, head_major, name):
    rows = src.shape[0]
    bm = _tile(rows, 2048 if bw <= 256 else 1024, 16)
    bd = _block_diag(hd)
    cb0 = c0 // bw

    def body(x_ref, g_ref, f_ref, bd_ref, o_ref):
        xv = x_ref[...].astype(F32)
        ss = _seg_sum(xv * xv, bd_ref[...])
        rstd = lax.rsqrt(ss * (1.0 / hd) + EPS)
        y = (xv * jnp.where(f_ref[...] > 0.0, rstd, 1.0) * g_ref[...]).astype(BF16)
        if head_major:
            for h in range(bw // HEAD_DIM):
                o_ref[h] = y[:, h * HEAD_DIM:(h + 1) * HEAD_DIM]
        else:
            o_ref[...] = y

    vec_spec = pl.BlockSpec((1, bw), lambda i, t: (0, t))
    if head_major:
        hpb = bw // HEAD_DIM
        out_spec = pl.BlockSpec((hpb, bm, HEAD_DIM), lambda i, t: (t, i, 0))
        out_shape = jax.ShapeDtypeStruct((width // HEAD_DIM, rows, HEAD_DIM), BF16)
    else:
        out_spec = pl.BlockSpec((bm, bw), lambda i, t: (i, t))
        out_shape = jax.ShapeDtypeStruct((rows, width), BF16)
    return pl.pallas_call(
        body, name=name, grid=(rows // bm, width // bw),
        in_specs=[pl.BlockSpec((bm, bw), lambda i, t: (i, cb0 + t)), vec_spec, vec_spec,
                  pl.BlockSpec((LANES, LANES), lambda i, t: (0, 0))],
        out_specs=out_spec, out_shape=out_shape,
        compiler_params=_params(("parallel", "parallel")),
    )(src, gain, nflag, bd)


def _headnorm_bwd(src, c0, width, bw, hd, gain, nflag, dyn, target, t0, name):
    rows = src.shape[0]
    bm = _tile(rows, 2048 if bw <= 256 else 1024, 16)
    bd = _block_diag(hd)
    cb0 = c0 // bw
    tb0 = t0 // bw
    aliased = target is not None

    def body(*refs):
        if aliased:
            x_ref, dy_ref, g_ref, f_ref, bd_ref, _, o_ref, dg_ref = refs
        else:
            x_ref, dy_ref, g_ref, f_ref, bd_ref, o_ref, dg_ref = refs
        i = pl.program_id(1)
        xv = x_ref[...].astype(F32)
        dyv = dy_ref[...]
        bdv = bd_ref[...]
        rstd = lax.rsqrt(_seg_sum(xv * xv, bdv) * (1.0 / hd) + EPS)
        xhat = xv * rstd
        g = dyv * g_ref[...]
        mean = _seg_sum(g * xhat, bdv) * (1.0 / hd)
        dx = jnp.where(f_ref[...] > 0.0, rstd * (g - xhat * mean), g)
        o_ref[...] = dx.astype(BF16)
        part = jnp.sum((dyv * xhat).reshape(bm // 8, 8, bw), axis=0)

        @pl.when(i == 0)
        def _():
            dg_ref[...] = part

        @pl.when(i > 0)
        def _():
            dg_ref[...] += part

    vec_spec = pl.BlockSpec((1, bw), lambda t, i: (0, t))
    in_specs = [pl.BlockSpec((bm, bw), lambda t, i: (i, cb0 + t)), pl.BlockSpec((bm, bw), lambda t, i: (i, t)),
                vec_spec, vec_spec, pl.BlockSpec((LANES, LANES), lambda t, i: (0, 0))]
    args = [src, dyn, gain, nflag, bd]
    aliases = {}
    if aliased:
        in_specs.append(pl.BlockSpec(memory_space=pl.ANY))
        args.append(target)
        aliases = {5: 0}
        o_shape = jax.ShapeDtypeStruct(target.shape, BF16)
    else:
        o_shape = jax.ShapeDtypeStruct((rows, width), BF16)
    out, dg = pl.pallas_call(
        body, name=name, grid=(width // bw, rows // bm), in_specs=in_specs,
        out_specs=[pl.BlockSpec((bm, bw), lambda t, i: (i, tb0 + t)), pl.BlockSpec((8, bw), lambda t, i: (0, t))],
        out_shape=[o_shape, jax.ShapeDtypeStruct((8, width), F32)],
        input_output_aliases=aliases,
        compiler_params=_params(("parallel", "arbitrary")),
    )(*args)
    return out, dg


def _fox_prep(pfb, bpad, name):
    s = pfb.shape[0]

    def body(p_ref, b_ref, c_ref):
        z = p_ref[...] + b_ref[...]
        logf = jnp.minimum(z, 0.0) - jnp.log(1.0 + jnp.exp(-jnp.abs(z)))
        x = logf.T[0:16, :]
        lane = lax.broadcasted_iota(jnp.int32, (16, s), 1)
        sh = 1
        while sh < s:
            x = x + jnp.where(lane >= sh, pltpu.roll(x, sh, 1), 0.0)
            sh *= 2
        c_ref[...] = x

    return pl.pallas_call(
        body, name=name, grid=(1,),
        in_specs=[pl.BlockSpec((s, FB_PAD), lambda i: (0, 0)), pl.BlockSpec((1, FB_PAD), lambda i: (0, 0))],
        out_specs=pl.BlockSpec((16, s), lambda i: (0, 0)),
        out_shape=jax.ShapeDtypeStruct((16, s), F32),
        compiler_params=_params(("arbitrary",)),
    )(pfb, bpad)


def _fox_prep_bwd(pfb, bpad, dct, name):
    s = pfb.shape[0]

    def body(p_ref, b_ref, dc_ref, df_ref, db_ref):
        zt = (p_ref[...] + b_ref[...]).T[0:16, :]
        y = dc_ref[...]
        lane = lax.broadcasted_iota(jnp.int32, (16, s), 1)
        sh = 1
        while sh < s:
            y = y + jnp.where(lane < s - sh, pltpu.roll(y, s - sh, 1), 0.0)
            sh *= 2
        dz = y * _sigmoid(-zt)
        db_ref[...] = jnp.broadcast_to(jnp.sum(dz, axis=1, keepdims=True), (16, FB_PAD))
        full = jnp.concatenate([dz, jnp.zeros((FB_PAD - 16, s), F32)], axis=0)
        df_ref[...] = full.T.astype(BF16)

    return pl.pallas_call(
        body, name=name, grid=(1,),
        in_specs=[pl.BlockSpec((s, FB_PAD), lambda i: (0, 0)), pl.BlockSpec((1, FB_PAD), lambda i: (0, 0)),
                  pl.BlockSpec((16, s), lambda i: (0, 0))],
        out_specs=[pl.BlockSpec((s, FB_PAD), lambda i: (0, 0)), pl.BlockSpec((16, FB_PAD), lambda i: (0, 0))],
        out_shape=[jax.ShapeDtypeStruct((s, FB_PAD), BF16), jax.ShapeDtypeStruct((16, FB_PAD), F32)],
        compiler_params=_params(("arbitrary",)),
    )(pfb, bpad, dct)


def _swa_window(n):
    ws = pl.multiple_of(jnp.maximum(n * WINDOW - WINDOW, 0), WINDOW)
    qi = lax.broadcasted_iota(jnp.int32, (WINDOW, 2 * WINDOW), 0)
    kj = lax.broadcasted_iota(jnp.int32, (WINDOW, 2 * WINDOW), 1)
    rel = qi + (n * WINDOW - ws) - kj
    valid = (rel >= 0) & (rel < WINDOW)
    return ws, valid, rel.astype(F32)


def _attn_a_fwd(qkv, sinks, slopes, name):
    s = qkv.shape[1]
    nb = s // WINDOW
    smem = pl.BlockSpec(memory_space=pltpu.SMEM)

    def body(sink_ref, slope_ref, q_ref, k_ref, v_ref, o_ref, lse_ref):
        n = pl.program_id(0)
        ws, valid, relf = _swa_window(n)
        outs = []
        for h in range(A_Q_HEADS):
            kvh = h // A_GROUP
            kw = k_ref[kvh, pl.ds(ws, 2 * WINDOW), :]
            vw = v_ref[kvh, pl.ds(ws, 2 * WINDOW), :]
            sc = lax.dot_general(q_ref[h], kw, (((1,), (1,)), ((), ())), preferred_element_type=F32)
            sc = jnp.where(valid, sc - slope_ref[h] * relf, NEG)
            sink = sink_ref[h]
            m = jnp.maximum(jnp.max(sc, axis=1, keepdims=True), sink)
            p = jnp.exp(sc - m)
            denom = jnp.sum(p, axis=1, keepdims=True) + jnp.exp(sink - m)
            pn = (p / denom).astype(BF16)
            outs.append(jnp.dot(pn, vw, preferred_element_type=F32))
            lse_ref[h] = jnp.broadcast_to(m + jnp.log(denom), (WINDOW, HEAD_DIM))
        o_ref[...] = jnp.concatenate(outs, axis=1)

    return pl.pallas_call(
        body, name=name, grid=(nb,),
        in_specs=[smem, smem,
                  pl.BlockSpec((A_Q_HEADS, WINDOW, HEAD_DIM), lambda n: (0, n, 0)),
                  pl.BlockSpec((A_KV_HEADS, s, HEAD_DIM), lambda n: (A_GROUP, 0, 0)),
                  pl.BlockSpec((A_KV_HEADS, s, HEAD_DIM), lambda n: (A_GROUP + 1, 0, 0))],
        out_specs=[pl.BlockSpec((WINDOW, A_WIDTH), lambda n: (n, 0)),
                   pl.BlockSpec((A_Q_HEADS, WINDOW, HEAD_DIM), lambda n: (0, n, 0))],
        out_shape=[jax.ShapeDtypeStruct((s, A_WIDTH), F32), jax.ShapeDtypeStruct((A_Q_HEADS, s, HEAD_DIM), F32)],
        compiler_params=_params(("parallel",), VMEM_BIG),
    )(sinks, slopes, qkv, qkv, qkv)


def _attn_a_bwd(qkv, do, lse, dd, sinks, slopes, name, comm=None):
    s = qkv.shape[1]
    nb = s // WINDOW
    smem = pl.BlockSpec(memory_space=pltpu.SMEM)
    last = nb - 1

    def body(sink_ref, slope_ref, q_ref, k_ref, v_ref, do_ref, lse_ref, dd_ref, dq_ref, dkv_ref, ds_ref, carry):
        n = pl.program_id(0)

        @pl.when(n == 0)
        def _():
            carry[...] = jnp.zeros(carry.shape, F32)
            ds_ref[...] = jnp.zeros(ds_ref.shape, F32)

        @pl.when(n < nb)
        def _():
            ws, valid, relf = _swa_window(n)
            dqs = []
            dkw = [None] * A_KV_HEADS
            dvw = [None] * A_KV_HEADS
            for h in range(A_Q_HEADS):
                kvh = h // A_GROUP
                qh = q_ref[h]
                doh = do_ref[h]
                kw = k_ref[kvh, pl.ds(ws, 2 * WINDOW), :]
                vw = v_ref[kvh, pl.ds(ws, 2 * WINDOW), :]
                lse_h = lse_ref[h]
                dd_h = dd_ref[h]
                sc = lax.dot_general(qh, kw, (((1,), (1,)), ((), ())), preferred_element_type=F32)
                sc = jnp.where(valid, sc - slope_ref[h] * relf, NEG)
                p = jnp.exp(sc - lse_h[:, 0:1])
                dp = lax.dot_general(doh, vw, (((1,), (1,)), ((), ())), preferred_element_type=F32)
                dsc = (p * (dp - dd_h[:, 0:1])).astype(BF16)
                pb = p.astype(BF16)
                dqs.append(jnp.dot(dsc, kw, preferred_element_type=F32))
                dk_h = jnp.dot(qh.T, dsc, preferred_element_type=F32)
                dv_h = jnp.dot(doh.T, pb, preferred_element_type=F32)
                dkw[kvh] = dk_h if dkw[kvh] is None else dkw[kvh] + dk_h
                dvw[kvh] = dv_h if dvw[kvh] is None else dvw[kvh] + dv_h
                psink = jnp.exp(sink_ref[h] - lse_h)
                ds_ref[h] += jnp.sum((-psink * dd_h).reshape(WINDOW // 8, 8, HEAD_DIM), axis=0)
            dq_ref[...] = jnp.concatenate(dqs, axis=1)
            win = jnp.concatenate(dkw + dvw, axis=0)
            first = win[:, 0:WINDOW]
            second = win[:, WINDOW:2 * WINDOW]
            dkv_ref[...] = (carry[...] + first).T
            carry[...] = jnp.where(n == 0, first, second)

        @pl.when(n == nb)
        def _():
            dkv_ref[...] = carry[...].T

    hm = lambda heads: pl.BlockSpec((heads, WINDOW, HEAD_DIM), lambda n: (0, jnp.minimum(n, last), 0))
    res = lambda blk: pl.BlockSpec((A_KV_HEADS, s, HEAD_DIM), lambda n: (blk, 0, 0))
    outs, comm_outs = _hosted_call(
        body, comm, name=name, grid=(nb + 1,),
        in_specs=[smem, smem, hm(A_Q_HEADS), res(A_GROUP), res(A_GROUP + 1), hm(A_Q_HEADS), hm(A_Q_HEADS), hm(A_Q_HEADS)],
        out_specs=[pl.BlockSpec((WINDOW, A_WIDTH), lambda n: (jnp.minimum(n, last), 0)),
                   pl.BlockSpec((WINDOW, 2 * A_KV_WIDTH), lambda n: (jnp.maximum(n - 1, 0), 0)),
                   pl.BlockSpec((A_Q_HEADS, 8, HEAD_DIM), lambda n: (0, 0, 0))],
        out_shape=[jax.ShapeDtypeStruct((s, A_WIDTH), F32), jax.ShapeDtypeStruct((s, 2 * A_KV_WIDTH), F32),
                   jax.ShapeDtypeStruct((A_Q_HEADS, 8, HEAD_DIM), F32)],
        scratch_shapes=[pltpu.VMEM((2 * A_KV_WIDTH, WINDOW), F32)],
        args=[sinks, slopes, qkv, qkv, qkv, do, lse, dd], sem=("arbitrary",), vmem=VMEM_BIG)
    return outs[0], outs[1], outs[2], comm_outs


def _attn_b_fwd(qkv, c3, name, comm=None):
    heads, s = qkv.shape[0] // 3, qkv.shape[1]
    hpairs = heads // 2
    bq = min(512, s)
    nq = s // bq
    nt = (((1,), (1,)), ((), ()))

    def body(q_ref, k_ref, v_ref, c_ref, o_ref, lse_ref, m_scr, l_scr, acc_scr):
        i = pl.program_id(1)
        r0 = pl.multiple_of(i * bq, bq)
        row = lax.broadcasted_iota(jnp.int32, (bq, bq), 0)
        col = lax.broadcasted_iota(jnp.int32, (bq, bq), 1)
        m_scr[...] = jnp.full((2, bq, LANES), NEG, F32)
        l_scr[...] = jnp.zeros((2, bq, LANES), F32)
        acc_scr[...] = jnp.zeros((2, bq, HEAD_DIM), F32)

        def step(j, masked):
            k0 = pl.multiple_of(j * bq, bq)
            for h2 in range(2):
                kv = k_ref[h2, pl.ds(k0, bq), :]
                vv = v_ref[h2, pl.ds(k0, bq), :]
                cq0 = c_ref[h2, :, pl.ds(r0, LANES)][:, 0:1]
                sc = lax.dot_general(q_ref[h2], kv, nt, preferred_element_type=F32)
                sc = sc + (cq0 - c_ref[h2, :, pl.ds(k0, bq)])
                if masked:
                    sc = jnp.where(col <= row, sc, NEG)
                m_prev = m_scr[h2]
                m_new = jnp.maximum(m_prev, jnp.max(sc, axis=1, keepdims=True))
                alpha = jnp.exp(m_prev - m_new)
                p = jnp.exp(sc - m_new[:, 0:1])
                l_scr[h2] = alpha * l_scr[h2] + jnp.sum(p, axis=1, keepdims=True)
                p_hi = p.astype(BF16)
                p_lo = (p - p_hi.astype(F32)).astype(BF16)
                pv = jnp.dot(p_hi, vv, preferred_element_type=F32) + jnp.dot(p_lo, vv, preferred_element_type=F32)
                acc_scr[h2] = acc_scr[h2] * alpha[:, 0:HEAD_DIM] + pv
                m_scr[h2] = m_new

        def loop_body(j, carry):
            step(j, False)
            return carry

        lax.fori_loop(0, i, loop_body, 0)
        step(i, True)
        outs = []
        for h2 in range(2):
            l = l_scr[h2]
            outs.append(acc_scr[h2] / l[:, 0:HEAD_DIM])
            lse_ref[h2] = (m_scr[h2] + jnp.log(l))[:, 0:HEAD_DIM]
        o_ref[...] = jnp.concatenate(outs, axis=1)

    res = lambda off: pl.BlockSpec((2, s, HEAD_DIM), lambda hp, i: (off + hp, 0, 0))
    outs, comm_outs = _hosted_call(
        body, comm, name=name, grid=(hpairs, nq),
        in_specs=[pl.BlockSpec((2, bq, HEAD_DIM), lambda hp, i: (hp, i, 0)), res(hpairs), res(2 * hpairs),
                  pl.BlockSpec((2, 1, s), lambda hp, i: (hp, 0, 0))],
        out_specs=[pl.BlockSpec((bq, 2 * HEAD_DIM), lambda hp, i: (i, hp)),
                   pl.BlockSpec((2, bq, HEAD_DIM), lambda hp, i: (hp, i, 0))],
        out_shape=[jax.ShapeDtypeStruct((s, heads * HEAD_DIM), F32), jax.ShapeDtypeStruct((heads, s, HEAD_DIM), F32)],
        scratch_shapes=[pltpu.VMEM((2, bq, LANES), F32), pltpu.VMEM((2, bq, LANES), F32), pltpu.VMEM((2, bq, HEAD_DIM), F32)],
        args=[qkv, qkv, qkv, c3], sem=("parallel", "parallel"), vmem=VMEM_BIG)
    return outs[0], outs[1], comm_outs


def _attn_b_bwd(qkv, do, lse, dd, c3, name, comm=None):
    heads, s = qkv.shape[0] // 3, qkv.shape[1]
    hpairs = heads // 2
    bq = min(512, s)
    nq = s // bq
    nt = (((1,), (1,)), ((), ()))
    tn = (((0,), (0,)), ((), ()))
    grid = (heads // 2, nq)

    def body(q_ref, k_ref, v_ref, do_ref, lse_ref, dd_ref, c_ref, dq_ref, dk_ref, dv_ref, dc_ref,
             dq_scr, dk_scr, dv_scr, dc_scr):
        j = pl.program_id(1)
        k0 = pl.multiple_of(j * bq, bq)
        row = lax.broadcasted_iota(jnp.int32, (bq, bq), 0)
        col = lax.broadcasted_iota(jnp.int32, (bq, bq), 1)

        @pl.when(j == 0)
        def _():
            dq_scr[...] = jnp.zeros(dq_scr.shape, F32)

        dk_scr[...] = jnp.zeros((2, HEAD_DIM, bq), F32)
        dv_scr[...] = jnp.zeros((2, HEAD_DIM, bq), F32)
        dc_scr[...] = jnp.zeros((2, 1, bq), F32)
        k_t = [k_ref[h2].T for h2 in range(2)]

        def step(i, masked):
            r0 = pl.multiple_of(i * bq, bq)
            for h2 in range(2):
                kv = k_ref[h2]
                vv = v_ref[h2]
                qv = q_ref[h2, pl.ds(r0, bq), :]
                dov = do_ref[h2, pl.ds(r0, bq), :]
                lse_v = lse_ref[h2, pl.ds(r0, bq), :][:, 0:1]
                dd_v = dd_ref[h2, pl.ds(r0, bq), :][:, 0:1]
                cq0 = c_ref[h2, :, pl.ds(r0, LANES)][:, 0:1]
                sc = lax.dot_general(qv, kv, nt, preferred_element_type=F32) + (cq0 - c_ref[h2, :, pl.ds(k0, bq)])
                if masked:
                    sc = jnp.where(col <= row, sc, NEG)
                p = jnp.exp(sc - lse_v)
                dp = lax.dot_general(dov, vv, nt, preferred_element_type=F32)
                dsc = p * (dp - dd_v)
                dsb = dsc.astype(BF16)
                dv_scr[h2] += jnp.dot(dov.T, p.astype(BF16), preferred_element_type=F32)
                dk_scr[h2] += jnp.dot(qv.T, dsb, preferred_element_type=F32)
                dq_scr[h2, :, pl.ds(r0, bq)] += jnp.dot(k_t[h2], dsb.T, preferred_element_type=F32)
                dc_scr[h2] -= jnp.sum(dsc, axis=0, keepdims=True)

        def loop_body(i, carry):
            step(i, False)
            return carry

        step(j, True)
        lax.fori_loop(j + 1, nq, loop_body, 0)
        dc_ref[...] = dc_scr[...]
        dk_ref[...] = jnp.concatenate([dk_scr[0].T, dk_scr[1].T], axis=1)
        dv_ref[...] = jnp.concatenate([dv_scr[0].T, dv_scr[1].T], axis=1)

        @pl.when(j == nq - 1)
        def _():
            dq_ref[...] = jnp.concatenate([dq_scr[0].T, dq_scr[1].T], axis=1)

    res = pl.BlockSpec((2, s, HEAD_DIM), lambda hp, j: (hp, 0, 0))
    blk = lambda off: pl.BlockSpec((2, bq, HEAD_DIM), lambda hp, j: (off + hp, j, 0))
    tm = jax.ShapeDtypeStruct((s, heads * HEAD_DIM), F32)
    in_specs = [res, blk(hpairs), blk(2 * hpairs), res, res, res, pl.BlockSpec((2, 1, s), lambda hp, j: (hp, 0, 0))]
    out_specs = [pl.BlockSpec((s, 2 * HEAD_DIM), lambda hp, j: (0, hp)),
                 pl.BlockSpec((bq, 2 * HEAD_DIM), lambda hp, j: (j, hp)),
                 pl.BlockSpec((bq, 2 * HEAD_DIM), lambda hp, j: (j, hp)),
                 pl.BlockSpec((2, 1, bq), lambda hp, j: (hp, 0, j))]
    out_shape = [tm, tm, tm, jax.ShapeDtypeStruct((heads, 1, s), F32)]
    scratch = [pltpu.VMEM((2, HEAD_DIM, s), F32), pltpu.VMEM((2, HEAD_DIM, bq), F32),
               pltpu.VMEM((2, HEAD_DIM, bq), F32), pltpu.VMEM((2, 1, bq), F32)]
    outs, comm_outs = _hosted_call(
        body, comm, name=name, grid=grid, in_specs=in_specs, out_specs=out_specs, out_shape=out_shape,
        scratch_shapes=scratch, args=[qkv, qkv, qkv, do, lse, dd, c3], sem=("parallel", "arbitrary"), vmem=VMEM_BIG)
    return outs[0], outs[1], outs[2], outs[3], comm_outs


def _attn_c_probs(qh, mkh):
    sc = lax.dot_general(qh, mkh, (((1,), (1,)), ((), ())), preferred_element_type=F32) * (C_HEAD_DIM ** -0.5)
    p = jnp.exp(sc - jnp.max(sc, axis=1, keepdims=True))
    return p / jnp.sum(p, axis=1, keepdims=True)


def _attn_c_fwd(q, mkv, name):
    s = q.shape[0]
    m = mkv.shape[0]
    bq = _tile(s, 512, 8)

    def body(q_ref, mk_ref, mv_ref, o_ref):
        outs = []
        for h in range(C_HEADS):
            sl = slice(h * C_HEAD_DIM, (h + 1) * C_HEAD_DIM)
            pn = _attn_c_probs(q_ref[:, sl], mk_ref[:, sl]).astype(BF16)
            outs.append(jnp.dot(pn, mv_ref[:, sl], preferred_element_type=F32))
        o_ref[...] = jnp.concatenate(outs, axis=1)

    return pl.pallas_call(
        body, name=name, grid=(s // bq,),
        in_specs=[pl.BlockSpec((bq, C_WIDTH), lambda i: (i, 0)), pl.BlockSpec((m, C_WIDTH), lambda i: (0, 0)),
                  pl.BlockSpec((m, C_WIDTH), lambda i: (0, 1))],
        out_specs=pl.BlockSpec((bq, C_WIDTH), lambda i: (i, 0)),
        out_shape=jax.ShapeDtypeStruct((s, C_WIDTH), F32),
        compiler_params=_params(("parallel",)),
    )(q, mkv, mkv)


def _attn_c_bwd(q, mkv, do, name):
    s = q.shape[0]
    m = mkv.shape[0]
    bq = _tile(s, 512, 8)
    tn = (((0,), (0,)), ((), ()))

    def body(q_ref, mk_ref, mv_ref, do_ref, dq_ref, dm_ref):
        i = pl.program_id(0)

        @pl.when(i == 0)
        def _():
            dm_ref[...] = jnp.zeros(dm_ref.shape, F32)

        dqs = []
        for h in range(C_HEADS):
            sl = slice(h * C_HEAD_DIM, (h + 1) * C_HEAD_DIM)
            qh, mkh, mvh, doh = q_ref[:, sl], mk_ref[:, sl], mv_ref[:, sl], do_ref[:, sl]
            pn = _attn_c_probs(qh, mkh)
            dp = lax.dot_general(doh, mvh, (((1,), (1,)), ((), ())), preferred_element_type=F32)
            dsc = (pn * (dp - jnp.sum(pn * dp, axis=1, keepdims=True)) * (C_HEAD_DIM ** -0.5)).astype(BF16)
            dqs.append(jnp.dot(dsc, mkh, preferred_element_type=F32))
            dm_ref[:, sl] += lax.dot_general(dsc, qh, tn, preferred_element_type=F32)
            sv = slice(C_WIDTH + h * C_HEAD_DIM, C_WIDTH + (h + 1) * C_HEAD_DIM)
            dm_ref[:, sv] += lax.dot_general(pn.astype(BF16), doh, tn, preferred_element_type=F32)
        dq_ref[...] = jnp.concatenate(dqs, axis=1)

    row = pl.BlockSpec((bq, C_WIDTH), lambda i: (i, 0))
    return pl.pallas_call(
        body, name=name, grid=(s // bq,),
        in_specs=[row, pl.BlockSpec((m, C_WIDTH), lambda i: (0, 0)), pl.BlockSpec((m, C_WIDTH), lambda i: (0, 1)), row],
        out_specs=[row, pl.BlockSpec((m, 2 * C_WIDTH), lambda i: (0, 0))],
        out_shape=[jax.ShapeDtypeStruct((s, C_WIDTH), F32), jax.ShapeDtypeStruct((m, 2 * C_WIDTH), F32)],
        compiler_params=_params(("arbitrary",)),
    )(q, mkv, mkv, do)


def _gate_fwd(y, proj, zc0, bw, name):
    rows, width = y.shape
    bm = _tile(rows, 2048 if bw <= 256 else 1024, 16)
    cb0 = zc0 // bw

    def body(y_ref, z_ref, o_ref):
        z = z_ref[...].astype(F32)
        o_ref[...] = (y_ref[...] * (z * _sigmoid(z))).astype(BF16)

    return pl.pallas_call(
        body, name=name, grid=(rows // bm, width // bw),
        in_specs=[pl.BlockSpec((bm, bw), lambda i, t: (i, t)), pl.BlockSpec((bm, bw), lambda i, t: (i, cb0 + t))],
        out_specs=pl.BlockSpec((bm, bw), lambda i, t: (i, t)),
        out_shape=jax.ShapeDtypeStruct((rows, width), BF16),
        compiler_params=_params(("parallel", "parallel")),
    )(y, proj)


def _gate_bwd(dsv, y, proj, zc0, bw, dproj, t0, head_major, name):
    rows, width = y.shape
    bm = _tile(rows, 2048 if bw <= 256 else 1024, 16)
    cb0 = zc0 // bw
    tb0 = t0 // bw
    bd = _block_diag(HEAD_DIM)
    hpb = bw // HEAD_DIM

    def body(*refs):
        if head_major:
            ds_ref, y_ref, z_ref, bd_ref, _, dp_ref, dy_ref, dd_ref = refs
        else:
            ds_ref, y_ref, z_ref, _, dp_ref, dy_ref = refs
        z = z_ref[...].astype(F32)
        sig = _sigmoid(z)
        dsx = ds_ref[...]
        yv = y_ref[...]
        dy = dsx * (z * sig)
        dp_ref[...] = (dsx * yv * (sig * (1.0 + z * (1.0 - sig)))).astype(BF16)
        if head_major:
            dyb = dy.astype(BF16)
            dd = _seg_sum(dyb.astype(F32) * yv, bd_ref[...])
            for h in range(hpb):
                sl = slice(h * HEAD_DIM, (h + 1) * HEAD_DIM)
                dy_ref[h] = dyb[:, sl]
                dd_ref[h] = dd[:, sl]
        else:
            dy_ref[...] = dy.astype(BF16)

    tile = pl.BlockSpec((bm, bw), lambda i, t: (i, t))
    ztile = pl.BlockSpec((bm, bw), lambda i, t: (i, cb0 + t))
    ttile = pl.BlockSpec((bm, bw), lambda i, t: (i, tb0 + t))
    any_spec = pl.BlockSpec(memory_space=pl.ANY)
    dp_shape = jax.ShapeDtypeStruct(dproj.shape, BF16)
    if head_major:
        hm_spec = pl.BlockSpec((hpb, bm, HEAD_DIM), lambda i, t: (t, i, 0))
        nh = width // HEAD_DIM
        outs = pl.pallas_call(
            body, name=name, grid=(rows // bm, width // bw),
            in_specs=[tile, tile, ztile, pl.BlockSpec((LANES, LANES), lambda i, t: (0, 0)), any_spec],
            out_specs=[ttile, hm_spec, hm_spec],
            out_shape=[dp_shape, jax.ShapeDtypeStruct((nh, rows, HEAD_DIM), BF16),
                       jax.ShapeDtypeStruct((nh, rows, HEAD_DIM), F32)],
            input_output_aliases={4: 0},
            compiler_params=_params(("parallel", "parallel")),
        )(dsv, y, proj, bd, dproj)
        return outs[0], outs[1], outs[2]
    outs = pl.pallas_call(
        body, name=name, grid=(rows // bm, width // bw),
        in_specs=[tile, tile, ztile, any_spec],
        out_specs=[ttile, tile],
        out_shape=[dp_shape, jax.ShapeDtypeStruct((rows, width), BF16)],
        input_output_aliases={3: 0},
        compiler_params=_params(("parallel", "parallel")),
    )(dsv, y, proj, dproj)
    return outs[0], outs[1], None


def _merge_fwd(proj, ua, ub, uc, name):
    rows, d = ua.shape
    bm = _tile(rows, 1024, 16)
    bw = _tile(d, 512)
    g0 = COL_GATE // bw
    gstep = d // bw

    def body(la_ref, lb_ref, lc_ref, ua_ref, ub_ref, uc_ref, o_ref, ga_ref, gb_ref, gc_ref):
        y = None
        for l_ref, u_ref, g_ref in ((la_ref, ua_ref, ga_ref), (lb_ref, ub_ref, gb_ref), (lc_ref, uc_ref, gc_ref)):
            g = _sigmoid(l_ref[...].astype(F32))
            g_ref[...] = g.astype(BF16)
            term = g * u_ref[...].astype(F32)
            y = term if y is None else y + term
        o_ref[...] = y.astype(BF16)

    tile = pl.BlockSpec((bm, bw), lambda i, t: (i, t))
    gate = lambda b: pl.BlockSpec((bm, bw), lambda i, t: (i, g0 + b * gstep + t))
    shape = jax.ShapeDtypeStruct((rows, d), BF16)
    return pl.pallas_call(
        body, name=name, grid=(rows // bm, d // bw),
        in_specs=[gate(0), gate(1), gate(2), tile, tile, tile],
        out_specs=[tile] * 4, out_shape=[shape] * 4,
        compiler_params=_params(("parallel", "parallel")),
    )(proj, proj, proj, ua, ub, uc)


def _merge_bwd(dym, us, gs, name):
    rows, d = dym.shape
    bm = _tile(rows, 256, 16)

    def body(dy_ref, ua_ref, ub_ref, uc_ref, ga_ref, gb_ref, gc_ref, dg_ref, da_ref, db_ref, dc_ref):
        dyv = dy_ref[...]
        for b, (u_ref, g_ref, du_ref) in enumerate(((ua_ref, ga_ref, da_ref), (ub_ref, gb_ref, db_ref), (uc_ref, gc_ref, dc_ref))):
            g = g_ref[...].astype(F32)
            du_ref[...] = (g * dyv).astype(BF16)
            dg_ref[:, b * d:(b + 1) * d] = (dyv * u_ref[...].astype(F32) * g * (1.0 - g)).astype(BF16)

    tile = pl.BlockSpec((bm, d), lambda i: (i, 0))
    shape = jax.ShapeDtypeStruct((rows, d), BF16)
    outs = pl.pallas_call(
        body, name=name, grid=(rows // bm,),
        in_specs=[tile] * 7,
        out_specs=[pl.BlockSpec((bm, 3 * d), lambda i: (i, 0)), tile, tile, tile],
        out_shape=[jax.ShapeDtypeStruct((rows, 3 * d), BF16), shape, shape, shape],
        compiler_params=_params(("parallel",), VMEM_BIG),
    )(dym, *us, *gs)
    return outs[0], outs[1], outs[2], outs[3]


def _out_proj_loss(ym, wo, x, target, name):
    m, d = x.shape
    bm, bn = _tile(m, 1024, 16), _tile(d, 1024)
    grid = (m // bm, d // bn)

    def body(a_ref, b_ref, x_ref, t_ref, dy_ref, dyb_ref, l_ref):
        first, _ = _grid_edges(grid)
        y = jnp.dot(a_ref[...], b_ref[...], preferred_element_type=F32) + x_ref[...]
        diff = y - t_ref[...]
        dy = diff * (1.0 / d)
        dy_ref[...] = dy
        dyb_ref[...] = dy.astype(BF16)
        sq = diff * diff
        part = sq[:, 0:LANES]
        for c in range(1, bn // LANES):
            part = part + sq[:, c * LANES:(c + 1) * LANES]
        part = jnp.sum(part.reshape(bm // 8, 8, LANES), axis=0)

        @pl.when(first)
        def _():
            l_ref[...] = part

        @pl.when(jnp.logical_not(first))
        def _():
            l_ref[...] += part

    tile = pl.BlockSpec((bm, bn), lambda i, j: (i, j))
    return pl.pallas_call(
        body, name=name, grid=grid,
        in_specs=[pl.BlockSpec((bm, d), lambda i, j: (i, 0)), pl.BlockSpec((d, bn), lambda i, j: (0, j)), tile, tile],
        out_specs=[tile, tile, pl.BlockSpec((8, LANES), lambda i, j: (0, 0))],
        out_shape=[jax.ShapeDtypeStruct((m, d), F32), jax.ShapeDtypeStruct((m, d), BF16),
                   jax.ShapeDtypeStruct((8, LANES), F32)],
        compiler_params=_params(("arbitrary", "arbitrary"), VMEM_BIG),
    )(ym, wo, x, target)


def _row(vec, reps=1):
    return jnp.tile(vec.reshape(1, -1).astype(F32), (1, reps))


def _local_step(x, mem, target, small, wg, shards=None):
    s, d = x.shape
    dist = shards is not None
    wg = dict(wg)
    ones = lambda n: jnp.ones((1, n), F32)
    zeros = lambda n: jnp.zeros((1, n), F32)
    scale_ab = HEAD_DIM ** -0.5
    split8 = lambda g: g.reshape(N_DEV, g.shape[0] // N_DEV, g.shape[1])
    flat8 = lambda g: g.reshape(g.shape[0] * g.shape[1], g.shape[2])
    gather = lambda names: _Comm("gather", [shards[n] for n in names]) if dist else None
    g = {}

    def scatter(names):
        return _Comm("scatter", [split8(g[n]) for n in names]) if dist else None

    def hosted(result, names, store):
        if not dist:
            return result
        out, got = result
        store.update(zip(names, got))
        return out

    hn = _rmsnorm_fwd(x, small["norm_gain"], "rms_x_fwd")
    got = {}
    proj = hosted(_mm_nn(hn, wg["qkv"], bm=1024, bn=1024, bk=d, o_dtype=BF16, name="proj_qkv",
                         comm=gather(("wa", "wb"))), ("wa", "wb"), got)
    wg.update({n: flat8(a) for n, a in got.items()})
    pfb = _mm_nn(hn, wg["wf"], bm=1024, bn=FB_PAD, bk=d, o_dtype=F32, name="proj_fb")
    mn = _rmsnorm_fwd(mem, small["mem_norm_gain"], "rms_mem_fwd")
    mkv = _mm_nn(mn, wg["wk"], bm=256, bn=1024, bk=d, o_dtype=F32, name="mem_kv")

    gain_a = jnp.concatenate([_row(small["q_gain_a"], A_Q_HEADS) * scale_ab, _row(small["k_gain_a"], A_KV_HEADS), ones(A_KV_WIDTH)], axis=1)
    flag_a = jnp.concatenate([ones(A_WIDTH + A_KV_WIDTH), zeros(A_KV_WIDTH)], axis=1)
    qkv_a = _headnorm_fwd(proj, COL_QA, 1280, 1280, HEAD_DIM, gain_a, flag_a, True, "hn_a_fwd")
    gain_b = jnp.concatenate([_row(small["q_gain_b"], B_HEADS) * scale_ab, _row(small["k_gain_b"], B_HEADS), ones(B_WIDTH)], axis=1)
    flag_b = jnp.concatenate([ones(2 * B_WIDTH), zeros(B_WIDTH)], axis=1)
    qkv_b = _headnorm_fwd(proj, COL_QB, 2304, 256, HEAD_DIM, gain_b, flag_b, True, "hn_b_fwd")
    gain_cq = _row(small["q_gain_c"], C_HEADS)
    q_c = _headnorm_fwd(proj, COL_QC, C_WIDTH, C_WIDTH, C_HEAD_DIM, gain_cq, ones(C_WIDTH), False, "hn_cq_fwd")
    gain_ck = jnp.concatenate([_row(small["k_gain_c"], C_HEADS), ones(C_WIDTH)], axis=1)
    flag_ck = jnp.concatenate([ones(C_WIDTH), zeros(C_WIDTH)], axis=1)
    mkvn = _headnorm_fwd(mkv, 0, 2 * C_WIDTH, 2 * C_WIDTH, C_HEAD_DIM, gain_ck, flag_ck, False, "hn_ck_fwd")


    bpad = jnp.pad(small["b_forget"].reshape(1, -1), ((0, 0), (0, FB_PAD - B_HEADS)))
    c16 = _fox_prep(pfb, bpad, "fox_prep")
    c3 = c16[0:B_HEADS].reshape(B_HEADS, 1, s)

    sinks = small["sinks_a"].reshape(-1)
    slopes = jnp.exp2(-8.0 * jnp.arange(1, A_Q_HEADS + 1, dtype=F32) / A_Q_HEADS)
    y_a, lse_a = _attn_a_fwd(qkv_a, sinks, slopes, "attn_a_fwd")
    y_b, lse_b, got_zg = _attn_b_fwd(qkv_b, c3, "attn_b_fwd", comm=gather(("zg",)))
    if dist:
        wg["zg"] = flat8(got_zg[0])
    y_c = _attn_c_fwd(q_c, mkvn, "attn_c_fwd")

    got = {}
    pzg = hosted(_mm_nn(hn, wg["zg"], bm=1024, bn=1024, bk=d, o_dtype=BF16, name="proj_zg", comm=gather(("wo", "wc"))),
                 ("wo", "wc"), got)
    wg.update({n: flat8(a) for n, a in got.items()})

    s_a = _gate_fwd(y_a, pzg, COL_ZA, 256, "gate_a_fwd")
    s_b = _gate_fwd(y_b, pzg, COL_ZB, 256, "gate_b_fwd")
    s_c = _gate_fwd(y_c, pzg, COL_ZC, 512, "gate_c_fwd")
    w_a, w_b, w_c = _branch_full(wg["wa"]), _branch_full(wg["wb"]), _branch_full(wg["wc"])
    u_a = _mm_nn(s_a, w_a, bm=1024, bn=2048, bk=A_WIDTH, o_dtype=BF16, name="branch_a_fwd")
    u_b = _mm_nn(s_b, w_b, bm=1024, bn=2048, bk=B_WIDTH, o_dtype=BF16, name="branch_b_fwd")
    u_c = _mm_nn(s_c, w_c, bm=1024, bn=2048, bk=C_WIDTH, o_dtype=BF16, name="branch_c_fwd")
    ym, gate_a, gate_b, gate_c = _merge_fwd(pzg, u_a, u_b, u_c, "merge_fwd")
    dy, dyb, lpart = _out_proj_loss(ym, wg["wo"], x, target, "out_proj_loss")
    loss = 0.5 / d * jnp.sum(lpart)

    dym = _mm_nt(dyb, wg["wo"], bm=1024, bn=1024, bk=d, o_dtype=F32, name="out_proj_bwd_act")
    g["wo"] = _mm_tn(ym, dyb, bm=512, bn=1024, bk=s, o_dtype=BF16, name="out_proj_bwd_w")

    dgate, du_a, du_b, du_c = _merge_bwd(dym, (u_a, u_b, u_c), (gate_a, gate_b, gate_c), "merge_bwd")
    parts = {}
    g["wm_g"] = hosted(_mm_tn(hn, dgate, bm=512, bn=1024, bk=s, o_dtype=BF16, name="proj_gate_bwd_w",
                              comm=scatter(("wo",))), ("wo",), parts)

    ds_a = _mm_nt(du_a, w_a, bm=1024, bn=A_WIDTH, bk=d, o_dtype=F32, name="branch_a_bwd_act")
    ds_b = _mm_nt(du_b, w_b, bm=1024, bn=B_WIDTH, bk=d, o_dtype=F32, name="branch_b_bwd_act")
    ds_c = _mm_nt(du_c, w_c, bm=1024, bn=C_WIDTH, bk=d, o_dtype=F32, name="branch_c_bwd_act")
    g["wa"] = _branch_bwd_w(s_a, du_a, "branch_a_bwd_w")
    g["wb"] = _branch_bwd_w(s_b, du_b, "branch_b_bwd_w")
    g["wc"] = _branch_bwd_w(s_c, du_c, "branch_c_bwd_w")

    dz = lax.empty((s, W_Z), BF16)
    dz, do_a, dd_a = _gate_bwd(ds_a, y_a, pzg, COL_ZA, 256, dz, COL_ZA, True, "gate_a_bwd")
    dz, do_b, dd_b = _gate_bwd(ds_b, y_b, pzg, COL_ZB, 256, dz, COL_ZB, True, "gate_b_bwd")
    dz, do_c, _ = _gate_bwd(ds_c, y_c, pzg, COL_ZC, 512, dz, COL_ZC, False, "gate_c_bwd")
    g["wm_z"] = _mm_tn(hn, dz, bm=512, bn=1024, bk=s, o_dtype=BF16, name="proj_z_bwd_w")

    names = ("wa", "wb", "wc")
    dq_a, dkv_a, dsink, got = _attn_a_bwd(qkv_a, do_a, lse_a, dd_a, sinks, slopes, "attn_a_bwd", comm=scatter(names))
    parts.update(zip(names, got))
    names = ("wm_g", "wm_z")
    dq_b, dk_b, dv_b, dc3, got = _attn_b_bwd(qkv_b, do_b, lse_b, dd_b, c3, "attn_b_bwd", comm=scatter(names))
    parts.update(zip(names, got))
    dq_c, dmkvn = _attn_c_bwd(q_c, mkvn, do_c, "attn_c_bwd")

    dqkv = lax.empty((s, W_QKV), BF16)
    dqkv, dg_qa = _headnorm_bwd(proj, COL_QA, A_WIDTH, 256, HEAD_DIM, gain_a[:, 0:768], flag_a[:, 0:768], dq_a, dqkv, COL_QA, "hn_qa_bwd")
    dqkv, dg_kva = _headnorm_bwd(proj, COL_KA, 512, 256, HEAD_DIM, gain_a[:, 768:1280], flag_a[:, 768:1280], dkv_a, dqkv, COL_KA, "hn_kva_bwd")
    dqkv, dg_qb = _headnorm_bwd(proj, COL_QB, B_WIDTH, 256, HEAD_DIM, gain_b[:, 0:768], flag_b[:, 0:768], dq_b, dqkv, COL_QB, "hn_qb_bwd")
    dqkv, dg_kb = _headnorm_bwd(proj, COL_KB, B_WIDTH, 256, HEAD_DIM, gain_b[:, 768:1536], flag_b[:, 768:1536], dk_b, dqkv, COL_KB, "hn_kb_bwd")
    dqkv, _ = _headnorm_bwd(proj, COL_VB, B_WIDTH, 256, HEAD_DIM, gain_b[:, 1536:2304], flag_b[:, 1536:2304], dv_b, dqkv, COL_VB, "hn_vb_bwd")
    dqkv, dg_qc = _headnorm_bwd(proj, COL_QC, C_WIDTH, 512, C_HEAD_DIM, gain_cq, ones(C_WIDTH), dq_c, dqkv, COL_QC, "hn_qc_bwd")
    dmkv, dg_kc = _headnorm_bwd(mkv, 0, 2 * C_WIDTH, 2 * C_WIDTH, C_HEAD_DIM, gain_ck, flag_ck, dmkvn, None, 0, "hn_kc_bwd")

    dct = jnp.pad(dc3.reshape(B_HEADS, s), ((0, 16 - B_HEADS), (0, 0)))
    dfb, dbf = _fox_prep_bwd(pfb, bpad, dct, "fox_prep_bwd")

    dmn = _mm_nt(dmkv, wg["wk"], bm=256, bn=1024, bk=1024, o_dtype=F32, name="mem_kv_bwd_act")
    g["wk"] = _mm_tn(mn, dmkv, bm=512, bn=1024, bk=mem.shape[0], o_dtype=BF16, name="mem_kv_bwd_w")
    _, dg_mem = _rmsnorm_bwd(mem, dmn, small["mem_norm_gain"], None, "rms_mem_bwd")

    g["wm_qkv"] = _mm_tn(hn, dqkv, bm=512, bn=1024, bk=s, o_dtype=BF16, name="proj_qkv_bwd_w")
    g["wf"] = _mm_tn(hn, dfb, bm=512, bn=FB_PAD, bk=s, o_dtype=BF16, name="proj_fb_bwd_w")
    half = Q_SPLIT
    g["wm_q1"], g["wm_q2"] = g["wm_qkv"][:, 0:half], g["wm_qkv"][:, half:W_QKV]
    names = ("wm_q1",)
    dhn = hosted(_mm_nt_sum([(dqkv, wg["qkv"], 0), (dfb, wg["wf"], 0)], bm=1024, bn=1024, bk=2048,
                            name="proj_qkv_bwd_act", comm=scatter(names)), names, parts)
    names = ("wm_q2", "wf", "wk")
    dhn = hosted(_mm_nt_sum([(dz, wg["zg"], COL_ZA), (dgate, wg["zg"], COL_GATE)], bm=1024, bn=1024, bk=2048,
                            name="proj_zg_bwd_act", add=dhn, comm=scatter(names)), names, parts)
    if dist:
        g = parts
    grad_x, dg_x = _rmsnorm_bwd(x, dhn, small["norm_gain"], dy, "rms_x_bwd")

    fold = lambda part, heads, hd: jnp.sum(jnp.sum(part, axis=0).reshape(heads, hd), axis=0).reshape(1, hd)
    small_grads = {
        "norm_gain": jnp.sum(dg_x, axis=0).reshape(1, d),
        "mem_norm_gain": jnp.sum(dg_mem, axis=0).reshape(1, d),
        "b_forget": dbf[0:B_HEADS, 0].reshape(1, B_HEADS),
        "q_gain_a": fold(dg_qa, A_Q_HEADS, HEAD_DIM) * scale_ab,
        "k_gain_a": fold(dg_kva[:, 0:A_KV_WIDTH], A_KV_HEADS, HEAD_DIM),
        "sinks_a": (jnp.sum(dsink, axis=(1, 2)) * (1.0 / HEAD_DIM)).reshape(1, A_Q_HEADS),
        "q_gain_b": fold(dg_qb, B_HEADS, HEAD_DIM) * scale_ab,
        "k_gain_b": fold(dg_kb, B_HEADS, HEAD_DIM),
        "q_gain_c": fold(dg_qc, C_HEADS, C_HEAD_DIM),
        "k_gain_c": fold(dg_kc[:, 0:C_WIDTH], C_HEADS, C_HEAD_DIM),
    }
    return loss, grad_x, small_grads, g


def _coords():
    return lax.axis_index("x"), lax.axis_index("y"), lax.axis_index("c")


def _all_gather(shards, name):
    n = len(shards)

    def body(*refs):
        ins = refs[0:n]
        outs = refs[n:2 * n]
        send_sems, recv_sems, local_sems = refs[2 * n:2 * n + 3]
        x, y, c = _coords()
        me, sibling = (x, y, c), (x, y, 1 - c)
        chips = [(1 - x, y), (x, 1 - y), (1 - x, 1 - y)]
        idx = lambda p: 4 * p[0] + 2 * p[1] + p[2]

        def copy(a, k, block, to, src=None):
            slot = outs[a].at[idx(block)]
            return pltpu.make_async_remote_copy(
                src_ref=slot if src is None else src, dst_ref=slot,
                send_sem=send_sems.at[a, k], recv_sem=recv_sems.at[a, k], device_id=to, device_id_type=MESH)

        mine = [pltpu.make_async_copy(ins[a], outs[a].at[idx(me)], local_sems.at[a]) for a in range(n)]
        for cp in mine:
            cp.start()
        first = []
        for a in range(n):
            first.append(copy(a, 0, me, sibling, src=ins[a]))
            first += [copy(a, 1 + j, me, (*chip, c), src=ins[a]) for j, chip in enumerate(chips)]
        for cp in first:
            cp.start()
        passed = []
        for j, chip in enumerate(chips):
            for a in range(n):
                copy(a, 1 + j, (*chip, c), me).wait_recv()
                fwd = copy(a, 4 + j, (*chip, c), sibling)
                fwd.start()
                passed.append(fwd)
        for a in range(n):
            copy(a, 0, sibling, me).wait_recv()
            for j, chip in enumerate(chips):
                copy(a, 4 + j, (*chip, 1 - c), me).wait_recv()
        for cp in first + passed:
            cp.wait_send()
        for cp in mine:
            cp.wait()

    any_spec = pl.BlockSpec(memory_space=pl.ANY)
    return pl.pallas_call(
        body, name=name,
        in_specs=[any_spec] * n, out_specs=[any_spec] * n,
        out_shape=[jax.ShapeDtypeStruct((N_DEV,) + sh.shape, sh.dtype) for sh in shards],
        scratch_shapes=[pltpu.SemaphoreType.DMA((n, 7)), pltpu.SemaphoreType.DMA((n, 7)), pltpu.SemaphoreType.DMA((n,))],
    )(*shards)


def _sum_parts(parts, name):
    _, rows, cols = parts.shape
    br = _tile(rows, 64, 16)

    def body(p_ref, o_ref):
        total = p_ref[0].astype(F32)
        for j in range(1, N_DEV):
            total = total + p_ref[j].astype(F32)
        o_ref[...] = total

    return pl.pallas_call(
        body, name=name, grid=(rows // br,),
        in_specs=[pl.BlockSpec((N_DEV, br, cols), lambda i: (0, i, 0))],
        out_specs=pl.BlockSpec((br, cols), lambda i: (i, 0)),
        out_shape=jax.ShapeDtypeStruct((rows, cols), F32),
        compiler_params=_params(("parallel",), VMEM_BIG),
    )(parts)


def _adamw(w, g, m, v, name, br=32):
    rows, cols = w.shape
    br = min(br, rows)
    c1 = 1.0 / (1.0 - ADAM_B1 ** ADAM_STEP)
    c2 = 1.0 / (1.0 - ADAM_B2 ** ADAM_STEP)

    def body(w_ref, g_ref, m_ref, v_ref, d_ref, nm_ref, nv_ref):
        gv = g_ref[...]
        nm = ADAM_B1 * m_ref[...] + (1.0 - ADAM_B1) * gv
        nv = ADAM_B2 * v_ref[...] + (1.0 - ADAM_B2) * (gv * gv)
        d_ref[...] = -ADAM_LR * ((nm * c1) / (jnp.sqrt(nv * c2) + ADAM_EPS) + ADAM_WD * w_ref[...])
        nm_ref[...] = nm
        nv_ref[...] = nv

    spec = pl.BlockSpec((br, cols), lambda i: (i, 0))
    shape = jax.ShapeDtypeStruct((rows, cols), F32)
    return pl.pallas_call(
        body, name=name, grid=(pl.cdiv(rows, br),), in_specs=[spec] * 4, out_specs=[spec] * 3, out_shape=[shape] * 3,
        compiler_params=_params(("parallel",), VMEM_BIG),
    )(w, g, m, v)


def _adamw_t(wt, g, mt, vt, name, br=1024, comm=None):
    n, r = wt.shape
    c1 = 1.0 / (1.0 - ADAM_B1 ** ADAM_STEP)
    c2 = 1.0 / (1.0 - ADAM_B2 ** ADAM_STEP)

    def body(w_ref, g_ref, m_ref, v_ref, d_ref, nm_ref, nv_ref):
        gv = g_ref[...].T
        nm = ADAM_B1 * m_ref[...] + (1.0 - ADAM_B1) * gv
        nv = ADAM_B2 * v_ref[...] + (1.0 - ADAM_B2) * (gv * gv)
        d_ref[...] = -ADAM_LR * ((nm * c1) / (jnp.sqrt(nv * c2) + ADAM_EPS) + ADAM_WD * w_ref[...])
        nm_ref[...] = nm
        nv_ref[...] = nv

    spec = pl.BlockSpec((br, r), lambda i: (i, 0))
    shape = jax.ShapeDtypeStruct((n, r), F32)
    return _hosted_call(
        body, comm, name=name, grid=(pl.cdiv(n, br),),
        in_specs=[spec, pl.BlockSpec((r, br), lambda i: (0, i)), spec, spec], out_specs=[spec] * 3, out_shape=[shape] * 3,
        scratch_shapes=[], args=[wt, g, mt, vt], sem=("parallel",), vmem=VMEM_BIG)


def _adamw_parts(w, parts, m, v, name):
    rows, cols = w.shape
    br = _tile(rows, 32, 16)
    c1 = 1.0 / (1.0 - ADAM_B1 ** ADAM_STEP)
    c2 = 1.0 / (1.0 - ADAM_B2 ** ADAM_STEP)

    def body(w_ref, p_ref, m_ref, v_ref, g_ref, d_ref, nm_ref, nv_ref):
        gv = p_ref[0].astype(F32)
        for j in range(1, N_DEV):
            gv = gv + p_ref[j].astype(F32)
        nm = ADAM_B1 * m_ref[...] + (1.0 - ADAM_B1) * gv
        nv = ADAM_B2 * v_ref[...] + (1.0 - ADAM_B2) * (gv * gv)
        g_ref[...] = gv
        d_ref[...] = -ADAM_LR * ((nm * c1) / (jnp.sqrt(nv * c2) + ADAM_EPS) + ADAM_WD * w_ref[...])
        nm_ref[...] = nm
        nv_ref[...] = nv

    spec = pl.BlockSpec((br, cols), lambda i: (i, 0))
    shape = jax.ShapeDtypeStruct((rows, cols), F32)
    return pl.pallas_call(
        body, name=name, grid=(rows // br,),
        in_specs=[spec, pl.BlockSpec((N_DEV, br, cols), lambda i: (0, i, 0)), spec, spec],
        out_specs=[spec] * 4, out_shape=[shape] * 4,
        compiler_params=_params(("parallel",), VMEM_BIG),
    )(w, parts, m, v)


SMALL_NAMES = ("norm_gain", "mem_norm_gain", "b_forget", "q_gain_a", "k_gain_a", "sinks_a",
               "q_gain_b", "k_gain_b", "q_gain_c", "k_gain_c")
BIG_NAMES = ("w_in", "w_mem_kv", "w_branch_a", "w_branch_b", "w_branch_c", "w_out")
WEIGHT_ORDER = ("norm_gain", "mem_norm_gain", "w_in", "b_forget", "q_gain_a", "k_gain_a", "sinks_a", "q_gain_b",
                "k_gain_b", "q_gain_c", "k_gain_c", "w_mem_kv", "w_branch_a", "w_branch_b", "w_branch_c", "w_out")


def _pack_small(tree):
    flat = jnp.concatenate([tree[n].reshape(1, -1) for n in SMALL_NAMES], axis=1)
    pad = (-flat.shape[1]) % LANES
    return jnp.pad(flat, ((0, 0), (0, pad)))


def _unpack_small(flat, like):
    out, off = {}, 0
    for n in SMALL_NAMES:
        size = like[n].size
        out[n] = flat[:, off:off + size].reshape(like[n].shape)
        off += size
    return out


def kernel(x, mem, norm_gain, mem_norm_gain, w_in, b_forget, q_gain_a, k_gain_a, sinks_a, q_gain_b, k_gain_b, q_gain_c, k_gain_c, w_mem_kv, w_branch_a, w_branch_b, w_branch_c, w_out, loss_target, m_norm_gain, m_mem_norm_gain, m_w_in, m_b_forget, m_q_gain_a, m_k_gain_a, m_sinks_a, m_q_gain_b, m_k_gain_b, m_q_gain_c, m_k_gain_c, m_w_mem_kv, m_w_branch_a, m_w_branch_b, m_w_branch_c, m_w_out, v_norm_gain, v_mem_norm_gain, v_w_in, v_b_forget, v_q_gain_a, v_k_gain_a, v_sinks_a, v_q_gain_b, v_k_gain_b, v_q_gain_c, v_k_gain_c, v_w_mem_kv, v_w_branch_a, v_w_branch_b, v_w_branch_c, v_w_out):
    weights = dict(norm_gain=norm_gain, mem_norm_gain=mem_norm_gain, w_in=w_in, b_forget=b_forget, q_gain_a=q_gain_a,
                   k_gain_a=k_gain_a, sinks_a=sinks_a, q_gain_b=q_gain_b, k_gain_b=k_gain_b, q_gain_c=q_gain_c,
                   k_gain_c=k_gain_c, w_mem_kv=w_mem_kv, w_branch_a=w_branch_a, w_branch_b=w_branch_b,
                   w_branch_c=w_branch_c, w_out=w_out)
    mom_m = dict(norm_gain=m_norm_gain, mem_norm_gain=m_mem_norm_gain, w_in=m_w_in, b_forget=m_b_forget,
                 q_gain_a=m_q_gain_a, k_gain_a=m_k_gain_a, sinks_a=m_sinks_a, q_gain_b=m_q_gain_b, k_gain_b=m_k_gain_b,
                 q_gain_c=m_q_gain_c, k_gain_c=m_k_gain_c, w_mem_kv=m_w_mem_kv, w_branch_a=m_w_branch_a,
                 w_branch_b=m_w_branch_b, w_branch_c=m_w_branch_c, w_out=m_w_out)
    mom_v = dict(norm_gain=v_norm_gain, mem_norm_gain=v_mem_norm_gain, w_in=v_w_in, b_forget=v_b_forget,
                 q_gain_a=v_q_gain_a, k_gain_a=v_k_gain_a, sinks_a=v_sinks_a, q_gain_b=v_q_gain_b, k_gain_b=v_k_gain_b,
                 q_gain_c=v_q_gain_c, k_gain_c=v_k_gain_c, w_mem_kv=v_w_mem_kv, w_branch_a=v_w_branch_a,
                 w_branch_b=v_w_branch_b, w_branch_c=v_w_branch_c, w_out=v_w_out)
    wi = w_in[0]
    sh_qkv = jnp.concatenate([wi[:, a:b] for a, b in SRC_RANGES[0:3]], axis=1).astype(BF16)
    sh_zg = jnp.concatenate([wi[:, a:b] for a, b in SRC_RANGES[3:6]] + [wi[:, SRC_GATE:]], axis=1).astype(BF16)
    sh_wf = jnp.pad(wi[:, FB_SRC:FB_SRC + B_HEADS], ((0, 0), (0, FB_PAD - B_HEADS))).astype(BF16)
    shards = {"zg": sh_zg, "wo": w_out[0].astype(BF16), "wa": w_branch_a[0].astype(BF16),
              "wb": w_branch_b[0].astype(BF16), "wc": w_branch_c[0].astype(BF16)}
    first = ("qkv", "wf", "wk")
    full = _all_gather([sh_qkv, sh_wf, w_mem_kv[0].astype(BF16)], "weights_all_gather")
    wg = {kname: arr.reshape(arr.shape[0] * arr.shape[1], arr.shape[2]) for kname, arr in zip(first, full)}

    small = {n: weights[n] for n in SMALL_NAMES}
    loss_local, grad_x, small_g, parts = _local_step(x[0], mem[0], loss_target[0], small, wg, shards)

    grads, delta, new_m, new_v = {}, {}, {}, {}
    g1, g2, gz, gf, gg = (_sum_parts(parts[k], "grad_sum_" + k) for k in ("wm_q1", "wm_q2", "wm_z", "wf", "wm_g"))
    half = Q_SPLIT
    g_in = jnp.concatenate([g1, g2[:, 0:COL_QB - half], gz[:, COL_ZA:COL_ZB], g2[:, COL_QB - half:COL_QC - half],
                            gz[:, COL_ZB:COL_ZC], gf[:, 0:B_HEADS], g2[:, COL_QC - half:W_QKV - half], gz[:, COL_ZC:W_Z], gg], axis=1)
    packed = _pack_small(small_g)
    packed = jnp.concatenate([packed[:, :-1], loss_local.reshape(1, 1)], axis=1)
    all_small = _Comm("gather", [jnp.broadcast_to(packed, (8, packed.shape[1]))])
    (dlt, nm, nv), (packed8,) = _adamw_t(w_in[0].T, g_in, m_w_in[0].T, v_w_in[0].T, "adamw_w_in", comm=all_small)
    others = ("wk", "wo", "wa", "wb", "wc")
    (dlt, nm, nv), held = lax.optimization_barrier(((dlt, nm, nv), [parts[k] for k in others]))
    parts.update(zip(others, held))
    grads["w_in"], delta["w_in"], new_m["w_in"], new_v["w_in"] = g_in, dlt.T[None], nm.T[None], nv.T[None]
    reduced = _sum_parts(packed8, "small_sum")[0:1]
    grads.update(_unpack_small(reduced, small))
    loss = reduced[0, -1]
    for n, kname in (("w_mem_kv", "wk"), ("w_out", "wo"), ("w_branch_a", "wa"), ("w_branch_b", "wb"), ("w_branch_c", "wc")):
        gsum, dlt, nm, nv = _adamw_parts(weights[n][0], parts[kname], mom_m[n][0], mom_v[n][0], "adamw_" + n)
        grads[n], delta[n], new_m[n], new_v[n] = gsum, dlt[None], nm[None], nv[None]

    pw, pm, pv = _pack_small(small), _pack_small({n: mom_m[n] for n in SMALL_NAMES}), _pack_small({n: mom_v[n] for n in SMALL_NAMES})
    rep8 = lambda a: jnp.broadcast_to(a, (8, a.shape[1]))
    dlt, nm, nv = _adamw(rep8(pw), rep8(reduced), rep8(pm), rep8(pv), "adamw_small")
    for tree, flat in ((delta, dlt), (new_m, nm), (new_v, nv)):
        tree.update(_unpack_small(flat[0:1], small))
    for n in BIG_NAMES:
        grads[n] = grads[n][None]
    return (loss, grad_x[None], *[grads[n] for n in WEIGHT_ORDER], *[delta[n] for n in WEIGHT_ORDER],
            *[new_m[n] for n in WEIGHT_ORDER], *[new_v[n] for n in WEIGHT_ORDER])
```

```python
import math

import jax
import jax.numpy as jnp
import numpy as np
from jax import lax
from jax.experimental import pallas as pl
from jax.experimental.pallas import tpu as pltpu

F32 = jnp.float32
BF16 = jnp.bfloat16

N_DEV = 8
HEAD_DIM = 64
A_Q_HEADS = 12
A_KV_HEADS = 4
A_GROUP = 3
B_HEADS = 12
C_HEADS = 4
C_HEAD_DIM = 128
WINDOW = 128
A_WIDTH = 768
A_KV_WIDTH = 256
B_WIDTH = 768
C_WIDTH = 512
EPS = 1e-6
NEG = -1e30

COL_QA, COL_KA, COL_VA = 0, 768, 1024
COL_QB, COL_KB, COL_VB = 1280, 2048, 2816
COL_QC = 3584
W_QKV = 4096
Q_SPLIT = 1280
COL_ZA, COL_ZB, COL_ZC = 0, 768, 1536
COL_GATE = W_Z = 2048
SRC_RANGES = ((0, 1280), (2048, 4352), (5132, 5644), (1280, 2048), (4352, 5120), (5644, 6156))
SRC_GATE = 6156
FB_SRC = 5120
FB_PAD = 128

ADAM_LR = 0.001
ADAM_B1 = 0.9
ADAM_B2 = 0.999
ADAM_EPS = 1e-08
ADAM_WD = 0.01
ADAM_STEP = 10

VMEM_BIG = 52 * 1024 * 1024
LANES = 128
MESH = pl.DeviceIdType.MESH


def _tile(n, pref, mult=128):
    if n <= pref:
        return n
    t = (pref // mult) * mult
    while t >= mult:
        if n % t == 0:
            return t
        t -= mult
    return n


def _params(sem=None, vmem=None):
    kw = {}
    if sem is not None:
        kw["dimension_semantics"] = sem
    if vmem is not None:
        kw["vmem_limit_bytes"] = vmem
    return pltpu.CompilerParams(**kw)


def _sigmoid(x):
    return 1.0 / (1.0 + jnp.exp(-x))


def _block_diag(hd):
    r = np.arange(LANES)
    return jnp.asarray((r[:, None] // hd) == (r[None, :] // hd), dtype=BF16)


def _seg_sum(t, bd):
    hi = t.astype(BF16)
    lo = (t - hi.astype(F32)).astype(BF16)
    outs = []
    for c in range(t.shape[1] // LANES):
        sl = slice(c * LANES, (c + 1) * LANES)
        outs.append(jnp.dot(hi[:, sl], bd, preferred_element_type=F32) + jnp.dot(lo[:, sl], bd, preferred_element_type=F32))
    return outs[0] if len(outs) == 1 else jnp.concatenate(outs, axis=1)


def _rmsnorm_fwd(x, gain, name):
    rows, d = x.shape
    bm = _tile(rows, 512, 8)

    def body(x_ref, g_ref, o_ref):
        xv = x_ref[...]
        ms = jnp.mean(xv * xv, axis=-1, keepdims=True)
        o_ref[...] = (xv * lax.rsqrt(ms + EPS) * g_ref[...]).astype(BF16)

    return pl.pallas_call(
        body, name=name, grid=(rows // bm,),
        in_specs=[pl.BlockSpec((bm, d), lambda i: (i, 0)), pl.BlockSpec((1, d), lambda i: (0, 0))],
        out_specs=pl.BlockSpec((bm, d), lambda i: (i, 0)),
        out_shape=jax.ShapeDtypeStruct((rows, d), BF16),
        compiler_params=_params(("parallel",)),
    )(x, gain)


def _rmsnorm_bwd(x, dhn, gain, dy, name):
    rows, d = x.shape
    bm = _tile(rows, 512, 8)
    with_dx = dy is not None

    def body(*refs):
        if with_dx:
            x_ref, dh_ref, g_ref, dy_ref, gx_ref, dg_ref = refs
        else:
            x_ref, dh_ref, g_ref, dg_ref = refs
        i = pl.program_id(0)
        xv = x_ref[...]
        rstd = lax.rsqrt(jnp.mean(xv * xv, axis=-1, keepdims=True) + EPS)
        xhat = xv * rstd
        dh = dh_ref[...]
        part = jnp.sum((dh * xhat).reshape(bm // 8, 8, d), axis=0)

        @pl.when(i == 0)
        def _():
            dg_ref[...] = part

        @pl.when(i > 0)
        def _():
            dg_ref[...] += part

        if with_dx:
            g = dh * g_ref[...]
            mean = jnp.mean(g * xhat, axis=-1, keepdims=True)
            gx_ref[...] = dy_ref[...] + rstd * (g - xhat * mean)

    row_spec = pl.BlockSpec((bm, d), lambda i: (i, 0))
    in_specs = [row_spec, row_spec, pl.BlockSpec((1, d), lambda i: (0, 0))]
    args = [x, dhn, gain]
    dg_spec = pl.BlockSpec((8, d), lambda i: (0, 0))
    dg_shape = jax.ShapeDtypeStruct((8, d), F32)
    if with_dx:
        in_specs.append(row_spec)
        args.append(dy)
        out_specs = [row_spec, dg_spec]
        out_shape = [jax.ShapeDtypeStruct((rows, d), F32), dg_shape]
    else:
        out_specs = [dg_spec]
        out_shape = [dg_shape]
    outs = pl.pallas_call(
        body, name=name, grid=(rows // bm,), in_specs=in_specs, out_specs=out_specs, out_shape=out_shape,
        compiler_params=_params(("arbitrary",), VMEM_BIG),
    )(*args)
    return outs if with_dx else (None, outs[0])


class _Comm:
    def __init__(self, kind, arrays, windows=None):
        self.kind = kind
        self.arrays = list(arrays)
        self.n = len(self.arrays)
        self.windows = list(windows) if windows is not None else [None] * self.n
        assert kind == "scatter" or all(w is None for w in self.windows)

    def out_shapes(self):
        if self.kind == "gather":
            return [jax.ShapeDtypeStruct((N_DEV,) + a.shape, a.dtype) for a in self.arrays]
        return [jax.ShapeDtypeStruct(a.shape if w is None else a.shape[0:2] + (w[1],), a.dtype)
                for a, w in zip(self.arrays, self.windows)]

    def _slice_for(self, ins, a, dev):
        if self.windows[a] is None:
            return ins[a].at[dev]
        col0, width = self.windows[a]
        return ins[a].at[dev, :, pl.ds(col0, width)]

    def scratch(self):
        return [pltpu.SemaphoreType.DMA((self.n, N_DEV - 1)), pltpu.SemaphoreType.DMA((self.n, N_DEV - 1)),
                pltpu.SemaphoreType.DMA((self.n,))]

    def _plan(self, ins, outs, sems, with_recvs):
        send_sems, recv_sems, local_sems = sems
        x, y, c = lax.axis_index("x"), lax.axis_index("y"), lax.axis_index("c")
        my = 4 * x + 2 * y + c
        gather = self.kind == "gather"
        local, sends, recvs = [], [], []
        for a in range(self.n):
            local.append(pltpu.make_async_copy(ins[a] if gather else self._slice_for(ins, a, my), outs[a].at[my], local_sems.at[a]))
            for k in range(1, N_DEV):
                peer = (x ^ ((k >> 2) & 1), y ^ ((k >> 1) & 1), c ^ (k & 1))
                pid = 4 * peer[0] + 2 * peer[1] + peer[2]
                src = ins[a] if gather else self._slice_for(ins, a, pid)
                sem = dict(send_sem=send_sems.at[a, k - 1], recv_sem=recv_sems.at[a, k - 1], device_id=peer, device_id_type=MESH)
                sends.append(pltpu.make_async_remote_copy(src_ref=src, dst_ref=outs[a].at[my], **sem))
                if with_recvs:
                    recvs.append(pltpu.make_async_remote_copy(src_ref=src, dst_ref=outs[a].at[pid], **sem))
        return local, sends, recvs

    def start(self, ins, outs, sems):
        local, sends, _ = self._plan(ins, outs, sems, False)
        for cp in local + sends:
            cp.start()

    def wait(self, ins, outs, sems):
        local, sends, recvs = self._plan(ins, outs, sems, True)
        for cp in recvs:
            cp.wait_recv()
        for cp in sends:
            cp.wait_send()
        for cp in local:
            cp.wait()


def _grid_edges(grid):
    first = last = None
    for ax, size in enumerate(grid):
        pid = pl.program_id(ax)
        f, l = pid == 0, pid == size - 1
        first = f if first is None else first & f
        last = l if last is None else last & l
    return first, last


def _hosted_call(body, comm, *, name, grid, in_specs, out_specs, out_shape, scratch_shapes, args, sem, vmem=None):
    in_specs, out_specs, out_shape, scratch_shapes = list(in_specs), list(out_specs), list(out_shape), list(scratch_shapes)
    if comm is None:
        res = pl.pallas_call(body, name=name, grid=grid, in_specs=in_specs, out_specs=out_specs, out_shape=out_shape,
                             scratch_shapes=scratch_shapes, compiler_params=_params(sem, vmem))(*args)
        return list(res), []
    n_in, n_out, n_scr, nc = len(in_specs), len(out_shape), len(scratch_shapes), comm.n

    def hosted(*refs):
        ins = refs[0:n_in]
        comm_in = refs[n_in:n_in + nc]
        outs = refs[n_in + nc:n_in + nc + n_out]
        comm_out = refs[n_in + nc + n_out:n_in + 2 * nc + n_out]
        scr = refs[n_in + 2 * nc + n_out:n_in + 2 * nc + n_out + n_scr]
        sems = refs[n_in + 2 * nc + n_out + n_scr:]
        first, last = _grid_edges(grid)

        @pl.when(first)
        def _():
            comm.start(comm_in, comm_out, sems)

        body(*ins, *outs, *scr)

        @pl.when(last)
        def _():
            comm.wait(comm_in, comm_out, sems)

    any_spec = pl.BlockSpec(memory_space=pl.ANY)
    res = pl.pallas_call(
        hosted, name=name, grid=grid, in_specs=in_specs + [any_spec] * nc, out_specs=out_specs + [any_spec] * nc,
        out_shape=out_shape + comm.out_shapes(), scratch_shapes=scratch_shapes + comm.scratch(),
        compiler_params=_params(("arbitrary",) * len(grid), vmem),
    )(*args, *comm.arrays)
    return list(res[0:n_out]), list(res[n_out:])


def _mm(a, b, *, grid, a_spec, b_spec, o_spec, o_shape, o_dtype, contract, name, add=None, add_spec=None, acc_shape=None,
        comm=None):
    nk = grid[2]
    has_add = add is not None

    def body(*refs):
        a_ref, b_ref = refs[0], refs[1]
        add_ref = refs[2] if has_add else None
        o_ref = refs[3] if has_add else refs[2]
        part = lax.dot_general(a_ref[...], b_ref[...], (contract, ((), ())), preferred_element_type=F32)
        if nk == 1:
            if has_add:
                part = part + add_ref[...]
            o_ref[...] = part.astype(o_dtype)
        else:
            acc = refs[-1]
            k = pl.program_id(2)

            @pl.when(k == 0)
            def _():
                acc[...] = part

            @pl.when(k > 0)
            def _():
                acc[...] += part

            @pl.when(k == nk - 1)
            def _():
                r = acc[...]
                if has_add:
                    r = r + add_ref[...]
                o_ref[...] = r.astype(o_dtype)

    in_specs = [a_spec, b_spec] + ([add_spec] if has_add else [])
    args = [a, b] + ([add] if has_add else [])
    scratch = [pltpu.VMEM(acc_shape, F32)] if nk > 1 else []
    outs, comm_outs = _hosted_call(
        body, comm, name=name, grid=grid, in_specs=in_specs, out_specs=[o_spec],
        out_shape=[jax.ShapeDtypeStruct(o_shape, o_dtype)], scratch_shapes=scratch, args=args,
        sem=("parallel", "parallel", "arbitrary"), vmem=VMEM_BIG)
    return outs[0] if comm is None else (outs[0], comm_outs)


def _mm_nn(a, b, *, bm, bn, bk, o_dtype, name, add=None, comm=None):
    m, kd = a.shape
    n = b.shape[1]
    bm, bn, bk = _tile(m, bm, 8), _tile(n, bn), _tile(kd, bk)
    o_spec = pl.BlockSpec((bm, bn), lambda i, j, k: (i, j))
    return _mm(a, b, grid=(m // bm, n // bn, kd // bk),
               a_spec=pl.BlockSpec((bm, bk), lambda i, j, k: (i, k)),
               b_spec=pl.BlockSpec((bk, bn), lambda i, j, k: (k, j)),
               o_spec=o_spec, o_shape=(m, n), o_dtype=o_dtype, contract=((1,), (0,)), name=name,
               add=add, add_spec=o_spec, acc_shape=(bm, bn), comm=comm)


def _mm_nt(a, b, *, bm, bn, bk, o_dtype, name, add=None, b_col0=0, comm=None):
    m, kd = a.shape
    n = b.shape[0]
    bm, bn, bk = _tile(m, bm, 8), _tile(n, bn), _tile(math.gcd(kd, b_col0), bk)
    kb0 = b_col0 // bk
    o_spec = pl.BlockSpec((bm, bn), lambda i, j, k: (i, j))
    return _mm(a, b, grid=(m // bm, n // bn, kd // bk),
               a_spec=pl.BlockSpec((bm, bk), lambda i, j, k: (i, k)),
               b_spec=pl.BlockSpec((bn, bk), lambda i, j, k: (j, kb0 + k)),
               o_spec=o_spec, o_shape=(m, n), o_dtype=o_dtype, contract=((1,), (1,)), name=name,
               add=add, add_spec=o_spec, acc_shape=(bm, bn), comm=comm)


def _mm_nt_sum(terms, *, bm, bn, bk, name, add=None, comm=None):
    m = terms[0][0].shape[0]
    n = terms[0][1].shape[0]
    bm, bn = _tile(m, bm, 8), _tile(n, bn)
    nt = (((1,), (1,)), ((), ()))
    plan, groups, start = [], [], 0
    for a, b, col0 in terms:
        kd = a.shape[1]
        tk = _tile(math.gcd(kd, col0), bk)
        steps = kd // tk
        if plan and kd < bk:
            groups.append([b, start - 1, 1, col0 // tk, tk])
            plan.append((start - 1, 1, len(groups) - 1, True))
            continue
        last = groups[-1] if groups else None
        if last is not None and last[0] is b and last[4] == tk and (last[3] + last[2]) * tk == col0:
            last[2] += steps
        else:
            groups.append([b, start, steps, col0 // tk, tk])
        plan.append((start, steps, len(groups) - 1, False))
        start += steps
    nk = start
    nterm, ngroup, has_add = len(terms), len(groups), add is not None

    def body(*refs):
        a_refs, b_refs = refs[0:nterm], refs[nterm:nterm + ngroup]
        add_ref = refs[nterm + ngroup] if has_add else None
        o_ref, acc = refs[nterm + ngroup + has_add], refs[nterm + ngroup + has_add + 1]
        k = pl.program_id(2)
        for t, (s0, steps, grp, rides) in enumerate(plan):
            @pl.when((k >= s0) & (k < s0 + steps))
            def _():
                part = lax.dot_general(a_refs[t][...], b_refs[grp][...], nt, preferred_element_type=F32)
                if rides:
                    acc[...] += part
                    return

                @pl.when(k == 0)
                def _():
                    acc[...] = part

                @pl.when(k > 0)
                def _():
                    acc[...] += part

        @pl.when(k == nk - 1)
        def _():
            o_ref[...] = acc[...] + add_ref[...] if has_add else acc[...]

    def a_spec(tk, s0, steps):
        return pl.BlockSpec((bm, tk), lambda i, j, k: (i, jnp.clip(k - s0, 0, steps - 1)))

    def b_spec(tk, s0, steps, off):
        return pl.BlockSpec((bn, tk), lambda i, j, k: (j, off + jnp.clip(k - s0, 0, steps - 1)))

    o_spec = pl.BlockSpec((bm, bn), lambda i, j, k: (i, j))
    in_specs = [a_spec(groups[grp][4], s0, steps) for s0, steps, grp, _ in plan]
    in_specs += [b_spec(tk, s0, steps, cb0) for _, s0, steps, cb0, tk in groups]
    args = [a for a, _, _ in terms] + [grp[0] for grp in groups]
    if has_add:
        in_specs.append(o_spec)
        args.append(add)
    outs, comm_outs = _hosted_call(
        body, comm, name=name, grid=(m // bm, n // bn, nk), in_specs=in_specs,
        out_specs=[o_spec], out_shape=[jax.ShapeDtypeStruct((m, n), F32)],
        scratch_shapes=[pltpu.VMEM((bm, bn), F32)], args=args,
        sem=("parallel", "parallel", "arbitrary"), vmem=VMEM_BIG)
    return outs[0] if comm is None else (outs[0], comm_outs)


def _mm_tn(a, b, *, bm, bn, bk, o_dtype, name, comm=None):
    kd, m = a.shape
    n = b.shape[1]
    bm, bn, bk = _tile(m, bm), _tile(n, bn), _tile(kd, bk, 8)
    return _mm(a, b, grid=(m // bm, n // bn, kd // bk),
               a_spec=pl.BlockSpec((bk, bm), lambda i, j, k: (k, i)),
               b_spec=pl.BlockSpec((bk, bn), lambda i, j, k: (k, j)),
               o_spec=pl.BlockSpec((bm, bn), lambda i, j, k: (i, j)),
               o_shape=(m, n), o_dtype=o_dtype, contract=((0,), (0,)), name=name, acc_shape=(bm, bn), comm=comm)


def _branch_full(w8):
    kb, ds = w8.shape[0] // N_DEV, w8.shape[1]
    return w8.reshape(N_DEV, kb, ds).transpose(1, 0, 2).reshape(kb, N_DEV * ds)


def _branch_bwd_w(s, du, name):
    m, kb = s.shape
    ds = du.shape[1] // N_DEV
    return _mm(s, du, grid=(1, N_DEV, 1),
               a_spec=pl.BlockSpec((m, kb), lambda i, j, k: (0, 0)),
               b_spec=pl.BlockSpec((m, ds), lambda i, j, k: (0, j)),
               o_spec=pl.BlockSpec((kb, ds), lambda i, j, k: (j, 0)),
               o_shape=(N_DEV * kb, ds), o_dtype=BF16, contract=((0,), (0,)), name=name)


def _headnorm_fwd(src, c0, width, bw, hd, gain, nflag, head_major, name):
    rows = src.shape[0]
    bm = _tile(rows, 2048 if bw <= 256 else 1024, 16)
    bd = _block_diag(hd)
    cb0 = c0 // bw

    def body(x_ref, g_ref, f_ref, bd_ref, o_ref):
        xv = x_ref[...].astype(F32)
        ss = _seg_sum(xv * xv, bd_ref[...])
        rstd = lax.rsqrt(ss * (1.0 / hd) + EPS)
        y = (xv * jnp.where(f_ref[...] > 0.0, rstd, 1.0) * g_ref[...]).astype(BF16)
        if head_major:
            for h in range(bw // HEAD_DIM):
                o_ref[h] = y[:, h * HEAD_DIM:(h + 1) * HEAD_DIM]
        else:
            o_ref[...] = y

    vec_spec = pl.BlockSpec((1, bw), lambda i, t: (0, t))
    if head_major:
        hpb = bw // HEAD_DIM
        out_spec = pl.BlockSpec((hpb, bm, HEAD_DIM), lambda i, t: (t, i, 0))
        out_shape = jax.ShapeDtypeStruct((width // HEAD_DIM, rows, HEAD_DIM), BF16)
    else:
        out_spec = pl.BlockSpec((bm, bw), lambda i, t: (i, t))
        out_shape = jax.ShapeDtypeStruct((rows, width), BF16)
    return pl.pallas_call(
        body, name=name, grid=(rows // bm, width // bw),
        in_specs=[pl.BlockSpec((bm, bw), lambda i, t: (i, cb0 + t)), vec_spec, vec_spec,
                  pl.BlockSpec((LANES, LANES), lambda i, t: (0, 0))],
        out_specs=out_spec, out_shape=out_shape,
        compiler_params=_params(("parallel", "parallel")),
    )(src, gain, nflag, bd)


def _headnorm_bwd(src, c0, width, bw, hd, gain, nflag, dyn, target, t0, name):
    rows = src.shape[0]
    bm = _tile(rows, 2048 if bw <= 256 else 1024, 16)
    bd = _block_diag(hd)
    cb0 = c0 // bw
    tb0 = t0 // bw
    aliased = target is not None

    def body(*refs):
        if aliased:
            x_ref, dy_ref, g_ref, f_ref, bd_ref, _, o_ref, dg_ref = refs
        else:
            x_ref, dy_ref, g_ref, f_ref, bd_ref, o_ref, dg_ref = refs
        i = pl.program_id(1)
        xv = x_ref[...].astype(F32)
        dyv = dy_ref[...]
        bdv = bd_ref[...]
        rstd = lax.rsqrt(_seg_sum(xv * xv, bdv) * (1.0 / hd) + EPS)
        xhat = xv * rstd
        g = dyv * g_ref[...]
        mean = _seg_sum(g * xhat, bdv) * (1.0 / hd)
        dx = jnp.where(f_ref[...] > 0.0, rstd * (g - xhat * mean), g)
        o_ref[...] = dx.astype(BF16)
        part = jnp.sum((dyv * xhat).reshape(bm // 8, 8, bw), axis=0)

        @pl.when(i == 0)
        def _():
            dg_ref[...] = part

        @pl.when(i > 0)
        def _():
            dg_ref[...] += part

    vec_spec = pl.BlockSpec((1, bw), lambda t, i: (0, t))
    in_specs = [pl.BlockSpec((bm, bw), lambda t, i: (i, cb0 + t)), pl.BlockSpec((bm, bw), lambda t, i: (i, t)),
                vec_spec, vec_spec, pl.BlockSpec((LANES, LANES), lambda t, i: (0, 0))]
    args = [src, dyn, gain, nflag, bd]
    aliases = {}
    if aliased:
        in_specs.append(pl.BlockSpec(memory_space=pl.ANY))
        args.append(target)
        aliases = {5: 0}
        o_shape = jax.ShapeDtypeStruct(target.shape, BF16)
    else:
        o_shape = jax.ShapeDtypeStruct((rows, width), BF16)
    out, dg = pl.pallas_call(
        body, name=name, grid=(width // bw, rows // bm), in_specs=in_specs,
        out_specs=[pl.BlockSpec((bm, bw), lambda t, i: (i, tb0 + t)), pl.BlockSpec((8, bw), lambda t, i: (0, t))],
        out_shape=[o_shape, jax.ShapeDtypeStruct((8, width), F32)],
        input_output_aliases=aliases,
        compiler_params=_params(("parallel", "arbitrary")),
    )(*args)
    return out, dg


def _fox_prep(pfb, bpad, name):
    s = pfb.shape[0]

    def body(p_ref, b_ref, c_ref):
        z = p_ref[...] + b_ref[...]
        logf = jnp.minimum(z, 0.0) - jnp.log(1.0 + jnp.exp(-jnp.abs(z)))
        x = logf.T[0:16, :]
        lane = lax.broadcasted_iota(jnp.int32, (16, s), 1)
        sh = 1
        while sh < s:
            x = x + jnp.where(lane >= sh, pltpu.roll(x, sh, 1), 0.0)
            sh *= 2
        c_ref[...] = x

    return pl.pallas_call(
        body, name=name, grid=(1,),
        in_specs=[pl.BlockSpec((s, FB_PAD), lambda i: (0, 0)), pl.BlockSpec((1, FB_PAD), lambda i: (0, 0))],
        out_specs=pl.BlockSpec((16, s), lambda i: (0, 0)),
        out_shape=jax.ShapeDtypeStruct((16, s), F32),
        compiler_params=_params(("arbitrary",)),
    )(pfb, bpad)


def _fox_prep_bwd(pfb, bpad, dct, name):
    s = pfb.shape[0]

    def body(p_ref, b_ref, dc_ref, df_ref, db_ref):
        zt = (p_ref[...] + b_ref[...]).T[0:16, :]
        y = dc_ref[...]
        lane = lax.broadcasted_iota(jnp.int32, (16, s), 1)
        sh = 1
        while sh < s:
            y = y + jnp.where(lane < s - sh, pltpu.roll(y, s - sh, 1), 0.0)
            sh *= 2
        dz = y * _sigmoid(-zt)
        db_ref[...] = jnp.broadcast_to(jnp.sum(dz, axis=1, keepdims=True), (16, FB_PAD))
        full = jnp.concatenate([dz, jnp.zeros((FB_PAD - 16, s), F32)], axis=0)
        df_ref[...] = full.T.astype(BF16)

    return pl.pallas_call(
        body, name=name, grid=(1,),
        in_specs=[pl.BlockSpec((s, FB_PAD), lambda i: (0, 0)), pl.BlockSpec((1, FB_PAD), lambda i: (0, 0)),
                  pl.BlockSpec((16, s), lambda i: (0, 0))],
        out_specs=[pl.BlockSpec((s, FB_PAD), lambda i: (0, 0)), pl.BlockSpec((16, FB_PAD), lambda i: (0, 0))],
        out_shape=[jax.ShapeDtypeStruct((s, FB_PAD), BF16), jax.ShapeDtypeStruct((16, FB_PAD), F32)],
        compiler_params=_params(("arbitrary",)),
    )(pfb, bpad, dct)


def _swa_window(n):
    ws = pl.multiple_of(jnp.maximum(n * WINDOW - WINDOW, 0), WINDOW)
    qi = lax.broadcasted_iota(jnp.int32, (WINDOW, 2 * WINDOW), 0)
    kj = lax.broadcasted_iota(jnp.int32, (WINDOW, 2 * WINDOW), 1)
    rel = qi + (n * WINDOW - ws) - kj
    valid = (rel >= 0) & (rel < WINDOW)
    return ws, valid, rel.astype(F32)


def _attn_a_fwd(qkv, sinks, slopes, name):
    s = qkv.shape[1]
    nb = s // WINDOW
    smem = pl.BlockSpec(memory_space=pltpu.SMEM)

    def body(sink_ref, slope_ref, q_ref, k_ref, v_ref, o_ref, lse_ref):
        n = pl.program_id(0)
        ws, valid, relf = _swa_window(n)
        outs = []
        for h in range(A_Q_HEADS):
            kvh = h // A_GROUP
            kw = k_ref[kvh, pl.ds(ws, 2 * WINDOW), :]
            vw = v_ref[kvh, pl.ds(ws, 2 * WINDOW), :]
            sc = lax.dot_general(q_ref[h], kw, (((1,), (1,)), ((), ())), preferred_element_type=F32)
            sc = jnp.where(valid, sc - slope_ref[h] * relf, NEG)
            sink = sink_ref[h]
            m = jnp.maximum(jnp.max(sc, axis=1, keepdims=True), sink)
            p = jnp.exp(sc - m)
            denom = jnp.sum(p, axis=1, keepdims=True) + jnp.exp(sink - m)
            pn = (p / denom).astype(BF16)
            outs.append(jnp.dot(pn, vw, preferred_element_type=F32))
            lse_ref[h] = jnp.broadcast_to(m + jnp.log(denom), (WINDOW, HEAD_DIM))
        o_ref[...] = jnp.concatenate(outs, axis=1)

    return pl.pallas_call(
        body, name=name, grid=(nb,),
        in_specs=[smem, smem,
                  pl.BlockSpec((A_Q_HEADS, WINDOW, HEAD_DIM), lambda n: (0, n, 0)),
                  pl.BlockSpec((A_KV_HEADS, s, HEAD_DIM), lambda n: (A_GROUP, 0, 0)),
                  pl.BlockSpec((A_KV_HEADS, s, HEAD_DIM), lambda n: (A_GROUP + 1, 0, 0))],
        out_specs=[pl.BlockSpec((WINDOW, A_WIDTH), lambda n: (n, 0)),
                   pl.BlockSpec((A_Q_HEADS, WINDOW, HEAD_DIM), lambda n: (0, n, 0))],
        out_shape=[jax.ShapeDtypeStruct((s, A_WIDTH), F32), jax.ShapeDtypeStruct((A_Q_HEADS, s, HEAD_DIM), F32)],
        compiler_params=_params(("parallel",), VMEM_BIG),
    )(sinks, slopes, qkv, qkv, qkv)


def _attn_a_bwd(qkv, do, lse, dd, sinks, slopes, name, comm=None):
    s = qkv.shape[1]
    nb = s // WINDOW
    smem = pl.BlockSpec(memory_space=pltpu.SMEM)
    last = nb - 1

    def body(sink_ref, slope_ref, q_ref, k_ref, v_ref, do_ref, lse_ref, dd_ref, dq_ref, dkv_ref, ds_ref, carry):
        n = pl.program_id(0)

        @pl.when(n == 0)
        def _():
            carry[...] = jnp.zeros(carry.shape, F32)
            ds_ref[...] = jnp.zeros(ds_ref.shape, F32)

        @pl.when(n < nb)
        def _():
            ws, valid, relf = _swa_window(n)
            dqs = []
            dkw = [None] * A_KV_HEADS
            dvw = [None] * A_KV_HEADS
            for h in range(A_Q_HEADS):
                kvh = h // A_GROUP
                qh = q_ref[h]
                doh = do_ref[h]
                kw = k_ref[kvh, pl.ds(ws, 2 * WINDOW), :]
                vw = v_ref[kvh, pl.ds(ws, 2 * WINDOW), :]
                lse_h = lse_ref[h]
                dd_h = dd_ref[h]
                sc = lax.dot_general(qh, kw, (((1,), (1,)), ((), ())), preferred_element_type=F32)
                sc = jnp.where(valid, sc - slope_ref[h] * relf, NEG)
                p = jnp.exp(sc - lse_h[:, 0:1])
                dp = lax.dot_general(doh, vw, (((1,), (1,)), ((), ())), preferred_element_type=F32)
                dsc = (p * (dp - dd_h[:, 0:1])).astype(BF16)
                pb = p.astype(BF16)
                dqs.append(jnp.dot(dsc, kw, preferred_element_type=F32))
                dk_h = jnp.dot(qh.T, dsc, preferred_element_type=F32)
                dv_h = jnp.dot(doh.T, pb, preferred_element_type=F32)
                dkw[kvh] = dk_h if dkw[kvh] is None else dkw[kvh] + dk_h
                dvw[kvh] = dv_h if dvw[kvh] is None else dvw[kvh] + dv_h
                psink = jnp.exp(sink_ref[h] - lse_h)
                ds_ref[h] += jnp.sum((-psink * dd_h).reshape(WINDOW // 8, 8, HEAD_DIM), axis=0)
            dq_ref[...] = jnp.concatenate(dqs, axis=1)
            win = jnp.concatenate(dkw + dvw, axis=0)
            first = win[:, 0:WINDOW]
            second = win[:, WINDOW:2 * WINDOW]
            dkv_ref[...] = (carry[...] + first).T
            carry[...] = jnp.where(n == 0, first, second)

        @pl.when(n == nb)
        def _():
            dkv_ref[...] = carry[...].T

    hm = lambda heads: pl.BlockSpec((heads, WINDOW, HEAD_DIM), lambda n: (0, jnp.minimum(n, last), 0))
    res = lambda blk: pl.BlockSpec((A_KV_HEADS, s, HEAD_DIM), lambda n: (blk, 0, 0))
    outs, comm_outs = _hosted_call(
        body, comm, name=name, grid=(nb + 1,),
        in_specs=[smem, smem, hm(A_Q_HEADS), res(A_GROUP), res(A_GROUP + 1), hm(A_Q_HEADS), hm(A_Q_HEADS), hm(A_Q_HEADS)],
        out_specs=[pl.BlockSpec((WINDOW, A_WIDTH), lambda n: (jnp.minimum(n, last), 0)),
                   pl.BlockSpec((WINDOW, 2 * A_KV_WIDTH), lambda n: (jnp.maximum(n - 1, 0), 0)),
                   pl.BlockSpec((A_Q_HEADS, 8, HEAD_DIM), lambda n: (0, 0, 0))],
        out_shape=[jax.ShapeDtypeStruct((s, A_WIDTH), F32), jax.ShapeDtypeStruct((s, 2 * A_KV_WIDTH), F32),
                   jax.ShapeDtypeStruct((A_Q_HEADS, 8, HEAD_DIM), F32)],
        scratch_shapes=[pltpu.VMEM((2 * A_KV_WIDTH, WINDOW), F32)],
        args=[sinks, slopes, qkv, qkv, qkv, do, lse, dd], sem=("arbitrary",), vmem=VMEM_BIG)
    return outs[0], outs[1], outs[2], comm_outs


def _attn_b_fwd(qkv, c3, name, comm=None):
    heads, s = qkv.shape[0] // 3, qkv.shape[1]
    hpairs = heads // 2
    bq = min(512, s)
    nq = s // bq
    nt = (((1,), (1,)), ((), ()))

    def body(q_ref, k_ref, v_ref, c_ref, o_ref, lse_ref, m_scr, l_scr, acc_scr):
        i = pl.program_id(1)
        r0 = pl.multiple_of(i * bq, bq)
        row = lax.broadcasted_iota(jnp.int32, (bq, bq), 0)
        col = lax.broadcasted_iota(jnp.int32, (bq, bq), 1)
        m_scr[...] = jnp.full((2, bq, LANES), NEG, F32)
        l_scr[...] = jnp.zeros((2, bq, LANES), F32)
        acc_scr[...] = jnp.zeros((2, bq, HEAD_DIM), F32)

        def step(j, masked):
            k0 = pl.multiple_of(j * bq, bq)
            for h2 in range(2):
                kv = k_ref[h2, pl.ds(k0, bq), :]
                vv = v_ref[h2, pl.ds(k0, bq), :]
                cq0 = c_ref[h2, :, pl.ds(r0, LANES)][:, 0:1]
                sc = lax.dot_general(q_ref[h2], kv, nt, preferred_element_type=F32)
                sc = sc + (cq0 - c_ref[h2, :, pl.ds(k0, bq)])
                if masked:
                    sc = jnp.where(col <= row, sc, NEG)
                m_prev = m_scr[h2]
                m_new = jnp.maximum(m_prev, jnp.max(sc, axis=1, keepdims=True))
                alpha = jnp.exp(m_prev - m_new)
                p = jnp.exp(sc - m_new[:, 0:1])
                l_scr[h2] = alpha * l_scr[h2] + jnp.sum(p, axis=1, keepdims=True)
                p_hi = p.astype(BF16)
                p_lo = (p - p_hi.astype(F32)).astype(BF16)
                pv = jnp.dot(p_hi, vv, preferred_element_type=F32) + jnp.dot(p_lo, vv, preferred_element_type=F32)
                acc_scr[h2] = acc_scr[h2] * alpha[:, 0:HEAD_DIM] + pv
                m_scr[h2] = m_new

        def loop_body(j, carry):
            step(j, False)
            return carry

        lax.fori_loop(0, i, loop_body, 0)
        step(i, True)
        outs = []
        for h2 in range(2):
            l = l_scr[h2]
            outs.append(acc_scr[h2] / l[:, 0:HEAD_DIM])
            lse_ref[h2] = (m_scr[h2] + jnp.log(l))[:, 0:HEAD_DIM]
        o_ref[...] = jnp.concatenate(outs, axis=1)

    res = lambda off: pl.BlockSpec((2, s, HEAD_DIM), lambda hp, i: (off + hp, 0, 0))
    outs, comm_outs = _hosted_call(
        body, comm, name=name, grid=(hpairs, nq),
        in_specs=[pl.BlockSpec((2, bq, HEAD_DIM), lambda hp, i: (hp, i, 0)), res(hpairs), res(2 * hpairs),
                  pl.BlockSpec((2, 1, s), lambda hp, i: (hp, 0, 0))],
        out_specs=[pl.BlockSpec((bq, 2 * HEAD_DIM), lambda hp, i: (i, hp)),
                   pl.BlockSpec((2, bq, HEAD_DIM), lambda hp, i: (hp, i, 0))],
        out_shape=[jax.ShapeDtypeStruct((s, heads * HEAD_DIM), F32), jax.ShapeDtypeStruct((heads, s, HEAD_DIM), F32)],
        scratch_shapes=[pltpu.VMEM((2, bq, LANES), F32), pltpu.VMEM((2, bq, LANES), F32), pltpu.VMEM((2, bq, HEAD_DIM), F32)],
        args=[qkv, qkv, qkv, c3], sem=("parallel", "parallel"), vmem=VMEM_BIG)
    return outs[0], outs[1], comm_outs


def _attn_b_bwd(qkv, do, lse, dd, c3, name, comm=None):
    heads, s = qkv.shape[0] // 3, qkv.shape[1]
    hpairs = heads // 2
    bq = min(512, s)
    nq = s // bq
    nt = (((1,), (1,)), ((), ()))
    tn = (((0,), (0,)), ((), ()))
    grid = (heads // 2, nq)

    def body(q_ref, k_ref, v_ref, do_ref, lse_ref, dd_ref, c_ref, dq_ref, dk_ref, dv_ref, dc_ref,
             dq_scr, dk_scr, dv_scr, dc_scr):
        j = pl.program_id(1)
        k0 = pl.multiple_of(j * bq, bq)
        row = lax.broadcasted_iota(jnp.int32, (bq, bq), 0)
        col = lax.broadcasted_iota(jnp.int32, (bq, bq), 1)

        @pl.when(j == 0)
        def _():
            dq_scr[...] = jnp.zeros(dq_scr.shape, F32)

        dk_scr[...] = jnp.zeros((2, HEAD_DIM, bq), F32)
        dv_scr[...] = jnp.zeros((2, HEAD_DIM, bq), F32)
        dc_scr[...] = jnp.zeros((2, 1, bq), F32)
        k_t = [k_ref[h2].T for h2 in range(2)]

        def step(i, masked):
            r0 = pl.multiple_of(i * bq, bq)
            for h2 in range(2):
                kv = k_ref[h2]
                vv = v_ref[h2]
                qv = q_ref[h2, pl.ds(r0, bq), :]
                dov = do_ref[h2, pl.ds(r0, bq), :]
                lse_v = lse_ref[h2, pl.ds(r0, bq), :][:, 0:1]
                dd_v = dd_ref[h2, pl.ds(r0, bq), :][:, 0:1]
                cq0 = c_ref[h2, :, pl.ds(r0, LANES)][:, 0:1]
                sc = lax.dot_general(qv, kv, nt, preferred_element_type=F32) + (cq0 - c_ref[h2, :, pl.ds(k0, bq)])
                if masked:
                    sc = jnp.where(col <= row, sc, NEG)
                p = jnp.exp(sc - lse_v)
                dp = lax.dot_general(dov, vv, nt, preferred_element_type=F32)
                dsc = p * (dp - dd_v)
                dsb = dsc.astype(BF16)
                dv_scr[h2] += jnp.dot(dov.T, p.astype(BF16), preferred_element_type=F32)
                dk_scr[h2] += jnp.dot(qv.T, dsb, preferred_element_type=F32)
                dq_scr[h2, :, pl.ds(r0, bq)] += jnp.dot(k_t[h2], dsb.T, preferred_element_type=F32)
                dc_scr[h2] -= jnp.sum(dsc, axis=0, keepdims=True)

        def loop_body(i, carry):
            step(i, False)
            return carry

        step(j, True)
        lax.fori_loop(j + 1, nq, loop_body, 0)
        dc_ref[...] = dc_scr[...]
        dk_ref[...] = jnp.concatenate([dk_scr[0].T, dk_scr[1].T], axis=1)
        dv_ref[...] = jnp.concatenate([dv_scr[0].T, dv_scr[1].T], axis=1)

        @pl.when(j == nq - 1)
        def _():
            dq_ref[...] = jnp.concatenate([dq_scr[0].T, dq_scr[1].T], axis=1)

    res = pl.BlockSpec((2, s, HEAD_DIM), lambda hp, j: (hp, 0, 0))
    blk = lambda off: pl.BlockSpec((2, bq, HEAD_DIM), lambda hp, j: (off + hp, j, 0))
    tm = jax.ShapeDtypeStruct((s, heads * HEAD_DIM), F32)
    in_specs = [res, blk(hpairs), blk(2 * hpairs), res, res, res, pl.BlockSpec((2, 1, s), lambda hp, j: (hp, 0, 0))]
    out_specs = [pl.BlockSpec((s, 2 * HEAD_DIM), lambda hp, j: (0, hp)),
                 pl.BlockSpec((bq, 2 * HEAD_DIM), lambda hp, j: (j, hp)),
                 pl.BlockSpec((bq, 2 * HEAD_DIM), lambda hp, j: (j, hp)),
                 pl.BlockSpec((2, 1, bq), lambda hp, j: (hp, 0, j))]
    out_shape = [tm, tm, tm, jax.ShapeDtypeStruct((heads, 1, s), F32)]
    scratch = [pltpu.VMEM((2, HEAD_DIM, s), F32), pltpu.VMEM((2, HEAD_DIM, bq), F32),
               pltpu.VMEM((2, HEAD_DIM, bq), F32), pltpu.VMEM((2, 1, bq), F32)]
    outs, comm_outs = _hosted_call(
        body, comm, name=name, grid=grid, in_specs=in_specs, out_specs=out_specs, out_shape=out_shape,
        scratch_shapes=scratch, args=[qkv, qkv, qkv, do, lse, dd, c3], sem=("parallel", "arbitrary"), vmem=VMEM_BIG)
    return outs[0], outs[1], outs[2], outs[3], comm_outs


def _attn_c_probs(qh, mkh):
    sc = lax.dot_general(qh, mkh, (((1,), (1,)), ((), ())), preferred_element_type=F32) * (C_HEAD_DIM ** -0.5)
    p = jnp.exp(sc - jnp.max(sc, axis=1, keepdims=True))
    return p / jnp.sum(p, axis=1, keepdims=True)


def _attn_c_fwd(q, mkv, name):
    s = q.shape[0]
    m = mkv.shape[0]
    bq = _tile(s, 512, 8)

    def body(q_ref, mk_ref, mv_ref, o_ref):
        outs = []
        for h in range(C_HEADS):
            sl = slice(h * C_HEAD_DIM, (h + 1) * C_HEAD_DIM)
            pn = _attn_c_probs(q_ref[:, sl], mk_ref[:, sl]).astype(BF16)
            outs.append(jnp.dot(pn, mv_ref[:, sl], preferred_element_type=F32))
        o_ref[...] = jnp.concatenate(outs, axis=1)

    return pl.pallas_call(
        body, name=name, grid=(s // bq,),
        in_specs=[pl.BlockSpec((bq, C_WIDTH), lambda i: (i, 0)), pl.BlockSpec((m, C_WIDTH), lambda i: (0, 0)),
                  pl.BlockSpec((m, C_WIDTH), lambda i: (0, 1))],
        out_specs=pl.BlockSpec((bq, C_WIDTH), lambda i: (i, 0)),
        out_shape=jax.ShapeDtypeStruct((s, C_WIDTH), F32),
        compiler_params=_params(("parallel",)),
    )(q, mkv, mkv)


def _attn_c_bwd(q, mkv, do, name):
    s = q.shape[0]
    m = mkv.shape[0]
    bq = _tile(s, 512, 8)
    tn = (((0,), (0,)), ((), ()))

    def body(q_ref, mk_ref, mv_ref, do_ref, dq_ref, dm_ref):
        i = pl.program_id(0)

        @pl.when(i == 0)
        def _():
            dm_ref[...] = jnp.zeros(dm_ref.shape, F32)

        dqs = []
        for h in range(C_HEADS):
            sl = slice(h * C_HEAD_DIM, (h + 1) * C_HEAD_DIM)
            qh, mkh, mvh, doh = q_ref[:, sl], mk_ref[:, sl], mv_ref[:, sl], do_ref[:, sl]
            pn = _attn_c_probs(qh, mkh)
            dp = lax.dot_general(doh, mvh, (((1,), (1,)), ((), ())), preferred_element_type=F32)
            dsc = (pn * (dp - jnp.sum(pn * dp, axis=1, keepdims=True)) * (C_HEAD_DIM ** -0.5)).astype(BF16)
            dqs.append(jnp.dot(dsc, mkh, preferred_element_type=F32))
            dm_ref[:, sl] += lax.dot_general(dsc, qh, tn, preferred_element_type=F32)
            sv = slice(C_WIDTH + h * C_HEAD_DIM, C_WIDTH + (h + 1) * C_HEAD_DIM)
            dm_ref[:, sv] += lax.dot_general(pn.astype(BF16), doh, tn, preferred_element_type=F32)
        dq_ref[...] = jnp.concatenate(dqs, axis=1)

    row = pl.BlockSpec((bq, C_WIDTH), lambda i: (i, 0))
    return pl.pallas_call(
        body, name=name, grid=(s // bq,),
        in_specs=[row, pl.BlockSpec((m, C_WIDTH), lambda i: (0, 0)), pl.BlockSpec((m, C_WIDTH), lambda i: (0, 1)), row],
        out_specs=[row, pl.BlockSpec((m, 2 * C_WIDTH), lambda i: (0, 0))],
        out_shape=[jax.ShapeDtypeStruct((s, C_WIDTH), F32), jax.ShapeDtypeStruct((m, 2 * C_WIDTH), F32)],
        compiler_params=_params(("arbitrary",)),
    )(q, mkv, mkv, do)


def _gate_fwd(y, proj, zc0, bw, name):
    rows, width = y.shape
    bm = _tile(rows, 2048 if bw <= 256 else 1024, 16)
    cb0 = zc0 // bw

    def body(y_ref, z_ref, o_ref):
        z = z_ref[...].astype(F32)
        o_ref[...] = (y_ref[...] * (z * _sigmoid(z))).astype(BF16)

    return pl.pallas_call(
        body, name=name, grid=(rows // bm, width // bw),
        in_specs=[pl.BlockSpec((bm, bw), lambda i, t: (i, t)), pl.BlockSpec((bm, bw), lambda i, t: (i, cb0 + t))],
        out_specs=pl.BlockSpec((bm, bw), lambda i, t: (i, t)),
        out_shape=jax.ShapeDtypeStruct((rows, width), BF16),
        compiler_params=_params(("parallel", "parallel")),
    )(y, proj)


def _gate_bwd(dsv, y, proj, zc0, bw, dproj, t0, head_major, name):
    rows, width = y.shape
    bm = _tile(rows, 2048 if bw <= 256 else 1024, 16)
    cb0 = zc0 // bw
    tb0 = t0 // bw
    bd = _block_diag(HEAD_DIM)
    hpb = bw // HEAD_DIM

    def body(*refs):
        if head_major:
            ds_ref, y_ref, z_ref, bd_ref, _, dp_ref, dy_ref, dd_ref = refs
        else:
            ds_ref, y_ref, z_ref, _, dp_ref, dy_ref = refs
        z = z_ref[...].astype(F32)
        sig = _sigmoid(z)
        dsx = ds_ref[...]
        yv = y_ref[...]
        dy = dsx * (z * sig)
        dp_ref[...] = (dsx * yv * (sig * (1.0 + z * (1.0 - sig)))).astype(BF16)
        if head_major:
            dyb = dy.astype(BF16)
            dd = _seg_sum(dyb.astype(F32) * yv, bd_ref[...])
            for h in range(hpb):
                sl = slice(h * HEAD_DIM, (h + 1) * HEAD_DIM)
                dy_ref[h] = dyb[:, sl]
                dd_ref[h] = dd[:, sl]
        else:
            dy_ref[...] = dy.astype(BF16)

    tile = pl.BlockSpec((bm, bw), lambda i, t: (i, t))
    ztile = pl.BlockSpec((bm, bw), lambda i, t: (i, cb0 + t))
    ttile = pl.BlockSpec((bm, bw), lambda i, t: (i, tb0 + t))
    any_spec = pl.BlockSpec(memory_space=pl.ANY)
    dp_shape = jax.ShapeDtypeStruct(dproj.shape, BF16)
    if head_major:
        hm_spec = pl.BlockSpec((hpb, bm, HEAD_DIM), lambda i, t: (t, i, 0))
        nh = width // HEAD_DIM
        outs = pl.pallas_call(
            body, name=name, grid=(rows // bm, width // bw),
            in_specs=[tile, tile, ztile, pl.BlockSpec((LANES, LANES), lambda i, t: (0, 0)), any_spec],
            out_specs=[ttile, hm_spec, hm_spec],
            out_shape=[dp_shape, jax.ShapeDtypeStruct((nh, rows, HEAD_DIM), BF16),
                       jax.ShapeDtypeStruct((nh, rows, HEAD_DIM), F32)],
            input_output_aliases={4: 0},
            compiler_params=_params(("parallel", "parallel")),
        )(dsv, y, proj, bd, dproj)
        return outs[0], outs[1], outs[2]
    outs = pl.pallas_call(
        body, name=name, grid=(rows // bm, width // bw),
        in_specs=[tile, tile, ztile, any_spec],
        out_specs=[ttile, tile],
        out_shape=[dp_shape, jax.ShapeDtypeStruct((rows, width), BF16)],
        input_output_aliases={3: 0},
        compiler_params=_params(("parallel", "parallel")),
    )(dsv, y, proj, dproj)
    return outs[0], outs[1], None


def _merge_fwd(proj, ua, ub, uc, name):
    rows, d = ua.shape
    bm = _tile(rows, 1024, 16)
    bw = _tile(d, 512)
    g0 = COL_GATE // bw
    gstep = d // bw

    def body(la_ref, lb_ref, lc_ref, ua_ref, ub_ref, uc_ref, o_ref, ga_ref, gb_ref, gc_ref):
        y = None
        for l_ref, u_ref, g_ref in ((la_ref, ua_ref, ga_ref), (lb_ref, ub_ref, gb_ref), (lc_ref, uc_ref, gc_ref)):
            g = _sigmoid(l_ref[...].astype(F32))
            g_ref[...] = g.astype(BF16)
            term = g * u_ref[...].astype(F32)
            y = term if y is None else y + term
        o_ref[...] = y.astype(BF16)

    tile = pl.BlockSpec((bm, bw), lambda i, t: (i, t))
    gate = lambda b: pl.BlockSpec((bm, bw), lambda i, t: (i, g0 + b * gstep + t))
    shape = jax.ShapeDtypeStruct((rows, d), BF16)
    return pl.pallas_call(
        body, name=name, grid=(rows // bm, d // bw),
        in_specs=[gate(0), gate(1), gate(2), tile, tile, tile],
        out_specs=[tile] * 4, out_shape=[shape] * 4,
        compiler_params=_params(("parallel", "parallel")),
    )(proj, proj, proj, ua, ub, uc)


def _merge_bwd(dym, us, gs, name):
    rows, d = dym.shape
    bm = _tile(rows, 256, 16)

    def body(dy_ref, ua_ref, ub_ref, uc_ref, ga_ref, gb_ref, gc_ref, dg_ref, da_ref, db_ref, dc_ref):
        dyv = dy_ref[...]
        for b, (u_ref, g_ref, du_ref) in enumerate(((ua_ref, ga_ref, da_ref), (ub_ref, gb_ref, db_ref), (uc_ref, gc_ref, dc_ref))):
            g = g_ref[...].astype(F32)
            du_ref[...] = (g * dyv).astype(BF16)
            dg_ref[:, b * d:(b + 1) * d] = (dyv * u_ref[...].astype(F32) * g * (1.0 - g)).astype(BF16)

    tile = pl.BlockSpec((bm, d), lambda i: (i, 0))
    shape = jax.ShapeDtypeStruct((rows, d), BF16)
    outs = pl.pallas_call(
        body, name=name, grid=(rows // bm,),
        in_specs=[tile] * 7,
        out_specs=[pl.BlockSpec((bm, 3 * d), lambda i: (i, 0)), tile, tile, tile],
        out_shape=[jax.ShapeDtypeStruct((rows, 3 * d), BF16), shape, shape, shape],
        compiler_params=_params(("parallel",), VMEM_BIG),
    )(dym, *us, *gs)
    return outs[0], outs[1], outs[2], outs[3]


def _out_proj_loss(ym, wo, x, target, name):
    m, d = x.shape
    bm, bn = _tile(m, 1024, 16), _tile(d, 1024)
    grid = (m // bm, d // bn)

    def body(a_ref, b_ref, x_ref, t_ref, dy_ref, dyb_ref, l_ref):
        first, _ = _grid_edges(grid)
        y = jnp.dot(a_ref[...], b_ref[...], preferred_element_type=F32) + x_ref[...]
        diff = y - t_ref[...]
        dy = diff * (1.0 / d)
        dy_ref[...] = dy
        dyb_ref[...] = dy.astype(BF16)
        sq = diff * diff
        part = sq[:, 0:LANES]
        for c in range(1, bn // LANES):
            part = part + sq[:, c * LANES:(c + 1) * LANES]
        part = jnp.sum(part.reshape(bm // 8, 8, LANES), axis=0)

        @pl.when(first)
        def _():
            l_ref[...] = part

        @pl.when(jnp.logical_not(first))
        def _():
            l_ref[...] += part

    tile = pl.BlockSpec((bm, bn), lambda i, j: (i, j))
    return pl.pallas_call(
        body, name=name, grid=grid,
        in_specs=[pl.BlockSpec((bm, d), lambda i, j: (i, 0)), pl.BlockSpec((d, bn), lambda i, j: (0, j)), tile, tile],
        out_specs=[tile, tile, pl.BlockSpec((8, LANES), lambda i, j: (0, 0))],
        out_shape=[jax.ShapeDtypeStruct((m, d), F32), jax.ShapeDtypeStruct((m, d), BF16),
                   jax.ShapeDtypeStruct((8, LANES), F32)],
        compiler_params=_params(("arbitrary", "arbitrary"), VMEM_BIG),
    )(ym, wo, x, target)


def _row(vec, reps=1):
    return jnp.tile(vec.reshape(1, -1).astype(F32), (1, reps))


def _local_step(x, mem, target, small, wg, shards=None):
    s, d = x.shape
    dist = shards is not None
    wg = dict(wg)
    ones = lambda n: jnp.ones((1, n), F32)
    zeros = lambda n: jnp.zeros((1, n), F32)
    scale_ab = HEAD_DIM ** -0.5
    split8 = lambda g: g.reshape(N_DEV, g.shape[0] // N_DEV, g.shape[1])
    flat8 = lambda g: g.reshape(g.shape[0] * g.shape[1], g.shape[2])
    gather = lambda names: _Comm("gather", [shards[n] for n in names]) if dist else None
    g = {}

    windows = {"wm_q1": ("wm_qkv", 0, Q_SPLIT), "wm_q2": ("wm_qkv", Q_SPLIT, W_QKV - Q_SPLIT)}

    def scatter(names):
        if not dist:
            return None
        whole = [windows.get(n, (n, None, None)) for n in names]
        return _Comm("scatter", [split8(g[src]) for src, _, _ in whole],
                     [None if col0 is None else (col0, width) for _, col0, width in whole])

    def hosted(result, names, store):
        if not dist:
            return result
        out, got = result
        store.update(zip(names, got))
        return out

    hn = _rmsnorm_fwd(x, small["norm_gain"], "rms_x_fwd")
    got = {}
    proj = hosted(_mm_nn(hn, wg["qkv"], bm=1024, bn=1024, bk=d, o_dtype=BF16, name="proj_qkv",
                         comm=gather(("wa", "wb"))), ("wa", "wb"), got)
    wg.update({n: flat8(a) for n, a in got.items()})
    pfb = _mm_nn(hn, wg["wf"], bm=1024, bn=FB_PAD, bk=d, o_dtype=F32, name="proj_fb")
    mn = _rmsnorm_fwd(mem, small["mem_norm_gain"], "rms_mem_fwd")
    mkv = _mm_nn(mn, wg["wk"], bm=256, bn=1024, bk=d, o_dtype=F32, name="mem_kv")

    gain_a = jnp.concatenate([_row(small["q_gain_a"], A_Q_HEADS) * scale_ab, _row(small["k_gain_a"], A_KV_HEADS), ones(A_KV_WIDTH)], axis=1)
    flag_a = jnp.concatenate([ones(A_WIDTH + A_KV_WIDTH), zeros(A_KV_WIDTH)], axis=1)
    qkv_a = _headnorm_fwd(proj, COL_QA, 1280, 1280, HEAD_DIM, gain_a, flag_a, True, "hn_a_fwd")
    gain_b = jnp.concatenate([_row(small["q_gain_b"], B_HEADS) * scale_ab, _row(small["k_gain_b"], B_HEADS), ones(B_WIDTH)], axis=1)
    flag_b = jnp.concatenate([ones(2 * B_WIDTH), zeros(B_WIDTH)], axis=1)
    qkv_b = _headnorm_fwd(proj, COL_QB, 2304, 256, HEAD_DIM, gain_b, flag_b, True, "hn_b_fwd")
    gain_cq = _row(small["q_gain_c"], C_HEADS)
    q_c = _headnorm_fwd(proj, COL_QC, C_WIDTH, C_WIDTH, C_HEAD_DIM, gain_cq, ones(C_WIDTH), False, "hn_cq_fwd")
    gain_ck = jnp.concatenate([_row(small["k_gain_c"], C_HEADS), ones(C_WIDTH)], axis=1)
    flag_ck = jnp.concatenate([ones(C_WIDTH), zeros(C_WIDTH)], axis=1)
    mkvn = _headnorm_fwd(mkv, 0, 2 * C_WIDTH, 2 * C_WIDTH, C_HEAD_DIM, gain_ck, flag_ck, False, "hn_ck_fwd")


    bpad = jnp.pad(small["b_forget"].reshape(1, -1), ((0, 0), (0, FB_PAD - B_HEADS)))
    c16 = _fox_prep(pfb, bpad, "fox_prep")
    c3 = c16[0:B_HEADS].reshape(B_HEADS, 1, s)

    sinks = small["sinks_a"].reshape(-1)
    slopes = jnp.exp2(-8.0 * jnp.arange(1, A_Q_HEADS + 1, dtype=F32) / A_Q_HEADS)
    y_a, lse_a = _attn_a_fwd(qkv_a, sinks, slopes, "attn_a_fwd")
    y_b, lse_b, got_zg = _attn_b_fwd(qkv_b, c3, "attn_b_fwd", comm=gather(("zg",)))
    if dist:
        wg["zg"] = flat8(got_zg[0])
    y_c = _attn_c_fwd(q_c, mkvn, "attn_c_fwd")

    got = {}
    pzg = hosted(_mm_nn(hn, wg["zg"], bm=1024, bn=1024, bk=d, o_dtype=BF16, name="proj_zg", comm=gather(("wo", "wc"))),
                 ("wo", "wc"), got)
    wg.update({n: flat8(a) for n, a in got.items()})

    s_a = _gate_fwd(y_a, pzg, COL_ZA, 256, "gate_a_fwd")
    s_b = _gate_fwd(y_b, pzg, COL_ZB, 256, "gate_b_fwd")
    s_c = _gate_fwd(y_c, pzg, COL_ZC, 512, "gate_c_fwd")
    w_a, w_b, w_c = _branch_full(wg["wa"]), _branch_full(wg["wb"]), _branch_full(wg["wc"])
    u_a = _mm_nn(s_a, w_a, bm=1024, bn=2048, bk=A_WIDTH, o_dtype=BF16, name="branch_a_fwd")
    u_b = _mm_nn(s_b, w_b, bm=1024, bn=2048, bk=B_WIDTH, o_dtype=BF16, name="branch_b_fwd")
    u_c = _mm_nn(s_c, w_c, bm=1024, bn=2048, bk=C_WIDTH, o_dtype=BF16, name="branch_c_fwd")
    ym, gate_a, gate_b, gate_c = _merge_fwd(pzg, u_a, u_b, u_c, "merge_fwd")
    dy, dyb, lpart = _out_proj_loss(ym, wg["wo"], x, target, "out_proj_loss")
    loss = 0.5 / d * jnp.sum(lpart)

    dym = _mm_nt(dyb, wg["wo"], bm=1024, bn=1024, bk=d, o_dtype=F32, name="out_proj_bwd_act")
    g["wo"] = _mm_tn(ym, dyb, bm=512, bn=1024, bk=s, o_dtype=BF16, name="out_proj_bwd_w")

    dgate, du_a, du_b, du_c = _merge_bwd(dym, (u_a, u_b, u_c), (gate_a, gate_b, gate_c), "merge_bwd")
    parts = {}
    g["wm_g"] = hosted(_mm_tn(hn, dgate, bm=512, bn=1024, bk=s, o_dtype=BF16, name="proj_gate_bwd_w",
                              comm=scatter(("wo",))), ("wo",), parts)

    ds_a = _mm_nt(du_a, w_a, bm=1024, bn=A_WIDTH, bk=d, o_dtype=F32, name="branch_a_bwd_act")
    ds_b = _mm_nt(du_b, w_b, bm=1024, bn=B_WIDTH, bk=d, o_dtype=F32, name="branch_b_bwd_act")
    ds_c = _mm_nt(du_c, w_c, bm=1024, bn=C_WIDTH, bk=d, o_dtype=F32, name="branch_c_bwd_act")
    g["wa"] = _branch_bwd_w(s_a, du_a, "branch_a_bwd_w")
    g["wb"] = _branch_bwd_w(s_b, du_b, "branch_b_bwd_w")
    g["wc"] = _branch_bwd_w(s_c, du_c, "branch_c_bwd_w")

    dz = lax.empty((s, W_Z), BF16)
    dz, do_a, dd_a = _gate_bwd(ds_a, y_a, pzg, COL_ZA, 256, dz, COL_ZA, True, "gate_a_bwd")
    dz, do_b, dd_b = _gate_bwd(ds_b, y_b, pzg, COL_ZB, 256, dz, COL_ZB, True, "gate_b_bwd")
    dz, do_c, _ = _gate_bwd(ds_c, y_c, pzg, COL_ZC, 512, dz, COL_ZC, False, "gate_c_bwd")
    g["wm_z"] = _mm_tn(hn, dz, bm=512, bn=1024, bk=s, o_dtype=BF16, name="proj_z_bwd_w")

    names = ("wa", "wb", "wc")
    dq_a, dkv_a, dsink, got = _attn_a_bwd(qkv_a, do_a, lse_a, dd_a, sinks, slopes, "attn_a_bwd", comm=scatter(names))
    parts.update(zip(names, got))
    names = ("wm_g", "wm_z")
    dq_b, dk_b, dv_b, dc3, got = _attn_b_bwd(qkv_b, do_b, lse_b, dd_b, c3, "attn_b_bwd", comm=scatter(names))
    parts.update(zip(names, got))
    dq_c, dmkvn = _attn_c_bwd(q_c, mkvn, do_c, "attn_c_bwd")

    dqkv = lax.empty((s, W_QKV), BF16)
    dqkv, dg_qa = _headnorm_bwd(proj, COL_QA, A_WIDTH, 256, HEAD_DIM, gain_a[:, 0:768], flag_a[:, 0:768], dq_a, dqkv, COL_QA, "hn_qa_bwd")
    dqkv, dg_kva = _headnorm_bwd(proj, COL_KA, 512, 256, HEAD_DIM, gain_a[:, 768:1280], flag_a[:, 768:1280], dkv_a, dqkv, COL_KA, "hn_kva_bwd")
    dqkv, dg_qb = _headnorm_bwd(proj, COL_QB, B_WIDTH, 256, HEAD_DIM, gain_b[:, 0:768], flag_b[:, 0:768], dq_b, dqkv, COL_QB, "hn_qb_bwd")
    dqkv, dg_kb = _headnorm_bwd(proj, COL_KB, B_WIDTH, 256, HEAD_DIM, gain_b[:, 768:1536], flag_b[:, 768:1536], dk_b, dqkv, COL_KB, "hn_kb_bwd")
    dqkv, _ = _headnorm_bwd(proj, COL_VB, B_WIDTH, 256, HEAD_DIM, gain_b[:, 1536:2304], flag_b[:, 1536:2304], dv_b, dqkv, COL_VB, "hn_vb_bwd")
    dqkv, dg_qc = _headnorm_bwd(proj, COL_QC, C_WIDTH, 512, C_HEAD_DIM, gain_cq, ones(C_WIDTH), dq_c, dqkv, COL_QC, "hn_qc_bwd")
    dmkv, dg_kc = _headnorm_bwd(mkv, 0, 2 * C_WIDTH, 2 * C_WIDTH, C_HEAD_DIM, gain_ck, flag_ck, dmkvn, None, 0, "hn_kc_bwd")

    dct = jnp.pad(dc3.reshape(B_HEADS, s), ((0, 16 - B_HEADS), (0, 0)))
    dfb, dbf = _fox_prep_bwd(pfb, bpad, dct, "fox_prep_bwd")

    dmn = _mm_nt(dmkv, wg["wk"], bm=256, bn=1024, bk=1024, o_dtype=F32, name="mem_kv_bwd_act")
    g["wk"] = _mm_tn(mn, dmkv, bm=512, bn=1024, bk=mem.shape[0], o_dtype=BF16, name="mem_kv_bwd_w")
    _, dg_mem = _rmsnorm_bwd(mem, dmn, small["mem_norm_gain"], None, "rms_mem_bwd")

    g["wm_qkv"] = _mm_tn(hn, dqkv, bm=512, bn=1024, bk=s, o_dtype=BF16, name="proj_qkv_bwd_w")
    g["wf"] = _mm_tn(hn, dfb, bm=512, bn=FB_PAD, bk=s, o_dtype=BF16, name="proj_fb_bwd_w")
    names = ("wm_q1",)
    dhn = hosted(_mm_nt_sum([(dqkv, wg["qkv"], 0), (dfb, wg["wf"], 0)], bm=1024, bn=1024, bk=2048,
                            name="proj_qkv_bwd_act", comm=scatter(names)), names, parts)
    names = ("wm_q2", "wf", "wk")
    dhn = hosted(_mm_nt_sum([(dz, wg["zg"], COL_ZA), (dgate, wg["zg"], COL_GATE)], bm=1024, bn=1024, bk=2048,
                            name="proj_zg_bwd_act", add=dhn, comm=scatter(names)), names, parts)
    if dist:
        g = parts
    grad_x, dg_x = _rmsnorm_bwd(x, dhn, small["norm_gain"], dy, "rms_x_bwd")

    fold = lambda part, heads, hd: jnp.sum(jnp.sum(part, axis=0).reshape(heads, hd), axis=0).reshape(1, hd)
    small_grads = {
        "norm_gain": jnp.sum(dg_x, axis=0).reshape(1, d),
        "mem_norm_gain": jnp.sum(dg_mem, axis=0).reshape(1, d),
        "b_forget": dbf[0:B_HEADS, 0].reshape(1, B_HEADS),
        "q_gain_a": fold(dg_qa, A_Q_HEADS, HEAD_DIM) * scale_ab,
        "k_gain_a": fold(dg_kva[:, 0:A_KV_WIDTH], A_KV_HEADS, HEAD_DIM),
        "sinks_a": (jnp.sum(dsink, axis=(1, 2)) * (1.0 / HEAD_DIM)).reshape(1, A_Q_HEADS),
        "q_gain_b": fold(dg_qb, B_HEADS, HEAD_DIM) * scale_ab,
        "k_gain_b": fold(dg_kb, B_HEADS, HEAD_DIM),
        "q_gain_c": fold(dg_qc, C_HEADS, C_HEAD_DIM),
        "k_gain_c": fold(dg_kc[:, 0:C_WIDTH], C_HEADS, C_HEAD_DIM),
    }
    return loss, grad_x, small_grads, g


def _coords():
    return lax.axis_index("x"), lax.axis_index("y"), lax.axis_index("c")


def _all_gather(shards, name):
    n = len(shards)

    def body(*refs):
        ins = refs[0:n]
        outs = refs[n:2 * n]
        send_sems, recv_sems, local_sems = refs[2 * n:2 * n + 3]
        x, y, c = _coords()
        me, sibling = (x, y, c), (x, y, 1 - c)
        chips = [(1 - x, y), (x, 1 - y), (1 - x, 1 - y)]
        idx = lambda p: 4 * p[0] + 2 * p[1] + p[2]

        def copy(a, k, block, to, src=None):
            slot = outs[a].at[idx(block)]
            return pltpu.make_async_remote_copy(
                src_ref=slot if src is None else src, dst_ref=slot,
                send_sem=send_sems.at[a, k], recv_sem=recv_sems.at[a, k], device_id=to, device_id_type=MESH)

        mine = [pltpu.make_async_copy(ins[a], outs[a].at[idx(me)], local_sems.at[a]) for a in range(n)]
        for cp in mine:
            cp.start()
        first = []
        for a in range(n):
            first.append(copy(a, 0, me, sibling, src=ins[a]))
            first += [copy(a, 1 + j, me, (*chip, c), src=ins[a]) for j, chip in enumerate(chips)]
        for cp in first:
            cp.start()
        passed = []
        for j, chip in enumerate(chips):
            for a in range(n):
                copy(a, 1 + j, (*chip, c), me).wait_recv()
                fwd = copy(a, 4 + j, (*chip, c), sibling)
                fwd.start()
                passed.append(fwd)
        for a in range(n):
            copy(a, 0, sibling, me).wait_recv()
            for j, chip in enumerate(chips):
                copy(a, 4 + j, (*chip, 1 - c), me).wait_recv()
        for cp in first + passed:
            cp.wait_send()
        for cp in mine:
            cp.wait()

    any_spec = pl.BlockSpec(memory_space=pl.ANY)
    return pl.pallas_call(
        body, name=name,
        in_specs=[any_spec] * n, out_specs=[any_spec] * n,
        out_shape=[jax.ShapeDtypeStruct((N_DEV,) + sh.shape, sh.dtype) for sh in shards],
        scratch_shapes=[pltpu.SemaphoreType.DMA((n, 7)), pltpu.SemaphoreType.DMA((n, 7)), pltpu.SemaphoreType.DMA((n,))],
    )(*shards)


def _sum_parts(parts, name):
    _, rows, cols = parts.shape
    br = _tile(rows, 64, 16)

    def body(p_ref, o_ref):
        total = p_ref[0].astype(F32)
        for j in range(1, N_DEV):
            total = total + p_ref[j].astype(F32)
        o_ref[...] = total

    return pl.pallas_call(
        body, name=name, grid=(rows // br,),
        in_specs=[pl.BlockSpec((N_DEV, br, cols), lambda i: (0, i, 0))],
        out_specs=pl.BlockSpec((br, cols), lambda i: (i, 0)),
        out_shape=jax.ShapeDtypeStruct((rows, cols), F32),
        compiler_params=_params(("parallel",), VMEM_BIG),
    )(parts)


def _adamw(w, g, m, v, name, br=32):
    rows, cols = w.shape
    br = min(br, rows)
    c1 = 1.0 / (1.0 - ADAM_B1 ** ADAM_STEP)
    c2 = 1.0 / (1.0 - ADAM_B2 ** ADAM_STEP)

    def body(w_ref, g_ref, m_ref, v_ref, d_ref, nm_ref, nv_ref):
        gv = g_ref[...]
        nm = ADAM_B1 * m_ref[...] + (1.0 - ADAM_B1) * gv
        nv = ADAM_B2 * v_ref[...] + (1.0 - ADAM_B2) * (gv * gv)
        d_ref[...] = -ADAM_LR * ((nm * c1) / (jnp.sqrt(nv * c2) + ADAM_EPS) + ADAM_WD * w_ref[...])
        nm_ref[...] = nm
        nv_ref[...] = nv

    spec = pl.BlockSpec((br, cols), lambda i: (i, 0))
    shape = jax.ShapeDtypeStruct((rows, cols), F32)
    return pl.pallas_call(
        body, name=name, grid=(pl.cdiv(rows, br),), in_specs=[spec] * 4, out_specs=[spec] * 3, out_shape=[shape] * 3,
        compiler_params=_params(("parallel",), VMEM_BIG),
    )(w, g, m, v)


def _adamw_t(wt, g, mt, vt, name, br=1024, comm=None):
    n, r = wt.shape
    c1 = 1.0 / (1.0 - ADAM_B1 ** ADAM_STEP)
    c2 = 1.0 / (1.0 - ADAM_B2 ** ADAM_STEP)

    def body(w_ref, g_ref, m_ref, v_ref, d_ref, nm_ref, nv_ref):
        gv = g_ref[...].T
        nm = ADAM_B1 * m_ref[...] + (1.0 - ADAM_B1) * gv
        nv = ADAM_B2 * v_ref[...] + (1.0 - ADAM_B2) * (gv * gv)
        d_ref[...] = -ADAM_LR * ((nm * c1) / (jnp.sqrt(nv * c2) + ADAM_EPS) + ADAM_WD * w_ref[...])
        nm_ref[...] = nm
        nv_ref[...] = nv

    spec = pl.BlockSpec((br, r), lambda i: (i, 0))
    shape = jax.ShapeDtypeStruct((n, r), F32)
    return _hosted_call(
        body, comm, name=name, grid=(pl.cdiv(n, br),),
        in_specs=[spec, pl.BlockSpec((r, br), lambda i: (0, i)), spec, spec], out_specs=[spec] * 3, out_shape=[shape] * 3,
        scratch_shapes=[], args=[wt, g, mt, vt], sem=("parallel",), vmem=VMEM_BIG)


def _adamw_parts(w, parts, m, v, name):
    rows, cols = w.shape
    br = _tile(rows, 32, 16)
    c1 = 1.0 / (1.0 - ADAM_B1 ** ADAM_STEP)
    c2 = 1.0 / (1.0 - ADAM_B2 ** ADAM_STEP)

    def body(w_ref, p_ref, m_ref, v_ref, g_ref, d_ref, nm_ref, nv_ref):
        gv = p_ref[0].astype(F32)
        for j in range(1, N_DEV):
            gv = gv + p_ref[j].astype(F32)
        nm = ADAM_B1 * m_ref[...] + (1.0 - ADAM_B1) * gv
        nv = ADAM_B2 * v_ref[...] + (1.0 - ADAM_B2) * (gv * gv)
        g_ref[...] = gv
        d_ref[...] = -ADAM_LR * ((nm * c1) / (jnp.sqrt(nv * c2) + ADAM_EPS) + ADAM_WD * w_ref[...])
        nm_ref[...] = nm
        nv_ref[...] = nv

    spec = pl.BlockSpec((br, cols), lambda i: (i, 0))
    shape = jax.ShapeDtypeStruct((rows, cols), F32)
    return pl.pallas_call(
        body, name=name, grid=(rows // br,),
        in_specs=[spec, pl.BlockSpec((N_DEV, br, cols), lambda i: (0, i, 0)), spec, spec],
        out_specs=[spec] * 4, out_shape=[shape] * 4,
        compiler_params=_params(("parallel",), VMEM_BIG),
    )(w, parts, m, v)


SMALL_NAMES = ("norm_gain", "mem_norm_gain", "b_forget", "q_gain_a", "k_gain_a", "sinks_a",
               "q_gain_b", "k_gain_b", "q_gain_c", "k_gain_c")
BIG_NAMES = ("w_in", "w_mem_kv", "w_branch_a", "w_branch_b", "w_branch_c", "w_out")
WEIGHT_ORDER = ("norm_gain", "mem_norm_gain", "w_in", "b_forget", "q_gain_a", "k_gain_a", "sinks_a", "q_gain_b",
                "k_gain_b", "q_gain_c", "k_gain_c", "w_mem_kv", "w_branch_a", "w_branch_b", "w_branch_c", "w_out")


def _pack_small(tree):
    flat = jnp.concatenate([tree[n].reshape(1, -1) for n in SMALL_NAMES], axis=1)
    pad = (-flat.shape[1]) % LANES
    return jnp.pad(flat, ((0, 0), (0, pad)))


def _unpack_small(flat, like):
    out, off = {}, 0
    for n in SMALL_NAMES:
        size = like[n].size
        out[n] = flat[:, off:off + size].reshape(like[n].shape)
        off += size
    return out


def kernel(x, mem, norm_gain, mem_norm_gain, w_in, b_forget, q_gain_a, k_gain_a, sinks_a, q_gain_b, k_gain_b, q_gain_c, k_gain_c, w_mem_kv, w_branch_a, w_branch_b, w_branch_c, w_out, loss_target, m_norm_gain, m_mem_norm_gain, m_w_in, m_b_forget, m_q_gain_a, m_k_gain_a, m_sinks_a, m_q_gain_b, m_k_gain_b, m_q_gain_c, m_k_gain_c, m_w_mem_kv, m_w_branch_a, m_w_branch_b, m_w_branch_c, m_w_out, v_norm_gain, v_mem_norm_gain, v_w_in, v_b_forget, v_q_gain_a, v_k_gain_a, v_sinks_a, v_q_gain_b, v_k_gain_b, v_q_gain_c, v_k_gain_c, v_w_mem_kv, v_w_branch_a, v_w_branch_b, v_w_branch_c, v_w_out):
    weights = dict(norm_gain=norm_gain, mem_norm_gain=mem_norm_gain, w_in=w_in, b_forget=b_forget, q_gain_a=q_gain_a,
                   k_gain_a=k_gain_a, sinks_a=sinks_a, q_gain_b=q_gain_b, k_gain_b=k_gain_b, q_gain_c=q_gain_c,
                   k_gain_c=k_gain_c, w_mem_kv=w_mem_kv, w_branch_a=w_branch_a, w_branch_b=w_branch_b,
                   w_branch_c=w_branch_c, w_out=w_out)
    mom_m = dict(norm_gain=m_norm_gain, mem_norm_gain=m_mem_norm_gain, w_in=m_w_in, b_forget=m_b_forget,
                 q_gain_a=m_q_gain_a, k_gain_a=m_k_gain_a, sinks_a=m_sinks_a, q_gain_b=m_q_gain_b, k_gain_b=m_k_gain_b,
                 q_gain_c=m_q_gain_c, k_gain_c=m_k_gain_c, w_mem_kv=m_w_mem_kv, w_branch_a=m_w_branch_a,
                 w_branch_b=m_w_branch_b, w_branch_c=m_w_branch_c, w_out=m_w_out)
    mom_v = dict(norm_gain=v_norm_gain, mem_norm_gain=v_mem_norm_gain, w_in=v_w_in, b_forget=v_b_forget,
                 q_gain_a=v_q_gain_a, k_gain_a=v_k_gain_a, sinks_a=v_sinks_a, q_gain_b=v_q_gain_b, k_gain_b=v_k_gain_b,
                 q_gain_c=v_q_gain_c, k_gain_c=v_k_gain_c, w_mem_kv=v_w_mem_kv, w_branch_a=v_w_branch_a,
                 w_branch_b=v_w_branch_b, w_branch_c=v_w_branch_c, w_out=v_w_out)
    wi = w_in[0]
    sh_qkv = jnp.concatenate([wi[:, a:b] for a, b in SRC_RANGES[0:3]], axis=1).astype(BF16)
    sh_zg = jnp.concatenate([wi[:, a:b] for a, b in SRC_RANGES[3:6]] + [wi[:, SRC_GATE:]], axis=1).astype(BF16)
    sh_wf = jnp.pad(wi[:, FB_SRC:FB_SRC + B_HEADS], ((0, 0), (0, FB_PAD - B_HEADS))).astype(BF16)
    shards = {"zg": sh_zg, "wo": w_out[0].astype(BF16), "wa": w_branch_a[0].astype(BF16),
              "wb": w_branch_b[0].astype(BF16), "wc": w_branch_c[0].astype(BF16)}
    first = ("qkv", "wf", "wk")
    full = _all_gather([sh_qkv, sh_wf, w_mem_kv[0].astype(BF16)], "weights_all_gather")
    wg = {kname: arr.reshape(arr.shape[0] * arr.shape[1], arr.shape[2]) for kname, arr in zip(first, full)}

    small = {n: weights[n] for n in SMALL_NAMES}
    loss_local, grad_x, small_g, parts = _local_step(x[0], mem[0], loss_target[0], small, wg, shards)

    grads, delta, new_m, new_v = {}, {}, {}, {}
    g1, g2, gz, gf, gg = (_sum_parts(parts[k], "grad_sum_" + k) for k in ("wm_q1", "wm_q2", "wm_z", "wf", "wm_g"))
    half = Q_SPLIT
    g_in = jnp.concatenate([g1, g2[:, 0:COL_QB - half], gz[:, COL_ZA:COL_ZB], g2[:, COL_QB - half:COL_QC - half],
                            gz[:, COL_ZB:COL_ZC], gf[:, 0:B_HEADS], g2[:, COL_QC - half:W_QKV - half], gz[:, COL_ZC:W_Z], gg], axis=1)
    packed = _pack_small(small_g)
    packed = jnp.concatenate([packed[:, :-1], loss_local.reshape(1, 1)], axis=1)
    all_small = _Comm("gather", [jnp.broadcast_to(packed, (8, packed.shape[1]))])
    (dlt, nm, nv), (packed8,) = _adamw_t(w_in[0].T, g_in, m_w_in[0].T, v_w_in[0].T, "adamw_w_in", comm=all_small)
    others = ("wk", "wo", "wa", "wb", "wc")
    (dlt, nm, nv), held = lax.optimization_barrier(((dlt, nm, nv), [parts[k] for k in others]))
    parts.update(zip(others, held))
    grads["w_in"], delta["w_in"], new_m["w_in"], new_v["w_in"] = g_in, dlt.T[None], nm.T[None], nv.T[None]
    reduced = _sum_parts(packed8, "small_sum")[0:1]
    grads.update(_unpack_small(reduced, small))
    loss = reduced[0, -1]
    for n, kname in (("w_mem_kv", "wk"), ("w_out", "wo"), ("w_branch_a", "wa"), ("w_branch_b", "wb"), ("w_branch_c", "wc")):
        gsum, dlt, nm, nv = _adamw_parts(weights[n][0], parts[kname], mom_m[n][0], mom_v[n][0], "adamw_" + n)
        grads[n], delta[n], new_m[n], new_v[n] = gsum, dlt[None], nm[None], nv[None]

    pw, pm, pv = _pack_small(small), _pack_small({n: mom_m[n] for n in SMALL_NAMES}), _pack_small({n: mom_v[n] for n in SMALL_NAMES})
    rep8 = lambda a: jnp.broadcast_to(a, (8, a.shape[1]))
    dlt, nm, nv = _adamw(rep8(pw), rep8(reduced), rep8(pm), rep8(pv), "adamw_small")
    for tree, flat in ((delta, dlt), (new_m, nm), (new_v, nv)):
        tree.update(_unpack_small(flat[0:1], small))
    for n in BIG_NAMES:
        grads[n] = grads[n][None]
    return (loss, grad_x[None], *[grads[n] for n in WEIGHT_ORDER], *[delta[n] for n in WEIGHT_ORDER],
            *[new_m[n] for n in WEIGHT_ORDER], *[new_v[n] for n in WEIGHT_ORDER])
```

```python
import math

import jax
import jax.numpy as jnp
import numpy as np
from jax import lax
from jax.experimental import pallas as pl
from jax.experimental.pallas import tpu as pltpu

F32 = jnp.float32
BF16 = jnp.bfloat16

N_DEV = 8
HEAD_DIM = 64
A_Q_HEADS = 12
A_KV_HEADS = 4
A_GROUP = 3
B_HEADS = 12
C_HEADS = 4
C_HEAD_DIM = 128
WINDOW = 128
A_WIDTH = 768
A_KV_WIDTH = 256
B_WIDTH = 768
C_WIDTH = 512
EPS = 1e-6
NEG = -1e30

COL_QA, COL_KA, COL_VA = 0, 768, 1024
COL_QB, COL_KB, COL_VB = 1280, 2048, 2816
COL_QC = 3584
W_QKV = 4096
Q_SPLIT = 1280
COL_ZA, COL_ZB, COL_ZC = 0, 768, 1536
COL_GATE = W_Z = 2048
SRC_RANGES = ((0, 1280), (2048, 4352), (5132, 5644), (1280, 2048), (4352, 5120), (5644, 6156))
SRC_GATE = 6156
FB_SRC = 5120
FB_PAD = 128

ADAM_LR = 0.001
ADAM_B1 = 0.9
ADAM_B2 = 0.999
ADAM_EPS = 1e-08
ADAM_WD = 0.01
ADAM_STEP = 10

VMEM_BIG = 52 * 1024 * 1024
LANES = 128
MESH = pl.DeviceIdType.MESH


def _tile(n, pref, mult=128):
    if n <= pref:
        return n
    t = (pref // mult) * mult
    while t >= mult:
        if n % t == 0:
            return t
        t -= mult
    return n


def _params(sem=None, vmem=None):
    kw = {}
    if sem is not None:
        kw["dimension_semantics"] = sem
    if vmem is not None:
        kw["vmem_limit_bytes"] = vmem
    return pltpu.CompilerParams(**kw)


def _sigmoid(x):
    return 1.0 / (1.0 + jnp.exp(-x))


def _block_diag(hd):
    r = np.arange(LANES)
    return jnp.asarray((r[:, None] // hd) == (r[None, :] // hd), dtype=BF16)


def _seg_sum(t, bd):
    hi = t.astype(BF16)
    lo = (t - hi.astype(F32)).astype(BF16)
    outs = []
    for c in range(t.shape[1] // LANES):
        sl = slice(c * LANES, (c + 1) * LANES)
        outs.append(jnp.dot(hi[:, sl], bd, preferred_element_type=F32) + jnp.dot(lo[:, sl], bd, preferred_element_type=F32))
    return outs[0] if len(outs) == 1 else jnp.concatenate(outs, axis=1)


def _rmsnorm_fwd(x, gain, name):
    rows, d = x.shape
    bm = _tile(rows, 512, 8)

    def body(x_ref, g_ref, o_ref):
        xv = x_ref[...]
        ms = jnp.mean(xv * xv, axis=-1, keepdims=True)
        o_ref[...] = (xv * lax.rsqrt(ms + EPS) * g_ref[...]).astype(BF16)

    return pl.pallas_call(
        body, name=name, grid=(rows // bm,),
        in_specs=[pl.BlockSpec((bm, d), lambda i: (i, 0)), pl.BlockSpec((1, d), lambda i: (0, 0))],
        out_specs=pl.BlockSpec((bm, d), lambda i: (i, 0)),
        out_shape=jax.ShapeDtypeStruct((rows, d), BF16),
        compiler_params=_params(("parallel",)),
    )(x, gain)


def _rmsnorm_bwd(x, dhn, gain, dy, name):
    rows, d = x.shape
    bm = _tile(rows, 512, 8)
    with_dx = dy is not None

    def body(*refs):
        if with_dx:
            x_ref, dh_ref, g_ref, dy_ref, gx_ref, dg_ref = refs
        else:
            x_ref, dh_ref, g_ref, dg_ref = refs
        i = pl.program_id(0)
        xv = x_ref[...]
        rstd = lax.rsqrt(jnp.mean(xv * xv, axis=-1, keepdims=True) + EPS)
        xhat = xv * rstd
        dh = dh_ref[...]
        part = jnp.sum((dh * xhat).reshape(bm // 8, 8, d), axis=0)

        @pl.when(i == 0)
        def _():
            dg_ref[...] = part

        @pl.when(i > 0)
        def _():
            dg_ref[...] += part

        if with_dx:
            g = dh * g_ref[...]
            mean = jnp.mean(g * xhat, axis=-1, keepdims=True)
            gx_ref[...] = dy_ref[...] + rstd * (g - xhat * mean)

    row_spec = pl.BlockSpec((bm, d), lambda i: (i, 0))
    in_specs = [row_spec, row_spec, pl.BlockSpec((1, d), lambda i: (0, 0))]
    args = [x, dhn, gain]
    dg_spec = pl.BlockSpec((8, d), lambda i: (0, 0))
    dg_shape = jax.ShapeDtypeStruct((8, d), F32)
    if with_dx:
        in_specs.append(row_spec)
        args.append(dy)
        out_specs = [row_spec, dg_spec]
        out_shape = [jax.ShapeDtypeStruct((rows, d), F32), dg_shape]
    else:
        out_specs = [dg_spec]
        out_shape = [dg_shape]
    outs = pl.pallas_call(
        body, name=name, grid=(rows // bm,), in_specs=in_specs, out_specs=out_specs, out_shape=out_shape,
        compiler_params=_params(("arbitrary",), VMEM_BIG),
    )(*args)
    return outs if with_dx else (None, outs[0])


class _Comm:
    def __init__(self, kind, arrays, windows=None):
        self.kind = kind
        self.arrays = list(arrays)
        self.n = len(self.arrays)
        self.windows = list(windows) if windows is not None else [None] * self.n
        assert kind == "scatter" or all(w is None for w in self.windows)

    def out_shapes(self):
        if self.kind == "gather":
            return [jax.ShapeDtypeStruct((N_DEV,) + a.shape, a.dtype) for a in self.arrays]
        return [jax.ShapeDtypeStruct(a.shape if w is None else a.shape[0:2] + (w[1],), a.dtype)
                for a, w in zip(self.arrays, self.windows)]

    def _slice_for(self, ins, a, dev):
        if self.windows[a] is None:
            return ins[a].at[dev]
        col0, width = self.windows[a]
        return ins[a].at[dev, :, pl.ds(col0, width)]

    def scratch(self):
        return [pltpu.SemaphoreType.DMA((self.n, N_DEV - 1)), pltpu.SemaphoreType.DMA((self.n, N_DEV - 1)),
                pltpu.SemaphoreType.DMA((self.n,))]

    def _plan(self, ins, outs, sems, with_recvs):
        send_sems, recv_sems, local_sems = sems
        x, y, c = lax.axis_index("x"), lax.axis_index("y"), lax.axis_index("c")
        my = 4 * x + 2 * y + c
        gather = self.kind == "gather"
        local, sends, recvs = [], [], []
        for a in range(self.n):
            local.append(pltpu.make_async_copy(ins[a] if gather else self._slice_for(ins, a, my), outs[a].at[my], local_sems.at[a]))
            for k in range(1, N_DEV):
                peer = (x ^ ((k >> 2) & 1), y ^ ((k >> 1) & 1), c ^ (k & 1))
                pid = 4 * peer[0] + 2 * peer[1] + peer[2]
                src = ins[a] if gather else self._slice_for(ins, a, pid)
                sem = dict(send_sem=send_sems.at[a, k - 1], recv_sem=recv_sems.at[a, k - 1], device_id=peer, device_id_type=MESH)
                sends.append(pltpu.make_async_remote_copy(src_ref=src, dst_ref=outs[a].at[my], **sem))
                if with_recvs:
                    recvs.append(pltpu.make_async_remote_copy(src_ref=src, dst_ref=outs[a].at[pid], **sem))
        return local, sends, recvs

    def start(self, ins, outs, sems):
        local, sends, _ = self._plan(ins, outs, sems, False)
        for cp in local + sends:
            cp.start()

    def wait(self, ins, outs, sems):
        local, sends, recvs = self._plan(ins, outs, sems, True)
        for cp in recvs:
            cp.wait_recv()
        for cp in sends:
            cp.wait_send()
        for cp in local:
            cp.wait()


def _grid_edges(grid):
    first = last = None
    for ax, size in enumerate(grid):
        pid = pl.program_id(ax)
        f, l = pid == 0, pid == size - 1
        first = f if first is None else first & f
        last = l if last is None else last & l
    return first, last


def _hosted_call(body, comm, *, name, grid, in_specs, out_specs, out_shape, scratch_shapes, args, sem, vmem=None):
    in_specs, out_specs, out_shape, scratch_shapes = list(in_specs), list(out_specs), list(out_shape), list(scratch_shapes)
    if comm is None:
        res = pl.pallas_call(body, name=name, grid=grid, in_specs=in_specs, out_specs=out_specs, out_shape=out_shape,
                             scratch_shapes=scratch_shapes, compiler_params=_params(sem, vmem))(*args)
        return list(res), []
    n_in, n_out, n_scr, nc = len(in_specs), len(out_shape), len(scratch_shapes), comm.n

    def hosted(*refs):
        ins = refs[0:n_in]
        comm_in = refs[n_in:n_in + nc]
        outs = refs[n_in + nc:n_in + nc + n_out]
        comm_out = refs[n_in + nc + n_out:n_in + 2 * nc + n_out]
        scr = refs[n_in + 2 * nc + n_out:n_in + 2 * nc + n_out + n_scr]
        sems = refs[n_in + 2 * nc + n_out + n_scr:]
        first, last = _grid_edges(grid)

        @pl.when(first)
        def _():
            comm.start(comm_in, comm_out, sems)

        body(*ins, *outs, *scr)

        @pl.when(last)
        def _():
            comm.wait(comm_in, comm_out, sems)

    any_spec = pl.BlockSpec(memory_space=pl.ANY)
    res = pl.pallas_call(
        hosted, name=name, grid=grid, in_specs=in_specs + [any_spec] * nc, out_specs=out_specs + [any_spec] * nc,
        out_shape=out_shape + comm.out_shapes(), scratch_shapes=scratch_shapes + comm.scratch(),
        compiler_params=_params(("arbitrary",) * len(grid), vmem),
    )(*args, *comm.arrays)
    return list(res[0:n_out]), list(res[n_out:])


def _mm(a, b, *, grid, a_spec, b_spec, o_spec, o_shape, o_dtype, contract, name, add=None, add_spec=None, acc_shape=None,
        comm=None):
    nk = grid[2]
    has_add = add is not None

    def body(*refs):
        a_ref, b_ref = refs[0], refs[1]
        add_ref = refs[2] if has_add else None
        o_ref = refs[3] if has_add else refs[2]
        part = lax.dot_general(a_ref[...], b_ref[...], (contract, ((), ())), preferred_element_type=F32)
        if nk == 1:
            if has_add:
                part = part + add_ref[...]
            o_ref[...] = part.astype(o_dtype)
        else:
            acc = refs[-1]
            k = pl.program_id(2)

            @pl.when(k == 0)
            def _():
                acc[...] = part

            @pl.when(k > 0)
            def _():
                acc[...] += part

            @pl.when(k == nk - 1)
            def _():
                r = acc[...]
                if has_add:
                    r = r + add_ref[...]
                o_ref[...] = r.astype(o_dtype)

    in_specs = [a_spec, b_spec] + ([add_spec] if has_add else [])
    args = [a, b] + ([add] if has_add else [])
    scratch = [pltpu.VMEM(acc_shape, F32)] if nk > 1 else []
    outs, comm_outs = _hosted_call(
        body, comm, name=name, grid=grid, in_specs=in_specs, out_specs=[o_spec],
        out_shape=[jax.ShapeDtypeStruct(o_shape, o_dtype)], scratch_shapes=scratch, args=args,
        sem=("parallel", "parallel", "arbitrary"), vmem=VMEM_BIG)
    return outs[0] if comm is None else (outs[0], comm_outs)


def _mm_nn(a, b, *, bm, bn, bk, o_dtype, name, add=None, comm=None):
    m, kd = a.shape
    n = b.shape[1]
    bm, bn, bk = _tile(m, bm, 8), _tile(n, bn), _tile(kd, bk)
    o_spec = pl.BlockSpec((bm, bn), lambda i, j, k: (i, j))
    return _mm(a, b, grid=(m // bm, n // bn, kd // bk),
               a_spec=pl.BlockSpec((bm, bk), lambda i, j, k: (i, k)),
               b_spec=pl.BlockSpec((bk, bn), lambda i, j, k: (k, j)),
               o_spec=o_spec, o_shape=(m, n), o_dtype=o_dtype, contract=((1,), (0,)), name=name,
               add=add, add_spec=o_spec, acc_shape=(bm, bn), comm=comm)


def _mm_nt(a, b, *, bm, bn, bk, o_dtype, name, add=None, b_col0=0, comm=None):
    m, kd = a.shape
    n = b.shape[0]
    bm, bn, bk = _tile(m, bm, 8), _tile(n, bn), _tile(math.gcd(kd, b_col0), bk)
    kb0 = b_col0 // bk
    o_spec = pl.BlockSpec((bm, bn), lambda i, j, k: (i, j))
    return _mm(a, b, grid=(m // bm, n // bn, kd // bk),
               a_spec=pl.BlockSpec((bm, bk), lambda i, j, k: (i, k)),
               b_spec=pl.BlockSpec((bn, bk), lambda i, j, k: (j, kb0 + k)),
               o_spec=o_spec, o_shape=(m, n), o_dtype=o_dtype, contract=((1,), (1,)), name=name,
               add=add, add_spec=o_spec, acc_shape=(bm, bn), comm=comm)


def _mm_nt_sum(terms, *, bm, bn, bk, name, add=None, comm=None):
    m = terms[0][0].shape[0]
    n = terms[0][1].shape[0]
    bm, bn = _tile(m, bm, 8), _tile(n, bn)
    nt = (((1,), (1,)), ((), ()))
    plan, groups, start = [], [], 0
    for a, b, col0 in terms:
        kd = a.shape[1]
        tk = _tile(math.gcd(kd, col0), bk)
        steps = kd // tk
        if plan and kd < bk:
            groups.append([b, start - 1, 1, col0 // tk, tk])
            plan.append((start - 1, 1, len(groups) - 1, True))
            continue
        last = groups[-1] if groups else None
        if last is not None and last[0] is b and last[4] == tk and (last[3] + last[2]) * tk == col0:
            last[2] += steps
        else:
            groups.append([b, start, steps, col0 // tk, tk])
        plan.append((start, steps, len(groups) - 1, False))
        start += steps
    nk = start
    nterm, ngroup, has_add = len(terms), len(groups), add is not None

    def body(*refs):
        a_refs, b_refs = refs[0:nterm], refs[nterm:nterm + ngroup]
        add_ref = refs[nterm + ngroup] if has_add else None
        o_ref, acc = refs[nterm + ngroup + has_add], refs[nterm + ngroup + has_add + 1]
        k = pl.program_id(2)
        for t, (s0, steps, grp, rides) in enumerate(plan):
            @pl.when((k >= s0) & (k < s0 + steps))
            def _():
                part = lax.dot_general(a_refs[t][...], b_refs[grp][...], nt, preferred_element_type=F32)
                if rides:
                    acc[...] += part
                    return

                @pl.when(k == 0)
                def _():
                    acc[...] = part

                @pl.when(k > 0)
                def _():
                    acc[...] += part

        @pl.when(k == nk - 1)
        def _():
            o_ref[...] = acc[...] + add_ref[...] if has_add else acc[...]

    def a_spec(tk, s0, steps):
        return pl.BlockSpec((bm, tk), lambda i, j, k: (i, jnp.clip(k - s0, 0, steps - 1)))

    def b_spec(tk, s0, steps, off):
        return pl.BlockSpec((bn, tk), lambda i, j, k: (j, off + jnp.clip(k - s0, 0, steps - 1)))

    o_spec = pl.BlockSpec((bm, bn), lambda i, j, k: (i, j))
    in_specs = [a_spec(groups[grp][4], s0, steps) for s0, steps, grp, _ in plan]
    in_specs += [b_spec(tk, s0, steps, cb0) for _, s0, steps, cb0, tk in groups]
    args = [a for a, _, _ in terms] + [grp[0] for grp in groups]
    if has_add:
        in_specs.append(o_spec)
        args.append(add)
    outs, comm_outs = _hosted_call(
        body, comm, name=name, grid=(m // bm, n // bn, nk), in_specs=in_specs,
        out_specs=[o_spec], out_shape=[jax.ShapeDtypeStruct((m, n), F32)],
        scratch_shapes=[pltpu.VMEM((bm, bn), F32)], args=args,
        sem=("parallel", "parallel", "arbitrary"), vmem=VMEM_BIG)
    return outs[0] if comm is None else (outs[0], comm_outs)


def _mm_tn(a, b, *, bm, bn, bk, o_dtype, name, comm=None):
    kd, m = a.shape
    n = b.shape[1]
    bm, bn, bk = _tile(m, bm), _tile(n, bn), _tile(kd, bk, 8)
    return _mm(a, b, grid=(m // bm, n // bn, kd // bk),
               a_spec=pl.BlockSpec((bk, bm), lambda i, j, k: (k, i)),
               b_spec=pl.BlockSpec((bk, bn), lambda i, j, k: (k, j)),
               o_spec=pl.BlockSpec((bm, bn), lambda i, j, k: (i, j)),
               o_shape=(m, n), o_dtype=o_dtype, contract=((0,), (0,)), name=name, acc_shape=(bm, bn), comm=comm)


def _branch_full(w8):
    kb, ds = w8.shape[0] // N_DEV, w8.shape[1]
    return w8.reshape(N_DEV, kb, ds).transpose(1, 0, 2).reshape(kb, N_DEV * ds)


def _branch_bwd_w(s, du, name):
    m, kb = s.shape
    ds = du.shape[1] // N_DEV
    return _mm(s, du, grid=(1, N_DEV, 1),
               a_spec=pl.BlockSpec((m, kb), lambda i, j, k: (0, 0)),
               b_spec=pl.BlockSpec((m, ds), lambda i, j, k: (0, j)),
               o_spec=pl.BlockSpec((kb, ds), lambda i, j, k: (j, 0)),
               o_shape=(N_DEV * kb, ds), o_dtype=BF16, contract=((0,), (0,)), name=name)


def _headnorm_fwd(src, c0, width, bw, hd, gain, nflag, head_major, name):
    rows = src.shape[0]
    bm = _tile(rows, 2048 if bw <= 256 else 1024, 16)
    bd = _block_diag(hd)
    cb0 = c0 // bw

    def body(x_ref, g_ref, f_ref, bd_ref, o_ref):
        xv = x_ref[...].astype(F32)
        ss = _seg_sum(xv * xv, bd_ref[...])
        rstd = lax.rsqrt(ss * (1.0 / hd) + EPS)
        y = (xv * jnp.where(f_ref[...] > 0.0, rstd, 1.0) * g_ref[...]).astype(BF16)
        if head_major:
            for h in range(bw // HEAD_DIM):
                o_ref[h] = y[:, h * HEAD_DIM:(h + 1) * HEAD_DIM]
        else:
            o_ref[...] = y

    vec_spec = pl.BlockSpec((1, bw), lambda i, t: (0, t))
    if head_major:
        hpb = bw // HEAD_DIM
        out_spec = pl.BlockSpec((hpb, bm, HEAD_DIM), lambda i, t: (t, i, 0))
        out_shape = jax.ShapeDtypeStruct((width // HEAD_DIM, rows, HEAD_DIM), BF16)
    else:
        out_spec = pl.BlockSpec((bm, bw), lambda i, t: (i, t))
        out_shape = jax.ShapeDtypeStruct((rows, width), BF16)
    return pl.pallas_call(
        body, name=name, grid=(rows // bm, width // bw),
        in_specs=[pl.BlockSpec((bm, bw), lambda i, t: (i, cb0 + t)), vec_spec, vec_spec,
                  pl.BlockSpec((LANES, LANES), lambda i, t: (0, 0))],
        out_specs=out_spec, out_shape=out_shape,
        compiler_params=_params(("parallel", "parallel")),
    )(src, gain, nflag, bd)


def _headnorm_bwd(src, c0, width, bw, hd, gain, nflag, dyn, target, t0, name):
    rows = src.shape[0]
    bm = _tile(rows, 2048 if bw <= 256 else 1024, 16)
    bd = _block_diag(hd)
    cb0 = c0 // bw
    tb0 = t0 // bw
    aliased = target is not None

    def body(*refs):
        if aliased:
            x_ref, dy_ref, g_ref, f_ref, bd_ref, _, o_ref, dg_ref = refs
        else:
            x_ref, dy_ref, g_ref, f_ref, bd_ref, o_ref, dg_ref = refs
        i = pl.program_id(1)
        xv = x_ref[...].astype(F32)
        dyv = dy_ref[...]
        bdv = bd_ref[...]
        rstd = lax.rsqrt(_seg_sum(xv * xv, bdv) * (1.0 / hd) + EPS)
        xhat = xv * rstd
        g = dyv * g_ref[...]
        mean = _seg_sum(g * xhat, bdv) * (1.0 / hd)
        dx = jnp.where(f_ref[...] > 0.0, rstd * (g - xhat * mean), g)
        o_ref[...] = dx.astype(BF16)
        part = jnp.sum((dyv * xhat).reshape(bm // 8, 8, bw), axis=0)

        @pl.when(i == 0)
        def _():
            dg_ref[...] = part

        @pl.when(i > 0)
        def _():
            dg_ref[...] += part

    vec_spec = pl.BlockSpec((1, bw), lambda t, i: (0, t))
    in_specs = [pl.BlockSpec((bm, bw), lambda t, i: (i, cb0 + t)), pl.BlockSpec((bm, bw), lambda t, i: (i, t)),
                vec_spec, vec_spec, pl.BlockSpec((LANES, LANES), lambda t, i: (0, 0))]
    args = [src, dyn, gain, nflag, bd]
    aliases = {}
    if aliased:
        in_specs.append(pl.BlockSpec(memory_space=pl.ANY))
        args.append(target)
        aliases = {5: 0}
        o_shape = jax.ShapeDtypeStruct(target.shape, BF16)
    else:
        o_shape = jax.ShapeDtypeStruct((rows, width), BF16)
    out, dg = pl.pallas_call(
        body, name=name, grid=(width // bw, rows // bm), in_specs=in_specs,
        out_specs=[pl.BlockSpec((bm, bw), lambda t, i: (i, tb0 + t)), pl.BlockSpec((8, bw), lambda t, i: (0, t))],
        out_shape=[o_shape, jax.ShapeDtypeStruct((8, width), F32)],
        input_output_aliases=aliases,
        compiler_params=_params(("parallel", "arbitrary")),
    )(*args)
    return out, dg


def _fox_prep(pfb, bpad, name):
    s = pfb.shape[0]

    def body(p_ref, b_ref, c_ref):
        z = p_ref[...] + b_ref[...]
        logf = jnp.minimum(z, 0.0) - jnp.log(1.0 + jnp.exp(-jnp.abs(z)))
        x = logf.T[0:16, :]
        lane = lax.broadcasted_iota(jnp.int32, (16, s), 1)
        sh = 1
        while sh < s:
            x = x + jnp.where(lane >= sh, pltpu.roll(x, sh, 1), 0.0)
            sh *= 2
        c_ref[...] = x

    return pl.pallas_call(
        body, name=name, grid=(1,),
        in_specs=[pl.BlockSpec((s, FB_PAD), lambda i: (0, 0)), pl.BlockSpec((1, FB_PAD), lambda i: (0, 0))],
        out_specs=pl.BlockSpec((16, s), lambda i: (0, 0)),
        out_shape=jax.ShapeDtypeStruct((16, s), F32),
        compiler_params=_params(("arbitrary",)),
    )(pfb, bpad)


def _fox_prep_bwd(pfb, bpad, dct, name):
    s = pfb.shape[0]

    def body(p_ref, b_ref, dc_ref, df_ref, db_ref):
        zt = (p_ref[...] + b_ref[...]).T[0:16, :]
        y = dc_ref[...]
        lane = lax.broadcasted_iota(jnp.int32, (16, s), 1)
        sh = 1
        while sh < s:
            y = y + jnp.where(lane < s - sh, pltpu.roll(y, s - sh, 1), 0.0)
            sh *= 2
        dz = y * _sigmoid(-zt)
        db_ref[...] = jnp.broadcast_to(jnp.sum(dz, axis=1, keepdims=True), (16, FB_PAD))
        full = jnp.concatenate([dz, jnp.zeros((FB_PAD - 16, s), F32)], axis=0)
        df_ref[...] = full.T.astype(BF16)

    return pl.pallas_call(
        body, name=name, grid=(1,),
        in_specs=[pl.BlockSpec((s, FB_PAD), lambda i: (0, 0)), pl.BlockSpec((1, FB_PAD), lambda i: (0, 0)),
                  pl.BlockSpec((16, s), lambda i: (0, 0))],
        out_specs=[pl.BlockSpec((s, FB_PAD), lambda i: (0, 0)), pl.BlockSpec((16, FB_PAD), lambda i: (0, 0))],
        out_shape=[jax.ShapeDtypeStruct((s, FB_PAD), BF16), jax.ShapeDtypeStruct((16, FB_PAD), F32)],
        compiler_params=_params(("arbitrary",)),
    )(pfb, bpad, dct)


def _swa_window(n):
    ws = pl.multiple_of(jnp.maximum(n * WINDOW - WINDOW, 0), WINDOW)
    qi = lax.broadcasted_iota(jnp.int32, (WINDOW, 2 * WINDOW), 0)
    kj = lax.broadcasted_iota(jnp.int32, (WINDOW, 2 * WINDOW), 1)
    rel = qi + (n * WINDOW - ws) - kj
    valid = (rel >= 0) & (rel < WINDOW)
    return ws, valid, rel.astype(F32)


def _attn_a_fwd(qkv, sinks, slopes, name):
    s = qkv.shape[1]
    nb = s // WINDOW
    smem = pl.BlockSpec(memory_space=pltpu.SMEM)

    def body(sink_ref, slope_ref, q_ref, k_ref, v_ref, o_ref, lse_ref):
        n = pl.program_id(0)
        ws, valid, relf = _swa_window(n)
        outs = []
        for h in range(A_Q_HEADS):
            kvh = h // A_GROUP
            kw = k_ref[kvh, pl.ds(ws, 2 * WINDOW), :]
            vw = v_ref[kvh, pl.ds(ws, 2 * WINDOW), :]
            sc = lax.dot_general(q_ref[h], kw, (((1,), (1,)), ((), ())), preferred_element_type=F32)
            sc = jnp.where(valid, sc - slope_ref[h] * relf, NEG)
            sink = sink_ref[h]
            m = jnp.maximum(jnp.max(sc, axis=1, keepdims=True), sink)
            p = jnp.exp(sc - m)
            denom = jnp.sum(p, axis=1, keepdims=True) + jnp.exp(sink - m)
            pn = (p / denom).astype(BF16)
            outs.append(jnp.dot(pn, vw, preferred_element_type=F32))
            lse_ref[h] = jnp.broadcast_to(m + jnp.log(denom), (WINDOW, HEAD_DIM))
        o_ref[...] = jnp.concatenate(outs, axis=1)

    return pl.pallas_call(
        body, name=name, grid=(nb,),
        in_specs=[smem, smem,
                  pl.BlockSpec((A_Q_HEADS, WINDOW, HEAD_DIM), lambda n: (0, n, 0)),
                  pl.BlockSpec((A_KV_HEADS, s, HEAD_DIM), lambda n: (A_GROUP, 0, 0)),
                  pl.BlockSpec((A_KV_HEADS, s, HEAD_DIM), lambda n: (A_GROUP + 1, 0, 0))],
        out_specs=[pl.BlockSpec((WINDOW, A_WIDTH), lambda n: (n, 0)),
                   pl.BlockSpec((A_Q_HEADS, WINDOW, HEAD_DIM), lambda n: (0, n, 0))],
        out_shape=[jax.ShapeDtypeStruct((s, A_WIDTH), F32), jax.ShapeDtypeStruct((A_Q_HEADS, s, HEAD_DIM), F32)],
        compiler_params=_params(("parallel",), VMEM_BIG),
    )(sinks, slopes, qkv, qkv, qkv)


def _attn_a_bwd(qkv, do, lse, dd, sinks, slopes, name, comm=None):
    s = qkv.shape[1]
    nb = s // WINDOW
    smem = pl.BlockSpec(memory_space=pltpu.SMEM)
    last = nb - 1

    def body(sink_ref, slope_ref, q_ref, k_ref, v_ref, do_ref, lse_ref, dd_ref, dq_ref, dkv_ref, ds_ref, carry):
        n = pl.program_id(0)

        @pl.when(n == 0)
        def _():
            carry[...] = jnp.zeros(carry.shape, F32)
            ds_ref[...] = jnp.zeros(ds_ref.shape, F32)

        @pl.when(n < nb)
        def _():
            ws, valid, relf = _swa_window(n)
            dqs = []
            dkw = [None] * A_KV_HEADS
            dvw = [None] * A_KV_HEADS
            for h in range(A_Q_HEADS):
                kvh = h // A_GROUP
                qh = q_ref[h]
                doh = do_ref[h]
                kw = k_ref[kvh, pl.ds(ws, 2 * WINDOW), :]
                vw = v_ref[kvh, pl.ds(ws, 2 * WINDOW), :]
                lse_h = lse_ref[h]
                dd_h = dd_ref[h]
                sc = lax.dot_general(qh, kw, (((1,), (1,)), ((), ())), preferred_element_type=F32)
                sc = jnp.where(valid, sc - slope_ref[h] * relf, NEG)
                p = jnp.exp(sc - lse_h[:, 0:1])
                dp = lax.dot_general(doh, vw, (((1,), (1,)), ((), ())), preferred_element_type=F32)
                dsc = (p * (dp - dd_h[:, 0:1])).astype(BF16)
                pb = p.astype(BF16)
                dqs.append(jnp.dot(dsc, kw, preferred_element_type=F32))
                dk_h = jnp.dot(qh.T, dsc, preferred_element_type=F32)
                dv_h = jnp.dot(doh.T, pb, preferred_element_type=F32)
                dkw[kvh] = dk_h if dkw[kvh] is None else dkw[kvh] + dk_h
                dvw[kvh] = dv_h if dvw[kvh] is None else dvw[kvh] + dv_h
                psink = jnp.exp(sink_ref[h] - lse_h)
                ds_ref[h] += jnp.sum((-psink * dd_h).reshape(WINDOW // 8, 8, HEAD_DIM), axis=0)
            dq_ref[...] = jnp.concatenate(dqs, axis=1)
            win = jnp.concatenate(dkw + dvw, axis=0)
            first = win[:, 0:WINDOW]
            second = win[:, WINDOW:2 * WINDOW]
            dkv_ref[...] = (carry[...] + first).T
            carry[...] = jnp.where(n == 0, first, second)

        @pl.when(n == nb)
        def _():
            dkv_ref[...] = carry[...].T

    hm = lambda heads: pl.BlockSpec((heads, WINDOW, HEAD_DIM), lambda n: (0, jnp.minimum(n, last), 0))
    res = lambda blk: pl.BlockSpec((A_KV_HEADS, s, HEAD_DIM), lambda n: (blk, 0, 0))
    outs, comm_outs = _hosted_call(
        body, comm, name=name, grid=(nb + 1,),
        in_specs=[smem, smem, hm(A_Q_HEADS), res(A_GROUP), res(A_GROUP + 1), hm(A_Q_HEADS), hm(A_Q_HEADS), hm(A_Q_HEADS)],
        out_specs=[pl.BlockSpec((WINDOW, A_WIDTH), lambda n: (jnp.minimum(n, last), 0)),
                   pl.BlockSpec((WINDOW, 2 * A_KV_WIDTH), lambda n: (jnp.maximum(n - 1, 0), 0)),
                   pl.BlockSpec((A_Q_HEADS, 8, HEAD_DIM), lambda n: (0, 0, 0))],
        out_shape=[jax.ShapeDtypeStruct((s, A_WIDTH), F32), jax.ShapeDtypeStruct((s, 2 * A_KV_WIDTH), F32),
                   jax.ShapeDtypeStruct((A_Q_HEADS, 8, HEAD_DIM), F32)],
        scratch_shapes=[pltpu.VMEM((2 * A_KV_WIDTH, WINDOW), F32)],
        args=[sinks, slopes, qkv, qkv, qkv, do, lse, dd], sem=("arbitrary",), vmem=VMEM_BIG)
    return outs[0], outs[1], outs[2], comm_outs


def _attn_b_fwd(qkv, c3, name, comm=None):
    heads, s = qkv.shape[0] // 3, qkv.shape[1]
    hpairs = heads // 2
    bq = min(512, s)
    nq = s // bq
    nt = (((1,), (1,)), ((), ()))

    def body(q_ref, k_ref, v_ref, c_ref, o_ref, lse_ref, m_scr, acc_scr):
        i = pl.program_id(1)
        r0 = pl.multiple_of(i * bq, bq)
        row = lax.broadcasted_iota(jnp.int32, (bq, bq), 0)
        col = lax.broadcasted_iota(jnp.int32, (bq, bq), 1)
        m_scr[...] = jnp.full((2, bq, LANES), NEG, F32)
        acc_scr[...] = jnp.zeros((2, bq, 2 * HEAD_DIM), F32)
        ones = jnp.ones((bq, HEAD_DIM), BF16)

        def step(j, masked):
            k0 = pl.multiple_of(j * bq, bq)
            for h2 in range(2):
                kv = k_ref[h2, pl.ds(k0, bq), :]
                vv = jnp.concatenate([v_ref[h2, pl.ds(k0, bq), :], ones], axis=1)
                cq0 = c_ref[h2, :, pl.ds(r0, LANES)][:, 0:1]
                sc = lax.dot_general(q_ref[h2], kv, nt, preferred_element_type=F32)
                sc = sc + (cq0 - c_ref[h2, :, pl.ds(k0, bq)])
                if masked:
                    sc = jnp.where(col <= row, sc, NEG)
                m_prev = m_scr[h2]
                m_new = jnp.maximum(m_prev, jnp.max(sc, axis=1, keepdims=True))
                alpha = jnp.exp(m_prev - m_new)
                p = jnp.exp(sc - m_new[:, 0:1])
                p_hi = p.astype(BF16)
                p_lo = (p - p_hi.astype(F32)).astype(BF16)
                pv = jnp.dot(p_hi, vv, preferred_element_type=F32) + jnp.dot(p_lo, vv, preferred_element_type=F32)
                acc_scr[h2] = acc_scr[h2] * alpha + pv
                m_scr[h2] = m_new

        def loop_body(j, carry):
            step(j, False)
            return carry

        lax.fori_loop(0, i, loop_body, 0)
        step(i, True)
        outs = []
        for h2 in range(2):
            acc = acc_scr[h2]
            l = acc[:, HEAD_DIM:]
            outs.append(acc[:, 0:HEAD_DIM] / l)
            lse_ref[h2] = m_scr[h2][:, 0:HEAD_DIM] + jnp.log(l)
        o_ref[...] = jnp.concatenate(outs, axis=1)

    res = lambda off: pl.BlockSpec((2, s, HEAD_DIM), lambda hp, i: (off + hp, 0, 0))
    outs, comm_outs = _hosted_call(
        body, comm, name=name, grid=(hpairs, nq),
        in_specs=[pl.BlockSpec((2, bq, HEAD_DIM), lambda hp, i: (hp, i, 0)), res(hpairs), res(2 * hpairs),
                  pl.BlockSpec((2, 1, s), lambda hp, i: (hp, 0, 0))],
        out_specs=[pl.BlockSpec((bq, 2 * HEAD_DIM), lambda hp, i: (i, hp)),
                   pl.BlockSpec((2, bq, HEAD_DIM), lambda hp, i: (hp, i, 0))],
        out_shape=[jax.ShapeDtypeStruct((s, heads * HEAD_DIM), F32), jax.ShapeDtypeStruct((heads, s, HEAD_DIM), F32)],
        scratch_shapes=[pltpu.VMEM((2, bq, LANES), F32), pltpu.VMEM((2, bq, 2 * HEAD_DIM), F32)],
        args=[qkv, qkv, qkv, c3], sem=("parallel", "parallel"), vmem=VMEM_BIG)
    return outs[0], outs[1], comm_outs


def _attn_b_bwd(qkv, do, lse, dd, c3, name, comm=None):
    heads, s = qkv.shape[0] // 3, qkv.shape[1]
    hpairs = heads // 2
    bq = min(512, s)
    nq = s // bq
    nt = (((1,), (1,)), ((), ()))
    tn = (((0,), (0,)), ((), ()))
    grid = (heads // 2, nq)

    def body(q_ref, k_ref, v_ref, do_ref, lse_ref, dd_ref, c_ref, dq_ref, dk_ref, dv_ref, dc_ref,
             dq_scr, dk_scr, dv_scr, dc_scr):
        j = pl.program_id(1)
        k0 = pl.multiple_of(j * bq, bq)
        row = lax.broadcasted_iota(jnp.int32, (bq, bq), 0)
        col = lax.broadcasted_iota(jnp.int32, (bq, bq), 1)

        @pl.when(j == 0)
        def _():
            dq_scr[...] = jnp.zeros(dq_scr.shape, F32)

        dk_scr[...] = jnp.zeros((2, HEAD_DIM, bq), F32)
        dv_scr[...] = jnp.zeros((2, HEAD_DIM, bq), F32)
        dc_scr[...] = jnp.zeros((2, 1, bq), F32)
        k_t = [k_ref[h2].T for h2 in range(2)]

        def step(i, masked):
            r0 = pl.multiple_of(i * bq, bq)
            for h2 in range(2):
                kv = k_ref[h2]
                vv = v_ref[h2]
                qv = q_ref[h2, pl.ds(r0, bq), :]
                dov = do_ref[h2, pl.ds(r0, bq), :]
                lse_v = lse_ref[h2, pl.ds(r0, bq), :][:, 0:1]
                dd_v = dd_ref[h2, pl.ds(r0, bq), :][:, 0:1]
                cq0 = c_ref[h2, :, pl.ds(r0, LANES)][:, 0:1]
                sc = lax.dot_general(qv, kv, nt, preferred_element_type=F32) + (cq0 - c_ref[h2, :, pl.ds(k0, bq)])
                if masked:
                    sc = jnp.where(col <= row, sc, NEG)
                p = jnp.exp(sc - lse_v)
                dp = lax.dot_general(dov, vv, nt, preferred_element_type=F32)
                dsc = p * (dp - dd_v)
                dsb = dsc.astype(BF16)
                dv_scr[h2] += jnp.dot(dov.T, p.astype(BF16), preferred_element_type=F32)
                dk_scr[h2] += jnp.dot(qv.T, dsb, preferred_element_type=F32)
                dq_scr[h2, :, pl.ds(r0, bq)] += jnp.dot(k_t[h2], dsb.T, preferred_element_type=F32)
                dc_scr[h2] -= jnp.sum(dsc, axis=0, keepdims=True)

        def loop_body(i, carry):
            step(i, False)
            return carry

        step(j, True)
        lax.fori_loop(j + 1, nq, loop_body, 0)
        dc_ref[...] = dc_scr[...]
        dk_ref[...] = jnp.concatenate([dk_scr[0].T, dk_scr[1].T], axis=1)
        dv_ref[...] = jnp.concatenate([dv_scr[0].T, dv_scr[1].T], axis=1)

        @pl.when(j == nq - 1)
        def _():
            dq_ref[...] = jnp.concatenate([dq_scr[0].T, dq_scr[1].T], axis=1)

    res = pl.BlockSpec((2, s, HEAD_DIM), lambda hp, j: (hp, 0, 0))
    blk = lambda off: pl.BlockSpec((2, bq, HEAD_DIM), lambda hp, j: (off + hp, j, 0))
    tm = jax.ShapeDtypeStruct((s, heads * HEAD_DIM), F32)
    in_specs = [res, blk(hpairs), blk(2 * hpairs), res, res, res, pl.BlockSpec((2, 1, s), lambda hp, j: (hp, 0, 0))]
    out_specs = [pl.BlockSpec((s, 2 * HEAD_DIM), lambda hp, j: (0, hp)),
                 pl.BlockSpec((bq, 2 * HEAD_DIM), lambda hp, j: (j, hp)),
                 pl.BlockSpec((bq, 2 * HEAD_DIM), lambda hp, j: (j, hp)),
                 pl.BlockSpec((2, 1, bq), lambda hp, j: (hp, 0, j))]
    out_shape = [tm, tm, tm, jax.ShapeDtypeStruct((heads, 1, s), F32)]
    scratch = [pltpu.VMEM((2, HEAD_DIM, s), F32), pltpu.VMEM((2, HEAD_DIM, bq), F32),
               pltpu.VMEM((2, HEAD_DIM, bq), F32), pltpu.VMEM((2, 1, bq), F32)]
    outs, comm_outs = _hosted_call(
        body, comm, name=name, grid=grid, in_specs=in_specs, out_specs=out_specs, out_shape=out_shape,
        scratch_shapes=scratch, args=[qkv, qkv, qkv, do, lse, dd, c3], sem=("parallel", "arbitrary"), vmem=VMEM_BIG)
    return outs[0], outs[1], outs[2], outs[3], comm_outs


def _attn_c_probs(qh, mkh):
    sc = lax.dot_general(qh, mkh, (((1,), (1,)), ((), ())), preferred_element_type=F32) * (C_HEAD_DIM ** -0.5)
    p = jnp.exp(sc - jnp.max(sc, axis=1, keepdims=True))
    return p / jnp.sum(p, axis=1, keepdims=True)


def _attn_c_fwd(q, mkv, name):
    s = q.shape[0]
    m = mkv.shape[0]
    bq = _tile(s, 512, 8)

    def body(q_ref, mk_ref, mv_ref, o_ref):
        outs = []
        for h in range(C_HEADS):
            sl = slice(h * C_HEAD_DIM, (h + 1) * C_HEAD_DIM)
            pn = _attn_c_probs(q_ref[:, sl], mk_ref[:, sl]).astype(BF16)
            outs.append(jnp.dot(pn, mv_ref[:, sl], preferred_element_type=F32))
        o_ref[...] = jnp.concatenate(outs, axis=1)

    return pl.pallas_call(
        body, name=name, grid=(s // bq,),
        in_specs=[pl.BlockSpec((bq, C_WIDTH), lambda i: (i, 0)), pl.BlockSpec((m, C_WIDTH), lambda i: (0, 0)),
                  pl.BlockSpec((m, C_WIDTH), lambda i: (0, 1))],
        out_specs=pl.BlockSpec((bq, C_WIDTH), lambda i: (i, 0)),
        out_shape=jax.ShapeDtypeStruct((s, C_WIDTH), F32),
        compiler_params=_params(("parallel",)),
    )(q, mkv, mkv)


def _attn_c_bwd(q, mkv, do, name):
    s = q.shape[0]
    m = mkv.shape[0]
    bq = _tile(s, 512, 8)
    tn = (((0,), (0,)), ((), ()))

    def body(q_ref, mk_ref, mv_ref, do_ref, dq_ref, dm_ref):
        i = pl.program_id(0)

        @pl.when(i == 0)
        def _():
            dm_ref[...] = jnp.zeros(dm_ref.shape, F32)

        dqs = []
        for h in range(C_HEADS):
            sl = slice(h * C_HEAD_DIM, (h + 1) * C_HEAD_DIM)
            qh, mkh, mvh, doh = q_ref[:, sl], mk_ref[:, sl], mv_ref[:, sl], do_ref[:, sl]
            pn = _attn_c_probs(qh, mkh)
            dp = lax.dot_general(doh, mvh, (((1,), (1,)), ((), ())), preferred_element_type=F32)
            dsc = (pn * (dp - jnp.sum(pn * dp, axis=1, keepdims=True)) * (C_HEAD_DIM ** -0.5)).astype(BF16)
            dqs.append(jnp.dot(dsc, mkh, preferred_element_type=F32))
            dm_ref[:, sl] += lax.dot_general(dsc, qh, tn, preferred_element_type=F32)
            sv = slice(C_WIDTH + h * C_HEAD_DIM, C_WIDTH + (h + 1) * C_HEAD_DIM)
            dm_ref[:, sv] += lax.dot_general(pn.astype(BF16), doh, tn, preferred_element_type=F32)
        dq_ref[...] = jnp.concatenate(dqs, axis=1)

    row = pl.BlockSpec((bq, C_WIDTH), lambda i: (i, 0))
    return pl.pallas_call(
        body, name=name, grid=(s // bq,),
        in_specs=[row, pl.BlockSpec((m, C_WIDTH), lambda i: (0, 0)), pl.BlockSpec((m, C_WIDTH), lambda i: (0, 1)), row],
        out_specs=[row, pl.BlockSpec((m, 2 * C_WIDTH), lambda i: (0, 0))],
        out_shape=[jax.ShapeDtypeStruct((s, C_WIDTH), F32), jax.ShapeDtypeStruct((m, 2 * C_WIDTH), F32)],
        compiler_params=_params(("arbitrary",)),
    )(q, mkv, mkv, do)


def _gate_fwd(y, proj, zc0, bw, name):
    rows, width = y.shape
    bm = _tile(rows, 2048 if bw <= 256 else 1024, 16)
    cb0 = zc0 // bw

    def body(y_ref, z_ref, o_ref):
        z = z_ref[...].astype(F32)
        o_ref[...] = (y_ref[...] * (z * _sigmoid(z))).astype(BF16)

    return pl.pallas_call(
        body, name=name, grid=(rows // bm, width // bw),
        in_specs=[pl.BlockSpec((bm, bw), lambda i, t: (i, t)), pl.BlockSpec((bm, bw), lambda i, t: (i, cb0 + t))],
        out_specs=pl.BlockSpec((bm, bw), lambda i, t: (i, t)),
        out_shape=jax.ShapeDtypeStruct((rows, width), BF16),
        compiler_params=_params(("parallel", "parallel")),
    )(y, proj)


def _gate_bwd(dsv, y, proj, zc0, bw, dproj, t0, head_major, name):
    rows, width = y.shape
    bm = _tile(rows, 2048 if bw <= 256 else 1024, 16)
    cb0 = zc0 // bw
    tb0 = t0 // bw
    bd = _block_diag(HEAD_DIM)
    hpb = bw // HEAD_DIM

    def body(*refs):
        if head_major:
            ds_ref, y_ref, z_ref, bd_ref, _, dp_ref, dy_ref, dd_ref = refs
        else:
            ds_ref, y_ref, z_ref, _, dp_ref, dy_ref = refs
        z = z_ref[...].astype(F32)
        sig = _sigmoid(z)
        dsx = ds_ref[...]
        yv = y_ref[...]
        dy = dsx * (z * sig)
        dp_ref[...] = (dsx * yv * (sig * (1.0 + z * (1.0 - sig)))).astype(BF16)
        if head_major:
            dyb = dy.astype(BF16)
            dd = _seg_sum(dyb.astype(F32) * yv, bd_ref[...])
            for h in range(hpb):
                sl = slice(h * HEAD_DIM, (h + 1) * HEAD_DIM)
                dy_ref[h] = dyb[:, sl]
                dd_ref[h] = dd[:, sl]
        else:
            dy_ref[...] = dy.astype(BF16)

    tile = pl.BlockSpec((bm, bw), lambda i, t: (i, t))
    ztile = pl.BlockSpec((bm, bw), lambda i, t: (i, cb0 + t))
    ttile = pl.BlockSpec((bm, bw), lambda i, t: (i, tb0 + t))
    any_spec = pl.BlockSpec(memory_space=pl.ANY)
    dp_shape = jax.ShapeDtypeStruct(dproj.shape, BF16)
    if head_major:
        hm_spec = pl.BlockSpec((hpb, bm, HEAD_DIM), lambda i, t: (t, i, 0))
        nh = width // HEAD_DIM
        outs = pl.pallas_call(
            body, name=name, grid=(rows // bm, width // bw),
            in_specs=[tile, tile, ztile, pl.BlockSpec((LANES, LANES), lambda i, t: (0, 0)), any_spec],
            out_specs=[ttile, hm_spec, hm_spec],
            out_shape=[dp_shape, jax.ShapeDtypeStruct((nh, rows, HEAD_DIM), BF16),
                       jax.ShapeDtypeStruct((nh, rows, HEAD_DIM), F32)],
            input_output_aliases={4: 0},
            compiler_params=_params(("parallel", "parallel")),
        )(dsv, y, proj, bd, dproj)
        return outs[0], outs[1], outs[2]
    outs = pl.pallas_call(
        body, name=name, grid=(rows // bm, width // bw),
        in_specs=[tile, tile, ztile, any_spec],
        out_specs=[ttile, tile],
        out_shape=[dp_shape, jax.ShapeDtypeStruct((rows, width), BF16)],
        input_output_aliases={3: 0},
        compiler_params=_params(("parallel", "parallel")),
    )(dsv, y, proj, dproj)
    return outs[0], outs[1], None


def _merge_fwd(proj, ua, ub, uc, name):
    rows, d = ua.shape
    bm = _tile(rows, 1024, 16)
    bw = _tile(d, 512)
    g0 = COL_GATE // bw
    gstep = d // bw

    def body(la_ref, lb_ref, lc_ref, ua_ref, ub_ref, uc_ref, o_ref, ga_ref, gb_ref, gc_ref):
        y = None
        for l_ref, u_ref, g_ref in ((la_ref, ua_ref, ga_ref), (lb_ref, ub_ref, gb_ref), (lc_ref, uc_ref, gc_ref)):
            g = _sigmoid(l_ref[...].astype(F32))
            g_ref[...] = g.astype(BF16)
            term = g * u_ref[...].astype(F32)
            y = term if y is None else y + term
        o_ref[...] = y.astype(BF16)

    tile = pl.BlockSpec((bm, bw), lambda i, t: (i, t))
    gate = lambda b: pl.BlockSpec((bm, bw), lambda i, t: (i, g0 + b * gstep + t))
    shape = jax.ShapeDtypeStruct((rows, d), BF16)
    return pl.pallas_call(
        body, name=name, grid=(rows // bm, d // bw),
        in_specs=[gate(0), gate(1), gate(2), tile, tile, tile],
        out_specs=[tile] * 4, out_shape=[shape] * 4,
        compiler_params=_params(("parallel", "parallel")),
    )(proj, proj, proj, ua, ub, uc)


def _merge_bwd(dym, us, gs, name):
    rows, d = dym.shape
    bm = _tile(rows, 256, 16)

    def body(dy_ref, ua_ref, ub_ref, uc_ref, ga_ref, gb_ref, gc_ref, dg_ref, da_ref, db_ref, dc_ref):
        dyv = dy_ref[...]
        for b, (u_ref, g_ref, du_ref) in enumerate(((ua_ref, ga_ref, da_ref), (ub_ref, gb_ref, db_ref), (uc_ref, gc_ref, dc_ref))):
            g = g_ref[...].astype(F32)
            du_ref[...] = (g * dyv).astype(BF16)
            dg_ref[:, b * d:(b + 1) * d] = (dyv * u_ref[...].astype(F32) * g * (1.0 - g)).astype(BF16)

    tile = pl.BlockSpec((bm, d), lambda i: (i, 0))
    shape = jax.ShapeDtypeStruct((rows, d), BF16)
    outs = pl.pallas_call(
        body, name=name, grid=(rows // bm,),
        in_specs=[tile] * 7,
        out_specs=[pl.BlockSpec((bm, 3 * d), lambda i: (i, 0)), tile, tile, tile],
        out_shape=[jax.ShapeDtypeStruct((rows, 3 * d), BF16), shape, shape, shape],
        compiler_params=_params(("parallel",), VMEM_BIG),
    )(dym, *us, *gs)
    return outs[0], outs[1], outs[2], outs[3]


def _out_proj_loss(ym, wo, x, target, name):
    m, d = x.shape
    bm, bn = _tile(m, 1024, 16), _tile(d, 1024)
    grid = (m // bm, d // bn)

    def body(a_ref, b_ref, x_ref, t_ref, dy_ref, dyb_ref, l_ref):
        first, _ = _grid_edges(grid)
        y = jnp.dot(a_ref[...], b_ref[...], preferred_element_type=F32) + x_ref[...]
        diff = y - t_ref[...]
        dy = diff * (1.0 / d)
        dy_ref[...] = dy
        dyb_ref[...] = dy.astype(BF16)
        sq = diff * diff
        part = sq[:, 0:LANES]
        for c in range(1, bn // LANES):
            part = part + sq[:, c * LANES:(c + 1) * LANES]
        part = jnp.sum(part.reshape(bm // 8, 8, LANES), axis=0)

        @pl.when(first)
        def _():
            l_ref[...] = part

        @pl.when(jnp.logical_not(first))
        def _():
            l_ref[...] += part

    tile = pl.BlockSpec((bm, bn), lambda i, j: (i, j))
    return pl.pallas_call(
        body, name=name, grid=grid,
        in_specs=[pl.BlockSpec((bm, d), lambda i, j: (i, 0)), pl.BlockSpec((d, bn), lambda i, j: (0, j)), tile, tile],
        out_specs=[tile, tile, pl.BlockSpec((8, LANES), lambda i, j: (0, 0))],
        out_shape=[jax.ShapeDtypeStruct((m, d), F32), jax.ShapeDtypeStruct((m, d), BF16),
                   jax.ShapeDtypeStruct((8, LANES), F32)],
        compiler_params=_params(("arbitrary", "arbitrary"), VMEM_BIG),
    )(ym, wo, x, target)


def _row(vec, reps=1):
    return jnp.tile(vec.reshape(1, -1).astype(F32), (1, reps))


def _local_step(x, mem, target, small, wg, shards=None):
    s, d = x.shape
    dist = shards is not None
    wg = dict(wg)
    ones = lambda n: jnp.ones((1, n), F32)
    zeros = lambda n: jnp.zeros((1, n), F32)
    scale_ab = HEAD_DIM ** -0.5
    split8 = lambda g: g.reshape(N_DEV, g.shape[0] // N_DEV, g.shape[1])
    flat8 = lambda g: g.reshape(g.shape[0] * g.shape[1], g.shape[2])
    gather = lambda names: _Comm("gather", [shards[n] for n in names]) if dist else None
    g = {}

    windows = {"wm_q1": ("wm_qkv", 0, Q_SPLIT), "wm_q2": ("wm_qkv", Q_SPLIT, W_QKV - Q_SPLIT)}

    def scatter(names):
        if not dist:
            return None
        whole = [windows.get(n, (n, None, None)) for n in names]
        return _Comm("scatter", [split8(g[src]) for src, _, _ in whole],
                     [None if col0 is None else (col0, width) for _, col0, width in whole])

    def hosted(result, names, store):
        if not dist:
            return result
        out, got = result
        store.update(zip(names, got))
        return out

    hn = _rmsnorm_fwd(x, small["norm_gain"], "rms_x_fwd")
    got = {}
    proj = hosted(_mm_nn(hn, wg["qkv"], bm=1024, bn=1024, bk=d, o_dtype=BF16, name="proj_qkv",
                         comm=gather(("wa", "wb"))), ("wa", "wb"), got)
    wg.update({n: flat8(a) for n, a in got.items()})
    pfb = _mm_nn(hn, wg["wf"], bm=1024, bn=FB_PAD, bk=d, o_dtype=F32, name="proj_fb")
    mn = _rmsnorm_fwd(mem, small["mem_norm_gain"], "rms_mem_fwd")
    mkv = _mm_nn(mn, wg["wk"], bm=256, bn=1024, bk=d, o_dtype=F32, name="mem_kv")

    gain_a = jnp.concatenate([_row(small["q_gain_a"], A_Q_HEADS) * scale_ab, _row(small["k_gain_a"], A_KV_HEADS), ones(A_KV_WIDTH)], axis=1)
    flag_a = jnp.concatenate([ones(A_WIDTH + A_KV_WIDTH), zeros(A_KV_WIDTH)], axis=1)
    qkv_a = _headnorm_fwd(proj, COL_QA, 1280, 1280, HEAD_DIM, gain_a, flag_a, True, "hn_a_fwd")
    gain_b = jnp.concatenate([_row(small["q_gain_b"], B_HEADS) * scale_ab, _row(small["k_gain_b"], B_HEADS), ones(B_WIDTH)], axis=1)
    flag_b = jnp.concatenate([ones(2 * B_WIDTH), zeros(B_WIDTH)], axis=1)
    qkv_b = _headnorm_fwd(proj, COL_QB, 2304, 256, HEAD_DIM, gain_b, flag_b, True, "hn_b_fwd")
    gain_cq = _row(small["q_gain_c"], C_HEADS)
    q_c = _headnorm_fwd(proj, COL_QC, C_WIDTH, C_WIDTH, C_HEAD_DIM, gain_cq, ones(C_WIDTH), False, "hn_cq_fwd")
    gain_ck = jnp.concatenate([_row(small["k_gain_c"], C_HEADS), ones(C_WIDTH)], axis=1)
    flag_ck = jnp.concatenate([ones(C_WIDTH), zeros(C_WIDTH)], axis=1)
    mkvn = _headnorm_fwd(mkv, 0, 2 * C_WIDTH, 2 * C_WIDTH, C_HEAD_DIM, gain_ck, flag_ck, False, "hn_ck_fwd")


    bpad = jnp.pad(small["b_forget"].reshape(1, -1), ((0, 0), (0, FB_PAD - B_HEADS)))
    c16 = _fox_prep(pfb, bpad, "fox_prep")
    c3 = c16[0:B_HEADS].reshape(B_HEADS, 1, s)

    sinks = small["sinks_a"].reshape(-1)
    slopes = jnp.exp2(-8.0 * jnp.arange(1, A_Q_HEADS + 1, dtype=F32) / A_Q_HEADS)
    y_a, lse_a = _attn_a_fwd(qkv_a, sinks, slopes, "attn_a_fwd")
    y_b, lse_b, got_zg = _attn_b_fwd(qkv_b, c3, "attn_b_fwd", comm=gather(("zg",)))
    if dist:
        wg["zg"] = flat8(got_zg[0])
    y_c = _attn_c_fwd(q_c, mkvn, "attn_c_fwd")

    got = {}
    pzg = hosted(_mm_nn(hn, wg["zg"], bm=1024, bn=1024, bk=d, o_dtype=BF16, name="proj_zg", comm=gather(("wo", "wc"))),
                 ("wo", "wc"), got)
    wg.update({n: flat8(a) for n, a in got.items()})

    s_a = _gate_fwd(y_a, pzg, COL_ZA, 256, "gate_a_fwd")
    s_b = _gate_fwd(y_b, pzg, COL_ZB, 256, "gate_b_fwd")
    s_c = _gate_fwd(y_c, pzg, COL_ZC, 512, "gate_c_fwd")
    w_a, w_b, w_c = _branch_full(wg["wa"]), _branch_full(wg["wb"]), _branch_full(wg["wc"])
    u_a = _mm_nn(s_a, w_a, bm=1024, bn=2048, bk=A_WIDTH, o_dtype=BF16, name="branch_a_fwd")
    u_b = _mm_nn(s_b, w_b, bm=1024, bn=2048, bk=B_WIDTH, o_dtype=BF16, name="branch_b_fwd")
    u_c = _mm_nn(s_c, w_c, bm=1024, bn=2048, bk=C_WIDTH, o_dtype=BF16, name="branch_c_fwd")
    ym, gate_a, gate_b, gate_c = _merge_fwd(pzg, u_a, u_b, u_c, "merge_fwd")
    dy, dyb, lpart = _out_proj_loss(ym, wg["wo"], x, target, "out_proj_loss")
    loss = 0.5 / d * jnp.sum(lpart)

    dym = _mm_nt(dyb, wg["wo"], bm=1024, bn=1024, bk=d, o_dtype=F32, name="out_proj_bwd_act")
    g["wo"] = _mm_tn(ym, dyb, bm=512, bn=1024, bk=s, o_dtype=BF16, name="out_proj_bwd_w")

    dgate, du_a, du_b, du_c = _merge_bwd(dym, (u_a, u_b, u_c), (gate_a, gate_b, gate_c), "merge_bwd")
    parts = {}
    g["wm_g"] = hosted(_mm_tn(hn, dgate, bm=512, bn=1024, bk=s, o_dtype=BF16, name="proj_gate_bwd_w",
                              comm=scatter(("wo",))), ("wo",), parts)

    ds_a = _mm_nt(du_a, w_a, bm=1024, bn=A_WIDTH, bk=d, o_dtype=F32, name="branch_a_bwd_act")
    ds_b = _mm_nt(du_b, w_b, bm=1024, bn=B_WIDTH, bk=d, o_dtype=F32, name="branch_b_bwd_act")
    ds_c = _mm_nt(du_c, w_c, bm=1024, bn=C_WIDTH, bk=d, o_dtype=F32, name="branch_c_bwd_act")
    g["wa"] = _branch_bwd_w(s_a, du_a, "branch_a_bwd_w")
    g["wb"] = _branch_bwd_w(s_b, du_b, "branch_b_bwd_w")
    g["wc"] = _branch_bwd_w(s_c, du_c, "branch_c_bwd_w")

    dz = lax.empty((s, W_Z), BF16)
    dz, do_a, dd_a = _gate_bwd(ds_a, y_a, pzg, COL_ZA, 256, dz, COL_ZA, True, "gate_a_bwd")
    dz, do_b, dd_b = _gate_bwd(ds_b, y_b, pzg, COL_ZB, 256, dz, COL_ZB, True, "gate_b_bwd")
    dz, do_c, _ = _gate_bwd(ds_c, y_c, pzg, COL_ZC, 512, dz, COL_ZC, False, "gate_c_bwd")
    g["wm_z"] = _mm_tn(hn, dz, bm=512, bn=1024, bk=s, o_dtype=BF16, name="proj_z_bwd_w")

    names = ("wa", "wb", "wc")
    dq_a, dkv_a, dsink, got = _attn_a_bwd(qkv_a, do_a, lse_a, dd_a, sinks, slopes, "attn_a_bwd", comm=scatter(names))
    parts.update(zip(names, got))
    names = ("wm_g", "wm_z")
    dq_b, dk_b, dv_b, dc3, got = _attn_b_bwd(qkv_b, do_b, lse_b, dd_b, c3, "attn_b_bwd", comm=scatter(names))
    parts.update(zip(names, got))
    dq_c, dmkvn = _attn_c_bwd(q_c, mkvn, do_c, "attn_c_bwd")

    dqkv = lax.empty((s, W_QKV), BF16)
    dqkv, dg_qa = _headnorm_bwd(proj, COL_QA, A_WIDTH, 256, HEAD_DIM, gain_a[:, 0:768], flag_a[:, 0:768], dq_a, dqkv, COL_QA, "hn_qa_bwd")
    dqkv, dg_kva = _headnorm_bwd(proj, COL_KA, 512, 256, HEAD_DIM, gain_a[:, 768:1280], flag_a[:, 768:1280], dkv_a, dqkv, COL_KA, "hn_kva_bwd")
    dqkv, dg_qb = _headnorm_bwd(proj, COL_QB, B_WIDTH, 256, HEAD_DIM, gain_b[:, 0:768], flag_b[:, 0:768], dq_b, dqkv, COL_QB, "hn_qb_bwd")
    dqkv, dg_kb = _headnorm_bwd(proj, COL_KB, B_WIDTH, 256, HEAD_DIM, gain_b[:, 768:1536], flag_b[:, 768:1536], dk_b, dqkv, COL_KB, "hn_kb_bwd")
    dqkv, _ = _headnorm_bwd(proj, COL_VB, B_WIDTH, 256, HEAD_DIM, gain_b[:, 1536:2304], flag_b[:, 1536:2304], dv_b, dqkv, COL_VB, "hn_vb_bwd")
    dqkv, dg_qc = _headnorm_bwd(proj, COL_QC, C_WIDTH, 512, C_HEAD_DIM, gain_cq, ones(C_WIDTH), dq_c, dqkv, COL_QC, "hn_qc_bwd")
    dmkv, dg_kc = _headnorm_bwd(mkv, 0, 2 * C_WIDTH, 2 * C_WIDTH, C_HEAD_DIM, gain_ck, flag_ck, dmkvn, None, 0, "hn_kc_bwd")

    dct = jnp.pad(dc3.reshape(B_HEADS, s), ((0, 16 - B_HEADS), (0, 0)))
    dfb, dbf = _fox_prep_bwd(pfb, bpad, dct, "fox_prep_bwd")

    dmn = _mm_nt(dmkv, wg["wk"], bm=256, bn=1024, bk=1024, o_dtype=F32, name="mem_kv_bwd_act")
    g["wk"] = _mm_tn(mn, dmkv, bm=512, bn=1024, bk=mem.shape[0], o_dtype=BF16, name="mem_kv_bwd_w")
    _, dg_mem = _rmsnorm_bwd(mem, dmn, small["mem_norm_gain"], None, "rms_mem_bwd")

    g["wm_qkv"] = _mm_tn(hn, dqkv, bm=512, bn=1024, bk=s, o_dtype=BF16, name="proj_qkv_bwd_w")
    g["wf"] = _mm_tn(hn, dfb, bm=512, bn=FB_PAD, bk=s, o_dtype=BF16, name="proj_fb_bwd_w")
    names = ("wm_q1",)
    dhn = hosted(_mm_nt_sum([(dqkv, wg["qkv"], 0), (dfb, wg["wf"], 0)], bm=1024, bn=1024, bk=2048,
                            name="proj_qkv_bwd_act", comm=scatter(names)), names, parts)
    names = ("wm_q2", "wf", "wk")
    dhn = hosted(_mm_nt_sum([(dz, wg["zg"], COL_ZA), (dgate, wg["zg"], COL_GATE)], bm=1024, bn=1024, bk=2048,
                            name="proj_zg_bwd_act", add=dhn, comm=scatter(names)), names, parts)
    if dist:
        g = parts
    grad_x, dg_x = _rmsnorm_bwd(x, dhn, small["norm_gain"], dy, "rms_x_bwd")

    fold = lambda part, heads, hd: jnp.sum(jnp.sum(part, axis=0).reshape(heads, hd), axis=0).reshape(1, hd)
    small_grads = {
        "norm_gain": jnp.sum(dg_x, axis=0).reshape(1, d),
        "mem_norm_gain": jnp.sum(dg_mem, axis=0).reshape(1, d),
        "b_forget": dbf[0:B_HEADS, 0].reshape(1, B_HEADS),
        "q_gain_a": fold(dg_qa, A_Q_HEADS, HEAD_DIM) * scale_ab,
        "k_gain_a": fold(dg_kva[:, 0:A_KV_WIDTH], A_KV_HEADS, HEAD_DIM),
        "sinks_a": (jnp.sum(dsink, axis=(1, 2)) * (1.0 / HEAD_DIM)).reshape(1, A_Q_HEADS),
        "q_gain_b": fold(dg_qb, B_HEADS, HEAD_DIM) * scale_ab,
        "k_gain_b": fold(dg_kb, B_HEADS, HEAD_DIM),
        "q_gain_c": fold(dg_qc, C_HEADS, C_HEAD_DIM),
        "k_gain_c": fold(dg_kc[:, 0:C_WIDTH], C_HEADS, C_HEAD_DIM),
    }
    return loss, grad_x, small_grads, g


def _coords():
    return lax.axis_index("x"), lax.axis_index("y"), lax.axis_index("c")


def _all_gather(shards, name):
    n = len(shards)

    def body(*refs):
        ins = refs[0:n]
        outs = refs[n:2 * n]
        send_sems, recv_sems, local_sems = refs[2 * n:2 * n + 3]
        x, y, c = _coords()
        me, sibling = (x, y, c), (x, y, 1 - c)
        chips = [(1 - x, y), (x, 1 - y), (1 - x, 1 - y)]
        idx = lambda p: 4 * p[0] + 2 * p[1] + p[2]

        def copy(a, k, block, to, src=None):
            slot = outs[a].at[idx(block)]
            return pltpu.make_async_remote_copy(
                src_ref=slot if src is None else src, dst_ref=slot,
                send_sem=send_sems.at[a, k], recv_sem=recv_sems.at[a, k], device_id=to, device_id_type=MESH)

        mine = [pltpu.make_async_copy(ins[a], outs[a].at[idx(me)], local_sems.at[a]) for a in range(n)]
        for cp in mine:
            cp.start()
        first = []
        for a in range(n):
            first.append(copy(a, 0, me, sibling, src=ins[a]))
            first += [copy(a, 1 + j, me, (*chip, c), src=ins[a]) for j, chip in enumerate(chips)]
        for cp in first:
            cp.start()
        passed = []
        for j, chip in enumerate(chips):
            for a in range(n):
                copy(a, 1 + j, (*chip, c), me).wait_recv()
                fwd = copy(a, 4 + j, (*chip, c), sibling)
                fwd.start()
                passed.append(fwd)
        for a in range(n):
            copy(a, 0, sibling, me).wait_recv()
            for j, chip in enumerate(chips):
                copy(a, 4 + j, (*chip, 1 - c), me).wait_recv()
        for cp in first + passed:
            cp.wait_send()
        for cp in mine:
            cp.wait()

    any_spec = pl.BlockSpec(memory_space=pl.ANY)
    return pl.pallas_call(
        body, name=name,
        in_specs=[any_spec] * n, out_specs=[any_spec] * n,
        out_shape=[jax.ShapeDtypeStruct((N_DEV,) + sh.shape, sh.dtype) for sh in shards],
        scratch_shapes=[pltpu.SemaphoreType.DMA((n, 7)), pltpu.SemaphoreType.DMA((n, 7)), pltpu.SemaphoreType.DMA((n,))],
    )(*shards)


def _sum_parts(parts, name):
    _, rows, cols = parts.shape
    br = _tile(rows, 64, 16)

    def body(p_ref, o_ref):
        total = p_ref[0].astype(F32)
        for j in range(1, N_DEV):
            total = total + p_ref[j].astype(F32)
        o_ref[...] = total

    return pl.pallas_call(
        body, name=name, grid=(rows // br,),
        in_specs=[pl.BlockSpec((N_DEV, br, cols), lambda i: (0, i, 0))],
        out_specs=pl.BlockSpec((br, cols), lambda i: (i, 0)),
        out_shape=jax.ShapeDtypeStruct((rows, cols), F32),
        compiler_params=_params(("parallel",), VMEM_BIG),
    )(parts)


def _adamw(w, g, m, v, name, br=32):
    rows, cols = w.shape
    br = min(br, rows)
    c1 = 1.0 / (1.0 - ADAM_B1 ** ADAM_STEP)
    c2 = 1.0 / (1.0 - ADAM_B2 ** ADAM_STEP)

    def body(w_ref, g_ref, m_ref, v_ref, d_ref, nm_ref, nv_ref):
        gv = g_ref[...]
        nm = ADAM_B1 * m_ref[...] + (1.0 - ADAM_B1) * gv
        nv = ADAM_B2 * v_ref[...] + (1.0 - ADAM_B2) * (gv * gv)
        d_ref[...] = -ADAM_LR * ((nm * c1) / (jnp.sqrt(nv * c2) + ADAM_EPS) + ADAM_WD * w_ref[...])
        nm_ref[...] = nm
        nv_ref[...] = nv

    spec = pl.BlockSpec((br, cols), lambda i: (i, 0))
    shape = jax.ShapeDtypeStruct((rows, cols), F32)
    return pl.pallas_call(
        body, name=name, grid=(pl.cdiv(rows, br),), in_specs=[spec] * 4, out_specs=[spec] * 3, out_shape=[shape] * 3,
        compiler_params=_params(("parallel",), VMEM_BIG),
    )(w, g, m, v)


def _adamw_t(wt, g, mt, vt, name, br=1024, comm=None):
    n, r = wt.shape
    c1 = 1.0 / (1.0 - ADAM_B1 ** ADAM_STEP)
    c2 = 1.0 / (1.0 - ADAM_B2 ** ADAM_STEP)

    def body(w_ref, g_ref, m_ref, v_ref, d_ref, nm_ref, nv_ref):
        gv = g_ref[...].T
        nm = ADAM_B1 * m_ref[...] + (1.0 - ADAM_B1) * gv
        nv = ADAM_B2 * v_ref[...] + (1.0 - ADAM_B2) * (gv * gv)
        d_ref[...] = -ADAM_LR * ((nm * c1) / (jnp.sqrt(nv * c2) + ADAM_EPS) + ADAM_WD * w_ref[...])
        nm_ref[...] = nm
        nv_ref[...] = nv

    spec = pl.BlockSpec((br, r), lambda i: (i, 0))
    shape = jax.ShapeDtypeStruct((n, r), F32)
    return _hosted_call(
        body, comm, name=name, grid=(pl.cdiv(n, br),),
        in_specs=[spec, pl.BlockSpec((r, br), lambda i: (0, i)), spec, spec], out_specs=[spec] * 3, out_shape=[shape] * 3,
        scratch_shapes=[], args=[wt, g, mt, vt], sem=("parallel",), vmem=VMEM_BIG)


def _adamw_parts(w, parts, m, v, name):
    rows, cols = w.shape
    br = _tile(rows, 32, 16)
    c1 = 1.0 / (1.0 - ADAM_B1 ** ADAM_STEP)
    c2 = 1.0 / (1.0 - ADAM_B2 ** ADAM_STEP)

    def body(w_ref, p_ref, m_ref, v_ref, g_ref, d_ref, nm_ref, nv_ref):
        gv = p_ref[0].astype(F32)
        for j in range(1, N_DEV):
            gv = gv + p_ref[j].astype(F32)
        nm = ADAM_B1 * m_ref[...] + (1.0 - ADAM_B1) * gv
        nv = ADAM_B2 * v_ref[...] + (1.0 - ADAM_B2) * (gv * gv)
        g_ref[...] = gv
        d_ref[...] = -ADAM_LR * ((nm * c1) / (jnp.sqrt(nv * c2) + ADAM_EPS) + ADAM_WD * w_ref[...])
        nm_ref[...] = nm
        nv_ref[...] = nv

    spec = pl.BlockSpec((br, cols), lambda i: (i, 0))
    shape = jax.ShapeDtypeStruct((rows, cols), F32)
    return pl.pallas_call(
        body, name=name, grid=(rows // br,),
        in_specs=[spec, pl.BlockSpec((N_DEV, br, cols), lambda i: (0, i, 0)), spec, spec],
        out_specs=[spec] * 4, out_shape=[shape] * 4,
        compiler_params=_params(("parallel",), VMEM_BIG),
    )(w, parts, m, v)


SMALL_NAMES = ("norm_gain", "mem_norm_gain", "b_forget", "q_gain_a", "k_gain_a", "sinks_a",
               "q_gain_b", "k_gain_b", "q_gain_c", "k_gain_c")
BIG_NAMES = ("w_in", "w_mem_kv", "w_branch_a", "w_branch_b", "w_branch_c", "w_out")
WEIGHT_ORDER = ("norm_gain", "mem_norm_gain", "w_in", "b_forget", "q_gain_a", "k_gain_a", "sinks_a", "q_gain_b",
                "k_gain_b", "q_gain_c", "k_gain_c", "w_mem_kv", "w_branch_a", "w_branch_b", "w_branch_c", "w_out")


def _pack_small(tree):
    flat = jnp.concatenate([tree[n].reshape(1, -1) for n in SMALL_NAMES], axis=1)
    pad = (-flat.shape[1]) % LANES
    return jnp.pad(flat, ((0, 0), (0, pad)))


def _unpack_small(flat, like):
    out, off = {}, 0
    for n in SMALL_NAMES:
        size = like[n].size
        out[n] = flat[:, off:off + size].reshape(like[n].shape)
        off += size
    return out


def kernel(x, mem, norm_gain, mem_norm_gain, w_in, b_forget, q_gain_a, k_gain_a, sinks_a, q_gain_b, k_gain_b, q_gain_c, k_gain_c, w_mem_kv, w_branch_a, w_branch_b, w_branch_c, w_out, loss_target, m_norm_gain, m_mem_norm_gain, m_w_in, m_b_forget, m_q_gain_a, m_k_gain_a, m_sinks_a, m_q_gain_b, m_k_gain_b, m_q_gain_c, m_k_gain_c, m_w_mem_kv, m_w_branch_a, m_w_branch_b, m_w_branch_c, m_w_out, v_norm_gain, v_mem_norm_gain, v_w_in, v_b_forget, v_q_gain_a, v_k_gain_a, v_sinks_a, v_q_gain_b, v_k_gain_b, v_q_gain_c, v_k_gain_c, v_w_mem_kv, v_w_branch_a, v_w_branch_b, v_w_branch_c, v_w_out):
    weights = dict(norm_gain=norm_gain, mem_norm_gain=mem_norm_gain, w_in=w_in, b_forget=b_forget, q_gain_a=q_gain_a,
                   k_gain_a=k_gain_a, sinks_a=sinks_a, q_gain_b=q_gain_b, k_gain_b=k_gain_b, q_gain_c=q_gain_c,
                   k_gain_c=k_gain_c, w_mem_kv=w_mem_kv, w_branch_a=w_branch_a, w_branch_b=w_branch_b,
                   w_branch_c=w_branch_c, w_out=w_out)
    mom_m = dict(norm_gain=m_norm_gain, mem_norm_gain=m_mem_norm_gain, w_in=m_w_in, b_forget=m_b_forget,
                 q_gain_a=m_q_gain_a, k_gain_a=m_k_gain_a, sinks_a=m_sinks_a, q_gain_b=m_q_gain_b, k_gain_b=m_k_gain_b,
                 q_gain_c=m_q_gain_c, k_gain_c=m_k_gain_c, w_mem_kv=m_w_mem_kv, w_branch_a=m_w_branch_a,
                 w_branch_b=m_w_branch_b, w_branch_c=m_w_branch_c, w_out=m_w_out)
    mom_v = dict(norm_gain=v_norm_gain, mem_norm_gain=v_mem_norm_gain, w_in=v_w_in, b_forget=v_b_forget,
                 q_gain_a=v_q_gain_a, k_gain_a=v_k_gain_a, sinks_a=v_sinks_a, q_gain_b=v_q_gain_b, k_gain_b=v_k_gain_b,
                 q_gain_c=v_q_gain_c, k_gain_c=v_k_gain_c, w_mem_kv=v_w_mem_kv, w_branch_a=v_w_branch_a,
                 w_branch_b=v_w_branch_b, w_branch_c=v_w_branch_c, w_out=v_w_out)
    wi = w_in[0]
    sh_qkv = jnp.concatenate([wi[:, a:b] for a, b in SRC_RANGES[0:3]], axis=1).astype(BF16)
    sh_zg = jnp.concatenate([wi[:, a:b] for a, b in SRC_RANGES[3:6]] + [wi[:, SRC_GATE:]], axis=1).astype(BF16)
    sh_wf = jnp.pad(wi[:, FB_SRC:FB_SRC + B_HEADS], ((0, 0), (0, FB_PAD - B_HEADS))).astype(BF16)
    shards = {"zg": sh_zg, "wo": w_out[0].astype(BF16), "wa": w_branch_a[0].astype(BF16),
              "wb": w_branch_b[0].astype(BF16), "wc": w_branch_c[0].astype(BF16)}
    first = ("qkv", "wf", "wk")
    full = _all_gather([sh_qkv, sh_wf, w_mem_kv[0].astype(BF16)], "weights_all_gather")
    wg = {kname: arr.reshape(arr.shape[0] * arr.shape[1], arr.shape[2]) for kname, arr in zip(first, full)}

    small = {n: weights[n] for n in SMALL_NAMES}
    loss_local, grad_x, small_g, parts = _local_step(x[0], mem[0], loss_target[0], small, wg, shards)

    grads, delta, new_m, new_v = {}, {}, {}, {}
    g1, g2, gz, gf, gg = (_sum_parts(parts[k], "grad_sum_" + k) for k in ("wm_q1", "wm_q2", "wm_z", "wf", "wm_g"))
    half = Q_SPLIT
    g_in = jnp.concatenate([g1, g2[:, 0:COL_QB - half], gz[:, COL_ZA:COL_ZB], g2[:, COL_QB - half:COL_QC - half],
                            gz[:, COL_ZB:COL_ZC], gf[:, 0:B_HEADS], g2[:, COL_QC - half:W_QKV - half], gz[:, COL_ZC:W_Z], gg], axis=1)
    packed = _pack_small(small_g)
    packed = jnp.concatenate([packed[:, :-1], loss_local.reshape(1, 1)], axis=1)
    all_small = _Comm("gather", [jnp.broadcast_to(packed, (8, packed.shape[1]))])
    (dlt, nm, nv), (packed8,) = _adamw_t(w_in[0].T, g_in, m_w_in[0].T, v_w_in[0].T, "adamw_w_in", comm=all_small)
    others = ("wk", "wo", "wa", "wb", "wc")
    (dlt, nm, nv), held = lax.optimization_barrier(((dlt, nm, nv), [parts[k] for k in others]))
    parts.update(zip(others, held))
    grads["w_in"], delta["w_in"], new_m["w_in"], new_v["w_in"] = g_in, dlt.T[None], nm.T[None], nv.T[None]
    reduced = _sum_parts(packed8, "small_sum")[0:1]
    grads.update(_unpack_small(reduced, small))
    loss = reduced[0, -1]
    for n, kname in (("w_mem_kv", "wk"), ("w_out", "wo"), ("w_branch_a", "wa"), ("w_branch_b", "wb"), ("w_branch_c", "wc")):
        gsum, dlt, nm, nv = _adamw_parts(weights[n][0], parts[kname], mom_m[n][0], mom_v[n][0], "adamw_" + n)
        grads[n], delta[n], new_m[n], new_v[n] = gsum, dlt[None], nm[None], nv[None]

    pw, pm, pv = _pack_small(small), _pack_small({n: mom_m[n] for n in SMALL_NAMES}), _pack_small({n: mom_v[n] for n in SMALL_NAMES})
    rep8 = lambda a: jnp.broadcast_to(a, (8, a.shape[1]))
    dlt, nm, nv = _adamw(rep8(pw), rep8(reduced), rep8(pm), rep8(pv), "adamw_small")
    for tree, flat in ((delta, dlt), (new_m, nm), (new_v, nv)):
        tree.update(_unpack_small(flat[0:1], small))
    for n in BIG_NAMES:
        grads[n] = grads[n][None]
    return (loss, grad_x[None], *[grads[n] for n in WEIGHT_ORDER], *[delta[n] for n in WEIGHT_ORDER],
            *[new_m[n] for n in WEIGHT_ORDER], *[new_v[n] for n in WEIGHT_ORDER])
```
